```python
import math
import jax, jax.numpy as jnp
from jax import lax
import numpy as np

D_MODEL = 1024
BATCH = 8
SEQ = 2048
DEPTH = 1

CHUNK = 64
CONV_DIM = D_MODEL // 2
CONV_WIDTH = 31
N_HEADS = 8
HEAD_DIM = 64
ATT_DIM = N_HEADS * HEAD_DIM
QBLOCK = 128
D_FF = ((8 * D_MODEL + 3 * 256 - 1) // (3 * 256)) * 256
EPS = 1e-6
IN_COLS = 2 * CONV_DIM + 3 * ATT_DIM + 2 * D_MODEL

kernel_name = "hybrid_conformer_conv_stickbreaking_block"


def rmsnorm(x, g):
    xf = x.astype(jnp.float32)
    y = xf * lax.rsqrt(jnp.mean(xf * xf, axis=-1, keepdims=True) + EPS)
    return (y * g.astype(jnp.float32)).astype(x.dtype)


def layernorm(x, g, b):
    xf = x.astype(jnp.float32)
    mu = jnp.mean(xf, axis=-1, keepdims=True)
    var = jnp.mean(jnp.square(xf - mu), axis=-1, keepdims=True)
    y = (xf - mu) * lax.rsqrt(var + EPS)
    return (y * g.astype(jnp.float32) + b.astype(jnp.float32)).astype(x.dtype)


def causal_depthwise_conv(u, w, b):
    k, c = w.shape
    up = jnp.pad(u, ((0, 0), (k - 1, 0), (0, 0)))
    y = lax.conv_general_dilated(
        up, w[:, None, :].astype(u.dtype), window_strides=(1,), padding='VALID',
        dimension_numbers=('NWC', 'WIO', 'NWC'), feature_group_count=c)
    return y + b.astype(u.dtype)


def stick_breaking_attention(q, k, v):
    s_len = q.shape[1]
    scale = 1.0 / math.sqrt(q.shape[-1])
    outs = []
    for i in range(s_len // QBLOCK):
        q0 = i * QBLOCK
        q1 = q0 + QBLOCK
        qb = q[:, q0:q1].astype(jnp.float32)
        kb = k[:, :q1].astype(jnp.float32)
        vb = v[:, :q1].astype(jnp.float32)
        z = jnp.einsum('bqhd,bkhd->bhqk', qb, kb) * scale
        t_idx = q0 + jnp.arange(QBLOCK)[:, None]
        s_idx = jnp.arange(q1)[None, :]
        mask = s_idx < t_idx
        log_1m = jnp.where(mask, jax.nn.log_sigmoid(-z), 0.0)
        rev_excl = lax.cumsum(log_1m, axis=3, reverse=True) - log_1m
        a = jnp.where(mask, jnp.exp(jax.nn.log_sigmoid(z) + rev_excl), 0.0)
        outs.append(jnp.einsum('bhqk,bkhd->bqhd', a, vb))
    return jnp.concatenate(outs, axis=1).astype(q.dtype)


def _fwd_setup_inputs(seed: int = 0) -> dict:
    key = jax.random.key(seed)
    ks = jax.random.split(key, 20)
    f32 = jnp.float32

    def w(k, shape, fan_in):
        return jax.random.normal(k, shape, f32) * (fan_in ** -0.5)

    def gain(k, n):
        return 1.0 + 0.05 * jax.random.normal(k, (n,), f32)

    def bias(k, n):
        return 0.02 * jax.random.normal(k, (n,), f32)

    return {
        'x': jax.random.normal(ks[0], (BATCH, SEQ, D_MODEL), f32),
        'norm_mix_pre': gain(ks[1], D_MODEL),
        'w_in': w(ks[2], (D_MODEL, IN_COLS), D_MODEL),
        'conv_dw_w': w(ks[3], (CONV_WIDTH, CONV_DIM), CONV_WIDTH),
        'conv_dw_b': bias(ks[4], CONV_DIM),
        'conv_ln_g': gain(ks[5], CONV_DIM),
        'conv_ln_b': bias(ks[6], CONV_DIM),
        'w_conv_branch': w(ks[7], (CONV_DIM, D_MODEL), CONV_DIM),
        'b_conv_branch': bias(ks[8], D_MODEL),
        'w_att_branch': w(ks[9], (ATT_DIM, D_MODEL), ATT_DIM),
        'w_out': w(ks[10], (D_MODEL, D_MODEL), D_MODEL),
        'norm_mix_post': gain(ks[11], D_MODEL),
        'norm_ffn_pre': gain(ks[12], D_MODEL),
        'w_ffn_up': w(ks[13], (D_MODEL, 2 * D_FF), D_MODEL),
        'w_ffn_down': w(ks[14], (D_FF, D_MODEL), D_FF),
        'norm_ffn_post': gain(ks[15], D_MODEL),
    }


def _fwd_reference(x, norm_mix_pre, w_in, conv_dw_w, conv_dw_b, conv_ln_g, conv_ln_b,
              w_conv_branch, b_conv_branch, w_att_branch, w_out, norm_mix_post,
              norm_ffn_pre, w_ffn_up, w_ffn_down, norm_ffn_post):
    b, s, d = x.shape
    for _ in range(DEPTH):
        h = rmsnorm(x, norm_mix_pre)
        proj = jnp.einsum('bsd,de->bse', h, w_in)
        splits = np.cumsum([2 * CONV_DIM, ATT_DIM, ATT_DIM, ATT_DIM, D_MODEL]).tolist()
        conv_in, q, k, v, g_conv, g_att = jnp.split(proj, splits, axis=-1)

        u = jax.nn.glu(conv_in, axis=-1)
        u = causal_depthwise_conv(u, conv_dw_w, conv_dw_b)
        u = jax.nn.silu(layernorm(u, conv_ln_g, conv_ln_b))
        conv_out = jnp.einsum('bsc,cd->bsd', u, w_conv_branch) + b_conv_branch

        q = q.reshape(b, s, N_HEADS, HEAD_DIM)
        k = k.reshape(b, s, N_HEADS, HEAD_DIM)
        v = v.reshape(b, s, N_HEADS, HEAD_DIM)
        att = stick_breaking_attention(q, k, v).reshape(b, s, ATT_DIM)
        att_out = jnp.einsum('bsc,cd->bsd', att, w_att_branch)

        merged = jax.nn.sigmoid(g_conv) * conv_out + jax.nn.sigmoid(g_att) * att_out
        mix = jnp.einsum('bsd,de->bse', merged, w_out)
        x = x + rmsnorm(mix, norm_mix_post)

        h = rmsnorm(x, norm_ffn_pre)
        gu = jnp.einsum('bsd,df->bsf', h, w_ffn_up)
        gate, up = jnp.split(gu, 2, axis=-1)
        ff = jnp.einsum('bsf,fd->bsd', jax.nn.silu(gate) * up, w_ffn_down)
        x = x + rmsnorm(ff, norm_ffn_post)
    return x


import jax as _jax
import jax.numpy as _jnp

TWIN_FORMAT = 'train_step'
FWD_PARAMS = ['x', 'norm_mix_pre', 'w_in', 'conv_dw_w', 'conv_dw_b', 'conv_ln_g', 'conv_ln_b', 'w_conv_branch', 'b_conv_branch', 'w_att_branch', 'w_out', 'norm_mix_post', 'norm_ffn_pre', 'w_ffn_up', 'w_ffn_down', 'norm_ffn_post']
TWIN_WEIGHTS = ['norm_mix_pre', 'w_in', 'conv_dw_w', 'conv_dw_b', 'conv_ln_g', 'conv_ln_b', 'w_conv_branch', 'b_conv_branch', 'w_att_branch', 'w_out', 'norm_mix_post', 'norm_ffn_pre', 'w_ffn_up', 'w_ffn_down', 'norm_ffn_post']
TWIN_DIFF_INPUT = 'x'
TWIN_INPUTS = ['x', 'norm_mix_pre', 'w_in', 'conv_dw_w', 'conv_dw_b', 'conv_ln_g', 'conv_ln_b', 'w_conv_branch', 'b_conv_branch', 'w_att_branch', 'w_out', 'norm_mix_post', 'norm_ffn_pre', 'w_ffn_up', 'w_ffn_down', 'norm_ffn_post', 'loss_target', 'm_norm_mix_pre', 'm_w_in', 'm_conv_dw_w', 'm_conv_dw_b', 'm_conv_ln_g', 'm_conv_ln_b', 'm_w_conv_branch', 'm_b_conv_branch', 'm_w_att_branch', 'm_w_out', 'm_norm_mix_post', 'm_norm_ffn_pre', 'm_w_ffn_up', 'm_w_ffn_down', 'm_norm_ffn_post', 'v_norm_mix_pre', 'v_w_in', 'v_conv_dw_w', 'v_conv_dw_b', 'v_conv_ln_g', 'v_conv_ln_b', 'v_w_conv_branch', 'v_b_conv_branch', 'v_w_att_branch', 'v_w_out', 'v_norm_mix_post', 'v_norm_ffn_pre', 'v_w_ffn_up', 'v_w_ffn_down', 'v_norm_ffn_post']
TWIN_OUTPUTS = ['loss', 'grad_x', 'grad_norm_mix_pre', 'grad_w_in', 'grad_conv_dw_w', 'grad_conv_dw_b', 'grad_conv_ln_g', 'grad_conv_ln_b', 'grad_w_conv_branch', 'grad_b_conv_branch', 'grad_w_att_branch', 'grad_w_out', 'grad_norm_mix_post', 'grad_norm_ffn_pre', 'grad_w_ffn_up', 'grad_w_ffn_down', 'grad_norm_ffn_post', 'delta_norm_mix_pre', 'delta_w_in', 'delta_conv_dw_w', 'delta_conv_dw_b', 'delta_conv_ln_g', 'delta_conv_ln_b', 'delta_w_conv_branch', 'delta_b_conv_branch', 'delta_w_att_branch', 'delta_w_out', 'delta_norm_mix_post', 'delta_norm_ffn_pre', 'delta_w_ffn_up', 'delta_w_ffn_down', 'delta_norm_ffn_post', 'new_m_norm_mix_pre', 'new_m_w_in', 'new_m_conv_dw_w', 'new_m_conv_dw_b', 'new_m_conv_ln_g', 'new_m_conv_ln_b', 'new_m_w_conv_branch', 'new_m_b_conv_branch', 'new_m_w_att_branch', 'new_m_w_out', 'new_m_norm_mix_post', 'new_m_norm_ffn_pre', 'new_m_w_ffn_up', 'new_m_w_ffn_down', 'new_m_norm_ffn_post', 'new_v_norm_mix_pre', 'new_v_w_in', 'new_v_conv_dw_w', 'new_v_conv_dw_b', 'new_v_conv_ln_g', 'new_v_conv_ln_b', 'new_v_w_conv_branch', 'new_v_b_conv_branch', 'new_v_w_att_branch', 'new_v_w_out', 'new_v_norm_mix_post', 'new_v_norm_ffn_pre', 'new_v_w_ffn_up', 'new_v_w_ffn_down', 'new_v_norm_ffn_post']
TWIN_LEAF_KINDS = {'loss': 'loss', 'grad_x': 'grad_x', 'grad_norm_mix_pre': 'grad_w', 'grad_w_in': 'grad_w', 'grad_conv_dw_w': 'grad_w', 'grad_conv_dw_b': 'grad_w', 'grad_conv_ln_g': 'grad_w', 'grad_conv_ln_b': 'grad_w', 'grad_w_conv_branch': 'grad_w', 'grad_b_conv_branch': 'grad_w', 'grad_w_att_branch': 'grad_w', 'grad_w_out': 'grad_w', 'grad_norm_mix_post': 'grad_w', 'grad_norm_ffn_pre': 'grad_w', 'grad_w_ffn_up': 'grad_w', 'grad_w_ffn_down': 'grad_w', 'grad_norm_ffn_post': 'grad_w', 'delta_norm_mix_pre': 'delta_w', 'delta_w_in': 'delta_w', 'delta_conv_dw_w': 'delta_w', 'delta_conv_dw_b': 'delta_w', 'delta_conv_ln_g': 'delta_w', 'delta_conv_ln_b': 'delta_w', 'delta_w_conv_branch': 'delta_w', 'delta_b_conv_branch': 'delta_w', 'delta_w_att_branch': 'delta_w', 'delta_w_out': 'delta_w', 'delta_norm_mix_post': 'delta_w', 'delta_norm_ffn_pre': 'delta_w', 'delta_w_ffn_up': 'delta_w', 'delta_w_ffn_down': 'delta_w', 'delta_norm_ffn_post': 'delta_w', 'new_m_norm_mix_pre': 'new_m', 'new_m_w_in': 'new_m', 'new_m_conv_dw_w': 'new_m', 'new_m_conv_dw_b': 'new_m', 'new_m_conv_ln_g': 'new_m', 'new_m_conv_ln_b': 'new_m', 'new_m_w_conv_branch': 'new_m', 'new_m_b_conv_branch': 'new_m', 'new_m_w_att_branch': 'new_m', 'new_m_w_out': 'new_m', 'new_m_norm_mix_post': 'new_m', 'new_m_norm_ffn_pre': 'new_m', 'new_m_w_ffn_up': 'new_m', 'new_m_w_ffn_down': 'new_m', 'new_m_norm_ffn_post': 'new_m', 'new_v_norm_mix_pre': 'new_v', 'new_v_w_in': 'new_v', 'new_v_conv_dw_w': 'new_v', 'new_v_conv_dw_b': 'new_v', 'new_v_conv_ln_g': 'new_v', 'new_v_conv_ln_b': 'new_v', 'new_v_w_conv_branch': 'new_v', 'new_v_b_conv_branch': 'new_v', 'new_v_w_att_branch': 'new_v', 'new_v_w_out': 'new_v', 'new_v_norm_mix_post': 'new_v', 'new_v_norm_ffn_pre': 'new_v', 'new_v_w_ffn_up': 'new_v', 'new_v_w_ffn_down': 'new_v', 'new_v_norm_ffn_post': 'new_v'}


def _forward(args):
    return _fwd_reference(*[args[k] for k in FWD_PARAMS])


def _output_shape():
    out = _jax.eval_shape(lambda: _forward(_fwd_setup_inputs(0)))
    return out.shape, out.dtype

N_MICROBATCH = 1
ADAM_LR = 0.001
ADAM_B1 = 0.9
ADAM_B2 = 0.999
ADAM_EPS = 1e-08
ADAM_WD = 0.01
ADAM_STEP = 10
PER_EXAMPLE_BATCH_AXIS = {'x': 0, 'loss_target': 0}
SHARED_INPUTS = []
_WEIGHT_DTYPES = {'norm_mix_pre': _jnp.float32, 'w_in': _jnp.float32, 'conv_dw_w': _jnp.float32, 'conv_dw_b': _jnp.float32, 'conv_ln_g': _jnp.float32, 'conv_ln_b': _jnp.float32, 'w_conv_branch': _jnp.float32, 'b_conv_branch': _jnp.float32, 'w_att_branch': _jnp.float32, 'w_out': _jnp.float32, 'norm_mix_post': _jnp.float32, 'norm_ffn_pre': _jnp.float32, 'w_ffn_up': _jnp.float32, 'w_ffn_down': _jnp.float32, 'norm_ffn_post': _jnp.float32}
MOMENT_SCALE = {'norm_mix_pre': 3.808924e-01, 'w_in': 1.820621e-01, 'conv_dw_w': 3.319006e-01, 'conv_dw_b': 1.933505e+00, 'conv_ln_g': 7.803060e-01, 'conv_ln_b': 1.161238e+00, 'w_conv_branch': 3.748289e-01, 'b_conv_branch': 1.742525e+00, 'w_att_branch': 2.519333e-01, 'w_out': 4.772344e-01, 'norm_mix_post': 1.598316e+01, 'norm_ffn_pre': 4.747043e-01, 'w_ffn_up': 1.977199e-01, 'w_ffn_down': 3.845292e-01, 'norm_ffn_post': 1.594908e+01}


def _to_microbatches(a, axis):
    t = _jnp.moveaxis(a, axis, 0)
    t = t.reshape((N_MICROBATCH, t.shape[0] // N_MICROBATCH) + t.shape[1:])
    return _jnp.moveaxis(t, 1, axis + 1)


def setup_inputs(seed: int = 0) -> dict:
    inp = _fwd_setup_inputs(seed)
    key = _jax.random.fold_in(_jax.random.key(seed), 7919)
    shape, _ = _output_shape()
    out = dict(inp)
    out["loss_target"] = _jax.random.normal(_jax.random.fold_in(key, 0), shape, _jnp.float32)
    for i, name in enumerate(TWIN_WEIGHTS):
        w = inp[name].astype(_jnp.float32)
        if MOMENT_SCALE is None:
            s = _jnp.sqrt(_jnp.mean(_jnp.square(w)) + 1e-30)
        else:
            s = MOMENT_SCALE[name]
        km, kv = _jax.random.split(_jax.random.fold_in(key, i + 1))
        out[name] = w
        out["m_" + name] = s * _jax.random.normal(km, w.shape, _jnp.float32)
        out["v_" + name] = (s * s) * _jax.random.uniform(kv, w.shape, _jnp.float32, 0.5, 1.5)
    if N_MICROBATCH > 1:
        for name, axis in PER_EXAMPLE_BATCH_AXIS.items():
            out[name] = _to_microbatches(out[name], axis)
    return {'x': out['x'], 'norm_mix_pre': out['norm_mix_pre'], 'w_in': out['w_in'], 'conv_dw_w': out['conv_dw_w'], 'conv_dw_b': out['conv_dw_b'], 'conv_ln_g': out['conv_ln_g'], 'conv_ln_b': out['conv_ln_b'], 'w_conv_branch': out['w_conv_branch'], 'b_conv_branch': out['b_conv_branch'], 'w_att_branch': out['w_att_branch'], 'w_out': out['w_out'], 'norm_mix_post': out['norm_mix_post'], 'norm_ffn_pre': out['norm_ffn_pre'], 'w_ffn_up': out['w_ffn_up'], 'w_ffn_down': out['w_ffn_down'], 'norm_ffn_post': out['norm_ffn_post'], 'loss_target': out['loss_target'], 'm_norm_mix_pre': out['m_norm_mix_pre'], 'm_w_in': out['m_w_in'], 'm_conv_dw_w': out['m_conv_dw_w'], 'm_conv_dw_b': out['m_conv_dw_b'], 'm_conv_ln_g': out['m_conv_ln_g'], 'm_conv_ln_b': out['m_conv_ln_b'], 'm_w_conv_branch': out['m_w_conv_branch'], 'm_b_conv_branch': out['m_b_conv_branch'], 'm_w_att_branch': out['m_w_att_branch'], 'm_w_out': out['m_w_out'], 'm_norm_mix_post': out['m_norm_mix_post'], 'm_norm_ffn_pre': out['m_norm_ffn_pre'], 'm_w_ffn_up': out['m_w_ffn_up'], 'm_w_ffn_down': out['m_w_ffn_down'], 'm_norm_ffn_post': out['m_norm_ffn_post'], 'v_norm_mix_pre': out['v_norm_mix_pre'], 'v_w_in': out['v_w_in'], 'v_conv_dw_w': out['v_conv_dw_w'], 'v_conv_dw_b': out['v_conv_dw_b'], 'v_conv_ln_g': out['v_conv_ln_g'], 'v_conv_ln_b': out['v_conv_ln_b'], 'v_w_conv_branch': out['v_w_conv_branch'], 'v_b_conv_branch': out['v_b_conv_branch'], 'v_w_att_branch': out['v_w_att_branch'], 'v_w_out': out['v_w_out'], 'v_norm_mix_post': out['v_norm_mix_post'], 'v_norm_ffn_pre': out['v_norm_ffn_pre'], 'v_w_ffn_up': out['v_w_ffn_up'], 'v_w_ffn_down': out['v_w_ffn_down'], 'v_norm_ffn_post': out['v_norm_ffn_post']}


def _loss(weights, diff, rest, loss_target):
    with _jax.named_scope("forward"):
        args = {**rest, TWIN_DIFF_INPUT: diff, **{k: w.astype(_WEIGHT_DTYPES[k]) for k, w in weights.items()}}
        y = _forward(args)
    with _jax.named_scope("loss_head"):
        err = _jnp.square(y.astype(_jnp.float32) - loss_target)
        return 0.5 * _jnp.sum(_jnp.mean(err, axis=-1)) if err.ndim else 0.5 * err


def _adamw(w, g, m, v):
    m = ADAM_B1 * m + (1.0 - ADAM_B1) * g
    v = ADAM_B2 * v + (1.0 - ADAM_B2) * _jnp.square(g)
    m_hat = m / (1.0 - ADAM_B1 ** ADAM_STEP)
    v_hat = v / (1.0 - ADAM_B2 ** ADAM_STEP)
    delta = -ADAM_LR * (m_hat / (_jnp.sqrt(v_hat) + ADAM_EPS) + ADAM_WD * w)
    return delta, m, v


def reference(x, norm_mix_pre, w_in, conv_dw_w, conv_dw_b, conv_ln_g, conv_ln_b, w_conv_branch, b_conv_branch, w_att_branch, w_out, norm_mix_post, norm_ffn_pre, w_ffn_up, w_ffn_down, norm_ffn_post, loss_target, m_norm_mix_pre, m_w_in, m_conv_dw_w, m_conv_dw_b, m_conv_ln_g, m_conv_ln_b, m_w_conv_branch, m_b_conv_branch, m_w_att_branch, m_w_out, m_norm_mix_post, m_norm_ffn_pre, m_w_ffn_up, m_w_ffn_down, m_norm_ffn_post, v_norm_mix_pre, v_w_in, v_conv_dw_w, v_conv_dw_b, v_conv_ln_g, v_conv_ln_b, v_w_conv_branch, v_b_conv_branch, v_w_att_branch, v_w_out, v_norm_mix_post, v_norm_ffn_pre, v_w_ffn_up, v_w_ffn_down, v_norm_ffn_post):
    given = dict(x=x, norm_mix_pre=norm_mix_pre, w_in=w_in, conv_dw_w=conv_dw_w, conv_dw_b=conv_dw_b, conv_ln_g=conv_ln_g, conv_ln_b=conv_ln_b, w_conv_branch=w_conv_branch, b_conv_branch=b_conv_branch, w_att_branch=w_att_branch, w_out=w_out, norm_mix_post=norm_mix_post, norm_ffn_pre=norm_ffn_pre, w_ffn_up=w_ffn_up, w_ffn_down=w_ffn_down, norm_ffn_post=norm_ffn_post, loss_target=loss_target, m_norm_mix_pre=m_norm_mix_pre, m_w_in=m_w_in, m_conv_dw_w=m_conv_dw_w, m_conv_dw_b=m_conv_dw_b, m_conv_ln_g=m_conv_ln_g, m_conv_ln_b=m_conv_ln_b, m_w_conv_branch=m_w_conv_branch, m_b_conv_branch=m_b_conv_branch, m_w_att_branch=m_w_att_branch, m_w_out=m_w_out, m_norm_mix_post=m_norm_mix_post, m_norm_ffn_pre=m_norm_ffn_pre, m_w_ffn_up=m_w_ffn_up, m_w_ffn_down=m_w_ffn_down, m_norm_ffn_post=m_norm_ffn_post, v_norm_mix_pre=v_norm_mix_pre, v_w_in=v_w_in, v_conv_dw_w=v_conv_dw_w, v_conv_dw_b=v_conv_dw_b, v_conv_ln_g=v_conv_ln_g, v_conv_ln_b=v_conv_ln_b, v_w_conv_branch=v_w_conv_branch, v_b_conv_branch=v_b_conv_branch, v_w_att_branch=v_w_att_branch, v_w_out=v_w_out, v_norm_mix_post=v_norm_mix_post, v_norm_ffn_pre=v_norm_ffn_pre, v_w_ffn_up=v_w_ffn_up, v_w_ffn_down=v_w_ffn_down, v_norm_ffn_post=v_norm_ffn_post)
    weights = {n: given[n] for n in TWIN_WEIGHTS}
    shared = {n: given[n] for n in SHARED_INPUTS}
    per_example = {n: given[n] for n in ['x']}
    grad_fn = _jax.value_and_grad(_loss, argnums=(0, 1))

    def one_microbatch(ex, loss_target):
        ex = dict(ex)
        diff = ex.pop(TWIN_DIFF_INPUT)
        return grad_fn(weights, diff, {**shared, **ex}, loss_target)

    if N_MICROBATCH == 1:
        loss, (grad_w, grad_x) = one_microbatch(per_example, given["loss_target"])
    else:
        def body(carry, xs):
            loss_sum, grad_sum = carry
            l_k, (gw_k, gx_k) = one_microbatch(xs[0], xs[1])
            with _jax.named_scope("update"):
                return (loss_sum + l_k, _jax.tree.map(_jnp.add, grad_sum, gw_k)), gx_k

        init = (_jnp.zeros((), _jnp.float32), _jax.tree.map(_jnp.zeros_like, weights))
        (loss, grad_w), grad_x = _jax.lax.scan(body, init, (per_example, given["loss_target"]))
    with _jax.named_scope("update"):
        delta_w, new_m, new_v = {}, {}, {}
        for n in TWIN_WEIGHTS:
            delta_w[n], new_m[n], new_v[n] = _adamw(weights[n], grad_w[n], given["m_" + n], given["v_" + n])
    return (loss, grad_x, *[grad_w[n] for n in TWIN_WEIGHTS], *[delta_w[n] for n in TWIN_WEIGHTS],
            *[new_m[n] for n in TWIN_WEIGHTS], *[new_v[n] for n in TWIN_WEIGHTS])
```

```python
import jax
import jax.numpy as jnp
from jax import lax
from jax.experimental import pallas as pl
from jax.experimental.pallas import tpu as pltpu

F32 = jnp.float32
MM = jnp.bfloat16

SEQ = 2048
D_MODEL = 1024
CONV_DIM = 512
ATT_DIM = 512
CONV_WIDTH = 31
D_FF = 2816
IN_COLS = 2 * CONV_DIM + 3 * ATT_DIM + 2 * D_MODEL
N_CHIPS = 4
IN_SHARD = IN_COLS // N_CHIPS
UP_SHARD = 2 * D_FF // N_CHIPS
BR_SHARD = D_MODEL // N_CHIPS
EPS = 1e-6
ATT_SCALE = 0.125

TM = 256
TQ = 128
CONV_TILE = 64
CONV_WIN = CONV_TILE + 32
VMEM_LIMIT = 56 * 1024 * 1024

ADAM_LR = 0.001
ADAM_B1 = 0.9
ADAM_B2 = 0.999
ADAM_EPS = 1e-08
ADAM_WD = 0.01
ADAM_STEP = 10

MESH = pl.DeviceIdType.MESH
ANY = pl.BlockSpec(memory_space=pl.ANY)
VMEM_SPEC = pl.BlockSpec(memory_space=pltpu.VMEM)

NT_DIMS = (((1,), (1,)), ((), ()))
TN_DIMS = (((0,), (0,)), ((), ()))

IN_PIECES = (("ci", 0, 1024), ("q", 1024, 1536), ("k", 1536, 2048), ("v", 2048, 2560),
             ("gc", 2560, 3584), ("ga", 3584, 4608))


def _params(sem=None, vmem=VMEM_LIMIT):
    return pltpu.CompilerParams(dimension_semantics=sem, vmem_limit_bytes=vmem)


def _dot(a, b):
    return jnp.dot(a, b, preferred_element_type=F32)


def _dot_nt(a, b):
    return lax.dot_general(a, b, NT_DIMS, preferred_element_type=F32)


def _dot_tn(a, b):
    return lax.dot_general(a, b, TN_DIMS, preferred_element_type=F32)


def _sigmoid(x):
    return 1.0 / (1.0 + jnp.exp(-x))


def _rms(x):
    r = lax.rsqrt(jnp.mean(x * x, axis=-1, keepdims=True) + EPS)
    return x * r, r


def _rms_bwd(dy_g, n, r):
    return r * (dy_g - n * jnp.mean(dy_g * n, axis=-1, keepdims=True))


def _row_tile_spec(width, tm=TM):
    return pl.BlockSpec((tm, width), lambda i: (i, 0))


def _full_spec(shape):
    nd = len(shape)
    return pl.BlockSpec(shape, lambda *_: (0,) * nd)


def _acc_rows(ref, val, first):
    @pl.when(first)
    def _():
        ref[...] = val

    @pl.when(jnp.logical_not(first))
    def _():
        ref[...] += val


def in_proj_fwd(x, g1, w_in_g):
    def body(x_ref, g_ref, w_ref, h_ref, ci_ref, q_ref, k_ref, v_ref, gc_ref, ga_ref):
        n, _ = _rms(x_ref[...])
        h = (n * g_ref[...]).astype(MM)
        h_ref[...] = h
        outs = dict(ci=ci_ref, q=q_ref, k=k_ref, v=v_ref, gc=gc_ref, ga=ga_ref)
        for j in range(N_CHIPS):
            p = _dot(h, w_ref[j])
            g0 = j * IN_SHARD
            for name, s, e in IN_PIECES:
                lo, hi = max(s, g0), min(e, g0 + IN_SHARD)
                if lo < hi:
                    ref = outs[name]
                    ref[:, lo - s:hi - s] = p[:, lo - g0:hi - g0].astype(ref.dtype)

    out_shape = (
        jax.ShapeDtypeStruct((SEQ, D_MODEL), MM),
        jax.ShapeDtypeStruct((SEQ, 2 * CONV_DIM), F32),
        jax.ShapeDtypeStruct((SEQ, ATT_DIM), MM),
        jax.ShapeDtypeStruct((SEQ, ATT_DIM), MM),
        jax.ShapeDtypeStruct((SEQ, ATT_DIM), MM),
        jax.ShapeDtypeStruct((SEQ, D_MODEL), F32),
        jax.ShapeDtypeStruct((SEQ, D_MODEL), F32),
    )
    return pl.pallas_call(
        body, name="in_proj_fwd", grid=(SEQ // TM,), out_shape=out_shape,
        in_specs=[_row_tile_spec(D_MODEL), _full_spec((1, D_MODEL)), _full_spec(w_in_g.shape)],
        out_specs=[_row_tile_spec(s.shape[1]) for s in out_shape],
        compiler_params=_params(("arbitrary",)),
    )(x, g1, w_in_g)


def _shifted_sum(win, terms):
    by_rot = {}
    for m, coef in terms:
        by_rot.setdefault(m % 8, []).append((m // 8, coef))
    acc = None
    n = win.shape[0]
    for rot in sorted(by_rot):
        shifted = win if rot == 0 else pltpu.roll(win, n - rot, 0)
        for a, coef in by_rot[rot]:
            t = coef * shifted[8 * a:8 * a + CONV_TILE, :]
            acc = t if acc is None else acc + t
    return acc


def _glu_into(ci_ref, upad_ref):
    upad_ref[0:32, :] = jnp.zeros((32, CONV_DIM), F32)

    def step(i, c):
        t0 = pl.multiple_of(i * TM, TM)
        a = ci_ref[pl.ds(t0, TM), 0:CONV_DIM]
        b = ci_ref[pl.ds(t0, TM), CONV_DIM:2 * CONV_DIM]
        upad_ref[pl.ds(t0 + 32, TM), :] = a * _sigmoid(b)
        return c

    lax.fori_loop(0, SEQ // TM, step, 0)


def _layernorm_parts(u1):
    mu = jnp.mean(u1, axis=-1, keepdims=True)
    xc = u1 - mu
    rstd = lax.rsqrt(jnp.mean(xc * xc, axis=-1, keepdims=True) + EPS)
    return xc * rstd, rstd


def conv_fwd(ci, w_dw, b_dw, ln_g, ln_b):
    def body(ci_ref, w_ref, b_ref, g_ref, bb_ref, u1_ref, u3_ref, upad_ref):
        _glu_into(ci_ref, upad_ref)

        def step(i, c):
            t0 = pl.multiple_of(i * CONV_TILE, CONV_TILE)
            win = upad_ref[pl.ds(t0, CONV_WIN), :]
            u1 = _shifted_sum(win, [(j + 2, w_ref[j:j + 1, :]) for j in range(CONV_WIDTH)]) + b_ref[...]
            u1_ref[pl.ds(t0, CONV_TILE), :] = u1
            xh, _ = _layernorm_parts(u1)
            u2 = xh * g_ref[...] + bb_ref[...]
            u3_ref[pl.ds(t0, CONV_TILE), :] = (u2 * _sigmoid(u2)).astype(MM)
            return c

        lax.fori_loop(0, SEQ // CONV_TILE, step, 0)

    return pl.pallas_call(
        body, name="conv_fwd",
        out_shape=(jax.ShapeDtypeStruct((SEQ, CONV_DIM), F32), jax.ShapeDtypeStruct((SEQ, CONV_DIM), MM)),
        in_specs=[VMEM_SPEC] * 5, out_specs=[VMEM_SPEC] * 2,
        scratch_shapes=[pltpu.VMEM((SEQ + 32, CONV_DIM), F32)],
        compiler_params=_params(),
    )(ci, w_dw, b_dw, ln_g, ln_b)


def _softplus(z):
    return jnp.maximum(z, 0.0) + jnp.log1p(jnp.exp(-jnp.abs(z)))


def _split_dot(a, u):
    hi = a.astype(MM)
    lo = (a - hi.astype(F32)).astype(MM)
    return _dot(hi, u) + _dot(lo, u)


def _head_masks():
    lane = lax.broadcasted_iota(jnp.int32, (TQ, 128), 1)
    row = lax.broadcasted_iota(jnp.int32, (TQ, 128), 0)
    return lane, row, lane < 64


def _pick_head(x, head0, h):
    zero = jnp.zeros_like(x)
    return jnp.where(head0, x, zero) if h == 0 else jnp.where(head0, zero, x)


def attn_fwd(q, k, v):
    def body(q_ref, k_ref, v_ref, o_ref, rc_ref, acc_ref, rcs_ref):
        i = pl.program_id(1)
        lane, row, head0 = _head_masks()
        qb = q_ref[...]
        qm = [_pick_head(qb, head0, h) for h in range(2)]
        u_suffix = (row >= lane).astype(MM)
        acc_ref[...] = jnp.zeros_like(acc_ref)
        rcs_ref[...] = jnp.zeros_like(rcs_ref)

        def kstep(n, carry):
            jb = i - n
            k0 = pl.multiple_of(jb * TQ, TQ)
            kb = k_ref[pl.ds(k0, TQ), :]
            vb = v_ref[pl.ds(k0, TQ), :]
            valid = (lane + jb * TQ) < (row + i * TQ)
            new = []
            for h in range(2):
                z = _dot_nt(qm[h], kb) * ATT_SCALE
                sp = jnp.where(valid, _softplus(z), 0.0)
                c_incl = _split_dot(sp, u_suffix) + carry[h]
                a = jnp.where(valid, jnp.exp(z - c_incl), 0.0)
                acc_ref[...] += _dot(a.astype(MM), _pick_head(vb, head0, h))
                rcs_ref[h] = jnp.where(lane == jb, carry[h], rcs_ref[h])
                new.append(carry[h] + jnp.sum(sp, axis=1, keepdims=True))
            return tuple(new)

        zero = jnp.zeros((TQ, 1), F32)
        lax.fori_loop(0, i + 1, kstep, (zero, zero))
        o_ref[...] = acc_ref[...].astype(MM)
        rc_ref[...] = rcs_ref[...]

    nqb = SEQ // TQ
    return pl.pallas_call(
        body, name="attn_fwd", grid=(ATT_DIM // 128, nqb),
        out_shape=(jax.ShapeDtypeStruct((SEQ, ATT_DIM), MM), jax.ShapeDtypeStruct((8, SEQ, 128), F32)),
        in_specs=[pl.BlockSpec((TQ, 128), lambda p, i: (i, p)),
                  pl.BlockSpec((SEQ, 128), lambda p, i: (0, p)),
                  pl.BlockSpec((SEQ, 128), lambda p, i: (0, p))],
        out_specs=[pl.BlockSpec((TQ, 128), lambda p, i: (i, p)),
                   pl.BlockSpec((2, TQ, 128), lambda p, i: (p, i, 0))],
        scratch_shapes=[pltpu.VMEM((TQ, 128), F32), pltpu.VMEM((2, TQ, 128), F32)],
        compiler_params=_params(("arbitrary", "arbitrary")),
    )(q, k, v)


def mix_fwd(u3, att, gc, ga, x, w_cb_g, b_cb, w_ab_g, w_out_g, g2, g3):
    def body(u_ref, a_ref, gc_ref, ga_ref, x_ref, wcb_ref, bcb_ref, wab_ref, wout_ref, g2_ref, g3_ref,
             co_ref, ao_ref, mg_ref, mix_ref, x2_ref, h2_ref):
        u = u_ref[...]
        a = a_ref[...]
        for j in range(N_CHIPS):
            cols = slice(j * BR_SHARD, (j + 1) * BR_SHARD)
            co_ref[:, cols] = _dot(u, wcb_ref[j]) + bcb_ref[:, cols]
            ao_ref[:, cols] = _dot(a, wab_ref[j])
        merged = (_sigmoid(gc_ref[...]) * co_ref[...] + _sigmoid(ga_ref[...]) * ao_ref[...]).astype(MM)
        mg_ref[...] = merged
        mix = _dot(merged, wout_ref[...])
        mix_ref[...] = mix
        n2, _ = _rms(mix)
        x2 = x_ref[...] + n2 * g2_ref[...]
        x2_ref[...] = x2
        n3, _ = _rms(x2)
        h2_ref[...] = (n3 * g3_ref[...]).astype(MM)

    out_shape = (
        jax.ShapeDtypeStruct((SEQ, D_MODEL), F32), jax.ShapeDtypeStruct((SEQ, D_MODEL), F32),
        jax.ShapeDtypeStruct((SEQ, D_MODEL), MM), jax.ShapeDtypeStruct((SEQ, D_MODEL), F32),
        jax.ShapeDtypeStruct((SEQ, D_MODEL), F32), jax.ShapeDtypeStruct((SEQ, D_MODEL), MM),
    )
    vec = _full_spec((1, D_MODEL))
    return pl.pallas_call(
        body, name="mix_fwd", grid=(SEQ // TM,), out_shape=out_shape,
        in_specs=[_row_tile_spec(CONV_DIM), _row_tile_spec(ATT_DIM), _row_tile_spec(D_MODEL),
                  _row_tile_spec(D_MODEL), _row_tile_spec(D_MODEL), _full_spec(w_cb_g.shape), vec,
                  _full_spec(w_ab_g.shape), _full_spec(w_out_g.shape), vec, vec],
        out_specs=[_row_tile_spec(D_MODEL)] * 6,
        compiler_params=_params(("arbitrary",)),
    )(u3, att, gc, ga, x, w_cb_g, b_cb, w_ab_g, w_out_g, g2, g3)


def ffn_up_fwd(h2, w_up_g):
    def body(h_ref, wg_ref, wu_ref, gate_ref, up_ref, act_ref):
        h = h_ref[...]
        gate = _dot(h, wg_ref[0])
        up = _dot(h, wu_ref[0])
        gate_ref[...] = gate
        up_ref[...] = up
        act_ref[...] = (gate * _sigmoid(gate) * up).astype(MM)

    tile = pl.BlockSpec((TM, UP_SHARD), lambda n, i: (i, n))
    return pl.pallas_call(
        body, name="ffn_up_fwd", grid=(2, SEQ // TM),
        out_shape=(jax.ShapeDtypeStruct((SEQ, D_FF), F32), jax.ShapeDtypeStruct((SEQ, D_FF), F32),
                   jax.ShapeDtypeStruct((SEQ, D_FF), MM)),
        in_specs=[pl.BlockSpec((TM, D_MODEL), lambda n, i: (i, 0)),
                  pl.BlockSpec((1, D_MODEL, UP_SHARD), lambda n, i: (n, 0, 0)),
                  pl.BlockSpec((1, D_MODEL, UP_SHARD), lambda n, i: (n + 2, 0, 0))],
        out_specs=[tile, tile, tile],
        compiler_params=_params(("arbitrary", "arbitrary")),
    )(h2, w_up_g, w_up_g)


def ffn_down_loss(act, w_down_g, x2, target, g4):
    def body(act_ref, wd_ref, x2_ref, t_ref, g_ref, dff_ref, dy_ref, loss_ref, dg_ref):
        ff = _dot(act_ref[...], wd_ref[...])
        n4, r4 = _rms(ff)
        g4v = g_ref[...]
        err = x2_ref[...] + n4 * g4v - t_ref[...]
        row_loss = jnp.mean(err * err, axis=-1, keepdims=True)
        loss_ref[...] = jnp.zeros((8, 128), F32) + 0.5 * jnp.sum(row_loss, axis=0, keepdims=True)
        dy = err * (1.0 / D_MODEL)
        dy_ref[...] = dy
        dff_ref[...] = _rms_bwd(dy * g4v, n4, r4).astype(MM)
        _acc_rows(dg_ref, jnp.sum(dy * n4, axis=0, keepdims=True), pl.program_id(0) == 0)

    nt = SEQ // TM
    vec = _full_spec((1, D_MODEL))
    return pl.pallas_call(
        body, name="ffn_down_loss", grid=(nt,),
        out_shape=(jax.ShapeDtypeStruct((SEQ, D_MODEL), MM), jax.ShapeDtypeStruct((SEQ, D_MODEL), F32),
                   jax.ShapeDtypeStruct((nt * 8, 128), F32), jax.ShapeDtypeStruct((1, D_MODEL), F32)),
        in_specs=[_row_tile_spec(D_FF), _full_spec(w_down_g.shape), _row_tile_spec(D_MODEL),
                  _row_tile_spec(D_MODEL), vec],
        out_specs=[_row_tile_spec(D_MODEL), _row_tile_spec(D_MODEL),
                   pl.BlockSpec((8, 128), lambda i: (i, 0)), vec],
        compiler_params=_params(("arbitrary",)),
    )(act, w_down_g, x2, target, g4)


def ffn_act_bwd(dff, w_down_g, gate, up):
    def body(dff_ref, wd_ref, gate_ref, up_ref, dgate_ref, dup_ref):
        dact = _dot_nt(dff_ref[...], wd_ref[...])
        gate = gate_ref[...]
        sg = _sigmoid(gate)
        dgate_ref[...] = (dact * up_ref[...] * (sg * (1.0 + gate * (1.0 - sg)))).astype(MM)
        dup_ref[...] = (dact * (gate * sg)).astype(MM)

    return pl.pallas_call(
        body, name="ffn_act_bwd", grid=(SEQ // TM,),
        out_shape=(jax.ShapeDtypeStruct((SEQ, D_FF), MM), jax.ShapeDtypeStruct((SEQ, D_FF), MM)),
        in_specs=[_row_tile_spec(D_MODEL), _full_spec(w_down_g.shape), _row_tile_spec(D_FF), _row_tile_spec(D_FF)],
        out_specs=[_row_tile_spec(D_FF), _row_tile_spec(D_FF)],
        compiler_params=_params(("arbitrary",)),
    )(dff, w_down_g, gate, up)


def ffn_in_bwd(dgate, dup, w_up_g, x2, mix, dy, g3, g2):
    def body(dg_ref, du_ref, w_ref, x2_ref, mix_ref, dy_ref, g3_ref, g2_ref, dx2_ref, dmix_ref, dg3_ref, dg2_ref):
        dh2 = None
        for j in range(2):
            cols = slice(j * UP_SHARD, (j + 1) * UP_SHARD)
            t = _dot_nt(dg_ref[:, cols], w_ref[j]) + _dot_nt(du_ref[:, cols], w_ref[j + 2])
            dh2 = t if dh2 is None else dh2 + t
        first = pl.program_id(0) == 0
        n3, r3 = _rms(x2_ref[...])
        dx2 = dy_ref[...] + _rms_bwd(dh2 * g3_ref[...], n3, r3)
        dx2_ref[...] = dx2
        _acc_rows(dg3_ref, jnp.sum(dh2 * n3, axis=0, keepdims=True), first)
        n2, r2 = _rms(mix_ref[...])
        dmix_ref[...] = _rms_bwd(dx2 * g2_ref[...], n2, r2).astype(MM)
        _acc_rows(dg2_ref, jnp.sum(dx2 * n2, axis=0, keepdims=True), first)

    vec = _full_spec((1, D_MODEL))
    return pl.pallas_call(
        body, name="ffn_in_bwd", grid=(SEQ // TM,),
        out_shape=(jax.ShapeDtypeStruct((SEQ, D_MODEL), F32), jax.ShapeDtypeStruct((SEQ, D_MODEL), MM),
                   jax.ShapeDtypeStruct((1, D_MODEL), F32), jax.ShapeDtypeStruct((1, D_MODEL), F32)),
        in_specs=[_row_tile_spec(D_FF), _row_tile_spec(D_FF), _full_spec(w_up_g.shape), _row_tile_spec(D_MODEL),
                  _row_tile_spec(D_MODEL), _row_tile_spec(D_MODEL), vec, vec],
        out_specs=[_row_tile_spec(D_MODEL), _row_tile_spec(D_MODEL), vec, vec],
        compiler_params=_params(("arbitrary",)),
    )(dgate, dup, w_up_g, x2, mix, dy, g3, g2)


def merge_bwd(dmix, w_out_g, gc, ga, co, ao, w_cb_g, w_ab_g):
    def body(dmix_ref, wout_ref, gc_ref, ga_ref, co_ref, ao_ref, wcb_ref, wab_ref,
             dco_ref, dao_ref, dgc_ref, dga_ref, du3_ref, datt_ref, dbcb_ref):
        dm = _dot_nt(dmix_ref[...], wout_ref[...])
        sgc = _sigmoid(gc_ref[...])
        sga = _sigmoid(ga_ref[...])
        dco = dm * sgc
        dao = dm * sga
        dgc_ref[...] = (dm * co_ref[...] * (sgc * (1.0 - sgc))).astype(MM)
        dga_ref[...] = (dm * ao_ref[...] * (sga * (1.0 - sga))).astype(MM)
        _acc_rows(dbcb_ref, jnp.sum(dco, axis=0, keepdims=True), pl.program_id(0) == 0)
        dco_ref[...] = dco.astype(MM)
        dao_ref[...] = dao.astype(MM)
        du3 = None
        datt = None
        for j in range(N_CHIPS):
            cols = slice(j * BR_SHARD, (j + 1) * BR_SHARD)
            t = _dot_nt(dco_ref[:, cols], wcb_ref[j])
            s = _dot_nt(dao_ref[:, cols], wab_ref[j])
            du3 = t if du3 is None else du3 + t
            datt = s if datt is None else datt + s
        du3_ref[...] = du3
        datt_ref[...] = datt.astype(MM)

    wide = _row_tile_spec(D_MODEL)
    return pl.pallas_call(
        body, name="merge_bwd", grid=(SEQ // TM,),
        out_shape=(jax.ShapeDtypeStruct((SEQ, D_MODEL), MM), jax.ShapeDtypeStruct((SEQ, D_MODEL), MM),
                   jax.ShapeDtypeStruct((SEQ, D_MODEL), MM), jax.ShapeDtypeStruct((SEQ, D_MODEL), MM),
                   jax.ShapeDtypeStruct((SEQ, CONV_DIM), F32), jax.ShapeDtypeStruct((SEQ, ATT_DIM), MM),
                   jax.ShapeDtypeStruct((1, D_MODEL), F32)),
        in_specs=[wide, _full_spec(w_out_g.shape), wide, wide, wide, wide,
                  _full_spec(w_cb_g.shape), _full_spec(w_ab_g.shape)],
        out_specs=[wide, wide, wide, wide, _row_tile_spec(CONV_DIM), _row_tile_spec(ATT_DIM),
                   _full_spec((1, D_MODEL))],
        compiler_params=_params(("arbitrary",)),
    )(dmix, w_out_g, gc, ga, co, ao, w_cb_g, w_ab_g)


def conv_bwd(du3, u1, ci, w_dw, ln_g, ln_b):
    def body(du3_ref, u1_ref, ci_ref, w_ref, g_ref, bb_ref,
             dci_ref, dw_ref, dbdw_ref, dg_ref, db_ref, upad_ref, dpad_ref, dwacc_ref, vacc_ref):
        _glu_into(ci_ref, upad_ref)
        dpad_ref[SEQ:SEQ + 32, :] = jnp.zeros((32, CONV_DIM), F32)
        dwacc_ref[...] = jnp.zeros_like(dwacc_ref)
        vacc_ref[...] = jnp.zeros_like(vacc_ref)

        def fold8(t):
            s = t[0:8, :]
            for r in range(1, CONV_TILE // 8):
                s = s + t[8 * r:8 * r + 8, :]
            return s

        def pass1(i, c):
            t0 = pl.multiple_of(i * CONV_TILE, CONV_TILE)
            xh, rstd = _layernorm_parts(u1_ref[pl.ds(t0, CONV_TILE), :])
            gv = g_ref[...]
            u2 = xh * gv + bb_ref[...]
            s2 = _sigmoid(u2)
            du2 = du3_ref[pl.ds(t0, CONV_TILE), :] * (s2 * (1.0 + u2 * (1.0 - s2)))
            wv = du2 * gv
            du1 = rstd * (wv - jnp.mean(wv, axis=-1, keepdims=True)
                          - xh * jnp.mean(wv * xh, axis=-1, keepdims=True))
            dpad_ref[pl.ds(t0, CONV_TILE), :] = du1
            vacc_ref[0] += fold8(du2 * xh)
            vacc_ref[1] += fold8(du2)
            vacc_ref[2] += fold8(du1)
            win = upad_ref[pl.ds(t0, CONV_WIN), :]
            n = win.shape[0]
            for rot in range(8):
                shifted = win if rot == 0 else pltpu.roll(win, n - rot, 0)
                for a in range(5):
                    j = 8 * a + rot - 2
                    if 0 <= j < CONV_WIDTH:
                        dwacc_ref[j] += fold8(du1 * shifted[8 * a:8 * a + CONV_TILE, :])
            return c

        lax.fori_loop(0, SEQ // CONV_TILE, pass1, 0)

        def pass2(i, c):
            t0 = pl.multiple_of(i * CONV_TILE, CONV_TILE)
            win = dpad_ref[pl.ds(t0, CONV_WIN), :]
            du0 = _shifted_sum(win, [(30 - j, w_ref[j:j + 1, :]) for j in range(CONV_WIDTH)])
            a = ci_ref[pl.ds(t0, CONV_TILE), 0:CONV_DIM]
            sb = _sigmoid(ci_ref[pl.ds(t0, CONV_TILE), CONV_DIM:2 * CONV_DIM])
            dci_ref[pl.ds(t0, CONV_TILE), 0:CONV_DIM] = (du0 * sb).astype(MM)
            dci_ref[pl.ds(t0, CONV_TILE), CONV_DIM:2 * CONV_DIM] = (du0 * a * (sb * (1.0 - sb))).astype(MM)
            return c

        lax.fori_loop(0, SEQ // CONV_TILE, pass2, 0)

        for j in range(CONV_WIDTH):
            dw_ref[j:j + 1, :] = jnp.sum(dwacc_ref[j], axis=0, keepdims=True)
        dw_ref[CONV_WIDTH:32, :] = jnp.zeros((32 - CONV_WIDTH, CONV_DIM), F32)
        dg_ref[...] = jnp.sum(vacc_ref[0], axis=0, keepdims=True)
        db_ref[...] = jnp.sum(vacc_ref[1], axis=0, keepdims=True)
        dbdw_ref[...] = jnp.sum(vacc_ref[2], axis=0, keepdims=True)

    vec = jax.ShapeDtypeStruct((1, CONV_DIM), F32)
    return pl.pallas_call(
        body, name="conv_bwd",
        out_shape=(jax.ShapeDtypeStruct((SEQ, 2 * CONV_DIM), MM), jax.ShapeDtypeStruct((32, CONV_DIM), F32),
                   vec, vec, vec),
        in_specs=[VMEM_SPEC] * 6, out_specs=[VMEM_SPEC] * 5,
        scratch_shapes=[pltpu.VMEM((SEQ + 32, CONV_DIM), F32), pltpu.VMEM((SEQ + 32, CONV_DIM), F32),
                        pltpu.VMEM((CONV_WIDTH, 8, CONV_DIM), F32), pltpu.VMEM((3, 8, CONV_DIM), F32)],
        compiler_params=_params(),
    )(du3, u1, ci, w_dw, ln_g, ln_b)


def attn_bwd(q, k, v, datt, rc):
    nqb = SEQ // TQ

    def body(q_ref, k_ref, v_ref, do_ref, rc_ref, dq_ref, dk_ref, dv_ref, dqa_ref, dka_ref, dva_ref):
        i = pl.program_id(1)
        lane, row, head0 = _head_masks()

        @pl.when(i == 0)
        def _():
            dka_ref[...] = jnp.zeros_like(dka_ref)
            dva_ref[...] = jnp.zeros_like(dva_ref)

        dqa_ref[...] = jnp.zeros_like(dqa_ref)
        qb = q_ref[...]
        dob = do_ref[...]
        qm = [_pick_head(qb, head0, h) for h in range(2)]
        dom = [_pick_head(dob, head0, h) for h in range(2)]
        u_suffix = (row >= lane).astype(MM)
        u_prefix = (row <= lane).astype(MM)

        def kstep(jb, carry):
            k0 = pl.multiple_of(jb * TQ, TQ)
            kb = k_ref[pl.ds(k0, TQ), :]
            vb = v_ref[pl.ds(k0, TQ), :]
            valid = (lane + jb * TQ) < (row + i * TQ)
            new = []
            for h in range(2):
                z = _dot_nt(qm[h], kb) * ATT_SCALE
                sp_all = _softplus(z)
                sp = jnp.where(valid, sp_all, 0.0)
                r_in = jnp.sum(jnp.where(lane == jb, rc_ref[h], 0.0), axis=1, keepdims=True)
                c_incl = _split_dot(sp, u_suffix) + r_in
                a = jnp.where(valid, jnp.exp(z - c_incl), 0.0)
                sig = jnp.exp(z - sp_all)
                g = _dot_nt(dom[h], vb) * a
                dva_ref[pl.ds(k0, TQ), :] += _dot_tn(a.astype(MM), dom[h])
                p_incl = _split_dot(g, u_prefix) + carry[h]
                dz = (jnp.where(valid, g - sig * p_incl, 0.0) * ATT_SCALE).astype(MM)
                dqa_ref[...] += _dot(dz, _pick_head(kb, head0, h))
                dka_ref[pl.ds(k0, TQ), :] += _dot_tn(dz, qm[h])
                new.append(carry[h] + jnp.sum(g, axis=1, keepdims=True))
            return tuple(new)

        zero = jnp.zeros((TQ, 1), F32)
        lax.fori_loop(0, i + 1, kstep, (zero, zero))
        dq_ref[...] = dqa_ref[...].astype(MM)

        @pl.when(i == nqb - 1)
        def _():
            dk_ref[...] = dka_ref[...].astype(MM)
            dv_ref[...] = dva_ref[...].astype(MM)

    tile = pl.BlockSpec((TQ, 128), lambda p, i: (i, p))
    full = pl.BlockSpec((SEQ, 128), lambda p, i: (0, p))
    out = jax.ShapeDtypeStruct((SEQ, ATT_DIM), MM)
    return pl.pallas_call(
        body, name="attn_bwd", grid=(ATT_DIM // 128, nqb), out_shape=(out, out, out),
        in_specs=[tile, full, full, tile, pl.BlockSpec((2, TQ, 128), lambda p, i: (p, i, 0))],
        out_specs=[tile, full, full],
        scratch_shapes=[pltpu.VMEM((TQ, 128), F32), pltpu.VMEM((SEQ, 128), F32), pltpu.VMEM((SEQ, 128), F32)],
        compiler_params=_params(("arbitrary", "arbitrary")),
    )(q, k, v, datt, rc)


def in_proj_bwd(dproj, w_in_g, x, dx2, g1):
    def body(dp_ref, w_ref, x_ref, dx2_ref, g_ref, dx_ref, dg_ref):
        dh = None
        for j in range(N_CHIPS):
            t = _dot_nt(dp_ref[:, j * IN_SHARD:(j + 1) * IN_SHARD], w_ref[j])
            dh = t if dh is None else dh + t
        n1, r1 = _rms(x_ref[...])
        dx_ref[...] = dx2_ref[...] + _rms_bwd(dh * g_ref[...], n1, r1)
        _acc_rows(dg_ref, jnp.sum(dh * n1, axis=0, keepdims=True), pl.program_id(0) == 0)

    vec = _full_spec((1, D_MODEL))
    return pl.pallas_call(
        body, name="in_proj_bwd", grid=(SEQ // TM,),
        out_shape=(jax.ShapeDtypeStruct((SEQ, D_MODEL), F32), jax.ShapeDtypeStruct((1, D_MODEL), F32)),
        in_specs=[_row_tile_spec(IN_COLS), _full_spec(w_in_g.shape), _row_tile_spec(D_MODEL),
                  _row_tile_spec(D_MODEL), vec],
        out_specs=[_row_tile_spec(D_MODEL), vec],
        compiler_params=_params(("arbitrary",)),
    )(dproj, w_in_g, x, dx2, g1)


def weight_grad(a, b, name, col_sharded, tk=None):
    kin, n = a.shape[1], b.shape[1]

    def body(a_ref, b_ref, o_ref):
        if col_sharded:
            o_ref[0, 0] = _dot_tn(a_ref[...], b_ref[...])
        else:
            o_ref[0] = _dot_tn(a_ref[...], b_ref[...])

    if col_sharded:
        kh, ns = kin // 2, n // N_CHIPS
        out = jax.ShapeDtypeStruct((2, N_CHIPS, kh, ns), F32)
        grid = (2, N_CHIPS)
        in_specs = [pl.BlockSpec((SEQ, kh), lambda h, j: (0, h)), pl.BlockSpec((SEQ, ns), lambda h, j: (0, j))]
        out_spec = pl.BlockSpec((1, 1, kh, ns), lambda h, j: (h, j, 0, 0))
    else:
        nh = n // 2
        out = jax.ShapeDtypeStruct((2, kin, nh), F32)
        grid = (2, kin // tk)
        in_specs = [pl.BlockSpec((SEQ, tk), lambda h, r: (0, r)), pl.BlockSpec((SEQ, nh), lambda h, r: (0, h))]
        out_spec = pl.BlockSpec((1, tk, nh), lambda h, r: (h, r, 0))
    res = pl.pallas_call(
        body, name=name, grid=grid, out_shape=out, in_specs=in_specs, out_specs=out_spec,
        compiler_params=_params(("arbitrary", "arbitrary")),
    )(a, b)
    if not col_sharded:
        res = res.reshape(2, N_CHIPS, kin // N_CHIPS, nh)
    return res


def _place():
    x, y, c = lax.axis_index("x"), lax.axis_index("y"), lax.axis_index("c")
    chips = [(1 - x, y), (x, 1 - y), (1 - x, 1 - y)]
    return x, y, c, chips


def _rcopy(src, dst, send_sem, recv_sem, dev):
    return pltpu.make_async_remote_copy(src_ref=src, dst_ref=dst, send_sem=send_sem, recv_sem=recv_sem,
                                        device_id=dev, device_id_type=MESH)


def all_gather_weights(shards, small):
    n = len(shards)

    def body(*refs):
        w = refs[:n]
        sm = refs[n]
        o = refs[n + 1:2 * n + 1]
        osm = refs[2 * n + 1]
        send, recv, fsend, frecv, ssend, srecv, loc = refs[2 * n + 2:]
        x, y, c, chips = _place()
        me = 2 * x + y
        sib = (x, y, 1 - c)

        def half(ref, k, cc, t):
            rh = shards[t].shape[0] // 2
            return ref.at[k, pl.ds(cc * rh, rh), :]

        local = [pltpu.make_async_copy(w[t], o[t].at[me], loc.at[t]) for t in range(n)]
        local.append(pltpu.make_async_copy(sm, osm.at[me], loc.at[n]))
        for cp in local:
            cp.start()
        first = []
        for j, chip in enumerate(chips):
            for t in range(n):
                rh = shards[t].shape[0] // 2
                first.append(_rcopy(w[t].at[pl.ds(c * rh, rh), :], half(o[t], me, c, t),
                                    send.at[j * n + t], recv.at[j * n + t], (*chip, c)))
            first.append(_rcopy(sm, osm.at[me], ssend.at[j], srecv.at[j], (*chip, c)))
        for cp in first:
            cp.start()
        passed = []
        for j, (cx, cy) in enumerate(chips):
            k = 2 * cx + cy
            for t in range(n):
                blk = half(o[t], k, c, t)
                _rcopy(blk, blk, send.at[j * n + t], recv.at[j * n + t], (cx, cy, c)).wait_recv()
                cp = _rcopy(blk, blk, fsend.at[j * n + t], frecv.at[j * n + t], sib)
                cp.start()
                passed.append(cp)
        for j, (cx, cy) in enumerate(chips):
            k = 2 * cx + cy
            for t in range(n):
                blk = half(o[t], k, 1 - c, t)
                _rcopy(blk, blk, fsend.at[j * n + t], frecv.at[j * n + t], sib).wait_recv()
            _rcopy(sm, osm.at[k], ssend.at[j], srecv.at[j], (cx, cy, c)).wait_recv()
        for cp in first + passed:
            cp.wait_send()
        for cp in local:
            cp.wait()

    out_shape = [jax.ShapeDtypeStruct((N_CHIPS,) + s.shape, s.dtype) for s in shards]
    out_shape.append(jax.ShapeDtypeStruct((N_CHIPS,) + small.shape, small.dtype))
    sems = pltpu.SemaphoreType.DMA
    return pl.pallas_call(
        body, name="all_gather_weights", out_shape=out_shape,
        in_specs=[ANY] * (n + 1), out_specs=[ANY] * (n + 1),
        scratch_shapes=[sems((3 * n,)), sems((3 * n,)), sems((3 * n,)), sems((3 * n,)),
                        sems((3,)), sems((3,)), sems((n + 1,))],
    )(*shards, small)


def sibling_exchange(grads):
    n = len(grads)

    def body(*refs):
        g = refs[:n]
        o = refs[n:2 * n]
        send, recv = refs[2 * n:]
        x, y, c, _ = _place()
        cps = [_rcopy(g[t].at[1 - c], o[t], send.at[t], recv.at[t], (x, y, 1 - c)) for t in range(n)]
        for cp in cps:
            cp.start()
        for cp in cps:
            cp.wait()

    sems = pltpu.SemaphoreType.DMA
    return pl.pallas_call(
        body, name="sibling_exchange", out_shape=[jax.ShapeDtypeStruct(a.shape[1:], a.dtype) for a in grads],
        in_specs=[ANY] * n, out_specs=[ANY] * n, scratch_shapes=[sems((n,)), sems((n,))],
    )(*grads)


def chip_exchange(parts):
    n = len(parts)

    def body(*refs):
        p = refs[:n]
        o = refs[n:2 * n]
        send, recv = refs[2 * n:]
        _, _, c, chips = _place()
        cps = []
        for j, (cx, cy) in enumerate(chips):
            for t in range(n):
                cps.append(_rcopy(p[t].at[2 * cx + cy], o[t].at[j], send.at[j * n + t], recv.at[j * n + t],
                                  (cx, cy, c)))
        for cp in cps:
            cp.start()
        for cp in cps:
            cp.wait()

    sems = pltpu.SemaphoreType.DMA
    return pl.pallas_call(
        body, name="chip_exchange",
        out_shape=[jax.ShapeDtypeStruct((3,) + a.shape[1:], a.dtype) for a in parts],
        in_specs=[ANY] * n, out_specs=[ANY] * n, scratch_shapes=[sems((3 * n,)), sems((3 * n,))],
    )(*parts)


def sibling_assemble(halves, split_rows):
    n = len(halves)

    def body(*refs):
        h = refs[:n]
        o = refs[n:2 * n]
        send, recv, loc = refs[2 * n:]
        x, y, c, _ = _place()

        def dst(t, cc):
            r, w = halves[t].shape
            return o[t].at[pl.ds(cc * r, r), :] if split_rows[t] else o[t].at[:, pl.ds(cc * w, w)]

        local = [pltpu.make_async_copy(h[t], dst(t, c), loc.at[t]) for t in range(n)]
        cps = [_rcopy(h[t], dst(t, c), send.at[t], recv.at[t], (x, y, 1 - c)) for t in range(n)]
        for cp in local + cps:
            cp.start()
        for t in range(n):
            cps[t].wait_send()
            _rcopy(h[t], dst(t, 1 - c), send.at[t], recv.at[t], (x, y, 1 - c)).wait_recv()
        for cp in local:
            cp.wait()

    def full(t):
        r, w = halves[t].shape
        return jax.ShapeDtypeStruct((2 * r, w) if split_rows[t] else (r, 2 * w), halves[t].dtype)

    sems = pltpu.SemaphoreType.DMA
    return pl.pallas_call(
        body, name="sibling_assemble", out_shape=[full(t) for t in range(n)],
        in_specs=[ANY] * n, out_specs=[ANY] * n, scratch_shapes=[sems((n,)), sems((n,)), sems((n,))],
    )(*halves)


def small_all_reduce(ddw, v512, v1024):
    rows, width = PACK_ROWS, 512
    n512, n1024 = len(VEC512), len(VEC1024)

    def body(*refs):
        ddw_ref = refs[0]
        a_refs = refs[1:1 + n512]
        b_refs = refs[1 + n512:1 + n512 + n1024]
        o_ref, p_ref, gath_ref, send, recv = refs[1 + n512 + n1024:]
        p_ref[...] = jnp.zeros_like(p_ref)
        p_ref[0:32, :] = ddw_ref[...]
        for i, r in enumerate(a_refs):
            p_ref[32 + i:33 + i, :] = r[...]
        for i, r in enumerate(b_refs):
            base = 32 + n512 + 2 * i
            p_ref[base:base + 1, :] = r[:, 0:512]
            p_ref[base + 1:base + 2, :] = r[:, 512:1024]
        x, y, c, _ = _place()
        me = 4 * x + 2 * y + c
        gath_ref[me] = p_ref[...]
        cps = []
        for k in range(1, 8):
            dx, dy, dc = (k >> 2) & 1, (k >> 1) & 1, k & 1
            px = 1 - x if dx else x
            py = 1 - y if dy else y
            pc = 1 - c if dc else c
            cps.append(_rcopy(p_ref, gath_ref.at[me], send.at[k - 1], recv.at[k - 1], (px, py, pc)))
        for cp in cps:
            cp.start()
        for k in range(1, 8):
            dx, dy, dc = (k >> 2) & 1, (k >> 1) & 1, k & 1
            px = 1 - x if dx else x
            py = 1 - y if dy else y
            pc = 1 - c if dc else c
            _rcopy(p_ref, gath_ref.at[4 * px + 2 * py + pc], send.at[k - 1], recv.at[k - 1], (px, py, pc)).wait_recv()
        for cp in cps:
            cp.wait_send()
        total = gath_ref[0]
        for d in range(1, 8):
            total = total + gath_ref[d]
        o_ref[...] = total

    sems = pltpu.SemaphoreType.DMA
    n_in = 1 + n512 + n1024
    return pl.pallas_call(
        body, name="small_all_reduce", out_shape=jax.ShapeDtypeStruct((rows, width), F32),
        in_specs=[VMEM_SPEC] * n_in, out_specs=VMEM_SPEC,
        scratch_shapes=[pltpu.VMEM((rows, width), F32), pltpu.VMEM((8, rows, width), F32), sems((7,)), sems((7,))],
    )(ddw, *[v512[n] for n in VEC512], *[v1024[n] for n in VEC1024])


def _row_block(r):
    for tr in (512, 352, 256, 128):
        if r % tr == 0:
            return tr
    return r


def add_halves(g, recv, name):
    _, _, r, w = g.shape
    tr = _row_block(r)

    def body(g0_ref, g1_ref, r_ref, ob_ref, own_ref):
        k = pl.program_id(1)
        c = lax.axis_index("c")
        me = 2 * lax.axis_index("x") + lax.axis_index("y")
        t = jnp.where(c == 0, g0_ref[0, 0], g1_ref[0, 0]) + r_ref[0]
        ob_ref[0] = t.astype(MM)
        mine = jnp.where(k == me, t, 0.0)

        @pl.when(k == 0)
        def _():
            own_ref[...] = mine

        @pl.when(k != 0)
        def _():
            own_ref[...] += mine

    return pl.pallas_call(
        body, name=name, grid=(r // tr, N_CHIPS),
        in_specs=[pl.BlockSpec((1, 1, tr, w), lambda i, k: (0, k, i, 0)),
                  pl.BlockSpec((1, 1, tr, w), lambda i, k: (1, k, i, 0)),
                  pl.BlockSpec((1, tr, w), lambda i, k: (k, i, 0))],
        out_specs=[pl.BlockSpec((1, tr, w), lambda i, k: (k, i, 0)),
                   pl.BlockSpec((tr, w), lambda i, k: (i, 0))],
        out_shape=(jax.ShapeDtypeStruct((N_CHIPS, r, w), MM), jax.ShapeDtypeStruct((r, w), F32)),
        compiler_params=_params(("arbitrary", "arbitrary")),
    )(g, g, recv)


def sum_parts(own, rin, name):
    _, r, w = rin.shape
    tr = _row_block(r)

    def body(o_ref, r_ref, out_ref):
        out_ref[...] = ((o_ref[...] + r_ref[0].astype(F32)) + r_ref[1].astype(F32)) + r_ref[2].astype(F32)

    return pl.pallas_call(
        body, name=name, grid=(r // tr,), out_shape=jax.ShapeDtypeStruct((r, w), F32),
        in_specs=[pl.BlockSpec((tr, w), lambda i: (i, 0)), pl.BlockSpec((3, tr, w), lambda i: (0, i, 0))],
        out_specs=pl.BlockSpec((tr, w), lambda i: (i, 0)),
        compiler_params=_params(("arbitrary",)),
    )(own, rin)


def _adamw_math(w, g, m, v):
    mn = ADAM_B1 * m + (1.0 - ADAM_B1) * g
    vn = ADAM_B2 * v + (1.0 - ADAM_B2) * (g * g)
    m_hat = mn / (1.0 - ADAM_B1 ** ADAM_STEP)
    v_hat = vn / (1.0 - ADAM_B2 ** ADAM_STEP)
    return -ADAM_LR * (m_hat / (jnp.sqrt(v_hat) + ADAM_EPS) + ADAM_WD * w), mn, vn


def adamw(w, g, m, v, name):
    r, c = w.shape
    tr = _row_block(r)
    if c >= 1024 and tr % 512 == 0:
        tr = 256

    def body(w_ref, g_ref, m_ref, v_ref, go_ref, d_ref, mo_ref, vo_ref):
        gv = g_ref[...]
        go_ref[...] = gv
        d_ref[...], mo_ref[...], vo_ref[...] = _adamw_math(w_ref[...], gv, m_ref[...], v_ref[...])

    spec = pl.BlockSpec((tr, c), lambda i: (i, 0))
    out = jax.ShapeDtypeStruct((r, c), F32)
    return pl.pallas_call(
        body, name=name, grid=(r // tr,), out_shape=(out, out, out, out),
        in_specs=[spec] * 4, out_specs=[spec] * 4, compiler_params=_params(("arbitrary",)),
    )(w, g, m, v)


def adamw_small(gsum, params):
    names = list(params)
    flat = [a for n in names for a in params[n]]

    def body(*refs):
        g_ref = refs[0]
        ins = refs[1:1 + 3 * len(names)]
        outs = refs[1 + 3 * len(names):]
        me = 2 * lax.axis_index("x") + lax.axis_index("y")
        for i, n in enumerate(names):
            w_ref, m_ref, v_ref = ins[3 * i:3 * i + 3]
            go_ref, d_ref, mo_ref, vo_ref = outs[4 * i:4 * i + 4]
            if n == "conv_dw_w":
                gv = jnp.zeros((CONV_WIDTH, 128), F32)
                for k in range(N_CHIPS):
                    gv = gv + jnp.where(me == k, g_ref[0:CONV_WIDTH, 128 * k:128 * (k + 1)], 0.0)
            elif n in VEC512:
                r0 = 32 + VEC512.index(n)
                gv = g_ref[r0:r0 + 1, :]
            else:
                r0 = 32 + len(VEC512) + 2 * VEC1024.index(n)
                gv = jnp.concatenate([g_ref[r0:r0 + 1, :], g_ref[r0 + 1:r0 + 2, :]], axis=1)
            go_ref[...] = gv
            d_ref[...], mo_ref[...], vo_ref[...] = _adamw_math(w_ref[...], gv, m_ref[...], v_ref[...])

    out_shape = [jax.ShapeDtypeStruct(params[n][0].shape, F32) for n in names for _ in range(4)]
    res = pl.pallas_call(
        body, name="adamw_small", out_shape=out_shape,
        in_specs=[VMEM_SPEC] * (1 + len(flat)), out_specs=[VMEM_SPEC] * len(out_shape),
        compiler_params=_params(),
    )(gsum, *flat)
    return {n: res[4 * i:4 * i + 4] for i, n in enumerate(names)}


BIG = ("w_in", "w_ffn_up", "w_ffn_down", "w_out", "w_conv_branch", "w_att_branch")
SPLIT_ROWS = dict(w_in=True, w_ffn_up=True, w_ffn_down=False, w_out=False, w_conv_branch=True, w_att_branch=True)
VEC512 = ("conv_dw_b", "conv_ln_g", "conv_ln_b")
VEC1024 = ("norm_mix_pre", "b_conv_branch", "norm_mix_post", "norm_ffn_pre", "norm_ffn_post")
PACK_ROWS = 48


def local_step(xs, tgt, weights, wg, w_dw_full):
    row = lambda a: a.reshape(1, -1)
    g1, g2, g3, g4 = (row(weights[n]) for n in ("norm_mix_pre", "norm_mix_post", "norm_ffn_pre", "norm_ffn_post"))
    ln_g, ln_b = row(weights["conv_ln_g"]), row(weights["conv_ln_b"])
    w_out_g = wg["w_out"].reshape(D_MODEL, D_MODEL)
    w_down_g = wg["w_ffn_down"].reshape(D_FF, D_MODEL)
    w_cb_g, w_ab_g = wg["w_conv_branch"], wg["w_att_branch"]

    h1, ci, q, k, v, gc, ga = in_proj_fwd(xs, g1, wg["w_in"])
    u1, u3 = conv_fwd(ci, w_dw_full, row(weights["conv_dw_b"]), ln_g, ln_b)
    att, rc = attn_fwd(q, k, v)
    co, ao, merged, mix, x2, h2 = mix_fwd(u3, att, gc, ga, xs, w_cb_g, row(weights["b_conv_branch"]),
                                          w_ab_g, w_out_g, g2, g3)
    gate, up, act = ffn_up_fwd(h2, wg["w_ffn_up"])
    dff, dy, loss_parts, dg4 = ffn_down_loss(act, w_down_g, x2, tgt, g4)
    loss_local = jnp.sum(loss_parts[::8, 0])

    dgate, dup = ffn_act_bwd(dff, w_down_g, gate, up)
    dx2, dmix, dg3, dg2 = ffn_in_bwd(dgate, dup, wg["w_ffn_up"], x2, mix, dy, g3, g2)
    dco, dao, dgc, dga, du3, datt, dbcb = merge_bwd(dmix, w_out_g, gc, ga, co, ao, w_cb_g, w_ab_g)
    dci, ddw, dbdw, dlng, dlnb = conv_bwd(du3, u1, ci, w_dw_full, ln_g, ln_b)
    dq, dk, dv = attn_bwd(q, k, v, datt, rc)
    dproj = jnp.concatenate([dci, dq, dk, dv, dgc, dga], axis=1)
    grad_x, dg1 = in_proj_bwd(dproj, wg["w_in"], xs, dx2, g1)

    partial = dict(
        w_in=weight_grad(h1, dproj, "dw_in", True),
        w_ffn_up=weight_grad(h2, jnp.concatenate([dgate, dup], axis=1), "dw_ffn_up", True),
        w_ffn_down=weight_grad(act, dff, "dw_ffn_down", False, tk=UP_SHARD),
        w_out=weight_grad(merged, dmix, "dw_out", False, tk=512),
        w_conv_branch=weight_grad(u3, dco, "dw_conv_branch", True),
        w_att_branch=weight_grad(att, dao, "dw_att_branch", True),
    )
    v512 = dict(conv_dw_b=dbdw, conv_ln_g=dlng, conv_ln_b=dlnb)
    v1024 = dict(norm_mix_pre=dg1, b_conv_branch=dbcb, norm_mix_post=dg2, norm_ffn_pre=dg3, norm_ffn_post=dg4)
    return loss_local, grad_x, partial, ddw, v512, v1024


def kernel(x, norm_mix_pre, w_in, conv_dw_w, conv_dw_b, conv_ln_g, conv_ln_b, w_conv_branch, b_conv_branch, w_att_branch, w_out, norm_mix_post, norm_ffn_pre, w_ffn_up, w_ffn_down, norm_ffn_post, loss_target, m_norm_mix_pre, m_w_in, m_conv_dw_w, m_conv_dw_b, m_conv_ln_g, m_conv_ln_b, m_w_conv_branch, m_b_conv_branch, m_w_att_branch, m_w_out, m_norm_mix_post, m_norm_ffn_pre, m_w_ffn_up, m_w_ffn_down, m_norm_ffn_post, v_norm_mix_pre, v_w_in, v_conv_dw_w, v_conv_dw_b, v_conv_ln_g, v_conv_ln_b, v_w_conv_branch, v_b_conv_branch, v_w_att_branch, v_w_out, v_norm_mix_post, v_norm_ffn_pre, v_w_ffn_up, v_w_ffn_down, v_norm_ffn_post):
    weights = dict(norm_mix_pre=norm_mix_pre, w_in=w_in, conv_dw_w=conv_dw_w, conv_dw_b=conv_dw_b, conv_ln_g=conv_ln_g, conv_ln_b=conv_ln_b, w_conv_branch=w_conv_branch, b_conv_branch=b_conv_branch, w_att_branch=w_att_branch, w_out=w_out, norm_mix_post=norm_mix_post, norm_ffn_pre=norm_ffn_pre, w_ffn_up=w_ffn_up, w_ffn_down=w_ffn_down, norm_ffn_post=norm_ffn_post)
    mom = dict(norm_mix_pre=m_norm_mix_pre, w_in=m_w_in, conv_dw_w=m_conv_dw_w, conv_dw_b=m_conv_dw_b, conv_ln_g=m_conv_ln_g, conv_ln_b=m_conv_ln_b, w_conv_branch=m_w_conv_branch, b_conv_branch=m_b_conv_branch, w_att_branch=m_w_att_branch, w_out=m_w_out, norm_mix_post=m_norm_mix_post, norm_ffn_pre=m_norm_ffn_pre, w_ffn_up=m_w_ffn_up, w_ffn_down=m_w_ffn_down, norm_ffn_post=m_norm_ffn_post)
    var = dict(norm_mix_pre=v_norm_mix_pre, w_in=v_w_in, conv_dw_w=v_conv_dw_w, conv_dw_b=v_conv_dw_b, conv_ln_g=v_conv_ln_g, conv_ln_b=v_conv_ln_b, w_conv_branch=v_w_conv_branch, b_conv_branch=v_b_conv_branch, w_att_branch=v_w_att_branch, w_out=v_w_out, norm_mix_post=v_norm_mix_post, norm_ffn_pre=v_norm_ffn_pre, w_ffn_up=v_w_ffn_up, w_ffn_down=v_w_ffn_down, norm_ffn_post=v_norm_ffn_post)
    order = list(weights)

    gathered = all_gather_weights([weights[n].astype(MM) for n in BIG], conv_dw_w)
    wg = dict(zip(BIG, gathered[:-1]))
    w_dw_full = jnp.concatenate([gathered[-1][k] for k in range(N_CHIPS)], axis=1)

    loss_local, grad_x, partial, ddw, v512, v1024 = local_step(
        x.reshape(SEQ, D_MODEL), loss_target.reshape(SEQ, D_MODEL), weights, wg, w_dw_full)
    loss = lax.psum(loss_local, ("x", "y", "c"))

    from_sib = sibling_exchange([partial[n] for n in BIG])
    summed = [add_halves(partial[n], r, "add_" + n) for n, r in zip(BIG, from_sib)]
    from_chips = chip_exchange([s[0] for s in summed])
    halves = [sum_parts(s[1], r, "sum_" + n) for n, s, r in zip(BIG, summed, from_chips)]
    full = sibling_assemble(halves, [SPLIT_ROWS[n] for n in BIG])

    grads, deltas, new_m, new_v = {}, {}, {}, {}
    for n, g in zip(BIG, full):
        grads[n], deltas[n], new_m[n], new_v[n] = adamw(weights[n], g, mom[n], var[n], "adamw_" + n)

    gsum = small_all_reduce(ddw, v512, v1024)
    as_rows = lambda n, a: a if n == "conv_dw_w" else a.reshape(1, -1)
    small_names = ("conv_dw_w",) + VEC512 + VEC1024
    small = adamw_small(gsum, {n: tuple(as_rows(n, d[n]) for d in (weights, mom, var)) for n in small_names})
    for n in small_names:
        grads[n], deltas[n], new_m[n], new_v[n] = (a.reshape(weights[n].shape) for a in small[n])

    return (loss, grad_x.reshape(1, SEQ, D_MODEL), *[grads[n] for n in order], *[deltas[n] for n in order],
            *[new_m[n] for n in order], *[new_v[n] for n in order])
```

```python
import jax
import jax.numpy as jnp
from jax import lax
from jax.experimental import pallas as pl
from jax.experimental.pallas import tpu as pltpu

F32 = jnp.float32
MM = jnp.bfloat16

SEQ = 2048
D_MODEL = 1024
CONV_DIM = 512
ATT_DIM = 512
CONV_WIDTH = 31
D_FF = 2816
IN_COLS = 2 * CONV_DIM + 3 * ATT_DIM + 2 * D_MODEL
N_CHIPS = 4
IN_SHARD = IN_COLS // N_CHIPS
UP_SHARD = 2 * D_FF // N_CHIPS
BR_SHARD = D_MODEL // N_CHIPS
EPS = 1e-6
ATT_SCALE = 0.125

TM = 256
TQ = 128
CONV_TILE = 64
CONV_WIN = CONV_TILE + 32
VMEM_LIMIT = 56 * 1024 * 1024

ADAM_LR = 0.001
ADAM_B1 = 0.9
ADAM_B2 = 0.999
ADAM_EPS = 1e-08
ADAM_WD = 0.01
ADAM_STEP = 10

MESH = pl.DeviceIdType.MESH
ANY = pl.BlockSpec(memory_space=pl.ANY)
VMEM_SPEC = pl.BlockSpec(memory_space=pltpu.VMEM)

NT_DIMS = (((1,), (1,)), ((), ()))
TN_DIMS = (((0,), (0,)), ((), ()))

IN_PIECES = (("ci", 0, 1024), ("q", 1024, 1536), ("k", 1536, 2048), ("v", 2048, 2560),
             ("gc", 2560, 3584), ("ga", 3584, 4608))


def _params(sem=None, vmem=VMEM_LIMIT):
    return pltpu.CompilerParams(dimension_semantics=sem, vmem_limit_bytes=vmem)


def _dot(a, b):
    return jnp.dot(a, b, preferred_element_type=F32)


def _dot_nt(a, b):
    return lax.dot_general(a, b, NT_DIMS, preferred_element_type=F32)


def _dot_tn(a, b):
    return lax.dot_general(a, b, TN_DIMS, preferred_element_type=F32)


def _sigmoid(x):
    return 1.0 / (1.0 + jnp.exp(-x))


def _rms(x):
    r = lax.rsqrt(jnp.mean(x * x, axis=-1, keepdims=True) + EPS)
    return x * r, r


def _rms_bwd(dy_g, n, r):
    return r * (dy_g - n * jnp.mean(dy_g * n, axis=-1, keepdims=True))


def _row_tile_spec(width, tm=TM):
    return pl.BlockSpec((tm, width), lambda i: (i, 0))


def _full_spec(shape):
    nd = len(shape)
    return pl.BlockSpec(shape, lambda *_: (0,) * nd)


def _acc_rows(ref, val, first):
    @pl.when(first)
    def _():
        ref[...] = val

    @pl.when(jnp.logical_not(first))
    def _():
        ref[...] += val


def in_proj_fwd(x, g1, w_in_g):
    def body(x_ref, g_ref, w_ref, h_ref, ci_ref, q_ref, k_ref, v_ref, gc_ref, ga_ref):
        n, _ = _rms(x_ref[...])
        h = (n * g_ref[...]).astype(MM)
        h_ref[...] = h
        outs = dict(ci=ci_ref, q=q_ref, k=k_ref, v=v_ref, gc=gc_ref, ga=ga_ref)
        for j in range(N_CHIPS):
            p = _dot(h, w_ref[j])
            g0 = j * IN_SHARD
            for name, s, e in IN_PIECES:
                lo, hi = max(s, g0), min(e, g0 + IN_SHARD)
                if lo < hi:
                    ref = outs[name]
                    ref[:, lo - s:hi - s] = p[:, lo - g0:hi - g0].astype(ref.dtype)

    out_shape = (
        jax.ShapeDtypeStruct((SEQ, D_MODEL), MM),
        jax.ShapeDtypeStruct((SEQ, 2 * CONV_DIM), F32),
        jax.ShapeDtypeStruct((SEQ, ATT_DIM), MM),
        jax.ShapeDtypeStruct((SEQ, ATT_DIM), MM),
        jax.ShapeDtypeStruct((SEQ, ATT_DIM), MM),
        jax.ShapeDtypeStruct((SEQ, D_MODEL), F32),
        jax.ShapeDtypeStruct((SEQ, D_MODEL), F32),
    )
    return pl.pallas_call(
        body, name="in_proj_fwd", grid=(SEQ // TM,), out_shape=out_shape,
        in_specs=[_row_tile_spec(D_MODEL), _full_spec((1, D_MODEL)), _full_spec(w_in_g.shape)],
        out_specs=[_row_tile_spec(s.shape[1]) for s in out_shape],
        compiler_params=_params(("arbitrary",)),
    )(x, g1, w_in_g)


def _shifted_sum(win, terms):
    by_rot = {}
    for m, coef in terms:
        by_rot.setdefault(m % 8, []).append((m // 8, coef))
    acc = None
    n = win.shape[0]
    for rot in sorted(by_rot):
        shifted = win if rot == 0 else pltpu.roll(win, n - rot, 0)
        for a, coef in by_rot[rot]:
            t = coef * shifted[8 * a:8 * a + CONV_TILE, :]
            acc = t if acc is None else acc + t
    return acc


def _glu_into(ci_ref, upad_ref):
    upad_ref[0:32, :] = jnp.zeros((32, CONV_DIM), F32)

    def step(i, c):
        t0 = pl.multiple_of(i * TM, TM)
        a = ci_ref[pl.ds(t0, TM), 0:CONV_DIM]
        b = ci_ref[pl.ds(t0, TM), CONV_DIM:2 * CONV_DIM]
        upad_ref[pl.ds(t0 + 32, TM), :] = a * _sigmoid(b)
        return c

    lax.fori_loop(0, SEQ // TM, step, 0)


def _layernorm_parts(u1):
    mu = jnp.mean(u1, axis=-1, keepdims=True)
    xc = u1 - mu
    rstd = lax.rsqrt(jnp.mean(xc * xc, axis=-1, keepdims=True) + EPS)
    return xc * rstd, rstd


def conv_fwd(ci, w_dw, b_dw, ln_g, ln_b):
    def body(ci_ref, w_ref, b_ref, g_ref, bb_ref, u1_ref, u3_ref, upad_ref):
        _glu_into(ci_ref, upad_ref)

        def step(i, c):
            t0 = pl.multiple_of(i * CONV_TILE, CONV_TILE)
            win = upad_ref[pl.ds(t0, CONV_WIN), :]
            u1 = _shifted_sum(win, [(j + 2, w_ref[j:j + 1, :]) for j in range(CONV_WIDTH)]) + b_ref[...]
            u1_ref[pl.ds(t0, CONV_TILE), :] = u1
            xh, _ = _layernorm_parts(u1)
            u2 = xh * g_ref[...] + bb_ref[...]
            u3_ref[pl.ds(t0, CONV_TILE), :] = (u2 * _sigmoid(u2)).astype(MM)
            return c

        lax.fori_loop(0, SEQ // CONV_TILE, step, 0)

    return pl.pallas_call(
        body, name="conv_fwd",
        out_shape=(jax.ShapeDtypeStruct((SEQ, CONV_DIM), F32), jax.ShapeDtypeStruct((SEQ, CONV_DIM), MM)),
        in_specs=[VMEM_SPEC] * 5, out_specs=[VMEM_SPEC] * 2,
        scratch_shapes=[pltpu.VMEM((SEQ + 32, CONV_DIM), F32)],
        compiler_params=_params(),
    )(ci, w_dw, b_dw, ln_g, ln_b)


def _softplus(z):
    return jnp.maximum(z, 0.0) + jnp.log(1.0 + jnp.exp(-jnp.abs(z)))


def _cumsum_weights(suffix, with_total):
    n = 256 if with_total else 128
    r = lax.broadcasted_iota(jnp.int32, (256, n), 0) & 127
    c = lax.broadcasted_iota(jnp.int32, (256, n), 1)
    tri = (r >= c) if suffix else (r <= c)
    return jnp.logical_or(tri, c >= 128).astype(MM)


def _cumsum_dot(x, w):
    hi = x.astype(MM)
    lo = (x - hi.astype(F32)).astype(MM)
    r = _dot(jnp.concatenate([hi, lo], axis=1), w)
    return (r[:, :128], r[:, 128:]) if w.shape[1] == 256 else (r, None)


def _head_masks():
    lane = lax.broadcasted_iota(jnp.int32, (TQ, 128), 1)
    row = lax.broadcasted_iota(jnp.int32, (TQ, 128), 0)
    return lane, row, lane < 64


def _pick_head(x, head0, h):
    zero = jnp.zeros_like(x)
    return jnp.where(head0, x, zero) if h == 0 else jnp.where(head0, zero, x)


N_PAIRS = ATT_DIM // 128


def attn_fwd(q, k, v):
    def body(q_ref, k_ref, v_ref, o_ref, rc_ref, acc_ref, r_ref):
        i = pl.program_id(0)
        lane, row, head0 = _head_masks()
        w = _cumsum_weights(suffix=True, with_total=True)
        acc_ref[...] = jnp.zeros_like(acc_ref)
        r_ref[...] = jnp.zeros_like(r_ref)
        rc_ref[...] = jnp.zeros_like(rc_ref)

        def block(jb, diag):
            k0 = pl.multiple_of(jb * TQ, TQ)
            valid = lane < row
            for p in range(N_PAIRS):
                cols = slice(128 * p, 128 * (p + 1))
                qb = q_ref[:, cols]
                kb = k_ref[pl.ds(k0, TQ), cols]
                vb = v_ref[pl.ds(k0, TQ), cols]
                for h in range(2):
                    hh = 2 * p + h
                    z = _dot_nt(_pick_head(qb, head0, h), kb) * ATT_SCALE
                    sp = _softplus(z)
                    if diag:
                        sp = jnp.where(valid, sp, 0.0)
                    c_loc, total = _cumsum_dot(sp, w)
                    r_in = r_ref[hh]
                    a = jnp.exp(z - (c_loc + r_in))
                    if diag:
                        a = jnp.where(valid, a, 0.0)
                    acc_ref[:, cols] += _dot(a.astype(MM), _pick_head(vb, head0, h))
                    rc_ref[hh] = jnp.where(lane == jb, r_in, rc_ref[hh])
                    r_ref[hh] = r_in + total

        block(i, True)

        def kstep(n, c):
            block(i - 1 - n, False)
            return c

        lax.fori_loop(0, i, kstep, 0)
        o_ref[...] = acc_ref[...].astype(MM)

    nqb = SEQ // TQ
    full = pl.BlockSpec((SEQ, ATT_DIM), lambda i: (0, 0))
    return pl.pallas_call(
        body, name="attn_fwd", grid=(nqb,),
        out_shape=(jax.ShapeDtypeStruct((SEQ, ATT_DIM), MM), jax.ShapeDtypeStruct((8, SEQ, 128), F32)),
        in_specs=[_row_tile_spec(ATT_DIM, TQ), full, full],
        out_specs=[_row_tile_spec(ATT_DIM, TQ), pl.BlockSpec((8, TQ, 128), lambda i: (0, i, 0))],
        scratch_shapes=[pltpu.VMEM((TQ, ATT_DIM), F32), pltpu.VMEM((8, TQ, 128), F32)],
        compiler_params=_params(("arbitrary",)),
    )(q, k, v)


def mix_fwd(u3, att, gc, ga, x, w_cb_g, b_cb, w_ab_g, w_out_g, g2, g3):
    def body(u_ref, a_ref, gc_ref, ga_ref, x_ref, wcb_ref, bcb_ref, wab_ref, wout_ref, g2_ref, g3_ref,
             co_ref, ao_ref, mg_ref, mix_ref, x2_ref, h2_ref):
        u = u_ref[...]
        a = a_ref[...]
        for j in range(N_CHIPS):
            cols = slice(j * BR_SHARD, (j + 1) * BR_SHARD)
            co_ref[:, cols] = _dot(u, wcb_ref[j]) + bcb_ref[:, cols]
            ao_ref[:, cols] = _dot(a, wab_ref[j])
        merged = (_sigmoid(gc_ref[...]) * co_ref[...] + _sigmoid(ga_ref[...]) * ao_ref[...]).astype(MM)
        mg_ref[...] = merged
        mix = _dot(merged, wout_ref[...])
        mix_ref[...] = mix
        n2, _ = _rms(mix)
        x2 = x_ref[...] + n2 * g2_ref[...]
        x2_ref[...] = x2
        n3, _ = _rms(x2)
        h2_ref[...] = (n3 * g3_ref[...]).astype(MM)

    out_shape = (
        jax.ShapeDtypeStruct((SEQ, D_MODEL), F32), jax.ShapeDtypeStruct((SEQ, D_MODEL), F32),
        jax.ShapeDtypeStruct((SEQ, D_MODEL), MM), jax.ShapeDtypeStruct((SEQ, D_MODEL), F32),
        jax.ShapeDtypeStruct((SEQ, D_MODEL), F32), jax.ShapeDtypeStruct((SEQ, D_MODEL), MM),
    )
    vec = _full_spec((1, D_MODEL))
    return pl.pallas_call(
        body, name="mix_fwd", grid=(SEQ // TM,), out_shape=out_shape,
        in_specs=[_row_tile_spec(CONV_DIM), _row_tile_spec(ATT_DIM), _row_tile_spec(D_MODEL),
                  _row_tile_spec(D_MODEL), _row_tile_spec(D_MODEL), _full_spec(w_cb_g.shape), vec,
                  _full_spec(w_ab_g.shape), _full_spec(w_out_g.shape), vec, vec],
        out_specs=[_row_tile_spec(D_MODEL)] * 6,
        compiler_params=_params(("arbitrary",)),
    )(u3, att, gc, ga, x, w_cb_g, b_cb, w_ab_g, w_out_g, g2, g3)


def ffn_up_fwd(h2, w_up_g):
    def body(h_ref, wg_ref, wu_ref, gate_ref, up_ref, act_ref):
        h = h_ref[...]
        gate = _dot(h, wg_ref[0])
        up = _dot(h, wu_ref[0])
        gate_ref[...] = gate
        up_ref[...] = up
        act_ref[...] = (gate * _sigmoid(gate) * up).astype(MM)

    tile = pl.BlockSpec((TM, UP_SHARD), lambda n, i: (i, n))
    return pl.pallas_call(
        body, name="ffn_up_fwd", grid=(2, SEQ // TM),
        out_shape=(jax.ShapeDtypeStruct((SEQ, D_FF), F32), jax.ShapeDtypeStruct((SEQ, D_FF), F32),
                   jax.ShapeDtypeStruct((SEQ, D_FF), MM)),
        in_specs=[pl.BlockSpec((TM, D_MODEL), lambda n, i: (i, 0)),
                  pl.BlockSpec((1, D_MODEL, UP_SHARD), lambda n, i: (n, 0, 0)),
                  pl.BlockSpec((1, D_MODEL, UP_SHARD), lambda n, i: (n + 2, 0, 0))],
        out_specs=[tile, tile, tile],
        compiler_params=_params(("arbitrary", "arbitrary")),
    )(h2, w_up_g, w_up_g)


def ffn_down_loss(act, w_down_g, x2, target, g4):
    def body(act_ref, wd_ref, x2_ref, t_ref, g_ref, dff_ref, dy_ref, loss_ref, dg_ref):
        ff = _dot(act_ref[...], wd_ref[...])
        n4, r4 = _rms(ff)
        g4v = g_ref[...]
        err = x2_ref[...] + n4 * g4v - t_ref[...]
        row_loss = jnp.mean(err * err, axis=-1, keepdims=True)
        loss_ref[...] = jnp.zeros((8, 128), F32) + 0.5 * jnp.sum(row_loss, axis=0, keepdims=True)
        dy = err * (1.0 / D_MODEL)
        dy_ref[...] = dy
        dff_ref[...] = _rms_bwd(dy * g4v, n4, r4).astype(MM)
        _acc_rows(dg_ref, jnp.sum(dy * n4, axis=0, keepdims=True), pl.program_id(0) == 0)

    nt = SEQ // TM
    vec = _full_spec((1, D_MODEL))
    return pl.pallas_call(
        body, name="ffn_down_loss", grid=(nt,),
        out_shape=(jax.ShapeDtypeStruct((SEQ, D_MODEL), MM), jax.ShapeDtypeStruct((SEQ, D_MODEL), F32),
                   jax.ShapeDtypeStruct((nt * 8, 128), F32), jax.ShapeDtypeStruct((1, D_MODEL), F32)),
        in_specs=[_row_tile_spec(D_FF), _full_spec(w_down_g.shape), _row_tile_spec(D_MODEL),
                  _row_tile_spec(D_MODEL), vec],
        out_specs=[_row_tile_spec(D_MODEL), _row_tile_spec(D_MODEL),
                   pl.BlockSpec((8, 128), lambda i: (i, 0)), vec],
        compiler_params=_params(("arbitrary",)),
    )(act, w_down_g, x2, target, g4)


def ffn_act_bwd(dff, w_down_g, gate, up):
    def body(dff_ref, wd_ref, gate_ref, up_ref, dgate_ref, dup_ref):
        dact = _dot_nt(dff_ref[...], wd_ref[...])
        gate = gate_ref[...]
        sg = _sigmoid(gate)
        dgate_ref[...] = (dact * up_ref[...] * (sg * (1.0 + gate * (1.0 - sg)))).astype(MM)
        dup_ref[...] = (dact * (gate * sg)).astype(MM)

    return pl.pallas_call(
        body, name="ffn_act_bwd", grid=(SEQ // TM,),
        out_shape=(jax.ShapeDtypeStruct((SEQ, D_FF), MM), jax.ShapeDtypeStruct((SEQ, D_FF), MM)),
        in_specs=[_row_tile_spec(D_MODEL), _full_spec(w_down_g.shape), _row_tile_spec(D_FF), _row_tile_spec(D_FF)],
        out_specs=[_row_tile_spec(D_FF), _row_tile_spec(D_FF)],
        compiler_params=_params(("arbitrary",)),
    )(dff, w_down_g, gate, up)


def ffn_in_bwd(dgate, dup, w_up_g, x2, mix, dy, g3, g2):
    def body(dg_ref, du_ref, w_ref, x2_ref, mix_ref, dy_ref, g3_ref, g2_ref, dx2_ref, dmix_ref, dg3_ref, dg2_ref):
        dh2 = None
        for j in range(2):
            cols = slice(j * UP_SHARD, (j + 1) * UP_SHARD)
            t = _dot_nt(dg_ref[:, cols], w_ref[j]) + _dot_nt(du_ref[:, cols], w_ref[j + 2])
            dh2 = t if dh2 is None else dh2 + t
        first = pl.program_id(0) == 0
        n3, r3 = _rms(x2_ref[...])
        dx2 = dy_ref[...] + _rms_bwd(dh2 * g3_ref[...], n3, r3)
        dx2_ref[...] = dx2
        _acc_rows(dg3_ref, jnp.sum(dh2 * n3, axis=0, keepdims=True), first)
        n2, r2 = _rms(mix_ref[...])
        dmix_ref[...] = _rms_bwd(dx2 * g2_ref[...], n2, r2).astype(MM)
        _acc_rows(dg2_ref, jnp.sum(dx2 * n2, axis=0, keepdims=True), first)

    vec = _full_spec((1, D_MODEL))
    return pl.pallas_call(
        body, name="ffn_in_bwd", grid=(SEQ // TM,),
        out_shape=(jax.ShapeDtypeStruct((SEQ, D_MODEL), F32), jax.ShapeDtypeStruct((SEQ, D_MODEL), MM),
                   jax.ShapeDtypeStruct((1, D_MODEL), F32), jax.ShapeDtypeStruct((1, D_MODEL), F32)),
        in_specs=[_row_tile_spec(D_FF), _row_tile_spec(D_FF), _full_spec(w_up_g.shape), _row_tile_spec(D_MODEL),
                  _row_tile_spec(D_MODEL), _row_tile_spec(D_MODEL), vec, vec],
        out_specs=[_row_tile_spec(D_MODEL), _row_tile_spec(D_MODEL), vec, vec],
        compiler_params=_params(("arbitrary",)),
    )(dgate, dup, w_up_g, x2, mix, dy, g3, g2)


def merge_bwd(dmix, w_out_g, gc, ga, co, ao, w_cb_g, w_ab_g):
    def body(dmix_ref, wout_ref, gc_ref, ga_ref, co_ref, ao_ref, wcb_ref, wab_ref,
             dco_ref, dao_ref, dgc_ref, dga_ref, du3_ref, datt_ref, dbcb_ref):
        dm = _dot_nt(dmix_ref[...], wout_ref[...])
        sgc = _sigmoid(gc_ref[...])
        sga = _sigmoid(ga_ref[...])
        dco = dm * sgc
        dao = dm * sga
        dgc_ref[...] = (dm * co_ref[...] * (sgc * (1.0 - sgc))).astype(MM)
        dga_ref[...] = (dm * ao_ref[...] * (sga * (1.0 - sga))).astype(MM)
        _acc_rows(dbcb_ref, jnp.sum(dco, axis=0, keepdims=True), pl.program_id(0) == 0)
        dco_ref[...] = dco.astype(MM)
        dao_ref[...] = dao.astype(MM)
        du3 = None
        datt = None
        for j in range(N_CHIPS):
            cols = slice(j * BR_SHARD, (j + 1) * BR_SHARD)
            t = _dot_nt(dco_ref[:, cols], wcb_ref[j])
            s = _dot_nt(dao_ref[:, cols], wab_ref[j])
            du3 = t if du3 is None else du3 + t
            datt = s if datt is None else datt + s
        du3_ref[...] = du3
        datt_ref[...] = datt.astype(MM)

    wide = _row_tile_spec(D_MODEL)
    return pl.pallas_call(
        body, name="merge_bwd", grid=(SEQ // TM,),
        out_shape=(jax.ShapeDtypeStruct((SEQ, D_MODEL), MM), jax.ShapeDtypeStruct((SEQ, D_MODEL), MM),
                   jax.ShapeDtypeStruct((SEQ, D_MODEL), MM), jax.ShapeDtypeStruct((SEQ, D_MODEL), MM),
                   jax.ShapeDtypeStruct((SEQ, CONV_DIM), F32), jax.ShapeDtypeStruct((SEQ, ATT_DIM), MM),
                   jax.ShapeDtypeStruct((1, D_MODEL), F32)),
        in_specs=[wide, _full_spec(w_out_g.shape), wide, wide, wide, wide,
                  _full_spec(w_cb_g.shape), _full_spec(w_ab_g.shape)],
        out_specs=[wide, wide, wide, wide, _row_tile_spec(CONV_DIM), _row_tile_spec(ATT_DIM),
                   _full_spec((1, D_MODEL))],
        compiler_params=_params(("arbitrary",)),
    )(dmix, w_out_g, gc, ga, co, ao, w_cb_g, w_ab_g)


def conv_bwd(du3, u1, ci, w_dw, ln_g, ln_b):
    def body(du3_ref, u1_ref, ci_ref, w_ref, g_ref, bb_ref,
             dci_ref, dw_ref, dbdw_ref, dg_ref, db_ref, upad_ref, dpad_ref, dwacc_ref, vacc_ref):
        _glu_into(ci_ref, upad_ref)
        dpad_ref[SEQ:SEQ + 32, :] = jnp.zeros((32, CONV_DIM), F32)
        dwacc_ref[...] = jnp.zeros_like(dwacc_ref)
        vacc_ref[...] = jnp.zeros_like(vacc_ref)

        def fold8(t):
            s = t[0:8, :]
            for r in range(1, CONV_TILE // 8):
                s = s + t[8 * r:8 * r + 8, :]
            return s

        def pass1(i, c):
            t0 = pl.multiple_of(i * CONV_TILE, CONV_TILE)
            xh, rstd = _layernorm_parts(u1_ref[pl.ds(t0, CONV_TILE), :])
            gv = g_ref[...]
            u2 = xh * gv + bb_ref[...]
            s2 = _sigmoid(u2)
            du2 = du3_ref[pl.ds(t0, CONV_TILE), :] * (s2 * (1.0 + u2 * (1.0 - s2)))
            wv = du2 * gv
            du1 = rstd * (wv - jnp.mean(wv, axis=-1, keepdims=True)
                          - xh * jnp.mean(wv * xh, axis=-1, keepdims=True))
            dpad_ref[pl.ds(t0, CONV_TILE), :] = du1
            vacc_ref[0] += fold8(du2 * xh)
            vacc_ref[1] += fold8(du2)
            vacc_ref[2] += fold8(du1)
            win = upad_ref[pl.ds(t0, CONV_WIN), :]
            n = win.shape[0]
            for rot in range(8):
                shifted = win if rot == 0 else pltpu.roll(win, n - rot, 0)
                for a in range(5):
                    j = 8 * a + rot - 2
                    if 0 <= j < CONV_WIDTH:
                        dwacc_ref[j] += fold8(du1 * shifted[8 * a:8 * a + CONV_TILE, :])
            return c

        lax.fori_loop(0, SEQ // CONV_TILE, pass1, 0)

        def pass2(i, c):
            t0 = pl.multiple_of(i * CONV_TILE, CONV_TILE)
            win = dpad_ref[pl.ds(t0, CONV_WIN), :]
            du0 = _shifted_sum(win, [(30 - j, w_ref[j:j + 1, :]) for j in range(CONV_WIDTH)])
            a = ci_ref[pl.ds(t0, CONV_TILE), 0:CONV_DIM]
            sb = _sigmoid(ci_ref[pl.ds(t0, CONV_TILE), CONV_DIM:2 * CONV_DIM])
            dci_ref[pl.ds(t0, CONV_TILE), 0:CONV_DIM] = (du0 * sb).astype(MM)
            dci_ref[pl.ds(t0, CONV_TILE), CONV_DIM:2 * CONV_DIM] = (du0 * a * (sb * (1.0 - sb))).astype(MM)
            return c

        lax.fori_loop(0, SEQ // CONV_TILE, pass2, 0)

        for j in range(CONV_WIDTH):
            dw_ref[j:j + 1, :] = jnp.sum(dwacc_ref[j], axis=0, keepdims=True)
        dw_ref[CONV_WIDTH:32, :] = jnp.zeros((32 - CONV_WIDTH, CONV_DIM), F32)
        dg_ref[...] = jnp.sum(vacc_ref[0], axis=0, keepdims=True)
        db_ref[...] = jnp.sum(vacc_ref[1], axis=0, keepdims=True)
        dbdw_ref[...] = jnp.sum(vacc_ref[2], axis=0, keepdims=True)

    vec = jax.ShapeDtypeStruct((1, CONV_DIM), F32)
    return pl.pallas_call(
        body, name="conv_bwd",
        out_shape=(jax.ShapeDtypeStruct((SEQ, 2 * CONV_DIM), MM), jax.ShapeDtypeStruct((32, CONV_DIM), F32),
                   vec, vec, vec),
        in_specs=[VMEM_SPEC] * 6, out_specs=[VMEM_SPEC] * 5,
        scratch_shapes=[pltpu.VMEM((SEQ + 32, CONV_DIM), F32), pltpu.VMEM((SEQ + 32, CONV_DIM), F32),
                        pltpu.VMEM((CONV_WIDTH, 8, CONV_DIM), F32), pltpu.VMEM((3, 8, CONV_DIM), F32)],
        compiler_params=_params(),
    )(du3, u1, ci, w_dw, ln_g, ln_b)


def attn_bwd(q, k, v, datt, rc):
    nqb = SEQ // TQ

    def body(q_ref, k_ref, v_ref, do_ref, rc_ref, dq_ref, dk_ref, dv_ref, dqa_ref, dka_ref, dva_ref, pc_ref):
        i = pl.program_id(0)
        lane, row, head0 = _head_masks()

        @pl.when(i == 0)
        def _():
            dka_ref[...] = jnp.zeros_like(dka_ref)
            dva_ref[...] = jnp.zeros_like(dva_ref)

        dqa_ref[...] = jnp.zeros_like(dqa_ref)
        pc_ref[...] = jnp.zeros_like(pc_ref)
        w_suffix = _cumsum_weights(suffix=True, with_total=False)
        w_prefix = _cumsum_weights(suffix=False, with_total=True)

        def block(jb, diag):
            k0 = pl.multiple_of(jb * TQ, TQ)
            valid = lane < row
            for p in range(N_PAIRS):
                cols = slice(128 * p, 128 * (p + 1))
                qb = q_ref[:, cols]
                dob = do_ref[:, cols]
                kb = k_ref[pl.ds(k0, TQ), cols]
                vb = v_ref[pl.ds(k0, TQ), cols]
                for h in range(2):
                    hh = 2 * p + h
                    qm = _pick_head(qb, head0, h)
                    dom = _pick_head(dob, head0, h)
                    z = _dot_nt(qm, kb) * ATT_SCALE
                    sp_all = _softplus(z)
                    sig = jnp.exp(z - sp_all)
                    sp = jnp.where(valid, sp_all, 0.0) if diag else sp_all
                    r_in = jnp.sum(jnp.where(lane == jb, rc_ref[hh], 0.0), axis=1, keepdims=True)
                    c_loc, _ = _cumsum_dot(sp, w_suffix)
                    a = jnp.exp(z - (c_loc + r_in))
                    if diag:
                        a = jnp.where(valid, a, 0.0)
                    g = _dot_nt(dom, vb) * a
                    dva_ref[pl.ds(k0, TQ), cols] += _dot_tn(a.astype(MM), dom)
                    p_loc, g_total = _cumsum_dot(g, w_prefix)
                    p_in = pc_ref[hh]
                    dz = (g - sig * (p_loc + p_in)) * ATT_SCALE
                    if diag:
                        dz = jnp.where(valid, dz, 0.0)
                    dz = dz.astype(MM)
                    dqa_ref[:, cols] += _dot(dz, _pick_head(kb, head0, h))
                    dka_ref[pl.ds(k0, TQ), cols] += _dot_tn(dz, qm)
                    pc_ref[hh] = p_in + g_total

        def kstep(jb, c):
            block(jb, False)
            return c

        lax.fori_loop(0, i, kstep, 0)
        block(i, True)
        dq_ref[...] = dqa_ref[...].astype(MM)

        @pl.when(i == nqb - 1)
        def _():
            dk_ref[...] = dka_ref[...].astype(MM)
            dv_ref[...] = dva_ref[...].astype(MM)

    tile = _row_tile_spec(ATT_DIM, TQ)
    full = pl.BlockSpec((SEQ, ATT_DIM), lambda i: (0, 0))
    out = jax.ShapeDtypeStruct((SEQ, ATT_DIM), MM)
    return pl.pallas_call(
        body, name="attn_bwd", grid=(nqb,), out_shape=(out, out, out),
        in_specs=[tile, full, full, tile, pl.BlockSpec((8, TQ, 128), lambda i: (0, i, 0))],
        out_specs=[tile, full, full],
        scratch_shapes=[pltpu.VMEM((TQ, ATT_DIM), F32), pltpu.VMEM((SEQ, ATT_DIM), F32),
                        pltpu.VMEM((SEQ, ATT_DIM), F32), pltpu.VMEM((8, TQ, 128), F32)],
        compiler_params=_params(("arbitrary",)),
    )(q, k, v, datt, rc)


def in_proj_bwd(dproj, w_in_g, x, dx2, g1):
    def body(dp_ref, w_ref, x_ref, dx2_ref, g_ref, dx_ref, dg_ref):
        dh = None
        for j in range(N_CHIPS):
            t = _dot_nt(dp_ref[:, j * IN_SHARD:(j + 1) * IN_SHARD], w_ref[j])
            dh = t if dh is None else dh + t
        n1, r1 = _rms(x_ref[...])
        dx_ref[...] = dx2_ref[...] + _rms_bwd(dh * g_ref[...], n1, r1)
        _acc_rows(dg_ref, jnp.sum(dh * n1, axis=0, keepdims=True), pl.program_id(0) == 0)

    vec = _full_spec((1, D_MODEL))
    return pl.pallas_call(
        body, name="in_proj_bwd", grid=(SEQ // TM,),
        out_shape=(jax.ShapeDtypeStruct((SEQ, D_MODEL), F32), jax.ShapeDtypeStruct((1, D_MODEL), F32)),
        in_specs=[_row_tile_spec(IN_COLS), _full_spec(w_in_g.shape), _row_tile_spec(D_MODEL),
                  _row_tile_spec(D_MODEL), vec],
        out_specs=[_row_tile_spec(D_MODEL), vec],
        compiler_params=_params(("arbitrary",)),
    )(dproj, w_in_g, x, dx2, g1)


def weight_grad(a, b, name, col_sharded, tk=None):
    kin, n = a.shape[1], b.shape[1]

    def body(a_ref, b_ref, o_ref):
        if col_sharded:
            o_ref[0, 0] = _dot_tn(a_ref[...], b_ref[...])
        else:
            o_ref[0] = _dot_tn(a_ref[...], b_ref[...])

    if col_sharded:
        kh, ns = kin // 2, n // N_CHIPS
        out = jax.ShapeDtypeStruct((2, N_CHIPS, kh, ns), F32)
        grid = (2, N_CHIPS)
        in_specs = [pl.BlockSpec((SEQ, kh), lambda h, j: (0, h)), pl.BlockSpec((SEQ, ns), lambda h, j: (0, j))]
        out_spec = pl.BlockSpec((1, 1, kh, ns), lambda h, j: (h, j, 0, 0))
    else:
        nh = n // 2
        out = jax.ShapeDtypeStruct((2, kin, nh), F32)
        grid = (2, kin // tk)
        in_specs = [pl.BlockSpec((SEQ, tk), lambda h, r: (0, r)), pl.BlockSpec((SEQ, nh), lambda h, r: (0, h))]
        out_spec = pl.BlockSpec((1, tk, nh), lambda h, r: (h, r, 0))
    res = pl.pallas_call(
        body, name=name, grid=grid, out_shape=out, in_specs=in_specs, out_specs=out_spec,
        compiler_params=_params(("arbitrary", "arbitrary")),
    )(a, b)
    if not col_sharded:
        res = res.reshape(2, N_CHIPS, kin // N_CHIPS, nh)
    return res


def _place():
    x, y, c = lax.axis_index("x"), lax.axis_index("y"), lax.axis_index("c")
    chips = [(1 - x, y), (x, 1 - y), (1 - x, 1 - y)]
    return x, y, c, chips


def _rcopy(src, dst, send_sem, recv_sem, dev):
    return pltpu.make_async_remote_copy(src_ref=src, dst_ref=dst, send_sem=send_sem, recv_sem=recv_sem,
                                        device_id=dev, device_id_type=MESH)


def all_gather_weights(shards, small):
    n = len(shards)

    def body(*refs):
        w = refs[:n]
        sm = refs[n]
        o = refs[n + 1:2 * n + 1]
        osm = refs[2 * n + 1]
        send, recv, fsend, frecv, ssend, srecv, loc = refs[2 * n + 2:]
        x, y, c, chips = _place()
        me = 2 * x + y
        sib = (x, y, 1 - c)

        def half(ref, k, cc, t):
            rh = shards[t].shape[0] // 2
            return ref.at[k, pl.ds(cc * rh, rh), :]

        local = [pltpu.make_async_copy(w[t], o[t].at[me], loc.at[t]) for t in range(n)]
        local.append(pltpu.make_async_copy(sm, osm.at[me], loc.at[n]))
        for cp in local:
            cp.start()
        first = []
        for j, chip in enumerate(chips):
            for t in range(n):
                rh = shards[t].shape[0] // 2
                first.append(_rcopy(w[t].at[pl.ds(c * rh, rh), :], half(o[t], me, c, t),
                                    send.at[j * n + t], recv.at[j * n + t], (*chip, c)))
            first.append(_rcopy(sm, osm.at[me], ssend.at[j], srecv.at[j], (*chip, c)))
        for cp in first:
            cp.start()
        passed = []
        for j, (cx, cy) in enumerate(chips):
            k = 2 * cx + cy
            for t in range(n):
                blk = half(o[t], k, c, t)
                _rcopy(blk, blk, send.at[j * n + t], recv.at[j * n + t], (cx, cy, c)).wait_recv()
                cp = _rcopy(blk, blk, fsend.at[j * n + t], frecv.at[j * n + t], sib)
                cp.start()
                passed.append(cp)
        for j, (cx, cy) in enumerate(chips):
            k = 2 * cx + cy
            for t in range(n):
                blk = half(o[t], k, 1 - c, t)
                _rcopy(blk, blk, fsend.at[j * n + t], frecv.at[j * n + t], sib).wait_recv()
            _rcopy(sm, osm.at[k], ssend.at[j], srecv.at[j], (cx, cy, c)).wait_recv()
        for cp in first + passed:
            cp.wait_send()
        for cp in local:
            cp.wait()

    out_shape = [jax.ShapeDtypeStruct((N_CHIPS,) + s.shape, s.dtype) for s in shards]
    out_shape.append(jax.ShapeDtypeStruct((N_CHIPS,) + small.shape, small.dtype))
    sems = pltpu.SemaphoreType.DMA
    return pl.pallas_call(
        body, name="all_gather_weights", out_shape=out_shape,
        in_specs=[ANY] * (n + 1), out_specs=[ANY] * (n + 1),
        scratch_shapes=[sems((3 * n,)), sems((3 * n,)), sems((3 * n,)), sems((3 * n,)),
                        sems((3,)), sems((3,)), sems((n + 1,))],
    )(*shards, small)


def sibling_exchange(grads):
    n = len(grads)

    def body(*refs):
        g = refs[:n]
        o = refs[n:2 * n]
        send, recv = refs[2 * n:]
        x, y, c, _ = _place()
        cps = [_rcopy(g[t].at[1 - c], o[t], send.at[t], recv.at[t], (x, y, 1 - c)) for t in range(n)]
        for cp in cps:
            cp.start()
        for cp in cps:
            cp.wait()

    sems = pltpu.SemaphoreType.DMA
    return pl.pallas_call(
        body, name="sibling_exchange", out_shape=[jax.ShapeDtypeStruct(a.shape[1:], a.dtype) for a in grads],
        in_specs=[ANY] * n, out_specs=[ANY] * n, scratch_shapes=[sems((n,)), sems((n,))],
    )(*grads)


def chip_exchange(parts):
    n = len(parts)

    def body(*refs):
        p = refs[:n]
        o = refs[n:2 * n]
        send, recv = refs[2 * n:]
        _, _, c, chips = _place()
        cps = []
        for j, (cx, cy) in enumerate(chips):
            for t in range(n):
                cps.append(_rcopy(p[t].at[2 * cx + cy], o[t].at[j], send.at[j * n + t], recv.at[j * n + t],
                                  (cx, cy, c)))
        for cp in cps:
            cp.start()
        for cp in cps:
            cp.wait()

    sems = pltpu.SemaphoreType.DMA
    return pl.pallas_call(
        body, name="chip_exchange",
        out_shape=[jax.ShapeDtypeStruct((3,) + a.shape[1:], a.dtype) for a in parts],
        in_specs=[ANY] * n, out_specs=[ANY] * n, scratch_shapes=[sems((3 * n,)), sems((3 * n,))],
    )(*parts)


def sibling_assemble(halves, split_rows):
    n = len(halves)

    def body(*refs):
        h = refs[:n]
        o = refs[n:2 * n]
        send, recv, loc = refs[2 * n:]
        x, y, c, _ = _place()

        def dst(t, cc):
            r, w = halves[t].shape
            return o[t].at[pl.ds(cc * r, r), :] if split_rows[t] else o[t].at[:, pl.ds(cc * w, w)]

        local = [pltpu.make_async_copy(h[t], dst(t, c), loc.at[t]) for t in range(n)]
        cps = [_rcopy(h[t], dst(t, c), send.at[t], recv.at[t], (x, y, 1 - c)) for t in range(n)]
        for cp in local + cps:
            cp.start()
        for t in range(n):
            cps[t].wait_send()
            _rcopy(h[t], dst(t, 1 - c), send.at[t], recv.at[t], (x, y, 1 - c)).wait_recv()
        for cp in local:
            cp.wait()

    def full(t):
        r, w = halves[t].shape
        return jax.ShapeDtypeStruct((2 * r, w) if split_rows[t] else (r, 2 * w), halves[t].dtype)

    sems = pltpu.SemaphoreType.DMA
    return pl.pallas_call(
        body, name="sibling_assemble", out_shape=[full(t) for t in range(n)],
        in_specs=[ANY] * n, out_specs=[ANY] * n, scratch_shapes=[sems((n,)), sems((n,)), sems((n,))],
    )(*halves)


def small_all_reduce(ddw, v512, v1024):
    rows, width = PACK_ROWS, 512
    n512, n1024 = len(VEC512), len(VEC1024)

    def body(*refs):
        ddw_ref = refs[0]
        a_refs = refs[1:1 + n512]
        b_refs = refs[1 + n512:1 + n512 + n1024]
        o_ref, p_ref, gath_ref, send, recv = refs[1 + n512 + n1024:]
        p_ref[...] = jnp.zeros_like(p_ref)
        p_ref[0:32, :] = ddw_ref[...]
        for i, r in enumerate(a_refs):
            p_ref[32 + i:33 + i, :] = r[...]
        for i, r in enumerate(b_refs):
            base = 32 + n512 + 2 * i
            p_ref[base:base + 1, :] = r[:, 0:512]
            p_ref[base + 1:base + 2, :] = r[:, 512:1024]
        x, y, c, _ = _place()
        me = 4 * x + 2 * y + c
        gath_ref[me] = p_ref[...]
        cps = []
        for k in range(1, 8):
            dx, dy, dc = (k >> 2) & 1, (k >> 1) & 1, k & 1
            px = 1 - x if dx else x
            py = 1 - y if dy else y
            pc = 1 - c if dc else c
            cps.append(_rcopy(p_ref, gath_ref.at[me], send.at[k - 1], recv.at[k - 1], (px, py, pc)))
        for cp in cps:
            cp.start()
        for k in range(1, 8):
            dx, dy, dc = (k >> 2) & 1, (k >> 1) & 1, k & 1
            px = 1 - x if dx else x
            py = 1 - y if dy else y
            pc = 1 - c if dc else c
            _rcopy(p_ref, gath_ref.at[4 * px + 2 * py + pc], send.at[k - 1], recv.at[k - 1], (px, py, pc)).wait_recv()
        for cp in cps:
            cp.wait_send()
        total = gath_ref[0]
        for d in range(1, 8):
            total = total + gath_ref[d]
        o_ref[...] = total

    sems = pltpu.SemaphoreType.DMA
    n_in = 1 + n512 + n1024
    return pl.pallas_call(
        body, name="small_all_reduce", out_shape=jax.ShapeDtypeStruct((rows, width), F32),
        in_specs=[VMEM_SPEC] * n_in, out_specs=VMEM_SPEC,
        scratch_shapes=[pltpu.VMEM((rows, width), F32), pltpu.VMEM((8, rows, width), F32), sems((7,)), sems((7,))],
    )(ddw, *[v512[n] for n in VEC512], *[v1024[n] for n in VEC1024])


def _row_block(r):
    for tr in (512, 352, 256, 128):
        if r % tr == 0:
            return tr
    return r


def add_halves(g, recv, name):
    _, _, r, w = g.shape
    tr = _row_block(r)

    def body(g0_ref, g1_ref, r_ref, ob_ref, own_ref):
        k = pl.program_id(1)
        c = lax.axis_index("c")
        me = 2 * lax.axis_index("x") + lax.axis_index("y")
        t = jnp.where(c == 0, g0_ref[0, 0], g1_ref[0, 0]) + r_ref[0]
        ob_ref[0] = t.astype(MM)
        mine = jnp.where(k == me, t, 0.0)

        @pl.when(k == 0)
        def _():
            own_ref[...] = mine

        @pl.when(k != 0)
        def _():
            own_ref[...] += mine

    return pl.pallas_call(
        body, name=name, grid=(r // tr, N_CHIPS),
        in_specs=[pl.BlockSpec((1, 1, tr, w), lambda i, k: (0, k, i, 0)),
                  pl.BlockSpec((1, 1, tr, w), lambda i, k: (1, k, i, 0)),
                  pl.BlockSpec((1, tr, w), lambda i, k: (k, i, 0))],
        out_specs=[pl.BlockSpec((1, tr, w), lambda i, k: (k, i, 0)),
                   pl.BlockSpec((tr, w), lambda i, k: (i, 0))],
        out_shape=(jax.ShapeDtypeStruct((N_CHIPS, r, w), MM), jax.ShapeDtypeStruct((r, w), F32)),
        compiler_params=_params(("arbitrary", "arbitrary")),
    )(g, g, recv)


def sum_parts(own, rin, name):
    _, r, w = rin.shape
    tr = _row_block(r)

    def body(o_ref, r_ref, out_ref):
        out_ref[...] = ((o_ref[...] + r_ref[0].astype(F32)) + r_ref[1].astype(F32)) + r_ref[2].astype(F32)

    return pl.pallas_call(
        body, name=name, grid=(r // tr,), out_shape=jax.ShapeDtypeStruct((r, w), F32),
        in_specs=[pl.BlockSpec((tr, w), lambda i: (i, 0)), pl.BlockSpec((3, tr, w), lambda i: (0, i, 0))],
        out_specs=pl.BlockSpec((tr, w), lambda i: (i, 0)),
        compiler_params=_params(("arbitrary",)),
    )(own, rin)


def _adamw_math(w, g, m, v):
    mn = ADAM_B1 * m + (1.0 - ADAM_B1) * g
    vn = ADAM_B2 * v + (1.0 - ADAM_B2) * (g * g)
    m_hat = mn / (1.0 - ADAM_B1 ** ADAM_STEP)
    v_hat = vn / (1.0 - ADAM_B2 ** ADAM_STEP)
    return -ADAM_LR * (m_hat / (jnp.sqrt(v_hat) + ADAM_EPS) + ADAM_WD * w), mn, vn


def adamw(w, g, m, v, name):
    r, c = w.shape
    tr = _row_block(r)
    if c >= 1024 and tr % 512 == 0:
        tr = 256

    def body(w_ref, g_ref, m_ref, v_ref, go_ref, d_ref, mo_ref, vo_ref):
        gv = g_ref[...]
        go_ref[...] = gv
        d_ref[...], mo_ref[...], vo_ref[...] = _adamw_math(w_ref[...], gv, m_ref[...], v_ref[...])

    spec = pl.BlockSpec((tr, c), lambda i: (i, 0))
    out = jax.ShapeDtypeStruct((r, c), F32)
    return pl.pallas_call(
        body, name=name, grid=(r // tr,), out_shape=(out, out, out, out),
        in_specs=[spec] * 4, out_specs=[spec] * 4, compiler_params=_params(("arbitrary",)),
    )(w, g, m, v)


def adamw_small(gsum, params):
    names = list(params)
    flat = [a for n in names for a in params[n]]

    def body(*refs):
        g_ref = refs[0]
        ins = refs[1:1 + 3 * len(names)]
        outs = refs[1 + 3 * len(names):]
        me = 2 * lax.axis_index("x") + lax.axis_index("y")
        for i, n in enumerate(names):
            w_ref, m_ref, v_ref = ins[3 * i:3 * i + 3]
            go_ref, d_ref, mo_ref, vo_ref = outs[4 * i:4 * i + 4]
            if n == "conv_dw_w":
                gv = jnp.zeros((CONV_WIDTH, 128), F32)
                for k in range(N_CHIPS):
                    gv = gv + jnp.where(me == k, g_ref[0:CONV_WIDTH, 128 * k:128 * (k + 1)], 0.0)
            elif n in VEC512:
                r0 = 32 + VEC512.index(n)
                gv = g_ref[r0:r0 + 1, :]
            else:
                r0 = 32 + len(VEC512) + 2 * VEC1024.index(n)
                gv = jnp.concatenate([g_ref[r0:r0 + 1, :], g_ref[r0 + 1:r0 + 2, :]], axis=1)
            go_ref[...] = gv
            d_ref[...], mo_ref[...], vo_ref[...] = _adamw_math(w_ref[...], gv, m_ref[...], v_ref[...])

    out_shape = [jax.ShapeDtypeStruct(params[n][0].shape, F32) for n in names for _ in range(4)]
    res = pl.pallas_call(
        body, name="adamw_small", out_shape=out_shape,
        in_specs=[VMEM_SPEC] * (1 + len(flat)), out_specs=[VMEM_SPEC] * len(out_shape),
        compiler_params=_params(),
    )(gsum, *flat)
    return {n: res[4 * i:4 * i + 4] for i, n in enumerate(names)}


BIG = ("w_in", "w_ffn_up", "w_ffn_down", "w_out", "w_conv_branch", "w_att_branch")
SPLIT_ROWS = dict(w_in=True, w_ffn_up=True, w_ffn_down=False, w_out=False, w_conv_branch=True, w_att_branch=True)
VEC512 = ("conv_dw_b", "conv_ln_g", "conv_ln_b")
VEC1024 = ("norm_mix_pre", "b_conv_branch", "norm_mix_post", "norm_ffn_pre", "norm_ffn_post")
PACK_ROWS = 48


def local_step(xs, tgt, weights, wg, w_dw_full):
    row = lambda a: a.reshape(1, -1)
    g1, g2, g3, g4 = (row(weights[n]) for n in ("norm_mix_pre", "norm_mix_post", "norm_ffn_pre", "norm_ffn_post"))
    ln_g, ln_b = row(weights["conv_ln_g"]), row(weights["conv_ln_b"])
    w_out_g = wg["w_out"].reshape(D_MODEL, D_MODEL)
    w_down_g = wg["w_ffn_down"].reshape(D_FF, D_MODEL)
    w_cb_g, w_ab_g = wg["w_conv_branch"], wg["w_att_branch"]

    h1, ci, q, k, v, gc, ga = in_proj_fwd(xs, g1, wg["w_in"])
    u1, u3 = conv_fwd(ci, w_dw_full, row(weights["conv_dw_b"]), ln_g, ln_b)
    att, rc = attn_fwd(q, k, v)
    co, ao, merged, mix, x2, h2 = mix_fwd(u3, att, gc, ga, xs, w_cb_g, row(weights["b_conv_branch"]),
                                          w_ab_g, w_out_g, g2, g3)
    gate, up, act = ffn_up_fwd(h2, wg["w_ffn_up"])
    dff, dy, loss_parts, dg4 = ffn_down_loss(act, w_down_g, x2, tgt, g4)
    loss_local = jnp.sum(loss_parts[::8, 0])

    dgate, dup = ffn_act_bwd(dff, w_down_g, gate, up)
    dx2, dmix, dg3, dg2 = ffn_in_bwd(dgate, dup, wg["w_ffn_up"], x2, mix, dy, g3, g2)
    dco, dao, dgc, dga, du3, datt, dbcb = merge_bwd(dmix, w_out_g, gc, ga, co, ao, w_cb_g, w_ab_g)
    dci, ddw, dbdw, dlng, dlnb = conv_bwd(du3, u1, ci, w_dw_full, ln_g, ln_b)
    dq, dk, dv = attn_bwd(q, k, v, datt, rc)
    dproj = jnp.concatenate([dci, dq, dk, dv, dgc, dga], axis=1)
    grad_x, dg1 = in_proj_bwd(dproj, wg["w_in"], xs, dx2, g1)

    partial = dict(
        w_in=weight_grad(h1, dproj, "dw_in", True),
        w_ffn_up=weight_grad(h2, jnp.concatenate([dgate, dup], axis=1), "dw_ffn_up", True),
        w_ffn_down=weight_grad(act, dff, "dw_ffn_down", False, tk=UP_SHARD),
        w_out=weight_grad(merged, dmix, "dw_out", False, tk=512),
        w_conv_branch=weight_grad(u3, dco, "dw_conv_branch", True),
        w_att_branch=weight_grad(att, dao, "dw_att_branch", True),
    )
    v512 = dict(conv_dw_b=dbdw, conv_ln_g=dlng, conv_ln_b=dlnb)
    v1024 = dict(norm_mix_pre=dg1, b_conv_branch=dbcb, norm_mix_post=dg2, norm_ffn_pre=dg3, norm_ffn_post=dg4)
    return loss_local, grad_x, partial, ddw, v512, v1024


def kernel(x, norm_mix_pre, w_in, conv_dw_w, conv_dw_b, conv_ln_g, conv_ln_b, w_conv_branch, b_conv_branch, w_att_branch, w_out, norm_mix_post, norm_ffn_pre, w_ffn_up, w_ffn_down, norm_ffn_post, loss_target, m_norm_mix_pre, m_w_in, m_conv_dw_w, m_conv_dw_b, m_conv_ln_g, m_conv_ln_b, m_w_conv_branch, m_b_conv_branch, m_w_att_branch, m_w_out, m_norm_mix_post, m_norm_ffn_pre, m_w_ffn_up, m_w_ffn_down, m_norm_ffn_post, v_norm_mix_pre, v_w_in, v_conv_dw_w, v_conv_dw_b, v_conv_ln_g, v_conv_ln_b, v_w_conv_branch, v_b_conv_branch, v_w_att_branch, v_w_out, v_norm_mix_post, v_norm_ffn_pre, v_w_ffn_up, v_w_ffn_down, v_norm_ffn_post):
    weights = dict(norm_mix_pre=norm_mix_pre, w_in=w_in, conv_dw_w=conv_dw_w, conv_dw_b=conv_dw_b, conv_ln_g=conv_ln_g, conv_ln_b=conv_ln_b, w_conv_branch=w_conv_branch, b_conv_branch=b_conv_branch, w_att_branch=w_att_branch, w_out=w_out, norm_mix_post=norm_mix_post, norm_ffn_pre=norm_ffn_pre, w_ffn_up=w_ffn_up, w_ffn_down=w_ffn_down, norm_ffn_post=norm_ffn_post)
    mom = dict(norm_mix_pre=m_norm_mix_pre, w_in=m_w_in, conv_dw_w=m_conv_dw_w, conv_dw_b=m_conv_dw_b, conv_ln_g=m_conv_ln_g, conv_ln_b=m_conv_ln_b, w_conv_branch=m_w_conv_branch, b_conv_branch=m_b_conv_branch, w_att_branch=m_w_att_branch, w_out=m_w_out, norm_mix_post=m_norm_mix_post, norm_ffn_pre=m_norm_ffn_pre, w_ffn_up=m_w_ffn_up, w_ffn_down=m_w_ffn_down, norm_ffn_post=m_norm_ffn_post)
    var = dict(norm_mix_pre=v_norm_mix_pre, w_in=v_w_in, conv_dw_w=v_conv_dw_w, conv_dw_b=v_conv_dw_b, conv_ln_g=v_conv_ln_g, conv_ln_b=v_conv_ln_b, w_conv_branch=v_w_conv_branch, b_conv_branch=v_b_conv_branch, w_att_branch=v_w_att_branch, w_out=v_w_out, norm_mix_post=v_norm_mix_post, norm_ffn_pre=v_norm_ffn_pre, w_ffn_up=v_w_ffn_up, w_ffn_down=v_w_ffn_down, norm_ffn_post=v_norm_ffn_post)
    order = list(weights)

    gathered = all_gather_weights([weights[n].astype(MM) for n in BIG], conv_dw_w)
    wg = dict(zip(BIG, gathered[:-1]))
    w_dw_full = jnp.concatenate([gathered[-1][k] for k in range(N_CHIPS)], axis=1)

    loss_local, grad_x, partial, ddw, v512, v1024 = local_step(
        x.reshape(SEQ, D_MODEL), loss_target.reshape(SEQ, D_MODEL), weights, wg, w_dw_full)
    loss = lax.psum(loss_local, ("x", "y", "c"))

    from_sib = sibling_exchange([partial[n] for n in BIG])
    summed = [add_halves(partial[n], r, "add_" + n) for n, r in zip(BIG, from_sib)]
    from_chips = chip_exchange([s[0] for s in summed])
    halves = [sum_parts(s[1], r, "sum_" + n) for n, s, r in zip(BIG, summed, from_chips)]
    full = sibling_assemble(halves, [SPLIT_ROWS[n] for n in BIG])

    grads, deltas, new_m, new_v = {}, {}, {}, {}
    for n, g in zip(BIG, full):
        grads[n], deltas[n], new_m[n], new_v[n] = adamw(weights[n], g, mom[n], var[n], "adamw_" + n)

    gsum = small_all_reduce(ddw, v512, v1024)
    as_rows = lambda n, a: a if n == "conv_dw_w" else a.reshape(1, -1)
    small_names = ("conv_dw_w",) + VEC512 + VEC1024
    small = adamw_small(gsum, {n: tuple(as_rows(n, d[n]) for d in (weights, mom, var)) for n in small_names})
    for n in small_names:
        grads[n], deltas[n], new_m[n], new_v[n] = (a.reshape(weights[n].shape) for a in small[n])

    return (loss, grad_x.reshape(1, SEQ, D_MODEL), *[grads[n] for n in order], *[deltas[n] for n in order],
            *[new_m[n] for n in order], *[new_v[n] for n in order])
```

```python
import jax
import jax.numpy as jnp
from jax import lax
from jax.experimental import pallas as pl
from jax.experimental.pallas import tpu as pltpu

F32 = jnp.float32
MM = jnp.bfloat16

SEQ = 2048
D_MODEL = 1024
CONV_DIM = 512
ATT_DIM = 512
CONV_WIDTH = 31
D_FF = 2816
IN_COLS = 2 * CONV_DIM + 3 * ATT_DIM + 2 * D_MODEL
N_CHIPS = 4
IN_SHARD = IN_COLS // N_CHIPS
UP_SHARD = 2 * D_FF // N_CHIPS
BR_SHARD = D_MODEL // N_CHIPS
EPS = 1e-6
ATT_SCALE = 0.125

TM = 256
TQ = 128
CONV_TILE = 64
CONV_WIN = CONV_TILE + 32
VMEM_LIMIT = 56 * 1024 * 1024

ADAM_LR = 0.001
ADAM_B1 = 0.9
ADAM_B2 = 0.999
ADAM_EPS = 1e-08
ADAM_WD = 0.01
ADAM_STEP = 10

MESH = pl.DeviceIdType.MESH
ANY = pl.BlockSpec(memory_space=pl.ANY)
VMEM_SPEC = pl.BlockSpec(memory_space=pltpu.VMEM)

NT_DIMS = (((1,), (1,)), ((), ()))
TN_DIMS = (((0,), (0,)), ((), ()))

IN_PIECES = (("ci", 0, 1024), ("q", 1024, 1536), ("k", 1536, 2048), ("v", 2048, 2560),
             ("gc", 2560, 3584), ("ga", 3584, 4608))


def _params(sem=None, vmem=VMEM_LIMIT):
    return pltpu.CompilerParams(dimension_semantics=sem, vmem_limit_bytes=vmem)


def _dot(a, b):
    return jnp.dot(a, b, preferred_element_type=F32)


def _dot_nt(a, b):
    return lax.dot_general(a, b, NT_DIMS, preferred_element_type=F32)


def _dot_tn(a, b):
    return lax.dot_general(a, b, TN_DIMS, preferred_element_type=F32)


def _sigmoid(x):
    return 1.0 / (1.0 + jnp.exp(-x))


def _rms(x):
    r = lax.rsqrt(jnp.mean(x * x, axis=-1, keepdims=True) + EPS)
    return x * r, r


def _rms_bwd(dy_g, n, r):
    return r * (dy_g - n * jnp.mean(dy_g * n, axis=-1, keepdims=True))


def _row_tile_spec(width, tm=TM):
    return pl.BlockSpec((tm, width), lambda i: (i, 0))


def _full_spec(shape):
    nd = len(shape)
    return pl.BlockSpec(shape, lambda *_: (0,) * nd)


def _acc_rows(ref, val, first):
    @pl.when(first)
    def _():
        ref[...] = val

    @pl.when(jnp.logical_not(first))
    def _():
        ref[...] += val


def in_proj_fwd(x, g1, w_in_g):
    def body(x_ref, g_ref, w_ref, h_ref, ci_ref, q_ref, k_ref, v_ref, gc_ref, ga_ref):
        n, _ = _rms(x_ref[...])
        h = (n * g_ref[...]).astype(MM)
        h_ref[...] = h
        outs = dict(ci=ci_ref, q=q_ref, k=k_ref, v=v_ref, gc=gc_ref, ga=ga_ref)
        for j in range(N_CHIPS):
            p = _dot(h, w_ref[j])
            g0 = j * IN_SHARD
            for name, s, e in IN_PIECES:
                lo, hi = max(s, g0), min(e, g0 + IN_SHARD)
                if lo < hi:
                    ref = outs[name]
                    ref[:, lo - s:hi - s] = p[:, lo - g0:hi - g0].astype(ref.dtype)

    out_shape = (
        jax.ShapeDtypeStruct((SEQ, D_MODEL), MM),
        jax.ShapeDtypeStruct((SEQ, 2 * CONV_DIM), F32),
        jax.ShapeDtypeStruct((SEQ, ATT_DIM), MM),
        jax.ShapeDtypeStruct((SEQ, ATT_DIM), MM),
        jax.ShapeDtypeStruct((SEQ, ATT_DIM), MM),
        jax.ShapeDtypeStruct((SEQ, D_MODEL), F32),
        jax.ShapeDtypeStruct((SEQ, D_MODEL), F32),
    )
    return pl.pallas_call(
        body, name="in_proj_fwd", grid=(SEQ // TM,), out_shape=out_shape,
        in_specs=[_row_tile_spec(D_MODEL), _full_spec((1, D_MODEL)), _full_spec(w_in_g.shape)],
        out_specs=[_row_tile_spec(s.shape[1]) for s in out_shape],
        compiler_params=_params(("arbitrary",)),
    )(x, g1, w_in_g)


def _shifted_sum(win, terms):
    by_rot = {}
    for m, coef in terms:
        by_rot.setdefault(m % 8, []).append((m // 8, coef))
    acc = None
    n = win.shape[0]
    for rot in sorted(by_rot):
        shifted = win if rot == 0 else pltpu.roll(win, n - rot, 0)
        for a, coef in by_rot[rot]:
            t = coef * shifted[8 * a:8 * a + CONV_TILE, :]
            acc = t if acc is None else acc + t
    return acc


def _glu_into(ci_ref, upad_ref):
    upad_ref[0:32, :] = jnp.zeros((32, CONV_DIM), F32)

    def step(i, c):
        t0 = pl.multiple_of(i * TM, TM)
        a = ci_ref[pl.ds(t0, TM), 0:CONV_DIM]
        b = ci_ref[pl.ds(t0, TM), CONV_DIM:2 * CONV_DIM]
        upad_ref[pl.ds(t0 + 32, TM), :] = a * _sigmoid(b)
        return c

    lax.fori_loop(0, SEQ // TM, step, 0)


def _layernorm_parts(u1):
    mu = jnp.mean(u1, axis=-1, keepdims=True)
    xc = u1 - mu
    rstd = lax.rsqrt(jnp.mean(xc * xc, axis=-1, keepdims=True) + EPS)
    return xc * rstd, rstd


def conv_fwd(ci, w_dw, b_dw, ln_g, ln_b):
    def body(ci_ref, w_ref, b_ref, g_ref, bb_ref, u1_ref, u3_ref, upad_ref):
        _glu_into(ci_ref, upad_ref)

        def step(i, c):
            t0 = pl.multiple_of(i * CONV_TILE, CONV_TILE)
            win = upad_ref[pl.ds(t0, CONV_WIN), :]
            u1 = _shifted_sum(win, [(j + 2, w_ref[j:j + 1, :]) for j in range(CONV_WIDTH)]) + b_ref[...]
            u1_ref[pl.ds(t0, CONV_TILE), :] = u1
            xh, _ = _layernorm_parts(u1)
            u2 = xh * g_ref[...] + bb_ref[...]
            u3_ref[pl.ds(t0, CONV_TILE), :] = (u2 * _sigmoid(u2)).astype(MM)
            return c

        lax.fori_loop(0, SEQ // CONV_TILE, step, 0)

    return pl.pallas_call(
        body, name="conv_fwd",
        out_shape=(jax.ShapeDtypeStruct((SEQ, CONV_DIM), F32), jax.ShapeDtypeStruct((SEQ, CONV_DIM), MM)),
        in_specs=[VMEM_SPEC] * 5, out_specs=[VMEM_SPEC] * 2,
        scratch_shapes=[pltpu.VMEM((SEQ + 32, CONV_DIM), F32)],
        compiler_params=_params(),
    )(ci, w_dw, b_dw, ln_g, ln_b)


def _softplus(z):
    return jnp.maximum(z, 0.0) + jnp.log(1.0 + jnp.exp(-jnp.abs(z)))


def _cumsum_weights(suffix, with_total):
    n = 256 if with_total else 128
    r = lax.broadcasted_iota(jnp.int32, (256, n), 0) & 127
    c = lax.broadcasted_iota(jnp.int32, (256, n), 1)
    tri = (r >= c) if suffix else (r <= c)
    return jnp.logical_or(tri, c >= 128).astype(MM)


def _hi_lo(x):
    hi = x.astype(MM)
    lo = (x - hi.astype(F32)).astype(MM)
    return jnp.concatenate([hi, lo], axis=1)


def _cumsum_dot(x, w):
    r = _dot(_hi_lo(x), w)
    return (r[:, :128], r[:, 128:]) if w.shape[1] == 256 else (r, None)


NO_SCORE = -1e30


def _head_masks():
    lane = lax.broadcasted_iota(jnp.int32, (TQ, 128), 1)
    row = lax.broadcasted_iota(jnp.int32, (TQ, 128), 0)
    return lane, row, lane < 64


def _pick_head(x, head0, h):
    zero = jnp.zeros_like(x)
    return jnp.where(head0, x, zero) if h == 0 else jnp.where(head0, zero, x)


N_PAIRS = ATT_DIM // 128


def attn_fwd(q, k, v):
    def body(q_ref, k_ref, v_ref, o_ref, rc_ref, acc_ref, r_ref, z_ref, spb_ref, ab_ref):
        i = pl.program_id(0)
        lane, row, head0 = _head_masks()
        w = _cumsum_weights(suffix=True, with_total=True)
        acc_ref[...] = jnp.zeros_like(acc_ref)
        r_ref[...] = jnp.zeros_like(r_ref)
        rc_ref[...] = jnp.zeros_like(rc_ref)
        z_ref[...] = jnp.full(z_ref.shape, NO_SCORE, F32)
        spb_ref[...] = jnp.zeros_like(spb_ref)
        ab_ref[...] = jnp.zeros_like(ab_ref)

        def step(n, c):
            b1 = i - n
            b2 = b1 + 1
            k1 = pl.multiple_of(jnp.maximum(b1, 0) * TQ, TQ)
            k3 = pl.multiple_of(jnp.clip(b1 + 2, 0, i) * TQ, TQ)
            keep1 = jnp.logical_and(b1 >= 0, jnp.logical_or(b1 < i, lane < row))
            mark2 = jnp.logical_and(lane == b2, b2 <= i)
            for p in range(N_PAIRS):
                cols = slice(128 * p, 128 * (p + 1))
                qb = q_ref[:, cols]
                kb = k_ref[pl.ds(k1, TQ), cols]
                vb = v_ref[pl.ds(k3, TQ), cols]
                for h in range(2):
                    hh = 2 * p + h
                    acc_ref[:, cols] += _dot(ab_ref[hh], _pick_head(vb, head0, h))
                    r = _dot(spb_ref[hh], w)
                    r_in = r_ref[hh]
                    ab_ref[hh] = jnp.exp(z_ref[hh] - (r[:, :128] + r_in)).astype(MM)
                    rc_ref[hh] = jnp.where(mark2, r_in, rc_ref[hh])
                    r_ref[hh] = r_in + r[:, 128:]
                    z = _dot_nt(_pick_head(qb, head0, h), kb) * ATT_SCALE
                    sp = jnp.where(keep1, _softplus(z), 0.0)
                    z_ref[hh] = jnp.where(keep1, z, NO_SCORE)
                    spb_ref[hh] = _hi_lo(sp)
            return c

        lax.fori_loop(0, i + 3, step, 0)
        o_ref[...] = acc_ref[...].astype(MM)

    nqb = SEQ // TQ
    full = pl.BlockSpec((SEQ, ATT_DIM), lambda i: (0, 0))
    return pl.pallas_call(
        body, name="attn_fwd", grid=(nqb,),
        out_shape=(jax.ShapeDtypeStruct((SEQ, ATT_DIM), MM), jax.ShapeDtypeStruct((8, SEQ, 128), F32)),
        in_specs=[_row_tile_spec(ATT_DIM, TQ), full, full],
        out_specs=[_row_tile_spec(ATT_DIM, TQ), pl.BlockSpec((8, TQ, 128), lambda i: (0, i, 0))],
        scratch_shapes=[pltpu.VMEM((TQ, ATT_DIM), F32), pltpu.VMEM((8, TQ, 128), F32),
                        pltpu.VMEM((8, TQ, 128), F32), pltpu.VMEM((8, TQ, 256), MM), pltpu.VMEM((8, TQ, 128), MM)],
        compiler_params=_params(("arbitrary",)),
    )(q, k, v)


def mix_fwd(u3, att, gc, ga, x, w_cb_g, b_cb, w_ab_g, w_out_g, g2, g3):
    def body(u_ref, a_ref, gc_ref, ga_ref, x_ref, wcb_ref, bcb_ref, wab_ref, wout_ref, g2_ref, g3_ref,
             co_ref, ao_ref, mg_ref, mix_ref, x2_ref, h2_ref):
        u = u_ref[...]
        a = a_ref[...]
        for j in range(N_CHIPS):
            cols = slice(j * BR_SHARD, (j + 1) * BR_SHARD)
            co_ref[:, cols] = _dot(u, wcb_ref[j]) + bcb_ref[:, cols]
            ao_ref[:, cols] = _dot(a, wab_ref[j])
        merged = (_sigmoid(gc_ref[...]) * co_ref[...] + _sigmoid(ga_ref[...]) * ao_ref[...]).astype(MM)
        mg_ref[...] = merged
        mix = _dot(merged, wout_ref[...])
        mix_ref[...] = mix
        n2, _ = _rms(mix)
        x2 = x_ref[...] + n2 * g2_ref[...]
        x2_ref[...] = x2
        n3, _ = _rms(x2)
        h2_ref[...] = (n3 * g3_ref[...]).astype(MM)

    out_shape = (
        jax.ShapeDtypeStruct((SEQ, D_MODEL), F32), jax.ShapeDtypeStruct((SEQ, D_MODEL), F32),
        jax.ShapeDtypeStruct((SEQ, D_MODEL), MM), jax.ShapeDtypeStruct((SEQ, D_MODEL), F32),
        jax.ShapeDtypeStruct((SEQ, D_MODEL), F32), jax.ShapeDtypeStruct((SEQ, D_MODEL), MM),
    )
    vec = _full_spec((1, D_MODEL))
    return pl.pallas_call(
        body, name="mix_fwd", grid=(SEQ // TM,), out_shape=out_shape,
        in_specs=[_row_tile_spec(CONV_DIM), _row_tile_spec(ATT_DIM), _row_tile_spec(D_MODEL),
                  _row_tile_spec(D_MODEL), _row_tile_spec(D_MODEL), _full_spec(w_cb_g.shape), vec,
                  _full_spec(w_ab_g.shape), _full_spec(w_out_g.shape), vec, vec],
        out_specs=[_row_tile_spec(D_MODEL)] * 6,
        compiler_params=_params(("arbitrary",)),
    )(u3, att, gc, ga, x, w_cb_g, b_cb, w_ab_g, w_out_g, g2, g3)


def ffn_up_fwd(h2, w_up_g):
    def body(h_ref, wg_ref, wu_ref, gate_ref, up_ref, act_ref):
        h = h_ref[...]
        gate = _dot(h, wg_ref[0])
        up = _dot(h, wu_ref[0])
        gate_ref[...] = gate
        up_ref[...] = up
        act_ref[...] = (gate * _sigmoid(gate) * up).astype(MM)

    tile = pl.BlockSpec((TM, UP_SHARD), lambda n, i: (i, n))
    return pl.pallas_call(
        body, name="ffn_up_fwd", grid=(2, SEQ // TM),
        out_shape=(jax.ShapeDtypeStruct((SEQ, D_FF), F32), jax.ShapeDtypeStruct((SEQ, D_FF), F32),
                   jax.ShapeDtypeStruct((SEQ, D_FF), MM)),
        in_specs=[pl.BlockSpec((TM, D_MODEL), lambda n, i: (i, 0)),
                  pl.BlockSpec((1, D_MODEL, UP_SHARD), lambda n, i: (n, 0, 0)),
                  pl.BlockSpec((1, D_MODEL, UP_SHARD), lambda n, i: (n + 2, 0, 0))],
        out_specs=[tile, tile, tile],
        compiler_params=_params(("arbitrary", "arbitrary")),
    )(h2, w_up_g, w_up_g)


def ffn_down_loss(act, w_down_g, x2, target, g4):
    def body(act_ref, wd_ref, x2_ref, t_ref, g_ref, dff_ref, dy_ref, loss_ref, dg_ref):
        ff = _dot(act_ref[...], wd_ref[...])
        n4, r4 = _rms(ff)
        g4v = g_ref[...]
        err = x2_ref[...] + n4 * g4v - t_ref[...]
        row_loss = jnp.mean(err * err, axis=-1, keepdims=True)
        loss_ref[...] = jnp.zeros((8, 128), F32) + 0.5 * jnp.sum(row_loss, axis=0, keepdims=True)
        dy = err * (1.0 / D_MODEL)
        dy_ref[...] = dy
        dff_ref[...] = _rms_bwd(dy * g4v, n4, r4).astype(MM)
        _acc_rows(dg_ref, jnp.sum(dy * n4, axis=0, keepdims=True), pl.program_id(0) == 0)

    nt = SEQ // TM
    vec = _full_spec((1, D_MODEL))
    return pl.pallas_call(
        body, name="ffn_down_loss", grid=(nt,),
        out_shape=(jax.ShapeDtypeStruct((SEQ, D_MODEL), MM), jax.ShapeDtypeStruct((SEQ, D_MODEL), F32),
                   jax.ShapeDtypeStruct((nt * 8, 128), F32), jax.ShapeDtypeStruct((1, D_MODEL), F32)),
        in_specs=[_row_tile_spec(D_FF), _full_spec(w_down_g.shape), _row_tile_spec(D_MODEL),
                  _row_tile_spec(D_MODEL), vec],
        out_specs=[_row_tile_spec(D_MODEL), _row_tile_spec(D_MODEL),
                   pl.BlockSpec((8, 128), lambda i: (i, 0)), vec],
        compiler_params=_params(("arbitrary",)),
    )(act, w_down_g, x2, target, g4)


def ffn_act_bwd(dff, w_down_g, gate, up):
    def body(dff_ref, wd_ref, gate_ref, up_ref, dgate_ref, dup_ref):
        dact = _dot_nt(dff_ref[...], wd_ref[...])
        gate = gate_ref[...]
        sg = _sigmoid(gate)
        dgate_ref[...] = (dact * up_ref[...] * (sg * (1.0 + gate * (1.0 - sg)))).astype(MM)
        dup_ref[...] = (dact * (gate * sg)).astype(MM)

    return pl.pallas_call(
        body, name="ffn_act_bwd", grid=(SEQ // TM,),
        out_shape=(jax.ShapeDtypeStruct((SEQ, D_FF), MM), jax.ShapeDtypeStruct((SEQ, D_FF), MM)),
        in_specs=[_row_tile_spec(D_MODEL), _full_spec(w_down_g.shape), _row_tile_spec(D_FF), _row_tile_spec(D_FF)],
        out_specs=[_row_tile_spec(D_FF), _row_tile_spec(D_FF)],
        compiler_params=_params(("arbitrary",)),
    )(dff, w_down_g, gate, up)


def ffn_in_bwd(dgate, dup, w_up_g, x2, mix, dy, g3, g2):
    def body(dg_ref, du_ref, w_ref, x2_ref, mix_ref, dy_ref, g3_ref, g2_ref, dx2_ref, dmix_ref, dg3_ref, dg2_ref):
        dh2 = None
        for j in range(2):
            cols = slice(j * UP_SHARD, (j + 1) * UP_SHARD)
            t = _dot_nt(dg_ref[:, cols], w_ref[j]) + _dot_nt(du_ref[:, cols], w_ref[j + 2])
            dh2 = t if dh2 is None else dh2 + t
        first = pl.program_id(0) == 0
        n3, r3 = _rms(x2_ref[...])
        dx2 = dy_ref[...] + _rms_bwd(dh2 * g3_ref[...], n3, r3)
        dx2_ref[...] = dx2
        _acc_rows(dg3_ref, jnp.sum(dh2 * n3, axis=0, keepdims=True), first)
        n2, r2 = _rms(mix_ref[...])
        dmix_ref[...] = _rms_bwd(dx2 * g2_ref[...], n2, r2).astype(MM)
        _acc_rows(dg2_ref, jnp.sum(dx2 * n2, axis=0, keepdims=True), first)

    vec = _full_spec((1, D_MODEL))
    return pl.pallas_call(
        body, name="ffn_in_bwd", grid=(SEQ // TM,),
        out_shape=(jax.ShapeDtypeStruct((SEQ, D_MODEL), F32), jax.ShapeDtypeStruct((SEQ, D_MODEL), MM),
                   jax.ShapeDtypeStruct((1, D_MODEL), F32), jax.ShapeDtypeStruct((1, D_MODEL), F32)),
        in_specs=[_row_tile_spec(D_FF), _row_tile_spec(D_FF), _full_spec(w_up_g.shape), _row_tile_spec(D_MODEL),
                  _row_tile_spec(D_MODEL), _row_tile_spec(D_MODEL), vec, vec],
        out_specs=[_row_tile_spec(D_MODEL), _row_tile_spec(D_MODEL), vec, vec],
        compiler_params=_params(("arbitrary",)),
    )(dgate, dup, w_up_g, x2, mix, dy, g3, g2)


def merge_bwd(dmix, w_out_g, gc, ga, co, ao, w_cb_g, w_ab_g):
    def body(dmix_ref, wout_ref, gc_ref, ga_ref, co_ref, ao_ref, wcb_ref, wab_ref,
             dco_ref, dao_ref, dgc_ref, dga_ref, du3_ref, datt_ref, dbcb_ref):
        dm = _dot_nt(dmix_ref[...], wout_ref[...])
        sgc = _sigmoid(gc_ref[...])
        sga = _sigmoid(ga_ref[...])
        dco = dm * sgc
        dao = dm * sga
        dgc_ref[...] = (dm * co_ref[...] * (sgc * (1.0 - sgc))).astype(MM)
        dga_ref[...] = (dm * ao_ref[...] * (sga * (1.0 - sga))).astype(MM)
        _acc_rows(dbcb_ref, jnp.sum(dco, axis=0, keepdims=True), pl.program_id(0) == 0)
        dco_ref[...] = dco.astype(MM)
        dao_ref[...] = dao.astype(MM)
        du3 = None
        datt = None
        for j in range(N_CHIPS):
            cols = slice(j * BR_SHARD, (j + 1) * BR_SHARD)
            t = _dot_nt(dco_ref[:, cols], wcb_ref[j])
            s = _dot_nt(dao_ref[:, cols], wab_ref[j])
            du3 = t if du3 is None else du3 + t
            datt = s if datt is None else datt + s
        du3_ref[...] = du3
        datt_ref[...] = datt.astype(MM)

    wide = _row_tile_spec(D_MODEL)
    return pl.pallas_call(
        body, name="merge_bwd", grid=(SEQ // TM,),
        out_shape=(jax.ShapeDtypeStruct((SEQ, D_MODEL), MM), jax.ShapeDtypeStruct((SEQ, D_MODEL), MM),
                   jax.ShapeDtypeStruct((SEQ, D_MODEL), MM), jax.ShapeDtypeStruct((SEQ, D_MODEL), MM),
                   jax.ShapeDtypeStruct((SEQ, CONV_DIM), F32), jax.ShapeDtypeStruct((SEQ, ATT_DIM), MM),
                   jax.ShapeDtypeStruct((1, D_MODEL), F32)),
        in_specs=[wide, _full_spec(w_out_g.shape), wide, wide, wide, wide,
                  _full_spec(w_cb_g.shape), _full_spec(w_ab_g.shape)],
        out_specs=[wide, wide, wide, wide, _row_tile_spec(CONV_DIM), _row_tile_spec(ATT_DIM),
                   _full_spec((1, D_MODEL))],
        compiler_params=_params(("arbitrary",)),
    )(dmix, w_out_g, gc, ga, co, ao, w_cb_g, w_ab_g)


def conv_bwd(du3, u1, ci, w_dw, ln_g, ln_b):
    def body(du3_ref, u1_ref, ci_ref, w_ref, g_ref, bb_ref,
             dci_ref, dw_ref, dbdw_ref, dg_ref, db_ref, upad_ref, dpad_ref, dwacc_ref, vacc_ref):
        _glu_into(ci_ref, upad_ref)
        dpad_ref[SEQ:SEQ + 32, :] = jnp.zeros((32, CONV_DIM), F32)
        dwacc_ref[...] = jnp.zeros_like(dwacc_ref)
        vacc_ref[...] = jnp.zeros_like(vacc_ref)

        def fold8(t):
            s = t[0:8, :]
            for r in range(1, CONV_TILE // 8):
                s = s + t[8 * r:8 * r + 8, :]
            return s

        def pass1(i, c):
            t0 = pl.multiple_of(i * CONV_TILE, CONV_TILE)
            xh, rstd = _layernorm_parts(u1_ref[pl.ds(t0, CONV_TILE), :])
            gv = g_ref[...]
            u2 = xh * gv + bb_ref[...]
            s2 = _sigmoid(u2)
            du2 = du3_ref[pl.ds(t0, CONV_TILE), :] * (s2 * (1.0 + u2 * (1.0 - s2)))
            wv = du2 * gv
            du1 = rstd * (wv - jnp.mean(wv, axis=-1, keepdims=True)
                          - xh * jnp.mean(wv * xh, axis=-1, keepdims=True))
            dpad_ref[pl.ds(t0, CONV_TILE), :] = du1
            vacc_ref[0] += fold8(du2 * xh)
            vacc_ref[1] += fold8(du2)
            vacc_ref[2] += fold8(du1)
            win = upad_ref[pl.ds(t0, CONV_WIN), :]
            n = win.shape[0]
            for rot in range(8):
                shifted = win if rot == 0 else pltpu.roll(win, n - rot, 0)
                for a in range(5):
                    j = 8 * a + rot - 2
                    if 0 <= j < CONV_WIDTH:
                        dwacc_ref[j] += fold8(du1 * shifted[8 * a:8 * a + CONV_TILE, :])
            return c

        lax.fori_loop(0, SEQ // CONV_TILE, pass1, 0)

        def pass2(i, c):
            t0 = pl.multiple_of(i * CONV_TILE, CONV_TILE)
            win = dpad_ref[pl.ds(t0, CONV_WIN), :]
            du0 = _shifted_sum(win, [(30 - j, w_ref[j:j + 1, :]) for j in range(CONV_WIDTH)])
            a = ci_ref[pl.ds(t0, CONV_TILE), 0:CONV_DIM]
            sb = _sigmoid(ci_ref[pl.ds(t0, CONV_TILE), CONV_DIM:2 * CONV_DIM])
            dci_ref[pl.ds(t0, CONV_TILE), 0:CONV_DIM] = (du0 * sb).astype(MM)
            dci_ref[pl.ds(t0, CONV_TILE), CONV_DIM:2 * CONV_DIM] = (du0 * a * (sb * (1.0 - sb))).astype(MM)
            return c

        lax.fori_loop(0, SEQ // CONV_TILE, pass2, 0)

        for j in range(CONV_WIDTH):
            dw_ref[j:j + 1, :] = jnp.sum(dwacc_ref[j], axis=0, keepdims=True)
        dw_ref[CONV_WIDTH:32, :] = jnp.zeros((32 - CONV_WIDTH, CONV_DIM), F32)
        dg_ref[...] = jnp.sum(vacc_ref[0], axis=0, keepdims=True)
        db_ref[...] = jnp.sum(vacc_ref[1], axis=0, keepdims=True)
        dbdw_ref[...] = jnp.sum(vacc_ref[2], axis=0, keepdims=True)

    vec = jax.ShapeDtypeStruct((1, CONV_DIM), F32)
    return pl.pallas_call(
        body, name="conv_bwd",
        out_shape=(jax.ShapeDtypeStruct((SEQ, 2 * CONV_DIM), MM), jax.ShapeDtypeStruct((32, CONV_DIM), F32),
                   vec, vec, vec),
        in_specs=[VMEM_SPEC] * 6, out_specs=[VMEM_SPEC] * 5,
        scratch_shapes=[pltpu.VMEM((SEQ + 32, CONV_DIM), F32), pltpu.VMEM((SEQ + 32, CONV_DIM), F32),
                        pltpu.VMEM((CONV_WIDTH, 8, CONV_DIM), F32), pltpu.VMEM((3, 8, CONV_DIM), F32)],
        compiler_params=_params(),
    )(du3, u1, ci, w_dw, ln_g, ln_b)


def attn_bwd(q, k, v, datt, rc):
    nqb = SEQ // TQ

    def body(q_ref, k_ref, v_ref, do_ref, rc_ref, dq_ref, dk_ref, dv_ref, dqa_ref, dka_ref, dva_ref, pc_ref,
             z_ref, sig1_ref, sig2_ref, g_ref, spb_ref, gb_ref, ab_ref, dzb_ref):
        i = pl.program_id(0)
        lane, row, head0 = _head_masks()

        @pl.when(i == 0)
        def _():
            dka_ref[...] = jnp.zeros_like(dka_ref)
            dva_ref[...] = jnp.zeros_like(dva_ref)

        dqa_ref[...] = jnp.zeros_like(dqa_ref)
        pc_ref[...] = jnp.zeros_like(pc_ref)
        z_ref[...] = jnp.full(z_ref.shape, NO_SCORE, F32)
        for ref in (sig1_ref, sig2_ref, spb_ref, ab_ref, g_ref, gb_ref, dzb_ref):
            ref[...] = jnp.zeros_like(ref)
        w_suffix = _cumsum_weights(suffix=True, with_total=False)
        w_prefix = _cumsum_weights(suffix=False, with_total=True)

        def step(n, c):
            ka, kb_, kc, kd = (pl.multiple_of(jnp.clip(n - s, 0, i) * TQ, TQ) for s in range(4))
            below = lane < row
            keep_a = jnp.logical_and(n <= i, jnp.logical_or(n < i, below))
            bc = n - 2
            keep_c = jnp.logical_and(jnp.logical_and(bc >= 0, bc <= i), jnp.logical_or(bc < i, below))
            pick_b = lane == n - 1
            for p in range(N_PAIRS):
                cols = slice(128 * p, 128 * (p + 1))
                qb = q_ref[:, cols]
                dob = do_ref[:, cols]
                k_a = k_ref[pl.ds(ka, TQ), cols]
                v_b = v_ref[pl.ds(kb_, TQ), cols]
                k_d = k_ref[pl.ds(kd, TQ), cols]
                for h in range(2):
                    hh = 2 * p + h
                    qm = _pick_head(qb, head0, h)
                    dom = _pick_head(dob, head0, h)
                    dzb = dzb_ref[hh]
                    dqa_ref[:, cols] += _dot(dzb, _pick_head(k_d, head0, h))
                    dka_ref[pl.ds(kd, TQ), cols] += _dot_tn(dzb, qm)
                    r = _dot(gb_ref[hh], w_prefix)
                    p_in = pc_ref[hh]
                    dz = (g_ref[hh] - sig2_ref[hh] * (r[:, :128] + p_in)) * ATT_SCALE
                    dzb_ref[hh] = jnp.where(keep_c, dz, 0.0).astype(MM)
                    pc_ref[hh] = p_in + r[:, 128:]
                    dva_ref[pl.ds(kc, TQ), cols] += _dot_tn(ab_ref[hh], dom)
                    r_in = jnp.sum(jnp.where(pick_b, rc_ref[hh], 0.0), axis=1, keepdims=True)
                    a = jnp.exp(z_ref[hh] - (_dot(spb_ref[hh], w_suffix) + r_in))
                    g = _dot_nt(dom, v_b) * a
                    ab_ref[hh] = a.astype(MM)
                    g_ref[hh] = g
                    gb_ref[hh] = _hi_lo(g)
                    sig2_ref[hh] = sig1_ref[hh]
                    z = _dot_nt(qm, k_a) * ATT_SCALE
                    sp = _softplus(z)
                    sig1_ref[hh] = jnp.exp(z - sp)
                    z_ref[hh] = jnp.where(keep_a, z, NO_SCORE)
                    spb_ref[hh] = _hi_lo(jnp.where(keep_a, sp, 0.0))
            return c

        lax.fori_loop(0, i + 4, step, 0)
        dq_ref[...] = dqa_ref[...].astype(MM)

        @pl.when(i == nqb - 1)
        def _():
            dk_ref[...] = dka_ref[...].astype(MM)
            dv_ref[...] = dva_ref[...].astype(MM)

    tile = _row_tile_spec(ATT_DIM, TQ)
    full = pl.BlockSpec((SEQ, ATT_DIM), lambda i: (0, 0))
    out = jax.ShapeDtypeStruct((SEQ, ATT_DIM), MM)
    return pl.pallas_call(
        body, name="attn_bwd", grid=(nqb,), out_shape=(out, out, out),
        in_specs=[tile, full, full, tile, pl.BlockSpec((8, TQ, 128), lambda i: (0, i, 0))],
        out_specs=[tile, full, full],
        scratch_shapes=[pltpu.VMEM((TQ, ATT_DIM), F32), pltpu.VMEM((SEQ, ATT_DIM), F32),
                        pltpu.VMEM((SEQ, ATT_DIM), F32)] + [pltpu.VMEM((8, TQ, 128), F32)] * 5
                       + [pltpu.VMEM((8, TQ, 256), MM)] * 2 + [pltpu.VMEM((8, TQ, 128), MM)] * 2,
        compiler_params=_params(("arbitrary",)),
    )(q, k, v, datt, rc)


def in_proj_bwd(dproj, w_in_g, x, dx2, g1):
    def body(dp_ref, w_ref, x_ref, dx2_ref, g_ref, dx_ref, dg_ref):
        dh = None
        for j in range(N_CHIPS):
            t = _dot_nt(dp_ref[:, j * IN_SHARD:(j + 1) * IN_SHARD], w_ref[j])
            dh = t if dh is None else dh + t
        n1, r1 = _rms(x_ref[...])
        dx_ref[...] = dx2_ref[...] + _rms_bwd(dh * g_ref[...], n1, r1)
        _acc_rows(dg_ref, jnp.sum(dh * n1, axis=0, keepdims=True), pl.program_id(0) == 0)

    vec = _full_spec((1, D_MODEL))
    return pl.pallas_call(
        body, name="in_proj_bwd", grid=(SEQ // TM,),
        out_shape=(jax.ShapeDtypeStruct((SEQ, D_MODEL), F32), jax.ShapeDtypeStruct((1, D_MODEL), F32)),
        in_specs=[_row_tile_spec(IN_COLS), _full_spec(w_in_g.shape), _row_tile_spec(D_MODEL),
                  _row_tile_spec(D_MODEL), vec],
        out_specs=[_row_tile_spec(D_MODEL), vec],
        compiler_params=_params(("arbitrary",)),
    )(dproj, w_in_g, x, dx2, g1)


def weight_grad(a, b, name, col_sharded, tk=None):
    kin, n = a.shape[1], b.shape[1]

    def body(a_ref, b_ref, o_ref):
        if col_sharded:
            o_ref[0, 0] = _dot_tn(a_ref[...], b_ref[...])
        else:
            o_ref[0] = _dot_tn(a_ref[...], b_ref[...])

    if col_sharded:
        kh, ns = kin // 2, n // N_CHIPS
        out = jax.ShapeDtypeStruct((2, N_CHIPS, kh, ns), F32)
        grid = (2, N_CHIPS)
        in_specs = [pl.BlockSpec((SEQ, kh), lambda h, j: (0, h)), pl.BlockSpec((SEQ, ns), lambda h, j: (0, j))]
        out_spec = pl.BlockSpec((1, 1, kh, ns), lambda h, j: (h, j, 0, 0))
    else:
        nh = n // 2
        out = jax.ShapeDtypeStruct((2, kin, nh), F32)
        grid = (2, kin // tk)
        in_specs = [pl.BlockSpec((SEQ, tk), lambda h, r: (0, r)), pl.BlockSpec((SEQ, nh), lambda h, r: (0, h))]
        out_spec = pl.BlockSpec((1, tk, nh), lambda h, r: (h, r, 0))
    res = pl.pallas_call(
        body, name=name, grid=grid, out_shape=out, in_specs=in_specs, out_specs=out_spec,
        compiler_params=_params(("arbitrary", "arbitrary")),
    )(a, b)
    if not col_sharded:
        res = res.reshape(2, N_CHIPS, kin // N_CHIPS, nh)
    return res


def _place():
    x, y, c = lax.axis_index("x"), lax.axis_index("y"), lax.axis_index("c")
    chips = [(1 - x, y), (x, 1 - y), (1 - x, 1 - y)]
    return x, y, c, chips


def _rcopy(src, dst, send_sem, recv_sem, dev):
    return pltpu.make_async_remote_copy(src_ref=src, dst_ref=dst, send_sem=send_sem, recv_sem=recv_sem,
                                        device_id=dev, device_id_type=MESH)


def all_gather_weights(shards, small):
    n = len(shards)

    def body(*refs):
        w = refs[:n]
        sm = refs[n]
        o = refs[n + 1:2 * n + 1]
        osm = refs[2 * n + 1]
        send, recv, fsend, frecv, ssend, srecv, loc = refs[2 * n + 2:]
        x, y, c, chips = _place()
        me = 2 * x + y
        sib = (x, y, 1 - c)

        def half(ref, k, cc, t):
            rh = shards[t].shape[0] // 2
            return ref.at[k, pl.ds(cc * rh, rh), :]

        local = [pltpu.make_async_copy(w[t], o[t].at[me], loc.at[t]) for t in range(n)]
        local.append(pltpu.make_async_copy(sm, osm.at[me], loc.at[n]))
        for cp in local:
            cp.start()
        first = []
        for j, chip in enumerate(chips):
            for t in range(n):
                rh = shards[t].shape[0] // 2
                first.append(_rcopy(w[t].at[pl.ds(c * rh, rh), :], half(o[t], me, c, t),
                                    send.at[j * n + t], recv.at[j * n + t], (*chip, c)))
            first.append(_rcopy(sm, osm.at[me], ssend.at[j], srecv.at[j], (*chip, c)))
        for cp in first:
            cp.start()
        passed = []
        for j, (cx, cy) in enumerate(chips):
            k = 2 * cx + cy
            for t in range(n):
                blk = half(o[t], k, c, t)
                _rcopy(blk, blk, send.at[j * n + t], recv.at[j * n + t], (cx, cy, c)).wait_recv()
                cp = _rcopy(blk, blk, fsend.at[j * n + t], frecv.at[j * n + t], sib)
                cp.start()
                passed.append(cp)
        for j, (cx, cy) in enumerate(chips):
            k = 2 * cx + cy
            for t in range(n):
                blk = half(o[t], k, 1 - c, t)
                _rcopy(blk, blk, fsend.at[j * n + t], frecv.at[j * n + t], sib).wait_recv()
            _rcopy(sm, osm.at[k], ssend.at[j], srecv.at[j], (cx, cy, c)).wait_recv()
        for cp in first + passed:
            cp.wait_send()
        for cp in local:
            cp.wait()

    out_shape = [jax.ShapeDtypeStruct((N_CHIPS,) + s.shape, s.dtype) for s in shards]
    out_shape.append(jax.ShapeDtypeStruct((N_CHIPS,) + small.shape, small.dtype))
    sems = pltpu.SemaphoreType.DMA
    return pl.pallas_call(
        body, name="all_gather_weights", out_shape=out_shape,
        in_specs=[ANY] * (n + 1), out_specs=[ANY] * (n + 1),
        scratch_shapes=[sems((3 * n,)), sems((3 * n,)), sems((3 * n,)), sems((3 * n,)),
                        sems((3,)), sems((3,)), sems((n + 1,))],
    )(*shards, small)


def sibling_exchange(grads):
    n = len(grads)

    def body(*refs):
        g = refs[:n]
        o = refs[n:2 * n]
        send, recv = refs[2 * n:]
        x, y, c, _ = _place()
        cps = [_rcopy(g[t].at[1 - c], o[t], send.at[t], recv.at[t], (x, y, 1 - c)) for t in range(n)]
        for cp in cps:
            cp.start()
        for cp in cps:
            cp.wait()

    sems = pltpu.SemaphoreType.DMA
    return pl.pallas_call(
        body, name="sibling_exchange", out_shape=[jax.ShapeDtypeStruct(a.shape[1:], a.dtype) for a in grads],
        in_specs=[ANY] * n, out_specs=[ANY] * n, scratch_shapes=[sems((n,)), sems((n,))],
    )(*grads)


def chip_exchange(parts):
    n = len(parts)

    def body(*refs):
        p = refs[:n]
        o = refs[n:2 * n]
        send, recv = refs[2 * n:]
        _, _, c, chips = _place()
        cps = []
        for j, (cx, cy) in enumerate(chips):
            for t in range(n):
                cps.append(_rcopy(p[t].at[2 * cx + cy], o[t].at[j], send.at[j * n + t], recv.at[j * n + t],
                                  (cx, cy, c)))
        for cp in cps:
            cp.start()
        for cp in cps:
            cp.wait()

    sems = pltpu.SemaphoreType.DMA
    return pl.pallas_call(
        body, name="chip_exchange",
        out_shape=[jax.ShapeDtypeStruct((3,) + a.shape[1:], a.dtype) for a in parts],
        in_specs=[ANY] * n, out_specs=[ANY] * n, scratch_shapes=[sems((3 * n,)), sems((3 * n,))],
    )(*parts)


def sibling_assemble(halves, split_rows):
    n = len(halves)

    def body(*refs):
        h = refs[:n]
        o = refs[n:2 * n]
        send, recv, loc = refs[2 * n:]
        x, y, c, _ = _place()

        def dst(t, cc):
            r, w = halves[t].shape
            return o[t].at[pl.ds(cc * r, r), :] if split_rows[t] else o[t].at[:, pl.ds(cc * w, w)]

        local = [pltpu.make_async_copy(h[t], dst(t, c), loc.at[t]) for t in range(n)]
        cps = [_rcopy(h[t], dst(t, c), send.at[t], recv.at[t], (x, y, 1 - c)) for t in range(n)]
        for cp in local + cps:
            cp.start()
        for t in range(n):
            cps[t].wait_send()
            _rcopy(h[t], dst(t, 1 - c), send.at[t], recv.at[t], (x, y, 1 - c)).wait_recv()
        for cp in local:
            cp.wait()

    def full(t):
        r, w = halves[t].shape
        return jax.ShapeDtypeStruct((2 * r, w) if split_rows[t] else (r, 2 * w), halves[t].dtype)

    sems = pltpu.SemaphoreType.DMA
    return pl.pallas_call(
        body, name="sibling_assemble", out_shape=[full(t) for t in range(n)],
        in_specs=[ANY] * n, out_specs=[ANY] * n, scratch_shapes=[sems((n,)), sems((n,)), sems((n,))],
    )(*halves)


def small_all_reduce(ddw, v512, v1024):
    rows, width = PACK_ROWS, 512
    n512, n1024 = len(VEC512), len(VEC1024)

    def body(*refs):
        ddw_ref = refs[0]
        a_refs = refs[1:1 + n512]
        b_refs = refs[1 + n512:1 + n512 + n1024]
        o_ref, p_ref, gath_ref, send, recv = refs[1 + n512 + n1024:]
        p_ref[...] = jnp.zeros_like(p_ref)
        p_ref[0:32, :] = ddw_ref[...]
        for i, r in enumerate(a_refs):
            p_ref[32 + i:33 + i, :] = r[...]
        for i, r in enumerate(b_refs):
            base = 32 + n512 + 2 * i
            p_ref[base:base + 1, :] = r[:, 0:512]
            p_ref[base + 1:base + 2, :] = r[:, 512:1024]
        x, y, c, _ = _place()
        me = 4 * x + 2 * y + c
        gath_ref[me] = p_ref[...]
        cps = []
        for k in range(1, 8):
            dx, dy, dc = (k >> 2) & 1, (k >> 1) & 1, k & 1
            px = 1 - x if dx else x
            py = 1 - y if dy else y
            pc = 1 - c if dc else c
            cps.append(_rcopy(p_ref, gath_ref.at[me], send.at[k - 1], recv.at[k - 1], (px, py, pc)))
        for cp in cps:
            cp.start()
        for k in range(1, 8):
            dx, dy, dc = (k >> 2) & 1, (k >> 1) & 1, k & 1
            px = 1 - x if dx else x
            py = 1 - y if dy else y
            pc = 1 - c if dc else c
            _rcopy(p_ref, gath_ref.at[4 * px + 2 * py + pc], send.at[k - 1], recv.at[k - 1], (px, py, pc)).wait_recv()
        for cp in cps:
            cp.wait_send()
        total = gath_ref[0]
        for d in range(1, 8):
            total = total + gath_ref[d]
        o_ref[...] = total

    sems = pltpu.SemaphoreType.DMA
    n_in = 1 + n512 + n1024
    return pl.pallas_call(
        body, name="small_all_reduce", out_shape=jax.ShapeDtypeStruct((rows, width), F32),
        in_specs=[VMEM_SPEC] * n_in, out_specs=VMEM_SPEC,
        scratch_shapes=[pltpu.VMEM((rows, width), F32), pltpu.VMEM((8, rows, width), F32), sems((7,)), sems((7,))],
    )(ddw, *[v512[n] for n in VEC512], *[v1024[n] for n in VEC1024])


def _row_block(r):
    for tr in (512, 352, 256, 128):
        if r % tr == 0:
            return tr
    return r


def add_halves(g, recv, name):
    _, _, r, w = g.shape
    tr = _row_block(r)

    def body(g0_ref, g1_ref, r_ref, ob_ref, own_ref):
        k = pl.program_id(1)
        c = lax.axis_index("c")
        me = 2 * lax.axis_index("x") + lax.axis_index("y")
        t = jnp.where(c == 0, g0_ref[0, 0], g1_ref[0, 0]) + r_ref[0]
        ob_ref[0] = t.astype(MM)
        mine = jnp.where(k == me, t, 0.0)

        @pl.when(k == 0)
        def _():
            own_ref[...] = mine

        @pl.when(k != 0)
        def _():
            own_ref[...] += mine

    return pl.pallas_call(
        body, name=name, grid=(r // tr, N_CHIPS),
        in_specs=[pl.BlockSpec((1, 1, tr, w), lambda i, k: (0, k, i, 0)),
                  pl.BlockSpec((1, 1, tr, w), lambda i, k: (1, k, i, 0)),
                  pl.BlockSpec((1, tr, w), lambda i, k: (k, i, 0))],
        out_specs=[pl.BlockSpec((1, tr, w), lambda i, k: (k, i, 0)),
                   pl.BlockSpec((tr, w), lambda i, k: (i, 0))],
        out_shape=(jax.ShapeDtypeStruct((N_CHIPS, r, w), MM), jax.ShapeDtypeStruct((r, w), F32)),
        compiler_params=_params(("arbitrary", "arbitrary")),
    )(g, g, recv)


def sum_parts(own, rin, name):
    _, r, w = rin.shape
    tr = _row_block(r)

    def body(o_ref, r_ref, out_ref):
        out_ref[...] = ((o_ref[...] + r_ref[0].astype(F32)) + r_ref[1].astype(F32)) + r_ref[2].astype(F32)

    return pl.pallas_call(
        body, name=name, grid=(r // tr,), out_shape=jax.ShapeDtypeStruct((r, w), F32),
        in_specs=[pl.BlockSpec((tr, w), lambda i: (i, 0)), pl.BlockSpec((3, tr, w), lambda i: (0, i, 0))],
        out_specs=pl.BlockSpec((tr, w), lambda i: (i, 0)),
        compiler_params=_params(("arbitrary",)),
    )(own, rin)


def _adamw_math(w, g, m, v):
    mn = ADAM_B1 * m + (1.0 - ADAM_B1) * g
    vn = ADAM_B2 * v + (1.0 - ADAM_B2) * (g * g)
    m_hat = mn / (1.0 - ADAM_B1 ** ADAM_STEP)
    v_hat = vn / (1.0 - ADAM_B2 ** ADAM_STEP)
    return -ADAM_LR * (m_hat / (jnp.sqrt(v_hat) + ADAM_EPS) + ADAM_WD * w), mn, vn


def adamw(w, g, m, v, name):
    r, c = w.shape
    tr = _row_block(r)
    if c >= 1024 and tr % 512 == 0:
        tr = 256

    def body(w_ref, g_ref, m_ref, v_ref, go_ref, d_ref, mo_ref, vo_ref):
        gv = g_ref[...]
        go_ref[...] = gv
        d_ref[...], mo_ref[...], vo_ref[...] = _adamw_math(w_ref[...], gv, m_ref[...], v_ref[...])

    spec = pl.BlockSpec((tr, c), lambda i: (i, 0))
    out = jax.ShapeDtypeStruct((r, c), F32)
    return pl.pallas_call(
        body, name=name, grid=(r // tr,), out_shape=(out, out, out, out),
        in_specs=[spec] * 4, out_specs=[spec] * 4, compiler_params=_params(("arbitrary",)),
    )(w, g, m, v)


def adamw_small(gsum, params):
    names = list(params)
    flat = [a for n in names for a in params[n]]

    def body(*refs):
        g_ref = refs[0]
        ins = refs[1:1 + 3 * len(names)]
        outs = refs[1 + 3 * len(names):]
        me = 2 * lax.axis_index("x") + lax.axis_index("y")
        for i, n in enumerate(names):
            w_ref, m_ref, v_ref = ins[3 * i:3 * i + 3]
            go_ref, d_ref, mo_ref, vo_ref = outs[4 * i:4 * i + 4]
            if n == "conv_dw_w":
                gv = jnp.zeros((CONV_WIDTH, 128), F32)
                for k in range(N_CHIPS):
                    gv = gv + jnp.where(me == k, g_ref[0:CONV_WIDTH, 128 * k:128 * (k + 1)], 0.0)
            elif n in VEC512:
                r0 = 32 + VEC512.index(n)
                gv = g_ref[r0:r0 + 1, :]
            else:
                r0 = 32 + len(VEC512) + 2 * VEC1024.index(n)
                gv = jnp.concatenate([g_ref[r0:r0 + 1, :], g_ref[r0 + 1:r0 + 2, :]], axis=1)
            go_ref[...] = gv
            d_ref[...], mo_ref[...], vo_ref[...] = _adamw_math(w_ref[...], gv, m_ref[...], v_ref[...])

    out_shape = [jax.ShapeDtypeStruct(params[n][0].shape, F32) for n in names for _ in range(4)]
    res = pl.pallas_call(
        body, name="adamw_small", out_shape=out_shape,
        in_specs=[VMEM_SPEC] * (1 + len(flat)), out_specs=[VMEM_SPEC] * len(out_shape),
        compiler_params=_params(),
    )(gsum, *flat)
    return {n: res[4 * i:4 * i + 4] for i, n in enumerate(names)}


BIG = ("w_in", "w_ffn_up", "w_ffn_down", "w_out", "w_conv_branch", "w_att_branch")
SPLIT_ROWS = dict(w_in=True, w_ffn_up=True, w_ffn_down=False, w_out=False, w_conv_branch=True, w_att_branch=True)
VEC512 = ("conv_dw_b", "conv_ln_g", "conv_ln_b")
VEC1024 = ("norm_mix_pre", "b_conv_branch", "norm_mix_post", "norm_ffn_pre", "norm_ffn_post")
PACK_ROWS = 48


def local_step(xs, tgt, weights, wg, w_dw_full):
    row = lambda a: a.reshape(1, -1)
    g1, g2, g3, g4 = (row(weights[n]) for n in ("norm_mix_pre", "norm_mix_post", "norm_ffn_pre", "norm_ffn_post"))
    ln_g, ln_b = row(weights["conv_ln_g"]), row(weights["conv_ln_b"])
    w_out_g = wg["w_out"].reshape(D_MODEL, D_MODEL)
    w_down_g = wg["w_ffn_down"].reshape(D_FF, D_MODEL)
    w_cb_g, w_ab_g = wg["w_conv_branch"], wg["w_att_branch"]

    h1, ci, q, k, v, gc, ga = in_proj_fwd(xs, g1, wg["w_in"])
    u1, u3 = conv_fwd(ci, w_dw_full, row(weights["conv_dw_b"]), ln_g, ln_b)
    att, rc = attn_fwd(q, k, v)
    co, ao, merged, mix, x2, h2 = mix_fwd(u3, att, gc, ga, xs, w_cb_g, row(weights["b_conv_branch"]),
                                          w_ab_g, w_out_g, g2, g3)
    gate, up, act = ffn_up_fwd(h2, wg["w_ffn_up"])
    dff, dy, loss_parts, dg4 = ffn_down_loss(act, w_down_g, x2, tgt, g4)
    loss_local = jnp.sum(loss_parts[::8, 0])

    dgate, dup = ffn_act_bwd(dff, w_down_g, gate, up)
    dx2, dmix, dg3, dg2 = ffn_in_bwd(dgate, dup, wg["w_ffn_up"], x2, mix, dy, g3, g2)
    dco, dao, dgc, dga, du3, datt, dbcb = merge_bwd(dmix, w_out_g, gc, ga, co, ao, w_cb_g, w_ab_g)
    dci, ddw, dbdw, dlng, dlnb = conv_bwd(du3, u1, ci, w_dw_full, ln_g, ln_b)
    dq, dk, dv = attn_bwd(q, k, v, datt, rc)
    dproj = jnp.concatenate([dci, dq, dk, dv, dgc, dga], axis=1)
    grad_x, dg1 = in_proj_bwd(dproj, wg["w_in"], xs, dx2, g1)

    partial = dict(
        w_in=weight_grad(h1, dproj, "dw_in", True),
        w_ffn_up=weight_grad(h2, jnp.concatenate([dgate, dup], axis=1), "dw_ffn_up", True),
        w_ffn_down=weight_grad(act, dff, "dw_ffn_down", False, tk=UP_SHARD),
        w_out=weight_grad(merged, dmix, "dw_out", False, tk=512),
        w_conv_branch=weight_grad(u3, dco, "dw_conv_branch", True),
        w_att_branch=weight_grad(att, dao, "dw_att_branch", True),
    )
    v512 = dict(conv_dw_b=dbdw, conv_ln_g=dlng, conv_ln_b=dlnb)
    v1024 = dict(norm_mix_pre=dg1, b_conv_branch=dbcb, norm_mix_post=dg2, norm_ffn_pre=dg3, norm_ffn_post=dg4)
    return loss_local, grad_x, partial, ddw, v512, v1024


def kernel(x, norm_mix_pre, w_in, conv_dw_w, conv_dw_b, conv_ln_g, conv_ln_b, w_conv_branch, b_conv_branch, w_att_branch, w_out, norm_mix_post, norm_ffn_pre, w_ffn_up, w_ffn_down, norm_ffn_post, loss_target, m_norm_mix_pre, m_w_in, m_conv_dw_w, m_conv_dw_b, m_conv_ln_g, m_conv_ln_b, m_w_conv_branch, m_b_conv_branch, m_w_att_branch, m_w_out, m_norm_mix_post, m_norm_ffn_pre, m_w_ffn_up, m_w_ffn_down, m_norm_ffn_post, v_norm_mix_pre, v_w_in, v_conv_dw_w, v_conv_dw_b, v_conv_ln_g, v_conv_ln_b, v_w_conv_branch, v_b_conv_branch, v_w_att_branch, v_w_out, v_norm_mix_post, v_norm_ffn_pre, v_w_ffn_up, v_w_ffn_down, v_norm_ffn_post):
    weights = dict(norm_mix_pre=norm_mix_pre, w_in=w_in, conv_dw_w=conv_dw_w, conv_dw_b=conv_dw_b, conv_ln_g=conv_ln_g, conv_ln_b=conv_ln_b, w_conv_branch=w_conv_branch, b_conv_branch=b_conv_branch, w_att_branch=w_att_branch, w_out=w_out, norm_mix_post=norm_mix_post, norm_ffn_pre=norm_ffn_pre, w_ffn_up=w_ffn_up, w_ffn_down=w_ffn_down, norm_ffn_post=norm_ffn_post)
    mom = dict(norm_mix_pre=m_norm_mix_pre, w_in=m_w_in, conv_dw_w=m_conv_dw_w, conv_dw_b=m_conv_dw_b, conv_ln_g=m_conv_ln_g, conv_ln_b=m_conv_ln_b, w_conv_branch=m_w_conv_branch, b_conv_branch=m_b_conv_branch, w_att_branch=m_w_att_branch, w_out=m_w_out, norm_mix_post=m_norm_mix_post, norm_ffn_pre=m_norm_ffn_pre, w_ffn_up=m_w_ffn_up, w_ffn_down=m_w_ffn_down, norm_ffn_post=m_norm_ffn_post)
    var = dict(norm_mix_pre=v_norm_mix_pre, w_in=v_w_in, conv_dw_w=v_conv_dw_w, conv_dw_b=v_conv_dw_b, conv_ln_g=v_conv_ln_g, conv_ln_b=v_conv_ln_b, w_conv_branch=v_w_conv_branch, b_conv_branch=v_b_conv_branch, w_att_branch=v_w_att_branch, w_out=v_w_out, norm_mix_post=v_norm_mix_post, norm_ffn_pre=v_norm_ffn_pre, w_ffn_up=v_w_ffn_up, w_ffn_down=v_w_ffn_down, norm_ffn_post=v_norm_ffn_post)
    order = list(weights)

    gathered = all_gather_weights([weights[n].astype(MM) for n in BIG], conv_dw_w)
    wg = dict(zip(BIG, gathered[:-1]))
    w_dw_full = jnp.concatenate([gathered[-1][k] for k in range(N_CHIPS)], axis=1)

    loss_local, grad_x, partial, ddw, v512, v1024 = local_step(
        x.reshape(SEQ, D_MODEL), loss_target.reshape(SEQ, D_MODEL), weights, wg, w_dw_full)
    loss = lax.psum(loss_local, ("x", "y", "c"))

    from_sib = sibling_exchange([partial[n] for n in BIG])
    summed = [add_halves(partial[n], r, "add_" + n) for n, r in zip(BIG, from_sib)]
    from_chips = chip_exchange([s[0] for s in summed])
    halves = [sum_parts(s[1], r, "sum_" + n) for n, s, r in zip(BIG, summed, from_chips)]
    full = sibling_assemble(halves, [SPLIT_ROWS[n] for n in BIG])

    grads, deltas, new_m, new_v = {}, {}, {}, {}
    for n, g in zip(BIG, full):
        grads[n], deltas[n], new_m[n], new_v[n] = adamw(weights[n], g, mom[n], var[n], "adamw_" + n)

    gsum = small_all_reduce(ddw, v512, v1024)
    as_rows = lambda n, a: a if n == "conv_dw_w" else a.reshape(1, -1)
    small_names = ("conv_dw_w",) + VEC512 + VEC1024
    small = adamw_small(gsum, {n: tuple(as_rows(n, d[n]) for d in (weights, mom, var)) for n in small_names})
    for n in small_names:
        grads[n], deltas[n], new_m[n], new_v[n] = (a.reshape(weights[n].shape) for a in small[n])

    return (loss, grad_x.reshape(1, SEQ, D_MODEL), *[grads[n] for n in order], *[deltas[n] for n in order],
            *[new_m[n] for n in order], *[new_v[n] for n in order])
```

```python
import jax
import jax.numpy as jnp
from jax import lax
from jax.experimental import pallas as pl
from jax.experimental.pallas import tpu as pltpu

F32 = jnp.float32
MM = jnp.bfloat16

SEQ = 2048
D_MODEL = 1024
CONV_DIM = 512
ATT_DIM = 512
CONV_WIDTH = 31
D_FF = 2816
IN_COLS = 2 * CONV_DIM + 3 * ATT_DIM + 2 * D_MODEL
N_CHIPS = 4
IN_SHARD = IN_COLS // N_CHIPS
UP_SHARD = 2 * D_FF // N_CHIPS
BR_SHARD = D_MODEL // N_CHIPS
EPS = 1e-6
ATT_SCALE = 0.125

TM = 256
TQ = 128
CONV_TILE = 64
CONV_WIN = CONV_TILE + 32
VMEM_LIMIT = 56 * 1024 * 1024

ADAM_LR = 0.001
ADAM_B1 = 0.9
ADAM_B2 = 0.999
ADAM_EPS = 1e-08
ADAM_WD = 0.01
ADAM_STEP = 10

MESH = pl.DeviceIdType.MESH
ANY = pl.BlockSpec(memory_space=pl.ANY)
VMEM_SPEC = pl.BlockSpec(memory_space=pltpu.VMEM)

NT_DIMS = (((1,), (1,)), ((), ()))
TN_DIMS = (((0,), (0,)), ((), ()))

IN_PIECES = (("ci", 0, 1024), ("q", 1024, 1536), ("k", 1536, 2048), ("v", 2048, 2560),
             ("gc", 2560, 3584), ("ga", 3584, 4608))


def _params(sem=None, vmem=VMEM_LIMIT):
    return pltpu.CompilerParams(dimension_semantics=sem, vmem_limit_bytes=vmem)


def _dot(a, b):
    return jnp.dot(a, b, preferred_element_type=F32)


def _dot_nt(a, b):
    return lax.dot_general(a, b, NT_DIMS, preferred_element_type=F32)


def _dot_tn(a, b):
    return lax.dot_general(a, b, TN_DIMS, preferred_element_type=F32)


def _sigmoid(x):
    return 1.0 / (1.0 + jnp.exp(-x))


def _rms(x):
    r = lax.rsqrt(jnp.mean(x * x, axis=-1, keepdims=True) + EPS)
    return x * r, r


def _rms_bwd(dy_g, n, r):
    return r * (dy_g - n * jnp.mean(dy_g * n, axis=-1, keepdims=True))


def _row_tile_spec(width, tm=TM):
    return pl.BlockSpec((tm, width), lambda i: (i, 0))


def _full_spec(shape):
    nd = len(shape)
    return pl.BlockSpec(shape, lambda *_: (0,) * nd)


def _acc_rows(ref, val, first):
    @pl.when(first)
    def _():
        ref[...] = val

    @pl.when(jnp.logical_not(first))
    def _():
        ref[...] += val


def in_proj_fwd(x, g1, w_in_g):
    def body(x_ref, g_ref, w_ref, h_ref, ci_ref, q_ref, k_ref, v_ref, gc_ref, ga_ref):
        n, _ = _rms(x_ref[...])
        h = (n * g_ref[...]).astype(MM)
        h_ref[...] = h
        outs = dict(ci=ci_ref, q=q_ref, k=k_ref, v=v_ref, gc=gc_ref, ga=ga_ref)
        for j in range(N_CHIPS):
            p = _dot(h, w_ref[j])
            g0 = j * IN_SHARD
            for name, s, e in IN_PIECES:
                lo, hi = max(s, g0), min(e, g0 + IN_SHARD)
                if lo < hi:
                    ref = outs[name]
                    ref[:, lo - s:hi - s] = p[:, lo - g0:hi - g0].astype(ref.dtype)

    out_shape = (
        jax.ShapeDtypeStruct((SEQ, D_MODEL), MM),
        jax.ShapeDtypeStruct((SEQ, 2 * CONV_DIM), F32),
        jax.ShapeDtypeStruct((SEQ, ATT_DIM), MM),
        jax.ShapeDtypeStruct((SEQ, ATT_DIM), MM),
        jax.ShapeDtypeStruct((SEQ, ATT_DIM), MM),
        jax.ShapeDtypeStruct((SEQ, D_MODEL), F32),
        jax.ShapeDtypeStruct((SEQ, D_MODEL), F32),
    )
    return pl.pallas_call(
        body, name="in_proj_fwd", grid=(SEQ // TM,), out_shape=out_shape,
        in_specs=[_row_tile_spec(D_MODEL), _full_spec((1, D_MODEL)), _full_spec(w_in_g.shape)],
        out_specs=[_row_tile_spec(s.shape[1]) for s in out_shape],
        compiler_params=_params(("arbitrary",)),
    )(x, g1, w_in_g)


def _shifted_sum(win, terms):
    by_rot = {}
    for m, coef in terms:
        by_rot.setdefault(m % 8, []).append((m // 8, coef))
    acc = None
    n = win.shape[0]
    for rot in sorted(by_rot):
        shifted = win if rot == 0 else pltpu.roll(win, n - rot, 0)
        for a, coef in by_rot[rot]:
            t = coef * shifted[8 * a:8 * a + CONV_TILE, :]
            acc = t if acc is None else acc + t
    return acc


def _glu_into(ci_ref, upad_ref):
    upad_ref[0:32, :] = jnp.zeros((32, CONV_DIM), F32)

    def step(i, c):
        t0 = pl.multiple_of(i * TM, TM)
        a = ci_ref[pl.ds(t0, TM), 0:CONV_DIM]
        b = ci_ref[pl.ds(t0, TM), CONV_DIM:2 * CONV_DIM]
        upad_ref[pl.ds(t0 + 32, TM), :] = a * _sigmoid(b)
        return c

    lax.fori_loop(0, SEQ // TM, step, 0)


def _layernorm_parts(u1):
    mu = jnp.mean(u1, axis=-1, keepdims=True)
    xc = u1 - mu
    rstd = lax.rsqrt(jnp.mean(xc * xc, axis=-1, keepdims=True) + EPS)
    return xc * rstd, rstd


def conv_fwd(ci, w_dw, b_dw, ln_g, ln_b):
    def body(ci_ref, w_ref, b_ref, g_ref, bb_ref, u1_ref, u3_ref, upad_ref):
        _glu_into(ci_ref, upad_ref)

        def step(i, c):
            t0 = pl.multiple_of(i * CONV_TILE, CONV_TILE)
            win = upad_ref[pl.ds(t0, CONV_WIN), :]
            u1 = _shifted_sum(win, [(j + 2, w_ref[j:j + 1, :]) for j in range(CONV_WIDTH)]) + b_ref[...]
            u1_ref[pl.ds(t0, CONV_TILE), :] = u1
            xh, _ = _layernorm_parts(u1)
            u2 = xh * g_ref[...] + bb_ref[...]
            u3_ref[pl.ds(t0, CONV_TILE), :] = (u2 * _sigmoid(u2)).astype(MM)
            return c

        lax.fori_loop(0, SEQ // CONV_TILE, step, 0)

    return pl.pallas_call(
        body, name="conv_fwd",
        out_shape=(jax.ShapeDtypeStruct((SEQ, CONV_DIM), F32), jax.ShapeDtypeStruct((SEQ, CONV_DIM), MM)),
        in_specs=[VMEM_SPEC] * 5, out_specs=[VMEM_SPEC] * 2,
        scratch_shapes=[pltpu.VMEM((SEQ + 32, CONV_DIM), F32)],
        compiler_params=_params(),
    )(ci, w_dw, b_dw, ln_g, ln_b)


def _softplus(z):
    return jnp.maximum(z, 0.0) + jnp.log(1.0 + jnp.exp(-jnp.abs(z)))


def _cumsum_weights(suffix, with_total):
    n = 256 if with_total else 128
    r = lax.broadcasted_iota(jnp.int32, (256, n), 0) & 127
    c = lax.broadcasted_iota(jnp.int32, (256, n), 1)
    tri = (r >= c) if suffix else (r <= c)
    return jnp.logical_or(tri, c >= 128).astype(MM)


def _hi_lo(x):
    hi = x.astype(MM)
    lo = (x - hi.astype(F32)).astype(MM)
    return jnp.concatenate([hi, lo], axis=1)


def _cumsum_dot(x, w):
    r = _dot(_hi_lo(x), w)
    return (r[:, :128], r[:, 128:]) if w.shape[1] == 256 else (r, None)


NO_SCORE = -1e30


def _head_masks():
    lane = lax.broadcasted_iota(jnp.int32, (TQ, 128), 1)
    row = lax.broadcasted_iota(jnp.int32, (TQ, 128), 0)
    return lane, row, lane < 64


def _pick_head(x, head0, h):
    zero = jnp.zeros_like(x)
    return jnp.where(head0, x, zero) if h == 0 else jnp.where(head0, zero, x)


N_PAIRS = ATT_DIM // 128


def attn_fwd(q, k, v, gather=()):
    ng = len(gather)
    nqb = SEQ // TQ

    def body(*refs):
        q_ref, k_ref, v_ref = refs[:3]
        o_ref, rc_ref = refs[3 + ng:5 + ng]
        acc_ref, r_ref, z_ref, spb_ref, ab_ref = refs[5 + 2 * ng:10 + 2 * ng]
        i = pl.program_id(0)
        if ng:
            ag = _Gather([s.shape for s in gather], refs[3:3 + ng], refs[5 + ng:5 + 2 * ng], refs[10 + 2 * ng:])
            pl.when(i == 0)(ag.start)
            pl.when(i == nqb - 1)(ag.forward)
        lane, row, head0 = _head_masks()
        w = _cumsum_weights(suffix=True, with_total=True)
        acc_ref[...] = jnp.zeros_like(acc_ref)
        r_ref[...] = jnp.zeros_like(r_ref)
        rc_ref[...] = jnp.zeros_like(rc_ref)
        z_ref[...] = jnp.full(z_ref.shape, NO_SCORE, F32)
        spb_ref[...] = jnp.zeros_like(spb_ref)
        ab_ref[...] = jnp.zeros_like(ab_ref)

        def step(n, c):
            b1 = i - n
            b2 = b1 + 1
            k1 = pl.multiple_of(jnp.maximum(b1, 0) * TQ, TQ)
            k3 = pl.multiple_of(jnp.clip(b1 + 2, 0, i) * TQ, TQ)
            keep1 = jnp.logical_and(b1 >= 0, jnp.logical_or(b1 < i, lane < row))
            mark2 = jnp.logical_and(lane == b2, b2 <= i)
            for p in range(N_PAIRS):
                cols = slice(128 * p, 128 * (p + 1))
                qb = q_ref[:, cols]
                kb = k_ref[pl.ds(k1, TQ), cols]
                vb = v_ref[pl.ds(k3, TQ), cols]
                for h in range(2):
                    hh = 2 * p + h
                    acc_ref[:, cols] += _dot(ab_ref[hh], _pick_head(vb, head0, h))
                    r = _dot(spb_ref[hh], w)
                    r_in = r_ref[hh]
                    ab_ref[hh] = jnp.exp(z_ref[hh] - (r[:, :128] + r_in)).astype(MM)
                    rc_ref[hh] = jnp.where(mark2, r_in, rc_ref[hh])
                    r_ref[hh] = r_in + r[:, 128:]
                    z = _dot_nt(_pick_head(qb, head0, h), kb) * ATT_SCALE
                    sp = jnp.where(keep1, _softplus(z), 0.0)
                    z_ref[hh] = jnp.where(keep1, z, NO_SCORE)
                    spb_ref[hh] = _hi_lo(sp)
            return c

        lax.fori_loop(0, i + 3, step, 0)
        o_ref[...] = acc_ref[...].astype(MM)
        if ng:
            pl.when(i == nqb - 1)(ag.finish)

    full = pl.BlockSpec((SEQ, ATT_DIM), lambda i: (0, 0))
    out_shape = [jax.ShapeDtypeStruct((SEQ, ATT_DIM), MM), jax.ShapeDtypeStruct((8, SEQ, 128), F32)]
    out_shape += [jax.ShapeDtypeStruct((N_CHIPS,) + s.shape, s.dtype) for s in gather]
    return pl.pallas_call(
        body, name="attn_fwd", grid=(nqb,), out_shape=out_shape,
        in_specs=[_row_tile_spec(ATT_DIM, TQ), full, full] + [ANY] * ng,
        out_specs=[_row_tile_spec(ATT_DIM, TQ), pl.BlockSpec((8, TQ, 128), lambda i: (0, i, 0))] + [ANY] * ng,
        scratch_shapes=[pltpu.VMEM((TQ, ATT_DIM), F32), pltpu.VMEM((8, TQ, 128), F32),
                        pltpu.VMEM((8, TQ, 128), F32), pltpu.VMEM((8, TQ, 256), MM), pltpu.VMEM((8, TQ, 128), MM)]
                       + (_Gather.scratch(ng) if ng else []),
        compiler_params=_params(("arbitrary",)),
    )(q, k, v, *gather)


def mix_fwd(u3, att, gc, ga, x, w_cb_g, b_cb, w_ab_g, w_out_g, g2, g3):
    def body(u_ref, a_ref, gc_ref, ga_ref, x_ref, wcb_ref, bcb_ref, wab_ref, wout_ref, g2_ref, g3_ref,
             co_ref, ao_ref, mg_ref, mix_ref, x2_ref, h2_ref):
        u = u_ref[...]
        a = a_ref[...]
        for j in range(N_CHIPS):
            cols = slice(j * BR_SHARD, (j + 1) * BR_SHARD)
            co_ref[:, cols] = _dot(u, wcb_ref[j]) + bcb_ref[:, cols]
            ao_ref[:, cols] = _dot(a, wab_ref[j])
        merged = (_sigmoid(gc_ref[...]) * co_ref[...] + _sigmoid(ga_ref[...]) * ao_ref[...]).astype(MM)
        mg_ref[...] = merged
        mix = _dot(merged, wout_ref[...])
        mix_ref[...] = mix
        n2, _ = _rms(mix)
        x2 = x_ref[...] + n2 * g2_ref[...]
        x2_ref[...] = x2
        n3, _ = _rms(x2)
        h2_ref[...] = (n3 * g3_ref[...]).astype(MM)

    out_shape = (
        jax.ShapeDtypeStruct((SEQ, D_MODEL), F32), jax.ShapeDtypeStruct((SEQ, D_MODEL), F32),
        jax.ShapeDtypeStruct((SEQ, D_MODEL), MM), jax.ShapeDtypeStruct((SEQ, D_MODEL), F32),
        jax.ShapeDtypeStruct((SEQ, D_MODEL), F32), jax.ShapeDtypeStruct((SEQ, D_MODEL), MM),
    )
    vec = _full_spec((1, D_MODEL))
    return pl.pallas_call(
        body, name="mix_fwd", grid=(SEQ // TM,), out_shape=out_shape,
        in_specs=[_row_tile_spec(CONV_DIM), _row_tile_spec(ATT_DIM), _row_tile_spec(D_MODEL),
                  _row_tile_spec(D_MODEL), _row_tile_spec(D_MODEL), _full_spec(w_cb_g.shape), vec,
                  _full_spec(w_ab_g.shape), _full_spec(w_out_g.shape), vec, vec],
        out_specs=[_row_tile_spec(D_MODEL)] * 6,
        compiler_params=_params(("arbitrary",)),
    )(u3, att, gc, ga, x, w_cb_g, b_cb, w_ab_g, w_out_g, g2, g3)


def ffn_up_fwd(h2, w_up_g):
    def body(h_ref, wg_ref, wu_ref, gate_ref, up_ref, act_ref):
        h = h_ref[...]
        gate = _dot(h, wg_ref[0])
        up = _dot(h, wu_ref[0])
        gate_ref[...] = gate
        up_ref[...] = up
        act_ref[...] = (gate * _sigmoid(gate) * up).astype(MM)

    tile = pl.BlockSpec((TM, UP_SHARD), lambda n, i: (i, n))
    return pl.pallas_call(
        body, name="ffn_up_fwd", grid=(2, SEQ // TM),
        out_shape=(jax.ShapeDtypeStruct((SEQ, D_FF), F32), jax.ShapeDtypeStruct((SEQ, D_FF), F32),
                   jax.ShapeDtypeStruct((SEQ, D_FF), MM)),
        in_specs=[pl.BlockSpec((TM, D_MODEL), lambda n, i: (i, 0)),
                  pl.BlockSpec((1, D_MODEL, UP_SHARD), lambda n, i: (n, 0, 0)),
                  pl.BlockSpec((1, D_MODEL, UP_SHARD), lambda n, i: (n + 2, 0, 0))],
        out_specs=[tile, tile, tile],
        compiler_params=_params(("arbitrary", "arbitrary")),
    )(h2, w_up_g, w_up_g)


def ffn_down_loss(act, w_down_g, x2, target, g4):
    def body(act_ref, wd_ref, x2_ref, t_ref, g_ref, dff_ref, dy_ref, loss_ref, dg_ref):
        ff = _dot(act_ref[...], wd_ref[...])
        n4, r4 = _rms(ff)
        g4v = g_ref[...]
        err = x2_ref[...] + n4 * g4v - t_ref[...]
        row_loss = jnp.mean(err * err, axis=-1, keepdims=True)
        loss_ref[...] = jnp.zeros((8, 128), F32) + 0.5 * jnp.sum(row_loss, axis=0, keepdims=True)
        dy = err * (1.0 / D_MODEL)
        dy_ref[...] = dy
        dff_ref[...] = _rms_bwd(dy * g4v, n4, r4).astype(MM)
        _acc_rows(dg_ref, jnp.sum(dy * n4, axis=0, keepdims=True), pl.program_id(0) == 0)

    nt = SEQ // TM
    vec = _full_spec((1, D_MODEL))
    return pl.pallas_call(
        body, name="ffn_down_loss", grid=(nt,),
        out_shape=(jax.ShapeDtypeStruct((SEQ, D_MODEL), MM), jax.ShapeDtypeStruct((SEQ, D_MODEL), F32),
                   jax.ShapeDtypeStruct((nt * 8, 128), F32), jax.ShapeDtypeStruct((1, D_MODEL), F32)),
        in_specs=[_row_tile_spec(D_FF), _full_spec(w_down_g.shape), _row_tile_spec(D_MODEL),
                  _row_tile_spec(D_MODEL), vec],
        out_specs=[_row_tile_spec(D_MODEL), _row_tile_spec(D_MODEL),
                   pl.BlockSpec((8, 128), lambda i: (i, 0)), vec],
        compiler_params=_params(("arbitrary",)),
    )(act, w_down_g, x2, target, g4)


def ffn_act_bwd(dff, w_down_g, gate, up):
    def body(dff_ref, wd_ref, gate_ref, up_ref, dgate_ref, dup_ref):
        dact = _dot_nt(dff_ref[...], wd_ref[...])
        gate = gate_ref[...]
        sg = _sigmoid(gate)
        dgate_ref[...] = (dact * up_ref[...] * (sg * (1.0 + gate * (1.0 - sg)))).astype(MM)
        dup_ref[...] = (dact * (gate * sg)).astype(MM)

    return pl.pallas_call(
        body, name="ffn_act_bwd", grid=(SEQ // TM,),
        out_shape=(jax.ShapeDtypeStruct((SEQ, D_FF), MM), jax.ShapeDtypeStruct((SEQ, D_FF), MM)),
        in_specs=[_row_tile_spec(D_MODEL), _full_spec(w_down_g.shape), _row_tile_spec(D_FF), _row_tile_spec(D_FF)],
        out_specs=[_row_tile_spec(D_FF), _row_tile_spec(D_FF)],
        compiler_params=_params(("arbitrary",)),
    )(dff, w_down_g, gate, up)


def ffn_in_bwd(dgate, dup, w_up_g, x2, mix, dy, g3, g2):
    def body(dg_ref, du_ref, w_ref, x2_ref, mix_ref, dy_ref, g3_ref, g2_ref, dx2_ref, dmix_ref, dg3_ref, dg2_ref):
        dh2 = None
        for j in range(2):
            cols = slice(j * UP_SHARD, (j + 1) * UP_SHARD)
            t = _dot_nt(dg_ref[:, cols], w_ref[j]) + _dot_nt(du_ref[:, cols], w_ref[j + 2])
            dh2 = t if dh2 is None else dh2 + t
        first = pl.program_id(0) == 0
        n3, r3 = _rms(x2_ref[...])
        dx2 = dy_ref[...] + _rms_bwd(dh2 * g3_ref[...], n3, r3)
        dx2_ref[...] = dx2
        _acc_rows(dg3_ref, jnp.sum(dh2 * n3, axis=0, keepdims=True), first)
        n2, r2 = _rms(mix_ref[...])
        dmix_ref[...] = _rms_bwd(dx2 * g2_ref[...], n2, r2).astype(MM)
        _acc_rows(dg2_ref, jnp.sum(dx2 * n2, axis=0, keepdims=True), first)

    vec = _full_spec((1, D_MODEL))
    return pl.pallas_call(
        body, name="ffn_in_bwd", grid=(SEQ // TM,),
        out_shape=(jax.ShapeDtypeStruct((SEQ, D_MODEL), F32), jax.ShapeDtypeStruct((SEQ, D_MODEL), MM),
                   jax.ShapeDtypeStruct((1, D_MODEL), F32), jax.ShapeDtypeStruct((1, D_MODEL), F32)),
        in_specs=[_row_tile_spec(D_FF), _row_tile_spec(D_FF), _full_spec(w_up_g.shape), _row_tile_spec(D_MODEL),
                  _row_tile_spec(D_MODEL), _row_tile_spec(D_MODEL), vec, vec],
        out_specs=[_row_tile_spec(D_MODEL), _row_tile_spec(D_MODEL), vec, vec],
        compiler_params=_params(("arbitrary",)),
    )(dgate, dup, w_up_g, x2, mix, dy, g3, g2)


def merge_bwd(dmix, w_out_g, gc, ga, co, ao, w_cb_g, w_ab_g):
    def body(dmix_ref, wout_ref, gc_ref, ga_ref, co_ref, ao_ref, wcb_ref, wab_ref,
             dco_ref, dao_ref, dgc_ref, dga_ref, du3_ref, datt_ref, dbcb_ref):
        dm = _dot_nt(dmix_ref[...], wout_ref[...])
        sgc = _sigmoid(gc_ref[...])
        sga = _sigmoid(ga_ref[...])
        dco = dm * sgc
        dao = dm * sga
        dgc_ref[...] = (dm * co_ref[...] * (sgc * (1.0 - sgc))).astype(MM)
        dga_ref[...] = (dm * ao_ref[...] * (sga * (1.0 - sga))).astype(MM)
        _acc_rows(dbcb_ref, jnp.sum(dco, axis=0, keepdims=True), pl.program_id(0) == 0)
        dco_ref[...] = dco.astype(MM)
        dao_ref[...] = dao.astype(MM)
        du3 = None
        datt = None
        for j in range(N_CHIPS):
            cols = slice(j * BR_SHARD, (j + 1) * BR_SHARD)
            t = _dot_nt(dco_ref[:, cols], wcb_ref[j])
            s = _dot_nt(dao_ref[:, cols], wab_ref[j])
            du3 = t if du3 is None else du3 + t
            datt = s if datt is None else datt + s
        du3_ref[...] = du3
        datt_ref[...] = datt.astype(MM)

    wide = _row_tile_spec(D_MODEL)
    return pl.pallas_call(
        body, name="merge_bwd", grid=(SEQ // TM,),
        out_shape=(jax.ShapeDtypeStruct((SEQ, D_MODEL), MM), jax.ShapeDtypeStruct((SEQ, D_MODEL), MM),
                   jax.ShapeDtypeStruct((SEQ, D_MODEL), MM), jax.ShapeDtypeStruct((SEQ, D_MODEL), MM),
                   jax.ShapeDtypeStruct((SEQ, CONV_DIM), F32), jax.ShapeDtypeStruct((SEQ, ATT_DIM), MM),
                   jax.ShapeDtypeStruct((1, D_MODEL), F32)),
        in_specs=[wide, _full_spec(w_out_g.shape), wide, wide, wide, wide,
                  _full_spec(w_cb_g.shape), _full_spec(w_ab_g.shape)],
        out_specs=[wide, wide, wide, wide, _row_tile_spec(CONV_DIM), _row_tile_spec(ATT_DIM),
                   _full_spec((1, D_MODEL))],
        compiler_params=_params(("arbitrary",)),
    )(dmix, w_out_g, gc, ga, co, ao, w_cb_g, w_ab_g)


def conv_bwd(du3, u1, ci, w_dw, ln_g, ln_b):
    def body(du3_ref, u1_ref, ci_ref, w_ref, g_ref, bb_ref,
             dci_ref, dw_ref, dbdw_ref, dg_ref, db_ref, upad_ref, dpad_ref, dwacc_ref, vacc_ref):
        _glu_into(ci_ref, upad_ref)
        dpad_ref[SEQ:SEQ + 32, :] = jnp.zeros((32, CONV_DIM), F32)
        dwacc_ref[...] = jnp.zeros_like(dwacc_ref)
        vacc_ref[...] = jnp.zeros_like(vacc_ref)

        def fold8(t):
            s = t[0:8, :]
            for r in range(1, CONV_TILE // 8):
                s = s + t[8 * r:8 * r + 8, :]
            return s

        def pass1(i, c):
            t0 = pl.multiple_of(i * CONV_TILE, CONV_TILE)
            xh, rstd = _layernorm_parts(u1_ref[pl.ds(t0, CONV_TILE), :])
            gv = g_ref[...]
            u2 = xh * gv + bb_ref[...]
            s2 = _sigmoid(u2)
            du2 = du3_ref[pl.ds(t0, CONV_TILE), :] * (s2 * (1.0 + u2 * (1.0 - s2)))
            wv = du2 * gv
            du1 = rstd * (wv - jnp.mean(wv, axis=-1, keepdims=True)
                          - xh * jnp.mean(wv * xh, axis=-1, keepdims=True))
            dpad_ref[pl.ds(t0, CONV_TILE), :] = du1
            vacc_ref[0] += fold8(du2 * xh)
            vacc_ref[1] += fold8(du2)
            vacc_ref[2] += fold8(du1)
            win = upad_ref[pl.ds(t0, CONV_WIN), :]
            n = win.shape[0]
            for rot in range(8):
                shifted = win if rot == 0 else pltpu.roll(win, n - rot, 0)
                for a in range(5):
                    j = 8 * a + rot - 2
                    if 0 <= j < CONV_WIDTH:
                        dwacc_ref[j] += fold8(du1 * shifted[8 * a:8 * a + CONV_TILE, :])
            return c

        lax.fori_loop(0, SEQ // CONV_TILE, pass1, 0)

        def pass2(i, c):
            t0 = pl.multiple_of(i * CONV_TILE, CONV_TILE)
            win = dpad_ref[pl.ds(t0, CONV_WIN), :]
            du0 = _shifted_sum(win, [(30 - j, w_ref[j:j + 1, :]) for j in range(CONV_WIDTH)])
            a = ci_ref[pl.ds(t0, CONV_TILE), 0:CONV_DIM]
            sb = _sigmoid(ci_ref[pl.ds(t0, CONV_TILE), CONV_DIM:2 * CONV_DIM])
            dci_ref[pl.ds(t0, CONV_TILE), 0:CONV_DIM] = (du0 * sb).astype(MM)
            dci_ref[pl.ds(t0, CONV_TILE), CONV_DIM:2 * CONV_DIM] = (du0 * a * (sb * (1.0 - sb))).astype(MM)
            return c

        lax.fori_loop(0, SEQ // CONV_TILE, pass2, 0)

        for j in range(CONV_WIDTH):
            dw_ref[j:j + 1, :] = jnp.sum(dwacc_ref[j], axis=0, keepdims=True)
        dw_ref[CONV_WIDTH:32, :] = jnp.zeros((32 - CONV_WIDTH, CONV_DIM), F32)
        dg_ref[...] = jnp.sum(vacc_ref[0], axis=0, keepdims=True)
        db_ref[...] = jnp.sum(vacc_ref[1], axis=0, keepdims=True)
        dbdw_ref[...] = jnp.sum(vacc_ref[2], axis=0, keepdims=True)

    vec = jax.ShapeDtypeStruct((1, CONV_DIM), F32)
    return pl.pallas_call(
        body, name="conv_bwd",
        out_shape=(jax.ShapeDtypeStruct((SEQ, 2 * CONV_DIM), MM), jax.ShapeDtypeStruct((32, CONV_DIM), F32),
                   vec, vec, vec),
        in_specs=[VMEM_SPEC] * 6, out_specs=[VMEM_SPEC] * 5,
        scratch_shapes=[pltpu.VMEM((SEQ + 32, CONV_DIM), F32), pltpu.VMEM((SEQ + 32, CONV_DIM), F32),
                        pltpu.VMEM((CONV_WIDTH, 8, CONV_DIM), F32), pltpu.VMEM((3, 8, CONV_DIM), F32)],
        compiler_params=_params(),
    )(du3, u1, ci, w_dw, ln_g, ln_b)


def attn_bwd(q, k, v, datt, rc, scatter=()):
    nqb = SEQ // TQ

    ns = len(scatter)

    def body(*refs):
        q_ref, k_ref, v_ref, do_ref, rc_ref = refs[:5]
        dq_ref, dk_ref, dv_ref = refs[5 + ns:8 + ns]
        (dqa_ref, dka_ref, dva_ref, pc_ref, z_ref, sig1_ref, sig2_ref, g_ref, spb_ref, gb_ref, ab_ref,
         dzb_ref) = refs[8 + 2 * ns:20 + 2 * ns]
        i = pl.program_id(0)
        if ns:
            sc = _Scatter(refs[5:5 + ns], refs[8 + ns:8 + 2 * ns], refs[20 + 2 * ns:])
            pl.when(i == 0)(sc.start)
        lane, row, head0 = _head_masks()

        @pl.when(i == 0)
        def _():
            dka_ref[...] = jnp.zeros_like(dka_ref)
            dva_ref[...] = jnp.zeros_like(dva_ref)

        dqa_ref[...] = jnp.zeros_like(dqa_ref)
        pc_ref[...] = jnp.zeros_like(pc_ref)
        z_ref[...] = jnp.full(z_ref.shape, NO_SCORE, F32)
        for ref in (sig1_ref, sig2_ref, spb_ref, ab_ref, g_ref, gb_ref, dzb_ref):
            ref[...] = jnp.zeros_like(ref)
        w_suffix = _cumsum_weights(suffix=True, with_total=False)
        w_prefix = _cumsum_weights(suffix=False, with_total=True)

        def step(n, c):
            ka, kb_, kc, kd = (pl.multiple_of(jnp.clip(n - s, 0, i) * TQ, TQ) for s in range(4))
            below = lane < row
            keep_a = jnp.logical_and(n <= i, jnp.logical_or(n < i, below))
            bc = n - 2
            keep_c = jnp.logical_and(jnp.logical_and(bc >= 0, bc <= i), jnp.logical_or(bc < i, below))
            pick_b = lane == n - 1
            for p in range(N_PAIRS):
                cols = slice(128 * p, 128 * (p + 1))
                qb = q_ref[:, cols]
                dob = do_ref[:, cols]
                k_a = k_ref[pl.ds(ka, TQ), cols]
                v_b = v_ref[pl.ds(kb_, TQ), cols]
                k_d = k_ref[pl.ds(kd, TQ), cols]
                for h in range(2):
                    hh = 2 * p + h
                    qm = _pick_head(qb, head0, h)
                    dom = _pick_head(dob, head0, h)
                    dzb = dzb_ref[hh]
                    dqa_ref[:, cols] += _dot(dzb, _pick_head(k_d, head0, h))
                    dka_ref[pl.ds(kd, TQ), cols] += _dot_tn(dzb, qm)
                    r = _dot(gb_ref[hh], w_prefix)
                    p_in = pc_ref[hh]
                    dz = (g_ref[hh] - sig2_ref[hh] * (r[:, :128] + p_in)) * ATT_SCALE
                    dzb_ref[hh] = jnp.where(keep_c, dz, 0.0).astype(MM)
                    pc_ref[hh] = p_in + r[:, 128:]
                    dva_ref[pl.ds(kc, TQ), cols] += _dot_tn(ab_ref[hh], dom)
                    r_in = jnp.sum(jnp.where(pick_b, rc_ref[hh], 0.0), axis=1, keepdims=True)
                    a = jnp.exp(z_ref[hh] - (_dot(spb_ref[hh], w_suffix) + r_in))
                    g = _dot_nt(dom, v_b) * a
                    ab_ref[hh] = a.astype(MM)
                    g_ref[hh] = g
                    gb_ref[hh] = _hi_lo(g)
                    sig2_ref[hh] = sig1_ref[hh]
                    z = _dot_nt(qm, k_a) * ATT_SCALE
                    sp = _softplus(z)
                    sig1_ref[hh] = jnp.exp(z - sp)
                    z_ref[hh] = jnp.where(keep_a, z, NO_SCORE)
                    spb_ref[hh] = _hi_lo(jnp.where(keep_a, sp, 0.0))
            return c

        lax.fori_loop(0, i + 4, step, 0)
        dq_ref[...] = dqa_ref[...].astype(MM)

        @pl.when(i == nqb - 1)
        def _():
            dk_ref[...] = dka_ref[...].astype(MM)
            dv_ref[...] = dva_ref[...].astype(MM)

        if ns:
            pl.when(i == nqb - 1)(sc.finish)

    tile = _row_tile_spec(ATT_DIM, TQ)
    full = pl.BlockSpec((SEQ, ATT_DIM), lambda i: (0, 0))
    out = jax.ShapeDtypeStruct((SEQ, ATT_DIM), MM)
    return pl.pallas_call(
        body, name="attn_bwd", grid=(nqb,), out_shape=[out, out, out] + _Scatter.out_shapes(scatter),
        in_specs=[tile, full, full, tile, pl.BlockSpec((8, TQ, 128), lambda i: (0, i, 0))] + [ANY] * ns,
        out_specs=[tile, full, full] + [ANY] * ns,
        scratch_shapes=[pltpu.VMEM((TQ, ATT_DIM), F32), pltpu.VMEM((SEQ, ATT_DIM), F32),
                        pltpu.VMEM((SEQ, ATT_DIM), F32)] + [pltpu.VMEM((8, TQ, 128), F32)] * 5
                       + [pltpu.VMEM((8, TQ, 256), MM)] * 2 + [pltpu.VMEM((8, TQ, 128), MM)] * 2
                       + (_Scatter.scratch(ns) if ns else []),
        compiler_params=_params(("arbitrary",)),
    )(q, k, v, datt, rc, *scatter)


def in_proj_bwd(dproj, w_in_g, x, dx2, g1, scatter=()):
    ns = len(scatter)
    nt = SEQ // TM

    def body(*refs):
        dp_ref, w_ref, x_ref, dx2_ref, g_ref = refs[:5]
        dx_ref, dg_ref = refs[5 + ns:7 + ns]
        if ns:
            sc = _Scatter(refs[5:5 + ns], refs[7 + ns:7 + 2 * ns], refs[7 + 2 * ns:])
            pl.when(pl.program_id(0) == 0)(sc.start)
        dh = None
        for j in range(N_CHIPS):
            t = _dot_nt(dp_ref[:, j * IN_SHARD:(j + 1) * IN_SHARD], w_ref[j])
            dh = t if dh is None else dh + t
        n1, r1 = _rms(x_ref[...])
        dx_ref[...] = dx2_ref[...] + _rms_bwd(dh * g_ref[...], n1, r1)
        _acc_rows(dg_ref, jnp.sum(dh * n1, axis=0, keepdims=True), pl.program_id(0) == 0)
        if ns:
            pl.when(pl.program_id(0) == nt - 1)(sc.finish)

    vec = _full_spec((1, D_MODEL))
    return pl.pallas_call(
        body, name="in_proj_bwd", grid=(nt,),
        out_shape=[jax.ShapeDtypeStruct((SEQ, D_MODEL), F32), jax.ShapeDtypeStruct((1, D_MODEL), F32)]
                  + _Scatter.out_shapes(scatter),
        in_specs=[_row_tile_spec(IN_COLS), _full_spec(w_in_g.shape), _row_tile_spec(D_MODEL),
                  _row_tile_spec(D_MODEL), vec] + [ANY] * ns,
        out_specs=[_row_tile_spec(D_MODEL), vec] + [ANY] * ns,
        scratch_shapes=_Scatter.scratch(ns) if ns else [],
        compiler_params=_params(("arbitrary",)),
    )(dproj, w_in_g, x, dx2, g1, *scatter)


def weight_grad(a, b, name, col_sharded, tk=None):
    kin, n = a.shape[1], b.shape[1]

    def body(a_ref, b_ref, o_ref):
        if col_sharded:
            o_ref[0, 0] = _dot_tn(a_ref[...], b_ref[...])
        else:
            o_ref[...] = _dot_tn(a_ref[...], b_ref[...])

    if col_sharded:
        kh, ns = kin // 2, n // N_CHIPS
        out = jax.ShapeDtypeStruct((N_CHIPS, 2, kh, ns), F32)
        grid = (2, N_CHIPS)
        in_specs = [pl.BlockSpec((SEQ, kh), lambda h, j: (0, h)), pl.BlockSpec((SEQ, ns), lambda h, j: (0, j))]
        out_spec = pl.BlockSpec((1, 1, kh, ns), lambda h, j: (j, h, 0, 0))
        sem = ("arbitrary", "arbitrary")
    else:
        out = jax.ShapeDtypeStruct((kin, n), F32)
        grid = (kin // tk,)
        in_specs = [pl.BlockSpec((SEQ, tk), lambda r: (0, r)), pl.BlockSpec((SEQ, n), lambda r: (0, 0))]
        out_spec = pl.BlockSpec((tk, n), lambda r: (r, 0))
        sem = ("arbitrary",)
    res = pl.pallas_call(
        body, name=name, grid=grid, out_shape=out, in_specs=in_specs, out_specs=out_spec,
        compiler_params=_params(sem),
    )(a, b)
    if not col_sharded:
        res = res.reshape(N_CHIPS, 2, kin // (2 * N_CHIPS), n)
    return res


def _place():
    x, y, c = lax.axis_index("x"), lax.axis_index("y"), lax.axis_index("c")
    chips = [(1 - x, y), (x, 1 - y), (1 - x, 1 - y)]
    return x, y, c, chips


def _rcopy(src, dst, send_sem, recv_sem, dev):
    return pltpu.make_async_remote_copy(src_ref=src, dst_ref=dst, send_sem=send_sem, recv_sem=recv_sem,
                                        device_id=dev, device_id_type=MESH)


class _Gather:
    def __init__(self, shapes, w, o, sems):
        self.n, self.shapes, self.w, self.o = len(w), shapes, w, o
        self.send, self.recv, self.fsend, self.frecv, self.loc = sems
        self.x, self.y, self.c, self.chips = _place()
        self.me = 2 * self.x + self.y
        self.sib = (self.x, self.y, 1 - self.c)
        self.pairs = [(j, t) for j in range(3) for t in range(self.n)]

    @staticmethod
    def scratch(n):
        sems = pltpu.SemaphoreType.DMA
        return [sems((3 * n,)), sems((3 * n,)), sems((3 * n,)), sems((3 * n,)), sems((n,))]

    def _half(self, t, k, cc):
        rh = self.shapes[t][0] // 2
        return self.o[t].at[k, pl.ds(cc * rh, rh), :]

    def _chip(self, j):
        cx, cy = self.chips[j]
        return 2 * cx + cy, (cx, cy, self.c)

    def local(self, t):
        return pltpu.make_async_copy(self.w[t], self.o[t].at[self.me], self.loc.at[t])

    def first(self, j, t):
        rh = self.shapes[t][0] // 2
        i = j * self.n + t
        return _rcopy(self.w[t].at[pl.ds(self.c * rh, rh), :], self._half(t, self.me, self.c),
                      self.send.at[i], self.recv.at[i], self._chip(j)[1])

    def arrived(self, j, t):
        k, dev = self._chip(j)
        i = j * self.n + t
        blk = self._half(t, k, self.c)
        return _rcopy(blk, blk, self.send.at[i], self.recv.at[i], dev)

    def passed(self, j, t, cc):
        i = j * self.n + t
        blk = self._half(t, self._chip(j)[0], cc)
        return _rcopy(blk, blk, self.fsend.at[i], self.frecv.at[i], self.sib)

    def start(self):
        for t in range(self.n):
            self.local(t).start()
        for j, t in self.pairs:
            self.first(j, t).start()

    def forward(self):
        for j, t in self.pairs:
            self.arrived(j, t).wait_recv()
            self.passed(j, t, self.c).start()

    def finish(self):
        for j, t in self.pairs:
            self.passed(j, t, 1 - self.c).wait_recv()
        for j, t in self.pairs:
            self.first(j, t).wait_send()
            self.passed(j, t, self.c).wait_send()
        for t in range(self.n):
            self.local(t).wait()


def all_gather_weights(shards, small):
    n = len(shards)
    shapes = [s.shape for s in shards]

    def body(*refs):
        w = refs[:n]
        sm = refs[n]
        o = refs[n + 1:2 * n + 1]
        osm = refs[2 * n + 1]
        ssend, srecv, sloc = refs[2 * n + 2:2 * n + 5]
        g = _Gather(shapes, w, o, refs[2 * n + 5:])
        own = pltpu.make_async_copy(sm, osm.at[g.me], sloc)
        own.start()
        g.start()
        small_cps = [_rcopy(sm, osm.at[g.me], ssend.at[j], srecv.at[j], g._chip(j)[1]) for j in range(3)]
        for cp in small_cps:
            cp.start()
        g.forward()
        g.finish()
        for j in range(3):
            k, dev = g._chip(j)
            _rcopy(sm, osm.at[k], ssend.at[j], srecv.at[j], dev).wait_recv()
            small_cps[j].wait_send()
        own.wait()

    out_shape = [jax.ShapeDtypeStruct((N_CHIPS,) + s.shape, s.dtype) for s in shards]
    out_shape.append(jax.ShapeDtypeStruct((N_CHIPS,) + small.shape, small.dtype))
    sems = pltpu.SemaphoreType.DMA
    return pl.pallas_call(
        body, name="all_gather_weights", out_shape=out_shape,
        in_specs=[ANY] * (n + 1), out_specs=[ANY] * (n + 1),
        scratch_shapes=[sems((3,)), sems((3,)), sems] + _Gather.scratch(n),
    )(*shards, small)


def sibling_exchange(grads, name):
    n = len(grads)

    def body(*refs):
        g = refs[:n]
        o = refs[n:2 * n]
        send, recv = refs[2 * n:]
        x, y, c, _ = _place()
        cps = [_rcopy(g[t].at[:, 1 - c], o[t], send.at[t], recv.at[t], (x, y, 1 - c)) for t in range(n)]
        for cp in cps:
            cp.start()
        for cp in cps:
            cp.wait()

    sems = pltpu.SemaphoreType.DMA
    return pl.pallas_call(
        body, name=name,
        out_shape=[jax.ShapeDtypeStruct((a.shape[0],) + a.shape[2:], a.dtype) for a in grads],
        in_specs=[ANY] * n, out_specs=[ANY] * n, scratch_shapes=[sems((n,)), sems((n,))],
    )(*grads)


class _Scatter:
    def __init__(self, p, o, sems):
        self.n, self.p, self.o = len(p), p, o
        self.send, self.recv = sems
        _, _, self.c, self.chips = _place()

    @staticmethod
    def scratch(n):
        sems = pltpu.SemaphoreType.DMA
        return [sems((3 * n,)), sems((3 * n,))]

    @staticmethod
    def out_shapes(parts):
        return [jax.ShapeDtypeStruct((3,) + a.shape[1:], a.dtype) for a in parts]

    def copies(self):
        cps = []
        for j, (cx, cy) in enumerate(self.chips):
            for t in range(self.n):
                i = j * self.n + t
                cps.append(_rcopy(self.p[t].at[2 * cx + cy], self.o[t].at[j], self.send.at[i], self.recv.at[i],
                                  (cx, cy, self.c)))
        return cps

    def start(self):
        for cp in self.copies():
            cp.start()

    def finish(self):
        for cp in self.copies():
            cp.wait()


def sibling_assemble(halves, name):
    n = len(halves)

    def body(*refs):
        h = refs[:n]
        o = refs[n:2 * n]
        send, recv, loc = refs[2 * n:]
        x, y, c, _ = _place()

        def dst(t, cc):
            r = halves[t].shape[0]
            return o[t].at[pl.ds(cc * r, r), :]

        local = [pltpu.make_async_copy(h[t], dst(t, c), loc.at[t]) for t in range(n)]
        cps = [_rcopy(h[t], dst(t, c), send.at[t], recv.at[t], (x, y, 1 - c)) for t in range(n)]
        for cp in local + cps:
            cp.start()
        for t in range(n):
            cps[t].wait_send()
            _rcopy(h[t], dst(t, 1 - c), send.at[t], recv.at[t], (x, y, 1 - c)).wait_recv()
        for cp in local:
            cp.wait()

    sems = pltpu.SemaphoreType.DMA
    return pl.pallas_call(
        body, name=name,
        out_shape=[jax.ShapeDtypeStruct((2 * a.shape[0], a.shape[1]), a.dtype) for a in halves],
        in_specs=[ANY] * n, out_specs=[ANY] * n, scratch_shapes=[sems((n,)), sems((n,)), sems((n,))],
    )(*halves)


def small_all_reduce(ddw, v512, v1024):
    rows, width = PACK_ROWS, 512
    n512, n1024 = len(VEC512), len(VEC1024)

    def body(*refs):
        ddw_ref = refs[0]
        a_refs = refs[1:1 + n512]
        b_refs = refs[1 + n512:1 + n512 + n1024]
        o_ref, p_ref, gath_ref, send, recv = refs[1 + n512 + n1024:]
        p_ref[...] = jnp.zeros_like(p_ref)
        p_ref[0:32, :] = ddw_ref[...]
        for i, r in enumerate(a_refs):
            p_ref[32 + i:33 + i, :] = r[...]
        for i, r in enumerate(b_refs):
            base = 32 + n512 + 2 * i
            p_ref[base:base + 1, :] = r[:, 0:512]
            p_ref[base + 1:base + 2, :] = r[:, 512:1024]
        x, y, c, _ = _place()
        me = 4 * x + 2 * y + c
        gath_ref[me] = p_ref[...]
        cps = []
        for k in range(1, 8):
            dx, dy, dc = (k >> 2) & 1, (k >> 1) & 1, k & 1
            px = 1 - x if dx else x
            py = 1 - y if dy else y
            pc = 1 - c if dc else c
            cps.append(_rcopy(p_ref, gath_ref.at[me], send.at[k - 1], recv.at[k - 1], (px, py, pc)))
        for cp in cps:
            cp.start()
        for k in range(1, 8):
            dx, dy, dc = (k >> 2) & 1, (k >> 1) & 1, k & 1
            px = 1 - x if dx else x
            py = 1 - y if dy else y
            pc = 1 - c if dc else c
            _rcopy(p_ref, gath_ref.at[4 * px + 2 * py + pc], send.at[k - 1], recv.at[k - 1], (px, py, pc)).wait_recv()
        for cp in cps:
            cp.wait_send()
        total = gath_ref[0]
        for d in range(1, 8):
            total = total + gath_ref[d]
        o_ref[...] = total

    sems = pltpu.SemaphoreType.DMA
    n_in = 1 + n512 + n1024
    return pl.pallas_call(
        body, name="small_all_reduce", out_shape=jax.ShapeDtypeStruct((rows, width), F32),
        in_specs=[VMEM_SPEC] * n_in, out_specs=VMEM_SPEC,
        scratch_shapes=[pltpu.VMEM((rows, width), F32), pltpu.VMEM((8, rows, width), F32), sems((7,)), sems((7,))],
    )(ddw, *[v512[n] for n in VEC512], *[v1024[n] for n in VEC1024])


def _row_block(r):
    for tr in (512, 352, 256, 128):
        if r % tr == 0:
            return tr
    return r


def add_halves(g, recv, name):
    _, _, r, w = g.shape
    tr = _row_block(r)

    def body(g0_ref, g1_ref, r_ref, ob_ref, own_ref):
        k = pl.program_id(1)
        c = lax.axis_index("c")
        me = 2 * lax.axis_index("x") + lax.axis_index("y")
        t = jnp.where(c == 0, g0_ref[0, 0], g1_ref[0, 0]) + r_ref[0]
        ob_ref[0] = t.astype(MM)
        mine = jnp.where(k == me, t, 0.0)

        @pl.when(k == 0)
        def _():
            own_ref[...] = mine

        @pl.when(k != 0)
        def _():
            own_ref[...] += mine

    return pl.pallas_call(
        body, name=name, grid=(r // tr, N_CHIPS),
        in_specs=[pl.BlockSpec((1, 1, tr, w), lambda i, k: (k, 0, i, 0)),
                  pl.BlockSpec((1, 1, tr, w), lambda i, k: (k, 1, i, 0)),
                  pl.BlockSpec((1, tr, w), lambda i, k: (k, i, 0))],
        out_specs=[pl.BlockSpec((1, tr, w), lambda i, k: (k, i, 0)),
                   pl.BlockSpec((tr, w), lambda i, k: (i, 0))],
        out_shape=(jax.ShapeDtypeStruct((N_CHIPS, r, w), MM), jax.ShapeDtypeStruct((r, w), F32)),
        compiler_params=_params(("arbitrary", "arbitrary")),
    )(g, g, recv)


def sum_parts(own, rin, name):
    _, r, w = rin.shape
    tr = _row_block(r)

    def body(o_ref, r_ref, out_ref):
        out_ref[...] = ((o_ref[...] + r_ref[0].astype(F32)) + r_ref[1].astype(F32)) + r_ref[2].astype(F32)

    return pl.pallas_call(
        body, name=name, grid=(r // tr,), out_shape=jax.ShapeDtypeStruct((r, w), F32),
        in_specs=[pl.BlockSpec((tr, w), lambda i: (i, 0)), pl.BlockSpec((3, tr, w), lambda i: (0, i, 0))],
        out_specs=pl.BlockSpec((tr, w), lambda i: (i, 0)),
        compiler_params=_params(("arbitrary",)),
    )(own, rin)


def _adamw_math(w, g, m, v):
    mn = ADAM_B1 * m + (1.0 - ADAM_B1) * g
    vn = ADAM_B2 * v + (1.0 - ADAM_B2) * (g * g)
    m_hat = mn / (1.0 - ADAM_B1 ** ADAM_STEP)
    v_hat = vn / (1.0 - ADAM_B2 ** ADAM_STEP)
    return -ADAM_LR * (m_hat / (jnp.sqrt(v_hat) + ADAM_EPS) + ADAM_WD * w), mn, vn


def adamw(w, g, m, v, name):
    r, c = w.shape
    tr = _row_block(r)
    if c >= 1024 and tr % 512 == 0:
        tr = 256

    def body(w_ref, g_ref, m_ref, v_ref, go_ref, d_ref, mo_ref, vo_ref):
        gv = g_ref[...]
        go_ref[...] = gv
        d_ref[...], mo_ref[...], vo_ref[...] = _adamw_math(w_ref[...], gv, m_ref[...], v_ref[...])

    spec = pl.BlockSpec((tr, c), lambda i: (i, 0))
    out = jax.ShapeDtypeStruct((r, c), F32)
    return pl.pallas_call(
        body, name=name, grid=(r // tr,), out_shape=(out, out, out, out),
        in_specs=[spec] * 4, out_specs=[spec] * 4, compiler_params=_params(("arbitrary",)),
    )(w, g, m, v)


def adamw_small(gsum, params):
    names = list(params)
    flat = [a for n in names for a in params[n]]

    def body(*refs):
        g_ref = refs[0]
        ins = refs[1:1 + 3 * len(names)]
        outs = refs[1 + 3 * len(names):]
        me = 2 * lax.axis_index("x") + lax.axis_index("y")
        for i, n in enumerate(names):
            w_ref, m_ref, v_ref = ins[3 * i:3 * i + 3]
            go_ref, d_ref, mo_ref, vo_ref = outs[4 * i:4 * i + 4]
            if n == "conv_dw_w":
                gv = jnp.zeros((CONV_WIDTH, 128), F32)
                for k in range(N_CHIPS):
                    gv = gv + jnp.where(me == k, g_ref[0:CONV_WIDTH, 128 * k:128 * (k + 1)], 0.0)
            elif n in VEC512:
                r0 = 32 + VEC512.index(n)
                gv = g_ref[r0:r0 + 1, :]
            else:
                r0 = 32 + len(VEC512) + 2 * VEC1024.index(n)
                gv = jnp.concatenate([g_ref[r0:r0 + 1, :], g_ref[r0 + 1:r0 + 2, :]], axis=1)
            go_ref[...] = gv
            d_ref[...], mo_ref[...], vo_ref[...] = _adamw_math(w_ref[...], gv, m_ref[...], v_ref[...])

    out_shape = [jax.ShapeDtypeStruct(params[n][0].shape, F32) for n in names for _ in range(4)]
    res = pl.pallas_call(
        body, name="adamw_small", out_shape=out_shape,
        in_specs=[VMEM_SPEC] * (1 + len(flat)), out_specs=[VMEM_SPEC] * len(out_shape),
        compiler_params=_params(),
    )(gsum, *flat)
    return {n: res[4 * i:4 * i + 4] for i, n in enumerate(names)}


REST = ("w_ffn_up", "w_ffn_down", "w_out", "w_conv_branch", "w_att_branch")
VEC512 = ("conv_dw_b", "conv_ln_g", "conv_ln_b")
VEC1024 = ("norm_mix_pre", "b_conv_branch", "norm_mix_post", "norm_ffn_pre", "norm_ffn_post")
PACK_ROWS = 48


def kernel(x, norm_mix_pre, w_in, conv_dw_w, conv_dw_b, conv_ln_g, conv_ln_b, w_conv_branch, b_conv_branch, w_att_branch, w_out, norm_mix_post, norm_ffn_pre, w_ffn_up, w_ffn_down, norm_ffn_post, loss_target, m_norm_mix_pre, m_w_in, m_conv_dw_w, m_conv_dw_b, m_conv_ln_g, m_conv_ln_b, m_w_conv_branch, m_b_conv_branch, m_w_att_branch, m_w_out, m_norm_mix_post, m_norm_ffn_pre, m_w_ffn_up, m_w_ffn_down, m_norm_ffn_post, v_norm_mix_pre, v_w_in, v_conv_dw_w, v_conv_dw_b, v_conv_ln_g, v_conv_ln_b, v_w_conv_branch, v_b_conv_branch, v_w_att_branch, v_w_out, v_norm_mix_post, v_norm_ffn_pre, v_w_ffn_up, v_w_ffn_down, v_norm_ffn_post):
    weights = dict(norm_mix_pre=norm_mix_pre, w_in=w_in, conv_dw_w=conv_dw_w, conv_dw_b=conv_dw_b, conv_ln_g=conv_ln_g, conv_ln_b=conv_ln_b, w_conv_branch=w_conv_branch, b_conv_branch=b_conv_branch, w_att_branch=w_att_branch, w_out=w_out, norm_mix_post=norm_mix_post, norm_ffn_pre=norm_ffn_pre, w_ffn_up=w_ffn_up, w_ffn_down=w_ffn_down, norm_ffn_post=norm_ffn_post)
    mom = dict(norm_mix_pre=m_norm_mix_pre, w_in=m_w_in, conv_dw_w=m_conv_dw_w, conv_dw_b=m_conv_dw_b, conv_ln_g=m_conv_ln_g, conv_ln_b=m_conv_ln_b, w_conv_branch=m_w_conv_branch, b_conv_branch=m_b_conv_branch, w_att_branch=m_w_att_branch, w_out=m_w_out, norm_mix_post=m_norm_mix_post, norm_ffn_pre=m_norm_ffn_pre, w_ffn_up=m_w_ffn_up, w_ffn_down=m_w_ffn_down, norm_ffn_post=m_norm_ffn_post)
    var = dict(norm_mix_pre=v_norm_mix_pre, w_in=v_w_in, conv_dw_w=v_conv_dw_w, conv_dw_b=v_conv_dw_b, conv_ln_g=v_conv_ln_g, conv_ln_b=v_conv_ln_b, w_conv_branch=v_w_conv_branch, b_conv_branch=v_b_conv_branch, w_att_branch=v_w_att_branch, w_out=v_w_out, norm_mix_post=v_norm_mix_post, norm_ffn_pre=v_norm_ffn_pre, w_ffn_up=v_w_ffn_up, w_ffn_down=v_w_ffn_down, norm_ffn_post=v_norm_ffn_post)
    order = list(weights)
    grads, deltas, new_m, new_v = {}, {}, {}, {}
    xs = x.reshape(SEQ, D_MODEL)
    tgt = loss_target.reshape(SEQ, D_MODEL)
    row = lambda a: a.reshape(1, -1)
    g1, g2, g3, g4 = (row(weights[n]) for n in ("norm_mix_pre", "norm_mix_post", "norm_ffn_pre", "norm_ffn_post"))
    ln_g, ln_b = row(conv_ln_g), row(conv_ln_b)

    def reduce_prepare(names, partial, tag):
        from_sib = sibling_exchange([partial[n] for n in names], "sibling_exchange_" + tag)
        return [add_halves(partial[n], r, "add_" + n) for n, r in zip(names, from_sib)]

    def reduce_finish(names, summed, from_chips, tag):
        halves = [sum_parts(s[1], r, "sum_" + n) for n, s, r in zip(names, summed, from_chips)]
        for n, g in zip(names, sibling_assemble(halves, "sibling_assemble_" + tag)):
            grads[n], deltas[n], new_m[n], new_v[n] = adamw(weights[n], g, mom[n], var[n], "adamw_" + n)

    w_in_g, dw_g = all_gather_weights([w_in.astype(MM)], conv_dw_w)
    w_dw_full = jnp.concatenate([dw_g[k] for k in range(N_CHIPS)], axis=1)
    h1, ci, q, k, v, gc, ga = in_proj_fwd(xs, g1, w_in_g)
    u1, u3 = conv_fwd(ci, w_dw_full, row(conv_dw_b), ln_g, ln_b)
    att, rc, *rest = attn_fwd(q, k, v, [weights[n].astype(MM) for n in REST])
    wg = dict(zip(REST, rest))
    w_out_g = wg["w_out"].reshape(D_MODEL, D_MODEL)
    w_down_g = wg["w_ffn_down"].reshape(D_FF, D_MODEL)
    w_cb_g, w_ab_g = wg["w_conv_branch"], wg["w_att_branch"]
    co, ao, merged, mix, x2, h2 = mix_fwd(u3, att, gc, ga, xs, w_cb_g, row(b_conv_branch), w_ab_g, w_out_g, g2, g3)
    gate, up, act = ffn_up_fwd(h2, wg["w_ffn_up"])
    dff, dy, loss_parts, dg4 = ffn_down_loss(act, w_down_g, x2, tgt, g4)
    loss = lax.psum(jnp.sum(loss_parts[::8, 0]), ("x", "y", "c"))

    partial = {}
    dgate, dup = ffn_act_bwd(dff, w_down_g, gate, up)
    partial["w_ffn_down"] = weight_grad(act, dff, "dw_ffn_down", False, tk=UP_SHARD)
    dx2, dmix, dg3, dg2 = ffn_in_bwd(dgate, dup, wg["w_ffn_up"], x2, mix, dy, g3, g2)
    partial["w_ffn_up"] = weight_grad(h2, jnp.concatenate([dgate, dup], axis=1), "dw_ffn_up", True)
    dco, dao, dgc, dga, du3, datt, dbcb = merge_bwd(dmix, w_out_g, gc, ga, co, ao, w_cb_g, w_ab_g)
    partial["w_out"] = weight_grad(merged, dmix, "dw_out", False, tk=512)
    partial["w_conv_branch"] = weight_grad(u3, dco, "dw_conv_branch", True)
    partial["w_att_branch"] = weight_grad(att, dao, "dw_att_branch", True)
    summed = reduce_prepare(REST, partial, "rest")
    dci, ddw, dbdw, dlng, dlnb = conv_bwd(du3, u1, ci, w_dw_full, ln_g, ln_b)
    dq, dk, dv, *from_chips = attn_bwd(q, k, v, datt, rc, [s[0] for s in summed])
    reduce_finish(REST, summed, from_chips, "rest")
    dproj = jnp.concatenate([dci, dq, dk, dv, dgc, dga], axis=1)
    partial["w_in"] = weight_grad(h1, dproj, "dw_in", True)
    summed = reduce_prepare(("w_in",), partial, "w_in")
    grad_x, dg1, *from_chips = in_proj_bwd(dproj, w_in_g, xs, dx2, g1, [summed[0][0]])
    reduce_finish(("w_in",), summed, from_chips, "w_in")

    v512 = dict(conv_dw_b=dbdw, conv_ln_g=dlng, conv_ln_b=dlnb)
    v1024 = dict(norm_mix_pre=dg1, b_conv_branch=dbcb, norm_mix_post=dg2, norm_ffn_pre=dg3, norm_ffn_post=dg4)
    gsum = small_all_reduce(ddw, v512, v1024)
    as_rows = lambda n, a: a if n == "conv_dw_w" else a.reshape(1, -1)
    small_names = ("conv_dw_w",) + VEC512 + VEC1024
    small = adamw_small(gsum, {n: tuple(as_rows(n, d[n]) for d in (weights, mom, var)) for n in small_names})
    for n in small_names:
        grads[n], deltas[n], new_m[n], new_v[n] = (a.reshape(weights[n].shape) for a in small[n])

    return (loss, grad_x.reshape(1, SEQ, D_MODEL), *[grads[n] for n in order], *[deltas[n] for n in order],
            *[new_m[n] for n in order], *[new_v[n] for n in order])
```

```python
import jax
import jax.numpy as jnp
from jax import lax
from jax.experimental import pallas as pl
from jax.experimental.pallas import tpu as pltpu

F32 = jnp.float32
MM = jnp.bfloat16

SEQ = 2048
D_MODEL = 1024
CONV_DIM = 512
ATT_DIM = 512
CONV_WIDTH = 31
D_FF = 2816
IN_COLS = 2 * CONV_DIM + 3 * ATT_DIM + 2 * D_MODEL
N_CHIPS = 4
IN_SHARD = IN_COLS // N_CHIPS
UP_SHARD = 2 * D_FF // N_CHIPS
BR_SHARD = D_MODEL // N_CHIPS
EPS = 1e-6
ATT_SCALE = 0.125

TM = 256
TQ = 128
CONV_TILE = 64
CONV_WIN = CONV_TILE + 32
VMEM_LIMIT = 56 * 1024 * 1024

ADAM_LR = 0.001
ADAM_B1 = 0.9
ADAM_B2 = 0.999
ADAM_EPS = 1e-08
ADAM_WD = 0.01
ADAM_STEP = 10

MESH = pl.DeviceIdType.MESH
ANY = pl.BlockSpec(memory_space=pl.ANY)
VMEM_SPEC = pl.BlockSpec(memory_space=pltpu.VMEM)

NT_DIMS = (((1,), (1,)), ((), ()))
TN_DIMS = (((0,), (0,)), ((), ()))

IN_PIECES = (("ci", 0, 1024), ("q", 1024, 1536), ("k", 1536, 2048), ("v", 2048, 2560),
             ("gc", 2560, 3584), ("ga", 3584, 4608))


def _params(sem=None, vmem=VMEM_LIMIT):
    return pltpu.CompilerParams(dimension_semantics=sem, vmem_limit_bytes=vmem)


def _dot(a, b):
    return jnp.dot(a, b, preferred_element_type=F32)


def _dot_nt(a, b):
    return lax.dot_general(a, b, NT_DIMS, preferred_element_type=F32)


def _dot_tn(a, b):
    return lax.dot_general(a, b, TN_DIMS, preferred_element_type=F32)


def _sigmoid(x):
    return 1.0 / (1.0 + jnp.exp(-x))


def _rms(x):
    r = lax.rsqrt(jnp.mean(x * x, axis=-1, keepdims=True) + EPS)
    return x * r, r


def _rms_bwd(dy_g, n, r):
    return r * (dy_g - n * jnp.mean(dy_g * n, axis=-1, keepdims=True))


def _row_tile_spec(width, tm=TM):
    return pl.BlockSpec((tm, width), lambda i: (i, 0))


def _full_spec(shape):
    nd = len(shape)
    return pl.BlockSpec(shape, lambda *_: (0,) * nd)


def _acc_rows(ref, val, first):
    @pl.when(first)
    def _():
        ref[...] = val

    @pl.when(jnp.logical_not(first))
    def _():
        ref[...] += val


def in_proj_fwd(x, g1, w_in_g):
    def body(x_ref, g_ref, w_ref, h_ref, ci_ref, q_ref, k_ref, v_ref, gc_ref, ga_ref):
        n, _ = _rms(x_ref[...])
        h = (n * g_ref[...]).astype(MM)
        h_ref[...] = h
        outs = dict(ci=ci_ref, q=q_ref, k=k_ref, v=v_ref, gc=gc_ref, ga=ga_ref)
        for j in range(N_CHIPS):
            p = _dot(h, w_ref[j])
            g0 = j * IN_SHARD
            for name, s, e in IN_PIECES:
                lo, hi = max(s, g0), min(e, g0 + IN_SHARD)
                if lo < hi:
                    ref = outs[name]
                    ref[:, lo - s:hi - s] = p[:, lo - g0:hi - g0].astype(ref.dtype)

    out_shape = (
        jax.ShapeDtypeStruct((SEQ, D_MODEL), MM),
        jax.ShapeDtypeStruct((SEQ, 2 * CONV_DIM), F32),
        jax.ShapeDtypeStruct((SEQ, ATT_DIM), MM),
        jax.ShapeDtypeStruct((SEQ, ATT_DIM), MM),
        jax.ShapeDtypeStruct((SEQ, ATT_DIM), MM),
        jax.ShapeDtypeStruct((SEQ, D_MODEL), F32),
        jax.ShapeDtypeStruct((SEQ, D_MODEL), F32),
    )
    return pl.pallas_call(
        body, name="in_proj_fwd", grid=(SEQ // TM,), out_shape=out_shape,
        in_specs=[_row_tile_spec(D_MODEL), _full_spec((1, D_MODEL)), _full_spec(w_in_g.shape)],
        out_specs=[_row_tile_spec(s.shape[1]) for s in out_shape],
        compiler_params=_params(("arbitrary",)),
    )(x, g1, w_in_g)


def _shifted_sum(win, terms):
    by_rot = {}
    for m, coef in terms:
        by_rot.setdefault(m % 8, []).append((m // 8, coef))
    acc = None
    n = win.shape[0]
    for rot in sorted(by_rot):
        shifted = win if rot == 0 else pltpu.roll(win, n - rot, 0)
        for a, coef in by_rot[rot]:
            t = coef * shifted[8 * a:8 * a + CONV_TILE, :]
            acc = t if acc is None else acc + t
    return acc


def _glu_into(ci_ref, upad_ref):
    upad_ref[0:32, :] = jnp.zeros((32, CONV_DIM), F32)

    def step(i, c):
        t0 = pl.multiple_of(i * TM, TM)
        a = ci_ref[pl.ds(t0, TM), 0:CONV_DIM]
        b = ci_ref[pl.ds(t0, TM), CONV_DIM:2 * CONV_DIM]
        upad_ref[pl.ds(t0 + 32, TM), :] = a * _sigmoid(b)
        return c

    lax.fori_loop(0, SEQ // TM, step, 0)


def _layernorm_parts(u1):
    mu = jnp.mean(u1, axis=-1, keepdims=True)
    xc = u1 - mu
    rstd = lax.rsqrt(jnp.mean(xc * xc, axis=-1, keepdims=True) + EPS)
    return xc * rstd, rstd


def conv_fwd(ci, w_dw, b_dw, ln_g, ln_b):
    def body(ci_ref, w_ref, b_ref, g_ref, bb_ref, u1_ref, u3_ref, upad_ref):
        _glu_into(ci_ref, upad_ref)

        def step(i, c):
            t0 = pl.multiple_of(i * CONV_TILE, CONV_TILE)
            win = upad_ref[pl.ds(t0, CONV_WIN), :]
            u1 = _shifted_sum(win, [(j + 2, w_ref[j:j + 1, :]) for j in range(CONV_WIDTH)]) + b_ref[...]
            u1_ref[pl.ds(t0, CONV_TILE), :] = u1
            xh, _ = _layernorm_parts(u1)
            u2 = xh * g_ref[...] + bb_ref[...]
            u3_ref[pl.ds(t0, CONV_TILE), :] = (u2 * _sigmoid(u2)).astype(MM)
            return c

        lax.fori_loop(0, SEQ // CONV_TILE, step, 0)

    return pl.pallas_call(
        body, name="conv_fwd",
        out_shape=(jax.ShapeDtypeStruct((SEQ, CONV_DIM), F32), jax.ShapeDtypeStruct((SEQ, CONV_DIM), MM)),
        in_specs=[VMEM_SPEC] * 5, out_specs=[VMEM_SPEC] * 2,
        scratch_shapes=[pltpu.VMEM((SEQ + 32, CONV_DIM), F32)],
        compiler_params=_params(),
    )(ci, w_dw, b_dw, ln_g, ln_b)


def _softplus(z):
    return jnp.maximum(z, 0.0) + jnp.log(1.0 + jnp.exp(-jnp.abs(z)))


def _cumsum_weights(suffix, with_total):
    n = 256 if with_total else 128
    r = lax.broadcasted_iota(jnp.int32, (256, n), 0) & 127
    c = lax.broadcasted_iota(jnp.int32, (256, n), 1)
    tri = (r >= c) if suffix else (r <= c)
    return jnp.logical_or(tri, c >= 128).astype(MM)


def _hi_lo(x):
    hi = x.astype(MM)
    lo = (x - hi.astype(F32)).astype(MM)
    return jnp.concatenate([hi, lo], axis=1)


def _cumsum_dot(x, w):
    r = _dot(_hi_lo(x), w)
    return (r[:, :128], r[:, 128:]) if w.shape[1] == 256 else (r, None)


NO_SCORE = -1e30


def _head_masks():
    lane = lax.broadcasted_iota(jnp.int32, (TQ, 128), 1)
    row = lax.broadcasted_iota(jnp.int32, (TQ, 128), 0)
    return lane, row, lane < 64


def _pick_head(x, head0, h):
    zero = jnp.zeros_like(x)
    return jnp.where(head0, x, zero) if h == 0 else jnp.where(head0, zero, x)


N_PAIRS = ATT_DIM // 128


def attn_fwd(q, k, v, gather=()):
    ng = len(gather)
    nqb = SEQ // TQ

    def body(*refs):
        q_ref, k_ref, v_ref = refs[:3]
        o_ref, rc_ref = refs[3 + ng:5 + ng]
        acc_ref, r_ref, z_ref, spb_ref, ab_ref = refs[5 + 2 * ng:10 + 2 * ng]
        i = pl.program_id(0)
        if ng:
            ag = _Gather([s.shape for s in gather], refs[3:3 + ng], refs[5 + ng:5 + 2 * ng], refs[10 + 2 * ng:])
            pl.when(i == 0)(ag.start)
            pl.when(i == nqb - 1)(ag.forward)
        lane, row, head0 = _head_masks()
        w = _cumsum_weights(suffix=True, with_total=True)
        acc_ref[...] = jnp.zeros_like(acc_ref)
        r_ref[...] = jnp.zeros_like(r_ref)
        rc_ref[...] = jnp.zeros_like(rc_ref)
        z_ref[...] = jnp.full(z_ref.shape, NO_SCORE, F32)
        spb_ref[...] = jnp.zeros_like(spb_ref)
        ab_ref[...] = jnp.zeros_like(ab_ref)

        def step(n, c):
            b1 = i - n
            b2 = b1 + 1
            k1 = pl.multiple_of(jnp.maximum(b1, 0) * TQ, TQ)
            k3 = pl.multiple_of(jnp.clip(b1 + 2, 0, i) * TQ, TQ)
            keep1 = jnp.logical_and(b1 >= 0, jnp.logical_or(b1 < i, lane < row))
            mark2 = jnp.logical_and(lane == b2, b2 <= i)
            for p in range(N_PAIRS):
                cols = slice(128 * p, 128 * (p + 1))
                qb = q_ref[:, cols]
                kb = k_ref[pl.ds(k1, TQ), cols]
                vb = v_ref[pl.ds(k3, TQ), cols]
                for h in range(2):
                    hh = 2 * p + h
                    acc_ref[:, cols] += _dot(ab_ref[hh], _pick_head(vb, head0, h))
                    r = _dot(spb_ref[hh], w)
                    r_in = r_ref[hh]
                    ab_ref[hh] = jnp.exp(z_ref[hh] - (r[:, :128] + r_in)).astype(MM)
                    rc_ref[hh] = jnp.where(mark2, r_in, rc_ref[hh])
                    r_ref[hh] = r_in + r[:, 128:]
                    z = _dot_nt(_pick_head(qb, head0, h), kb) * ATT_SCALE
                    sp = jnp.where(keep1, _softplus(z), 0.0)
                    z_ref[hh] = jnp.where(keep1, z, NO_SCORE)
                    spb_ref[hh] = _hi_lo(sp)
            return c

        lax.fori_loop(0, i + 3, step, 0)
        o_ref[...] = acc_ref[...].astype(MM)
        if ng:
            pl.when(i == nqb - 1)(ag.finish)

    full = pl.BlockSpec((SEQ, ATT_DIM), lambda i: (0, 0))
    out_shape = [jax.ShapeDtypeStruct((SEQ, ATT_DIM), MM), jax.ShapeDtypeStruct((8, SEQ, 128), F32)]
    out_shape += [jax.ShapeDtypeStruct((N_CHIPS,) + s.shape, s.dtype) for s in gather]
    return pl.pallas_call(
        body, name="attn_fwd", grid=(nqb,), out_shape=out_shape,
        in_specs=[_row_tile_spec(ATT_DIM, TQ), full, full] + [ANY] * ng,
        out_specs=[_row_tile_spec(ATT_DIM, TQ), pl.BlockSpec((8, TQ, 128), lambda i: (0, i, 0))] + [ANY] * ng,
        scratch_shapes=[pltpu.VMEM((TQ, ATT_DIM), F32), pltpu.VMEM((8, TQ, 128), F32),
                        pltpu.VMEM((8, TQ, 128), F32), pltpu.VMEM((8, TQ, 256), MM), pltpu.VMEM((8, TQ, 128), MM)]
                       + (_Gather.scratch(gather) if ng else []),
        compiler_params=_params(("arbitrary",)),
    )(q, k, v, *gather)


def mix_fwd(u3, att, gc, ga, x, w_cb_g, b_cb, w_ab_g, w_out_g, g2, g3):
    def body(u_ref, a_ref, gc_ref, ga_ref, x_ref, wcb_ref, bcb_ref, wab_ref, wout_ref, g2_ref, g3_ref,
             co_ref, ao_ref, mg_ref, mix_ref, x2_ref, h2_ref):
        u = u_ref[...]
        a = a_ref[...]
        for j in range(N_CHIPS):
            cols = slice(j * BR_SHARD, (j + 1) * BR_SHARD)
            co_ref[:, cols] = _dot(u, wcb_ref[j]) + bcb_ref[:, cols]
            ao_ref[:, cols] = _dot(a, wab_ref[j])
        merged = (_sigmoid(gc_ref[...]) * co_ref[...] + _sigmoid(ga_ref[...]) * ao_ref[...]).astype(MM)
        mg_ref[...] = merged
        mix = _dot(merged, wout_ref[...])
        mix_ref[...] = mix
        n2, _ = _rms(mix)
        x2 = x_ref[...] + n2 * g2_ref[...]
        x2_ref[...] = x2
        n3, _ = _rms(x2)
        h2_ref[...] = (n3 * g3_ref[...]).astype(MM)

    out_shape = (
        jax.ShapeDtypeStruct((SEQ, D_MODEL), F32), jax.ShapeDtypeStruct((SEQ, D_MODEL), F32),
        jax.ShapeDtypeStruct((SEQ, D_MODEL), MM), jax.ShapeDtypeStruct((SEQ, D_MODEL), F32),
        jax.ShapeDtypeStruct((SEQ, D_MODEL), F32), jax.ShapeDtypeStruct((SEQ, D_MODEL), MM),
    )
    vec = _full_spec((1, D_MODEL))
    return pl.pallas_call(
        body, name="mix_fwd", grid=(SEQ // TM,), out_shape=out_shape,
        in_specs=[_row_tile_spec(CONV_DIM), _row_tile_spec(ATT_DIM), _row_tile_spec(D_MODEL),
                  _row_tile_spec(D_MODEL), _row_tile_spec(D_MODEL), _full_spec(w_cb_g.shape), vec,
                  _full_spec(w_ab_g.shape), _full_spec(w_out_g.shape), vec, vec],
        out_specs=[_row_tile_spec(D_MODEL)] * 6,
        compiler_params=_params(("arbitrary",)),
    )(u3, att, gc, ga, x, w_cb_g, b_cb, w_ab_g, w_out_g, g2, g3)


def ffn_up_fwd(h2, w_up_g):
    def body(h_ref, wg_ref, wu_ref, gate_ref, up_ref, act_ref):
        h = h_ref[...]
        gate = _dot(h, wg_ref[0])
        up = _dot(h, wu_ref[0])
        gate_ref[...] = gate
        up_ref[...] = up
        act_ref[...] = (gate * _sigmoid(gate) * up).astype(MM)

    tile = pl.BlockSpec((TM, UP_SHARD), lambda n, i: (i, n))
    return pl.pallas_call(
        body, name="ffn_up_fwd", grid=(2, SEQ // TM),
        out_shape=(jax.ShapeDtypeStruct((SEQ, D_FF), F32), jax.ShapeDtypeStruct((SEQ, D_FF), F32),
                   jax.ShapeDtypeStruct((SEQ, D_FF), MM)),
        in_specs=[pl.BlockSpec((TM, D_MODEL), lambda n, i: (i, 0)),
                  pl.BlockSpec((1, D_MODEL, UP_SHARD), lambda n, i: (n, 0, 0)),
                  pl.BlockSpec((1, D_MODEL, UP_SHARD), lambda n, i: (n + 2, 0, 0))],
        out_specs=[tile, tile, tile],
        compiler_params=_params(("arbitrary", "arbitrary")),
    )(h2, w_up_g, w_up_g)


def ffn_down_loss(act, w_down_g, x2, target, g4):
    def body(act_ref, wd_ref, x2_ref, t_ref, g_ref, dff_ref, dy_ref, loss_ref, dg_ref):
        ff = _dot(act_ref[...], wd_ref[...])
        n4, r4 = _rms(ff)
        g4v = g_ref[...]
        err = x2_ref[...] + n4 * g4v - t_ref[...]
        row_loss = jnp.mean(err * err, axis=-1, keepdims=True)
        loss_ref[...] = jnp.zeros((8, 128), F32) + 0.5 * jnp.sum(row_loss, axis=0, keepdims=True)
        dy = err * (1.0 / D_MODEL)
        dy_ref[...] = dy
        dff_ref[...] = _rms_bwd(dy * g4v, n4, r4).astype(MM)
        _acc_rows(dg_ref, jnp.sum(dy * n4, axis=0, keepdims=True), pl.program_id(0) == 0)

    nt = SEQ // TM
    vec = _full_spec((1, D_MODEL))
    return pl.pallas_call(
        body, name="ffn_down_loss", grid=(nt,),
        out_shape=(jax.ShapeDtypeStruct((SEQ, D_MODEL), MM), jax.ShapeDtypeStruct((SEQ, D_MODEL), F32),
                   jax.ShapeDtypeStruct((nt * 8, 128), F32), jax.ShapeDtypeStruct((1, D_MODEL), F32)),
        in_specs=[_row_tile_spec(D_FF), _full_spec(w_down_g.shape), _row_tile_spec(D_MODEL),
                  _row_tile_spec(D_MODEL), vec],
        out_specs=[_row_tile_spec(D_MODEL), _row_tile_spec(D_MODEL),
                   pl.BlockSpec((8, 128), lambda i: (i, 0)), vec],
        compiler_params=_params(("arbitrary",)),
    )(act, w_down_g, x2, target, g4)


def ffn_act_bwd(dff, w_down_g, gate, up):
    def body(dff_ref, wd_ref, gate_ref, up_ref, dgu_ref):
        dact = _dot_nt(dff_ref[...], wd_ref[...])
        gate = gate_ref[...]
        sg = _sigmoid(gate)
        dgu_ref[:, 0:D_FF] = (dact * up_ref[...] * (sg * (1.0 + gate * (1.0 - sg)))).astype(MM)
        dgu_ref[:, D_FF:2 * D_FF] = (dact * (gate * sg)).astype(MM)

    return pl.pallas_call(
        body, name="ffn_act_bwd", grid=(SEQ // TM,),
        out_shape=jax.ShapeDtypeStruct((SEQ, 2 * D_FF), MM),
        in_specs=[_row_tile_spec(D_MODEL), _full_spec(w_down_g.shape), _row_tile_spec(D_FF), _row_tile_spec(D_FF)],
        out_specs=_row_tile_spec(2 * D_FF),
        compiler_params=_params(("arbitrary",)),
    )(dff, w_down_g, gate, up)


def ffn_in_bwd(dgu, w_up_g, x2, mix, dy, g3, g2):
    def body(dgu_ref, w_ref, x2_ref, mix_ref, dy_ref, g3_ref, g2_ref, dx2_ref, dmix_ref, dg3_ref, dg2_ref):
        dh2 = None
        for j in range(N_CHIPS):
            t = _dot_nt(dgu_ref[:, j * UP_SHARD:(j + 1) * UP_SHARD], w_ref[j])
            dh2 = t if dh2 is None else dh2 + t
        first = pl.program_id(0) == 0
        n3, r3 = _rms(x2_ref[...])
        dx2 = dy_ref[...] + _rms_bwd(dh2 * g3_ref[...], n3, r3)
        dx2_ref[...] = dx2
        _acc_rows(dg3_ref, jnp.sum(dh2 * n3, axis=0, keepdims=True), first)
        n2, r2 = _rms(mix_ref[...])
        dmix_ref[...] = _rms_bwd(dx2 * g2_ref[...], n2, r2).astype(MM)
        _acc_rows(dg2_ref, jnp.sum(dx2 * n2, axis=0, keepdims=True), first)

    vec = _full_spec((1, D_MODEL))
    return pl.pallas_call(
        body, name="ffn_in_bwd", grid=(SEQ // TM,),
        out_shape=(jax.ShapeDtypeStruct((SEQ, D_MODEL), F32), jax.ShapeDtypeStruct((SEQ, D_MODEL), MM),
                   jax.ShapeDtypeStruct((1, D_MODEL), F32), jax.ShapeDtypeStruct((1, D_MODEL), F32)),
        in_specs=[_row_tile_spec(2 * D_FF), _full_spec(w_up_g.shape), _row_tile_spec(D_MODEL),
                  _row_tile_spec(D_MODEL), _row_tile_spec(D_MODEL), vec, vec],
        out_specs=[_row_tile_spec(D_MODEL), _row_tile_spec(D_MODEL), vec, vec],
        compiler_params=_params(("arbitrary",)),
    )(dgu, w_up_g, x2, mix, dy, g3, g2)


def merge_bwd(dmix, w_out_g, gc, ga, co, ao, w_cb_g, w_ab_g):
    def body(dmix_ref, wout_ref, gc_ref, ga_ref, co_ref, ao_ref, wcb_ref, wab_ref,
             dco_ref, dao_ref, dgc_ref, dga_ref, du3_ref, datt_ref, dbcb_ref):
        dm = _dot_nt(dmix_ref[...], wout_ref[...])
        sgc = _sigmoid(gc_ref[...])
        sga = _sigmoid(ga_ref[...])
        dco = dm * sgc
        dao = dm * sga
        dgc_ref[...] = (dm * co_ref[...] * (sgc * (1.0 - sgc))).astype(MM)
        dga_ref[...] = (dm * ao_ref[...] * (sga * (1.0 - sga))).astype(MM)
        _acc_rows(dbcb_ref, jnp.sum(dco, axis=0, keepdims=True), pl.program_id(0) == 0)
        dco_ref[...] = dco.astype(MM)
        dao_ref[...] = dao.astype(MM)
        du3 = None
        datt = None
        for j in range(N_CHIPS):
            cols = slice(j * BR_SHARD, (j + 1) * BR_SHARD)
            t = _dot_nt(dco_ref[:, cols], wcb_ref[j])
            s = _dot_nt(dao_ref[:, cols], wab_ref[j])
            du3 = t if du3 is None else du3 + t
            datt = s if datt is None else datt + s
        du3_ref[...] = du3
        datt_ref[...] = datt.astype(MM)

    wide = _row_tile_spec(D_MODEL)
    return pl.pallas_call(
        body, name="merge_bwd", grid=(SEQ // TM,),
        out_shape=(jax.ShapeDtypeStruct((SEQ, D_MODEL), MM), jax.ShapeDtypeStruct((SEQ, D_MODEL), MM),
                   jax.ShapeDtypeStruct((SEQ, D_MODEL), MM), jax.ShapeDtypeStruct((SEQ, D_MODEL), MM),
                   jax.ShapeDtypeStruct((SEQ, CONV_DIM), F32), jax.ShapeDtypeStruct((SEQ, ATT_DIM), MM),
                   jax.ShapeDtypeStruct((1, D_MODEL), F32)),
        in_specs=[wide, _full_spec(w_out_g.shape), wide, wide, wide, wide,
                  _full_spec(w_cb_g.shape), _full_spec(w_ab_g.shape)],
        out_specs=[wide, wide, wide, wide, _row_tile_spec(CONV_DIM), _row_tile_spec(ATT_DIM),
                   _full_spec((1, D_MODEL))],
        compiler_params=_params(("arbitrary",)),
    )(dmix, w_out_g, gc, ga, co, ao, w_cb_g, w_ab_g)


def conv_bwd(du3, u1, ci, w_dw, ln_g, ln_b):
    def body(du3_ref, u1_ref, ci_ref, w_ref, g_ref, bb_ref,
             dci_ref, dw_ref, dbdw_ref, dg_ref, db_ref, upad_ref, dpad_ref, dwacc_ref, vacc_ref):
        _glu_into(ci_ref, upad_ref)
        dpad_ref[SEQ:SEQ + 32, :] = jnp.zeros((32, CONV_DIM), F32)
        dwacc_ref[...] = jnp.zeros_like(dwacc_ref)
        vacc_ref[...] = jnp.zeros_like(vacc_ref)

        def fold8(t):
            s = t[0:8, :]
            for r in range(1, CONV_TILE // 8):
                s = s + t[8 * r:8 * r + 8, :]
            return s

        def pass1(i, c):
            t0 = pl.multiple_of(i * CONV_TILE, CONV_TILE)
            xh, rstd = _layernorm_parts(u1_ref[pl.ds(t0, CONV_TILE), :])
            gv = g_ref[...]
            u2 = xh * gv + bb_ref[...]
            s2 = _sigmoid(u2)
            du2 = du3_ref[pl.ds(t0, CONV_TILE), :] * (s2 * (1.0 + u2 * (1.0 - s2)))
            wv = du2 * gv
            du1 = rstd * (wv - jnp.mean(wv, axis=-1, keepdims=True)
                          - xh * jnp.mean(wv * xh, axis=-1, keepdims=True))
            dpad_ref[pl.ds(t0, CONV_TILE), :] = du1
            vacc_ref[0] += fold8(du2 * xh)
            vacc_ref[1] += fold8(du2)
            vacc_ref[2] += fold8(du1)
            win = upad_ref[pl.ds(t0, CONV_WIN), :]
            n = win.shape[0]
            for rot in range(8):
                shifted = win if rot == 0 else pltpu.roll(win, n - rot, 0)
                for a in range(5):
                    j = 8 * a + rot - 2
                    if 0 <= j < CONV_WIDTH:
                        dwacc_ref[j] += fold8(du1 * shifted[8 * a:8 * a + CONV_TILE, :])
            return c

        lax.fori_loop(0, SEQ // CONV_TILE, pass1, 0)

        def pass2(i, c):
            t0 = pl.multiple_of(i * CONV_TILE, CONV_TILE)
            win = dpad_ref[pl.ds(t0, CONV_WIN), :]
            du0 = _shifted_sum(win, [(30 - j, w_ref[j:j + 1, :]) for j in range(CONV_WIDTH)])
            a = ci_ref[pl.ds(t0, CONV_TILE), 0:CONV_DIM]
            sb = _sigmoid(ci_ref[pl.ds(t0, CONV_TILE), CONV_DIM:2 * CONV_DIM])
            dci_ref[pl.ds(t0, CONV_TILE), 0:CONV_DIM] = (du0 * sb).astype(MM)
            dci_ref[pl.ds(t0, CONV_TILE), CONV_DIM:2 * CONV_DIM] = (du0 * a * (sb * (1.0 - sb))).astype(MM)
            return c

        lax.fori_loop(0, SEQ // CONV_TILE, pass2, 0)

        for j in range(CONV_WIDTH):
            dw_ref[j:j + 1, :] = jnp.sum(dwacc_ref[j], axis=0, keepdims=True)
        dw_ref[CONV_WIDTH:32, :] = jnp.zeros((32 - CONV_WIDTH, CONV_DIM), F32)
        dg_ref[...] = jnp.sum(vacc_ref[0], axis=0, keepdims=True)
        db_ref[...] = jnp.sum(vacc_ref[1], axis=0, keepdims=True)
        dbdw_ref[...] = jnp.sum(vacc_ref[2], axis=0, keepdims=True)

    vec = jax.ShapeDtypeStruct((1, CONV_DIM), F32)
    return pl.pallas_call(
        body, name="conv_bwd",
        out_shape=(jax.ShapeDtypeStruct((SEQ, 2 * CONV_DIM), MM), jax.ShapeDtypeStruct((32, CONV_DIM), F32),
                   vec, vec, vec),
        in_specs=[VMEM_SPEC] * 6, out_specs=[VMEM_SPEC] * 5,
        scratch_shapes=[pltpu.VMEM((SEQ + 32, CONV_DIM), F32), pltpu.VMEM((SEQ + 32, CONV_DIM), F32),
                        pltpu.VMEM((CONV_WIDTH, 8, CONV_DIM), F32), pltpu.VMEM((3, 8, CONV_DIM), F32)],
        compiler_params=_params(),
    )(du3, u1, ci, w_dw, ln_g, ln_b)


def attn_bwd(q, k, v, datt, rc, scatter=()):
    nqb = SEQ // TQ

    ns = len(scatter)

    def body(*refs):
        q_ref, k_ref, v_ref, do_ref, rc_ref = refs[:5]
        dq_ref, dk_ref, dv_ref = refs[5 + ns:8 + ns]
        (dqa_ref, dka_ref, dva_ref, pc_ref, z_ref, sig1_ref, sig2_ref, g_ref, spb_ref, gb_ref, ab_ref,
         dzb_ref) = refs[8 + 2 * ns:20 + 2 * ns]
        i = pl.program_id(0)
        if ns:
            sc = _Scatter(refs[5:5 + ns], refs[8 + ns:8 + 2 * ns], refs[20 + 2 * ns:])
            pl.when(i == 0)(sc.start)
        lane, row, head0 = _head_masks()

        @pl.when(i == 0)
        def _():
            dka_ref[...] = jnp.zeros_like(dka_ref)
            dva_ref[...] = jnp.zeros_like(dva_ref)

        dqa_ref[...] = jnp.zeros_like(dqa_ref)
        pc_ref[...] = jnp.zeros_like(pc_ref)
        z_ref[...] = jnp.full(z_ref.shape, NO_SCORE, F32)
        for ref in (sig1_ref, sig2_ref, spb_ref, ab_ref, g_ref, gb_ref, dzb_ref):
            ref[...] = jnp.zeros_like(ref)
        w_suffix = _cumsum_weights(suffix=True, with_total=False)
        w_prefix = _cumsum_weights(suffix=False, with_total=True)

        def step(n, c):
            ka, kb_, kc, kd = (pl.multiple_of(jnp.clip(n - s, 0, i) * TQ, TQ) for s in range(4))
            below = lane < row
            keep_a = jnp.logical_and(n <= i, jnp.logical_or(n < i, below))
            bc = n - 2
            keep_c = jnp.logical_and(jnp.logical_and(bc >= 0, bc <= i), jnp.logical_or(bc < i, below))
            pick_b = lane == n - 1
            for p in range(N_PAIRS):
                cols = slice(128 * p, 128 * (p + 1))
                qb = q_ref[:, cols]
                dob = do_ref[:, cols]
                k_a = k_ref[pl.ds(ka, TQ), cols]
                v_b = v_ref[pl.ds(kb_, TQ), cols]
                k_d = k_ref[pl.ds(kd, TQ), cols]
                for h in range(2):
                    hh = 2 * p + h
                    qm = _pick_head(qb, head0, h)
                    dom = _pick_head(dob, head0, h)
                    dzb = dzb_ref[hh]
                    dqa_ref[:, cols] += _dot(dzb, _pick_head(k_d, head0, h))
                    dka_ref[pl.ds(kd, TQ), cols] += _dot_tn(dzb, qm)
                    r = _dot(gb_ref[hh], w_prefix)
                    p_in = pc_ref[hh]
                    dz = (g_ref[hh] - sig2_ref[hh] * (r[:, :128] + p_in)) * ATT_SCALE
                    dzb_ref[hh] = jnp.where(keep_c, dz, 0.0).astype(MM)
                    pc_ref[hh] = p_in + r[:, 128:]
                    dva_ref[pl.ds(kc, TQ), cols] += _dot_tn(ab_ref[hh], dom)
                    r_in = jnp.sum(jnp.where(pick_b, rc_ref[hh], 0.0), axis=1, keepdims=True)
                    a = jnp.exp(z_ref[hh] - (_dot(spb_ref[hh], w_suffix) + r_in))
                    g = _dot_nt(dom, v_b) * a
                    ab_ref[hh] = a.astype(MM)
                    g_ref[hh] = g
                    gb_ref[hh] = _hi_lo(g)
                    sig2_ref[hh] = sig1_ref[hh]
                    z = _dot_nt(qm, k_a) * ATT_SCALE
                    sp = _softplus(z)
                    sig1_ref[hh] = jnp.exp(z - sp)
                    z_ref[hh] = jnp.where(keep_a, z, NO_SCORE)
                    spb_ref[hh] = _hi_lo(jnp.where(keep_a, sp, 0.0))
            return c

        lax.fori_loop(0, i + 4, step, 0)
        dq_ref[...] = dqa_ref[...].astype(MM)

        @pl.when(i == nqb - 1)
        def _():
            dk_ref[...] = dka_ref[...].astype(MM)
            dv_ref[...] = dva_ref[...].astype(MM)

        if ns:
            pl.when(i == nqb - 1)(sc.finish)

    tile = _row_tile_spec(ATT_DIM, TQ)
    full = pl.BlockSpec((SEQ, ATT_DIM), lambda i: (0, 0))
    out = jax.ShapeDtypeStruct((SEQ, ATT_DIM), MM)
    return pl.pallas_call(
        body, name="attn_bwd", grid=(nqb,), out_shape=[out, out, out] + _Scatter.out_shapes(scatter),
        in_specs=[tile, full, full, tile, pl.BlockSpec((8, TQ, 128), lambda i: (0, i, 0))] + [ANY] * ns,
        out_specs=[tile, full, full] + [ANY] * ns,
        scratch_shapes=[pltpu.VMEM((TQ, ATT_DIM), F32), pltpu.VMEM((SEQ, ATT_DIM), F32),
                        pltpu.VMEM((SEQ, ATT_DIM), F32)] + [pltpu.VMEM((8, TQ, 128), F32)] * 5
                       + [pltpu.VMEM((8, TQ, 256), MM)] * 2 + [pltpu.VMEM((8, TQ, 128), MM)] * 2
                       + (_Scatter.scratch(ns) if ns else []),
        compiler_params=_params(("arbitrary",)),
    )(q, k, v, datt, rc, *scatter)


def in_proj_bwd(dproj, w_in_g, x, dx2, g1, scatter=()):
    ns = len(scatter)
    nt = SEQ // TM

    def body(*refs):
        dp_ref, w_ref, x_ref, dx2_ref, g_ref = refs[:5]
        dx_ref, dg_ref = refs[5 + ns:7 + ns]
        if ns:
            sc = _Scatter(refs[5:5 + ns], refs[7 + ns:7 + 2 * ns], refs[7 + 2 * ns:])
            pl.when(pl.program_id(0) == 0)(sc.start)
        dh = None
        for j in range(N_CHIPS):
            t = _dot_nt(dp_ref[:, j * IN_SHARD:(j + 1) * IN_SHARD], w_ref[j])
            dh = t if dh is None else dh + t
        n1, r1 = _rms(x_ref[...])
        dx_ref[...] = dx2_ref[...] + _rms_bwd(dh * g_ref[...], n1, r1)
        _acc_rows(dg_ref, jnp.sum(dh * n1, axis=0, keepdims=True), pl.program_id(0) == 0)
        if ns:
            pl.when(pl.program_id(0) == nt - 1)(sc.finish)

    vec = _full_spec((1, D_MODEL))
    return pl.pallas_call(
        body, name="in_proj_bwd", grid=(nt,),
        out_shape=[jax.ShapeDtypeStruct((SEQ, D_MODEL), F32), jax.ShapeDtypeStruct((1, D_MODEL), F32)]
                  + _Scatter.out_shapes(scatter),
        in_specs=[_row_tile_spec(IN_COLS), _full_spec(w_in_g.shape), _row_tile_spec(D_MODEL),
                  _row_tile_spec(D_MODEL), vec] + [ANY] * ns,
        out_specs=[_row_tile_spec(D_MODEL), vec] + [ANY] * ns,
        scratch_shapes=_Scatter.scratch(ns) if ns else [],
        compiler_params=_params(("arbitrary",)),
    )(dproj, w_in_g, x, dx2, g1, *scatter)


def weight_grad(a, b, name, col_sharded, tk=None):
    kin, n = a.shape[1], b.shape[1]

    def body(a_ref, b_ref, o_ref):
        if col_sharded:
            o_ref[0, 0] = _dot_tn(a_ref[...], b_ref[...])
        else:
            o_ref[...] = _dot_tn(a_ref[...], b_ref[...])

    if col_sharded:
        kh, ns = kin // 2, n // N_CHIPS
        out = jax.ShapeDtypeStruct((N_CHIPS, 2, kh, ns), F32)
        grid = (2, N_CHIPS)
        in_specs = [pl.BlockSpec((SEQ, kh), lambda h, j: (0, h)), pl.BlockSpec((SEQ, ns), lambda h, j: (0, j))]
        out_spec = pl.BlockSpec((1, 1, kh, ns), lambda h, j: (j, h, 0, 0))
        sem = ("arbitrary", "arbitrary")
    else:
        out = jax.ShapeDtypeStruct((kin, n), F32)
        grid = (kin // tk,)
        in_specs = [pl.BlockSpec((SEQ, tk), lambda r: (0, r)), pl.BlockSpec((SEQ, n), lambda r: (0, 0))]
        out_spec = pl.BlockSpec((tk, n), lambda r: (r, 0))
        sem = ("arbitrary",)
    res = pl.pallas_call(
        body, name=name, grid=grid, out_shape=out, in_specs=in_specs, out_specs=out_spec,
        compiler_params=_params(sem),
    )(a, b)
    if not col_sharded:
        res = res.reshape(N_CHIPS, 2, kin // (2 * N_CHIPS), n)
    return res


def _place():
    x, y, c = lax.axis_index("x"), lax.axis_index("y"), lax.axis_index("c")
    chips = [(1 - x, y), (x, 1 - y), (1 - x, 1 - y)]
    return x, y, c, chips


def _rcopy(src, dst, send_sem, recv_sem, dev):
    return pltpu.make_async_remote_copy(src_ref=src, dst_ref=dst, send_sem=send_sem, recv_sem=recv_sem,
                                        device_id=dev, device_id_type=MESH)


class _Gather:
    def __init__(self, shapes, w, o, scratch):
        self.n, self.shapes, self.w, self.o = len(w), shapes, w, o
        self.send, self.recv, self.fsend, self.frecv, self.loc_in, self.loc_out = scratch[:6]
        self.stage = scratch[6:]
        self.x, self.y, self.c, self.chips = _place()
        self.me = 2 * self.x + self.y
        self.sib = (self.x, self.y, 1 - self.c)
        self.pairs = [(j, t) for j in range(3) for t in range(self.n)]

    @staticmethod
    def scratch(shards):
        n = len(shards)
        sems = pltpu.SemaphoreType.DMA
        return ([sems((3 * n,)), sems((3 * n,)), sems((3 * n,)), sems((3 * n,)), sems((n,)), sems((n,))]
                + [pltpu.VMEM(s.shape, s.dtype) for s in shards])

    def _half(self, t, k, cc):
        rh = self.shapes[t][0] // 2
        return self.o[t].at[k, pl.ds(cc * rh, rh), :]

    def _chip(self, j):
        cx, cy = self.chips[j]
        return 2 * cx + cy, (cx, cy, self.c)

    def local_in(self, t):
        return pltpu.make_async_copy(self.w[t], self.stage[t], self.loc_in.at[t])

    def local_out(self, t):
        return pltpu.make_async_copy(self.stage[t], self.o[t].at[self.me], self.loc_out.at[t])

    def first(self, j, t):
        rh = self.shapes[t][0] // 2
        i = j * self.n + t
        return _rcopy(self.w[t].at[pl.ds(self.c * rh, rh), :], self._half(t, self.me, self.c),
                      self.send.at[i], self.recv.at[i], self._chip(j)[1])

    def arrived(self, j, t):
        k, dev = self._chip(j)
        i = j * self.n + t
        blk = self._half(t, k, self.c)
        return _rcopy(blk, blk, self.send.at[i], self.recv.at[i], dev)

    def passed(self, j, t, cc):
        i = j * self.n + t
        blk = self._half(t, self._chip(j)[0], cc)
        return _rcopy(blk, blk, self.fsend.at[i], self.frecv.at[i], self.sib)

    def start(self):
        for t in range(self.n):
            self.local_in(t).start()
        for j, t in self.pairs:
            self.first(j, t).start()

    def forward(self):
        for t in range(self.n):
            self.local_in(t).wait()
            self.local_out(t).start()
        for j, t in self.pairs:
            self.arrived(j, t).wait_recv()
            self.passed(j, t, self.c).start()

    def finish(self):
        for j, t in self.pairs:
            self.passed(j, t, 1 - self.c).wait_recv()
        for j, t in self.pairs:
            self.first(j, t).wait_send()
            self.passed(j, t, self.c).wait_send()
        for t in range(self.n):
            self.local_out(t).wait()


def all_gather_weights(shards, small):
    n = len(shards)
    shapes = [s.shape for s in shards]

    def body(*refs):
        w = refs[:n]
        sm = refs[n]
        o = refs[n + 1:2 * n + 1]
        osm = refs[2 * n + 1]
        ssend, srecv, sloc = refs[2 * n + 2:2 * n + 5]
        g = _Gather(shapes, w, o, refs[2 * n + 5:])
        own = pltpu.make_async_copy(sm, osm.at[g.me], sloc)
        own.start()
        g.start()
        small_cps = [_rcopy(sm, osm.at[g.me], ssend.at[j], srecv.at[j], g._chip(j)[1]) for j in range(3)]
        for cp in small_cps:
            cp.start()
        g.forward()
        g.finish()
        for j in range(3):
            k, dev = g._chip(j)
            _rcopy(sm, osm.at[k], ssend.at[j], srecv.at[j], dev).wait_recv()
            small_cps[j].wait_send()
        own.wait()

    out_shape = [jax.ShapeDtypeStruct((N_CHIPS,) + s.shape, s.dtype) for s in shards]
    out_shape.append(jax.ShapeDtypeStruct((N_CHIPS,) + small.shape, small.dtype))
    sems = pltpu.SemaphoreType.DMA
    return pl.pallas_call(
        body, name="all_gather_weights", out_shape=out_shape,
        in_specs=[ANY] * (n + 1), out_specs=[ANY] * (n + 1),
        scratch_shapes=[sems((3,)), sems((3,)), sems] + _Gather.scratch(shards),
        compiler_params=_params(),
    )(*shards, small)


def sibling_exchange(grads, name):
    n = len(grads)

    def body(*refs):
        g = refs[:n]
        o = refs[n:2 * n]
        send, recv = refs[2 * n:]
        x, y, c, _ = _place()
        cps = [_rcopy(g[t].at[:, 1 - c], o[t], send.at[t], recv.at[t], (x, y, 1 - c)) for t in range(n)]
        for cp in cps:
            cp.start()
        for cp in cps:
            cp.wait()

    sems = pltpu.SemaphoreType.DMA
    return pl.pallas_call(
        body, name=name,
        out_shape=[jax.ShapeDtypeStruct((a.shape[0],) + a.shape[2:], a.dtype) for a in grads],
        in_specs=[ANY] * n, out_specs=[ANY] * n, scratch_shapes=[sems((n,)), sems((n,))],
    )(*grads)


class _Scatter:
    def __init__(self, p, o, sems):
        self.n, self.p, self.o = len(p), p, o
        self.send, self.recv = sems
        _, _, self.c, self.chips = _place()

    @staticmethod
    def scratch(n):
        sems = pltpu.SemaphoreType.DMA
        return [sems((3 * n,)), sems((3 * n,))]

    @staticmethod
    def out_shapes(parts):
        return [jax.ShapeDtypeStruct((3,) + a.shape[1:], a.dtype) for a in parts]

    def copies(self):
        cps = []
        for j, (cx, cy) in enumerate(self.chips):
            for t in range(self.n):
                i = j * self.n + t
                cps.append(_rcopy(self.p[t].at[2 * cx + cy], self.o[t].at[j], self.send.at[i], self.recv.at[i],
                                  (cx, cy, self.c)))
        return cps

    def start(self):
        for cp in self.copies():
            cp.start()

    def finish(self):
        for cp in self.copies():
            cp.wait()


def sibling_swap(halves, name):
    n = len(halves)

    def body(*refs):
        h = refs[:n]
        o = refs[n:2 * n]
        send, recv = refs[2 * n:]
        x, y, c, _ = _place()
        cps = [_rcopy(h[t], o[t], send.at[t], recv.at[t], (x, y, 1 - c)) for t in range(n)]
        for cp in cps:
            cp.start()
        for cp in cps:
            cp.wait()

    sems = pltpu.SemaphoreType.DMA
    return pl.pallas_call(
        body, name=name, out_shape=[jax.ShapeDtypeStruct(a.shape, a.dtype) for a in halves],
        in_specs=[ANY] * n, out_specs=[ANY] * n, scratch_shapes=[sems((n,)), sems((n,))],
    )(*halves)


def small_all_reduce(ddw, v512, v1024):
    rows, width = PACK_ROWS, 512
    n512, n1024 = len(VEC512), len(VEC1024)

    def body(*refs):
        ddw_ref = refs[0]
        a_refs = refs[1:1 + n512]
        b_refs = refs[1 + n512:1 + n512 + n1024]
        o_ref, p_ref, gath_ref, send, recv = refs[1 + n512 + n1024:]
        p_ref[...] = jnp.zeros_like(p_ref)
        p_ref[0:32, :] = ddw_ref[...]
        for i, r in enumerate(a_refs):
            p_ref[32 + i:33 + i, :] = r[...]
        for i, r in enumerate(b_refs):
            base = 32 + n512 + 2 * i
            p_ref[base:base + 1, :] = r[:, 0:512]
            p_ref[base + 1:base + 2, :] = r[:, 512:1024]
        x, y, c, _ = _place()
        me = 4 * x + 2 * y + c
        gath_ref[me] = p_ref[...]
        cps = []
        for k in range(1, 8):
            dx, dy, dc = (k >> 2) & 1, (k >> 1) & 1, k & 1
            px = 1 - x if dx else x
            py = 1 - y if dy else y
            pc = 1 - c if dc else c
            cps.append(_rcopy(p_ref, gath_ref.at[me], send.at[k - 1], recv.at[k - 1], (px, py, pc)))
        for cp in cps:
            cp.start()
        for k in range(1, 8):
            dx, dy, dc = (k >> 2) & 1, (k >> 1) & 1, k & 1
            px = 1 - x if dx else x
            py = 1 - y if dy else y
            pc = 1 - c if dc else c
            _rcopy(p_ref, gath_ref.at[4 * px + 2 * py + pc], send.at[k - 1], recv.at[k - 1], (px, py, pc)).wait_recv()
        for cp in cps:
            cp.wait_send()
        total = gath_ref[0]
        for d in range(1, 8):
            total = total + gath_ref[d]
        o_ref[...] = total

    sems = pltpu.SemaphoreType.DMA
    n_in = 1 + n512 + n1024
    return pl.pallas_call(
        body, name="small_all_reduce", out_shape=jax.ShapeDtypeStruct((rows, width), F32),
        in_specs=[VMEM_SPEC] * n_in, out_specs=VMEM_SPEC,
        scratch_shapes=[pltpu.VMEM((rows, width), F32), pltpu.VMEM((8, rows, width), F32), sems((7,)), sems((7,))],
    )(ddw, *[v512[n] for n in VEC512], *[v1024[n] for n in VEC1024])


def _row_block(r):
    for tr in (512, 352, 256, 128):
        if r % tr == 0:
            return tr
    return r


def add_halves(g, recv, name):
    _, _, r, w = g.shape
    tr = _row_block(r)

    def body(g0_ref, g1_ref, r_ref, ob_ref, own_ref):
        k = pl.program_id(1)
        c = lax.axis_index("c")
        me = 2 * lax.axis_index("x") + lax.axis_index("y")
        t = jnp.where(c == 0, g0_ref[0, 0], g1_ref[0, 0]) + r_ref[0]
        ob_ref[0] = t.astype(MM)
        mine = jnp.where(k == me, t, 0.0)

        @pl.when(k == 0)
        def _():
            own_ref[...] = mine

        @pl.when(k != 0)
        def _():
            own_ref[...] += mine

    return pl.pallas_call(
        body, name=name, grid=(r // tr, N_CHIPS),
        in_specs=[pl.BlockSpec((1, 1, tr, w), lambda i, k: (k, 0, i, 0)),
                  pl.BlockSpec((1, 1, tr, w), lambda i, k: (k, 1, i, 0)),
                  pl.BlockSpec((1, tr, w), lambda i, k: (k, i, 0))],
        out_specs=[pl.BlockSpec((1, tr, w), lambda i, k: (k, i, 0)),
                   pl.BlockSpec((tr, w), lambda i, k: (i, 0))],
        out_shape=(jax.ShapeDtypeStruct((N_CHIPS, r, w), MM), jax.ShapeDtypeStruct((r, w), F32)),
        compiler_params=_params(("arbitrary", "arbitrary")),
    )(g, g, recv)


def sum_parts(own, rin, name):
    _, r, w = rin.shape
    tr = _row_block(r)

    def body(o_ref, r_ref, out_ref):
        out_ref[...] = ((o_ref[...] + r_ref[0].astype(F32)) + r_ref[1].astype(F32)) + r_ref[2].astype(F32)

    return pl.pallas_call(
        body, name=name, grid=(r // tr,), out_shape=jax.ShapeDtypeStruct((r, w), F32),
        in_specs=[pl.BlockSpec((tr, w), lambda i: (i, 0)), pl.BlockSpec((3, tr, w), lambda i: (0, i, 0))],
        out_specs=pl.BlockSpec((tr, w), lambda i: (i, 0)),
        compiler_params=_params(("arbitrary",)),
    )(own, rin)


def _adamw_math(w, g, m, v):
    mn = ADAM_B1 * m + (1.0 - ADAM_B1) * g
    vn = ADAM_B2 * v + (1.0 - ADAM_B2) * (g * g)
    m_hat = mn / (1.0 - ADAM_B1 ** ADAM_STEP)
    v_hat = vn / (1.0 - ADAM_B2 ** ADAM_STEP)
    return -ADAM_LR * (m_hat / (jnp.sqrt(v_hat) + ADAM_EPS) + ADAM_WD * w), mn, vn


def adamw(w, mine, other, m, v, name):
    r, c = w.shape
    rh = r // 2
    tr = _row_block(rh)
    if c >= 1024 and tr % 512 == 0:
        tr = 256
    nb = rh // tr

    def body(w_ref, a_ref, b_ref, m_ref, v_ref, go_ref, d_ref, mo_ref, vo_ref):
        gv = jnp.where(lax.axis_index("c") == pl.program_id(0), a_ref[...], b_ref[...])
        go_ref[...] = gv
        d_ref[...], mo_ref[...], vo_ref[...] = _adamw_math(w_ref[...], gv, m_ref[...], v_ref[...])

    spec = pl.BlockSpec((tr, c), lambda h, i: (h * nb + i, 0))
    half = pl.BlockSpec((tr, c), lambda h, i: (i, 0))
    out = jax.ShapeDtypeStruct((r, c), F32)
    return pl.pallas_call(
        body, name=name, grid=(2, nb), out_shape=(out, out, out, out),
        in_specs=[spec, half, half, spec, spec], out_specs=[spec] * 4,
        compiler_params=_params(("arbitrary", "arbitrary")),
    )(w, mine, other, m, v)


def adamw_small(gsum, params):
    names = list(params)
    flat = [a for n in names for a in params[n]]

    def body(*refs):
        g_ref = refs[0]
        ins = refs[1:1 + 3 * len(names)]
        outs = refs[1 + 3 * len(names):]
        me = 2 * lax.axis_index("x") + lax.axis_index("y")
        for i, n in enumerate(names):
            w_ref, m_ref, v_ref = ins[3 * i:3 * i + 3]
            go_ref, d_ref, mo_ref, vo_ref = outs[4 * i:4 * i + 4]
            if n == "conv_dw_w":
                gv = jnp.zeros((CONV_WIDTH, 128), F32)
                for k in range(N_CHIPS):
                    gv = gv + jnp.where(me == k, g_ref[0:CONV_WIDTH, 128 * k:128 * (k + 1)], 0.0)
            elif n in VEC512:
                r0 = 32 + VEC512.index(n)
                gv = g_ref[r0:r0 + 1, :]
            else:
                r0 = 32 + len(VEC512) + 2 * VEC1024.index(n)
                gv = jnp.concatenate([g_ref[r0:r0 + 1, :], g_ref[r0 + 1:r0 + 2, :]], axis=1)
            go_ref[...] = gv
            d_ref[...], mo_ref[...], vo_ref[...] = _adamw_math(w_ref[...], gv, m_ref[...], v_ref[...])

    out_shape = [jax.ShapeDtypeStruct(params[n][0].shape, F32) for n in names for _ in range(4)]
    res = pl.pallas_call(
        body, name="adamw_small", out_shape=out_shape,
        in_specs=[VMEM_SPEC] * (1 + len(flat)), out_specs=[VMEM_SPEC] * len(out_shape),
        compiler_params=_params(),
    )(gsum, *flat)
    return {n: res[4 * i:4 * i + 4] for i, n in enumerate(names)}


REST = ("w_ffn_up", "w_ffn_down", "w_out", "w_conv_branch", "w_att_branch")
VEC512 = ("conv_dw_b", "conv_ln_g", "conv_ln_b")
VEC1024 = ("norm_mix_pre", "b_conv_branch", "norm_mix_post", "norm_ffn_pre", "norm_ffn_post")
PACK_ROWS = 48


def kernel(x, norm_mix_pre, w_in, conv_dw_w, conv_dw_b, conv_ln_g, conv_ln_b, w_conv_branch, b_conv_branch, w_att_branch, w_out, norm_mix_post, norm_ffn_pre, w_ffn_up, w_ffn_down, norm_ffn_post, loss_target, m_norm_mix_pre, m_w_in, m_conv_dw_w, m_conv_dw_b, m_conv_ln_g, m_conv_ln_b, m_w_conv_branch, m_b_conv_branch, m_w_att_branch, m_w_out, m_norm_mix_post, m_norm_ffn_pre, m_w_ffn_up, m_w_ffn_down, m_norm_ffn_post, v_norm_mix_pre, v_w_in, v_conv_dw_w, v_conv_dw_b, v_conv_ln_g, v_conv_ln_b, v_w_conv_branch, v_b_conv_branch, v_w_att_branch, v_w_out, v_norm_mix_post, v_norm_ffn_pre, v_w_ffn_up, v_w_ffn_down, v_norm_ffn_post):
    weights = dict(norm_mix_pre=norm_mix_pre, w_in=w_in, conv_dw_w=conv_dw_w, conv_dw_b=conv_dw_b, conv_ln_g=conv_ln_g, conv_ln_b=conv_ln_b, w_conv_branch=w_conv_branch, b_conv_branch=b_conv_branch, w_att_branch=w_att_branch, w_out=w_out, norm_mix_post=norm_mix_post, norm_ffn_pre=norm_ffn_pre, w_ffn_up=w_ffn_up, w_ffn_down=w_ffn_down, norm_ffn_post=norm_ffn_post)
    mom = dict(norm_mix_pre=m_norm_mix_pre, w_in=m_w_in, conv_dw_w=m_conv_dw_w, conv_dw_b=m_conv_dw_b, conv_ln_g=m_conv_ln_g, conv_ln_b=m_conv_ln_b, w_conv_branch=m_w_conv_branch, b_conv_branch=m_b_conv_branch, w_att_branch=m_w_att_branch, w_out=m_w_out, norm_mix_post=m_norm_mix_post, norm_ffn_pre=m_norm_ffn_pre, w_ffn_up=m_w_ffn_up, w_ffn_down=m_w_ffn_down, norm_ffn_post=m_norm_ffn_post)
    var = dict(norm_mix_pre=v_norm_mix_pre, w_in=v_w_in, conv_dw_w=v_conv_dw_w, conv_dw_b=v_conv_dw_b, conv_ln_g=v_conv_ln_g, conv_ln_b=v_conv_ln_b, w_conv_branch=v_w_conv_branch, b_conv_branch=v_b_conv_branch, w_att_branch=v_w_att_branch, w_out=v_w_out, norm_mix_post=v_norm_mix_post, norm_ffn_pre=v_norm_ffn_pre, w_ffn_up=v_w_ffn_up, w_ffn_down=v_w_ffn_down, norm_ffn_post=v_norm_ffn_post)
    order = list(weights)
    grads, deltas, new_m, new_v = {}, {}, {}, {}
    xs = x.reshape(SEQ, D_MODEL)
    tgt = loss_target.reshape(SEQ, D_MODEL)
    row = lambda a: a.reshape(1, -1)
    g1, g2, g3, g4 = (row(weights[n]) for n in ("norm_mix_pre", "norm_mix_post", "norm_ffn_pre", "norm_ffn_post"))
    ln_g, ln_b = row(conv_ln_g), row(conv_ln_b)

    def reduce_prepare(names, partial, tag):
        from_sib = sibling_exchange([partial[n] for n in names], "sibling_exchange_" + tag)
        return [add_halves(partial[n], r, "add_" + n) for n, r in zip(names, from_sib)]

    def reduce_finish(names, summed, from_chips, tag):
        halves = [sum_parts(s[1], r, "sum_" + n) for n, s, r in zip(names, summed, from_chips)]
        for n, a, b in zip(names, halves, sibling_swap(halves, "sibling_swap_" + tag)):
            grads[n], deltas[n], new_m[n], new_v[n] = adamw(weights[n], a, b, mom[n], var[n], "adamw_" + n)

    w_in_g, dw_g = all_gather_weights([w_in.astype(MM)], conv_dw_w)
    w_dw_full = jnp.concatenate([dw_g[k] for k in range(N_CHIPS)], axis=1)
    h1, ci, q, k, v, gc, ga = in_proj_fwd(xs, g1, w_in_g)
    u1, u3 = conv_fwd(ci, w_dw_full, row(conv_dw_b), ln_g, ln_b)
    att, rc, *rest = attn_fwd(q, k, v, [weights[n].astype(MM) for n in REST])
    wg = dict(zip(REST, rest))
    w_out_g = wg["w_out"].reshape(D_MODEL, D_MODEL)
    w_down_g = wg["w_ffn_down"].reshape(D_FF, D_MODEL)
    w_cb_g, w_ab_g = wg["w_conv_branch"], wg["w_att_branch"]
    co, ao, merged, mix, x2, h2 = mix_fwd(u3, att, gc, ga, xs, w_cb_g, row(b_conv_branch), w_ab_g, w_out_g, g2, g3)
    gate, up, act = ffn_up_fwd(h2, wg["w_ffn_up"])
    dff, dy, loss_parts, dg4 = ffn_down_loss(act, w_down_g, x2, tgt, g4)
    loss = lax.psum(jnp.sum(loss_parts[::8, 0]), ("x", "y", "c"))

    partial = {}
    dgu = ffn_act_bwd(dff, w_down_g, gate, up)
    partial["w_ffn_down"] = weight_grad(act, dff, "dw_ffn_down", False, tk=UP_SHARD)
    dx2, dmix, dg3, dg2 = ffn_in_bwd(dgu, wg["w_ffn_up"], x2, mix, dy, g3, g2)
    partial["w_ffn_up"] = weight_grad(h2, dgu, "dw_ffn_up", True)
    dco, dao, dgc, dga, du3, datt, dbcb = merge_bwd(dmix, w_out_g, gc, ga, co, ao, w_cb_g, w_ab_g)
    partial["w_out"] = weight_grad(merged, dmix, "dw_out", False, tk=512)
    partial["w_conv_branch"] = weight_grad(u3, dco, "dw_conv_branch", True)
    partial["w_att_branch"] = weight_grad(att, dao, "dw_att_branch", True)
    summed = reduce_prepare(REST, partial, "rest")
    dci, ddw, dbdw, dlng, dlnb = conv_bwd(du3, u1, ci, w_dw_full, ln_g, ln_b)
    dq, dk, dv, *from_chips = attn_bwd(q, k, v, datt, rc, [s[0] for s in summed])
    reduce_finish(REST, summed, from_chips, "rest")
    dproj = jnp.concatenate([dci, dq, dk, dv, dgc, dga], axis=1)
    partial["w_in"] = weight_grad(h1, dproj, "dw_in", True)
    summed = reduce_prepare(("w_in",), partial, "w_in")
    grad_x, dg1, *from_chips = in_proj_bwd(dproj, w_in_g, xs, dx2, g1, [summed[0][0]])
    reduce_finish(("w_in",), summed, from_chips, "w_in")

    v512 = dict(conv_dw_b=dbdw, conv_ln_g=dlng, conv_ln_b=dlnb)
    v1024 = dict(norm_mix_pre=dg1, b_conv_branch=dbcb, norm_mix_post=dg2, norm_ffn_pre=dg3, norm_ffn_post=dg4)
    gsum = small_all_reduce(ddw, v512, v1024)
    as_rows = lambda n, a: a if n == "conv_dw_w" else a.reshape(1, -1)
    small_names = ("conv_dw_w",) + VEC512 + VEC1024
    small = adamw_small(gsum, {n: tuple(as_rows(n, d[n]) for d in (weights, mom, var)) for n in small_names})
    for n in small_names:
        grads[n], deltas[n], new_m[n], new_v[n] = (a.reshape(weights[n].shape) for a in small[n])

    return (loss, grad_x.reshape(1, SEQ, D_MODEL), *[grads[n] for n in order], *[deltas[n] for n in order],
            *[new_m[n] for n in order], *[new_v[n] for n in order])
```

```python
import jax
import jax.numpy as jnp
from jax import lax
from jax.experimental import pallas as pl
from jax.experimental.pallas import tpu as pltpu

F32 = jnp.float32
MM = jnp.bfloat16

SEQ = 2048
D_MODEL = 1024
CONV_DIM = 512
ATT_DIM = 512
CONV_WIDTH = 31
D_FF = 2816
IN_COLS = 2 * CONV_DIM + 3 * ATT_DIM + 2 * D_MODEL
N_CHIPS = 4
IN_SHARD = IN_COLS // N_CHIPS
UP_SHARD = 2 * D_FF // N_CHIPS
BR_SHARD = D_MODEL // N_CHIPS
EPS = 1e-6
ATT_SCALE = 0.125

TM = 256
TQ = 128
CONV_TILE = 64
CONV_WIN = CONV_TILE + 32
VMEM_LIMIT = 56 * 1024 * 1024

ADAM_LR = 0.001
ADAM_B1 = 0.9
ADAM_B2 = 0.999
ADAM_EPS = 1e-08
ADAM_WD = 0.01
ADAM_STEP = 10

MESH = pl.DeviceIdType.MESH
ANY = pl.BlockSpec(memory_space=pl.ANY)
VMEM_SPEC = pl.BlockSpec(memory_space=pltpu.VMEM)

NT_DIMS = (((1,), (1,)), ((), ()))
TN_DIMS = (((0,), (0,)), ((), ()))

IN_PIECES = (("ci", 0, 1024), ("q", 1024, 1536), ("k", 1536, 2048), ("v", 2048, 2560),
             ("gc", 2560, 3584), ("ga", 3584, 4608))


def _params(sem=None, vmem=VMEM_LIMIT):
    return pltpu.CompilerParams(dimension_semantics=sem, vmem_limit_bytes=vmem)


def _dot(a, b):
    return jnp.dot(a, b, preferred_element_type=F32)


def _dot_nt(a, b):
    return lax.dot_general(a, b, NT_DIMS, preferred_element_type=F32)


def _dot_tn(a, b):
    return lax.dot_general(a, b, TN_DIMS, preferred_element_type=F32)


def _sigmoid(x):
    return 1.0 / (1.0 + jnp.exp(-x))


def _rms(x):
    r = lax.rsqrt(jnp.mean(x * x, axis=-1, keepdims=True) + EPS)
    return x * r, r


def _rms_bwd(dy_g, n, r):
    return r * (dy_g - n * jnp.mean(dy_g * n, axis=-1, keepdims=True))


def _row_tile_spec(width, tm=TM):
    return pl.BlockSpec((tm, width), lambda i: (i, 0))


def _full_spec(shape):
    nd = len(shape)
    return pl.BlockSpec(shape, lambda *_: (0,) * nd)


def _acc_rows(ref, val, first):
    @pl.when(first)
    def _():
        ref[...] = val

    @pl.when(jnp.logical_not(first))
    def _():
        ref[...] += val


def in_proj_fwd(x, g1, w_in_g):
    def body(x_ref, g_ref, w_ref, h_ref, ci_ref, q_ref, k_ref, v_ref, gc_ref, ga_ref):
        n, _ = _rms(x_ref[...])
        h = (n * g_ref[...]).astype(MM)
        h_ref[...] = h
        outs = dict(ci=ci_ref, q=q_ref, k=k_ref, v=v_ref, gc=gc_ref, ga=ga_ref)
        for j in range(N_CHIPS):
            p = _dot(h, w_ref[j])
            g0 = j * IN_SHARD
            for name, s, e in IN_PIECES:
                lo, hi = max(s, g0), min(e, g0 + IN_SHARD)
                if lo < hi:
                    ref = outs[name]
                    part = p[:, lo - g0:hi - g0]
                    if name == "q":
                        part = part * ATT_SCALE
                    ref[:, lo - s:hi - s] = part.astype(ref.dtype)

    out_shape = (
        jax.ShapeDtypeStruct((SEQ, D_MODEL), MM),
        jax.ShapeDtypeStruct((SEQ, 2 * CONV_DIM), F32),
        jax.ShapeDtypeStruct((SEQ, ATT_DIM), MM),
        jax.ShapeDtypeStruct((SEQ, ATT_DIM), MM),
        jax.ShapeDtypeStruct((SEQ, ATT_DIM), MM),
        jax.ShapeDtypeStruct((SEQ, D_MODEL), F32),
        jax.ShapeDtypeStruct((SEQ, D_MODEL), F32),
    )
    return pl.pallas_call(
        body, name="in_proj_fwd", grid=(SEQ // TM,), out_shape=out_shape,
        in_specs=[_row_tile_spec(D_MODEL), _full_spec((1, D_MODEL)), _full_spec(w_in_g.shape)],
        out_specs=[_row_tile_spec(s.shape[1]) for s in out_shape],
        compiler_params=_params(("arbitrary",)),
    )(x, g1, w_in_g)


def _shifted_sum(win, terms):
    by_rot = {}
    for m, coef in terms:
        by_rot.setdefault(m % 8, []).append((m // 8, coef))
    acc = None
    n = win.shape[0]
    for rot in sorted(by_rot):
        shifted = win if rot == 0 else pltpu.roll(win, n - rot, 0)
        for a, coef in by_rot[rot]:
            t = coef * shifted[8 * a:8 * a + CONV_TILE, :]
            acc = t if acc is None else acc + t
    return acc


def _glu_into(ci_ref, upad_ref):
    upad_ref[0:32, :] = jnp.zeros((32, CONV_DIM), F32)

    def step(i, c):
        t0 = pl.multiple_of(i * TM, TM)
        a = ci_ref[pl.ds(t0, TM), 0:CONV_DIM]
        b = ci_ref[pl.ds(t0, TM), CONV_DIM:2 * CONV_DIM]
        upad_ref[pl.ds(t0 + 32, TM), :] = a * _sigmoid(b)
        return c

    lax.fori_loop(0, SEQ // TM, step, 0)


def _layernorm_parts(u1):
    mu = jnp.mean(u1, axis=-1, keepdims=True)
    xc = u1 - mu
    rstd = lax.rsqrt(jnp.mean(xc * xc, axis=-1, keepdims=True) + EPS)
    return xc * rstd, rstd


def conv_fwd(ci, w_dw, b_dw, ln_g, ln_b):
    def body(ci_ref, w_ref, b_ref, g_ref, bb_ref, u1_ref, u3_ref, upad_ref):
        _glu_into(ci_ref, upad_ref)

        def step(i, c):
            t0 = pl.multiple_of(i * CONV_TILE, CONV_TILE)
            win = upad_ref[pl.ds(t0, CONV_WIN), :]
            u1 = _shifted_sum(win, [(j + 2, w_ref[j:j + 1, :]) for j in range(CONV_WIDTH)]) + b_ref[...]
            u1_ref[pl.ds(t0, CONV_TILE), :] = u1
            xh, _ = _layernorm_parts(u1)
            u2 = xh * g_ref[...] + bb_ref[...]
            u3_ref[pl.ds(t0, CONV_TILE), :] = (u2 * _sigmoid(u2)).astype(MM)
            return c

        lax.fori_loop(0, SEQ // CONV_TILE, step, 0)

    return pl.pallas_call(
        body, name="conv_fwd",
        out_shape=(jax.ShapeDtypeStruct((SEQ, CONV_DIM), F32), jax.ShapeDtypeStruct((SEQ, CONV_DIM), MM)),
        in_specs=[VMEM_SPEC] * 5, out_specs=[VMEM_SPEC] * 2,
        scratch_shapes=[pltpu.VMEM((SEQ + 32, CONV_DIM), F32)],
        compiler_params=_params(),
    )(ci, w_dw, b_dw, ln_g, ln_b)


def _softplus(z):
    return jnp.maximum(z, 0.0) + jnp.log(1.0 + jnp.exp(-jnp.abs(z)))


def _cumsum_weights(suffix, with_total):
    n = 256 if with_total else 128
    r = lax.broadcasted_iota(jnp.int32, (128, n), 0)
    c = lax.broadcasted_iota(jnp.int32, (128, n), 1)
    tri = (r >= c) if suffix else (r <= c)
    return jnp.logical_or(tri, c >= 128).astype(MM)


NO_SCORE = -1e30
N_KB = SEQ // TQ


def _score_bias(lane, row, i, j):
    keep = jnp.logical_and(i >= 0, jnp.logical_or(j < i, lane < row))
    return jnp.where(keep, 0.0, NO_SCORE)


def _block_pipeline(n_stages, descending, step, on_query_block=None):
    n_lag = n_stages - 1
    none = jnp.int32(-1)

    def shift(cur, lag):
        step([cur] + [(lag[2 * s], lag[2 * s + 1]) for s in range(n_lag)])
        return (cur[0], cur[1]) + tuple(lag[:-2])

    def outer(i, lag):
        if on_query_block is not None:
            on_query_block(i)

        def inner(n, lag):
            return shift((i, i - n if descending else n), lag)
        return lax.fori_loop(0, i + 1, inner, lag)

    lag = lax.fori_loop(0, N_KB, outer, (none,) * (2 * n_lag))
    lax.fori_loop(0, n_lag, lambda n, lag: shift((none, none), lag), lag)


def _head_masks():
    lane = lax.broadcasted_iota(jnp.int32, (TQ, 128), 1)
    row = lax.broadcasted_iota(jnp.int32, (TQ, 128), 0)
    return lane, row, lane < 64


def _pick_head(x, head0, h):
    zero = jnp.zeros_like(x)
    return jnp.where(head0, x, zero) if h == 0 else jnp.where(head0, zero, x)


N_PAIRS = ATT_DIM // 128


def attn_fwd(q, k, v, gather=()):
    ng = len(gather)

    def body(*refs):
        q_ref, k_ref, v_ref = refs[:3]
        o_ref, rc_ref = refs[3 + ng:5 + ng]
        acc_ref, r_ref, z_ref, spb_ref, ab_ref = refs[5 + 2 * ng:10 + 2 * ng]
        if ng:
            ag = _Gather([s.shape for s in gather], refs[3:3 + ng], refs[5 + ng:5 + 2 * ng], refs[10 + 2 * ng:])
            ag.start()
        lane, row, head0 = _head_masks()
        w = _cumsum_weights(suffix=True, with_total=True)
        acc_ref[...] = jnp.zeros_like(acc_ref)
        r_ref[...] = jnp.zeros_like(r_ref)
        rc_ref[...] = jnp.zeros_like(rc_ref)
        z_ref[...] = jnp.full(z_ref.shape, NO_SCORE, F32)
        spb_ref[...] = jnp.zeros_like(spb_ref)
        ab_ref[...] = jnp.zeros_like(ab_ref)

        def step(pairs):
            (i1, j1), (i2, j2), (i3, j3) = pairs
            q1, k1, q2, q3, k3 = (pl.multiple_of(jnp.maximum(b, 0) * TQ, TQ) for b in (i1, j1, i2, i3, j3))
            bias1 = _score_bias(lane, row, i1, j1)
            first2 = j2 == i2
            rc_rows = rc_ref[pl.ds(q2, TQ), :]
            for p in range(N_PAIRS):
                cols = slice(128 * p, 128 * (p + 1))
                qb = q_ref[pl.ds(q1, TQ), cols]
                kb = k_ref[pl.ds(k1, TQ), cols]
                vb = v_ref[pl.ds(k3, TQ), cols]
                for h in range(2):
                    hh = 2 * p + h
                    acc_ref[pl.ds(q3, TQ), cols] += _dot(ab_ref[hh], _pick_head(vb, head0, h))
                    r = _dot(spb_ref[hh], w)
                    r_in = jnp.where(first2, 0.0, r_ref[hh])
                    ab_ref[hh] = jnp.exp(z_ref[hh] - (r[:, :128] + r_in)).astype(MM)
                    rc_rows = jnp.where(jnp.logical_and(lane == 16 * hh + j2, i2 >= 0), r_in, rc_rows)
                    r_ref[hh] = r_in + r[:, 128:]
                    z = _dot_nt(_pick_head(qb, head0, h), kb) + bias1
                    z_ref[hh] = z
                    spb_ref[hh] = _softplus(z).astype(MM)
            rc_ref[pl.ds(q2, TQ), :] = rc_rows

        if ng:
            _block_pipeline(3, True, step, lambda i: pl.when(i == N_KB - 2)(ag.forward))
        else:
            _block_pipeline(3, True, step)
        o_ref[...] = acc_ref[...].astype(MM)
        if ng:
            ag.finish()

    out_shape = [jax.ShapeDtypeStruct((SEQ, ATT_DIM), MM), jax.ShapeDtypeStruct((SEQ, 128), F32)]
    out_shape += [jax.ShapeDtypeStruct((N_CHIPS,) + s.shape, s.dtype) for s in gather]
    return pl.pallas_call(
        body, name="attn_fwd", out_shape=out_shape,
        in_specs=[VMEM_SPEC] * 3 + [ANY] * ng, out_specs=[VMEM_SPEC] * 2 + [ANY] * ng,
        scratch_shapes=[pltpu.VMEM((SEQ, ATT_DIM), F32), pltpu.VMEM((8, TQ, 128), F32),
                        pltpu.VMEM((8, TQ, 128), F32), pltpu.VMEM((8, TQ, 128), MM), pltpu.VMEM((8, TQ, 128), MM)]
                       + (_Gather.scratch(gather) if ng else []),
        compiler_params=_params(),
    )(q, k, v, *gather)


def mix_fwd(u3, att, gc, ga, x, w_cb_g, b_cb, w_ab_g, w_out_g, g2, g3):
    def body(u_ref, a_ref, gc_ref, ga_ref, x_ref, wcb_ref, bcb_ref, wab_ref, wout_ref, g2_ref, g3_ref,
             co_ref, ao_ref, mg_ref, mix_ref, x2_ref, h2_ref):
        u = u_ref[...]
        a = a_ref[...]
        for j in range(N_CHIPS):
            cols = slice(j * BR_SHARD, (j + 1) * BR_SHARD)
            co_ref[:, cols] = _dot(u, wcb_ref[j]) + bcb_ref[:, cols]
            ao_ref[:, cols] = _dot(a, wab_ref[j])
        merged = (_sigmoid(gc_ref[...]) * co_ref[...] + _sigmoid(ga_ref[...]) * ao_ref[...]).astype(MM)
        mg_ref[...] = merged
        mix = _dot(merged, wout_ref[...])
        mix_ref[...] = mix
        n2, _ = _rms(mix)
        x2 = x_ref[...] + n2 * g2_ref[...]
        x2_ref[...] = x2
        n3, _ = _rms(x2)
        h2_ref[...] = (n3 * g3_ref[...]).astype(MM)

    out_shape = (
        jax.ShapeDtypeStruct((SEQ, D_MODEL), F32), jax.ShapeDtypeStruct((SEQ, D_MODEL), F32),
        jax.ShapeDtypeStruct((SEQ, D_MODEL), MM), jax.ShapeDtypeStruct((SEQ, D_MODEL), F32),
        jax.ShapeDtypeStruct((SEQ, D_MODEL), F32), jax.ShapeDtypeStruct((SEQ, D_MODEL), MM),
    )
    vec = _full_spec((1, D_MODEL))
    return pl.pallas_call(
        body, name="mix_fwd", grid=(SEQ // TM,), out_shape=out_shape,
        in_specs=[_row_tile_spec(CONV_DIM), _row_tile_spec(ATT_DIM), _row_tile_spec(D_MODEL),
                  _row_tile_spec(D_MODEL), _row_tile_spec(D_MODEL), _full_spec(w_cb_g.shape), vec,
                  _full_spec(w_ab_g.shape), _full_spec(w_out_g.shape), vec, vec],
        out_specs=[_row_tile_spec(D_MODEL)] * 6,
        compiler_params=_params(("arbitrary",)),
    )(u3, att, gc, ga, x, w_cb_g, b_cb, w_ab_g, w_out_g, g2, g3)


def ffn_up_fwd(h2, w_up_g):
    def body(h_ref, wg_ref, wu_ref, gate_ref, up_ref, act_ref):
        h = h_ref[...]
        gate = _dot(h, wg_ref[0])
        up = _dot(h, wu_ref[0])
        gate_ref[...] = gate
        up_ref[...] = up
        act_ref[...] = (gate * _sigmoid(gate) * up).astype(MM)

    tile = pl.BlockSpec((TM, UP_SHARD), lambda n, i: (i, n))
    return pl.pallas_call(
        body, name="ffn_up_fwd", grid=(2, SEQ // TM),
        out_shape=(jax.ShapeDtypeStruct((SEQ, D_FF), F32), jax.ShapeDtypeStruct((SEQ, D_FF), F32),
                   jax.ShapeDtypeStruct((SEQ, D_FF), MM)),
        in_specs=[pl.BlockSpec((TM, D_MODEL), lambda n, i: (i, 0)),
                  pl.BlockSpec((1, D_MODEL, UP_SHARD), lambda n, i: (n, 0, 0)),
                  pl.BlockSpec((1, D_MODEL, UP_SHARD), lambda n, i: (n + 2, 0, 0))],
        out_specs=[tile, tile, tile],
        compiler_params=_params(("arbitrary", "arbitrary")),
    )(h2, w_up_g, w_up_g)


def ffn_down_loss(act, w_down_g, x2, target, g4):
    def body(act_ref, wd_ref, x2_ref, t_ref, g_ref, dff_ref, dy_ref, loss_ref, dg_ref):
        ff = _dot(act_ref[...], wd_ref[...])
        n4, r4 = _rms(ff)
        g4v = g_ref[...]
        err = x2_ref[...] + n4 * g4v - t_ref[...]
        row_loss = jnp.mean(err * err, axis=-1, keepdims=True)
        loss_ref[...] = jnp.zeros((8, 128), F32) + 0.5 * jnp.sum(row_loss, axis=0, keepdims=True)
        dy = err * (1.0 / D_MODEL)
        dy_ref[...] = dy
        dff_ref[...] = _rms_bwd(dy * g4v, n4, r4).astype(MM)
        _acc_rows(dg_ref, jnp.sum(dy * n4, axis=0, keepdims=True), pl.program_id(0) == 0)

    nt = SEQ // TM
    vec = _full_spec((1, D_MODEL))
    return pl.pallas_call(
        body, name="ffn_down_loss", grid=(nt,),
        out_shape=(jax.ShapeDtypeStruct((SEQ, D_MODEL), MM), jax.ShapeDtypeStruct((SEQ, D_MODEL), F32),
                   jax.ShapeDtypeStruct((nt * 8, 128), F32), jax.ShapeDtypeStruct((1, D_MODEL), F32)),
        in_specs=[_row_tile_spec(D_FF), _full_spec(w_down_g.shape), _row_tile_spec(D_MODEL),
                  _row_tile_spec(D_MODEL), vec],
        out_specs=[_row_tile_spec(D_MODEL), _row_tile_spec(D_MODEL),
                   pl.BlockSpec((8, 128), lambda i: (i, 0)), vec],
        compiler_params=_params(("arbitrary",)),
    )(act, w_down_g, x2, target, g4)


def ffn_act_bwd(dff, w_down_g, gate, up):
    def body(dff_ref, wd_ref, gate_ref, up_ref, dgu_ref):
        dact = _dot_nt(dff_ref[...], wd_ref[...])
        gate = gate_ref[...]
        sg = _sigmoid(gate)
        dgu_ref[:, 0:D_FF] = (dact * up_ref[...] * (sg * (1.0 + gate * (1.0 - sg)))).astype(MM)
        dgu_ref[:, D_FF:2 * D_FF] = (dact * (gate * sg)).astype(MM)

    return pl.pallas_call(
        body, name="ffn_act_bwd", grid=(SEQ // TM,),
        out_shape=jax.ShapeDtypeStruct((SEQ, 2 * D_FF), MM),
        in_specs=[_row_tile_spec(D_MODEL), _full_spec(w_down_g.shape), _row_tile_spec(D_FF), _row_tile_spec(D_FF)],
        out_specs=_row_tile_spec(2 * D_FF),
        compiler_params=_params(("arbitrary",)),
    )(dff, w_down_g, gate, up)


def ffn_in_bwd(dgu, w_up_g, x2, mix, dy, g3, g2):
    def body(dgu_ref, w_ref, x2_ref, mix_ref, dy_ref, g3_ref, g2_ref, dx2_ref, dmix_ref, dg3_ref, dg2_ref):
        dh2 = None
        for j in range(N_CHIPS):
            t = _dot_nt(dgu_ref[:, j * UP_SHARD:(j + 1) * UP_SHARD], w_ref[j])
            dh2 = t if dh2 is None else dh2 + t
        first = pl.program_id(0) == 0
        n3, r3 = _rms(x2_ref[...])
        dx2 = dy_ref[...] + _rms_bwd(dh2 * g3_ref[...], n3, r3)
        dx2_ref[...] = dx2
        _acc_rows(dg3_ref, jnp.sum(dh2 * n3, axis=0, keepdims=True), first)
        n2, r2 = _rms(mix_ref[...])
        dmix_ref[...] = _rms_bwd(dx2 * g2_ref[...], n2, r2).astype(MM)
        _acc_rows(dg2_ref, jnp.sum(dx2 * n2, axis=0, keepdims=True), first)

    vec = _full_spec((1, D_MODEL))
    return pl.pallas_call(
        body, name="ffn_in_bwd", grid=(SEQ // TM,),
        out_shape=(jax.ShapeDtypeStruct((SEQ, D_MODEL), F32), jax.ShapeDtypeStruct((SEQ, D_MODEL), MM),
                   jax.ShapeDtypeStruct((1, D_MODEL), F32), jax.ShapeDtypeStruct((1, D_MODEL), F32)),
        in_specs=[_row_tile_spec(2 * D_FF), _full_spec(w_up_g.shape), _row_tile_spec(D_MODEL),
                  _row_tile_spec(D_MODEL), _row_tile_spec(D_MODEL), vec, vec],
        out_specs=[_row_tile_spec(D_MODEL), _row_tile_spec(D_MODEL), vec, vec],
        compiler_params=_params(("arbitrary",)),
    )(dgu, w_up_g, x2, mix, dy, g3, g2)


def merge_bwd(dmix, w_out_g, gc, ga, co, ao, w_cb_g, w_ab_g):
    def body(dmix_ref, wout_ref, gc_ref, ga_ref, co_ref, ao_ref, wcb_ref, wab_ref,
             dco_ref, dao_ref, dgc_ref, dga_ref, du3_ref, datt_ref, dbcb_ref):
        dm = _dot_nt(dmix_ref[...], wout_ref[...])
        sgc = _sigmoid(gc_ref[...])
        sga = _sigmoid(ga_ref[...])
        dco = dm * sgc
        dao = dm * sga
        dgc_ref[...] = (dm * co_ref[...] * (sgc * (1.0 - sgc))).astype(MM)
        dga_ref[...] = (dm * ao_ref[...] * (sga * (1.0 - sga))).astype(MM)
        _acc_rows(dbcb_ref, jnp.sum(dco, axis=0, keepdims=True), pl.program_id(0) == 0)
        dco_ref[...] = dco.astype(MM)
        dao_ref[...] = dao.astype(MM)
        du3 = None
        datt = None
        for j in range(N_CHIPS):
            cols = slice(j * BR_SHARD, (j + 1) * BR_SHARD)
            t = _dot_nt(dco_ref[:, cols], wcb_ref[j])
            s = _dot_nt(dao_ref[:, cols], wab_ref[j])
            du3 = t if du3 is None else du3 + t
            datt = s if datt is None else datt + s
        du3_ref[...] = du3
        datt_ref[...] = datt.astype(MM)

    wide = _row_tile_spec(D_MODEL)
    return pl.pallas_call(
        body, name="merge_bwd", grid=(SEQ // TM,),
        out_shape=(jax.ShapeDtypeStruct((SEQ, D_MODEL), MM), jax.ShapeDtypeStruct((SEQ, D_MODEL), MM),
                   jax.ShapeDtypeStruct((SEQ, D_MODEL), MM), jax.ShapeDtypeStruct((SEQ, D_MODEL), MM),
                   jax.ShapeDtypeStruct((SEQ, CONV_DIM), F32), jax.ShapeDtypeStruct((SEQ, ATT_DIM), MM),
                   jax.ShapeDtypeStruct((1, D_MODEL), F32)),
        in_specs=[wide, _full_spec(w_out_g.shape), wide, wide, wide, wide,
                  _full_spec(w_cb_g.shape), _full_spec(w_ab_g.shape)],
        out_specs=[wide, wide, wide, wide, _row_tile_spec(CONV_DIM), _row_tile_spec(ATT_DIM),
                   _full_spec((1, D_MODEL))],
        compiler_params=_params(("arbitrary",)),
    )(dmix, w_out_g, gc, ga, co, ao, w_cb_g, w_ab_g)


def conv_bwd(du3, u1, ci, w_dw, ln_g, ln_b):
    def body(du3_ref, u1_ref, ci_ref, w_ref, g_ref, bb_ref,
             dci_ref, dw_ref, dbdw_ref, dg_ref, db_ref, upad_ref, dpad_ref, dwacc_ref, vacc_ref):
        _glu_into(ci_ref, upad_ref)
        dpad_ref[SEQ:SEQ + 32, :] = jnp.zeros((32, CONV_DIM), F32)
        dwacc_ref[...] = jnp.zeros_like(dwacc_ref)
        vacc_ref[...] = jnp.zeros_like(vacc_ref)

        def fold8(t):
            s = t[0:8, :]
            for r in range(1, CONV_TILE // 8):
                s = s + t[8 * r:8 * r + 8, :]
            return s

        def pass1(i, c):
            t0 = pl.multiple_of(i * CONV_TILE, CONV_TILE)
            xh, rstd = _layernorm_parts(u1_ref[pl.ds(t0, CONV_TILE), :])
            gv = g_ref[...]
            u2 = xh * gv + bb_ref[...]
            s2 = _sigmoid(u2)
            du2 = du3_ref[pl.ds(t0, CONV_TILE), :] * (s2 * (1.0 + u2 * (1.0 - s2)))
            wv = du2 * gv
            du1 = rstd * (wv - jnp.mean(wv, axis=-1, keepdims=True)
                          - xh * jnp.mean(wv * xh, axis=-1, keepdims=True))
            dpad_ref[pl.ds(t0, CONV_TILE), :] = du1
            vacc_ref[0] += fold8(du2 * xh)
            vacc_ref[1] += fold8(du2)
            vacc_ref[2] += fold8(du1)
            win = upad_ref[pl.ds(t0, CONV_WIN), :]
            n = win.shape[0]
            for rot in range(8):
                shifted = win if rot == 0 else pltpu.roll(win, n - rot, 0)
                for a in range(5):
                    j = 8 * a + rot - 2
                    if 0 <= j < CONV_WIDTH:
                        dwacc_ref[j] += fold8(du1 * shifted[8 * a:8 * a + CONV_TILE, :])
            return c

        lax.fori_loop(0, SEQ // CONV_TILE, pass1, 0)

        def pass2(i, c):
            t0 = pl.multiple_of(i * CONV_TILE, CONV_TILE)
            win = dpad_ref[pl.ds(t0, CONV_WIN), :]
            du0 = _shifted_sum(win, [(30 - j, w_ref[j:j + 1, :]) for j in range(CONV_WIDTH)])
            a = ci_ref[pl.ds(t0, CONV_TILE), 0:CONV_DIM]
            sb = _sigmoid(ci_ref[pl.ds(t0, CONV_TILE), CONV_DIM:2 * CONV_DIM])
            dci_ref[pl.ds(t0, CONV_TILE), 0:CONV_DIM] = (du0 * sb).astype(MM)
            dci_ref[pl.ds(t0, CONV_TILE), CONV_DIM:2 * CONV_DIM] = (du0 * a * (sb * (1.0 - sb))).astype(MM)
            return c

        lax.fori_loop(0, SEQ // CONV_TILE, pass2, 0)

        for j in range(CONV_WIDTH):
            dw_ref[j:j + 1, :] = jnp.sum(dwacc_ref[j], axis=0, keepdims=True)
        dw_ref[CONV_WIDTH:32, :] = jnp.zeros((32 - CONV_WIDTH, CONV_DIM), F32)
        dg_ref[...] = jnp.sum(vacc_ref[0], axis=0, keepdims=True)
        db_ref[...] = jnp.sum(vacc_ref[1], axis=0, keepdims=True)
        dbdw_ref[...] = jnp.sum(vacc_ref[2], axis=0, keepdims=True)

    vec = jax.ShapeDtypeStruct((1, CONV_DIM), F32)
    return pl.pallas_call(
        body, name="conv_bwd",
        out_shape=(jax.ShapeDtypeStruct((SEQ, 2 * CONV_DIM), MM), jax.ShapeDtypeStruct((32, CONV_DIM), F32),
                   vec, vec, vec),
        in_specs=[VMEM_SPEC] * 6, out_specs=[VMEM_SPEC] * 5,
        scratch_shapes=[pltpu.VMEM((SEQ + 32, CONV_DIM), F32), pltpu.VMEM((SEQ + 32, CONV_DIM), F32),
                        pltpu.VMEM((CONV_WIDTH, 8, CONV_DIM), F32), pltpu.VMEM((3, 8, CONV_DIM), F32)],
        compiler_params=_params(),
    )(du3, u1, ci, w_dw, ln_g, ln_b)


def attn_bwd(q, k, v, datt, rc, scatter=()):
    ns = len(scatter)

    def body(*refs):
        q_ref, k_ref, v_ref, do_ref, rc_ref = refs[:5]
        dq_ref, dk_ref, dv_ref = refs[5 + ns:8 + ns]
        (dqa_ref, dka_ref, dva_ref, pc_ref, z_ref, sig1_ref, sig2_ref, g_ref, spb_ref, gb_ref, ab_ref,
         dzb_ref) = refs[8 + 2 * ns:20 + 2 * ns]
        if ns:
            sc = _Scatter(refs[5:5 + ns], refs[8 + ns:8 + 2 * ns], refs[20 + 2 * ns:])
            sc.start()
        lane, row, head0 = _head_masks()
        for ref in (dqa_ref, dka_ref, dva_ref, pc_ref):
            ref[...] = jnp.zeros_like(ref)
        z_ref[...] = jnp.full(z_ref.shape, NO_SCORE, F32)
        for ref in (sig1_ref, sig2_ref, spb_ref, ab_ref, g_ref, gb_ref, dzb_ref):
            ref[...] = jnp.zeros_like(ref)
        w_suffix = _cumsum_weights(suffix=True, with_total=False)
        w_prefix = _cumsum_weights(suffix=False, with_total=True)

        def step(pairs):
            (ia, ja), (ib, jb), (ic, jc), (id_, jd) = pairs
            qa, ka, qb_, kb_, qc, kc, qd, kd = (pl.multiple_of(jnp.maximum(b, 0) * TQ, TQ)
                                                for b in (ia, ja, ib, jb, ic, jc, id_, jd))
            bias_a = _score_bias(lane, row, ia, ja)
            rc_rows = rc_ref[pl.ds(qb_, TQ), :]
            first_c = jc == 0
            for p in range(N_PAIRS):
                cols = slice(128 * p, 128 * (p + 1))
                q_a = q_ref[pl.ds(qa, TQ), cols]
                k_a = k_ref[pl.ds(ka, TQ), cols]
                do_b = do_ref[pl.ds(qb_, TQ), cols]
                v_b = v_ref[pl.ds(kb_, TQ), cols]
                do_c = do_ref[pl.ds(qc, TQ), cols]
                q_d = q_ref[pl.ds(qd, TQ), cols]
                k_d = k_ref[pl.ds(kd, TQ), cols]
                for h in range(2):
                    hh = 2 * p + h
                    dzb = dzb_ref[hh]
                    dqa_ref[pl.ds(qd, TQ), cols] += _dot(dzb, _pick_head(k_d, head0, h))
                    dka_ref[pl.ds(kd, TQ), cols] += _dot_tn(dzb, _pick_head(q_d, head0, h))
                    r = _dot(gb_ref[hh], w_prefix)
                    p_in = jnp.where(first_c, 0.0, pc_ref[hh])
                    dzb_ref[hh] = (g_ref[hh] - sig2_ref[hh] * (r[:, :128] + p_in)).astype(MM)
                    pc_ref[hh] = p_in + r[:, 128:]
                    dva_ref[pl.ds(kc, TQ), cols] += _dot_tn(ab_ref[hh], _pick_head(do_c, head0, h))
                    r_in = jnp.sum(jnp.where(lane == 16 * hh + jb, rc_rows, 0.0), axis=1, keepdims=True)
                    a = jnp.exp(z_ref[hh] - (_dot(spb_ref[hh], w_suffix) + r_in))
                    g = _dot_nt(_pick_head(do_b, head0, h), v_b) * a
                    ab_ref[hh] = a.astype(MM)
                    g_ref[hh] = g
                    gb_ref[hh] = g.astype(MM)
                    sig2_ref[hh] = sig1_ref[hh]
                    z = _dot_nt(_pick_head(q_a, head0, h), k_a) + bias_a
                    sp = _softplus(z)
                    sig1_ref[hh] = jnp.exp(z - sp)
                    z_ref[hh] = z
                    spb_ref[hh] = sp.astype(MM)

        _block_pipeline(4, False, step)
        dq_ref[...] = (dqa_ref[...] * ATT_SCALE).astype(MM)
        dk_ref[...] = dka_ref[...].astype(MM)
        dv_ref[...] = dva_ref[...].astype(MM)
        if ns:
            sc.finish()

    out = jax.ShapeDtypeStruct((SEQ, ATT_DIM), MM)
    return pl.pallas_call(
        body, name="attn_bwd", out_shape=[out, out, out] + _Scatter.out_shapes(scatter),
        in_specs=[VMEM_SPEC] * 5 + [ANY] * ns, out_specs=[VMEM_SPEC] * 3 + [ANY] * ns,
        scratch_shapes=[pltpu.VMEM((SEQ, ATT_DIM), F32)] * 3 + [pltpu.VMEM((8, TQ, 128), F32)] * 5
                       + [pltpu.VMEM((8, TQ, 128), MM)] * 4 + (_Scatter.scratch(ns) if ns else []),
        compiler_params=_params(),
    )(q, k, v, datt, rc, *scatter)


def in_proj_bwd(dproj, w_in_g, x, dx2, g1, scatter=()):
    ns = len(scatter)
    nt = SEQ // TM

    def body(*refs):
        dp_ref, w_ref, x_ref, dx2_ref, g_ref = refs[:5]
        dx_ref, dg_ref = refs[5 + ns:7 + ns]
        if ns:
            sc = _Scatter(refs[5:5 + ns], refs[7 + ns:7 + 2 * ns], refs[7 + 2 * ns:])
            pl.when(pl.program_id(0) == 0)(sc.start)
        dh = None
        for j in range(N_CHIPS):
            t = _dot_nt(dp_ref[:, j * IN_SHARD:(j + 1) * IN_SHARD], w_ref[j])
            dh = t if dh is None else dh + t
        n1, r1 = _rms(x_ref[...])
        dx_ref[...] = dx2_ref[...] + _rms_bwd(dh * g_ref[...], n1, r1)
        _acc_rows(dg_ref, jnp.sum(dh * n1, axis=0, keepdims=True), pl.program_id(0) == 0)
        if ns:
            pl.when(pl.program_id(0) == nt - 1)(sc.finish)

    vec = _full_spec((1, D_MODEL))
    return pl.pallas_call(
        body, name="in_proj_bwd", grid=(nt,),
        out_shape=[jax.ShapeDtypeStruct((SEQ, D_MODEL), F32), jax.ShapeDtypeStruct((1, D_MODEL), F32)]
                  + _Scatter.out_shapes(scatter),
        in_specs=[_row_tile_spec(IN_COLS), _full_spec(w_in_g.shape), _row_tile_spec(D_MODEL),
                  _row_tile_spec(D_MODEL), vec] + [ANY] * ns,
        out_specs=[_row_tile_spec(D_MODEL), vec] + [ANY] * ns,
        scratch_shapes=_Scatter.scratch(ns) if ns else [],
        compiler_params=_params(("arbitrary",)),
    )(dproj, w_in_g, x, dx2, g1, *scatter)


def weight_grad(a, b, name, col_sharded, tk=None):
    kin, n = a.shape[1], b.shape[1]

    def body(a_ref, b_ref, o_ref):
        if col_sharded:
            o_ref[0, 0] = _dot_tn(a_ref[...], b_ref[...])
        else:
            o_ref[...] = _dot_tn(a_ref[...], b_ref[...])

    if col_sharded:
        kh, ns = kin // 2, n // N_CHIPS
        out = jax.ShapeDtypeStruct((N_CHIPS, 2, kh, ns), F32)
        grid = (2, N_CHIPS)
        in_specs = [pl.BlockSpec((SEQ, kh), lambda h, j: (0, h)), pl.BlockSpec((SEQ, ns), lambda h, j: (0, j))]
        out_spec = pl.BlockSpec((1, 1, kh, ns), lambda h, j: (j, h, 0, 0))
        sem = ("arbitrary", "arbitrary")
    else:
        out = jax.ShapeDtypeStruct((kin, n), F32)
        grid = (kin // tk,)
        in_specs = [pl.BlockSpec((SEQ, tk), lambda r: (0, r)), pl.BlockSpec((SEQ, n), lambda r: (0, 0))]
        out_spec = pl.BlockSpec((tk, n), lambda r: (r, 0))
        sem = ("arbitrary",)
    res = pl.pallas_call(
        body, name=name, grid=grid, out_shape=out, in_specs=in_specs, out_specs=out_spec,
        compiler_params=_params(sem),
    )(a, b)
    if not col_sharded:
        res = res.reshape(N_CHIPS, 2, kin // (2 * N_CHIPS), n)
    return res


def _place():
    x, y, c = lax.axis_index("x"), lax.axis_index("y"), lax.axis_index("c")
    chips = [(1 - x, y), (x, 1 - y), (1 - x, 1 - y)]
    return x, y, c, chips


def _rcopy(src, dst, send_sem, recv_sem, dev):
    return pltpu.make_async_remote_copy(src_ref=src, dst_ref=dst, send_sem=send_sem, recv_sem=recv_sem,
                                        device_id=dev, device_id_type=MESH)


class _Gather:
    def __init__(self, shapes, w, o, scratch):
        self.n, self.shapes, self.w, self.o = len(w), shapes, w, o
        self.send, self.recv, self.fsend, self.frecv, self.loc_in, self.loc_out = scratch[:6]
        self.stage = scratch[6:]
        self.x, self.y, self.c, self.chips = _place()
        self.me = 2 * self.x + self.y
        self.sib = (self.x, self.y, 1 - self.c)
        self.pairs = [(j, t) for j in range(3) for t in range(self.n)]

    @staticmethod
    def scratch(shards):
        n = len(shards)
        sems = pltpu.SemaphoreType.DMA
        return ([sems((3 * n,)), sems((3 * n,)), sems((3 * n,)), sems((3 * n,)), sems((n,)), sems((n,))]
                + [pltpu.VMEM(s.shape, s.dtype) for s in shards])

    def _half(self, t, k, cc):
        rh = self.shapes[t][0] // 2
        return self.o[t].at[k, pl.ds(cc * rh, rh), :]

    def _chip(self, j):
        cx, cy = self.chips[j]
        return 2 * cx + cy, (cx, cy, self.c)

    def local_in(self, t):
        return pltpu.make_async_copy(self.w[t], self.stage[t], self.loc_in.at[t])

    def local_out(self, t):
        return pltpu.make_async_copy(self.stage[t], self.o[t].at[self.me], self.loc_out.at[t])

    def first(self, j, t):
        rh = self.shapes[t][0] // 2
        i = j * self.n + t
        return _rcopy(self.w[t].at[pl.ds(self.c * rh, rh), :], self._half(t, self.me, self.c),
                      self.send.at[i], self.recv.at[i], self._chip(j)[1])

    def arrived(self, j, t):
        k, dev = self._chip(j)
        i = j * self.n + t
        blk = self._half(t, k, self.c)
        return _rcopy(blk, blk, self.send.at[i], self.recv.at[i], dev)

    def passed(self, j, t, cc):
        i = j * self.n + t
        blk = self._half(t, self._chip(j)[0], cc)
        return _rcopy(blk, blk, self.fsend.at[i], self.frecv.at[i], self.sib)

    def start(self):
        for t in range(self.n):
            self.local_in(t).start()
        for j, t in self.pairs:
            self.first(j, t).start()

    def forward(self):
        for t in range(self.n):
            self.local_in(t).wait()
            self.local_out(t).start()
        for j, t in self.pairs:
            self.arrived(j, t).wait_recv()
            self.passed(j, t, self.c).start()

    def finish(self):
        for j, t in self.pairs:
            self.passed(j, t, 1 - self.c).wait_recv()
        for j, t in self.pairs:
            self.first(j, t).wait_send()
            self.passed(j, t, self.c).wait_send()
        for t in range(self.n):
            self.local_out(t).wait()


def all_gather_weights(shards, small):
    n = len(shards)
    shapes = [s.shape for s in shards]

    def body(*refs):
        w = refs[:n]
        sm = refs[n]
        o = refs[n + 1:2 * n + 1]
        osm = refs[2 * n + 1]
        ssend, srecv, sloc = refs[2 * n + 2:2 * n + 5]
        g = _Gather(shapes, w, o, refs[2 * n + 5:])
        own = pltpu.make_async_copy(sm, osm.at[g.me], sloc)
        own.start()
        g.start()
        small_cps = [_rcopy(sm, osm.at[g.me], ssend.at[j], srecv.at[j], g._chip(j)[1]) for j in range(3)]
        for cp in small_cps:
            cp.start()
        g.forward()
        g.finish()
        for j in range(3):
            k, dev = g._chip(j)
            _rcopy(sm, osm.at[k], ssend.at[j], srecv.at[j], dev).wait_recv()
            small_cps[j].wait_send()
        own.wait()

    out_shape = [jax.ShapeDtypeStruct((N_CHIPS,) + s.shape, s.dtype) for s in shards]
    out_shape.append(jax.ShapeDtypeStruct((N_CHIPS,) + small.shape, small.dtype))
    sems = pltpu.SemaphoreType.DMA
    return pl.pallas_call(
        body, name="all_gather_weights", out_shape=out_shape,
        in_specs=[ANY] * (n + 1), out_specs=[ANY] * (n + 1),
        scratch_shapes=[sems((3,)), sems((3,)), sems] + _Gather.scratch(shards),
        compiler_params=_params(),
    )(*shards, small)


def sibling_exchange(grads, name):
    n = len(grads)

    def body(*refs):
        g = refs[:n]
        o = refs[n:2 * n]
        send, recv = refs[2 * n:]
        x, y, c, _ = _place()
        cps = [_rcopy(g[t].at[:, 1 - c], o[t], send.at[t], recv.at[t], (x, y, 1 - c)) for t in range(n)]
        for cp in cps:
            cp.start()
        for cp in cps:
            cp.wait()

    sems = pltpu.SemaphoreType.DMA
    return pl.pallas_call(
        body, name=name,
        out_shape=[jax.ShapeDtypeStruct((a.shape[0],) + a.shape[2:], a.dtype) for a in grads],
        in_specs=[ANY] * n, out_specs=[ANY] * n, scratch_shapes=[sems((n,)), sems((n,))],
    )(*grads)


class _Scatter:
    def __init__(self, p, o, sems):
        self.n, self.p, self.o = len(p), p, o
        self.send, self.recv = sems
        _, _, self.c, self.chips = _place()

    @staticmethod
    def scratch(n):
        sems = pltpu.SemaphoreType.DMA
        return [sems((3 * n,)), sems((3 * n,))]

    @staticmethod
    def out_shapes(parts):
        return [jax.ShapeDtypeStruct((3,) + a.shape[1:], a.dtype) for a in parts]

    def copies(self):
        cps = []
        for j, (cx, cy) in enumerate(self.chips):
            for t in range(self.n):
                i = j * self.n + t
                cps.append(_rcopy(self.p[t].at[2 * cx + cy], self.o[t].at[j], self.send.at[i], self.recv.at[i],
                                  (cx, cy, self.c)))
        return cps

    def start(self):
        for cp in self.copies():
            cp.start()

    def finish(self):
        for cp in self.copies():
            cp.wait()


def sibling_swap(halves, name):
    n = len(halves)

    def body(*refs):
        h = refs[:n]
        o = refs[n:2 * n]
        send, recv = refs[2 * n:]
        x, y, c, _ = _place()
        cps = [_rcopy(h[t], o[t], send.at[t], recv.at[t], (x, y, 1 - c)) for t in range(n)]
        for cp in cps:
            cp.start()
        for cp in cps:
            cp.wait()

    sems = pltpu.SemaphoreType.DMA
    return pl.pallas_call(
        body, name=name, out_shape=[jax.ShapeDtypeStruct(a.shape, a.dtype) for a in halves],
        in_specs=[ANY] * n, out_specs=[ANY] * n, scratch_shapes=[sems((n,)), sems((n,))],
    )(*halves)


def small_all_reduce(ddw, v512, v1024):
    rows, width = PACK_ROWS, 512
    n512, n1024 = len(VEC512), len(VEC1024)

    def body(*refs):
        ddw_ref = refs[0]
        a_refs = refs[1:1 + n512]
        b_refs = refs[1 + n512:1 + n512 + n1024]
        o_ref, p_ref, gath_ref, send, recv = refs[1 + n512 + n1024:]
        p_ref[...] = jnp.zeros_like(p_ref)
        p_ref[0:32, :] = ddw_ref[...]
        for i, r in enumerate(a_refs):
            p_ref[32 + i:33 + i, :] = r[...]
        for i, r in enumerate(b_refs):
            base = 32 + n512 + 2 * i
            p_ref[base:base + 1, :] = r[:, 0:512]
            p_ref[base + 1:base + 2, :] = r[:, 512:1024]
        x, y, c, _ = _place()
        me = 4 * x + 2 * y + c
        gath_ref[me] = p_ref[...]
        cps = []
        for k in range(1, 8):
            dx, dy, dc = (k >> 2) & 1, (k >> 1) & 1, k & 1
            px = 1 - x if dx else x
            py = 1 - y if dy else y
            pc = 1 - c if dc else c
            cps.append(_rcopy(p_ref, gath_ref.at[me], send.at[k - 1], recv.at[k - 1], (px, py, pc)))
        for cp in cps:
            cp.start()
        for k in range(1, 8):
            dx, dy, dc = (k >> 2) & 1, (k >> 1) & 1, k & 1
            px = 1 - x if dx else x
            py = 1 - y if dy else y
            pc = 1 - c if dc else c
            _rcopy(p_ref, gath_ref.at[4 * px + 2 * py + pc], send.at[k - 1], recv.at[k - 1], (px, py, pc)).wait_recv()
        for cp in cps:
            cp.wait_send()
        total = gath_ref[0]
        for d in range(1, 8):
            total = total + gath_ref[d]
        o_ref[...] = total

    sems = pltpu.SemaphoreType.DMA
    n_in = 1 + n512 + n1024
    return pl.pallas_call(
        body, name="small_all_reduce", out_shape=jax.ShapeDtypeStruct((rows, width), F32),
        in_specs=[VMEM_SPEC] * n_in, out_specs=VMEM_SPEC,
        scratch_shapes=[pltpu.VMEM((rows, width), F32), pltpu.VMEM((8, rows, width), F32), sems((7,)), sems((7,))],
    )(ddw, *[v512[n] for n in VEC512], *[v1024[n] for n in VEC1024])


def _row_block(r):
    for tr in (512, 352, 256, 128):
        if r % tr == 0:
            return tr
    return r


def add_halves(g, recv, name):
    _, _, r, w = g.shape
    tr = _row_block(r)

    def body(g0_ref, g1_ref, r_ref, ob_ref, own_ref):
        k = pl.program_id(1)
        c = lax.axis_index("c")
        me = 2 * lax.axis_index("x") + lax.axis_index("y")
        t = jnp.where(c == 0, g0_ref[0, 0], g1_ref[0, 0]) + r_ref[0]
        ob_ref[0] = t.astype(MM)
        mine = jnp.where(k == me, t, 0.0)

        @pl.when(k == 0)
        def _():
            own_ref[...] = mine

        @pl.when(k != 0)
        def _():
            own_ref[...] += mine

    return pl.pallas_call(
        body, name=name, grid=(r // tr, N_CHIPS),
        in_specs=[pl.BlockSpec((1, 1, tr, w), lambda i, k: (k, 0, i, 0)),
                  pl.BlockSpec((1, 1, tr, w), lambda i, k: (k, 1, i, 0)),
                  pl.BlockSpec((1, tr, w), lambda i, k: (k, i, 0))],
        out_specs=[pl.BlockSpec((1, tr, w), lambda i, k: (k, i, 0)),
                   pl.BlockSpec((tr, w), lambda i, k: (i, 0))],
        out_shape=(jax.ShapeDtypeStruct((N_CHIPS, r, w), MM), jax.ShapeDtypeStruct((r, w), F32)),
        compiler_params=_params(("arbitrary", "arbitrary")),
    )(g, g, recv)


def sum_parts(own, rin, name):
    _, r, w = rin.shape
    tr = _row_block(r)

    def body(o_ref, r_ref, out_ref):
        out_ref[...] = ((o_ref[...] + r_ref[0].astype(F32)) + r_ref[1].astype(F32)) + r_ref[2].astype(F32)

    return pl.pallas_call(
        body, name=name, grid=(r // tr,), out_shape=jax.ShapeDtypeStruct((r, w), F32),
        in_specs=[pl.BlockSpec((tr, w), lambda i: (i, 0)), pl.BlockSpec((3, tr, w), lambda i: (0, i, 0))],
        out_specs=pl.BlockSpec((tr, w), lambda i: (i, 0)),
        compiler_params=_params(("arbitrary",)),
    )(own, rin)


def _adamw_math(w, g, m, v):
    mn = ADAM_B1 * m + (1.0 - ADAM_B1) * g
    vn = ADAM_B2 * v + (1.0 - ADAM_B2) * (g * g)
    m_hat = mn / (1.0 - ADAM_B1 ** ADAM_STEP)
    v_hat = vn / (1.0 - ADAM_B2 ** ADAM_STEP)
    return -ADAM_LR * (m_hat / (jnp.sqrt(v_hat) + ADAM_EPS) + ADAM_WD * w), mn, vn


def adamw(w, mine, other, m, v, name):
    r, c = w.shape
    rh = r // 2
    tr = _row_block(rh)
    if c >= 1024 and tr % 512 == 0:
        tr = 256
    nb = rh // tr

    def body(w_ref, a_ref, b_ref, m_ref, v_ref, go_ref, d_ref, mo_ref, vo_ref):
        gv = jnp.where(lax.axis_index("c") == pl.program_id(0), a_ref[...], b_ref[...])
        go_ref[...] = gv
        d_ref[...], mo_ref[...], vo_ref[...] = _adamw_math(w_ref[...], gv, m_ref[...], v_ref[...])

    spec = pl.BlockSpec((tr, c), lambda h, i: (h * nb + i, 0))
    half = pl.BlockSpec((tr, c), lambda h, i: (i, 0))
    out = jax.ShapeDtypeStruct((r, c), F32)
    return pl.pallas_call(
        body, name=name, grid=(2, nb), out_shape=(out, out, out, out),
        in_specs=[spec, half, half, spec, spec], out_specs=[spec] * 4,
        compiler_params=_params(("arbitrary", "arbitrary")),
    )(w, mine, other, m, v)


def adamw_small(gsum, params):
    names = list(params)
    flat = [a for n in names for a in params[n]]

    def body(*refs):
        g_ref = refs[0]
        ins = refs[1:1 + 3 * len(names)]
        outs = refs[1 + 3 * len(names):]
        me = 2 * lax.axis_index("x") + lax.axis_index("y")
        for i, n in enumerate(names):
            w_ref, m_ref, v_ref = ins[3 * i:3 * i + 3]
            go_ref, d_ref, mo_ref, vo_ref = outs[4 * i:4 * i + 4]
            if n == "conv_dw_w":
                gv = jnp.zeros((CONV_WIDTH, 128), F32)
                for k in range(N_CHIPS):
                    gv = gv + jnp.where(me == k, g_ref[0:CONV_WIDTH, 128 * k:128 * (k + 1)], 0.0)
            elif n in VEC512:
                r0 = 32 + VEC512.index(n)
                gv = g_ref[r0:r0 + 1, :]
            else:
                r0 = 32 + len(VEC512) + 2 * VEC1024.index(n)
                gv = jnp.concatenate([g_ref[r0:r0 + 1, :], g_ref[r0 + 1:r0 + 2, :]], axis=1)
            go_ref[...] = gv
            d_ref[...], mo_ref[...], vo_ref[...] = _adamw_math(w_ref[...], gv, m_ref[...], v_ref[...])

    out_shape = [jax.ShapeDtypeStruct(params[n][0].shape, F32) for n in names for _ in range(4)]
    res = pl.pallas_call(
        body, name="adamw_small", out_shape=out_shape,
        in_specs=[VMEM_SPEC] * (1 + len(flat)), out_specs=[VMEM_SPEC] * len(out_shape),
        compiler_params=_params(),
    )(gsum, *flat)
    return {n: res[4 * i:4 * i + 4] for i, n in enumerate(names)}


REST = ("w_ffn_up", "w_ffn_down", "w_out", "w_conv_branch", "w_att_branch")
VEC512 = ("conv_dw_b", "conv_ln_g", "conv_ln_b")
VEC1024 = ("norm_mix_pre", "b_conv_branch", "norm_mix_post", "norm_ffn_pre", "norm_ffn_post")
PACK_ROWS = 48


def kernel(x, norm_mix_pre, w_in, conv_dw_w, conv_dw_b, conv_ln_g, conv_ln_b, w_conv_branch, b_conv_branch, w_att_branch, w_out, norm_mix_post, norm_ffn_pre, w_ffn_up, w_ffn_down, norm_ffn_post, loss_target, m_norm_mix_pre, m_w_in, m_conv_dw_w, m_conv_dw_b, m_conv_ln_g, m_conv_ln_b, m_w_conv_branch, m_b_conv_branch, m_w_att_branch, m_w_out, m_norm_mix_post, m_norm_ffn_pre, m_w_ffn_up, m_w_ffn_down, m_norm_ffn_post, v_norm_mix_pre, v_w_in, v_conv_dw_w, v_conv_dw_b, v_conv_ln_g, v_conv_ln_b, v_w_conv_branch, v_b_conv_branch, v_w_att_branch, v_w_out, v_norm_mix_post, v_norm_ffn_pre, v_w_ffn_up, v_w_ffn_down, v_norm_ffn_post):
    weights = dict(norm_mix_pre=norm_mix_pre, w_in=w_in, conv_dw_w=conv_dw_w, conv_dw_b=conv_dw_b, conv_ln_g=conv_ln_g, conv_ln_b=conv_ln_b, w_conv_branch=w_conv_branch, b_conv_branch=b_conv_branch, w_att_branch=w_att_branch, w_out=w_out, norm_mix_post=norm_mix_post, norm_ffn_pre=norm_ffn_pre, w_ffn_up=w_ffn_up, w_ffn_down=w_ffn_down, norm_ffn_post=norm_ffn_post)
    mom = dict(norm_mix_pre=m_norm_mix_pre, w_in=m_w_in, conv_dw_w=m_conv_dw_w, conv_dw_b=m_conv_dw_b, conv_ln_g=m_conv_ln_g, conv_ln_b=m_conv_ln_b, w_conv_branch=m_w_conv_branch, b_conv_branch=m_b_conv_branch, w_att_branch=m_w_att_branch, w_out=m_w_out, norm_mix_post=m_norm_mix_post, norm_ffn_pre=m_norm_ffn_pre, w_ffn_up=m_w_ffn_up, w_ffn_down=m_w_ffn_down, norm_ffn_post=m_norm_ffn_post)
    var = dict(norm_mix_pre=v_norm_mix_pre, w_in=v_w_in, conv_dw_w=v_conv_dw_w, conv_dw_b=v_conv_dw_b, conv_ln_g=v_conv_ln_g, conv_ln_b=v_conv_ln_b, w_conv_branch=v_w_conv_branch, b_conv_branch=v_b_conv_branch, w_att_branch=v_w_att_branch, w_out=v_w_out, norm_mix_post=v_norm_mix_post, norm_ffn_pre=v_norm_ffn_pre, w_ffn_up=v_w_ffn_up, w_ffn_down=v_w_ffn_down, norm_ffn_post=v_norm_ffn_post)
    order = list(weights)
    grads, deltas, new_m, new_v = {}, {}, {}, {}
    xs = x.reshape(SEQ, D_MODEL)
    tgt = loss_target.reshape(SEQ, D_MODEL)
    row = lambda a: a.reshape(1, -1)
    g1, g2, g3, g4 = (row(weights[n]) for n in ("norm_mix_pre", "norm_mix_post", "norm_ffn_pre", "norm_ffn_post"))
    ln_g, ln_b = row(conv_ln_g), row(conv_ln_b)

    def reduce_prepare(names, partial, tag):
        from_sib = sibling_exchange([partial[n] for n in names], "sibling_exchange_" + tag)
        return [add_halves(partial[n], r, "add_" + n) for n, r in zip(names, from_sib)]

    def reduce_finish(names, summed, from_chips, tag):
        halves = [sum_parts(s[1], r, "sum_" + n) for n, s, r in zip(names, summed, from_chips)]
        for n, a, b in zip(names, halves, sibling_swap(halves, "sibling_swap_" + tag)):
            grads[n], deltas[n], new_m[n], new_v[n] = adamw(weights[n], a, b, mom[n], var[n], "adamw_" + n)

    w_in_g, dw_g = all_gather_weights([w_in.astype(MM)], conv_dw_w)
    w_dw_full = jnp.concatenate([dw_g[k] for k in range(N_CHIPS)], axis=1)
    h1, ci, q, k, v, gc, ga = in_proj_fwd(xs, g1, w_in_g)
    u1, u3 = conv_fwd(ci, w_dw_full, row(conv_dw_b), ln_g, ln_b)
    att, rc, *rest = attn_fwd(q, k, v, [weights[n].astype(MM) for n in REST])
    wg = dict(zip(REST, rest))
    w_out_g = wg["w_out"].reshape(D_MODEL, D_MODEL)
    w_down_g = wg["w_ffn_down"].reshape(D_FF, D_MODEL)
    w_cb_g, w_ab_g = wg["w_conv_branch"], wg["w_att_branch"]
    co, ao, merged, mix, x2, h2 = mix_fwd(u3, att, gc, ga, xs, w_cb_g, row(b_conv_branch), w_ab_g, w_out_g, g2, g3)
    gate, up, act = ffn_up_fwd(h2, wg["w_ffn_up"])
    dff, dy, loss_parts, dg4 = ffn_down_loss(act, w_down_g, x2, tgt, g4)
    loss = lax.psum(jnp.sum(loss_parts[::8, 0]), ("x", "y", "c"))

    partial = {}
    dgu = ffn_act_bwd(dff, w_down_g, gate, up)
    partial["w_ffn_down"] = weight_grad(act, dff, "dw_ffn_down", False, tk=UP_SHARD)
    dx2, dmix, dg3, dg2 = ffn_in_bwd(dgu, wg["w_ffn_up"], x2, mix, dy, g3, g2)
    partial["w_ffn_up"] = weight_grad(h2, dgu, "dw_ffn_up", True)
    dco, dao, dgc, dga, du3, datt, dbcb = merge_bwd(dmix, w_out_g, gc, ga, co, ao, w_cb_g, w_ab_g)
    partial["w_out"] = weight_grad(merged, dmix, "dw_out", False, tk=512)
    partial["w_conv_branch"] = weight_grad(u3, dco, "dw_conv_branch", True)
    partial["w_att_branch"] = weight_grad(att, dao, "dw_att_branch", True)
    summed = reduce_prepare(REST, partial, "rest")
    dci, ddw, dbdw, dlng, dlnb = conv_bwd(du3, u1, ci, w_dw_full, ln_g, ln_b)
    dq, dk, dv, *from_chips = attn_bwd(q, k, v, datt, rc, [s[0] for s in summed])
    reduce_finish(REST, summed, from_chips, "rest")
    dproj = jnp.concatenate([dci, dq, dk, dv, dgc, dga], axis=1)
    partial["w_in"] = weight_grad(h1, dproj, "dw_in", True)
    summed = reduce_prepare(("w_in",), partial, "w_in")
    grad_x, dg1, *from_chips = in_proj_bwd(dproj, w_in_g, xs, dx2, g1, [summed[0][0]])
    reduce_finish(("w_in",), summed, from_chips, "w_in")

    v512 = dict(conv_dw_b=dbdw, conv_ln_g=dlng, conv_ln_b=dlnb)
    v1024 = dict(norm_mix_pre=dg1, b_conv_branch=dbcb, norm_mix_post=dg2, norm_ffn_pre=dg3, norm_ffn_post=dg4)
    gsum = small_all_reduce(ddw, v512, v1024)
    as_rows = lambda n, a: a if n == "conv_dw_w" else a.reshape(1, -1)
    small_names = ("conv_dw_w",) + VEC512 + VEC1024
    small = adamw_small(gsum, {n: tuple(as_rows(n, d[n]) for d in (weights, mom, var)) for n in small_names})
    for n in small_names:
        grads[n], deltas[n], new_m[n], new_v[n] = (a.reshape(weights[n].shape) for a in small[n])

    return (loss, grad_x.reshape(1, SEQ, D_MODEL), *[grads[n] for n in order], *[deltas[n] for n in order],
            *[new_m[n] for n in order], *[new_v[n] for n in order])
```

```python
import jax
import jax.numpy as jnp
from jax import lax
from jax.experimental import pallas as pl
from jax.experimental.pallas import tpu as pltpu

F32 = jnp.float32
MM = jnp.bfloat16

SEQ = 2048
D_MODEL = 1024
CONV_DIM = 512
ATT_DIM = 512
CONV_WIDTH = 31
D_FF = 2816
IN_COLS = 2 * CONV_DIM + 3 * ATT_DIM + 2 * D_MODEL
N_CHIPS = 4
IN_SHARD = IN_COLS // N_CHIPS
UP_SHARD = 2 * D_FF // N_CHIPS
BR_SHARD = D_MODEL // N_CHIPS
EPS = 1e-6
ATT_SCALE = 0.125

TM = 256
TQ = 128
CONV_TILE = 64
CONV_WIN = CONV_TILE + 32
VMEM_LIMIT = 56 * 1024 * 1024

ADAM_LR = 0.001
ADAM_B1 = 0.9
ADAM_B2 = 0.999
ADAM_EPS = 1e-08
ADAM_WD = 0.01
ADAM_STEP = 10

MESH = pl.DeviceIdType.MESH
ANY = pl.BlockSpec(memory_space=pl.ANY)
VMEM_SPEC = pl.BlockSpec(memory_space=pltpu.VMEM)

NT_DIMS = (((1,), (1,)), ((), ()))
TN_DIMS = (((0,), (0,)), ((), ()))

IN_PIECES = (("ci", 0, 1024), ("q", 1024, 1536), ("k", 1536, 2048), ("v", 2048, 2560),
             ("gc", 2560, 3584), ("ga", 3584, 4608))


def _params(sem=None, vmem=VMEM_LIMIT):
    return pltpu.CompilerParams(dimension_semantics=sem, vmem_limit_bytes=vmem)


def _dot(a, b):
    return jnp.dot(a, b, preferred_element_type=F32)


def _dot_nt(a, b):
    return lax.dot_general(a, b, NT_DIMS, preferred_element_type=F32)


def _dot_tn(a, b):
    return lax.dot_general(a, b, TN_DIMS, preferred_element_type=F32)


def _sigmoid(x):
    return 1.0 / (1.0 + jnp.exp(-x))


def _rms(x):
    r = lax.rsqrt(jnp.mean(x * x, axis=-1, keepdims=True) + EPS)
    return x * r, r


def _rms_bwd(dy_g, n, r):
    return r * (dy_g - n * jnp.mean(dy_g * n, axis=-1, keepdims=True))


def _row_tile_spec(width, tm=TM):
    return pl.BlockSpec((tm, width), lambda i: (i, 0))


def _full_spec(shape):
    nd = len(shape)
    return pl.BlockSpec(shape, lambda *_: (0,) * nd)


def _acc_rows(ref, val, first):
    @pl.when(first)
    def _():
        ref[...] = val

    @pl.when(jnp.logical_not(first))
    def _():
        ref[...] += val


def _gather_behind_grid(ag, n_steps):
    step = pl.program_id(0)
    pl.when(step == 0)(ag.start)
    pl.when(step == n_steps - 2)(ag.forward)
    return lambda: pl.when(step == n_steps - 1)(ag.finish)


def in_proj_fwd(x, g1, w_in_g, gather=()):
    ng = len(gather)
    nt = SEQ // TM

    def body(*refs):
        x_ref, g_ref, w_ref = refs[:3]
        h_ref, ci_ref, q_ref, k_ref, v_ref, gc_ref, ga_ref = refs[3 + ng:10 + ng]
        if ng:
            done = _gather_behind_grid(_Gather([s.shape for s in gather], refs[3:3 + ng],
                                               refs[10 + ng:10 + 2 * ng], refs[10 + 2 * ng:]), nt)
        n, _ = _rms(x_ref[...])
        h = (n * g_ref[...]).astype(MM)
        h_ref[...] = h
        outs = dict(ci=ci_ref, q=q_ref, k=k_ref, v=v_ref, gc=gc_ref, ga=ga_ref)
        for j in range(N_CHIPS):
            p = _dot(h, w_ref[j])
            g0 = j * IN_SHARD
            for name, s, e in IN_PIECES:
                lo, hi = max(s, g0), min(e, g0 + IN_SHARD)
                if lo < hi:
                    ref = outs[name]
                    part = p[:, lo - g0:hi - g0]
                    if name == "q":
                        part = part * ATT_SCALE
                    ref[:, lo - s:hi - s] = part.astype(ref.dtype)
        if ng:
            done()

    out_shape = [
        jax.ShapeDtypeStruct((SEQ, D_MODEL), MM),
        jax.ShapeDtypeStruct((SEQ, 2 * CONV_DIM), F32),
        jax.ShapeDtypeStruct((SEQ, ATT_DIM), MM),
        jax.ShapeDtypeStruct((SEQ, ATT_DIM), MM),
        jax.ShapeDtypeStruct((SEQ, ATT_DIM), MM),
        jax.ShapeDtypeStruct((SEQ, D_MODEL), F32),
        jax.ShapeDtypeStruct((SEQ, D_MODEL), F32),
    ]
    return pl.pallas_call(
        body, name="in_proj_fwd", grid=(nt,),
        out_shape=out_shape + [jax.ShapeDtypeStruct((N_CHIPS,) + s.shape, s.dtype) for s in gather],
        in_specs=[_row_tile_spec(D_MODEL), _full_spec((1, D_MODEL)), _full_spec(w_in_g.shape)] + [ANY] * ng,
        out_specs=[_row_tile_spec(s.shape[1]) for s in out_shape] + [ANY] * ng,
        scratch_shapes=_Gather.scratch(gather) if ng else [],
        compiler_params=_params(("arbitrary",)),
    )(x, g1, w_in_g, *gather)


def _shifted_sum(win, terms):
    by_rot = {}
    for m, coef in terms:
        by_rot.setdefault(m % 8, []).append((m // 8, coef))
    acc = None
    n = win.shape[0]
    for rot in sorted(by_rot):
        shifted = win if rot == 0 else pltpu.roll(win, n - rot, 0)
        for a, coef in by_rot[rot]:
            t = coef * shifted[8 * a:8 * a + CONV_TILE, :]
            acc = t if acc is None else acc + t
    return acc


def _glu_into(ci_ref, upad_ref):
    upad_ref[0:32, :] = jnp.zeros((32, CONV_DIM), F32)

    def step(i, c):
        t0 = pl.multiple_of(i * TM, TM)
        a = ci_ref[pl.ds(t0, TM), 0:CONV_DIM]
        b = ci_ref[pl.ds(t0, TM), CONV_DIM:2 * CONV_DIM]
        upad_ref[pl.ds(t0 + 32, TM), :] = a * _sigmoid(b)
        return c

    lax.fori_loop(0, SEQ // TM, step, 0)


def _layernorm_parts(u1):
    mu = jnp.mean(u1, axis=-1, keepdims=True)
    xc = u1 - mu
    rstd = lax.rsqrt(jnp.mean(xc * xc, axis=-1, keepdims=True) + EPS)
    return xc * rstd, rstd


def conv_fwd(ci, w_dw, b_dw, ln_g, ln_b):
    def body(ci_ref, w_ref, b_ref, g_ref, bb_ref, u1_ref, u3_ref, upad_ref):
        _glu_into(ci_ref, upad_ref)

        def step(i, c):
            t0 = pl.multiple_of(i * CONV_TILE, CONV_TILE)
            win = upad_ref[pl.ds(t0, CONV_WIN), :]
            u1 = _shifted_sum(win, [(j + 2, w_ref[j:j + 1, :]) for j in range(CONV_WIDTH)]) + b_ref[...]
            u1_ref[pl.ds(t0, CONV_TILE), :] = u1
            xh, _ = _layernorm_parts(u1)
            u2 = xh * g_ref[...] + bb_ref[...]
            u3_ref[pl.ds(t0, CONV_TILE), :] = (u2 * _sigmoid(u2)).astype(MM)
            return c

        lax.fori_loop(0, SEQ // CONV_TILE, step, 0)

    return pl.pallas_call(
        body, name="conv_fwd",
        out_shape=(jax.ShapeDtypeStruct((SEQ, CONV_DIM), F32), jax.ShapeDtypeStruct((SEQ, CONV_DIM), MM)),
        in_specs=[VMEM_SPEC] * 5, out_specs=[VMEM_SPEC] * 2,
        scratch_shapes=[pltpu.VMEM((SEQ + 32, CONV_DIM), F32)],
        compiler_params=_params(),
    )(ci, w_dw, b_dw, ln_g, ln_b)


def _softplus(z):
    return jnp.maximum(z, 0.0) + jnp.log(1.0 + jnp.exp(-jnp.abs(z)))


def _cumsum_weights(suffix, with_total):
    n = 256 if with_total else 128
    r = lax.broadcasted_iota(jnp.int32, (128, n), 0)
    c = lax.broadcasted_iota(jnp.int32, (128, n), 1)
    tri = (r >= c) if suffix else (r <= c)
    return jnp.logical_or(tri, c >= 128).astype(MM)


NO_SCORE = -1e30
N_KB = SEQ // TQ


def _score_bias(lane, row, i, j):
    keep = jnp.logical_and(i >= 0, jnp.logical_or(j < i, lane < row))
    return jnp.where(keep, 0.0, NO_SCORE)


def _block_pipeline(n_stages, descending, step, on_query_block=None):
    n_lag = n_stages - 1
    none = jnp.int32(-1)

    def shift(cur, lag):
        step([cur] + [(lag[2 * s], lag[2 * s + 1]) for s in range(n_lag)])
        return (cur[0], cur[1]) + tuple(lag[:-2])

    def outer(i, lag):
        if on_query_block is not None:
            on_query_block(i)

        def inner(n, lag):
            return shift((i, i - n if descending else n), lag)
        return lax.fori_loop(0, i + 1, inner, lag)

    lag = lax.fori_loop(0, N_KB, outer, (none,) * (2 * n_lag))
    lax.fori_loop(0, n_lag, lambda n, lag: shift((none, none), lag), lag)


def _head_masks():
    lane = lax.broadcasted_iota(jnp.int32, (TQ, 128), 1)
    row = lax.broadcasted_iota(jnp.int32, (TQ, 128), 0)
    return lane, row, lane < 64


def _pick_head(x, head0, h):
    zero = jnp.zeros_like(x)
    return jnp.where(head0, x, zero) if h == 0 else jnp.where(head0, zero, x)


N_PAIRS = ATT_DIM // 128


def attn_fwd(q, k, v, gather=()):
    ng = len(gather)

    def body(*refs):
        q_ref, k_ref, v_ref = refs[:3]
        o_ref, rc_ref = refs[3 + ng:5 + ng]
        acc_ref, r_ref, z_ref, spb_ref, ab_ref = refs[5 + 2 * ng:10 + 2 * ng]
        if ng:
            ag = _Gather([s.shape for s in gather], refs[3:3 + ng], refs[5 + ng:5 + 2 * ng], refs[10 + 2 * ng:])
            ag.start()
        lane, row, head0 = _head_masks()
        w = _cumsum_weights(suffix=True, with_total=True)
        acc_ref[...] = jnp.zeros_like(acc_ref)
        r_ref[...] = jnp.zeros_like(r_ref)
        rc_ref[...] = jnp.zeros_like(rc_ref)
        z_ref[...] = jnp.full(z_ref.shape, NO_SCORE, F32)
        spb_ref[...] = jnp.zeros_like(spb_ref)
        ab_ref[...] = jnp.zeros_like(ab_ref)

        def step(pairs):
            (i1, j1), (i2, j2), (i3, j3) = pairs
            q1, k1, q2, q3, k3 = (pl.multiple_of(jnp.maximum(b, 0) * TQ, TQ) for b in (i1, j1, i2, i3, j3))
            bias1 = _score_bias(lane, row, i1, j1)
            first2 = j2 == i2
            rc_rows = rc_ref[pl.ds(q2, TQ), :]
            for p in range(N_PAIRS):
                cols = slice(128 * p, 128 * (p + 1))
                qb = q_ref[pl.ds(q1, TQ), cols]
                kb = k_ref[pl.ds(k1, TQ), cols]
                vb = v_ref[pl.ds(k3, TQ), cols]
                for h in range(2):
                    hh = 2 * p + h
                    acc_ref[pl.ds(q3, TQ), cols] += _dot(ab_ref[hh], _pick_head(vb, head0, h))
                    r = _dot(spb_ref[hh], w)
                    r_in = jnp.where(first2, 0.0, r_ref[hh])
                    ab_ref[hh] = jnp.exp(z_ref[hh] - (r[:, :128] + r_in)).astype(MM)
                    rc_rows = jnp.where(jnp.logical_and(lane == 16 * hh + j2, i2 >= 0), r_in, rc_rows)
                    r_ref[hh] = r_in + r[:, 128:]
                    z = _dot_nt(_pick_head(qb, head0, h), kb) + bias1
                    z_ref[hh] = z
                    spb_ref[hh] = _softplus(z).astype(MM)
            rc_ref[pl.ds(q2, TQ), :] = rc_rows

        if ng:
            _block_pipeline(3, True, step, lambda i: pl.when(i == N_KB - 2)(ag.forward))
        else:
            _block_pipeline(3, True, step)
        o_ref[...] = acc_ref[...].astype(MM)
        if ng:
            ag.finish()

    out_shape = [jax.ShapeDtypeStruct((SEQ, ATT_DIM), MM), jax.ShapeDtypeStruct((SEQ, 128), F32)]
    out_shape += [jax.ShapeDtypeStruct((N_CHIPS,) + s.shape, s.dtype) for s in gather]
    return pl.pallas_call(
        body, name="attn_fwd", out_shape=out_shape,
        in_specs=[VMEM_SPEC] * 3 + [ANY] * ng, out_specs=[VMEM_SPEC] * 2 + [ANY] * ng,
        scratch_shapes=[pltpu.VMEM((SEQ, ATT_DIM), F32), pltpu.VMEM((8, TQ, 128), F32),
                        pltpu.VMEM((8, TQ, 128), F32), pltpu.VMEM((8, TQ, 128), MM), pltpu.VMEM((8, TQ, 128), MM)]
                       + (_Gather.scratch(gather) if ng else []),
        compiler_params=_params(),
    )(q, k, v, *gather)


def mix_fwd(u3, att, gc, ga, x, w_cb_g, b_cb, w_ab_g, w_out_g, g2, g3, gather=()):
    ng = len(gather)
    nt = SEQ // TM

    def body(*refs):
        u_ref, a_ref, gc_ref, ga_ref, x_ref, wcb_ref, bcb_ref, wab_ref, wout_ref, g2_ref, g3_ref = refs[:11]
        co_ref, ao_ref, mg_ref, mix_ref, x2_ref, h2_ref = refs[11 + ng:17 + ng]
        if ng:
            done = _gather_behind_grid(_Gather([s.shape for s in gather], refs[11:11 + ng],
                                               refs[17 + ng:17 + 2 * ng], refs[17 + 2 * ng:]), nt)
        u = u_ref[...]
        a = a_ref[...]
        for j in range(N_CHIPS):
            cols = slice(j * BR_SHARD, (j + 1) * BR_SHARD)
            co_ref[:, cols] = _dot(u, wcb_ref[j]) + bcb_ref[:, cols]
            ao_ref[:, cols] = _dot(a, wab_ref[j])
        merged = (_sigmoid(gc_ref[...]) * co_ref[...] + _sigmoid(ga_ref[...]) * ao_ref[...]).astype(MM)
        mg_ref[...] = merged
        mix = _dot(merged, wout_ref[...])
        mix_ref[...] = mix
        n2, _ = _rms(mix)
        x2 = x_ref[...] + n2 * g2_ref[...]
        x2_ref[...] = x2
        n3, _ = _rms(x2)
        h2_ref[...] = (n3 * g3_ref[...]).astype(MM)
        if ng:
            done()

    out_shape = [
        jax.ShapeDtypeStruct((SEQ, D_MODEL), F32), jax.ShapeDtypeStruct((SEQ, D_MODEL), F32),
        jax.ShapeDtypeStruct((SEQ, D_MODEL), MM), jax.ShapeDtypeStruct((SEQ, D_MODEL), F32),
        jax.ShapeDtypeStruct((SEQ, D_MODEL), F32), jax.ShapeDtypeStruct((SEQ, D_MODEL), MM),
    ]
    vec = _full_spec((1, D_MODEL))
    return pl.pallas_call(
        body, name="mix_fwd", grid=(nt,),
        out_shape=out_shape + [jax.ShapeDtypeStruct((N_CHIPS,) + s.shape, s.dtype) for s in gather],
        in_specs=[_row_tile_spec(CONV_DIM), _row_tile_spec(ATT_DIM), _row_tile_spec(D_MODEL),
                  _row_tile_spec(D_MODEL), _row_tile_spec(D_MODEL), _full_spec(w_cb_g.shape), vec,
                  _full_spec(w_ab_g.shape), _full_spec(w_out_g.shape), vec, vec] + [ANY] * ng,
        out_specs=[_row_tile_spec(D_MODEL)] * 6 + [ANY] * ng,
        scratch_shapes=_Gather.scratch(gather) if ng else [],
        compiler_params=_params(("arbitrary",)),
    )(u3, att, gc, ga, x, w_cb_g, b_cb, w_ab_g, w_out_g, g2, g3, *gather)


def ffn_up_fwd(h2, w_up_g):
    def body(h_ref, wg_ref, wu_ref, gate_ref, up_ref, act_ref):
        h = h_ref[...]
        gate = _dot(h, wg_ref[0])
        up = _dot(h, wu_ref[0])
        gate_ref[...] = gate
        up_ref[...] = up
        act_ref[...] = (gate * _sigmoid(gate) * up).astype(MM)

    tile = pl.BlockSpec((TM, UP_SHARD), lambda n, i: (i, n))
    return pl.pallas_call(
        body, name="ffn_up_fwd", grid=(2, SEQ // TM),
        out_shape=(jax.ShapeDtypeStruct((SEQ, D_FF), F32), jax.ShapeDtypeStruct((SEQ, D_FF), F32),
                   jax.ShapeDtypeStruct((SEQ, D_FF), MM)),
        in_specs=[pl.BlockSpec((TM, D_MODEL), lambda n, i: (i, 0)),
                  pl.BlockSpec((1, D_MODEL, UP_SHARD), lambda n, i: (n, 0, 0)),
                  pl.BlockSpec((1, D_MODEL, UP_SHARD), lambda n, i: (n + 2, 0, 0))],
        out_specs=[tile, tile, tile],
        compiler_params=_params(("arbitrary", "arbitrary")),
    )(h2, w_up_g, w_up_g)


def ffn_down_loss(act, w_down_g, x2, target, g4):
    def body(act_ref, wd_ref, x2_ref, t_ref, g_ref, dff_ref, dy_ref, loss_ref, dg_ref):
        ff = _dot(act_ref[...], wd_ref[...])
        n4, r4 = _rms(ff)
        g4v = g_ref[...]
        err = x2_ref[...] + n4 * g4v - t_ref[...]
        row_loss = jnp.mean(err * err, axis=-1, keepdims=True)
        loss_ref[...] = jnp.zeros((8, 128), F32) + 0.5 * jnp.sum(row_loss, axis=0, keepdims=True)
        dy = err * (1.0 / D_MODEL)
        dy_ref[...] = dy
        dff_ref[...] = _rms_bwd(dy * g4v, n4, r4).astype(MM)
        _acc_rows(dg_ref, jnp.sum(dy * n4, axis=0, keepdims=True), pl.program_id(0) == 0)

    nt = SEQ // TM
    vec = _full_spec((1, D_MODEL))
    return pl.pallas_call(
        body, name="ffn_down_loss", grid=(nt,),
        out_shape=(jax.ShapeDtypeStruct((SEQ, D_MODEL), MM), jax.ShapeDtypeStruct((SEQ, D_MODEL), F32),
                   jax.ShapeDtypeStruct((nt * 8, 128), F32), jax.ShapeDtypeStruct((1, D_MODEL), F32)),
        in_specs=[_row_tile_spec(D_FF), _full_spec(w_down_g.shape), _row_tile_spec(D_MODEL),
                  _row_tile_spec(D_MODEL), vec],
        out_specs=[_row_tile_spec(D_MODEL), _row_tile_spec(D_MODEL),
                   pl.BlockSpec((8, 128), lambda i: (i, 0)), vec],
        compiler_params=_params(("arbitrary",)),
    )(act, w_down_g, x2, target, g4)


def ffn_act_bwd(dff, w_down_g, gate, up):
    def body(dff_ref, wd_ref, gate_ref, up_ref, dgu_ref):
        dact = _dot_nt(dff_ref[...], wd_ref[...])
        gate = gate_ref[...]
        sg = _sigmoid(gate)
        dgu_ref[:, 0:D_FF] = (dact * up_ref[...] * (sg * (1.0 + gate * (1.0 - sg)))).astype(MM)
        dgu_ref[:, D_FF:2 * D_FF] = (dact * (gate * sg)).astype(MM)

    return pl.pallas_call(
        body, name="ffn_act_bwd", grid=(SEQ // TM,),
        out_shape=jax.ShapeDtypeStruct((SEQ, 2 * D_FF), MM),
        in_specs=[_row_tile_spec(D_MODEL), _full_spec(w_down_g.shape), _row_tile_spec(D_FF), _row_tile_spec(D_FF)],
        out_specs=_row_tile_spec(2 * D_FF),
        compiler_params=_params(("arbitrary",)),
    )(dff, w_down_g, gate, up)


def ffn_in_bwd(dgu, w_up_g, x2, mix, dy, g3, g2):
    def body(dgu_ref, w_ref, x2_ref, mix_ref, dy_ref, g3_ref, g2_ref, dx2_ref, dmix_ref, dg3_ref, dg2_ref):
        dh2 = None
        for j in range(N_CHIPS):
            t = _dot_nt(dgu_ref[:, j * UP_SHARD:(j + 1) * UP_SHARD], w_ref[j])
            dh2 = t if dh2 is None else dh2 + t
        first = pl.program_id(0) == 0
        n3, r3 = _rms(x2_ref[...])
        dx2 = dy_ref[...] + _rms_bwd(dh2 * g3_ref[...], n3, r3)
        dx2_ref[...] = dx2
        _acc_rows(dg3_ref, jnp.sum(dh2 * n3, axis=0, keepdims=True), first)
        n2, r2 = _rms(mix_ref[...])
        dmix_ref[...] = _rms_bwd(dx2 * g2_ref[...], n2, r2).astype(MM)
        _acc_rows(dg2_ref, jnp.sum(dx2 * n2, axis=0, keepdims=True), first)

    vec = _full_spec((1, D_MODEL))
    return pl.pallas_call(
        body, name="ffn_in_bwd", grid=(SEQ // TM,),
        out_shape=(jax.ShapeDtypeStruct((SEQ, D_MODEL), F32), jax.ShapeDtypeStruct((SEQ, D_MODEL), MM),
                   jax.ShapeDtypeStruct((1, D_MODEL), F32), jax.ShapeDtypeStruct((1, D_MODEL), F32)),
        in_specs=[_row_tile_spec(2 * D_FF), _full_spec(w_up_g.shape), _row_tile_spec(D_MODEL),
                  _row_tile_spec(D_MODEL), _row_tile_spec(D_MODEL), vec, vec],
        out_specs=[_row_tile_spec(D_MODEL), _row_tile_spec(D_MODEL), vec, vec],
        compiler_params=_params(("arbitrary",)),
    )(dgu, w_up_g, x2, mix, dy, g3, g2)


def merge_bwd(dmix, w_out_g, gc, ga, co, ao, w_cb_g, w_ab_g):
    def body(dmix_ref, wout_ref, gc_ref, ga_ref, co_ref, ao_ref, wcb_ref, wab_ref,
             dco_ref, dao_ref, dgc_ref, dga_ref, du3_ref, datt_ref, dbcb_ref):
        dm = _dot_nt(dmix_ref[...], wout_ref[...])
        sgc = _sigmoid(gc_ref[...])
        sga = _sigmoid(ga_ref[...])
        dco = dm * sgc
        dao = dm * sga
        dgc_ref[...] = (dm * co_ref[...] * (sgc * (1.0 - sgc))).astype(MM)
        dga_ref[...] = (dm * ao_ref[...] * (sga * (1.0 - sga))).astype(MM)
        _acc_rows(dbcb_ref, jnp.sum(dco, axis=0, keepdims=True), pl.program_id(0) == 0)
        dco_ref[...] = dco.astype(MM)
        dao_ref[...] = dao.astype(MM)
        du3 = None
        datt = None
        for j in range(N_CHIPS):
            cols = slice(j * BR_SHARD, (j + 1) * BR_SHARD)
            t = _dot_nt(dco_ref[:, cols], wcb_ref[j])
            s = _dot_nt(dao_ref[:, cols], wab_ref[j])
            du3 = t if du3 is None else du3 + t
            datt = s if datt is None else datt + s
        du3_ref[...] = du3
        datt_ref[...] = datt.astype(MM)

    wide = _row_tile_spec(D_MODEL)
    return pl.pallas_call(
        body, name="merge_bwd", grid=(SEQ // TM,),
        out_shape=(jax.ShapeDtypeStruct((SEQ, D_MODEL), MM), jax.ShapeDtypeStruct((SEQ, D_MODEL), MM),
                   jax.ShapeDtypeStruct((SEQ, D_MODEL), MM), jax.ShapeDtypeStruct((SEQ, D_MODEL), MM),
                   jax.ShapeDtypeStruct((SEQ, CONV_DIM), F32), jax.ShapeDtypeStruct((SEQ, ATT_DIM), MM),
                   jax.ShapeDtypeStruct((1, D_MODEL), F32)),
        in_specs=[wide, _full_spec(w_out_g.shape), wide, wide, wide, wide,
                  _full_spec(w_cb_g.shape), _full_spec(w_ab_g.shape)],
        out_specs=[wide, wide, wide, wide, _row_tile_spec(CONV_DIM), _row_tile_spec(ATT_DIM),
                   _full_spec((1, D_MODEL))],
        compiler_params=_params(("arbitrary",)),
    )(dmix, w_out_g, gc, ga, co, ao, w_cb_g, w_ab_g)


def conv_bwd(du3, u1, ci, w_dw, ln_g, ln_b):
    def body(du3_ref, u1_ref, ci_ref, w_ref, g_ref, bb_ref,
             dci_ref, dw_ref, dbdw_ref, dg_ref, db_ref, upad_ref, dpad_ref, dwacc_ref, vacc_ref):
        _glu_into(ci_ref, upad_ref)
        dpad_ref[SEQ:SEQ + 32, :] = jnp.zeros((32, CONV_DIM), F32)
        dwacc_ref[...] = jnp.zeros_like(dwacc_ref)
        vacc_ref[...] = jnp.zeros_like(vacc_ref)

        def fold8(t):
            s = t[0:8, :]
            for r in range(1, CONV_TILE // 8):
                s = s + t[8 * r:8 * r + 8, :]
            return s

        def pass1(i, c):
            t0 = pl.multiple_of(i * CONV_TILE, CONV_TILE)
            xh, rstd = _layernorm_parts(u1_ref[pl.ds(t0, CONV_TILE), :])
            gv = g_ref[...]
            u2 = xh * gv + bb_ref[...]
            s2 = _sigmoid(u2)
            du2 = du3_ref[pl.ds(t0, CONV_TILE), :] * (s2 * (1.0 + u2 * (1.0 - s2)))
            wv = du2 * gv
            du1 = rstd * (wv - jnp.mean(wv, axis=-1, keepdims=True)
                          - xh * jnp.mean(wv * xh, axis=-1, keepdims=True))
            dpad_ref[pl.ds(t0, CONV_TILE), :] = du1
            vacc_ref[0] += fold8(du2 * xh)
            vacc_ref[1] += fold8(du2)
            vacc_ref[2] += fold8(du1)
            win = upad_ref[pl.ds(t0, CONV_WIN), :]
            n = win.shape[0]
            for rot in range(8):
                shifted = win if rot == 0 else pltpu.roll(win, n - rot, 0)
                for a in range(5):
                    j = 8 * a + rot - 2
                    if 0 <= j < CONV_WIDTH:
                        dwacc_ref[j] += fold8(du1 * shifted[8 * a:8 * a + CONV_TILE, :])
            return c

        lax.fori_loop(0, SEQ // CONV_TILE, pass1, 0)

        def pass2(i, c):
            t0 = pl.multiple_of(i * CONV_TILE, CONV_TILE)
            win = dpad_ref[pl.ds(t0, CONV_WIN), :]
            du0 = _shifted_sum(win, [(30 - j, w_ref[j:j + 1, :]) for j in range(CONV_WIDTH)])
            a = ci_ref[pl.ds(t0, CONV_TILE), 0:CONV_DIM]
            sb = _sigmoid(ci_ref[pl.ds(t0, CONV_TILE), CONV_DIM:2 * CONV_DIM])
            dci_ref[pl.ds(t0, CONV_TILE), 0:CONV_DIM] = (du0 * sb).astype(MM)
            dci_ref[pl.ds(t0, CONV_TILE), CONV_DIM:2 * CONV_DIM] = (du0 * a * (sb * (1.0 - sb))).astype(MM)
            return c

        lax.fori_loop(0, SEQ // CONV_TILE, pass2, 0)

        for j in range(CONV_WIDTH):
            dw_ref[j:j + 1, :] = jnp.sum(dwacc_ref[j], axis=0, keepdims=True)
        dw_ref[CONV_WIDTH:32, :] = jnp.zeros((32 - CONV_WIDTH, CONV_DIM), F32)
        dg_ref[...] = jnp.sum(vacc_ref[0], axis=0, keepdims=True)
        db_ref[...] = jnp.sum(vacc_ref[1], axis=0, keepdims=True)
        dbdw_ref[...] = jnp.sum(vacc_ref[2], axis=0, keepdims=True)

    vec = jax.ShapeDtypeStruct((1, CONV_DIM), F32)
    return pl.pallas_call(
        body, name="conv_bwd",
        out_shape=(jax.ShapeDtypeStruct((SEQ, 2 * CONV_DIM), MM), jax.ShapeDtypeStruct((32, CONV_DIM), F32),
                   vec, vec, vec),
        in_specs=[VMEM_SPEC] * 6, out_specs=[VMEM_SPEC] * 5,
        scratch_shapes=[pltpu.VMEM((SEQ + 32, CONV_DIM), F32), pltpu.VMEM((SEQ + 32, CONV_DIM), F32),
                        pltpu.VMEM((CONV_WIDTH, 8, CONV_DIM), F32), pltpu.VMEM((3, 8, CONV_DIM), F32)],
        compiler_params=_params(),
    )(du3, u1, ci, w_dw, ln_g, ln_b)


def attn_bwd(q, k, v, datt, rc, scatter=()):
    ns = len(scatter)

    def body(*refs):
        q_ref, k_ref, v_ref, do_ref, rc_ref = refs[:5]
        dq_ref, dk_ref, dv_ref = refs[5 + ns:8 + ns]
        (dqa_ref, dka_ref, dva_ref, pc_ref, z_ref, sig1_ref, sig2_ref, g_ref, spb_ref, gb_ref, ab_ref,
         dzb_ref) = refs[8 + 2 * ns:20 + 2 * ns]
        if ns:
            sc = _Scatter(refs[5:5 + ns], refs[8 + ns:8 + 2 * ns], refs[20 + 2 * ns:])
            sc.start()
        lane, row, head0 = _head_masks()
        for ref in (dqa_ref, dka_ref, dva_ref, pc_ref):
            ref[...] = jnp.zeros_like(ref)
        z_ref[...] = jnp.full(z_ref.shape, NO_SCORE, F32)
        for ref in (sig1_ref, sig2_ref, spb_ref, ab_ref, g_ref, gb_ref, dzb_ref):
            ref[...] = jnp.zeros_like(ref)
        w_suffix = _cumsum_weights(suffix=True, with_total=False)
        w_prefix = _cumsum_weights(suffix=False, with_total=True)

        def step(pairs):
            (ia, ja), (ib, jb), (ic, jc), (id_, jd) = pairs
            qa, ka, qb_, kb_, qc, kc, qd, kd = (pl.multiple_of(jnp.maximum(b, 0) * TQ, TQ)
                                                for b in (ia, ja, ib, jb, ic, jc, id_, jd))
            bias_a = _score_bias(lane, row, ia, ja)
            rc_rows = rc_ref[pl.ds(qb_, TQ), :]
            first_c = jc == 0
            for p in range(N_PAIRS):
                cols = slice(128 * p, 128 * (p + 1))
                q_a = q_ref[pl.ds(qa, TQ), cols]
                k_a = k_ref[pl.ds(ka, TQ), cols]
                do_b = do_ref[pl.ds(qb_, TQ), cols]
                v_b = v_ref[pl.ds(kb_, TQ), cols]
                do_c = do_ref[pl.ds(qc, TQ), cols]
                q_d = q_ref[pl.ds(qd, TQ), cols]
                k_d = k_ref[pl.ds(kd, TQ), cols]
                for h in range(2):
                    hh = 2 * p + h
                    dzb = dzb_ref[hh]
                    dqa_ref[pl.ds(qd, TQ), cols] += _dot(dzb, _pick_head(k_d, head0, h))
                    dka_ref[pl.ds(kd, TQ), cols] += _dot_tn(dzb, _pick_head(q_d, head0, h))
                    r = _dot(gb_ref[hh], w_prefix)
                    p_in = jnp.where(first_c, 0.0, pc_ref[hh])
                    dzb_ref[hh] = (g_ref[hh] - sig2_ref[hh] * (r[:, :128] + p_in)).astype(MM)
                    pc_ref[hh] = p_in + r[:, 128:]
                    dva_ref[pl.ds(kc, TQ), cols] += _dot_tn(ab_ref[hh], _pick_head(do_c, head0, h))
                    r_in = jnp.sum(jnp.where(lane == 16 * hh + jb, rc_rows, 0.0), axis=1, keepdims=True)
                    a = jnp.exp(z_ref[hh] - (_dot(spb_ref[hh], w_suffix) + r_in))
                    g = _dot_nt(_pick_head(do_b, head0, h), v_b) * a
                    ab_ref[hh] = a.astype(MM)
                    g_ref[hh] = g
                    gb_ref[hh] = g.astype(MM)
                    sig2_ref[hh] = sig1_ref[hh]
                    z = _dot_nt(_pick_head(q_a, head0, h), k_a) + bias_a
                    sp = _softplus(z)
                    sig1_ref[hh] = jnp.exp(z - sp)
                    z_ref[hh] = z
                    spb_ref[hh] = sp.astype(MM)

        _block_pipeline(4, False, step)
        dq_ref[...] = (dqa_ref[...] * ATT_SCALE).astype(MM)
        dk_ref[...] = dka_ref[...].astype(MM)
        dv_ref[...] = dva_ref[...].astype(MM)
        if ns:
            sc.finish()

    out = jax.ShapeDtypeStruct((SEQ, ATT_DIM), MM)
    return pl.pallas_call(
        body, name="attn_bwd", out_shape=[out, out, out] + _Scatter.out_shapes(scatter),
        in_specs=[VMEM_SPEC] * 5 + [ANY] * ns, out_specs=[VMEM_SPEC] * 3 + [ANY] * ns,
        scratch_shapes=[pltpu.VMEM((SEQ, ATT_DIM), F32)] * 3 + [pltpu.VMEM((8, TQ, 128), F32)] * 5
                       + [pltpu.VMEM((8, TQ, 128), MM)] * 4 + (_Scatter.scratch(ns) if ns else []),
        compiler_params=_params(),
    )(q, k, v, datt, rc, *scatter)


def in_proj_bwd(dproj, w_in_g, x, dx2, g1, scatter=()):
    ns = len(scatter)
    nt = SEQ // TM

    def body(*refs):
        dp_ref, w_ref, x_ref, dx2_ref, g_ref = refs[:5]
        dx_ref, dg_ref = refs[5 + ns:7 + ns]
        if ns:
            sc = _Scatter(refs[5:5 + ns], refs[7 + ns:7 + 2 * ns], refs[7 + 2 * ns:])
            pl.when(pl.program_id(0) == 0)(sc.start)
        dh = None
        for j in range(N_CHIPS):
            t = _dot_nt(dp_ref[:, j * IN_SHARD:(j + 1) * IN_SHARD], w_ref[j])
            dh = t if dh is None else dh + t
        n1, r1 = _rms(x_ref[...])
        dx_ref[...] = dx2_ref[...] + _rms_bwd(dh * g_ref[...], n1, r1)
        _acc_rows(dg_ref, jnp.sum(dh * n1, axis=0, keepdims=True), pl.program_id(0) == 0)
        if ns:
            pl.when(pl.program_id(0) == nt - 1)(sc.finish)

    vec = _full_spec((1, D_MODEL))
    return pl.pallas_call(
        body, name="in_proj_bwd", grid=(nt,),
        out_shape=[jax.ShapeDtypeStruct((SEQ, D_MODEL), F32), jax.ShapeDtypeStruct((1, D_MODEL), F32)]
                  + _Scatter.out_shapes(scatter),
        in_specs=[_row_tile_spec(IN_COLS), _full_spec(w_in_g.shape), _row_tile_spec(D_MODEL),
                  _row_tile_spec(D_MODEL), vec] + [ANY] * ns,
        out_specs=[_row_tile_spec(D_MODEL), vec] + [ANY] * ns,
        scratch_shapes=_Scatter.scratch(ns) if ns else [],
        compiler_params=_params(("arbitrary",)),
    )(dproj, w_in_g, x, dx2, g1, *scatter)


def weight_grad(a, b, name, col_sharded, tk=None):
    kin, n = a.shape[1], b.shape[1]

    def body(a_ref, b_ref, o_ref):
        if col_sharded:
            o_ref[0, 0] = _dot_tn(a_ref[...], b_ref[...]).astype(MM)
        else:
            o_ref[...] = _dot_tn(a_ref[...], b_ref[...]).astype(MM)

    if col_sharded:
        kh, ns = kin // 2, n // N_CHIPS
        out = jax.ShapeDtypeStruct((N_CHIPS, 2, kh, ns), MM)
        grid = (2, N_CHIPS)
        in_specs = [pl.BlockSpec((SEQ, kh), lambda h, j: (0, h)), pl.BlockSpec((SEQ, ns), lambda h, j: (0, j))]
        out_spec = pl.BlockSpec((1, 1, kh, ns), lambda h, j: (j, h, 0, 0))
        sem = ("arbitrary", "arbitrary")
    else:
        out = jax.ShapeDtypeStruct((kin, n), MM)
        grid = (kin // tk,)
        in_specs = [pl.BlockSpec((SEQ, tk), lambda r: (0, r)), pl.BlockSpec((SEQ, n), lambda r: (0, 0))]
        out_spec = pl.BlockSpec((tk, n), lambda r: (r, 0))
        sem = ("arbitrary",)
    res = pl.pallas_call(
        body, name=name, grid=grid, out_shape=out, in_specs=in_specs, out_specs=out_spec,
        compiler_params=_params(sem),
    )(a, b)
    if not col_sharded:
        res = res.reshape(N_CHIPS, 2, kin // (2 * N_CHIPS), n)
    return res


def _place():
    x, y, c = lax.axis_index("x"), lax.axis_index("y"), lax.axis_index("c")
    chips = [(1 - x, y), (x, 1 - y), (1 - x, 1 - y)]
    return x, y, c, chips


def _rcopy(src, dst, send_sem, recv_sem, dev):
    return pltpu.make_async_remote_copy(src_ref=src, dst_ref=dst, send_sem=send_sem, recv_sem=recv_sem,
                                        device_id=dev, device_id_type=MESH)


class _Gather:
    def __init__(self, shapes, w, o, scratch):
        self.n, self.shapes, self.w, self.o = len(w), shapes, w, o
        self.send, self.recv, self.fsend, self.frecv, self.loc_in, self.loc_out = scratch[:6]
        self.stage = scratch[6:]
        self.x, self.y, self.c, self.chips = _place()
        self.me = 2 * self.x + self.y
        self.sib = (self.x, self.y, 1 - self.c)
        self.pairs = [(j, t) for j in range(3) for t in range(self.n)]

    @staticmethod
    def scratch(shards):
        n = len(shards)
        sems = pltpu.SemaphoreType.DMA
        return ([sems((3 * n,)), sems((3 * n,)), sems((3 * n,)), sems((3 * n,)), sems((n,)), sems((n,))]
                + [pltpu.VMEM(s.shape, s.dtype) for s in shards])

    def _half(self, t, k, cc):
        rh = self.shapes[t][0] // 2
        return self.o[t].at[k, pl.ds(cc * rh, rh), :]

    def _chip(self, j):
        cx, cy = self.chips[j]
        return 2 * cx + cy, (cx, cy, self.c)

    def local_in(self, t):
        return pltpu.make_async_copy(self.w[t], self.stage[t], self.loc_in.at[t])

    def local_out(self, t):
        return pltpu.make_async_copy(self.stage[t], self.o[t].at[self.me], self.loc_out.at[t])

    def first(self, j, t):
        rh = self.shapes[t][0] // 2
        i = j * self.n + t
        return _rcopy(self.w[t].at[pl.ds(self.c * rh, rh), :], self._half(t, self.me, self.c),
                      self.send.at[i], self.recv.at[i], self._chip(j)[1])

    def arrived(self, j, t):
        k, dev = self._chip(j)
        i = j * self.n + t
        blk = self._half(t, k, self.c)
        return _rcopy(blk, blk, self.send.at[i], self.recv.at[i], dev)

    def passed(self, j, t, cc):
        i = j * self.n + t
        blk = self._half(t, self._chip(j)[0], cc)
        return _rcopy(blk, blk, self.fsend.at[i], self.frecv.at[i], self.sib)

    def start(self):
        for t in range(self.n):
            self.local_in(t).start()
        for j, t in self.pairs:
            self.first(j, t).start()

    def forward(self):
        for t in range(self.n):
            self.local_in(t).wait()
            self.local_out(t).start()
        for j, t in self.pairs:
            self.arrived(j, t).wait_recv()
            self.passed(j, t, self.c).start()

    def finish(self):
        for j, t in self.pairs:
            self.passed(j, t, 1 - self.c).wait_recv()
        for j, t in self.pairs:
            self.first(j, t).wait_send()
            self.passed(j, t, self.c).wait_send()
        for t in range(self.n):
            self.local_out(t).wait()


def all_gather_weights(shards, small):
    n = len(shards)
    shapes = [s.shape for s in shards]

    def body(*refs):
        w = refs[:n]
        sm = refs[n]
        o = refs[n + 1:2 * n + 1]
        osm = refs[2 * n + 1]
        ssend, srecv, sloc = refs[2 * n + 2:2 * n + 5]
        g = _Gather(shapes, w, o, refs[2 * n + 5:])
        own = pltpu.make_async_copy(sm, osm.at[g.me], sloc)
        own.start()
        g.start()
        small_cps = [_rcopy(sm, osm.at[g.me], ssend.at[j], srecv.at[j], g._chip(j)[1]) for j in range(3)]
        for cp in small_cps:
            cp.start()
        g.forward()
        g.finish()
        for j in range(3):
            k, dev = g._chip(j)
            _rcopy(sm, osm.at[k], ssend.at[j], srecv.at[j], dev).wait_recv()
            small_cps[j].wait_send()
        own.wait()

    out_shape = [jax.ShapeDtypeStruct((N_CHIPS,) + s.shape, s.dtype) for s in shards]
    out_shape.append(jax.ShapeDtypeStruct((N_CHIPS,) + small.shape, small.dtype))
    sems = pltpu.SemaphoreType.DMA
    return pl.pallas_call(
        body, name="all_gather_weights", out_shape=out_shape,
        in_specs=[ANY] * (n + 1), out_specs=[ANY] * (n + 1),
        scratch_shapes=[sems((3,)), sems((3,)), sems] + _Gather.scratch(shards),
        compiler_params=_params(),
    )(*shards, small)


def sibling_exchange(grads, name):
    n = len(grads)

    def body(*refs):
        g = refs[:n]
        o = refs[n:2 * n]
        send, recv = refs[2 * n:]
        x, y, c, _ = _place()
        cps = [_rcopy(g[t].at[:, 1 - c], o[t], send.at[t], recv.at[t], (x, y, 1 - c)) for t in range(n)]
        for cp in cps:
            cp.start()
        for cp in cps:
            cp.wait()

    sems = pltpu.SemaphoreType.DMA
    return pl.pallas_call(
        body, name=name,
        out_shape=[jax.ShapeDtypeStruct((a.shape[0],) + a.shape[2:], a.dtype) for a in grads],
        in_specs=[ANY] * n, out_specs=[ANY] * n, scratch_shapes=[sems((n,)), sems((n,))],
    )(*grads)


class _Scatter:
    def __init__(self, p, o, sems):
        self.n, self.p, self.o = len(p), p, o
        self.send, self.recv = sems
        _, _, self.c, self.chips = _place()

    @staticmethod
    def scratch(n):
        sems = pltpu.SemaphoreType.DMA
        return [sems((3 * n,)), sems((3 * n,))]

    @staticmethod
    def out_shapes(parts):
        return [jax.ShapeDtypeStruct((3,) + a.shape[1:], a.dtype) for a in parts]

    def copies(self):
        cps = []
        for j, (cx, cy) in enumerate(self.chips):
            for t in range(self.n):
                i = j * self.n + t
                cps.append(_rcopy(self.p[t].at[2 * cx + cy], self.o[t].at[j], self.send.at[i], self.recv.at[i],
                                  (cx, cy, self.c)))
        return cps

    def start(self):
        for cp in self.copies():
            cp.start()

    def finish(self):
        for cp in self.copies():
            cp.wait()


def sibling_swap(halves, name):
    n = len(halves)

    def body(*refs):
        h = refs[:n]
        o = refs[n:2 * n]
        send, recv = refs[2 * n:]
        x, y, c, _ = _place()
        cps = [_rcopy(h[t], o[t], send.at[t], recv.at[t], (x, y, 1 - c)) for t in range(n)]
        for cp in cps:
            cp.start()
        for cp in cps:
            cp.wait()

    sems = pltpu.SemaphoreType.DMA
    return pl.pallas_call(
        body, name=name, out_shape=[jax.ShapeDtypeStruct(a.shape, a.dtype) for a in halves],
        in_specs=[ANY] * n, out_specs=[ANY] * n, scratch_shapes=[sems((n,)), sems((n,))],
    )(*halves)


def small_all_reduce(ddw, v512, v1024, loss_parts):
    rows, width = PACK_ROWS, 512
    n512, n1024 = len(VEC512), len(VEC1024)

    def body(*refs):
        ddw_ref = refs[0]
        a_refs = refs[1:1 + n512]
        b_refs = refs[1 + n512:1 + n512 + n1024]
        lp_ref, o_ref, p_ref, gath_ref, send, recv = refs[1 + n512 + n1024:]
        p_ref[...] = jnp.zeros_like(p_ref)
        p_ref[0:32, :] = ddw_ref[...]
        p_ref[LOSS_ROW:LOSS_ROW + 1, 0:128] = jnp.sum(lp_ref[...], axis=0, keepdims=True) * 0.125
        for i, r in enumerate(a_refs):
            p_ref[32 + i:33 + i, :] = r[...]
        for i, r in enumerate(b_refs):
            base = 32 + n512 + 2 * i
            p_ref[base:base + 1, :] = r[:, 0:512]
            p_ref[base + 1:base + 2, :] = r[:, 512:1024]
        x, y, c, _ = _place()
        me = 4 * x + 2 * y + c
        gath_ref[me] = p_ref[...]
        cps = []
        for k in range(1, 8):
            dx, dy, dc = (k >> 2) & 1, (k >> 1) & 1, k & 1
            px = 1 - x if dx else x
            py = 1 - y if dy else y
            pc = 1 - c if dc else c
            cps.append(_rcopy(p_ref, gath_ref.at[me], send.at[k - 1], recv.at[k - 1], (px, py, pc)))
        for cp in cps:
            cp.start()
        for k in range(1, 8):
            dx, dy, dc = (k >> 2) & 1, (k >> 1) & 1, k & 1
            px = 1 - x if dx else x
            py = 1 - y if dy else y
            pc = 1 - c if dc else c
            _rcopy(p_ref, gath_ref.at[4 * px + 2 * py + pc], send.at[k - 1], recv.at[k - 1], (px, py, pc)).wait_recv()
        for cp in cps:
            cp.wait_send()
        total = gath_ref[0]
        for d in range(1, 8):
            total = total + gath_ref[d]
        o_ref[...] = total

    sems = pltpu.SemaphoreType.DMA
    n_in = 2 + n512 + n1024
    return pl.pallas_call(
        body, name="small_all_reduce", out_shape=jax.ShapeDtypeStruct((rows, width), F32),
        in_specs=[VMEM_SPEC] * n_in, out_specs=VMEM_SPEC,
        scratch_shapes=[pltpu.VMEM((rows, width), F32), pltpu.VMEM((8, rows, width), F32), sems((7,)), sems((7,))],
    )(ddw, *[v512[n] for n in VEC512], *[v1024[n] for n in VEC1024], loss_parts)


def _row_block(r):
    for tr in (512, 352, 256, 128):
        if r % tr == 0:
            return tr
    return r


def add_halves(g, recv, name):
    _, _, r, w = g.shape
    tr = _row_block(r)

    def body(g0_ref, g1_ref, r_ref, ob_ref, own_ref):
        k = pl.program_id(1)
        c = lax.axis_index("c")
        me = 2 * lax.axis_index("x") + lax.axis_index("y")
        t = jnp.where(c == 0, g0_ref[0, 0], g1_ref[0, 0]).astype(F32) + r_ref[0].astype(F32)
        ob_ref[0] = t.astype(MM)
        mine = jnp.where(k == me, t, 0.0)

        @pl.when(k == 0)
        def _():
            own_ref[...] = mine

        @pl.when(k != 0)
        def _():
            own_ref[...] += mine

    return pl.pallas_call(
        body, name=name, grid=(r // tr, N_CHIPS),
        in_specs=[pl.BlockSpec((1, 1, tr, w), lambda i, k: (k, 0, i, 0)),
                  pl.BlockSpec((1, 1, tr, w), lambda i, k: (k, 1, i, 0)),
                  pl.BlockSpec((1, tr, w), lambda i, k: (k, i, 0))],
        out_specs=[pl.BlockSpec((1, tr, w), lambda i, k: (k, i, 0)),
                   pl.BlockSpec((tr, w), lambda i, k: (i, 0))],
        out_shape=(jax.ShapeDtypeStruct((N_CHIPS, r, w), MM), jax.ShapeDtypeStruct((r, w), F32)),
        compiler_params=_params(("arbitrary", "arbitrary")),
    )(g, g, recv)


def sum_parts(own, rin, name):
    _, r, w = rin.shape
    tr = _row_block(r)

    def body(o_ref, r_ref, out_ref):
        out_ref[...] = ((o_ref[...] + r_ref[0].astype(F32)) + r_ref[1].astype(F32)) + r_ref[2].astype(F32)

    return pl.pallas_call(
        body, name=name, grid=(r // tr,), out_shape=jax.ShapeDtypeStruct((r, w), F32),
        in_specs=[pl.BlockSpec((tr, w), lambda i: (i, 0)), pl.BlockSpec((3, tr, w), lambda i: (0, i, 0))],
        out_specs=pl.BlockSpec((tr, w), lambda i: (i, 0)),
        compiler_params=_params(("arbitrary",)),
    )(own, rin)


def _adamw_math(w, g, m, v):
    mn = ADAM_B1 * m + (1.0 - ADAM_B1) * g
    vn = ADAM_B2 * v + (1.0 - ADAM_B2) * (g * g)
    m_hat = mn / (1.0 - ADAM_B1 ** ADAM_STEP)
    v_hat = vn / (1.0 - ADAM_B2 ** ADAM_STEP)
    return -ADAM_LR * (m_hat / (jnp.sqrt(v_hat) + ADAM_EPS) + ADAM_WD * w), mn, vn


def adamw(w, mine, other, m, v, name):
    r, c = w.shape
    rh = r // 2
    tr = _row_block(rh)
    if c >= 1024 and tr % 512 == 0:
        tr = 256
    nb = rh // tr

    def body(w_ref, a_ref, b_ref, m_ref, v_ref, go_ref, d_ref, mo_ref, vo_ref):
        gv = jnp.where(lax.axis_index("c") == pl.program_id(0), a_ref[...], b_ref[...])
        go_ref[...] = gv
        d_ref[...], mo_ref[...], vo_ref[...] = _adamw_math(w_ref[...], gv, m_ref[...], v_ref[...])

    spec = pl.BlockSpec((tr, c), lambda h, i: (h * nb + i, 0))
    half = pl.BlockSpec((tr, c), lambda h, i: (i, 0))
    out = jax.ShapeDtypeStruct((r, c), F32)
    return pl.pallas_call(
        body, name=name, grid=(2, nb), out_shape=(out, out, out, out),
        in_specs=[spec, half, half, spec, spec], out_specs=[spec] * 4,
        compiler_params=_params(("arbitrary", "arbitrary")),
    )(w, mine, other, m, v)


def adamw_small(gsum, params):
    names = list(params)
    flat = [a for n in names for a in params[n]]

    def body(*refs):
        g_ref = refs[0]
        ins = refs[1:1 + 3 * len(names)]
        outs = refs[1 + 3 * len(names):]
        me = 2 * lax.axis_index("x") + lax.axis_index("y")
        for i, n in enumerate(names):
            w_ref, m_ref, v_ref = ins[3 * i:3 * i + 3]
            go_ref, d_ref, mo_ref, vo_ref = outs[4 * i:4 * i + 4]
            if n == "conv_dw_w":
                gv = jnp.zeros((CONV_WIDTH, 128), F32)
                for k in range(N_CHIPS):
                    gv = gv + jnp.where(me == k, g_ref[0:CONV_WIDTH, 128 * k:128 * (k + 1)], 0.0)
            elif n in VEC512:
                r0 = 32 + VEC512.index(n)
                gv = g_ref[r0:r0 + 1, :]
            else:
                r0 = 32 + len(VEC512) + 2 * VEC1024.index(n)
                gv = jnp.concatenate([g_ref[r0:r0 + 1, :], g_ref[r0 + 1:r0 + 2, :]], axis=1)
            go_ref[...] = gv
            d_ref[...], mo_ref[...], vo_ref[...] = _adamw_math(w_ref[...], gv, m_ref[...], v_ref[...])

    out_shape = [jax.ShapeDtypeStruct(params[n][0].shape, F32) for n in names for _ in range(4)]
    res = pl.pallas_call(
        body, name="adamw_small", out_shape=out_shape,
        in_specs=[VMEM_SPEC] * (1 + len(flat)), out_specs=[VMEM_SPEC] * len(out_shape),
        compiler_params=_params(),
    )(gsum, *flat)
    return {n: res[4 * i:4 * i + 4] for i, n in enumerate(names)}


REST = ("w_ffn_up", "w_ffn_down", "w_out", "w_conv_branch", "w_att_branch")
VEC512 = ("conv_dw_b", "conv_ln_g", "conv_ln_b")
VEC1024 = ("norm_mix_pre", "b_conv_branch", "norm_mix_post", "norm_ffn_pre", "norm_ffn_post")
PACK_ROWS = 48
LOSS_ROW = 47


def kernel(x, norm_mix_pre, w_in, conv_dw_w, conv_dw_b, conv_ln_g, conv_ln_b, w_conv_branch, b_conv_branch, w_att_branch, w_out, norm_mix_post, norm_ffn_pre, w_ffn_up, w_ffn_down, norm_ffn_post, loss_target, m_norm_mix_pre, m_w_in, m_conv_dw_w, m_conv_dw_b, m_conv_ln_g, m_conv_ln_b, m_w_conv_branch, m_b_conv_branch, m_w_att_branch, m_w_out, m_norm_mix_post, m_norm_ffn_pre, m_w_ffn_up, m_w_ffn_down, m_norm_ffn_post, v_norm_mix_pre, v_w_in, v_conv_dw_w, v_conv_dw_b, v_conv_ln_g, v_conv_ln_b, v_w_conv_branch, v_b_conv_branch, v_w_att_branch, v_w_out, v_norm_mix_post, v_norm_ffn_pre, v_w_ffn_up, v_w_ffn_down, v_norm_ffn_post):
    weights = dict(norm_mix_pre=norm_mix_pre, w_in=w_in, conv_dw_w=conv_dw_w, conv_dw_b=conv_dw_b, conv_ln_g=conv_ln_g, conv_ln_b=conv_ln_b, w_conv_branch=w_conv_branch, b_conv_branch=b_conv_branch, w_att_branch=w_att_branch, w_out=w_out, norm_mix_post=norm_mix_post, norm_ffn_pre=norm_ffn_pre, w_ffn_up=w_ffn_up, w_ffn_down=w_ffn_down, norm_ffn_post=norm_ffn_post)
    mom = dict(norm_mix_pre=m_norm_mix_pre, w_in=m_w_in, conv_dw_w=m_conv_dw_w, conv_dw_b=m_conv_dw_b, conv_ln_g=m_conv_ln_g, conv_ln_b=m_conv_ln_b, w_conv_branch=m_w_conv_branch, b_conv_branch=m_b_conv_branch, w_att_branch=m_w_att_branch, w_out=m_w_out, norm_mix_post=m_norm_mix_post, norm_ffn_pre=m_norm_ffn_pre, w_ffn_up=m_w_ffn_up, w_ffn_down=m_w_ffn_down, norm_ffn_post=m_norm_ffn_post)
    var = dict(norm_mix_pre=v_norm_mix_pre, w_in=v_w_in, conv_dw_w=v_conv_dw_w, conv_dw_b=v_conv_dw_b, conv_ln_g=v_conv_ln_g, conv_ln_b=v_conv_ln_b, w_conv_branch=v_w_conv_branch, b_conv_branch=v_b_conv_branch, w_att_branch=v_w_att_branch, w_out=v_w_out, norm_mix_post=v_norm_mix_post, norm_ffn_pre=v_norm_ffn_pre, w_ffn_up=v_w_ffn_up, w_ffn_down=v_w_ffn_down, norm_ffn_post=v_norm_ffn_post)
    order = list(weights)
    grads, deltas, new_m, new_v = {}, {}, {}, {}
    xs = x.reshape(SEQ, D_MODEL)
    tgt = loss_target.reshape(SEQ, D_MODEL)
    row = lambda a: a.reshape(1, -1)
    g1, g2, g3, g4 = (row(weights[n]) for n in ("norm_mix_pre", "norm_mix_post", "norm_ffn_pre", "norm_ffn_post"))
    ln_g, ln_b = row(conv_ln_g), row(conv_ln_b)

    def reduce_prepare(names, partial, tag):
        from_sib = sibling_exchange([partial[n] for n in names], "sibling_exchange_" + tag)
        return [add_halves(partial[n], r, "add_" + n) for n, r in zip(names, from_sib)]

    def reduce_finish(names, summed, from_chips, tag):
        halves = [sum_parts(s[1], r, "sum_" + n) for n, s, r in zip(names, summed, from_chips)]
        for n, a, b in zip(names, halves, sibling_swap(halves, "sibling_swap_" + tag)):
            grads[n], deltas[n], new_m[n], new_v[n] = adamw(weights[n], a, b, mom[n], var[n], "adamw_" + n)

    w_in_g, dw_g = all_gather_weights([w_in.astype(MM)], conv_dw_w)
    w_dw_full = jnp.concatenate([dw_g[k] for k in range(N_CHIPS)], axis=1)
    shard = lambda n: weights[n].astype(MM)
    h1, ci, q, k, v, gc, ga, w_out_g, w_cb_g, w_ab_g = in_proj_fwd(
        xs, g1, w_in_g, [shard("w_out"), shard("w_conv_branch"), shard("w_att_branch")])
    u1, u3 = conv_fwd(ci, w_dw_full, row(conv_dw_b), ln_g, ln_b)
    att, rc, w_up_g = attn_fwd(q, k, v, [shard("w_ffn_up")])
    w_out_g = w_out_g.reshape(D_MODEL, D_MODEL)
    co, ao, merged, mix, x2, h2, w_down_g = mix_fwd(u3, att, gc, ga, xs, w_cb_g, row(b_conv_branch), w_ab_g, w_out_g,
                                                    g2, g3, [shard("w_ffn_down")])
    w_down_g = w_down_g.reshape(D_FF, D_MODEL)
    gate, up, act = ffn_up_fwd(h2, w_up_g)
    dff, dy, loss_parts, dg4 = ffn_down_loss(act, w_down_g, x2, tgt, g4)

    partial = {}
    dgu = ffn_act_bwd(dff, w_down_g, gate, up)
    partial["w_ffn_down"] = weight_grad(act, dff, "dw_ffn_down", False, tk=UP_SHARD)
    dx2, dmix, dg3, dg2 = ffn_in_bwd(dgu, w_up_g, x2, mix, dy, g3, g2)
    partial["w_ffn_up"] = weight_grad(h2, dgu, "dw_ffn_up", True)
    dco, dao, dgc, dga, du3, datt, dbcb = merge_bwd(dmix, w_out_g, gc, ga, co, ao, w_cb_g, w_ab_g)
    partial["w_out"] = weight_grad(merged, dmix, "dw_out", False, tk=512)
    partial["w_conv_branch"] = weight_grad(u3, dco, "dw_conv_branch", True)
    partial["w_att_branch"] = weight_grad(att, dao, "dw_att_branch", True)
    summed = reduce_prepare(REST, partial, "rest")
    dci, ddw, dbdw, dlng, dlnb = conv_bwd(du3, u1, ci, w_dw_full, ln_g, ln_b)
    dq, dk, dv, *from_chips = attn_bwd(q, k, v, datt, rc, [s[0] for s in summed])
    reduce_finish(REST, summed, from_chips, "rest")
    dproj = jnp.concatenate([dci, dq, dk, dv, dgc, dga], axis=1)
    partial["w_in"] = weight_grad(h1, dproj, "dw_in", True)
    summed = reduce_prepare(("w_in",), partial, "w_in")
    grad_x, dg1, *from_chips = in_proj_bwd(dproj, w_in_g, xs, dx2, g1, [summed[0][0]])
    reduce_finish(("w_in",), summed, from_chips, "w_in")

    v512 = dict(conv_dw_b=dbdw, conv_ln_g=dlng, conv_ln_b=dlnb)
    v1024 = dict(norm_mix_pre=dg1, b_conv_branch=dbcb, norm_mix_post=dg2, norm_ffn_pre=dg3, norm_ffn_post=dg4)
    gsum = small_all_reduce(ddw, v512, v1024, loss_parts)
    loss = gsum[LOSS_ROW, 0]
    as_rows = lambda n, a: a if n == "conv_dw_w" else a.reshape(1, -1)
    small_names = ("conv_dw_w",) + VEC512 + VEC1024
    small = adamw_small(gsum, {n: tuple(as_rows(n, d[n]) for d in (weights, mom, var)) for n in small_names})
    for n in small_names:
        grads[n], deltas[n], new_m[n], new_v[n] = (a.reshape(weights[n].shape) for a in small[n])

    return (loss, grad_x.reshape(1, SEQ, D_MODEL), *[grads[n] for n in order], *[deltas[n] for n in order],
            *[new_m[n] for n in order], *[new_v[n] for n in order])
```

```python
import jax
import jax.numpy as jnp
from jax import lax
from jax.experimental import pallas as pl
from jax.experimental.pallas import tpu as pltpu

F32 = jnp.float32
MM = jnp.bfloat16

SEQ = 2048
D_MODEL = 1024
CONV_DIM = 512
ATT_DIM = 512
CONV_WIDTH = 31
D_FF = 2816
IN_COLS = 2 * CONV_DIM + 3 * ATT_DIM + 2 * D_MODEL
N_CHIPS = 4
IN_SHARD = IN_COLS // N_CHIPS
UP_SHARD = 2 * D_FF // N_CHIPS
BR_SHARD = D_MODEL // N_CHIPS
EPS = 1e-6
ATT_SCALE = 0.125

TM = 256
GLU_ROWS = 256
TQ = 128
CONV_TILE = 64
CONV_WIN = CONV_TILE + 32
VMEM_LIMIT = 56 * 1024 * 1024

ADAM_LR = 0.001
ADAM_B1 = 0.9
ADAM_B2 = 0.999
ADAM_EPS = 1e-08
ADAM_WD = 0.01
ADAM_STEP = 10

MESH = pl.DeviceIdType.MESH
ANY = pl.BlockSpec(memory_space=pl.ANY)
VMEM_SPEC = pl.BlockSpec(memory_space=pltpu.VMEM)

NT_DIMS = (((1,), (1,)), ((), ()))
TN_DIMS = (((0,), (0,)), ((), ()))

IN_PIECES = (("ci", 0, 1024), ("q", 1024, 1536), ("k", 1536, 2048), ("v", 2048, 2560),
             ("gc", 2560, 3584), ("ga", 3584, 4608))


def _params(sem=None, vmem=VMEM_LIMIT):
    return pltpu.CompilerParams(dimension_semantics=sem, vmem_limit_bytes=vmem)


def _dot(a, b):
    return jnp.dot(a, b, preferred_element_type=F32)


def _dot_nt(a, b):
    return lax.dot_general(a, b, NT_DIMS, preferred_element_type=F32)


def _dot_tn(a, b):
    return lax.dot_general(a, b, TN_DIMS, preferred_element_type=F32)


def _sigmoid(x):
    return 1.0 / (1.0 + jnp.exp(-x))


def _rms(x):
    r = lax.rsqrt(jnp.mean(x * x, axis=-1, keepdims=True) + EPS)
    return x * r, r


def _rms_bwd(dy_g, n, r):
    return r * (dy_g - n * jnp.mean(dy_g * n, axis=-1, keepdims=True))


def _row_tile_spec(width, tm=TM):
    return pl.BlockSpec((tm, width), lambda i: (i, 0))


def _full_spec(shape):
    nd = len(shape)
    return pl.BlockSpec(shape, lambda *_: (0,) * nd)


def _weight_spec(shape):
    nd = len(shape)
    return pl.BlockSpec(shape, lambda *_: (0,) * nd, pipeline_mode=pl.Buffered(1))


def _acc_rows(ref, val, first):
    @pl.when(first)
    def _():
        ref[...] = val

    @pl.when(jnp.logical_not(first))
    def _():
        ref[...] += val


def _gather_behind_grid(ag, n_steps):
    step = pl.program_id(0)
    pl.when(step == 0)(ag.start)
    pl.when(step == n_steps - 2)(ag.forward)
    return lambda: pl.when(step == n_steps - 1)(ag.finish)


def in_proj_fwd(x, g1, w_in_g, gather=()):
    ng = len(gather)
    nt = SEQ // TM

    def body(*refs):
        x_ref, g_ref, w_ref = refs[:3]
        h_ref, ci_ref, q_ref, k_ref, v_ref, gc_ref, ga_ref = refs[3 + ng:10 + ng]
        if ng:
            done = _gather_behind_grid(_Gather([s.shape for s in gather], refs[3:3 + ng],
                                               refs[10 + ng:10 + 2 * ng], refs[10 + 2 * ng:]), nt)
        n, _ = _rms(x_ref[...])
        h = (n * g_ref[...]).astype(MM)
        h_ref[...] = h
        outs = dict(ci=ci_ref, q=q_ref, k=k_ref, v=v_ref, gc=gc_ref, ga=ga_ref)
        for j in range(N_CHIPS):
            p = _dot(h, w_ref[j])
            g0 = j * IN_SHARD
            for name, s, e in IN_PIECES:
                lo, hi = max(s, g0), min(e, g0 + IN_SHARD)
                if lo < hi:
                    ref = outs[name]
                    part = p[:, lo - g0:hi - g0]
                    if name == "q":
                        part = part * ATT_SCALE
                    ref[:, lo - s:hi - s] = part.astype(ref.dtype)
        if ng:
            done()

    out_shape = [
        jax.ShapeDtypeStruct((SEQ, D_MODEL), MM),
        jax.ShapeDtypeStruct((SEQ, 2 * CONV_DIM), F32),
        jax.ShapeDtypeStruct((SEQ, ATT_DIM), MM),
        jax.ShapeDtypeStruct((SEQ, ATT_DIM), MM),
        jax.ShapeDtypeStruct((SEQ, ATT_DIM), MM),
        jax.ShapeDtypeStruct((SEQ, D_MODEL), F32),
        jax.ShapeDtypeStruct((SEQ, D_MODEL), F32),
    ]
    return pl.pallas_call(
        body, name="in_proj_fwd", grid=(nt,),
        out_shape=out_shape + _Gather.out_shapes(gather),
        in_specs=[_row_tile_spec(D_MODEL), _full_spec((1, D_MODEL)), _weight_spec(w_in_g.shape)] + [ANY] * ng,
        out_specs=[_row_tile_spec(s.shape[1]) for s in out_shape] + [ANY] * ng,
        scratch_shapes=_Gather.scratch(gather) if ng else [],
        compiler_params=_params(("arbitrary",)),
    )(x, g1, w_in_g, *gather)


def _shifted_sum(win, terms):
    by_rot = {}
    for m, coef in terms:
        by_rot.setdefault(m % 8, []).append((m // 8, coef))
    acc = None
    n = win.shape[0]
    for rot in sorted(by_rot):
        shifted = win if rot == 0 else pltpu.roll(win, n - rot, 0)
        for a, coef in by_rot[rot]:
            t = coef * shifted[8 * a:8 * a + CONV_TILE, :]
            acc = t if acc is None else acc + t
    return acc


def _glu_into(ci_ref, upad_ref):
    upad_ref[0:32, :] = jnp.zeros((32, CONV_DIM), F32)

    def step(i, c):
        t0 = pl.multiple_of(i * GLU_ROWS, GLU_ROWS)
        a = ci_ref[pl.ds(t0, GLU_ROWS), 0:CONV_DIM]
        b = ci_ref[pl.ds(t0, GLU_ROWS), CONV_DIM:2 * CONV_DIM]
        upad_ref[pl.ds(t0 + 32, GLU_ROWS), :] = a * _sigmoid(b)
        return c

    lax.fori_loop(0, SEQ // GLU_ROWS, step, 0)


def _layernorm_parts(u1):
    mu = jnp.mean(u1, axis=-1, keepdims=True)
    xc = u1 - mu
    rstd = lax.rsqrt(jnp.mean(xc * xc, axis=-1, keepdims=True) + EPS)
    return xc * rstd, rstd


def conv_fwd(ci, w_dw, b_dw, ln_g, ln_b):
    def body(ci_ref, w_ref, b_ref, g_ref, bb_ref, u1_ref, u3_ref, upad_ref):
        _glu_into(ci_ref, upad_ref)

        def step(i, c):
            t0 = pl.multiple_of(i * CONV_TILE, CONV_TILE)
            win = upad_ref[pl.ds(t0, CONV_WIN), :]
            u1 = _shifted_sum(win, [(j + 2, w_ref[j:j + 1, :]) for j in range(CONV_WIDTH)]) + b_ref[...]
            u1_ref[pl.ds(t0, CONV_TILE), :] = u1
            xh, _ = _layernorm_parts(u1)
            u2 = xh * g_ref[...] + bb_ref[...]
            u3_ref[pl.ds(t0, CONV_TILE), :] = (u2 * _sigmoid(u2)).astype(MM)
            return c

        lax.fori_loop(0, SEQ // CONV_TILE, step, 0)

    return pl.pallas_call(
        body, name="conv_fwd",
        out_shape=(jax.ShapeDtypeStruct((SEQ, CONV_DIM), F32), jax.ShapeDtypeStruct((SEQ, CONV_DIM), MM)),
        in_specs=[VMEM_SPEC] * 5, out_specs=[VMEM_SPEC] * 2,
        scratch_shapes=[pltpu.VMEM((SEQ + 32, CONV_DIM), F32)],
        compiler_params=_params(),
    )(ci, w_dw, b_dw, ln_g, ln_b)


def _softplus(z):
    return jnp.maximum(z, 0.0) + jnp.log(1.0 + jnp.exp(-jnp.abs(z)))


def _cumsum_weights(suffix, with_total):
    n = 256 if with_total else 128
    r = lax.broadcasted_iota(jnp.int32, (128, n), 0)
    c = lax.broadcasted_iota(jnp.int32, (128, n), 1)
    tri = (r >= c) if suffix else (r <= c)
    return jnp.logical_or(tri, c >= 128).astype(MM)


NO_SCORE = -1e30
N_KB = SEQ // TQ


def _score_bias(lane, row, i, j):
    keep = jnp.logical_and(i >= 0, jnp.logical_or(j < i, lane < row))
    return jnp.where(keep, 0.0, NO_SCORE)


def _block_pipeline(n_stages, descending, step, on_query_block=None):
    n_lag = n_stages - 1
    none = jnp.int32(-1)

    def shift(cur, lag):
        step([cur] + [(lag[2 * s], lag[2 * s + 1]) for s in range(n_lag)])
        return (cur[0], cur[1]) + tuple(lag[:-2])

    def outer(i, lag):
        if on_query_block is not None:
            on_query_block(i)

        def inner(n, lag):
            return shift((i, i - n if descending else n), lag)
        return lax.fori_loop(0, i + 1, inner, lag)

    lag = lax.fori_loop(0, N_KB, outer, (none,) * (2 * n_lag))
    lax.fori_loop(0, n_lag, lambda n, lag: shift((none, none), lag), lag)


def _head_masks():
    lane = lax.broadcasted_iota(jnp.int32, (TQ, 128), 1)
    row = lax.broadcasted_iota(jnp.int32, (TQ, 128), 0)
    return lane, row, lane < 64


def _pick_head(x, head0, h):
    zero = jnp.zeros_like(x)
    return jnp.where(head0, x, zero) if h == 0 else jnp.where(head0, zero, x)


N_PAIRS = ATT_DIM // 128


def attn_fwd(q, k, v, gather=()):
    ng = len(gather)

    def body(*refs):
        q_ref, k_ref, v_ref = refs[:3]
        o_ref, rc_ref = refs[3 + ng:5 + ng]
        acc_ref, r_ref, z_ref, spb_ref, ab_ref = refs[5 + 2 * ng:10 + 2 * ng]
        if ng:
            ag = _Gather([s.shape for s in gather], refs[3:3 + ng], refs[5 + ng:5 + 2 * ng], refs[10 + 2 * ng:])
            ag.start()
        lane, row, head0 = _head_masks()
        w = _cumsum_weights(suffix=True, with_total=True)
        acc_ref[...] = jnp.zeros_like(acc_ref)
        r_ref[...] = jnp.zeros_like(r_ref)
        rc_ref[...] = jnp.zeros_like(rc_ref)
        z_ref[...] = jnp.full(z_ref.shape, NO_SCORE, F32)
        spb_ref[...] = jnp.zeros_like(spb_ref)
        ab_ref[...] = jnp.zeros_like(ab_ref)

        def step(pairs):
            (i1, j1), (i2, j2), (i3, j3) = pairs
            q1, k1, q2, q3, k3 = (pl.multiple_of(jnp.maximum(b, 0) * TQ, TQ) for b in (i1, j1, i2, i3, j3))
            bias1 = _score_bias(lane, row, i1, j1)
            first2 = j2 == i2
            rc_rows = rc_ref[pl.ds(q2, TQ), :]
            for p in range(N_PAIRS):
                cols = slice(128 * p, 128 * (p + 1))
                qb = q_ref[pl.ds(q1, TQ), cols]
                kb = k_ref[pl.ds(k1, TQ), cols]
                vb = v_ref[pl.ds(k3, TQ), cols]
                for h in range(2):
                    hh = 2 * p + h
                    acc_ref[pl.ds(q3, TQ), cols] += _dot(ab_ref[hh], _pick_head(vb, head0, h))
                    r = _dot(spb_ref[hh], w)
                    r_in = jnp.where(first2, 0.0, r_ref[hh])
                    ab_ref[hh] = jnp.exp(z_ref[hh] - (r[:, :128] + r_in)).astype(MM)
                    rc_rows = jnp.where(jnp.logical_and(lane == 16 * hh + j2, i2 >= 0), r_in, rc_rows)
                    r_ref[hh] = r_in + r[:, 128:]
                    z = _dot_nt(_pick_head(qb, head0, h), kb) + bias1
                    z_ref[hh] = z
                    spb_ref[hh] = _softplus(z).astype(MM)
            rc_ref[pl.ds(q2, TQ), :] = rc_rows

        if ng:
            _block_pipeline(3, True, step, lambda i: pl.when(i == N_KB - 2)(ag.forward))
        else:
            _block_pipeline(3, True, step)
        o_ref[...] = acc_ref[...].astype(MM)
        if ng:
            ag.finish()

    out_shape = [jax.ShapeDtypeStruct((SEQ, ATT_DIM), MM), jax.ShapeDtypeStruct((SEQ, 128), F32)]
    out_shape += _Gather.out_shapes(gather)
    return pl.pallas_call(
        body, name="attn_fwd", out_shape=out_shape,
        in_specs=[VMEM_SPEC] * 3 + [ANY] * ng, out_specs=[VMEM_SPEC] * 2 + [ANY] * ng,
        scratch_shapes=[pltpu.VMEM((SEQ, ATT_DIM), F32), pltpu.VMEM((8, TQ, 128), F32),
                        pltpu.VMEM((8, TQ, 128), F32), pltpu.VMEM((8, TQ, 128), MM), pltpu.VMEM((8, TQ, 128), MM)]
                       + (_Gather.scratch(gather) if ng else []),
        compiler_params=_params(),
    )(q, k, v, *gather)


def mix_fwd(u3, att, gc, ga, x, w_cb_g, b_cb, w_ab_g, w_out_g, g2, g3, gather=()):
    ng = len(gather)
    nt = SEQ // TM

    def body(*refs):
        u_ref, a_ref, gc_ref, ga_ref, x_ref, wcb_ref, bcb_ref, wab_ref, wout_ref, g2_ref, g3_ref = refs[:11]
        co_ref, ao_ref, mg_ref, mix_ref, x2_ref, h2_ref = refs[11 + ng:17 + ng]
        if ng:
            done = _gather_behind_grid(_Gather([s.shape for s in gather], refs[11:11 + ng],
                                               refs[17 + ng:17 + 2 * ng], refs[17 + 2 * ng:]), nt)
        u = u_ref[...]
        a = a_ref[...]
        co = jnp.concatenate([_dot(u, wcb_ref[j]) for j in range(N_CHIPS)], axis=1) + bcb_ref[...]
        ao = jnp.concatenate([_dot(a, wab_ref[j]) for j in range(N_CHIPS)], axis=1)
        co_ref[...] = co.astype(MM)
        ao_ref[...] = ao.astype(MM)
        merged = (_sigmoid(gc_ref[...]) * co + _sigmoid(ga_ref[...]) * ao).astype(MM)
        mg_ref[...] = merged
        mix = _dot(merged, wout_ref[...])
        mix_ref[...] = mix
        n2, _ = _rms(mix)
        x2 = x_ref[...] + n2 * g2_ref[...]
        x2_ref[...] = x2
        n3, _ = _rms(x2)
        h2_ref[...] = (n3 * g3_ref[...]).astype(MM)
        if ng:
            done()

    out_shape = [
        jax.ShapeDtypeStruct((SEQ, D_MODEL), MM), jax.ShapeDtypeStruct((SEQ, D_MODEL), MM),
        jax.ShapeDtypeStruct((SEQ, D_MODEL), MM), jax.ShapeDtypeStruct((SEQ, D_MODEL), F32),
        jax.ShapeDtypeStruct((SEQ, D_MODEL), F32), jax.ShapeDtypeStruct((SEQ, D_MODEL), MM),
    ]
    vec = _full_spec((1, D_MODEL))
    return pl.pallas_call(
        body, name="mix_fwd", grid=(nt,),
        out_shape=out_shape + _Gather.out_shapes(gather),
        in_specs=[_row_tile_spec(CONV_DIM), _row_tile_spec(ATT_DIM), _row_tile_spec(D_MODEL),
                  _row_tile_spec(D_MODEL), _row_tile_spec(D_MODEL), _weight_spec(w_cb_g.shape), vec,
                  _weight_spec(w_ab_g.shape), _weight_spec(w_out_g.shape), vec, vec] + [ANY] * ng,
        out_specs=[_row_tile_spec(D_MODEL)] * 6 + [ANY] * ng,
        scratch_shapes=_Gather.scratch(gather) if ng else [],
        compiler_params=_params(("arbitrary",)),
    )(u3, att, gc, ga, x, w_cb_g, b_cb, w_ab_g, w_out_g, g2, g3, *gather)


def ffn_up_fwd(h2, w_up_g):
    def body(h_ref, wg_ref, wu_ref, gate_ref, up_ref, act_ref):
        h = h_ref[...]
        gate = _dot(h, wg_ref[0])
        up = _dot(h, wu_ref[0])
        gate_ref[...] = gate.astype(MM)
        up_ref[...] = up.astype(MM)
        act_ref[...] = (gate * _sigmoid(gate) * up).astype(MM)

    tile = pl.BlockSpec((TM, UP_SHARD), lambda n, i: (i, n))
    return pl.pallas_call(
        body, name="ffn_up_fwd", grid=(2, SEQ // TM),
        out_shape=(jax.ShapeDtypeStruct((SEQ, D_FF), MM), jax.ShapeDtypeStruct((SEQ, D_FF), MM),
                   jax.ShapeDtypeStruct((SEQ, D_FF), MM)),
        in_specs=[pl.BlockSpec((TM, D_MODEL), lambda n, i: (i, 0)),
                  pl.BlockSpec((1, D_MODEL, UP_SHARD), lambda n, i: (n, 0, 0)),
                  pl.BlockSpec((1, D_MODEL, UP_SHARD), lambda n, i: (n + 2, 0, 0))],
        out_specs=[tile, tile, tile],
        compiler_params=_params(("arbitrary", "arbitrary")),
    )(h2, w_up_g, w_up_g)


def ffn_down_loss(act, w_down_g, x2, target, g4):
    def body(act_ref, wd_ref, x2_ref, t_ref, g_ref, dff_ref, dy_ref, loss_ref, dg_ref):
        ff = _dot(act_ref[...], wd_ref[...])
        n4, r4 = _rms(ff)
        g4v = g_ref[...]
        err = x2_ref[...] + n4 * g4v - t_ref[...]
        row_loss = jnp.mean(err * err, axis=-1, keepdims=True)
        loss_ref[...] = jnp.zeros((8, 128), F32) + 0.5 * jnp.sum(row_loss, axis=0, keepdims=True)
        dy = err * (1.0 / D_MODEL)
        dy_ref[...] = dy
        dff_ref[...] = _rms_bwd(dy * g4v, n4, r4).astype(MM)
        _acc_rows(dg_ref, jnp.sum(dy * n4, axis=0, keepdims=True), pl.program_id(0) == 0)

    nt = SEQ // TM
    vec = _full_spec((1, D_MODEL))
    return pl.pallas_call(
        body, name="ffn_down_loss", grid=(nt,),
        out_shape=(jax.ShapeDtypeStruct((SEQ, D_MODEL), MM), jax.ShapeDtypeStruct((SEQ, D_MODEL), F32),
                   jax.ShapeDtypeStruct((nt * 8, 128), F32), jax.ShapeDtypeStruct((1, D_MODEL), F32)),
        in_specs=[_row_tile_spec(D_FF), _weight_spec(w_down_g.shape), _row_tile_spec(D_MODEL),
                  _row_tile_spec(D_MODEL), vec],
        out_specs=[_row_tile_spec(D_MODEL), _row_tile_spec(D_MODEL),
                   pl.BlockSpec((8, 128), lambda i: (i, 0)), vec],
        compiler_params=_params(("arbitrary",)),
    )(act, w_down_g, x2, target, g4)


def ffn_act_bwd(dff, w_down_g, gate, up):
    def body(dff_ref, wd_ref, gate_ref, up_ref, dgu_ref):
        dact = _dot_nt(dff_ref[...], wd_ref[...])
        gate = gate_ref[...].astype(F32)
        sg = _sigmoid(gate)
        dgu_ref[:, 0:D_FF] = (dact * up_ref[...].astype(F32) * (sg * (1.0 + gate * (1.0 - sg)))).astype(MM)
        dgu_ref[:, D_FF:2 * D_FF] = (dact * (gate * sg)).astype(MM)

    return pl.pallas_call(
        body, name="ffn_act_bwd", grid=(SEQ // TM,),
        out_shape=jax.ShapeDtypeStruct((SEQ, 2 * D_FF), MM),
        in_specs=[_row_tile_spec(D_MODEL), _weight_spec(w_down_g.shape), _row_tile_spec(D_FF), _row_tile_spec(D_FF)],
        out_specs=_row_tile_spec(2 * D_FF),
        compiler_params=_params(("arbitrary",)),
    )(dff, w_down_g, gate, up)


def ffn_in_bwd(dgu, w_up_g, x2, mix, dy, g3, g2):
    def body(dgu_ref, w_ref, x2_ref, mix_ref, dy_ref, g3_ref, g2_ref, dx2_ref, dmix_ref, dg3_ref, dg2_ref):
        dh2 = None
        for j in range(N_CHIPS):
            t = _dot_nt(dgu_ref[:, j * UP_SHARD:(j + 1) * UP_SHARD], w_ref[j])
            dh2 = t if dh2 is None else dh2 + t
        first = pl.program_id(0) == 0
        n3, r3 = _rms(x2_ref[...])
        dx2 = dy_ref[...] + _rms_bwd(dh2 * g3_ref[...], n3, r3)
        dx2_ref[...] = dx2
        _acc_rows(dg3_ref, jnp.sum(dh2 * n3, axis=0, keepdims=True), first)
        n2, r2 = _rms(mix_ref[...])
        dmix_ref[...] = _rms_bwd(dx2 * g2_ref[...], n2, r2).astype(MM)
        _acc_rows(dg2_ref, jnp.sum(dx2 * n2, axis=0, keepdims=True), first)

    vec = _full_spec((1, D_MODEL))
    return pl.pallas_call(
        body, name="ffn_in_bwd", grid=(SEQ // TM,),
        out_shape=(jax.ShapeDtypeStruct((SEQ, D_MODEL), F32), jax.ShapeDtypeStruct((SEQ, D_MODEL), MM),
                   jax.ShapeDtypeStruct((1, D_MODEL), F32), jax.ShapeDtypeStruct((1, D_MODEL), F32)),
        in_specs=[_row_tile_spec(2 * D_FF), _weight_spec(w_up_g.shape), _row_tile_spec(D_MODEL),
                  _row_tile_spec(D_MODEL), _row_tile_spec(D_MODEL), vec, vec],
        out_specs=[_row_tile_spec(D_MODEL), _row_tile_spec(D_MODEL), vec, vec],
        compiler_params=_params(("arbitrary",)),
    )(dgu, w_up_g, x2, mix, dy, g3, g2)


def merge_bwd(dmix, w_out_g, gc, ga, co, ao, w_cb_g, w_ab_g):
    def body(dmix_ref, wout_ref, gc_ref, ga_ref, co_ref, ao_ref, wcb_ref, wab_ref,
             dco_ref, dao_ref, dg_ref, du3_ref, datt_ref, dbcb_ref):
        dm = _dot_nt(dmix_ref[...], wout_ref[...])
        sgc = _sigmoid(gc_ref[...])
        sga = _sigmoid(ga_ref[...])
        dco = dm * sgc
        dao = dm * sga
        dg_ref[:, 0:D_MODEL] = (dm * co_ref[...].astype(F32) * (sgc * (1.0 - sgc))).astype(MM)
        dg_ref[:, D_MODEL:2 * D_MODEL] = (dm * ao_ref[...].astype(F32) * (sga * (1.0 - sga))).astype(MM)
        _acc_rows(dbcb_ref, jnp.sum(dco, axis=0, keepdims=True), pl.program_id(0) == 0)
        dco_ref[...] = dco.astype(MM)
        dao_ref[...] = dao.astype(MM)
        du3 = None
        datt = None
        for j in range(N_CHIPS):
            cols = slice(j * BR_SHARD, (j + 1) * BR_SHARD)
            t = _dot_nt(dco_ref[:, cols], wcb_ref[j])
            s = _dot_nt(dao_ref[:, cols], wab_ref[j])
            du3 = t if du3 is None else du3 + t
            datt = s if datt is None else datt + s
        du3_ref[...] = du3
        datt_ref[...] = datt.astype(MM)

    wide = _row_tile_spec(D_MODEL)
    return pl.pallas_call(
        body, name="merge_bwd", grid=(SEQ // TM,),
        out_shape=(jax.ShapeDtypeStruct((SEQ, D_MODEL), MM), jax.ShapeDtypeStruct((SEQ, D_MODEL), MM),
                   jax.ShapeDtypeStruct((SEQ, 2 * D_MODEL), MM),
                   jax.ShapeDtypeStruct((SEQ, CONV_DIM), F32), jax.ShapeDtypeStruct((SEQ, ATT_DIM), MM),
                   jax.ShapeDtypeStruct((1, D_MODEL), F32)),
        in_specs=[wide, _weight_spec(w_out_g.shape), wide, wide, wide, wide,
                  _weight_spec(w_cb_g.shape), _weight_spec(w_ab_g.shape)],
        out_specs=[wide, wide, _row_tile_spec(2 * D_MODEL), _row_tile_spec(CONV_DIM), _row_tile_spec(ATT_DIM),
                   _full_spec((1, D_MODEL))],
        compiler_params=_params(("arbitrary",)),
    )(dmix, w_out_g, gc, ga, co, ao, w_cb_g, w_ab_g)


def conv_bwd(du3, u1, ci, w_dw, ln_g, ln_b):
    def body(du3_ref, u1_ref, ci_ref, w_ref, g_ref, bb_ref,
             dci_ref, dw_ref, dbdw_ref, dg_ref, db_ref, upad_ref, dpad_ref, dwacc_ref, vacc_ref):
        _glu_into(ci_ref, upad_ref)
        dpad_ref[SEQ:SEQ + 32, :] = jnp.zeros((32, CONV_DIM), F32)
        dwacc_ref[...] = jnp.zeros_like(dwacc_ref)
        vacc_ref[...] = jnp.zeros_like(vacc_ref)

        def fold8(t):
            s = t[0:8, :]
            for r in range(1, CONV_TILE // 8):
                s = s + t[8 * r:8 * r + 8, :]
            return s

        def pass1(i, c):
            t0 = pl.multiple_of(i * CONV_TILE, CONV_TILE)
            xh, rstd = _layernorm_parts(u1_ref[pl.ds(t0, CONV_TILE), :])
            gv = g_ref[...]
            u2 = xh * gv + bb_ref[...]
            s2 = _sigmoid(u2)
            du2 = du3_ref[pl.ds(t0, CONV_TILE), :] * (s2 * (1.0 + u2 * (1.0 - s2)))
            wv = du2 * gv
            du1 = rstd * (wv - jnp.mean(wv, axis=-1, keepdims=True)
                          - xh * jnp.mean(wv * xh, axis=-1, keepdims=True))
            dpad_ref[pl.ds(t0, CONV_TILE), :] = du1
            vacc_ref[0] += fold8(du2 * xh)
            vacc_ref[1] += fold8(du2)
            vacc_ref[2] += fold8(du1)
            win = upad_ref[pl.ds(t0, CONV_WIN), :]
            n = win.shape[0]
            for rot in range(8):
                shifted = win if rot == 0 else pltpu.roll(win, n - rot, 0)
                for a in range(5):
                    j = 8 * a + rot - 2
                    if 0 <= j < CONV_WIDTH:
                        dwacc_ref[j] += fold8(du1 * shifted[8 * a:8 * a + CONV_TILE, :])
            return c

        lax.fori_loop(0, SEQ // CONV_TILE, pass1, 0)

        def pass2(i, c):
            t0 = pl.multiple_of(i * CONV_TILE, CONV_TILE)
            win = dpad_ref[pl.ds(t0, CONV_WIN), :]
            du0 = _shifted_sum(win, [(30 - j, w_ref[j:j + 1, :]) for j in range(CONV_WIDTH)])
            a = ci_ref[pl.ds(t0, CONV_TILE), 0:CONV_DIM]
            sb = _sigmoid(ci_ref[pl.ds(t0, CONV_TILE), CONV_DIM:2 * CONV_DIM])
            dci_ref[pl.ds(t0, CONV_TILE), 0:CONV_DIM] = (du0 * sb).astype(MM)
            dci_ref[pl.ds(t0, CONV_TILE), CONV_DIM:2 * CONV_DIM] = (du0 * a * (sb * (1.0 - sb))).astype(MM)
            return c

        lax.fori_loop(0, SEQ // CONV_TILE, pass2, 0)

        for j in range(CONV_WIDTH):
            dw_ref[j:j + 1, :] = jnp.sum(dwacc_ref[j], axis=0, keepdims=True)
        dw_ref[CONV_WIDTH:32, :] = jnp.zeros((32 - CONV_WIDTH, CONV_DIM), F32)
        dg_ref[...] = jnp.sum(vacc_ref[0], axis=0, keepdims=True)
        db_ref[...] = jnp.sum(vacc_ref[1], axis=0, keepdims=True)
        dbdw_ref[...] = jnp.sum(vacc_ref[2], axis=0, keepdims=True)

    vec = jax.ShapeDtypeStruct((1, CONV_DIM), F32)
    return pl.pallas_call(
        body, name="conv_bwd",
        out_shape=(jax.ShapeDtypeStruct((SEQ, 2 * CONV_DIM), MM), jax.ShapeDtypeStruct((32, CONV_DIM), F32),
                   vec, vec, vec),
        in_specs=[VMEM_SPEC] * 6, out_specs=[VMEM_SPEC] * 5,
        scratch_shapes=[pltpu.VMEM((SEQ + 32, CONV_DIM), F32), pltpu.VMEM((SEQ + 32, CONV_DIM), F32),
                        pltpu.VMEM((CONV_WIDTH, 8, CONV_DIM), F32), pltpu.VMEM((3, 8, CONV_DIM), F32)],
        compiler_params=_params(),
    )(du3, u1, ci, w_dw, ln_g, ln_b)


def attn_bwd(q, k, v, datt, rc, scatter=()):
    ns = len(scatter)

    def body(*refs):
        q_ref, k_ref, v_ref, do_ref, rc_ref = refs[:5]
        dqkv_ref = refs[5 + ns]
        (dqa_ref, dka_ref, dva_ref, pc_ref, z_ref, sig1_ref, sig2_ref, g_ref, spb_ref, gb_ref, ab_ref,
         dzb_ref) = refs[6 + 2 * ns:18 + 2 * ns]
        if ns:
            sc = _Scatter(refs[5:5 + ns], refs[6 + ns:6 + 2 * ns], refs[18 + 2 * ns:])
            sc.start()
        lane, row, head0 = _head_masks()
        for ref in (dqa_ref, dka_ref, dva_ref, pc_ref):
            ref[...] = jnp.zeros_like(ref)
        z_ref[...] = jnp.full(z_ref.shape, NO_SCORE, F32)
        for ref in (sig1_ref, sig2_ref, spb_ref, ab_ref, g_ref, gb_ref, dzb_ref):
            ref[...] = jnp.zeros_like(ref)
        w_suffix = _cumsum_weights(suffix=True, with_total=False)
        w_prefix = _cumsum_weights(suffix=False, with_total=True)

        def step(pairs):
            (ia, ja), (ib, jb), (ic, jc), (id_, jd) = pairs
            qa, ka, qb_, kb_, qc, kc, qd, kd = (pl.multiple_of(jnp.maximum(b, 0) * TQ, TQ)
                                                for b in (ia, ja, ib, jb, ic, jc, id_, jd))
            bias_a = _score_bias(lane, row, ia, ja)
            rc_rows = rc_ref[pl.ds(qb_, TQ), :]
            first_c = jc == 0
            for p in range(N_PAIRS):
                cols = slice(128 * p, 128 * (p + 1))
                q_a = q_ref[pl.ds(qa, TQ), cols]
                k_a = k_ref[pl.ds(ka, TQ), cols]
                do_b = do_ref[pl.ds(qb_, TQ), cols]
                v_b = v_ref[pl.ds(kb_, TQ), cols]
                do_c = do_ref[pl.ds(qc, TQ), cols]
                q_d = q_ref[pl.ds(qd, TQ), cols]
                k_d = k_ref[pl.ds(kd, TQ), cols]
                for h in range(2):
                    hh = 2 * p + h
                    dzb = dzb_ref[hh]
                    dqa_ref[pl.ds(qd, TQ), cols] += _dot(dzb, _pick_head(k_d, head0, h))
                    dka_ref[pl.ds(kd, TQ), cols] += _dot_tn(dzb, _pick_head(q_d, head0, h))
                    r = _dot(gb_ref[hh], w_prefix)
                    p_in = jnp.where(first_c, 0.0, pc_ref[hh])
                    dzb_ref[hh] = (g_ref[hh] - sig2_ref[hh] * (r[:, :128] + p_in)).astype(MM)
                    pc_ref[hh] = p_in + r[:, 128:]
                    dva_ref[pl.ds(kc, TQ), cols] += _dot_tn(ab_ref[hh], _pick_head(do_c, head0, h))
                    r_in = jnp.sum(jnp.where(lane == 16 * hh + jb, rc_rows, 0.0), axis=1, keepdims=True)
                    a = jnp.exp(z_ref[hh] - (_dot(spb_ref[hh], w_suffix) + r_in))
                    g = _dot_nt(_pick_head(do_b, head0, h), v_b) * a
                    ab_ref[hh] = a.astype(MM)
                    g_ref[hh] = g
                    gb_ref[hh] = g.astype(MM)
                    sig2_ref[hh] = sig1_ref[hh]
                    z = _dot_nt(_pick_head(q_a, head0, h), k_a) + bias_a
                    sp = _softplus(z)
                    sig1_ref[hh] = jnp.exp(z - sp)
                    z_ref[hh] = z
                    spb_ref[hh] = sp.astype(MM)

        _block_pipeline(4, False, step)
        dqkv_ref[:, 0:ATT_DIM] = (dqa_ref[...] * ATT_SCALE).astype(MM)
        dqkv_ref[:, ATT_DIM:2 * ATT_DIM] = dka_ref[...].astype(MM)
        dqkv_ref[:, 2 * ATT_DIM:3 * ATT_DIM] = dva_ref[...].astype(MM)
        if ns:
            sc.finish()

    out = jax.ShapeDtypeStruct((SEQ, 3 * ATT_DIM), MM)
    return pl.pallas_call(
        body, name="attn_bwd", out_shape=[out] + _Scatter.out_shapes(scatter),
        in_specs=[VMEM_SPEC] * 5 + [ANY] * ns, out_specs=[VMEM_SPEC] + [ANY] * ns,
        scratch_shapes=[pltpu.VMEM((SEQ, ATT_DIM), F32)] * 3 + [pltpu.VMEM((8, TQ, 128), F32)] * 5
                       + [pltpu.VMEM((8, TQ, 128), MM)] * 4 + (_Scatter.scratch(ns) if ns else []),
        compiler_params=_params(),
    )(q, k, v, datt, rc, *scatter)


DPROJ_PIECES = ((0, 1024), (1024, 2560), (2560, 4608))


def _dproj_segments(j):
    g0, g1 = j * IN_SHARD, (j + 1) * IN_SHARD
    segs = []
    for p, (s, e) in enumerate(DPROJ_PIECES):
        lo, hi = max(s, g0), min(e, g1)
        if lo < hi:
            segs.append((p, lo - s, lo - g0, hi - lo))
    return segs


def in_proj_bwd(pieces, w_in_g, x, dx2, g1, scatter=()):
    ns = len(scatter)
    nt = SEQ // TM

    def body(*refs):
        p_refs = refs[:3]
        w_ref, x_ref, dx2_ref, g_ref = refs[3:7]
        dx_ref, dg_ref = refs[7 + ns:9 + ns]
        if ns:
            sc = _Scatter(refs[7:7 + ns], refs[9 + ns:9 + 2 * ns], refs[9 + 2 * ns:])
            pl.when(pl.program_id(0) == 0)(sc.start)
        dh = None
        for j in range(N_CHIPS):
            for p, lo, off, width in _dproj_segments(j):
                t = _dot_nt(p_refs[p][:, lo:lo + width], w_ref[j, :, off:off + width])
                dh = t if dh is None else dh + t
        n1, r1 = _rms(x_ref[...])
        dx_ref[...] = dx2_ref[...] + _rms_bwd(dh * g_ref[...], n1, r1)
        _acc_rows(dg_ref, jnp.sum(dh * n1, axis=0, keepdims=True), pl.program_id(0) == 0)
        if ns:
            pl.when(pl.program_id(0) == nt - 1)(sc.finish)

    vec = _full_spec((1, D_MODEL))
    return pl.pallas_call(
        body, name="in_proj_bwd", grid=(nt,),
        out_shape=[jax.ShapeDtypeStruct((SEQ, D_MODEL), F32), jax.ShapeDtypeStruct((1, D_MODEL), F32)]
                  + _Scatter.out_shapes(scatter),
        in_specs=[_row_tile_spec(p.shape[1]) for p in pieces]
                 + [_weight_spec(w_in_g.shape), _row_tile_spec(D_MODEL), _row_tile_spec(D_MODEL), vec] + [ANY] * ns,
        out_specs=[_row_tile_spec(D_MODEL), vec] + [ANY] * ns,
        scratch_shapes=_Scatter.scratch(ns) if ns else [],
        compiler_params=_params(("arbitrary",)),
    )(*pieces, w_in_g, x, dx2, g1, *scatter)


def weight_grad_in(h1, pieces):
    kh = D_MODEL // 2

    def body(a_ref, p0_ref, p1_ref, p2_ref, o_ref):
        p_refs = (p0_ref, p1_ref, p2_ref)
        a = a_ref[...]
        for j in range(N_CHIPS):
            @pl.when(pl.program_id(1) == j)
            def _():
                for p, lo, off, width in _dproj_segments(j):
                    o_ref[0, 0, :, off:off + width] = _dot_tn(a, p_refs[p][:, lo:lo + width]).astype(MM)

    return pl.pallas_call(
        body, name="dw_in", grid=(2, N_CHIPS), out_shape=jax.ShapeDtypeStruct((N_CHIPS, 2, kh, IN_SHARD), MM),
        in_specs=[pl.BlockSpec((SEQ, kh), lambda h, j: (0, h))] + [_weight_spec(p.shape) for p in pieces],
        out_specs=pl.BlockSpec((1, 1, kh, IN_SHARD), lambda h, j: (j, h, 0, 0)),
        compiler_params=_params(("arbitrary", "arbitrary")),
    )(h1, *pieces)


def weight_grad(a, b, name, col_sharded, tk=None):
    kin, n = a.shape[1], b.shape[1]

    def body(a_ref, b_ref, o_ref):
        if col_sharded:
            o_ref[0, 0] = _dot_tn(a_ref[...], b_ref[...]).astype(MM)
        else:
            o_ref[...] = _dot_tn(a_ref[...], b_ref[...]).astype(MM)

    if col_sharded:
        kh, ns = kin // 2, n // N_CHIPS
        out = jax.ShapeDtypeStruct((N_CHIPS, 2, kh, ns), MM)
        grid = (2, N_CHIPS)
        in_specs = [pl.BlockSpec((SEQ, kh), lambda h, j: (0, h)), pl.BlockSpec((SEQ, ns), lambda h, j: (0, j))]
        out_spec = pl.BlockSpec((1, 1, kh, ns), lambda h, j: (j, h, 0, 0))
        sem = ("arbitrary", "arbitrary")
    else:
        out = jax.ShapeDtypeStruct((kin, n), MM)
        grid = (kin // tk,)
        in_specs = [pl.BlockSpec((SEQ, tk), lambda r: (0, r)), pl.BlockSpec((SEQ, n), lambda r: (0, 0))]
        out_spec = pl.BlockSpec((tk, n), lambda r: (r, 0))
        sem = ("arbitrary",)
    res = pl.pallas_call(
        body, name=name, grid=grid, out_shape=out, in_specs=in_specs, out_specs=out_spec,
        compiler_params=_params(sem),
    )(a, b)
    if not col_sharded:
        res = res.reshape(N_CHIPS, 2, kin // (2 * N_CHIPS), n)
    return res


def _place():
    x, y, c = lax.axis_index("x"), lax.axis_index("y"), lax.axis_index("c")
    chips = [(1 - x, y), (x, 1 - y), (1 - x, 1 - y)]
    return x, y, c, chips


def _rcopy(src, dst, send_sem, recv_sem, dev):
    return pltpu.make_async_remote_copy(src_ref=src, dst_ref=dst, send_sem=send_sem, recv_sem=recv_sem,
                                        device_id=dev, device_id_type=MESH)


class _Gather:
    def __init__(self, shapes, w, o, scratch):
        self.n, self.shapes, self.w, self.o = len(w), shapes, w, o
        self.send, self.recv, self.fsend, self.frecv, self.loc_in, self.loc_out = scratch[:6]
        self.raw, self.stage = scratch[6:6 + self.n], scratch[6 + self.n:]
        self.x, self.y, self.c, self.chips = _place()
        self.me = 2 * self.x + self.y
        self.sib = (self.x, self.y, 1 - self.c)
        self.pairs = [(j, t) for j in range(3) for t in range(self.n)]

    @staticmethod
    def scratch(shards):
        n = len(shards)
        sems = pltpu.SemaphoreType.DMA
        return ([sems((3 * n,)), sems((3 * n,)), sems((3 * n,)), sems((3 * n,)), sems((n,)), sems((n,))]
                + [pltpu.VMEM(s.shape, s.dtype) for s in shards] + [pltpu.VMEM(s.shape, MM) for s in shards])

    @staticmethod
    def out_shapes(shards):
        return [jax.ShapeDtypeStruct((N_CHIPS,) + s.shape, MM) for s in shards]

    def _half(self, t, k, cc):
        rh = self.shapes[t][0] // 2
        return self.o[t].at[k, pl.ds(cc * rh, rh), :]

    def _chip(self, j):
        cx, cy = self.chips[j]
        return 2 * cx + cy, (cx, cy, self.c)

    def local_in(self, t):
        return pltpu.make_async_copy(self.w[t], self.raw[t], self.loc_in.at[t])

    def local_out(self, t):
        return pltpu.make_async_copy(self.stage[t], self.o[t].at[self.me], self.loc_out.at[t])

    def first(self, j, t):
        rh = self.shapes[t][0] // 2
        i = j * self.n + t
        return _rcopy(self.stage[t].at[pl.ds(self.c * rh, rh), :], self._half(t, self.me, self.c),
                      self.send.at[i], self.recv.at[i], self._chip(j)[1])

    def arrived(self, j, t):
        k, dev = self._chip(j)
        i = j * self.n + t
        blk = self._half(t, k, self.c)
        return _rcopy(blk, blk, self.send.at[i], self.recv.at[i], dev)

    def passed(self, j, t, cc):
        i = j * self.n + t
        blk = self._half(t, self._chip(j)[0], cc)
        return _rcopy(blk, blk, self.fsend.at[i], self.frecv.at[i], self.sib)

    def start(self):
        for t in range(self.n):
            self.local_in(t).start()
        for t in range(self.n):
            self.local_in(t).wait()
            self.stage[t][...] = self.raw[t][...].astype(MM)
            self.local_out(t).start()
        for j, t in self.pairs:
            self.first(j, t).start()

    def forward(self):
        for j, t in self.pairs:
            self.arrived(j, t).wait_recv()
            self.passed(j, t, self.c).start()

    def finish(self):
        for j, t in self.pairs:
            self.passed(j, t, 1 - self.c).wait_recv()
        for j, t in self.pairs:
            self.first(j, t).wait_send()
            self.passed(j, t, self.c).wait_send()
        for t in range(self.n):
            self.local_out(t).wait()


def all_gather_weights(shards, small):
    n = len(shards)
    shapes = [s.shape for s in shards]

    def body(*refs):
        w = refs[:n]
        sm = refs[n]
        o = refs[n + 1:2 * n + 1]
        osm = refs[2 * n + 1]
        ssend, srecv, sloc = refs[2 * n + 2:2 * n + 5]
        g = _Gather(shapes, w, o, refs[2 * n + 5:])
        own = pltpu.make_async_copy(sm, osm.at[g.me], sloc)
        own.start()
        g.start()
        small_cps = [_rcopy(sm, osm.at[g.me], ssend.at[j], srecv.at[j], g._chip(j)[1]) for j in range(3)]
        for cp in small_cps:
            cp.start()
        g.forward()
        g.finish()
        for j in range(3):
            k, dev = g._chip(j)
            _rcopy(sm, osm.at[k], ssend.at[j], srecv.at[j], dev).wait_recv()
            small_cps[j].wait_send()
        own.wait()

    out_shape = _Gather.out_shapes(shards)
    out_shape.append(jax.ShapeDtypeStruct((N_CHIPS,) + small.shape, small.dtype))
    sems = pltpu.SemaphoreType.DMA
    return pl.pallas_call(
        body, name="all_gather_weights", out_shape=out_shape,
        in_specs=[ANY] * (n + 1), out_specs=[ANY] * (n + 1),
        scratch_shapes=[sems((3,)), sems((3,)), sems] + _Gather.scratch(shards),
        compiler_params=_params(),
    )(*shards, small)


def sibling_exchange(grads, name):
    n = len(grads)

    def body(*refs):
        g = refs[:n]
        o = refs[n:2 * n]
        send, recv = refs[2 * n:]
        x, y, c, _ = _place()
        cps = [_rcopy(g[t].at[:, 1 - c], o[t], send.at[t], recv.at[t], (x, y, 1 - c)) for t in range(n)]
        for cp in cps:
            cp.start()
        for cp in cps:
            cp.wait()

    sems = pltpu.SemaphoreType.DMA
    return pl.pallas_call(
        body, name=name,
        out_shape=[jax.ShapeDtypeStruct((a.shape[0],) + a.shape[2:], a.dtype) for a in grads],
        in_specs=[ANY] * n, out_specs=[ANY] * n, scratch_shapes=[sems((n,)), sems((n,))],
    )(*grads)


class _Scatter:
    def __init__(self, p, o, sems):
        self.n, self.p, self.o = len(p), p, o
        self.send, self.recv = sems
        _, _, self.c, self.chips = _place()

    @staticmethod
    def scratch(n):
        sems = pltpu.SemaphoreType.DMA
        return [sems((3 * n,)), sems((3 * n,))]

    @staticmethod
    def out_shapes(parts):
        return [jax.ShapeDtypeStruct((3,) + a.shape[1:], a.dtype) for a in parts]

    def copies(self):
        cps = []
        for j, (cx, cy) in enumerate(self.chips):
            for t in range(self.n):
                i = j * self.n + t
                cps.append(_rcopy(self.p[t].at[2 * cx + cy], self.o[t].at[j], self.send.at[i], self.recv.at[i],
                                  (cx, cy, self.c)))
        return cps

    def start(self):
        for cp in self.copies():
            cp.start()

    def finish(self):
        for cp in self.copies():
            cp.wait()


def sibling_swap(halves, name):
    n = len(halves)

    def body(*refs):
        h = refs[:n]
        o = refs[n:2 * n]
        send, recv = refs[2 * n:]
        x, y, c, _ = _place()
        cps = [_rcopy(h[t], o[t], send.at[t], recv.at[t], (x, y, 1 - c)) for t in range(n)]
        for cp in cps:
            cp.start()
        for cp in cps:
            cp.wait()

    sems = pltpu.SemaphoreType.DMA
    return pl.pallas_call(
        body, name=name, out_shape=[jax.ShapeDtypeStruct(a.shape, a.dtype) for a in halves],
        in_specs=[ANY] * n, out_specs=[ANY] * n, scratch_shapes=[sems((n,)), sems((n,))],
    )(*halves)


def small_all_reduce(ddw, v512, v1024, loss_parts):
    rows, width = PACK_ROWS, 512
    n512, n1024 = len(VEC512), len(VEC1024)

    def body(*refs):
        ddw_ref = refs[0]
        a_refs = refs[1:1 + n512]
        b_refs = refs[1 + n512:1 + n512 + n1024]
        lp_ref, o_ref, p_ref, gath_ref, send, recv = refs[1 + n512 + n1024:]
        p_ref[...] = jnp.zeros_like(p_ref)
        p_ref[0:32, :] = ddw_ref[...]
        p_ref[LOSS_ROW:LOSS_ROW + 1, 0:128] = jnp.sum(lp_ref[...], axis=0, keepdims=True) * 0.125
        for i, r in enumerate(a_refs):
            p_ref[32 + i:33 + i, :] = r[...]
        for i, r in enumerate(b_refs):
            base = 32 + n512 + 2 * i
            p_ref[base:base + 1, :] = r[:, 0:512]
            p_ref[base + 1:base + 2, :] = r[:, 512:1024]
        x, y, c, _ = _place()
        me = 4 * x + 2 * y + c
        gath_ref[me] = p_ref[...]
        cps = []
        for k in range(1, 8):
            dx, dy, dc = (k >> 2) & 1, (k >> 1) & 1, k & 1
            px = 1 - x if dx else x
            py = 1 - y if dy else y
            pc = 1 - c if dc else c
            cps.append(_rcopy(p_ref, gath_ref.at[me], send.at[k - 1], recv.at[k - 1], (px, py, pc)))
        for cp in cps:
            cp.start()
        for k in range(1, 8):
            dx, dy, dc = (k >> 2) & 1, (k >> 1) & 1, k & 1
            px = 1 - x if dx else x
            py = 1 - y if dy else y
            pc = 1 - c if dc else c
            _rcopy(p_ref, gath_ref.at[4 * px + 2 * py + pc], send.at[k - 1], recv.at[k - 1], (px, py, pc)).wait_recv()
        for cp in cps:
            cp.wait_send()
        total = gath_ref[0]
        for d in range(1, 8):
            total = total + gath_ref[d]
        o_ref[...] = total

    sems = pltpu.SemaphoreType.DMA
    n_in = 2 + n512 + n1024
    return pl.pallas_call(
        body, name="small_all_reduce", out_shape=jax.ShapeDtypeStruct((rows, width), F32),
        in_specs=[VMEM_SPEC] * n_in, out_specs=VMEM_SPEC,
        scratch_shapes=[pltpu.VMEM((rows, width), F32), pltpu.VMEM((8, rows, width), F32), sems((7,)), sems((7,))],
    )(ddw, *[v512[n] for n in VEC512], *[v1024[n] for n in VEC1024], loss_parts)


def _row_block(r):
    for tr in (512, 352, 256, 128):
        if r % tr == 0:
            return tr
    return r


def add_halves(g, recv, name):
    _, _, r, w = g.shape
    tr = _row_block(r)

    def body(g0_ref, g1_ref, r_ref, ob_ref, own_ref):
        k = pl.program_id(1)
        c = lax.axis_index("c")
        me = 2 * lax.axis_index("x") + lax.axis_index("y")
        t = jnp.where(c == 0, g0_ref[0, 0], g1_ref[0, 0]).astype(F32) + r_ref[0].astype(F32)
        ob_ref[0] = t.astype(MM)
        mine = jnp.where(k == me, t, 0.0)

        @pl.when(k == 0)
        def _():
            own_ref[...] = mine

        @pl.when(k != 0)
        def _():
            own_ref[...] += mine

    return pl.pallas_call(
        body, name=name, grid=(r // tr, N_CHIPS),
        in_specs=[pl.BlockSpec((1, 1, tr, w), lambda i, k: (k, 0, i, 0)),
                  pl.BlockSpec((1, 1, tr, w), lambda i, k: (k, 1, i, 0)),
                  pl.BlockSpec((1, tr, w), lambda i, k: (k, i, 0))],
        out_specs=[pl.BlockSpec((1, tr, w), lambda i, k: (k, i, 0)),
                   pl.BlockSpec((tr, w), lambda i, k: (i, 0))],
        out_shape=(jax.ShapeDtypeStruct((N_CHIPS, r, w), MM), jax.ShapeDtypeStruct((r, w), F32)),
        compiler_params=_params(("arbitrary", "arbitrary")),
    )(g, g, recv)


def sum_parts(own, rin, name):
    _, r, w = rin.shape
    tr = _row_block(r)

    def body(o_ref, r_ref, out_ref):
        out_ref[...] = ((o_ref[...] + r_ref[0].astype(F32)) + r_ref[1].astype(F32)) + r_ref[2].astype(F32)

    return pl.pallas_call(
        body, name=name, grid=(r // tr,), out_shape=jax.ShapeDtypeStruct((r, w), F32),
        in_specs=[pl.BlockSpec((tr, w), lambda i: (i, 0)), pl.BlockSpec((3, tr, w), lambda i: (0, i, 0))],
        out_specs=pl.BlockSpec((tr, w), lambda i: (i, 0)),
        compiler_params=_params(("arbitrary",)),
    )(own, rin)


def _adamw_math(w, g, m, v):
    mn = ADAM_B1 * m + (1.0 - ADAM_B1) * g
    vn = ADAM_B2 * v + (1.0 - ADAM_B2) * (g * g)
    m_hat = mn / (1.0 - ADAM_B1 ** ADAM_STEP)
    v_hat = vn / (1.0 - ADAM_B2 ** ADAM_STEP)
    return -ADAM_LR * (m_hat / (jnp.sqrt(v_hat) + ADAM_EPS) + ADAM_WD * w), mn, vn


def adamw(w, mine, other, m, v, name):
    r, c = w.shape
    rh = r // 2
    tr = _row_block(rh)
    if c >= 1024 and tr % 512 == 0:
        tr = 256
    nb = rh // tr

    def body(w_ref, a_ref, b_ref, m_ref, v_ref, go_ref, d_ref, mo_ref, vo_ref):
        gv = jnp.where(lax.axis_index("c") == pl.program_id(0), a_ref[...], b_ref[...])
        go_ref[...] = gv
        d_ref[...], mo_ref[...], vo_ref[...] = _adamw_math(w_ref[...], gv, m_ref[...], v_ref[...])

    spec = pl.BlockSpec((tr, c), lambda h, i: (h * nb + i, 0))
    half = pl.BlockSpec((tr, c), lambda h, i: (i, 0))
    out = jax.ShapeDtypeStruct((r, c), F32)
    return pl.pallas_call(
        body, name=name, grid=(2, nb), out_shape=(out, out, out, out),
        in_specs=[spec, half, half, spec, spec], out_specs=[spec] * 4,
        compiler_params=_params(("arbitrary", "arbitrary")),
    )(w, mine, other, m, v)


def adamw_small(gsum, params):
    names = list(params)
    flat = [a for n in names for a in params[n]]

    def body(*refs):
        g_ref = refs[0]
        ins = refs[1:1 + 3 * len(names)]
        outs = refs[1 + 3 * len(names):]
        me = 2 * lax.axis_index("x") + lax.axis_index("y")
        for i, n in enumerate(names):
            w_ref, m_ref, v_ref = ins[3 * i:3 * i + 3]
            go_ref, d_ref, mo_ref, vo_ref = outs[4 * i:4 * i + 4]
            if n == "conv_dw_w":
                gv = jnp.zeros((CONV_WIDTH, 128), F32)
                for k in range(N_CHIPS):
                    gv = gv + jnp.where(me == k, g_ref[0:CONV_WIDTH, 128 * k:128 * (k + 1)], 0.0)
            elif n in VEC512:
                r0 = 32 + VEC512.index(n)
                gv = g_ref[r0:r0 + 1, :]
            else:
                r0 = 32 + len(VEC512) + 2 * VEC1024.index(n)
                gv = jnp.concatenate([g_ref[r0:r0 + 1, :], g_ref[r0 + 1:r0 + 2, :]], axis=1)
            go_ref[...] = gv
            d_ref[...], mo_ref[...], vo_ref[...] = _adamw_math(w_ref[...], gv, m_ref[...], v_ref[...])

    out_shape = [jax.ShapeDtypeStruct(params[n][0].shape, F32) for n in names for _ in range(4)]
    res = pl.pallas_call(
        body, name="adamw_small", out_shape=out_shape,
        in_specs=[VMEM_SPEC] * (1 + len(flat)), out_specs=[VMEM_SPEC] * len(out_shape),
        compiler_params=_params(),
    )(gsum, *flat)
    return {n: res[4 * i:4 * i + 4] for i, n in enumerate(names)}


REST = ("w_ffn_up", "w_ffn_down", "w_out", "w_conv_branch", "w_att_branch")
VEC512 = ("conv_dw_b", "conv_ln_g", "conv_ln_b")
VEC1024 = ("norm_mix_pre", "b_conv_branch", "norm_mix_post", "norm_ffn_pre", "norm_ffn_post")
PACK_ROWS = 48
LOSS_ROW = 47


def kernel(x, norm_mix_pre, w_in, conv_dw_w, conv_dw_b, conv_ln_g, conv_ln_b, w_conv_branch, b_conv_branch, w_att_branch, w_out, norm_mix_post, norm_ffn_pre, w_ffn_up, w_ffn_down, norm_ffn_post, loss_target, m_norm_mix_pre, m_w_in, m_conv_dw_w, m_conv_dw_b, m_conv_ln_g, m_conv_ln_b, m_w_conv_branch, m_b_conv_branch, m_w_att_branch, m_w_out, m_norm_mix_post, m_norm_ffn_pre, m_w_ffn_up, m_w_ffn_down, m_norm_ffn_post, v_norm_mix_pre, v_w_in, v_conv_dw_w, v_conv_dw_b, v_conv_ln_g, v_conv_ln_b, v_w_conv_branch, v_b_conv_branch, v_w_att_branch, v_w_out, v_norm_mix_post, v_norm_ffn_pre, v_w_ffn_up, v_w_ffn_down, v_norm_ffn_post):
    weights = dict(norm_mix_pre=norm_mix_pre, w_in=w_in, conv_dw_w=conv_dw_w, conv_dw_b=conv_dw_b, conv_ln_g=conv_ln_g, conv_ln_b=conv_ln_b, w_conv_branch=w_conv_branch, b_conv_branch=b_conv_branch, w_att_branch=w_att_branch, w_out=w_out, norm_mix_post=norm_mix_post, norm_ffn_pre=norm_ffn_pre, w_ffn_up=w_ffn_up, w_ffn_down=w_ffn_down, norm_ffn_post=norm_ffn_post)
    mom = dict(norm_mix_pre=m_norm_mix_pre, w_in=m_w_in, conv_dw_w=m_conv_dw_w, conv_dw_b=m_conv_dw_b, conv_ln_g=m_conv_ln_g, conv_ln_b=m_conv_ln_b, w_conv_branch=m_w_conv_branch, b_conv_branch=m_b_conv_branch, w_att_branch=m_w_att_branch, w_out=m_w_out, norm_mix_post=m_norm_mix_post, norm_ffn_pre=m_norm_ffn_pre, w_ffn_up=m_w_ffn_up, w_ffn_down=m_w_ffn_down, norm_ffn_post=m_norm_ffn_post)
    var = dict(norm_mix_pre=v_norm_mix_pre, w_in=v_w_in, conv_dw_w=v_conv_dw_w, conv_dw_b=v_conv_dw_b, conv_ln_g=v_conv_ln_g, conv_ln_b=v_conv_ln_b, w_conv_branch=v_w_conv_branch, b_conv_branch=v_b_conv_branch, w_att_branch=v_w_att_branch, w_out=v_w_out, norm_mix_post=v_norm_mix_post, norm_ffn_pre=v_norm_ffn_pre, w_ffn_up=v_w_ffn_up, w_ffn_down=v_w_ffn_down, norm_ffn_post=v_norm_ffn_post)
    order = list(weights)
    grads, deltas, new_m, new_v = {}, {}, {}, {}
    xs = x.reshape(SEQ, D_MODEL)
    tgt = loss_target.reshape(SEQ, D_MODEL)
    row = lambda a: a.reshape(1, -1)
    g1, g2, g3, g4 = (row(weights[n]) for n in ("norm_mix_pre", "norm_mix_post", "norm_ffn_pre", "norm_ffn_post"))
    ln_g, ln_b = row(conv_ln_g), row(conv_ln_b)

    def reduce_prepare(names, partial, tag):
        from_sib = sibling_exchange([partial[n] for n in names], "sibling_exchange_" + tag)
        return [add_halves(partial[n], r, "add_" + n) for n, r in zip(names, from_sib)]

    def reduce_finish(names, summed, from_chips, tag):
        halves = [sum_parts(s[1], r, "sum_" + n) for n, s, r in zip(names, summed, from_chips)]
        for n, a, b in zip(names, halves, sibling_swap(halves, "sibling_swap_" + tag)):
            grads[n], deltas[n], new_m[n], new_v[n] = adamw(weights[n], a, b, mom[n], var[n], "adamw_" + n)

    w_in_g, dw_g = all_gather_weights([w_in], conv_dw_w)
    w_dw_full = jnp.concatenate([dw_g[k] for k in range(N_CHIPS)], axis=1)
    h1, ci, q, k, v, gc, ga, w_out_g, w_cb_g, w_ab_g = in_proj_fwd(
        xs, g1, w_in_g, [w_out, w_conv_branch, w_att_branch])
    u1, u3 = conv_fwd(ci, w_dw_full, row(conv_dw_b), ln_g, ln_b)
    att, rc, w_up_g = attn_fwd(q, k, v, [w_ffn_up])
    w_out_g = w_out_g.reshape(D_MODEL, D_MODEL)
    co, ao, merged, mix, x2, h2, w_down_g = mix_fwd(u3, att, gc, ga, xs, w_cb_g, row(b_conv_branch), w_ab_g, w_out_g,
                                                    g2, g3, [w_ffn_down])
    w_down_g = w_down_g.reshape(D_FF, D_MODEL)
    gate, up, act = ffn_up_fwd(h2, w_up_g)
    dff, dy, loss_parts, dg4 = ffn_down_loss(act, w_down_g, x2, tgt, g4)

    partial = {}
    dgu = ffn_act_bwd(dff, w_down_g, gate, up)
    partial["w_ffn_down"] = weight_grad(act, dff, "dw_ffn_down", False, tk=UP_SHARD)
    dx2, dmix, dg3, dg2 = ffn_in_bwd(dgu, w_up_g, x2, mix, dy, g3, g2)
    partial["w_ffn_up"] = weight_grad(h2, dgu, "dw_ffn_up", True)
    dco, dao, dg, du3, datt, dbcb = merge_bwd(dmix, w_out_g, gc, ga, co, ao, w_cb_g, w_ab_g)
    partial["w_out"] = weight_grad(merged, dmix, "dw_out", False, tk=512)
    partial["w_conv_branch"] = weight_grad(u3, dco, "dw_conv_branch", True)
    partial["w_att_branch"] = weight_grad(att, dao, "dw_att_branch", True)
    summed = reduce_prepare(REST, partial, "rest")
    dci, ddw, dbdw, dlng, dlnb = conv_bwd(du3, u1, ci, w_dw_full, ln_g, ln_b)
    dqkv, *from_chips = attn_bwd(q, k, v, datt, rc, [s[0] for s in summed])
    reduce_finish(REST, summed, from_chips, "rest")
    dproj = (dci, dqkv, dg)
    partial["w_in"] = weight_grad_in(h1, dproj)
    summed = reduce_prepare(("w_in",), partial, "w_in")
    grad_x, dg1, *from_chips = in_proj_bwd(dproj, w_in_g, xs, dx2, g1, [summed[0][0]])
    reduce_finish(("w_in",), summed, from_chips, "w_in")

    v512 = dict(conv_dw_b=dbdw, conv_ln_g=dlng, conv_ln_b=dlnb)
    v1024 = dict(norm_mix_pre=dg1, b_conv_branch=dbcb, norm_mix_post=dg2, norm_ffn_pre=dg3, norm_ffn_post=dg4)
    gsum = small_all_reduce(ddw, v512, v1024, loss_parts)
    loss = gsum[LOSS_ROW, 0]
    as_rows = lambda n, a: a if n == "conv_dw_w" else a.reshape(1, -1)
    small_names = ("conv_dw_w",) + VEC512 + VEC1024
    small = adamw_small(gsum, {n: tuple(as_rows(n, d[n]) for d in (weights, mom, var)) for n in small_names})
    for n in small_names:
        grads[n], deltas[n], new_m[n], new_v[n] = (a.reshape(weights[n].shape) for a in small[n])

    return (loss, grad_x.reshape(1, SEQ, D_MODEL), *[grads[n] for n in order], *[deltas[n] for n in order],
            *[new_m[n] for n in order], *[new_v[n] for n in order])
```

```python
import jax
import jax.numpy as jnp
from jax import lax
from jax.experimental import pallas as pl
from jax.experimental.pallas import tpu as pltpu

F32 = jnp.float32
MM = jnp.bfloat16

SEQ = 2048
D_MODEL = 1024
CONV_DIM = 512
ATT_DIM = 512
CONV_WIDTH = 31
D_FF = 2816
IN_COLS = 2 * CONV_DIM + 3 * ATT_DIM + 2 * D_MODEL
N_CHIPS = 4
IN_SHARD = IN_COLS // N_CHIPS
UP_SHARD = 2 * D_FF // N_CHIPS
BR_SHARD = D_MODEL // N_CHIPS
EPS = 1e-6
ATT_SCALE = 0.125

TM = 256
GLU_ROWS = 256
TQ = 128
CONV_TILE = 64
CONV_WIN = CONV_TILE + 32
VMEM_LIMIT = 56 * 1024 * 1024

ADAM_LR = 0.001
ADAM_B1 = 0.9
ADAM_B2 = 0.999
ADAM_EPS = 1e-08
ADAM_WD = 0.01
ADAM_STEP = 10

MESH = pl.DeviceIdType.MESH
ANY = pl.BlockSpec(memory_space=pl.ANY)
VMEM_SPEC = pl.BlockSpec(memory_space=pltpu.VMEM)

NT_DIMS = (((1,), (1,)), ((), ()))
TN_DIMS = (((0,), (0,)), ((), ()))

IN_PIECES = (("ci", 0, 1024), ("q", 1024, 1536), ("k", 1536, 2048), ("v", 2048, 2560),
             ("gc", 2560, 3584), ("ga", 3584, 4608))


def _params(sem=None, vmem=VMEM_LIMIT):
    return pltpu.CompilerParams(dimension_semantics=sem, vmem_limit_bytes=vmem)


def _dot(a, b):
    return jnp.dot(a, b, preferred_element_type=F32)


def _dot_nt(a, b):
    return lax.dot_general(a, b, NT_DIMS, preferred_element_type=F32)


def _dot_tn(a, b):
    return lax.dot_general(a, b, TN_DIMS, preferred_element_type=F32)


def _sigmoid(x):
    return 1.0 / (1.0 + jnp.exp(-x))


def _rms(x):
    r = lax.rsqrt(jnp.mean(x * x, axis=-1, keepdims=True) + EPS)
    return x * r, r


def _rms_bwd(dy_g, n, r):
    return r * (dy_g - n * jnp.mean(dy_g * n, axis=-1, keepdims=True))


def _row_tile_spec(width, tm=TM):
    return pl.BlockSpec((tm, width), lambda i: (i, 0))


def _full_spec(shape):
    nd = len(shape)
    return pl.BlockSpec(shape, lambda *_: (0,) * nd)


def _weight_spec(shape):
    nd = len(shape)
    return pl.BlockSpec(shape, lambda *_: (0,) * nd, pipeline_mode=pl.Buffered(1))


def _acc_rows(ref, val, first):
    @pl.when(first)
    def _():
        ref[...] = val

    @pl.when(jnp.logical_not(first))
    def _():
        ref[...] += val


def _gather_behind_grid(ag, n_steps, step=None):
    step = pl.program_id(0) if step is None else step
    pl.when(step == 0)(ag.start)
    pl.when(step == n_steps - 2)(ag.forward)
    return lambda: pl.when(step == n_steps - 1)(ag.finish)


def in_proj_fwd(x, g1, w_in_g, gather=()):
    ng = len(gather)
    nt = SEQ // TM

    def body(*refs):
        x_ref, g_ref, w_ref = refs[:3]
        h_ref, ci_ref, q_ref, k_ref, v_ref, gc_ref, ga_ref = refs[3 + ng:10 + ng]
        if ng:
            done = _gather_behind_grid(_Gather([s.shape for s in gather], refs[3:3 + ng],
                                               refs[10 + ng:10 + 2 * ng], refs[10 + 2 * ng:]), nt)
        n, _ = _rms(x_ref[...])
        h = (n * g_ref[...]).astype(MM)
        h_ref[...] = h
        outs = dict(ci=ci_ref, q=q_ref, k=k_ref, v=v_ref, gc=gc_ref, ga=ga_ref)
        for j in range(N_CHIPS):
            p = _dot(h, w_ref[j])
            g0 = j * IN_SHARD
            for name, s, e in IN_PIECES:
                lo, hi = max(s, g0), min(e, g0 + IN_SHARD)
                if lo < hi:
                    ref = outs[name]
                    part = p[:, lo - g0:hi - g0]
                    if name == "q":
                        part = part * ATT_SCALE
                    ref[:, lo - s:hi - s] = part.astype(ref.dtype)
        if ng:
            done()

    out_shape = [
        jax.ShapeDtypeStruct((SEQ, D_MODEL), MM),
        jax.ShapeDtypeStruct((SEQ, 2 * CONV_DIM), F32),
        jax.ShapeDtypeStruct((SEQ, ATT_DIM), MM),
        jax.ShapeDtypeStruct((SEQ, ATT_DIM), MM),
        jax.ShapeDtypeStruct((SEQ, ATT_DIM), MM),
        jax.ShapeDtypeStruct((SEQ, D_MODEL), F32),
        jax.ShapeDtypeStruct((SEQ, D_MODEL), F32),
    ]
    return pl.pallas_call(
        body, name="in_proj_fwd", grid=(nt,),
        out_shape=out_shape + _Gather.out_shapes(gather),
        in_specs=[_row_tile_spec(D_MODEL), _full_spec((1, D_MODEL)), _weight_spec(w_in_g.shape)] + [ANY] * ng,
        out_specs=[_row_tile_spec(s.shape[1]) for s in out_shape] + [ANY] * ng,
        scratch_shapes=_Gather.scratch(gather) if ng else [],
        compiler_params=_params(("arbitrary",)),
    )(x, g1, w_in_g, *gather)


def _shifted_sum(win, terms):
    by_rot = {}
    for m, coef in terms:
        by_rot.setdefault(m % 8, []).append((m // 8, coef))
    acc = None
    n = win.shape[0]
    for rot in sorted(by_rot):
        shifted = win if rot == 0 else pltpu.roll(win, n - rot, 0)
        for a, coef in by_rot[rot]:
            t = coef * shifted[8 * a:8 * a + CONV_TILE, :]
            acc = t if acc is None else acc + t
    return acc


def _glu_into(ci_ref, upad_ref):
    upad_ref[0:32, :] = jnp.zeros((32, CONV_DIM), F32)

    def step(i, c):
        t0 = pl.multiple_of(i * GLU_ROWS, GLU_ROWS)
        a = ci_ref[pl.ds(t0, GLU_ROWS), 0:CONV_DIM]
        b = ci_ref[pl.ds(t0, GLU_ROWS), CONV_DIM:2 * CONV_DIM]
        upad_ref[pl.ds(t0 + 32, GLU_ROWS), :] = a * _sigmoid(b)
        return c

    lax.fori_loop(0, SEQ // GLU_ROWS, step, 0)


def _layernorm_parts(u1):
    mu = jnp.mean(u1, axis=-1, keepdims=True)
    xc = u1 - mu
    rstd = lax.rsqrt(jnp.mean(xc * xc, axis=-1, keepdims=True) + EPS)
    return xc * rstd, rstd


def conv_fwd(ci, w_dw, b_dw, ln_g, ln_b, gather=()):
    ng = len(gather)
    n_tiles = SEQ // CONV_TILE

    def body(*refs):
        ci_ref, w_ref, b_ref, g_ref, bb_ref = refs[:5]
        u1_ref, u3_ref = refs[5 + ng:7 + ng]
        upad_ref = refs[7 + 2 * ng]
        if ng:
            ag = _Gather([s.shape for s in gather], refs[5:5 + ng], refs[7 + ng:7 + 2 * ng], refs[8 + 2 * ng:])
            ag.start()
        _glu_into(ci_ref, upad_ref)

        def step(i, c):
            if ng:
                pl.when(i == n_tiles - n_tiles // 4)(ag.forward)
            t0 = pl.multiple_of(i * CONV_TILE, CONV_TILE)
            win = upad_ref[pl.ds(t0, CONV_WIN), :]
            u1 = _shifted_sum(win, [(j + 2, w_ref[j:j + 1, :]) for j in range(CONV_WIDTH)]) + b_ref[...]
            u1_ref[pl.ds(t0, CONV_TILE), :] = u1
            xh, _ = _layernorm_parts(u1)
            u2 = xh * g_ref[...] + bb_ref[...]
            u3_ref[pl.ds(t0, CONV_TILE), :] = (u2 * _sigmoid(u2)).astype(MM)
            return c

        lax.fori_loop(0, n_tiles, step, 0)
        if ng:
            ag.finish()

    return pl.pallas_call(
        body, name="conv_fwd",
        out_shape=[jax.ShapeDtypeStruct((SEQ, CONV_DIM), F32), jax.ShapeDtypeStruct((SEQ, CONV_DIM), MM)]
                  + _Gather.out_shapes(gather),
        in_specs=[VMEM_SPEC] * 5 + [ANY] * ng, out_specs=[VMEM_SPEC] * 2 + [ANY] * ng,
        scratch_shapes=[pltpu.VMEM((SEQ + 32, CONV_DIM), F32)] + (_Gather.scratch(gather) if ng else []),
        compiler_params=_params(),
    )(ci, w_dw, b_dw, ln_g, ln_b, *gather)


def _softplus(z):
    return jnp.maximum(z, 0.0) + jnp.log(1.0 + jnp.exp(-jnp.abs(z)))


def _cumsum_weights(suffix, with_total):
    n = 256 if with_total else 128
    r = lax.broadcasted_iota(jnp.int32, (128, n), 0)
    c = lax.broadcasted_iota(jnp.int32, (128, n), 1)
    tri = (r >= c) if suffix else (r <= c)
    return jnp.logical_or(tri, c >= 128).astype(MM)


NO_SCORE = -1e30
N_KB = SEQ // TQ


def _score_bias(lane, row, i, j):
    keep = jnp.logical_and(i >= 0, jnp.logical_or(j < i, lane < row))
    return jnp.where(keep, 0.0, NO_SCORE)


def _block_pipeline(n_stages, descending, step, on_query_block=None):
    n_lag = n_stages - 1
    none = jnp.int32(-1)

    def shift(cur, lag):
        step([cur] + [(lag[2 * s], lag[2 * s + 1]) for s in range(n_lag)])
        return (cur[0], cur[1]) + tuple(lag[:-2])

    def outer(i, lag):
        if on_query_block is not None:
            on_query_block(i)

        def inner(n, lag):
            return shift((i, i - n if descending else n), lag)
        return lax.fori_loop(0, i + 1, inner, lag)

    lag = lax.fori_loop(0, N_KB, outer, (none,) * (2 * n_lag))
    lax.fori_loop(0, n_lag, lambda n, lag: shift((none, none), lag), lag)


def _head_masks():
    lane = lax.broadcasted_iota(jnp.int32, (TQ, 128), 1)
    row = lax.broadcasted_iota(jnp.int32, (TQ, 128), 0)
    return lane, row, lane < 64


def _pick_head(x, head0, h):
    zero = jnp.zeros_like(x)
    return jnp.where(head0, x, zero) if h == 0 else jnp.where(head0, zero, x)


N_PAIRS = ATT_DIM // 128


def attn_fwd(q, k, v, gather=()):
    ng = len(gather)

    def body(*refs):
        q_ref, k_ref, v_ref = refs[:3]
        o_ref, rc_ref = refs[3 + ng:5 + ng]
        acc_ref, r_ref, z_ref, spb_ref, ab_ref = refs[5 + 2 * ng:10 + 2 * ng]
        if ng:
            ag = _Gather([s.shape for s in gather], refs[3:3 + ng], refs[5 + ng:5 + 2 * ng], refs[10 + 2 * ng:])
            ag.start()
        lane, row, head0 = _head_masks()
        w = _cumsum_weights(suffix=True, with_total=True)
        acc_ref[...] = jnp.zeros_like(acc_ref)
        r_ref[...] = jnp.zeros_like(r_ref)
        rc_ref[...] = jnp.zeros_like(rc_ref)
        z_ref[...] = jnp.full(z_ref.shape, NO_SCORE, F32)
        spb_ref[...] = jnp.zeros_like(spb_ref)
        ab_ref[...] = jnp.zeros_like(ab_ref)

        def step(pairs):
            (i1, j1), (i2, j2), (i3, j3) = pairs
            q1, k1, q2, q3, k3 = (pl.multiple_of(jnp.maximum(b, 0) * TQ, TQ) for b in (i1, j1, i2, i3, j3))
            bias1 = _score_bias(lane, row, i1, j1)
            first2 = j2 == i2
            rc_rows = rc_ref[pl.ds(q2, TQ), :]
            for p in range(N_PAIRS):
                cols = slice(128 * p, 128 * (p + 1))
                qb = q_ref[pl.ds(q1, TQ), cols]
                kb = k_ref[pl.ds(k1, TQ), cols]
                vb = v_ref[pl.ds(k3, TQ), cols]
                for h in range(2):
                    hh = 2 * p + h
                    acc_ref[pl.ds(q3, TQ), cols] += _dot(ab_ref[hh], _pick_head(vb, head0, h))
                    r = _dot(spb_ref[hh], w)
                    r_in = jnp.where(first2, 0.0, r_ref[hh])
                    ab_ref[hh] = jnp.exp(z_ref[hh] - (r[:, :128] + r_in)).astype(MM)
                    rc_rows = jnp.where(jnp.logical_and(lane == 16 * hh + j2, i2 >= 0), r_in, rc_rows)
                    r_ref[hh] = r_in + r[:, 128:]
                    z = _dot_nt(_pick_head(qb, head0, h), kb) + bias1
                    z_ref[hh] = z
                    spb_ref[hh] = _softplus(z).astype(MM)
            rc_ref[pl.ds(q2, TQ), :] = rc_rows

        if ng:
            _block_pipeline(3, True, step, lambda i: pl.when(i == N_KB - 2)(ag.forward))
        else:
            _block_pipeline(3, True, step)
        o_ref[...] = acc_ref[...].astype(MM)
        if ng:
            ag.finish()

    out_shape = [jax.ShapeDtypeStruct((SEQ, ATT_DIM), MM), jax.ShapeDtypeStruct((SEQ, 128), F32)]
    out_shape += _Gather.out_shapes(gather)
    return pl.pallas_call(
        body, name="attn_fwd", out_shape=out_shape,
        in_specs=[VMEM_SPEC] * 3 + [ANY] * ng, out_specs=[VMEM_SPEC] * 2 + [ANY] * ng,
        scratch_shapes=[pltpu.VMEM((SEQ, ATT_DIM), F32), pltpu.VMEM((8, TQ, 128), F32),
                        pltpu.VMEM((8, TQ, 128), F32), pltpu.VMEM((8, TQ, 128), MM), pltpu.VMEM((8, TQ, 128), MM)]
                       + (_Gather.scratch(gather) if ng else []),
        compiler_params=_params(),
    )(q, k, v, *gather)


def mix_fwd(u3, att, gc, ga, x, w_cb_g, b_cb, w_ab_g, w_out_g, g2, g3, gather=()):
    ng = len(gather)
    nt = SEQ // TM

    def body(*refs):
        u_ref, a_ref, gc_ref, ga_ref, x_ref, wcb_ref, bcb_ref, wab_ref, wout_ref, g2_ref, g3_ref = refs[:11]
        co_ref, ao_ref, mg_ref, mix_ref, x2_ref, h2_ref = refs[11 + ng:17 + ng]
        if ng:
            done = _gather_behind_grid(_Gather([s.shape for s in gather], refs[11:11 + ng],
                                               refs[17 + ng:17 + 2 * ng], refs[17 + 2 * ng:]), nt)
        u = u_ref[...]
        a = a_ref[...]
        co = jnp.concatenate([_dot(u, wcb_ref[j]) for j in range(N_CHIPS)], axis=1) + bcb_ref[...]
        ao = jnp.concatenate([_dot(a, wab_ref[j]) for j in range(N_CHIPS)], axis=1)
        co_ref[...] = co.astype(MM)
        ao_ref[...] = ao.astype(MM)
        merged = (_sigmoid(gc_ref[...]) * co + _sigmoid(ga_ref[...]) * ao).astype(MM)
        mg_ref[...] = merged
        mix = _dot(merged, wout_ref[...])
        mix_ref[...] = mix
        n2, _ = _rms(mix)
        x2 = x_ref[...] + n2 * g2_ref[...]
        x2_ref[...] = x2
        n3, _ = _rms(x2)
        h2_ref[...] = (n3 * g3_ref[...]).astype(MM)
        if ng:
            done()

    out_shape = [
        jax.ShapeDtypeStruct((SEQ, D_MODEL), MM), jax.ShapeDtypeStruct((SEQ, D_MODEL), MM),
        jax.ShapeDtypeStruct((SEQ, D_MODEL), MM), jax.ShapeDtypeStruct((SEQ, D_MODEL), F32),
        jax.ShapeDtypeStruct((SEQ, D_MODEL), F32), jax.ShapeDtypeStruct((SEQ, D_MODEL), MM),
    ]
    vec = _full_spec((1, D_MODEL))
    return pl.pallas_call(
        body, name="mix_fwd", grid=(nt,),
        out_shape=out_shape + _Gather.out_shapes(gather),
        in_specs=[_row_tile_spec(CONV_DIM), _row_tile_spec(ATT_DIM), _row_tile_spec(D_MODEL),
                  _row_tile_spec(D_MODEL), _row_tile_spec(D_MODEL), _weight_spec(w_cb_g.shape), vec,
                  _weight_spec(w_ab_g.shape), _weight_spec(w_out_g.shape), vec, vec] + [ANY] * ng,
        out_specs=[_row_tile_spec(D_MODEL)] * 6 + [ANY] * ng,
        scratch_shapes=_Gather.scratch(gather) if ng else [],
        compiler_params=_params(("arbitrary",)),
    )(u3, att, gc, ga, x, w_cb_g, b_cb, w_ab_g, w_out_g, g2, g3, *gather)


def ffn_up_fwd(h2, w_up_g, gather=()):
    ng = len(gather)
    nt = SEQ // TM

    def body(*refs):
        h_ref, wg_ref, wu_ref = refs[:3]
        gate_ref, up_ref, act_ref = refs[3 + ng:6 + ng]
        if ng:
            done = _gather_behind_grid(_Gather([s.shape for s in gather], refs[3:3 + ng],
                                               refs[6 + ng:6 + 2 * ng], refs[6 + 2 * ng:]),
                                       2 * nt, pl.program_id(0) * nt + pl.program_id(1))
        h = h_ref[...]
        gate = _dot(h, wg_ref[0])
        up = _dot(h, wu_ref[0])
        gate_ref[...] = gate.astype(MM)
        up_ref[...] = up.astype(MM)
        act_ref[...] = (gate * _sigmoid(gate) * up).astype(MM)
        if ng:
            done()

    tile = pl.BlockSpec((TM, UP_SHARD), lambda n, i: (i, n))
    act = jax.ShapeDtypeStruct((SEQ, D_FF), MM)
    return pl.pallas_call(
        body, name="ffn_up_fwd", grid=(2, nt), out_shape=[act, act, act] + _Gather.out_shapes(gather),
        in_specs=[pl.BlockSpec((TM, D_MODEL), lambda n, i: (i, 0)),
                  pl.BlockSpec((1, D_MODEL, UP_SHARD), lambda n, i: (n, 0, 0)),
                  pl.BlockSpec((1, D_MODEL, UP_SHARD), lambda n, i: (n + 2, 0, 0))] + [ANY] * ng,
        out_specs=[tile, tile, tile] + [ANY] * ng,
        scratch_shapes=_Gather.scratch(gather) if ng else [],
        compiler_params=_params(("arbitrary", "arbitrary")),
    )(h2, w_up_g, w_up_g, *gather)


def ffn_down_loss(act, w_down_g, x2, target, g4):
    def body(act_ref, wd_ref, x2_ref, t_ref, g_ref, dff_ref, dy_ref, loss_ref, dg_ref):
        ff = _dot(act_ref[...], wd_ref[...])
        n4, r4 = _rms(ff)
        g4v = g_ref[...]
        err = x2_ref[...] + n4 * g4v - t_ref[...]
        row_loss = jnp.mean(err * err, axis=-1, keepdims=True)
        loss_ref[...] = jnp.zeros((8, 128), F32) + 0.5 * jnp.sum(row_loss, axis=0, keepdims=True)
        dy = err * (1.0 / D_MODEL)
        dy_ref[...] = dy
        dff_ref[...] = _rms_bwd(dy * g4v, n4, r4).astype(MM)
        _acc_rows(dg_ref, jnp.sum(dy * n4, axis=0, keepdims=True), pl.program_id(0) == 0)

    nt = SEQ // TM
    vec = _full_spec((1, D_MODEL))
    return pl.pallas_call(
        body, name="ffn_down_loss", grid=(nt,),
        out_shape=(jax.ShapeDtypeStruct((SEQ, D_MODEL), MM), jax.ShapeDtypeStruct((SEQ, D_MODEL), F32),
                   jax.ShapeDtypeStruct((nt * 8, 128), F32), jax.ShapeDtypeStruct((1, D_MODEL), F32)),
        in_specs=[_row_tile_spec(D_FF), _weight_spec(w_down_g.shape), _row_tile_spec(D_MODEL),
                  _row_tile_spec(D_MODEL), vec],
        out_specs=[_row_tile_spec(D_MODEL), _row_tile_spec(D_MODEL),
                   pl.BlockSpec((8, 128), lambda i: (i, 0)), vec],
        compiler_params=_params(("arbitrary",)),
    )(act, w_down_g, x2, target, g4)


def ffn_act_bwd(dff, w_down_g, gate, up):
    def body(dff_ref, wd_ref, gate_ref, up_ref, dgu_ref):
        dact = _dot_nt(dff_ref[...], wd_ref[...])
        gate = gate_ref[...].astype(F32)
        sg = _sigmoid(gate)
        dgu_ref[:, 0:D_FF] = (dact * up_ref[...].astype(F32) * (sg * (1.0 + gate * (1.0 - sg)))).astype(MM)
        dgu_ref[:, D_FF:2 * D_FF] = (dact * (gate * sg)).astype(MM)

    return pl.pallas_call(
        body, name="ffn_act_bwd", grid=(SEQ // TM,),
        out_shape=jax.ShapeDtypeStruct((SEQ, 2 * D_FF), MM),
        in_specs=[_row_tile_spec(D_MODEL), _weight_spec(w_down_g.shape), _row_tile_spec(D_FF), _row_tile_spec(D_FF)],
        out_specs=_row_tile_spec(2 * D_FF),
        compiler_params=_params(("arbitrary",)),
    )(dff, w_down_g, gate, up)


def ffn_in_bwd(dgu, w_up_g, x2, mix, dy, g3, g2):
    def body(dgu_ref, w_ref, x2_ref, mix_ref, dy_ref, g3_ref, g2_ref, dx2_ref, dmix_ref, dg3_ref, dg2_ref):
        dh2 = None
        for j in range(N_CHIPS):
            t = _dot_nt(dgu_ref[:, j * UP_SHARD:(j + 1) * UP_SHARD], w_ref[j])
            dh2 = t if dh2 is None else dh2 + t
        first = pl.program_id(0) == 0
        n3, r3 = _rms(x2_ref[...])
        dx2 = dy_ref[...] + _rms_bwd(dh2 * g3_ref[...], n3, r3)
        dx2_ref[...] = dx2
        _acc_rows(dg3_ref, jnp.sum(dh2 * n3, axis=0, keepdims=True), first)
        n2, r2 = _rms(mix_ref[...])
        dmix_ref[...] = _rms_bwd(dx2 * g2_ref[...], n2, r2).astype(MM)
        _acc_rows(dg2_ref, jnp.sum(dx2 * n2, axis=0, keepdims=True), first)

    vec = _full_spec((1, D_MODEL))
    return pl.pallas_call(
        body, name="ffn_in_bwd", grid=(SEQ // TM,),
        out_shape=(jax.ShapeDtypeStruct((SEQ, D_MODEL), F32), jax.ShapeDtypeStruct((SEQ, D_MODEL), MM),
                   jax.ShapeDtypeStruct((1, D_MODEL), F32), jax.ShapeDtypeStruct((1, D_MODEL), F32)),
        in_specs=[_row_tile_spec(2 * D_FF), _weight_spec(w_up_g.shape), _row_tile_spec(D_MODEL),
                  _row_tile_spec(D_MODEL), _row_tile_spec(D_MODEL), vec, vec],
        out_specs=[_row_tile_spec(D_MODEL), _row_tile_spec(D_MODEL), vec, vec],
        compiler_params=_params(("arbitrary",)),
    )(dgu, w_up_g, x2, mix, dy, g3, g2)


def merge_bwd(dmix, w_out_g, gc, ga, co, ao, w_cb_g, w_ab_g):
    def body(dmix_ref, wout_ref, gc_ref, ga_ref, co_ref, ao_ref, wcb_ref, wab_ref,
             dco_ref, dao_ref, dg_ref, du3_ref, datt_ref, dbcb_ref):
        dm = _dot_nt(dmix_ref[...], wout_ref[...])
        sgc = _sigmoid(gc_ref[...])
        sga = _sigmoid(ga_ref[...])
        dco = dm * sgc
        dao = dm * sga
        dg_ref[:, 0:D_MODEL] = (dm * co_ref[...].astype(F32) * (sgc * (1.0 - sgc))).astype(MM)
        dg_ref[:, D_MODEL:2 * D_MODEL] = (dm * ao_ref[...].astype(F32) * (sga * (1.0 - sga))).astype(MM)
        _acc_rows(dbcb_ref, jnp.sum(dco, axis=0, keepdims=True), pl.program_id(0) == 0)
        dco_ref[...] = dco.astype(MM)
        dao_ref[...] = dao.astype(MM)
        du3 = None
        datt = None
        for j in range(N_CHIPS):
            cols = slice(j * BR_SHARD, (j + 1) * BR_SHARD)
            t = _dot_nt(dco_ref[:, cols], wcb_ref[j])
            s = _dot_nt(dao_ref[:, cols], wab_ref[j])
            du3 = t if du3 is None else du3 + t
            datt = s if datt is None else datt + s
        du3_ref[...] = du3
        datt_ref[...] = datt.astype(MM)

    wide = _row_tile_spec(D_MODEL)
    return pl.pallas_call(
        body, name="merge_bwd", grid=(SEQ // TM,),
        out_shape=(jax.ShapeDtypeStruct((SEQ, D_MODEL), MM), jax.ShapeDtypeStruct((SEQ, D_MODEL), MM),
                   jax.ShapeDtypeStruct((SEQ, 2 * D_MODEL), MM),
                   jax.ShapeDtypeStruct((SEQ, CONV_DIM), F32), jax.ShapeDtypeStruct((SEQ, ATT_DIM), MM),
                   jax.ShapeDtypeStruct((1, D_MODEL), F32)),
        in_specs=[wide, _weight_spec(w_out_g.shape), wide, wide, wide, wide,
                  _weight_spec(w_cb_g.shape), _weight_spec(w_ab_g.shape)],
        out_specs=[wide, wide, _row_tile_spec(2 * D_MODEL), _row_tile_spec(CONV_DIM), _row_tile_spec(ATT_DIM),
                   _full_spec((1, D_MODEL))],
        compiler_params=_params(("arbitrary",)),
    )(dmix, w_out_g, gc, ga, co, ao, w_cb_g, w_ab_g)


def conv_bwd(du3, u1, ci, w_dw, ln_g, ln_b):
    def body(du3_ref, u1_ref, ci_ref, w_ref, g_ref, bb_ref,
             dci_ref, dw_ref, dbdw_ref, dg_ref, db_ref, upad_ref, dpad_ref, dwacc_ref, vacc_ref):
        _glu_into(ci_ref, upad_ref)
        dpad_ref[SEQ:SEQ + 32, :] = jnp.zeros((32, CONV_DIM), F32)
        dwacc_ref[...] = jnp.zeros_like(dwacc_ref)
        vacc_ref[...] = jnp.zeros_like(vacc_ref)

        def fold8(t):
            s = t[0:8, :]
            for r in range(1, CONV_TILE // 8):
                s = s + t[8 * r:8 * r + 8, :]
            return s

        def pass1(i, c):
            t0 = pl.multiple_of(i * CONV_TILE, CONV_TILE)
            xh, rstd = _layernorm_parts(u1_ref[pl.ds(t0, CONV_TILE), :])
            gv = g_ref[...]
            u2 = xh * gv + bb_ref[...]
            s2 = _sigmoid(u2)
            du2 = du3_ref[pl.ds(t0, CONV_TILE), :] * (s2 * (1.0 + u2 * (1.0 - s2)))
            wv = du2 * gv
            du1 = rstd * (wv - jnp.mean(wv, axis=-1, keepdims=True)
                          - xh * jnp.mean(wv * xh, axis=-1, keepdims=True))
            dpad_ref[pl.ds(t0, CONV_TILE), :] = du1
            vacc_ref[0] += fold8(du2 * xh)
            vacc_ref[1] += fold8(du2)
            vacc_ref[2] += fold8(du1)
            win = upad_ref[pl.ds(t0, CONV_WIN), :]
            n = win.shape[0]
            for rot in range(8):
                shifted = win if rot == 0 else pltpu.roll(win, n - rot, 0)
                for a in range(5):
                    j = 8 * a + rot - 2
                    if 0 <= j < CONV_WIDTH:
                        dwacc_ref[j] += fold8(du1 * shifted[8 * a:8 * a + CONV_TILE, :])
            return c

        lax.fori_loop(0, SEQ // CONV_TILE, pass1, 0)

        def pass2(i, c):
            t0 = pl.multiple_of(i * CONV_TILE, CONV_TILE)
            win = dpad_ref[pl.ds(t0, CONV_WIN), :]
            du0 = _shifted_sum(win, [(30 - j, w_ref[j:j + 1, :]) for j in range(CONV_WIDTH)])
            a = ci_ref[pl.ds(t0, CONV_TILE), 0:CONV_DIM]
            sb = _sigmoid(ci_ref[pl.ds(t0, CONV_TILE), CONV_DIM:2 * CONV_DIM])
            dci_ref[pl.ds(t0, CONV_TILE), 0:CONV_DIM] = (du0 * sb).astype(MM)
            dci_ref[pl.ds(t0, CONV_TILE), CONV_DIM:2 * CONV_DIM] = (du0 * a * (sb * (1.0 - sb))).astype(MM)
            return c

        lax.fori_loop(0, SEQ // CONV_TILE, pass2, 0)

        for j in range(CONV_WIDTH):
            dw_ref[j:j + 1, :] = jnp.sum(dwacc_ref[j], axis=0, keepdims=True)
        dw_ref[CONV_WIDTH:32, :] = jnp.zeros((32 - CONV_WIDTH, CONV_DIM), F32)
        dg_ref[...] = jnp.sum(vacc_ref[0], axis=0, keepdims=True)
        db_ref[...] = jnp.sum(vacc_ref[1], axis=0, keepdims=True)
        dbdw_ref[...] = jnp.sum(vacc_ref[2], axis=0, keepdims=True)

    vec = jax.ShapeDtypeStruct((1, CONV_DIM), F32)
    return pl.pallas_call(
        body, name="conv_bwd",
        out_shape=(jax.ShapeDtypeStruct((SEQ, 2 * CONV_DIM), MM), jax.ShapeDtypeStruct((32, CONV_DIM), F32),
                   vec, vec, vec),
        in_specs=[VMEM_SPEC] * 6, out_specs=[VMEM_SPEC] * 5,
        scratch_shapes=[pltpu.VMEM((SEQ + 32, CONV_DIM), F32), pltpu.VMEM((SEQ + 32, CONV_DIM), F32),
                        pltpu.VMEM((CONV_WIDTH, 8, CONV_DIM), F32), pltpu.VMEM((3, 8, CONV_DIM), F32)],
        compiler_params=_params(),
    )(du3, u1, ci, w_dw, ln_g, ln_b)


def attn_bwd(q, k, v, datt, rc, scatter=()):
    ns = len(scatter)

    def body(*refs):
        q_ref, k_ref, v_ref, do_ref, rc_ref = refs[:5]
        dqkv_ref = refs[5 + ns]
        (dqa_ref, dka_ref, dva_ref, pc_ref, z_ref, sig1_ref, sig2_ref, g_ref, spb_ref, gb_ref, ab_ref,
         dzb_ref) = refs[6 + 2 * ns:18 + 2 * ns]
        if ns:
            sc = _Scatter(refs[5:5 + ns], refs[6 + ns:6 + 2 * ns], refs[18 + 2 * ns:])
            sc.start()
        lane, row, head0 = _head_masks()
        for ref in (dqa_ref, dka_ref, dva_ref, pc_ref):
            ref[...] = jnp.zeros_like(ref)
        z_ref[...] = jnp.full(z_ref.shape, NO_SCORE, F32)
        for ref in (sig1_ref, sig2_ref, spb_ref, ab_ref, g_ref, gb_ref, dzb_ref):
            ref[...] = jnp.zeros_like(ref)
        w_suffix = _cumsum_weights(suffix=True, with_total=False)
        w_prefix = _cumsum_weights(suffix=False, with_total=True)

        def step(pairs):
            (ia, ja), (ib, jb), (ic, jc), (id_, jd) = pairs
            qa, ka, qb_, kb_, qc, kc, qd, kd = (pl.multiple_of(jnp.maximum(b, 0) * TQ, TQ)
                                                for b in (ia, ja, ib, jb, ic, jc, id_, jd))
            bias_a = _score_bias(lane, row, ia, ja)
            rc_rows = rc_ref[pl.ds(qb_, TQ), :]
            first_c = jc == 0
            for p in range(N_PAIRS):
                cols = slice(128 * p, 128 * (p + 1))
                q_a = q_ref[pl.ds(qa, TQ), cols]
                k_a = k_ref[pl.ds(ka, TQ), cols]
                do_b = do_ref[pl.ds(qb_, TQ), cols]
                v_b = v_ref[pl.ds(kb_, TQ), cols]
                do_c = do_ref[pl.ds(qc, TQ), cols]
                q_d = q_ref[pl.ds(qd, TQ), cols]
                k_d = k_ref[pl.ds(kd, TQ), cols]
                for h in range(2):
                    hh = 2 * p + h
                    dzb = dzb_ref[hh]
                    dqa_ref[pl.ds(qd, TQ), cols] += _dot(dzb, _pick_head(k_d, head0, h))
                    dka_ref[pl.ds(kd, TQ), cols] += _dot_tn(dzb, _pick_head(q_d, head0, h))
                    r = _dot(gb_ref[hh], w_prefix)
                    p_in = jnp.where(first_c, 0.0, pc_ref[hh])
                    dzb_ref[hh] = (g_ref[hh] - sig2_ref[hh] * (r[:, :128] + p_in)).astype(MM)
                    pc_ref[hh] = p_in + r[:, 128:]
                    dva_ref[pl.ds(kc, TQ), cols] += _dot_tn(ab_ref[hh], _pick_head(do_c, head0, h))
                    r_in = jnp.sum(jnp.where(lane == 16 * hh + jb, rc_rows, 0.0), axis=1, keepdims=True)
                    a = jnp.exp(z_ref[hh] - (_dot(spb_ref[hh], w_suffix) + r_in))
                    g = _dot_nt(_pick_head(do_b, head0, h), v_b) * a
                    ab_ref[hh] = a.astype(MM)
                    g_ref[hh] = g
                    gb_ref[hh] = g.astype(MM)
                    sig2_ref[hh] = sig1_ref[hh]
                    z = _dot_nt(_pick_head(q_a, head0, h), k_a) + bias_a
                    sp = _softplus(z)
                    sig1_ref[hh] = jnp.exp(z - sp)
                    z_ref[hh] = z
                    spb_ref[hh] = sp.astype(MM)

        _block_pipeline(4, False, step)
        dqkv_ref[:, 0:ATT_DIM] = (dqa_ref[...] * ATT_SCALE).astype(MM)
        dqkv_ref[:, ATT_DIM:2 * ATT_DIM] = dka_ref[...].astype(MM)
        dqkv_ref[:, 2 * ATT_DIM:3 * ATT_DIM] = dva_ref[...].astype(MM)
        if ns:
            sc.finish()

    out = jax.ShapeDtypeStruct((SEQ, 3 * ATT_DIM), MM)
    return pl.pallas_call(
        body, name="attn_bwd", out_shape=[out] + _Scatter.out_shapes(scatter),
        in_specs=[VMEM_SPEC] * 5 + [ANY] * ns, out_specs=[VMEM_SPEC] + [ANY] * ns,
        scratch_shapes=[pltpu.VMEM((SEQ, ATT_DIM), F32)] * 3 + [pltpu.VMEM((8, TQ, 128), F32)] * 5
                       + [pltpu.VMEM((8, TQ, 128), MM)] * 4 + (_Scatter.scratch(ns) if ns else []),
        compiler_params=_params(),
    )(q, k, v, datt, rc, *scatter)


DPROJ_PIECES = ((0, 1024), (1024, 2560), (2560, 4608))


def _dproj_segments(j):
    g0, g1 = j * IN_SHARD, (j + 1) * IN_SHARD
    segs = []
    for p, (s, e) in enumerate(DPROJ_PIECES):
        lo, hi = max(s, g0), min(e, g1)
        if lo < hi:
            segs.append((p, lo - s, lo - g0, hi - lo))
    return segs


def in_proj_bwd(pieces, w_in_g, x, dx2, g1, scatter=()):
    ns = len(scatter)
    nt = SEQ // TM

    def body(*refs):
        p_refs = refs[:3]
        w_ref, x_ref, dx2_ref, g_ref = refs[3:7]
        dx_ref, dg_ref = refs[7 + ns:9 + ns]
        if ns:
            sc = _Scatter(refs[7:7 + ns], refs[9 + ns:9 + 2 * ns], refs[9 + 2 * ns:])
            pl.when(pl.program_id(0) == 0)(sc.start)
        dh = None
        for j in range(N_CHIPS):
            for p, lo, off, width in _dproj_segments(j):
                t = _dot_nt(p_refs[p][:, lo:lo + width], w_ref[j, :, off:off + width])
                dh = t if dh is None else dh + t
        n1, r1 = _rms(x_ref[...])
        dx_ref[...] = dx2_ref[...] + _rms_bwd(dh * g_ref[...], n1, r1)
        _acc_rows(dg_ref, jnp.sum(dh * n1, axis=0, keepdims=True), pl.program_id(0) == 0)
        if ns:
            pl.when(pl.program_id(0) == nt - 1)(sc.finish)

    vec = _full_spec((1, D_MODEL))
    return pl.pallas_call(
        body, name="in_proj_bwd", grid=(nt,),
        out_shape=[jax.ShapeDtypeStruct((SEQ, D_MODEL), F32), jax.ShapeDtypeStruct((1, D_MODEL), F32)]
                  + _Scatter.out_shapes(scatter),
        in_specs=[_row_tile_spec(p.shape[1]) for p in pieces]
                 + [_weight_spec(w_in_g.shape), _row_tile_spec(D_MODEL), _row_tile_spec(D_MODEL), vec] + [ANY] * ns,
        out_specs=[_row_tile_spec(D_MODEL), vec] + [ANY] * ns,
        scratch_shapes=_Scatter.scratch(ns) if ns else [],
        compiler_params=_params(("arbitrary",)),
    )(*pieces, w_in_g, x, dx2, g1, *scatter)


def weight_grad_in(h1, pieces):
    kh = D_MODEL // 2

    def body(a_ref, p0_ref, p1_ref, p2_ref, o_ref):
        p_refs = (p0_ref, p1_ref, p2_ref)
        a = a_ref[...]
        for j in range(N_CHIPS):
            @pl.when(pl.program_id(1) == j)
            def _():
                for p, lo, off, width in _dproj_segments(j):
                    o_ref[0, 0, :, off:off + width] = _dot_tn(a, p_refs[p][:, lo:lo + width]).astype(MM)

    return pl.pallas_call(
        body, name="dw_in", grid=(2, N_CHIPS), out_shape=jax.ShapeDtypeStruct((N_CHIPS, 2, kh, IN_SHARD), MM),
        in_specs=[pl.BlockSpec((SEQ, kh), lambda h, j: (0, h))] + [_weight_spec(p.shape) for p in pieces],
        out_specs=pl.BlockSpec((1, 1, kh, IN_SHARD), lambda h, j: (j, h, 0, 0)),
        compiler_params=_params(("arbitrary", "arbitrary")),
    )(h1, *pieces)


def weight_grad(a, b, name, col_sharded, tk=None):
    kin, n = a.shape[1], b.shape[1]

    def body(a_ref, b_ref, o_ref):
        if col_sharded:
            o_ref[0, 0] = _dot_tn(a_ref[...], b_ref[...]).astype(MM)
        else:
            o_ref[...] = _dot_tn(a_ref[...], b_ref[...]).astype(MM)

    if col_sharded:
        kh, ns = kin // 2, n // N_CHIPS
        out = jax.ShapeDtypeStruct((N_CHIPS, 2, kh, ns), MM)
        grid = (2, N_CHIPS)
        in_specs = [pl.BlockSpec((SEQ, kh), lambda h, j: (0, h)), pl.BlockSpec((SEQ, ns), lambda h, j: (0, j))]
        out_spec = pl.BlockSpec((1, 1, kh, ns), lambda h, j: (j, h, 0, 0))
        sem = ("arbitrary", "arbitrary")
    else:
        out = jax.ShapeDtypeStruct((kin, n), MM)
        grid = (kin // tk,)
        in_specs = [pl.BlockSpec((SEQ, tk), lambda r: (0, r)), pl.BlockSpec((SEQ, n), lambda r: (0, 0))]
        out_spec = pl.BlockSpec((tk, n), lambda r: (r, 0))
        sem = ("arbitrary",)
    res = pl.pallas_call(
        body, name=name, grid=grid, out_shape=out, in_specs=in_specs, out_specs=out_spec,
        compiler_params=_params(sem),
    )(a, b)
    if not col_sharded:
        res = res.reshape(N_CHIPS, 2, kin // (2 * N_CHIPS), n)
    return res


def _place():
    x, y, c = lax.axis_index("x"), lax.axis_index("y"), lax.axis_index("c")
    chips = [(1 - x, y), (x, 1 - y), (1 - x, 1 - y)]
    return x, y, c, chips


def _rcopy(src, dst, send_sem, recv_sem, dev):
    return pltpu.make_async_remote_copy(src_ref=src, dst_ref=dst, send_sem=send_sem, recv_sem=recv_sem,
                                        device_id=dev, device_id_type=MESH)


class _Gather:
    def __init__(self, shapes, w, o, scratch):
        self.n, self.shapes, self.w, self.o = len(w), shapes, w, o
        self.send, self.recv, self.fsend, self.frecv, self.loc_in, self.loc_out = scratch[:6]
        self.raw, self.stage = scratch[6:6 + self.n], scratch[6 + self.n:]
        self.x, self.y, self.c, self.chips = _place()
        self.me = 2 * self.x + self.y
        self.sib = (self.x, self.y, 1 - self.c)
        self.pairs = [(j, t) for j in range(3) for t in range(self.n)]

    @staticmethod
    def scratch(shards):
        n = len(shards)
        sems = pltpu.SemaphoreType.DMA
        return ([sems((3 * n,)), sems((3 * n,)), sems((3 * n,)), sems((3 * n,)), sems((n,)), sems((n,))]
                + [pltpu.VMEM(s.shape, s.dtype) for s in shards] + [pltpu.VMEM(s.shape, MM) for s in shards])

    @staticmethod
    def out_shapes(shards):
        return [jax.ShapeDtypeStruct((N_CHIPS,) + s.shape, MM) for s in shards]

    def _half(self, t, k, cc):
        rh = self.shapes[t][0] // 2
        return self.o[t].at[k, pl.ds(cc * rh, rh), :]

    def _chip(self, j):
        cx, cy = self.chips[j]
        return 2 * cx + cy, (cx, cy, self.c)

    def local_in(self, t):
        return pltpu.make_async_copy(self.w[t], self.raw[t], self.loc_in.at[t])

    def local_out(self, t):
        return pltpu.make_async_copy(self.stage[t], self.o[t].at[self.me], self.loc_out.at[t])

    def first(self, j, t):
        rh = self.shapes[t][0] // 2
        i = j * self.n + t
        return _rcopy(self.stage[t].at[pl.ds(self.c * rh, rh), :], self._half(t, self.me, self.c),
                      self.send.at[i], self.recv.at[i], self._chip(j)[1])

    def arrived(self, j, t):
        k, dev = self._chip(j)
        i = j * self.n + t
        blk = self._half(t, k, self.c)
        return _rcopy(blk, blk, self.send.at[i], self.recv.at[i], dev)

    def passed(self, j, t, cc):
        i = j * self.n + t
        blk = self._half(t, self._chip(j)[0], cc)
        return _rcopy(blk, blk, self.fsend.at[i], self.frecv.at[i], self.sib)

    def start(self):
        for t in range(self.n):
            self.local_in(t).start()
        for t in range(self.n):
            self.local_in(t).wait()
            self.stage[t][...] = self.raw[t][...].astype(MM)
            self.local_out(t).start()
        for j, t in self.pairs:
            self.first(j, t).start()

    def forward(self):
        for j, t in self.pairs:
            self.arrived(j, t).wait_recv()
            self.passed(j, t, self.c).start()

    def finish(self):
        for j, t in self.pairs:
            self.passed(j, t, 1 - self.c).wait_recv()
        for j, t in self.pairs:
            self.first(j, t).wait_send()
            self.passed(j, t, self.c).wait_send()
        for t in range(self.n):
            self.local_out(t).wait()


def all_gather_weights(shards, small):
    n = len(shards)
    shapes = [s.shape for s in shards]

    def body(*refs):
        w = refs[:n]
        sm = refs[n]
        o = refs[n + 1:2 * n + 1]
        osm = refs[2 * n + 1]
        ssend, srecv, sloc = refs[2 * n + 2:2 * n + 5]
        g = _Gather(shapes, w, o, refs[2 * n + 5:])
        own = pltpu.make_async_copy(sm, osm.at[g.me], sloc)
        own.start()
        g.start()
        small_cps = [_rcopy(sm, osm.at[g.me], ssend.at[j], srecv.at[j], g._chip(j)[1]) for j in range(3)]
        for cp in small_cps:
            cp.start()
        g.forward()
        g.finish()
        for j in range(3):
            k, dev = g._chip(j)
            _rcopy(sm, osm.at[k], ssend.at[j], srecv.at[j], dev).wait_recv()
            small_cps[j].wait_send()
        own.wait()

    out_shape = _Gather.out_shapes(shards)
    out_shape.append(jax.ShapeDtypeStruct((N_CHIPS,) + small.shape, small.dtype))
    sems = pltpu.SemaphoreType.DMA
    return pl.pallas_call(
        body, name="all_gather_weights", out_shape=out_shape,
        in_specs=[ANY] * (n + 1), out_specs=[ANY] * (n + 1),
        scratch_shapes=[sems((3,)), sems((3,)), sems] + _Gather.scratch(shards),
        compiler_params=_params(),
    )(*shards, small)


def sibling_exchange(grads, name):
    n = len(grads)

    def body(*refs):
        g = refs[:n]
        o = refs[n:2 * n]
        send, recv = refs[2 * n:]
        x, y, c, _ = _place()
        cps = [_rcopy(g[t].at[:, 1 - c], o[t], send.at[t], recv.at[t], (x, y, 1 - c)) for t in range(n)]
        for cp in cps:
            cp.start()
        for cp in cps:
            cp.wait()

    sems = pltpu.SemaphoreType.DMA
    return pl.pallas_call(
        body, name=name,
        out_shape=[jax.ShapeDtypeStruct((a.shape[0],) + a.shape[2:], a.dtype) for a in grads],
        in_specs=[ANY] * n, out_specs=[ANY] * n, scratch_shapes=[sems((n,)), sems((n,))],
    )(*grads)


class _Scatter:
    def __init__(self, p, o, sems):
        self.n, self.p, self.o = len(p), p, o
        self.send, self.recv = sems
        _, _, self.c, self.chips = _place()

    @staticmethod
    def scratch(n):
        sems = pltpu.SemaphoreType.DMA
        return [sems((3 * n,)), sems((3 * n,))]

    @staticmethod
    def out_shapes(parts):
        return [jax.ShapeDtypeStruct((3,) + a.shape[1:], a.dtype) for a in parts]

    def copies(self):
        cps = []
        for j, (cx, cy) in enumerate(self.chips):
            for t in range(self.n):
                i = j * self.n + t
                cps.append(_rcopy(self.p[t].at[2 * cx + cy], self.o[t].at[j], self.send.at[i], self.recv.at[i],
                                  (cx, cy, self.c)))
        return cps

    def start(self):
        for cp in self.copies():
            cp.start()

    def finish(self):
        for cp in self.copies():
            cp.wait()


def sibling_swap(halves, name):
    n = len(halves)

    def body(*refs):
        h = refs[:n]
        o = refs[n:2 * n]
        send, recv = refs[2 * n:]
        x, y, c, _ = _place()
        cps = [_rcopy(h[t], o[t], send.at[t], recv.at[t], (x, y, 1 - c)) for t in range(n)]
        for cp in cps:
            cp.start()
        for cp in cps:
            cp.wait()

    sems = pltpu.SemaphoreType.DMA
    return pl.pallas_call(
        body, name=name, out_shape=[jax.ShapeDtypeStruct(a.shape, a.dtype) for a in halves],
        in_specs=[ANY] * n, out_specs=[ANY] * n, scratch_shapes=[sems((n,)), sems((n,))],
    )(*halves)


def small_all_reduce(ddw, v512, v1024, loss_parts):
    rows, width = PACK_ROWS, 512
    n512, n1024 = len(VEC512), len(VEC1024)

    def body(*refs):
        ddw_ref = refs[0]
        a_refs = refs[1:1 + n512]
        b_refs = refs[1 + n512:1 + n512 + n1024]
        lp_ref, o_ref, p_ref, gath_ref, send, recv = refs[1 + n512 + n1024:]
        p_ref[...] = jnp.zeros_like(p_ref)
        p_ref[0:32, :] = ddw_ref[...]
        p_ref[LOSS_ROW:LOSS_ROW + 1, 0:128] = jnp.sum(lp_ref[...], axis=0, keepdims=True) * 0.125
        for i, r in enumerate(a_refs):
            p_ref[32 + i:33 + i, :] = r[...]
        for i, r in enumerate(b_refs):
            base = 32 + n512 + 2 * i
            p_ref[base:base + 1, :] = r[:, 0:512]
            p_ref[base + 1:base + 2, :] = r[:, 512:1024]
        x, y, c, _ = _place()
        me = 4 * x + 2 * y + c
        gath_ref[me] = p_ref[...]
        cps = []
        for k in range(1, 8):
            dx, dy, dc = (k >> 2) & 1, (k >> 1) & 1, k & 1
            px = 1 - x if dx else x
            py = 1 - y if dy else y
            pc = 1 - c if dc else c
            cps.append(_rcopy(p_ref, gath_ref.at[me], send.at[k - 1], recv.at[k - 1], (px, py, pc)))
        for cp in cps:
            cp.start()
        for k in range(1, 8):
            dx, dy, dc = (k >> 2) & 1, (k >> 1) & 1, k & 1
            px = 1 - x if dx else x
            py = 1 - y if dy else y
            pc = 1 - c if dc else c
            _rcopy(p_ref, gath_ref.at[4 * px + 2 * py + pc], send.at[k - 1], recv.at[k - 1], (px, py, pc)).wait_recv()
        for cp in cps:
            cp.wait_send()
        total = gath_ref[0]
        for d in range(1, 8):
            total = total + gath_ref[d]
        o_ref[...] = total

    sems = pltpu.SemaphoreType.DMA
    n_in = 2 + n512 + n1024
    return pl.pallas_call(
        body, name="small_all_reduce", out_shape=jax.ShapeDtypeStruct((rows, width), F32),
        in_specs=[VMEM_SPEC] * n_in, out_specs=VMEM_SPEC,
        scratch_shapes=[pltpu.VMEM((rows, width), F32), pltpu.VMEM((8, rows, width), F32), sems((7,)), sems((7,))],
    )(ddw, *[v512[n] for n in VEC512], *[v1024[n] for n in VEC1024], loss_parts)


def _row_block(r):
    for tr in (512, 352, 256, 128):
        if r % tr == 0:
            return tr
    return r


def add_halves(g, recv, name):
    _, _, r, w = g.shape
    tr = _row_block(r)

    def body(g0_ref, g1_ref, r_ref, ob_ref, own_ref):
        k = pl.program_id(1)
        c = lax.axis_index("c")
        me = 2 * lax.axis_index("x") + lax.axis_index("y")
        t = jnp.where(c == 0, g0_ref[0, 0], g1_ref[0, 0]).astype(F32) + r_ref[0].astype(F32)
        ob_ref[0] = t.astype(MM)
        mine = jnp.where(k == me, t, 0.0)

        @pl.when(k == 0)
        def _():
            own_ref[...] = mine

        @pl.when(k != 0)
        def _():
            own_ref[...] += mine

    return pl.pallas_call(
        body, name=name, grid=(r // tr, N_CHIPS),
        in_specs=[pl.BlockSpec((1, 1, tr, w), lambda i, k: (k, 0, i, 0)),
                  pl.BlockSpec((1, 1, tr, w), lambda i, k: (k, 1, i, 0)),
                  pl.BlockSpec((1, tr, w), lambda i, k: (k, i, 0))],
        out_specs=[pl.BlockSpec((1, tr, w), lambda i, k: (k, i, 0)),
                   pl.BlockSpec((tr, w), lambda i, k: (i, 0))],
        out_shape=(jax.ShapeDtypeStruct((N_CHIPS, r, w), MM), jax.ShapeDtypeStruct((r, w), F32)),
        compiler_params=_params(("arbitrary", "arbitrary")),
    )(g, g, recv)


def sum_parts(own, rin, name):
    _, r, w = rin.shape
    tr = _row_block(r)

    def body(o_ref, r_ref, out_ref):
        out_ref[...] = ((o_ref[...] + r_ref[0].astype(F32)) + r_ref[1].astype(F32)) + r_ref[2].astype(F32)

    return pl.pallas_call(
        body, name=name, grid=(r // tr,), out_shape=jax.ShapeDtypeStruct((r, w), F32),
        in_specs=[pl.BlockSpec((tr, w), lambda i: (i, 0)), pl.BlockSpec((3, tr, w), lambda i: (0, i, 0))],
        out_specs=pl.BlockSpec((tr, w), lambda i: (i, 0)),
        compiler_params=_params(("arbitrary",)),
    )(own, rin)


def _adamw_math(w, g, m, v):
    mn = ADAM_B1 * m + (1.0 - ADAM_B1) * g
    vn = ADAM_B2 * v + (1.0 - ADAM_B2) * (g * g)
    m_hat = mn / (1.0 - ADAM_B1 ** ADAM_STEP)
    v_hat = vn / (1.0 - ADAM_B2 ** ADAM_STEP)
    return -ADAM_LR * (m_hat / (jnp.sqrt(v_hat) + ADAM_EPS) + ADAM_WD * w), mn, vn


def adamw(w, mine, other, m, v, name):
    r, c = w.shape
    rh = r // 2
    tr = _row_block(rh)
    if c >= 1024 and tr % 512 == 0:
        tr = 256
    nb = rh // tr

    def body(w_ref, a_ref, b_ref, m_ref, v_ref, go_ref, d_ref, mo_ref, vo_ref):
        gv = jnp.where(lax.axis_index("c") == pl.program_id(0), a_ref[...], b_ref[...])
        go_ref[...] = gv
        d_ref[...], mo_ref[...], vo_ref[...] = _adamw_math(w_ref[...], gv, m_ref[...], v_ref[...])

    spec = pl.BlockSpec((tr, c), lambda h, i: (h * nb + i, 0))
    half = pl.BlockSpec((tr, c), lambda h, i: (i, 0))
    out = jax.ShapeDtypeStruct((r, c), F32)
    return pl.pallas_call(
        body, name=name, grid=(2, nb), out_shape=(out, out, out, out),
        in_specs=[spec, half, half, spec, spec], out_specs=[spec] * 4,
        compiler_params=_params(("arbitrary", "arbitrary")),
    )(w, mine, other, m, v)


def adamw_small(gsum, params):
    names = list(params)
    flat = [a for n in names for a in params[n]]

    def body(*refs):
        g_ref = refs[0]
        ins = refs[1:1 + 3 * len(names)]
        outs = refs[1 + 3 * len(names):]
        me = 2 * lax.axis_index("x") + lax.axis_index("y")
        for i, n in enumerate(names):
            w_ref, m_ref, v_ref = ins[3 * i:3 * i + 3]
            go_ref, d_ref, mo_ref, vo_ref = outs[4 * i:4 * i + 4]
            if n == "conv_dw_w":
                gv = jnp.zeros((CONV_WIDTH, 128), F32)
                for k in range(N_CHIPS):
                    gv = gv + jnp.where(me == k, g_ref[0:CONV_WIDTH, 128 * k:128 * (k + 1)], 0.0)
            elif n in VEC512:
                r0 = 32 + VEC512.index(n)
                gv = g_ref[r0:r0 + 1, :]
            else:
                r0 = 32 + len(VEC512) + 2 * VEC1024.index(n)
                gv = jnp.concatenate([g_ref[r0:r0 + 1, :], g_ref[r0 + 1:r0 + 2, :]], axis=1)
            go_ref[...] = gv
            d_ref[...], mo_ref[...], vo_ref[...] = _adamw_math(w_ref[...], gv, m_ref[...], v_ref[...])

    out_shape = [jax.ShapeDtypeStruct(params[n][0].shape, F32) for n in names for _ in range(4)]
    res = pl.pallas_call(
        body, name="adamw_small", out_shape=out_shape,
        in_specs=[VMEM_SPEC] * (1 + len(flat)), out_specs=[VMEM_SPEC] * len(out_shape),
        compiler_params=_params(),
    )(gsum, *flat)
    return {n: res[4 * i:4 * i + 4] for i, n in enumerate(names)}


REST = ("w_ffn_up", "w_ffn_down", "w_out", "w_conv_branch", "w_att_branch")
VEC512 = ("conv_dw_b", "conv_ln_g", "conv_ln_b")
VEC1024 = ("norm_mix_pre", "b_conv_branch", "norm_mix_post", "norm_ffn_pre", "norm_ffn_post")
PACK_ROWS = 48
LOSS_ROW = 47


def kernel(x, norm_mix_pre, w_in, conv_dw_w, conv_dw_b, conv_ln_g, conv_ln_b, w_conv_branch, b_conv_branch, w_att_branch, w_out, norm_mix_post, norm_ffn_pre, w_ffn_up, w_ffn_down, norm_ffn_post, loss_target, m_norm_mix_pre, m_w_in, m_conv_dw_w, m_conv_dw_b, m_conv_ln_g, m_conv_ln_b, m_w_conv_branch, m_b_conv_branch, m_w_att_branch, m_w_out, m_norm_mix_post, m_norm_ffn_pre, m_w_ffn_up, m_w_ffn_down, m_norm_ffn_post, v_norm_mix_pre, v_w_in, v_conv_dw_w, v_conv_dw_b, v_conv_ln_g, v_conv_ln_b, v_w_conv_branch, v_b_conv_branch, v_w_att_branch, v_w_out, v_norm_mix_post, v_norm_ffn_pre, v_w_ffn_up, v_w_ffn_down, v_norm_ffn_post):
    weights = dict(norm_mix_pre=norm_mix_pre, w_in=w_in, conv_dw_w=conv_dw_w, conv_dw_b=conv_dw_b, conv_ln_g=conv_ln_g, conv_ln_b=conv_ln_b, w_conv_branch=w_conv_branch, b_conv_branch=b_conv_branch, w_att_branch=w_att_branch, w_out=w_out, norm_mix_post=norm_mix_post, norm_ffn_pre=norm_ffn_pre, w_ffn_up=w_ffn_up, w_ffn_down=w_ffn_down, norm_ffn_post=norm_ffn_post)
    mom = dict(norm_mix_pre=m_norm_mix_pre, w_in=m_w_in, conv_dw_w=m_conv_dw_w, conv_dw_b=m_conv_dw_b, conv_ln_g=m_conv_ln_g, conv_ln_b=m_conv_ln_b, w_conv_branch=m_w_conv_branch, b_conv_branch=m_b_conv_branch, w_att_branch=m_w_att_branch, w_out=m_w_out, norm_mix_post=m_norm_mix_post, norm_ffn_pre=m_norm_ffn_pre, w_ffn_up=m_w_ffn_up, w_ffn_down=m_w_ffn_down, norm_ffn_post=m_norm_ffn_post)
    var = dict(norm_mix_pre=v_norm_mix_pre, w_in=v_w_in, conv_dw_w=v_conv_dw_w, conv_dw_b=v_conv_dw_b, conv_ln_g=v_conv_ln_g, conv_ln_b=v_conv_ln_b, w_conv_branch=v_w_conv_branch, b_conv_branch=v_b_conv_branch, w_att_branch=v_w_att_branch, w_out=v_w_out, norm_mix_post=v_norm_mix_post, norm_ffn_pre=v_norm_ffn_pre, w_ffn_up=v_w_ffn_up, w_ffn_down=v_w_ffn_down, norm_ffn_post=v_norm_ffn_post)
    order = list(weights)
    grads, deltas, new_m, new_v = {}, {}, {}, {}
    xs = x.reshape(SEQ, D_MODEL)
    tgt = loss_target.reshape(SEQ, D_MODEL)
    row = lambda a: a.reshape(1, -1)
    g1, g2, g3, g4 = (row(weights[n]) for n in ("norm_mix_pre", "norm_mix_post", "norm_ffn_pre", "norm_ffn_post"))
    ln_g, ln_b = row(conv_ln_g), row(conv_ln_b)

    def reduce_prepare(names, partial, tag):
        from_sib = sibling_exchange([partial[n] for n in names], "sibling_exchange_" + tag)
        return [add_halves(partial[n], r, "add_" + n) for n, r in zip(names, from_sib)]

    def reduce_finish(names, summed, from_chips, tag):
        halves = [sum_parts(s[1], r, "sum_" + n) for n, s, r in zip(names, summed, from_chips)]
        for n, a, b in zip(names, halves, sibling_swap(halves, "sibling_swap_" + tag)):
            grads[n], deltas[n], new_m[n], new_v[n] = adamw(weights[n], a, b, mom[n], var[n], "adamw_" + n)

    w_in_g, dw_g = all_gather_weights([w_in], conv_dw_w)
    w_dw_full = jnp.concatenate([dw_g[k] for k in range(N_CHIPS)], axis=1)
    h1, ci, q, k, v, gc, ga = in_proj_fwd(xs, g1, w_in_g)
    u1, u3, w_out_g, w_cb_g, w_ab_g = conv_fwd(ci, w_dw_full, row(conv_dw_b), ln_g, ln_b,
                                               [w_out, w_conv_branch, w_att_branch])
    att, rc, w_up_g = attn_fwd(q, k, v, [w_ffn_up])
    w_out_g = w_out_g.reshape(D_MODEL, D_MODEL)
    co, ao, merged, mix, x2, h2 = mix_fwd(u3, att, gc, ga, xs, w_cb_g, row(b_conv_branch), w_ab_g, w_out_g, g2, g3)
    gate, up, act, w_down_g = ffn_up_fwd(h2, w_up_g, [w_ffn_down])
    w_down_g = w_down_g.reshape(D_FF, D_MODEL)
    dff, dy, loss_parts, dg4 = ffn_down_loss(act, w_down_g, x2, tgt, g4)

    partial = {}
    dgu = ffn_act_bwd(dff, w_down_g, gate, up)
    partial["w_ffn_down"] = weight_grad(act, dff, "dw_ffn_down", False, tk=UP_SHARD)
    dx2, dmix, dg3, dg2 = ffn_in_bwd(dgu, w_up_g, x2, mix, dy, g3, g2)
    partial["w_ffn_up"] = weight_grad(h2, dgu, "dw_ffn_up", True)
    dco, dao, dg, du3, datt, dbcb = merge_bwd(dmix, w_out_g, gc, ga, co, ao, w_cb_g, w_ab_g)
    partial["w_out"] = weight_grad(merged, dmix, "dw_out", False, tk=512)
    partial["w_conv_branch"] = weight_grad(u3, dco, "dw_conv_branch", True)
    partial["w_att_branch"] = weight_grad(att, dao, "dw_att_branch", True)
    summed = reduce_prepare(REST, partial, "rest")
    dci, ddw, dbdw, dlng, dlnb = conv_bwd(du3, u1, ci, w_dw_full, ln_g, ln_b)
    dqkv, *from_chips = attn_bwd(q, k, v, datt, rc, [s[0] for s in summed])
    reduce_finish(REST, summed, from_chips, "rest")
    dproj = (dci, dqkv, dg)
    partial["w_in"] = weight_grad_in(h1, dproj)
    summed = reduce_prepare(("w_in",), partial, "w_in")
    grad_x, dg1, *from_chips = in_proj_bwd(dproj, w_in_g, xs, dx2, g1, [summed[0][0]])
    reduce_finish(("w_in",), summed, from_chips, "w_in")

    v512 = dict(conv_dw_b=dbdw, conv_ln_g=dlng, conv_ln_b=dlnb)
    v1024 = dict(norm_mix_pre=dg1, b_conv_branch=dbcb, norm_mix_post=dg2, norm_ffn_pre=dg3, norm_ffn_post=dg4)
    gsum = small_all_reduce(ddw, v512, v1024, loss_parts)
    loss = gsum[LOSS_ROW, 0]
    as_rows = lambda n, a: a if n == "conv_dw_w" else a.reshape(1, -1)
    small_names = ("conv_dw_w",) + VEC512 + VEC1024
    small = adamw_small(gsum, {n: tuple(as_rows(n, d[n]) for d in (weights, mom, var)) for n in small_names})
    for n in small_names:
        grads[n], deltas[n], new_m[n], new_v[n] = (a.reshape(weights[n].shape) for a in small[n])

    return (loss, grad_x.reshape(1, SEQ, D_MODEL), *[grads[n] for n in order], *[deltas[n] for n in order],
            *[new_m[n] for n in order], *[new_v[n] for n in order])
```

```python
import jax
import jax.numpy as jnp
from jax import lax
from jax.experimental import pallas as pl
from jax.experimental.pallas import tpu as pltpu

F32 = jnp.float32
MM = jnp.bfloat16

SEQ = 2048
D_MODEL = 1024
CONV_DIM = 512
ATT_DIM = 512
CONV_WIDTH = 31
D_FF = 2816
IN_COLS = 2 * CONV_DIM + 3 * ATT_DIM + 2 * D_MODEL
N_CHIPS = 4
IN_SHARD = IN_COLS // N_CHIPS
UP_SHARD = 2 * D_FF // N_CHIPS
BR_SHARD = D_MODEL // N_CHIPS
EPS = 1e-6
ATT_SCALE = 0.125

TM = 512
GLU_ROWS = 256
TQ = 128
CONV_TILE = 64
CONV_WIN = CONV_TILE + 32
VMEM_LIMIT = 56 * 1024 * 1024

ADAM_LR = 0.001
ADAM_B1 = 0.9
ADAM_B2 = 0.999
ADAM_EPS = 1e-08
ADAM_WD = 0.01
ADAM_STEP = 10

MESH = pl.DeviceIdType.MESH
ANY = pl.BlockSpec(memory_space=pl.ANY)
VMEM_SPEC = pl.BlockSpec(memory_space=pltpu.VMEM)

NT_DIMS = (((1,), (1,)), ((), ()))
TN_DIMS = (((0,), (0,)), ((), ()))

IN_PIECES = (("ci", 0, 1024), ("q", 1024, 1536), ("k", 1536, 2048), ("v", 2048, 2560),
             ("gc", 2560, 3584), ("ga", 3584, 4608))


def _params(sem=None, vmem=VMEM_LIMIT):
    return pltpu.CompilerParams(dimension_semantics=sem, vmem_limit_bytes=vmem)


def _dot(a, b):
    return jnp.dot(a, b, preferred_element_type=F32)


def _dot_nt(a, b):
    return lax.dot_general(a, b, NT_DIMS, preferred_element_type=F32)


def _dot_tn(a, b):
    return lax.dot_general(a, b, TN_DIMS, preferred_element_type=F32)


def _sigmoid(x):
    return 1.0 / (1.0 + jnp.exp(-x))


def _rms(x):
    r = lax.rsqrt(jnp.mean(x * x, axis=-1, keepdims=True) + EPS)
    return x * r, r


def _rms_bwd(dy_g, n, r):
    return r * (dy_g - n * jnp.mean(dy_g * n, axis=-1, keepdims=True))


def _row_tile_spec(width, tm=TM):
    return pl.BlockSpec((tm, width), lambda i: (i, 0))


def _full_spec(shape):
    nd = len(shape)
    return pl.BlockSpec(shape, lambda *_: (0,) * nd)


def _weight_spec(shape):
    nd = len(shape)
    return pl.BlockSpec(shape, lambda *_: (0,) * nd, pipeline_mode=pl.Buffered(1))


def _acc_rows(ref, val, first):
    @pl.when(first)
    def _():
        ref[...] = val

    @pl.when(jnp.logical_not(first))
    def _():
        ref[...] += val


def _gather_behind_grid(ag, n_steps, step=None):
    step = pl.program_id(0) if step is None else step
    pl.when(step == 0)(ag.start)
    pl.when(step == n_steps - 2)(ag.forward)
    return lambda: pl.when(step == n_steps - 1)(ag.finish)


def in_proj_fwd(x, g1, w_in_g, gather=()):
    ng = len(gather)
    nt = SEQ // TM

    def body(*refs):
        x_ref, g_ref, w_ref = refs[:3]
        h_ref, ci_ref, q_ref, k_ref, v_ref, gc_ref, ga_ref = refs[3 + ng:10 + ng]
        if ng:
            done = _gather_behind_grid(_Gather([s.shape for s in gather], refs[3:3 + ng],
                                               refs[10 + ng:10 + 2 * ng], refs[10 + 2 * ng:]), nt)
        n, _ = _rms(x_ref[...])
        h = (n * g_ref[...]).astype(MM)
        h_ref[...] = h
        outs = dict(ci=ci_ref, q=q_ref, k=k_ref, v=v_ref, gc=gc_ref, ga=ga_ref)
        for j in range(N_CHIPS):
            p = _dot(h, w_ref[j])
            g0 = j * IN_SHARD
            for name, s, e in IN_PIECES:
                lo, hi = max(s, g0), min(e, g0 + IN_SHARD)
                if lo < hi:
                    ref = outs[name]
                    part = p[:, lo - g0:hi - g0]
                    if name == "q":
                        part = part * ATT_SCALE
                    ref[:, lo - s:hi - s] = part.astype(ref.dtype)
        if ng:
            done()

    out_shape = [
        jax.ShapeDtypeStruct((SEQ, D_MODEL), MM),
        jax.ShapeDtypeStruct((SEQ, 2 * CONV_DIM), F32),
        jax.ShapeDtypeStruct((SEQ, ATT_DIM), MM),
        jax.ShapeDtypeStruct((SEQ, ATT_DIM), MM),
        jax.ShapeDtypeStruct((SEQ, ATT_DIM), MM),
        jax.ShapeDtypeStruct((SEQ, D_MODEL), F32),
        jax.ShapeDtypeStruct((SEQ, D_MODEL), F32),
    ]
    return pl.pallas_call(
        body, name="in_proj_fwd", grid=(nt,),
        out_shape=out_shape + _Gather.out_shapes(gather),
        in_specs=[_row_tile_spec(D_MODEL), _full_spec((1, D_MODEL)), _weight_spec(w_in_g.shape)] + [ANY] * ng,
        out_specs=[_row_tile_spec(s.shape[1]) for s in out_shape] + [ANY] * ng,
        scratch_shapes=_Gather.scratch(gather) if ng else [],
        compiler_params=_params(("arbitrary",)),
    )(x, g1, w_in_g, *gather)


def _shifted_sum(win, terms):
    by_rot = {}
    for m, coef in terms:
        by_rot.setdefault(m % 8, []).append((m // 8, coef))
    acc = None
    n = win.shape[0]
    for rot in sorted(by_rot):
        shifted = win if rot == 0 else pltpu.roll(win, n - rot, 0)
        for a, coef in by_rot[rot]:
            t = coef * shifted[8 * a:8 * a + CONV_TILE, :]
            acc = t if acc is None else acc + t
    return acc


def _glu_into(ci_ref, upad_ref):
    upad_ref[0:32, :] = jnp.zeros((32, CONV_DIM), F32)

    def step(i, c):
        t0 = pl.multiple_of(i * GLU_ROWS, GLU_ROWS)
        a = ci_ref[pl.ds(t0, GLU_ROWS), 0:CONV_DIM]
        b = ci_ref[pl.ds(t0, GLU_ROWS), CONV_DIM:2 * CONV_DIM]
        upad_ref[pl.ds(t0 + 32, GLU_ROWS), :] = a * _sigmoid(b)
        return c

    lax.fori_loop(0, SEQ // GLU_ROWS, step, 0)


def _layernorm_parts(u1):
    mu = jnp.mean(u1, axis=-1, keepdims=True)
    xc = u1 - mu
    rstd = lax.rsqrt(jnp.mean(xc * xc, axis=-1, keepdims=True) + EPS)
    return xc * rstd, rstd


def conv_fwd(ci, w_dw, b_dw, ln_g, ln_b, gather=()):
    ng = len(gather)
    n_tiles = SEQ // CONV_TILE

    def body(*refs):
        ci_ref, w_ref, b_ref, g_ref, bb_ref = refs[:5]
        u1_ref, u3_ref = refs[5 + ng:7 + ng]
        upad_ref = refs[7 + 2 * ng]
        if ng:
            ag = _Gather([s.shape for s in gather], refs[5:5 + ng], refs[7 + ng:7 + 2 * ng], refs[8 + 2 * ng:])
            ag.start()
        _glu_into(ci_ref, upad_ref)

        def step(i, c):
            if ng:
                pl.when(i == n_tiles - n_tiles // 4)(ag.forward)
            t0 = pl.multiple_of(i * CONV_TILE, CONV_TILE)
            win = upad_ref[pl.ds(t0, CONV_WIN), :]
            u1 = _shifted_sum(win, [(j + 2, w_ref[j:j + 1, :]) for j in range(CONV_WIDTH)]) + b_ref[...]
            u1_ref[pl.ds(t0, CONV_TILE), :] = u1
            xh, _ = _layernorm_parts(u1)
            u2 = xh * g_ref[...] + bb_ref[...]
            u3_ref[pl.ds(t0, CONV_TILE), :] = (u2 * _sigmoid(u2)).astype(MM)
            return c

        lax.fori_loop(0, n_tiles, step, 0)
        if ng:
            ag.finish()

    return pl.pallas_call(
        body, name="conv_fwd",
        out_shape=[jax.ShapeDtypeStruct((SEQ, CONV_DIM), F32), jax.ShapeDtypeStruct((SEQ, CONV_DIM), MM)]
                  + _Gather.out_shapes(gather),
        in_specs=[VMEM_SPEC] * 5 + [ANY] * ng, out_specs=[VMEM_SPEC] * 2 + [ANY] * ng,
        scratch_shapes=[pltpu.VMEM((SEQ + 32, CONV_DIM), F32)] + (_Gather.scratch(gather) if ng else []),
        compiler_params=_params(),
    )(ci, w_dw, b_dw, ln_g, ln_b, *gather)


def _softplus(z):
    return jnp.maximum(z, 0.0) + jnp.log(1.0 + jnp.exp(-jnp.abs(z)))


def _cumsum_weights(suffix, with_total):
    n = 256 if with_total else 128
    r = lax.broadcasted_iota(jnp.int32, (128, n), 0)
    c = lax.broadcasted_iota(jnp.int32, (128, n), 1)
    tri = (r >= c) if suffix else (r <= c)
    return jnp.logical_or(tri, c >= 128).astype(MM)


NO_SCORE = -1e30
N_KB = SEQ // TQ


def _score_bias(lane, row, i, j):
    keep = jnp.logical_and(i >= 0, jnp.logical_or(j < i, lane < row))
    return jnp.where(keep, 0.0, NO_SCORE)


def _block_pipeline(n_stages, descending, step, on_query_block=None):
    n_lag = n_stages - 1
    none = jnp.int32(-1)

    def shift(cur, lag):
        step([cur] + [(lag[2 * s], lag[2 * s + 1]) for s in range(n_lag)])
        return (cur[0], cur[1]) + tuple(lag[:-2])

    def outer(i, lag):
        if on_query_block is not None:
            on_query_block(i)

        def inner(n, lag):
            return shift((i, i - n if descending else n), lag)
        return lax.fori_loop(0, i + 1, inner, lag)

    lag = lax.fori_loop(0, N_KB, outer, (none,) * (2 * n_lag))
    lax.fori_loop(0, n_lag, lambda n, lag: shift((none, none), lag), lag)


def _head_masks():
    lane = lax.broadcasted_iota(jnp.int32, (TQ, 128), 1)
    row = lax.broadcasted_iota(jnp.int32, (TQ, 128), 0)
    return lane, row, lane < 64


def _pick_head(x, head0, h):
    zero = jnp.zeros_like(x)
    return jnp.where(head0, x, zero) if h == 0 else jnp.where(head0, zero, x)


N_PAIRS = ATT_DIM // 128


def attn_fwd(q, k, v, gather=()):
    ng = len(gather)

    def body(*refs):
        q_ref, k_ref, v_ref = refs[:3]
        o_ref, rc_ref = refs[3 + ng:5 + ng]
        acc_ref, r_ref, z_ref, spb_ref, ab_ref = refs[5 + 2 * ng:10 + 2 * ng]
        if ng:
            ag = _Gather([s.shape for s in gather], refs[3:3 + ng], refs[5 + ng:5 + 2 * ng], refs[10 + 2 * ng:])
            ag.start()
        lane, row, head0 = _head_masks()
        w = _cumsum_weights(suffix=True, with_total=True)
        acc_ref[...] = jnp.zeros_like(acc_ref)
        r_ref[...] = jnp.zeros_like(r_ref)
        rc_ref[...] = jnp.zeros_like(rc_ref)
        z_ref[...] = jnp.full(z_ref.shape, NO_SCORE, F32)
        spb_ref[...] = jnp.zeros_like(spb_ref)
        ab_ref[...] = jnp.zeros_like(ab_ref)

        def step(pairs):
            (i1, j1), (i2, j2), (i3, j3) = pairs
            q1, k1, q2, q3, k3 = (pl.multiple_of(jnp.maximum(b, 0) * TQ, TQ) for b in (i1, j1, i2, i3, j3))
            bias1 = _score_bias(lane, row, i1, j1)
            first2 = j2 == i2
            rc_rows = rc_ref[pl.ds(q2, TQ), :]
            for p in range(N_PAIRS):
                cols = slice(128 * p, 128 * (p + 1))
                qb = q_ref[pl.ds(q1, TQ), cols]
                kb = k_ref[pl.ds(k1, TQ), cols]
                vb = v_ref[pl.ds(k3, TQ), cols]
                for h in range(2):
                    hh = 2 * p + h
                    acc_ref[pl.ds(q3, TQ), cols] += _dot(ab_ref[hh], _pick_head(vb, head0, h))
                    r = _dot(spb_ref[hh], w)
                    r_in = jnp.where(first2, 0.0, r_ref[hh])
                    ab_ref[hh] = jnp.exp(z_ref[hh] - (r[:, :128] + r_in)).astype(MM)
                    rc_rows = jnp.where(jnp.logical_and(lane == 16 * hh + j2, i2 >= 0), r_in, rc_rows)
                    r_ref[hh] = r_in + r[:, 128:]
                    z = _dot_nt(_pick_head(qb, head0, h), kb) + bias1
                    z_ref[hh] = z
                    spb_ref[hh] = _softplus(z).astype(MM)
            rc_ref[pl.ds(q2, TQ), :] = rc_rows

        if ng:
            _block_pipeline(3, True, step, lambda i: pl.when(i == N_KB - 2)(ag.forward))
        else:
            _block_pipeline(3, True, step)
        o_ref[...] = acc_ref[...].astype(MM)
        if ng:
            ag.finish()

    out_shape = [jax.ShapeDtypeStruct((SEQ, ATT_DIM), MM), jax.ShapeDtypeStruct((SEQ, 128), F32)]
    out_shape += _Gather.out_shapes(gather)
    return pl.pallas_call(
        body, name="attn_fwd", out_shape=out_shape,
        in_specs=[VMEM_SPEC] * 3 + [ANY] * ng, out_specs=[VMEM_SPEC] * 2 + [ANY] * ng,
        scratch_shapes=[pltpu.VMEM((SEQ, ATT_DIM), F32), pltpu.VMEM((8, TQ, 128), F32),
                        pltpu.VMEM((8, TQ, 128), F32), pltpu.VMEM((8, TQ, 128), MM), pltpu.VMEM((8, TQ, 128), MM)]
                       + (_Gather.scratch(gather) if ng else []),
        compiler_params=_params(),
    )(q, k, v, *gather)


def mix_fwd(u3, att, gc, ga, x, w_cb_g, b_cb, w_ab_g, w_out_g, g2, g3, gather=()):
    ng = len(gather)
    nt = SEQ // TM

    def body(*refs):
        u_ref, a_ref, gc_ref, ga_ref, x_ref, wcb_ref, bcb_ref, wab_ref, wout_ref, g2_ref, g3_ref = refs[:11]
        co_ref, ao_ref, mg_ref, mix_ref, x2_ref, h2_ref = refs[11 + ng:17 + ng]
        if ng:
            done = _gather_behind_grid(_Gather([s.shape for s in gather], refs[11:11 + ng],
                                               refs[17 + ng:17 + 2 * ng], refs[17 + 2 * ng:]), nt)
        u = u_ref[...]
        a = a_ref[...]
        co = jnp.concatenate([_dot(u, wcb_ref[j]) for j in range(N_CHIPS)], axis=1) + bcb_ref[...]
        ao = jnp.concatenate([_dot(a, wab_ref[j]) for j in range(N_CHIPS)], axis=1)
        co_ref[...] = co.astype(MM)
        ao_ref[...] = ao.astype(MM)
        merged = (_sigmoid(gc_ref[...]) * co + _sigmoid(ga_ref[...]) * ao).astype(MM)
        mg_ref[...] = merged
        mix = _dot(merged, wout_ref[...])
        mix_ref[...] = mix
        n2, _ = _rms(mix)
        x2 = x_ref[...] + n2 * g2_ref[...]
        x2_ref[...] = x2
        n3, _ = _rms(x2)
        h2_ref[...] = (n3 * g3_ref[...]).astype(MM)
        if ng:
            done()

    out_shape = [
        jax.ShapeDtypeStruct((SEQ, D_MODEL), MM), jax.ShapeDtypeStruct((SEQ, D_MODEL), MM),
        jax.ShapeDtypeStruct((SEQ, D_MODEL), MM), jax.ShapeDtypeStruct((SEQ, D_MODEL), F32),
        jax.ShapeDtypeStruct((SEQ, D_MODEL), F32), jax.ShapeDtypeStruct((SEQ, D_MODEL), MM),
    ]
    vec = _full_spec((1, D_MODEL))
    return pl.pallas_call(
        body, name="mix_fwd", grid=(nt,),
        out_shape=out_shape + _Gather.out_shapes(gather),
        in_specs=[_row_tile_spec(CONV_DIM), _row_tile_spec(ATT_DIM), _row_tile_spec(D_MODEL),
                  _row_tile_spec(D_MODEL), _row_tile_spec(D_MODEL), _weight_spec(w_cb_g.shape), vec,
                  _weight_spec(w_ab_g.shape), _weight_spec(w_out_g.shape), vec, vec] + [ANY] * ng,
        out_specs=[_row_tile_spec(D_MODEL)] * 6 + [ANY] * ng,
        scratch_shapes=_Gather.scratch(gather) if ng else [],
        compiler_params=_params(("arbitrary",)),
    )(u3, att, gc, ga, x, w_cb_g, b_cb, w_ab_g, w_out_g, g2, g3, *gather)


def ffn_up_fwd(h2, w_up_g, gather=()):
    ng = len(gather)
    nt = SEQ // TM

    def body(*refs):
        h_ref, wg_ref, wu_ref = refs[:3]
        gate_ref, up_ref, act_ref = refs[3 + ng:6 + ng]
        if ng:
            done = _gather_behind_grid(_Gather([s.shape for s in gather], refs[3:3 + ng],
                                               refs[6 + ng:6 + 2 * ng], refs[6 + 2 * ng:]),
                                       2 * nt, pl.program_id(0) * nt + pl.program_id(1))
        h = h_ref[...]
        gate = _dot(h, wg_ref[0])
        up = _dot(h, wu_ref[0])
        gate_ref[...] = gate.astype(MM)
        up_ref[...] = up.astype(MM)
        act_ref[...] = (gate * _sigmoid(gate) * up).astype(MM)
        if ng:
            done()

    tile = pl.BlockSpec((TM, UP_SHARD), lambda n, i: (i, n))
    act = jax.ShapeDtypeStruct((SEQ, D_FF), MM)
    return pl.pallas_call(
        body, name="ffn_up_fwd", grid=(2, nt), out_shape=[act, act, act] + _Gather.out_shapes(gather),
        in_specs=[pl.BlockSpec((TM, D_MODEL), lambda n, i: (i, 0)),
                  pl.BlockSpec((1, D_MODEL, UP_SHARD), lambda n, i: (n, 0, 0)),
                  pl.BlockSpec((1, D_MODEL, UP_SHARD), lambda n, i: (n + 2, 0, 0))] + [ANY] * ng,
        out_specs=[tile, tile, tile] + [ANY] * ng,
        scratch_shapes=_Gather.scratch(gather) if ng else [],
        compiler_params=_params(("arbitrary", "arbitrary")),
    )(h2, w_up_g, w_up_g, *gather)


def ffn_down_loss(act, w_down_g, x2, target, g4):
    def body(act_ref, wd_ref, x2_ref, t_ref, g_ref, dff_ref, dy_ref, loss_ref, dg_ref):
        ff = _dot(act_ref[...], wd_ref[...])
        n4, r4 = _rms(ff)
        g4v = g_ref[...]
        err = x2_ref[...] + n4 * g4v - t_ref[...]
        row_loss = jnp.mean(err * err, axis=-1, keepdims=True)
        loss_ref[...] = jnp.zeros((8, 128), F32) + 0.5 * jnp.sum(row_loss, axis=0, keepdims=True)
        dy = err * (1.0 / D_MODEL)
        dy_ref[...] = dy
        dff_ref[...] = _rms_bwd(dy * g4v, n4, r4).astype(MM)
        _acc_rows(dg_ref, jnp.sum(dy * n4, axis=0, keepdims=True), pl.program_id(0) == 0)

    nt = SEQ // TM
    vec = _full_spec((1, D_MODEL))
    return pl.pallas_call(
        body, name="ffn_down_loss", grid=(nt,),
        out_shape=(jax.ShapeDtypeStruct((SEQ, D_MODEL), MM), jax.ShapeDtypeStruct((SEQ, D_MODEL), F32),
                   jax.ShapeDtypeStruct((nt * 8, 128), F32), jax.ShapeDtypeStruct((1, D_MODEL), F32)),
        in_specs=[_row_tile_spec(D_FF), _weight_spec(w_down_g.shape), _row_tile_spec(D_MODEL),
                  _row_tile_spec(D_MODEL), vec],
        out_specs=[_row_tile_spec(D_MODEL), _row_tile_spec(D_MODEL),
                   pl.BlockSpec((8, 128), lambda i: (i, 0)), vec],
        compiler_params=_params(("arbitrary",)),
    )(act, w_down_g, x2, target, g4)


def ffn_act_bwd(dff, w_down_g, gate, up):
    def body(dff_ref, wd_ref, gate_ref, up_ref, dgu_ref):
        dact = _dot_nt(dff_ref[...], wd_ref[...])
        gate = gate_ref[...].astype(F32)
        sg = _sigmoid(gate)
        dgu_ref[:, 0:D_FF] = (dact * up_ref[...].astype(F32) * (sg * (1.0 + gate * (1.0 - sg)))).astype(MM)
        dgu_ref[:, D_FF:2 * D_FF] = (dact * (gate * sg)).astype(MM)

    return pl.pallas_call(
        body, name="ffn_act_bwd", grid=(SEQ // TM,),
        out_shape=jax.ShapeDtypeStruct((SEQ, 2 * D_FF), MM),
        in_specs=[_row_tile_spec(D_MODEL), _weight_spec(w_down_g.shape), _row_tile_spec(D_FF), _row_tile_spec(D_FF)],
        out_specs=_row_tile_spec(2 * D_FF),
        compiler_params=_params(("arbitrary",)),
    )(dff, w_down_g, gate, up)


def ffn_in_bwd(dgu, w_up_g, x2, mix, dy, g3, g2):
    def body(dgu_ref, w_ref, x2_ref, mix_ref, dy_ref, g3_ref, g2_ref, dx2_ref, dmix_ref, dg3_ref, dg2_ref):
        dh2 = None
        for j in range(N_CHIPS):
            t = _dot_nt(dgu_ref[:, j * UP_SHARD:(j + 1) * UP_SHARD], w_ref[j])
            dh2 = t if dh2 is None else dh2 + t
        first = pl.program_id(0) == 0
        n3, r3 = _rms(x2_ref[...])
        dx2 = dy_ref[...] + _rms_bwd(dh2 * g3_ref[...], n3, r3)
        dx2_ref[...] = dx2
        _acc_rows(dg3_ref, jnp.sum(dh2 * n3, axis=0, keepdims=True), first)
        n2, r2 = _rms(mix_ref[...])
        dmix_ref[...] = _rms_bwd(dx2 * g2_ref[...], n2, r2).astype(MM)
        _acc_rows(dg2_ref, jnp.sum(dx2 * n2, axis=0, keepdims=True), first)

    vec = _full_spec((1, D_MODEL))
    return pl.pallas_call(
        body, name="ffn_in_bwd", grid=(SEQ // TM,),
        out_shape=(jax.ShapeDtypeStruct((SEQ, D_MODEL), F32), jax.ShapeDtypeStruct((SEQ, D_MODEL), MM),
                   jax.ShapeDtypeStruct((1, D_MODEL), F32), jax.ShapeDtypeStruct((1, D_MODEL), F32)),
        in_specs=[_row_tile_spec(2 * D_FF), _weight_spec(w_up_g.shape), _row_tile_spec(D_MODEL),
                  _row_tile_spec(D_MODEL), _row_tile_spec(D_MODEL), vec, vec],
        out_specs=[_row_tile_spec(D_MODEL), _row_tile_spec(D_MODEL), vec, vec],
        compiler_params=_params(("arbitrary",)),
    )(dgu, w_up_g, x2, mix, dy, g3, g2)


def merge_bwd(dmix, w_out_g, gc, ga, co, ao, w_cb_g, w_ab_g):
    def body(dmix_ref, wout_ref, gc_ref, ga_ref, co_ref, ao_ref, wcb_ref, wab_ref,
             dco_ref, dao_ref, dg_ref, du3_ref, datt_ref, dbcb_ref):
        dm = _dot_nt(dmix_ref[...], wout_ref[...])
        sgc = _sigmoid(gc_ref[...])
        sga = _sigmoid(ga_ref[...])
        dco = dm * sgc
        dao = dm * sga
        dg_ref[:, 0:D_MODEL] = (dm * co_ref[...].astype(F32) * (sgc * (1.0 - sgc))).astype(MM)
        dg_ref[:, D_MODEL:2 * D_MODEL] = (dm * ao_ref[...].astype(F32) * (sga * (1.0 - sga))).astype(MM)
        _acc_rows(dbcb_ref, jnp.sum(dco, axis=0, keepdims=True), pl.program_id(0) == 0)
        dco_ref[...] = dco.astype(MM)
        dao_ref[...] = dao.astype(MM)
        du3 = None
        datt = None
        for j in range(N_CHIPS):
            cols = slice(j * BR_SHARD, (j + 1) * BR_SHARD)
            t = _dot_nt(dco_ref[:, cols], wcb_ref[j])
            s = _dot_nt(dao_ref[:, cols], wab_ref[j])
            du3 = t if du3 is None else du3 + t
            datt = s if datt is None else datt + s
        du3_ref[...] = du3
        datt_ref[...] = datt.astype(MM)

    wide = _row_tile_spec(D_MODEL)
    return pl.pallas_call(
        body, name="merge_bwd", grid=(SEQ // TM,),
        out_shape=(jax.ShapeDtypeStruct((SEQ, D_MODEL), MM), jax.ShapeDtypeStruct((SEQ, D_MODEL), MM),
                   jax.ShapeDtypeStruct((SEQ, 2 * D_MODEL), MM),
                   jax.ShapeDtypeStruct((SEQ, CONV_DIM), F32), jax.ShapeDtypeStruct((SEQ, ATT_DIM), MM),
                   jax.ShapeDtypeStruct((1, D_MODEL), F32)),
        in_specs=[wide, _weight_spec(w_out_g.shape), wide, wide, wide, wide,
                  _weight_spec(w_cb_g.shape), _weight_spec(w_ab_g.shape)],
        out_specs=[wide, wide, _row_tile_spec(2 * D_MODEL), _row_tile_spec(CONV_DIM), _row_tile_spec(ATT_DIM),
                   _full_spec((1, D_MODEL))],
        compiler_params=_params(("arbitrary",)),
    )(dmix, w_out_g, gc, ga, co, ao, w_cb_g, w_ab_g)


def conv_bwd(du3, u1, ci, w_dw, ln_g, ln_b):
    def body(du3_ref, u1_ref, ci_ref, w_ref, g_ref, bb_ref,
             dci_ref, dw_ref, dbdw_ref, dg_ref, db_ref, upad_ref, dpad_ref, dwacc_ref, vacc_ref):
        _glu_into(ci_ref, upad_ref)
        dpad_ref[SEQ:SEQ + 32, :] = jnp.zeros((32, CONV_DIM), F32)
        dwacc_ref[...] = jnp.zeros_like(dwacc_ref)
        vacc_ref[...] = jnp.zeros_like(vacc_ref)

        def fold8(t):
            s = t[0:8, :]
            for r in range(1, CONV_TILE // 8):
                s = s + t[8 * r:8 * r + 8, :]
            return s

        def pass1(i, c):
            t0 = pl.multiple_of(i * CONV_TILE, CONV_TILE)
            xh, rstd = _layernorm_parts(u1_ref[pl.ds(t0, CONV_TILE), :])
            gv = g_ref[...]
            u2 = xh * gv + bb_ref[...]
            s2 = _sigmoid(u2)
            du2 = du3_ref[pl.ds(t0, CONV_TILE), :] * (s2 * (1.0 + u2 * (1.0 - s2)))
            wv = du2 * gv
            du1 = rstd * (wv - jnp.mean(wv, axis=-1, keepdims=True)
                          - xh * jnp.mean(wv * xh, axis=-1, keepdims=True))
            dpad_ref[pl.ds(t0, CONV_TILE), :] = du1
            vacc_ref[0] += fold8(du2 * xh)
            vacc_ref[1] += fold8(du2)
            vacc_ref[2] += fold8(du1)
            win = upad_ref[pl.ds(t0, CONV_WIN), :]
            n = win.shape[0]
            for rot in range(8):
                shifted = win if rot == 0 else pltpu.roll(win, n - rot, 0)
                for a in range(5):
                    j = 8 * a + rot - 2
                    if 0 <= j < CONV_WIDTH:
                        dwacc_ref[j] += fold8(du1 * shifted[8 * a:8 * a + CONV_TILE, :])
            return c

        lax.fori_loop(0, SEQ // CONV_TILE, pass1, 0)

        def pass2(i, c):
            t0 = pl.multiple_of(i * CONV_TILE, CONV_TILE)
            win = dpad_ref[pl.ds(t0, CONV_WIN), :]
            du0 = _shifted_sum(win, [(30 - j, w_ref[j:j + 1, :]) for j in range(CONV_WIDTH)])
            a = ci_ref[pl.ds(t0, CONV_TILE), 0:CONV_DIM]
            sb = _sigmoid(ci_ref[pl.ds(t0, CONV_TILE), CONV_DIM:2 * CONV_DIM])
            dci_ref[pl.ds(t0, CONV_TILE), 0:CONV_DIM] = (du0 * sb).astype(MM)
            dci_ref[pl.ds(t0, CONV_TILE), CONV_DIM:2 * CONV_DIM] = (du0 * a * (sb * (1.0 - sb))).astype(MM)
            return c

        lax.fori_loop(0, SEQ // CONV_TILE, pass2, 0)

        for j in range(CONV_WIDTH):
            dw_ref[j:j + 1, :] = jnp.sum(dwacc_ref[j], axis=0, keepdims=True)
        dw_ref[CONV_WIDTH:32, :] = jnp.zeros((32 - CONV_WIDTH, CONV_DIM), F32)
        dg_ref[...] = jnp.sum(vacc_ref[0], axis=0, keepdims=True)
        db_ref[...] = jnp.sum(vacc_ref[1], axis=0, keepdims=True)
        dbdw_ref[...] = jnp.sum(vacc_ref[2], axis=0, keepdims=True)

    vec = jax.ShapeDtypeStruct((1, CONV_DIM), F32)
    return pl.pallas_call(
        body, name="conv_bwd",
        out_shape=(jax.ShapeDtypeStruct((SEQ, 2 * CONV_DIM), MM), jax.ShapeDtypeStruct((32, CONV_DIM), F32),
                   vec, vec, vec),
        in_specs=[VMEM_SPEC] * 6, out_specs=[VMEM_SPEC] * 5,
        scratch_shapes=[pltpu.VMEM((SEQ + 32, CONV_DIM), F32), pltpu.VMEM((SEQ + 32, CONV_DIM), F32),
                        pltpu.VMEM((CONV_WIDTH, 8, CONV_DIM), F32), pltpu.VMEM((3, 8, CONV_DIM), F32)],
        compiler_params=_params(),
    )(du3, u1, ci, w_dw, ln_g, ln_b)


def attn_bwd(q, k, v, datt, rc, scatter=()):
    ns = len(scatter)

    def body(*refs):
        q_ref, k_ref, v_ref, do_ref, rc_ref = refs[:5]
        dqkv_ref = refs[5 + ns]
        (dqa_ref, dka_ref, dva_ref, pc_ref, z_ref, sig1_ref, sig2_ref, g_ref, spb_ref, gb_ref, ab_ref,
         dzb_ref) = refs[6 + 2 * ns:18 + 2 * ns]
        if ns:
            sc = _Scatter(refs[5:5 + ns], refs[6 + ns:6 + 2 * ns], refs[18 + 2 * ns:])
            sc.start()
        lane, row, head0 = _head_masks()
        for ref in (dqa_ref, dka_ref, dva_ref, pc_ref):
            ref[...] = jnp.zeros_like(ref)
        z_ref[...] = jnp.full(z_ref.shape, NO_SCORE, F32)
        for ref in (sig1_ref, sig2_ref, spb_ref, ab_ref, g_ref, gb_ref, dzb_ref):
            ref[...] = jnp.zeros_like(ref)
        w_suffix = _cumsum_weights(suffix=True, with_total=False)
        w_prefix = _cumsum_weights(suffix=False, with_total=True)

        def step(pairs):
            (ia, ja), (ib, jb), (ic, jc), (id_, jd) = pairs
            qa, ka, qb_, kb_, qc, kc, qd, kd = (pl.multiple_of(jnp.maximum(b, 0) * TQ, TQ)
                                                for b in (ia, ja, ib, jb, ic, jc, id_, jd))
            bias_a = _score_bias(lane, row, ia, ja)
            rc_rows = rc_ref[pl.ds(qb_, TQ), :]
            first_c = jc == 0
            for p in range(N_PAIRS):
                cols = slice(128 * p, 128 * (p + 1))
                q_a = q_ref[pl.ds(qa, TQ), cols]
                k_a = k_ref[pl.ds(ka, TQ), cols]
                do_b = do_ref[pl.ds(qb_, TQ), cols]
                v_b = v_ref[pl.ds(kb_, TQ), cols]
                do_c = do_ref[pl.ds(qc, TQ), cols]
                q_d = q_ref[pl.ds(qd, TQ), cols]
                k_d = k_ref[pl.ds(kd, TQ), cols]
                for h in range(2):
                    hh = 2 * p + h
                    dzb = dzb_ref[hh]
                    dqa_ref[pl.ds(qd, TQ), cols] += _dot(dzb, _pick_head(k_d, head0, h))
                    dka_ref[pl.ds(kd, TQ), cols] += _dot_tn(dzb, _pick_head(q_d, head0, h))
                    r = _dot(gb_ref[hh], w_prefix)
                    p_in = jnp.where(first_c, 0.0, pc_ref[hh])
                    dzb_ref[hh] = (g_ref[hh] - sig2_ref[hh] * (r[:, :128] + p_in)).astype(MM)
                    pc_ref[hh] = p_in + r[:, 128:]
                    dva_ref[pl.ds(kc, TQ), cols] += _dot_tn(ab_ref[hh], _pick_head(do_c, head0, h))
                    r_in = jnp.sum(jnp.where(lane == 16 * hh + jb, rc_rows, 0.0), axis=1, keepdims=True)
                    a = jnp.exp(z_ref[hh] - (_dot(spb_ref[hh], w_suffix) + r_in))
                    g = _dot_nt(_pick_head(do_b, head0, h), v_b) * a
                    ab_ref[hh] = a.astype(MM)
                    g_ref[hh] = g
                    gb_ref[hh] = g.astype(MM)
                    sig2_ref[hh] = sig1_ref[hh]
                    z = _dot_nt(_pick_head(q_a, head0, h), k_a) + bias_a
                    sp = _softplus(z)
                    sig1_ref[hh] = jnp.exp(z - sp)
                    z_ref[hh] = z
                    spb_ref[hh] = sp.astype(MM)

        _block_pipeline(4, False, step)
        dqkv_ref[:, 0:ATT_DIM] = (dqa_ref[...] * ATT_SCALE).astype(MM)
        dqkv_ref[:, ATT_DIM:2 * ATT_DIM] = dka_ref[...].astype(MM)
        dqkv_ref[:, 2 * ATT_DIM:3 * ATT_DIM] = dva_ref[...].astype(MM)
        if ns:
            sc.finish()

    out = jax.ShapeDtypeStruct((SEQ, 3 * ATT_DIM), MM)
    return pl.pallas_call(
        body, name="attn_bwd", out_shape=[out] + _Scatter.out_shapes(scatter),
        in_specs=[VMEM_SPEC] * 5 + [ANY] * ns, out_specs=[VMEM_SPEC] + [ANY] * ns,
        scratch_shapes=[pltpu.VMEM((SEQ, ATT_DIM), F32)] * 3 + [pltpu.VMEM((8, TQ, 128), F32)] * 5
                       + [pltpu.VMEM((8, TQ, 128), MM)] * 4 + (_Scatter.scratch(ns) if ns else []),
        compiler_params=_params(),
    )(q, k, v, datt, rc, *scatter)


DPROJ_PIECES = ((0, 1024), (1024, 2560), (2560, 4608))


def _dproj_segments(j):
    g0, g1 = j * IN_SHARD, (j + 1) * IN_SHARD
    segs = []
    for p, (s, e) in enumerate(DPROJ_PIECES):
        lo, hi = max(s, g0), min(e, g1)
        if lo < hi:
            segs.append((p, lo - s, lo - g0, hi - lo))
    return segs


def in_proj_bwd(pieces, w_in_g, x, dx2, g1, scatter=()):
    ns = len(scatter)
    nt = SEQ // TM

    def body(*refs):
        p_refs = refs[:3]
        w_ref, x_ref, dx2_ref, g_ref = refs[3:7]
        dx_ref, dg_ref = refs[7 + ns:9 + ns]
        if ns:
            sc = _Scatter(refs[7:7 + ns], refs[9 + ns:9 + 2 * ns], refs[9 + 2 * ns:])
            pl.when(pl.program_id(0) == 0)(sc.start)
        dh = None
        for j in range(N_CHIPS):
            for p, lo, off, width in _dproj_segments(j):
                t = _dot_nt(p_refs[p][:, lo:lo + width], w_ref[j, :, off:off + width])
                dh = t if dh is None else dh + t
        n1, r1 = _rms(x_ref[...])
        dx_ref[...] = dx2_ref[...] + _rms_bwd(dh * g_ref[...], n1, r1)
        _acc_rows(dg_ref, jnp.sum(dh * n1, axis=0, keepdims=True), pl.program_id(0) == 0)
        if ns:
            pl.when(pl.program_id(0) == nt - 1)(sc.finish)

    vec = _full_spec((1, D_MODEL))
    return pl.pallas_call(
        body, name="in_proj_bwd", grid=(nt,),
        out_shape=[jax.ShapeDtypeStruct((SEQ, D_MODEL), F32), jax.ShapeDtypeStruct((1, D_MODEL), F32)]
                  + _Scatter.out_shapes(scatter),
        in_specs=[_row_tile_spec(p.shape[1]) for p in pieces]
                 + [_weight_spec(w_in_g.shape), _row_tile_spec(D_MODEL), _row_tile_spec(D_MODEL), vec] + [ANY] * ns,
        out_specs=[_row_tile_spec(D_MODEL), vec] + [ANY] * ns,
        scratch_shapes=_Scatter.scratch(ns) if ns else [],
        compiler_params=_params(("arbitrary",)),
    )(*pieces, w_in_g, x, dx2, g1, *scatter)


def weight_grad_in(h1, pieces):
    kh = D_MODEL // 2

    def body(a_ref, p0_ref, p1_ref, p2_ref, o_ref):
        p_refs = (p0_ref, p1_ref, p2_ref)
        a = a_ref[...]
        for j in range(N_CHIPS):
            @pl.when(pl.program_id(1) == j)
            def _():
                for p, lo, off, width in _dproj_segments(j):
                    o_ref[0, 0, :, off:off + width] = _dot_tn(a, p_refs[p][:, lo:lo + width]).astype(MM)

    return pl.pallas_call(
        body, name="dw_in", grid=(2, N_CHIPS), out_shape=jax.ShapeDtypeStruct((N_CHIPS, 2, kh, IN_SHARD), MM),
        in_specs=[pl.BlockSpec((SEQ, kh), lambda h, j: (0, h))] + [_weight_spec(p.shape) for p in pieces],
        out_specs=pl.BlockSpec((1, 1, kh, IN_SHARD), lambda h, j: (j, h, 0, 0)),
        compiler_params=_params(("arbitrary", "arbitrary")),
    )(h1, *pieces)


def weight_grad(a, b, name, col_sharded, tk=None):
    kin, n = a.shape[1], b.shape[1]

    def body(a_ref, b_ref, o_ref):
        if col_sharded:
            o_ref[0, 0] = _dot_tn(a_ref[...], b_ref[...]).astype(MM)
        else:
            o_ref[...] = _dot_tn(a_ref[...], b_ref[...]).astype(MM)

    if col_sharded:
        kh, ns = kin // 2, n // N_CHIPS
        out = jax.ShapeDtypeStruct((N_CHIPS, 2, kh, ns), MM)
        grid = (2, N_CHIPS)
        in_specs = [pl.BlockSpec((SEQ, kh), lambda h, j: (0, h)), pl.BlockSpec((SEQ, ns), lambda h, j: (0, j))]
        out_spec = pl.BlockSpec((1, 1, kh, ns), lambda h, j: (j, h, 0, 0))
        sem = ("arbitrary", "arbitrary")
    else:
        out = jax.ShapeDtypeStruct((kin, n), MM)
        grid = (kin // tk,)
        in_specs = [pl.BlockSpec((SEQ, tk), lambda r: (0, r)), pl.BlockSpec((SEQ, n), lambda r: (0, 0))]
        out_spec = pl.BlockSpec((tk, n), lambda r: (r, 0))
        sem = ("arbitrary",)
    res = pl.pallas_call(
        body, name=name, grid=grid, out_shape=out, in_specs=in_specs, out_specs=out_spec,
        compiler_params=_params(sem),
    )(a, b)
    if not col_sharded:
        res = res.reshape(N_CHIPS, 2, kin // (2 * N_CHIPS), n)
    return res


def _place():
    x, y, c = lax.axis_index("x"), lax.axis_index("y"), lax.axis_index("c")
    chips = [(1 - x, y), (x, 1 - y), (1 - x, 1 - y)]
    return x, y, c, chips


def _rcopy(src, dst, send_sem, recv_sem, dev):
    return pltpu.make_async_remote_copy(src_ref=src, dst_ref=dst, send_sem=send_sem, recv_sem=recv_sem,
                                        device_id=dev, device_id_type=MESH)


class _Gather:
    def __init__(self, shapes, w, o, scratch):
        self.n, self.shapes, self.w, self.o = len(w), shapes, w, o
        self.send, self.recv, self.fsend, self.frecv, self.loc_in, self.loc_out = scratch[:6]
        self.raw, self.stage = scratch[6:6 + self.n], scratch[6 + self.n:]
        self.x, self.y, self.c, self.chips = _place()
        self.me = 2 * self.x + self.y
        self.sib = (self.x, self.y, 1 - self.c)
        self.pairs = [(j, t) for j in range(3) for t in range(self.n)]

    @staticmethod
    def scratch(shards):
        n = len(shards)
        sems = pltpu.SemaphoreType.DMA
        return ([sems((3 * n,)), sems((3 * n,)), sems((3 * n,)), sems((3 * n,)), sems((n,)), sems((n,))]
                + [pltpu.VMEM(s.shape, s.dtype) for s in shards] + [pltpu.VMEM(s.shape, MM) for s in shards])

    @staticmethod
    def out_shapes(shards):
        return [jax.ShapeDtypeStruct((N_CHIPS,) + s.shape, MM) for s in shards]

    def _half(self, t, k, cc):
        rh = self.shapes[t][0] // 2
        return self.o[t].at[k, pl.ds(cc * rh, rh), :]

    def _chip(self, j):
        cx, cy = self.chips[j]
        return 2 * cx + cy, (cx, cy, self.c)

    def local_in(self, t):
        return pltpu.make_async_copy(self.w[t], self.raw[t], self.loc_in.at[t])

    def local_out(self, t):
        return pltpu.make_async_copy(self.stage[t], self.o[t].at[self.me], self.loc_out.at[t])

    def first(self, j, t):
        rh = self.shapes[t][0] // 2
        i = j * self.n + t
        return _rcopy(self.stage[t].at[pl.ds(self.c * rh, rh), :], self._half(t, self.me, self.c),
                      self.send.at[i], self.recv.at[i], self._chip(j)[1])

    def arrived(self, j, t):
        k, dev = self._chip(j)
        i = j * self.n + t
        blk = self._half(t, k, self.c)
        return _rcopy(blk, blk, self.send.at[i], self.recv.at[i], dev)

    def passed(self, j, t, cc):
        i = j * self.n + t
        blk = self._half(t, self._chip(j)[0], cc)
        return _rcopy(blk, blk, self.fsend.at[i], self.frecv.at[i], self.sib)

    def start(self):
        for t in range(self.n):
            self.local_in(t).start()
        for t in range(self.n):
            self.local_in(t).wait()
            self.stage[t][...] = self.raw[t][...].astype(MM)
            self.local_out(t).start()
        for j, t in self.pairs:
            self.first(j, t).start()

    def forward(self):
        for j, t in self.pairs:
            self.arrived(j, t).wait_recv()
            self.passed(j, t, self.c).start()

    def finish(self):
        for j, t in self.pairs:
            self.passed(j, t, 1 - self.c).wait_recv()
        for j, t in self.pairs:
            self.first(j, t).wait_send()
            self.passed(j, t, self.c).wait_send()
        for t in range(self.n):
            self.local_out(t).wait()


def all_gather_weights(shards, small):
    n = len(shards)
    shapes = [s.shape for s in shards]

    def body(*refs):
        w = refs[:n]
        sm = refs[n]
        o = refs[n + 1:2 * n + 1]
        osm = refs[2 * n + 1]
        ssend, srecv, sloc = refs[2 * n + 2:2 * n + 5]
        g = _Gather(shapes, w, o, refs[2 * n + 5:])
        own = pltpu.make_async_copy(sm, osm.at[g.me], sloc)
        own.start()
        g.start()
        small_cps = [_rcopy(sm, osm.at[g.me], ssend.at[j], srecv.at[j], g._chip(j)[1]) for j in range(3)]
        for cp in small_cps:
            cp.start()
        g.forward()
        g.finish()
        for j in range(3):
            k, dev = g._chip(j)
            _rcopy(sm, osm.at[k], ssend.at[j], srecv.at[j], dev).wait_recv()
            small_cps[j].wait_send()
        own.wait()

    out_shape = _Gather.out_shapes(shards)
    out_shape.append(jax.ShapeDtypeStruct((N_CHIPS,) + small.shape, small.dtype))
    sems = pltpu.SemaphoreType.DMA
    return pl.pallas_call(
        body, name="all_gather_weights", out_shape=out_shape,
        in_specs=[ANY] * (n + 1), out_specs=[ANY] * (n + 1),
        scratch_shapes=[sems((3,)), sems((3,)), sems] + _Gather.scratch(shards),
        compiler_params=_params(),
    )(*shards, small)


def sibling_exchange(grads, name):
    n = len(grads)

    def body(*refs):
        g = refs[:n]
        o = refs[n:2 * n]
        send, recv = refs[2 * n:]
        x, y, c, _ = _place()
        cps = [_rcopy(g[t].at[:, 1 - c], o[t], send.at[t], recv.at[t], (x, y, 1 - c)) for t in range(n)]
        for cp in cps:
            cp.start()
        for cp in cps:
            cp.wait()

    sems = pltpu.SemaphoreType.DMA
    return pl.pallas_call(
        body, name=name,
        out_shape=[jax.ShapeDtypeStruct((a.shape[0],) + a.shape[2:], a.dtype) for a in grads],
        in_specs=[ANY] * n, out_specs=[ANY] * n, scratch_shapes=[sems((n,)), sems((n,))],
    )(*grads)


class _Scatter:
    def __init__(self, p, o, sems):
        self.n, self.p, self.o = len(p), p, o
        self.send, self.recv = sems
        _, _, self.c, self.chips = _place()

    @staticmethod
    def scratch(n):
        sems = pltpu.SemaphoreType.DMA
        return [sems((3 * n,)), sems((3 * n,))]

    @staticmethod
    def out_shapes(parts):
        return [jax.ShapeDtypeStruct((3,) + a.shape[1:], a.dtype) for a in parts]

    def copies(self):
        cps = []
        for j, (cx, cy) in enumerate(self.chips):
            for t in range(self.n):
                i = j * self.n + t
                cps.append(_rcopy(self.p[t].at[2 * cx + cy], self.o[t].at[j], self.send.at[i], self.recv.at[i],
                                  (cx, cy, self.c)))
        return cps

    def start(self):
        for cp in self.copies():
            cp.start()

    def finish(self):
        for cp in self.copies():
            cp.wait()


def sibling_swap(halves, name):
    n = len(halves)

    def body(*refs):
        h = refs[:n]
        o = refs[n:2 * n]
        send, recv = refs[2 * n:]
        x, y, c, _ = _place()
        cps = [_rcopy(h[t], o[t], send.at[t], recv.at[t], (x, y, 1 - c)) for t in range(n)]
        for cp in cps:
            cp.start()
        for cp in cps:
            cp.wait()

    sems = pltpu.SemaphoreType.DMA
    return pl.pallas_call(
        body, name=name, out_shape=[jax.ShapeDtypeStruct(a.shape, a.dtype) for a in halves],
        in_specs=[ANY] * n, out_specs=[ANY] * n, scratch_shapes=[sems((n,)), sems((n,))],
    )(*halves)


def small_all_reduce(ddw, v512, v1024, loss_parts):
    rows, width = PACK_ROWS, 512
    n512, n1024 = len(VEC512), len(VEC1024)

    def body(*refs):
        ddw_ref = refs[0]
        a_refs = refs[1:1 + n512]
        b_refs = refs[1 + n512:1 + n512 + n1024]
        lp_ref, o_ref, p_ref, gath_ref, send, recv = refs[1 + n512 + n1024:]
        p_ref[...] = jnp.zeros_like(p_ref)
        p_ref[0:32, :] = ddw_ref[...]
        p_ref[LOSS_ROW:LOSS_ROW + 1, 0:128] = jnp.sum(lp_ref[...], axis=0, keepdims=True) * 0.125
        for i, r in enumerate(a_refs):
            p_ref[32 + i:33 + i, :] = r[...]
        for i, r in enumerate(b_refs):
            base = 32 + n512 + 2 * i
            p_ref[base:base + 1, :] = r[:, 0:512]
            p_ref[base + 1:base + 2, :] = r[:, 512:1024]
        x, y, c, _ = _place()
        me = 4 * x + 2 * y + c
        gath_ref[me] = p_ref[...]
        cps = []
        for k in range(1, 8):
            dx, dy, dc = (k >> 2) & 1, (k >> 1) & 1, k & 1
            px = 1 - x if dx else x
            py = 1 - y if dy else y
            pc = 1 - c if dc else c
            cps.append(_rcopy(p_ref, gath_ref.at[me], send.at[k - 1], recv.at[k - 1], (px, py, pc)))
        for cp in cps:
            cp.start()
        for k in range(1, 8):
            dx, dy, dc = (k >> 2) & 1, (k >> 1) & 1, k & 1
            px = 1 - x if dx else x
            py = 1 - y if dy else y
            pc = 1 - c if dc else c
            _rcopy(p_ref, gath_ref.at[4 * px + 2 * py + pc], send.at[k - 1], recv.at[k - 1], (px, py, pc)).wait_recv()
        for cp in cps:
            cp.wait_send()
        total = gath_ref[0]
        for d in range(1, 8):
            total = total + gath_ref[d]
        o_ref[...] = total

    sems = pltpu.SemaphoreType.DMA
    n_in = 2 + n512 + n1024
    return pl.pallas_call(
        body, name="small_all_reduce", out_shape=jax.ShapeDtypeStruct((rows, width), F32),
        in_specs=[VMEM_SPEC] * n_in, out_specs=VMEM_SPEC,
        scratch_shapes=[pltpu.VMEM((rows, width), F32), pltpu.VMEM((8, rows, width), F32), sems((7,)), sems((7,))],
    )(ddw, *[v512[n] for n in VEC512], *[v1024[n] for n in VEC1024], loss_parts)


def _row_block(r):
    for tr in (512, 352, 256, 128):
        if r % tr == 0:
            return tr
    return r


def add_halves(g, recv, name):
    _, _, r, w = g.shape
    tr = _row_block(r)

    def body(g0_ref, g1_ref, r_ref, ob_ref, own_ref):
        k = pl.program_id(1)
        c = lax.axis_index("c")
        me = 2 * lax.axis_index("x") + lax.axis_index("y")
        t = jnp.where(c == 0, g0_ref[0, 0], g1_ref[0, 0]).astype(F32) + r_ref[0].astype(F32)
        ob_ref[0] = t.astype(MM)
        mine = jnp.where(k == me, t, 0.0)

        @pl.when(k == 0)
        def _():
            own_ref[...] = mine

        @pl.when(k != 0)
        def _():
            own_ref[...] += mine

    return pl.pallas_call(
        body, name=name, grid=(r // tr, N_CHIPS),
        in_specs=[pl.BlockSpec((1, 1, tr, w), lambda i, k: (k, 0, i, 0)),
                  pl.BlockSpec((1, 1, tr, w), lambda i, k: (k, 1, i, 0)),
                  pl.BlockSpec((1, tr, w), lambda i, k: (k, i, 0))],
        out_specs=[pl.BlockSpec((1, tr, w), lambda i, k: (k, i, 0)),
                   pl.BlockSpec((tr, w), lambda i, k: (i, 0))],
        out_shape=(jax.ShapeDtypeStruct((N_CHIPS, r, w), MM), jax.ShapeDtypeStruct((r, w), F32)),
        compiler_params=_params(("arbitrary", "arbitrary")),
    )(g, g, recv)


def sum_parts(own, rin, name):
    _, r, w = rin.shape
    tr = _row_block(r)

    def body(o_ref, r_ref, out_ref):
        out_ref[...] = ((o_ref[...] + r_ref[0].astype(F32)) + r_ref[1].astype(F32)) + r_ref[2].astype(F32)

    return pl.pallas_call(
        body, name=name, grid=(r // tr,), out_shape=jax.ShapeDtypeStruct((r, w), F32),
        in_specs=[pl.BlockSpec((tr, w), lambda i: (i, 0)), pl.BlockSpec((3, tr, w), lambda i: (0, i, 0))],
        out_specs=pl.BlockSpec((tr, w), lambda i: (i, 0)),
        compiler_params=_params(("arbitrary",)),
    )(own, rin)


def _adamw_math(w, g, m, v):
    mn = ADAM_B1 * m + (1.0 - ADAM_B1) * g
    vn = ADAM_B2 * v + (1.0 - ADAM_B2) * (g * g)
    m_hat = mn / (1.0 - ADAM_B1 ** ADAM_STEP)
    v_hat = vn / (1.0 - ADAM_B2 ** ADAM_STEP)
    return -ADAM_LR * (m_hat / (jnp.sqrt(v_hat) + ADAM_EPS) + ADAM_WD * w), mn, vn


def adamw(w, mine, other, m, v, name):
    r, c = w.shape
    rh = r // 2
    tr = _row_block(rh)
    if c >= 1024 and tr % 512 == 0:
        tr = 256
    nb = rh // tr

    def body(w_ref, a_ref, b_ref, m_ref, v_ref, go_ref, d_ref, mo_ref, vo_ref):
        gv = jnp.where(lax.axis_index("c") == pl.program_id(0), a_ref[...], b_ref[...])
        go_ref[...] = gv
        d_ref[...], mo_ref[...], vo_ref[...] = _adamw_math(w_ref[...], gv, m_ref[...], v_ref[...])

    spec = pl.BlockSpec((tr, c), lambda h, i: (h * nb + i, 0))
    half = pl.BlockSpec((tr, c), lambda h, i: (i, 0))
    out = jax.ShapeDtypeStruct((r, c), F32)
    return pl.pallas_call(
        body, name=name, grid=(2, nb), out_shape=(out, out, out, out),
        in_specs=[spec, half, half, spec, spec], out_specs=[spec] * 4,
        compiler_params=_params(("arbitrary", "arbitrary")),
    )(w, mine, other, m, v)


def adamw_small(gsum, params):
    names = list(params)
    flat = [a for n in names for a in params[n]]

    def body(*refs):
        g_ref = refs[0]
        ins = refs[1:1 + 3 * len(names)]
        outs = refs[1 + 3 * len(names):]
        me = 2 * lax.axis_index("x") + lax.axis_index("y")
        for i, n in enumerate(names):
            w_ref, m_ref, v_ref = ins[3 * i:3 * i + 3]
            go_ref, d_ref, mo_ref, vo_ref = outs[4 * i:4 * i + 4]
            if n == "conv_dw_w":
                gv = jnp.zeros((CONV_WIDTH, 128), F32)
                for k in range(N_CHIPS):
                    gv = gv + jnp.where(me == k, g_ref[0:CONV_WIDTH, 128 * k:128 * (k + 1)], 0.0)
            elif n in VEC512:
                r0 = 32 + VEC512.index(n)
                gv = g_ref[r0:r0 + 1, :]
            else:
                r0 = 32 + len(VEC512) + 2 * VEC1024.index(n)
                gv = jnp.concatenate([g_ref[r0:r0 + 1, :], g_ref[r0 + 1:r0 + 2, :]], axis=1)
            go_ref[...] = gv
            d_ref[...], mo_ref[...], vo_ref[...] = _adamw_math(w_ref[...], gv, m_ref[...], v_ref[...])

    out_shape = [jax.ShapeDtypeStruct(params[n][0].shape, F32) for n in names for _ in range(4)]
    res = pl.pallas_call(
        body, name="adamw_small", out_shape=out_shape,
        in_specs=[VMEM_SPEC] * (1 + len(flat)), out_specs=[VMEM_SPEC] * len(out_shape),
        compiler_params=_params(),
    )(gsum, *flat)
    return {n: res[4 * i:4 * i + 4] for i, n in enumerate(names)}


REST = ("w_ffn_up", "w_ffn_down", "w_out", "w_conv_branch", "w_att_branch")
VEC512 = ("conv_dw_b", "conv_ln_g", "conv_ln_b")
VEC1024 = ("norm_mix_pre", "b_conv_branch", "norm_mix_post", "norm_ffn_pre", "norm_ffn_post")
PACK_ROWS = 48
LOSS_ROW = 47


def kernel(x, norm_mix_pre, w_in, conv_dw_w, conv_dw_b, conv_ln_g, conv_ln_b, w_conv_branch, b_conv_branch, w_att_branch, w_out, norm_mix_post, norm_ffn_pre, w_ffn_up, w_ffn_down, norm_ffn_post, loss_target, m_norm_mix_pre, m_w_in, m_conv_dw_w, m_conv_dw_b, m_conv_ln_g, m_conv_ln_b, m_w_conv_branch, m_b_conv_branch, m_w_att_branch, m_w_out, m_norm_mix_post, m_norm_ffn_pre, m_w_ffn_up, m_w_ffn_down, m_norm_ffn_post, v_norm_mix_pre, v_w_in, v_conv_dw_w, v_conv_dw_b, v_conv_ln_g, v_conv_ln_b, v_w_conv_branch, v_b_conv_branch, v_w_att_branch, v_w_out, v_norm_mix_post, v_norm_ffn_pre, v_w_ffn_up, v_w_ffn_down, v_norm_ffn_post):
    weights = dict(norm_mix_pre=norm_mix_pre, w_in=w_in, conv_dw_w=conv_dw_w, conv_dw_b=conv_dw_b, conv_ln_g=conv_ln_g, conv_ln_b=conv_ln_b, w_conv_branch=w_conv_branch, b_conv_branch=b_conv_branch, w_att_branch=w_att_branch, w_out=w_out, norm_mix_post=norm_mix_post, norm_ffn_pre=norm_ffn_pre, w_ffn_up=w_ffn_up, w_ffn_down=w_ffn_down, norm_ffn_post=norm_ffn_post)
    mom = dict(norm_mix_pre=m_norm_mix_pre, w_in=m_w_in, conv_dw_w=m_conv_dw_w, conv_dw_b=m_conv_dw_b, conv_ln_g=m_conv_ln_g, conv_ln_b=m_conv_ln_b, w_conv_branch=m_w_conv_branch, b_conv_branch=m_b_conv_branch, w_att_branch=m_w_att_branch, w_out=m_w_out, norm_mix_post=m_norm_mix_post, norm_ffn_pre=m_norm_ffn_pre, w_ffn_up=m_w_ffn_up, w_ffn_down=m_w_ffn_down, norm_ffn_post=m_norm_ffn_post)
    var = dict(norm_mix_pre=v_norm_mix_pre, w_in=v_w_in, conv_dw_w=v_conv_dw_w, conv_dw_b=v_conv_dw_b, conv_ln_g=v_conv_ln_g, conv_ln_b=v_conv_ln_b, w_conv_branch=v_w_conv_branch, b_conv_branch=v_b_conv_branch, w_att_branch=v_w_att_branch, w_out=v_w_out, norm_mix_post=v_norm_mix_post, norm_ffn_pre=v_norm_ffn_pre, w_ffn_up=v_w_ffn_up, w_ffn_down=v_w_ffn_down, norm_ffn_post=v_norm_ffn_post)
    order = list(weights)
    grads, deltas, new_m, new_v = {}, {}, {}, {}
    xs = x.reshape(SEQ, D_MODEL)
    tgt = loss_target.reshape(SEQ, D_MODEL)
    row = lambda a: a.reshape(1, -1)
    g1, g2, g3, g4 = (row(weights[n]) for n in ("norm_mix_pre", "norm_mix_post", "norm_ffn_pre", "norm_ffn_post"))
    ln_g, ln_b = row(conv_ln_g), row(conv_ln_b)

    def reduce_prepare(names, partial, tag):
        from_sib = sibling_exchange([partial[n] for n in names], "sibling_exchange_" + tag)
        return [add_halves(partial[n], r, "add_" + n) for n, r in zip(names, from_sib)]

    def reduce_finish(names, summed, from_chips, tag):
        halves = [sum_parts(s[1], r, "sum_" + n) for n, s, r in zip(names, summed, from_chips)]
        for n, a, b in zip(names, halves, sibling_swap(halves, "sibling_swap_" + tag)):
            grads[n], deltas[n], new_m[n], new_v[n] = adamw(weights[n], a, b, mom[n], var[n], "adamw_" + n)

    w_in_g, dw_g = all_gather_weights([w_in], conv_dw_w)
    w_dw_full = jnp.concatenate([dw_g[k] for k in range(N_CHIPS)], axis=1)
    h1, ci, q, k, v, gc, ga = in_proj_fwd(xs, g1, w_in_g)
    u1, u3, w_out_g, w_cb_g, w_ab_g = conv_fwd(ci, w_dw_full, row(conv_dw_b), ln_g, ln_b,
                                               [w_out, w_conv_branch, w_att_branch])
    att, rc, w_up_g = attn_fwd(q, k, v, [w_ffn_up])
    w_out_g = w_out_g.reshape(D_MODEL, D_MODEL)
    co, ao, merged, mix, x2, h2 = mix_fwd(u3, att, gc, ga, xs, w_cb_g, row(b_conv_branch), w_ab_g, w_out_g, g2, g3)
    gate, up, act, w_down_g = ffn_up_fwd(h2, w_up_g, [w_ffn_down])
    w_down_g = w_down_g.reshape(D_FF, D_MODEL)
    dff, dy, loss_parts, dg4 = ffn_down_loss(act, w_down_g, x2, tgt, g4)

    partial = {}
    dgu = ffn_act_bwd(dff, w_down_g, gate, up)
    partial["w_ffn_down"] = weight_grad(act, dff, "dw_ffn_down", False, tk=UP_SHARD)
    dx2, dmix, dg3, dg2 = ffn_in_bwd(dgu, w_up_g, x2, mix, dy, g3, g2)
    partial["w_ffn_up"] = weight_grad(h2, dgu, "dw_ffn_up", True)
    dco, dao, dg, du3, datt, dbcb = merge_bwd(dmix, w_out_g, gc, ga, co, ao, w_cb_g, w_ab_g)
    partial["w_out"] = weight_grad(merged, dmix, "dw_out", False, tk=512)
    partial["w_conv_branch"] = weight_grad(u3, dco, "dw_conv_branch", True)
    partial["w_att_branch"] = weight_grad(att, dao, "dw_att_branch", True)
    summed = reduce_prepare(REST, partial, "rest")
    dci, ddw, dbdw, dlng, dlnb = conv_bwd(du3, u1, ci, w_dw_full, ln_g, ln_b)
    dqkv, *from_chips = attn_bwd(q, k, v, datt, rc, [s[0] for s in summed])
    reduce_finish(REST, summed, from_chips, "rest")
    dproj = (dci, dqkv, dg)
    partial["w_in"] = weight_grad_in(h1, dproj)
    summed = reduce_prepare(("w_in",), partial, "w_in")
    grad_x, dg1, *from_chips = in_proj_bwd(dproj, w_in_g, xs, dx2, g1, [summed[0][0]])
    reduce_finish(("w_in",), summed, from_chips, "w_in")

    v512 = dict(conv_dw_b=dbdw, conv_ln_g=dlng, conv_ln_b=dlnb)
    v1024 = dict(norm_mix_pre=dg1, b_conv_branch=dbcb, norm_mix_post=dg2, norm_ffn_pre=dg3, norm_ffn_post=dg4)
    gsum = small_all_reduce(ddw, v512, v1024, loss_parts)
    loss = gsum[LOSS_ROW, 0]
    as_rows = lambda n, a: a if n == "conv_dw_w" else a.reshape(1, -1)
    small_names = ("conv_dw_w",) + VEC512 + VEC1024
    small = adamw_small(gsum, {n: tuple(as_rows(n, d[n]) for d in (weights, mom, var)) for n in small_names})
    for n in small_names:
        grads[n], deltas[n], new_m[n], new_v[n] = (a.reshape(weights[n].shape) for a in small[n])

    return (loss, grad_x.reshape(1, SEQ, D_MODEL), *[grads[n] for n in order], *[deltas[n] for n in order],
            *[new_m[n] for n in order], *[new_v[n] for n in order])
```

```python
import jax
import jax.numpy as jnp
from jax import lax
from jax.experimental import pallas as pl
from jax.experimental.pallas import tpu as pltpu

F32 = jnp.float32
MM = jnp.bfloat16

SEQ = 2048
D_MODEL = 1024
CONV_DIM = 512
ATT_DIM = 512
CONV_WIDTH = 31
D_FF = 2816
IN_COLS = 2 * CONV_DIM + 3 * ATT_DIM + 2 * D_MODEL
N_CHIPS = 4
IN_SHARD = IN_COLS // N_CHIPS
UP_SHARD = 2 * D_FF // N_CHIPS
BR_SHARD = D_MODEL // N_CHIPS
EPS = 1e-6
ATT_SCALE = 0.125

TM = 256
GLU_ROWS = 256
TQ = 128
CONV_TILE = 64
CONV_WIN = CONV_TILE + 32
VMEM_LIMIT = 56 * 1024 * 1024

ADAM_LR = 0.001
ADAM_B1 = 0.9
ADAM_B2 = 0.999
ADAM_EPS = 1e-08
ADAM_WD = 0.01
ADAM_STEP = 10

MESH = pl.DeviceIdType.MESH
ANY = pl.BlockSpec(memory_space=pl.ANY)
VMEM_SPEC = pl.BlockSpec(memory_space=pltpu.VMEM)

NT_DIMS = (((1,), (1,)), ((), ()))
TN_DIMS = (((0,), (0,)), ((), ()))

IN_PIECES = (("ci", 0, 1024), ("q", 1024, 1536), ("k", 1536, 2048), ("v", 2048, 2560),
             ("gc", 2560, 3584), ("ga", 3584, 4608))


def _params(sem=None, vmem=VMEM_LIMIT):
    return pltpu.CompilerParams(dimension_semantics=sem, vmem_limit_bytes=vmem)


def _dot(a, b):
    return jnp.dot(a, b, preferred_element_type=F32)


def _dot_nt(a, b):
    return lax.dot_general(a, b, NT_DIMS, preferred_element_type=F32)


def _dot_tn(a, b):
    return lax.dot_general(a, b, TN_DIMS, preferred_element_type=F32)


def _sigmoid(x):
    return 1.0 / (1.0 + jnp.exp(-x))


def _rms(x):
    r = lax.rsqrt(jnp.mean(x * x, axis=-1, keepdims=True) + EPS)
    return x * r, r


def _rms_bwd(dy_g, n, r):
    return r * (dy_g - n * jnp.mean(dy_g * n, axis=-1, keepdims=True))


def _row_tile_spec(width, tm=TM):
    return pl.BlockSpec((tm, width), lambda i: (i, 0))


def _full_spec(shape):
    nd = len(shape)
    return pl.BlockSpec(shape, lambda *_: (0,) * nd)


def _weight_spec(shape):
    nd = len(shape)
    return pl.BlockSpec(shape, lambda *_: (0,) * nd, pipeline_mode=pl.Buffered(1))


def _acc_rows(ref, val, first):
    @pl.when(first)
    def _():
        ref[...] = val

    @pl.when(jnp.logical_not(first))
    def _():
        ref[...] += val


def _gather_behind_grid(ag, n_steps, step=None):
    step = pl.program_id(0) if step is None else step
    pl.when(step == 0)(ag.start)
    pl.when(step == n_steps - 2)(ag.forward)
    return lambda: pl.when(step == n_steps - 1)(ag.finish)


def in_proj_fwd(x, g1, w_in_g, gather=()):
    ng = len(gather)
    nt = SEQ // TM

    def body(*refs):
        x_ref, g_ref, w_ref = refs[:3]
        h_ref, ci_ref, q_ref, k_ref, v_ref, gc_ref, ga_ref = refs[3 + ng:10 + ng]
        if ng:
            done = _gather_behind_grid(_Gather([s.shape for s in gather], refs[3:3 + ng],
                                               refs[10 + ng:10 + 2 * ng], refs[10 + 2 * ng:]), nt)
        n, _ = _rms(x_ref[...])
        h = (n * g_ref[...]).astype(MM)
        h_ref[...] = h
        outs = dict(ci=ci_ref, q=q_ref, k=k_ref, v=v_ref, gc=gc_ref, ga=ga_ref)
        for j in range(N_CHIPS):
            p = _dot(h, w_ref[j])
            g0 = j * IN_SHARD
            for name, s, e in IN_PIECES:
                lo, hi = max(s, g0), min(e, g0 + IN_SHARD)
                if lo < hi:
                    ref = outs[name]
                    part = p[:, lo - g0:hi - g0]
                    if name == "q":
                        part = part * ATT_SCALE
                    ref[:, lo - s:hi - s] = part.astype(ref.dtype)
        if ng:
            done()

    out_shape = [
        jax.ShapeDtypeStruct((SEQ, D_MODEL), MM),
        jax.ShapeDtypeStruct((SEQ, 2 * CONV_DIM), F32),
        jax.ShapeDtypeStruct((SEQ, ATT_DIM), MM),
        jax.ShapeDtypeStruct((SEQ, ATT_DIM), MM),
        jax.ShapeDtypeStruct((SEQ, ATT_DIM), MM),
        jax.ShapeDtypeStruct((SEQ, D_MODEL), F32),
        jax.ShapeDtypeStruct((SEQ, D_MODEL), F32),
    ]
    return pl.pallas_call(
        body, name="in_proj_fwd", grid=(nt,),
        out_shape=out_shape + _Gather.out_shapes(gather),
        in_specs=[_row_tile_spec(D_MODEL), _full_spec((1, D_MODEL)), _weight_spec(w_in_g.shape)] + [ANY] * ng,
        out_specs=[_row_tile_spec(s.shape[1]) for s in out_shape] + [ANY] * ng,
        scratch_shapes=_Gather.scratch(gather) if ng else [],
        compiler_params=_params(("arbitrary",)),
    )(x, g1, w_in_g, *gather)


def _shifted_sum(win, terms):
    by_rot = {}
    for m, coef in terms:
        by_rot.setdefault(m % 8, []).append((m // 8, coef))
    acc = None
    n = win.shape[0]
    for rot in sorted(by_rot):
        shifted = win if rot == 0 else pltpu.roll(win, n - rot, 0)
        for a, coef in by_rot[rot]:
            t = coef * shifted[8 * a:8 * a + CONV_TILE, :]
            acc = t if acc is None else acc + t
    return acc


def _glu_into(ci_ref, upad_ref):
    upad_ref[0:32, :] = jnp.zeros((32, CONV_DIM), F32)

    def step(i, c):
        t0 = pl.multiple_of(i * GLU_ROWS, GLU_ROWS)
        a = ci_ref[pl.ds(t0, GLU_ROWS), 0:CONV_DIM]
        b = ci_ref[pl.ds(t0, GLU_ROWS), CONV_DIM:2 * CONV_DIM]
        upad_ref[pl.ds(t0 + 32, GLU_ROWS), :] = a * _sigmoid(b)
        return c

    lax.fori_loop(0, SEQ // GLU_ROWS, step, 0)


def _layernorm_parts(u1):
    mu = jnp.mean(u1, axis=-1, keepdims=True)
    xc = u1 - mu
    rstd = lax.rsqrt(jnp.mean(xc * xc, axis=-1, keepdims=True) + EPS)
    return xc * rstd, rstd


def conv_fwd(ci, w_dw, b_dw, ln_g, ln_b, gather=()):
    ng = len(gather)
    n_tiles = SEQ // CONV_TILE

    def body(*refs):
        ci_ref, w_ref, b_ref, g_ref, bb_ref = refs[:5]
        u1_ref, u3_ref = refs[5 + ng:7 + ng]
        upad_ref = refs[7 + 2 * ng]
        if ng:
            ag = _Gather([s.shape for s in gather], refs[5:5 + ng], refs[7 + ng:7 + 2 * ng], refs[8 + 2 * ng:])
            ag.start()
        _glu_into(ci_ref, upad_ref)

        def step(i, c):
            if ng:
                pl.when(i == n_tiles - n_tiles // 4)(ag.forward)
            t0 = pl.multiple_of(i * CONV_TILE, CONV_TILE)
            win = upad_ref[pl.ds(t0, CONV_WIN), :]
            u1 = _shifted_sum(win, [(j + 2, w_ref[j:j + 1, :]) for j in range(CONV_WIDTH)]) + b_ref[...]
            u1_ref[pl.ds(t0, CONV_TILE), :] = u1
            xh, _ = _layernorm_parts(u1)
            u2 = xh * g_ref[...] + bb_ref[...]
            u3_ref[pl.ds(t0, CONV_TILE), :] = (u2 * _sigmoid(u2)).astype(MM)
            return c

        lax.fori_loop(0, n_tiles, step, 0)
        if ng:
            ag.finish()

    return pl.pallas_call(
        body, name="conv_fwd",
        out_shape=[jax.ShapeDtypeStruct((SEQ, CONV_DIM), F32), jax.ShapeDtypeStruct((SEQ, CONV_DIM), MM)]
                  + _Gather.out_shapes(gather),
        in_specs=[VMEM_SPEC] * 5 + [ANY] * ng, out_specs=[VMEM_SPEC] * 2 + [ANY] * ng,
        scratch_shapes=[pltpu.VMEM((SEQ + 32, CONV_DIM), F32)] + (_Gather.scratch(gather) if ng else []),
        compiler_params=_params(),
    )(ci, w_dw, b_dw, ln_g, ln_b, *gather)


def _softplus(z):
    return jnp.maximum(z, 0.0) + jnp.log(1.0 + jnp.exp(-jnp.abs(z)))


def _cumsum_weights(suffix, with_total):
    n = 256 if with_total else 128
    r = lax.broadcasted_iota(jnp.int32, (128, n), 0)
    c = lax.broadcasted_iota(jnp.int32, (128, n), 1)
    tri = (r >= c) if suffix else (r <= c)
    return jnp.logical_or(tri, c >= 128).astype(MM)


NO_SCORE = -1e30
N_KB = SEQ // TQ


def _score_bias(lane, row, i, j):
    keep = jnp.logical_and(i >= 0, jnp.logical_or(j < i, lane < row))
    return jnp.where(keep, 0.0, NO_SCORE)


def _block_pipeline(n_stages, descending, step, on_query_block=None):
    n_lag = n_stages - 1
    none = jnp.int32(-1)

    def shift(cur, lag):
        step([cur] + [(lag[2 * s], lag[2 * s + 1]) for s in range(n_lag)])
        return (cur[0], cur[1]) + tuple(lag[:-2])

    def outer(i, lag):
        if on_query_block is not None:
            on_query_block(i)

        def inner(n, lag):
            return shift((i, i - n if descending else n), lag)
        return lax.fori_loop(0, i + 1, inner, lag)

    lag = lax.fori_loop(0, N_KB, outer, (none,) * (2 * n_lag))
    lax.fori_loop(0, n_lag, lambda n, lag: shift((none, none), lag), lag)


def _head_masks():
    lane = lax.broadcasted_iota(jnp.int32, (TQ, 128), 1)
    row = lax.broadcasted_iota(jnp.int32, (TQ, 128), 0)
    return lane, row, lane < 64


def _pick_head(x, head0, h):
    zero = jnp.zeros_like(x)
    return jnp.where(head0, x, zero) if h == 0 else jnp.where(head0, zero, x)


N_PAIRS = ATT_DIM // 128


def attn_fwd(q, k, v, gather=()):
    ng = len(gather)

    def body(*refs):
        q_ref, k_ref, v_ref = refs[:3]
        o_ref, rc_ref = refs[3 + ng:5 + ng]
        acc_ref, r_ref, z_ref, spb_ref, ab_ref = refs[5 + 2 * ng:10 + 2 * ng]
        if ng:
            ag = _Gather([s.shape for s in gather], refs[3:3 + ng], refs[5 + ng:5 + 2 * ng], refs[10 + 2 * ng:])
            ag.start()
        lane, row, head0 = _head_masks()
        w = _cumsum_weights(suffix=True, with_total=True)
        acc_ref[...] = jnp.zeros_like(acc_ref)
        r_ref[...] = jnp.zeros_like(r_ref)
        rc_ref[...] = jnp.zeros_like(rc_ref)
        z_ref[...] = jnp.full(z_ref.shape, NO_SCORE, F32)
        spb_ref[...] = jnp.zeros_like(spb_ref)
        ab_ref[...] = jnp.zeros_like(ab_ref)

        def step(pairs):
            (i1, j1), (i2, j2), (i3, j3) = pairs
            q1, k1, q2, q3, k3 = (pl.multiple_of(jnp.maximum(b, 0) * TQ, TQ) for b in (i1, j1, i2, i3, j3))
            bias1 = _score_bias(lane, row, i1, j1)
            first2 = j2 == i2
            rc_rows = rc_ref[pl.ds(q2, TQ), :]
            for p in range(N_PAIRS):
                cols = slice(128 * p, 128 * (p + 1))
                qb = q_ref[pl.ds(q1, TQ), cols]
                kb = k_ref[pl.ds(k1, TQ), cols]
                vb = v_ref[pl.ds(k3, TQ), cols]
                for h in range(2):
                    hh = 2 * p + h
                    acc_ref[pl.ds(q3, TQ), cols] += _dot(ab_ref[hh], _pick_head(vb, head0, h))
                    r = _dot(spb_ref[hh], w)
                    r_in = jnp.where(first2, 0.0, r_ref[hh])
                    ab_ref[hh] = jnp.exp(z_ref[hh] - (r[:, :128] + r_in)).astype(MM)
                    rc_rows = jnp.where(jnp.logical_and(lane == 16 * hh + j2, i2 >= 0), r_in, rc_rows)
                    r_ref[hh] = r_in + r[:, 128:]
                    z = _dot_nt(_pick_head(qb, head0, h), kb) + bias1
                    z_ref[hh] = z
                    spb_ref[hh] = _softplus(z).astype(MM)
            rc_ref[pl.ds(q2, TQ), :] = rc_rows

        if ng:
            _block_pipeline(3, True, step, lambda i: pl.when(i == N_KB - 2)(ag.forward))
        else:
            _block_pipeline(3, True, step)
        o_ref[...] = acc_ref[...].astype(MM)
        if ng:
            ag.finish()

    out_shape = [jax.ShapeDtypeStruct((SEQ, ATT_DIM), MM), jax.ShapeDtypeStruct((SEQ, 128), F32)]
    out_shape += _Gather.out_shapes(gather)
    return pl.pallas_call(
        body, name="attn_fwd", out_shape=out_shape,
        in_specs=[VMEM_SPEC] * 3 + [ANY] * ng, out_specs=[VMEM_SPEC] * 2 + [ANY] * ng,
        scratch_shapes=[pltpu.VMEM((SEQ, ATT_DIM), F32), pltpu.VMEM((8, TQ, 128), F32),
                        pltpu.VMEM((8, TQ, 128), F32), pltpu.VMEM((8, TQ, 128), MM), pltpu.VMEM((8, TQ, 128), MM)]
                       + (_Gather.scratch(gather) if ng else []),
        compiler_params=_params(),
    )(q, k, v, *gather)


def mix_fwd(u3, att, gc, ga, x, w_cb_g, b_cb, w_ab_g, w_out_g, g2, g3, gather=()):
    ng = len(gather)
    nt = SEQ // TM

    def body(*refs):
        u_ref, a_ref, gc_ref, ga_ref, x_ref, wcb_ref, bcb_ref, wab_ref, wout_ref, g2_ref, g3_ref = refs[:11]
        co_ref, ao_ref, mg_ref, mix_ref, x2_ref, h2_ref = refs[11 + ng:17 + ng]
        if ng:
            done = _gather_behind_grid(_Gather([s.shape for s in gather], refs[11:11 + ng],
                                               refs[17 + ng:17 + 2 * ng], refs[17 + 2 * ng:]), nt)
        u = u_ref[...]
        a = a_ref[...]
        co = jnp.concatenate([_dot(u, wcb_ref[j]) for j in range(N_CHIPS)], axis=1) + bcb_ref[...]
        ao = jnp.concatenate([_dot(a, wab_ref[j]) for j in range(N_CHIPS)], axis=1)
        co_ref[...] = co.astype(MM)
        ao_ref[...] = ao.astype(MM)
        merged = (_sigmoid(gc_ref[...]) * co + _sigmoid(ga_ref[...]) * ao).astype(MM)
        mg_ref[...] = merged
        mix = _dot(merged, wout_ref[...])
        mix_ref[...] = mix
        n2, _ = _rms(mix)
        x2 = x_ref[...] + n2 * g2_ref[...]
        x2_ref[...] = x2
        n3, _ = _rms(x2)
        h2_ref[...] = (n3 * g3_ref[...]).astype(MM)
        if ng:
            done()

    out_shape = [
        jax.ShapeDtypeStruct((SEQ, D_MODEL), MM), jax.ShapeDtypeStruct((SEQ, D_MODEL), MM),
        jax.ShapeDtypeStruct((SEQ, D_MODEL), MM), jax.ShapeDtypeStruct((SEQ, D_MODEL), F32),
        jax.ShapeDtypeStruct((SEQ, D_MODEL), F32), jax.ShapeDtypeStruct((SEQ, D_MODEL), MM),
    ]
    vec = _full_spec((1, D_MODEL))
    return pl.pallas_call(
        body, name="mix_fwd", grid=(nt,),
        out_shape=out_shape + _Gather.out_shapes(gather),
        in_specs=[_row_tile_spec(CONV_DIM), _row_tile_spec(ATT_DIM), _row_tile_spec(D_MODEL),
                  _row_tile_spec(D_MODEL), _row_tile_spec(D_MODEL), _weight_spec(w_cb_g.shape), vec,
                  _weight_spec(w_ab_g.shape), _weight_spec(w_out_g.shape), vec, vec] + [ANY] * ng,
        out_specs=[_row_tile_spec(D_MODEL)] * 6 + [ANY] * ng,
        scratch_shapes=_Gather.scratch(gather) if ng else [],
        compiler_params=_params(("arbitrary",)),
    )(u3, att, gc, ga, x, w_cb_g, b_cb, w_ab_g, w_out_g, g2, g3, *gather)


def ffn_up_fwd(h2, w_up_g, gather=()):
    ng = len(gather)
    nt = SEQ // TM

    def body(*refs):
        h_ref, wg_ref, wu_ref = refs[:3]
        gate_ref, up_ref, act_ref = refs[3 + ng:6 + ng]
        if ng:
            done = _gather_behind_grid(_Gather([s.shape for s in gather], refs[3:3 + ng],
                                               refs[6 + ng:6 + 2 * ng], refs[6 + 2 * ng:]),
                                       2 * nt, pl.program_id(0) * nt + pl.program_id(1))
        h = h_ref[...]
        gate = _dot(h, wg_ref[0])
        up = _dot(h, wu_ref[0])
        gate_ref[...] = gate.astype(MM)
        up_ref[...] = up.astype(MM)
        act_ref[...] = (gate * _sigmoid(gate) * up).astype(MM)
        if ng:
            done()

    tile = pl.BlockSpec((TM, UP_SHARD), lambda n, i: (i, n))
    act = jax.ShapeDtypeStruct((SEQ, D_FF), MM)
    return pl.pallas_call(
        body, name="ffn_up_fwd", grid=(2, nt), out_shape=[act, act, act] + _Gather.out_shapes(gather),
        in_specs=[pl.BlockSpec((TM, D_MODEL), lambda n, i: (i, 0)),
                  pl.BlockSpec((1, D_MODEL, UP_SHARD), lambda n, i: (n, 0, 0)),
                  pl.BlockSpec((1, D_MODEL, UP_SHARD), lambda n, i: (n + 2, 0, 0))] + [ANY] * ng,
        out_specs=[tile, tile, tile] + [ANY] * ng,
        scratch_shapes=_Gather.scratch(gather) if ng else [],
        compiler_params=_params(("arbitrary", "arbitrary")),
    )(h2, w_up_g, w_up_g, *gather)


def ffn_down_loss(act, w_down_g, x2, target, g4):
    def body(act_ref, wd_ref, x2_ref, t_ref, g_ref, dff_ref, dy_ref, loss_ref, dg_ref):
        ff = _dot(act_ref[...], wd_ref[...])
        n4, r4 = _rms(ff)
        g4v = g_ref[...]
        err = x2_ref[...] + n4 * g4v - t_ref[...]
        row_loss = jnp.mean(err * err, axis=-1, keepdims=True)
        loss_ref[...] = jnp.zeros((8, 128), F32) + 0.5 * jnp.sum(row_loss, axis=0, keepdims=True)
        dy = err * (1.0 / D_MODEL)
        dy_ref[...] = dy
        dff_ref[...] = _rms_bwd(dy * g4v, n4, r4).astype(MM)
        _acc_rows(dg_ref, jnp.sum(dy * n4, axis=0, keepdims=True), pl.program_id(0) == 0)

    nt = SEQ // TM
    vec = _full_spec((1, D_MODEL))
    return pl.pallas_call(
        body, name="ffn_down_loss", grid=(nt,),
        out_shape=(jax.ShapeDtypeStruct((SEQ, D_MODEL), MM), jax.ShapeDtypeStruct((SEQ, D_MODEL), F32),
                   jax.ShapeDtypeStruct((nt * 8, 128), F32), jax.ShapeDtypeStruct((1, D_MODEL), F32)),
        in_specs=[_row_tile_spec(D_FF), _weight_spec(w_down_g.shape), _row_tile_spec(D_MODEL),
                  _row_tile_spec(D_MODEL), vec],
        out_specs=[_row_tile_spec(D_MODEL), _row_tile_spec(D_MODEL),
                   pl.BlockSpec((8, 128), lambda i: (i, 0)), vec],
        compiler_params=_params(("arbitrary",)),
    )(act, w_down_g, x2, target, g4)


def ffn_act_bwd(dff, w_down_g, gate, up):
    def body(dff_ref, wd_ref, gate_ref, up_ref, dgu_ref):
        dact = _dot_nt(dff_ref[...], wd_ref[...])
        gate = gate_ref[...].astype(F32)
        sg = _sigmoid(gate)
        dgu_ref[:, 0:D_FF] = (dact * up_ref[...].astype(F32) * (sg * (1.0 + gate * (1.0 - sg)))).astype(MM)
        dgu_ref[:, D_FF:2 * D_FF] = (dact * (gate * sg)).astype(MM)

    return pl.pallas_call(
        body, name="ffn_act_bwd", grid=(SEQ // TM,),
        out_shape=jax.ShapeDtypeStruct((SEQ, 2 * D_FF), MM),
        in_specs=[_row_tile_spec(D_MODEL), _weight_spec(w_down_g.shape), _row_tile_spec(D_FF), _row_tile_spec(D_FF)],
        out_specs=_row_tile_spec(2 * D_FF),
        compiler_params=_params(("arbitrary",)),
    )(dff, w_down_g, gate, up)


def ffn_in_bwd(dgu, w_up_g, x2, mix, dy, g3, g2):
    def body(dgu_ref, w_ref, x2_ref, mix_ref, dy_ref, g3_ref, g2_ref, dx2_ref, dmix_ref, dg3_ref, dg2_ref):
        dh2 = None
        for j in range(N_CHIPS):
            t = _dot_nt(dgu_ref[:, j * UP_SHARD:(j + 1) * UP_SHARD], w_ref[j])
            dh2 = t if dh2 is None else dh2 + t
        first = pl.program_id(0) == 0
        n3, r3 = _rms(x2_ref[...])
        dx2 = dy_ref[...] + _rms_bwd(dh2 * g3_ref[...], n3, r3)
        dx2_ref[...] = dx2
        _acc_rows(dg3_ref, jnp.sum(dh2 * n3, axis=0, keepdims=True), first)
        n2, r2 = _rms(mix_ref[...])
        dmix_ref[...] = _rms_bwd(dx2 * g2_ref[...], n2, r2).astype(MM)
        _acc_rows(dg2_ref, jnp.sum(dx2 * n2, axis=0, keepdims=True), first)

    vec = _full_spec((1, D_MODEL))
    return pl.pallas_call(
        body, name="ffn_in_bwd", grid=(SEQ // TM,),
        out_shape=(jax.ShapeDtypeStruct((SEQ, D_MODEL), F32), jax.ShapeDtypeStruct((SEQ, D_MODEL), MM),
                   jax.ShapeDtypeStruct((1, D_MODEL), F32), jax.ShapeDtypeStruct((1, D_MODEL), F32)),
        in_specs=[_row_tile_spec(2 * D_FF), _weight_spec(w_up_g.shape), _row_tile_spec(D_MODEL),
                  _row_tile_spec(D_MODEL), _row_tile_spec(D_MODEL), vec, vec],
        out_specs=[_row_tile_spec(D_MODEL), _row_tile_spec(D_MODEL), vec, vec],
        compiler_params=_params(("arbitrary",)),
    )(dgu, w_up_g, x2, mix, dy, g3, g2)


def merge_bwd(dmix, w_out_g, gc, ga, co, ao, w_cb_g, w_ab_g):
    def body(dmix_ref, wout_ref, gc_ref, ga_ref, co_ref, ao_ref, wcb_ref, wab_ref,
             dco_ref, dao_ref, dg_ref, du3_ref, datt_ref, dbcb_ref):
        dm = _dot_nt(dmix_ref[...], wout_ref[...])
        sgc = _sigmoid(gc_ref[...])
        sga = _sigmoid(ga_ref[...])
        dco = dm * sgc
        dao = dm * sga
        dg_ref[:, 0:D_MODEL] = (dm * co_ref[...].astype(F32) * (sgc * (1.0 - sgc))).astype(MM)
        dg_ref[:, D_MODEL:2 * D_MODEL] = (dm * ao_ref[...].astype(F32) * (sga * (1.0 - sga))).astype(MM)
        _acc_rows(dbcb_ref, jnp.sum(dco, axis=0, keepdims=True), pl.program_id(0) == 0)
        dco_ref[...] = dco.astype(MM)
        dao_ref[...] = dao.astype(MM)
        du3 = None
        datt = None
        for j in range(N_CHIPS):
            cols = slice(j * BR_SHARD, (j + 1) * BR_SHARD)
            t = _dot_nt(dco_ref[:, cols], wcb_ref[j])
            s = _dot_nt(dao_ref[:, cols], wab_ref[j])
            du3 = t if du3 is None else du3 + t
            datt = s if datt is None else datt + s
        du3_ref[...] = du3
        datt_ref[...] = datt.astype(MM)

    wide = _row_tile_spec(D_MODEL)
    return pl.pallas_call(
        body, name="merge_bwd", grid=(SEQ // TM,),
        out_shape=(jax.ShapeDtypeStruct((SEQ, D_MODEL), MM), jax.ShapeDtypeStruct((SEQ, D_MODEL), MM),
                   jax.ShapeDtypeStruct((SEQ, 2 * D_MODEL), MM),
                   jax.ShapeDtypeStruct((SEQ, CONV_DIM), F32), jax.ShapeDtypeStruct((SEQ, ATT_DIM), MM),
                   jax.ShapeDtypeStruct((1, D_MODEL), F32)),
        in_specs=[wide, _weight_spec(w_out_g.shape), wide, wide, wide, wide,
                  _weight_spec(w_cb_g.shape), _weight_spec(w_ab_g.shape)],
        out_specs=[wide, wide, _row_tile_spec(2 * D_MODEL), _row_tile_spec(CONV_DIM), _row_tile_spec(ATT_DIM),
                   _full_spec((1, D_MODEL))],
        compiler_params=_params(("arbitrary",)),
    )(dmix, w_out_g, gc, ga, co, ao, w_cb_g, w_ab_g)


def conv_bwd(du3, u1, ci, w_dw, ln_g, ln_b):
    def body(du3_ref, u1_ref, ci_ref, w_ref, g_ref, bb_ref,
             dci_ref, dw_ref, dbdw_ref, dg_ref, db_ref, upad_ref, dpad_ref, dwacc_ref, vacc_ref):
        _glu_into(ci_ref, upad_ref)
        dpad_ref[SEQ:SEQ + 32, :] = jnp.zeros((32, CONV_DIM), F32)
        dwacc_ref[...] = jnp.zeros_like(dwacc_ref)
        vacc_ref[...] = jnp.zeros_like(vacc_ref)

        def fold8(t):
            s = t[0:8, :]
            for r in range(1, CONV_TILE // 8):
                s = s + t[8 * r:8 * r + 8, :]
            return s

        def pass1(i, c):
            t0 = pl.multiple_of(i * CONV_TILE, CONV_TILE)
            xh, rstd = _layernorm_parts(u1_ref[pl.ds(t0, CONV_TILE), :])
            gv = g_ref[...]
            u2 = xh * gv + bb_ref[...]
            s2 = _sigmoid(u2)
            du2 = du3_ref[pl.ds(t0, CONV_TILE), :] * (s2 * (1.0 + u2 * (1.0 - s2)))
            wv = du2 * gv
            du1 = rstd * (wv - jnp.mean(wv, axis=-1, keepdims=True)
                          - xh * jnp.mean(wv * xh, axis=-1, keepdims=True))
            dpad_ref[pl.ds(t0, CONV_TILE), :] = du1
            vacc_ref[0] += fold8(du2 * xh)
            vacc_ref[1] += fold8(du2)
            vacc_ref[2] += fold8(du1)
            win = upad_ref[pl.ds(t0, CONV_WIN), :]
            n = win.shape[0]
            for rot in range(8):
                shifted = win if rot == 0 else pltpu.roll(win, n - rot, 0)
                for a in range(5):
                    j = 8 * a + rot - 2
                    if 0 <= j < CONV_WIDTH:
                        dwacc_ref[j] += fold8(du1 * shifted[8 * a:8 * a + CONV_TILE, :])
            return c

        lax.fori_loop(0, SEQ // CONV_TILE, pass1, 0)

        def pass2(i, c):
            t0 = pl.multiple_of(i * CONV_TILE, CONV_TILE)
            win = dpad_ref[pl.ds(t0, CONV_WIN), :]
            du0 = _shifted_sum(win, [(30 - j, w_ref[j:j + 1, :]) for j in range(CONV_WIDTH)])
            a = ci_ref[pl.ds(t0, CONV_TILE), 0:CONV_DIM]
            sb = _sigmoid(ci_ref[pl.ds(t0, CONV_TILE), CONV_DIM:2 * CONV_DIM])
            dci_ref[pl.ds(t0, CONV_TILE), 0:CONV_DIM] = (du0 * sb).astype(MM)
            dci_ref[pl.ds(t0, CONV_TILE), CONV_DIM:2 * CONV_DIM] = (du0 * a * (sb * (1.0 - sb))).astype(MM)
            return c

        lax.fori_loop(0, SEQ // CONV_TILE, pass2, 0)

        for j in range(CONV_WIDTH):
            dw_ref[j:j + 1, :] = jnp.sum(dwacc_ref[j], axis=0, keepdims=True)
        dw_ref[CONV_WIDTH:32, :] = jnp.zeros((32 - CONV_WIDTH, CONV_DIM), F32)
        dg_ref[...] = jnp.sum(vacc_ref[0], axis=0, keepdims=True)
        db_ref[...] = jnp.sum(vacc_ref[1], axis=0, keepdims=True)
        dbdw_ref[...] = jnp.sum(vacc_ref[2], axis=0, keepdims=True)

    vec = jax.ShapeDtypeStruct((1, CONV_DIM), F32)
    return pl.pallas_call(
        body, name="conv_bwd",
        out_shape=(jax.ShapeDtypeStruct((SEQ, 2 * CONV_DIM), MM), jax.ShapeDtypeStruct((32, CONV_DIM), F32),
                   vec, vec, vec),
        in_specs=[VMEM_SPEC] * 6, out_specs=[VMEM_SPEC] * 5,
        scratch_shapes=[pltpu.VMEM((SEQ + 32, CONV_DIM), F32), pltpu.VMEM((SEQ + 32, CONV_DIM), F32),
                        pltpu.VMEM((CONV_WIDTH, 8, CONV_DIM), F32), pltpu.VMEM((3, 8, CONV_DIM), F32)],
        compiler_params=_params(),
    )(du3, u1, ci, w_dw, ln_g, ln_b)


def attn_bwd(q, k, v, datt, rc, scatter=()):
    ns = len(scatter)

    def body(*refs):
        q_ref, k_ref, v_ref, do_ref, rc_ref = refs[:5]
        dqkv_ref = refs[5 + ns]
        (dqa_ref, dka_ref, dva_ref, pc_ref, z_ref, sig1_ref, sig2_ref, g_ref, spb_ref, gb_ref, ab_ref,
         dzb_ref) = refs[6 + 2 * ns:18 + 2 * ns]
        if ns:
            sc = _Scatter(refs[5:5 + ns], refs[6 + ns:6 + 2 * ns], refs[18 + 2 * ns:])
            sc.start()
        lane, row, head0 = _head_masks()
        for ref in (dqa_ref, dka_ref, dva_ref, pc_ref):
            ref[...] = jnp.zeros_like(ref)
        z_ref[...] = jnp.full(z_ref.shape, NO_SCORE, F32)
        for ref in (sig1_ref, sig2_ref, spb_ref, ab_ref, g_ref, gb_ref, dzb_ref):
            ref[...] = jnp.zeros_like(ref)
        w_suffix = _cumsum_weights(suffix=True, with_total=False)
        w_prefix = _cumsum_weights(suffix=False, with_total=True)

        def step(pairs):
            (ia, ja), (ib, jb), (ic, jc), (id_, jd) = pairs
            qa, ka, qb_, kb_, qc, kc, qd, kd = (pl.multiple_of(jnp.maximum(b, 0) * TQ, TQ)
                                                for b in (ia, ja, ib, jb, ic, jc, id_, jd))
            bias_a = _score_bias(lane, row, ia, ja)
            rc_rows = rc_ref[pl.ds(qb_, TQ), :]
            first_c = jc == 0
            for p in range(N_PAIRS):
                cols = slice(128 * p, 128 * (p + 1))
                q_a = q_ref[pl.ds(qa, TQ), cols]
                k_a = k_ref[pl.ds(ka, TQ), cols]
                do_b = do_ref[pl.ds(qb_, TQ), cols]
                v_b = v_ref[pl.ds(kb_, TQ), cols]
                do_c = do_ref[pl.ds(qc, TQ), cols]
                q_d = q_ref[pl.ds(qd, TQ), cols]
                k_d = k_ref[pl.ds(kd, TQ), cols]
                for h in range(2):
                    hh = 2 * p + h
                    dzb = dzb_ref[hh]
                    dqa_ref[pl.ds(qd, TQ), cols] += _dot(dzb, _pick_head(k_d, head0, h))
                    dka_ref[pl.ds(kd, TQ), cols] += _dot_tn(dzb, _pick_head(q_d, head0, h))
                    r = _dot(gb_ref[hh], w_prefix)
                    p_in = jnp.where(first_c, 0.0, pc_ref[hh])
                    dzb_ref[hh] = (g_ref[hh] - sig2_ref[hh] * (r[:, :128] + p_in)).astype(MM)
                    pc_ref[hh] = p_in + r[:, 128:]
                    dva_ref[pl.ds(kc, TQ), cols] += _dot_tn(ab_ref[hh], _pick_head(do_c, head0, h))
                    r_in = jnp.sum(jnp.where(lane == 16 * hh + jb, rc_rows, 0.0), axis=1, keepdims=True)
                    a = jnp.exp(z_ref[hh] - (_dot(spb_ref[hh], w_suffix) + r_in))
                    g = _dot_nt(_pick_head(do_b, head0, h), v_b) * a
                    ab_ref[hh] = a.astype(MM)
                    g_ref[hh] = g
                    gb_ref[hh] = g.astype(MM)
                    sig2_ref[hh] = sig1_ref[hh]
                    z = _dot_nt(_pick_head(q_a, head0, h), k_a) + bias_a
                    sp = _softplus(z)
                    sig1_ref[hh] = jnp.exp(z - sp)
                    z_ref[hh] = z
                    spb_ref[hh] = sp.astype(MM)

        _block_pipeline(4, False, step)
        dqkv_ref[:, 0:ATT_DIM] = (dqa_ref[...] * ATT_SCALE).astype(MM)
        dqkv_ref[:, ATT_DIM:2 * ATT_DIM] = dka_ref[...].astype(MM)
        dqkv_ref[:, 2 * ATT_DIM:3 * ATT_DIM] = dva_ref[...].astype(MM)
        if ns:
            sc.finish()

    out = jax.ShapeDtypeStruct((SEQ, 3 * ATT_DIM), MM)
    return pl.pallas_call(
        body, name="attn_bwd", out_shape=[out] + _Scatter.out_shapes(scatter),
        in_specs=[VMEM_SPEC] * 5 + [ANY] * ns, out_specs=[VMEM_SPEC] + [ANY] * ns,
        scratch_shapes=[pltpu.VMEM((SEQ, ATT_DIM), F32)] * 3 + [pltpu.VMEM((8, TQ, 128), F32)] * 5
                       + [pltpu.VMEM((8, TQ, 128), MM)] * 4 + (_Scatter.scratch(ns) if ns else []),
        compiler_params=_params(),
    )(q, k, v, datt, rc, *scatter)


DPROJ_PIECES = ((0, 1024), (1024, 2560), (2560, 4608))


def _dproj_segments(j):
    g0, g1 = j * IN_SHARD, (j + 1) * IN_SHARD
    segs = []
    for p, (s, e) in enumerate(DPROJ_PIECES):
        lo, hi = max(s, g0), min(e, g1)
        if lo < hi:
            segs.append((p, lo - s, lo - g0, hi - lo))
    return segs


def in_proj_bwd(pieces, w_in_g, x, dx2, g1, scatter=()):
    ns = len(scatter)
    nt = SEQ // TM

    def body(*refs):
        p_refs = refs[:3]
        w_ref, x_ref, dx2_ref, g_ref = refs[3:7]
        dx_ref, dg_ref = refs[7 + ns:9 + ns]
        if ns:
            sc = _Scatter(refs[7:7 + ns], refs[9 + ns:9 + 2 * ns], refs[9 + 2 * ns:])
            pl.when(pl.program_id(0) == 0)(sc.start)
        dh = None
        for j in range(N_CHIPS):
            for p, lo, off, width in _dproj_segments(j):
                t = _dot_nt(p_refs[p][:, lo:lo + width], w_ref[j, :, off:off + width])
                dh = t if dh is None else dh + t
        n1, r1 = _rms(x_ref[...])
        dx_ref[...] = dx2_ref[...] + _rms_bwd(dh * g_ref[...], n1, r1)
        _acc_rows(dg_ref, jnp.sum(dh * n1, axis=0, keepdims=True), pl.program_id(0) == 0)
        if ns:
            pl.when(pl.program_id(0) == nt - 1)(sc.finish)

    vec = _full_spec((1, D_MODEL))
    return pl.pallas_call(
        body, name="in_proj_bwd", grid=(nt,),
        out_shape=[jax.ShapeDtypeStruct((SEQ, D_MODEL), F32), jax.ShapeDtypeStruct((1, D_MODEL), F32)]
                  + _Scatter.out_shapes(scatter),
        in_specs=[_row_tile_spec(p.shape[1]) for p in pieces]
                 + [_weight_spec(w_in_g.shape), _row_tile_spec(D_MODEL), _row_tile_spec(D_MODEL), vec] + [ANY] * ns,
        out_specs=[_row_tile_spec(D_MODEL), vec] + [ANY] * ns,
        scratch_shapes=_Scatter.scratch(ns) if ns else [],
        compiler_params=_params(("arbitrary",)),
    )(*pieces, w_in_g, x, dx2, g1, *scatter)


def weight_grad_in(h1, pieces):
    kh = D_MODEL // 2

    def body(a_ref, p0_ref, p1_ref, p2_ref, o_ref):
        p_refs = (p0_ref, p1_ref, p2_ref)
        a = a_ref[...]
        for j in range(N_CHIPS):
            @pl.when(pl.program_id(1) == j)
            def _():
                for p, lo, off, width in _dproj_segments(j):
                    o_ref[0, 0, :, off:off + width] = _dot_tn(a, p_refs[p][:, lo:lo + width]).astype(MM)

    return pl.pallas_call(
        body, name="dw_in", grid=(2, N_CHIPS), out_shape=jax.ShapeDtypeStruct((N_CHIPS, 2, kh, IN_SHARD), MM),
        in_specs=[pl.BlockSpec((SEQ, kh), lambda h, j: (0, h))] + [_weight_spec(p.shape) for p in pieces],
        out_specs=pl.BlockSpec((1, 1, kh, IN_SHARD), lambda h, j: (j, h, 0, 0)),
        compiler_params=_params(("arbitrary", "arbitrary")),
    )(h1, *pieces)


def weight_grad(a, b, name, col_sharded, tk=None):
    kin, n = a.shape[1], b.shape[1]

    def body(a_ref, b_ref, o_ref):
        if col_sharded:
            o_ref[0, 0] = _dot_tn(a_ref[...], b_ref[...]).astype(MM)
        else:
            o_ref[...] = _dot_tn(a_ref[...], b_ref[...]).astype(MM)

    if col_sharded:
        kh, ns = kin // 2, n // N_CHIPS
        out = jax.ShapeDtypeStruct((N_CHIPS, 2, kh, ns), MM)
        grid = (2, N_CHIPS)
        in_specs = [pl.BlockSpec((SEQ, kh), lambda h, j: (0, h)), pl.BlockSpec((SEQ, ns), lambda h, j: (0, j))]
        out_spec = pl.BlockSpec((1, 1, kh, ns), lambda h, j: (j, h, 0, 0))
        sem = ("arbitrary", "arbitrary")
    else:
        out = jax.ShapeDtypeStruct((kin, n), MM)
        grid = (kin // tk,)
        in_specs = [pl.BlockSpec((SEQ, tk), lambda r: (0, r)), pl.BlockSpec((SEQ, n), lambda r: (0, 0))]
        out_spec = pl.BlockSpec((tk, n), lambda r: (r, 0))
        sem = ("arbitrary",)
    res = pl.pallas_call(
        body, name=name, grid=grid, out_shape=out, in_specs=in_specs, out_specs=out_spec,
        compiler_params=_params(sem),
    )(a, b)
    if not col_sharded:
        res = res.reshape(N_CHIPS, 2, kin // (2 * N_CHIPS), n)
    return res


def _place():
    x, y, c = lax.axis_index("x"), lax.axis_index("y"), lax.axis_index("c")
    chips = [(1 - x, y), (x, 1 - y), (1 - x, 1 - y)]
    return x, y, c, chips


def _rcopy(src, dst, send_sem, recv_sem, dev):
    return pltpu.make_async_remote_copy(src_ref=src, dst_ref=dst, send_sem=send_sem, recv_sem=recv_sem,
                                        device_id=dev, device_id_type=MESH)


class _Gather:
    def __init__(self, shapes, w, o, scratch):
        self.n, self.shapes, self.w, self.o = len(w), shapes, w, o
        self.send, self.recv, self.fsend, self.frecv, self.loc_in, self.loc_out = scratch[:6]
        self.raw, self.stage = scratch[6:6 + self.n], scratch[6 + self.n:]
        self.x, self.y, self.c, self.chips = _place()
        self.me = 2 * self.x + self.y
        self.sib = (self.x, self.y, 1 - self.c)
        self.pairs = [(j, t) for j in range(3) for t in range(self.n)]

    @staticmethod
    def scratch(shards):
        n = len(shards)
        sems = pltpu.SemaphoreType.DMA
        return ([sems((3 * n,)), sems((3 * n,)), sems((3 * n,)), sems((3 * n,)), sems((n,)), sems((n,))]
                + [pltpu.VMEM(s.shape, s.dtype) for s in shards] + [pltpu.VMEM(s.shape, MM) for s in shards])

    @staticmethod
    def out_shapes(shards):
        return [jax.ShapeDtypeStruct((N_CHIPS,) + s.shape, MM) for s in shards]

    def _half(self, t, k, cc):
        rh = self.shapes[t][0] // 2
        return self.o[t].at[k, pl.ds(cc * rh, rh), :]

    def _chip(self, j):
        cx, cy = self.chips[j]
        return 2 * cx + cy, (cx, cy, self.c)

    def local_in(self, t):
        return pltpu.make_async_copy(self.w[t], self.raw[t], self.loc_in.at[t])

    def local_out(self, t):
        return pltpu.make_async_copy(self.stage[t], self.o[t].at[self.me], self.loc_out.at[t])

    def first(self, j, t):
        rh = self.shapes[t][0] // 2
        i = j * self.n + t
        return _rcopy(self.stage[t].at[pl.ds(self.c * rh, rh), :], self._half(t, self.me, self.c),
                      self.send.at[i], self.recv.at[i], self._chip(j)[1])

    def arrived(self, j, t):
        k, dev = self._chip(j)
        i = j * self.n + t
        blk = self._half(t, k, self.c)
        return _rcopy(blk, blk, self.send.at[i], self.recv.at[i], dev)

    def passed(self, j, t, cc):
        i = j * self.n + t
        blk = self._half(t, self._chip(j)[0], cc)
        return _rcopy(blk, blk, self.fsend.at[i], self.frecv.at[i], self.sib)

    def start(self):
        for t in range(self.n):
            self.local_in(t).start()
        for t in range(self.n):
            self.local_in(t).wait()
            self.stage[t][...] = self.raw[t][...].astype(MM)
            self.local_out(t).start()
        for j, t in self.pairs:
            self.first(j, t).start()

    def forward(self):
        for j, t in self.pairs:
            self.arrived(j, t).wait_recv()
            self.passed(j, t, self.c).start()

    def finish(self):
        for j, t in self.pairs:
            self.passed(j, t, 1 - self.c).wait_recv()
        for j, t in self.pairs:
            self.first(j, t).wait_send()
            self.passed(j, t, self.c).wait_send()
        for t in range(self.n):
            self.local_out(t).wait()


def all_gather_weights(shards, small):
    n = len(shards)
    shapes = [s.shape for s in shards]

    def body(*refs):
        w = refs[:n]
        sm = refs[n]
        o = refs[n + 1:2 * n + 1]
        osm = refs[2 * n + 1]
        ssend, srecv, sloc = refs[2 * n + 2:2 * n + 5]
        g = _Gather(shapes, w, o, refs[2 * n + 5:])
        own = pltpu.make_async_copy(sm, osm.at[g.me], sloc)
        own.start()
        g.start()
        small_cps = [_rcopy(sm, osm.at[g.me], ssend.at[j], srecv.at[j], g._chip(j)[1]) for j in range(3)]
        for cp in small_cps:
            cp.start()
        g.forward()
        g.finish()
        for j in range(3):
            k, dev = g._chip(j)
            _rcopy(sm, osm.at[k], ssend.at[j], srecv.at[j], dev).wait_recv()
            small_cps[j].wait_send()
        own.wait()

    out_shape = _Gather.out_shapes(shards)
    out_shape.append(jax.ShapeDtypeStruct((N_CHIPS,) + small.shape, small.dtype))
    sems = pltpu.SemaphoreType.DMA
    return pl.pallas_call(
        body, name="all_gather_weights", out_shape=out_shape,
        in_specs=[ANY] * (n + 1), out_specs=[ANY] * (n + 1),
        scratch_shapes=[sems((3,)), sems((3,)), sems] + _Gather.scratch(shards),
        compiler_params=_params(),
    )(*shards, small)


def sibling_exchange(grads, name):
    n = len(grads)

    def body(*refs):
        g = refs[:n]
        o = refs[n:2 * n]
        send, recv = refs[2 * n:]
        x, y, c, _ = _place()
        cps = [_rcopy(g[t].at[:, 1 - c], o[t], send.at[t], recv.at[t], (x, y, 1 - c)) for t in range(n)]
        for cp in cps:
            cp.start()
        for cp in cps:
            cp.wait()

    sems = pltpu.SemaphoreType.DMA
    return pl.pallas_call(
        body, name=name,
        out_shape=[jax.ShapeDtypeStruct((a.shape[0],) + a.shape[2:], a.dtype) for a in grads],
        in_specs=[ANY] * n, out_specs=[ANY] * n, scratch_shapes=[sems((n,)), sems((n,))],
    )(*grads)


class _Scatter:
    def __init__(self, p, o, sems):
        self.n, self.p, self.o = len(p), p, o
        self.send, self.recv = sems
        _, _, self.c, self.chips = _place()

    @staticmethod
    def scratch(n):
        sems = pltpu.SemaphoreType.DMA
        return [sems((3 * n,)), sems((3 * n,))]

    @staticmethod
    def out_shapes(parts):
        return [jax.ShapeDtypeStruct((3,) + a.shape[1:], a.dtype) for a in parts]

    def copies(self):
        cps = []
        for j, (cx, cy) in enumerate(self.chips):
            for t in range(self.n):
                i = j * self.n + t
                cps.append(_rcopy(self.p[t].at[2 * cx + cy], self.o[t].at[j], self.send.at[i], self.recv.at[i],
                                  (cx, cy, self.c)))
        return cps

    def start(self):
        for cp in self.copies():
            cp.start()

    def finish(self):
        for cp in self.copies():
            cp.wait()


HBM_SPEC = pl.BlockSpec(memory_space=pltpu.HBM)
SEM_SPEC = pl.BlockSpec(memory_space=pltpu.SEMAPHORE)
DATAFLOW = pltpu.SideEffectType.DATAFLOW_SIDE_EFFECTING


def _scatter_copies(p_ref, land_ref, send, recv):
    _, _, c, chips = _place()
    return [_rcopy(p_ref.at[2 * cx + cy], land_ref.at[j], send.at[j], recv.at[j], (cx, cy, c))
            for j, (cx, cy) in enumerate(chips)]


def scatter_start(part):
    land = lax.empty((3,) + part.shape[1:], part.dtype)

    def body(p_ref, land_ref, send, recv, p_thru, land_thru, token):
        for cp in _scatter_copies(p_ref, land_ref, send, recv):
            cp.start()
        token[...] = jnp.zeros_like(token)

    sems = pltpu.SemaphoreType.DMA((3,))
    return pl.pallas_call(
        body, name="scatter_start",
        out_shape=(sems, sems, pltpu.HBM(part.shape, part.dtype), pltpu.HBM(land.shape, land.dtype),
                   jax.ShapeDtypeStruct((8, 128), F32)),
        in_specs=(HBM_SPEC, HBM_SPEC), out_specs=(SEM_SPEC, SEM_SPEC, HBM_SPEC, HBM_SPEC, VMEM_SPEC),
        input_output_aliases={0: 2, 1: 3},
        compiler_params=pltpu.CompilerParams(has_side_effects=DATAFLOW),
    )(pltpu.with_memory_space_constraint(part, pltpu.HBM), pltpu.with_memory_space_constraint(land, pltpu.HBM))


def scatter_wait(send, recv, part, land, after):
    def body(*refs):
        p_ref, land_ref, send, recv = refs[:4]
        for cp in _scatter_copies(p_ref, land_ref, send, recv):
            cp.wait_send()
            cp.wait_recv()

    return pl.pallas_call(
        body, name="scatter_wait",
        out_shape=(pltpu.HBM(part.shape, part.dtype), pltpu.HBM(land.shape, land.dtype)),
        in_specs=(HBM_SPEC, HBM_SPEC, SEM_SPEC, SEM_SPEC) + (ANY,) * len(after), out_specs=(HBM_SPEC, HBM_SPEC),
        input_output_aliases={0: 0, 1: 1},
        compiler_params=pltpu.CompilerParams(has_side_effects=DATAFLOW),
    )(part, land, send, recv, *after)[1]


def sibling_swap(halves, name):
    n = len(halves)

    def body(*refs):
        h = refs[:n]
        o = refs[n:2 * n]
        send, recv = refs[2 * n:]
        x, y, c, _ = _place()
        cps = [_rcopy(h[t], o[t], send.at[t], recv.at[t], (x, y, 1 - c)) for t in range(n)]
        for cp in cps:
            cp.start()
        for cp in cps:
            cp.wait()

    sems = pltpu.SemaphoreType.DMA
    return pl.pallas_call(
        body, name=name, out_shape=[jax.ShapeDtypeStruct(a.shape, a.dtype) for a in halves],
        in_specs=[ANY] * n, out_specs=[ANY] * n, scratch_shapes=[sems((n,)), sems((n,))],
    )(*halves)


def small_all_reduce(ddw, v512, v1024, loss_parts):
    rows, width = PACK_ROWS, 512
    n512, n1024 = len(VEC512), len(VEC1024)

    def body(*refs):
        ddw_ref = refs[0]
        a_refs = refs[1:1 + n512]
        b_refs = refs[1 + n512:1 + n512 + n1024]
        lp_ref, o_ref, p_ref, gath_ref, send, recv = refs[1 + n512 + n1024:]
        p_ref[...] = jnp.zeros_like(p_ref)
        p_ref[0:32, :] = ddw_ref[...]
        p_ref[LOSS_ROW:LOSS_ROW + 1, 0:128] = jnp.sum(lp_ref[...], axis=0, keepdims=True) * 0.125
        for i, r in enumerate(a_refs):
            p_ref[32 + i:33 + i, :] = r[...]
        for i, r in enumerate(b_refs):
            base = 32 + n512 + 2 * i
            p_ref[base:base + 1, :] = r[:, 0:512]
            p_ref[base + 1:base + 2, :] = r[:, 512:1024]
        x, y, c, _ = _place()
        me = 4 * x + 2 * y + c
        gath_ref[me] = p_ref[...]
        cps = []
        for k in range(1, 8):
            dx, dy, dc = (k >> 2) & 1, (k >> 1) & 1, k & 1
            px = 1 - x if dx else x
            py = 1 - y if dy else y
            pc = 1 - c if dc else c
            cps.append(_rcopy(p_ref, gath_ref.at[me], send.at[k - 1], recv.at[k - 1], (px, py, pc)))
        for cp in cps:
            cp.start()
        for k in range(1, 8):
            dx, dy, dc = (k >> 2) & 1, (k >> 1) & 1, k & 1
            px = 1 - x if dx else x
            py = 1 - y if dy else y
            pc = 1 - c if dc else c
            _rcopy(p_ref, gath_ref.at[4 * px + 2 * py + pc], send.at[k - 1], recv.at[k - 1], (px, py, pc)).wait_recv()
        for cp in cps:
            cp.wait_send()
        total = gath_ref[0]
        for d in range(1, 8):
            total = total + gath_ref[d]
        o_ref[...] = total

    sems = pltpu.SemaphoreType.DMA
    n_in = 2 + n512 + n1024
    return pl.pallas_call(
        body, name="small_all_reduce", out_shape=jax.ShapeDtypeStruct((rows, width), F32),
        in_specs=[VMEM_SPEC] * n_in, out_specs=VMEM_SPEC,
        scratch_shapes=[pltpu.VMEM((rows, width), F32), pltpu.VMEM((8, rows, width), F32), sems((7,)), sems((7,))],
    )(ddw, *[v512[n] for n in VEC512], *[v1024[n] for n in VEC1024], loss_parts)


def _row_block(r):
    for tr in (512, 352, 256, 128):
        if r % tr == 0:
            return tr
    return r


def add_halves(g, recv, name):
    _, _, r, w = g.shape
    tr = _row_block(r)

    def body(g0_ref, g1_ref, r_ref, ob_ref, own_ref):
        k = pl.program_id(1)
        c = lax.axis_index("c")
        me = 2 * lax.axis_index("x") + lax.axis_index("y")
        t = jnp.where(c == 0, g0_ref[0, 0], g1_ref[0, 0]).astype(F32) + r_ref[0].astype(F32)
        ob_ref[0] = t.astype(MM)
        mine = jnp.where(k == me, t, 0.0)

        @pl.when(k == 0)
        def _():
            own_ref[...] = mine

        @pl.when(k != 0)
        def _():
            own_ref[...] += mine

    return pl.pallas_call(
        body, name=name, grid=(r // tr, N_CHIPS),
        in_specs=[pl.BlockSpec((1, 1, tr, w), lambda i, k: (k, 0, i, 0)),
                  pl.BlockSpec((1, 1, tr, w), lambda i, k: (k, 1, i, 0)),
                  pl.BlockSpec((1, tr, w), lambda i, k: (k, i, 0))],
        out_specs=[pl.BlockSpec((1, tr, w), lambda i, k: (k, i, 0)),
                   pl.BlockSpec((tr, w), lambda i, k: (i, 0))],
        out_shape=(jax.ShapeDtypeStruct((N_CHIPS, r, w), MM), jax.ShapeDtypeStruct((r, w), F32)),
        compiler_params=_params(("arbitrary", "arbitrary")),
    )(g, g, recv)


def sum_parts(own, rin, after, name):
    _, r, w = rin.shape
    tr = _row_block(r)

    def body(o_ref, r_ref, after_ref, out_ref):
        out_ref[...] = ((o_ref[...] + r_ref[0].astype(F32)) + r_ref[1].astype(F32)) + r_ref[2].astype(F32)

    return pl.pallas_call(
        body, name=name, grid=(r // tr,), out_shape=jax.ShapeDtypeStruct((r, w), F32),
        in_specs=[pl.BlockSpec((tr, w), lambda i: (i, 0)), pl.BlockSpec((3, tr, w), lambda i: (0, i, 0)),
                  _full_spec((8, 128))],
        out_specs=pl.BlockSpec((tr, w), lambda i: (i, 0)),
        compiler_params=_params(("arbitrary",)),
    )(own, rin, after)


def _adamw_math(w, g, m, v):
    mn = ADAM_B1 * m + (1.0 - ADAM_B1) * g
    vn = ADAM_B2 * v + (1.0 - ADAM_B2) * (g * g)
    m_hat = mn / (1.0 - ADAM_B1 ** ADAM_STEP)
    v_hat = vn / (1.0 - ADAM_B2 ** ADAM_STEP)
    return -ADAM_LR * (m_hat / (jnp.sqrt(v_hat) + ADAM_EPS) + ADAM_WD * w), mn, vn


def adamw(w, mine, other, m, v, name):
    r, c = w.shape
    rh = r // 2
    tr = _row_block(rh)
    if c >= 1024 and tr % 512 == 0:
        tr = 256
    nb = rh // tr

    def body(w_ref, a_ref, b_ref, m_ref, v_ref, go_ref, d_ref, mo_ref, vo_ref):
        gv = jnp.where(lax.axis_index("c") == pl.program_id(0), a_ref[...], b_ref[...])
        go_ref[...] = gv
        d_ref[...], mo_ref[...], vo_ref[...] = _adamw_math(w_ref[...], gv, m_ref[...], v_ref[...])

    spec = pl.BlockSpec((tr, c), lambda h, i: (h * nb + i, 0))
    half = pl.BlockSpec((tr, c), lambda h, i: (i, 0))
    out = jax.ShapeDtypeStruct((r, c), F32)
    return pl.pallas_call(
        body, name=name, grid=(2, nb), out_shape=(out, out, out, out),
        in_specs=[spec, half, half, spec, spec], out_specs=[spec] * 4,
        compiler_params=_params(("arbitrary", "arbitrary")),
    )(w, mine, other, m, v)


def adamw_small(gsum, params):
    names = list(params)
    flat = [a for n in names for a in params[n]]

    def body(*refs):
        g_ref = refs[0]
        ins = refs[1:1 + 3 * len(names)]
        outs = refs[1 + 3 * len(names):]
        me = 2 * lax.axis_index("x") + lax.axis_index("y")
        for i, n in enumerate(names):
            w_ref, m_ref, v_ref = ins[3 * i:3 * i + 3]
            go_ref, d_ref, mo_ref, vo_ref = outs[4 * i:4 * i + 4]
            if n == "conv_dw_w":
                gv = jnp.zeros((CONV_WIDTH, 128), F32)
                for k in range(N_CHIPS):
                    gv = gv + jnp.where(me == k, g_ref[0:CONV_WIDTH, 128 * k:128 * (k + 1)], 0.0)
            elif n in VEC512:
                r0 = 32 + VEC512.index(n)
                gv = g_ref[r0:r0 + 1, :]
            else:
                r0 = 32 + len(VEC512) + 2 * VEC1024.index(n)
                gv = jnp.concatenate([g_ref[r0:r0 + 1, :], g_ref[r0 + 1:r0 + 2, :]], axis=1)
            go_ref[...] = gv
            d_ref[...], mo_ref[...], vo_ref[...] = _adamw_math(w_ref[...], gv, m_ref[...], v_ref[...])

    out_shape = [jax.ShapeDtypeStruct(params[n][0].shape, F32) for n in names for _ in range(4)]
    res = pl.pallas_call(
        body, name="adamw_small", out_shape=out_shape,
        in_specs=[VMEM_SPEC] * (1 + len(flat)), out_specs=[VMEM_SPEC] * len(out_shape),
        compiler_params=_params(),
    )(gsum, *flat)
    return {n: res[4 * i:4 * i + 4] for i, n in enumerate(names)}


REST = ("w_ffn_up", "w_ffn_down", "w_out", "w_conv_branch", "w_att_branch")
VEC512 = ("conv_dw_b", "conv_ln_g", "conv_ln_b")
VEC1024 = ("norm_mix_pre", "b_conv_branch", "norm_mix_post", "norm_ffn_pre", "norm_ffn_post")
PACK_ROWS = 48
LOSS_ROW = 47


def kernel(x, norm_mix_pre, w_in, conv_dw_w, conv_dw_b, conv_ln_g, conv_ln_b, w_conv_branch, b_conv_branch, w_att_branch, w_out, norm_mix_post, norm_ffn_pre, w_ffn_up, w_ffn_down, norm_ffn_post, loss_target, m_norm_mix_pre, m_w_in, m_conv_dw_w, m_conv_dw_b, m_conv_ln_g, m_conv_ln_b, m_w_conv_branch, m_b_conv_branch, m_w_att_branch, m_w_out, m_norm_mix_post, m_norm_ffn_pre, m_w_ffn_up, m_w_ffn_down, m_norm_ffn_post, v_norm_mix_pre, v_w_in, v_conv_dw_w, v_conv_dw_b, v_conv_ln_g, v_conv_ln_b, v_w_conv_branch, v_b_conv_branch, v_w_att_branch, v_w_out, v_norm_mix_post, v_norm_ffn_pre, v_w_ffn_up, v_w_ffn_down, v_norm_ffn_post):
    weights = dict(norm_mix_pre=norm_mix_pre, w_in=w_in, conv_dw_w=conv_dw_w, conv_dw_b=conv_dw_b, conv_ln_g=conv_ln_g, conv_ln_b=conv_ln_b, w_conv_branch=w_conv_branch, b_conv_branch=b_conv_branch, w_att_branch=w_att_branch, w_out=w_out, norm_mix_post=norm_mix_post, norm_ffn_pre=norm_ffn_pre, w_ffn_up=w_ffn_up, w_ffn_down=w_ffn_down, norm_ffn_post=norm_ffn_post)
    mom = dict(norm_mix_pre=m_norm_mix_pre, w_in=m_w_in, conv_dw_w=m_conv_dw_w, conv_dw_b=m_conv_dw_b, conv_ln_g=m_conv_ln_g, conv_ln_b=m_conv_ln_b, w_conv_branch=m_w_conv_branch, b_conv_branch=m_b_conv_branch, w_att_branch=m_w_att_branch, w_out=m_w_out, norm_mix_post=m_norm_mix_post, norm_ffn_pre=m_norm_ffn_pre, w_ffn_up=m_w_ffn_up, w_ffn_down=m_w_ffn_down, norm_ffn_post=m_norm_ffn_post)
    var = dict(norm_mix_pre=v_norm_mix_pre, w_in=v_w_in, conv_dw_w=v_conv_dw_w, conv_dw_b=v_conv_dw_b, conv_ln_g=v_conv_ln_g, conv_ln_b=v_conv_ln_b, w_conv_branch=v_w_conv_branch, b_conv_branch=v_b_conv_branch, w_att_branch=v_w_att_branch, w_out=v_w_out, norm_mix_post=v_norm_mix_post, norm_ffn_pre=v_norm_ffn_pre, w_ffn_up=v_w_ffn_up, w_ffn_down=v_w_ffn_down, norm_ffn_post=v_norm_ffn_post)
    order = list(weights)
    grads, deltas, new_m, new_v = {}, {}, {}, {}
    xs = x.reshape(SEQ, D_MODEL)
    tgt = loss_target.reshape(SEQ, D_MODEL)
    row = lambda a: a.reshape(1, -1)
    g1, g2, g3, g4 = (row(weights[n]) for n in ("norm_mix_pre", "norm_mix_post", "norm_ffn_pre", "norm_ffn_post"))
    ln_g, ln_b = row(conv_ln_g), row(conv_ln_b)

    def reduce_prepare(names, partial, tag):
        from_sib = sibling_exchange([partial[n] for n in names], "sibling_exchange_" + tag)
        return [add_halves(partial[n], r, "add_" + n) for n, r in zip(names, from_sib)]

    def reduce_finish(names, summed, from_chips, after, tag):
        halves = [sum_parts(s[1], r, after, "sum_" + n) for n, s, r in zip(names, summed, from_chips)]
        for n, a, b in zip(names, halves, sibling_swap(halves, "sibling_swap_" + tag)):
            grads[n], deltas[n], new_m[n], new_v[n] = adamw(weights[n], a, b, mom[n], var[n], "adamw_" + n)

    w_in_g, dw_g = all_gather_weights([w_in], conv_dw_w)
    w_dw_full = jnp.concatenate([dw_g[k] for k in range(N_CHIPS)], axis=1)
    h1, ci, q, k, v, gc, ga = in_proj_fwd(xs, g1, w_in_g)
    u1, u3, w_out_g, w_cb_g, w_ab_g = conv_fwd(ci, w_dw_full, row(conv_dw_b), ln_g, ln_b,
                                               [w_out, w_conv_branch, w_att_branch])
    att, rc, w_up_g = attn_fwd(q, k, v, [w_ffn_up])
    w_out_g = w_out_g.reshape(D_MODEL, D_MODEL)
    co, ao, merged, mix, x2, h2 = mix_fwd(u3, att, gc, ga, xs, w_cb_g, row(b_conv_branch), w_ab_g, w_out_g, g2, g3)
    gate, up, act, w_down_g = ffn_up_fwd(h2, w_up_g, [w_ffn_down])
    w_down_g = w_down_g.reshape(D_FF, D_MODEL)
    dff, dy, loss_parts, dg4 = ffn_down_loss(act, w_down_g, x2, tgt, g4)

    partial = {}
    dgu = ffn_act_bwd(dff, w_down_g, gate, up)
    partial["w_ffn_down"] = weight_grad(act, dff, "dw_ffn_down", False, tk=UP_SHARD)
    dx2, dmix, dg3, dg2 = ffn_in_bwd(dgu, w_up_g, x2, mix, dy, g3, g2)
    partial["w_ffn_up"] = weight_grad(h2, dgu, "dw_ffn_up", True)
    dco, dao, dg, du3, datt, dbcb = merge_bwd(dmix, w_out_g, gc, ga, co, ao, w_cb_g, w_ab_g)
    partial["w_out"] = weight_grad(merged, dmix, "dw_out", False, tk=512)
    partial["w_conv_branch"] = weight_grad(u3, dco, "dw_conv_branch", True)
    partial["w_att_branch"] = weight_grad(att, dao, "dw_att_branch", True)
    summed = reduce_prepare(REST, partial, "rest")
    dci, ddw, dbdw, dlng, dlnb = conv_bwd(du3, u1, ci, w_dw_full, ln_g, ln_b)
    dqkv, *from_chips = attn_bwd(q, k, v, datt, rc, [s[0] for s in summed])
    dproj = (dci, dqkv, dg)
    partial["w_in"] = weight_grad_in(h1, dproj)
    summed_in = reduce_prepare(("w_in",), partial, "w_in")
    send, recv, part, land, token = scatter_start(summed_in[0][0])
    grad_x, dg1 = in_proj_bwd(dproj, w_in_g, xs, dx2, g1 + token[0:1, 0:1])
    reduce_finish(REST, summed, from_chips, token, "rest")
    from_chips_in = scatter_wait(send, recv, part, land, [dg1] + [new_v[n] for n in REST])
    reduce_finish(("w_in",), summed_in, [from_chips_in], token, "w_in")

    v512 = dict(conv_dw_b=dbdw, conv_ln_g=dlng, conv_ln_b=dlnb)
    v1024 = dict(norm_mix_pre=dg1, b_conv_branch=dbcb, norm_mix_post=dg2, norm_ffn_pre=dg3, norm_ffn_post=dg4)
    gsum = small_all_reduce(ddw, v512, v1024, loss_parts)
    loss = gsum[LOSS_ROW, 0]
    as_rows = lambda n, a: a if n == "conv_dw_w" else a.reshape(1, -1)
    small_names = ("conv_dw_w",) + VEC512 + VEC1024
    small = adamw_small(gsum, {n: tuple(as_rows(n, d[n]) for d in (weights, mom, var)) for n in small_names})
    for n in small_names:
        grads[n], deltas[n], new_m[n], new_v[n] = (a.reshape(weights[n].shape) for a in small[n])

    return (loss, grad_x.reshape(1, SEQ, D_MODEL), *[grads[n] for n in order], *[deltas[n] for n in order],
            *[new_m[n] for n in order], *[new_v[n] for n in order])
```

```python
import jax
import jax.numpy as jnp
from jax import lax
from jax.experimental import pallas as pl
from jax.experimental.pallas import tpu as pltpu

F32 = jnp.float32
MM = jnp.bfloat16

SEQ = 2048
D_MODEL = 1024
CONV_DIM = 512
ATT_DIM = 512
CONV_WIDTH = 31
D_FF = 2816
IN_COLS = 2 * CONV_DIM + 3 * ATT_DIM + 2 * D_MODEL
N_CHIPS = 4
IN_SHARD = IN_COLS // N_CHIPS
UP_SHARD = 2 * D_FF // N_CHIPS
BR_SHARD = D_MODEL // N_CHIPS
EPS = 1e-6
ATT_SCALE = 0.125

TM = 256
GLU_ROWS = 256
TQ = 128
CONV_TILE = 64
CONV_WIN = CONV_TILE + 32
VMEM_LIMIT = 56 * 1024 * 1024

ADAM_LR = 0.001
ADAM_B1 = 0.9
ADAM_B2 = 0.999
ADAM_EPS = 1e-08
ADAM_WD = 0.01
ADAM_STEP = 10

MESH = pl.DeviceIdType.MESH
ANY = pl.BlockSpec(memory_space=pl.ANY)
VMEM_SPEC = pl.BlockSpec(memory_space=pltpu.VMEM)

NT_DIMS = (((1,), (1,)), ((), ()))
TN_DIMS = (((0,), (0,)), ((), ()))

IN_PIECES = (("ci", 0, 1024), ("q", 1024, 1536), ("k", 1536, 2048), ("v", 2048, 2560),
             ("gc", 2560, 3584), ("ga", 3584, 4608))


def _params(sem=None, vmem=VMEM_LIMIT):
    return pltpu.CompilerParams(dimension_semantics=sem, vmem_limit_bytes=vmem)


def _dot(a, b):
    return jnp.dot(a, b, preferred_element_type=F32)


def _dot_nt(a, b):
    return lax.dot_general(a, b, NT_DIMS, preferred_element_type=F32)


def _dot_tn(a, b):
    return lax.dot_general(a, b, TN_DIMS, preferred_element_type=F32)


def _sigmoid(x):
    return 1.0 / (1.0 + jnp.exp(-x))


def _rms(x):
    r = lax.rsqrt(jnp.mean(x * x, axis=-1, keepdims=True) + EPS)
    return x * r, r


def _rms_bwd(dy_g, n, r):
    return r * (dy_g - n * jnp.mean(dy_g * n, axis=-1, keepdims=True))


def _row_tile_spec(width, tm=TM):
    return pl.BlockSpec((tm, width), lambda i: (i, 0))


def _full_spec(shape):
    nd = len(shape)
    return pl.BlockSpec(shape, lambda *_: (0,) * nd)


def _weight_spec(shape):
    nd = len(shape)
    return pl.BlockSpec(shape, lambda *_: (0,) * nd, pipeline_mode=pl.Buffered(1))


def _acc_rows(ref, val, first):
    @pl.when(first)
    def _():
        ref[...] = val

    @pl.when(jnp.logical_not(first))
    def _():
        ref[...] += val


def _gather_behind_grid(ag, n_steps, step=None):
    step = pl.program_id(0) if step is None else step
    pl.when(step == 0)(ag.start)
    pl.when(step == n_steps - 2)(ag.forward)
    return lambda: pl.when(step == n_steps - 1)(ag.finish)


def in_proj_fwd(x, g1, w_in_g, gather=()):
    ng = len(gather)
    nt = SEQ // TM

    def body(*refs):
        x_ref, g_ref, w_ref = refs[:3]
        h_ref, ci_ref, q_ref, k_ref, v_ref, gc_ref, ga_ref = refs[3 + ng:10 + ng]
        if ng:
            done = _gather_behind_grid(_Gather([s.shape for s in gather], refs[3:3 + ng],
                                               refs[10 + ng:10 + 2 * ng], refs[10 + 2 * ng:]), nt)
        n, _ = _rms(x_ref[...])
        h = (n * g_ref[...]).astype(MM)
        h_ref[...] = h
        outs = dict(ci=ci_ref, q=q_ref, k=k_ref, v=v_ref, gc=gc_ref, ga=ga_ref)
        for j in range(N_CHIPS):
            p = _dot(h, w_ref[j])
            g0 = j * IN_SHARD
            for name, s, e in IN_PIECES:
                lo, hi = max(s, g0), min(e, g0 + IN_SHARD)
                if lo < hi:
                    ref = outs[name]
                    part = p[:, lo - g0:hi - g0]
                    if name == "q":
                        part = part * ATT_SCALE
                    ref[:, lo - s:hi - s] = part.astype(ref.dtype)
        if ng:
            done()

    out_shape = [
        jax.ShapeDtypeStruct((SEQ, D_MODEL), MM),
        jax.ShapeDtypeStruct((SEQ, 2 * CONV_DIM), F32),
        jax.ShapeDtypeStruct((SEQ, ATT_DIM), MM),
        jax.ShapeDtypeStruct((SEQ, ATT_DIM), MM),
        jax.ShapeDtypeStruct((SEQ, ATT_DIM), MM),
        jax.ShapeDtypeStruct((SEQ, D_MODEL), F32),
        jax.ShapeDtypeStruct((SEQ, D_MODEL), F32),
    ]
    return pl.pallas_call(
        body, name="in_proj_fwd", grid=(nt,),
        out_shape=out_shape + _Gather.out_shapes(gather),
        in_specs=[_row_tile_spec(D_MODEL), _full_spec((1, D_MODEL)), _weight_spec(w_in_g.shape)] + [ANY] * ng,
        out_specs=[_row_tile_spec(s.shape[1]) for s in out_shape] + [ANY] * ng,
        scratch_shapes=_Gather.scratch(gather) if ng else [],
        compiler_params=_params(("arbitrary",)),
    )(x, g1, w_in_g, *gather)


def _shifted_sum(win, terms):
    by_rot = {}
    for m, coef in terms:
        by_rot.setdefault(m % 8, []).append((m // 8, coef))
    acc = None
    n = win.shape[0]
    for rot in sorted(by_rot):
        shifted = win if rot == 0 else pltpu.roll(win, n - rot, 0)
        for a, coef in by_rot[rot]:
            t = coef * shifted[8 * a:8 * a + CONV_TILE, :]
            acc = t if acc is None else acc + t
    return acc


def _glu_into(ci_ref, upad_ref):
    upad_ref[0:32, :] = jnp.zeros((32, CONV_DIM), F32)

    def step(i, c):
        t0 = pl.multiple_of(i * GLU_ROWS, GLU_ROWS)
        a = ci_ref[pl.ds(t0, GLU_ROWS), 0:CONV_DIM]
        b = ci_ref[pl.ds(t0, GLU_ROWS), CONV_DIM:2 * CONV_DIM]
        upad_ref[pl.ds(t0 + 32, GLU_ROWS), :] = a * _sigmoid(b)
        return c

    lax.fori_loop(0, SEQ // GLU_ROWS, step, 0)


def _layernorm_parts(u1):
    mu = jnp.mean(u1, axis=-1, keepdims=True)
    xc = u1 - mu
    rstd = lax.rsqrt(jnp.mean(xc * xc, axis=-1, keepdims=True) + EPS)
    return xc * rstd, rstd


def conv_fwd(ci, w_dw, b_dw, ln_g, ln_b, gather=()):
    ng = len(gather)
    n_tiles = SEQ // CONV_TILE

    def body(*refs):
        ci_ref, w_ref, b_ref, g_ref, bb_ref = refs[:5]
        u1_ref, u3_ref = refs[5 + ng:7 + ng]
        upad_ref = refs[7 + 2 * ng]
        if ng:
            ag = _Gather([s.shape for s in gather], refs[5:5 + ng], refs[7 + ng:7 + 2 * ng], refs[8 + 2 * ng:])
            ag.start()
        _glu_into(ci_ref, upad_ref)

        def step(i, c):
            if ng:
                pl.when(i == n_tiles - n_tiles // 4)(ag.forward)
            t0 = pl.multiple_of(i * CONV_TILE, CONV_TILE)
            win = upad_ref[pl.ds(t0, CONV_WIN), :]
            u1 = _shifted_sum(win, [(j + 2, w_ref[j:j + 1, :]) for j in range(CONV_WIDTH)]) + b_ref[...]
            u1_ref[pl.ds(t0, CONV_TILE), :] = u1
            xh, _ = _layernorm_parts(u1)
            u2 = xh * g_ref[...] + bb_ref[...]
            u3_ref[pl.ds(t0, CONV_TILE), :] = (u2 * _sigmoid(u2)).astype(MM)
            return c

        lax.fori_loop(0, n_tiles, step, 0)
        if ng:
            ag.finish()

    return pl.pallas_call(
        body, name="conv_fwd",
        out_shape=[jax.ShapeDtypeStruct((SEQ, CONV_DIM), F32), jax.ShapeDtypeStruct((SEQ, CONV_DIM), MM)]
                  + _Gather.out_shapes(gather),
        in_specs=[VMEM_SPEC] * 5 + [ANY] * ng, out_specs=[VMEM_SPEC] * 2 + [ANY] * ng,
        scratch_shapes=[pltpu.VMEM((SEQ + 32, CONV_DIM), F32)] + (_Gather.scratch(gather) if ng else []),
        compiler_params=_params(),
    )(ci, w_dw, b_dw, ln_g, ln_b, *gather)


def _softplus(z):
    return jnp.maximum(z, 0.0) + jnp.log(1.0 + jnp.exp(-jnp.abs(z)))


def _cumsum_weights(suffix, with_total):
    n = 256 if with_total else 128
    r = lax.broadcasted_iota(jnp.int32, (128, n), 0)
    c = lax.broadcasted_iota(jnp.int32, (128, n), 1)
    tri = (r >= c) if suffix else (r <= c)
    return jnp.logical_or(tri, c >= 128).astype(MM)


NO_SCORE = -1e30
N_KB = SEQ // TQ


def _score_bias(lane, row, i, j):
    keep = jnp.logical_and(i >= 0, jnp.logical_or(j < i, lane < row))
    return jnp.where(keep, 0.0, NO_SCORE)


def _block_pipeline(n_stages, descending, step, on_query_block=None):
    n_lag = n_stages - 1
    none = jnp.int32(-1)

    def shift(cur, lag):
        step([cur] + [(lag[2 * s], lag[2 * s + 1]) for s in range(n_lag)])
        return (cur[0], cur[1]) + tuple(lag[:-2])

    def outer(i, lag):
        if on_query_block is not None:
            on_query_block(i)

        def inner(n, lag):
            return shift((i, i - n if descending else n), lag)
        return lax.fori_loop(0, i + 1, inner, lag)

    lag = lax.fori_loop(0, N_KB, outer, (none,) * (2 * n_lag))
    lax.fori_loop(0, n_lag, lambda n, lag: shift((none, none), lag), lag)


def _head_masks():
    lane = lax.broadcasted_iota(jnp.int32, (TQ, 128), 1)
    row = lax.broadcasted_iota(jnp.int32, (TQ, 128), 0)
    return lane, row, lane < 64


def _pick_head(x, head0, h):
    zero = jnp.zeros_like(x)
    return jnp.where(head0, x, zero) if h == 0 else jnp.where(head0, zero, x)


N_PAIRS = ATT_DIM // 128


def attn_fwd(q, k, v, gather=()):
    ng = len(gather)

    def body(*refs):
        q_ref, k_ref, v_ref = refs[:3]
        o_ref, rc_ref = refs[3 + ng:5 + ng]
        acc_ref, r_ref, z_ref, spb_ref, ab_ref = refs[5 + 2 * ng:10 + 2 * ng]
        if ng:
            ag = _Gather([s.shape for s in gather], refs[3:3 + ng], refs[5 + ng:5 + 2 * ng], refs[10 + 2 * ng:])
            ag.start()
        lane, row, head0 = _head_masks()
        w = _cumsum_weights(suffix=True, with_total=True)
        acc_ref[...] = jnp.zeros_like(acc_ref)
        r_ref[...] = jnp.zeros_like(r_ref)
        rc_ref[...] = jnp.zeros_like(rc_ref)
        z_ref[...] = jnp.full(z_ref.shape, NO_SCORE, F32)
        spb_ref[...] = jnp.zeros_like(spb_ref)
        ab_ref[...] = jnp.zeros_like(ab_ref)

        def step(pairs):
            (i1, j1), (i2, j2), (i3, j3) = pairs
            q1, k1, q2, q3, k3 = (pl.multiple_of(jnp.maximum(b, 0) * TQ, TQ) for b in (i1, j1, i2, i3, j3))
            bias1 = _score_bias(lane, row, i1, j1)
            first2 = j2 == i2
            rc_rows = rc_ref[pl.ds(q2, TQ), :]
            for p in range(N_PAIRS):
                cols = slice(128 * p, 128 * (p + 1))
                qb = q_ref[pl.ds(q1, TQ), cols]
                kb = k_ref[pl.ds(k1, TQ), cols]
                vb = v_ref[pl.ds(k3, TQ), cols]
                for h in range(2):
                    hh = 2 * p + h
                    acc_ref[pl.ds(q3, TQ), cols] += _dot(ab_ref[hh], _pick_head(vb, head0, h))
                    r = _dot(spb_ref[hh], w)
                    r_in = jnp.where(first2, 0.0, r_ref[hh])
                    ab_ref[hh] = jnp.exp(z_ref[hh] - (r[:, :128] + r_in)).astype(MM)
                    rc_rows = jnp.where(jnp.logical_and(lane == 16 * hh + j2, i2 >= 0), r_in, rc_rows)
                    r_ref[hh] = r_in + r[:, 128:]
                    z = _dot_nt(_pick_head(qb, head0, h), kb) + bias1
                    z_ref[hh] = z
                    spb_ref[hh] = _softplus(z).astype(MM)
            rc_ref[pl.ds(q2, TQ), :] = rc_rows

        if ng:
            _block_pipeline(3, True, step, lambda i: pl.when(i == N_KB - 2)(ag.forward))
        else:
            _block_pipeline(3, True, step)
        o_ref[...] = acc_ref[...].astype(MM)
        if ng:
            ag.finish()

    out_shape = [jax.ShapeDtypeStruct((SEQ, ATT_DIM), MM), jax.ShapeDtypeStruct((SEQ, 128), F32)]
    out_shape += _Gather.out_shapes(gather)
    return pl.pallas_call(
        body, name="attn_fwd", out_shape=out_shape,
        in_specs=[VMEM_SPEC] * 3 + [ANY] * ng, out_specs=[VMEM_SPEC] * 2 + [ANY] * ng,
        scratch_shapes=[pltpu.VMEM((SEQ, ATT_DIM), F32), pltpu.VMEM((8, TQ, 128), F32),
                        pltpu.VMEM((8, TQ, 128), F32), pltpu.VMEM((8, TQ, 128), MM), pltpu.VMEM((8, TQ, 128), MM)]
                       + (_Gather.scratch(gather) if ng else []),
        compiler_params=_params(),
    )(q, k, v, *gather)


def mix_fwd(u3, att, gc, ga, x, w_cb_g, b_cb, w_ab_g, w_out_g, g2, g3, gather=()):
    ng = len(gather)
    nt = SEQ // TM

    def body(*refs):
        u_ref, a_ref, gc_ref, ga_ref, x_ref, wcb_ref, bcb_ref, wab_ref, wout_ref, g2_ref, g3_ref = refs[:11]
        co_ref, ao_ref, mg_ref, mix_ref, x2_ref, h2_ref = refs[11 + ng:17 + ng]
        if ng:
            done = _gather_behind_grid(_Gather([s.shape for s in gather], refs[11:11 + ng],
                                               refs[17 + ng:17 + 2 * ng], refs[17 + 2 * ng:]), nt)
        u = u_ref[...]
        a = a_ref[...]
        co = jnp.concatenate([_dot(u, wcb_ref[j]) for j in range(N_CHIPS)], axis=1) + bcb_ref[...]
        ao = jnp.concatenate([_dot(a, wab_ref[j]) for j in range(N_CHIPS)], axis=1)
        co_ref[...] = co.astype(MM)
        ao_ref[...] = ao.astype(MM)
        merged = (_sigmoid(gc_ref[...]) * co + _sigmoid(ga_ref[...]) * ao).astype(MM)
        mg_ref[...] = merged
        mix = _dot(merged, wout_ref[...])
        mix_ref[...] = mix
        n2, _ = _rms(mix)
        x2 = x_ref[...] + n2 * g2_ref[...]
        x2_ref[...] = x2
        n3, _ = _rms(x2)
        h2_ref[...] = (n3 * g3_ref[...]).astype(MM)
        if ng:
            done()

    out_shape = [
        jax.ShapeDtypeStruct((SEQ, D_MODEL), MM), jax.ShapeDtypeStruct((SEQ, D_MODEL), MM),
        jax.ShapeDtypeStruct((SEQ, D_MODEL), MM), jax.ShapeDtypeStruct((SEQ, D_MODEL), F32),
        jax.ShapeDtypeStruct((SEQ, D_MODEL), F32), jax.ShapeDtypeStruct((SEQ, D_MODEL), MM),
    ]
    vec = _full_spec((1, D_MODEL))
    return pl.pallas_call(
        body, name="mix_fwd", grid=(nt,),
        out_shape=out_shape + _Gather.out_shapes(gather),
        in_specs=[_row_tile_spec(CONV_DIM), _row_tile_spec(ATT_DIM), _row_tile_spec(D_MODEL),
                  _row_tile_spec(D_MODEL), _row_tile_spec(D_MODEL), _weight_spec(w_cb_g.shape), vec,
                  _weight_spec(w_ab_g.shape), _weight_spec(w_out_g.shape), vec, vec] + [ANY] * ng,
        out_specs=[_row_tile_spec(D_MODEL)] * 6 + [ANY] * ng,
        scratch_shapes=_Gather.scratch(gather) if ng else [],
        compiler_params=_params(("arbitrary",)),
    )(u3, att, gc, ga, x, w_cb_g, b_cb, w_ab_g, w_out_g, g2, g3, *gather)


def ffn_up_fwd(h2, w_up_g, gather=()):
    ng = len(gather)
    nt = SEQ // TM

    def body(*refs):
        h_ref, wg_ref, wu_ref = refs[:3]
        gate_ref, up_ref, act_ref = refs[3 + ng:6 + ng]
        if ng:
            done = _gather_behind_grid(_Gather([s.shape for s in gather], refs[3:3 + ng],
                                               refs[6 + ng:6 + 2 * ng], refs[6 + 2 * ng:]),
                                       2 * nt, pl.program_id(0) * nt + pl.program_id(1))
        h = h_ref[...]
        gate = _dot(h, wg_ref[0])
        up = _dot(h, wu_ref[0])
        gate_ref[...] = gate.astype(MM)
        up_ref[...] = up.astype(MM)
        act_ref[...] = (gate * _sigmoid(gate) * up).astype(MM)
        if ng:
            done()

    tile = pl.BlockSpec((TM, UP_SHARD), lambda n, i: (i, n))
    act = jax.ShapeDtypeStruct((SEQ, D_FF), MM)
    return pl.pallas_call(
        body, name="ffn_up_fwd", grid=(2, nt), out_shape=[act, act, act] + _Gather.out_shapes(gather),
        in_specs=[pl.BlockSpec((TM, D_MODEL), lambda n, i: (i, 0)),
                  pl.BlockSpec((1, D_MODEL, UP_SHARD), lambda n, i: (n, 0, 0)),
                  pl.BlockSpec((1, D_MODEL, UP_SHARD), lambda n, i: (n + 2, 0, 0))] + [ANY] * ng,
        out_specs=[tile, tile, tile] + [ANY] * ng,
        scratch_shapes=_Gather.scratch(gather) if ng else [],
        compiler_params=_params(("arbitrary", "arbitrary")),
    )(h2, w_up_g, w_up_g, *gather)


def ffn_down_loss(act, w_down_g, x2, target, g4):
    def body(act_ref, wd_ref, x2_ref, t_ref, g_ref, dff_ref, dy_ref, loss_ref, dg_ref):
        ff = _dot(act_ref[...], wd_ref[...])
        n4, r4 = _rms(ff)
        g4v = g_ref[...]
        err = x2_ref[...] + n4 * g4v - t_ref[...]
        row_loss = jnp.mean(err * err, axis=-1, keepdims=True)
        loss_ref[...] = jnp.zeros((8, 128), F32) + 0.5 * jnp.sum(row_loss, axis=0, keepdims=True)
        dy = err * (1.0 / D_MODEL)
        dy_ref[...] = dy
        dff_ref[...] = _rms_bwd(dy * g4v, n4, r4).astype(MM)
        _acc_rows(dg_ref, jnp.sum(dy * n4, axis=0, keepdims=True), pl.program_id(0) == 0)

    nt = SEQ // TM
    vec = _full_spec((1, D_MODEL))
    return pl.pallas_call(
        body, name="ffn_down_loss", grid=(nt,),
        out_shape=(jax.ShapeDtypeStruct((SEQ, D_MODEL), MM), jax.ShapeDtypeStruct((SEQ, D_MODEL), F32),
                   jax.ShapeDtypeStruct((nt * 8, 128), F32), jax.ShapeDtypeStruct((1, D_MODEL), F32)),
        in_specs=[_row_tile_spec(D_FF), _weight_spec(w_down_g.shape), _row_tile_spec(D_MODEL),
                  _row_tile_spec(D_MODEL), vec],
        out_specs=[_row_tile_spec(D_MODEL), _row_tile_spec(D_MODEL),
                   pl.BlockSpec((8, 128), lambda i: (i, 0)), vec],
        compiler_params=_params(("arbitrary",)),
    )(act, w_down_g, x2, target, g4)


def ffn_act_bwd(dff, w_down_g, gate, up):
    def body(dff_ref, wd_ref, gate_ref, up_ref, dgu_ref):
        dact = _dot_nt(dff_ref[...], wd_ref[...])
        gate = gate_ref[...].astype(F32)
        sg = _sigmoid(gate)
        dgu_ref[:, 0:D_FF] = (dact * up_ref[...].astype(F32) * (sg * (1.0 + gate * (1.0 - sg)))).astype(MM)
        dgu_ref[:, D_FF:2 * D_FF] = (dact * (gate * sg)).astype(MM)

    return pl.pallas_call(
        body, name="ffn_act_bwd", grid=(SEQ // TM,),
        out_shape=jax.ShapeDtypeStruct((SEQ, 2 * D_FF), MM),
        in_specs=[_row_tile_spec(D_MODEL), _weight_spec(w_down_g.shape), _row_tile_spec(D_FF), _row_tile_spec(D_FF)],
        out_specs=_row_tile_spec(2 * D_FF),
        compiler_params=_params(("arbitrary",)),
    )(dff, w_down_g, gate, up)


def ffn_in_bwd(dgu, w_up_g, x2, mix, dy, g3, g2):
    def body(dgu_ref, w_ref, x2_ref, mix_ref, dy_ref, g3_ref, g2_ref, dx2_ref, dmix_ref, dg3_ref, dg2_ref):
        dh2 = None
        for j in range(N_CHIPS):
            t = _dot_nt(dgu_ref[:, j * UP_SHARD:(j + 1) * UP_SHARD], w_ref[j])
            dh2 = t if dh2 is None else dh2 + t
        first = pl.program_id(0) == 0
        n3, r3 = _rms(x2_ref[...])
        dx2 = dy_ref[...] + _rms_bwd(dh2 * g3_ref[...], n3, r3)
        dx2_ref[...] = dx2
        _acc_rows(dg3_ref, jnp.sum(dh2 * n3, axis=0, keepdims=True), first)
        n2, r2 = _rms(mix_ref[...])
        dmix_ref[...] = _rms_bwd(dx2 * g2_ref[...], n2, r2).astype(MM)
        _acc_rows(dg2_ref, jnp.sum(dx2 * n2, axis=0, keepdims=True), first)

    vec = _full_spec((1, D_MODEL))
    return pl.pallas_call(
        body, name="ffn_in_bwd", grid=(SEQ // TM,),
        out_shape=(jax.ShapeDtypeStruct((SEQ, D_MODEL), F32), jax.ShapeDtypeStruct((SEQ, D_MODEL), MM),
                   jax.ShapeDtypeStruct((1, D_MODEL), F32), jax.ShapeDtypeStruct((1, D_MODEL), F32)),
        in_specs=[_row_tile_spec(2 * D_FF), _weight_spec(w_up_g.shape), _row_tile_spec(D_MODEL),
                  _row_tile_spec(D_MODEL), _row_tile_spec(D_MODEL), vec, vec],
        out_specs=[_row_tile_spec(D_MODEL), _row_tile_spec(D_MODEL), vec, vec],
        compiler_params=_params(("arbitrary",)),
    )(dgu, w_up_g, x2, mix, dy, g3, g2)


def merge_bwd(dmix, w_out_g, gc, ga, co, ao, w_cb_g, w_ab_g):
    def body(dmix_ref, wout_ref, gc_ref, ga_ref, co_ref, ao_ref, wcb_ref, wab_ref,
             dco_ref, dao_ref, dg_ref, du3_ref, datt_ref, dbcb_ref):
        dm = _dot_nt(dmix_ref[...], wout_ref[...])
        sgc = _sigmoid(gc_ref[...])
        sga = _sigmoid(ga_ref[...])
        dco = dm * sgc
        dao = dm * sga
        dg_ref[:, 0:D_MODEL] = (dm * co_ref[...].astype(F32) * (sgc * (1.0 - sgc))).astype(MM)
        dg_ref[:, D_MODEL:2 * D_MODEL] = (dm * ao_ref[...].astype(F32) * (sga * (1.0 - sga))).astype(MM)
        _acc_rows(dbcb_ref, jnp.sum(dco, axis=0, keepdims=True), pl.program_id(0) == 0)
        dco_ref[...] = dco.astype(MM)
        dao_ref[...] = dao.astype(MM)
        du3 = None
        datt = None
        for j in range(N_CHIPS):
            cols = slice(j * BR_SHARD, (j + 1) * BR_SHARD)
            t = _dot_nt(dco_ref[:, cols], wcb_ref[j])
            s = _dot_nt(dao_ref[:, cols], wab_ref[j])
            du3 = t if du3 is None else du3 + t
            datt = s if datt is None else datt + s
        du3_ref[...] = du3
        datt_ref[...] = datt.astype(MM)

    wide = _row_tile_spec(D_MODEL)
    return pl.pallas_call(
        body, name="merge_bwd", grid=(SEQ // TM,),
        out_shape=(jax.ShapeDtypeStruct((SEQ, D_MODEL), MM), jax.ShapeDtypeStruct((SEQ, D_MODEL), MM),
                   jax.ShapeDtypeStruct((SEQ, 2 * D_MODEL), MM),
                   jax.ShapeDtypeStruct((SEQ, CONV_DIM), F32), jax.ShapeDtypeStruct((SEQ, ATT_DIM), MM),
                   jax.ShapeDtypeStruct((1, D_MODEL), F32)),
        in_specs=[wide, _weight_spec(w_out_g.shape), wide, wide, wide, wide,
                  _weight_spec(w_cb_g.shape), _weight_spec(w_ab_g.shape)],
        out_specs=[wide, wide, _row_tile_spec(2 * D_MODEL), _row_tile_spec(CONV_DIM), _row_tile_spec(ATT_DIM),
                   _full_spec((1, D_MODEL))],
        compiler_params=_params(("arbitrary",)),
    )(dmix, w_out_g, gc, ga, co, ao, w_cb_g, w_ab_g)


def conv_bwd(du3, u1, ci, w_dw, ln_g, ln_b):
    def body(du3_ref, u1_ref, ci_ref, w_ref, g_ref, bb_ref,
             dci_ref, dw_ref, dbdw_ref, dg_ref, db_ref, upad_ref, dpad_ref, dwacc_ref, vacc_ref):
        _glu_into(ci_ref, upad_ref)
        dpad_ref[SEQ:SEQ + 32, :] = jnp.zeros((32, CONV_DIM), F32)
        dwacc_ref[...] = jnp.zeros_like(dwacc_ref)
        vacc_ref[...] = jnp.zeros_like(vacc_ref)

        def fold8(t):
            s = t[0:8, :]
            for r in range(1, CONV_TILE // 8):
                s = s + t[8 * r:8 * r + 8, :]
            return s

        def pass1(i, c):
            t0 = pl.multiple_of(i * CONV_TILE, CONV_TILE)
            xh, rstd = _layernorm_parts(u1_ref[pl.ds(t0, CONV_TILE), :])
            gv = g_ref[...]
            u2 = xh * gv + bb_ref[...]
            s2 = _sigmoid(u2)
            du2 = du3_ref[pl.ds(t0, CONV_TILE), :] * (s2 * (1.0 + u2 * (1.0 - s2)))
            wv = du2 * gv
            du1 = rstd * (wv - jnp.mean(wv, axis=-1, keepdims=True)
                          - xh * jnp.mean(wv * xh, axis=-1, keepdims=True))
            dpad_ref[pl.ds(t0, CONV_TILE), :] = du1
            vacc_ref[0] += fold8(du2 * xh)
            vacc_ref[1] += fold8(du2)
            vacc_ref[2] += fold8(du1)
            win = upad_ref[pl.ds(t0, CONV_WIN), :]
            n = win.shape[0]
            for rot in range(8):
                shifted = win if rot == 0 else pltpu.roll(win, n - rot, 0)
                for a in range(5):
                    j = 8 * a + rot - 2
                    if 0 <= j < CONV_WIDTH:
                        dwacc_ref[j] += fold8(du1 * shifted[8 * a:8 * a + CONV_TILE, :])
            return c

        lax.fori_loop(0, SEQ // CONV_TILE, pass1, 0)

        def pass2(i, c):
            t0 = pl.multiple_of(i * CONV_TILE, CONV_TILE)
            win = dpad_ref[pl.ds(t0, CONV_WIN), :]
            du0 = _shifted_sum(win, [(30 - j, w_ref[j:j + 1, :]) for j in range(CONV_WIDTH)])
            a = ci_ref[pl.ds(t0, CONV_TILE), 0:CONV_DIM]
            sb = _sigmoid(ci_ref[pl.ds(t0, CONV_TILE), CONV_DIM:2 * CONV_DIM])
            dci_ref[pl.ds(t0, CONV_TILE), 0:CONV_DIM] = (du0 * sb).astype(MM)
            dci_ref[pl.ds(t0, CONV_TILE), CONV_DIM:2 * CONV_DIM] = (du0 * a * (sb * (1.0 - sb))).astype(MM)
            return c

        lax.fori_loop(0, SEQ // CONV_TILE, pass2, 0)

        for j in range(CONV_WIDTH):
            dw_ref[j:j + 1, :] = jnp.sum(dwacc_ref[j], axis=0, keepdims=True)
        dw_ref[CONV_WIDTH:32, :] = jnp.zeros((32 - CONV_WIDTH, CONV_DIM), F32)
        dg_ref[...] = jnp.sum(vacc_ref[0], axis=0, keepdims=True)
        db_ref[...] = jnp.sum(vacc_ref[1], axis=0, keepdims=True)
        dbdw_ref[...] = jnp.sum(vacc_ref[2], axis=0, keepdims=True)

    vec = jax.ShapeDtypeStruct((1, CONV_DIM), F32)
    return pl.pallas_call(
        body, name="conv_bwd",
        out_shape=(jax.ShapeDtypeStruct((SEQ, 2 * CONV_DIM), MM), jax.ShapeDtypeStruct((32, CONV_DIM), F32),
                   vec, vec, vec),
        in_specs=[VMEM_SPEC] * 6, out_specs=[VMEM_SPEC] * 5,
        scratch_shapes=[pltpu.VMEM((SEQ + 32, CONV_DIM), F32), pltpu.VMEM((SEQ + 32, CONV_DIM), F32),
                        pltpu.VMEM((CONV_WIDTH, 8, CONV_DIM), F32), pltpu.VMEM((3, 8, CONV_DIM), F32)],
        compiler_params=_params(),
    )(du3, u1, ci, w_dw, ln_g, ln_b)


def attn_bwd(q, k, v, datt, rc, scatter=()):
    ns = len(scatter)

    def body(*refs):
        q_ref, k_ref, v_ref, do_ref, rc_ref = refs[:5]
        dqkv_ref = refs[5 + ns]
        (dqa_ref, dka_ref, dva_ref, pc_ref, z_ref, sig1_ref, sig2_ref, g_ref, spb_ref, gb_ref, ab_ref,
         dzb_ref) = refs[6 + 2 * ns:18 + 2 * ns]
        if ns:
            sc = _Scatter(refs[5:5 + ns], refs[6 + ns:6 + 2 * ns], refs[18 + 2 * ns:])
            sc.start()
        lane, row, head0 = _head_masks()
        for ref in (dqa_ref, dka_ref, dva_ref, pc_ref):
            ref[...] = jnp.zeros_like(ref)
        z_ref[...] = jnp.full(z_ref.shape, NO_SCORE, F32)
        for ref in (sig1_ref, sig2_ref, spb_ref, ab_ref, g_ref, gb_ref, dzb_ref):
            ref[...] = jnp.zeros_like(ref)
        w_suffix = _cumsum_weights(suffix=True, with_total=False)
        w_prefix = _cumsum_weights(suffix=False, with_total=True)

        def step(pairs):
            (ia, ja), (ib, jb), (ic, jc), (id_, jd) = pairs
            qa, ka, qb_, kb_, qc, kc, qd, kd = (pl.multiple_of(jnp.maximum(b, 0) * TQ, TQ)
                                                for b in (ia, ja, ib, jb, ic, jc, id_, jd))
            bias_a = _score_bias(lane, row, ia, ja)
            rc_rows = rc_ref[pl.ds(qb_, TQ), :]
            first_c = jc == 0
            for p in range(N_PAIRS):
                cols = slice(128 * p, 128 * (p + 1))
                q_a = q_ref[pl.ds(qa, TQ), cols]
                k_a = k_ref[pl.ds(ka, TQ), cols]
                do_b = do_ref[pl.ds(qb_, TQ), cols]
                v_b = v_ref[pl.ds(kb_, TQ), cols]
                do_c = do_ref[pl.ds(qc, TQ), cols]
                q_d = q_ref[pl.ds(qd, TQ), cols]
                k_d = k_ref[pl.ds(kd, TQ), cols]
                for h in range(2):
                    hh = 2 * p + h
                    dzb = dzb_ref[hh]
                    dqa_ref[pl.ds(qd, TQ), cols] += _dot(dzb, _pick_head(k_d, head0, h))
                    dka_ref[pl.ds(kd, TQ), cols] += _dot_tn(dzb, _pick_head(q_d, head0, h))
                    r = _dot(gb_ref[hh], w_prefix)
                    p_in = jnp.where(first_c, 0.0, pc_ref[hh])
                    dzb_ref[hh] = (g_ref[hh] - sig2_ref[hh] * (r[:, :128] + p_in)).astype(MM)
                    pc_ref[hh] = p_in + r[:, 128:]
                    dva_ref[pl.ds(kc, TQ), cols] += _dot_tn(ab_ref[hh], _pick_head(do_c, head0, h))
                    r_in = jnp.sum(jnp.where(lane == 16 * hh + jb, rc_rows, 0.0), axis=1, keepdims=True)
                    a = jnp.exp(z_ref[hh] - (_dot(spb_ref[hh], w_suffix) + r_in))
                    g = _dot_nt(_pick_head(do_b, head0, h), v_b) * a
                    ab_ref[hh] = a.astype(MM)
                    g_ref[hh] = g
                    gb_ref[hh] = g.astype(MM)
                    sig2_ref[hh] = sig1_ref[hh]
                    z = _dot_nt(_pick_head(q_a, head0, h), k_a) + bias_a
                    sp = _softplus(z)
                    sig1_ref[hh] = jnp.exp(z - sp)
                    z_ref[hh] = z
                    spb_ref[hh] = sp.astype(MM)

        _block_pipeline(4, False, step)
        dqkv_ref[:, 0:ATT_DIM] = (dqa_ref[...] * ATT_SCALE).astype(MM)
        dqkv_ref[:, ATT_DIM:2 * ATT_DIM] = dka_ref[...].astype(MM)
        dqkv_ref[:, 2 * ATT_DIM:3 * ATT_DIM] = dva_ref[...].astype(MM)
        if ns:
            sc.finish()

    out = jax.ShapeDtypeStruct((SEQ, 3 * ATT_DIM), MM)
    return pl.pallas_call(
        body, name="attn_bwd", out_shape=[out] + _Scatter.out_shapes(scatter),
        in_specs=[VMEM_SPEC] * 5 + [ANY] * ns, out_specs=[VMEM_SPEC] + [ANY] * ns,
        scratch_shapes=[pltpu.VMEM((SEQ, ATT_DIM), F32)] * 3 + [pltpu.VMEM((8, TQ, 128), F32)] * 5
                       + [pltpu.VMEM((8, TQ, 128), MM)] * 4 + (_Scatter.scratch(ns) if ns else []),
        compiler_params=_params(),
    )(q, k, v, datt, rc, *scatter)


DPROJ_PIECES = ((0, 1024), (1024, 2560), (2560, 4608))


def _dproj_segments(j):
    g0, g1 = j * IN_SHARD, (j + 1) * IN_SHARD
    segs = []
    for p, (s, e) in enumerate(DPROJ_PIECES):
        lo, hi = max(s, g0), min(e, g1)
        if lo < hi:
            segs.append((p, lo - s, lo - g0, hi - lo))
    return segs


def in_proj_bwd(pieces, w_in_g, x, dx2, g1, scatter=()):
    ns = len(scatter)
    nt = SEQ // TM

    def body(*refs):
        p_refs = refs[:3]
        w_ref, x_ref, dx2_ref, g_ref = refs[3:7]
        dx_ref, dg_ref = refs[7 + ns:9 + ns]
        if ns:
            sc = _Scatter(refs[7:7 + ns], refs[9 + ns:9 + 2 * ns], refs[9 + 2 * ns:])
            pl.when(pl.program_id(0) == 0)(sc.start)
        dh = None
        for j in range(N_CHIPS):
            for p, lo, off, width in _dproj_segments(j):
                t = _dot_nt(p_refs[p][:, lo:lo + width], w_ref[j, :, off:off + width])
                dh = t if dh is None else dh + t
        n1, r1 = _rms(x_ref[...])
        dx_ref[...] = dx2_ref[...] + _rms_bwd(dh * g_ref[...], n1, r1)
        _acc_rows(dg_ref, jnp.sum(dh * n1, axis=0, keepdims=True), pl.program_id(0) == 0)
        if ns:
            pl.when(pl.program_id(0) == nt - 1)(sc.finish)

    vec = _full_spec((1, D_MODEL))
    return pl.pallas_call(
        body, name="in_proj_bwd", grid=(nt,),
        out_shape=[jax.ShapeDtypeStruct((SEQ, D_MODEL), F32), jax.ShapeDtypeStruct((1, D_MODEL), F32)]
                  + _Scatter.out_shapes(scatter),
        in_specs=[_row_tile_spec(p.shape[1]) for p in pieces]
                 + [_weight_spec(w_in_g.shape), _row_tile_spec(D_MODEL), _row_tile_spec(D_MODEL), vec] + [ANY] * ns,
        out_specs=[_row_tile_spec(D_MODEL), vec] + [ANY] * ns,
        scratch_shapes=_Scatter.scratch(ns) if ns else [],
        compiler_params=_params(("arbitrary",)),
    )(*pieces, w_in_g, x, dx2, g1, *scatter)


def weight_grad_in(h1, pieces):
    kh = D_MODEL // 2

    def body(a_ref, p0_ref, p1_ref, p2_ref, o_ref):
        p_refs = (p0_ref, p1_ref, p2_ref)
        a = a_ref[...]
        for j in range(N_CHIPS):
            @pl.when(pl.program_id(1) == j)
            def _():
                for p, lo, off, width in _dproj_segments(j):
                    o_ref[0, 0, :, off:off + width] = _dot_tn(a, p_refs[p][:, lo:lo + width]).astype(MM)

    return pl.pallas_call(
        body, name="dw_in", grid=(2, N_CHIPS), out_shape=jax.ShapeDtypeStruct((N_CHIPS, 2, kh, IN_SHARD), MM),
        in_specs=[pl.BlockSpec((SEQ, kh), lambda h, j: (0, h))] + [_weight_spec(p.shape) for p in pieces],
        out_specs=pl.BlockSpec((1, 1, kh, IN_SHARD), lambda h, j: (j, h, 0, 0)),
        compiler_params=_params(("arbitrary", "arbitrary")),
    )(h1, *pieces)


def weight_grad(a, b, name, col_sharded, tk=None):
    kin, n = a.shape[1], b.shape[1]

    def body(a_ref, b_ref, o_ref):
        if col_sharded:
            o_ref[0, 0] = _dot_tn(a_ref[...], b_ref[...]).astype(MM)
        else:
            o_ref[...] = _dot_tn(a_ref[...], b_ref[...]).astype(MM)

    if col_sharded:
        kh, ns = kin // 2, n // N_CHIPS
        out = jax.ShapeDtypeStruct((N_CHIPS, 2, kh, ns), MM)
        grid = (2, N_CHIPS)
        in_specs = [pl.BlockSpec((SEQ, kh), lambda h, j: (0, h)), pl.BlockSpec((SEQ, ns), lambda h, j: (0, j))]
        out_spec = pl.BlockSpec((1, 1, kh, ns), lambda h, j: (j, h, 0, 0))
        sem = ("arbitrary", "arbitrary")
    else:
        out = jax.ShapeDtypeStruct((kin, n), MM)
        grid = (kin // tk,)
        in_specs = [pl.BlockSpec((SEQ, tk), lambda r: (0, r)), pl.BlockSpec((SEQ, n), lambda r: (0, 0))]
        out_spec = pl.BlockSpec((tk, n), lambda r: (r, 0))
        sem = ("arbitrary",)
    res = pl.pallas_call(
        body, name=name, grid=grid, out_shape=out, in_specs=in_specs, out_specs=out_spec,
        compiler_params=_params(sem),
    )(a, b)
    if not col_sharded:
        res = res.reshape(N_CHIPS, 2, kin // (2 * N_CHIPS), n)
    return res


def _place():
    x, y, c = lax.axis_index("x"), lax.axis_index("y"), lax.axis_index("c")
    chips = [(1 - x, y), (x, 1 - y), (1 - x, 1 - y)]
    return x, y, c, chips


def _rcopy(src, dst, send_sem, recv_sem, dev):
    return pltpu.make_async_remote_copy(src_ref=src, dst_ref=dst, send_sem=send_sem, recv_sem=recv_sem,
                                        device_id=dev, device_id_type=MESH)


class _Gather:
    def __init__(self, shapes, w, o, scratch):
        self.n, self.shapes, self.w, self.o = len(w), shapes, w, o
        self.send, self.recv, self.fsend, self.frecv, self.loc_in, self.loc_out = scratch[:6]
        self.raw, self.stage = scratch[6:6 + self.n], scratch[6 + self.n:]
        self.x, self.y, self.c, self.chips = _place()
        self.me = 2 * self.x + self.y
        self.sib = (self.x, self.y, 1 - self.c)
        self.pairs = [(j, t) for j in range(3) for t in range(self.n)]

    @staticmethod
    def scratch(shards):
        n = len(shards)
        sems = pltpu.SemaphoreType.DMA
        return ([sems((3 * n,)), sems((3 * n,)), sems((3 * n,)), sems((3 * n,)), sems((n,)), sems((n,))]
                + [pltpu.VMEM(s.shape, s.dtype) for s in shards] + [pltpu.VMEM(s.shape, MM) for s in shards])

    @staticmethod
    def out_shapes(shards):
        return [jax.ShapeDtypeStruct((N_CHIPS,) + s.shape, MM) for s in shards]

    def _half(self, t, k, cc):
        rh = self.shapes[t][0] // 2
        return self.o[t].at[k, pl.ds(cc * rh, rh), :]

    def _chip(self, j):
        cx, cy = self.chips[j]
        return 2 * cx + cy, (cx, cy, self.c)

    def local_in(self, t):
        return pltpu.make_async_copy(self.w[t], self.raw[t], self.loc_in.at[t])

    def local_out(self, t):
        return pltpu.make_async_copy(self.stage[t], self.o[t].at[self.me], self.loc_out.at[t])

    def first(self, j, t):
        rh = self.shapes[t][0] // 2
        i = j * self.n + t
        return _rcopy(self.stage[t].at[pl.ds(self.c * rh, rh), :], self._half(t, self.me, self.c),
                      self.send.at[i], self.recv.at[i], self._chip(j)[1])

    def arrived(self, j, t):
        k, dev = self._chip(j)
        i = j * self.n + t
        blk = self._half(t, k, self.c)
        return _rcopy(blk, blk, self.send.at[i], self.recv.at[i], dev)

    def passed(self, j, t, cc):
        i = j * self.n + t
        blk = self._half(t, self._chip(j)[0], cc)
        return _rcopy(blk, blk, self.fsend.at[i], self.frecv.at[i], self.sib)

    def start(self):
        for t in range(self.n):
            self.local_in(t).start()
        for t in range(self.n):
            self.local_in(t).wait()
            self.stage[t][...] = self.raw[t][...].astype(MM)
            self.local_out(t).start()
        for j, t in self.pairs:
            self.first(j, t).start()

    def forward(self):
        for j, t in self.pairs:
            self.arrived(j, t).wait_recv()
            self.passed(j, t, self.c).start()

    def finish(self):
        for j, t in self.pairs:
            self.passed(j, t, 1 - self.c).wait_recv()
        for j, t in self.pairs:
            self.first(j, t).wait_send()
            self.passed(j, t, self.c).wait_send()
        for t in range(self.n):
            self.local_out(t).wait()


def all_gather_weights(shards, small):
    n = len(shards)
    shapes = [s.shape for s in shards]

    def body(*refs):
        w = refs[:n]
        sm = refs[n]
        o = refs[n + 1:2 * n + 1]
        osm = refs[2 * n + 1]
        ssend, srecv, sloc = refs[2 * n + 2:2 * n + 5]
        g = _Gather(shapes, w, o, refs[2 * n + 5:])
        own = pltpu.make_async_copy(sm, osm.at[g.me], sloc)
        own.start()
        g.start()
        small_cps = [_rcopy(sm, osm.at[g.me], ssend.at[j], srecv.at[j], g._chip(j)[1]) for j in range(3)]
        for cp in small_cps:
            cp.start()
        g.forward()
        g.finish()
        for j in range(3):
            k, dev = g._chip(j)
            _rcopy(sm, osm.at[k], ssend.at[j], srecv.at[j], dev).wait_recv()
            small_cps[j].wait_send()
        own.wait()

    out_shape = _Gather.out_shapes(shards)
    out_shape.append(jax.ShapeDtypeStruct((N_CHIPS,) + small.shape, small.dtype))
    sems = pltpu.SemaphoreType.DMA
    return pl.pallas_call(
        body, name="all_gather_weights", out_shape=out_shape,
        in_specs=[ANY] * (n + 1), out_specs=[ANY] * (n + 1),
        scratch_shapes=[sems((3,)), sems((3,)), sems] + _Gather.scratch(shards),
        compiler_params=_params(),
    )(*shards, small)


def sibling_exchange(grads, name):
    n = len(grads)

    def body(*refs):
        g = refs[:n]
        o = refs[n:2 * n]
        send, recv = refs[2 * n:]
        x, y, c, _ = _place()
        cps = [_rcopy(g[t].at[:, 1 - c], o[t], send.at[t], recv.at[t], (x, y, 1 - c)) for t in range(n)]
        for cp in cps:
            cp.start()
        for cp in cps:
            cp.wait()

    sems = pltpu.SemaphoreType.DMA
    return pl.pallas_call(
        body, name=name,
        out_shape=[jax.ShapeDtypeStruct((a.shape[0],) + a.shape[2:], a.dtype) for a in grads],
        in_specs=[ANY] * n, out_specs=[ANY] * n, scratch_shapes=[sems((n,)), sems((n,))],
    )(*grads)


class _Scatter:
    def __init__(self, p, o, sems):
        self.n, self.p, self.o = len(p), p, o
        self.send, self.recv = sems
        _, _, self.c, self.chips = _place()

    @staticmethod
    def scratch(n):
        sems = pltpu.SemaphoreType.DMA
        return [sems((3 * n,)), sems((3 * n,))]

    @staticmethod
    def out_shapes(parts):
        return [jax.ShapeDtypeStruct((3,) + a.shape[1:], a.dtype) for a in parts]

    def copies(self):
        cps = []
        for j, (cx, cy) in enumerate(self.chips):
            for t in range(self.n):
                i = j * self.n + t
                cps.append(_rcopy(self.p[t].at[2 * cx + cy], self.o[t].at[j], self.send.at[i], self.recv.at[i],
                                  (cx, cy, self.c)))
        return cps

    def start(self):
        for cp in self.copies():
            cp.start()

    def finish(self):
        for cp in self.copies():
            cp.wait()


HBM_SPEC = pl.BlockSpec(memory_space=pltpu.HBM)
SEM_SPEC = pl.BlockSpec(memory_space=pltpu.SEMAPHORE)
DATAFLOW = pltpu.SideEffectType.DATAFLOW_SIDE_EFFECTING


def _scatter_copies(p_ref, land_ref, send, recv):
    _, _, c, chips = _place()
    return [_rcopy(p_ref.at[2 * cx + cy], land_ref.at[j], send.at[j], recv.at[j], (cx, cy, c))
            for j, (cx, cy) in enumerate(chips)]


def scatter_start(parts, tag):
    n = len(parts)
    lands = [lax.empty((3,) + p.shape[1:], p.dtype) for p in parts]

    def body(*refs):
        p, land = refs[:n], refs[n:2 * n]
        send, recv = refs[2 * n:3 * n], refs[3 * n:4 * n]
        for t in range(n):
            for cp in _scatter_copies(p[t], land[t], send[t], recv[t]):
                cp.start()
        token = refs[6 * n]
        token[...] = jnp.zeros_like(token)

    sems = [pltpu.SemaphoreType.DMA((3,))] * (2 * n)
    hbm = [pltpu.HBM(a.shape, a.dtype) for a in list(parts) + lands]
    res = pl.pallas_call(
        body, name="scatter_start_" + tag,
        out_shape=sems + hbm + [jax.ShapeDtypeStruct((8, 128), F32)],
        in_specs=[HBM_SPEC] * (2 * n), out_specs=[SEM_SPEC] * (2 * n) + [HBM_SPEC] * (2 * n) + [VMEM_SPEC],
        input_output_aliases={i: 2 * n + i for i in range(2 * n)},
        compiler_params=pltpu.CompilerParams(has_side_effects=DATAFLOW),
    )(*[pltpu.with_memory_space_constraint(a, pltpu.HBM) for a in list(parts) + lands])
    return res[:-1], res[-1]


def scatter_wait(state, after, tag):
    n = len(state) // 4
    sems, bufs = state[:2 * n], state[2 * n:]

    def body(*refs):
        p, land = refs[:n], refs[n:2 * n]
        send, recv = refs[2 * n:3 * n], refs[3 * n:4 * n]
        for t in range(n):
            for cp in _scatter_copies(p[t], land[t], send[t], recv[t]):
                cp.wait_send()
                cp.wait_recv()

    return pl.pallas_call(
        body, name="scatter_wait_" + tag, out_shape=[pltpu.HBM(a.shape, a.dtype) for a in bufs],
        in_specs=[HBM_SPEC] * (2 * n) + [SEM_SPEC] * (2 * n) + [ANY] * len(after), out_specs=[HBM_SPEC] * (2 * n),
        input_output_aliases={i: i for i in range(2 * n)},
        compiler_params=pltpu.CompilerParams(has_side_effects=DATAFLOW),
    )(*bufs, *sems, *after)[n:]


def sibling_swap(halves, name):
    n = len(halves)

    def body(*refs):
        h = refs[:n]
        o = refs[n:2 * n]
        send, recv = refs[2 * n:]
        x, y, c, _ = _place()
        cps = [_rcopy(h[t], o[t], send.at[t], recv.at[t], (x, y, 1 - c)) for t in range(n)]
        for cp in cps:
            cp.start()
        for cp in cps:
            cp.wait()

    sems = pltpu.SemaphoreType.DMA
    return pl.pallas_call(
        body, name=name, out_shape=[jax.ShapeDtypeStruct(a.shape, a.dtype) for a in halves],
        in_specs=[ANY] * n, out_specs=[ANY] * n, scratch_shapes=[sems((n,)), sems((n,))],
    )(*halves)


def small_all_reduce(ddw, v512, v1024, loss_parts):
    rows, width = PACK_ROWS, 512
    n512, n1024 = len(VEC512), len(VEC1024)

    def body(*refs):
        ddw_ref = refs[0]
        a_refs = refs[1:1 + n512]
        b_refs = refs[1 + n512:1 + n512 + n1024]
        lp_ref, o_ref, p_ref, gath_ref, send, recv = refs[1 + n512 + n1024:]
        p_ref[...] = jnp.zeros_like(p_ref)
        p_ref[0:32, :] = ddw_ref[...]
        p_ref[LOSS_ROW:LOSS_ROW + 1, 0:128] = jnp.sum(lp_ref[...], axis=0, keepdims=True) * 0.125
        for i, r in enumerate(a_refs):
            p_ref[32 + i:33 + i, :] = r[...]
        for i, r in enumerate(b_refs):
            base = 32 + n512 + 2 * i
            p_ref[base:base + 1, :] = r[:, 0:512]
            p_ref[base + 1:base + 2, :] = r[:, 512:1024]
        x, y, c, _ = _place()
        me = 4 * x + 2 * y + c
        gath_ref[me] = p_ref[...]
        cps = []
        for k in range(1, 8):
            dx, dy, dc = (k >> 2) & 1, (k >> 1) & 1, k & 1
            px = 1 - x if dx else x
            py = 1 - y if dy else y
            pc = 1 - c if dc else c
            cps.append(_rcopy(p_ref, gath_ref.at[me], send.at[k - 1], recv.at[k - 1], (px, py, pc)))
        for cp in cps:
            cp.start()
        for k in range(1, 8):
            dx, dy, dc = (k >> 2) & 1, (k >> 1) & 1, k & 1
            px = 1 - x if dx else x
            py = 1 - y if dy else y
            pc = 1 - c if dc else c
            _rcopy(p_ref, gath_ref.at[4 * px + 2 * py + pc], send.at[k - 1], recv.at[k - 1], (px, py, pc)).wait_recv()
        for cp in cps:
            cp.wait_send()
        total = gath_ref[0]
        for d in range(1, 8):
            total = total + gath_ref[d]
        o_ref[...] = total

    sems = pltpu.SemaphoreType.DMA
    n_in = 2 + n512 + n1024
    return pl.pallas_call(
        body, name="small_all_reduce", out_shape=jax.ShapeDtypeStruct((rows, width), F32),
        in_specs=[VMEM_SPEC] * n_in, out_specs=VMEM_SPEC,
        scratch_shapes=[pltpu.VMEM((rows, width), F32), pltpu.VMEM((8, rows, width), F32), sems((7,)), sems((7,))],
    )(ddw, *[v512[n] for n in VEC512], *[v1024[n] for n in VEC1024], loss_parts)


def _row_block(r):
    for tr in (512, 352, 256, 128):
        if r % tr == 0:
            return tr
    return r


def add_halves(g, recv, name):
    _, _, r, w = g.shape
    tr = _row_block(r)

    def body(g0_ref, g1_ref, r_ref, ob_ref, own_ref):
        k = pl.program_id(1)
        c = lax.axis_index("c")
        me = 2 * lax.axis_index("x") + lax.axis_index("y")
        t = jnp.where(c == 0, g0_ref[0, 0], g1_ref[0, 0]).astype(F32) + r_ref[0].astype(F32)
        ob_ref[0] = t.astype(MM)
        mine = jnp.where(k == me, t, 0.0)

        @pl.when(k == 0)
        def _():
            own_ref[...] = mine

        @pl.when(k != 0)
        def _():
            own_ref[...] += mine

    return pl.pallas_call(
        body, name=name, grid=(r // tr, N_CHIPS),
        in_specs=[pl.BlockSpec((1, 1, tr, w), lambda i, k: (k, 0, i, 0)),
                  pl.BlockSpec((1, 1, tr, w), lambda i, k: (k, 1, i, 0)),
                  pl.BlockSpec((1, tr, w), lambda i, k: (k, i, 0))],
        out_specs=[pl.BlockSpec((1, tr, w), lambda i, k: (k, i, 0)),
                   pl.BlockSpec((tr, w), lambda i, k: (i, 0))],
        out_shape=(jax.ShapeDtypeStruct((N_CHIPS, r, w), MM), jax.ShapeDtypeStruct((r, w), F32)),
        compiler_params=_params(("arbitrary", "arbitrary")),
    )(g, g, recv)


def sum_parts(own, rin, after, name):
    _, r, w = rin.shape
    tr = _row_block(r)

    def body(o_ref, r_ref, after_ref, out_ref):
        out_ref[...] = ((o_ref[...] + r_ref[0].astype(F32)) + r_ref[1].astype(F32)) + r_ref[2].astype(F32)

    return pl.pallas_call(
        body, name=name, grid=(r // tr,), out_shape=jax.ShapeDtypeStruct((r, w), F32),
        in_specs=[pl.BlockSpec((tr, w), lambda i: (i, 0)), pl.BlockSpec((3, tr, w), lambda i: (0, i, 0)),
                  _full_spec((8, 128))],
        out_specs=pl.BlockSpec((tr, w), lambda i: (i, 0)),
        compiler_params=_params(("arbitrary",)),
    )(own, rin, after)


def _adamw_math(w, g, m, v):
    mn = ADAM_B1 * m + (1.0 - ADAM_B1) * g
    vn = ADAM_B2 * v + (1.0 - ADAM_B2) * (g * g)
    m_hat = mn / (1.0 - ADAM_B1 ** ADAM_STEP)
    v_hat = vn / (1.0 - ADAM_B2 ** ADAM_STEP)
    return -ADAM_LR * (m_hat / (jnp.sqrt(v_hat) + ADAM_EPS) + ADAM_WD * w), mn, vn


def adamw(w, mine, other, m, v, name):
    r, c = w.shape
    rh = r // 2
    tr = _row_block(rh)
    if c >= 1024 and tr % 512 == 0:
        tr = 256
    nb = rh // tr

    def body(w_ref, a_ref, b_ref, m_ref, v_ref, go_ref, d_ref, mo_ref, vo_ref):
        gv = jnp.where(lax.axis_index("c") == pl.program_id(0), a_ref[...], b_ref[...])
        go_ref[...] = gv
        d_ref[...], mo_ref[...], vo_ref[...] = _adamw_math(w_ref[...], gv, m_ref[...], v_ref[...])

    spec = pl.BlockSpec((tr, c), lambda h, i: (h * nb + i, 0))
    half = pl.BlockSpec((tr, c), lambda h, i: (i, 0))
    out = jax.ShapeDtypeStruct((r, c), F32)
    return pl.pallas_call(
        body, name=name, grid=(2, nb), out_shape=(out, out, out, out),
        in_specs=[spec, half, half, spec, spec], out_specs=[spec] * 4,
        compiler_params=_params(("arbitrary", "arbitrary")),
    )(w, mine, other, m, v)


def adamw_small(gsum, params):
    names = list(params)
    flat = [a for n in names for a in params[n]]

    def body(*refs):
        g_ref = refs[0]
        ins = refs[1:1 + 3 * len(names)]
        outs = refs[1 + 3 * len(names):]
        me = 2 * lax.axis_index("x") + lax.axis_index("y")
        for i, n in enumerate(names):
            w_ref, m_ref, v_ref = ins[3 * i:3 * i + 3]
            go_ref, d_ref, mo_ref, vo_ref = outs[4 * i:4 * i + 4]
            if n == "conv_dw_w":
                gv = jnp.zeros((CONV_WIDTH, 128), F32)
                for k in range(N_CHIPS):
                    gv = gv + jnp.where(me == k, g_ref[0:CONV_WIDTH, 128 * k:128 * (k + 1)], 0.0)
            elif n in VEC512:
                r0 = 32 + VEC512.index(n)
                gv = g_ref[r0:r0 + 1, :]
            else:
                r0 = 32 + len(VEC512) + 2 * VEC1024.index(n)
                gv = jnp.concatenate([g_ref[r0:r0 + 1, :], g_ref[r0 + 1:r0 + 2, :]], axis=1)
            go_ref[...] = gv
            d_ref[...], mo_ref[...], vo_ref[...] = _adamw_math(w_ref[...], gv, m_ref[...], v_ref[...])

    out_shape = [jax.ShapeDtypeStruct(params[n][0].shape, F32) for n in names for _ in range(4)]
    res = pl.pallas_call(
        body, name="adamw_small", out_shape=out_shape,
        in_specs=[VMEM_SPEC] * (1 + len(flat)), out_specs=[VMEM_SPEC] * len(out_shape),
        compiler_params=_params(),
    )(gsum, *flat)
    return {n: res[4 * i:4 * i + 4] for i, n in enumerate(names)}


REST = ("w_ffn_up", "w_ffn_down", "w_out", "w_conv_branch", "w_att_branch")
VEC512 = ("conv_dw_b", "conv_ln_g", "conv_ln_b")
VEC1024 = ("norm_mix_pre", "b_conv_branch", "norm_mix_post", "norm_ffn_pre", "norm_ffn_post")
PACK_ROWS = 48
LOSS_ROW = 47


def kernel(x, norm_mix_pre, w_in, conv_dw_w, conv_dw_b, conv_ln_g, conv_ln_b, w_conv_branch, b_conv_branch, w_att_branch, w_out, norm_mix_post, norm_ffn_pre, w_ffn_up, w_ffn_down, norm_ffn_post, loss_target, m_norm_mix_pre, m_w_in, m_conv_dw_w, m_conv_dw_b, m_conv_ln_g, m_conv_ln_b, m_w_conv_branch, m_b_conv_branch, m_w_att_branch, m_w_out, m_norm_mix_post, m_norm_ffn_pre, m_w_ffn_up, m_w_ffn_down, m_norm_ffn_post, v_norm_mix_pre, v_w_in, v_conv_dw_w, v_conv_dw_b, v_conv_ln_g, v_conv_ln_b, v_w_conv_branch, v_b_conv_branch, v_w_att_branch, v_w_out, v_norm_mix_post, v_norm_ffn_pre, v_w_ffn_up, v_w_ffn_down, v_norm_ffn_post):
    weights = dict(norm_mix_pre=norm_mix_pre, w_in=w_in, conv_dw_w=conv_dw_w, conv_dw_b=conv_dw_b, conv_ln_g=conv_ln_g, conv_ln_b=conv_ln_b, w_conv_branch=w_conv_branch, b_conv_branch=b_conv_branch, w_att_branch=w_att_branch, w_out=w_out, norm_mix_post=norm_mix_post, norm_ffn_pre=norm_ffn_pre, w_ffn_up=w_ffn_up, w_ffn_down=w_ffn_down, norm_ffn_post=norm_ffn_post)
    mom = dict(norm_mix_pre=m_norm_mix_pre, w_in=m_w_in, conv_dw_w=m_conv_dw_w, conv_dw_b=m_conv_dw_b, conv_ln_g=m_conv_ln_g, conv_ln_b=m_conv_ln_b, w_conv_branch=m_w_conv_branch, b_conv_branch=m_b_conv_branch, w_att_branch=m_w_att_branch, w_out=m_w_out, norm_mix_post=m_norm_mix_post, norm_ffn_pre=m_norm_ffn_pre, w_ffn_up=m_w_ffn_up, w_ffn_down=m_w_ffn_down, norm_ffn_post=m_norm_ffn_post)
    var = dict(norm_mix_pre=v_norm_mix_pre, w_in=v_w_in, conv_dw_w=v_conv_dw_w, conv_dw_b=v_conv_dw_b, conv_ln_g=v_conv_ln_g, conv_ln_b=v_conv_ln_b, w_conv_branch=v_w_conv_branch, b_conv_branch=v_b_conv_branch, w_att_branch=v_w_att_branch, w_out=v_w_out, norm_mix_post=v_norm_mix_post, norm_ffn_pre=v_norm_ffn_pre, w_ffn_up=v_w_ffn_up, w_ffn_down=v_w_ffn_down, norm_ffn_post=v_norm_ffn_post)
    order = list(weights)
    grads, deltas, new_m, new_v = {}, {}, {}, {}
    xs = x.reshape(SEQ, D_MODEL)
    tgt = loss_target.reshape(SEQ, D_MODEL)
    row = lambda a: a.reshape(1, -1)
    g1, g2, g3, g4 = (row(weights[n]) for n in ("norm_mix_pre", "norm_mix_post", "norm_ffn_pre", "norm_ffn_post"))
    ln_g, ln_b = row(conv_ln_g), row(conv_ln_b)

    def reduce_prepare(names, partial, tag):
        from_sib = sibling_exchange([partial[n] for n in names], "sibling_exchange_" + tag)
        return [add_halves(partial[n], r, "add_" + n) for n, r in zip(names, from_sib)]

    def reduce_finish(names, summed, from_chips, after, tag):
        halves = [sum_parts(s[1], r, after, "sum_" + n) for n, s, r in zip(names, summed, from_chips)]
        for n, a, b in zip(names, halves, sibling_swap(halves, "sibling_swap_" + tag)):
            grads[n], deltas[n], new_m[n], new_v[n] = adamw(weights[n], a, b, mom[n], var[n], "adamw_" + n)

    w_in_g, dw_g = all_gather_weights([w_in], conv_dw_w)
    w_dw_full = jnp.concatenate([dw_g[k] for k in range(N_CHIPS)], axis=1)
    h1, ci, q, k, v, gc, ga = in_proj_fwd(xs, g1, w_in_g)
    u1, u3, w_out_g, w_cb_g, w_ab_g = conv_fwd(ci, w_dw_full, row(conv_dw_b), ln_g, ln_b,
                                               [w_out, w_conv_branch, w_att_branch])
    att, rc, w_up_g = attn_fwd(q, k, v, [w_ffn_up])
    w_out_g = w_out_g.reshape(D_MODEL, D_MODEL)
    co, ao, merged, mix, x2, h2 = mix_fwd(u3, att, gc, ga, xs, w_cb_g, row(b_conv_branch), w_ab_g, w_out_g, g2, g3)
    gate, up, act, w_down_g = ffn_up_fwd(h2, w_up_g, [w_ffn_down])
    w_down_g = w_down_g.reshape(D_FF, D_MODEL)
    dff, dy, loss_parts, dg4 = ffn_down_loss(act, w_down_g, x2, tgt, g4)

    partial = {}
    dgu = ffn_act_bwd(dff, w_down_g, gate, up)
    partial["w_ffn_down"] = weight_grad(act, dff, "dw_ffn_down", False, tk=UP_SHARD)
    dx2, dmix, dg3, dg2 = ffn_in_bwd(dgu, w_up_g, x2, mix, dy, g3, g2)
    partial["w_ffn_up"] = weight_grad(h2, dgu, "dw_ffn_up", True)
    dco, dao, dg, du3, datt, dbcb = merge_bwd(dmix, w_out_g, gc, ga, co, ao, w_cb_g, w_ab_g)
    partial["w_out"] = weight_grad(merged, dmix, "dw_out", False, tk=512)
    partial["w_conv_branch"] = weight_grad(u3, dco, "dw_conv_branch", True)
    partial["w_att_branch"] = weight_grad(att, dao, "dw_att_branch", True)
    summed = reduce_prepare(REST, partial, "rest")
    state, token = scatter_start([s[0] for s in summed], "rest")
    zero = token[0:1, 0:1]
    dci, ddw, dbdw, dlng, dlnb = conv_bwd(du3, u1, ci, w_dw_full, ln_g + zero, ln_b)
    (dqkv,) = attn_bwd(q, k, v, datt, rc + zero)
    from_chips = scatter_wait(state, [dci, dqkv], "rest")
    dproj = (dci, dqkv, dg)
    partial["w_in"] = weight_grad_in(h1, dproj)
    summed_in = reduce_prepare(("w_in",), partial, "w_in")
    state, token = scatter_start([summed_in[0][0]], "w_in")
    grad_x, dg1 = in_proj_bwd(dproj, w_in_g, xs, dx2, g1 + token[0:1, 0:1])
    reduce_finish(REST, summed, from_chips, token, "rest")
    from_chips_in = scatter_wait(state, [dg1] + [new_v[n] for n in REST], "w_in")
    reduce_finish(("w_in",), summed_in, from_chips_in, token, "w_in")

    v512 = dict(conv_dw_b=dbdw, conv_ln_g=dlng, conv_ln_b=dlnb)
    v1024 = dict(norm_mix_pre=dg1, b_conv_branch=dbcb, norm_mix_post=dg2, norm_ffn_pre=dg3, norm_ffn_post=dg4)
    gsum = small_all_reduce(ddw, v512, v1024, loss_parts)
    loss = gsum[LOSS_ROW, 0]
    as_rows = lambda n, a: a if n == "conv_dw_w" else a.reshape(1, -1)
    small_names = ("conv_dw_w",) + VEC512 + VEC1024
    small = adamw_small(gsum, {n: tuple(as_rows(n, d[n]) for d in (weights, mom, var)) for n in small_names})
    for n in small_names:
        grads[n], deltas[n], new_m[n], new_v[n] = (a.reshape(weights[n].shape) for a in small[n])

    return (loss, grad_x.reshape(1, SEQ, D_MODEL), *[grads[n] for n in order], *[deltas[n] for n in order],
            *[new_m[n] for n in order], *[new_v[n] for n in order])
```

```python
import jax
import jax.numpy as jnp
from jax import lax
from jax.experimental import pallas as pl
from jax.experimental.pallas import tpu as pltpu

F32 = jnp.float32
MM = jnp.bfloat16

SEQ = 2048
D_MODEL = 1024
CONV_DIM = 512
ATT_DIM = 512
CONV_WIDTH = 31
D_FF = 2816
IN_COLS = 2 * CONV_DIM + 3 * ATT_DIM + 2 * D_MODEL
N_CHIPS = 4
IN_SHARD = IN_COLS // N_CHIPS
UP_SHARD = 2 * D_FF // N_CHIPS
BR_SHARD = D_MODEL // N_CHIPS
EPS = 1e-6
ATT_SCALE = 0.125

TM = 256
GLU_ROWS = 256
TQ = 128
CONV_TILE = 64
CONV_WIN = CONV_TILE + 32
VMEM_LIMIT = 56 * 1024 * 1024

ADAM_LR = 0.001
ADAM_B1 = 0.9
ADAM_B2 = 0.999
ADAM_EPS = 1e-08
ADAM_WD = 0.01
ADAM_STEP = 10

MESH = pl.DeviceIdType.MESH
ANY = pl.BlockSpec(memory_space=pl.ANY)
VMEM_SPEC = pl.BlockSpec(memory_space=pltpu.VMEM)

NT_DIMS = (((1,), (1,)), ((), ()))
TN_DIMS = (((0,), (0,)), ((), ()))

IN_PIECES = (("ci", 0, 1024), ("q", 1024, 1536), ("k", 1536, 2048), ("v", 2048, 2560),
             ("gc", 2560, 3584), ("ga", 3584, 4608))


def _params(sem=None, vmem=VMEM_LIMIT):
    return pltpu.CompilerParams(dimension_semantics=sem, vmem_limit_bytes=vmem)


def _dot(a, b):
    return jnp.dot(a, b, preferred_element_type=F32)


def _dot_nt(a, b):
    return lax.dot_general(a, b, NT_DIMS, preferred_element_type=F32)


def _dot_tn(a, b):
    return lax.dot_general(a, b, TN_DIMS, preferred_element_type=F32)


def _sigmoid(x):
    return 1.0 / (1.0 + jnp.exp(-x))


def _rms(x):
    r = lax.rsqrt(jnp.mean(x * x, axis=-1, keepdims=True) + EPS)
    return x * r, r


def _rms_bwd(dy_g, n, r):
    return r * (dy_g - n * jnp.mean(dy_g * n, axis=-1, keepdims=True))


def _row_tile_spec(width, tm=TM):
    return pl.BlockSpec((tm, width), lambda i: (i, 0))


def _full_spec(shape):
    nd = len(shape)
    return pl.BlockSpec(shape, lambda *_: (0,) * nd)


def _weight_spec(shape):
    nd = len(shape)
    return pl.BlockSpec(shape, lambda *_: (0,) * nd, pipeline_mode=pl.Buffered(1))


def _acc_rows(ref, val, first):
    @pl.when(first)
    def _():
        ref[...] = val

    @pl.when(jnp.logical_not(first))
    def _():
        ref[...] += val


def _gather_behind_grid(ag, n_steps, step=None):
    step = pl.program_id(0) if step is None else step
    pl.when(step == 0)(ag.start)
    pl.when(step == n_steps - 2)(ag.forward)
    return lambda: pl.when(step == n_steps - 1)(ag.finish)


def in_proj_fwd(x, g1, w_in_g, gather=()):
    ng = len(gather)
    nt = SEQ // TM

    def body(*refs):
        x_ref, g_ref, w_ref = refs[:3]
        h_ref, ci_ref, q_ref, k_ref, v_ref, gc_ref, ga_ref = refs[3 + ng:10 + ng]
        if ng:
            done = _gather_behind_grid(_Gather([s.shape for s in gather], refs[3:3 + ng],
                                               refs[10 + ng:10 + 2 * ng], refs[10 + 2 * ng:]), nt)
        n, _ = _rms(x_ref[...])
        h = (n * g_ref[...]).astype(MM)
        h_ref[...] = h
        outs = dict(ci=ci_ref, q=q_ref, k=k_ref, v=v_ref, gc=gc_ref, ga=ga_ref)
        for j in range(N_CHIPS):
            p = _dot(h, w_ref[j])
            g0 = j * IN_SHARD
            for name, s, e in IN_PIECES:
                lo, hi = max(s, g0), min(e, g0 + IN_SHARD)
                if lo < hi:
                    ref = outs[name]
                    part = p[:, lo - g0:hi - g0]
                    if name == "q":
                        part = part * ATT_SCALE
                    ref[:, lo - s:hi - s] = part.astype(ref.dtype)
        if ng:
            done()

    out_shape = [
        jax.ShapeDtypeStruct((SEQ, D_MODEL), MM),
        jax.ShapeDtypeStruct((SEQ, 2 * CONV_DIM), F32),
        jax.ShapeDtypeStruct((SEQ, ATT_DIM), MM),
        jax.ShapeDtypeStruct((SEQ, ATT_DIM), MM),
        jax.ShapeDtypeStruct((SEQ, ATT_DIM), MM),
        jax.ShapeDtypeStruct((SEQ, D_MODEL), F32),
        jax.ShapeDtypeStruct((SEQ, D_MODEL), F32),
    ]
    return pl.pallas_call(
        body, name="in_proj_fwd", grid=(nt,),
        out_shape=out_shape + _Gather.out_shapes(gather),
        in_specs=[_row_tile_spec(D_MODEL), _full_spec((1, D_MODEL)), _weight_spec(w_in_g.shape)] + [ANY] * ng,
        out_specs=[_row_tile_spec(s.shape[1]) for s in out_shape] + [ANY] * ng,
        scratch_shapes=_Gather.scratch(gather) if ng else [],
        compiler_params=_params(("arbitrary",)),
    )(x, g1, w_in_g, *gather)


def _shifted_sum(win, terms):
    by_rot = {}
    for m, coef in terms:
        by_rot.setdefault(m % 8, []).append((m // 8, coef))
    acc = None
    n = win.shape[0]
    for rot in sorted(by_rot):
        shifted = win if rot == 0 else pltpu.roll(win, n - rot, 0)
        for a, coef in by_rot[rot]:
            t = coef * shifted[8 * a:8 * a + CONV_TILE, :]
            acc = t if acc is None else acc + t
    return acc


def _glu_into(ci_ref, upad_ref):
    upad_ref[0:32, :] = jnp.zeros((32, CONV_DIM), F32)

    def step(i, c):
        t0 = pl.multiple_of(i * GLU_ROWS, GLU_ROWS)
        a = ci_ref[pl.ds(t0, GLU_ROWS), 0:CONV_DIM]
        b = ci_ref[pl.ds(t0, GLU_ROWS), CONV_DIM:2 * CONV_DIM]
        upad_ref[pl.ds(t0 + 32, GLU_ROWS), :] = a * _sigmoid(b)
        return c

    lax.fori_loop(0, SEQ // GLU_ROWS, step, 0)


def _layernorm_parts(u1):
    mu = jnp.mean(u1, axis=-1, keepdims=True)
    xc = u1 - mu
    rstd = lax.rsqrt(jnp.mean(xc * xc, axis=-1, keepdims=True) + EPS)
    return xc * rstd, rstd


def conv_fwd(ci, w_dw, b_dw, ln_g, ln_b, gather=()):
    ng = len(gather)
    n_tiles = SEQ // CONV_TILE

    def body(*refs):
        ci_ref, w_ref, b_ref, g_ref, bb_ref = refs[:5]
        u1_ref, u3_ref = refs[5 + ng:7 + ng]
        upad_ref = refs[7 + 2 * ng]
        if ng:
            ag = _Gather([s.shape for s in gather], refs[5:5 + ng], refs[7 + ng:7 + 2 * ng], refs[8 + 2 * ng:])
            ag.start()
        _glu_into(ci_ref, upad_ref)

        def step(i, c):
            if ng:
                pl.when(i == n_tiles - n_tiles // 4)(ag.forward)
            t0 = pl.multiple_of(i * CONV_TILE, CONV_TILE)
            win = upad_ref[pl.ds(t0, CONV_WIN), :]
            u1 = _shifted_sum(win, [(j + 2, w_ref[j:j + 1, :]) for j in range(CONV_WIDTH)]) + b_ref[...]
            u1_ref[pl.ds(t0, CONV_TILE), :] = u1
            xh, _ = _layernorm_parts(u1)
            u2 = xh * g_ref[...] + bb_ref[...]
            u3_ref[pl.ds(t0, CONV_TILE), :] = (u2 * _sigmoid(u2)).astype(MM)
            return c

        lax.fori_loop(0, n_tiles, step, 0)
        if ng:
            ag.finish()

    return pl.pallas_call(
        body, name="conv_fwd",
        out_shape=[jax.ShapeDtypeStruct((SEQ, CONV_DIM), F32), jax.ShapeDtypeStruct((SEQ, CONV_DIM), MM)]
                  + _Gather.out_shapes(gather),
        in_specs=[VMEM_SPEC] * 5 + [ANY] * ng, out_specs=[VMEM_SPEC] * 2 + [ANY] * ng,
        scratch_shapes=[pltpu.VMEM((SEQ + 32, CONV_DIM), F32)] + (_Gather.scratch(gather) if ng else []),
        compiler_params=_params(),
    )(ci, w_dw, b_dw, ln_g, ln_b, *gather)


def _softplus(z):
    return jnp.maximum(z, 0.0) + jnp.log(1.0 + jnp.exp(-jnp.abs(z)))


def _cumsum_weights(suffix, with_total):
    n = 256 if with_total else 128
    r = lax.broadcasted_iota(jnp.int32, (128, n), 0)
    c = lax.broadcasted_iota(jnp.int32, (128, n), 1)
    tri = (r >= c) if suffix else (r <= c)
    return jnp.logical_or(tri, c >= 128).astype(MM)


NO_SCORE = -1e30
N_KB = SEQ // TQ


def _score_bias(lane, row, i, j):
    keep = jnp.logical_and(i >= 0, jnp.logical_or(j < i, lane < row))
    return jnp.where(keep, 0.0, NO_SCORE)


def _block_pipeline(n_stages, descending, step, on_query_block=None):
    n_lag = n_stages - 1
    none = jnp.int32(-1)

    def shift(cur, lag):
        step([cur] + [(lag[2 * s], lag[2 * s + 1]) for s in range(n_lag)])
        return (cur[0], cur[1]) + tuple(lag[:-2])

    def outer(i, lag):
        if on_query_block is not None:
            on_query_block(i)

        def inner(n, lag):
            return shift((i, i - n if descending else n), lag)
        return lax.fori_loop(0, i + 1, inner, lag)

    lag = lax.fori_loop(0, N_KB, outer, (none,) * (2 * n_lag))
    lax.fori_loop(0, n_lag, lambda n, lag: shift((none, none), lag), lag)


def _head_masks():
    lane = lax.broadcasted_iota(jnp.int32, (TQ, 128), 1)
    row = lax.broadcasted_iota(jnp.int32, (TQ, 128), 0)
    return lane, row, lane < 64


def _pick_head(x, head0, h):
    zero = jnp.zeros_like(x)
    return jnp.where(head0, x, zero) if h == 0 else jnp.where(head0, zero, x)


N_PAIRS = ATT_DIM // 128


def _split_heads(src_ref, dst_ref):
    _, _, head0 = _head_masks()

    def block(b, c):
        r0 = pl.multiple_of(b * TQ, TQ)
        d0 = pl.multiple_of(b * 2 * TQ, 2 * TQ)
        for p in range(N_PAIRS):
            x = src_ref[pl.ds(r0, TQ), 128 * p:128 * (p + 1)]
            for h in range(2):
                dst_ref[p, pl.ds(d0 + TQ * h, TQ), :] = _pick_head(x, head0, h)
        return c

    lax.fori_loop(0, N_KB, block, 0)


def attn_fwd(q, k, v, gather=()):
    ng = len(gather)

    def body(*refs):
        q_ref, k_ref, v_ref = refs[:3]
        o_ref, rc_ref = refs[3 + ng:5 + ng]
        acc_ref, r_ref, z_ref, spb_ref, ab_ref, qm_ref, vm_ref = refs[5 + 2 * ng:12 + 2 * ng]
        if ng:
            ag = _Gather([s.shape for s in gather], refs[3:3 + ng], refs[5 + ng:5 + 2 * ng], refs[12 + 2 * ng:])
            ag.start()
        lane, row, _ = _head_masks()
        w = _cumsum_weights(suffix=True, with_total=True)
        _split_heads(q_ref, qm_ref)
        _split_heads(v_ref, vm_ref)
        acc_ref[...] = jnp.zeros_like(acc_ref)
        r_ref[...] = jnp.zeros_like(r_ref)
        rc_ref[...] = jnp.zeros_like(rc_ref)
        z_ref[...] = jnp.full(z_ref.shape, NO_SCORE, F32)
        spb_ref[...] = jnp.zeros_like(spb_ref)
        ab_ref[...] = jnp.zeros_like(ab_ref)

        def step(pairs):
            (i1, j1), (i2, j2), (i3, j3) = pairs
            k1, q2, q3 = (pl.multiple_of(jnp.maximum(b, 0) * TQ, TQ) for b in (j1, i2, i3))
            q1, k3 = (pl.multiple_of(jnp.maximum(b, 0) * 2 * TQ, 2 * TQ) for b in (i1, j3))
            bias1 = _score_bias(lane, row, i1, j1)
            first2 = j2 == i2
            rc_rows = rc_ref[pl.ds(q2, TQ), :]
            for p in range(N_PAIRS):
                cols = slice(128 * p, 128 * (p + 1))
                kb = k_ref[pl.ds(k1, TQ), cols]
                acc_ref[pl.ds(q3, TQ), cols] += _dot(ab_ref[p], vm_ref[p, pl.ds(k3, 2 * TQ), :])
                for h in range(2):
                    hh = 2 * p + h
                    r = _dot(spb_ref[hh], w)
                    r_in = jnp.where(first2, 0.0, r_ref[hh])
                    ab_ref[p, :, 128 * h:128 * (h + 1)] = jnp.exp(z_ref[hh] - (r[:, :128] + r_in)).astype(MM)
                    rc_rows = jnp.where(jnp.logical_and(lane == 16 * hh + j2, i2 >= 0), r_in, rc_rows)
                    r_ref[hh] = r_in + r[:, 128:]
                    z = _dot_nt(qm_ref[p, pl.ds(q1 + TQ * h, TQ), :], kb) + bias1
                    z_ref[hh] = z
                    spb_ref[hh] = _softplus(z).astype(MM)
            rc_ref[pl.ds(q2, TQ), :] = rc_rows

        if ng:
            _block_pipeline(3, True, step, lambda i: pl.when(i == N_KB - 2)(ag.forward))
        else:
            _block_pipeline(3, True, step)
        o_ref[...] = acc_ref[...].astype(MM)
        if ng:
            ag.finish()

    out_shape = [jax.ShapeDtypeStruct((SEQ, ATT_DIM), MM), jax.ShapeDtypeStruct((SEQ, 128), F32)]
    out_shape += _Gather.out_shapes(gather)
    return pl.pallas_call(
        body, name="attn_fwd", out_shape=out_shape,
        in_specs=[VMEM_SPEC] * 3 + [ANY] * ng, out_specs=[VMEM_SPEC] * 2 + [ANY] * ng,
        scratch_shapes=[pltpu.VMEM((SEQ, ATT_DIM), F32), pltpu.VMEM((8, TQ, 128), F32),
                        pltpu.VMEM((8, TQ, 128), F32), pltpu.VMEM((8, TQ, 128), MM),
                        pltpu.VMEM((N_PAIRS, TQ, 256), MM), pltpu.VMEM((N_PAIRS, 2 * SEQ, 128), MM),
                        pltpu.VMEM((N_PAIRS, 2 * SEQ, 128), MM)]
                       + (_Gather.scratch(gather) if ng else []),
        compiler_params=_params(),
    )(q, k, v, *gather)


def mix_fwd(u3, att, gc, ga, x, w_cb_g, b_cb, w_ab_g, w_out_g, g2, g3, gather=()):
    ng = len(gather)
    nt = SEQ // TM

    def body(*refs):
        u_ref, a_ref, gc_ref, ga_ref, x_ref, wcb_ref, bcb_ref, wab_ref, wout_ref, g2_ref, g3_ref = refs[:11]
        co_ref, ao_ref, mg_ref, mix_ref, x2_ref, h2_ref = refs[11 + ng:17 + ng]
        if ng:
            done = _gather_behind_grid(_Gather([s.shape for s in gather], refs[11:11 + ng],
                                               refs[17 + ng:17 + 2 * ng], refs[17 + 2 * ng:]), nt)
        u = u_ref[...]
        a = a_ref[...]
        co = jnp.concatenate([_dot(u, wcb_ref[j]) for j in range(N_CHIPS)], axis=1) + bcb_ref[...]
        ao = jnp.concatenate([_dot(a, wab_ref[j]) for j in range(N_CHIPS)], axis=1)
        co_ref[...] = co.astype(MM)
        ao_ref[...] = ao.astype(MM)
        merged = (_sigmoid(gc_ref[...]) * co + _sigmoid(ga_ref[...]) * ao).astype(MM)
        mg_ref[...] = merged
        mix = _dot(merged, wout_ref[...])
        mix_ref[...] = mix
        n2, _ = _rms(mix)
        x2 = x_ref[...] + n2 * g2_ref[...]
        x2_ref[...] = x2
        n3, _ = _rms(x2)
        h2_ref[...] = (n3 * g3_ref[...]).astype(MM)
        if ng:
            done()

    out_shape = [
        jax.ShapeDtypeStruct((SEQ, D_MODEL), MM), jax.ShapeDtypeStruct((SEQ, D_MODEL), MM),
        jax.ShapeDtypeStruct((SEQ, D_MODEL), MM), jax.ShapeDtypeStruct((SEQ, D_MODEL), F32),
        jax.ShapeDtypeStruct((SEQ, D_MODEL), F32), jax.ShapeDtypeStruct((SEQ, D_MODEL), MM),
    ]
    vec = _full_spec((1, D_MODEL))
    return pl.pallas_call(
        body, name="mix_fwd", grid=(nt,),
        out_shape=out_shape + _Gather.out_shapes(gather),
        in_specs=[_row_tile_spec(CONV_DIM), _row_tile_spec(ATT_DIM), _row_tile_spec(D_MODEL),
                  _row_tile_spec(D_MODEL), _row_tile_spec(D_MODEL), _weight_spec(w_cb_g.shape), vec,
                  _weight_spec(w_ab_g.shape), _weight_spec(w_out_g.shape), vec, vec] + [ANY] * ng,
        out_specs=[_row_tile_spec(D_MODEL)] * 6 + [ANY] * ng,
        scratch_shapes=_Gather.scratch(gather) if ng else [],
        compiler_params=_params(("arbitrary",)),
    )(u3, att, gc, ga, x, w_cb_g, b_cb, w_ab_g, w_out_g, g2, g3, *gather)


def ffn_up_fwd(h2, w_up_g, gather=()):
    ng = len(gather)
    nt = SEQ // TM

    def body(*refs):
        h_ref, wg_ref, wu_ref = refs[:3]
        gate_ref, up_ref, act_ref = refs[3 + ng:6 + ng]
        if ng:
            done = _gather_behind_grid(_Gather([s.shape for s in gather], refs[3:3 + ng],
                                               refs[6 + ng:6 + 2 * ng], refs[6 + 2 * ng:]),
                                       2 * nt, pl.program_id(0) * nt + pl.program_id(1))
        h = h_ref[...]
        gate = _dot(h, wg_ref[0])
        up = _dot(h, wu_ref[0])
        gate_ref[...] = gate.astype(MM)
        up_ref[...] = up.astype(MM)
        act_ref[...] = (gate * _sigmoid(gate) * up).astype(MM)
        if ng:
            done()

    tile = pl.BlockSpec((TM, UP_SHARD), lambda n, i: (i, n))
    act = jax.ShapeDtypeStruct((SEQ, D_FF), MM)
    return pl.pallas_call(
        body, name="ffn_up_fwd", grid=(2, nt), out_shape=[act, act, act] + _Gather.out_shapes(gather),
        in_specs=[pl.BlockSpec((TM, D_MODEL), lambda n, i: (i, 0)),
                  pl.BlockSpec((1, D_MODEL, UP_SHARD), lambda n, i: (n, 0, 0)),
                  pl.BlockSpec((1, D_MODEL, UP_SHARD), lambda n, i: (n + 2, 0, 0))] + [ANY] * ng,
        out_specs=[tile, tile, tile] + [ANY] * ng,
        scratch_shapes=_Gather.scratch(gather) if ng else [],
        compiler_params=_params(("arbitrary", "arbitrary")),
    )(h2, w_up_g, w_up_g, *gather)


def ffn_down_loss(act, w_down_g, x2, target, g4):
    def body(act_ref, wd_ref, x2_ref, t_ref, g_ref, dff_ref, dy_ref, loss_ref, dg_ref):
        ff = _dot(act_ref[...], wd_ref[...])
        n4, r4 = _rms(ff)
        g4v = g_ref[...]
        err = x2_ref[...] + n4 * g4v - t_ref[...]
        row_loss = jnp.mean(err * err, axis=-1, keepdims=True)
        loss_ref[...] = jnp.zeros((8, 128), F32) + 0.5 * jnp.sum(row_loss, axis=0, keepdims=True)
        dy = err * (1.0 / D_MODEL)
        dy_ref[...] = dy
        dff_ref[...] = _rms_bwd(dy * g4v, n4, r4).astype(MM)
        _acc_rows(dg_ref, jnp.sum(dy * n4, axis=0, keepdims=True), pl.program_id(0) == 0)

    nt = SEQ // TM
    vec = _full_spec((1, D_MODEL))
    return pl.pallas_call(
        body, name="ffn_down_loss", grid=(nt,),
        out_shape=(jax.ShapeDtypeStruct((SEQ, D_MODEL), MM), jax.ShapeDtypeStruct((SEQ, D_MODEL), F32),
                   jax.ShapeDtypeStruct((nt * 8, 128), F32), jax.ShapeDtypeStruct((1, D_MODEL), F32)),
        in_specs=[_row_tile_spec(D_FF), _weight_spec(w_down_g.shape), _row_tile_spec(D_MODEL),
                  _row_tile_spec(D_MODEL), vec],
        out_specs=[_row_tile_spec(D_MODEL), _row_tile_spec(D_MODEL),
                   pl.BlockSpec((8, 128), lambda i: (i, 0)), vec],
        compiler_params=_params(("arbitrary",)),
    )(act, w_down_g, x2, target, g4)


def ffn_act_bwd(dff, w_down_g, gate, up):
    def body(dff_ref, wd_ref, gate_ref, up_ref, dgu_ref):
        dact = _dot_nt(dff_ref[...], wd_ref[...])
        gate = gate_ref[...].astype(F32)
        sg = _sigmoid(gate)
        dgu_ref[:, 0:D_FF] = (dact * up_ref[...].astype(F32) * (sg * (1.0 + gate * (1.0 - sg)))).astype(MM)
        dgu_ref[:, D_FF:2 * D_FF] = (dact * (gate * sg)).astype(MM)

    return pl.pallas_call(
        body, name="ffn_act_bwd", grid=(SEQ // TM,),
        out_shape=jax.ShapeDtypeStruct((SEQ, 2 * D_FF), MM),
        in_specs=[_row_tile_spec(D_MODEL), _weight_spec(w_down_g.shape), _row_tile_spec(D_FF), _row_tile_spec(D_FF)],
        out_specs=_row_tile_spec(2 * D_FF),
        compiler_params=_params(("arbitrary",)),
    )(dff, w_down_g, gate, up)


def ffn_in_bwd(dgu, w_up_g, x2, mix, dy, g3, g2):
    def body(dgu_ref, w_ref, x2_ref, mix_ref, dy_ref, g3_ref, g2_ref, dx2_ref, dmix_ref, dg3_ref, dg2_ref):
        dh2 = None
        for j in range(N_CHIPS):
            t = _dot_nt(dgu_ref[:, j * UP_SHARD:(j + 1) * UP_SHARD], w_ref[j])
            dh2 = t if dh2 is None else dh2 + t
        first = pl.program_id(0) == 0
        n3, r3 = _rms(x2_ref[...])
        dx2 = dy_ref[...] + _rms_bwd(dh2 * g3_ref[...], n3, r3)
        dx2_ref[...] = dx2
        _acc_rows(dg3_ref, jnp.sum(dh2 * n3, axis=0, keepdims=True), first)
        n2, r2 = _rms(mix_ref[...])
        dmix_ref[...] = _rms_bwd(dx2 * g2_ref[...], n2, r2).astype(MM)
        _acc_rows(dg2_ref, jnp.sum(dx2 * n2, axis=0, keepdims=True), first)

    vec = _full_spec((1, D_MODEL))
    return pl.pallas_call(
        body, name="ffn_in_bwd", grid=(SEQ // TM,),
        out_shape=(jax.ShapeDtypeStruct((SEQ, D_MODEL), F32), jax.ShapeDtypeStruct((SEQ, D_MODEL), MM),
                   jax.ShapeDtypeStruct((1, D_MODEL), F32), jax.ShapeDtypeStruct((1, D_MODEL), F32)),
        in_specs=[_row_tile_spec(2 * D_FF), _weight_spec(w_up_g.shape), _row_tile_spec(D_MODEL),
                  _row_tile_spec(D_MODEL), _row_tile_spec(D_MODEL), vec, vec],
        out_specs=[_row_tile_spec(D_MODEL), _row_tile_spec(D_MODEL), vec, vec],
        compiler_params=_params(("arbitrary",)),
    )(dgu, w_up_g, x2, mix, dy, g3, g2)


def merge_bwd(dmix, w_out_g, gc, ga, co, ao, w_cb_g, w_ab_g):
    def body(dmix_ref, wout_ref, gc_ref, ga_ref, co_ref, ao_ref, wcb_ref, wab_ref,
             dco_ref, dao_ref, dg_ref, du3_ref, datt_ref, dbcb_ref):
        dm = _dot_nt(dmix_ref[...], wout_ref[...])
        sgc = _sigmoid(gc_ref[...])
        sga = _sigmoid(ga_ref[...])
        dco = dm * sgc
        dao = dm * sga
        dg_ref[:, 0:D_MODEL] = (dm * co_ref[...].astype(F32) * (sgc * (1.0 - sgc))).astype(MM)
        dg_ref[:, D_MODEL:2 * D_MODEL] = (dm * ao_ref[...].astype(F32) * (sga * (1.0 - sga))).astype(MM)
        _acc_rows(dbcb_ref, jnp.sum(dco, axis=0, keepdims=True), pl.program_id(0) == 0)
        dco_ref[...] = dco.astype(MM)
        dao_ref[...] = dao.astype(MM)
        du3 = None
        datt = None
        for j in range(N_CHIPS):
            cols = slice(j * BR_SHARD, (j + 1) * BR_SHARD)
            t = _dot_nt(dco_ref[:, cols], wcb_ref[j])
            s = _dot_nt(dao_ref[:, cols], wab_ref[j])
            du3 = t if du3 is None else du3 + t
            datt = s if datt is None else datt + s
        du3_ref[...] = du3
        datt_ref[...] = datt.astype(MM)

    wide = _row_tile_spec(D_MODEL)
    return pl.pallas_call(
        body, name="merge_bwd", grid=(SEQ // TM,),
        out_shape=(jax.ShapeDtypeStruct((SEQ, D_MODEL), MM), jax.ShapeDtypeStruct((SEQ, D_MODEL), MM),
                   jax.ShapeDtypeStruct((SEQ, 2 * D_MODEL), MM),
                   jax.ShapeDtypeStruct((SEQ, CONV_DIM), F32), jax.ShapeDtypeStruct((SEQ, ATT_DIM), MM),
                   jax.ShapeDtypeStruct((1, D_MODEL), F32)),
        in_specs=[wide, _weight_spec(w_out_g.shape), wide, wide, wide, wide,
                  _weight_spec(w_cb_g.shape), _weight_spec(w_ab_g.shape)],
        out_specs=[wide, wide, _row_tile_spec(2 * D_MODEL), _row_tile_spec(CONV_DIM), _row_tile_spec(ATT_DIM),
                   _full_spec((1, D_MODEL))],
        compiler_params=_params(("arbitrary",)),
    )(dmix, w_out_g, gc, ga, co, ao, w_cb_g, w_ab_g)


def conv_bwd(du3, u1, ci, w_dw, ln_g, ln_b):
    def body(du3_ref, u1_ref, ci_ref, w_ref, g_ref, bb_ref,
             dci_ref, dw_ref, dbdw_ref, dg_ref, db_ref, upad_ref, dpad_ref, dwacc_ref, vacc_ref):
        _glu_into(ci_ref, upad_ref)
        dpad_ref[SEQ:SEQ + 32, :] = jnp.zeros((32, CONV_DIM), F32)
        dwacc_ref[...] = jnp.zeros_like(dwacc_ref)
        vacc_ref[...] = jnp.zeros_like(vacc_ref)

        def fold8(t):
            s = t[0:8, :]
            for r in range(1, CONV_TILE // 8):
                s = s + t[8 * r:8 * r + 8, :]
            return s

        def pass1(i, c):
            t0 = pl.multiple_of(i * CONV_TILE, CONV_TILE)
            xh, rstd = _layernorm_parts(u1_ref[pl.ds(t0, CONV_TILE), :])
            gv = g_ref[...]
            u2 = xh * gv + bb_ref[...]
            s2 = _sigmoid(u2)
            du2 = du3_ref[pl.ds(t0, CONV_TILE), :] * (s2 * (1.0 + u2 * (1.0 - s2)))
            wv = du2 * gv
            du1 = rstd * (wv - jnp.mean(wv, axis=-1, keepdims=True)
                          - xh * jnp.mean(wv * xh, axis=-1, keepdims=True))
            dpad_ref[pl.ds(t0, CONV_TILE), :] = du1
            vacc_ref[0] += fold8(du2 * xh)
            vacc_ref[1] += fold8(du2)
            vacc_ref[2] += fold8(du1)
            win = upad_ref[pl.ds(t0, CONV_WIN), :]
            n = win.shape[0]
            for rot in range(8):
                shifted = win if rot == 0 else pltpu.roll(win, n - rot, 0)
                for a in range(5):
                    j = 8 * a + rot - 2
                    if 0 <= j < CONV_WIDTH:
                        dwacc_ref[j] += fold8(du1 * shifted[8 * a:8 * a + CONV_TILE, :])
            return c

        lax.fori_loop(0, SEQ // CONV_TILE, pass1, 0)

        def pass2(i, c):
            t0 = pl.multiple_of(i * CONV_TILE, CONV_TILE)
            win = dpad_ref[pl.ds(t0, CONV_WIN), :]
            du0 = _shifted_sum(win, [(30 - j, w_ref[j:j + 1, :]) for j in range(CONV_WIDTH)])
            a = ci_ref[pl.ds(t0, CONV_TILE), 0:CONV_DIM]
            sb = _sigmoid(ci_ref[pl.ds(t0, CONV_TILE), CONV_DIM:2 * CONV_DIM])
            dci_ref[pl.ds(t0, CONV_TILE), 0:CONV_DIM] = (du0 * sb).astype(MM)
            dci_ref[pl.ds(t0, CONV_TILE), CONV_DIM:2 * CONV_DIM] = (du0 * a * (sb * (1.0 - sb))).astype(MM)
            return c

        lax.fori_loop(0, SEQ // CONV_TILE, pass2, 0)

        for j in range(CONV_WIDTH):
            dw_ref[j:j + 1, :] = jnp.sum(dwacc_ref[j], axis=0, keepdims=True)
        dw_ref[CONV_WIDTH:32, :] = jnp.zeros((32 - CONV_WIDTH, CONV_DIM), F32)
        dg_ref[...] = jnp.sum(vacc_ref[0], axis=0, keepdims=True)
        db_ref[...] = jnp.sum(vacc_ref[1], axis=0, keepdims=True)
        dbdw_ref[...] = jnp.sum(vacc_ref[2], axis=0, keepdims=True)

    vec = jax.ShapeDtypeStruct((1, CONV_DIM), F32)
    return pl.pallas_call(
        body, name="conv_bwd",
        out_shape=(jax.ShapeDtypeStruct((SEQ, 2 * CONV_DIM), MM), jax.ShapeDtypeStruct((32, CONV_DIM), F32),
                   vec, vec, vec),
        in_specs=[VMEM_SPEC] * 6, out_specs=[VMEM_SPEC] * 5,
        scratch_shapes=[pltpu.VMEM((SEQ + 32, CONV_DIM), F32), pltpu.VMEM((SEQ + 32, CONV_DIM), F32),
                        pltpu.VMEM((CONV_WIDTH, 8, CONV_DIM), F32), pltpu.VMEM((3, 8, CONV_DIM), F32)],
        compiler_params=_params(),
    )(du3, u1, ci, w_dw, ln_g, ln_b)


def attn_bwd(q, k, v, datt, rc):
    def body(q_ref, k_ref, v_ref, do_ref, rc_ref, dqkv_ref, dqa_ref, dka_ref, dva_ref, pc_ref, z_ref, sig1_ref,
             sig2_ref, g_ref, spb_ref, gb_ref, ar_ref, dzr_ref, dzc_ref, qm_ref, km_ref, dom_ref):
        lane, row, _ = _head_masks()
        _split_heads(q_ref, qm_ref)
        _split_heads(k_ref, km_ref)
        _split_heads(do_ref, dom_ref)
        for ref in (dqa_ref, dka_ref, dva_ref, pc_ref):
            ref[...] = jnp.zeros_like(ref)
        z_ref[...] = jnp.full(z_ref.shape, NO_SCORE, F32)
        for ref in (sig1_ref, sig2_ref, spb_ref, ar_ref, g_ref, gb_ref, dzr_ref, dzc_ref):
            ref[...] = jnp.zeros_like(ref)
        w_suffix = _cumsum_weights(suffix=True, with_total=False)
        w_prefix = _cumsum_weights(suffix=False, with_total=True)

        def step(pairs):
            (ia, ja), (ib, jb), (ic, jc), (id_, jd) = pairs
            ka, qb_, kb_, kc, qd, kd = (pl.multiple_of(jnp.maximum(b, 0) * TQ, TQ) for b in (ja, ib, jb, jc, id_, jd))
            qa2, qb2, qc2, qd2, kd2 = (pl.multiple_of(jnp.maximum(b, 0) * 2 * TQ, 2 * TQ)
                                       for b in (ia, ib, ic, id_, jd))
            bias_a = _score_bias(lane, row, ia, ja)
            rc_rows = rc_ref[pl.ds(qb_, TQ), :]
            first_c = jc == 0
            for p in range(N_PAIRS):
                cols = slice(128 * p, 128 * (p + 1))
                k_a = k_ref[pl.ds(ka, TQ), cols]
                v_b = v_ref[pl.ds(kb_, TQ), cols]
                dqa_ref[pl.ds(qd, TQ), cols] += _dot(dzc_ref[p], km_ref[p, pl.ds(kd2, 2 * TQ), :])
                dka_ref[pl.ds(kd, TQ), cols] += _dot_tn(dzr_ref[p], qm_ref[p, pl.ds(qd2, 2 * TQ), :])
                dva_ref[pl.ds(kc, TQ), cols] += _dot_tn(ar_ref[p], dom_ref[p, pl.ds(qc2, 2 * TQ), :])
                for h in range(2):
                    hh = 2 * p + h
                    rows = slice(TQ * h, TQ * (h + 1))
                    r = _dot(gb_ref[hh], w_prefix)
                    p_in = jnp.where(first_c, 0.0, pc_ref[hh])
                    dz = (g_ref[hh] - sig2_ref[hh] * (r[:, :128] + p_in)).astype(MM)
                    dzc_ref[p, :, rows] = dz
                    dzr_ref[p, rows, :] = dz
                    pc_ref[hh] = p_in + r[:, 128:]
                    r_in = jnp.sum(jnp.where(lane == 16 * hh + jb, rc_rows, 0.0), axis=1, keepdims=True)
                    a = jnp.exp(z_ref[hh] - (_dot(spb_ref[hh], w_suffix) + r_in))
                    g = _dot_nt(dom_ref[p, pl.ds(qb2 + TQ * h, TQ), :], v_b) * a
                    ar_ref[p, rows, :] = a.astype(MM)
                    g_ref[hh] = g
                    gb_ref[hh] = g.astype(MM)
                    sig2_ref[hh] = sig1_ref[hh]
                    z = _dot_nt(qm_ref[p, pl.ds(qa2 + TQ * h, TQ), :], k_a) + bias_a
                    sp = _softplus(z)
                    sig1_ref[hh] = jnp.exp(z - sp)
                    z_ref[hh] = z
                    spb_ref[hh] = sp.astype(MM)

        _block_pipeline(4, False, step)
        dqkv_ref[:, 0:ATT_DIM] = (dqa_ref[...] * ATT_SCALE).astype(MM)
        dqkv_ref[:, ATT_DIM:2 * ATT_DIM] = dka_ref[...].astype(MM)
        dqkv_ref[:, 2 * ATT_DIM:3 * ATT_DIM] = dva_ref[...].astype(MM)

    split = pltpu.VMEM((N_PAIRS, 2 * SEQ, 128), MM)
    return pl.pallas_call(
        body, name="attn_bwd", out_shape=jax.ShapeDtypeStruct((SEQ, 3 * ATT_DIM), MM),
        in_specs=[VMEM_SPEC] * 5, out_specs=VMEM_SPEC,
        scratch_shapes=[pltpu.VMEM((SEQ, ATT_DIM), F32)] * 3 + [pltpu.VMEM((8, TQ, 128), F32)] * 5
                       + [pltpu.VMEM((8, TQ, 128), MM)] * 2
                       + [pltpu.VMEM((N_PAIRS, 2 * TQ, 128), MM)] * 2 + [pltpu.VMEM((N_PAIRS, TQ, 256), MM)]
                       + [split] * 3,
        compiler_params=_params(),
    )(q, k, v, datt, rc)


DPROJ_PIECES = ((0, 1024), (1024, 2560), (2560, 4608))


def _dproj_segments(j):
    g0, g1 = j * IN_SHARD, (j + 1) * IN_SHARD
    segs = []
    for p, (s, e) in enumerate(DPROJ_PIECES):
        lo, hi = max(s, g0), min(e, g1)
        if lo < hi:
            segs.append((p, lo - s, lo - g0, hi - lo))
    return segs


def in_proj_bwd(pieces, w_in_g, x, dx2, g1, scatter=()):
    ns = len(scatter)
    nt = SEQ // TM

    def body(*refs):
        p_refs = refs[:3]
        w_ref, x_ref, dx2_ref, g_ref = refs[3:7]
        dx_ref, dg_ref = refs[7 + ns:9 + ns]
        if ns:
            sc = _Scatter(refs[7:7 + ns], refs[9 + ns:9 + 2 * ns], refs[9 + 2 * ns:])
            pl.when(pl.program_id(0) == 0)(sc.start)
        dh = None
        for j in range(N_CHIPS):
            for p, lo, off, width in _dproj_segments(j):
                t = _dot_nt(p_refs[p][:, lo:lo + width], w_ref[j, :, off:off + width])
                dh = t if dh is None else dh + t
        n1, r1 = _rms(x_ref[...])
        dx_ref[...] = dx2_ref[...] + _rms_bwd(dh * g_ref[...], n1, r1)
        _acc_rows(dg_ref, jnp.sum(dh * n1, axis=0, keepdims=True), pl.program_id(0) == 0)
        if ns:
            pl.when(pl.program_id(0) == nt - 1)(sc.finish)

    vec = _full_spec((1, D_MODEL))
    return pl.pallas_call(
        body, name="in_proj_bwd", grid=(nt,),
        out_shape=[jax.ShapeDtypeStruct((SEQ, D_MODEL), F32), jax.ShapeDtypeStruct((1, D_MODEL), F32)]
                  + _Scatter.out_shapes(scatter),
        in_specs=[_row_tile_spec(p.shape[1]) for p in pieces]
                 + [_weight_spec(w_in_g.shape), _row_tile_spec(D_MODEL), _row_tile_spec(D_MODEL), vec] + [ANY] * ns,
        out_specs=[_row_tile_spec(D_MODEL), vec] + [ANY] * ns,
        scratch_shapes=_Scatter.scratch(ns) if ns else [],
        compiler_params=_params(("arbitrary",)),
    )(*pieces, w_in_g, x, dx2, g1, *scatter)


def weight_grad_in(h1, pieces):
    kh = D_MODEL // 2

    def body(a_ref, p0_ref, p1_ref, p2_ref, o_ref):
        p_refs = (p0_ref, p1_ref, p2_ref)
        a = a_ref[...]
        for j in range(N_CHIPS):
            @pl.when(pl.program_id(1) == j)
            def _():
                for p, lo, off, width in _dproj_segments(j):
                    o_ref[0, 0, :, off:off + width] = _dot_tn(a, p_refs[p][:, lo:lo + width]).astype(MM)

    return pl.pallas_call(
        body, name="dw_in", grid=(2, N_CHIPS), out_shape=jax.ShapeDtypeStruct((N_CHIPS, 2, kh, IN_SHARD), MM),
        in_specs=[pl.BlockSpec((SEQ, kh), lambda h, j: (0, h))] + [_weight_spec(p.shape) for p in pieces],
        out_specs=pl.BlockSpec((1, 1, kh, IN_SHARD), lambda h, j: (j, h, 0, 0)),
        compiler_params=_params(("arbitrary", "arbitrary")),
    )(h1, *pieces)


def weight_grad(a, b, name, col_sharded, tk=None):
    kin, n = a.shape[1], b.shape[1]

    def body(a_ref, b_ref, o_ref):
        if col_sharded:
            o_ref[0, 0] = _dot_tn(a_ref[...], b_ref[...]).astype(MM)
        else:
            o_ref[...] = _dot_tn(a_ref[...], b_ref[...]).astype(MM)

    if col_sharded:
        kh, ns = kin // 2, n // N_CHIPS
        out = jax.ShapeDtypeStruct((N_CHIPS, 2, kh, ns), MM)
        grid = (2, N_CHIPS)
        in_specs = [pl.BlockSpec((SEQ, kh), lambda h, j: (0, h)), pl.BlockSpec((SEQ, ns), lambda h, j: (0, j))]
        out_spec = pl.BlockSpec((1, 1, kh, ns), lambda h, j: (j, h, 0, 0))
        sem = ("arbitrary", "arbitrary")
    else:
        out = jax.ShapeDtypeStruct((kin, n), MM)
        grid = (kin // tk,)
        in_specs = [pl.BlockSpec((SEQ, tk), lambda r: (0, r)), pl.BlockSpec((SEQ, n), lambda r: (0, 0))]
        out_spec = pl.BlockSpec((tk, n), lambda r: (r, 0))
        sem = ("arbitrary",)
    res = pl.pallas_call(
        body, name=name, grid=grid, out_shape=out, in_specs=in_specs, out_specs=out_spec,
        compiler_params=_params(sem),
    )(a, b)
    if not col_sharded:
        res = res.reshape(N_CHIPS, 2, kin // (2 * N_CHIPS), n)
    return res


def _place():
    x, y, c = lax.axis_index("x"), lax.axis_index("y"), lax.axis_index("c")
    chips = [(1 - x, y), (x, 1 - y), (1 - x, 1 - y)]
    return x, y, c, chips


def _rcopy(src, dst, send_sem, recv_sem, dev):
    return pltpu.make_async_remote_copy(src_ref=src, dst_ref=dst, send_sem=send_sem, recv_sem=recv_sem,
                                        device_id=dev, device_id_type=MESH)


class _Gather:
    def __init__(self, shapes, w, o, scratch):
        self.n, self.shapes, self.w, self.o = len(w), shapes, w, o
        self.send, self.recv, self.fsend, self.frecv, self.loc_in, self.loc_out = scratch[:6]
        self.raw, self.stage = scratch[6:6 + self.n], scratch[6 + self.n:]
        self.x, self.y, self.c, self.chips = _place()
        self.me = 2 * self.x + self.y
        self.sib = (self.x, self.y, 1 - self.c)
        self.pairs = [(j, t) for j in range(3) for t in range(self.n)]

    @staticmethod
    def scratch(shards):
        n = len(shards)
        sems = pltpu.SemaphoreType.DMA
        return ([sems((3 * n,)), sems((3 * n,)), sems((3 * n,)), sems((3 * n,)), sems((n,)), sems((n,))]
                + [pltpu.VMEM(s.shape, s.dtype) for s in shards] + [pltpu.VMEM(s.shape, MM) for s in shards])

    @staticmethod
    def out_shapes(shards):
        return [jax.ShapeDtypeStruct((N_CHIPS,) + s.shape, MM) for s in shards]

    def _half(self, t, k, cc):
        rh = self.shapes[t][0] // 2
        return self.o[t].at[k, pl.ds(cc * rh, rh), :]

    def _chip(self, j):
        cx, cy = self.chips[j]
        return 2 * cx + cy, (cx, cy, self.c)

    def local_in(self, t):
        return pltpu.make_async_copy(self.w[t], self.raw[t], self.loc_in.at[t])

    def local_out(self, t):
        return pltpu.make_async_copy(self.stage[t], self.o[t].at[self.me], self.loc_out.at[t])

    def first(self, j, t):
        rh = self.shapes[t][0] // 2
        i = j * self.n + t
        return _rcopy(self.stage[t].at[pl.ds(self.c * rh, rh), :], self._half(t, self.me, self.c),
                      self.send.at[i], self.recv.at[i], self._chip(j)[1])

    def arrived(self, j, t):
        k, dev = self._chip(j)
        i = j * self.n + t
        blk = self._half(t, k, self.c)
        return _rcopy(blk, blk, self.send.at[i], self.recv.at[i], dev)

    def passed(self, j, t, cc):
        i = j * self.n + t
        blk = self._half(t, self._chip(j)[0], cc)
        return _rcopy(blk, blk, self.fsend.at[i], self.frecv.at[i], self.sib)

    def start(self):
        for t in range(self.n):
            self.local_in(t).start()
        for t in range(self.n):
            self.local_in(t).wait()
            self.stage[t][...] = self.raw[t][...].astype(MM)
            self.local_out(t).start()
        for j, t in self.pairs:
            self.first(j, t).start()

    def forward(self):
        for j, t in self.pairs:
            self.arrived(j, t).wait_recv()
            self.passed(j, t, self.c).start()

    def finish(self):
        for j, t in self.pairs:
            self.passed(j, t, 1 - self.c).wait_recv()
        for j, t in self.pairs:
            self.first(j, t).wait_send()
            self.passed(j, t, self.c).wait_send()
        for t in range(self.n):
            self.local_out(t).wait()


def all_gather_weights(shards, small):
    n = len(shards)
    shapes = [s.shape for s in shards]

    def body(*refs):
        w = refs[:n]
        sm = refs[n]
        o = refs[n + 1:2 * n + 1]
        osm = refs[2 * n + 1]
        ssend, srecv, sloc = refs[2 * n + 2:2 * n + 5]
        g = _Gather(shapes, w, o, refs[2 * n + 5:])
        own = pltpu.make_async_copy(sm, osm.at[g.me], sloc)
        own.start()
        g.start()
        small_cps = [_rcopy(sm, osm.at[g.me], ssend.at[j], srecv.at[j], g._chip(j)[1]) for j in range(3)]
        for cp in small_cps:
            cp.start()
        g.forward()
        g.finish()
        for j in range(3):
            k, dev = g._chip(j)
            _rcopy(sm, osm.at[k], ssend.at[j], srecv.at[j], dev).wait_recv()
            small_cps[j].wait_send()
        own.wait()

    out_shape = _Gather.out_shapes(shards)
    out_shape.append(jax.ShapeDtypeStruct((N_CHIPS,) + small.shape, small.dtype))
    sems = pltpu.SemaphoreType.DMA
    return pl.pallas_call(
        body, name="all_gather_weights", out_shape=out_shape,
        in_specs=[ANY] * (n + 1), out_specs=[ANY] * (n + 1),
        scratch_shapes=[sems((3,)), sems((3,)), sems] + _Gather.scratch(shards),
        compiler_params=_params(),
    )(*shards, small)


def sibling_exchange(grads, name):
    n = len(grads)

    def body(*refs):
        g = refs[:n]
        o = refs[n:2 * n]
        send, recv = refs[2 * n:]
        x, y, c, _ = _place()
        cps = [_rcopy(g[t].at[:, 1 - c], o[t], send.at[t], recv.at[t], (x, y, 1 - c)) for t in range(n)]
        for cp in cps:
            cp.start()
        for cp in cps:
            cp.wait()

    sems = pltpu.SemaphoreType.DMA
    return pl.pallas_call(
        body, name=name,
        out_shape=[jax.ShapeDtypeStruct((a.shape[0],) + a.shape[2:], a.dtype) for a in grads],
        in_specs=[ANY] * n, out_specs=[ANY] * n, scratch_shapes=[sems((n,)), sems((n,))],
    )(*grads)


class _Scatter:
    def __init__(self, p, o, sems):
        self.n, self.p, self.o = len(p), p, o
        self.send, self.recv = sems
        _, _, self.c, self.chips = _place()

    @staticmethod
    def scratch(n):
        sems = pltpu.SemaphoreType.DMA
        return [sems((3 * n,)), sems((3 * n,))]

    @staticmethod
    def out_shapes(parts):
        return [jax.ShapeDtypeStruct((3,) + a.shape[1:], a.dtype) for a in parts]

    def copies(self):
        cps = []
        for j, (cx, cy) in enumerate(self.chips):
            for t in range(self.n):
                i = j * self.n + t
                cps.append(_rcopy(self.p[t].at[2 * cx + cy], self.o[t].at[j], self.send.at[i], self.recv.at[i],
                                  (cx, cy, self.c)))
        return cps

    def start(self):
        for cp in self.copies():
            cp.start()

    def finish(self):
        for cp in self.copies():
            cp.wait()


HBM_SPEC = pl.BlockSpec(memory_space=pltpu.HBM)
SEM_SPEC = pl.BlockSpec(memory_space=pltpu.SEMAPHORE)
DATAFLOW = pltpu.SideEffectType.DATAFLOW_SIDE_EFFECTING


def _scatter_copies(p_ref, land_ref, send, recv):
    _, _, c, chips = _place()
    return [_rcopy(p_ref.at[2 * cx + cy], land_ref.at[j], send.at[j], recv.at[j], (cx, cy, c))
            for j, (cx, cy) in enumerate(chips)]


def scatter_start(parts, tag):
    n = len(parts)
    lands = [lax.empty((3,) + p.shape[1:], p.dtype) for p in parts]

    def body(*refs):
        p, land = refs[:n], refs[n:2 * n]
        send, recv = refs[2 * n:3 * n], refs[3 * n:4 * n]
        for t in range(n):
            for cp in _scatter_copies(p[t], land[t], send[t], recv[t]):
                cp.start()
        token = refs[6 * n]
        token[...] = jnp.zeros_like(token)

    sems = [pltpu.SemaphoreType.DMA((3,))] * (2 * n)
    hbm = [pltpu.HBM(a.shape, a.dtype) for a in list(parts) + lands]
    res = pl.pallas_call(
        body, name="scatter_start_" + tag,
        out_shape=sems + hbm + [jax.ShapeDtypeStruct((8, 128), F32)],
        in_specs=[HBM_SPEC] * (2 * n), out_specs=[SEM_SPEC] * (2 * n) + [HBM_SPEC] * (2 * n) + [VMEM_SPEC],
        input_output_aliases={i: 2 * n + i for i in range(2 * n)},
        compiler_params=pltpu.CompilerParams(has_side_effects=DATAFLOW),
    )(*[pltpu.with_memory_space_constraint(a, pltpu.HBM) for a in list(parts) + lands])
    return res[:-1], res[-1]


def scatter_wait(state, after, tag):
    n = len(state) // 4
    sems, bufs = state[:2 * n], state[2 * n:]

    def body(*refs):
        p, land = refs[:n], refs[n:2 * n]
        send, recv = refs[2 * n:3 * n], refs[3 * n:4 * n]
        for t in range(n):
            for cp in _scatter_copies(p[t], land[t], send[t], recv[t]):
                cp.wait_send()
                cp.wait_recv()

    return pl.pallas_call(
        body, name="scatter_wait_" + tag, out_shape=[pltpu.HBM(a.shape, a.dtype) for a in bufs],
        in_specs=[HBM_SPEC] * (2 * n) + [SEM_SPEC] * (2 * n) + [ANY] * len(after), out_specs=[HBM_SPEC] * (2 * n),
        input_output_aliases={i: i for i in range(2 * n)},
        compiler_params=pltpu.CompilerParams(has_side_effects=DATAFLOW),
    )(*bufs, *sems, *after)[n:]


def sibling_swap(halves, name):
    n = len(halves)

    def body(*refs):
        h = refs[:n]
        o = refs[n:2 * n]
        send, recv = refs[2 * n:]
        x, y, c, _ = _place()
        cps = [_rcopy(h[t], o[t], send.at[t], recv.at[t], (x, y, 1 - c)) for t in range(n)]
        for cp in cps:
            cp.start()
        for cp in cps:
            cp.wait()

    sems = pltpu.SemaphoreType.DMA
    return pl.pallas_call(
        body, name=name, out_shape=[jax.ShapeDtypeStruct(a.shape, a.dtype) for a in halves],
        in_specs=[ANY] * n, out_specs=[ANY] * n, scratch_shapes=[sems((n,)), sems((n,))],
    )(*halves)


def small_all_reduce(ddw, v512, v1024, loss_parts):
    rows, width = PACK_ROWS, 512
    n512, n1024 = len(VEC512), len(VEC1024)

    def body(*refs):
        ddw_ref = refs[0]
        a_refs = refs[1:1 + n512]
        b_refs = refs[1 + n512:1 + n512 + n1024]
        lp_ref, o_ref, p_ref, gath_ref, send, recv = refs[1 + n512 + n1024:]
        p_ref[...] = jnp.zeros_like(p_ref)
        p_ref[0:32, :] = ddw_ref[...]
        p_ref[LOSS_ROW:LOSS_ROW + 1, 0:128] = jnp.sum(lp_ref[...], axis=0, keepdims=True) * 0.125
        for i, r in enumerate(a_refs):
            p_ref[32 + i:33 + i, :] = r[...]
        for i, r in enumerate(b_refs):
            base = 32 + n512 + 2 * i
            p_ref[base:base + 1, :] = r[:, 0:512]
            p_ref[base + 1:base + 2, :] = r[:, 512:1024]
        x, y, c, _ = _place()
        me = 4 * x + 2 * y + c
        gath_ref[me] = p_ref[...]
        cps = []
        for k in range(1, 8):
            dx, dy, dc = (k >> 2) & 1, (k >> 1) & 1, k & 1
            px = 1 - x if dx else x
            py = 1 - y if dy else y
            pc = 1 - c if dc else c
            cps.append(_rcopy(p_ref, gath_ref.at[me], send.at[k - 1], recv.at[k - 1], (px, py, pc)))
        for cp in cps:
            cp.start()
        for k in range(1, 8):
            dx, dy, dc = (k >> 2) & 1, (k >> 1) & 1, k & 1
            px = 1 - x if dx else x
            py = 1 - y if dy else y
            pc = 1 - c if dc else c
            _rcopy(p_ref, gath_ref.at[4 * px + 2 * py + pc], send.at[k - 1], recv.at[k - 1], (px, py, pc)).wait_recv()
        for cp in cps:
            cp.wait_send()
        total = gath_ref[0]
        for d in range(1, 8):
            total = total + gath_ref[d]
        o_ref[...] = total

    sems = pltpu.SemaphoreType.DMA
    n_in = 2 + n512 + n1024
    return pl.pallas_call(
        body, name="small_all_reduce", out_shape=jax.ShapeDtypeStruct((rows, width), F32),
        in_specs=[VMEM_SPEC] * n_in, out_specs=VMEM_SPEC,
        scratch_shapes=[pltpu.VMEM((rows, width), F32), pltpu.VMEM((8, rows, width), F32), sems((7,)), sems((7,))],
    )(ddw, *[v512[n] for n in VEC512], *[v1024[n] for n in VEC1024], loss_parts)


def _row_block(r):
    for tr in (512, 352, 256, 128):
        if r % tr == 0:
            return tr
    return r


def add_halves(g, recv, name):
    _, _, r, w = g.shape
    tr = _row_block(r)

    def body(g0_ref, g1_ref, r_ref, ob_ref, own_ref):
        k = pl.program_id(1)
        c = lax.axis_index("c")
        me = 2 * lax.axis_index("x") + lax.axis_index("y")
        t = jnp.where(c == 0, g0_ref[0, 0], g1_ref[0, 0]).astype(F32) + r_ref[0].astype(F32)
        ob_ref[0] = t.astype(MM)
        mine = jnp.where(k == me, t, 0.0)

        @pl.when(k == 0)
        def _():
            own_ref[...] = mine

        @pl.when(k != 0)
        def _():
            own_ref[...] += mine

    return pl.pallas_call(
        body, name=name, grid=(r // tr, N_CHIPS),
        in_specs=[pl.BlockSpec((1, 1, tr, w), lambda i, k: (k, 0, i, 0)),
                  pl.BlockSpec((1, 1, tr, w), lambda i, k: (k, 1, i, 0)),
                  pl.BlockSpec((1, tr, w), lambda i, k: (k, i, 0))],
        out_specs=[pl.BlockSpec((1, tr, w), lambda i, k: (k, i, 0)),
                   pl.BlockSpec((tr, w), lambda i, k: (i, 0))],
        out_shape=(jax.ShapeDtypeStruct((N_CHIPS, r, w), MM), jax.ShapeDtypeStruct((r, w), F32)),
        compiler_params=_params(("arbitrary", "arbitrary")),
    )(g, g, recv)


def sum_parts(own, rin, after, name):
    _, r, w = rin.shape
    tr = _row_block(r)

    def body(o_ref, r_ref, after_ref, out_ref):
        out_ref[...] = ((o_ref[...] + r_ref[0].astype(F32)) + r_ref[1].astype(F32)) + r_ref[2].astype(F32)

    return pl.pallas_call(
        body, name=name, grid=(r // tr,), out_shape=jax.ShapeDtypeStruct((r, w), F32),
        in_specs=[pl.BlockSpec((tr, w), lambda i: (i, 0)), pl.BlockSpec((3, tr, w), lambda i: (0, i, 0)),
                  _full_spec((8, 128))],
        out_specs=pl.BlockSpec((tr, w), lambda i: (i, 0)),
        compiler_params=_params(("arbitrary",)),
    )(own, rin, after)


def _adamw_math(w, g, m, v):
    mn = ADAM_B1 * m + (1.0 - ADAM_B1) * g
    vn = ADAM_B2 * v + (1.0 - ADAM_B2) * (g * g)
    m_hat = mn / (1.0 - ADAM_B1 ** ADAM_STEP)
    v_hat = vn / (1.0 - ADAM_B2 ** ADAM_STEP)
    return -ADAM_LR * (m_hat / (jnp.sqrt(v_hat) + ADAM_EPS) + ADAM_WD * w), mn, vn


def adamw(w, mine, other, m, v, name):
    r, c = w.shape
    rh = r // 2
    tr = _row_block(rh)
    if c >= 1024 and tr % 512 == 0:
        tr = 256
    nb = rh // tr

    def body(w_ref, a_ref, b_ref, m_ref, v_ref, go_ref, d_ref, mo_ref, vo_ref):
        gv = jnp.where(lax.axis_index("c") == pl.program_id(0), a_ref[...], b_ref[...])
        go_ref[...] = gv
        d_ref[...], mo_ref[...], vo_ref[...] = _adamw_math(w_ref[...], gv, m_ref[...], v_ref[...])

    spec = pl.BlockSpec((tr, c), lambda h, i: (h * nb + i, 0))
    half = pl.BlockSpec((tr, c), lambda h, i: (i, 0))
    out = jax.ShapeDtypeStruct((r, c), F32)
    return pl.pallas_call(
        body, name=name, grid=(2, nb), out_shape=(out, out, out, out),
        in_specs=[spec, half, half, spec, spec], out_specs=[spec] * 4,
        compiler_params=_params(("arbitrary", "arbitrary")),
    )(w, mine, other, m, v)


def adamw_small(gsum, params):
    names = list(params)
    flat = [a for n in names for a in params[n]]

    def body(*refs):
        g_ref = refs[0]
        ins = refs[1:1 + 3 * len(names)]
        outs = refs[1 + 3 * len(names):]
        me = 2 * lax.axis_index("x") + lax.axis_index("y")
        for i, n in enumerate(names):
            w_ref, m_ref, v_ref = ins[3 * i:3 * i + 3]
            go_ref, d_ref, mo_ref, vo_ref = outs[4 * i:4 * i + 4]
            if n == "conv_dw_w":
                gv = jnp.zeros((CONV_WIDTH, 128), F32)
                for k in range(N_CHIPS):
                    gv = gv + jnp.where(me == k, g_ref[0:CONV_WIDTH, 128 * k:128 * (k + 1)], 0.0)
            elif n in VEC512:
                r0 = 32 + VEC512.index(n)
                gv = g_ref[r0:r0 + 1, :]
            else:
                r0 = 32 + len(VEC512) + 2 * VEC1024.index(n)
                gv = jnp.concatenate([g_ref[r0:r0 + 1, :], g_ref[r0 + 1:r0 + 2, :]], axis=1)
            go_ref[...] = gv
            d_ref[...], mo_ref[...], vo_ref[...] = _adamw_math(w_ref[...], gv, m_ref[...], v_ref[...])

    out_shape = [jax.ShapeDtypeStruct(params[n][0].shape, F32) for n in names for _ in range(4)]
    res = pl.pallas_call(
        body, name="adamw_small", out_shape=out_shape,
        in_specs=[VMEM_SPEC] * (1 + len(flat)), out_specs=[VMEM_SPEC] * len(out_shape),
        compiler_params=_params(),
    )(gsum, *flat)
    return {n: res[4 * i:4 * i + 4] for i, n in enumerate(names)}


REST = ("w_ffn_up", "w_ffn_down", "w_out", "w_conv_branch", "w_att_branch")
VEC512 = ("conv_dw_b", "conv_ln_g", "conv_ln_b")
VEC1024 = ("norm_mix_pre", "b_conv_branch", "norm_mix_post", "norm_ffn_pre", "norm_ffn_post")
PACK_ROWS = 48
LOSS_ROW = 47


def kernel(x, norm_mix_pre, w_in, conv_dw_w, conv_dw_b, conv_ln_g, conv_ln_b, w_conv_branch, b_conv_branch, w_att_branch, w_out, norm_mix_post, norm_ffn_pre, w_ffn_up, w_ffn_down, norm_ffn_post, loss_target, m_norm_mix_pre, m_w_in, m_conv_dw_w, m_conv_dw_b, m_conv_ln_g, m_conv_ln_b, m_w_conv_branch, m_b_conv_branch, m_w_att_branch, m_w_out, m_norm_mix_post, m_norm_ffn_pre, m_w_ffn_up, m_w_ffn_down, m_norm_ffn_post, v_norm_mix_pre, v_w_in, v_conv_dw_w, v_conv_dw_b, v_conv_ln_g, v_conv_ln_b, v_w_conv_branch, v_b_conv_branch, v_w_att_branch, v_w_out, v_norm_mix_post, v_norm_ffn_pre, v_w_ffn_up, v_w_ffn_down, v_norm_ffn_post):
    weights = dict(norm_mix_pre=norm_mix_pre, w_in=w_in, conv_dw_w=conv_dw_w, conv_dw_b=conv_dw_b, conv_ln_g=conv_ln_g, conv_ln_b=conv_ln_b, w_conv_branch=w_conv_branch, b_conv_branch=b_conv_branch, w_att_branch=w_att_branch, w_out=w_out, norm_mix_post=norm_mix_post, norm_ffn_pre=norm_ffn_pre, w_ffn_up=w_ffn_up, w_ffn_down=w_ffn_down, norm_ffn_post=norm_ffn_post)
    mom = dict(norm_mix_pre=m_norm_mix_pre, w_in=m_w_in, conv_dw_w=m_conv_dw_w, conv_dw_b=m_conv_dw_b, conv_ln_g=m_conv_ln_g, conv_ln_b=m_conv_ln_b, w_conv_branch=m_w_conv_branch, b_conv_branch=m_b_conv_branch, w_att_branch=m_w_att_branch, w_out=m_w_out, norm_mix_post=m_norm_mix_post, norm_ffn_pre=m_norm_ffn_pre, w_ffn_up=m_w_ffn_up, w_ffn_down=m_w_ffn_down, norm_ffn_post=m_norm_ffn_post)
    var = dict(norm_mix_pre=v_norm_mix_pre, w_in=v_w_in, conv_dw_w=v_conv_dw_w, conv_dw_b=v_conv_dw_b, conv_ln_g=v_conv_ln_g, conv_ln_b=v_conv_ln_b, w_conv_branch=v_w_conv_branch, b_conv_branch=v_b_conv_branch, w_att_branch=v_w_att_branch, w_out=v_w_out, norm_mix_post=v_norm_mix_post, norm_ffn_pre=v_norm_ffn_pre, w_ffn_up=v_w_ffn_up, w_ffn_down=v_w_ffn_down, norm_ffn_post=v_norm_ffn_post)
    order = list(weights)
    grads, deltas, new_m, new_v = {}, {}, {}, {}
    xs = x.reshape(SEQ, D_MODEL)
    tgt = loss_target.reshape(SEQ, D_MODEL)
    row = lambda a: a.reshape(1, -1)
    g1, g2, g3, g4 = (row(weights[n]) for n in ("norm_mix_pre", "norm_mix_post", "norm_ffn_pre", "norm_ffn_post"))
    ln_g, ln_b = row(conv_ln_g), row(conv_ln_b)

    def reduce_prepare(names, partial, tag):
        from_sib = sibling_exchange([partial[n] for n in names], "sibling_exchange_" + tag)
        return [add_halves(partial[n], r, "add_" + n) for n, r in zip(names, from_sib)]

    def reduce_finish(names, summed, from_chips, after, tag):
        halves = [sum_parts(s[1], r, after, "sum_" + n) for n, s, r in zip(names, summed, from_chips)]
        for n, a, b in zip(names, halves, sibling_swap(halves, "sibling_swap_" + tag)):
            grads[n], deltas[n], new_m[n], new_v[n] = adamw(weights[n], a, b, mom[n], var[n], "adamw_" + n)

    w_in_g, dw_g = all_gather_weights([w_in], conv_dw_w)
    w_dw_full = jnp.concatenate([dw_g[k] for k in range(N_CHIPS)], axis=1)
    h1, ci, q, k, v, gc, ga = in_proj_fwd(xs, g1, w_in_g)
    u1, u3, w_out_g, w_cb_g, w_ab_g = conv_fwd(ci, w_dw_full, row(conv_dw_b), ln_g, ln_b,
                                               [w_out, w_conv_branch, w_att_branch])
    att, rc, w_up_g = attn_fwd(q, k, v, [w_ffn_up])
    w_out_g = w_out_g.reshape(D_MODEL, D_MODEL)
    co, ao, merged, mix, x2, h2 = mix_fwd(u3, att, gc, ga, xs, w_cb_g, row(b_conv_branch), w_ab_g, w_out_g, g2, g3)
    gate, up, act, w_down_g = ffn_up_fwd(h2, w_up_g, [w_ffn_down])
    w_down_g = w_down_g.reshape(D_FF, D_MODEL)
    dff, dy, loss_parts, dg4 = ffn_down_loss(act, w_down_g, x2, tgt, g4)

    partial = {}
    dgu = ffn_act_bwd(dff, w_down_g, gate, up)
    partial["w_ffn_down"] = weight_grad(act, dff, "dw_ffn_down", False, tk=UP_SHARD)
    dx2, dmix, dg3, dg2 = ffn_in_bwd(dgu, w_up_g, x2, mix, dy, g3, g2)
    partial["w_ffn_up"] = weight_grad(h2, dgu, "dw_ffn_up", True)
    dco, dao, dg, du3, datt, dbcb = merge_bwd(dmix, w_out_g, gc, ga, co, ao, w_cb_g, w_ab_g)
    partial["w_out"] = weight_grad(merged, dmix, "dw_out", False, tk=512)
    partial["w_conv_branch"] = weight_grad(u3, dco, "dw_conv_branch", True)
    partial["w_att_branch"] = weight_grad(att, dao, "dw_att_branch", True)
    summed = reduce_prepare(REST, partial, "rest")
    state, token = scatter_start([s[0] for s in summed], "rest")
    zero = token[0:1, 0:1]
    dci, ddw, dbdw, dlng, dlnb = conv_bwd(du3, u1, ci, w_dw_full, ln_g + zero, ln_b)
    dqkv = attn_bwd(q, k, v, datt, rc + zero)
    from_chips = scatter_wait(state, [dci, dqkv], "rest")
    dproj = (dci, dqkv, dg)
    partial["w_in"] = weight_grad_in(h1, dproj)
    summed_in = reduce_prepare(("w_in",), partial, "w_in")
    state, token = scatter_start([summed_in[0][0]], "w_in")
    grad_x, dg1 = in_proj_bwd(dproj, w_in_g, xs, dx2, g1 + token[0:1, 0:1])
    reduce_finish(REST, summed, from_chips, token, "rest")
    from_chips_in = scatter_wait(state, [dg1] + [new_v[n] for n in REST], "w_in")
    reduce_finish(("w_in",), summed_in, from_chips_in, token, "w_in")

    v512 = dict(conv_dw_b=dbdw, conv_ln_g=dlng, conv_ln_b=dlnb)
    v1024 = dict(norm_mix_pre=dg1, b_conv_branch=dbcb, norm_mix_post=dg2, norm_ffn_pre=dg3, norm_ffn_post=dg4)
    gsum = small_all_reduce(ddw, v512, v1024, loss_parts)
    loss = gsum[LOSS_ROW, 0]
    as_rows = lambda n, a: a if n == "conv_dw_w" else a.reshape(1, -1)
    small_names = ("conv_dw_w",) + VEC512 + VEC1024
    small = adamw_small(gsum, {n: tuple(as_rows(n, d[n]) for d in (weights, mom, var)) for n in small_names})
    for n in small_names:
        grads[n], deltas[n], new_m[n], new_v[n] = (a.reshape(weights[n].shape) for a in small[n])

    return (loss, grad_x.reshape(1, SEQ, D_MODEL), *[grads[n] for n in order], *[deltas[n] for n in order],
            *[new_m[n] for n in order], *[new_v[n] for n in order])
```

```python
import jax
import jax.numpy as jnp
from jax import lax
from jax.experimental import pallas as pl
from jax.experimental.pallas import tpu as pltpu

F32 = jnp.float32
MM = jnp.bfloat16

SEQ = 2048
D_MODEL = 1024
CONV_DIM = 512
ATT_DIM = 512
CONV_WIDTH = 31
D_FF = 2816
IN_COLS = 2 * CONV_DIM + 3 * ATT_DIM + 2 * D_MODEL
N_CHIPS = 4
IN_SHARD = IN_COLS // N_CHIPS
UP_SHARD = 2 * D_FF // N_CHIPS
BR_SHARD = D_MODEL // N_CHIPS
EPS = 1e-6
ATT_SCALE = 0.125

TM = 256
GLU_ROWS = 256
TQ = 128
CONV_TILE = 64
CONV_WIN = CONV_TILE + 32
VMEM_LIMIT = 56 * 1024 * 1024

ADAM_LR = 0.001
ADAM_B1 = 0.9
ADAM_B2 = 0.999
ADAM_EPS = 1e-08
ADAM_WD = 0.01
ADAM_STEP = 10

MESH = pl.DeviceIdType.MESH
ANY = pl.BlockSpec(memory_space=pl.ANY)
VMEM_SPEC = pl.BlockSpec(memory_space=pltpu.VMEM)

NT_DIMS = (((1,), (1,)), ((), ()))
TN_DIMS = (((0,), (0,)), ((), ()))

IN_PIECES = (("ci", 0, 1024), ("q", 1024, 1536), ("k", 1536, 2048), ("v", 2048, 2560),
             ("gc", 2560, 3584), ("ga", 3584, 4608))


def _params(sem=None, vmem=VMEM_LIMIT):
    return pltpu.CompilerParams(dimension_semantics=sem, vmem_limit_bytes=vmem)


def _dot(a, b):
    return jnp.dot(a, b, preferred_element_type=F32)


def _dot_nt(a, b):
    return lax.dot_general(a, b, NT_DIMS, preferred_element_type=F32)


def _dot_tn(a, b):
    return lax.dot_general(a, b, TN_DIMS, preferred_element_type=F32)


def _sigmoid(x):
    return 1.0 / (1.0 + jnp.exp(-x))


def _rms(x):
    r = lax.rsqrt(jnp.mean(x * x, axis=-1, keepdims=True) + EPS)
    return x * r, r


def _rms_bwd(dy_g, n, r):
    return r * (dy_g - n * jnp.mean(dy_g * n, axis=-1, keepdims=True))


def _row_tile_spec(width, tm=TM):
    return pl.BlockSpec((tm, width), lambda i: (i, 0))


def _full_spec(shape):
    nd = len(shape)
    return pl.BlockSpec(shape, lambda *_: (0,) * nd)


def _weight_spec(shape):
    nd = len(shape)
    return pl.BlockSpec(shape, lambda *_: (0,) * nd, pipeline_mode=pl.Buffered(1))


def _acc_rows(ref, val, first):
    @pl.when(first)
    def _():
        ref[...] = val

    @pl.when(jnp.logical_not(first))
    def _():
        ref[...] += val


def _gather_behind_grid(ag, n_steps, step=None):
    step = pl.program_id(0) if step is None else step
    pl.when(step == 0)(ag.start)
    pl.when(step == n_steps - 2)(ag.forward)
    return lambda: pl.when(step == n_steps - 1)(ag.finish)


def in_proj_fwd(x, g1, w_in_g, gather=()):
    ng = len(gather)
    nt = SEQ // TM

    def body(*refs):
        x_ref, g_ref, w_ref = refs[:3]
        h_ref, ci_ref, q_ref, k_ref, v_ref, gc_ref, ga_ref = refs[3 + ng:10 + ng]
        if ng:
            done = _gather_behind_grid(_Gather([s.shape for s in gather], refs[3:3 + ng],
                                               refs[10 + ng:10 + 2 * ng], refs[10 + 2 * ng:]), nt)
        n, _ = _rms(x_ref[...])
        h = (n * g_ref[...]).astype(MM)
        h_ref[...] = h
        outs = dict(ci=ci_ref, q=q_ref, k=k_ref, v=v_ref, gc=gc_ref, ga=ga_ref)
        for j in range(N_CHIPS):
            p = _dot(h, w_ref[j])
            g0 = j * IN_SHARD
            for name, s, e in IN_PIECES:
                lo, hi = max(s, g0), min(e, g0 + IN_SHARD)
                if lo < hi:
                    ref = outs[name]
                    part = p[:, lo - g0:hi - g0]
                    if name == "q":
                        part = part * ATT_SCALE
                    ref[:, lo - s:hi - s] = part.astype(ref.dtype)
        if ng:
            done()

    out_shape = [
        jax.ShapeDtypeStruct((SEQ, D_MODEL), MM),
        jax.ShapeDtypeStruct((SEQ, 2 * CONV_DIM), F32),
        jax.ShapeDtypeStruct((SEQ, ATT_DIM), MM),
        jax.ShapeDtypeStruct((SEQ, ATT_DIM), MM),
        jax.ShapeDtypeStruct((SEQ, ATT_DIM), MM),
        jax.ShapeDtypeStruct((SEQ, D_MODEL), F32),
        jax.ShapeDtypeStruct((SEQ, D_MODEL), F32),
    ]
    return pl.pallas_call(
        body, name="in_proj_fwd", grid=(nt,),
        out_shape=out_shape + _Gather.out_shapes(gather),
        in_specs=[_row_tile_spec(D_MODEL), _full_spec((1, D_MODEL)), _weight_spec(w_in_g.shape)] + [ANY] * ng,
        out_specs=[_row_tile_spec(s.shape[1]) for s in out_shape] + [ANY] * ng,
        scratch_shapes=_Gather.scratch(gather) if ng else [],
        compiler_params=_params(("arbitrary",)),
    )(x, g1, w_in_g, *gather)


def _shifted_sum(win, terms):
    by_rot = {}
    for m, coef in terms:
        by_rot.setdefault(m % 8, []).append((m // 8, coef))
    acc = None
    n = win.shape[0]
    for rot in sorted(by_rot):
        shifted = win if rot == 0 else pltpu.roll(win, n - rot, 0)
        for a, coef in by_rot[rot]:
            t = coef * shifted[8 * a:8 * a + CONV_TILE, :]
            acc = t if acc is None else acc + t
    return acc


def _glu_into(ci_ref, upad_ref):
    upad_ref[0:32, :] = jnp.zeros((32, CONV_DIM), F32)

    def step(i, c):
        t0 = pl.multiple_of(i * GLU_ROWS, GLU_ROWS)
        a = ci_ref[pl.ds(t0, GLU_ROWS), 0:CONV_DIM]
        b = ci_ref[pl.ds(t0, GLU_ROWS), CONV_DIM:2 * CONV_DIM]
        upad_ref[pl.ds(t0 + 32, GLU_ROWS), :] = a * _sigmoid(b)
        return c

    lax.fori_loop(0, SEQ // GLU_ROWS, step, 0)


def _layernorm_parts(u1):
    mu = jnp.mean(u1, axis=-1, keepdims=True)
    xc = u1 - mu
    rstd = lax.rsqrt(jnp.mean(xc * xc, axis=-1, keepdims=True) + EPS)
    return xc * rstd, rstd


def conv_fwd(ci, w_dw, b_dw, ln_g, ln_b, gather=()):
    ng = len(gather)
    n_tiles = SEQ // CONV_TILE

    def body(*refs):
        ci_ref, w_ref, b_ref, g_ref, bb_ref = refs[:5]
        u1_ref, u3_ref = refs[5 + ng:7 + ng]
        upad_ref = refs[7 + 2 * ng]
        if ng:
            ag = _Gather([s.shape for s in gather], refs[5:5 + ng], refs[7 + ng:7 + 2 * ng], refs[8 + 2 * ng:])
            ag.start()
        _glu_into(ci_ref, upad_ref)

        def step(i, c):
            if ng:
                pl.when(i == n_tiles - n_tiles // 4)(ag.forward)
            t0 = pl.multiple_of(i * CONV_TILE, CONV_TILE)
            win = upad_ref[pl.ds(t0, CONV_WIN), :]
            u1 = _shifted_sum(win, [(j + 2, w_ref[j:j + 1, :]) for j in range(CONV_WIDTH)]) + b_ref[...]
            u1_ref[pl.ds(t0, CONV_TILE), :] = u1
            xh, _ = _layernorm_parts(u1)
            u2 = xh * g_ref[...] + bb_ref[...]
            u3_ref[pl.ds(t0, CONV_TILE), :] = (u2 * _sigmoid(u2)).astype(MM)
            return c

        lax.fori_loop(0, n_tiles, step, 0)
        if ng:
            ag.finish()

    return pl.pallas_call(
        body, name="conv_fwd",
        out_shape=[jax.ShapeDtypeStruct((SEQ, CONV_DIM), F32), jax.ShapeDtypeStruct((SEQ, CONV_DIM), MM)]
                  + _Gather.out_shapes(gather),
        in_specs=[VMEM_SPEC] * 5 + [ANY] * ng, out_specs=[VMEM_SPEC] * 2 + [ANY] * ng,
        scratch_shapes=[pltpu.VMEM((SEQ + 32, CONV_DIM), F32)] + (_Gather.scratch(gather) if ng else []),
        compiler_params=_params(),
    )(ci, w_dw, b_dw, ln_g, ln_b, *gather)


def _softplus(z):
    return jnp.maximum(z, 0.0) + jnp.log(1.0 + jnp.exp(-jnp.abs(z)))


def _cumsum_weights(suffix, with_total):
    n = 256 if with_total else 128
    r = lax.broadcasted_iota(jnp.int32, (128, n), 0)
    c = lax.broadcasted_iota(jnp.int32, (128, n), 1)
    tri = (r >= c) if suffix else (r <= c)
    return jnp.logical_or(tri, c >= 128).astype(MM)


NO_SCORE = -1e30
N_KB = SEQ // TQ


def _score_bias(lane, row, i, j):
    keep = jnp.logical_and(i >= 0, jnp.logical_or(j < i, lane < row))
    return jnp.where(keep, 0.0, NO_SCORE)


def _block_pipeline(n_stages, descending, step, on_query_block=None):
    n_lag = n_stages - 1
    none = jnp.int32(-1)

    def shift(cur, lag):
        step([cur] + [(lag[2 * s], lag[2 * s + 1]) for s in range(n_lag)])
        return (cur[0], cur[1]) + tuple(lag[:-2])

    def outer(i, lag):
        if on_query_block is not None:
            on_query_block(i)

        def inner(n, lag):
            return shift((i, i - n if descending else n), lag)
        return lax.fori_loop(0, i + 1, inner, lag)

    lag = lax.fori_loop(0, N_KB, outer, (none,) * (2 * n_lag))
    lax.fori_loop(0, n_lag, lambda n, lag: shift((none, none), lag), lag)


def _head_masks():
    lane = lax.broadcasted_iota(jnp.int32, (TQ, 128), 1)
    row = lax.broadcasted_iota(jnp.int32, (TQ, 128), 0)
    return lane, row, lane < 64


def _pick_head(x, head0, h):
    zero = jnp.zeros_like(x)
    return jnp.where(head0, x, zero) if h == 0 else jnp.where(head0, zero, x)


N_PAIRS = ATT_DIM // 128


def _split_heads(src_ref, dst_ref):
    _, _, head0 = _head_masks()

    def block(b, c):
        r0 = pl.multiple_of(b * TQ, TQ)
        d0 = pl.multiple_of(b * 2 * TQ, 2 * TQ)
        for p in range(N_PAIRS):
            x = src_ref[pl.ds(r0, TQ), 128 * p:128 * (p + 1)]
            for h in range(2):
                dst_ref[p, pl.ds(d0 + TQ * h, TQ), :] = _pick_head(x, head0, h)
        return c

    lax.fori_loop(0, N_KB, block, 0)


def attn_fwd(q, k, v, gather=()):
    ng = len(gather)

    def body(*refs):
        q_ref, k_ref, v_ref = refs[:3]
        o_ref, rc_ref = refs[3 + ng:5 + ng]
        acc_ref, r_ref, z_ref, spb_ref, ab_ref, qm_ref, vm_ref = refs[5 + 2 * ng:12 + 2 * ng]
        if ng:
            ag = _Gather([s.shape for s in gather], refs[3:3 + ng], refs[5 + ng:5 + 2 * ng], refs[12 + 2 * ng:])
            ag.start()
        lane, row, _ = _head_masks()
        w = _cumsum_weights(suffix=True, with_total=True)
        _split_heads(q_ref, qm_ref)
        _split_heads(v_ref, vm_ref)
        acc_ref[...] = jnp.zeros_like(acc_ref)
        r_ref[...] = jnp.zeros_like(r_ref)
        rc_ref[...] = jnp.zeros_like(rc_ref)
        z_ref[...] = jnp.full(z_ref.shape, NO_SCORE, F32)
        spb_ref[...] = jnp.zeros_like(spb_ref)
        ab_ref[...] = jnp.zeros_like(ab_ref)

        def step(pairs):
            (i1, j1), (i2, j2), (i3, j3) = pairs
            k1, q2, q3 = (pl.multiple_of(jnp.maximum(b, 0) * TQ, TQ) for b in (j1, i2, i3))
            q1, k3 = (pl.multiple_of(jnp.maximum(b, 0) * 2 * TQ, 2 * TQ) for b in (i1, j3))
            bias1 = _score_bias(lane, row, i1, j1)
            first2 = j2 == i2
            rc_rows = rc_ref[pl.ds(q2, TQ), :]
            for p in range(N_PAIRS):
                cols = slice(128 * p, 128 * (p + 1))
                kb = k_ref[pl.ds(k1, TQ), cols]
                acc_ref[pl.ds(q3, TQ), cols] += _dot(ab_ref[p], vm_ref[p, pl.ds(k3, 2 * TQ), :])
                for h in range(2):
                    hh = 2 * p + h
                    r = _dot(spb_ref[hh], w)
                    r_in = jnp.where(first2, 0.0, r_ref[hh])
                    ab_ref[p, :, 128 * h:128 * (h + 1)] = jnp.exp(z_ref[hh] - (r[:, :128] + r_in)).astype(MM)
                    rc_rows = jnp.where(jnp.logical_and(lane == 16 * hh + j2, i2 >= 0), r_in, rc_rows)
                    r_ref[hh] = r_in + r[:, 128:]
                    z = _dot_nt(qm_ref[p, pl.ds(q1 + TQ * h, TQ), :], kb) + bias1
                    z_ref[hh] = z
                    spb_ref[hh] = _softplus(z).astype(MM)
            rc_ref[pl.ds(q2, TQ), :] = rc_rows

        if ng:
            _block_pipeline(3, True, step, lambda i: pl.when(i == N_KB - 2)(ag.forward))
        else:
            _block_pipeline(3, True, step)
        o_ref[...] = acc_ref[...].astype(MM)
        if ng:
            ag.finish()

    out_shape = [jax.ShapeDtypeStruct((SEQ, ATT_DIM), MM), jax.ShapeDtypeStruct((SEQ, 128), F32)]
    out_shape += _Gather.out_shapes(gather)
    return pl.pallas_call(
        body, name="attn_fwd", out_shape=out_shape,
        in_specs=[VMEM_SPEC] * 3 + [ANY] * ng, out_specs=[VMEM_SPEC] * 2 + [ANY] * ng,
        scratch_shapes=[pltpu.VMEM((SEQ, ATT_DIM), F32), pltpu.VMEM((8, TQ, 128), F32),
                        pltpu.VMEM((8, TQ, 128), F32), pltpu.VMEM((8, TQ, 128), MM),
                        pltpu.VMEM((N_PAIRS, TQ, 256), MM), pltpu.VMEM((N_PAIRS, 2 * SEQ, 128), MM),
                        pltpu.VMEM((N_PAIRS, 2 * SEQ, 128), MM)]
                       + (_Gather.scratch(gather) if ng else []),
        compiler_params=_params(),
    )(q, k, v, *gather)


def mix_fwd(u3, att, gc, ga, x, w_cb_g, b_cb, w_ab_g, w_out_g, g2, g3, gather=()):
    ng = len(gather)
    nt = SEQ // TM

    def body(*refs):
        u_ref, a_ref, gc_ref, ga_ref, x_ref, wcb_ref, bcb_ref, wab_ref, wout_ref, g2_ref, g3_ref = refs[:11]
        co_ref, ao_ref, mg_ref, mix_ref, x2_ref, h2_ref = refs[11 + ng:17 + ng]
        if ng:
            done = _gather_behind_grid(_Gather([s.shape for s in gather], refs[11:11 + ng],
                                               refs[17 + ng:17 + 2 * ng], refs[17 + 2 * ng:]), nt)
        u = u_ref[...]
        a = a_ref[...]
        co = jnp.concatenate([_dot(u, wcb_ref[j]) for j in range(N_CHIPS)], axis=1) + bcb_ref[...]
        ao = jnp.concatenate([_dot(a, wab_ref[j]) for j in range(N_CHIPS)], axis=1)
        co_ref[...] = co.astype(MM)
        ao_ref[...] = ao.astype(MM)
        merged = (_sigmoid(gc_ref[...]) * co + _sigmoid(ga_ref[...]) * ao).astype(MM)
        mg_ref[...] = merged
        mix = _dot(merged, wout_ref[...])
        mix_ref[...] = mix
        n2, _ = _rms(mix)
        x2 = x_ref[...] + n2 * g2_ref[...]
        x2_ref[...] = x2
        n3, _ = _rms(x2)
        h2_ref[...] = (n3 * g3_ref[...]).astype(MM)
        if ng:
            done()

    out_shape = [
        jax.ShapeDtypeStruct((SEQ, D_MODEL), MM), jax.ShapeDtypeStruct((SEQ, D_MODEL), MM),
        jax.ShapeDtypeStruct((SEQ, D_MODEL), MM), jax.ShapeDtypeStruct((SEQ, D_MODEL), F32),
        jax.ShapeDtypeStruct((SEQ, D_MODEL), F32), jax.ShapeDtypeStruct((SEQ, D_MODEL), MM),
    ]
    vec = _full_spec((1, D_MODEL))
    return pl.pallas_call(
        body, name="mix_fwd", grid=(nt,),
        out_shape=out_shape + _Gather.out_shapes(gather),
        in_specs=[_row_tile_spec(CONV_DIM), _row_tile_spec(ATT_DIM), _row_tile_spec(D_MODEL),
                  _row_tile_spec(D_MODEL), _row_tile_spec(D_MODEL), _weight_spec(w_cb_g.shape), vec,
                  _weight_spec(w_ab_g.shape), _weight_spec(w_out_g.shape), vec, vec] + [ANY] * ng,
        out_specs=[_row_tile_spec(D_MODEL)] * 6 + [ANY] * ng,
        scratch_shapes=_Gather.scratch(gather) if ng else [],
        compiler_params=_params(("arbitrary",)),
    )(u3, att, gc, ga, x, w_cb_g, b_cb, w_ab_g, w_out_g, g2, g3, *gather)


def ffn_up_fwd(h2, w_up_g, gather=()):
    ng = len(gather)
    nt = SEQ // TM

    def body(*refs):
        h_ref, wg_ref, wu_ref = refs[:3]
        gate_ref, up_ref, act_ref = refs[3 + ng:6 + ng]
        if ng:
            done = _gather_behind_grid(_Gather([s.shape for s in gather], refs[3:3 + ng],
                                               refs[6 + ng:6 + 2 * ng], refs[6 + 2 * ng:]),
                                       2 * nt, pl.program_id(0) * nt + pl.program_id(1))
        h = h_ref[...]
        gate = _dot(h, wg_ref[0])
        up = _dot(h, wu_ref[0])
        gate_ref[...] = gate.astype(MM)
        up_ref[...] = up.astype(MM)
        act_ref[...] = (gate * _sigmoid(gate) * up).astype(MM)
        if ng:
            done()

    tile = pl.BlockSpec((TM, UP_SHARD), lambda n, i: (i, n))
    act = jax.ShapeDtypeStruct((SEQ, D_FF), MM)
    return pl.pallas_call(
        body, name="ffn_up_fwd", grid=(2, nt), out_shape=[act, act, act] + _Gather.out_shapes(gather),
        in_specs=[pl.BlockSpec((TM, D_MODEL), lambda n, i: (i, 0)),
                  pl.BlockSpec((1, D_MODEL, UP_SHARD), lambda n, i: (n, 0, 0)),
                  pl.BlockSpec((1, D_MODEL, UP_SHARD), lambda n, i: (n + 2, 0, 0))] + [ANY] * ng,
        out_specs=[tile, tile, tile] + [ANY] * ng,
        scratch_shapes=_Gather.scratch(gather) if ng else [],
        compiler_params=_params(("arbitrary", "arbitrary")),
    )(h2, w_up_g, w_up_g, *gather)


def ffn_down_loss(act, w_down_g, x2, target, g4):
    def body(act_ref, wd_ref, x2_ref, t_ref, g_ref, dff_ref, dy_ref, loss_ref, dg_ref):
        ff = _dot(act_ref[...], wd_ref[...])
        n4, r4 = _rms(ff)
        g4v = g_ref[...]
        err = x2_ref[...] + n4 * g4v - t_ref[...]
        row_loss = jnp.mean(err * err, axis=-1, keepdims=True)
        loss_ref[...] = jnp.zeros((8, 128), F32) + 0.5 * jnp.sum(row_loss, axis=0, keepdims=True)
        dy = err * (1.0 / D_MODEL)
        dy_ref[...] = dy
        dff_ref[...] = _rms_bwd(dy * g4v, n4, r4).astype(MM)
        _acc_rows(dg_ref, jnp.sum(dy * n4, axis=0, keepdims=True), pl.program_id(0) == 0)

    nt = SEQ // TM
    vec = _full_spec((1, D_MODEL))
    return pl.pallas_call(
        body, name="ffn_down_loss", grid=(nt,),
        out_shape=(jax.ShapeDtypeStruct((SEQ, D_MODEL), MM), jax.ShapeDtypeStruct((SEQ, D_MODEL), F32),
                   jax.ShapeDtypeStruct((nt * 8, 128), F32), jax.ShapeDtypeStruct((1, D_MODEL), F32)),
        in_specs=[_row_tile_spec(D_FF), _weight_spec(w_down_g.shape), _row_tile_spec(D_MODEL),
                  _row_tile_spec(D_MODEL), vec],
        out_specs=[_row_tile_spec(D_MODEL), _row_tile_spec(D_MODEL),
                   pl.BlockSpec((8, 128), lambda i: (i, 0)), vec],
        compiler_params=_params(("arbitrary",)),
    )(act, w_down_g, x2, target, g4)


def ffn_act_bwd(dff, w_down_g, gate, up):
    def body(dff_ref, wd_ref, gate_ref, up_ref, dgu_ref):
        dact = _dot_nt(dff_ref[...], wd_ref[...])
        gate = gate_ref[...].astype(F32)
        sg = _sigmoid(gate)
        dgu_ref[:, 0:D_FF] = (dact * up_ref[...].astype(F32) * (sg * (1.0 + gate * (1.0 - sg)))).astype(MM)
        dgu_ref[:, D_FF:2 * D_FF] = (dact * (gate * sg)).astype(MM)

    return pl.pallas_call(
        body, name="ffn_act_bwd", grid=(SEQ // TM,),
        out_shape=jax.ShapeDtypeStruct((SEQ, 2 * D_FF), MM),
        in_specs=[_row_tile_spec(D_MODEL), _weight_spec(w_down_g.shape), _row_tile_spec(D_FF), _row_tile_spec(D_FF)],
        out_specs=_row_tile_spec(2 * D_FF),
        compiler_params=_params(("arbitrary",)),
    )(dff, w_down_g, gate, up)


def ffn_in_bwd(dgu, w_up_g, x2, mix, dy, g3, g2):
    def body(dgu_ref, w_ref, x2_ref, mix_ref, dy_ref, g3_ref, g2_ref, dx2_ref, dmix_ref, dg3_ref, dg2_ref):
        dh2 = None
        for j in range(N_CHIPS):
            t = _dot_nt(dgu_ref[:, j * UP_SHARD:(j + 1) * UP_SHARD], w_ref[j])
            dh2 = t if dh2 is None else dh2 + t
        first = pl.program_id(0) == 0
        n3, r3 = _rms(x2_ref[...])
        dx2 = dy_ref[...] + _rms_bwd(dh2 * g3_ref[...], n3, r3)
        dx2_ref[...] = dx2
        _acc_rows(dg3_ref, jnp.sum(dh2 * n3, axis=0, keepdims=True), first)
        n2, r2 = _rms(mix_ref[...])
        dmix_ref[...] = _rms_bwd(dx2 * g2_ref[...], n2, r2).astype(MM)
        _acc_rows(dg2_ref, jnp.sum(dx2 * n2, axis=0, keepdims=True), first)

    vec = _full_spec((1, D_MODEL))
    return pl.pallas_call(
        body, name="ffn_in_bwd", grid=(SEQ // TM,),
        out_shape=(jax.ShapeDtypeStruct((SEQ, D_MODEL), F32), jax.ShapeDtypeStruct((SEQ, D_MODEL), MM),
                   jax.ShapeDtypeStruct((1, D_MODEL), F32), jax.ShapeDtypeStruct((1, D_MODEL), F32)),
        in_specs=[_row_tile_spec(2 * D_FF), _weight_spec(w_up_g.shape), _row_tile_spec(D_MODEL),
                  _row_tile_spec(D_MODEL), _row_tile_spec(D_MODEL), vec, vec],
        out_specs=[_row_tile_spec(D_MODEL), _row_tile_spec(D_MODEL), vec, vec],
        compiler_params=_params(("arbitrary",)),
    )(dgu, w_up_g, x2, mix, dy, g3, g2)


def merge_bwd(dmix, w_out_g, gc, ga, co, ao, w_cb_g, w_ab_g):
    def body(dmix_ref, wout_ref, gc_ref, ga_ref, co_ref, ao_ref, wcb_ref, wab_ref,
             dco_ref, dao_ref, dg_ref, du3_ref, datt_ref, dbcb_ref):
        dm = _dot_nt(dmix_ref[...], wout_ref[...])
        sgc = _sigmoid(gc_ref[...])
        sga = _sigmoid(ga_ref[...])
        dco = dm * sgc
        dao = dm * sga
        dg_ref[:, 0:D_MODEL] = (dm * co_ref[...].astype(F32) * (sgc * (1.0 - sgc))).astype(MM)
        dg_ref[:, D_MODEL:2 * D_MODEL] = (dm * ao_ref[...].astype(F32) * (sga * (1.0 - sga))).astype(MM)
        _acc_rows(dbcb_ref, jnp.sum(dco, axis=0, keepdims=True), pl.program_id(0) == 0)
        dco_ref[...] = dco.astype(MM)
        dao_ref[...] = dao.astype(MM)
        du3 = None
        datt = None
        for j in range(N_CHIPS):
            cols = slice(j * BR_SHARD, (j + 1) * BR_SHARD)
            t = _dot_nt(dco_ref[:, cols], wcb_ref[j])
            s = _dot_nt(dao_ref[:, cols], wab_ref[j])
            du3 = t if du3 is None else du3 + t
            datt = s if datt is None else datt + s
        du3_ref[...] = du3
        datt_ref[...] = datt.astype(MM)

    wide = _row_tile_spec(D_MODEL)
    return pl.pallas_call(
        body, name="merge_bwd", grid=(SEQ // TM,),
        out_shape=(jax.ShapeDtypeStruct((SEQ, D_MODEL), MM), jax.ShapeDtypeStruct((SEQ, D_MODEL), MM),
                   jax.ShapeDtypeStruct((SEQ, 2 * D_MODEL), MM),
                   jax.ShapeDtypeStruct((SEQ, CONV_DIM), F32), jax.ShapeDtypeStruct((SEQ, ATT_DIM), MM),
                   jax.ShapeDtypeStruct((1, D_MODEL), F32)),
        in_specs=[wide, _weight_spec(w_out_g.shape), wide, wide, wide, wide,
                  _weight_spec(w_cb_g.shape), _weight_spec(w_ab_g.shape)],
        out_specs=[wide, wide, _row_tile_spec(2 * D_MODEL), _row_tile_spec(CONV_DIM), _row_tile_spec(ATT_DIM),
                   _full_spec((1, D_MODEL))],
        compiler_params=_params(("arbitrary",)),
    )(dmix, w_out_g, gc, ga, co, ao, w_cb_g, w_ab_g)


def conv_bwd(du3, u1, ci, w_dw, ln_g, ln_b):
    def body(du3_ref, u1_ref, ci_ref, w_ref, g_ref, bb_ref,
             dci_ref, dw_ref, dbdw_ref, dg_ref, db_ref, upad_ref, dpad_ref, dwacc_ref, vacc_ref):
        _glu_into(ci_ref, upad_ref)
        dpad_ref[SEQ:SEQ + 32, :] = jnp.zeros((32, CONV_DIM), F32)
        dwacc_ref[...] = jnp.zeros_like(dwacc_ref)
        vacc_ref[...] = jnp.zeros_like(vacc_ref)

        def fold8(t):
            s = t[0:8, :]
            for r in range(1, CONV_TILE // 8):
                s = s + t[8 * r:8 * r + 8, :]
            return s

        def pass1(i, c):
            t0 = pl.multiple_of(i * CONV_TILE, CONV_TILE)
            xh, rstd = _layernorm_parts(u1_ref[pl.ds(t0, CONV_TILE), :])
            gv = g_ref[...]
            u2 = xh * gv + bb_ref[...]
            s2 = _sigmoid(u2)
            du2 = du3_ref[pl.ds(t0, CONV_TILE), :] * (s2 * (1.0 + u2 * (1.0 - s2)))
            wv = du2 * gv
            du1 = rstd * (wv - jnp.mean(wv, axis=-1, keepdims=True)
                          - xh * jnp.mean(wv * xh, axis=-1, keepdims=True))
            dpad_ref[pl.ds(t0, CONV_TILE), :] = du1
            vacc_ref[0] += fold8(du2 * xh)
            vacc_ref[1] += fold8(du2)
            vacc_ref[2] += fold8(du1)
            win = upad_ref[pl.ds(t0, CONV_WIN), :]
            n = win.shape[0]
            for rot in range(8):
                shifted = win if rot == 0 else pltpu.roll(win, n - rot, 0)
                for a in range(5):
                    j = 8 * a + rot - 2
                    if 0 <= j < CONV_WIDTH:
                        dwacc_ref[j] += fold8(du1 * shifted[8 * a:8 * a + CONV_TILE, :])
            return c

        lax.fori_loop(0, SEQ // CONV_TILE, pass1, 0)

        def pass2(i, c):
            t0 = pl.multiple_of(i * CONV_TILE, CONV_TILE)
            win = dpad_ref[pl.ds(t0, CONV_WIN), :]
            du0 = _shifted_sum(win, [(30 - j, w_ref[j:j + 1, :]) for j in range(CONV_WIDTH)])
            a = ci_ref[pl.ds(t0, CONV_TILE), 0:CONV_DIM]
            sb = _sigmoid(ci_ref[pl.ds(t0, CONV_TILE), CONV_DIM:2 * CONV_DIM])
            dci_ref[pl.ds(t0, CONV_TILE), 0:CONV_DIM] = (du0 * sb).astype(MM)
            dci_ref[pl.ds(t0, CONV_TILE), CONV_DIM:2 * CONV_DIM] = (du0 * a * (sb * (1.0 - sb))).astype(MM)
            return c

        lax.fori_loop(0, SEQ // CONV_TILE, pass2, 0)

        for j in range(CONV_WIDTH):
            dw_ref[j:j + 1, :] = jnp.sum(dwacc_ref[j], axis=0, keepdims=True)
        dw_ref[CONV_WIDTH:32, :] = jnp.zeros((32 - CONV_WIDTH, CONV_DIM), F32)
        dg_ref[...] = jnp.sum(vacc_ref[0], axis=0, keepdims=True)
        db_ref[...] = jnp.sum(vacc_ref[1], axis=0, keepdims=True)
        dbdw_ref[...] = jnp.sum(vacc_ref[2], axis=0, keepdims=True)

    vec = jax.ShapeDtypeStruct((1, CONV_DIM), F32)
    return pl.pallas_call(
        body, name="conv_bwd",
        out_shape=(jax.ShapeDtypeStruct((SEQ, 2 * CONV_DIM), MM), jax.ShapeDtypeStruct((32, CONV_DIM), F32),
                   vec, vec, vec),
        in_specs=[VMEM_SPEC] * 6, out_specs=[VMEM_SPEC] * 5,
        scratch_shapes=[pltpu.VMEM((SEQ + 32, CONV_DIM), F32), pltpu.VMEM((SEQ + 32, CONV_DIM), F32),
                        pltpu.VMEM((CONV_WIDTH, 8, CONV_DIM), F32), pltpu.VMEM((3, 8, CONV_DIM), F32)],
        compiler_params=_params(),
    )(du3, u1, ci, w_dw, ln_g, ln_b)


def attn_bwd(q, k, v, datt, rc):
    def body(q_ref, k_ref, v_ref, do_ref, rc_ref, dqkv_ref, dqa_ref, dka_ref, dva_ref, pc_ref, z_ref, sig1_ref,
             sig2_ref, g_ref, spb_ref, gb_ref, ar_ref, dzr_ref, dzc_ref, qm_ref, km_ref, dom_ref):
        lane, row, _ = _head_masks()
        _split_heads(q_ref, qm_ref)
        _split_heads(k_ref, km_ref)
        _split_heads(do_ref, dom_ref)
        for ref in (dqa_ref, dka_ref, dva_ref, pc_ref):
            ref[...] = jnp.zeros_like(ref)
        z_ref[...] = jnp.full(z_ref.shape, NO_SCORE, F32)
        for ref in (sig1_ref, sig2_ref, spb_ref, ar_ref, g_ref, gb_ref, dzr_ref, dzc_ref):
            ref[...] = jnp.zeros_like(ref)
        w_suffix = _cumsum_weights(suffix=True, with_total=False)
        w_prefix = _cumsum_weights(suffix=False, with_total=True)

        def step(pairs):
            (ia, ja), (ib, jb), (ic, jc), (id_, jd) = pairs
            ka, qb_, kb_, kc, qd, kd = (pl.multiple_of(jnp.maximum(b, 0) * TQ, TQ) for b in (ja, ib, jb, jc, id_, jd))
            qa2, qb2, qc2, qd2, kd2 = (pl.multiple_of(jnp.maximum(b, 0) * 2 * TQ, 2 * TQ)
                                       for b in (ia, ib, ic, id_, jd))
            bias_a = _score_bias(lane, row, ia, ja)
            rc_rows = rc_ref[pl.ds(qb_, TQ), :]
            first_c = jc == 0
            for p in range(N_PAIRS):
                cols = slice(128 * p, 128 * (p + 1))
                k_a = k_ref[pl.ds(ka, TQ), cols]
                v_b = v_ref[pl.ds(kb_, TQ), cols]
                dqa_ref[pl.ds(qd, TQ), cols] += _dot(dzc_ref[p], km_ref[p, pl.ds(kd2, 2 * TQ), :])
                dka_ref[pl.ds(kd, TQ), cols] += _dot_tn(dzr_ref[p], qm_ref[p, pl.ds(qd2, 2 * TQ), :])
                dva_ref[pl.ds(kc, TQ), cols] += _dot_tn(ar_ref[p], dom_ref[p, pl.ds(qc2, 2 * TQ), :])
                for h in range(2):
                    hh = 2 * p + h
                    rows = slice(TQ * h, TQ * (h + 1))
                    r = _dot(gb_ref[hh], w_prefix)
                    p_in = jnp.where(first_c, 0.0, pc_ref[hh])
                    dz = (g_ref[hh] - sig2_ref[hh] * (r[:, :128] + p_in)).astype(MM)
                    dzc_ref[p, :, rows] = dz
                    dzr_ref[p, rows, :] = dz
                    pc_ref[hh] = p_in + r[:, 128:]
                    r_in = jnp.sum(jnp.where(lane == 16 * hh + jb, rc_rows, 0.0), axis=1, keepdims=True)
                    a = jnp.exp(z_ref[hh] - (_dot(spb_ref[hh], w_suffix) + r_in))
                    g = _dot_nt(dom_ref[p, pl.ds(qb2 + TQ * h, TQ), :], v_b) * a
                    ar_ref[p, rows, :] = a.astype(MM)
                    g_ref[hh] = g
                    gb_ref[hh] = g.astype(MM)
                    sig2_ref[hh] = sig1_ref[hh]
                    z = _dot_nt(qm_ref[p, pl.ds(qa2 + TQ * h, TQ), :], k_a) + bias_a
                    sp = _softplus(z)
                    sig1_ref[hh] = jnp.exp(z - sp)
                    z_ref[hh] = z
                    spb_ref[hh] = sp.astype(MM)

        _block_pipeline(4, False, step)
        dqkv_ref[:, 0:ATT_DIM] = (dqa_ref[...] * ATT_SCALE).astype(MM)
        dqkv_ref[:, ATT_DIM:2 * ATT_DIM] = dka_ref[...].astype(MM)
        dqkv_ref[:, 2 * ATT_DIM:3 * ATT_DIM] = dva_ref[...].astype(MM)

    split = pltpu.VMEM((N_PAIRS, 2 * SEQ, 128), MM)
    return pl.pallas_call(
        body, name="attn_bwd", out_shape=jax.ShapeDtypeStruct((SEQ, 3 * ATT_DIM), MM),
        in_specs=[VMEM_SPEC] * 5, out_specs=VMEM_SPEC,
        scratch_shapes=[pltpu.VMEM((SEQ, ATT_DIM), F32)] * 3 + [pltpu.VMEM((8, TQ, 128), F32)] * 5
                       + [pltpu.VMEM((8, TQ, 128), MM)] * 2
                       + [pltpu.VMEM((N_PAIRS, 2 * TQ, 128), MM)] * 2 + [pltpu.VMEM((N_PAIRS, TQ, 256), MM)]
                       + [split] * 3,
        compiler_params=_params(),
    )(q, k, v, datt, rc)


DPROJ_PIECES = ((0, 1024), (1024, 2560), (2560, 4608))


def _dproj_segments(j):
    g0, g1 = j * IN_SHARD, (j + 1) * IN_SHARD
    segs = []
    for p, (s, e) in enumerate(DPROJ_PIECES):
        lo, hi = max(s, g0), min(e, g1)
        if lo < hi:
            segs.append((p, lo - s, lo - g0, hi - lo))
    return segs


def in_proj_bwd(pieces, w_in_g, x, dx2, g1, scatter=()):
    ns = len(scatter)
    nt = SEQ // TM

    def body(*refs):
        p_refs = refs[:3]
        w_ref, x_ref, dx2_ref, g_ref = refs[3:7]
        dx_ref, dg_ref = refs[7 + ns:9 + ns]
        if ns:
            sc = _Scatter(refs[7:7 + ns], refs[9 + ns:9 + 2 * ns], refs[9 + 2 * ns:])
            pl.when(pl.program_id(0) == 0)(sc.start)
        dh = None
        for j in range(N_CHIPS):
            for p, lo, off, width in _dproj_segments(j):
                t = _dot_nt(p_refs[p][:, lo:lo + width], w_ref[j, :, off:off + width])
                dh = t if dh is None else dh + t
        n1, r1 = _rms(x_ref[...])
        dx_ref[...] = dx2_ref[...] + _rms_bwd(dh * g_ref[...], n1, r1)
        _acc_rows(dg_ref, jnp.sum(dh * n1, axis=0, keepdims=True), pl.program_id(0) == 0)
        if ns:
            pl.when(pl.program_id(0) == nt - 1)(sc.finish)

    vec = _full_spec((1, D_MODEL))
    return pl.pallas_call(
        body, name="in_proj_bwd", grid=(nt,),
        out_shape=[jax.ShapeDtypeStruct((SEQ, D_MODEL), F32), jax.ShapeDtypeStruct((1, D_MODEL), F32)]
                  + _Scatter.out_shapes(scatter),
        in_specs=[_row_tile_spec(p.shape[1]) for p in pieces]
                 + [_weight_spec(w_in_g.shape), _row_tile_spec(D_MODEL), _row_tile_spec(D_MODEL), vec] + [ANY] * ns,
        out_specs=[_row_tile_spec(D_MODEL), vec] + [ANY] * ns,
        scratch_shapes=_Scatter.scratch(ns) if ns else [],
        compiler_params=_params(("arbitrary",)),
    )(*pieces, w_in_g, x, dx2, g1, *scatter)


def weight_grad_in(h1, pieces):
    kh = D_MODEL // 2

    def body(a_ref, p0_ref, p1_ref, p2_ref, o_ref):
        p_refs = (p0_ref, p1_ref, p2_ref)
        a = a_ref[...]
        for j in range(N_CHIPS):
            @pl.when(pl.program_id(1) == j)
            def _():
                for p, lo, off, width in _dproj_segments(j):
                    o_ref[0, 0, :, off:off + width] = _dot_tn(a, p_refs[p][:, lo:lo + width]).astype(MM)

    return pl.pallas_call(
        body, name="dw_in", grid=(2, N_CHIPS), out_shape=jax.ShapeDtypeStruct((N_CHIPS, 2, kh, IN_SHARD), MM),
        in_specs=[pl.BlockSpec((SEQ, kh), lambda h, j: (0, h))] + [_weight_spec(p.shape) for p in pieces],
        out_specs=pl.BlockSpec((1, 1, kh, IN_SHARD), lambda h, j: (j, h, 0, 0)),
        compiler_params=_params(("arbitrary", "arbitrary")),
    )(h1, *pieces)


def weight_grad(a, b, name, col_sharded, tk=None):
    kin, n = a.shape[1], b.shape[1]

    def body(a_ref, b_ref, o_ref):
        if col_sharded:
            o_ref[0, 0] = _dot_tn(a_ref[...], b_ref[...]).astype(MM)
        else:
            o_ref[...] = _dot_tn(a_ref[...], b_ref[...]).astype(MM)

    if col_sharded:
        kh, ns = kin // 2, n // N_CHIPS
        out = jax.ShapeDtypeStruct((N_CHIPS, 2, kh, ns), MM)
        grid = (2, N_CHIPS)
        in_specs = [pl.BlockSpec((SEQ, kh), lambda h, j: (0, h)), pl.BlockSpec((SEQ, ns), lambda h, j: (0, j))]
        out_spec = pl.BlockSpec((1, 1, kh, ns), lambda h, j: (j, h, 0, 0))
        sem = ("arbitrary", "arbitrary")
    else:
        out = jax.ShapeDtypeStruct((kin, n), MM)
        grid = (kin // tk,)
        in_specs = [pl.BlockSpec((SEQ, tk), lambda r: (0, r)), pl.BlockSpec((SEQ, n), lambda r: (0, 0))]
        out_spec = pl.BlockSpec((tk, n), lambda r: (r, 0))
        sem = ("arbitrary",)
    res = pl.pallas_call(
        body, name=name, grid=grid, out_shape=out, in_specs=in_specs, out_specs=out_spec,
        compiler_params=_params(sem),
    )(a, b)
    if not col_sharded:
        res = res.reshape(N_CHIPS, 2, kin // (2 * N_CHIPS), n)
    return res


def _place():
    x, y, c = lax.axis_index("x"), lax.axis_index("y"), lax.axis_index("c")
    chips = [(1 - x, y), (x, 1 - y), (1 - x, 1 - y)]
    return x, y, c, chips


def _rcopy(src, dst, send_sem, recv_sem, dev):
    return pltpu.make_async_remote_copy(src_ref=src, dst_ref=dst, send_sem=send_sem, recv_sem=recv_sem,
                                        device_id=dev, device_id_type=MESH)


class _Gather:
    def __init__(self, shapes, w, o, scratch):
        self.n, self.shapes, self.w, self.o = len(w), shapes, w, o
        self.send, self.recv, self.fsend, self.frecv, self.loc_in, self.loc_out = scratch[:6]
        self.raw, self.stage = scratch[6:6 + self.n], scratch[6 + self.n:]
        self.x, self.y, self.c, self.chips = _place()
        self.me = 2 * self.x + self.y
        self.sib = (self.x, self.y, 1 - self.c)
        self.pairs = [(j, t) for j in range(3) for t in range(self.n)]

    @staticmethod
    def scratch(shards):
        n = len(shards)
        sems = pltpu.SemaphoreType.DMA
        return ([sems((3 * n,)), sems((3 * n,)), sems((3 * n,)), sems((3 * n,)), sems((n,)), sems((n,))]
                + [pltpu.VMEM(s.shape, s.dtype) for s in shards] + [pltpu.VMEM(s.shape, MM) for s in shards])

    @staticmethod
    def out_shapes(shards):
        return [jax.ShapeDtypeStruct((N_CHIPS,) + s.shape, MM) for s in shards]

    def _half(self, t, k, cc):
        rh = self.shapes[t][0] // 2
        return self.o[t].at[k, pl.ds(cc * rh, rh), :]

    def _chip(self, j):
        cx, cy = self.chips[j]
        return 2 * cx + cy, (cx, cy, self.c)

    def local_in(self, t):
        return pltpu.make_async_copy(self.w[t], self.raw[t], self.loc_in.at[t])

    def local_out(self, t):
        return pltpu.make_async_copy(self.stage[t], self.o[t].at[self.me], self.loc_out.at[t])

    def first(self, j, t):
        rh = self.shapes[t][0] // 2
        i = j * self.n + t
        return _rcopy(self.stage[t].at[pl.ds(self.c * rh, rh), :], self._half(t, self.me, self.c),
                      self.send.at[i], self.recv.at[i], self._chip(j)[1])

    def arrived(self, j, t):
        k, dev = self._chip(j)
        i = j * self.n + t
        blk = self._half(t, k, self.c)
        return _rcopy(blk, blk, self.send.at[i], self.recv.at[i], dev)

    def passed(self, j, t, cc):
        i = j * self.n + t
        blk = self._half(t, self._chip(j)[0], cc)
        return _rcopy(blk, blk, self.fsend.at[i], self.frecv.at[i], self.sib)

    def start(self):
        for t in range(self.n):
            self.local_in(t).start()
        for t in range(self.n):
            self.local_in(t).wait()
            self.stage[t][...] = self.raw[t][...].astype(MM)
            self.local_out(t).start()
        for j, t in self.pairs:
            self.first(j, t).start()

    def forward(self):
        for j, t in self.pairs:
            self.arrived(j, t).wait_recv()
            self.passed(j, t, self.c).start()

    def finish(self):
        for j, t in self.pairs:
            self.passed(j, t, 1 - self.c).wait_recv()
        for j, t in self.pairs:
            self.first(j, t).wait_send()
            self.passed(j, t, self.c).wait_send()
        for t in range(self.n):
            self.local_out(t).wait()


def all_gather_weights(shards, small, later):
    n, m = len(shards), len(later)
    shapes = [s.shape for s in shards]

    def body(*refs):
        w = refs[:n]
        sm = refs[n]
        lw = refs[n + 1:n + 1 + m]
        o = refs[n + 1 + m:2 * n + 1 + m]
        osm = refs[2 * n + 1 + m]
        lo = refs[2 * n + 2 + m:2 * n + 2 + 2 * m]
        scratch = refs[2 * n + 2 + 2 * m:]
        ssend, srecv, sloc, lsem_in, lsem_out = scratch[:5]
        lraw, lstage = scratch[5:5 + m], scratch[5 + m:5 + 2 * m]
        g = _Gather(shapes, w, o, scratch[5 + 2 * m:])
        own = pltpu.make_async_copy(sm, osm.at[g.me], sloc)
        own.start()
        loads = [pltpu.make_async_copy(lw[t], lraw[t], lsem_in.at[t]) for t in range(m)]
        for cp in loads:
            cp.start()
        g.start()
        small_cps = [_rcopy(sm, osm.at[g.me], ssend.at[j], srecv.at[j], g._chip(j)[1]) for j in range(3)]
        for cp in small_cps:
            cp.start()
        places = []
        for t in range(m):
            loads[t].wait()
            lstage[t][...] = lraw[t][...].astype(MM)
            places.append(pltpu.make_async_copy(lstage[t], lo[t].at[g.me], lsem_out.at[t]))
            places[t].start()
        g.forward()
        g.finish()
        for j in range(3):
            k, dev = g._chip(j)
            _rcopy(sm, osm.at[k], ssend.at[j], srecv.at[j], dev).wait_recv()
            small_cps[j].wait_send()
        own.wait()
        for cp in places:
            cp.wait()

    out_shape = _Gather.out_shapes(shards)
    out_shape.append(jax.ShapeDtypeStruct((N_CHIPS,) + small.shape, small.dtype))
    out_shape += _Gather.out_shapes(later)
    sems = pltpu.SemaphoreType.DMA
    return pl.pallas_call(
        body, name="all_gather_weights", out_shape=out_shape,
        in_specs=[ANY] * (n + 1 + m), out_specs=[ANY] * (n + 1 + m),
        scratch_shapes=[sems((3,)), sems((3,)), sems, sems((m,)), sems((m,))]
                       + [pltpu.VMEM(s.shape, s.dtype) for s in later] + [pltpu.VMEM(s.shape, MM) for s in later]
                       + _Gather.scratch(shards),
        compiler_params=_params(),
    )(*shards, small, *later)


def sibling_exchange(grads, name):
    n = len(grads)

    def body(*refs):
        g = refs[:n]
        o = refs[n:2 * n]
        send, recv = refs[2 * n:]
        x, y, c, _ = _place()
        cps = [_rcopy(g[t].at[:, 1 - c], o[t], send.at[t], recv.at[t], (x, y, 1 - c)) for t in range(n)]
        for cp in cps:
            cp.start()
        for cp in cps:
            cp.wait()

    sems = pltpu.SemaphoreType.DMA
    return pl.pallas_call(
        body, name=name,
        out_shape=[jax.ShapeDtypeStruct((a.shape[0],) + a.shape[2:], a.dtype) for a in grads],
        in_specs=[ANY] * n, out_specs=[ANY] * n, scratch_shapes=[sems((n,)), sems((n,))],
    )(*grads)


class _Scatter:
    def __init__(self, p, o, sems):
        self.n, self.p, self.o = len(p), p, o
        self.send, self.recv = sems
        _, _, self.c, self.chips = _place()

    @staticmethod
    def scratch(n):
        sems = pltpu.SemaphoreType.DMA
        return [sems((3 * n,)), sems((3 * n,))]

    @staticmethod
    def out_shapes(parts):
        return [jax.ShapeDtypeStruct((3,) + a.shape[1:], a.dtype) for a in parts]

    def copies(self):
        cps = []
        for j, (cx, cy) in enumerate(self.chips):
            for t in range(self.n):
                i = j * self.n + t
                cps.append(_rcopy(self.p[t].at[2 * cx + cy], self.o[t].at[j], self.send.at[i], self.recv.at[i],
                                  (cx, cy, self.c)))
        return cps

    def start(self):
        for cp in self.copies():
            cp.start()

    def finish(self):
        for cp in self.copies():
            cp.wait()


HBM_SPEC = pl.BlockSpec(memory_space=pltpu.HBM)
SEM_SPEC = pl.BlockSpec(memory_space=pltpu.SEMAPHORE)
DATAFLOW = pltpu.SideEffectType.DATAFLOW_SIDE_EFFECTING


def split_start(name, bufs, n_copies, copies):
    nb = len(bufs)

    def body(*refs):
        for cp in copies(refs[:nb], refs[nb], refs[nb + 1]):
            cp.start()
        token = refs[2 * nb + 2]
        token[...] = jnp.zeros_like(token)

    sems = [pltpu.SemaphoreType.DMA((n_copies,))] * 2
    res = pl.pallas_call(
        body, name=name,
        out_shape=sems + [pltpu.HBM(a.shape, a.dtype) for a in bufs] + [jax.ShapeDtypeStruct((8, 128), F32)],
        in_specs=[HBM_SPEC] * nb, out_specs=[SEM_SPEC] * 2 + [HBM_SPEC] * nb + [VMEM_SPEC],
        input_output_aliases={i: 2 + i for i in range(nb)},
        compiler_params=pltpu.CompilerParams(has_side_effects=DATAFLOW),
    )(*[pltpu.with_memory_space_constraint(a, pltpu.HBM) for a in bufs])
    return res[:-1], res[-1]


def split_wait(name, state, after, copies):
    sems, bufs = state[:2], state[2:]
    nb = len(bufs)

    def body(*refs):
        for cp in copies(refs[:nb], refs[nb], refs[nb + 1]):
            cp.wait_send()
            cp.wait_recv()

    return pl.pallas_call(
        body, name=name, out_shape=[pltpu.HBM(a.shape, a.dtype) for a in bufs],
        in_specs=[HBM_SPEC] * nb + [SEM_SPEC] * 2 + [ANY] * len(after), out_specs=[HBM_SPEC] * nb,
        input_output_aliases={i: i for i in range(nb)},
        compiler_params=pltpu.CompilerParams(has_side_effects=DATAFLOW),
    )(*bufs, *sems, *after)


def _scatter_copies(n):
    def copies(refs, send, recv):
        _, _, c, chips = _place()
        return [_rcopy(refs[t].at[2 * cx + cy], refs[n + t].at[j], send.at[3 * t + j], recv.at[3 * t + j], (cx, cy, c))
                for t in range(n) for j, (cx, cy) in enumerate(chips)]
    return copies


def scatter_start(parts, tag):
    lands = [lax.empty((3,) + p.shape[1:], p.dtype) for p in parts]
    return split_start("scatter_start_" + tag, list(parts) + lands, 3 * len(parts), _scatter_copies(len(parts)))


def scatter_wait(state, after, tag):
    n = (len(state) - 2) // 2
    return split_wait("scatter_wait_" + tag, state, after, _scatter_copies(n))[n:]


def _gather_copies(shapes, level):
    n = len(shapes)

    def copies(refs, send, recv):
        x, y, c, chips = _place()
        out = []
        for t in range(n):
            rh = shapes[t][0] // 2
            for j, (cx, cy) in enumerate(chips):
                k, dev = (2 * x + y, (cx, cy, c)) if level == 1 else (2 * cx + cy, (x, y, 1 - c))
                blk = refs[t].at[k, pl.ds(c * rh, rh), :]
                out.append(_rcopy(blk, blk, send.at[3 * t + j], recv.at[3 * t + j], dev))
        return out
    return copies


def sibling_swap(halves, name):
    n = len(halves)

    def body(*refs):
        h = refs[:n]
        o = refs[n:2 * n]
        send, recv = refs[2 * n:]
        x, y, c, _ = _place()
        cps = [_rcopy(h[t], o[t], send.at[t], recv.at[t], (x, y, 1 - c)) for t in range(n)]
        for cp in cps:
            cp.start()
        for cp in cps:
            cp.wait()

    sems = pltpu.SemaphoreType.DMA
    return pl.pallas_call(
        body, name=name, out_shape=[jax.ShapeDtypeStruct(a.shape, a.dtype) for a in halves],
        in_specs=[ANY] * n, out_specs=[ANY] * n, scratch_shapes=[sems((n,)), sems((n,))],
    )(*halves)


def small_all_reduce(ddw, v512, v1024, loss_parts):
    rows, width = PACK_ROWS, 512
    n512, n1024 = len(VEC512), len(VEC1024)

    def body(*refs):
        ddw_ref = refs[0]
        a_refs = refs[1:1 + n512]
        b_refs = refs[1 + n512:1 + n512 + n1024]
        lp_ref, o_ref, p_ref, gath_ref, send, recv = refs[1 + n512 + n1024:]
        p_ref[...] = jnp.zeros_like(p_ref)
        p_ref[0:32, :] = ddw_ref[...]
        p_ref[LOSS_ROW:LOSS_ROW + 1, 0:128] = jnp.sum(lp_ref[...], axis=0, keepdims=True) * 0.125
        for i, r in enumerate(a_refs):
            p_ref[32 + i:33 + i, :] = r[...]
        for i, r in enumerate(b_refs):
            base = 32 + n512 + 2 * i
            p_ref[base:base + 1, :] = r[:, 0:512]
            p_ref[base + 1:base + 2, :] = r[:, 512:1024]
        x, y, c, _ = _place()
        me = 4 * x + 2 * y + c
        gath_ref[me] = p_ref[...]
        cps = []
        for k in range(1, 8):
            dx, dy, dc = (k >> 2) & 1, (k >> 1) & 1, k & 1
            px = 1 - x if dx else x
            py = 1 - y if dy else y
            pc = 1 - c if dc else c
            cps.append(_rcopy(p_ref, gath_ref.at[me], send.at[k - 1], recv.at[k - 1], (px, py, pc)))
        for cp in cps:
            cp.start()
        for k in range(1, 8):
            dx, dy, dc = (k >> 2) & 1, (k >> 1) & 1, k & 1
            px = 1 - x if dx else x
            py = 1 - y if dy else y
            pc = 1 - c if dc else c
            _rcopy(p_ref, gath_ref.at[4 * px + 2 * py + pc], send.at[k - 1], recv.at[k - 1], (px, py, pc)).wait_recv()
        for cp in cps:
            cp.wait_send()
        total = gath_ref[0]
        for d in range(1, 8):
            total = total + gath_ref[d]
        o_ref[...] = total

    sems = pltpu.SemaphoreType.DMA
    n_in = 2 + n512 + n1024
    return pl.pallas_call(
        body, name="small_all_reduce", out_shape=jax.ShapeDtypeStruct((rows, width), F32),
        in_specs=[VMEM_SPEC] * n_in, out_specs=VMEM_SPEC,
        scratch_shapes=[pltpu.VMEM((rows, width), F32), pltpu.VMEM((8, rows, width), F32), sems((7,)), sems((7,))],
    )(ddw, *[v512[n] for n in VEC512], *[v1024[n] for n in VEC1024], loss_parts)


def _row_block(r):
    for tr in (512, 352, 256, 128):
        if r % tr == 0:
            return tr
    return r


def add_halves(g, recv, name):
    _, _, r, w = g.shape
    tr = _row_block(r)

    def body(g0_ref, g1_ref, r_ref, ob_ref, own_ref):
        k = pl.program_id(1)
        c = lax.axis_index("c")
        me = 2 * lax.axis_index("x") + lax.axis_index("y")
        t = jnp.where(c == 0, g0_ref[0, 0], g1_ref[0, 0]).astype(F32) + r_ref[0].astype(F32)
        ob_ref[0] = t.astype(MM)
        mine = jnp.where(k == me, t, 0.0)

        @pl.when(k == 0)
        def _():
            own_ref[...] = mine

        @pl.when(k != 0)
        def _():
            own_ref[...] += mine

    return pl.pallas_call(
        body, name=name, grid=(r // tr, N_CHIPS),
        in_specs=[pl.BlockSpec((1, 1, tr, w), lambda i, k: (k, 0, i, 0)),
                  pl.BlockSpec((1, 1, tr, w), lambda i, k: (k, 1, i, 0)),
                  pl.BlockSpec((1, tr, w), lambda i, k: (k, i, 0))],
        out_specs=[pl.BlockSpec((1, tr, w), lambda i, k: (k, i, 0)),
                   pl.BlockSpec((tr, w), lambda i, k: (i, 0))],
        out_shape=(jax.ShapeDtypeStruct((N_CHIPS, r, w), MM), jax.ShapeDtypeStruct((r, w), F32)),
        compiler_params=_params(("arbitrary", "arbitrary")),
    )(g, g, recv)


def sum_parts(own, rin, after, name):
    _, r, w = rin.shape
    tr = _row_block(r)

    def body(o_ref, r_ref, after_ref, out_ref):
        out_ref[...] = ((o_ref[...] + r_ref[0].astype(F32)) + r_ref[1].astype(F32)) + r_ref[2].astype(F32)

    return pl.pallas_call(
        body, name=name, grid=(r // tr,), out_shape=jax.ShapeDtypeStruct((r, w), F32),
        in_specs=[pl.BlockSpec((tr, w), lambda i: (i, 0)), pl.BlockSpec((3, tr, w), lambda i: (0, i, 0)),
                  _full_spec((8, 128))],
        out_specs=pl.BlockSpec((tr, w), lambda i: (i, 0)),
        compiler_params=_params(("arbitrary",)),
    )(own, rin, after)


def _adamw_math(w, g, m, v):
    mn = ADAM_B1 * m + (1.0 - ADAM_B1) * g
    vn = ADAM_B2 * v + (1.0 - ADAM_B2) * (g * g)
    m_hat = mn / (1.0 - ADAM_B1 ** ADAM_STEP)
    v_hat = vn / (1.0 - ADAM_B2 ** ADAM_STEP)
    return -ADAM_LR * (m_hat / (jnp.sqrt(v_hat) + ADAM_EPS) + ADAM_WD * w), mn, vn


def adamw(w, mine, other, m, v, name):
    r, c = w.shape
    rh = r // 2
    tr = _row_block(rh)
    if c >= 1024 and tr % 512 == 0:
        tr = 256
    nb = rh // tr

    def body(w_ref, a_ref, b_ref, m_ref, v_ref, go_ref, d_ref, mo_ref, vo_ref):
        gv = jnp.where(lax.axis_index("c") == pl.program_id(0), a_ref[...], b_ref[...])
        go_ref[...] = gv
        d_ref[...], mo_ref[...], vo_ref[...] = _adamw_math(w_ref[...], gv, m_ref[...], v_ref[...])

    spec = pl.BlockSpec((tr, c), lambda h, i: (h * nb + i, 0))
    half = pl.BlockSpec((tr, c), lambda h, i: (i, 0))
    out = jax.ShapeDtypeStruct((r, c), F32)
    return pl.pallas_call(
        body, name=name, grid=(2, nb), out_shape=(out, out, out, out),
        in_specs=[spec, half, half, spec, spec], out_specs=[spec] * 4,
        compiler_params=_params(("arbitrary", "arbitrary")),
    )(w, mine, other, m, v)


def adamw_small(gsum, params):
    names = list(params)
    flat = [a for n in names for a in params[n]]

    def body(*refs):
        g_ref = refs[0]
        ins = refs[1:1 + 3 * len(names)]
        outs = refs[1 + 3 * len(names):]
        me = 2 * lax.axis_index("x") + lax.axis_index("y")
        for i, n in enumerate(names):
            w_ref, m_ref, v_ref = ins[3 * i:3 * i + 3]
            go_ref, d_ref, mo_ref, vo_ref = outs[4 * i:4 * i + 4]
            if n == "conv_dw_w":
                gv = jnp.zeros((CONV_WIDTH, 128), F32)
                for k in range(N_CHIPS):
                    gv = gv + jnp.where(me == k, g_ref[0:CONV_WIDTH, 128 * k:128 * (k + 1)], 0.0)
            elif n in VEC512:
                r0 = 32 + VEC512.index(n)
                gv = g_ref[r0:r0 + 1, :]
            else:
                r0 = 32 + len(VEC512) + 2 * VEC1024.index(n)
                gv = jnp.concatenate([g_ref[r0:r0 + 1, :], g_ref[r0 + 1:r0 + 2, :]], axis=1)
            go_ref[...] = gv
            d_ref[...], mo_ref[...], vo_ref[...] = _adamw_math(w_ref[...], gv, m_ref[...], v_ref[...])

    out_shape = [jax.ShapeDtypeStruct(params[n][0].shape, F32) for n in names for _ in range(4)]
    res = pl.pallas_call(
        body, name="adamw_small", out_shape=out_shape,
        in_specs=[VMEM_SPEC] * (1 + len(flat)), out_specs=[VMEM_SPEC] * len(out_shape),
        compiler_params=_params(),
    )(gsum, *flat)
    return {n: res[4 * i:4 * i + 4] for i, n in enumerate(names)}


REST = ("w_ffn_up", "w_ffn_down", "w_out", "w_conv_branch", "w_att_branch")
VEC512 = ("conv_dw_b", "conv_ln_g", "conv_ln_b")
VEC1024 = ("norm_mix_pre", "b_conv_branch", "norm_mix_post", "norm_ffn_pre", "norm_ffn_post")
PACK_ROWS = 48
LOSS_ROW = 47


def kernel(x, norm_mix_pre, w_in, conv_dw_w, conv_dw_b, conv_ln_g, conv_ln_b, w_conv_branch, b_conv_branch, w_att_branch, w_out, norm_mix_post, norm_ffn_pre, w_ffn_up, w_ffn_down, norm_ffn_post, loss_target, m_norm_mix_pre, m_w_in, m_conv_dw_w, m_conv_dw_b, m_conv_ln_g, m_conv_ln_b, m_w_conv_branch, m_b_conv_branch, m_w_att_branch, m_w_out, m_norm_mix_post, m_norm_ffn_pre, m_w_ffn_up, m_w_ffn_down, m_norm_ffn_post, v_norm_mix_pre, v_w_in, v_conv_dw_w, v_conv_dw_b, v_conv_ln_g, v_conv_ln_b, v_w_conv_branch, v_b_conv_branch, v_w_att_branch, v_w_out, v_norm_mix_post, v_norm_ffn_pre, v_w_ffn_up, v_w_ffn_down, v_norm_ffn_post):
    weights = dict(norm_mix_pre=norm_mix_pre, w_in=w_in, conv_dw_w=conv_dw_w, conv_dw_b=conv_dw_b, conv_ln_g=conv_ln_g, conv_ln_b=conv_ln_b, w_conv_branch=w_conv_branch, b_conv_branch=b_conv_branch, w_att_branch=w_att_branch, w_out=w_out, norm_mix_post=norm_mix_post, norm_ffn_pre=norm_ffn_pre, w_ffn_up=w_ffn_up, w_ffn_down=w_ffn_down, norm_ffn_post=norm_ffn_post)
    mom = dict(norm_mix_pre=m_norm_mix_pre, w_in=m_w_in, conv_dw_w=m_conv_dw_w, conv_dw_b=m_conv_dw_b, conv_ln_g=m_conv_ln_g, conv_ln_b=m_conv_ln_b, w_conv_branch=m_w_conv_branch, b_conv_branch=m_b_conv_branch, w_att_branch=m_w_att_branch, w_out=m_w_out, norm_mix_post=m_norm_mix_post, norm_ffn_pre=m_norm_ffn_pre, w_ffn_up=m_w_ffn_up, w_ffn_down=m_w_ffn_down, norm_ffn_post=m_norm_ffn_post)
    var = dict(norm_mix_pre=v_norm_mix_pre, w_in=v_w_in, conv_dw_w=v_conv_dw_w, conv_dw_b=v_conv_dw_b, conv_ln_g=v_conv_ln_g, conv_ln_b=v_conv_ln_b, w_conv_branch=v_w_conv_branch, b_conv_branch=v_b_conv_branch, w_att_branch=v_w_att_branch, w_out=v_w_out, norm_mix_post=v_norm_mix_post, norm_ffn_pre=v_norm_ffn_pre, w_ffn_up=v_w_ffn_up, w_ffn_down=v_w_ffn_down, norm_ffn_post=v_norm_ffn_post)
    order = list(weights)
    grads, deltas, new_m, new_v = {}, {}, {}, {}
    xs = x.reshape(SEQ, D_MODEL)
    tgt = loss_target.reshape(SEQ, D_MODEL)
    row = lambda a: a.reshape(1, -1)
    g1, g2, g3, g4 = (row(weights[n]) for n in ("norm_mix_pre", "norm_mix_post", "norm_ffn_pre", "norm_ffn_post"))
    ln_g, ln_b = row(conv_ln_g), row(conv_ln_b)

    def reduce_prepare(names, partial, tag):
        from_sib = sibling_exchange([partial[n] for n in names], "sibling_exchange_" + tag)
        return [add_halves(partial[n], r, "add_" + n) for n, r in zip(names, from_sib)]

    def reduce_finish(names, summed, from_chips, after, tag):
        halves = [sum_parts(s[1], r, after, "sum_" + n) for n, s, r in zip(names, summed, from_chips)]
        for n, a, b in zip(names, halves, sibling_swap(halves, "sibling_swap_" + tag)):
            grads[n], deltas[n], new_m[n], new_v[n] = adamw(weights[n], a, b, mom[n], var[n], "adamw_" + n)

    w_in_g, dw_g, *rest = all_gather_weights([w_in], conv_dw_w, [weights[n] for n in REST])
    w_dw_full = jnp.concatenate([dw_g[k] for k in range(N_CHIPS)], axis=1)
    rest_shapes = [weights[n].shape for n in REST]
    state, token = split_start("gather_start", rest, 3 * len(REST), _gather_copies(rest_shapes, 1))
    h1, ci, q, k, v, gc, ga = in_proj_fwd(xs, g1 + token[0:1, 0:1], w_in_g)
    u1, u3 = conv_fwd(ci, w_dw_full, row(conv_dw_b), ln_g, ln_b)
    att, rc = attn_fwd(q, k, v)
    rest = split_wait("gather_wait", state, [att], _gather_copies(rest_shapes, 1))
    state, token = split_start("gather_pass_start", rest, 3 * len(REST), _gather_copies(rest_shapes, 2))
    wg = dict(zip(REST, split_wait("gather_pass_wait", state, [], _gather_copies(rest_shapes, 2))))
    w_up_g, w_cb_g, w_ab_g = wg["w_ffn_up"], wg["w_conv_branch"], wg["w_att_branch"]
    w_out_g = wg["w_out"].reshape(D_MODEL, D_MODEL)
    w_down_g = wg["w_ffn_down"].reshape(D_FF, D_MODEL)
    co, ao, merged, mix, x2, h2 = mix_fwd(u3, att, gc, ga, xs, w_cb_g, row(b_conv_branch), w_ab_g, w_out_g, g2, g3)
    gate, up, act = ffn_up_fwd(h2, w_up_g)
    dff, dy, loss_parts, dg4 = ffn_down_loss(act, w_down_g, x2, tgt, g4)

    partial = {}
    dgu = ffn_act_bwd(dff, w_down_g, gate, up)
    partial["w_ffn_down"] = weight_grad(act, dff, "dw_ffn_down", False, tk=UP_SHARD)
    dx2, dmix, dg3, dg2 = ffn_in_bwd(dgu, w_up_g, x2, mix, dy, g3, g2)
    partial["w_ffn_up"] = weight_grad(h2, dgu, "dw_ffn_up", True)
    dco, dao, dg, du3, datt, dbcb = merge_bwd(dmix, w_out_g, gc, ga, co, ao, w_cb_g, w_ab_g)
    partial["w_out"] = weight_grad(merged, dmix, "dw_out", False, tk=512)
    partial["w_conv_branch"] = weight_grad(u3, dco, "dw_conv_branch", True)
    partial["w_att_branch"] = weight_grad(att, dao, "dw_att_branch", True)
    summed = reduce_prepare(REST, partial, "rest")
    state, token = scatter_start([s[0] for s in summed], "rest")
    zero = token[0:1, 0:1]
    dci, ddw, dbdw, dlng, dlnb = conv_bwd(du3, u1, ci, w_dw_full, ln_g + zero, ln_b)
    dqkv = attn_bwd(q, k, v, datt, rc + zero)
    from_chips = scatter_wait(state, [dci, dqkv], "rest")
    dproj = (dci, dqkv, dg)
    partial["w_in"] = weight_grad_in(h1, dproj)
    summed_in = reduce_prepare(("w_in",), partial, "w_in")
    state, token = scatter_start([summed_in[0][0]], "w_in")
    grad_x, dg1 = in_proj_bwd(dproj, w_in_g, xs, dx2, g1 + token[0:1, 0:1])
    reduce_finish(REST, summed, from_chips, token, "rest")
    from_chips_in = scatter_wait(state, [dg1] + [new_v[n] for n in REST], "w_in")
    reduce_finish(("w_in",), summed_in, from_chips_in, token, "w_in")

    v512 = dict(conv_dw_b=dbdw, conv_ln_g=dlng, conv_ln_b=dlnb)
    v1024 = dict(norm_mix_pre=dg1, b_conv_branch=dbcb, norm_mix_post=dg2, norm_ffn_pre=dg3, norm_ffn_post=dg4)
    gsum = small_all_reduce(ddw, v512, v1024, loss_parts)
    loss = gsum[LOSS_ROW, 0]
    as_rows = lambda n, a: a if n == "conv_dw_w" else a.reshape(1, -1)
    small_names = ("conv_dw_w",) + VEC512 + VEC1024
    small = adamw_small(gsum, {n: tuple(as_rows(n, d[n]) for d in (weights, mom, var)) for n in small_names})
    for n in small_names:
        grads[n], deltas[n], new_m[n], new_v[n] = (a.reshape(weights[n].shape) for a in small[n])

    return (loss, grad_x.reshape(1, SEQ, D_MODEL), *[grads[n] for n in order], *[deltas[n] for n in order],
            *[new_m[n] for n in order], *[new_v[n] for n in order])
```

```python
import jax
import jax.numpy as jnp
from jax import lax
from jax.experimental import pallas as pl
from jax.experimental.pallas import tpu as pltpu

F32 = jnp.float32
MM = jnp.bfloat16

SEQ = 2048
D_MODEL = 1024
CONV_DIM = 512
ATT_DIM = 512
CONV_WIDTH = 31
D_FF = 2816
IN_COLS = 2 * CONV_DIM + 3 * ATT_DIM + 2 * D_MODEL
N_CHIPS = 4
IN_SHARD = IN_COLS // N_CHIPS
UP_SHARD = 2 * D_FF // N_CHIPS
BR_SHARD = D_MODEL // N_CHIPS
EPS = 1e-6
ATT_SCALE = 0.125

TM = 256
GLU_ROWS = 256
TQ = 128
CONV_TILE = 64
CONV_WIN = CONV_TILE + 32
VMEM_LIMIT = 56 * 1024 * 1024

ADAM_LR = 0.001
ADAM_B1 = 0.9
ADAM_B2 = 0.999
ADAM_EPS = 1e-08
ADAM_WD = 0.01
ADAM_STEP = 10

MESH = pl.DeviceIdType.MESH
ANY = pl.BlockSpec(memory_space=pl.ANY)
VMEM_SPEC = pl.BlockSpec(memory_space=pltpu.VMEM)

NT_DIMS = (((1,), (1,)), ((), ()))
TN_DIMS = (((0,), (0,)), ((), ()))

IN_PIECES = (("ci", 0, 1024), ("q", 1024, 1536), ("k", 1536, 2048), ("v", 2048, 2560),
             ("gc", 2560, 3584), ("ga", 3584, 4608))


def _params(sem=None, vmem=VMEM_LIMIT):
    return pltpu.CompilerParams(dimension_semantics=sem, vmem_limit_bytes=vmem)


def _dot(a, b):
    return jnp.dot(a, b, preferred_element_type=F32)


def _dot_nt(a, b):
    return lax.dot_general(a, b, NT_DIMS, preferred_element_type=F32)


def _dot_tn(a, b):
    return lax.dot_general(a, b, TN_DIMS, preferred_element_type=F32)


def _sigmoid(x):
    return 1.0 / (1.0 + jnp.exp(-x))


def _rms(x):
    r = lax.rsqrt(jnp.mean(x * x, axis=-1, keepdims=True) + EPS)
    return x * r, r


def _rms_bwd(dy_g, n, r):
    return r * (dy_g - n * jnp.mean(dy_g * n, axis=-1, keepdims=True))


def _row_tile_spec(width, tm=TM):
    return pl.BlockSpec((tm, width), lambda i: (i, 0))


def _full_spec(shape):
    nd = len(shape)
    return pl.BlockSpec(shape, lambda *_: (0,) * nd)


def _weight_spec(shape):
    nd = len(shape)
    return pl.BlockSpec(shape, lambda *_: (0,) * nd, pipeline_mode=pl.Buffered(1))


def _acc_rows(ref, val, first):
    @pl.when(first)
    def _():
        ref[...] = val

    @pl.when(jnp.logical_not(first))
    def _():
        ref[...] += val


def _gather_behind_grid(ag, n_steps, step=None):
    step = pl.program_id(0) if step is None else step
    pl.when(step == 0)(ag.start)
    pl.when(step == n_steps - 2)(ag.forward)
    return lambda: pl.when(step == n_steps - 1)(ag.finish)


def in_proj_fwd(x, g1, w_in_g, gather=()):
    ng = len(gather)
    nt = SEQ // TM

    def body(*refs):
        x_ref, g_ref, w_ref = refs[:3]
        h_ref, ci_ref, q_ref, k_ref, v_ref, gc_ref, ga_ref = refs[3 + ng:10 + ng]
        if ng:
            done = _gather_behind_grid(_Gather([s.shape for s in gather], refs[3:3 + ng],
                                               refs[10 + ng:10 + 2 * ng], refs[10 + 2 * ng:]), nt)
        n, _ = _rms(x_ref[...])
        h = (n * g_ref[...]).astype(MM)
        h_ref[...] = h
        outs = dict(ci=ci_ref, q=q_ref, k=k_ref, v=v_ref, gc=gc_ref, ga=ga_ref)
        for j in range(N_CHIPS):
            p = _dot(h, w_ref[j])
            g0 = j * IN_SHARD
            for name, s, e in IN_PIECES:
                lo, hi = max(s, g0), min(e, g0 + IN_SHARD)
                if lo < hi:
                    ref = outs[name]
                    part = p[:, lo - g0:hi - g0]
                    if name == "q":
                        part = part * ATT_SCALE
                    ref[:, lo - s:hi - s] = part.astype(ref.dtype)
        if ng:
            done()

    out_shape = [
        jax.ShapeDtypeStruct((SEQ, D_MODEL), MM),
        jax.ShapeDtypeStruct((SEQ, 2 * CONV_DIM), F32),
        jax.ShapeDtypeStruct((SEQ, ATT_DIM), MM),
        jax.ShapeDtypeStruct((SEQ, ATT_DIM), MM),
        jax.ShapeDtypeStruct((SEQ, ATT_DIM), MM),
        jax.ShapeDtypeStruct((SEQ, D_MODEL), F32),
        jax.ShapeDtypeStruct((SEQ, D_MODEL), F32),
    ]
    return pl.pallas_call(
        body, name="in_proj_fwd", grid=(nt,),
        out_shape=out_shape + _Gather.out_shapes(gather),
        in_specs=[_row_tile_spec(D_MODEL), _full_spec((1, D_MODEL)), _weight_spec(w_in_g.shape)] + [ANY] * ng,
        out_specs=[_row_tile_spec(s.shape[1]) for s in out_shape] + [ANY] * ng,
        scratch_shapes=_Gather.scratch(gather) if ng else [],
        compiler_params=_params(("arbitrary",)),
    )(x, g1, w_in_g, *gather)


def _shifted_sum(win, terms):
    by_rot = {}
    for m, coef in terms:
        by_rot.setdefault(m % 8, []).append((m // 8, coef))
    acc = None
    n = win.shape[0]
    for rot in sorted(by_rot):
        shifted = win if rot == 0 else pltpu.roll(win, n - rot, 0)
        for a, coef in by_rot[rot]:
            t = coef * shifted[8 * a:8 * a + CONV_TILE, :]
            acc = t if acc is None else acc + t
    return acc


def _glu_into(ci_ref, upad_ref):
    upad_ref[0:32, :] = jnp.zeros((32, CONV_DIM), F32)

    def step(i, c):
        t0 = pl.multiple_of(i * GLU_ROWS, GLU_ROWS)
        a = ci_ref[pl.ds(t0, GLU_ROWS), 0:CONV_DIM]
        b = ci_ref[pl.ds(t0, GLU_ROWS), CONV_DIM:2 * CONV_DIM]
        upad_ref[pl.ds(t0 + 32, GLU_ROWS), :] = a * _sigmoid(b)
        return c

    lax.fori_loop(0, SEQ // GLU_ROWS, step, 0)


def _layernorm_parts(u1):
    mu = jnp.mean(u1, axis=-1, keepdims=True)
    xc = u1 - mu
    rstd = lax.rsqrt(jnp.mean(xc * xc, axis=-1, keepdims=True) + EPS)
    return xc * rstd, rstd


def conv_fwd(ci, w_dw, b_dw, ln_g, ln_b, gather=()):
    ng = len(gather)
    n_tiles = SEQ // CONV_TILE

    def body(*refs):
        ci_ref, w_ref, b_ref, g_ref, bb_ref = refs[:5]
        u1_ref, u3_ref = refs[5 + ng:7 + ng]
        upad_ref = refs[7 + 2 * ng]
        if ng:
            ag = _Gather([s.shape for s in gather], refs[5:5 + ng], refs[7 + ng:7 + 2 * ng], refs[8 + 2 * ng:])
            ag.start()
        _glu_into(ci_ref, upad_ref)

        def step(i, c):
            if ng:
                pl.when(i == n_tiles - n_tiles // 4)(ag.forward)
            t0 = pl.multiple_of(i * CONV_TILE, CONV_TILE)
            win = upad_ref[pl.ds(t0, CONV_WIN), :]
            u1 = _shifted_sum(win, [(j + 2, w_ref[j:j + 1, :]) for j in range(CONV_WIDTH)]) + b_ref[...]
            u1_ref[pl.ds(t0, CONV_TILE), :] = u1
            xh, _ = _layernorm_parts(u1)
            u2 = xh * g_ref[...] + bb_ref[...]
            u3_ref[pl.ds(t0, CONV_TILE), :] = (u2 * _sigmoid(u2)).astype(MM)
            return c

        lax.fori_loop(0, n_tiles, step, 0)
        if ng:
            ag.finish()

    return pl.pallas_call(
        body, name="conv_fwd",
        out_shape=[jax.ShapeDtypeStruct((SEQ, CONV_DIM), F32), jax.ShapeDtypeStruct((SEQ, CONV_DIM), MM)]
                  + _Gather.out_shapes(gather),
        in_specs=[VMEM_SPEC] * 5 + [ANY] * ng, out_specs=[VMEM_SPEC] * 2 + [ANY] * ng,
        scratch_shapes=[pltpu.VMEM((SEQ + 32, CONV_DIM), F32)] + (_Gather.scratch(gather) if ng else []),
        compiler_params=_params(),
    )(ci, w_dw, b_dw, ln_g, ln_b, *gather)


def _softplus(z):
    return jnp.maximum(z, 0.0) + jnp.log(1.0 + jnp.exp(-jnp.abs(z)))


def _cumsum_weights(suffix, with_total):
    n = 256 if with_total else 128
    r = lax.broadcasted_iota(jnp.int32, (128, n), 0)
    c = lax.broadcasted_iota(jnp.int32, (128, n), 1)
    tri = (r >= c) if suffix else (r <= c)
    return jnp.logical_or(tri, c >= 128).astype(MM)


NO_SCORE = -1e30
N_KB = SEQ // TQ


def _score_bias(lane, row, i, j):
    keep = jnp.logical_and(i >= 0, jnp.logical_or(j < i, lane < row))
    return jnp.where(keep, 0.0, NO_SCORE)


def _block_pipeline(n_stages, descending, step, on_query_block=None):
    n_lag = n_stages - 1
    none = jnp.int32(-1)

    def shift(cur, lag):
        step([cur] + [(lag[2 * s], lag[2 * s + 1]) for s in range(n_lag)])
        return (cur[0], cur[1]) + tuple(lag[:-2])

    def outer(i, lag):
        if on_query_block is not None:
            on_query_block(i)

        def inner(n, lag):
            return shift((i, i - n if descending else n), lag)
        return lax.fori_loop(0, i + 1, inner, lag)

    lag = lax.fori_loop(0, N_KB, outer, (none,) * (2 * n_lag))
    lax.fori_loop(0, n_lag, lambda n, lag: shift((none, none), lag), lag)


def _head_masks():
    lane = lax.broadcasted_iota(jnp.int32, (TQ, 128), 1)
    row = lax.broadcasted_iota(jnp.int32, (TQ, 128), 0)
    return lane, row, lane < 64


def _pick_head(x, head0, h):
    zero = jnp.zeros_like(x)
    return jnp.where(head0, x, zero) if h == 0 else jnp.where(head0, zero, x)


N_PAIRS = ATT_DIM // 128


def _split_heads(src_ref, dst_ref):
    _, _, head0 = _head_masks()

    def block(b, c):
        r0 = pl.multiple_of(b * TQ, TQ)
        d0 = pl.multiple_of(b * 2 * TQ, 2 * TQ)
        for p in range(N_PAIRS):
            x = src_ref[pl.ds(r0, TQ), 128 * p:128 * (p + 1)]
            for h in range(2):
                dst_ref[p, pl.ds(d0 + TQ * h, TQ), :] = _pick_head(x, head0, h)
        return c

    lax.fori_loop(0, N_KB, block, 0)


def attn_fwd(q, k, v, gather=()):
    ng = len(gather)

    def body(*refs):
        q_ref, k_ref, v_ref = refs[:3]
        o_ref, rc_ref = refs[3 + ng:5 + ng]
        acc_ref, r_ref, z_ref, spb_ref, ab_ref, qm_ref, vm_ref = refs[5 + 2 * ng:12 + 2 * ng]
        if ng:
            ag = _Gather([s.shape for s in gather], refs[3:3 + ng], refs[5 + ng:5 + 2 * ng], refs[12 + 2 * ng:])
            ag.start()
        lane, row, _ = _head_masks()
        w = _cumsum_weights(suffix=True, with_total=True)
        _split_heads(q_ref, qm_ref)
        _split_heads(v_ref, vm_ref)
        acc_ref[...] = jnp.zeros_like(acc_ref)
        r_ref[...] = jnp.zeros_like(r_ref)
        rc_ref[...] = jnp.zeros_like(rc_ref)
        z_ref[...] = jnp.full(z_ref.shape, NO_SCORE, F32)
        spb_ref[...] = jnp.zeros_like(spb_ref)
        ab_ref[...] = jnp.zeros_like(ab_ref)

        def step(pairs):
            (i1, j1), (i2, j2), (i3, j3) = pairs
            k1, q2, q3 = (pl.multiple_of(jnp.maximum(b, 0) * TQ, TQ) for b in (j1, i2, i3))
            q1, k3 = (pl.multiple_of(jnp.maximum(b, 0) * 2 * TQ, 2 * TQ) for b in (i1, j3))
            bias1 = _score_bias(lane, row, i1, j1)
            first2 = j2 == i2
            rc_rows = rc_ref[pl.ds(q2, TQ), :]
            for p in range(N_PAIRS):
                cols = slice(128 * p, 128 * (p + 1))
                kb = k_ref[pl.ds(k1, TQ), cols]
                acc_ref[pl.ds(q3, TQ), cols] += _dot(ab_ref[p], vm_ref[p, pl.ds(k3, 2 * TQ), :])
                for h in range(2):
                    hh = 2 * p + h
                    r = _dot(spb_ref[hh], w)
                    r_in = jnp.where(first2, 0.0, r_ref[hh])
                    ab_ref[p, :, 128 * h:128 * (h + 1)] = jnp.exp(z_ref[hh] - (r[:, :128] + r_in)).astype(MM)
                    rc_rows = jnp.where(jnp.logical_and(lane == 16 * hh + j2, i2 >= 0), r_in, rc_rows)
                    r_ref[hh] = r_in + r[:, 128:]
                    z = _dot_nt(qm_ref[p, pl.ds(q1 + TQ * h, TQ), :], kb) + bias1
                    z_ref[hh] = z
                    spb_ref[hh] = _softplus(z).astype(MM)
            rc_ref[pl.ds(q2, TQ), :] = rc_rows

        if ng:
            _block_pipeline(3, True, step, lambda i: pl.when(i == N_KB - 2)(ag.forward))
        else:
            _block_pipeline(3, True, step)
        o_ref[...] = acc_ref[...].astype(MM)
        if ng:
            ag.finish()

    out_shape = [jax.ShapeDtypeStruct((SEQ, ATT_DIM), MM), jax.ShapeDtypeStruct((SEQ, 128), F32)]
    out_shape += _Gather.out_shapes(gather)
    return pl.pallas_call(
        body, name="attn_fwd", out_shape=out_shape,
        in_specs=[VMEM_SPEC] * 3 + [ANY] * ng, out_specs=[VMEM_SPEC] * 2 + [ANY] * ng,
        scratch_shapes=[pltpu.VMEM((SEQ, ATT_DIM), F32), pltpu.VMEM((8, TQ, 128), F32),
                        pltpu.VMEM((8, TQ, 128), F32), pltpu.VMEM((8, TQ, 128), MM),
                        pltpu.VMEM((N_PAIRS, TQ, 256), MM), pltpu.VMEM((N_PAIRS, 2 * SEQ, 128), MM),
                        pltpu.VMEM((N_PAIRS, 2 * SEQ, 128), MM)]
                       + (_Gather.scratch(gather) if ng else []),
        compiler_params=_params(),
    )(q, k, v, *gather)


def mix_fwd(u3, att, gc, ga, x, w_cb_g, b_cb, w_ab_g, w_out_g, g2, g3, gather=()):
    ng = len(gather)
    nt = SEQ // TM

    def body(*refs):
        u_ref, a_ref, gc_ref, ga_ref, x_ref, wcb_ref, bcb_ref, wab_ref, wout_ref, g2_ref, g3_ref = refs[:11]
        co_ref, ao_ref, mg_ref, mix_ref, x2_ref, h2_ref = refs[11 + ng:17 + ng]
        if ng:
            done = _gather_behind_grid(_Gather([s.shape for s in gather], refs[11:11 + ng],
                                               refs[17 + ng:17 + 2 * ng], refs[17 + 2 * ng:]), nt)
        u = u_ref[...]
        a = a_ref[...]
        co = jnp.concatenate([_dot(u, wcb_ref[j]) for j in range(N_CHIPS)], axis=1) + bcb_ref[...]
        ao = jnp.concatenate([_dot(a, wab_ref[j]) for j in range(N_CHIPS)], axis=1)
        co_ref[...] = co.astype(MM)
        ao_ref[...] = ao.astype(MM)
        merged = (_sigmoid(gc_ref[...]) * co + _sigmoid(ga_ref[...]) * ao).astype(MM)
        mg_ref[...] = merged
        mix = _dot(merged, wout_ref[...])
        mix_ref[...] = mix
        n2, _ = _rms(mix)
        x2 = x_ref[...] + n2 * g2_ref[...]
        x2_ref[...] = x2
        n3, _ = _rms(x2)
        h2_ref[...] = (n3 * g3_ref[...]).astype(MM)
        if ng:
            done()

    out_shape = [
        jax.ShapeDtypeStruct((SEQ, D_MODEL), MM), jax.ShapeDtypeStruct((SEQ, D_MODEL), MM),
        jax.ShapeDtypeStruct((SEQ, D_MODEL), MM), jax.ShapeDtypeStruct((SEQ, D_MODEL), F32),
        jax.ShapeDtypeStruct((SEQ, D_MODEL), F32), jax.ShapeDtypeStruct((SEQ, D_MODEL), MM),
    ]
    vec = _full_spec((1, D_MODEL))
    return pl.pallas_call(
        body, name="mix_fwd", grid=(nt,),
        out_shape=out_shape + _Gather.out_shapes(gather),
        in_specs=[_row_tile_spec(CONV_DIM), _row_tile_spec(ATT_DIM), _row_tile_spec(D_MODEL),
                  _row_tile_spec(D_MODEL), _row_tile_spec(D_MODEL), _weight_spec(w_cb_g.shape), vec,
                  _weight_spec(w_ab_g.shape), _weight_spec(w_out_g.shape), vec, vec] + [ANY] * ng,
        out_specs=[_row_tile_spec(D_MODEL)] * 6 + [ANY] * ng,
        scratch_shapes=_Gather.scratch(gather) if ng else [],
        compiler_params=_params(("arbitrary",)),
    )(u3, att, gc, ga, x, w_cb_g, b_cb, w_ab_g, w_out_g, g2, g3, *gather)


def ffn_up_fwd(h2, w_up_g, gather=()):
    ng = len(gather)
    nt = SEQ // TM

    def body(*refs):
        h_ref, wg_ref, wu_ref = refs[:3]
        gate_ref, up_ref, act_ref = refs[3 + ng:6 + ng]
        if ng:
            done = _gather_behind_grid(_Gather([s.shape for s in gather], refs[3:3 + ng],
                                               refs[6 + ng:6 + 2 * ng], refs[6 + 2 * ng:]),
                                       2 * nt, pl.program_id(0) * nt + pl.program_id(1))
        h = h_ref[...]
        gate = _dot(h, wg_ref[0])
        up = _dot(h, wu_ref[0])
        gate_ref[...] = gate.astype(MM)
        up_ref[...] = up.astype(MM)
        act_ref[...] = (gate * _sigmoid(gate) * up).astype(MM)
        if ng:
            done()

    tile = pl.BlockSpec((TM, UP_SHARD), lambda n, i: (i, n))
    act = jax.ShapeDtypeStruct((SEQ, D_FF), MM)
    return pl.pallas_call(
        body, name="ffn_up_fwd", grid=(2, nt), out_shape=[act, act, act] + _Gather.out_shapes(gather),
        in_specs=[pl.BlockSpec((TM, D_MODEL), lambda n, i: (i, 0)),
                  pl.BlockSpec((1, D_MODEL, UP_SHARD), lambda n, i: (n, 0, 0)),
                  pl.BlockSpec((1, D_MODEL, UP_SHARD), lambda n, i: (n + 2, 0, 0))] + [ANY] * ng,
        out_specs=[tile, tile, tile] + [ANY] * ng,
        scratch_shapes=_Gather.scratch(gather) if ng else [],
        compiler_params=_params(("arbitrary", "arbitrary")),
    )(h2, w_up_g, w_up_g, *gather)


def ffn_down_loss(act, w_down_g, x2, target, g4):
    def body(act_ref, wd_ref, x2_ref, t_ref, g_ref, dff_ref, dy_ref, loss_ref, dg_ref):
        ff = _dot(act_ref[...], wd_ref[...])
        n4, r4 = _rms(ff)
        g4v = g_ref[...]
        err = x2_ref[...] + n4 * g4v - t_ref[...]
        row_loss = jnp.mean(err * err, axis=-1, keepdims=True)
        loss_ref[...] = jnp.zeros((8, 128), F32) + 0.5 * jnp.sum(row_loss, axis=0, keepdims=True)
        dy = err * (1.0 / D_MODEL)
        dy_ref[...] = dy
        dff_ref[...] = _rms_bwd(dy * g4v, n4, r4).astype(MM)
        _acc_rows(dg_ref, jnp.sum(dy * n4, axis=0, keepdims=True), pl.program_id(0) == 0)

    nt = SEQ // TM
    vec = _full_spec((1, D_MODEL))
    return pl.pallas_call(
        body, name="ffn_down_loss", grid=(nt,),
        out_shape=(jax.ShapeDtypeStruct((SEQ, D_MODEL), MM), jax.ShapeDtypeStruct((SEQ, D_MODEL), F32),
                   jax.ShapeDtypeStruct((nt * 8, 128), F32), jax.ShapeDtypeStruct((1, D_MODEL), F32)),
        in_specs=[_row_tile_spec(D_FF), _weight_spec(w_down_g.shape), _row_tile_spec(D_MODEL),
                  _row_tile_spec(D_MODEL), vec],
        out_specs=[_row_tile_spec(D_MODEL), _row_tile_spec(D_MODEL),
                   pl.BlockSpec((8, 128), lambda i: (i, 0)), vec],
        compiler_params=_params(("arbitrary",)),
    )(act, w_down_g, x2, target, g4)


def ffn_act_bwd(dff, w_down_g, gate, up):
    def body(dff_ref, wd_ref, gate_ref, up_ref, dgu_ref):
        dact = _dot_nt(dff_ref[...], wd_ref[...])
        gate = gate_ref[...].astype(F32)
        sg = _sigmoid(gate)
        dgu_ref[:, 0:D_FF] = (dact * up_ref[...].astype(F32) * (sg * (1.0 + gate * (1.0 - sg)))).astype(MM)
        dgu_ref[:, D_FF:2 * D_FF] = (dact * (gate * sg)).astype(MM)

    return pl.pallas_call(
        body, name="ffn_act_bwd", grid=(SEQ // TM,),
        out_shape=jax.ShapeDtypeStruct((SEQ, 2 * D_FF), MM),
        in_specs=[_row_tile_spec(D_MODEL), _weight_spec(w_down_g.shape), _row_tile_spec(D_FF), _row_tile_spec(D_FF)],
        out_specs=_row_tile_spec(2 * D_FF),
        compiler_params=_params(("arbitrary",)),
    )(dff, w_down_g, gate, up)


def ffn_in_bwd(dgu, w_up_g, x2, mix, dy, g3, g2):
    def body(dgu_ref, w_ref, x2_ref, mix_ref, dy_ref, g3_ref, g2_ref, dx2_ref, dmix_ref, dg3_ref, dg2_ref):
        dh2 = None
        for j in range(N_CHIPS):
            t = _dot_nt(dgu_ref[:, j * UP_SHARD:(j + 1) * UP_SHARD], w_ref[j])
            dh2 = t if dh2 is None else dh2 + t
        first = pl.program_id(0) == 0
        n3, r3 = _rms(x2_ref[...])
        dx2 = dy_ref[...] + _rms_bwd(dh2 * g3_ref[...], n3, r3)
        dx2_ref[...] = dx2
        _acc_rows(dg3_ref, jnp.sum(dh2 * n3, axis=0, keepdims=True), first)
        n2, r2 = _rms(mix_ref[...])
        dmix_ref[...] = _rms_bwd(dx2 * g2_ref[...], n2, r2).astype(MM)
        _acc_rows(dg2_ref, jnp.sum(dx2 * n2, axis=0, keepdims=True), first)

    vec = _full_spec((1, D_MODEL))
    return pl.pallas_call(
        body, name="ffn_in_bwd", grid=(SEQ // TM,),
        out_shape=(jax.ShapeDtypeStruct((SEQ, D_MODEL), F32), jax.ShapeDtypeStruct((SEQ, D_MODEL), MM),
                   jax.ShapeDtypeStruct((1, D_MODEL), F32), jax.ShapeDtypeStruct((1, D_MODEL), F32)),
        in_specs=[_row_tile_spec(2 * D_FF), _weight_spec(w_up_g.shape), _row_tile_spec(D_MODEL),
                  _row_tile_spec(D_MODEL), _row_tile_spec(D_MODEL), vec, vec],
        out_specs=[_row_tile_spec(D_MODEL), _row_tile_spec(D_MODEL), vec, vec],
        compiler_params=_params(("arbitrary",)),
    )(dgu, w_up_g, x2, mix, dy, g3, g2)


def merge_bwd(dmix, w_out_g, gc, ga, co, ao, w_cb_g, w_ab_g):
    def body(dmix_ref, wout_ref, gc_ref, ga_ref, co_ref, ao_ref, wcb_ref, wab_ref,
             dco_ref, dao_ref, dg_ref, du3_ref, datt_ref, dbcb_ref):
        dm = _dot_nt(dmix_ref[...], wout_ref[...])
        sgc = _sigmoid(gc_ref[...])
        sga = _sigmoid(ga_ref[...])
        dco = dm * sgc
        dao = dm * sga
        dg_ref[:, 0:D_MODEL] = (dm * co_ref[...].astype(F32) * (sgc * (1.0 - sgc))).astype(MM)
        dg_ref[:, D_MODEL:2 * D_MODEL] = (dm * ao_ref[...].astype(F32) * (sga * (1.0 - sga))).astype(MM)
        _acc_rows(dbcb_ref, jnp.sum(dco, axis=0, keepdims=True), pl.program_id(0) == 0)
        dco_ref[...] = dco.astype(MM)
        dao_ref[...] = dao.astype(MM)
        du3 = None
        datt = None
        for j in range(N_CHIPS):
            cols = slice(j * BR_SHARD, (j + 1) * BR_SHARD)
            t = _dot_nt(dco_ref[:, cols], wcb_ref[j])
            s = _dot_nt(dao_ref[:, cols], wab_ref[j])
            du3 = t if du3 is None else du3 + t
            datt = s if datt is None else datt + s
        du3_ref[...] = du3
        datt_ref[...] = datt.astype(MM)

    wide = _row_tile_spec(D_MODEL)
    return pl.pallas_call(
        body, name="merge_bwd", grid=(SEQ // TM,),
        out_shape=(jax.ShapeDtypeStruct((SEQ, D_MODEL), MM), jax.ShapeDtypeStruct((SEQ, D_MODEL), MM),
                   jax.ShapeDtypeStruct((SEQ, 2 * D_MODEL), MM),
                   jax.ShapeDtypeStruct((SEQ, CONV_DIM), F32), jax.ShapeDtypeStruct((SEQ, ATT_DIM), MM),
                   jax.ShapeDtypeStruct((1, D_MODEL), F32)),
        in_specs=[wide, _weight_spec(w_out_g.shape), wide, wide, wide, wide,
                  _weight_spec(w_cb_g.shape), _weight_spec(w_ab_g.shape)],
        out_specs=[wide, wide, _row_tile_spec(2 * D_MODEL), _row_tile_spec(CONV_DIM), _row_tile_spec(ATT_DIM),
                   _full_spec((1, D_MODEL))],
        compiler_params=_params(("arbitrary",)),
    )(dmix, w_out_g, gc, ga, co, ao, w_cb_g, w_ab_g)


def conv_bwd(du3, u1, ci, w_dw, ln_g, ln_b):
    def body(du3_ref, u1_ref, ci_ref, w_ref, g_ref, bb_ref,
             dci_ref, dw_ref, dbdw_ref, dg_ref, db_ref, upad_ref, dpad_ref, dwacc_ref, vacc_ref):
        _glu_into(ci_ref, upad_ref)
        dpad_ref[SEQ:SEQ + 32, :] = jnp.zeros((32, CONV_DIM), F32)
        dwacc_ref[...] = jnp.zeros_like(dwacc_ref)
        vacc_ref[...] = jnp.zeros_like(vacc_ref)

        def fold8(t):
            s = t[0:8, :]
            for r in range(1, CONV_TILE // 8):
                s = s + t[8 * r:8 * r + 8, :]
            return s

        def pass1(i, c):
            t0 = pl.multiple_of(i * CONV_TILE, CONV_TILE)
            xh, rstd = _layernorm_parts(u1_ref[pl.ds(t0, CONV_TILE), :])
            gv = g_ref[...]
            u2 = xh * gv + bb_ref[...]
            s2 = _sigmoid(u2)
            du2 = du3_ref[pl.ds(t0, CONV_TILE), :] * (s2 * (1.0 + u2 * (1.0 - s2)))
            wv = du2 * gv
            du1 = rstd * (wv - jnp.mean(wv, axis=-1, keepdims=True)
                          - xh * jnp.mean(wv * xh, axis=-1, keepdims=True))
            dpad_ref[pl.ds(t0, CONV_TILE), :] = du1
            vacc_ref[0] += fold8(du2 * xh)
            vacc_ref[1] += fold8(du2)
            vacc_ref[2] += fold8(du1)
            win = upad_ref[pl.ds(t0, CONV_WIN), :]
            n = win.shape[0]
            for rot in range(8):
                shifted = win if rot == 0 else pltpu.roll(win, n - rot, 0)
                for a in range(5):
                    j = 8 * a + rot - 2
                    if 0 <= j < CONV_WIDTH:
                        dwacc_ref[j] += fold8(du1 * shifted[8 * a:8 * a + CONV_TILE, :])
            return c

        lax.fori_loop(0, SEQ // CONV_TILE, pass1, 0)

        def pass2(i, c):
            t0 = pl.multiple_of(i * CONV_TILE, CONV_TILE)
            win = dpad_ref[pl.ds(t0, CONV_WIN), :]
            du0 = _shifted_sum(win, [(30 - j, w_ref[j:j + 1, :]) for j in range(CONV_WIDTH)])
            a = ci_ref[pl.ds(t0, CONV_TILE), 0:CONV_DIM]
            sb = _sigmoid(ci_ref[pl.ds(t0, CONV_TILE), CONV_DIM:2 * CONV_DIM])
            dci_ref[pl.ds(t0, CONV_TILE), 0:CONV_DIM] = (du0 * sb).astype(MM)
            dci_ref[pl.ds(t0, CONV_TILE), CONV_DIM:2 * CONV_DIM] = (du0 * a * (sb * (1.0 - sb))).astype(MM)
            return c

        lax.fori_loop(0, SEQ // CONV_TILE, pass2, 0)

        for j in range(CONV_WIDTH):
            dw_ref[j:j + 1, :] = jnp.sum(dwacc_ref[j], axis=0, keepdims=True)
        dw_ref[CONV_WIDTH:32, :] = jnp.zeros((32 - CONV_WIDTH, CONV_DIM), F32)
        dg_ref[...] = jnp.sum(vacc_ref[0], axis=0, keepdims=True)
        db_ref[...] = jnp.sum(vacc_ref[1], axis=0, keepdims=True)
        dbdw_ref[...] = jnp.sum(vacc_ref[2], axis=0, keepdims=True)

    vec = jax.ShapeDtypeStruct((1, CONV_DIM), F32)
    return pl.pallas_call(
        body, name="conv_bwd",
        out_shape=(jax.ShapeDtypeStruct((SEQ, 2 * CONV_DIM), MM), jax.ShapeDtypeStruct((32, CONV_DIM), F32),
                   vec, vec, vec),
        in_specs=[VMEM_SPEC] * 6, out_specs=[VMEM_SPEC] * 5,
        scratch_shapes=[pltpu.VMEM((SEQ + 32, CONV_DIM), F32), pltpu.VMEM((SEQ + 32, CONV_DIM), F32),
                        pltpu.VMEM((CONV_WIDTH, 8, CONV_DIM), F32), pltpu.VMEM((3, 8, CONV_DIM), F32)],
        compiler_params=_params(),
    )(du3, u1, ci, w_dw, ln_g, ln_b)


def attn_bwd(q, k, v, datt, rc):
    def body(q_ref, k_ref, v_ref, do_ref, rc_ref, dqkv_ref, dqa_ref, dka_ref, dva_ref, pc_ref, z_ref, sig1_ref,
             sig2_ref, g_ref, spb_ref, gb_ref, ar_ref, dzr_ref, dzc_ref, qm_ref, km_ref, dom_ref):
        lane, row, _ = _head_masks()
        _split_heads(q_ref, qm_ref)
        _split_heads(k_ref, km_ref)
        _split_heads(do_ref, dom_ref)
        for ref in (dqa_ref, dka_ref, dva_ref, pc_ref):
            ref[...] = jnp.zeros_like(ref)
        z_ref[...] = jnp.full(z_ref.shape, NO_SCORE, F32)
        for ref in (sig1_ref, sig2_ref, spb_ref, ar_ref, g_ref, gb_ref, dzr_ref, dzc_ref):
            ref[...] = jnp.zeros_like(ref)
        w_suffix = _cumsum_weights(suffix=True, with_total=False)
        w_prefix = _cumsum_weights(suffix=False, with_total=True)

        def step(pairs):
            (ia, ja), (ib, jb), (ic, jc), (id_, jd) = pairs
            ka, qb_, kb_, kc, qd, kd = (pl.multiple_of(jnp.maximum(b, 0) * TQ, TQ) for b in (ja, ib, jb, jc, id_, jd))
            qa2, qb2, qc2, qd2, kd2 = (pl.multiple_of(jnp.maximum(b, 0) * 2 * TQ, 2 * TQ)
                                       for b in (ia, ib, ic, id_, jd))
            bias_a = _score_bias(lane, row, ia, ja)
            rc_rows = rc_ref[pl.ds(qb_, TQ), :]
            first_c = jc == 0
            for p in range(N_PAIRS):
                cols = slice(128 * p, 128 * (p + 1))
                k_a = k_ref[pl.ds(ka, TQ), cols]
                v_b = v_ref[pl.ds(kb_, TQ), cols]
                dqa_ref[pl.ds(qd, TQ), cols] += _dot(dzc_ref[p], km_ref[p, pl.ds(kd2, 2 * TQ), :])
                dka_ref[pl.ds(kd, TQ), cols] += _dot_tn(dzr_ref[p], qm_ref[p, pl.ds(qd2, 2 * TQ), :])
                dva_ref[pl.ds(kc, TQ), cols] += _dot_tn(ar_ref[p], dom_ref[p, pl.ds(qc2, 2 * TQ), :])
                for h in range(2):
                    hh = 2 * p + h
                    rows = slice(TQ * h, TQ * (h + 1))
                    r = _dot(gb_ref[hh], w_prefix)
                    p_in = jnp.where(first_c, 0.0, pc_ref[hh])
                    dz = (g_ref[hh] - sig2_ref[hh] * (r[:, :128] + p_in)).astype(MM)
                    dzc_ref[p, :, rows] = dz
                    dzr_ref[p, rows, :] = dz
                    pc_ref[hh] = p_in + r[:, 128:]
                    r_in = jnp.sum(jnp.where(lane == 16 * hh + jb, rc_rows, 0.0), axis=1, keepdims=True)
                    a = jnp.exp(z_ref[hh] - (_dot(spb_ref[hh], w_suffix) + r_in))
                    g = _dot_nt(dom_ref[p, pl.ds(qb2 + TQ * h, TQ), :], v_b) * a
                    ar_ref[p, rows, :] = a.astype(MM)
                    g_ref[hh] = g
                    gb_ref[hh] = g.astype(MM)
                    sig2_ref[hh] = sig1_ref[hh]
                    z = _dot_nt(qm_ref[p, pl.ds(qa2 + TQ * h, TQ), :], k_a) + bias_a
                    sp = _softplus(z)
                    sig1_ref[hh] = jnp.exp(z - sp)
                    z_ref[hh] = z
                    spb_ref[hh] = sp.astype(MM)

        _block_pipeline(4, False, step)
        dqkv_ref[:, 0:ATT_DIM] = (dqa_ref[...] * ATT_SCALE).astype(MM)
        dqkv_ref[:, ATT_DIM:2 * ATT_DIM] = dka_ref[...].astype(MM)
        dqkv_ref[:, 2 * ATT_DIM:3 * ATT_DIM] = dva_ref[...].astype(MM)

    split = pltpu.VMEM((N_PAIRS, 2 * SEQ, 128), MM)
    return pl.pallas_call(
        body, name="attn_bwd", out_shape=jax.ShapeDtypeStruct((SEQ, 3 * ATT_DIM), MM),
        in_specs=[VMEM_SPEC] * 5, out_specs=VMEM_SPEC,
        scratch_shapes=[pltpu.VMEM((SEQ, ATT_DIM), F32)] * 3 + [pltpu.VMEM((8, TQ, 128), F32)] * 5
                       + [pltpu.VMEM((8, TQ, 128), MM)] * 2
                       + [pltpu.VMEM((N_PAIRS, 2 * TQ, 128), MM)] * 2 + [pltpu.VMEM((N_PAIRS, TQ, 256), MM)]
                       + [split] * 3,
        compiler_params=_params(),
    )(q, k, v, datt, rc)


DPROJ_PIECES = ((0, 1024), (1024, 2560), (2560, 4608))


def _dproj_segments(j):
    g0, g1 = j * IN_SHARD, (j + 1) * IN_SHARD
    segs = []
    for p, (s, e) in enumerate(DPROJ_PIECES):
        lo, hi = max(s, g0), min(e, g1)
        if lo < hi:
            segs.append((p, lo - s, lo - g0, hi - lo))
    return segs


def in_proj_bwd(pieces, w_in_g, x, dx2, g1, scatter=()):
    ns = len(scatter)
    nt = SEQ // TM

    def body(*refs):
        p_refs = refs[:3]
        w_ref, x_ref, dx2_ref, g_ref = refs[3:7]
        dx_ref, dg_ref = refs[7 + ns:9 + ns]
        if ns:
            sc = _Scatter(refs[7:7 + ns], refs[9 + ns:9 + 2 * ns], refs[9 + 2 * ns:])
            pl.when(pl.program_id(0) == 0)(sc.start)
        dh = None
        for j in range(N_CHIPS):
            for p, lo, off, width in _dproj_segments(j):
                t = _dot_nt(p_refs[p][:, lo:lo + width], w_ref[j, :, off:off + width])
                dh = t if dh is None else dh + t
        n1, r1 = _rms(x_ref[...])
        dx_ref[...] = dx2_ref[...] + _rms_bwd(dh * g_ref[...], n1, r1)
        _acc_rows(dg_ref, jnp.sum(dh * n1, axis=0, keepdims=True), pl.program_id(0) == 0)
        if ns:
            pl.when(pl.program_id(0) == nt - 1)(sc.finish)

    vec = _full_spec((1, D_MODEL))
    return pl.pallas_call(
        body, name="in_proj_bwd", grid=(nt,),
        out_shape=[jax.ShapeDtypeStruct((SEQ, D_MODEL), F32), jax.ShapeDtypeStruct((1, D_MODEL), F32)]
                  + _Scatter.out_shapes(scatter),
        in_specs=[_row_tile_spec(p.shape[1]) for p in pieces]
                 + [_weight_spec(w_in_g.shape), _row_tile_spec(D_MODEL), _row_tile_spec(D_MODEL), vec] + [ANY] * ns,
        out_specs=[_row_tile_spec(D_MODEL), vec] + [ANY] * ns,
        scratch_shapes=_Scatter.scratch(ns) if ns else [],
        compiler_params=_params(("arbitrary",)),
    )(*pieces, w_in_g, x, dx2, g1, *scatter)


def weight_grad_in(h1, pieces):
    kh = D_MODEL // 2

    def body(a_ref, p0_ref, p1_ref, p2_ref, o_ref):
        p_refs = (p0_ref, p1_ref, p2_ref)
        a = a_ref[...]
        for j in range(N_CHIPS):
            @pl.when(pl.program_id(1) == j)
            def _():
                for p, lo, off, width in _dproj_segments(j):
                    o_ref[0, 0, :, off:off + width] = _dot_tn(a, p_refs[p][:, lo:lo + width]).astype(MM)

    return pl.pallas_call(
        body, name="dw_in", grid=(2, N_CHIPS), out_shape=jax.ShapeDtypeStruct((N_CHIPS, 2, kh, IN_SHARD), MM),
        in_specs=[pl.BlockSpec((SEQ, kh), lambda h, j: (0, h))] + [_weight_spec(p.shape) for p in pieces],
        out_specs=pl.BlockSpec((1, 1, kh, IN_SHARD), lambda h, j: (j, h, 0, 0)),
        compiler_params=_params(("arbitrary", "arbitrary")),
    )(h1, *pieces)


def weight_grad(a, b, name, col_sharded, tk=None):
    kin, n = a.shape[1], b.shape[1]

    def body(a_ref, b_ref, o_ref):
        if col_sharded:
            o_ref[0, 0] = _dot_tn(a_ref[...], b_ref[...]).astype(MM)
        else:
            o_ref[...] = _dot_tn(a_ref[...], b_ref[...]).astype(MM)

    if col_sharded:
        kh, ns = kin // 2, n // N_CHIPS
        out = jax.ShapeDtypeStruct((N_CHIPS, 2, kh, ns), MM)
        grid = (2, N_CHIPS)
        in_specs = [pl.BlockSpec((SEQ, kh), lambda h, j: (0, h)), pl.BlockSpec((SEQ, ns), lambda h, j: (0, j))]
        out_spec = pl.BlockSpec((1, 1, kh, ns), lambda h, j: (j, h, 0, 0))
        sem = ("arbitrary", "arbitrary")
    else:
        out = jax.ShapeDtypeStruct((kin, n), MM)
        grid = (kin // tk,)
        in_specs = [pl.BlockSpec((SEQ, tk), lambda r: (0, r)), pl.BlockSpec((SEQ, n), lambda r: (0, 0))]
        out_spec = pl.BlockSpec((tk, n), lambda r: (r, 0))
        sem = ("arbitrary",)
    res = pl.pallas_call(
        body, name=name, grid=grid, out_shape=out, in_specs=in_specs, out_specs=out_spec,
        compiler_params=_params(sem),
    )(a, b)
    if not col_sharded:
        res = res.reshape(N_CHIPS, 2, kin // (2 * N_CHIPS), n)
    return res


def _place():
    x, y, c = lax.axis_index("x"), lax.axis_index("y"), lax.axis_index("c")
    chips = [(1 - x, y), (x, 1 - y), (1 - x, 1 - y)]
    return x, y, c, chips


def _rcopy(src, dst, send_sem, recv_sem, dev):
    return pltpu.make_async_remote_copy(src_ref=src, dst_ref=dst, send_sem=send_sem, recv_sem=recv_sem,
                                        device_id=dev, device_id_type=MESH)


class _Gather:
    def __init__(self, shapes, w, o, scratch):
        self.n, self.shapes, self.w, self.o = len(w), shapes, w, o
        self.send, self.recv, self.fsend, self.frecv, self.loc_in, self.loc_out = scratch[:6]
        self.raw, self.stage = scratch[6:6 + self.n], scratch[6 + self.n:]
        self.x, self.y, self.c, self.chips = _place()
        self.me = 2 * self.x + self.y
        self.sib = (self.x, self.y, 1 - self.c)
        self.pairs = [(j, t) for j in range(3) for t in range(self.n)]

    @staticmethod
    def scratch(shards):
        n = len(shards)
        sems = pltpu.SemaphoreType.DMA
        return ([sems((3 * n,)), sems((3 * n,)), sems((3 * n,)), sems((3 * n,)), sems((n,)), sems((n,))]
                + [pltpu.VMEM(s.shape, s.dtype) for s in shards] + [pltpu.VMEM(s.shape, MM) for s in shards])

    @staticmethod
    def out_shapes(shards):
        return [jax.ShapeDtypeStruct((N_CHIPS,) + s.shape, MM) for s in shards]

    def _half(self, t, k, cc):
        rh = self.shapes[t][0] // 2
        return self.o[t].at[k, pl.ds(cc * rh, rh), :]

    def _chip(self, j):
        cx, cy = self.chips[j]
        return 2 * cx + cy, (cx, cy, self.c)

    def local_in(self, t):
        return pltpu.make_async_copy(self.w[t], self.raw[t], self.loc_in.at[t])

    def local_out(self, t):
        return pltpu.make_async_copy(self.stage[t], self.o[t].at[self.me], self.loc_out.at[t])

    def first(self, j, t):
        rh = self.shapes[t][0] // 2
        i = j * self.n + t
        return _rcopy(self.stage[t].at[pl.ds(self.c * rh, rh), :], self._half(t, self.me, self.c),
                      self.send.at[i], self.recv.at[i], self._chip(j)[1])

    def arrived(self, j, t):
        k, dev = self._chip(j)
        i = j * self.n + t
        blk = self._half(t, k, self.c)
        return _rcopy(blk, blk, self.send.at[i], self.recv.at[i], dev)

    def passed(self, j, t, cc):
        i = j * self.n + t
        blk = self._half(t, self._chip(j)[0], cc)
        return _rcopy(blk, blk, self.fsend.at[i], self.frecv.at[i], self.sib)

    def start(self):
        for t in range(self.n):
            self.local_in(t).start()
        for t in range(self.n):
            self.local_in(t).wait()
            self.stage[t][...] = self.raw[t][...].astype(MM)
            self.local_out(t).start()
        for j, t in self.pairs:
            self.first(j, t).start()

    def forward(self):
        for j, t in self.pairs:
            self.arrived(j, t).wait_recv()
            self.passed(j, t, self.c).start()

    def finish(self):
        for j, t in self.pairs:
            self.passed(j, t, 1 - self.c).wait_recv()
        for j, t in self.pairs:
            self.first(j, t).wait_send()
            self.passed(j, t, self.c).wait_send()
        for t in range(self.n):
            self.local_out(t).wait()


def all_gather_weights(shards, small, later):
    n, m = len(shards), len(later)
    shapes = [s.shape for s in shards]

    def body(*refs):
        w = refs[:n]
        sm = refs[n]
        lw = refs[n + 1:n + 1 + m]
        o = refs[n + 1 + m:2 * n + 1 + m]
        osm = refs[2 * n + 1 + m]
        lo = refs[2 * n + 2 + m:2 * n + 2 + 2 * m]
        scratch = refs[2 * n + 2 + 2 * m:]
        ssend, srecv, sloc, lsem_in, lsem_out = scratch[:5]
        lraw, lstage = scratch[5:5 + m], scratch[5 + m:5 + 2 * m]
        g = _Gather(shapes, w, o, scratch[5 + 2 * m:])
        own = pltpu.make_async_copy(sm, osm.at[g.me], sloc)
        own.start()
        loads = [pltpu.make_async_copy(lw[t], lraw[t], lsem_in.at[t]) for t in range(m)]
        for cp in loads:
            cp.start()
        g.start()
        small_cps = [_rcopy(sm, osm.at[g.me], ssend.at[j], srecv.at[j], g._chip(j)[1]) for j in range(3)]
        for cp in small_cps:
            cp.start()
        places = []
        for t in range(m):
            loads[t].wait()
            lstage[t][...] = lraw[t][...].astype(MM)
            places.append(pltpu.make_async_copy(lstage[t], lo[t].at[g.me], lsem_out.at[t]))
            places[t].start()
        g.forward()
        g.finish()
        for j in range(3):
            k, dev = g._chip(j)
            _rcopy(sm, osm.at[k], ssend.at[j], srecv.at[j], dev).wait_recv()
            small_cps[j].wait_send()
        own.wait()
        for cp in places:
            cp.wait()

    out_shape = _Gather.out_shapes(shards)
    out_shape.append(jax.ShapeDtypeStruct((N_CHIPS,) + small.shape, small.dtype))
    out_shape += _Gather.out_shapes(later)
    sems = pltpu.SemaphoreType.DMA
    return pl.pallas_call(
        body, name="all_gather_weights", out_shape=out_shape,
        in_specs=[ANY] * (n + 1 + m), out_specs=[ANY] * (n + 1 + m),
        scratch_shapes=[sems((3,)), sems((3,)), sems, sems((m,)), sems((m,))]
                       + [pltpu.VMEM(s.shape, s.dtype) for s in later] + [pltpu.VMEM(s.shape, MM) for s in later]
                       + _Gather.scratch(shards),
        compiler_params=_params(),
    )(*shards, small, *later)


def sibling_exchange(grads, name):
    n = len(grads)

    def body(*refs):
        g = refs[:n]
        o = refs[n:2 * n]
        send, recv = refs[2 * n:]
        x, y, c, _ = _place()
        cps = [_rcopy(g[t].at[:, 1 - c], o[t], send.at[t], recv.at[t], (x, y, 1 - c)) for t in range(n)]
        for cp in cps:
            cp.start()
        for cp in cps:
            cp.wait()

    sems = pltpu.SemaphoreType.DMA
    return pl.pallas_call(
        body, name=name,
        out_shape=[jax.ShapeDtypeStruct((a.shape[0],) + a.shape[2:], a.dtype) for a in grads],
        in_specs=[ANY] * n, out_specs=[ANY] * n, scratch_shapes=[sems((n,)), sems((n,))],
    )(*grads)


class _Scatter:
    def __init__(self, p, o, sems):
        self.n, self.p, self.o = len(p), p, o
        self.send, self.recv = sems
        _, _, self.c, self.chips = _place()

    @staticmethod
    def scratch(n):
        sems = pltpu.SemaphoreType.DMA
        return [sems((3 * n,)), sems((3 * n,))]

    @staticmethod
    def out_shapes(parts):
        return [jax.ShapeDtypeStruct((3,) + a.shape[1:], a.dtype) for a in parts]

    def copies(self):
        cps = []
        for j, (cx, cy) in enumerate(self.chips):
            for t in range(self.n):
                i = j * self.n + t
                cps.append(_rcopy(self.p[t].at[2 * cx + cy], self.o[t].at[j], self.send.at[i], self.recv.at[i],
                                  (cx, cy, self.c)))
        return cps

    def start(self):
        for cp in self.copies():
            cp.start()

    def finish(self):
        for cp in self.copies():
            cp.wait()


HBM_SPEC = pl.BlockSpec(memory_space=pltpu.HBM)
SEM_SPEC = pl.BlockSpec(memory_space=pltpu.SEMAPHORE)
DATAFLOW = pltpu.SideEffectType.DATAFLOW_SIDE_EFFECTING


def split_start(name, bufs, n_copies, copies):
    nb = len(bufs)

    def body(*refs):
        for cp in copies(refs[:nb], refs[nb], refs[nb + 1]):
            cp.start()
        token = refs[2 * nb + 2]
        token[...] = jnp.zeros_like(token)

    sems = [pltpu.SemaphoreType.DMA((n_copies,))] * 2
    res = pl.pallas_call(
        body, name=name,
        out_shape=sems + [pltpu.HBM(a.shape, a.dtype) for a in bufs] + [jax.ShapeDtypeStruct((8, 128), F32)],
        in_specs=[HBM_SPEC] * nb, out_specs=[SEM_SPEC] * 2 + [HBM_SPEC] * nb + [VMEM_SPEC],
        input_output_aliases={i: 2 + i for i in range(nb)},
        compiler_params=pltpu.CompilerParams(has_side_effects=DATAFLOW),
    )(*[pltpu.with_memory_space_constraint(a, pltpu.HBM) for a in bufs])
    return res[:-1], res[-1]


def split_wait(name, state, after, copies):
    sems, bufs = state[:2], state[2:]
    nb = len(bufs)

    def body(*refs):
        for cp in copies(refs[:nb], refs[nb], refs[nb + 1]):
            cp.wait_send()
            cp.wait_recv()

    return pl.pallas_call(
        body, name=name, out_shape=[pltpu.HBM(a.shape, a.dtype) for a in bufs],
        in_specs=[HBM_SPEC] * nb + [SEM_SPEC] * 2 + [ANY] * len(after), out_specs=[HBM_SPEC] * nb,
        input_output_aliases={i: i for i in range(nb)},
        compiler_params=pltpu.CompilerParams(has_side_effects=DATAFLOW),
    )(*bufs, *sems, *after)


def _scatter_copies(n):
    def copies(refs, send, recv):
        _, _, c, chips = _place()
        return [_rcopy(refs[t].at[2 * cx + cy], refs[n + t].at[j], send.at[3 * t + j], recv.at[3 * t + j], (cx, cy, c))
                for t in range(n) for j, (cx, cy) in enumerate(chips)]
    return copies


def scatter_start(parts, tag):
    lands = [lax.empty((3,) + p.shape[1:], p.dtype) for p in parts]
    return split_start("scatter_start_" + tag, list(parts) + lands, 3 * len(parts), _scatter_copies(len(parts)))


def scatter_wait(state, after, tag):
    n = (len(state) - 2) // 2
    return split_wait("scatter_wait_" + tag, state, after, _scatter_copies(n))[n:]


def _gather_copies(shapes, level):
    n = len(shapes)

    def copies(refs, send, recv):
        x, y, c, chips = _place()
        out = []
        for t in range(n):
            rh = shapes[t][0] // 2
            for j, (cx, cy) in enumerate(chips):
                k, dev = (2 * x + y, (cx, cy, c)) if level == 1 else (2 * cx + cy, (x, y, 1 - c))
                blk = refs[t].at[k, pl.ds(c * rh, rh), :]
                out.append(_rcopy(blk, blk, send.at[3 * t + j], recv.at[3 * t + j], dev))
        return out
    return copies


def sibling_swap(halves, name):
    n = len(halves)

    def body(*refs):
        h = refs[:n]
        o = refs[n:2 * n]
        send, recv = refs[2 * n:]
        x, y, c, _ = _place()
        cps = [_rcopy(h[t], o[t], send.at[t], recv.at[t], (x, y, 1 - c)) for t in range(n)]
        for cp in cps:
            cp.start()
        for cp in cps:
            cp.wait()

    sems = pltpu.SemaphoreType.DMA
    return pl.pallas_call(
        body, name=name, out_shape=[jax.ShapeDtypeStruct(a.shape, a.dtype) for a in halves],
        in_specs=[ANY] * n, out_specs=[ANY] * n, scratch_shapes=[sems((n,)), sems((n,))],
    )(*halves)


def small_all_reduce(ddw, v512, v1024, loss_parts):
    rows, width = PACK_ROWS, 512
    n512, n1024 = len(VEC512), len(VEC1024)

    def body(*refs):
        ddw_ref = refs[0]
        a_refs = refs[1:1 + n512]
        b_refs = refs[1 + n512:1 + n512 + n1024]
        lp_ref, o_ref, p_ref, gath_ref, send, recv = refs[1 + n512 + n1024:]
        p_ref[...] = jnp.zeros_like(p_ref)
        p_ref[0:32, :] = ddw_ref[...]
        p_ref[LOSS_ROW:LOSS_ROW + 1, 0:128] = jnp.sum(lp_ref[...], axis=0, keepdims=True) * 0.125
        for i, r in enumerate(a_refs):
            p_ref[32 + i:33 + i, :] = r[...]
        for i, r in enumerate(b_refs):
            base = 32 + n512 + 2 * i
            p_ref[base:base + 1, :] = r[:, 0:512]
            p_ref[base + 1:base + 2, :] = r[:, 512:1024]
        x, y, c, _ = _place()
        me = 4 * x + 2 * y + c
        gath_ref[me] = p_ref[...]
        cps = []
        for k in range(1, 8):
            dx, dy, dc = (k >> 2) & 1, (k >> 1) & 1, k & 1
            px = 1 - x if dx else x
            py = 1 - y if dy else y
            pc = 1 - c if dc else c
            cps.append(_rcopy(p_ref, gath_ref.at[me], send.at[k - 1], recv.at[k - 1], (px, py, pc)))
        for cp in cps:
            cp.start()
        for k in range(1, 8):
            dx, dy, dc = (k >> 2) & 1, (k >> 1) & 1, k & 1
            px = 1 - x if dx else x
            py = 1 - y if dy else y
            pc = 1 - c if dc else c
            _rcopy(p_ref, gath_ref.at[4 * px + 2 * py + pc], send.at[k - 1], recv.at[k - 1], (px, py, pc)).wait_recv()
        for cp in cps:
            cp.wait_send()
        total = gath_ref[0]
        for d in range(1, 8):
            total = total + gath_ref[d]
        o_ref[...] = total

    sems = pltpu.SemaphoreType.DMA
    n_in = 2 + n512 + n1024
    return pl.pallas_call(
        body, name="small_all_reduce", out_shape=jax.ShapeDtypeStruct((rows, width), F32),
        in_specs=[VMEM_SPEC] * n_in, out_specs=VMEM_SPEC,
        scratch_shapes=[pltpu.VMEM((rows, width), F32), pltpu.VMEM((8, rows, width), F32), sems((7,)), sems((7,))],
    )(ddw, *[v512[n] for n in VEC512], *[v1024[n] for n in VEC1024], loss_parts)


def _row_block(r):
    for tr in (512, 352, 256, 128):
        if r % tr == 0:
            return tr
    return r


def add_halves(g, recv, name):
    _, _, r, w = g.shape
    tr = _row_block(r)

    def body(g0_ref, g1_ref, r_ref, ob_ref, own_ref):
        k = pl.program_id(1)
        c = lax.axis_index("c")
        me = 2 * lax.axis_index("x") + lax.axis_index("y")
        t = jnp.where(c == 0, g0_ref[0, 0], g1_ref[0, 0]).astype(F32) + r_ref[0].astype(F32)
        ob_ref[0] = t.astype(MM)
        mine = jnp.where(k == me, t, 0.0)

        @pl.when(k == 0)
        def _():
            own_ref[...] = mine

        @pl.when(k != 0)
        def _():
            own_ref[...] += mine

    return pl.pallas_call(
        body, name=name, grid=(r // tr, N_CHIPS),
        in_specs=[pl.BlockSpec((1, 1, tr, w), lambda i, k: (k, 0, i, 0)),
                  pl.BlockSpec((1, 1, tr, w), lambda i, k: (k, 1, i, 0)),
                  pl.BlockSpec((1, tr, w), lambda i, k: (k, i, 0))],
        out_specs=[pl.BlockSpec((1, tr, w), lambda i, k: (k, i, 0)),
                   pl.BlockSpec((tr, w), lambda i, k: (i, 0))],
        out_shape=(jax.ShapeDtypeStruct((N_CHIPS, r, w), MM), jax.ShapeDtypeStruct((r, w), F32)),
        compiler_params=_params(("arbitrary", "arbitrary")),
    )(g, g, recv)


def sum_parts(own, rin, after, name):
    _, r, w = rin.shape
    tr = _row_block(r)

    def body(o_ref, r_ref, after_ref, out_ref):
        out_ref[...] = ((o_ref[...] + r_ref[0].astype(F32)) + r_ref[1].astype(F32)) + r_ref[2].astype(F32)

    return pl.pallas_call(
        body, name=name, grid=(r // tr,), out_shape=jax.ShapeDtypeStruct((r, w), F32),
        in_specs=[pl.BlockSpec((tr, w), lambda i: (i, 0)), pl.BlockSpec((3, tr, w), lambda i: (0, i, 0)),
                  _full_spec((8, 128))],
        out_specs=pl.BlockSpec((tr, w), lambda i: (i, 0)),
        compiler_params=_params(("arbitrary",)),
    )(own, rin, after)


def _adamw_math(w, g, m, v):
    mn = ADAM_B1 * m + (1.0 - ADAM_B1) * g
    vn = ADAM_B2 * v + (1.0 - ADAM_B2) * (g * g)
    m_hat = mn / (1.0 - ADAM_B1 ** ADAM_STEP)
    v_hat = vn / (1.0 - ADAM_B2 ** ADAM_STEP)
    return -ADAM_LR * (m_hat / (jnp.sqrt(v_hat) + ADAM_EPS) + ADAM_WD * w), mn, vn


def adamw(w, mine, other, m, v, name):
    r, c = w.shape
    rh = r // 2
    tr = _row_block(rh)
    if c >= 1024 and tr % 512 == 0:
        tr = 256
    nb = rh // tr

    def body(w_ref, a_ref, b_ref, m_ref, v_ref, go_ref, d_ref, mo_ref, vo_ref):
        gv = jnp.where(lax.axis_index("c") == pl.program_id(0), a_ref[...], b_ref[...])
        go_ref[...] = gv
        d_ref[...], mo_ref[...], vo_ref[...] = _adamw_math(w_ref[...], gv, m_ref[...], v_ref[...])

    spec = pl.BlockSpec((tr, c), lambda h, i: (h * nb + i, 0))
    half = pl.BlockSpec((tr, c), lambda h, i: (i, 0))
    out = jax.ShapeDtypeStruct((r, c), F32)
    return pl.pallas_call(
        body, name=name, grid=(2, nb), out_shape=(out, out, out, out),
        in_specs=[spec, half, half, spec, spec], out_specs=[spec] * 4,
        compiler_params=_params(("arbitrary", "arbitrary")),
    )(w, mine, other, m, v)


def adamw_small(gsum, params):
    names = list(params)
    flat = [a for n in names for a in params[n]]

    def body(*refs):
        g_ref = refs[0]
        ins = refs[1:1 + 3 * len(names)]
        outs = refs[1 + 3 * len(names):]
        me = 2 * lax.axis_index("x") + lax.axis_index("y")
        for i, n in enumerate(names):
            w_ref, m_ref, v_ref = ins[3 * i:3 * i + 3]
            go_ref, d_ref, mo_ref, vo_ref = outs[4 * i:4 * i + 4]
            if n == "conv_dw_w":
                gv = jnp.zeros((CONV_WIDTH, 128), F32)
                for k in range(N_CHIPS):
                    gv = gv + jnp.where(me == k, g_ref[0:CONV_WIDTH, 128 * k:128 * (k + 1)], 0.0)
            elif n in VEC512:
                r0 = 32 + VEC512.index(n)
                gv = g_ref[r0:r0 + 1, :]
            else:
                r0 = 32 + len(VEC512) + 2 * VEC1024.index(n)
                gv = jnp.concatenate([g_ref[r0:r0 + 1, :], g_ref[r0 + 1:r0 + 2, :]], axis=1)
            go_ref[...] = gv
            d_ref[...], mo_ref[...], vo_ref[...] = _adamw_math(w_ref[...], gv, m_ref[...], v_ref[...])

    out_shape = [jax.ShapeDtypeStruct(params[n][0].shape, F32) for n in names for _ in range(4)]
    res = pl.pallas_call(
        body, name="adamw_small", out_shape=out_shape,
        in_specs=[VMEM_SPEC] * (1 + len(flat)), out_specs=[VMEM_SPEC] * len(out_shape),
        compiler_params=_params(),
    )(gsum, *flat)
    return {n: res[4 * i:4 * i + 4] for i, n in enumerate(names)}


REST = ("w_ffn_up", "w_ffn_down", "w_out", "w_conv_branch", "w_att_branch")
VEC512 = ("conv_dw_b", "conv_ln_g", "conv_ln_b")
VEC1024 = ("norm_mix_pre", "b_conv_branch", "norm_mix_post", "norm_ffn_pre", "norm_ffn_post")
PACK_ROWS = 48
LOSS_ROW = 47


def kernel(x, norm_mix_pre, w_in, conv_dw_w, conv_dw_b, conv_ln_g, conv_ln_b, w_conv_branch, b_conv_branch, w_att_branch, w_out, norm_mix_post, norm_ffn_pre, w_ffn_up, w_ffn_down, norm_ffn_post, loss_target, m_norm_mix_pre, m_w_in, m_conv_dw_w, m_conv_dw_b, m_conv_ln_g, m_conv_ln_b, m_w_conv_branch, m_b_conv_branch, m_w_att_branch, m_w_out, m_norm_mix_post, m_norm_ffn_pre, m_w_ffn_up, m_w_ffn_down, m_norm_ffn_post, v_norm_mix_pre, v_w_in, v_conv_dw_w, v_conv_dw_b, v_conv_ln_g, v_conv_ln_b, v_w_conv_branch, v_b_conv_branch, v_w_att_branch, v_w_out, v_norm_mix_post, v_norm_ffn_pre, v_w_ffn_up, v_w_ffn_down, v_norm_ffn_post):
    weights = dict(norm_mix_pre=norm_mix_pre, w_in=w_in, conv_dw_w=conv_dw_w, conv_dw_b=conv_dw_b, conv_ln_g=conv_ln_g, conv_ln_b=conv_ln_b, w_conv_branch=w_conv_branch, b_conv_branch=b_conv_branch, w_att_branch=w_att_branch, w_out=w_out, norm_mix_post=norm_mix_post, norm_ffn_pre=norm_ffn_pre, w_ffn_up=w_ffn_up, w_ffn_down=w_ffn_down, norm_ffn_post=norm_ffn_post)
    mom = dict(norm_mix_pre=m_norm_mix_pre, w_in=m_w_in, conv_dw_w=m_conv_dw_w, conv_dw_b=m_conv_dw_b, conv_ln_g=m_conv_ln_g, conv_ln_b=m_conv_ln_b, w_conv_branch=m_w_conv_branch, b_conv_branch=m_b_conv_branch, w_att_branch=m_w_att_branch, w_out=m_w_out, norm_mix_post=m_norm_mix_post, norm_ffn_pre=m_norm_ffn_pre, w_ffn_up=m_w_ffn_up, w_ffn_down=m_w_ffn_down, norm_ffn_post=m_norm_ffn_post)
    var = dict(norm_mix_pre=v_norm_mix_pre, w_in=v_w_in, conv_dw_w=v_conv_dw_w, conv_dw_b=v_conv_dw_b, conv_ln_g=v_conv_ln_g, conv_ln_b=v_conv_ln_b, w_conv_branch=v_w_conv_branch, b_conv_branch=v_b_conv_branch, w_att_branch=v_w_att_branch, w_out=v_w_out, norm_mix_post=v_norm_mix_post, norm_ffn_pre=v_norm_ffn_pre, w_ffn_up=v_w_ffn_up, w_ffn_down=v_w_ffn_down, norm_ffn_post=v_norm_ffn_post)
    order = list(weights)
    grads, deltas, new_m, new_v = {}, {}, {}, {}
    xs = x.reshape(SEQ, D_MODEL)
    tgt = loss_target.reshape(SEQ, D_MODEL)
    row = lambda a: a.reshape(1, -1)
    g1, g2, g3, g4 = (row(weights[n]) for n in ("norm_mix_pre", "norm_mix_post", "norm_ffn_pre", "norm_ffn_post"))
    ln_g, ln_b = row(conv_ln_g), row(conv_ln_b)

    def reduce_prepare(names, partial, tag):
        from_sib = sibling_exchange([partial[n] for n in names], "sibling_exchange_" + tag)
        return [add_halves(partial[n], r, "add_" + n) for n, r in zip(names, from_sib)]

    def reduce_finish(names, summed, from_chips, after, tag):
        halves = [sum_parts(s[1], r, after, "sum_" + n) for n, s, r in zip(names, summed, from_chips)]
        for n, a, b in zip(names, halves, sibling_swap(halves, "sibling_swap_" + tag)):
            grads[n], deltas[n], new_m[n], new_v[n] = adamw(weights[n], a, b, mom[n], var[n], "adamw_" + n)

    w_in_g, dw_g, *rest = all_gather_weights([w_in], conv_dw_w, [weights[n] for n in REST])
    w_dw_full = jnp.concatenate([dw_g[k] for k in range(N_CHIPS)], axis=1)
    rest_shapes = [weights[n].shape for n in REST]
    state, token = split_start("gather_start", rest, 3 * len(REST), _gather_copies(rest_shapes, 1))
    h1, ci, q, k, v, gc, ga = in_proj_fwd(xs, g1 + token[0:1, 0:1], w_in_g)
    u1, u3 = conv_fwd(ci, w_dw_full, row(conv_dw_b), ln_g, ln_b)
    att, rc = attn_fwd(q, k, v)
    rest = split_wait("gather_wait", state, [att], _gather_copies(rest_shapes, 1))
    ffn, mix_w = rest[:2], rest[2:]
    state, token = split_start("pass_ffn_start", ffn, 3 * 2, _gather_copies(rest_shapes[:2], 2))
    state_mix, _ = split_start("pass_mix_start", mix_w, 3 * 3, _gather_copies(rest_shapes[2:], 2))
    w_out_g, w_cb_g, w_ab_g = split_wait("pass_mix_wait", state_mix, [], _gather_copies(rest_shapes[2:], 2))
    w_out_g = w_out_g.reshape(D_MODEL, D_MODEL)
    co, ao, merged, mix, x2, h2 = mix_fwd(u3, att, gc, ga, xs, w_cb_g, row(b_conv_branch), w_ab_g, w_out_g,
                                          g2 + token[0:1, 0:1], g3)
    w_up_g, w_down_g = split_wait("pass_ffn_wait", state, [h2], _gather_copies(rest_shapes[:2], 2))
    w_down_g = w_down_g.reshape(D_FF, D_MODEL)
    gate, up, act = ffn_up_fwd(h2, w_up_g)
    dff, dy, loss_parts, dg4 = ffn_down_loss(act, w_down_g, x2, tgt, g4)

    partial = {}
    dgu = ffn_act_bwd(dff, w_down_g, gate, up)
    partial["w_ffn_down"] = weight_grad(act, dff, "dw_ffn_down", False, tk=UP_SHARD)
    dx2, dmix, dg3, dg2 = ffn_in_bwd(dgu, w_up_g, x2, mix, dy, g3, g2)
    partial["w_ffn_up"] = weight_grad(h2, dgu, "dw_ffn_up", True)
    dco, dao, dg, du3, datt, dbcb = merge_bwd(dmix, w_out_g, gc, ga, co, ao, w_cb_g, w_ab_g)
    partial["w_out"] = weight_grad(merged, dmix, "dw_out", False, tk=512)
    partial["w_conv_branch"] = weight_grad(u3, dco, "dw_conv_branch", True)
    partial["w_att_branch"] = weight_grad(att, dao, "dw_att_branch", True)
    summed = reduce_prepare(REST, partial, "rest")
    state, token = scatter_start([s[0] for s in summed], "rest")
    zero = token[0:1, 0:1]
    dci, ddw, dbdw, dlng, dlnb = conv_bwd(du3, u1, ci, w_dw_full, ln_g + zero, ln_b)
    dqkv = attn_bwd(q, k, v, datt, rc + zero)
    from_chips = scatter_wait(state, [dci, dqkv], "rest")
    dproj = (dci, dqkv, dg)
    partial["w_in"] = weight_grad_in(h1, dproj)
    summed_in = reduce_prepare(("w_in",), partial, "w_in")
    state, token = scatter_start([summed_in[0][0]], "w_in")
    grad_x, dg1 = in_proj_bwd(dproj, w_in_g, xs, dx2, g1 + token[0:1, 0:1])
    reduce_finish(REST, summed, from_chips, token, "rest")
    from_chips_in = scatter_wait(state, [dg1] + [new_v[n] for n in REST], "w_in")
    reduce_finish(("w_in",), summed_in, from_chips_in, token, "w_in")

    v512 = dict(conv_dw_b=dbdw, conv_ln_g=dlng, conv_ln_b=dlnb)
    v1024 = dict(norm_mix_pre=dg1, b_conv_branch=dbcb, norm_mix_post=dg2, norm_ffn_pre=dg3, norm_ffn_post=dg4)
    gsum = small_all_reduce(ddw, v512, v1024, loss_parts)
    loss = gsum[LOSS_ROW, 0]
    as_rows = lambda n, a: a if n == "conv_dw_w" else a.reshape(1, -1)
    small_names = ("conv_dw_w",) + VEC512 + VEC1024
    small = adamw_small(gsum, {n: tuple(as_rows(n, d[n]) for d in (weights, mom, var)) for n in small_names})
    for n in small_names:
        grads[n], deltas[n], new_m[n], new_v[n] = (a.reshape(weights[n].shape) for a in small[n])

    return (loss, grad_x.reshape(1, SEQ, D_MODEL), *[grads[n] for n in order], *[deltas[n] for n in order],
            *[new_m[n] for n in order], *[new_v[n] for n in order])
```

```python
import jax
import jax.numpy as jnp
from jax import lax
from jax.experimental import pallas as pl
from jax.experimental.pallas import tpu as pltpu

F32 = jnp.float32
MM = jnp.bfloat16

SEQ = 2048
D_MODEL = 1024
CONV_DIM = 512
ATT_DIM = 512
CONV_WIDTH = 31
D_FF = 2816
IN_COLS = 2 * CONV_DIM + 3 * ATT_DIM + 2 * D_MODEL
N_CHIPS = 4
IN_SHARD = IN_COLS // N_CHIPS
UP_SHARD = 2 * D_FF // N_CHIPS
BR_SHARD = D_MODEL // N_CHIPS
EPS = 1e-6
ATT_SCALE = 0.125

TM = 256
GLU_ROWS = 256
TQ = 128
CONV_TILE = 64
CONV_WIN = CONV_TILE + 32
VMEM_LIMIT = 56 * 1024 * 1024

ADAM_LR = 0.001
ADAM_B1 = 0.9
ADAM_B2 = 0.999
ADAM_EPS = 1e-08
ADAM_WD = 0.01
ADAM_STEP = 10

MESH = pl.DeviceIdType.MESH
ANY = pl.BlockSpec(memory_space=pl.ANY)
VMEM_SPEC = pl.BlockSpec(memory_space=pltpu.VMEM)

NT_DIMS = (((1,), (1,)), ((), ()))
TN_DIMS = (((0,), (0,)), ((), ()))

IN_PIECES = (("ci", 0, 1024), ("q", 1024, 1536), ("k", 1536, 2048), ("v", 2048, 2560),
             ("gc", 2560, 3584), ("ga", 3584, 4608))


def _params(sem=None, vmem=VMEM_LIMIT):
    return pltpu.CompilerParams(dimension_semantics=sem, vmem_limit_bytes=vmem)


def _dot(a, b):
    return jnp.dot(a, b, preferred_element_type=F32)


def _dot_nt(a, b):
    return lax.dot_general(a, b, NT_DIMS, preferred_element_type=F32)


def _dot_tn(a, b):
    return lax.dot_general(a, b, TN_DIMS, preferred_element_type=F32)


def _sigmoid(x):
    return 1.0 / (1.0 + jnp.exp(-x))


def _rms(x):
    r = lax.rsqrt(jnp.mean(x * x, axis=-1, keepdims=True) + EPS)
    return x * r, r


def _rms_bwd(dy_g, n, r):
    return r * (dy_g - n * jnp.mean(dy_g * n, axis=-1, keepdims=True))


def _row_tile_spec(width, tm=TM):
    return pl.BlockSpec((tm, width), lambda i: (i, 0))


def _full_spec(shape):
    nd = len(shape)
    return pl.BlockSpec(shape, lambda *_: (0,) * nd)


def _weight_spec(shape):
    nd = len(shape)
    return pl.BlockSpec(shape, lambda *_: (0,) * nd, pipeline_mode=pl.Buffered(1))


def _acc_rows(ref, val, first):
    @pl.when(first)
    def _():
        ref[...] = val

    @pl.when(jnp.logical_not(first))
    def _():
        ref[...] += val


TOKEN_SPEC = pl.BlockSpec((8, 128), lambda *_: (0, 0))


def in_proj_fwd(x, g1, w_in_g, after):
    def body(x_ref, g_ref, w_ref, after_ref, h_ref, ci_ref, q_ref, k_ref, v_ref, gc_ref, ga_ref):
        n, _ = _rms(x_ref[...])
        h = (n * g_ref[...]).astype(MM)
        h_ref[...] = h
        outs = dict(ci=ci_ref, q=q_ref, k=k_ref, v=v_ref, gc=gc_ref, ga=ga_ref)
        for j in range(N_CHIPS):
            p = _dot(h, w_ref[j])
            g0 = j * IN_SHARD
            for name, s, e in IN_PIECES:
                lo, hi = max(s, g0), min(e, g0 + IN_SHARD)
                if lo < hi:
                    ref = outs[name]
                    part = p[:, lo - g0:hi - g0]
                    if name == "q":
                        part = part * ATT_SCALE
                    ref[:, lo - s:hi - s] = part.astype(ref.dtype)

    out_shape = [
        jax.ShapeDtypeStruct((SEQ, D_MODEL), MM),
        jax.ShapeDtypeStruct((SEQ, 2 * CONV_DIM), F32),
        jax.ShapeDtypeStruct((SEQ, ATT_DIM), MM),
        jax.ShapeDtypeStruct((SEQ, ATT_DIM), MM),
        jax.ShapeDtypeStruct((SEQ, ATT_DIM), MM),
        jax.ShapeDtypeStruct((SEQ, D_MODEL), F32),
        jax.ShapeDtypeStruct((SEQ, D_MODEL), F32),
    ]
    return pl.pallas_call(
        body, name="in_proj_fwd", grid=(SEQ // TM,), out_shape=out_shape,
        in_specs=[_row_tile_spec(D_MODEL), _full_spec((1, D_MODEL)), _weight_spec(w_in_g.shape), TOKEN_SPEC],
        out_specs=[_row_tile_spec(s.shape[1]) for s in out_shape],
        compiler_params=_params(("arbitrary",)),
    )(x, g1, w_in_g, after)


def _shifted_sum(win, terms):
    by_rot = {}
    for m, coef in terms:
        by_rot.setdefault(m % 8, []).append((m // 8, coef))
    acc = None
    n = win.shape[0]
    for rot in sorted(by_rot):
        shifted = win if rot == 0 else pltpu.roll(win, n - rot, 0)
        for a, coef in by_rot[rot]:
            t = coef * shifted[8 * a:8 * a + CONV_TILE, :]
            acc = t if acc is None else acc + t
    return acc


def _glu_into(ci_ref, upad_ref):
    upad_ref[0:32, :] = jnp.zeros((32, CONV_DIM), F32)

    def step(i, c):
        t0 = pl.multiple_of(i * GLU_ROWS, GLU_ROWS)
        a = ci_ref[pl.ds(t0, GLU_ROWS), 0:CONV_DIM]
        b = ci_ref[pl.ds(t0, GLU_ROWS), CONV_DIM:2 * CONV_DIM]
        upad_ref[pl.ds(t0 + 32, GLU_ROWS), :] = a * _sigmoid(b)
        return c

    lax.fori_loop(0, SEQ // GLU_ROWS, step, 0)


def _layernorm_parts(u1):
    mu = jnp.mean(u1, axis=-1, keepdims=True)
    xc = u1 - mu
    rstd = lax.rsqrt(jnp.mean(xc * xc, axis=-1, keepdims=True) + EPS)
    return xc * rstd, rstd


def conv_fwd(ci, w_dw, b_dw, ln_g, ln_b):
    def body(ci_ref, w_ref, b_ref, g_ref, bb_ref, u1_ref, u3_ref, upad_ref):
        _glu_into(ci_ref, upad_ref)

        def step(i, c):
            t0 = pl.multiple_of(i * CONV_TILE, CONV_TILE)
            win = upad_ref[pl.ds(t0, CONV_WIN), :]
            u1 = _shifted_sum(win, [(j + 2, w_ref[j:j + 1, :]) for j in range(CONV_WIDTH)]) + b_ref[...]
            u1_ref[pl.ds(t0, CONV_TILE), :] = u1
            xh, _ = _layernorm_parts(u1)
            u2 = xh * g_ref[...] + bb_ref[...]
            u3_ref[pl.ds(t0, CONV_TILE), :] = (u2 * _sigmoid(u2)).astype(MM)
            return c

        lax.fori_loop(0, SEQ // CONV_TILE, step, 0)

    return pl.pallas_call(
        body, name="conv_fwd",
        out_shape=[jax.ShapeDtypeStruct((SEQ, CONV_DIM), F32), jax.ShapeDtypeStruct((SEQ, CONV_DIM), MM)],
        in_specs=[VMEM_SPEC] * 5, out_specs=[VMEM_SPEC] * 2,
        scratch_shapes=[pltpu.VMEM((SEQ + 32, CONV_DIM), F32)],
        compiler_params=_params(),
    )(ci, w_dw, b_dw, ln_g, ln_b)


def _softplus(z):
    return jnp.maximum(z, 0.0) + jnp.log(1.0 + jnp.exp(-jnp.abs(z)))


def _cumsum_weights(suffix, with_total):
    n = 256 if with_total else 128
    r = lax.broadcasted_iota(jnp.int32, (128, n), 0)
    c = lax.broadcasted_iota(jnp.int32, (128, n), 1)
    tri = (r >= c) if suffix else (r <= c)
    return jnp.logical_or(tri, c >= 128).astype(MM)


NO_SCORE = -1e30
N_KB = SEQ // TQ


def _score_bias(lane, row, i, j):
    keep = jnp.logical_and(i >= 0, jnp.logical_or(j < i, lane < row))
    return jnp.where(keep, 0.0, NO_SCORE)


def _block_pipeline(n_stages, descending, step, on_query_block=None):
    n_lag = n_stages - 1
    none = jnp.int32(-1)

    def shift(cur, lag):
        step([cur] + [(lag[2 * s], lag[2 * s + 1]) for s in range(n_lag)])
        return (cur[0], cur[1]) + tuple(lag[:-2])

    def outer(i, lag):
        if on_query_block is not None:
            on_query_block(i)

        def inner(n, lag):
            return shift((i, i - n if descending else n), lag)
        return lax.fori_loop(0, i + 1, inner, lag)

    lag = lax.fori_loop(0, N_KB, outer, (none,) * (2 * n_lag))
    lax.fori_loop(0, n_lag, lambda n, lag: shift((none, none), lag), lag)


def _head_masks():
    lane = lax.broadcasted_iota(jnp.int32, (TQ, 128), 1)
    row = lax.broadcasted_iota(jnp.int32, (TQ, 128), 0)
    return lane, row, lane < 64


def _pick_head(x, head0, h):
    zero = jnp.zeros_like(x)
    return jnp.where(head0, x, zero) if h == 0 else jnp.where(head0, zero, x)


N_PAIRS = ATT_DIM // 128


def _split_heads(src_ref, dst_ref):
    _, _, head0 = _head_masks()

    def block(b, c):
        r0 = pl.multiple_of(b * TQ, TQ)
        d0 = pl.multiple_of(b * 2 * TQ, 2 * TQ)
        for p in range(N_PAIRS):
            x = src_ref[pl.ds(r0, TQ), 128 * p:128 * (p + 1)]
            for h in range(2):
                dst_ref[p, pl.ds(d0 + TQ * h, TQ), :] = _pick_head(x, head0, h)
        return c

    lax.fori_loop(0, N_KB, block, 0)


def attn_fwd(q, k, v):
    def body(q_ref, k_ref, v_ref, o_ref, rc_ref, acc_ref, r_ref, z_ref, spb_ref, ab_ref, qm_ref, vm_ref):
        lane, row, _ = _head_masks()
        w = _cumsum_weights(suffix=True, with_total=True)
        _split_heads(q_ref, qm_ref)
        _split_heads(v_ref, vm_ref)
        acc_ref[...] = jnp.zeros_like(acc_ref)
        r_ref[...] = jnp.zeros_like(r_ref)
        rc_ref[...] = jnp.zeros_like(rc_ref)
        z_ref[...] = jnp.full(z_ref.shape, NO_SCORE, F32)
        spb_ref[...] = jnp.zeros_like(spb_ref)
        ab_ref[...] = jnp.zeros_like(ab_ref)

        def step(pairs):
            (i1, j1), (i2, j2), (i3, j3) = pairs
            k1, q2, q3 = (pl.multiple_of(jnp.maximum(b, 0) * TQ, TQ) for b in (j1, i2, i3))
            q1, k3 = (pl.multiple_of(jnp.maximum(b, 0) * 2 * TQ, 2 * TQ) for b in (i1, j3))
            bias1 = _score_bias(lane, row, i1, j1)
            first2 = j2 == i2
            rc_rows = rc_ref[pl.ds(q2, TQ), :]
            for p in range(N_PAIRS):
                cols = slice(128 * p, 128 * (p + 1))
                kb = k_ref[pl.ds(k1, TQ), cols]
                acc_ref[pl.ds(q3, TQ), cols] += _dot(ab_ref[p], vm_ref[p, pl.ds(k3, 2 * TQ), :])
                for h in range(2):
                    hh = 2 * p + h
                    r = _dot(spb_ref[hh], w)
                    r_in = jnp.where(first2, 0.0, r_ref[hh])
                    ab_ref[p, :, 128 * h:128 * (h + 1)] = jnp.exp(z_ref[hh] - (r[:, :128] + r_in)).astype(MM)
                    rc_rows = jnp.where(jnp.logical_and(lane == 16 * hh + j2, i2 >= 0), r_in, rc_rows)
                    r_ref[hh] = r_in + r[:, 128:]
                    z = _dot_nt(qm_ref[p, pl.ds(q1 + TQ * h, TQ), :], kb) + bias1
                    z_ref[hh] = z
                    spb_ref[hh] = _softplus(z).astype(MM)
            rc_ref[pl.ds(q2, TQ), :] = rc_rows

        _block_pipeline(3, True, step)
        o_ref[...] = acc_ref[...].astype(MM)

    return pl.pallas_call(
        body, name="attn_fwd",
        out_shape=[jax.ShapeDtypeStruct((SEQ, ATT_DIM), MM), jax.ShapeDtypeStruct((SEQ, 128), F32)],
        in_specs=[VMEM_SPEC] * 3, out_specs=[VMEM_SPEC] * 2,
        scratch_shapes=[pltpu.VMEM((SEQ, ATT_DIM), F32), pltpu.VMEM((8, TQ, 128), F32),
                        pltpu.VMEM((8, TQ, 128), F32), pltpu.VMEM((8, TQ, 128), MM),
                        pltpu.VMEM((N_PAIRS, TQ, 256), MM), pltpu.VMEM((N_PAIRS, 2 * SEQ, 128), MM),
                        pltpu.VMEM((N_PAIRS, 2 * SEQ, 128), MM)],
        compiler_params=_params(),
    )(q, k, v)


def mix_fwd(u3, att, gc, ga, x, w_cb_g, b_cb, w_ab_g, w_out_g, g2, g3, after):
    def body(u_ref, a_ref, gc_ref, ga_ref, x_ref, wcb_ref, bcb_ref, wab_ref, wout_ref, g2_ref, g3_ref, after_ref,
             co_ref, ao_ref, mg_ref, mix_ref, x2_ref, h2_ref):
        u = u_ref[...]
        a = a_ref[...]
        co = jnp.concatenate([_dot(u, wcb_ref[j]) for j in range(N_CHIPS)], axis=1) + bcb_ref[...]
        ao = jnp.concatenate([_dot(a, wab_ref[j]) for j in range(N_CHIPS)], axis=1)
        co_ref[...] = co.astype(MM)
        ao_ref[...] = ao.astype(MM)
        merged = (_sigmoid(gc_ref[...]) * co + _sigmoid(ga_ref[...]) * ao).astype(MM)
        mg_ref[...] = merged
        mix = _dot(merged, wout_ref[...])
        mix_ref[...] = mix
        n2, _ = _rms(mix)
        x2 = x_ref[...] + n2 * g2_ref[...]
        x2_ref[...] = x2
        n3, _ = _rms(x2)
        h2_ref[...] = (n3 * g3_ref[...]).astype(MM)

    out_shape = [
        jax.ShapeDtypeStruct((SEQ, D_MODEL), MM), jax.ShapeDtypeStruct((SEQ, D_MODEL), MM),
        jax.ShapeDtypeStruct((SEQ, D_MODEL), MM), jax.ShapeDtypeStruct((SEQ, D_MODEL), F32),
        jax.ShapeDtypeStruct((SEQ, D_MODEL), F32), jax.ShapeDtypeStruct((SEQ, D_MODEL), MM),
    ]
    vec = _full_spec((1, D_MODEL))
    return pl.pallas_call(
        body, name="mix_fwd", grid=(SEQ // TM,), out_shape=out_shape,
        in_specs=[_row_tile_spec(CONV_DIM), _row_tile_spec(ATT_DIM), _row_tile_spec(D_MODEL),
                  _row_tile_spec(D_MODEL), _row_tile_spec(D_MODEL), _weight_spec(w_cb_g.shape), vec,
                  _weight_spec(w_ab_g.shape), _weight_spec(w_out_g.shape), vec, vec, TOKEN_SPEC],
        out_specs=[_row_tile_spec(D_MODEL)] * 6,
        compiler_params=_params(("arbitrary",)),
    )(u3, att, gc, ga, x, w_cb_g, b_cb, w_ab_g, w_out_g, g2, g3, after)


def ffn_up_fwd(h2, w_up_g):
    def body(h_ref, wg_ref, wu_ref, gate_ref, up_ref, act_ref):
        h = h_ref[...]
        gate = _dot(h, wg_ref[0])
        up = _dot(h, wu_ref[0])
        gate_ref[...] = gate.astype(MM)
        up_ref[...] = up.astype(MM)
        act_ref[...] = (gate * _sigmoid(gate) * up).astype(MM)

    tile = pl.BlockSpec((TM, UP_SHARD), lambda n, i: (i, n))
    act = jax.ShapeDtypeStruct((SEQ, D_FF), MM)
    return pl.pallas_call(
        body, name="ffn_up_fwd", grid=(2, SEQ // TM), out_shape=[act, act, act],
        in_specs=[pl.BlockSpec((TM, D_MODEL), lambda n, i: (i, 0)),
                  pl.BlockSpec((1, D_MODEL, UP_SHARD), lambda n, i: (n, 0, 0)),
                  pl.BlockSpec((1, D_MODEL, UP_SHARD), lambda n, i: (n + 2, 0, 0))],
        out_specs=[tile, tile, tile],
        compiler_params=_params(("arbitrary", "arbitrary")),
    )(h2, w_up_g, w_up_g)


def ffn_down_loss(act, w_down_g, x2, target, g4):
    def body(act_ref, wd_ref, x2_ref, t_ref, g_ref, dff_ref, dy_ref, loss_ref, dg_ref):
        ff = _dot(act_ref[...], wd_ref[...])
        n4, r4 = _rms(ff)
        g4v = g_ref[...]
        err = x2_ref[...] + n4 * g4v - t_ref[...]
        row_loss = jnp.mean(err * err, axis=-1, keepdims=True)
        loss_ref[...] = jnp.zeros((8, 128), F32) + 0.5 * jnp.sum(row_loss, axis=0, keepdims=True)
        dy = err * (1.0 / D_MODEL)
        dy_ref[...] = dy
        dff_ref[...] = _rms_bwd(dy * g4v, n4, r4).astype(MM)
        _acc_rows(dg_ref, jnp.sum(dy * n4, axis=0, keepdims=True), pl.program_id(0) == 0)

    nt = SEQ // TM
    vec = _full_spec((1, D_MODEL))
    return pl.pallas_call(
        body, name="ffn_down_loss", grid=(nt,),
        out_shape=(jax.ShapeDtypeStruct((SEQ, D_MODEL), MM), jax.ShapeDtypeStruct((SEQ, D_MODEL), F32),
                   jax.ShapeDtypeStruct((nt * 8, 128), F32), jax.ShapeDtypeStruct((1, D_MODEL), F32)),
        in_specs=[_row_tile_spec(D_FF), _weight_spec(w_down_g.shape), _row_tile_spec(D_MODEL),
                  _row_tile_spec(D_MODEL), vec],
        out_specs=[_row_tile_spec(D_MODEL), _row_tile_spec(D_MODEL),
                   pl.BlockSpec((8, 128), lambda i: (i, 0)), vec],
        compiler_params=_params(("arbitrary",)),
    )(act, w_down_g, x2, target, g4)


def ffn_act_bwd(dff, w_down_g, gate, up):
    def body(dff_ref, wd_ref, gate_ref, up_ref, dgu_ref):
        dact = _dot_nt(dff_ref[...], wd_ref[...])
        gate = gate_ref[...].astype(F32)
        sg = _sigmoid(gate)
        dgu_ref[:, 0:D_FF] = (dact * up_ref[...].astype(F32) * (sg * (1.0 + gate * (1.0 - sg)))).astype(MM)
        dgu_ref[:, D_FF:2 * D_FF] = (dact * (gate * sg)).astype(MM)

    return pl.pallas_call(
        body, name="ffn_act_bwd", grid=(SEQ // TM,),
        out_shape=jax.ShapeDtypeStruct((SEQ, 2 * D_FF), MM),
        in_specs=[_row_tile_spec(D_MODEL), _weight_spec(w_down_g.shape), _row_tile_spec(D_FF), _row_tile_spec(D_FF)],
        out_specs=_row_tile_spec(2 * D_FF),
        compiler_params=_params(("arbitrary",)),
    )(dff, w_down_g, gate, up)


def ffn_in_bwd(dgu, w_up_g, x2, mix, dy, g3, g2):
    def body(dgu_ref, w_ref, x2_ref, mix_ref, dy_ref, g3_ref, g2_ref, dx2_ref, dmix_ref, dg3_ref, dg2_ref):
        dh2 = None
        for j in range(N_CHIPS):
            t = _dot_nt(dgu_ref[:, j * UP_SHARD:(j + 1) * UP_SHARD], w_ref[j])
            dh2 = t if dh2 is None else dh2 + t
        first = pl.program_id(0) == 0
        n3, r3 = _rms(x2_ref[...])
        dx2 = dy_ref[...] + _rms_bwd(dh2 * g3_ref[...], n3, r3)
        dx2_ref[...] = dx2
        _acc_rows(dg3_ref, jnp.sum(dh2 * n3, axis=0, keepdims=True), first)
        n2, r2 = _rms(mix_ref[...])
        dmix_ref[...] = _rms_bwd(dx2 * g2_ref[...], n2, r2).astype(MM)
        _acc_rows(dg2_ref, jnp.sum(dx2 * n2, axis=0, keepdims=True), first)

    vec = _full_spec((1, D_MODEL))
    return pl.pallas_call(
        body, name="ffn_in_bwd", grid=(SEQ // TM,),
        out_shape=(jax.ShapeDtypeStruct((SEQ, D_MODEL), F32), jax.ShapeDtypeStruct((SEQ, D_MODEL), MM),
                   jax.ShapeDtypeStruct((1, D_MODEL), F32), jax.ShapeDtypeStruct((1, D_MODEL), F32)),
        in_specs=[_row_tile_spec(2 * D_FF), _weight_spec(w_up_g.shape), _row_tile_spec(D_MODEL),
                  _row_tile_spec(D_MODEL), _row_tile_spec(D_MODEL), vec, vec],
        out_specs=[_row_tile_spec(D_MODEL), _row_tile_spec(D_MODEL), vec, vec],
        compiler_params=_params(("arbitrary",)),
    )(dgu, w_up_g, x2, mix, dy, g3, g2)


def merge_bwd(dmix, w_out_g, gc, ga, co, ao, w_cb_g, w_ab_g):
    def body(dmix_ref, wout_ref, gc_ref, ga_ref, co_ref, ao_ref, wcb_ref, wab_ref,
             dco_ref, dao_ref, dg_ref, du3_ref, datt_ref, dbcb_ref):
        dm = _dot_nt(dmix_ref[...], wout_ref[...])
        sgc = _sigmoid(gc_ref[...])
        sga = _sigmoid(ga_ref[...])
        dco = dm * sgc
        dao = dm * sga
        dg_ref[:, 0:D_MODEL] = (dm * co_ref[...].astype(F32) * (sgc * (1.0 - sgc))).astype(MM)
        dg_ref[:, D_MODEL:2 * D_MODEL] = (dm * ao_ref[...].astype(F32) * (sga * (1.0 - sga))).astype(MM)
        _acc_rows(dbcb_ref, jnp.sum(dco, axis=0, keepdims=True), pl.program_id(0) == 0)
        dco_ref[...] = dco.astype(MM)
        dao_ref[...] = dao.astype(MM)
        du3 = None
        datt = None
        for j in range(N_CHIPS):
            cols = slice(j * BR_SHARD, (j + 1) * BR_SHARD)
            t = _dot_nt(dco_ref[:, cols], wcb_ref[j])
            s = _dot_nt(dao_ref[:, cols], wab_ref[j])
            du3 = t if du3 is None else du3 + t
            datt = s if datt is None else datt + s
        du3_ref[...] = du3
        datt_ref[...] = datt.astype(MM)

    wide = _row_tile_spec(D_MODEL)
    return pl.pallas_call(
        body, name="merge_bwd", grid=(SEQ // TM,),
        out_shape=(jax.ShapeDtypeStruct((SEQ, D_MODEL), MM), jax.ShapeDtypeStruct((SEQ, D_MODEL), MM),
                   jax.ShapeDtypeStruct((SEQ, 2 * D_MODEL), MM),
                   jax.ShapeDtypeStruct((SEQ, CONV_DIM), F32), jax.ShapeDtypeStruct((SEQ, ATT_DIM), MM),
                   jax.ShapeDtypeStruct((1, D_MODEL), F32)),
        in_specs=[wide, _weight_spec(w_out_g.shape), wide, wide, wide, wide,
                  _weight_spec(w_cb_g.shape), _weight_spec(w_ab_g.shape)],
        out_specs=[wide, wide, _row_tile_spec(2 * D_MODEL), _row_tile_spec(CONV_DIM), _row_tile_spec(ATT_DIM),
                   _full_spec((1, D_MODEL))],
        compiler_params=_params(("arbitrary",)),
    )(dmix, w_out_g, gc, ga, co, ao, w_cb_g, w_ab_g)


def conv_bwd(du3, u1, ci, w_dw, ln_g, ln_b, after):
    def body(du3_ref, u1_ref, ci_ref, w_ref, g_ref, bb_ref, after_ref,
             dci_ref, dw_ref, dbdw_ref, dg_ref, db_ref, upad_ref, dpad_ref, dwacc_ref, vacc_ref):
        _glu_into(ci_ref, upad_ref)
        dpad_ref[SEQ:SEQ + 32, :] = jnp.zeros((32, CONV_DIM), F32)
        dwacc_ref[...] = jnp.zeros_like(dwacc_ref)
        vacc_ref[...] = jnp.zeros_like(vacc_ref)

        def fold8(t):
            s = t[0:8, :]
            for r in range(1, CONV_TILE // 8):
                s = s + t[8 * r:8 * r + 8, :]
            return s

        def pass1(i, c):
            t0 = pl.multiple_of(i * CONV_TILE, CONV_TILE)
            xh, rstd = _layernorm_parts(u1_ref[pl.ds(t0, CONV_TILE), :])
            gv = g_ref[...]
            u2 = xh * gv + bb_ref[...]
            s2 = _sigmoid(u2)
            du2 = du3_ref[pl.ds(t0, CONV_TILE), :] * (s2 * (1.0 + u2 * (1.0 - s2)))
            wv = du2 * gv
            du1 = rstd * (wv - jnp.mean(wv, axis=-1, keepdims=True)
                          - xh * jnp.mean(wv * xh, axis=-1, keepdims=True))
            dpad_ref[pl.ds(t0, CONV_TILE), :] = du1
            vacc_ref[0] += fold8(du2 * xh)
            vacc_ref[1] += fold8(du2)
            vacc_ref[2] += fold8(du1)
            win = upad_ref[pl.ds(t0, CONV_WIN), :]
            n = win.shape[0]
            for rot in range(8):
                shifted = win if rot == 0 else pltpu.roll(win, n - rot, 0)
                for a in range(5):
                    j = 8 * a + rot - 2
                    if 0 <= j < CONV_WIDTH:
                        dwacc_ref[j] += fold8(du1 * shifted[8 * a:8 * a + CONV_TILE, :])
            return c

        lax.fori_loop(0, SEQ // CONV_TILE, pass1, 0)

        def pass2(i, c):
            t0 = pl.multiple_of(i * CONV_TILE, CONV_TILE)
            win = dpad_ref[pl.ds(t0, CONV_WIN), :]
            du0 = _shifted_sum(win, [(30 - j, w_ref[j:j + 1, :]) for j in range(CONV_WIDTH)])
            a = ci_ref[pl.ds(t0, CONV_TILE), 0:CONV_DIM]
            sb = _sigmoid(ci_ref[pl.ds(t0, CONV_TILE), CONV_DIM:2 * CONV_DIM])
            dci_ref[pl.ds(t0, CONV_TILE), 0:CONV_DIM] = (du0 * sb).astype(MM)
            dci_ref[pl.ds(t0, CONV_TILE), CONV_DIM:2 * CONV_DIM] = (du0 * a * (sb * (1.0 - sb))).astype(MM)
            return c

        lax.fori_loop(0, SEQ // CONV_TILE, pass2, 0)

        for j in range(CONV_WIDTH):
            dw_ref[j:j + 1, :] = jnp.sum(dwacc_ref[j], axis=0, keepdims=True)
        dw_ref[CONV_WIDTH:32, :] = jnp.zeros((32 - CONV_WIDTH, CONV_DIM), F32)
        dg_ref[...] = jnp.sum(vacc_ref[0], axis=0, keepdims=True)
        db_ref[...] = jnp.sum(vacc_ref[1], axis=0, keepdims=True)
        dbdw_ref[...] = jnp.sum(vacc_ref[2], axis=0, keepdims=True)

    vec = jax.ShapeDtypeStruct((1, CONV_DIM), F32)
    return pl.pallas_call(
        body, name="conv_bwd",
        out_shape=(jax.ShapeDtypeStruct((SEQ, 2 * CONV_DIM), MM), jax.ShapeDtypeStruct((32, CONV_DIM), F32),
                   vec, vec, vec),
        in_specs=[VMEM_SPEC] * 7, out_specs=[VMEM_SPEC] * 5,
        scratch_shapes=[pltpu.VMEM((SEQ + 32, CONV_DIM), F32), pltpu.VMEM((SEQ + 32, CONV_DIM), F32),
                        pltpu.VMEM((CONV_WIDTH, 8, CONV_DIM), F32), pltpu.VMEM((3, 8, CONV_DIM), F32)],
        compiler_params=_params(),
    )(du3, u1, ci, w_dw, ln_g, ln_b, after)


def attn_bwd(q, k, v, datt, rc, after):
    def body(q_ref, k_ref, v_ref, do_ref, rc_ref, after_ref, dqkv_ref, dqa_ref, dka_ref, dva_ref, pc_ref, z_ref,
             sig1_ref, sig2_ref, g_ref, spb_ref, gb_ref, ar_ref, dzr_ref, dzc_ref, qm_ref, km_ref, dom_ref):
        lane, row, _ = _head_masks()
        _split_heads(q_ref, qm_ref)
        _split_heads(k_ref, km_ref)
        _split_heads(do_ref, dom_ref)
        for ref in (dqa_ref, dka_ref, dva_ref, pc_ref):
            ref[...] = jnp.zeros_like(ref)
        z_ref[...] = jnp.full(z_ref.shape, NO_SCORE, F32)
        for ref in (sig1_ref, sig2_ref, spb_ref, ar_ref, g_ref, gb_ref, dzr_ref, dzc_ref):
            ref[...] = jnp.zeros_like(ref)
        w_suffix = _cumsum_weights(suffix=True, with_total=False)
        w_prefix = _cumsum_weights(suffix=False, with_total=True)

        def step(pairs):
            (ia, ja), (ib, jb), (ic, jc), (id_, jd) = pairs
            ka, qb_, kb_, kc, qd, kd = (pl.multiple_of(jnp.maximum(b, 0) * TQ, TQ) for b in (ja, ib, jb, jc, id_, jd))
            qa2, qb2, qc2, qd2, kd2 = (pl.multiple_of(jnp.maximum(b, 0) * 2 * TQ, 2 * TQ)
                                       for b in (ia, ib, ic, id_, jd))
            bias_a = _score_bias(lane, row, ia, ja)
            rc_rows = rc_ref[pl.ds(qb_, TQ), :]
            first_c = jc == 0
            for p in range(N_PAIRS):
                cols = slice(128 * p, 128 * (p + 1))
                k_a = k_ref[pl.ds(ka, TQ), cols]
                v_b = v_ref[pl.ds(kb_, TQ), cols]
                dqa_ref[pl.ds(qd, TQ), cols] += _dot(dzc_ref[p], km_ref[p, pl.ds(kd2, 2 * TQ), :])
                dka_ref[pl.ds(kd, TQ), cols] += _dot_tn(dzr_ref[p], qm_ref[p, pl.ds(qd2, 2 * TQ), :])
                dva_ref[pl.ds(kc, TQ), cols] += _dot_tn(ar_ref[p], dom_ref[p, pl.ds(qc2, 2 * TQ), :])
                for h in range(2):
                    hh = 2 * p + h
                    rows = slice(TQ * h, TQ * (h + 1))
                    r = _dot(gb_ref[hh], w_prefix)
                    p_in = jnp.where(first_c, 0.0, pc_ref[hh])
                    dz = (g_ref[hh] - sig2_ref[hh] * (r[:, :128] + p_in)).astype(MM)
                    dzc_ref[p, :, rows] = dz
                    dzr_ref[p, rows, :] = dz
                    pc_ref[hh] = p_in + r[:, 128:]
                    r_in = jnp.sum(jnp.where(lane == 16 * hh + jb, rc_rows, 0.0), axis=1, keepdims=True)
                    a = jnp.exp(z_ref[hh] - (_dot(spb_ref[hh], w_suffix) + r_in))
                    g = _dot_nt(dom_ref[p, pl.ds(qb2 + TQ * h, TQ), :], v_b) * a
                    ar_ref[p, rows, :] = a.astype(MM)
                    g_ref[hh] = g
                    gb_ref[hh] = g.astype(MM)
                    sig2_ref[hh] = sig1_ref[hh]
                    z = _dot_nt(qm_ref[p, pl.ds(qa2 + TQ * h, TQ), :], k_a) + bias_a
                    sp = _softplus(z)
                    sig1_ref[hh] = jnp.exp(z - sp)
                    z_ref[hh] = z
                    spb_ref[hh] = sp.astype(MM)

        _block_pipeline(4, False, step)
        dqkv_ref[:, 0:ATT_DIM] = (dqa_ref[...] * ATT_SCALE).astype(MM)
        dqkv_ref[:, ATT_DIM:2 * ATT_DIM] = dka_ref[...].astype(MM)
        dqkv_ref[:, 2 * ATT_DIM:3 * ATT_DIM] = dva_ref[...].astype(MM)

    split = pltpu.VMEM((N_PAIRS, 2 * SEQ, 128), MM)
    return pl.pallas_call(
        body, name="attn_bwd", out_shape=jax.ShapeDtypeStruct((SEQ, 3 * ATT_DIM), MM),
        in_specs=[VMEM_SPEC] * 6, out_specs=VMEM_SPEC,
        scratch_shapes=[pltpu.VMEM((SEQ, ATT_DIM), F32)] * 3 + [pltpu.VMEM((8, TQ, 128), F32)] * 5
                       + [pltpu.VMEM((8, TQ, 128), MM)] * 2
                       + [pltpu.VMEM((N_PAIRS, 2 * TQ, 128), MM)] * 2 + [pltpu.VMEM((N_PAIRS, TQ, 256), MM)]
                       + [split] * 3,
        compiler_params=_params(),
    )(q, k, v, datt, rc, after)


DPROJ_PIECES = ((0, 1024), (1024, 2560), (2560, 4608))


def _dproj_segments(j):
    g0, g1 = j * IN_SHARD, (j + 1) * IN_SHARD
    segs = []
    for p, (s, e) in enumerate(DPROJ_PIECES):
        lo, hi = max(s, g0), min(e, g1)
        if lo < hi:
            segs.append((p, lo - s, lo - g0, hi - lo))
    return segs


def in_proj_bwd(pieces, w_in_g, x, dx2, g1, after):
    def body(p0_ref, p1_ref, p2_ref, w_ref, x_ref, dx2_ref, g_ref, after_ref, dx_ref, dg_ref):
        p_refs = (p0_ref, p1_ref, p2_ref)
        dh = None
        for j in range(N_CHIPS):
            for p, lo, off, width in _dproj_segments(j):
                t = _dot_nt(p_refs[p][:, lo:lo + width], w_ref[j, :, off:off + width])
                dh = t if dh is None else dh + t
        n1, r1 = _rms(x_ref[...])
        dx_ref[...] = dx2_ref[...] + _rms_bwd(dh * g_ref[...], n1, r1)
        _acc_rows(dg_ref, jnp.sum(dh * n1, axis=0, keepdims=True), pl.program_id(0) == 0)

    vec = _full_spec((1, D_MODEL))
    return pl.pallas_call(
        body, name="in_proj_bwd", grid=(SEQ // TM,),
        out_shape=[jax.ShapeDtypeStruct((SEQ, D_MODEL), F32), jax.ShapeDtypeStruct((1, D_MODEL), F32)],
        in_specs=[_row_tile_spec(p.shape[1]) for p in pieces]
                 + [_weight_spec(w_in_g.shape), _row_tile_spec(D_MODEL), _row_tile_spec(D_MODEL), vec, TOKEN_SPEC],
        out_specs=[_row_tile_spec(D_MODEL), vec],
        compiler_params=_params(("arbitrary",)),
    )(*pieces, w_in_g, x, dx2, g1, after)


def weight_grad_in(h1, pieces):
    kh = D_MODEL // 2

    def body(a_ref, p0_ref, p1_ref, p2_ref, o_ref):
        p_refs = (p0_ref, p1_ref, p2_ref)
        a = a_ref[...]
        for j in range(N_CHIPS):
            @pl.when(pl.program_id(1) == j)
            def _():
                for p, lo, off, width in _dproj_segments(j):
                    o_ref[0, 0, :, off:off + width] = _dot_tn(a, p_refs[p][:, lo:lo + width]).astype(MM)

    return pl.pallas_call(
        body, name="dw_in", grid=(2, N_CHIPS), out_shape=jax.ShapeDtypeStruct((N_CHIPS, 2, kh, IN_SHARD), MM),
        in_specs=[pl.BlockSpec((SEQ, kh), lambda h, j: (0, h))] + [_weight_spec(p.shape) for p in pieces],
        out_specs=pl.BlockSpec((1, 1, kh, IN_SHARD), lambda h, j: (j, h, 0, 0)),
        compiler_params=_params(("arbitrary", "arbitrary")),
    )(h1, *pieces)


def weight_grad(a, b, name, col_sharded, tk=None):
    kin, n = a.shape[1], b.shape[1]

    def body(a_ref, b_ref, o_ref):
        if col_sharded:
            o_ref[0, 0] = _dot_tn(a_ref[...], b_ref[...]).astype(MM)
        else:
            o_ref[...] = _dot_tn(a_ref[...], b_ref[...]).astype(MM)

    if col_sharded:
        kh, ns = kin // 2, n // N_CHIPS
        out = jax.ShapeDtypeStruct((N_CHIPS, 2, kh, ns), MM)
        grid = (2, N_CHIPS)
        in_specs = [pl.BlockSpec((SEQ, kh), lambda h, j: (0, h)), pl.BlockSpec((SEQ, ns), lambda h, j: (0, j))]
        out_spec = pl.BlockSpec((1, 1, kh, ns), lambda h, j: (j, h, 0, 0))
        sem = ("arbitrary", "arbitrary")
    else:
        out = jax.ShapeDtypeStruct((kin, n), MM)
        grid = (kin // tk,)
        in_specs = [pl.BlockSpec((SEQ, tk), lambda r: (0, r)), pl.BlockSpec((SEQ, n), lambda r: (0, 0))]
        out_spec = pl.BlockSpec((tk, n), lambda r: (r, 0))
        sem = ("arbitrary",)
    res = pl.pallas_call(
        body, name=name, grid=grid, out_shape=out, in_specs=in_specs, out_specs=out_spec,
        compiler_params=_params(sem),
    )(a, b)
    if not col_sharded:
        res = res.reshape(N_CHIPS, 2, kin // (2 * N_CHIPS), n)
    return res


def _place():
    x, y, c = lax.axis_index("x"), lax.axis_index("y"), lax.axis_index("c")
    chips = [(1 - x, y), (x, 1 - y), (1 - x, 1 - y)]
    return x, y, c, chips


def _rcopy(src, dst, send_sem, recv_sem, dev):
    return pltpu.make_async_remote_copy(src_ref=src, dst_ref=dst, send_sem=send_sem, recv_sem=recv_sem,
                                        device_id=dev, device_id_type=MESH)


class _Gather:
    def __init__(self, shapes, w, o, scratch):
        self.n, self.shapes, self.w, self.o = len(w), shapes, w, o
        self.send, self.recv, self.fsend, self.frecv, self.loc_in, self.loc_out = scratch[:6]
        self.raw, self.stage = scratch[6:6 + self.n], scratch[6 + self.n:]
        self.x, self.y, self.c, self.chips = _place()
        self.me = 2 * self.x + self.y
        self.sib = (self.x, self.y, 1 - self.c)
        self.pairs = [(j, t) for j in range(3) for t in range(self.n)]

    @staticmethod
    def scratch(shards):
        n = len(shards)
        sems = pltpu.SemaphoreType.DMA
        return ([sems((3 * n,)), sems((3 * n,)), sems((3 * n,)), sems((3 * n,)), sems((n,)), sems((n,))]
                + [pltpu.VMEM(s.shape, s.dtype) for s in shards] + [pltpu.VMEM(s.shape, MM) for s in shards])

    @staticmethod
    def out_shapes(shards):
        return [jax.ShapeDtypeStruct((N_CHIPS,) + s.shape, MM) for s in shards]

    def _half(self, t, k, cc):
        rh = self.shapes[t][0] // 2
        return self.o[t].at[k, pl.ds(cc * rh, rh), :]

    def _chip(self, j):
        cx, cy = self.chips[j]
        return 2 * cx + cy, (cx, cy, self.c)

    def local_in(self, t):
        return pltpu.make_async_copy(self.w[t], self.raw[t], self.loc_in.at[t])

    def local_out(self, t):
        return pltpu.make_async_copy(self.stage[t], self.o[t].at[self.me], self.loc_out.at[t])

    def first(self, j, t):
        rh = self.shapes[t][0] // 2
        i = j * self.n + t
        return _rcopy(self.stage[t].at[pl.ds(self.c * rh, rh), :], self._half(t, self.me, self.c),
                      self.send.at[i], self.recv.at[i], self._chip(j)[1])

    def arrived(self, j, t):
        k, dev = self._chip(j)
        i = j * self.n + t
        blk = self._half(t, k, self.c)
        return _rcopy(blk, blk, self.send.at[i], self.recv.at[i], dev)

    def passed(self, j, t, cc):
        i = j * self.n + t
        blk = self._half(t, self._chip(j)[0], cc)
        return _rcopy(blk, blk, self.fsend.at[i], self.frecv.at[i], self.sib)

    def start(self):
        for t in range(self.n):
            self.local_in(t).start()
        for t in range(self.n):
            self.local_in(t).wait()
            self.stage[t][...] = self.raw[t][...].astype(MM)
            self.local_out(t).start()
        for j, t in self.pairs:
            self.first(j, t).start()

    def forward(self):
        for j, t in self.pairs:
            self.arrived(j, t).wait_recv()
            self.passed(j, t, self.c).start()

    def finish(self):
        for j, t in self.pairs:
            self.passed(j, t, 1 - self.c).wait_recv()
        for j, t in self.pairs:
            self.first(j, t).wait_send()
            self.passed(j, t, self.c).wait_send()
        for t in range(self.n):
            self.local_out(t).wait()


def all_gather_weights(shards, small, later):
    n, m = len(shards), len(later)
    shapes = [s.shape for s in shards]

    def body(*refs):
        w = refs[:n]
        sm = refs[n]
        lw = refs[n + 1:n + 1 + m]
        o = refs[n + 1 + m:2 * n + 1 + m]
        osm = refs[2 * n + 1 + m]
        lo = refs[2 * n + 2 + m:2 * n + 2 + 2 * m]
        scratch = refs[2 * n + 2 + 2 * m:]
        ssend, srecv, sloc, lsem_in, lsem_out = scratch[:5]
        lraw, lstage = scratch[5:5 + m], scratch[5 + m:5 + 2 * m]
        g = _Gather(shapes, w, o, scratch[5 + 2 * m:])
        own = pltpu.make_async_copy(sm, osm.at[g.me], sloc)
        own.start()
        loads = [pltpu.make_async_copy(lw[t], lraw[t], lsem_in.at[t]) for t in range(m)]
        for cp in loads:
            cp.start()
        g.start()
        small_cps = [_rcopy(sm, osm.at[g.me], ssend.at[j], srecv.at[j], g._chip(j)[1]) for j in range(3)]
        for cp in small_cps:
            cp.start()
        places = []
        for t in range(m):
            loads[t].wait()
            lstage[t][...] = lraw[t][...].astype(MM)
            places.append(pltpu.make_async_copy(lstage[t], lo[t].at[g.me], lsem_out.at[t]))
            places[t].start()
        g.forward()
        g.finish()
        for j in range(3):
            k, dev = g._chip(j)
            _rcopy(sm, osm.at[k], ssend.at[j], srecv.at[j], dev).wait_recv()
            small_cps[j].wait_send()
        own.wait()
        for cp in places:
            cp.wait()

    out_shape = _Gather.out_shapes(shards)
    out_shape.append(jax.ShapeDtypeStruct((N_CHIPS,) + small.shape, small.dtype))
    out_shape += _Gather.out_shapes(later)
    sems = pltpu.SemaphoreType.DMA
    return pl.pallas_call(
        body, name="all_gather_weights", out_shape=out_shape,
        in_specs=[ANY] * (n + 1 + m), out_specs=[ANY] * (n + 1 + m),
        scratch_shapes=[sems((3,)), sems((3,)), sems, sems((m,)), sems((m,))]
                       + [pltpu.VMEM(s.shape, s.dtype) for s in later] + [pltpu.VMEM(s.shape, MM) for s in later]
                       + _Gather.scratch(shards),
        compiler_params=_params(),
    )(*shards, small, *later)


def sibling_exchange(grads, name):
    n = len(grads)

    def body(*refs):
        g = refs[:n]
        o = refs[n:2 * n]
        send, recv = refs[2 * n:]
        x, y, c, _ = _place()
        cps = [_rcopy(g[t].at[:, 1 - c], o[t], send.at[t], recv.at[t], (x, y, 1 - c)) for t in range(n)]
        for cp in cps:
            cp.start()
        for cp in cps:
            cp.wait()

    sems = pltpu.SemaphoreType.DMA
    return pl.pallas_call(
        body, name=name,
        out_shape=[jax.ShapeDtypeStruct((a.shape[0],) + a.shape[2:], a.dtype) for a in grads],
        in_specs=[ANY] * n, out_specs=[ANY] * n, scratch_shapes=[sems((n,)), sems((n,))],
    )(*grads)


HBM_SPEC = pl.BlockSpec(memory_space=pltpu.HBM)
SEM_SPEC = pl.BlockSpec(memory_space=pltpu.SEMAPHORE)
DATAFLOW = pltpu.SideEffectType.DATAFLOW_SIDE_EFFECTING


def split_start(name, bufs, n_copies, copies):
    nb = len(bufs)

    def body(*refs):
        for cp in copies(refs[:nb], refs[nb], refs[nb + 1]):
            cp.start()
        token = refs[2 * nb + 2]
        token[...] = jnp.zeros_like(token)

    sems = [pltpu.SemaphoreType.DMA((n_copies,))] * 2
    res = pl.pallas_call(
        body, name=name,
        out_shape=sems + [pltpu.HBM(a.shape, a.dtype) for a in bufs] + [jax.ShapeDtypeStruct((8, 128), F32)],
        in_specs=[HBM_SPEC] * nb, out_specs=[SEM_SPEC] * 2 + [HBM_SPEC] * nb + [VMEM_SPEC],
        input_output_aliases={i: 2 + i for i in range(nb)},
        compiler_params=pltpu.CompilerParams(has_side_effects=DATAFLOW),
    )(*[pltpu.with_memory_space_constraint(a, pltpu.HBM) for a in bufs])
    return res[:-1], res[-1]


def split_wait(name, state, after, copies):
    sems, bufs = state[:2], state[2:]
    nb = len(bufs)

    def body(*refs):
        for cp in copies(refs[:nb], refs[nb], refs[nb + 1]):
            cp.wait_send()
            cp.wait_recv()

    return pl.pallas_call(
        body, name=name, out_shape=[pltpu.HBM(a.shape, a.dtype) for a in bufs],
        in_specs=[HBM_SPEC] * nb + [SEM_SPEC] * 2 + [ANY] * len(after), out_specs=[HBM_SPEC] * nb,
        input_output_aliases={i: i for i in range(nb)},
        compiler_params=pltpu.CompilerParams(has_side_effects=DATAFLOW),
    )(*bufs, *sems, *after)


def _scatter_copies(n):
    def copies(refs, send, recv):
        _, _, c, chips = _place()
        return [_rcopy(refs[t].at[2 * cx + cy], refs[n + t].at[j], send.at[3 * t + j], recv.at[3 * t + j], (cx, cy, c))
                for t in range(n) for j, (cx, cy) in enumerate(chips)]
    return copies


def scatter_start(parts, tag):
    lands = [lax.empty((3,) + p.shape[1:], p.dtype) for p in parts]
    return split_start("scatter_start_" + tag, list(parts) + lands, 3 * len(parts), _scatter_copies(len(parts)))


def scatter_wait(state, after, tag):
    n = (len(state) - 2) // 2
    return split_wait("scatter_wait_" + tag, state, after, _scatter_copies(n))[n:]


def _gather_copies(shapes, level):
    n = len(shapes)

    def copies(refs, send, recv):
        x, y, c, chips = _place()
        out = []
        for t in range(n):
            rh = shapes[t][0] // 2
            for j, (cx, cy) in enumerate(chips):
                k, dev = (2 * x + y, (cx, cy, c)) if level == 1 else (2 * cx + cy, (x, y, 1 - c))
                blk = refs[t].at[k, pl.ds(c * rh, rh), :]
                out.append(_rcopy(blk, blk, send.at[3 * t + j], recv.at[3 * t + j], dev))
        return out
    return copies


def sibling_swap(halves, name):
    n = len(halves)

    def body(*refs):
        h = refs[:n]
        o = refs[n:2 * n]
        send, recv = refs[2 * n:]
        x, y, c, _ = _place()
        cps = [_rcopy(h[t], o[t], send.at[t], recv.at[t], (x, y, 1 - c)) for t in range(n)]
        for cp in cps:
            cp.start()
        for cp in cps:
            cp.wait()

    sems = pltpu.SemaphoreType.DMA
    return pl.pallas_call(
        body, name=name, out_shape=[jax.ShapeDtypeStruct(a.shape, a.dtype) for a in halves],
        in_specs=[ANY] * n, out_specs=[ANY] * n, scratch_shapes=[sems((n,)), sems((n,))],
    )(*halves)


def small_all_reduce(ddw, v512, v1024, loss_parts):
    rows, width = PACK_ROWS, 512
    n512, n1024 = len(VEC512), len(VEC1024)

    def body(*refs):
        ddw_ref = refs[0]
        a_refs = refs[1:1 + n512]
        b_refs = refs[1 + n512:1 + n512 + n1024]
        lp_ref, o_ref, p_ref, gath_ref, send, recv = refs[1 + n512 + n1024:]
        p_ref[...] = jnp.zeros_like(p_ref)
        p_ref[0:32, :] = ddw_ref[...]
        p_ref[LOSS_ROW:LOSS_ROW + 1, 0:128] = jnp.sum(lp_ref[...], axis=0, keepdims=True) * 0.125
        for i, r in enumerate(a_refs):
            p_ref[32 + i:33 + i, :] = r[...]
        for i, r in enumerate(b_refs):
            base = 32 + n512 + 2 * i
            p_ref[base:base + 1, :] = r[:, 0:512]
            p_ref[base + 1:base + 2, :] = r[:, 512:1024]
        x, y, c, _ = _place()
        me = 4 * x + 2 * y + c
        gath_ref[me] = p_ref[...]
        cps = []
        for k in range(1, 8):
            dx, dy, dc = (k >> 2) & 1, (k >> 1) & 1, k & 1
            px = 1 - x if dx else x
            py = 1 - y if dy else y
            pc = 1 - c if dc else c
            cps.append(_rcopy(p_ref, gath_ref.at[me], send.at[k - 1], recv.at[k - 1], (px, py, pc)))
        for cp in cps:
            cp.start()
        for k in range(1, 8):
            dx, dy, dc = (k >> 2) & 1, (k >> 1) & 1, k & 1
            px = 1 - x if dx else x
            py = 1 - y if dy else y
            pc = 1 - c if dc else c
            _rcopy(p_ref, gath_ref.at[4 * px + 2 * py + pc], send.at[k - 1], recv.at[k - 1], (px, py, pc)).wait_recv()
        for cp in cps:
            cp.wait_send()
        total = gath_ref[0]
        for d in range(1, 8):
            total = total + gath_ref[d]
        o_ref[...] = total

    sems = pltpu.SemaphoreType.DMA
    n_in = 2 + n512 + n1024
    return pl.pallas_call(
        body, name="small_all_reduce", out_shape=jax.ShapeDtypeStruct((rows, width), F32),
        in_specs=[VMEM_SPEC] * n_in, out_specs=VMEM_SPEC,
        scratch_shapes=[pltpu.VMEM((rows, width), F32), pltpu.VMEM((8, rows, width), F32), sems((7,)), sems((7,))],
    )(ddw, *[v512[n] for n in VEC512], *[v1024[n] for n in VEC1024], loss_parts)


def _row_block(r):
    for tr in (512, 352, 256, 128):
        if r % tr == 0:
            return tr
    return r


def add_halves(g, recv, name):
    _, _, r, w = g.shape
    tr = _row_block(r)

    def body(g0_ref, g1_ref, r_ref, ob_ref, own_ref):
        k = pl.program_id(1)
        c = lax.axis_index("c")
        me = 2 * lax.axis_index("x") + lax.axis_index("y")
        t = jnp.where(c == 0, g0_ref[0, 0], g1_ref[0, 0]).astype(F32) + r_ref[0].astype(F32)
        ob_ref[0] = t.astype(MM)
        mine = jnp.where(k == me, t, 0.0)

        @pl.when(k == 0)
        def _():
            own_ref[...] = mine

        @pl.when(k != 0)
        def _():
            own_ref[...] += mine

    return pl.pallas_call(
        body, name=name, grid=(r // tr, N_CHIPS),
        in_specs=[pl.BlockSpec((1, 1, tr, w), lambda i, k: (k, 0, i, 0)),
                  pl.BlockSpec((1, 1, tr, w), lambda i, k: (k, 1, i, 0)),
                  pl.BlockSpec((1, tr, w), lambda i, k: (k, i, 0))],
        out_specs=[pl.BlockSpec((1, tr, w), lambda i, k: (k, i, 0)),
                   pl.BlockSpec((tr, w), lambda i, k: (i, 0))],
        out_shape=(jax.ShapeDtypeStruct((N_CHIPS, r, w), MM), jax.ShapeDtypeStruct((r, w), F32)),
        compiler_params=_params(("arbitrary", "arbitrary")),
    )(g, g, recv)


def sum_parts(own, rin, after, name):
    _, r, w = rin.shape
    tr = _row_block(r)

    def body(o_ref, r_ref, after_ref, out_ref):
        out_ref[...] = ((o_ref[...] + r_ref[0].astype(F32)) + r_ref[1].astype(F32)) + r_ref[2].astype(F32)

    return pl.pallas_call(
        body, name=name, grid=(r // tr,), out_shape=jax.ShapeDtypeStruct((r, w), F32),
        in_specs=[pl.BlockSpec((tr, w), lambda i: (i, 0)), pl.BlockSpec((3, tr, w), lambda i: (0, i, 0)),
                  _full_spec((8, 128))],
        out_specs=pl.BlockSpec((tr, w), lambda i: (i, 0)),
        compiler_params=_params(("arbitrary",)),
    )(own, rin, after)


def _adamw_math(w, g, m, v):
    mn = ADAM_B1 * m + (1.0 - ADAM_B1) * g
    vn = ADAM_B2 * v + (1.0 - ADAM_B2) * (g * g)
    m_hat = mn / (1.0 - ADAM_B1 ** ADAM_STEP)
    v_hat = vn / (1.0 - ADAM_B2 ** ADAM_STEP)
    return -ADAM_LR * (m_hat / (jnp.sqrt(v_hat) + ADAM_EPS) + ADAM_WD * w), mn, vn


def adamw(w, mine, other, m, v, name):
    r, c = w.shape
    rh = r // 2
    tr = _row_block(rh)
    if c >= 1024 and tr % 512 == 0:
        tr = 256
    nb = rh // tr

    def body(w_ref, a_ref, b_ref, m_ref, v_ref, go_ref, d_ref, mo_ref, vo_ref):
        gv = jnp.where(lax.axis_index("c") == pl.program_id(0), a_ref[...], b_ref[...])
        go_ref[...] = gv
        d_ref[...], mo_ref[...], vo_ref[...] = _adamw_math(w_ref[...], gv, m_ref[...], v_ref[...])

    spec = pl.BlockSpec((tr, c), lambda h, i: (h * nb + i, 0))
    half = pl.BlockSpec((tr, c), lambda h, i: (i, 0))
    out = jax.ShapeDtypeStruct((r, c), F32)
    return pl.pallas_call(
        body, name=name, grid=(2, nb), out_shape=(out, out, out, out),
        in_specs=[spec, half, half, spec, spec], out_specs=[spec] * 4,
        compiler_params=_params(("arbitrary", "arbitrary")),
    )(w, mine, other, m, v)


def adamw_small(gsum, params):
    names = list(params)
    flat = [a for n in names for a in params[n]]

    def body(*refs):
        g_ref = refs[0]
        ins = refs[1:1 + 3 * len(names)]
        outs = refs[1 + 3 * len(names):]
        me = 2 * lax.axis_index("x") + lax.axis_index("y")
        for i, n in enumerate(names):
            w_ref, m_ref, v_ref = ins[3 * i:3 * i + 3]
            go_ref, d_ref, mo_ref, vo_ref = outs[4 * i:4 * i + 4]
            if n == "conv_dw_w":
                gv = jnp.zeros((CONV_WIDTH, 128), F32)
                for k in range(N_CHIPS):
                    gv = gv + jnp.where(me == k, g_ref[0:CONV_WIDTH, 128 * k:128 * (k + 1)], 0.0)
            elif n in VEC512:
                r0 = 32 + VEC512.index(n)
                gv = g_ref[r0:r0 + 1, :]
            else:
                r0 = 32 + len(VEC512) + 2 * VEC1024.index(n)
                gv = jnp.concatenate([g_ref[r0:r0 + 1, :], g_ref[r0 + 1:r0 + 2, :]], axis=1)
            go_ref[...] = gv
            d_ref[...], mo_ref[...], vo_ref[...] = _adamw_math(w_ref[...], gv, m_ref[...], v_ref[...])

    out_shape = [jax.ShapeDtypeStruct(params[n][0].shape, F32) for n in names for _ in range(4)]
    res = pl.pallas_call(
        body, name="adamw_small", out_shape=out_shape,
        in_specs=[VMEM_SPEC] * (1 + len(flat)), out_specs=[VMEM_SPEC] * len(out_shape),
        compiler_params=_params(),
    )(gsum, *flat)
    return {n: res[4 * i:4 * i + 4] for i, n in enumerate(names)}


REST = ("w_ffn_up", "w_ffn_down", "w_out", "w_conv_branch", "w_att_branch")
VEC512 = ("conv_dw_b", "conv_ln_g", "conv_ln_b")
VEC1024 = ("norm_mix_pre", "b_conv_branch", "norm_mix_post", "norm_ffn_pre", "norm_ffn_post")
PACK_ROWS = 48
LOSS_ROW = 47


def kernel(x, norm_mix_pre, w_in, conv_dw_w, conv_dw_b, conv_ln_g, conv_ln_b, w_conv_branch, b_conv_branch, w_att_branch, w_out, norm_mix_post, norm_ffn_pre, w_ffn_up, w_ffn_down, norm_ffn_post, loss_target, m_norm_mix_pre, m_w_in, m_conv_dw_w, m_conv_dw_b, m_conv_ln_g, m_conv_ln_b, m_w_conv_branch, m_b_conv_branch, m_w_att_branch, m_w_out, m_norm_mix_post, m_norm_ffn_pre, m_w_ffn_up, m_w_ffn_down, m_norm_ffn_post, v_norm_mix_pre, v_w_in, v_conv_dw_w, v_conv_dw_b, v_conv_ln_g, v_conv_ln_b, v_w_conv_branch, v_b_conv_branch, v_w_att_branch, v_w_out, v_norm_mix_post, v_norm_ffn_pre, v_w_ffn_up, v_w_ffn_down, v_norm_ffn_post):
    weights = dict(norm_mix_pre=norm_mix_pre, w_in=w_in, conv_dw_w=conv_dw_w, conv_dw_b=conv_dw_b, conv_ln_g=conv_ln_g, conv_ln_b=conv_ln_b, w_conv_branch=w_conv_branch, b_conv_branch=b_conv_branch, w_att_branch=w_att_branch, w_out=w_out, norm_mix_post=norm_mix_post, norm_ffn_pre=norm_ffn_pre, w_ffn_up=w_ffn_up, w_ffn_down=w_ffn_down, norm_ffn_post=norm_ffn_post)
    mom = dict(norm_mix_pre=m_norm_mix_pre, w_in=m_w_in, conv_dw_w=m_conv_dw_w, conv_dw_b=m_conv_dw_b, conv_ln_g=m_conv_ln_g, conv_ln_b=m_conv_ln_b, w_conv_branch=m_w_conv_branch, b_conv_branch=m_b_conv_branch, w_att_branch=m_w_att_branch, w_out=m_w_out, norm_mix_post=m_norm_mix_post, norm_ffn_pre=m_norm_ffn_pre, w_ffn_up=m_w_ffn_up, w_ffn_down=m_w_ffn_down, norm_ffn_post=m_norm_ffn_post)
    var = dict(norm_mix_pre=v_norm_mix_pre, w_in=v_w_in, conv_dw_w=v_conv_dw_w, conv_dw_b=v_conv_dw_b, conv_ln_g=v_conv_ln_g, conv_ln_b=v_conv_ln_b, w_conv_branch=v_w_conv_branch, b_conv_branch=v_b_conv_branch, w_att_branch=v_w_att_branch, w_out=v_w_out, norm_mix_post=v_norm_mix_post, norm_ffn_pre=v_norm_ffn_pre, w_ffn_up=v_w_ffn_up, w_ffn_down=v_w_ffn_down, norm_ffn_post=v_norm_ffn_post)
    order = list(weights)
    grads, deltas, new_m, new_v = {}, {}, {}, {}
    xs = x.reshape(SEQ, D_MODEL)
    tgt = loss_target.reshape(SEQ, D_MODEL)
    row = lambda a: a.reshape(1, -1)
    g1, g2, g3, g4 = (row(weights[n]) for n in ("norm_mix_pre", "norm_mix_post", "norm_ffn_pre", "norm_ffn_post"))
    ln_g, ln_b = row(conv_ln_g), row(conv_ln_b)

    def reduce_prepare(names, partial, tag):
        from_sib = sibling_exchange([partial[n] for n in names], "sibling_exchange_" + tag)
        return [add_halves(partial[n], r, "add_" + n) for n, r in zip(names, from_sib)]

    def reduce_finish(names, summed, from_chips, after, tag):
        halves = [sum_parts(s[1], r, after, "sum_" + n) for n, s, r in zip(names, summed, from_chips)]
        for n, a, b in zip(names, halves, sibling_swap(halves, "sibling_swap_" + tag)):
            grads[n], deltas[n], new_m[n], new_v[n] = adamw(weights[n], a, b, mom[n], var[n], "adamw_" + n)

    w_in_g, dw_g, *rest = all_gather_weights([w_in], conv_dw_w, [weights[n] for n in REST])
    w_dw_full = jnp.concatenate([dw_g[k] for k in range(N_CHIPS)], axis=1)
    rest_shapes = [weights[n].shape for n in REST]
    state, token = split_start("gather_start", rest, 3 * len(REST), _gather_copies(rest_shapes, 1))
    h1, ci, q, k, v, gc, ga = in_proj_fwd(xs, g1, w_in_g, token)
    u1, u3 = conv_fwd(ci, w_dw_full, row(conv_dw_b), ln_g, ln_b)
    att, rc = attn_fwd(q, k, v)
    rest = split_wait("gather_wait", state, [att], _gather_copies(rest_shapes, 1))
    ffn, mix_w = rest[:2], rest[2:]
    state_mix, _ = split_start("pass_mix_start", mix_w, 3 * 3, _gather_copies(rest_shapes[2:], 2))
    state, token = split_start("pass_ffn_start", ffn, 3 * 2, _gather_copies(rest_shapes[:2], 2))
    w_out_g, w_cb_g, w_ab_g = split_wait("pass_mix_wait", state_mix, [token], _gather_copies(rest_shapes[2:], 2))
    w_out_g = w_out_g.reshape(D_MODEL, D_MODEL)
    co, ao, merged, mix, x2, h2 = mix_fwd(u3, att, gc, ga, xs, w_cb_g, row(b_conv_branch), w_ab_g, w_out_g,
                                          g2, g3, token)
    w_up_g, w_down_g = split_wait("pass_ffn_wait", state, [h2], _gather_copies(rest_shapes[:2], 2))
    w_down_g = w_down_g.reshape(D_FF, D_MODEL)
    gate, up, act = ffn_up_fwd(h2, w_up_g)
    dff, dy, loss_parts, dg4 = ffn_down_loss(act, w_down_g, x2, tgt, g4)

    partial = {}
    dgu = ffn_act_bwd(dff, w_down_g, gate, up)
    partial["w_ffn_down"] = weight_grad(act, dff, "dw_ffn_down", False, tk=UP_SHARD)
    dx2, dmix, dg3, dg2 = ffn_in_bwd(dgu, w_up_g, x2, mix, dy, g3, g2)
    partial["w_ffn_up"] = weight_grad(h2, dgu, "dw_ffn_up", True)
    dco, dao, dg, du3, datt, dbcb = merge_bwd(dmix, w_out_g, gc, ga, co, ao, w_cb_g, w_ab_g)
    partial["w_out"] = weight_grad(merged, dmix, "dw_out", False, tk=512)
    partial["w_conv_branch"] = weight_grad(u3, dco, "dw_conv_branch", True)
    partial["w_att_branch"] = weight_grad(att, dao, "dw_att_branch", True)
    summed = reduce_prepare(REST, partial, "rest")
    state, token = scatter_start([s[0] for s in summed], "rest")
    dci, ddw, dbdw, dlng, dlnb = conv_bwd(du3, u1, ci, w_dw_full, ln_g, ln_b, token)
    dqkv = attn_bwd(q, k, v, datt, rc, token)
    from_chips = scatter_wait(state, [dci, dqkv], "rest")
    dproj = (dci, dqkv, dg)
    partial["w_in"] = weight_grad_in(h1, dproj)
    summed_in = reduce_prepare(("w_in",), partial, "w_in")
    state, token = scatter_start([summed_in[0][0]], "w_in")
    grad_x, dg1 = in_proj_bwd(dproj, w_in_g, xs, dx2, g1, token)
    reduce_finish(REST, summed, from_chips, token, "rest")
    from_chips_in = scatter_wait(state, [dg1] + [new_v[n] for n in REST], "w_in")
    reduce_finish(("w_in",), summed_in, from_chips_in, token, "w_in")

    v512 = dict(conv_dw_b=dbdw, conv_ln_g=dlng, conv_ln_b=dlnb)
    v1024 = dict(norm_mix_pre=dg1, b_conv_branch=dbcb, norm_mix_post=dg2, norm_ffn_pre=dg3, norm_ffn_post=dg4)
    gsum = small_all_reduce(ddw, v512, v1024, loss_parts)
    loss = gsum[LOSS_ROW, 0]
    as_rows = lambda n, a: a if n == "conv_dw_w" else a.reshape(1, -1)
    small_names = ("conv_dw_w",) + VEC512 + VEC1024
    small = adamw_small(gsum, {n: tuple(as_rows(n, d[n]) for d in (weights, mom, var)) for n in small_names})
    for n in small_names:
        grads[n], deltas[n], new_m[n], new_v[n] = (a.reshape(weights[n].shape) for a in small[n])

    return (loss, grad_x.reshape(1, SEQ, D_MODEL), *[grads[n] for n in order], *[deltas[n] for n in order],
            *[new_m[n] for n in order], *[new_v[n] for n in order])
```

```python
import jax
import jax.numpy as jnp
from jax import lax
from jax.experimental import pallas as pl
from jax.experimental.pallas import tpu as pltpu

F32 = jnp.float32
MM = jnp.bfloat16

SEQ = 2048
D_MODEL = 1024
CONV_DIM = 512
ATT_DIM = 512
CONV_WIDTH = 31
D_FF = 2816
IN_COLS = 2 * CONV_DIM + 3 * ATT_DIM + 2 * D_MODEL
N_CHIPS = 4
IN_SHARD = IN_COLS // N_CHIPS
UP_SHARD = 2 * D_FF // N_CHIPS
BR_SHARD = D_MODEL // N_CHIPS
EPS = 1e-6
ATT_SCALE = 0.125

TM = 256
GLU_ROWS = 256
TQ = 128
CONV_TILE = 64
CONV_WIN = CONV_TILE + 32
VMEM_LIMIT = 56 * 1024 * 1024

ADAM_LR = 0.001
ADAM_B1 = 0.9
ADAM_B2 = 0.999
ADAM_EPS = 1e-08
ADAM_WD = 0.01
ADAM_STEP = 10

MESH = pl.DeviceIdType.MESH
ANY = pl.BlockSpec(memory_space=pl.ANY)
VMEM_SPEC = pl.BlockSpec(memory_space=pltpu.VMEM)

NT_DIMS = (((1,), (1,)), ((), ()))
TN_DIMS = (((0,), (0,)), ((), ()))

IN_PIECES = (("ci", 0, 1024), ("q", 1024, 1536), ("k", 1536, 2048), ("v", 2048, 2560),
             ("gc", 2560, 3584), ("ga", 3584, 4608))


def _params(sem=None, vmem=VMEM_LIMIT):
    return pltpu.CompilerParams(dimension_semantics=sem, vmem_limit_bytes=vmem)


def _dot(a, b):
    return jnp.dot(a, b, preferred_element_type=F32)


def _dot_nt(a, b):
    return lax.dot_general(a, b, NT_DIMS, preferred_element_type=F32)


def _dot_tn(a, b):
    return lax.dot_general(a, b, TN_DIMS, preferred_element_type=F32)


def _sigmoid(x):
    return 1.0 / (1.0 + jnp.exp(-x))


def _rms(x):
    r = lax.rsqrt(jnp.mean(x * x, axis=-1, keepdims=True) + EPS)
    return x * r, r


def _rms_bwd(dy_g, n, r):
    return r * (dy_g - n * jnp.mean(dy_g * n, axis=-1, keepdims=True))


def _row_tile_spec(width, tm=TM):
    return pl.BlockSpec((tm, width), lambda i: (i, 0))


def _full_spec(shape):
    nd = len(shape)
    return pl.BlockSpec(shape, lambda *_: (0,) * nd)


def _weight_spec(shape):
    nd = len(shape)
    return pl.BlockSpec(shape, lambda *_: (0,) * nd, pipeline_mode=pl.Buffered(1))


def _acc_rows(ref, val, first):
    @pl.when(first)
    def _():
        ref[...] = val

    @pl.when(jnp.logical_not(first))
    def _():
        ref[...] += val


TOKEN_SPEC = pl.BlockSpec((8, 128), lambda *_: (0, 0))


def in_proj_fwd(x, g1, w_in_g, after):
    def body(x_ref, g_ref, w_ref, after_ref, h_ref, ci_ref, q_ref, k_ref, v_ref, gc_ref, ga_ref):
        n, _ = _rms(x_ref[...])
        h = (n * g_ref[...]).astype(MM)
        h_ref[...] = h
        outs = dict(ci=ci_ref, q=q_ref, k=k_ref, v=v_ref, gc=gc_ref, ga=ga_ref)
        for j in range(N_CHIPS):
            p = _dot(h, w_ref[j])
            g0 = j * IN_SHARD
            for name, s, e in IN_PIECES:
                lo, hi = max(s, g0), min(e, g0 + IN_SHARD)
                if lo < hi:
                    ref = outs[name]
                    part = p[:, lo - g0:hi - g0]
                    if name == "q":
                        part = part * ATT_SCALE
                    ref[:, lo - s:hi - s] = part.astype(ref.dtype)

    out_shape = [
        jax.ShapeDtypeStruct((SEQ, D_MODEL), MM),
        jax.ShapeDtypeStruct((SEQ, 2 * CONV_DIM), F32),
        jax.ShapeDtypeStruct((SEQ, ATT_DIM), MM),
        jax.ShapeDtypeStruct((SEQ, ATT_DIM), MM),
        jax.ShapeDtypeStruct((SEQ, ATT_DIM), MM),
        jax.ShapeDtypeStruct((SEQ, D_MODEL), F32),
        jax.ShapeDtypeStruct((SEQ, D_MODEL), F32),
    ]
    return pl.pallas_call(
        body, name="in_proj_fwd", grid=(SEQ // TM,), out_shape=out_shape,
        in_specs=[_row_tile_spec(D_MODEL), _full_spec((1, D_MODEL)), _weight_spec(w_in_g.shape), TOKEN_SPEC],
        out_specs=[_row_tile_spec(s.shape[1]) for s in out_shape],
        compiler_params=_params(("arbitrary",)),
    )(x, g1, w_in_g, after)


def _shifted_sum(win, terms):
    by_rot = {}
    for m, coef in terms:
        by_rot.setdefault(m % 8, []).append((m // 8, coef))
    acc = None
    n = win.shape[0]
    for rot in sorted(by_rot):
        shifted = win if rot == 0 else pltpu.roll(win, n - rot, 0)
        for a, coef in by_rot[rot]:
            t = coef * shifted[8 * a:8 * a + CONV_TILE, :]
            acc = t if acc is None else acc + t
    return acc


def _glu_into(ci_ref, upad_ref):
    upad_ref[0:32, :] = jnp.zeros((32, CONV_DIM), F32)

    def step(i, c):
        t0 = pl.multiple_of(i * GLU_ROWS, GLU_ROWS)
        a = ci_ref[pl.ds(t0, GLU_ROWS), 0:CONV_DIM]
        b = ci_ref[pl.ds(t0, GLU_ROWS), CONV_DIM:2 * CONV_DIM]
        upad_ref[pl.ds(t0 + 32, GLU_ROWS), :] = a * _sigmoid(b)
        return c

    lax.fori_loop(0, SEQ // GLU_ROWS, step, 0)


def _layernorm_parts(u1):
    mu = jnp.mean(u1, axis=-1, keepdims=True)
    xc = u1 - mu
    rstd = lax.rsqrt(jnp.mean(xc * xc, axis=-1, keepdims=True) + EPS)
    return xc * rstd, rstd


def conv_fwd(ci, w_dw, b_dw, ln_g, ln_b):
    def body(ci_ref, w_ref, b_ref, g_ref, bb_ref, u1_ref, u3_ref, upad_ref):
        _glu_into(ci_ref, upad_ref)

        def step(i, c):
            t0 = pl.multiple_of(i * CONV_TILE, CONV_TILE)
            win = upad_ref[pl.ds(t0, CONV_WIN), :]
            u1 = _shifted_sum(win, [(j + 2, w_ref[j:j + 1, :]) for j in range(CONV_WIDTH)]) + b_ref[...]
            u1_ref[pl.ds(t0, CONV_TILE), :] = u1
            xh, _ = _layernorm_parts(u1)
            u2 = xh * g_ref[...] + bb_ref[...]
            u3_ref[pl.ds(t0, CONV_TILE), :] = (u2 * _sigmoid(u2)).astype(MM)
            return c

        lax.fori_loop(0, SEQ // CONV_TILE, step, 0)

    return pl.pallas_call(
        body, name="conv_fwd",
        out_shape=[jax.ShapeDtypeStruct((SEQ, CONV_DIM), F32), jax.ShapeDtypeStruct((SEQ, CONV_DIM), MM)],
        in_specs=[VMEM_SPEC] * 5, out_specs=[VMEM_SPEC] * 2,
        scratch_shapes=[pltpu.VMEM((SEQ + 32, CONV_DIM), F32)],
        compiler_params=_params(),
    )(ci, w_dw, b_dw, ln_g, ln_b)


def _softplus(z):
    return jnp.maximum(z, 0.0) + jnp.log(1.0 + jnp.exp(-jnp.abs(z)))


def _cumsum_weights(suffix, with_total):
    n = 256 if with_total else 128
    r = lax.broadcasted_iota(jnp.int32, (128, n), 0)
    c = lax.broadcasted_iota(jnp.int32, (128, n), 1)
    tri = (r >= c) if suffix else (r <= c)
    return jnp.logical_or(tri, c >= 128).astype(MM)


NO_SCORE = -1e30
N_KB = SEQ // TQ


def _score_bias(lane, row, i, j):
    keep = jnp.logical_and(i >= 0, jnp.logical_or(j < i, lane < row))
    return jnp.where(keep, 0.0, NO_SCORE)


def _block_pipeline(n_stages, descending, step, on_query_block=None):
    n_lag = n_stages - 1
    none = jnp.int32(-1)

    def shift(cur, lag):
        step([cur] + [(lag[2 * s], lag[2 * s + 1]) for s in range(n_lag)])
        return (cur[0], cur[1]) + tuple(lag[:-2])

    def outer(i, lag):
        if on_query_block is not None:
            on_query_block(i)

        def inner(n, lag):
            return shift((i, i - n if descending else n), lag)
        return lax.fori_loop(0, i + 1, inner, lag)

    lag = lax.fori_loop(0, N_KB, outer, (none,) * (2 * n_lag))
    lax.fori_loop(0, n_lag, lambda n, lag: shift((none, none), lag), lag)


def _head_masks():
    lane = lax.broadcasted_iota(jnp.int32, (TQ, 128), 1)
    row = lax.broadcasted_iota(jnp.int32, (TQ, 128), 0)
    return lane, row, lane < 64


def _pick_head(x, head0, h):
    zero = jnp.zeros_like(x)
    return jnp.where(head0, x, zero) if h == 0 else jnp.where(head0, zero, x)


N_PAIRS = ATT_DIM // 128


def _split_heads(src_ref, dst_ref):
    _, _, head0 = _head_masks()

    def block(b, c):
        r0 = pl.multiple_of(b * TQ, TQ)
        d0 = pl.multiple_of(b * 2 * TQ, 2 * TQ)
        for p in range(N_PAIRS):
            x = src_ref[pl.ds(r0, TQ), 128 * p:128 * (p + 1)]
            for h in range(2):
                dst_ref[p, pl.ds(d0 + TQ * h, TQ), :] = _pick_head(x, head0, h)
        return c

    lax.fori_loop(0, N_KB, block, 0)


def attn_fwd(q, k, v):
    def body(q_ref, k_ref, v_ref, o_ref, rc_ref, acc_ref, r_ref, z_ref, spb_ref, ab_ref, qm_ref, vm_ref):
        lane, row, _ = _head_masks()
        w = _cumsum_weights(suffix=True, with_total=True)
        _split_heads(q_ref, qm_ref)
        _split_heads(v_ref, vm_ref)
        acc_ref[...] = jnp.zeros_like(acc_ref)
        r_ref[...] = jnp.zeros_like(r_ref)
        rc_ref[...] = jnp.zeros_like(rc_ref)
        z_ref[...] = jnp.full(z_ref.shape, NO_SCORE, F32)
        spb_ref[...] = jnp.zeros_like(spb_ref)
        ab_ref[...] = jnp.zeros_like(ab_ref)

        def step(pairs):
            (i1, j1), (i2, j2), (i3, j3) = pairs
            k1, q2, q3 = (pl.multiple_of(jnp.maximum(b, 0) * TQ, TQ) for b in (j1, i2, i3))
            q1, k3 = (pl.multiple_of(jnp.maximum(b, 0) * 2 * TQ, 2 * TQ) for b in (i1, j3))
            bias1 = _score_bias(lane, row, i1, j1)
            first2 = j2 == i2
            rc_rows = rc_ref[pl.ds(q2, TQ), :]
            for p in range(N_PAIRS):
                cols = slice(128 * p, 128 * (p + 1))
                kb = k_ref[pl.ds(k1, TQ), cols]
                acc_ref[pl.ds(q3, TQ), cols] += _dot(ab_ref[p], vm_ref[p, pl.ds(k3, 2 * TQ), :])
                for h in range(2):
                    hh = 2 * p + h
                    r = _dot(spb_ref[hh], w)
                    r_in = jnp.where(first2, 0.0, r_ref[hh])
                    ab_ref[p, :, 128 * h:128 * (h + 1)] = jnp.exp(z_ref[hh] - (r[:, :128] + r_in)).astype(MM)
                    rc_rows = jnp.where(jnp.logical_and(lane == 16 * hh + j2, i2 >= 0), r_in, rc_rows)
                    r_ref[hh] = r_in + r[:, 128:]
                    z = _dot_nt(qm_ref[p, pl.ds(q1 + TQ * h, TQ), :], kb) + bias1
                    z_ref[hh] = z
                    spb_ref[hh] = _softplus(z).astype(MM)
            rc_ref[pl.ds(q2, TQ), :] = rc_rows

        _block_pipeline(3, True, step)
        o_ref[...] = acc_ref[...].astype(MM)

    return pl.pallas_call(
        body, name="attn_fwd",
        out_shape=[jax.ShapeDtypeStruct((SEQ, ATT_DIM), MM), jax.ShapeDtypeStruct((SEQ, 128), F32)],
        in_specs=[VMEM_SPEC] * 3, out_specs=[VMEM_SPEC] * 2,
        scratch_shapes=[pltpu.VMEM((SEQ, ATT_DIM), F32), pltpu.VMEM((8, TQ, 128), F32),
                        pltpu.VMEM((8, TQ, 128), F32), pltpu.VMEM((8, TQ, 128), MM),
                        pltpu.VMEM((N_PAIRS, TQ, 256), MM), pltpu.VMEM((N_PAIRS, 2 * SEQ, 128), MM),
                        pltpu.VMEM((N_PAIRS, 2 * SEQ, 128), MM)],
        compiler_params=_params(),
    )(q, k, v)


def mix_fwd(u3, att, gc, ga, x, w_cb_g, b_cb, w_ab_g, w_out_g, g2, g3, after):
    def body(u_ref, a_ref, gc_ref, ga_ref, x_ref, wcb_ref, bcb_ref, wab_ref, wout_ref, g2_ref, g3_ref, after_ref,
             co_ref, ao_ref, mg_ref, mix_ref, x2_ref, h2_ref):
        u = u_ref[...]
        a = a_ref[...]
        co = jnp.concatenate([_dot(u, wcb_ref[j]) for j in range(N_CHIPS)], axis=1) + bcb_ref[...]
        ao = jnp.concatenate([_dot(a, wab_ref[j]) for j in range(N_CHIPS)], axis=1)
        co_ref[...] = co.astype(MM)
        ao_ref[...] = ao.astype(MM)
        merged = (_sigmoid(gc_ref[...]) * co + _sigmoid(ga_ref[...]) * ao).astype(MM)
        mg_ref[...] = merged
        mix = _dot(merged, wout_ref[...])
        mix_ref[...] = mix
        n2, _ = _rms(mix)
        x2 = x_ref[...] + n2 * g2_ref[...]
        x2_ref[...] = x2
        n3, _ = _rms(x2)
        h2_ref[...] = (n3 * g3_ref[...]).astype(MM)

    out_shape = [
        jax.ShapeDtypeStruct((SEQ, D_MODEL), MM), jax.ShapeDtypeStruct((SEQ, D_MODEL), MM),
        jax.ShapeDtypeStruct((SEQ, D_MODEL), MM), jax.ShapeDtypeStruct((SEQ, D_MODEL), F32),
        jax.ShapeDtypeStruct((SEQ, D_MODEL), F32), jax.ShapeDtypeStruct((SEQ, D_MODEL), MM),
    ]
    vec = _full_spec((1, D_MODEL))
    return pl.pallas_call(
        body, name="mix_fwd", grid=(SEQ // TM,), out_shape=out_shape,
        in_specs=[_row_tile_spec(CONV_DIM), _row_tile_spec(ATT_DIM), _row_tile_spec(D_MODEL),
                  _row_tile_spec(D_MODEL), _row_tile_spec(D_MODEL), _weight_spec(w_cb_g.shape), vec,
                  _weight_spec(w_ab_g.shape), _weight_spec(w_out_g.shape), vec, vec, TOKEN_SPEC],
        out_specs=[_row_tile_spec(D_MODEL)] * 6,
        compiler_params=_params(("arbitrary",)),
    )(u3, att, gc, ga, x, w_cb_g, b_cb, w_ab_g, w_out_g, g2, g3, after)


def ffn_up_fwd(h2, w_up_g):
    def body(h_ref, wg_ref, wu_ref, gate_ref, up_ref, act_ref):
        h = h_ref[...]
        gate = _dot(h, wg_ref[0])
        up = _dot(h, wu_ref[0])
        gate_ref[...] = gate.astype(MM)
        up_ref[...] = up.astype(MM)
        act_ref[...] = (gate * _sigmoid(gate) * up).astype(MM)

    tile = pl.BlockSpec((TM, UP_SHARD), lambda n, i: (i, n))
    act = jax.ShapeDtypeStruct((SEQ, D_FF), MM)
    return pl.pallas_call(
        body, name="ffn_up_fwd", grid=(2, SEQ // TM), out_shape=[act, act, act],
        in_specs=[pl.BlockSpec((TM, D_MODEL), lambda n, i: (i, 0)),
                  pl.BlockSpec((1, D_MODEL, UP_SHARD), lambda n, i: (n, 0, 0)),
                  pl.BlockSpec((1, D_MODEL, UP_SHARD), lambda n, i: (n + 2, 0, 0))],
        out_specs=[tile, tile, tile],
        compiler_params=_params(("arbitrary", "arbitrary")),
    )(h2, w_up_g, w_up_g)


def ffn_down_loss(act, w_down_g, x2, target, g4):
    def body(act_ref, wd_ref, x2_ref, t_ref, g_ref, dff_ref, dy_ref, loss_ref, dg_ref):
        ff = _dot(act_ref[...], wd_ref[...])
        n4, r4 = _rms(ff)
        g4v = g_ref[...]
        err = x2_ref[...] + n4 * g4v - t_ref[...]
        row_loss = jnp.mean(err * err, axis=-1, keepdims=True)
        loss_ref[...] = jnp.zeros((8, 128), F32) + 0.5 * jnp.sum(row_loss, axis=0, keepdims=True)
        dy = err * (1.0 / D_MODEL)
        dy_ref[...] = dy
        dff_ref[...] = _rms_bwd(dy * g4v, n4, r4).astype(MM)
        _acc_rows(dg_ref, jnp.sum(dy * n4, axis=0, keepdims=True), pl.program_id(0) == 0)

    nt = SEQ // TM
    vec = _full_spec((1, D_MODEL))
    return pl.pallas_call(
        body, name="ffn_down_loss", grid=(nt,),
        out_shape=(jax.ShapeDtypeStruct((SEQ, D_MODEL), MM), jax.ShapeDtypeStruct((SEQ, D_MODEL), F32),
                   jax.ShapeDtypeStruct((nt * 8, 128), F32), jax.ShapeDtypeStruct((1, D_MODEL), F32)),
        in_specs=[_row_tile_spec(D_FF), _weight_spec(w_down_g.shape), _row_tile_spec(D_MODEL),
                  _row_tile_spec(D_MODEL), vec],
        out_specs=[_row_tile_spec(D_MODEL), _row_tile_spec(D_MODEL),
                   pl.BlockSpec((8, 128), lambda i: (i, 0)), vec],
        compiler_params=_params(("arbitrary",)),
    )(act, w_down_g, x2, target, g4)


def ffn_act_bwd(dff, w_down_g, gate, up):
    def body(dff_ref, wd_ref, gate_ref, up_ref, dgu_ref):
        dact = _dot_nt(dff_ref[...], wd_ref[...])
        gate = gate_ref[...].astype(F32)
        sg = _sigmoid(gate)
        dgu_ref[:, 0:D_FF] = (dact * up_ref[...].astype(F32) * (sg * (1.0 + gate * (1.0 - sg)))).astype(MM)
        dgu_ref[:, D_FF:2 * D_FF] = (dact * (gate * sg)).astype(MM)

    return pl.pallas_call(
        body, name="ffn_act_bwd", grid=(SEQ // TM,),
        out_shape=jax.ShapeDtypeStruct((SEQ, 2 * D_FF), MM),
        in_specs=[_row_tile_spec(D_MODEL), _weight_spec(w_down_g.shape), _row_tile_spec(D_FF), _row_tile_spec(D_FF)],
        out_specs=_row_tile_spec(2 * D_FF),
        compiler_params=_params(("arbitrary",)),
    )(dff, w_down_g, gate, up)


def ffn_in_bwd(dgu, w_up_g, x2, mix, dy, g3, g2):
    def body(dgu_ref, w_ref, x2_ref, mix_ref, dy_ref, g3_ref, g2_ref, dx2_ref, dmix_ref, dg3_ref, dg2_ref):
        dh2 = None
        for j in range(N_CHIPS):
            t = _dot_nt(dgu_ref[:, j * UP_SHARD:(j + 1) * UP_SHARD], w_ref[j])
            dh2 = t if dh2 is None else dh2 + t
        first = pl.program_id(0) == 0
        n3, r3 = _rms(x2_ref[...])
        dx2 = dy_ref[...] + _rms_bwd(dh2 * g3_ref[...], n3, r3)
        dx2_ref[...] = dx2
        _acc_rows(dg3_ref, jnp.sum(dh2 * n3, axis=0, keepdims=True), first)
        n2, r2 = _rms(mix_ref[...])
        dmix_ref[...] = _rms_bwd(dx2 * g2_ref[...], n2, r2).astype(MM)
        _acc_rows(dg2_ref, jnp.sum(dx2 * n2, axis=0, keepdims=True), first)

    vec = _full_spec((1, D_MODEL))
    return pl.pallas_call(
        body, name="ffn_in_bwd", grid=(SEQ // TM,),
        out_shape=(jax.ShapeDtypeStruct((SEQ, D_MODEL), F32), jax.ShapeDtypeStruct((SEQ, D_MODEL), MM),
                   jax.ShapeDtypeStruct((1, D_MODEL), F32), jax.ShapeDtypeStruct((1, D_MODEL), F32)),
        in_specs=[_row_tile_spec(2 * D_FF), _weight_spec(w_up_g.shape), _row_tile_spec(D_MODEL),
                  _row_tile_spec(D_MODEL), _row_tile_spec(D_MODEL), vec, vec],
        out_specs=[_row_tile_spec(D_MODEL), _row_tile_spec(D_MODEL), vec, vec],
        compiler_params=_params(("arbitrary",)),
    )(dgu, w_up_g, x2, mix, dy, g3, g2)


def merge_bwd(dmix, w_out_g, gc, ga, co, ao, w_cb_g, w_ab_g):
    def body(dmix_ref, wout_ref, gc_ref, ga_ref, co_ref, ao_ref, wcb_ref, wab_ref,
             dco_ref, dao_ref, dg_ref, du3_ref, datt_ref, dbcb_ref):
        dm = _dot_nt(dmix_ref[...], wout_ref[...])
        sgc = _sigmoid(gc_ref[...])
        sga = _sigmoid(ga_ref[...])
        dco = dm * sgc
        dao = dm * sga
        dg_ref[:, 0:D_MODEL] = (dm * co_ref[...].astype(F32) * (sgc * (1.0 - sgc))).astype(MM)
        dg_ref[:, D_MODEL:2 * D_MODEL] = (dm * ao_ref[...].astype(F32) * (sga * (1.0 - sga))).astype(MM)
        _acc_rows(dbcb_ref, jnp.sum(dco, axis=0, keepdims=True), pl.program_id(0) == 0)
        dco_ref[...] = dco.astype(MM)
        dao_ref[...] = dao.astype(MM)
        du3 = None
        datt = None
        for j in range(N_CHIPS):
            cols = slice(j * BR_SHARD, (j + 1) * BR_SHARD)
            t = _dot_nt(dco_ref[:, cols], wcb_ref[j])
            s = _dot_nt(dao_ref[:, cols], wab_ref[j])
            du3 = t if du3 is None else du3 + t
            datt = s if datt is None else datt + s
        du3_ref[...] = du3
        datt_ref[...] = datt.astype(MM)

    wide = _row_tile_spec(D_MODEL)
    return pl.pallas_call(
        body, name="merge_bwd", grid=(SEQ // TM,),
        out_shape=(jax.ShapeDtypeStruct((SEQ, D_MODEL), MM), jax.ShapeDtypeStruct((SEQ, D_MODEL), MM),
                   jax.ShapeDtypeStruct((SEQ, 2 * D_MODEL), MM),
                   jax.ShapeDtypeStruct((SEQ, CONV_DIM), F32), jax.ShapeDtypeStruct((SEQ, ATT_DIM), MM),
                   jax.ShapeDtypeStruct((1, D_MODEL), F32)),
        in_specs=[wide, _weight_spec(w_out_g.shape), wide, wide, wide, wide,
                  _weight_spec(w_cb_g.shape), _weight_spec(w_ab_g.shape)],
        out_specs=[wide, wide, _row_tile_spec(2 * D_MODEL), _row_tile_spec(CONV_DIM), _row_tile_spec(ATT_DIM),
                   _full_spec((1, D_MODEL))],
        compiler_params=_params(("arbitrary",)),
    )(dmix, w_out_g, gc, ga, co, ao, w_cb_g, w_ab_g)


def conv_bwd(du3, u1, ci, w_dw, ln_g, ln_b, after):
    def body(du3_ref, u1_ref, ci_ref, w_ref, g_ref, bb_ref, after_ref,
             dci_ref, dw_ref, dbdw_ref, dg_ref, db_ref, upad_ref, dpad_ref, dwacc_ref, vacc_ref):
        _glu_into(ci_ref, upad_ref)
        dpad_ref[SEQ:SEQ + 32, :] = jnp.zeros((32, CONV_DIM), F32)
        dwacc_ref[...] = jnp.zeros_like(dwacc_ref)
        vacc_ref[...] = jnp.zeros_like(vacc_ref)

        def fold8(t):
            s = t[0:8, :]
            for r in range(1, CONV_TILE // 8):
                s = s + t[8 * r:8 * r + 8, :]
            return s

        def pass1(i, c):
            t0 = pl.multiple_of(i * CONV_TILE, CONV_TILE)
            xh, rstd = _layernorm_parts(u1_ref[pl.ds(t0, CONV_TILE), :])
            gv = g_ref[...]
            u2 = xh * gv + bb_ref[...]
            s2 = _sigmoid(u2)
            du2 = du3_ref[pl.ds(t0, CONV_TILE), :] * (s2 * (1.0 + u2 * (1.0 - s2)))
            wv = du2 * gv
            du1 = rstd * (wv - jnp.mean(wv, axis=-1, keepdims=True)
                          - xh * jnp.mean(wv * xh, axis=-1, keepdims=True))
            dpad_ref[pl.ds(t0, CONV_TILE), :] = du1
            vacc_ref[0] += fold8(du2 * xh)
            vacc_ref[1] += fold8(du2)
            vacc_ref[2] += fold8(du1)
            win = upad_ref[pl.ds(t0, CONV_WIN), :]
            n = win.shape[0]
            for rot in range(8):
                shifted = win if rot == 0 else pltpu.roll(win, n - rot, 0)
                for a in range(5):
                    j = 8 * a + rot - 2
                    if 0 <= j < CONV_WIDTH:
                        dwacc_ref[j] += fold8(du1 * shifted[8 * a:8 * a + CONV_TILE, :])
            return c

        lax.fori_loop(0, SEQ // CONV_TILE, pass1, 0)

        def pass2(i, c):
            t0 = pl.multiple_of(i * CONV_TILE, CONV_TILE)
            win = dpad_ref[pl.ds(t0, CONV_WIN), :]
            du0 = _shifted_sum(win, [(30 - j, w_ref[j:j + 1, :]) for j in range(CONV_WIDTH)])
            a = ci_ref[pl.ds(t0, CONV_TILE), 0:CONV_DIM]
            sb = _sigmoid(ci_ref[pl.ds(t0, CONV_TILE), CONV_DIM:2 * CONV_DIM])
            dci_ref[pl.ds(t0, CONV_TILE), 0:CONV_DIM] = (du0 * sb).astype(MM)
            dci_ref[pl.ds(t0, CONV_TILE), CONV_DIM:2 * CONV_DIM] = (du0 * a * (sb * (1.0 - sb))).astype(MM)
            return c

        lax.fori_loop(0, SEQ // CONV_TILE, pass2, 0)

        for j in range(CONV_WIDTH):
            dw_ref[j:j + 1, :] = jnp.sum(dwacc_ref[j], axis=0, keepdims=True)
        dw_ref[CONV_WIDTH:32, :] = jnp.zeros((32 - CONV_WIDTH, CONV_DIM), F32)
        dg_ref[...] = jnp.sum(vacc_ref[0], axis=0, keepdims=True)
        db_ref[...] = jnp.sum(vacc_ref[1], axis=0, keepdims=True)
        dbdw_ref[...] = jnp.sum(vacc_ref[2], axis=0, keepdims=True)

    vec = jax.ShapeDtypeStruct((1, CONV_DIM), F32)
    return pl.pallas_call(
        body, name="conv_bwd",
        out_shape=(jax.ShapeDtypeStruct((SEQ, 2 * CONV_DIM), MM), jax.ShapeDtypeStruct((32, CONV_DIM), F32),
                   vec, vec, vec),
        in_specs=[VMEM_SPEC] * 7, out_specs=[VMEM_SPEC] * 5,
        scratch_shapes=[pltpu.VMEM((SEQ + 32, CONV_DIM), F32), pltpu.VMEM((SEQ + 32, CONV_DIM), F32),
                        pltpu.VMEM((CONV_WIDTH, 8, CONV_DIM), F32), pltpu.VMEM((3, 8, CONV_DIM), F32)],
        compiler_params=_params(),
    )(du3, u1, ci, w_dw, ln_g, ln_b, after)


def attn_bwd(q, k, v, datt, rc, after):
    def body(q_ref, k_ref, v_ref, do_ref, rc_ref, after_ref, dqkv_ref, dqa_ref, dka_ref, dva_ref, pc_ref, z_ref,
             sig1_ref, sig2_ref, g_ref, spb_ref, gb_ref, ar_ref, dzr_ref, dzc_ref, qm_ref, km_ref, dom_ref):
        lane, row, _ = _head_masks()
        _split_heads(q_ref, qm_ref)
        _split_heads(k_ref, km_ref)
        _split_heads(do_ref, dom_ref)
        for ref in (dqa_ref, dka_ref, dva_ref, pc_ref):
            ref[...] = jnp.zeros_like(ref)
        z_ref[...] = jnp.full(z_ref.shape, NO_SCORE, F32)
        for ref in (sig1_ref, sig2_ref, spb_ref, ar_ref, g_ref, gb_ref, dzr_ref, dzc_ref):
            ref[...] = jnp.zeros_like(ref)
        w_suffix = _cumsum_weights(suffix=True, with_total=False)
        w_prefix = _cumsum_weights(suffix=False, with_total=True)

        def step(pairs):
            (ia, ja), (ib, jb), (ic, jc), (id_, jd) = pairs
            ka, qb_, kb_, kc, qd, kd = (pl.multiple_of(jnp.maximum(b, 0) * TQ, TQ) for b in (ja, ib, jb, jc, id_, jd))
            qa2, qb2, qc2, qd2, kd2 = (pl.multiple_of(jnp.maximum(b, 0) * 2 * TQ, 2 * TQ)
                                       for b in (ia, ib, ic, id_, jd))
            bias_a = _score_bias(lane, row, ia, ja)
            rc_rows = rc_ref[pl.ds(qb_, TQ), :]
            first_c = jc == 0
            for p in range(N_PAIRS):
                cols = slice(128 * p, 128 * (p + 1))
                k_a = k_ref[pl.ds(ka, TQ), cols]
                v_b = v_ref[pl.ds(kb_, TQ), cols]
                dqa_ref[pl.ds(qd, TQ), cols] += _dot(dzc_ref[p], km_ref[p, pl.ds(kd2, 2 * TQ), :])
                dka_ref[pl.ds(kd, TQ), cols] += _dot_tn(dzr_ref[p], qm_ref[p, pl.ds(qd2, 2 * TQ), :])
                dva_ref[pl.ds(kc, TQ), cols] += _dot_tn(ar_ref[p], dom_ref[p, pl.ds(qc2, 2 * TQ), :])
                for h in range(2):
                    hh = 2 * p + h
                    rows = slice(TQ * h, TQ * (h + 1))
                    r = _dot(gb_ref[hh], w_prefix)
                    p_in = jnp.where(first_c, 0.0, pc_ref[hh])
                    dz = (g_ref[hh] - sig2_ref[hh] * (r[:, :128] + p_in)).astype(MM)
                    dzc_ref[p, :, rows] = dz
                    dzr_ref[p, rows, :] = dz
                    pc_ref[hh] = p_in + r[:, 128:]
                    r_in = jnp.sum(jnp.where(lane == 16 * hh + jb, rc_rows, 0.0), axis=1, keepdims=True)
                    a = jnp.exp(z_ref[hh] - (_dot(spb_ref[hh], w_suffix) + r_in))
                    g = _dot_nt(dom_ref[p, pl.ds(qb2 + TQ * h, TQ), :], v_b) * a
                    ar_ref[p, rows, :] = a.astype(MM)
                    g_ref[hh] = g
                    gb_ref[hh] = g.astype(MM)
                    sig2_ref[hh] = sig1_ref[hh]
                    z = _dot_nt(qm_ref[p, pl.ds(qa2 + TQ * h, TQ), :], k_a) + bias_a
                    sp = _softplus(z)
                    sig1_ref[hh] = jnp.exp(z - sp)
                    z_ref[hh] = z
                    spb_ref[hh] = sp.astype(MM)

        _block_pipeline(4, False, step)
        dqkv_ref[:, 0:ATT_DIM] = (dqa_ref[...] * ATT_SCALE).astype(MM)
        dqkv_ref[:, ATT_DIM:2 * ATT_DIM] = dka_ref[...].astype(MM)
        dqkv_ref[:, 2 * ATT_DIM:3 * ATT_DIM] = dva_ref[...].astype(MM)

    split = pltpu.VMEM((N_PAIRS, 2 * SEQ, 128), MM)
    return pl.pallas_call(
        body, name="attn_bwd", out_shape=jax.ShapeDtypeStruct((SEQ, 3 * ATT_DIM), MM),
        in_specs=[VMEM_SPEC] * 6, out_specs=VMEM_SPEC,
        scratch_shapes=[pltpu.VMEM((SEQ, ATT_DIM), F32)] * 3 + [pltpu.VMEM((8, TQ, 128), F32)] * 5
                       + [pltpu.VMEM((8, TQ, 128), MM)] * 2
                       + [pltpu.VMEM((N_PAIRS, 2 * TQ, 128), MM)] * 2 + [pltpu.VMEM((N_PAIRS, TQ, 256), MM)]
                       + [split] * 3,
        compiler_params=_params(),
    )(q, k, v, datt, rc, after)


DPROJ_PIECES = ((0, 1024), (1024, 2560), (2560, 4608))


def _dproj_segments(j):
    g0, g1 = j * IN_SHARD, (j + 1) * IN_SHARD
    segs = []
    for p, (s, e) in enumerate(DPROJ_PIECES):
        lo, hi = max(s, g0), min(e, g1)
        if lo < hi:
            segs.append((p, lo - s, lo - g0, hi - lo))
    return segs


def in_proj_bwd(pieces, w_in_g, x, dx2, g1, after):
    def body(p0_ref, p1_ref, p2_ref, w_ref, x_ref, dx2_ref, g_ref, after_ref, dx_ref, dg_ref):
        p_refs = (p0_ref, p1_ref, p2_ref)
        dh = None
        for j in range(N_CHIPS):
            for p, lo, off, width in _dproj_segments(j):
                t = _dot_nt(p_refs[p][:, lo:lo + width], w_ref[j, :, off:off + width])
                dh = t if dh is None else dh + t
        n1, r1 = _rms(x_ref[...])
        dx_ref[...] = dx2_ref[...] + _rms_bwd(dh * g_ref[...], n1, r1)
        _acc_rows(dg_ref, jnp.sum(dh * n1, axis=0, keepdims=True), pl.program_id(0) == 0)

    vec = _full_spec((1, D_MODEL))
    return pl.pallas_call(
        body, name="in_proj_bwd", grid=(SEQ // TM,),
        out_shape=[jax.ShapeDtypeStruct((SEQ, D_MODEL), F32), jax.ShapeDtypeStruct((1, D_MODEL), F32)],
        in_specs=[_row_tile_spec(p.shape[1]) for p in pieces]
                 + [_weight_spec(w_in_g.shape), _row_tile_spec(D_MODEL), _row_tile_spec(D_MODEL), vec, TOKEN_SPEC],
        out_specs=[_row_tile_spec(D_MODEL), vec],
        compiler_params=_params(("arbitrary",)),
    )(*pieces, w_in_g, x, dx2, g1, after)


def weight_grad_in(h1, pieces):
    kh = D_MODEL // 2

    def body(a_ref, p0_ref, p1_ref, p2_ref, o_ref):
        p_refs = (p0_ref, p1_ref, p2_ref)
        a = a_ref[...]
        for j in range(N_CHIPS):
            @pl.when(pl.program_id(1) == j)
            def _():
                for p, lo, off, width in _dproj_segments(j):
                    o_ref[0, 0, :, off:off + width] = _dot_tn(a, p_refs[p][:, lo:lo + width]).astype(MM)

    return pl.pallas_call(
        body, name="dw_in", grid=(2, N_CHIPS), out_shape=jax.ShapeDtypeStruct((N_CHIPS, 2, kh, IN_SHARD), MM),
        in_specs=[pl.BlockSpec((SEQ, kh), lambda h, j: (0, h))] + [_weight_spec(p.shape) for p in pieces],
        out_specs=pl.BlockSpec((1, 1, kh, IN_SHARD), lambda h, j: (j, h, 0, 0)),
        compiler_params=_params(("arbitrary", "arbitrary")),
    )(h1, *pieces)


def weight_grad(a, b, name, col_sharded, tk=None):
    kin, n = a.shape[1], b.shape[1]

    def body(a_ref, b_ref, o_ref):
        if col_sharded:
            o_ref[0, 0] = _dot_tn(a_ref[...], b_ref[...]).astype(MM)
        else:
            o_ref[...] = _dot_tn(a_ref[...], b_ref[...]).astype(MM)

    if col_sharded:
        kh, ns = kin // 2, n // N_CHIPS
        out = jax.ShapeDtypeStruct((N_CHIPS, 2, kh, ns), MM)
        grid = (2, N_CHIPS)
        in_specs = [pl.BlockSpec((SEQ, kh), lambda h, j: (0, h)), pl.BlockSpec((SEQ, ns), lambda h, j: (0, j))]
        out_spec = pl.BlockSpec((1, 1, kh, ns), lambda h, j: (j, h, 0, 0))
        sem = ("arbitrary", "arbitrary")
    else:
        out = jax.ShapeDtypeStruct((kin, n), MM)
        grid = (kin // tk,)
        in_specs = [pl.BlockSpec((SEQ, tk), lambda r: (0, r)), pl.BlockSpec((SEQ, n), lambda r: (0, 0))]
        out_spec = pl.BlockSpec((tk, n), lambda r: (r, 0))
        sem = ("arbitrary",)
    res = pl.pallas_call(
        body, name=name, grid=grid, out_shape=out, in_specs=in_specs, out_specs=out_spec,
        compiler_params=_params(sem),
    )(a, b)
    if not col_sharded:
        res = res.reshape(N_CHIPS, 2, kin // (2 * N_CHIPS), n)
    return res


def _place():
    x, y, c = lax.axis_index("x"), lax.axis_index("y"), lax.axis_index("c")
    chips = [(1 - x, y), (x, 1 - y), (1 - x, 1 - y)]
    return x, y, c, chips


def _rcopy(src, dst, send_sem, recv_sem, dev):
    return pltpu.make_async_remote_copy(src_ref=src, dst_ref=dst, send_sem=send_sem, recv_sem=recv_sem,
                                        device_id=dev, device_id_type=MESH)


class _Gather:
    def __init__(self, shapes, w, o, scratch):
        self.n, self.shapes, self.w, self.o = len(w), shapes, w, o
        self.send, self.recv, self.fsend, self.frecv, self.loc_in, self.loc_out = scratch[:6]
        self.raw, self.stage = scratch[6:6 + self.n], scratch[6 + self.n:]
        self.x, self.y, self.c, self.chips = _place()
        self.me = 2 * self.x + self.y
        self.sib = (self.x, self.y, 1 - self.c)
        self.pairs = [(j, t) for j in range(3) for t in range(self.n)]

    @staticmethod
    def scratch(shards):
        n = len(shards)
        sems = pltpu.SemaphoreType.DMA
        return ([sems((3 * n,)), sems((3 * n,)), sems((3 * n,)), sems((3 * n,)), sems((n,)), sems((n,))]
                + [pltpu.VMEM(s.shape, s.dtype) for s in shards] + [pltpu.VMEM(s.shape, MM) for s in shards])

    @staticmethod
    def out_shapes(shards):
        return [jax.ShapeDtypeStruct((N_CHIPS,) + s.shape, MM) for s in shards]

    def _half(self, t, k, cc):
        rh = self.shapes[t][0] // 2
        return self.o[t].at[k, pl.ds(cc * rh, rh), :]

    def _chip(self, j):
        cx, cy = self.chips[j]
        return 2 * cx + cy, (cx, cy, self.c)

    def local_in(self, t):
        return pltpu.make_async_copy(self.w[t], self.raw[t], self.loc_in.at[t])

    def local_out(self, t):
        return pltpu.make_async_copy(self.stage[t], self.o[t].at[self.me], self.loc_out.at[t])

    def first(self, j, t):
        rh = self.shapes[t][0] // 2
        i = j * self.n + t
        return _rcopy(self.stage[t].at[pl.ds(self.c * rh, rh), :], self._half(t, self.me, self.c),
                      self.send.at[i], self.recv.at[i], self._chip(j)[1])

    def arrived(self, j, t):
        k, dev = self._chip(j)
        i = j * self.n + t
        blk = self._half(t, k, self.c)
        return _rcopy(blk, blk, self.send.at[i], self.recv.at[i], dev)

    def passed(self, j, t, cc):
        i = j * self.n + t
        blk = self._half(t, self._chip(j)[0], cc)
        return _rcopy(blk, blk, self.fsend.at[i], self.frecv.at[i], self.sib)

    def start(self):
        for t in range(self.n):
            self.local_in(t).start()
        for t in range(self.n):
            self.local_in(t).wait()
            self.stage[t][...] = self.raw[t][...].astype(MM)
            self.local_out(t).start()
        for j, t in self.pairs:
            self.first(j, t).start()

    def forward(self):
        for j, t in self.pairs:
            self.arrived(j, t).wait_recv()
            self.passed(j, t, self.c).start()

    def finish(self):
        for j, t in self.pairs:
            self.passed(j, t, 1 - self.c).wait_recv()
        for j, t in self.pairs:
            self.first(j, t).wait_send()
            self.passed(j, t, self.c).wait_send()
        for t in range(self.n):
            self.local_out(t).wait()


def all_gather_weights(shards, small, later):
    n, m = len(shards), len(later)
    shapes = [s.shape for s in shards]

    def body(*refs):
        w = refs[:n]
        sm = refs[n]
        lw = refs[n + 1:n + 1 + m]
        o = refs[n + 1 + m:2 * n + 1 + m]
        osm = refs[2 * n + 1 + m]
        lo = refs[2 * n + 2 + m:2 * n + 2 + 2 * m]
        scratch = refs[2 * n + 2 + 2 * m:]
        ssend, srecv, sloc, lsem_in, lsem_out = scratch[:5]
        lraw, lstage = scratch[5:5 + m], scratch[5 + m:5 + 2 * m]
        g = _Gather(shapes, w, o, scratch[5 + 2 * m:])
        own = pltpu.make_async_copy(sm, osm.at[g.me], sloc)
        own.start()
        loads = [pltpu.make_async_copy(lw[t], lraw[t], lsem_in.at[t]) for t in range(m)]
        for cp in loads:
            cp.start()
        g.start()
        small_cps = [_rcopy(sm, osm.at[g.me], ssend.at[j], srecv.at[j], g._chip(j)[1]) for j in range(3)]
        for cp in small_cps:
            cp.start()
        places = []
        for t in range(m):
            loads[t].wait()
            lstage[t][...] = lraw[t][...].astype(MM)
            places.append(pltpu.make_async_copy(lstage[t], lo[t].at[g.me], lsem_out.at[t]))
            places[t].start()
        g.forward()
        g.finish()
        for j in range(3):
            k, dev = g._chip(j)
            _rcopy(sm, osm.at[k], ssend.at[j], srecv.at[j], dev).wait_recv()
            small_cps[j].wait_send()
        own.wait()
        for cp in places:
            cp.wait()

    out_shape = _Gather.out_shapes(shards)
    out_shape.append(jax.ShapeDtypeStruct((N_CHIPS,) + small.shape, small.dtype))
    out_shape += _Gather.out_shapes(later)
    sems = pltpu.SemaphoreType.DMA
    return pl.pallas_call(
        body, name="all_gather_weights", out_shape=out_shape,
        in_specs=[ANY] * (n + 1 + m), out_specs=[ANY] * (n + 1 + m),
        scratch_shapes=[sems((3,)), sems((3,)), sems, sems((m,)), sems((m,))]
                       + [pltpu.VMEM(s.shape, s.dtype) for s in later] + [pltpu.VMEM(s.shape, MM) for s in later]
                       + _Gather.scratch(shards),
        compiler_params=_params(),
    )(*shards, small, *later)


def sibling_exchange(grads, name):
    n = len(grads)

    def body(*refs):
        g = refs[:n]
        o = refs[n:2 * n]
        send, recv = refs[2 * n:]
        x, y, c, _ = _place()
        cps = [_rcopy(g[t].at[:, 1 - c], o[t], send.at[t], recv.at[t], (x, y, 1 - c)) for t in range(n)]
        for cp in cps:
            cp.start()
        for cp in cps:
            cp.wait()

    sems = pltpu.SemaphoreType.DMA
    return pl.pallas_call(
        body, name=name,
        out_shape=[jax.ShapeDtypeStruct((a.shape[0],) + a.shape[2:], a.dtype) for a in grads],
        in_specs=[ANY] * n, out_specs=[ANY] * n, scratch_shapes=[sems((n,)), sems((n,))],
    )(*grads)


HBM_SPEC = pl.BlockSpec(memory_space=pltpu.HBM)
SEM_SPEC = pl.BlockSpec(memory_space=pltpu.SEMAPHORE)
DATAFLOW = pltpu.SideEffectType.DATAFLOW_SIDE_EFFECTING


def split_start(name, bufs, n_copies, copies):
    nb = len(bufs)

    def body(*refs):
        for cp in copies(refs[:nb], refs[nb], refs[nb + 1]):
            cp.start()
        token = refs[2 * nb + 2]
        token[...] = jnp.zeros_like(token)

    sems = [pltpu.SemaphoreType.DMA((n_copies,))] * 2
    res = pl.pallas_call(
        body, name=name,
        out_shape=sems + [pltpu.HBM(a.shape, a.dtype) for a in bufs] + [jax.ShapeDtypeStruct((8, 128), F32)],
        in_specs=[HBM_SPEC] * nb, out_specs=[SEM_SPEC] * 2 + [HBM_SPEC] * nb + [VMEM_SPEC],
        input_output_aliases={i: 2 + i for i in range(nb)},
        compiler_params=pltpu.CompilerParams(has_side_effects=DATAFLOW),
    )(*[pltpu.with_memory_space_constraint(a, pltpu.HBM) for a in bufs])
    return res[:-1], res[-1]


def split_wait(name, state, after, copies):
    sems, bufs = state[:2], state[2:]
    nb = len(bufs)

    def body(*refs):
        for cp in copies(refs[:nb], refs[nb], refs[nb + 1]):
            cp.wait_send()
            cp.wait_recv()

    return pl.pallas_call(
        body, name=name, out_shape=[pltpu.HBM(a.shape, a.dtype) for a in bufs],
        in_specs=[HBM_SPEC] * nb + [SEM_SPEC] * 2 + [ANY] * len(after), out_specs=[HBM_SPEC] * nb,
        input_output_aliases={i: i for i in range(nb)},
        compiler_params=pltpu.CompilerParams(has_side_effects=DATAFLOW),
    )(*bufs, *sems, *after)


def _scatter_copies(n):
    def copies(refs, send, recv):
        _, _, c, chips = _place()
        return [_rcopy(refs[t].at[2 * cx + cy], refs[n + t].at[j], send.at[3 * t + j], recv.at[3 * t + j], (cx, cy, c))
                for t in range(n) for j, (cx, cy) in enumerate(chips)]
    return copies


def scatter_start(parts, tag):
    lands = [lax.empty((3,) + p.shape[1:], p.dtype) for p in parts]
    return split_start("scatter_start_" + tag, list(parts) + lands, 3 * len(parts), _scatter_copies(len(parts)))


def scatter_wait(state, after, tag):
    n = (len(state) - 2) // 2
    return split_wait("scatter_wait_" + tag, state, after, _scatter_copies(n))[n:]


def _gather_copies(shapes, level):
    n = len(shapes)

    def copies(refs, send, recv):
        x, y, c, chips = _place()
        out = []
        for t in range(n):
            rh = shapes[t][0] // 2
            for j, (cx, cy) in enumerate(chips):
                k, dev = (2 * x + y, (cx, cy, c)) if level == 1 else (2 * cx + cy, (x, y, 1 - c))
                blk = refs[t].at[k, pl.ds(c * rh, rh), :]
                out.append(_rcopy(blk, blk, send.at[3 * t + j], recv.at[3 * t + j], dev))
        return out
    return copies


def sibling_swap(halves, name):
    n = len(halves)

    def body(*refs):
        h = refs[:n]
        o = refs[n:2 * n]
        send, recv = refs[2 * n:]
        x, y, c, _ = _place()
        cps = [_rcopy(h[t], o[t], send.at[t], recv.at[t], (x, y, 1 - c)) for t in range(n)]
        for cp in cps:
            cp.start()
        for cp in cps:
            cp.wait()

    sems = pltpu.SemaphoreType.DMA
    return pl.pallas_call(
        body, name=name, out_shape=[jax.ShapeDtypeStruct(a.shape, a.dtype) for a in halves],
        in_specs=[ANY] * n, out_specs=[ANY] * n, scratch_shapes=[sems((n,)), sems((n,))],
    )(*halves)


def small_all_reduce(ddw, v512, v1024, loss_parts):
    rows, width = PACK_ROWS, 512
    n512, n1024 = len(VEC512), len(VEC1024)

    def body(*refs):
        ddw_ref = refs[0]
        a_refs = refs[1:1 + n512]
        b_refs = refs[1 + n512:1 + n512 + n1024]
        lp_ref, o_ref, p_ref, gath_ref, send, recv = refs[1 + n512 + n1024:]
        p_ref[...] = jnp.zeros_like(p_ref)
        p_ref[0:32, :] = ddw_ref[...]
        p_ref[LOSS_ROW:LOSS_ROW + 1, 0:128] = jnp.sum(lp_ref[...], axis=0, keepdims=True) * 0.125
        for i, r in enumerate(a_refs):
            p_ref[32 + i:33 + i, :] = r[...]
        for i, r in enumerate(b_refs):
            base = 32 + n512 + 2 * i
            p_ref[base:base + 1, :] = r[:, 0:512]
            p_ref[base + 1:base + 2, :] = r[:, 512:1024]
        x, y, c, _ = _place()
        me = 4 * x + 2 * y + c
        gath_ref[me] = p_ref[...]
        cps = []
        for k in range(1, 8):
            dx, dy, dc = (k >> 2) & 1, (k >> 1) & 1, k & 1
            px = 1 - x if dx else x
            py = 1 - y if dy else y
            pc = 1 - c if dc else c
            cps.append(_rcopy(p_ref, gath_ref.at[me], send.at[k - 1], recv.at[k - 1], (px, py, pc)))
        for cp in cps:
            cp.start()
        for k in range(1, 8):
            dx, dy, dc = (k >> 2) & 1, (k >> 1) & 1, k & 1
            px = 1 - x if dx else x
            py = 1 - y if dy else y
            pc = 1 - c if dc else c
            _rcopy(p_ref, gath_ref.at[4 * px + 2 * py + pc], send.at[k - 1], recv.at[k - 1], (px, py, pc)).wait_recv()
        for cp in cps:
            cp.wait_send()
        total = gath_ref[0]
        for d in range(1, 8):
            total = total + gath_ref[d]
        o_ref[...] = total

    sems = pltpu.SemaphoreType.DMA
    n_in = 2 + n512 + n1024
    return pl.pallas_call(
        body, name="small_all_reduce", out_shape=jax.ShapeDtypeStruct((rows, width), F32),
        in_specs=[VMEM_SPEC] * n_in, out_specs=VMEM_SPEC,
        scratch_shapes=[pltpu.VMEM((rows, width), F32), pltpu.VMEM((8, rows, width), F32), sems((7,)), sems((7,))],
    )(ddw, *[v512[n] for n in VEC512], *[v1024[n] for n in VEC1024], loss_parts)


def _row_block(r):
    for tr in (512, 352, 256, 128):
        if r % tr == 0:
            return tr
    return r


def add_halves(g, recv, name):
    _, _, r, w = g.shape
    tr = _row_block(r)

    def body(g_ref, r_ref, ob_ref, own_ref):
        k = pl.program_id(1)
        me = 2 * lax.axis_index("x") + lax.axis_index("y")
        t = g_ref[0, 0].astype(F32) + r_ref[0].astype(F32)
        ob_ref[0] = t.astype(MM)
        mine = jnp.where(k == me, t, 0.0)

        @pl.when(k == 0)
        def _():
            own_ref[...] = mine

        @pl.when(k != 0)
        def _():
            own_ref[...] += mine

    return pl.pallas_call(
        body, name=name, grid=(r // tr, N_CHIPS),
        in_specs=[pl.BlockSpec((1, 1, tr, w), lambda i, k: (k, lax.axis_index("c"), i, 0)),
                  pl.BlockSpec((1, tr, w), lambda i, k: (k, i, 0))],
        out_specs=[pl.BlockSpec((1, tr, w), lambda i, k: (k, i, 0)),
                   pl.BlockSpec((tr, w), lambda i, k: (i, 0))],
        out_shape=(jax.ShapeDtypeStruct((N_CHIPS, r, w), MM), jax.ShapeDtypeStruct((r, w), F32)),
        compiler_params=_params(("arbitrary", "arbitrary")),
    )(g, recv)


def sum_parts(own, rin, after, name):
    _, r, w = rin.shape
    tr = _row_block(r)

    def body(o_ref, r_ref, after_ref, out_ref):
        out_ref[...] = ((o_ref[...] + r_ref[0].astype(F32)) + r_ref[1].astype(F32)) + r_ref[2].astype(F32)

    return pl.pallas_call(
        body, name=name, grid=(r // tr,), out_shape=jax.ShapeDtypeStruct((r, w), F32),
        in_specs=[pl.BlockSpec((tr, w), lambda i: (i, 0)), pl.BlockSpec((3, tr, w), lambda i: (0, i, 0)),
                  _full_spec((8, 128))],
        out_specs=pl.BlockSpec((tr, w), lambda i: (i, 0)),
        compiler_params=_params(("arbitrary",)),
    )(own, rin, after)


def _adamw_math(w, g, m, v):
    mn = ADAM_B1 * m + (1.0 - ADAM_B1) * g
    vn = ADAM_B2 * v + (1.0 - ADAM_B2) * (g * g)
    m_hat = mn / (1.0 - ADAM_B1 ** ADAM_STEP)
    v_hat = vn / (1.0 - ADAM_B2 ** ADAM_STEP)
    return -ADAM_LR * (m_hat / (jnp.sqrt(v_hat) + ADAM_EPS) + ADAM_WD * w), mn, vn


def adamw(w, mine, other, m, v, name):
    r, c = w.shape
    rh = r // 2
    tr = _row_block(rh)
    if c >= 1024 and tr % 512 == 0:
        tr = 256
    nb = rh // tr

    def body(w_ref, a_ref, b_ref, m_ref, v_ref, go_ref, d_ref, mo_ref, vo_ref):
        gv = jnp.where(lax.axis_index("c") == pl.program_id(0), a_ref[...], b_ref[...])
        go_ref[...] = gv
        d_ref[...], mo_ref[...], vo_ref[...] = _adamw_math(w_ref[...], gv, m_ref[...], v_ref[...])

    def half(of_sibling):
        def index(h, i):
            owner = lax.axis_index("c")
            owner = 1 - owner if of_sibling else owner
            return jnp.where(h == owner, i, jnp.where(h < owner, 0, nb - 1)), 0
        return pl.BlockSpec((tr, c), index)

    spec = pl.BlockSpec((tr, c), lambda h, i: (h * nb + i, 0))
    out = jax.ShapeDtypeStruct((r, c), F32)
    return pl.pallas_call(
        body, name=name, grid=(2, nb), out_shape=(out, out, out, out),
        in_specs=[spec, half(False), half(True), spec, spec], out_specs=[spec] * 4,
        compiler_params=_params(("arbitrary", "arbitrary")),
    )(w, mine, other, m, v)


def adamw_small(gsum, params):
    names = list(params)
    flat = [a for n in names for a in params[n]]

    def body(*refs):
        g_ref = refs[0]
        ins = refs[1:1 + 3 * len(names)]
        outs = refs[1 + 3 * len(names):]
        me = 2 * lax.axis_index("x") + lax.axis_index("y")
        for i, n in enumerate(names):
            w_ref, m_ref, v_ref = ins[3 * i:3 * i + 3]
            go_ref, d_ref, mo_ref, vo_ref = outs[4 * i:4 * i + 4]
            if n == "conv_dw_w":
                gv = jnp.zeros((CONV_WIDTH, 128), F32)
                for k in range(N_CHIPS):
                    gv = gv + jnp.where(me == k, g_ref[0:CONV_WIDTH, 128 * k:128 * (k + 1)], 0.0)
            elif n in VEC512:
                r0 = 32 + VEC512.index(n)
                gv = g_ref[r0:r0 + 1, :]
            else:
                r0 = 32 + len(VEC512) + 2 * VEC1024.index(n)
                gv = jnp.concatenate([g_ref[r0:r0 + 1, :], g_ref[r0 + 1:r0 + 2, :]], axis=1)
            go_ref[...] = gv
            d_ref[...], mo_ref[...], vo_ref[...] = _adamw_math(w_ref[...], gv, m_ref[...], v_ref[...])

    out_shape = [jax.ShapeDtypeStruct(params[n][0].shape, F32) for n in names for _ in range(4)]
    res = pl.pallas_call(
        body, name="adamw_small", out_shape=out_shape,
        in_specs=[VMEM_SPEC] * (1 + len(flat)), out_specs=[VMEM_SPEC] * len(out_shape),
        compiler_params=_params(),
    )(gsum, *flat)
    return {n: res[4 * i:4 * i + 4] for i, n in enumerate(names)}


REST = ("w_ffn_up", "w_ffn_down", "w_out", "w_conv_branch", "w_att_branch")
VEC512 = ("conv_dw_b", "conv_ln_g", "conv_ln_b")
VEC1024 = ("norm_mix_pre", "b_conv_branch", "norm_mix_post", "norm_ffn_pre", "norm_ffn_post")
PACK_ROWS = 48
LOSS_ROW = 47


def kernel(x, norm_mix_pre, w_in, conv_dw_w, conv_dw_b, conv_ln_g, conv_ln_b, w_conv_branch, b_conv_branch, w_att_branch, w_out, norm_mix_post, norm_ffn_pre, w_ffn_up, w_ffn_down, norm_ffn_post, loss_target, m_norm_mix_pre, m_w_in, m_conv_dw_w, m_conv_dw_b, m_conv_ln_g, m_conv_ln_b, m_w_conv_branch, m_b_conv_branch, m_w_att_branch, m_w_out, m_norm_mix_post, m_norm_ffn_pre, m_w_ffn_up, m_w_ffn_down, m_norm_ffn_post, v_norm_mix_pre, v_w_in, v_conv_dw_w, v_conv_dw_b, v_conv_ln_g, v_conv_ln_b, v_w_conv_branch, v_b_conv_branch, v_w_att_branch, v_w_out, v_norm_mix_post, v_norm_ffn_pre, v_w_ffn_up, v_w_ffn_down, v_norm_ffn_post):
    weights = dict(norm_mix_pre=norm_mix_pre, w_in=w_in, conv_dw_w=conv_dw_w, conv_dw_b=conv_dw_b, conv_ln_g=conv_ln_g, conv_ln_b=conv_ln_b, w_conv_branch=w_conv_branch, b_conv_branch=b_conv_branch, w_att_branch=w_att_branch, w_out=w_out, norm_mix_post=norm_mix_post, norm_ffn_pre=norm_ffn_pre, w_ffn_up=w_ffn_up, w_ffn_down=w_ffn_down, norm_ffn_post=norm_ffn_post)
    mom = dict(norm_mix_pre=m_norm_mix_pre, w_in=m_w_in, conv_dw_w=m_conv_dw_w, conv_dw_b=m_conv_dw_b, conv_ln_g=m_conv_ln_g, conv_ln_b=m_conv_ln_b, w_conv_branch=m_w_conv_branch, b_conv_branch=m_b_conv_branch, w_att_branch=m_w_att_branch, w_out=m_w_out, norm_mix_post=m_norm_mix_post, norm_ffn_pre=m_norm_ffn_pre, w_ffn_up=m_w_ffn_up, w_ffn_down=m_w_ffn_down, norm_ffn_post=m_norm_ffn_post)
    var = dict(norm_mix_pre=v_norm_mix_pre, w_in=v_w_in, conv_dw_w=v_conv_dw_w, conv_dw_b=v_conv_dw_b, conv_ln_g=v_conv_ln_g, conv_ln_b=v_conv_ln_b, w_conv_branch=v_w_conv_branch, b_conv_branch=v_b_conv_branch, w_att_branch=v_w_att_branch, w_out=v_w_out, norm_mix_post=v_norm_mix_post, norm_ffn_pre=v_norm_ffn_pre, w_ffn_up=v_w_ffn_up, w_ffn_down=v_w_ffn_down, norm_ffn_post=v_norm_ffn_post)
    order = list(weights)
    grads, deltas, new_m, new_v = {}, {}, {}, {}
    xs = x.reshape(SEQ, D_MODEL)
    tgt = loss_target.reshape(SEQ, D_MODEL)
    row = lambda a: a.reshape(1, -1)
    g1, g2, g3, g4 = (row(weights[n]) for n in ("norm_mix_pre", "norm_mix_post", "norm_ffn_pre", "norm_ffn_post"))
    ln_g, ln_b = row(conv_ln_g), row(conv_ln_b)

    def reduce_prepare(names, partial, tag):
        from_sib = sibling_exchange([partial[n] for n in names], "sibling_exchange_" + tag)
        return [add_halves(partial[n], r, "add_" + n) for n, r in zip(names, from_sib)]

    def reduce_finish(names, summed, from_chips, after, tag):
        halves = [sum_parts(s[1], r, after, "sum_" + n) for n, s, r in zip(names, summed, from_chips)]
        for n, a, b in zip(names, halves, sibling_swap(halves, "sibling_swap_" + tag)):
            grads[n], deltas[n], new_m[n], new_v[n] = adamw(weights[n], a, b, mom[n], var[n], "adamw_" + n)

    w_in_g, dw_g, *rest = all_gather_weights([w_in], conv_dw_w, [weights[n] for n in REST])
    w_dw_full = jnp.concatenate([dw_g[k] for k in range(N_CHIPS)], axis=1)
    rest_shapes = [weights[n].shape for n in REST]
    state, token = split_start("gather_start", rest, 3 * len(REST), _gather_copies(rest_shapes, 1))
    h1, ci, q, k, v, gc, ga = in_proj_fwd(xs, g1, w_in_g, token)
    u1, u3 = conv_fwd(ci, w_dw_full, row(conv_dw_b), ln_g, ln_b)
    att, rc = attn_fwd(q, k, v)
    rest = split_wait("gather_wait", state, [att], _gather_copies(rest_shapes, 1))
    ffn, mix_w = rest[:2], rest[2:]
    state_mix, _ = split_start("pass_mix_start", mix_w, 3 * 3, _gather_copies(rest_shapes[2:], 2))
    state, token = split_start("pass_ffn_start", ffn, 3 * 2, _gather_copies(rest_shapes[:2], 2))
    w_out_g, w_cb_g, w_ab_g = split_wait("pass_mix_wait", state_mix, [token], _gather_copies(rest_shapes[2:], 2))
    w_out_g = w_out_g.reshape(D_MODEL, D_MODEL)
    co, ao, merged, mix, x2, h2 = mix_fwd(u3, att, gc, ga, xs, w_cb_g, row(b_conv_branch), w_ab_g, w_out_g,
                                          g2, g3, token)
    w_up_g, w_down_g = split_wait("pass_ffn_wait", state, [h2], _gather_copies(rest_shapes[:2], 2))
    w_down_g = w_down_g.reshape(D_FF, D_MODEL)
    gate, up, act = ffn_up_fwd(h2, w_up_g)
    dff, dy, loss_parts, dg4 = ffn_down_loss(act, w_down_g, x2, tgt, g4)

    partial = {}
    dgu = ffn_act_bwd(dff, w_down_g, gate, up)
    partial["w_ffn_down"] = weight_grad(act, dff, "dw_ffn_down", False, tk=UP_SHARD)
    dx2, dmix, dg3, dg2 = ffn_in_bwd(dgu, w_up_g, x2, mix, dy, g3, g2)
    partial["w_ffn_up"] = weight_grad(h2, dgu, "dw_ffn_up", True)
    dco, dao, dg, du3, datt, dbcb = merge_bwd(dmix, w_out_g, gc, ga, co, ao, w_cb_g, w_ab_g)
    partial["w_out"] = weight_grad(merged, dmix, "dw_out", False, tk=512)
    partial["w_conv_branch"] = weight_grad(u3, dco, "dw_conv_branch", True)
    partial["w_att_branch"] = weight_grad(att, dao, "dw_att_branch", True)
    summed = reduce_prepare(REST, partial, "rest")
    state, token = scatter_start([s[0] for s in summed], "rest")
    dci, ddw, dbdw, dlng, dlnb = conv_bwd(du3, u1, ci, w_dw_full, ln_g, ln_b, token)
    dqkv = attn_bwd(q, k, v, datt, rc, token)
    from_chips = scatter_wait(state, [dci, dqkv], "rest")
    dproj = (dci, dqkv, dg)
    partial["w_in"] = weight_grad_in(h1, dproj)
    summed_in = reduce_prepare(("w_in",), partial, "w_in")
    state, token = scatter_start([summed_in[0][0]], "w_in")
    grad_x, dg1 = in_proj_bwd(dproj, w_in_g, xs, dx2, g1, token)
    reduce_finish(REST, summed, from_chips, token, "rest")
    from_chips_in = scatter_wait(state, [dg1] + [new_v[n] for n in REST], "w_in")
    reduce_finish(("w_in",), summed_in, from_chips_in, token, "w_in")

    v512 = dict(conv_dw_b=dbdw, conv_ln_g=dlng, conv_ln_b=dlnb)
    v1024 = dict(norm_mix_pre=dg1, b_conv_branch=dbcb, norm_mix_post=dg2, norm_ffn_pre=dg3, norm_ffn_post=dg4)
    gsum = small_all_reduce(ddw, v512, v1024, loss_parts)
    loss = gsum[LOSS_ROW, 0]
    as_rows = lambda n, a: a if n == "conv_dw_w" else a.reshape(1, -1)
    small_names = ("conv_dw_w",) + VEC512 + VEC1024
    small = adamw_small(gsum, {n: tuple(as_rows(n, d[n]) for d in (weights, mom, var)) for n in small_names})
    for n in small_names:
        grads[n], deltas[n], new_m[n], new_v[n] = (a.reshape(weights[n].shape) for a in small[n])

    return (loss, grad_x.reshape(1, SEQ, D_MODEL), *[grads[n] for n in order], *[deltas[n] for n in order],
            *[new_m[n] for n in order], *[new_v[n] for n in order])
```

```python
import jax
import jax.numpy as jnp
from jax import lax
from jax.experimental import pallas as pl
from jax.experimental.pallas import tpu as pltpu

F32 = jnp.float32
MM = jnp.bfloat16

SEQ = 2048
D_MODEL = 1024
CONV_DIM = 512
ATT_DIM = 512
CONV_WIDTH = 31
D_FF = 2816
IN_COLS = 2 * CONV_DIM + 3 * ATT_DIM + 2 * D_MODEL
N_CHIPS = 4
IN_SHARD = IN_COLS // N_CHIPS
UP_SHARD = 2 * D_FF // N_CHIPS
BR_SHARD = D_MODEL // N_CHIPS
EPS = 1e-6
ATT_SCALE = 0.125

TM = 256
GLU_ROWS = 256
TQ = 128
CONV_TILE = 64
CONV_WIN = CONV_TILE + 32
VMEM_LIMIT = 56 * 1024 * 1024

ADAM_LR = 0.001
ADAM_B1 = 0.9
ADAM_B2 = 0.999
ADAM_EPS = 1e-08
ADAM_WD = 0.01
ADAM_STEP = 10

MESH = pl.DeviceIdType.MESH
ANY = pl.BlockSpec(memory_space=pl.ANY)
VMEM_SPEC = pl.BlockSpec(memory_space=pltpu.VMEM)

NT_DIMS = (((1,), (1,)), ((), ()))
TN_DIMS = (((0,), (0,)), ((), ()))

IN_PIECES = (("ci", 0, 1024), ("q", 1024, 1536), ("k", 1536, 2048), ("v", 2048, 2560),
             ("gc", 2560, 3584), ("ga", 3584, 4608))


def _params(sem=None, vmem=VMEM_LIMIT):
    return pltpu.CompilerParams(dimension_semantics=sem, vmem_limit_bytes=vmem)


def _dot(a, b):
    return jnp.dot(a, b, preferred_element_type=F32)


def _dot_nt(a, b):
    return lax.dot_general(a, b, NT_DIMS, preferred_element_type=F32)


def _dot_tn(a, b):
    return lax.dot_general(a, b, TN_DIMS, preferred_element_type=F32)


def _sigmoid(x):
    return 1.0 / (1.0 + jnp.exp(-x))


def _rms(x):
    r = lax.rsqrt(jnp.mean(x * x, axis=-1, keepdims=True) + EPS)
    return x * r, r


def _rms_bwd(dy_g, n, r):
    return r * (dy_g - n * jnp.mean(dy_g * n, axis=-1, keepdims=True))


def _row_tile_spec(width, tm=TM):
    return pl.BlockSpec((tm, width), lambda i: (i, 0))


def _full_spec(shape):
    nd = len(shape)
    return pl.BlockSpec(shape, lambda *_: (0,) * nd)


def _weight_spec(shape):
    nd = len(shape)
    return pl.BlockSpec(shape, lambda *_: (0,) * nd, pipeline_mode=pl.Buffered(1))


def _acc_rows(ref, val, first):
    @pl.when(first)
    def _():
        ref[...] = val

    @pl.when(jnp.logical_not(first))
    def _():
        ref[...] += val


TOKEN_SPEC = pl.BlockSpec((8, 128), lambda *_: (0, 0))


def in_proj_fwd(x, g1, w_in_g, after):
    def body(x_ref, g_ref, w_ref, after_ref, h_ref, ci_ref, q_ref, k_ref, v_ref, gc_ref, ga_ref):
        n, _ = _rms(x_ref[...])
        h = (n * g_ref[...]).astype(MM)
        h_ref[...] = h
        outs = dict(ci=ci_ref, q=q_ref, k=k_ref, v=v_ref, gc=gc_ref, ga=ga_ref)
        for j in range(N_CHIPS):
            p = _dot(h, w_ref[j])
            g0 = j * IN_SHARD
            for name, s, e in IN_PIECES:
                lo, hi = max(s, g0), min(e, g0 + IN_SHARD)
                if lo < hi:
                    ref = outs[name]
                    part = p[:, lo - g0:hi - g0]
                    if name == "q":
                        part = part * ATT_SCALE
                    ref[:, lo - s:hi - s] = part.astype(ref.dtype)

    out_shape = [
        jax.ShapeDtypeStruct((SEQ, D_MODEL), MM),
        jax.ShapeDtypeStruct((SEQ, 2 * CONV_DIM), F32),
        jax.ShapeDtypeStruct((SEQ, ATT_DIM), MM),
        jax.ShapeDtypeStruct((SEQ, ATT_DIM), MM),
        jax.ShapeDtypeStruct((SEQ, ATT_DIM), MM),
        jax.ShapeDtypeStruct((SEQ, D_MODEL), F32),
        jax.ShapeDtypeStruct((SEQ, D_MODEL), F32),
    ]
    return pl.pallas_call(
        body, name="in_proj_fwd", grid=(SEQ // TM,), out_shape=out_shape,
        in_specs=[_row_tile_spec(D_MODEL), _full_spec((1, D_MODEL)), _weight_spec(w_in_g.shape), TOKEN_SPEC],
        out_specs=[_row_tile_spec(s.shape[1]) for s in out_shape],
        compiler_params=_params(("arbitrary",)),
    )(x, g1, w_in_g, after)


def _shifted_sum(win, terms):
    by_rot = {}
    for m, coef in terms:
        by_rot.setdefault(m % 8, []).append((m // 8, coef))
    acc = None
    n = win.shape[0]
    for rot in sorted(by_rot):
        shifted = win if rot == 0 else pltpu.roll(win, n - rot, 0)
        for a, coef in by_rot[rot]:
            t = coef * shifted[8 * a:8 * a + CONV_TILE, :]
            acc = t if acc is None else acc + t
    return acc


def _glu_into(ci_ref, upad_ref):
    upad_ref[0:32, :] = jnp.zeros((32, CONV_DIM), F32)

    def step(i, c):
        t0 = pl.multiple_of(i * GLU_ROWS, GLU_ROWS)
        a = ci_ref[pl.ds(t0, GLU_ROWS), 0:CONV_DIM]
        b = ci_ref[pl.ds(t0, GLU_ROWS), CONV_DIM:2 * CONV_DIM]
        upad_ref[pl.ds(t0 + 32, GLU_ROWS), :] = a * _sigmoid(b)
        return c

    lax.fori_loop(0, SEQ // GLU_ROWS, step, 0)


def _layernorm_parts(u1):
    mu = jnp.mean(u1, axis=-1, keepdims=True)
    xc = u1 - mu
    rstd = lax.rsqrt(jnp.mean(xc * xc, axis=-1, keepdims=True) + EPS)
    return xc * rstd, rstd


def conv_fwd(ci, w_dw, b_dw, ln_g, ln_b):
    def body(ci_ref, w_ref, b_ref, g_ref, bb_ref, u1_ref, u3_ref, upad_ref):
        _glu_into(ci_ref, upad_ref)

        def step(i, c):
            t0 = pl.multiple_of(i * CONV_TILE, CONV_TILE)
            win = upad_ref[pl.ds(t0, CONV_WIN), :]
            u1 = _shifted_sum(win, [(j + 2, w_ref[j:j + 1, :]) for j in range(CONV_WIDTH)]) + b_ref[...]
            u1_ref[pl.ds(t0, CONV_TILE), :] = u1
            xh, _ = _layernorm_parts(u1)
            u2 = xh * g_ref[...] + bb_ref[...]
            u3_ref[pl.ds(t0, CONV_TILE), :] = (u2 * _sigmoid(u2)).astype(MM)
            return c

        lax.fori_loop(0, SEQ // CONV_TILE, step, 0)

    return pl.pallas_call(
        body, name="conv_fwd",
        out_shape=[jax.ShapeDtypeStruct((SEQ, CONV_DIM), F32), jax.ShapeDtypeStruct((SEQ, CONV_DIM), MM)],
        in_specs=[VMEM_SPEC] * 5, out_specs=[VMEM_SPEC] * 2,
        scratch_shapes=[pltpu.VMEM((SEQ + 32, CONV_DIM), F32)],
        compiler_params=_params(),
    )(ci, w_dw, b_dw, ln_g, ln_b)


def _softplus(z):
    return jnp.maximum(z, 0.0) + jnp.log(1.0 + jnp.exp(-jnp.abs(z)))


def _cumsum_weights(suffix, with_total):
    n = 256 if with_total else 128
    r = lax.broadcasted_iota(jnp.int32, (128, n), 0)
    c = lax.broadcasted_iota(jnp.int32, (128, n), 1)
    tri = (r >= c) if suffix else (r <= c)
    return jnp.logical_or(tri, c >= 128).astype(MM)


NO_SCORE = -1e30
N_KB = SEQ // TQ


def _score_bias(lane, row, i, j):
    keep = jnp.logical_and(i >= 0, jnp.logical_or(j < i, lane < row))
    return jnp.where(keep, 0.0, NO_SCORE)


def _block_pipeline(n_stages, descending, step, on_query_block=None):
    n_lag = n_stages - 1
    none = jnp.int32(-1)

    def shift(cur, lag):
        step([cur] + [(lag[2 * s], lag[2 * s + 1]) for s in range(n_lag)])
        return (cur[0], cur[1]) + tuple(lag[:-2])

    def outer(i, lag):
        if on_query_block is not None:
            on_query_block(i)

        def inner(n, lag):
            return shift((i, i - n if descending else n), lag)
        return lax.fori_loop(0, i + 1, inner, lag)

    lag = lax.fori_loop(0, N_KB, outer, (none,) * (2 * n_lag))
    lax.fori_loop(0, n_lag, lambda n, lag: shift((none, none), lag), lag)


def _head_masks():
    lane = lax.broadcasted_iota(jnp.int32, (TQ, 128), 1)
    row = lax.broadcasted_iota(jnp.int32, (TQ, 128), 0)
    return lane, row, lane < 64


def _pick_head(x, head0, h):
    zero = jnp.zeros_like(x)
    return jnp.where(head0, x, zero) if h == 0 else jnp.where(head0, zero, x)


N_PAIRS = ATT_DIM // 128


def _split_heads(src_ref, dst_ref):
    _, _, head0 = _head_masks()

    def block(b, c):
        r0 = pl.multiple_of(b * TQ, TQ)
        d0 = pl.multiple_of(b * 2 * TQ, 2 * TQ)
        for p in range(N_PAIRS):
            x = src_ref[pl.ds(r0, TQ), 128 * p:128 * (p + 1)]
            for h in range(2):
                dst_ref[p, pl.ds(d0 + TQ * h, TQ), :] = _pick_head(x, head0, h)
        return c

    lax.fori_loop(0, N_KB, block, 0)


def attn_fwd(q, k, v):
    def body(q_ref, k_ref, v_ref, o_ref, rc_ref, acc_ref, r_ref, z_ref, spb_ref, ab_ref, qm_ref, vm_ref):
        lane, row, _ = _head_masks()
        w = _cumsum_weights(suffix=True, with_total=True)
        _split_heads(q_ref, qm_ref)
        _split_heads(v_ref, vm_ref)
        acc_ref[...] = jnp.zeros_like(acc_ref)
        r_ref[...] = jnp.zeros_like(r_ref)
        rc_ref[...] = jnp.zeros_like(rc_ref)
        z_ref[...] = jnp.full(z_ref.shape, NO_SCORE, F32)
        spb_ref[...] = jnp.zeros_like(spb_ref)
        ab_ref[...] = jnp.zeros_like(ab_ref)

        def step(pairs):
            (i1, j1), (i2, j2), (i3, j3) = pairs
            k1, q2, q3 = (pl.multiple_of(jnp.maximum(b, 0) * TQ, TQ) for b in (j1, i2, i3))
            q1, k3 = (pl.multiple_of(jnp.maximum(b, 0) * 2 * TQ, 2 * TQ) for b in (i1, j3))
            bias1 = _score_bias(lane, row, i1, j1)
            first2 = j2 == i2
            rc_rows = rc_ref[pl.ds(q2, TQ), :]
            for p in range(N_PAIRS):
                cols = slice(128 * p, 128 * (p + 1))
                kb = k_ref[pl.ds(k1, TQ), cols]
                acc_ref[pl.ds(q3, TQ), cols] += _dot(ab_ref[p], vm_ref[p, pl.ds(k3, 2 * TQ), :])
                for h in range(2):
                    hh = 2 * p + h
                    r = _dot(spb_ref[hh], w)
                    r_in = jnp.where(first2, 0.0, r_ref[hh])
                    ab_ref[p, :, 128 * h:128 * (h + 1)] = jnp.exp(z_ref[hh] - (r[:, :128] + r_in)).astype(MM)
                    rc_rows = jnp.where(jnp.logical_and(lane == 16 * hh + j2, i2 >= 0), r_in, rc_rows)
                    r_ref[hh] = r_in + r[:, 128:]
                    z = _dot_nt(qm_ref[p, pl.ds(q1 + TQ * h, TQ), :], kb) + bias1
                    z_ref[hh] = z
                    spb_ref[hh] = _softplus(z).astype(MM)
            rc_ref[pl.ds(q2, TQ), :] = rc_rows

        _block_pipeline(3, True, step)
        o_ref[...] = acc_ref[...].astype(MM)

    return pl.pallas_call(
        body, name="attn_fwd",
        out_shape=[jax.ShapeDtypeStruct((SEQ, ATT_DIM), MM), jax.ShapeDtypeStruct((SEQ, 128), F32)],
        in_specs=[VMEM_SPEC] * 3, out_specs=[VMEM_SPEC] * 2,
        scratch_shapes=[pltpu.VMEM((SEQ, ATT_DIM), F32), pltpu.VMEM((8, TQ, 128), F32),
                        pltpu.VMEM((8, TQ, 128), F32), pltpu.VMEM((8, TQ, 128), MM),
                        pltpu.VMEM((N_PAIRS, TQ, 256), MM), pltpu.VMEM((N_PAIRS, 2 * SEQ, 128), MM),
                        pltpu.VMEM((N_PAIRS, 2 * SEQ, 128), MM)],
        compiler_params=_params(),
    )(q, k, v)


def mix_fwd(u3, att, gc, ga, x, w_cb_g, b_cb, w_ab_g, w_out_g, g2, g3, after):
    def body(u_ref, a_ref, gc_ref, ga_ref, x_ref, wcb_ref, bcb_ref, wab_ref, wout_ref, g2_ref, g3_ref, after_ref,
             co_ref, ao_ref, mg_ref, mix_ref, x2_ref, h2_ref):
        u = u_ref[...]
        a = a_ref[...]
        co = jnp.concatenate([_dot(u, wcb_ref[j]) for j in range(N_CHIPS)], axis=1) + bcb_ref[...]
        ao = jnp.concatenate([_dot(a, wab_ref[j]) for j in range(N_CHIPS)], axis=1)
        co_ref[...] = co.astype(MM)
        ao_ref[...] = ao.astype(MM)
        merged = (_sigmoid(gc_ref[...]) * co + _sigmoid(ga_ref[...]) * ao).astype(MM)
        mg_ref[...] = merged
        mix = _dot(merged, wout_ref[...])
        mix_ref[...] = mix
        n2, _ = _rms(mix)
        x2 = x_ref[...] + n2 * g2_ref[...]
        x2_ref[...] = x2
        n3, _ = _rms(x2)
        h2_ref[...] = (n3 * g3_ref[...]).astype(MM)

    out_shape = [
        jax.ShapeDtypeStruct((SEQ, D_MODEL), MM), jax.ShapeDtypeStruct((SEQ, D_MODEL), MM),
        jax.ShapeDtypeStruct((SEQ, D_MODEL), MM), jax.ShapeDtypeStruct((SEQ, D_MODEL), F32),
        jax.ShapeDtypeStruct((SEQ, D_MODEL), F32), jax.ShapeDtypeStruct((SEQ, D_MODEL), MM),
    ]
    vec = _full_spec((1, D_MODEL))
    return pl.pallas_call(
        body, name="mix_fwd", grid=(SEQ // TM,), out_shape=out_shape,
        in_specs=[_row_tile_spec(CONV_DIM), _row_tile_spec(ATT_DIM), _row_tile_spec(D_MODEL),
                  _row_tile_spec(D_MODEL), _row_tile_spec(D_MODEL), _weight_spec(w_cb_g.shape), vec,
                  _weight_spec(w_ab_g.shape), _weight_spec(w_out_g.shape), vec, vec, TOKEN_SPEC],
        out_specs=[_row_tile_spec(D_MODEL)] * 6,
        compiler_params=_params(("arbitrary",)),
    )(u3, att, gc, ga, x, w_cb_g, b_cb, w_ab_g, w_out_g, g2, g3, after)


def ffn_up_fwd(h2, w_up_g):
    def body(h_ref, wg_ref, wu_ref, gate_ref, up_ref, act_ref):
        h = h_ref[...]
        gate = _dot(h, wg_ref[0])
        up = _dot(h, wu_ref[0])
        gate_ref[...] = gate.astype(MM)
        up_ref[...] = up.astype(MM)
        act_ref[...] = (gate * _sigmoid(gate) * up).astype(MM)

    tile = pl.BlockSpec((TM, UP_SHARD), lambda n, i: (i, n))
    act = jax.ShapeDtypeStruct((SEQ, D_FF), MM)
    return pl.pallas_call(
        body, name="ffn_up_fwd", grid=(2, SEQ // TM), out_shape=[act, act, act],
        in_specs=[pl.BlockSpec((TM, D_MODEL), lambda n, i: (i, 0)),
                  pl.BlockSpec((1, D_MODEL, UP_SHARD), lambda n, i: (n, 0, 0)),
                  pl.BlockSpec((1, D_MODEL, UP_SHARD), lambda n, i: (n + 2, 0, 0))],
        out_specs=[tile, tile, tile],
        compiler_params=_params(("arbitrary", "arbitrary")),
    )(h2, w_up_g, w_up_g)


def ffn_down_loss(act, w_down_g, x2, target, g4):
    def body(act_ref, wd_ref, x2_ref, t_ref, g_ref, dff_ref, dy_ref, loss_ref, dg_ref):
        ff = _dot(act_ref[...], wd_ref[...])
        n4, r4 = _rms(ff)
        g4v = g_ref[...]
        err = x2_ref[...] + n4 * g4v - t_ref[...]
        row_loss = jnp.mean(err * err, axis=-1, keepdims=True)
        loss_ref[...] = jnp.zeros((8, 128), F32) + 0.5 * jnp.sum(row_loss, axis=0, keepdims=True)
        dy = err * (1.0 / D_MODEL)
        dy_ref[...] = dy
        dff_ref[...] = _rms_bwd(dy * g4v, n4, r4).astype(MM)
        _acc_rows(dg_ref, jnp.sum(dy * n4, axis=0, keepdims=True), pl.program_id(0) == 0)

    nt = SEQ // TM
    vec = _full_spec((1, D_MODEL))
    return pl.pallas_call(
        body, name="ffn_down_loss", grid=(nt,),
        out_shape=(jax.ShapeDtypeStruct((SEQ, D_MODEL), MM), jax.ShapeDtypeStruct((SEQ, D_MODEL), F32),
                   jax.ShapeDtypeStruct((nt * 8, 128), F32), jax.ShapeDtypeStruct((1, D_MODEL), F32)),
        in_specs=[_row_tile_spec(D_FF), _weight_spec(w_down_g.shape), _row_tile_spec(D_MODEL),
                  _row_tile_spec(D_MODEL), vec],
        out_specs=[_row_tile_spec(D_MODEL), _row_tile_spec(D_MODEL),
                   pl.BlockSpec((8, 128), lambda i: (i, 0)), vec],
        compiler_params=_params(("arbitrary",)),
    )(act, w_down_g, x2, target, g4)


def ffn_act_bwd(dff, w_down_g, gate, up):
    def body(dff_ref, wd_ref, gate_ref, up_ref, dgu_ref):
        dact = _dot_nt(dff_ref[...], wd_ref[...])
        gate = gate_ref[...].astype(F32)
        sg = _sigmoid(gate)
        dgu_ref[:, 0:D_FF] = (dact * up_ref[...].astype(F32) * (sg * (1.0 + gate * (1.0 - sg)))).astype(MM)
        dgu_ref[:, D_FF:2 * D_FF] = (dact * (gate * sg)).astype(MM)

    return pl.pallas_call(
        body, name="ffn_act_bwd", grid=(SEQ // TM,),
        out_shape=jax.ShapeDtypeStruct((SEQ, 2 * D_FF), MM),
        in_specs=[_row_tile_spec(D_MODEL), _weight_spec(w_down_g.shape), _row_tile_spec(D_FF), _row_tile_spec(D_FF)],
        out_specs=_row_tile_spec(2 * D_FF),
        compiler_params=_params(("arbitrary",)),
    )(dff, w_down_g, gate, up)


def ffn_in_bwd(dgu, w_up_g, x2, mix, dy, g3, g2, after):
    def body(dgu_ref, w_ref, x2_ref, mix_ref, dy_ref, g3_ref, g2_ref, after_ref,
             dx2_ref, dmix_ref, dg3_ref, dg2_ref):
        dh2 = None
        for j in range(N_CHIPS):
            t = _dot_nt(dgu_ref[:, j * UP_SHARD:(j + 1) * UP_SHARD], w_ref[j])
            dh2 = t if dh2 is None else dh2 + t
        first = pl.program_id(0) == 0
        n3, r3 = _rms(x2_ref[...])
        dx2 = dy_ref[...] + _rms_bwd(dh2 * g3_ref[...], n3, r3)
        dx2_ref[...] = dx2
        _acc_rows(dg3_ref, jnp.sum(dh2 * n3, axis=0, keepdims=True), first)
        n2, r2 = _rms(mix_ref[...])
        dmix_ref[...] = _rms_bwd(dx2 * g2_ref[...], n2, r2).astype(MM)
        _acc_rows(dg2_ref, jnp.sum(dx2 * n2, axis=0, keepdims=True), first)

    vec = _full_spec((1, D_MODEL))
    return pl.pallas_call(
        body, name="ffn_in_bwd", grid=(SEQ // TM,),
        out_shape=(jax.ShapeDtypeStruct((SEQ, D_MODEL), F32), jax.ShapeDtypeStruct((SEQ, D_MODEL), MM),
                   jax.ShapeDtypeStruct((1, D_MODEL), F32), jax.ShapeDtypeStruct((1, D_MODEL), F32)),
        in_specs=[_row_tile_spec(2 * D_FF), _weight_spec(w_up_g.shape), _row_tile_spec(D_MODEL),
                  _row_tile_spec(D_MODEL), _row_tile_spec(D_MODEL), vec, vec, TOKEN_SPEC],
        out_specs=[_row_tile_spec(D_MODEL), _row_tile_spec(D_MODEL), vec, vec],
        compiler_params=_params(("arbitrary",)),
    )(dgu, w_up_g, x2, mix, dy, g3, g2, after)


def merge_bwd(dmix, w_out_g, gc, ga, co, ao, w_cb_g, w_ab_g, after):
    def body(dmix_ref, wout_ref, gc_ref, ga_ref, co_ref, ao_ref, wcb_ref, wab_ref, after_ref,
             dco_ref, dao_ref, dg_ref, du3_ref, datt_ref, dbcb_ref):
        dm = _dot_nt(dmix_ref[...], wout_ref[...])
        sgc = _sigmoid(gc_ref[...])
        sga = _sigmoid(ga_ref[...])
        dco = dm * sgc
        dao = dm * sga
        dg_ref[:, 0:D_MODEL] = (dm * co_ref[...].astype(F32) * (sgc * (1.0 - sgc))).astype(MM)
        dg_ref[:, D_MODEL:2 * D_MODEL] = (dm * ao_ref[...].astype(F32) * (sga * (1.0 - sga))).astype(MM)
        _acc_rows(dbcb_ref, jnp.sum(dco, axis=0, keepdims=True), pl.program_id(0) == 0)
        dco_ref[...] = dco.astype(MM)
        dao_ref[...] = dao.astype(MM)
        du3 = None
        datt = None
        for j in range(N_CHIPS):
            cols = slice(j * BR_SHARD, (j + 1) * BR_SHARD)
            t = _dot_nt(dco_ref[:, cols], wcb_ref[j])
            s = _dot_nt(dao_ref[:, cols], wab_ref[j])
            du3 = t if du3 is None else du3 + t
            datt = s if datt is None else datt + s
        du3_ref[...] = du3
        datt_ref[...] = datt.astype(MM)

    wide = _row_tile_spec(D_MODEL)
    return pl.pallas_call(
        body, name="merge_bwd", grid=(SEQ // TM,),
        out_shape=(jax.ShapeDtypeStruct((SEQ, D_MODEL), MM), jax.ShapeDtypeStruct((SEQ, D_MODEL), MM),
                   jax.ShapeDtypeStruct((SEQ, 2 * D_MODEL), MM),
                   jax.ShapeDtypeStruct((SEQ, CONV_DIM), F32), jax.ShapeDtypeStruct((SEQ, ATT_DIM), MM),
                   jax.ShapeDtypeStruct((1, D_MODEL), F32)),
        in_specs=[wide, _weight_spec(w_out_g.shape), wide, wide, wide, wide,
                  _weight_spec(w_cb_g.shape), _weight_spec(w_ab_g.shape), TOKEN_SPEC],
        out_specs=[wide, wide, _row_tile_spec(2 * D_MODEL), _row_tile_spec(CONV_DIM), _row_tile_spec(ATT_DIM),
                   _full_spec((1, D_MODEL))],
        compiler_params=_params(("arbitrary",)),
    )(dmix, w_out_g, gc, ga, co, ao, w_cb_g, w_ab_g, after)


def conv_bwd(du3, u1, ci, w_dw, ln_g, ln_b, after):
    def body(du3_ref, u1_ref, ci_ref, w_ref, g_ref, bb_ref, after_ref,
             dci_ref, dw_ref, dbdw_ref, dg_ref, db_ref, upad_ref, dpad_ref, dwacc_ref, vacc_ref):
        _glu_into(ci_ref, upad_ref)
        dpad_ref[SEQ:SEQ + 32, :] = jnp.zeros((32, CONV_DIM), F32)
        dwacc_ref[...] = jnp.zeros_like(dwacc_ref)
        vacc_ref[...] = jnp.zeros_like(vacc_ref)

        def fold8(t):
            s = t[0:8, :]
            for r in range(1, CONV_TILE // 8):
                s = s + t[8 * r:8 * r + 8, :]
            return s

        def pass1(i, c):
            t0 = pl.multiple_of(i * CONV_TILE, CONV_TILE)
            xh, rstd = _layernorm_parts(u1_ref[pl.ds(t0, CONV_TILE), :])
            gv = g_ref[...]
            u2 = xh * gv + bb_ref[...]
            s2 = _sigmoid(u2)
            du2 = du3_ref[pl.ds(t0, CONV_TILE), :] * (s2 * (1.0 + u2 * (1.0 - s2)))
            wv = du2 * gv
            du1 = rstd * (wv - jnp.mean(wv, axis=-1, keepdims=True)
                          - xh * jnp.mean(wv * xh, axis=-1, keepdims=True))
            dpad_ref[pl.ds(t0, CONV_TILE), :] = du1
            vacc_ref[0] += fold8(du2 * xh)
            vacc_ref[1] += fold8(du2)
            vacc_ref[2] += fold8(du1)
            win = upad_ref[pl.ds(t0, CONV_WIN), :]
            n = win.shape[0]
            for rot in range(8):
                shifted = win if rot == 0 else pltpu.roll(win, n - rot, 0)
                for a in range(5):
                    j = 8 * a + rot - 2
                    if 0 <= j < CONV_WIDTH:
                        dwacc_ref[j] += fold8(du1 * shifted[8 * a:8 * a + CONV_TILE, :])
            return c

        lax.fori_loop(0, SEQ // CONV_TILE, pass1, 0)

        def pass2(i, c):
            t0 = pl.multiple_of(i * CONV_TILE, CONV_TILE)
            win = dpad_ref[pl.ds(t0, CONV_WIN), :]
            du0 = _shifted_sum(win, [(30 - j, w_ref[j:j + 1, :]) for j in range(CONV_WIDTH)])
            a = ci_ref[pl.ds(t0, CONV_TILE), 0:CONV_DIM]
            sb = _sigmoid(ci_ref[pl.ds(t0, CONV_TILE), CONV_DIM:2 * CONV_DIM])
            dci_ref[pl.ds(t0, CONV_TILE), 0:CONV_DIM] = (du0 * sb).astype(MM)
            dci_ref[pl.ds(t0, CONV_TILE), CONV_DIM:2 * CONV_DIM] = (du0 * a * (sb * (1.0 - sb))).astype(MM)
            return c

        lax.fori_loop(0, SEQ // CONV_TILE, pass2, 0)

        for j in range(CONV_WIDTH):
            dw_ref[j:j + 1, :] = jnp.sum(dwacc_ref[j], axis=0, keepdims=True)
        dw_ref[CONV_WIDTH:32, :] = jnp.zeros((32 - CONV_WIDTH, CONV_DIM), F32)
        dg_ref[...] = jnp.sum(vacc_ref[0], axis=0, keepdims=True)
        db_ref[...] = jnp.sum(vacc_ref[1], axis=0, keepdims=True)
        dbdw_ref[...] = jnp.sum(vacc_ref[2], axis=0, keepdims=True)

    vec = jax.ShapeDtypeStruct((1, CONV_DIM), F32)
    return pl.pallas_call(
        body, name="conv_bwd",
        out_shape=(jax.ShapeDtypeStruct((SEQ, 2 * CONV_DIM), MM), jax.ShapeDtypeStruct((32, CONV_DIM), F32),
                   vec, vec, vec),
        in_specs=[VMEM_SPEC] * 7, out_specs=[VMEM_SPEC] * 5,
        scratch_shapes=[pltpu.VMEM((SEQ + 32, CONV_DIM), F32), pltpu.VMEM((SEQ + 32, CONV_DIM), F32),
                        pltpu.VMEM((CONV_WIDTH, 8, CONV_DIM), F32), pltpu.VMEM((3, 8, CONV_DIM), F32)],
        compiler_params=_params(),
    )(du3, u1, ci, w_dw, ln_g, ln_b, after)


def attn_bwd(q, k, v, datt, rc, after):
    def body(q_ref, k_ref, v_ref, do_ref, rc_ref, after_ref, dqkv_ref, dqa_ref, dka_ref, dva_ref, pc_ref, z_ref,
             sig1_ref, sig2_ref, g_ref, spb_ref, gb_ref, ar_ref, dzr_ref, dzc_ref, qm_ref, km_ref, dom_ref):
        lane, row, _ = _head_masks()
        _split_heads(q_ref, qm_ref)
        _split_heads(k_ref, km_ref)
        _split_heads(do_ref, dom_ref)
        for ref in (dqa_ref, dka_ref, dva_ref, pc_ref):
            ref[...] = jnp.zeros_like(ref)
        z_ref[...] = jnp.full(z_ref.shape, NO_SCORE, F32)
        for ref in (sig1_ref, sig2_ref, spb_ref, ar_ref, g_ref, gb_ref, dzr_ref, dzc_ref):
            ref[...] = jnp.zeros_like(ref)
        w_suffix = _cumsum_weights(suffix=True, with_total=False)
        w_prefix = _cumsum_weights(suffix=False, with_total=True)

        def step(pairs):
            (ia, ja), (ib, jb), (ic, jc), (id_, jd) = pairs
            ka, qb_, kb_, kc, qd, kd = (pl.multiple_of(jnp.maximum(b, 0) * TQ, TQ) for b in (ja, ib, jb, jc, id_, jd))
            qa2, qb2, qc2, qd2, kd2 = (pl.multiple_of(jnp.maximum(b, 0) * 2 * TQ, 2 * TQ)
                                       for b in (ia, ib, ic, id_, jd))
            bias_a = _score_bias(lane, row, ia, ja)
            rc_rows = rc_ref[pl.ds(qb_, TQ), :]
            first_c = jc == 0
            for p in range(N_PAIRS):
                cols = slice(128 * p, 128 * (p + 1))
                k_a = k_ref[pl.ds(ka, TQ), cols]
                v_b = v_ref[pl.ds(kb_, TQ), cols]
                dqa_ref[pl.ds(qd, TQ), cols] += _dot(dzc_ref[p], km_ref[p, pl.ds(kd2, 2 * TQ), :])
                dka_ref[pl.ds(kd, TQ), cols] += _dot_tn(dzr_ref[p], qm_ref[p, pl.ds(qd2, 2 * TQ), :])
                dva_ref[pl.ds(kc, TQ), cols] += _dot_tn(ar_ref[p], dom_ref[p, pl.ds(qc2, 2 * TQ), :])
                for h in range(2):
                    hh = 2 * p + h
                    rows = slice(TQ * h, TQ * (h + 1))
                    r = _dot(gb_ref[hh], w_prefix)
                    p_in = jnp.where(first_c, 0.0, pc_ref[hh])
                    dz = (g_ref[hh] - sig2_ref[hh] * (r[:, :128] + p_in)).astype(MM)
                    dzc_ref[p, :, rows] = dz
                    dzr_ref[p, rows, :] = dz
                    pc_ref[hh] = p_in + r[:, 128:]
                    r_in = jnp.sum(jnp.where(lane == 16 * hh + jb, rc_rows, 0.0), axis=1, keepdims=True)
                    a = jnp.exp(z_ref[hh] - (_dot(spb_ref[hh], w_suffix) + r_in))
                    g = _dot_nt(dom_ref[p, pl.ds(qb2 + TQ * h, TQ), :], v_b) * a
                    ar_ref[p, rows, :] = a.astype(MM)
                    g_ref[hh] = g
                    gb_ref[hh] = g.astype(MM)
                    sig2_ref[hh] = sig1_ref[hh]
                    z = _dot_nt(qm_ref[p, pl.ds(qa2 + TQ * h, TQ), :], k_a) + bias_a
                    sp = _softplus(z)
                    sig1_ref[hh] = jnp.exp(z - sp)
                    z_ref[hh] = z
                    spb_ref[hh] = sp.astype(MM)

        _block_pipeline(4, False, step)
        dqkv_ref[:, 0:ATT_DIM] = (dqa_ref[...] * ATT_SCALE).astype(MM)
        dqkv_ref[:, ATT_DIM:2 * ATT_DIM] = dka_ref[...].astype(MM)
        dqkv_ref[:, 2 * ATT_DIM:3 * ATT_DIM] = dva_ref[...].astype(MM)

    split = pltpu.VMEM((N_PAIRS, 2 * SEQ, 128), MM)
    return pl.pallas_call(
        body, name="attn_bwd", out_shape=jax.ShapeDtypeStruct((SEQ, 3 * ATT_DIM), MM),
        in_specs=[VMEM_SPEC] * 6, out_specs=VMEM_SPEC,
        scratch_shapes=[pltpu.VMEM((SEQ, ATT_DIM), F32)] * 3 + [pltpu.VMEM((8, TQ, 128), F32)] * 5
                       + [pltpu.VMEM((8, TQ, 128), MM)] * 2
                       + [pltpu.VMEM((N_PAIRS, 2 * TQ, 128), MM)] * 2 + [pltpu.VMEM((N_PAIRS, TQ, 256), MM)]
                       + [split] * 3,
        compiler_params=_params(),
    )(q, k, v, datt, rc, after)


DPROJ_PIECES = ((0, 1024), (1024, 2560), (2560, 4608))


def _dproj_segments(j):
    g0, g1 = j * IN_SHARD, (j + 1) * IN_SHARD
    segs = []
    for p, (s, e) in enumerate(DPROJ_PIECES):
        lo, hi = max(s, g0), min(e, g1)
        if lo < hi:
            segs.append((p, lo - s, lo - g0, hi - lo))
    return segs


def in_proj_bwd(pieces, w_in_g, x, dx2, g1, after):
    def body(p0_ref, p1_ref, p2_ref, w_ref, x_ref, dx2_ref, g_ref, after_ref, dx_ref, dg_ref):
        p_refs = (p0_ref, p1_ref, p2_ref)
        dh = None
        for j in range(N_CHIPS):
            for p, lo, off, width in _dproj_segments(j):
                t = _dot_nt(p_refs[p][:, lo:lo + width], w_ref[j, :, off:off + width])
                dh = t if dh is None else dh + t
        n1, r1 = _rms(x_ref[...])
        dx_ref[...] = dx2_ref[...] + _rms_bwd(dh * g_ref[...], n1, r1)
        _acc_rows(dg_ref, jnp.sum(dh * n1, axis=0, keepdims=True), pl.program_id(0) == 0)

    vec = _full_spec((1, D_MODEL))
    return pl.pallas_call(
        body, name="in_proj_bwd", grid=(SEQ // TM,),
        out_shape=[jax.ShapeDtypeStruct((SEQ, D_MODEL), F32), jax.ShapeDtypeStruct((1, D_MODEL), F32)],
        in_specs=[_row_tile_spec(p.shape[1]) for p in pieces]
                 + [_weight_spec(w_in_g.shape), _row_tile_spec(D_MODEL), _row_tile_spec(D_MODEL), vec, TOKEN_SPEC],
        out_specs=[_row_tile_spec(D_MODEL), vec],
        compiler_params=_params(("arbitrary",)),
    )(*pieces, w_in_g, x, dx2, g1, after)


def weight_grad_in(h1, pieces):
    kh = D_MODEL // 2

    def body(a_ref, p0_ref, p1_ref, p2_ref, o_ref):
        p_refs = (p0_ref, p1_ref, p2_ref)
        a = a_ref[...]
        for j in range(N_CHIPS):
            @pl.when(pl.program_id(1) == j)
            def _():
                for p, lo, off, width in _dproj_segments(j):
                    o_ref[0, 0, :, off:off + width] = _dot_tn(a, p_refs[p][:, lo:lo + width]).astype(MM)

    return pl.pallas_call(
        body, name="dw_in", grid=(2, N_CHIPS), out_shape=jax.ShapeDtypeStruct((N_CHIPS, 2, kh, IN_SHARD), MM),
        in_specs=[pl.BlockSpec((SEQ, kh), lambda h, j: (0, h))] + [_weight_spec(p.shape) for p in pieces],
        out_specs=pl.BlockSpec((1, 1, kh, IN_SHARD), lambda h, j: (j, h, 0, 0)),
        compiler_params=_params(("arbitrary", "arbitrary")),
    )(h1, *pieces)


def weight_grad(a, b, name, col_sharded, tk=None):
    kin, n = a.shape[1], b.shape[1]

    def body(a_ref, b_ref, o_ref):
        if col_sharded:
            o_ref[0, 0] = _dot_tn(a_ref[...], b_ref[...]).astype(MM)
        else:
            o_ref[...] = _dot_tn(a_ref[...], b_ref[...]).astype(MM)

    if col_sharded:
        kh, ns = kin // 2, n // N_CHIPS
        out = jax.ShapeDtypeStruct((N_CHIPS, 2, kh, ns), MM)
        grid = (2, N_CHIPS)
        in_specs = [pl.BlockSpec((SEQ, kh), lambda h, j: (0, h)), pl.BlockSpec((SEQ, ns), lambda h, j: (0, j))]
        out_spec = pl.BlockSpec((1, 1, kh, ns), lambda h, j: (j, h, 0, 0))
        sem = ("arbitrary", "arbitrary")
    else:
        out = jax.ShapeDtypeStruct((kin, n), MM)
        grid = (kin // tk,)
        in_specs = [pl.BlockSpec((SEQ, tk), lambda r: (0, r)), pl.BlockSpec((SEQ, n), lambda r: (0, 0))]
        out_spec = pl.BlockSpec((tk, n), lambda r: (r, 0))
        sem = ("arbitrary",)
    res = pl.pallas_call(
        body, name=name, grid=grid, out_shape=out, in_specs=in_specs, out_specs=out_spec,
        compiler_params=_params(sem),
    )(a, b)
    if not col_sharded:
        res = res.reshape(N_CHIPS, 2, kin // (2 * N_CHIPS), n)
    return res


def _place():
    x, y, c = lax.axis_index("x"), lax.axis_index("y"), lax.axis_index("c")
    chips = [(1 - x, y), (x, 1 - y), (1 - x, 1 - y)]
    return x, y, c, chips


def _rcopy(src, dst, send_sem, recv_sem, dev):
    return pltpu.make_async_remote_copy(src_ref=src, dst_ref=dst, send_sem=send_sem, recv_sem=recv_sem,
                                        device_id=dev, device_id_type=MESH)


class _Gather:
    def __init__(self, shapes, w, o, scratch):
        self.n, self.shapes, self.w, self.o = len(w), shapes, w, o
        self.send, self.recv, self.fsend, self.frecv, self.loc_in, self.loc_out = scratch[:6]
        self.raw, self.stage = scratch[6:6 + self.n], scratch[6 + self.n:]
        self.x, self.y, self.c, self.chips = _place()
        self.me = 2 * self.x + self.y
        self.sib = (self.x, self.y, 1 - self.c)
        self.pairs = [(j, t) for j in range(3) for t in range(self.n)]

    @staticmethod
    def scratch(shards):
        n = len(shards)
        sems = pltpu.SemaphoreType.DMA
        return ([sems((3 * n,)), sems((3 * n,)), sems((3 * n,)), sems((3 * n,)), sems((n,)), sems((n,))]
                + [pltpu.VMEM(s.shape, s.dtype) for s in shards] + [pltpu.VMEM(s.shape, MM) for s in shards])

    @staticmethod
    def out_shapes(shards):
        return [jax.ShapeDtypeStruct((N_CHIPS,) + s.shape, MM) for s in shards]

    def _half(self, t, k, cc):
        rh = self.shapes[t][0] // 2
        return self.o[t].at[k, pl.ds(cc * rh, rh), :]

    def _chip(self, j):
        cx, cy = self.chips[j]
        return 2 * cx + cy, (cx, cy, self.c)

    def local_in(self, t):
        return pltpu.make_async_copy(self.w[t], self.raw[t], self.loc_in.at[t])

    def local_out(self, t):
        return pltpu.make_async_copy(self.stage[t], self.o[t].at[self.me], self.loc_out.at[t])

    def first(self, j, t):
        rh = self.shapes[t][0] // 2
        i = j * self.n + t
        return _rcopy(self.stage[t].at[pl.ds(self.c * rh, rh), :], self._half(t, self.me, self.c),
                      self.send.at[i], self.recv.at[i], self._chip(j)[1])

    def arrived(self, j, t):
        k, dev = self._chip(j)
        i = j * self.n + t
        blk = self._half(t, k, self.c)
        return _rcopy(blk, blk, self.send.at[i], self.recv.at[i], dev)

    def passed(self, j, t, cc):
        i = j * self.n + t
        blk = self._half(t, self._chip(j)[0], cc)
        return _rcopy(blk, blk, self.fsend.at[i], self.frecv.at[i], self.sib)

    def start(self):
        for t in range(self.n):
            self.local_in(t).start()
        for t in range(self.n):
            self.local_in(t).wait()
            self.stage[t][...] = self.raw[t][...].astype(MM)
            self.local_out(t).start()
        for j, t in self.pairs:
            self.first(j, t).start()

    def forward(self):
        for j, t in self.pairs:
            self.arrived(j, t).wait_recv()
            self.passed(j, t, self.c).start()

    def finish(self):
        for j, t in self.pairs:
            self.passed(j, t, 1 - self.c).wait_recv()
        for j, t in self.pairs:
            self.first(j, t).wait_send()
            self.passed(j, t, self.c).wait_send()
        for t in range(self.n):
            self.local_out(t).wait()


def all_gather_weights(shards, small, later):
    n, m = len(shards), len(later)
    shapes = [s.shape for s in shards]

    def body(*refs):
        w = refs[:n]
        sm = refs[n]
        lw = refs[n + 1:n + 1 + m]
        o = refs[n + 1 + m:2 * n + 1 + m]
        osm = refs[2 * n + 1 + m]
        lo = refs[2 * n + 2 + m:2 * n + 2 + 2 * m]
        scratch = refs[2 * n + 2 + 2 * m:]
        ssend, srecv, sloc, lsem_in, lsem_out = scratch[:5]
        lraw, lstage = scratch[5:5 + m], scratch[5 + m:5 + 2 * m]
        g = _Gather(shapes, w, o, scratch[5 + 2 * m:])
        own = pltpu.make_async_copy(sm, osm.at[g.me], sloc)
        own.start()
        loads = [pltpu.make_async_copy(lw[t], lraw[t], lsem_in.at[t]) for t in range(m)]
        for cp in loads:
            cp.start()
        g.start()
        small_cps = [_rcopy(sm, osm.at[g.me], ssend.at[j], srecv.at[j], g._chip(j)[1]) for j in range(3)]
        for cp in small_cps:
            cp.start()
        places = []
        for t in range(m):
            loads[t].wait()
            lstage[t][...] = lraw[t][...].astype(MM)
            places.append(pltpu.make_async_copy(lstage[t], lo[t].at[g.me], lsem_out.at[t]))
            places[t].start()
        g.forward()
        g.finish()
        for j in range(3):
            k, dev = g._chip(j)
            _rcopy(sm, osm.at[k], ssend.at[j], srecv.at[j], dev).wait_recv()
            small_cps[j].wait_send()
        own.wait()
        for cp in places:
            cp.wait()

    out_shape = _Gather.out_shapes(shards)
    out_shape.append(jax.ShapeDtypeStruct((N_CHIPS,) + small.shape, small.dtype))
    out_shape += _Gather.out_shapes(later)
    sems = pltpu.SemaphoreType.DMA
    return pl.pallas_call(
        body, name="all_gather_weights", out_shape=out_shape,
        in_specs=[ANY] * (n + 1 + m), out_specs=[ANY] * (n + 1 + m),
        scratch_shapes=[sems((3,)), sems((3,)), sems, sems((m,)), sems((m,))]
                       + [pltpu.VMEM(s.shape, s.dtype) for s in later] + [pltpu.VMEM(s.shape, MM) for s in later]
                       + _Gather.scratch(shards),
        compiler_params=_params(),
    )(*shards, small, *later)


HBM_SPEC = pl.BlockSpec(memory_space=pltpu.HBM)
SEM_SPEC = pl.BlockSpec(memory_space=pltpu.SEMAPHORE)
DATAFLOW = pltpu.SideEffectType.DATAFLOW_SIDE_EFFECTING


def split_start(name, bufs, n_copies, copies):
    nb = len(bufs)

    def body(*refs):
        for cp in copies(refs[:nb], refs[nb], refs[nb + 1]):
            cp.start()
        token = refs[2 * nb + 2]
        token[...] = jnp.zeros_like(token)

    sems = [pltpu.SemaphoreType.DMA((n_copies,))] * 2
    res = pl.pallas_call(
        body, name=name,
        out_shape=sems + [pltpu.HBM(a.shape, a.dtype) for a in bufs] + [jax.ShapeDtypeStruct((8, 128), F32)],
        in_specs=[HBM_SPEC] * nb, out_specs=[SEM_SPEC] * 2 + [HBM_SPEC] * nb + [VMEM_SPEC],
        input_output_aliases={i: 2 + i for i in range(nb)},
        compiler_params=pltpu.CompilerParams(has_side_effects=DATAFLOW),
    )(*[pltpu.with_memory_space_constraint(a, pltpu.HBM) for a in bufs])
    return res[:-1], res[-1]


def split_wait(name, state, after, copies):
    sems, bufs = state[:2], state[2:]
    nb = len(bufs)

    def body(*refs):
        for cp in copies(refs[:nb], refs[nb], refs[nb + 1]):
            cp.wait_send()
            cp.wait_recv()

    return pl.pallas_call(
        body, name=name, out_shape=[pltpu.HBM(a.shape, a.dtype) for a in bufs],
        in_specs=[HBM_SPEC] * nb + [SEM_SPEC] * 2 + [ANY] * len(after), out_specs=[HBM_SPEC] * nb,
        input_output_aliases={i: i for i in range(nb)},
        compiler_params=pltpu.CompilerParams(has_side_effects=DATAFLOW),
    )(*bufs, *sems, *after)


def _scatter_copies(n):
    def copies(refs, send, recv):
        _, _, c, chips = _place()
        return [_rcopy(refs[t].at[2 * cx + cy], refs[n + t].at[j], send.at[3 * t + j], recv.at[3 * t + j], (cx, cy, c))
                for t in range(n) for j, (cx, cy) in enumerate(chips)]
    return copies


def scatter_start(parts, tag):
    lands = [lax.empty((3,) + p.shape[1:], p.dtype) for p in parts]
    return split_start("scatter_start_" + tag, list(parts) + lands, 3 * len(parts), _scatter_copies(len(parts)))


def scatter_wait(state, after, tag):
    n = (len(state) - 2) // 2
    return split_wait("scatter_wait_" + tag, state, after, _scatter_copies(n))[n:]


def _gather_copies(shapes, level):
    n = len(shapes)

    def copies(refs, send, recv):
        x, y, c, chips = _place()
        out = []
        for t in range(n):
            rh = shapes[t][0] // 2
            for j, (cx, cy) in enumerate(chips):
                k, dev = (2 * x + y, (cx, cy, c)) if level == 1 else (2 * cx + cy, (x, y, 1 - c))
                blk = refs[t].at[k, pl.ds(c * rh, rh), :]
                out.append(_rcopy(blk, blk, send.at[3 * t + j], recv.at[3 * t + j], dev))
        return out
    return copies


def _sibling_copies(n, other_half):
    def copies(refs, send, recv):
        x, y, c, _ = _place()
        return [_rcopy(refs[t].at[:, 1 - c] if other_half else refs[t], refs[n + t], send.at[t], recv.at[t],
                       (x, y, 1 - c)) for t in range(n)]
    return copies


def sibling_start(srcs, other_half, tag):
    lands = [lax.empty((a.shape[0],) + a.shape[2:] if other_half else a.shape, a.dtype) for a in srcs]
    return split_start("sibling_start_" + tag, list(srcs) + lands, len(srcs),
                       _sibling_copies(len(srcs), other_half))


def sibling_wait(state, after, other_half, tag):
    n = (len(state) - 2) // 2
    res = split_wait("sibling_wait_" + tag, state, after, _sibling_copies(n, other_half))
    return res[:n], res[n:]


def small_pack(ddw, v512, v1024, loss_parts):
    rows, width = PACK_ROWS, 512
    n512, n1024 = len(VEC512), len(VEC1024)

    def body(*refs):
        ddw_ref = refs[0]
        a_refs = refs[1:1 + n512]
        b_refs = refs[1 + n512:1 + n512 + n1024]
        lp_ref, o_ref, p_ref = refs[1 + n512 + n1024:]
        p_ref[...] = jnp.zeros_like(p_ref)
        p_ref[0:32, :] = ddw_ref[...]
        p_ref[LOSS_ROW:LOSS_ROW + 1, 0:128] = jnp.sum(lp_ref[...], axis=0, keepdims=True) * 0.125
        for i, r in enumerate(a_refs):
            p_ref[32 + i:33 + i, :] = r[...]
        for i, r in enumerate(b_refs):
            base = 32 + n512 + 2 * i
            p_ref[base:base + 1, :] = r[:, 0:512]
            p_ref[base + 1:base + 2, :] = r[:, 512:1024]
        x, y, c, _ = _place()
        o_ref[4 * x + 2 * y + c] = p_ref[...]

    n_in = 2 + n512 + n1024
    return pl.pallas_call(
        body, name="small_pack", out_shape=jax.ShapeDtypeStruct((8, rows, width), F32),
        in_specs=[VMEM_SPEC] * n_in, out_specs=VMEM_SPEC,
        scratch_shapes=[pltpu.VMEM((rows, width), F32)],
    )(ddw, *[v512[n] for n in VEC512], *[v1024[n] for n in VEC1024], loss_parts)


def _small_copies(refs, send, recv):
    x, y, c, _ = _place()
    mine = refs[0].at[4 * x + 2 * y + c]
    peers = [(1 - x if k & 4 else x, 1 - y if k & 2 else y, 1 - c if k & 1 else c) for k in range(1, 8)]
    return [_rcopy(mine, mine, send.at[i], recv.at[i], dev) for i, dev in enumerate(peers)]


def _row_block(r):
    for tr in (512, 352, 256, 128):
        if r % tr == 0:
            return tr
    return r


def add_halves(g, recv, name):
    _, _, r, w = g.shape
    tr = _row_block(r)

    def body(g_ref, r_ref, ob_ref, own_ref):
        k = pl.program_id(1)
        me = 2 * lax.axis_index("x") + lax.axis_index("y")
        t = g_ref[0, 0].astype(F32) + r_ref[0].astype(F32)
        ob_ref[0] = t.astype(MM)
        mine = jnp.where(k == me, t, 0.0)

        @pl.when(k == 0)
        def _():
            own_ref[...] = mine

        @pl.when(k != 0)
        def _():
            own_ref[...] += mine

    return pl.pallas_call(
        body, name=name, grid=(r // tr, N_CHIPS),
        in_specs=[pl.BlockSpec((1, 1, tr, w), lambda i, k: (k, lax.axis_index("c"), i, 0)),
                  pl.BlockSpec((1, tr, w), lambda i, k: (k, i, 0))],
        out_specs=[pl.BlockSpec((1, tr, w), lambda i, k: (k, i, 0)),
                   pl.BlockSpec((tr, w), lambda i, k: (i, 0))],
        out_shape=(jax.ShapeDtypeStruct((N_CHIPS, r, w), MM), jax.ShapeDtypeStruct((r, w), F32)),
        compiler_params=_params(("arbitrary", "arbitrary")),
    )(g, recv)


def sum_parts(own, rin, after, name):
    _, r, w = rin.shape
    tr = _row_block(r)

    def body(o_ref, r_ref, after_ref, out_ref):
        out_ref[...] = ((o_ref[...] + r_ref[0].astype(F32)) + r_ref[1].astype(F32)) + r_ref[2].astype(F32)

    return pl.pallas_call(
        body, name=name, grid=(r // tr,), out_shape=jax.ShapeDtypeStruct((r, w), F32),
        in_specs=[pl.BlockSpec((tr, w), lambda i: (i, 0)), pl.BlockSpec((3, tr, w), lambda i: (0, i, 0)),
                  _full_spec((8, 128))],
        out_specs=pl.BlockSpec((tr, w), lambda i: (i, 0)),
        compiler_params=_params(("arbitrary",)),
    )(own, rin, after)


def _adamw_math(w, g, m, v):
    mn = ADAM_B1 * m + (1.0 - ADAM_B1) * g
    vn = ADAM_B2 * v + (1.0 - ADAM_B2) * (g * g)
    m_hat = mn / (1.0 - ADAM_B1 ** ADAM_STEP)
    v_hat = vn / (1.0 - ADAM_B2 ** ADAM_STEP)
    return -ADAM_LR * (m_hat / (jnp.sqrt(v_hat) + ADAM_EPS) + ADAM_WD * w), mn, vn


def adamw(w, mine, other, m, v, name):
    r, c = w.shape
    rh = r // 2
    tr = _row_block(rh)
    if c >= 1024 and tr % 512 == 0:
        tr = 256
    nb = rh // tr

    def body(w_ref, a_ref, b_ref, m_ref, v_ref, go_ref, d_ref, mo_ref, vo_ref):
        gv = jnp.where(lax.axis_index("c") == pl.program_id(0), a_ref[...], b_ref[...])
        go_ref[...] = gv
        d_ref[...], mo_ref[...], vo_ref[...] = _adamw_math(w_ref[...], gv, m_ref[...], v_ref[...])

    def half(of_sibling):
        def index(h, i):
            owner = lax.axis_index("c")
            owner = 1 - owner if of_sibling else owner
            return jnp.where(h == owner, i, jnp.where(h < owner, 0, nb - 1)), 0
        return pl.BlockSpec((tr, c), index)

    spec = pl.BlockSpec((tr, c), lambda h, i: (h * nb + i, 0))
    out = jax.ShapeDtypeStruct((r, c), F32)
    return pl.pallas_call(
        body, name=name, grid=(2, nb), out_shape=(out, out, out, out),
        in_specs=[spec, half(False), half(True), spec, spec], out_specs=[spec] * 4,
        compiler_params=_params(("arbitrary", "arbitrary")),
    )(w, mine, other, m, v)


def adamw_small(packs, params):
    names = list(params)
    flat = [a for n in names for a in params[n]]

    def body(*refs):
        p_ref = refs[0]
        ins = refs[1:1 + 3 * len(names)]
        g_ref = refs[1 + 3 * len(names)]
        outs = refs[2 + 3 * len(names):]
        total = p_ref[0]
        for d in range(1, 8):
            total = total + p_ref[d]
        g_ref[...] = total
        me = 2 * lax.axis_index("x") + lax.axis_index("y")
        for i, n in enumerate(names):
            w_ref, m_ref, v_ref = ins[3 * i:3 * i + 3]
            go_ref, d_ref, mo_ref, vo_ref = outs[4 * i:4 * i + 4]
            if n == "conv_dw_w":
                gv = jnp.zeros((CONV_WIDTH, 128), F32)
                for k in range(N_CHIPS):
                    gv = gv + jnp.where(me == k, g_ref[0:CONV_WIDTH, 128 * k:128 * (k + 1)], 0.0)
            elif n in VEC512:
                r0 = 32 + VEC512.index(n)
                gv = g_ref[r0:r0 + 1, :]
            else:
                r0 = 32 + len(VEC512) + 2 * VEC1024.index(n)
                gv = jnp.concatenate([g_ref[r0:r0 + 1, :], g_ref[r0 + 1:r0 + 2, :]], axis=1)
            go_ref[...] = gv
            d_ref[...], mo_ref[...], vo_ref[...] = _adamw_math(w_ref[...], gv, m_ref[...], v_ref[...])

    out_shape = [jax.ShapeDtypeStruct(packs.shape[1:], F32)]
    out_shape += [jax.ShapeDtypeStruct(params[n][0].shape, F32) for n in names for _ in range(4)]
    res = pl.pallas_call(
        body, name="adamw_small", out_shape=out_shape,
        in_specs=[VMEM_SPEC] * (1 + len(flat)), out_specs=[VMEM_SPEC] * len(out_shape),
        compiler_params=_params(),
    )(packs, *flat)
    return res[0], {n: res[1 + 4 * i:5 + 4 * i] for i, n in enumerate(names)}


REST = ("w_ffn_up", "w_ffn_down", "w_out", "w_conv_branch", "w_att_branch")
VEC512 = ("conv_dw_b", "conv_ln_g", "conv_ln_b")
VEC1024 = ("norm_mix_pre", "b_conv_branch", "norm_mix_post", "norm_ffn_pre", "norm_ffn_post")
PACK_ROWS = 48
LOSS_ROW = 47


def kernel(x, norm_mix_pre, w_in, conv_dw_w, conv_dw_b, conv_ln_g, conv_ln_b, w_conv_branch, b_conv_branch, w_att_branch, w_out, norm_mix_post, norm_ffn_pre, w_ffn_up, w_ffn_down, norm_ffn_post, loss_target, m_norm_mix_pre, m_w_in, m_conv_dw_w, m_conv_dw_b, m_conv_ln_g, m_conv_ln_b, m_w_conv_branch, m_b_conv_branch, m_w_att_branch, m_w_out, m_norm_mix_post, m_norm_ffn_pre, m_w_ffn_up, m_w_ffn_down, m_norm_ffn_post, v_norm_mix_pre, v_w_in, v_conv_dw_w, v_conv_dw_b, v_conv_ln_g, v_conv_ln_b, v_w_conv_branch, v_b_conv_branch, v_w_att_branch, v_w_out, v_norm_mix_post, v_norm_ffn_pre, v_w_ffn_up, v_w_ffn_down, v_norm_ffn_post):
    weights = dict(norm_mix_pre=norm_mix_pre, w_in=w_in, conv_dw_w=conv_dw_w, conv_dw_b=conv_dw_b, conv_ln_g=conv_ln_g, conv_ln_b=conv_ln_b, w_conv_branch=w_conv_branch, b_conv_branch=b_conv_branch, w_att_branch=w_att_branch, w_out=w_out, norm_mix_post=norm_mix_post, norm_ffn_pre=norm_ffn_pre, w_ffn_up=w_ffn_up, w_ffn_down=w_ffn_down, norm_ffn_post=norm_ffn_post)
    mom = dict(norm_mix_pre=m_norm_mix_pre, w_in=m_w_in, conv_dw_w=m_conv_dw_w, conv_dw_b=m_conv_dw_b, conv_ln_g=m_conv_ln_g, conv_ln_b=m_conv_ln_b, w_conv_branch=m_w_conv_branch, b_conv_branch=m_b_conv_branch, w_att_branch=m_w_att_branch, w_out=m_w_out, norm_mix_post=m_norm_mix_post, norm_ffn_pre=m_norm_ffn_pre, w_ffn_up=m_w_ffn_up, w_ffn_down=m_w_ffn_down, norm_ffn_post=m_norm_ffn_post)
    var = dict(norm_mix_pre=v_norm_mix_pre, w_in=v_w_in, conv_dw_w=v_conv_dw_w, conv_dw_b=v_conv_dw_b, conv_ln_g=v_conv_ln_g, conv_ln_b=v_conv_ln_b, w_conv_branch=v_w_conv_branch, b_conv_branch=v_b_conv_branch, w_att_branch=v_w_att_branch, w_out=v_w_out, norm_mix_post=v_norm_mix_post, norm_ffn_pre=v_norm_ffn_pre, w_ffn_up=v_w_ffn_up, w_ffn_down=v_w_ffn_down, norm_ffn_post=v_norm_ffn_post)
    order = list(weights)
    grads, deltas, new_m, new_v = {}, {}, {}, {}
    xs = x.reshape(SEQ, D_MODEL)
    tgt = loss_target.reshape(SEQ, D_MODEL)
    row = lambda a: a.reshape(1, -1)
    g1, g2, g3, g4 = (row(weights[n]) for n in ("norm_mix_pre", "norm_mix_post", "norm_ffn_pre", "norm_ffn_post"))
    ln_g, ln_b = row(conv_ln_g), row(conv_ln_b)

    summed, from_chips = {}, {}

    def core_sums(names, state, after, tag):
        own, from_sibling = sibling_wait(state, after, True, tag)
        for n, g, r in zip(names, own, from_sibling):
            summed[n] = add_halves(g, r, "add_" + n)

    def chip_sums(names, after):
        return [sum_parts(summed[n][1], from_chips[n], after, "sum_" + n) for n in names]

    def optimize(names, state, after, tag):
        mine, other = sibling_wait(state, after, False, tag)
        for n, a, b in zip(names, mine, other):
            grads[n], deltas[n], new_m[n], new_v[n] = adamw(weights[n], a, b, mom[n], var[n], "adamw_" + n)

    w_in_g, dw_g, *rest = all_gather_weights([w_in], conv_dw_w, [weights[n] for n in REST])
    w_dw_full = jnp.concatenate([dw_g[k] for k in range(N_CHIPS)], axis=1)
    rest_shapes = [weights[n].shape for n in REST]
    state, token = split_start("gather_start", rest, 3 * len(REST), _gather_copies(rest_shapes, 1))
    h1, ci, q, k, v, gc, ga = in_proj_fwd(xs, g1, w_in_g, token)
    u1, u3 = conv_fwd(ci, w_dw_full, row(conv_dw_b), ln_g, ln_b)
    att, rc = attn_fwd(q, k, v)
    rest = split_wait("gather_wait", state, [att], _gather_copies(rest_shapes, 1))
    ffn, mix_w = rest[:2], rest[2:]
    state_mix, _ = split_start("pass_mix_start", mix_w, 3 * 3, _gather_copies(rest_shapes[2:], 2))
    state, token = split_start("pass_ffn_start", ffn, 3 * 2, _gather_copies(rest_shapes[:2], 2))
    w_out_g, w_cb_g, w_ab_g = split_wait("pass_mix_wait", state_mix, [token], _gather_copies(rest_shapes[2:], 2))
    w_out_g = w_out_g.reshape(D_MODEL, D_MODEL)
    co, ao, merged, mix, x2, h2 = mix_fwd(u3, att, gc, ga, xs, w_cb_g, row(b_conv_branch), w_ab_g, w_out_g,
                                          g2, g3, token)
    w_up_g, w_down_g = split_wait("pass_ffn_wait", state, [h2], _gather_copies(rest_shapes[:2], 2))
    w_down_g = w_down_g.reshape(D_FF, D_MODEL)
    gate, up, act = ffn_up_fwd(h2, w_up_g)
    dff, dy, loss_parts, dg4 = ffn_down_loss(act, w_down_g, x2, tgt, g4)

    dgu = ffn_act_bwd(dff, w_down_g, gate, up)
    to_down, token = sibling_start([weight_grad(act, dff, "dw_ffn_down", False, tk=UP_SHARD)], True, "dw_ffn_down")
    dx2, dmix, dg3, dg2 = ffn_in_bwd(dgu, w_up_g, x2, mix, dy, g3, g2, token)
    to_up, token = sibling_start([weight_grad(h2, dgu, "dw_ffn_up", True)], True, "dw_ffn_up")
    dco, dao, dg, du3, datt, dbcb = merge_bwd(dmix, w_out_g, gc, ga, co, ao, w_cb_g, w_ab_g, token)
    mix_grads = [weight_grad(merged, dmix, "dw_out", False, tk=512), weight_grad(u3, dco, "dw_conv_branch", True),
                 weight_grad(att, dao, "dw_att_branch", True)]
    to_mix, token = sibling_start(mix_grads, True, "dw_mix")
    core_sums(("w_ffn_down",), to_down, [token], "dw_ffn_down")
    core_sums(("w_ffn_up",), to_up, [summed["w_ffn_down"][1]], "dw_ffn_up")
    core_sums(REST[2:], to_mix, [summed["w_ffn_up"][1]], "dw_mix")
    state, token = scatter_start([summed[n][0] for n in REST], "rest")
    dci, ddw, dbdw, dlng, dlnb = conv_bwd(du3, u1, ci, w_dw_full, ln_g, ln_b, token)
    dqkv = attn_bwd(q, k, v, datt, rc, token)
    from_chips.update(zip(REST, scatter_wait(state, [dci, dqkv], "rest")))
    dproj = (dci, dqkv, dg)
    to_in, token = sibling_start([weight_grad_in(h1, dproj)], True, "dw_in")
    grad_x, dg1 = in_proj_bwd(dproj, w_in_g, xs, dx2, g1, token)
    v512 = dict(conv_dw_b=dbdw, conv_ln_g=dlng, conv_ln_b=dlnb)
    v1024 = dict(norm_mix_pre=dg1, b_conv_branch=dbcb, norm_mix_post=dg2, norm_ffn_pre=dg3, norm_ffn_post=dg4)
    packs, token = split_start("small_start", [small_pack(ddw, v512, v1024, loss_parts)], 7, _small_copies)
    core_sums(("w_in",), to_in, [token], "dw_in")
    state, token = scatter_start([summed["w_in"][0]], "w_in")
    swap_up, token = sibling_start(chip_sums(REST[:1], token), False, "sum_ffn_up")
    swap_rest, token = sibling_start(chip_sums(REST[1:], token), False, "sum_rest")
    optimize(REST[:1], swap_up, [token], "sum_ffn_up")
    optimize(REST[1:], swap_rest, [new_v["w_ffn_up"]], "sum_rest")
    from_chips["w_in"], = scatter_wait(state, [new_v[n] for n in REST], "w_in")
    swap_in, token = sibling_start(chip_sums(("w_in",), token), False, "sum_w_in")
    packs, = split_wait("small_wait", packs, [token], _small_copies)
    as_rows = lambda n, a: a if n == "conv_dw_w" else a.reshape(1, -1)
    small_names = ("conv_dw_w",) + VEC512 + VEC1024
    gsum, small = adamw_small(packs, {n: tuple(as_rows(n, d[n]) for d in (weights, mom, var)) for n in small_names})
    loss = gsum[LOSS_ROW, 0]
    optimize(("w_in",), swap_in, [gsum], "sum_w_in")
    for n in small_names:
        grads[n], deltas[n], new_m[n], new_v[n] = (a.reshape(weights[n].shape) for a in small[n])

    return (loss, grad_x.reshape(1, SEQ, D_MODEL), *[grads[n] for n in order], *[deltas[n] for n in order],
            *[new_m[n] for n in order], *[new_v[n] for n in order])
```

```python
import jax
import jax.numpy as jnp
from jax import lax
from jax.experimental import pallas as pl
from jax.experimental.pallas import tpu as pltpu

F32 = jnp.float32
MM = jnp.bfloat16

SEQ = 2048
D_MODEL = 1024
CONV_DIM = 512
ATT_DIM = 512
CONV_WIDTH = 31
D_FF = 2816
IN_COLS = 2 * CONV_DIM + 3 * ATT_DIM + 2 * D_MODEL
N_CHIPS = 4
IN_SHARD = IN_COLS // N_CHIPS
UP_SHARD = 2 * D_FF // N_CHIPS
BR_SHARD = D_MODEL // N_CHIPS
EPS = 1e-6
ATT_SCALE = 0.125

TM = 256
GLU_ROWS = 256
TQ = 128
CONV_TILE = 64
CONV_WIN = CONV_TILE + 32
VMEM_LIMIT = 56 * 1024 * 1024

ADAM_LR = 0.001
ADAM_B1 = 0.9
ADAM_B2 = 0.999
ADAM_EPS = 1e-08
ADAM_WD = 0.01
ADAM_STEP = 10

MESH = pl.DeviceIdType.MESH
ANY = pl.BlockSpec(memory_space=pl.ANY)
VMEM_SPEC = pl.BlockSpec(memory_space=pltpu.VMEM)

NT_DIMS = (((1,), (1,)), ((), ()))
TN_DIMS = (((0,), (0,)), ((), ()))

IN_PIECES = (("ci", 0, 1024), ("q", 1024, 1536), ("k", 1536, 2048), ("v", 2048, 2560),
             ("gc", 2560, 3584), ("ga", 3584, 4608))


def _params(sem=None, vmem=VMEM_LIMIT):
    return pltpu.CompilerParams(dimension_semantics=sem, vmem_limit_bytes=vmem)


def _dot(a, b):
    return jnp.dot(a, b, preferred_element_type=F32)


def _dot_nt(a, b):
    return lax.dot_general(a, b, NT_DIMS, preferred_element_type=F32)


def _dot_tn(a, b):
    return lax.dot_general(a, b, TN_DIMS, preferred_element_type=F32)


def _sigmoid(x):
    return 1.0 / (1.0 + jnp.exp(-x))


def _rms(x):
    r = lax.rsqrt(jnp.mean(x * x, axis=-1, keepdims=True) + EPS)
    return x * r, r


def _rms_bwd(dy_g, n, r):
    return r * (dy_g - n * jnp.mean(dy_g * n, axis=-1, keepdims=True))


def _row_tile_spec(width, tm=TM):
    return pl.BlockSpec((tm, width), lambda i: (i, 0))


def _full_spec(shape):
    nd = len(shape)
    return pl.BlockSpec(shape, lambda *_: (0,) * nd)


def _weight_spec(shape):
    nd = len(shape)
    return pl.BlockSpec(shape, lambda *_: (0,) * nd, pipeline_mode=pl.Buffered(1))


def _acc_rows(ref, val, first):
    @pl.when(first)
    def _():
        ref[...] = val

    @pl.when(jnp.logical_not(first))
    def _():
        ref[...] += val


TOKEN_SPEC = pl.BlockSpec((8, 128), lambda *_: (0, 0))


def in_proj_fwd(x, g1, w_in_g, after):
    def body(x_ref, g_ref, w_ref, after_ref, h_ref, ci_ref, q_ref, k_ref, v_ref, gc_ref, ga_ref):
        n, _ = _rms(x_ref[...])
        h = (n * g_ref[...]).astype(MM)
        h_ref[...] = h
        outs = dict(ci=ci_ref, q=q_ref, k=k_ref, v=v_ref, gc=gc_ref, ga=ga_ref)
        for j in range(N_CHIPS):
            p = _dot(h, w_ref[j])
            g0 = j * IN_SHARD
            for name, s, e in IN_PIECES:
                lo, hi = max(s, g0), min(e, g0 + IN_SHARD)
                if lo < hi:
                    ref = outs[name]
                    part = p[:, lo - g0:hi - g0]
                    if name == "q":
                        part = part * ATT_SCALE
                    ref[:, lo - s:hi - s] = part.astype(ref.dtype)

    out_shape = [
        jax.ShapeDtypeStruct((SEQ, D_MODEL), MM),
        jax.ShapeDtypeStruct((SEQ, 2 * CONV_DIM), F32),
        jax.ShapeDtypeStruct((SEQ, ATT_DIM), MM),
        jax.ShapeDtypeStruct((SEQ, ATT_DIM), MM),
        jax.ShapeDtypeStruct((SEQ, ATT_DIM), MM),
        jax.ShapeDtypeStruct((SEQ, D_MODEL), F32),
        jax.ShapeDtypeStruct((SEQ, D_MODEL), F32),
    ]
    return pl.pallas_call(
        body, name="in_proj_fwd", grid=(SEQ // TM,), out_shape=out_shape,
        in_specs=[_row_tile_spec(D_MODEL), _full_spec((1, D_MODEL)), _weight_spec(w_in_g.shape), TOKEN_SPEC],
        out_specs=[_row_tile_spec(s.shape[1]) for s in out_shape],
        compiler_params=_params(("arbitrary",)),
    )(x, g1, w_in_g, after)


def _shifted_sum(win, terms):
    by_rot = {}
    for m, coef in terms:
        by_rot.setdefault(m % 8, []).append((m // 8, coef))
    acc = None
    n = win.shape[0]
    for rot in sorted(by_rot):
        shifted = win if rot == 0 else pltpu.roll(win, n - rot, 0)
        for a, coef in by_rot[rot]:
            t = coef * shifted[8 * a:8 * a + CONV_TILE, :]
            acc = t if acc is None else acc + t
    return acc


def _glu_into(ci_ref, upad_ref):
    upad_ref[0:32, :] = jnp.zeros((32, CONV_DIM), F32)

    def step(i, c):
        t0 = pl.multiple_of(i * GLU_ROWS, GLU_ROWS)
        a = ci_ref[pl.ds(t0, GLU_ROWS), 0:CONV_DIM]
        b = ci_ref[pl.ds(t0, GLU_ROWS), CONV_DIM:2 * CONV_DIM]
        upad_ref[pl.ds(t0 + 32, GLU_ROWS), :] = a * _sigmoid(b)
        return c

    lax.fori_loop(0, SEQ // GLU_ROWS, step, 0)


def _layernorm_parts(u1):
    mu = jnp.mean(u1, axis=-1, keepdims=True)
    xc = u1 - mu
    rstd = lax.rsqrt(jnp.mean(xc * xc, axis=-1, keepdims=True) + EPS)
    return xc * rstd, rstd


def conv_fwd(ci, w_dw, b_dw, ln_g, ln_b):
    def body(ci_ref, w_ref, b_ref, g_ref, bb_ref, u1_ref, u3_ref, upad_ref):
        _glu_into(ci_ref, upad_ref)

        def step(i, c):
            t0 = pl.multiple_of(i * CONV_TILE, CONV_TILE)
            win = upad_ref[pl.ds(t0, CONV_WIN), :]
            u1 = _shifted_sum(win, [(j + 2, w_ref[j:j + 1, :]) for j in range(CONV_WIDTH)]) + b_ref[...]
            u1_ref[pl.ds(t0, CONV_TILE), :] = u1
            xh, _ = _layernorm_parts(u1)
            u2 = xh * g_ref[...] + bb_ref[...]
            u3_ref[pl.ds(t0, CONV_TILE), :] = (u2 * _sigmoid(u2)).astype(MM)
            return c

        lax.fori_loop(0, SEQ // CONV_TILE, step, 0)

    return pl.pallas_call(
        body, name="conv_fwd",
        out_shape=[jax.ShapeDtypeStruct((SEQ, CONV_DIM), F32), jax.ShapeDtypeStruct((SEQ, CONV_DIM), MM)],
        in_specs=[VMEM_SPEC] * 5, out_specs=[VMEM_SPEC] * 2,
        scratch_shapes=[pltpu.VMEM((SEQ + 32, CONV_DIM), F32)],
        compiler_params=_params(),
    )(ci, w_dw, b_dw, ln_g, ln_b)


def _softplus(z):
    return jnp.maximum(z, 0.0) + jnp.log(1.0 + jnp.exp(-jnp.abs(z)))


def _cumsum_weights(suffix, with_total):
    n = 256 if with_total else 128
    r = lax.broadcasted_iota(jnp.int32, (128, n), 0)
    c = lax.broadcasted_iota(jnp.int32, (128, n), 1)
    tri = (r >= c) if suffix else (r <= c)
    return jnp.logical_or(tri, c >= 128).astype(MM)


NO_SCORE = -1e30
N_KB = SEQ // TQ


def _score_bias(lane, row, i, j):
    keep = jnp.logical_and(i >= 0, jnp.logical_or(j < i, lane < row))
    return jnp.where(keep, 0.0, NO_SCORE)


def _block_pipeline(n_stages, descending, step, on_query_block=None):
    n_lag = n_stages - 1
    none = jnp.int32(-1)

    def shift(cur, lag):
        step([cur] + [(lag[2 * s], lag[2 * s + 1]) for s in range(n_lag)])
        return (cur[0], cur[1]) + tuple(lag[:-2])

    def outer(i, lag):
        if on_query_block is not None:
            on_query_block(i)

        def inner(n, lag):
            return shift((i, i - n if descending else n), lag)
        return lax.fori_loop(0, i + 1, inner, lag)

    lag = lax.fori_loop(0, N_KB, outer, (none,) * (2 * n_lag))
    lax.fori_loop(0, n_lag, lambda n, lag: shift((none, none), lag), lag)


def _head_masks():
    lane = lax.broadcasted_iota(jnp.int32, (TQ, 128), 1)
    row = lax.broadcasted_iota(jnp.int32, (TQ, 128), 0)
    return lane, row, lane < 64


def _pick_head(x, head0, h):
    zero = jnp.zeros_like(x)
    return jnp.where(head0, x, zero) if h == 0 else jnp.where(head0, zero, x)


N_PAIRS = ATT_DIM // 128


def _split_heads(src_ref, dst_ref):
    _, _, head0 = _head_masks()

    def block(b, c):
        r0 = pl.multiple_of(b * TQ, TQ)
        d0 = pl.multiple_of(b * 2 * TQ, 2 * TQ)
        for p in range(N_PAIRS):
            x = src_ref[pl.ds(r0, TQ), 128 * p:128 * (p + 1)]
            for h in range(2):
                dst_ref[p, pl.ds(d0 + TQ * h, TQ), :] = _pick_head(x, head0, h)
        return c

    lax.fori_loop(0, N_KB, block, 0)


def attn_fwd(q, k, v):
    def body(q_ref, k_ref, v_ref, o_ref, rc_ref, acc_ref, r_ref, z_ref, spb_ref, ab_ref, qm_ref, vm_ref):
        lane, row, _ = _head_masks()
        w = _cumsum_weights(suffix=True, with_total=True)
        _split_heads(q_ref, qm_ref)
        _split_heads(v_ref, vm_ref)
        acc_ref[...] = jnp.zeros_like(acc_ref)
        r_ref[...] = jnp.zeros_like(r_ref)
        rc_ref[...] = jnp.zeros_like(rc_ref)
        z_ref[...] = jnp.full(z_ref.shape, NO_SCORE, F32)
        spb_ref[...] = jnp.zeros_like(spb_ref)
        ab_ref[...] = jnp.zeros_like(ab_ref)

        def step(pairs):
            (i1, j1), (i2, j2), (i3, j3) = pairs
            k1, q2, q3 = (pl.multiple_of(jnp.maximum(b, 0) * TQ, TQ) for b in (j1, i2, i3))
            q1, k3 = (pl.multiple_of(jnp.maximum(b, 0) * 2 * TQ, 2 * TQ) for b in (i1, j3))
            bias1 = _score_bias(lane, row, i1, j1)
            first2 = j2 == i2
            rc_rows = rc_ref[pl.ds(q2, TQ), :]
            for p in range(N_PAIRS):
                cols = slice(128 * p, 128 * (p + 1))
                kb = k_ref[pl.ds(k1, TQ), cols]
                acc_ref[pl.ds(q3, TQ), cols] += _dot(ab_ref[p], vm_ref[p, pl.ds(k3, 2 * TQ), :])
                for h in range(2):
                    hh = 2 * p + h
                    r = _dot(spb_ref[hh], w)
                    r_in = jnp.where(first2, 0.0, r_ref[hh])
                    ab_ref[p, :, 128 * h:128 * (h + 1)] = jnp.exp(z_ref[hh] - (r[:, :128] + r_in)).astype(MM)
                    rc_rows = jnp.where(jnp.logical_and(lane == 16 * hh + j2, i2 >= 0), r_in, rc_rows)
                    r_ref[hh] = r_in + r[:, 128:]
                    z = _dot_nt(qm_ref[p, pl.ds(q1 + TQ * h, TQ), :], kb) + bias1
                    z_ref[hh] = z
                    spb_ref[hh] = _softplus(z).astype(MM)
            rc_ref[pl.ds(q2, TQ), :] = rc_rows

        _block_pipeline(3, True, step)
        o_ref[...] = acc_ref[...].astype(MM)

    return pl.pallas_call(
        body, name="attn_fwd",
        out_shape=[jax.ShapeDtypeStruct((SEQ, ATT_DIM), MM), jax.ShapeDtypeStruct((SEQ, 128), F32)],
        in_specs=[VMEM_SPEC] * 3, out_specs=[VMEM_SPEC] * 2,
        scratch_shapes=[pltpu.VMEM((SEQ, ATT_DIM), F32), pltpu.VMEM((8, TQ, 128), F32),
                        pltpu.VMEM((8, TQ, 128), F32), pltpu.VMEM((8, TQ, 128), MM),
                        pltpu.VMEM((N_PAIRS, TQ, 256), MM), pltpu.VMEM((N_PAIRS, 2 * SEQ, 128), MM),
                        pltpu.VMEM((N_PAIRS, 2 * SEQ, 128), MM)],
        compiler_params=_params(),
    )(q, k, v)


def mix_fwd(u3, att, gc, ga, x, w_cb_g, b_cb, w_ab_g, w_out_g, g2, g3, after):
    def body(u_ref, a_ref, gc_ref, ga_ref, x_ref, wcb_ref, bcb_ref, wab_ref, wout_ref, g2_ref, g3_ref, after_ref,
             co_ref, ao_ref, mg_ref, mix_ref, x2_ref, h2_ref):
        u = u_ref[...]
        a = a_ref[...]
        co = jnp.concatenate([_dot(u, wcb_ref[j]) for j in range(N_CHIPS)], axis=1) + bcb_ref[...]
        ao = jnp.concatenate([_dot(a, wab_ref[j]) for j in range(N_CHIPS)], axis=1)
        co_ref[...] = co.astype(MM)
        ao_ref[...] = ao.astype(MM)
        merged = (_sigmoid(gc_ref[...]) * co + _sigmoid(ga_ref[...]) * ao).astype(MM)
        mg_ref[...] = merged
        mix = _dot(merged, wout_ref[...])
        mix_ref[...] = mix
        n2, _ = _rms(mix)
        x2 = x_ref[...] + n2 * g2_ref[...]
        x2_ref[...] = x2
        n3, _ = _rms(x2)
        h2_ref[...] = (n3 * g3_ref[...]).astype(MM)

    out_shape = [
        jax.ShapeDtypeStruct((SEQ, D_MODEL), MM), jax.ShapeDtypeStruct((SEQ, D_MODEL), MM),
        jax.ShapeDtypeStruct((SEQ, D_MODEL), MM), jax.ShapeDtypeStruct((SEQ, D_MODEL), F32),
        jax.ShapeDtypeStruct((SEQ, D_MODEL), F32), jax.ShapeDtypeStruct((SEQ, D_MODEL), MM),
    ]
    vec = _full_spec((1, D_MODEL))
    return pl.pallas_call(
        body, name="mix_fwd", grid=(SEQ // TM,), out_shape=out_shape,
        in_specs=[_row_tile_spec(CONV_DIM), _row_tile_spec(ATT_DIM), _row_tile_spec(D_MODEL),
                  _row_tile_spec(D_MODEL), _row_tile_spec(D_MODEL), _weight_spec(w_cb_g.shape), vec,
                  _weight_spec(w_ab_g.shape), _weight_spec(w_out_g.shape), vec, vec, TOKEN_SPEC],
        out_specs=[_row_tile_spec(D_MODEL)] * 6,
        compiler_params=_params(("arbitrary",)),
    )(u3, att, gc, ga, x, w_cb_g, b_cb, w_ab_g, w_out_g, g2, g3, after)


def ffn_up_fwd(h2, w_up_g):
    def body(h_ref, wg_ref, wu_ref, gate_ref, up_ref, act_ref):
        h = h_ref[...]
        gate = _dot(h, wg_ref[0])
        up = _dot(h, wu_ref[0])
        gate_ref[...] = gate.astype(MM)
        up_ref[...] = up.astype(MM)
        act_ref[...] = (gate * _sigmoid(gate) * up).astype(MM)

    tile = pl.BlockSpec((TM, UP_SHARD), lambda n, i: (i, n))
    act = jax.ShapeDtypeStruct((SEQ, D_FF), MM)
    return pl.pallas_call(
        body, name="ffn_up_fwd", grid=(2, SEQ // TM), out_shape=[act, act, act],
        in_specs=[pl.BlockSpec((TM, D_MODEL), lambda n, i: (i, 0)),
                  pl.BlockSpec((1, D_MODEL, UP_SHARD), lambda n, i: (n, 0, 0)),
                  pl.BlockSpec((1, D_MODEL, UP_SHARD), lambda n, i: (n + 2, 0, 0))],
        out_specs=[tile, tile, tile],
        compiler_params=_params(("arbitrary", "arbitrary")),
    )(h2, w_up_g, w_up_g)


def ffn_down_loss(act, w_down_g, x2, target, g4):
    def body(act_ref, wd_ref, x2_ref, t_ref, g_ref, dff_ref, dy_ref, loss_ref, dg_ref):
        ff = _dot(act_ref[...], wd_ref[...])
        n4, r4 = _rms(ff)
        g4v = g_ref[...]
        err = x2_ref[...] + n4 * g4v - t_ref[...]
        row_loss = jnp.mean(err * err, axis=-1, keepdims=True)
        loss_ref[...] = jnp.zeros((8, 128), F32) + 0.5 * jnp.sum(row_loss, axis=0, keepdims=True)
        dy = err * (1.0 / D_MODEL)
        dy_ref[...] = dy
        dff_ref[...] = _rms_bwd(dy * g4v, n4, r4).astype(MM)
        _acc_rows(dg_ref, jnp.sum(dy * n4, axis=0, keepdims=True), pl.program_id(0) == 0)

    nt = SEQ // TM
    vec = _full_spec((1, D_MODEL))
    return pl.pallas_call(
        body, name="ffn_down_loss", grid=(nt,),
        out_shape=(jax.ShapeDtypeStruct((SEQ, D_MODEL), MM), jax.ShapeDtypeStruct((SEQ, D_MODEL), F32),
                   jax.ShapeDtypeStruct((nt * 8, 128), F32), jax.ShapeDtypeStruct((1, D_MODEL), F32)),
        in_specs=[_row_tile_spec(D_FF), _weight_spec(w_down_g.shape), _row_tile_spec(D_MODEL),
                  _row_tile_spec(D_MODEL), vec],
        out_specs=[_row_tile_spec(D_MODEL), _row_tile_spec(D_MODEL),
                   pl.BlockSpec((8, 128), lambda i: (i, 0)), vec],
        compiler_params=_params(("arbitrary",)),
    )(act, w_down_g, x2, target, g4)


def ffn_act_bwd(dff, w_down_g, gate, up):
    def body(dff_ref, wd_ref, gate_ref, up_ref, dgu_ref):
        dact = _dot_nt(dff_ref[...], wd_ref[...])
        gate = gate_ref[...].astype(F32)
        sg = _sigmoid(gate)
        dgu_ref[:, 0:D_FF] = (dact * up_ref[...].astype(F32) * (sg * (1.0 + gate * (1.0 - sg)))).astype(MM)
        dgu_ref[:, D_FF:2 * D_FF] = (dact * (gate * sg)).astype(MM)

    return pl.pallas_call(
        body, name="ffn_act_bwd", grid=(SEQ // TM,),
        out_shape=jax.ShapeDtypeStruct((SEQ, 2 * D_FF), MM),
        in_specs=[_row_tile_spec(D_MODEL), _weight_spec(w_down_g.shape), _row_tile_spec(D_FF), _row_tile_spec(D_FF)],
        out_specs=_row_tile_spec(2 * D_FF),
        compiler_params=_params(("arbitrary",)),
    )(dff, w_down_g, gate, up)


def ffn_in_bwd(dgu, w_up_g, x2, mix, dy, g3, g2):
    def body(dgu_ref, w_ref, x2_ref, mix_ref, dy_ref, g3_ref, g2_ref, dx2_ref, dmix_ref, dg3_ref, dg2_ref):
        dh2 = None
        for j in range(N_CHIPS):
            t = _dot_nt(dgu_ref[:, j * UP_SHARD:(j + 1) * UP_SHARD], w_ref[j])
            dh2 = t if dh2 is None else dh2 + t
        first = pl.program_id(0) == 0
        n3, r3 = _rms(x2_ref[...])
        dx2 = dy_ref[...] + _rms_bwd(dh2 * g3_ref[...], n3, r3)
        dx2_ref[...] = dx2
        _acc_rows(dg3_ref, jnp.sum(dh2 * n3, axis=0, keepdims=True), first)
        n2, r2 = _rms(mix_ref[...])
        dmix_ref[...] = _rms_bwd(dx2 * g2_ref[...], n2, r2).astype(MM)
        _acc_rows(dg2_ref, jnp.sum(dx2 * n2, axis=0, keepdims=True), first)

    vec = _full_spec((1, D_MODEL))
    return pl.pallas_call(
        body, name="ffn_in_bwd", grid=(SEQ // TM,),
        out_shape=(jax.ShapeDtypeStruct((SEQ, D_MODEL), F32), jax.ShapeDtypeStruct((SEQ, D_MODEL), MM),
                   jax.ShapeDtypeStruct((1, D_MODEL), F32), jax.ShapeDtypeStruct((1, D_MODEL), F32)),
        in_specs=[_row_tile_spec(2 * D_FF), _weight_spec(w_up_g.shape), _row_tile_spec(D_MODEL),
                  _row_tile_spec(D_MODEL), _row_tile_spec(D_MODEL), vec, vec],
        out_specs=[_row_tile_spec(D_MODEL), _row_tile_spec(D_MODEL), vec, vec],
        compiler_params=_params(("arbitrary",)),
    )(dgu, w_up_g, x2, mix, dy, g3, g2)


def merge_bwd(dmix, w_out_g, gc, ga, co, ao, w_cb_g, w_ab_g, after):
    def body(dmix_ref, wout_ref, gc_ref, ga_ref, co_ref, ao_ref, wcb_ref, wab_ref, after_ref,
             dco_ref, dao_ref, dg_ref, du3_ref, datt_ref, dbcb_ref):
        dm = _dot_nt(dmix_ref[...], wout_ref[...])
        sgc = _sigmoid(gc_ref[...])
        sga = _sigmoid(ga_ref[...])
        dco = dm * sgc
        dao = dm * sga
        dg_ref[:, 0:D_MODEL] = (dm * co_ref[...].astype(F32) * (sgc * (1.0 - sgc))).astype(MM)
        dg_ref[:, D_MODEL:2 * D_MODEL] = (dm * ao_ref[...].astype(F32) * (sga * (1.0 - sga))).astype(MM)
        _acc_rows(dbcb_ref, jnp.sum(dco, axis=0, keepdims=True), pl.program_id(0) == 0)
        dco_ref[...] = dco.astype(MM)
        dao_ref[...] = dao.astype(MM)
        du3 = None
        datt = None
        for j in range(N_CHIPS):
            cols = slice(j * BR_SHARD, (j + 1) * BR_SHARD)
            t = _dot_nt(dco_ref[:, cols], wcb_ref[j])
            s = _dot_nt(dao_ref[:, cols], wab_ref[j])
            du3 = t if du3 is None else du3 + t
            datt = s if datt is None else datt + s
        du3_ref[...] = du3
        datt_ref[...] = datt.astype(MM)

    wide = _row_tile_spec(D_MODEL)
    return pl.pallas_call(
        body, name="merge_bwd", grid=(SEQ // TM,),
        out_shape=(jax.ShapeDtypeStruct((SEQ, D_MODEL), MM), jax.ShapeDtypeStruct((SEQ, D_MODEL), MM),
                   jax.ShapeDtypeStruct((SEQ, 2 * D_MODEL), MM),
                   jax.ShapeDtypeStruct((SEQ, CONV_DIM), F32), jax.ShapeDtypeStruct((SEQ, ATT_DIM), MM),
                   jax.ShapeDtypeStruct((1, D_MODEL), F32)),
        in_specs=[wide, _weight_spec(w_out_g.shape), wide, wide, wide, wide,
                  _weight_spec(w_cb_g.shape), _weight_spec(w_ab_g.shape), TOKEN_SPEC],
        out_specs=[wide, wide, _row_tile_spec(2 * D_MODEL), _row_tile_spec(CONV_DIM), _row_tile_spec(ATT_DIM),
                   _full_spec((1, D_MODEL))],
        compiler_params=_params(("arbitrary",)),
    )(dmix, w_out_g, gc, ga, co, ao, w_cb_g, w_ab_g, after)


def conv_bwd(du3, u1, ci, w_dw, ln_g, ln_b, after):
    def body(du3_ref, u1_ref, ci_ref, w_ref, g_ref, bb_ref, after_ref,
             dci_ref, dw_ref, dbdw_ref, dg_ref, db_ref, upad_ref, dpad_ref, dwacc_ref, vacc_ref):
        _glu_into(ci_ref, upad_ref)
        dpad_ref[SEQ:SEQ + 32, :] = jnp.zeros((32, CONV_DIM), F32)
        dwacc_ref[...] = jnp.zeros_like(dwacc_ref)
        vacc_ref[...] = jnp.zeros_like(vacc_ref)

        def fold8(t):
            s = t[0:8, :]
            for r in range(1, CONV_TILE // 8):
                s = s + t[8 * r:8 * r + 8, :]
            return s

        def pass1(i, c):
            t0 = pl.multiple_of(i * CONV_TILE, CONV_TILE)
            xh, rstd = _layernorm_parts(u1_ref[pl.ds(t0, CONV_TILE), :])
            gv = g_ref[...]
            u2 = xh * gv + bb_ref[...]
            s2 = _sigmoid(u2)
            du2 = du3_ref[pl.ds(t0, CONV_TILE), :] * (s2 * (1.0 + u2 * (1.0 - s2)))
            wv = du2 * gv
            du1 = rstd * (wv - jnp.mean(wv, axis=-1, keepdims=True)
                          - xh * jnp.mean(wv * xh, axis=-1, keepdims=True))
            dpad_ref[pl.ds(t0, CONV_TILE), :] = du1
            vacc_ref[0] += fold8(du2 * xh)
            vacc_ref[1] += fold8(du2)
            vacc_ref[2] += fold8(du1)
            win = upad_ref[pl.ds(t0, CONV_WIN), :]
            n = win.shape[0]
            for rot in range(8):
                shifted = win if rot == 0 else pltpu.roll(win, n - rot, 0)
                for a in range(5):
                    j = 8 * a + rot - 2
                    if 0 <= j < CONV_WIDTH:
                        dwacc_ref[j] += fold8(du1 * shifted[8 * a:8 * a + CONV_TILE, :])
            return c

        lax.fori_loop(0, SEQ // CONV_TILE, pass1, 0)

        def pass2(i, c):
            t0 = pl.multiple_of(i * CONV_TILE, CONV_TILE)
            win = dpad_ref[pl.ds(t0, CONV_WIN), :]
            du0 = _shifted_sum(win, [(30 - j, w_ref[j:j + 1, :]) for j in range(CONV_WIDTH)])
            a = ci_ref[pl.ds(t0, CONV_TILE), 0:CONV_DIM]
            sb = _sigmoid(ci_ref[pl.ds(t0, CONV_TILE), CONV_DIM:2 * CONV_DIM])
            dci_ref[pl.ds(t0, CONV_TILE), 0:CONV_DIM] = (du0 * sb).astype(MM)
            dci_ref[pl.ds(t0, CONV_TILE), CONV_DIM:2 * CONV_DIM] = (du0 * a * (sb * (1.0 - sb))).astype(MM)
            return c

        lax.fori_loop(0, SEQ // CONV_TILE, pass2, 0)

        for j in range(CONV_WIDTH):
            dw_ref[j:j + 1, :] = jnp.sum(dwacc_ref[j], axis=0, keepdims=True)
        dw_ref[CONV_WIDTH:32, :] = jnp.zeros((32 - CONV_WIDTH, CONV_DIM), F32)
        dg_ref[...] = jnp.sum(vacc_ref[0], axis=0, keepdims=True)
        db_ref[...] = jnp.sum(vacc_ref[1], axis=0, keepdims=True)
        dbdw_ref[...] = jnp.sum(vacc_ref[2], axis=0, keepdims=True)

    vec = jax.ShapeDtypeStruct((1, CONV_DIM), F32)
    return pl.pallas_call(
        body, name="conv_bwd",
        out_shape=(jax.ShapeDtypeStruct((SEQ, 2 * CONV_DIM), MM), jax.ShapeDtypeStruct((32, CONV_DIM), F32),
                   vec, vec, vec),
        in_specs=[VMEM_SPEC] * 7, out_specs=[VMEM_SPEC] * 5,
        scratch_shapes=[pltpu.VMEM((SEQ + 32, CONV_DIM), F32), pltpu.VMEM((SEQ + 32, CONV_DIM), F32),
                        pltpu.VMEM((CONV_WIDTH, 8, CONV_DIM), F32), pltpu.VMEM((3, 8, CONV_DIM), F32)],
        compiler_params=_params(),
    )(du3, u1, ci, w_dw, ln_g, ln_b, after)


def attn_bwd(q, k, v, datt, rc, after):
    def body(q_ref, k_ref, v_ref, do_ref, rc_ref, after_ref, dqkv_ref, dqa_ref, dka_ref, dva_ref, pc_ref, z_ref,
             sig1_ref, sig2_ref, g_ref, spb_ref, gb_ref, ar_ref, dzr_ref, dzc_ref, qm_ref, km_ref, dom_ref):
        lane, row, _ = _head_masks()
        _split_heads(q_ref, qm_ref)
        _split_heads(k_ref, km_ref)
        _split_heads(do_ref, dom_ref)
        for ref in (dqa_ref, dka_ref, dva_ref, pc_ref):
            ref[...] = jnp.zeros_like(ref)
        z_ref[...] = jnp.full(z_ref.shape, NO_SCORE, F32)
        for ref in (sig1_ref, sig2_ref, spb_ref, ar_ref, g_ref, gb_ref, dzr_ref, dzc_ref):
            ref[...] = jnp.zeros_like(ref)
        w_suffix = _cumsum_weights(suffix=True, with_total=False)
        w_prefix = _cumsum_weights(suffix=False, with_total=True)

        def step(pairs):
            (ia, ja), (ib, jb), (ic, jc), (id_, jd) = pairs
            ka, qb_, kb_, kc, qd, kd = (pl.multiple_of(jnp.maximum(b, 0) * TQ, TQ) for b in (ja, ib, jb, jc, id_, jd))
            qa2, qb2, qc2, qd2, kd2 = (pl.multiple_of(jnp.maximum(b, 0) * 2 * TQ, 2 * TQ)
                                       for b in (ia, ib, ic, id_, jd))
            bias_a = _score_bias(lane, row, ia, ja)
            rc_rows = rc_ref[pl.ds(qb_, TQ), :]
            first_c = jc == 0
            for p in range(N_PAIRS):
                cols = slice(128 * p, 128 * (p + 1))
                k_a = k_ref[pl.ds(ka, TQ), cols]
                v_b = v_ref[pl.ds(kb_, TQ), cols]
                dqa_ref[pl.ds(qd, TQ), cols] += _dot(dzc_ref[p], km_ref[p, pl.ds(kd2, 2 * TQ), :])
                dka_ref[pl.ds(kd, TQ), cols] += _dot_tn(dzr_ref[p], qm_ref[p, pl.ds(qd2, 2 * TQ), :])
                dva_ref[pl.ds(kc, TQ), cols] += _dot_tn(ar_ref[p], dom_ref[p, pl.ds(qc2, 2 * TQ), :])
                for h in range(2):
                    hh = 2 * p + h
                    rows = slice(TQ * h, TQ * (h + 1))
                    r = _dot(gb_ref[hh], w_prefix)
                    p_in = jnp.where(first_c, 0.0, pc_ref[hh])
                    dz = (g_ref[hh] - sig2_ref[hh] * (r[:, :128] + p_in)).astype(MM)
                    dzc_ref[p, :, rows] = dz
                    dzr_ref[p, rows, :] = dz
                    pc_ref[hh] = p_in + r[:, 128:]
                    r_in = jnp.sum(jnp.where(lane == 16 * hh + jb, rc_rows, 0.0), axis=1, keepdims=True)
                    a = jnp.exp(z_ref[hh] - (_dot(spb_ref[hh], w_suffix) + r_in))
                    g = _dot_nt(dom_ref[p, pl.ds(qb2 + TQ * h, TQ), :], v_b) * a
                    ar_ref[p, rows, :] = a.astype(MM)
                    g_ref[hh] = g
                    gb_ref[hh] = g.astype(MM)
                    sig2_ref[hh] = sig1_ref[hh]
                    z = _dot_nt(qm_ref[p, pl.ds(qa2 + TQ * h, TQ), :], k_a) + bias_a
                    sp = _softplus(z)
                    sig1_ref[hh] = jnp.exp(z - sp)
                    z_ref[hh] = z
                    spb_ref[hh] = sp.astype(MM)

        _block_pipeline(4, False, step)
        dqkv_ref[:, 0:ATT_DIM] = (dqa_ref[...] * ATT_SCALE).astype(MM)
        dqkv_ref[:, ATT_DIM:2 * ATT_DIM] = dka_ref[...].astype(MM)
        dqkv_ref[:, 2 * ATT_DIM:3 * ATT_DIM] = dva_ref[...].astype(MM)

    split = pltpu.VMEM((N_PAIRS, 2 * SEQ, 128), MM)
    return pl.pallas_call(
        body, name="attn_bwd", out_shape=jax.ShapeDtypeStruct((SEQ, 3 * ATT_DIM), MM),
        in_specs=[VMEM_SPEC] * 6, out_specs=VMEM_SPEC,
        scratch_shapes=[pltpu.VMEM((SEQ, ATT_DIM), F32)] * 3 + [pltpu.VMEM((8, TQ, 128), F32)] * 5
                       + [pltpu.VMEM((8, TQ, 128), MM)] * 2
                       + [pltpu.VMEM((N_PAIRS, 2 * TQ, 128), MM)] * 2 + [pltpu.VMEM((N_PAIRS, TQ, 256), MM)]
                       + [split] * 3,
        compiler_params=_params(),
    )(q, k, v, datt, rc, after)


DPROJ_PIECES = ((0, 1024), (1024, 2560), (2560, 4608))


def _dproj_segments(j):
    g0, g1 = j * IN_SHARD, (j + 1) * IN_SHARD
    segs = []
    for p, (s, e) in enumerate(DPROJ_PIECES):
        lo, hi = max(s, g0), min(e, g1)
        if lo < hi:
            segs.append((p, lo - s, lo - g0, hi - lo))
    return segs


def in_proj_bwd(pieces, w_in_g, x, dx2, g1, after):
    def body(p0_ref, p1_ref, p2_ref, w_ref, x_ref, dx2_ref, g_ref, after_ref, dx_ref, dg_ref):
        p_refs = (p0_ref, p1_ref, p2_ref)
        dh = None
        for j in range(N_CHIPS):
            for p, lo, off, width in _dproj_segments(j):
                t = _dot_nt(p_refs[p][:, lo:lo + width], w_ref[j, :, off:off + width])
                dh = t if dh is None else dh + t
        n1, r1 = _rms(x_ref[...])
        dx_ref[...] = dx2_ref[...] + _rms_bwd(dh * g_ref[...], n1, r1)
        _acc_rows(dg_ref, jnp.sum(dh * n1, axis=0, keepdims=True), pl.program_id(0) == 0)

    vec = _full_spec((1, D_MODEL))
    return pl.pallas_call(
        body, name="in_proj_bwd", grid=(SEQ // TM,),
        out_shape=[jax.ShapeDtypeStruct((SEQ, D_MODEL), F32), jax.ShapeDtypeStruct((1, D_MODEL), F32)],
        in_specs=[_row_tile_spec(p.shape[1]) for p in pieces]
                 + [_weight_spec(w_in_g.shape), _row_tile_spec(D_MODEL), _row_tile_spec(D_MODEL), vec, TOKEN_SPEC],
        out_specs=[_row_tile_spec(D_MODEL), vec],
        compiler_params=_params(("arbitrary",)),
    )(*pieces, w_in_g, x, dx2, g1, after)


def weight_grad_in(h1, pieces):
    kh = D_MODEL // 2

    def body(a_ref, p0_ref, p1_ref, p2_ref, o_ref):
        p_refs = (p0_ref, p1_ref, p2_ref)
        a = a_ref[...]
        for j in range(N_CHIPS):
            @pl.when(pl.program_id(1) == j)
            def _():
                for p, lo, off, width in _dproj_segments(j):
                    o_ref[0, 0, :, off:off + width] = _dot_tn(a, p_refs[p][:, lo:lo + width]).astype(MM)

    return pl.pallas_call(
        body, name="dw_in", grid=(2, N_CHIPS), out_shape=jax.ShapeDtypeStruct((N_CHIPS, 2, kh, IN_SHARD), MM),
        in_specs=[pl.BlockSpec((SEQ, kh), lambda h, j: (0, h))] + [_weight_spec(p.shape) for p in pieces],
        out_specs=pl.BlockSpec((1, 1, kh, IN_SHARD), lambda h, j: (j, h, 0, 0)),
        compiler_params=_params(("arbitrary", "arbitrary")),
    )(h1, *pieces)


def weight_grad(a, b, name, col_sharded, tk=None):
    kin, n = a.shape[1], b.shape[1]

    def body(a_ref, b_ref, o_ref):
        if col_sharded:
            o_ref[0, 0] = _dot_tn(a_ref[...], b_ref[...]).astype(MM)
        else:
            o_ref[...] = _dot_tn(a_ref[...], b_ref[...]).astype(MM)

    if col_sharded:
        kh, ns = kin // 2, n // N_CHIPS
        out = jax.ShapeDtypeStruct((N_CHIPS, 2, kh, ns), MM)
        grid = (2, N_CHIPS)
        in_specs = [pl.BlockSpec((SEQ, kh), lambda h, j: (0, h)), pl.BlockSpec((SEQ, ns), lambda h, j: (0, j))]
        out_spec = pl.BlockSpec((1, 1, kh, ns), lambda h, j: (j, h, 0, 0))
        sem = ("arbitrary", "arbitrary")
    else:
        out = jax.ShapeDtypeStruct((kin, n), MM)
        grid = (kin // tk,)
        in_specs = [pl.BlockSpec((SEQ, tk), lambda r: (0, r)), pl.BlockSpec((SEQ, n), lambda r: (0, 0))]
        out_spec = pl.BlockSpec((tk, n), lambda r: (r, 0))
        sem = ("arbitrary",)
    res = pl.pallas_call(
        body, name=name, grid=grid, out_shape=out, in_specs=in_specs, out_specs=out_spec,
        compiler_params=_params(sem),
    )(a, b)
    if not col_sharded:
        res = res.reshape(N_CHIPS, 2, kin // (2 * N_CHIPS), n)
    return res


def _place():
    x, y, c = lax.axis_index("x"), lax.axis_index("y"), lax.axis_index("c")
    chips = [(1 - x, y), (x, 1 - y), (1 - x, 1 - y)]
    return x, y, c, chips


def _rcopy(src, dst, send_sem, recv_sem, dev):
    return pltpu.make_async_remote_copy(src_ref=src, dst_ref=dst, send_sem=send_sem, recv_sem=recv_sem,
                                        device_id=dev, device_id_type=MESH)


class _Gather:
    def __init__(self, shapes, w, o, scratch):
        self.n, self.shapes, self.w, self.o = len(w), shapes, w, o
        self.send, self.recv, self.fsend, self.frecv, self.loc_in, self.loc_out = scratch[:6]
        self.raw, self.stage = scratch[6:6 + self.n], scratch[6 + self.n:]
        self.x, self.y, self.c, self.chips = _place()
        self.me = 2 * self.x + self.y
        self.sib = (self.x, self.y, 1 - self.c)
        self.pairs = [(j, t) for j in range(3) for t in range(self.n)]

    @staticmethod
    def scratch(shards):
        n = len(shards)
        sems = pltpu.SemaphoreType.DMA
        return ([sems((3 * n,)), sems((3 * n,)), sems((3 * n,)), sems((3 * n,)), sems((n,)), sems((n,))]
                + [pltpu.VMEM(s.shape, s.dtype) for s in shards] + [pltpu.VMEM(s.shape, MM) for s in shards])

    @staticmethod
    def out_shapes(shards):
        return [jax.ShapeDtypeStruct((N_CHIPS,) + s.shape, MM) for s in shards]

    def _half(self, t, k, cc):
        rh = self.shapes[t][0] // 2
        return self.o[t].at[k, pl.ds(cc * rh, rh), :]

    def _chip(self, j):
        cx, cy = self.chips[j]
        return 2 * cx + cy, (cx, cy, self.c)

    def local_in(self, t):
        return pltpu.make_async_copy(self.w[t], self.raw[t], self.loc_in.at[t])

    def local_out(self, t):
        return pltpu.make_async_copy(self.stage[t], self.o[t].at[self.me], self.loc_out.at[t])

    def first(self, j, t):
        rh = self.shapes[t][0] // 2
        i = j * self.n + t
        return _rcopy(self.stage[t].at[pl.ds(self.c * rh, rh), :], self._half(t, self.me, self.c),
                      self.send.at[i], self.recv.at[i], self._chip(j)[1])

    def arrived(self, j, t):
        k, dev = self._chip(j)
        i = j * self.n + t
        blk = self._half(t, k, self.c)
        return _rcopy(blk, blk, self.send.at[i], self.recv.at[i], dev)

    def passed(self, j, t, cc):
        i = j * self.n + t
        blk = self._half(t, self._chip(j)[0], cc)
        return _rcopy(blk, blk, self.fsend.at[i], self.frecv.at[i], self.sib)

    def start(self):
        for t in range(self.n):
            self.local_in(t).start()
        for t in range(self.n):
            self.local_in(t).wait()
            self.stage[t][...] = self.raw[t][...].astype(MM)
            self.local_out(t).start()
        for j, t in self.pairs:
            self.first(j, t).start()

    def forward(self):
        for j, t in self.pairs:
            self.arrived(j, t).wait_recv()
            self.passed(j, t, self.c).start()

    def finish(self):
        for j, t in self.pairs:
            self.passed(j, t, 1 - self.c).wait_recv()
        for j, t in self.pairs:
            self.first(j, t).wait_send()
            self.passed(j, t, self.c).wait_send()
        for t in range(self.n):
            self.local_out(t).wait()


def all_gather_weights(shards, small, later):
    n, m = len(shards), len(later)
    shapes = [s.shape for s in shards]

    def body(*refs):
        w = refs[:n]
        sm = refs[n]
        lw = refs[n + 1:n + 1 + m]
        o = refs[n + 1 + m:2 * n + 1 + m]
        osm = refs[2 * n + 1 + m]
        lo = refs[2 * n + 2 + m:2 * n + 2 + 2 * m]
        scratch = refs[2 * n + 2 + 2 * m:]
        ssend, srecv, sloc, lsem_in, lsem_out = scratch[:5]
        lraw, lstage = scratch[5:5 + m], scratch[5 + m:5 + 2 * m]
        g = _Gather(shapes, w, o, scratch[5 + 2 * m:])
        own = pltpu.make_async_copy(sm, osm.at[g.me], sloc)
        own.start()
        loads = [pltpu.make_async_copy(lw[t], lraw[t], lsem_in.at[t]) for t in range(m)]
        for cp in loads:
            cp.start()
        g.start()
        small_cps = [_rcopy(sm, osm.at[g.me], ssend.at[j], srecv.at[j], g._chip(j)[1]) for j in range(3)]
        for cp in small_cps:
            cp.start()
        places = []
        for t in range(m):
            loads[t].wait()
            lstage[t][...] = lraw[t][...].astype(MM)
            places.append(pltpu.make_async_copy(lstage[t], lo[t].at[g.me], lsem_out.at[t]))
            places[t].start()
        g.forward()
        g.finish()
        for j in range(3):
            k, dev = g._chip(j)
            _rcopy(sm, osm.at[k], ssend.at[j], srecv.at[j], dev).wait_recv()
            small_cps[j].wait_send()
        own.wait()
        for cp in places:
            cp.wait()

    out_shape = _Gather.out_shapes(shards)
    out_shape.append(jax.ShapeDtypeStruct((N_CHIPS,) + small.shape, small.dtype))
    out_shape += _Gather.out_shapes(later)
    sems = pltpu.SemaphoreType.DMA
    return pl.pallas_call(
        body, name="all_gather_weights", out_shape=out_shape,
        in_specs=[ANY] * (n + 1 + m), out_specs=[ANY] * (n + 1 + m),
        scratch_shapes=[sems((3,)), sems((3,)), sems, sems((m,)), sems((m,))]
                       + [pltpu.VMEM(s.shape, s.dtype) for s in later] + [pltpu.VMEM(s.shape, MM) for s in later]
                       + _Gather.scratch(shards),
        compiler_params=_params(),
    )(*shards, small, *later)


HBM_SPEC = pl.BlockSpec(memory_space=pltpu.HBM)
SEM_SPEC = pl.BlockSpec(memory_space=pltpu.SEMAPHORE)
DATAFLOW = pltpu.SideEffectType.DATAFLOW_SIDE_EFFECTING


def split_start(name, bufs, n_copies, copies):
    nb = len(bufs)

    def body(*refs):
        for cp in copies(refs[:nb], refs[nb], refs[nb + 1]):
            cp.start()
        token = refs[2 * nb + 2]
        token[...] = jnp.zeros_like(token)

    sems = [pltpu.SemaphoreType.DMA((n_copies,))] * 2
    res = pl.pallas_call(
        body, name=name,
        out_shape=sems + [pltpu.HBM(a.shape, a.dtype) for a in bufs] + [jax.ShapeDtypeStruct((8, 128), F32)],
        in_specs=[HBM_SPEC] * nb, out_specs=[SEM_SPEC] * 2 + [HBM_SPEC] * nb + [VMEM_SPEC],
        input_output_aliases={i: 2 + i for i in range(nb)},
        compiler_params=pltpu.CompilerParams(has_side_effects=DATAFLOW),
    )(*[pltpu.with_memory_space_constraint(a, pltpu.HBM) for a in bufs])
    return res[:-1], res[-1]


def split_wait(name, state, after, copies):
    sems, bufs = state[:2], state[2:]
    nb = len(bufs)

    def body(*refs):
        for cp in copies(refs[:nb], refs[nb], refs[nb + 1]):
            cp.wait_send()
            cp.wait_recv()

    return pl.pallas_call(
        body, name=name, out_shape=[pltpu.HBM(a.shape, a.dtype) for a in bufs],
        in_specs=[HBM_SPEC] * nb + [SEM_SPEC] * 2 + [ANY] * len(after), out_specs=[HBM_SPEC] * nb,
        input_output_aliases={i: i for i in range(nb)},
        compiler_params=pltpu.CompilerParams(has_side_effects=DATAFLOW),
    )(*bufs, *sems, *after)


class _Shifted:
    def __init__(self, sems, first):
        self.sems, self.first = sems, first

    @property
    def at(self):
        return self

    def __getitem__(self, i):
        return self.sems.at[self.first + i]


def _scatter_copies(n):
    def copies(refs, send, recv):
        _, _, c, chips = _place()
        return [_rcopy(refs[t].at[2 * cx + cy], refs[n + t].at[j], send.at[3 * t + j], recv.at[3 * t + j], (cx, cy, c))
                for t in range(n) for j, (cx, cy) in enumerate(chips)]
    return copies


def scatter_start(parts, tag):
    lands = [lax.empty((3,) + p.shape[1:], p.dtype) for p in parts]
    return split_start("scatter_start_" + tag, list(parts) + lands, 3 * len(parts), _scatter_copies(len(parts)))


def scatter_wait(state, after, tag):
    n = (len(state) - 2) // 2
    return split_wait("scatter_wait_" + tag, state, after, _scatter_copies(n))[n:]


def _gather_copies(shapes, level):
    n = len(shapes)

    def copies(refs, send, recv):
        x, y, c, chips = _place()
        out = []
        for t in range(n):
            rh = shapes[t][0] // 2
            for j, (cx, cy) in enumerate(chips):
                k, dev = (2 * x + y, (cx, cy, c)) if level == 1 else (2 * cx + cy, (x, y, 1 - c))
                blk = refs[t].at[k, pl.ds(c * rh, rh), :]
                out.append(_rcopy(blk, blk, send.at[3 * t + j], recv.at[3 * t + j], dev))
        return out
    return copies


def _sibling_copies(n, other_half):
    def copies(refs, send, recv):
        x, y, c, _ = _place()
        return [_rcopy(refs[t].at[:, 1 - c] if other_half else refs[t], refs[n + t], send.at[t], recv.at[t],
                       (x, y, 1 - c)) for t in range(n)]
    return copies


def sibling_start(srcs, other_half, tag):
    lands = [lax.empty((a.shape[0],) + a.shape[2:] if other_half else a.shape, a.dtype) for a in srcs]
    return split_start("sibling_start_" + tag, list(srcs) + lands, len(srcs),
                       _sibling_copies(len(srcs), other_half))


def sibling_wait(state, after, other_half, tag):
    n = (len(state) - 2) // 2
    res = split_wait("sibling_wait_" + tag, state, after, _sibling_copies(n, other_half))
    return res[:n], res[n:]


def small_pack(ddw, v512, v1024, loss_parts):
    rows, width = PACK_ROWS, 512
    n512, n1024 = len(VEC512), len(VEC1024)

    def body(*refs):
        ddw_ref = refs[0]
        a_refs = refs[1:1 + n512]
        b_refs = refs[1 + n512:1 + n512 + n1024]
        lp_ref, o_ref, p_ref = refs[1 + n512 + n1024:]
        p_ref[...] = jnp.zeros_like(p_ref)
        p_ref[0:32, :] = ddw_ref[...]
        p_ref[LOSS_ROW:LOSS_ROW + 1, 0:128] = jnp.sum(lp_ref[...], axis=0, keepdims=True) * 0.125
        for i, r in enumerate(a_refs):
            p_ref[32 + i:33 + i, :] = r[...]
        for i, r in enumerate(b_refs):
            base = 32 + n512 + 2 * i
            p_ref[base:base + 1, :] = r[:, 0:512]
            p_ref[base + 1:base + 2, :] = r[:, 512:1024]
        x, y, c, _ = _place()
        o_ref[4 * x + 2 * y + c] = p_ref[...]

    n_in = 2 + n512 + n1024
    return pl.pallas_call(
        body, name="small_pack", out_shape=jax.ShapeDtypeStruct((8, rows, width), F32),
        in_specs=[VMEM_SPEC] * n_in, out_specs=VMEM_SPEC,
        scratch_shapes=[pltpu.VMEM((rows, width), F32)],
    )(ddw, *[v512[n] for n in VEC512], *[v1024[n] for n in VEC1024], loss_parts)


def _small_copies(refs, send, recv):
    x, y, c, _ = _place()
    mine = refs[0].at[4 * x + 2 * y + c]
    peers = [(1 - x if k & 4 else x, 1 - y if k & 2 else y, 1 - c if k & 1 else c) for k in range(1, 8)]
    return [_rcopy(mine, mine, send.at[i], recv.at[i], dev) for i, dev in enumerate(peers)]


def _row_block(r):
    for tr in (512, 352, 256, 128):
        if r % tr == 0:
            return tr
    return r


def add_halves(g, recv, name):
    _, _, r, w = g.shape
    tr = _row_block(r)

    def body(g_ref, r_ref, ob_ref, own_ref):
        k = pl.program_id(1)
        me = 2 * lax.axis_index("x") + lax.axis_index("y")
        t = g_ref[0, 0].astype(F32) + r_ref[0].astype(F32)
        ob_ref[0] = t.astype(MM)
        mine = jnp.where(k == me, t, 0.0)

        @pl.when(k == 0)
        def _():
            own_ref[...] = mine

        @pl.when(k != 0)
        def _():
            own_ref[...] += mine

    return pl.pallas_call(
        body, name=name, grid=(r // tr, N_CHIPS),
        in_specs=[pl.BlockSpec((1, 1, tr, w), lambda i, k: (k, lax.axis_index("c"), i, 0)),
                  pl.BlockSpec((1, tr, w), lambda i, k: (k, i, 0))],
        out_specs=[pl.BlockSpec((1, tr, w), lambda i, k: (k, i, 0)),
                   pl.BlockSpec((tr, w), lambda i, k: (i, 0))],
        out_shape=(jax.ShapeDtypeStruct((N_CHIPS, r, w), MM), jax.ShapeDtypeStruct((r, w), F32)),
        compiler_params=_params(("arbitrary", "arbitrary")),
    )(g, recv)


def sum_parts(own, rin, after, name):
    _, r, w = rin.shape
    tr = _row_block(r)

    def body(o_ref, r_ref, after_ref, out_ref):
        out_ref[...] = ((o_ref[...] + r_ref[0].astype(F32)) + r_ref[1].astype(F32)) + r_ref[2].astype(F32)

    return pl.pallas_call(
        body, name=name, grid=(r // tr,), out_shape=jax.ShapeDtypeStruct((r, w), F32),
        in_specs=[pl.BlockSpec((tr, w), lambda i: (i, 0)), pl.BlockSpec((3, tr, w), lambda i: (0, i, 0)),
                  _full_spec((8, 128))],
        out_specs=pl.BlockSpec((tr, w), lambda i: (i, 0)),
        compiler_params=_params(("arbitrary",)),
    )(own, rin, after)


def _adamw_math(w, g, m, v):
    mn = ADAM_B1 * m + (1.0 - ADAM_B1) * g
    vn = ADAM_B2 * v + (1.0 - ADAM_B2) * (g * g)
    m_hat = mn / (1.0 - ADAM_B1 ** ADAM_STEP)
    v_hat = vn / (1.0 - ADAM_B2 ** ADAM_STEP)
    return -ADAM_LR * (m_hat / (jnp.sqrt(v_hat) + ADAM_EPS) + ADAM_WD * w), mn, vn


def adamw(w, mine, other, m, v, name):
    r, c = w.shape
    rh = r // 2
    tr = _row_block(rh)
    if c >= 1024 and tr % 512 == 0:
        tr = 256
    nb = rh // tr

    def body(w_ref, a_ref, b_ref, m_ref, v_ref, go_ref, d_ref, mo_ref, vo_ref):
        gv = jnp.where(lax.axis_index("c") == pl.program_id(0), a_ref[...], b_ref[...])
        go_ref[...] = gv
        d_ref[...], mo_ref[...], vo_ref[...] = _adamw_math(w_ref[...], gv, m_ref[...], v_ref[...])

    def half(of_sibling):
        def index(h, i):
            owner = lax.axis_index("c")
            owner = 1 - owner if of_sibling else owner
            return jnp.where(h == owner, i, jnp.where(h < owner, 0, nb - 1)), 0
        return pl.BlockSpec((tr, c), index)

    spec = pl.BlockSpec((tr, c), lambda h, i: (h * nb + i, 0))
    out = jax.ShapeDtypeStruct((r, c), F32)
    return pl.pallas_call(
        body, name=name, grid=(2, nb), out_shape=(out, out, out, out),
        in_specs=[spec, half(False), half(True), spec, spec], out_specs=[spec] * 4,
        compiler_params=_params(("arbitrary", "arbitrary")),
    )(w, mine, other, m, v)


def adamw_small(packs, params, after):
    names = list(params)
    flat = [a for n in names for a in params[n]]

    def body(*refs):
        p_ref = refs[0]
        ins = refs[1:1 + 3 * len(names)]
        g_ref = refs[2 + 3 * len(names)]
        outs = refs[3 + 3 * len(names):]
        total = p_ref[0]
        for d in range(1, 8):
            total = total + p_ref[d]
        g_ref[...] = total
        me = 2 * lax.axis_index("x") + lax.axis_index("y")
        for i, n in enumerate(names):
            w_ref, m_ref, v_ref = ins[3 * i:3 * i + 3]
            go_ref, d_ref, mo_ref, vo_ref = outs[4 * i:4 * i + 4]
            if n == "conv_dw_w":
                gv = jnp.zeros((CONV_WIDTH, 128), F32)
                for k in range(N_CHIPS):
                    gv = gv + jnp.where(me == k, g_ref[0:CONV_WIDTH, 128 * k:128 * (k + 1)], 0.0)
            elif n in VEC512:
                r0 = 32 + VEC512.index(n)
                gv = g_ref[r0:r0 + 1, :]
            else:
                r0 = 32 + len(VEC512) + 2 * VEC1024.index(n)
                gv = jnp.concatenate([g_ref[r0:r0 + 1, :], g_ref[r0 + 1:r0 + 2, :]], axis=1)
            go_ref[...] = gv
            d_ref[...], mo_ref[...], vo_ref[...] = _adamw_math(w_ref[...], gv, m_ref[...], v_ref[...])

    out_shape = [jax.ShapeDtypeStruct(packs.shape[1:], F32)]
    out_shape += [jax.ShapeDtypeStruct(params[n][0].shape, F32) for n in names for _ in range(4)]
    res = pl.pallas_call(
        body, name="adamw_small", out_shape=out_shape,
        in_specs=[VMEM_SPEC] * (2 + len(flat)), out_specs=[VMEM_SPEC] * len(out_shape),
        compiler_params=_params(),
    )(packs, *flat, after)
    return res[0], {n: res[1 + 4 * i:5 + 4 * i] for i, n in enumerate(names)}


REST = ("w_ffn_up", "w_ffn_down", "w_out", "w_conv_branch", "w_att_branch")
VEC512 = ("conv_dw_b", "conv_ln_g", "conv_ln_b")
VEC1024 = ("norm_mix_pre", "b_conv_branch", "norm_mix_post", "norm_ffn_pre", "norm_ffn_post")
PACK_ROWS = 48
LOSS_ROW = 47


def kernel(x, norm_mix_pre, w_in, conv_dw_w, conv_dw_b, conv_ln_g, conv_ln_b, w_conv_branch, b_conv_branch, w_att_branch, w_out, norm_mix_post, norm_ffn_pre, w_ffn_up, w_ffn_down, norm_ffn_post, loss_target, m_norm_mix_pre, m_w_in, m_conv_dw_w, m_conv_dw_b, m_conv_ln_g, m_conv_ln_b, m_w_conv_branch, m_b_conv_branch, m_w_att_branch, m_w_out, m_norm_mix_post, m_norm_ffn_pre, m_w_ffn_up, m_w_ffn_down, m_norm_ffn_post, v_norm_mix_pre, v_w_in, v_conv_dw_w, v_conv_dw_b, v_conv_ln_g, v_conv_ln_b, v_w_conv_branch, v_b_conv_branch, v_w_att_branch, v_w_out, v_norm_mix_post, v_norm_ffn_pre, v_w_ffn_up, v_w_ffn_down, v_norm_ffn_post):
    weights = dict(norm_mix_pre=norm_mix_pre, w_in=w_in, conv_dw_w=conv_dw_w, conv_dw_b=conv_dw_b, conv_ln_g=conv_ln_g, conv_ln_b=conv_ln_b, w_conv_branch=w_conv_branch, b_conv_branch=b_conv_branch, w_att_branch=w_att_branch, w_out=w_out, norm_mix_post=norm_mix_post, norm_ffn_pre=norm_ffn_pre, w_ffn_up=w_ffn_up, w_ffn_down=w_ffn_down, norm_ffn_post=norm_ffn_post)
    mom = dict(norm_mix_pre=m_norm_mix_pre, w_in=m_w_in, conv_dw_w=m_conv_dw_w, conv_dw_b=m_conv_dw_b, conv_ln_g=m_conv_ln_g, conv_ln_b=m_conv_ln_b, w_conv_branch=m_w_conv_branch, b_conv_branch=m_b_conv_branch, w_att_branch=m_w_att_branch, w_out=m_w_out, norm_mix_post=m_norm_mix_post, norm_ffn_pre=m_norm_ffn_pre, w_ffn_up=m_w_ffn_up, w_ffn_down=m_w_ffn_down, norm_ffn_post=m_norm_ffn_post)
    var = dict(norm_mix_pre=v_norm_mix_pre, w_in=v_w_in, conv_dw_w=v_conv_dw_w, conv_dw_b=v_conv_dw_b, conv_ln_g=v_conv_ln_g, conv_ln_b=v_conv_ln_b, w_conv_branch=v_w_conv_branch, b_conv_branch=v_b_conv_branch, w_att_branch=v_w_att_branch, w_out=v_w_out, norm_mix_post=v_norm_mix_post, norm_ffn_pre=v_norm_ffn_pre, w_ffn_up=v_w_ffn_up, w_ffn_down=v_w_ffn_down, norm_ffn_post=v_norm_ffn_post)
    order = list(weights)
    grads, deltas, new_m, new_v = {}, {}, {}, {}
    xs = x.reshape(SEQ, D_MODEL)
    tgt = loss_target.reshape(SEQ, D_MODEL)
    row = lambda a: a.reshape(1, -1)
    g1, g2, g3, g4 = (row(weights[n]) for n in ("norm_mix_pre", "norm_mix_post", "norm_ffn_pre", "norm_ffn_post"))
    ln_g, ln_b = row(conv_ln_g), row(conv_ln_b)

    summed, from_chips = {}, {}

    def core_sums(names, state, after, tag):
        own, from_sibling = sibling_wait(state, after, True, tag)
        for n, g, r in zip(names, own, from_sibling):
            summed[n] = add_halves(g, r, "add_" + n)

    def chip_sums(names, after):
        return [sum_parts(summed[n][1], from_chips[n], after, "sum_" + n) for n in names]

    def optimize(names, state, after, tag):
        mine, other = sibling_wait(state, after, False, tag)
        for n, a, b in zip(names, mine, other):
            grads[n], deltas[n], new_m[n], new_v[n] = adamw(weights[n], a, b, mom[n], var[n], "adamw_" + n)

    w_in_g, dw_g, *rest = all_gather_weights([w_in], conv_dw_w, [weights[n] for n in REST])
    w_dw_full = jnp.concatenate([dw_g[k] for k in range(N_CHIPS)], axis=1)
    rest_shapes = [weights[n].shape for n in REST]
    state, token = split_start("gather_start", rest, 3 * len(REST), _gather_copies(rest_shapes, 1))
    h1, ci, q, k, v, gc, ga = in_proj_fwd(xs, g1, w_in_g, token)
    u1, u3 = conv_fwd(ci, w_dw_full, row(conv_dw_b), ln_g, ln_b)
    att, rc = attn_fwd(q, k, v)
    rest = split_wait("gather_wait", state, [att], _gather_copies(rest_shapes, 1))
    pass_copies = _gather_copies(rest_shapes[2:] + rest_shapes[:2], 2)
    n_first = 3 * len(REST[2:])
    state, token = split_start("pass_start", rest[2:] + rest[:2], 3 * len(REST), pass_copies)
    passed = split_wait("pass_mix_wait", state, [], lambda *a: pass_copies(*a)[:n_first])
    w_out_g, w_cb_g, w_ab_g = passed[:3]
    w_out_g = w_out_g.reshape(D_MODEL, D_MODEL)
    co, ao, merged, mix, x2, h2 = mix_fwd(u3, att, gc, ga, xs, w_cb_g, row(b_conv_branch), w_ab_g, w_out_g,
                                          g2, g3, token)
    w_up_g, w_down_g = split_wait("pass_ffn_wait", list(state[:2]) + list(passed), [h2],
                                  lambda *a: pass_copies(*a)[n_first:])[3:]
    w_down_g = w_down_g.reshape(D_FF, D_MODEL)
    gate, up, act = ffn_up_fwd(h2, w_up_g)
    dff, dy, loss_parts, dg4 = ffn_down_loss(act, w_down_g, x2, tgt, g4)

    dgu = ffn_act_bwd(dff, w_down_g, gate, up)
    dx2, dmix, dg3, dg2 = ffn_in_bwd(dgu, w_up_g, x2, mix, dy, g3, g2)
    ffn_grads = [weight_grad(h2, dgu, "dw_ffn_up", True), weight_grad(act, dff, "dw_ffn_down", False, tk=UP_SHARD)]
    to_ffn, token = sibling_start(ffn_grads, True, "dw_ffn")
    dco, dao, dg, du3, datt, dbcb = merge_bwd(dmix, w_out_g, gc, ga, co, ao, w_cb_g, w_ab_g, token)
    mix_grads = [weight_grad(merged, dmix, "dw_out", False, tk=512), weight_grad(u3, dco, "dw_conv_branch", True),
                 weight_grad(att, dao, "dw_att_branch", True)]
    to_mix, token = sibling_start(mix_grads, True, "dw_mix")
    core_sums(REST[:2], to_ffn, [token], "dw_ffn")
    core_sums(REST[2:], to_mix, [summed["w_ffn_down"][1]], "dw_mix")
    state, token = scatter_start([summed[n][0] for n in REST], "rest")
    dci, ddw, dbdw, dlng, dlnb = conv_bwd(du3, u1, ci, w_dw_full, ln_g, ln_b, token)
    dqkv = attn_bwd(q, k, v, datt, rc, token)
    from_chips.update(zip(REST, scatter_wait(state, [dci, dqkv], "rest")))
    dproj = (dci, dqkv, dg)
    to_in, token = sibling_start([weight_grad_in(h1, dproj)], True, "dw_in")
    grad_x, dg1 = in_proj_bwd(dproj, w_in_g, xs, dx2, g1, token)
    v512 = dict(conv_dw_b=dbdw, conv_ln_g=dlng, conv_ln_b=dlnb)
    v1024 = dict(norm_mix_pre=dg1, b_conv_branch=dbcb, norm_mix_post=dg2, norm_ffn_pre=dg3, norm_ffn_post=dg4)
    packs = small_pack(ddw, v512, v1024, loss_parts)
    core_sums(("w_in",), to_in, [packs], "dw_in")
    to_chips = summed["w_in"][0]
    landing = lax.empty((3,) + to_chips.shape[1:], to_chips.dtype)

    def scatter_and_packs(refs, send, recv):
        return (_scatter_copies(1)(refs[:2], send, recv)
                + _small_copies(refs[2:], _Shifted(send, 3), _Shifted(recv, 3)))

    state, token = split_start("scatter_start_w_in", [to_chips, landing, packs], 3 + 7, scatter_and_packs)
    swap_up, token = sibling_start(chip_sums(REST[:1], token), False, "sum_ffn_up")
    swap_rest, token = sibling_start(chip_sums(REST[1:], token), False, "sum_rest")
    optimize(REST[:1], swap_up, [token], "sum_ffn_up")
    optimize(REST[1:], swap_rest, [new_v["w_ffn_up"]], "sum_rest")
    _, from_chips["w_in"], packs = split_wait("scatter_wait_w_in", state, [new_v[n] for n in REST], scatter_and_packs)
    swap_in, token = sibling_start(chip_sums(("w_in",), token), False, "sum_w_in")
    as_rows = lambda n, a: a if n == "conv_dw_w" else a.reshape(1, -1)
    small_names = ("conv_dw_w",) + VEC512 + VEC1024
    gsum, small = adamw_small(packs, {n: tuple(as_rows(n, d[n]) for d in (weights, mom, var)) for n in small_names},
                              token)
    loss = gsum[LOSS_ROW, 0]
    optimize(("w_in",), swap_in, [gsum], "sum_w_in")
    for n in small_names:
        grads[n], deltas[n], new_m[n], new_v[n] = (a.reshape(weights[n].shape) for a in small[n])

    return (loss, grad_x.reshape(1, SEQ, D_MODEL), *[grads[n] for n in order], *[deltas[n] for n in order],
            *[new_m[n] for n in order], *[new_v[n] for n in order])
```

```python
import jax
import jax.numpy as jnp
from jax import lax
from jax.experimental import pallas as pl
from jax.experimental.pallas import tpu as pltpu

F32 = jnp.float32
MM = jnp.bfloat16

SEQ = 2048
D_MODEL = 1024
CONV_DIM = 512
ATT_DIM = 512
CONV_WIDTH = 31
D_FF = 2816
IN_COLS = 2 * CONV_DIM + 3 * ATT_DIM + 2 * D_MODEL
N_CHIPS = 4
IN_SHARD = IN_COLS // N_CHIPS
UP_SHARD = 2 * D_FF // N_CHIPS
BR_SHARD = D_MODEL // N_CHIPS
EPS = 1e-6
ATT_SCALE = 0.125

TM = 256
GLU_ROWS = 256
TQ = 128
CONV_TILE = 64
CONV_WIN = CONV_TILE + 32
VMEM_LIMIT = 56 * 1024 * 1024

ADAM_LR = 0.001
ADAM_B1 = 0.9
ADAM_B2 = 0.999
ADAM_EPS = 1e-08
ADAM_WD = 0.01
ADAM_STEP = 10

MESH = pl.DeviceIdType.MESH
ANY = pl.BlockSpec(memory_space=pl.ANY)
VMEM_SPEC = pl.BlockSpec(memory_space=pltpu.VMEM)

NT_DIMS = (((1,), (1,)), ((), ()))
TN_DIMS = (((0,), (0,)), ((), ()))

IN_PIECES = (("ci", 0, 1024), ("q", 1024, 1536), ("k", 1536, 2048), ("v", 2048, 2560),
             ("gc", 2560, 3584), ("ga", 3584, 4608))


def _params(sem=None, vmem=VMEM_LIMIT):
    return pltpu.CompilerParams(dimension_semantics=sem, vmem_limit_bytes=vmem)


def _dot(a, b):
    return jnp.dot(a, b, preferred_element_type=F32)


def _dot_nt(a, b):
    return lax.dot_general(a, b, NT_DIMS, preferred_element_type=F32)


def _dot_tn(a, b):
    return lax.dot_general(a, b, TN_DIMS, preferred_element_type=F32)


def _sigmoid(x):
    return 1.0 / (1.0 + jnp.exp(-x))


def _rms(x):
    r = lax.rsqrt(jnp.mean(x * x, axis=-1, keepdims=True) + EPS)
    return x * r, r


def _rms_bwd(dy_g, n, r):
    return r * (dy_g - n * jnp.mean(dy_g * n, axis=-1, keepdims=True))


def _row_tile_spec(width, tm=TM):
    return pl.BlockSpec((tm, width), lambda i: (i, 0))


def _full_spec(shape):
    nd = len(shape)
    return pl.BlockSpec(shape, lambda *_: (0,) * nd)


def _weight_spec(shape):
    nd = len(shape)
    return pl.BlockSpec(shape, lambda *_: (0,) * nd, pipeline_mode=pl.Buffered(1))


def _acc_rows(ref, val, first):
    @pl.when(first)
    def _():
        ref[...] = val

    @pl.when(jnp.logical_not(first))
    def _():
        ref[...] += val


TOKEN_SPEC = pl.BlockSpec((8, 128), lambda *_: (0, 0))


def in_proj_fwd(x, g1, w_in_g, after):
    def body(x_ref, g_ref, w_ref, after_ref, h_ref, ci_ref, q_ref, k_ref, v_ref, gc_ref, ga_ref):
        n, _ = _rms(x_ref[...])
        h = (n * g_ref[...]).astype(MM)
        h_ref[...] = h
        outs = dict(ci=ci_ref, q=q_ref, k=k_ref, v=v_ref, gc=gc_ref, ga=ga_ref)
        for j in range(N_CHIPS):
            p = _dot(h, w_ref[j])
            g0 = j * IN_SHARD
            for name, s, e in IN_PIECES:
                lo, hi = max(s, g0), min(e, g0 + IN_SHARD)
                if lo < hi:
                    ref = outs[name]
                    part = p[:, lo - g0:hi - g0]
                    if name == "q":
                        part = part * ATT_SCALE
                    ref[:, lo - s:hi - s] = part.astype(ref.dtype)

    out_shape = [
        jax.ShapeDtypeStruct((SEQ, D_MODEL), MM),
        jax.ShapeDtypeStruct((SEQ, 2 * CONV_DIM), F32),
        jax.ShapeDtypeStruct((SEQ, ATT_DIM), MM),
        jax.ShapeDtypeStruct((SEQ, ATT_DIM), MM),
        jax.ShapeDtypeStruct((SEQ, ATT_DIM), MM),
        jax.ShapeDtypeStruct((SEQ, D_MODEL), F32),
        jax.ShapeDtypeStruct((SEQ, D_MODEL), F32),
    ]
    return pl.pallas_call(
        body, name="in_proj_fwd", grid=(SEQ // TM,), out_shape=out_shape,
        in_specs=[_row_tile_spec(D_MODEL), _full_spec((1, D_MODEL)), _weight_spec(w_in_g.shape), TOKEN_SPEC],
        out_specs=[_row_tile_spec(s.shape[1]) for s in out_shape],
        compiler_params=_params(("arbitrary",)),
    )(x, g1, w_in_g, after)


def _shifted_sum(win, terms):
    by_rot = {}
    for m, coef in terms:
        by_rot.setdefault(m % 8, []).append((m // 8, coef))
    acc = None
    n = win.shape[0]
    for rot in sorted(by_rot):
        shifted = win if rot == 0 else pltpu.roll(win, n - rot, 0)
        for a, coef in by_rot[rot]:
            t = coef * shifted[8 * a:8 * a + CONV_TILE, :]
            acc = t if acc is None else acc + t
    return acc


def _glu_into(ci_ref, upad_ref):
    upad_ref[0:32, :] = jnp.zeros((32, CONV_DIM), F32)

    def step(i, c):
        t0 = pl.multiple_of(i * GLU_ROWS, GLU_ROWS)
        a = ci_ref[pl.ds(t0, GLU_ROWS), 0:CONV_DIM]
        b = ci_ref[pl.ds(t0, GLU_ROWS), CONV_DIM:2 * CONV_DIM]
        upad_ref[pl.ds(t0 + 32, GLU_ROWS), :] = a * _sigmoid(b)
        return c

    lax.fori_loop(0, SEQ // GLU_ROWS, step, 0)


def _layernorm_parts(u1):
    mu = jnp.mean(u1, axis=-1, keepdims=True)
    xc = u1 - mu
    rstd = lax.rsqrt(jnp.mean(xc * xc, axis=-1, keepdims=True) + EPS)
    return xc * rstd, rstd


def conv_fwd(ci, w_dw, b_dw, ln_g, ln_b):
    def body(ci_ref, w_ref, b_ref, g_ref, bb_ref, u1_ref, u3_ref, upad_ref):
        _glu_into(ci_ref, upad_ref)

        def step(i, c):
            t0 = pl.multiple_of(i * CONV_TILE, CONV_TILE)
            win = upad_ref[pl.ds(t0, CONV_WIN), :]
            u1 = _shifted_sum(win, [(j + 2, w_ref[j:j + 1, :]) for j in range(CONV_WIDTH)]) + b_ref[...]
            u1_ref[pl.ds(t0, CONV_TILE), :] = u1
            xh, _ = _layernorm_parts(u1)
            u2 = xh * g_ref[...] + bb_ref[...]
            u3_ref[pl.ds(t0, CONV_TILE), :] = (u2 * _sigmoid(u2)).astype(MM)
            return c

        lax.fori_loop(0, SEQ // CONV_TILE, step, 0)

    return pl.pallas_call(
        body, name="conv_fwd",
        out_shape=[jax.ShapeDtypeStruct((SEQ, CONV_DIM), F32), jax.ShapeDtypeStruct((SEQ, CONV_DIM), MM)],
        in_specs=[VMEM_SPEC] * 5, out_specs=[VMEM_SPEC] * 2,
        scratch_shapes=[pltpu.VMEM((SEQ + 32, CONV_DIM), F32)],
        compiler_params=_params(),
    )(ci, w_dw, b_dw, ln_g, ln_b)


def _softplus(z):
    return jnp.maximum(z, 0.0) + jnp.log(1.0 + jnp.exp(-jnp.abs(z)))


def _cumsum_weights(suffix, with_total):
    n = 256 if with_total else 128
    r = lax.broadcasted_iota(jnp.int32, (128, n), 0)
    c = lax.broadcasted_iota(jnp.int32, (128, n), 1)
    tri = (r >= c) if suffix else (r <= c)
    return jnp.logical_or(tri, c >= 128).astype(MM)


NO_SCORE = -1e30
N_KB = SEQ // TQ


def _score_bias(lane, row, i, j):
    keep = jnp.logical_and(i >= 0, jnp.logical_or(j < i, lane < row))
    return jnp.where(keep, 0.0, NO_SCORE)


def _block_pipeline(n_stages, descending, step, on_query_block=None):
    n_lag = n_stages - 1
    none = jnp.int32(-1)

    def shift(cur, lag):
        step([cur] + [(lag[2 * s], lag[2 * s + 1]) for s in range(n_lag)])
        return (cur[0], cur[1]) + tuple(lag[:-2])

    def outer(i, lag):
        if on_query_block is not None:
            on_query_block(i)

        def inner(n, lag):
            return shift((i, i - n if descending else n), lag)
        return lax.fori_loop(0, i + 1, inner, lag)

    lag = lax.fori_loop(0, N_KB, outer, (none,) * (2 * n_lag))
    lax.fori_loop(0, n_lag, lambda n, lag: shift((none, none), lag), lag)


def _head_masks():
    lane = lax.broadcasted_iota(jnp.int32, (TQ, 128), 1)
    row = lax.broadcasted_iota(jnp.int32, (TQ, 128), 0)
    return lane, row, lane < 64


def _pick_head(x, head0, h):
    zero = jnp.zeros_like(x)
    return jnp.where(head0, x, zero) if h == 0 else jnp.where(head0, zero, x)


N_PAIRS = ATT_DIM // 128


def _split_heads(src_ref, dst_ref):
    _, _, head0 = _head_masks()

    def block(b, c):
        r0 = pl.multiple_of(b * TQ, TQ)
        d0 = pl.multiple_of(b * 2 * TQ, 2 * TQ)
        for p in range(N_PAIRS):
            x = src_ref[pl.ds(r0, TQ), 128 * p:128 * (p + 1)]
            for h in range(2):
                dst_ref[p, pl.ds(d0 + TQ * h, TQ), :] = _pick_head(x, head0, h)
        return c

    lax.fori_loop(0, N_KB, block, 0)


def attn_fwd(q, k, v):
    def body(q_ref, k_ref, v_ref, o_ref, rc_ref, acc_ref, r_ref, z_ref, spb_ref, ab_ref, qm_ref, vm_ref):
        lane, row, _ = _head_masks()
        w = _cumsum_weights(suffix=True, with_total=True)
        _split_heads(q_ref, qm_ref)
        _split_heads(v_ref, vm_ref)
        acc_ref[...] = jnp.zeros_like(acc_ref)
        r_ref[...] = jnp.zeros_like(r_ref)
        rc_ref[...] = jnp.zeros_like(rc_ref)
        z_ref[...] = jnp.full(z_ref.shape, NO_SCORE, F32)
        spb_ref[...] = jnp.zeros_like(spb_ref)
        ab_ref[...] = jnp.zeros_like(ab_ref)

        def step(pairs):
            (i1, j1), (i2, j2), (i3, j3) = pairs
            k1, q2, q3 = (pl.multiple_of(jnp.maximum(b, 0) * TQ, TQ) for b in (j1, i2, i3))
            q1, k3 = (pl.multiple_of(jnp.maximum(b, 0) * 2 * TQ, 2 * TQ) for b in (i1, j3))
            bias1 = _score_bias(lane, row, i1, j1)
            first2 = j2 == i2
            rc_rows = rc_ref[pl.ds(q2, TQ), :]
            for p in range(N_PAIRS):
                cols = slice(128 * p, 128 * (p + 1))
                kb = k_ref[pl.ds(k1, TQ), cols]
                acc_ref[pl.ds(q3, TQ), cols] += _dot(ab_ref[p], vm_ref[p, pl.ds(k3, 2 * TQ), :])
                for h in range(2):
                    hh = 2 * p + h
                    r = _dot(spb_ref[hh], w)
                    r_in = jnp.where(first2, 0.0, r_ref[hh])
                    ab_ref[p, :, 128 * h:128 * (h + 1)] = jnp.exp(z_ref[hh] - (r[:, :128] + r_in)).astype(MM)
                    rc_rows = jnp.where(jnp.logical_and(lane == 16 * hh + j2, i2 >= 0), r_in, rc_rows)
                    r_ref[hh] = r_in + r[:, 128:]
                    z = _dot_nt(qm_ref[p, pl.ds(q1 + TQ * h, TQ), :], kb) + bias1
                    z_ref[hh] = z
                    spb_ref[hh] = _softplus(z).astype(MM)
            rc_ref[pl.ds(q2, TQ), :] = rc_rows

        _block_pipeline(3, True, step)
        o_ref[...] = acc_ref[...].astype(MM)

    return pl.pallas_call(
        body, name="attn_fwd",
        out_shape=[jax.ShapeDtypeStruct((SEQ, ATT_DIM), MM), jax.ShapeDtypeStruct((SEQ, 128), F32)],
        in_specs=[VMEM_SPEC] * 3, out_specs=[VMEM_SPEC] * 2,
        scratch_shapes=[pltpu.VMEM((SEQ, ATT_DIM), F32), pltpu.VMEM((8, TQ, 128), F32),
                        pltpu.VMEM((8, TQ, 128), F32), pltpu.VMEM((8, TQ, 128), MM),
                        pltpu.VMEM((N_PAIRS, TQ, 256), MM), pltpu.VMEM((N_PAIRS, 2 * SEQ, 128), MM),
                        pltpu.VMEM((N_PAIRS, 2 * SEQ, 128), MM)],
        compiler_params=_params(),
    )(q, k, v)


def mix_fwd(u3, att, gc, ga, x, w_cb_g, b_cb, w_ab_g, w_out_g, g2, g3, after):
    def body(u_ref, a_ref, gc_ref, ga_ref, x_ref, wcb_ref, bcb_ref, wab_ref, wout_ref, g2_ref, g3_ref, after_ref,
             co_ref, ao_ref, mg_ref, mix_ref, x2_ref, h2_ref):
        u = u_ref[...]
        a = a_ref[...]
        co = jnp.concatenate([_dot(u, wcb_ref[j]) for j in range(N_CHIPS)], axis=1) + bcb_ref[...]
        ao = jnp.concatenate([_dot(a, wab_ref[j]) for j in range(N_CHIPS)], axis=1)
        co_ref[...] = co.astype(MM)
        ao_ref[...] = ao.astype(MM)
        merged = (_sigmoid(gc_ref[...]) * co + _sigmoid(ga_ref[...]) * ao).astype(MM)
        mg_ref[...] = merged
        mix = _dot(merged, wout_ref[...])
        mix_ref[...] = mix
        n2, _ = _rms(mix)
        x2 = x_ref[...] + n2 * g2_ref[...]
        x2_ref[...] = x2
        n3, _ = _rms(x2)
        h2_ref[...] = (n3 * g3_ref[...]).astype(MM)

    out_shape = [
        jax.ShapeDtypeStruct((SEQ, D_MODEL), MM), jax.ShapeDtypeStruct((SEQ, D_MODEL), MM),
        jax.ShapeDtypeStruct((SEQ, D_MODEL), MM), jax.ShapeDtypeStruct((SEQ, D_MODEL), F32),
        jax.ShapeDtypeStruct((SEQ, D_MODEL), F32), jax.ShapeDtypeStruct((SEQ, D_MODEL), MM),
    ]
    vec = _full_spec((1, D_MODEL))
    return pl.pallas_call(
        body, name="mix_fwd", grid=(SEQ // TM,), out_shape=out_shape,
        in_specs=[_row_tile_spec(CONV_DIM), _row_tile_spec(ATT_DIM), _row_tile_spec(D_MODEL),
                  _row_tile_spec(D_MODEL), _row_tile_spec(D_MODEL), _weight_spec(w_cb_g.shape), vec,
                  _weight_spec(w_ab_g.shape), _weight_spec(w_out_g.shape), vec, vec, TOKEN_SPEC],
        out_specs=[_row_tile_spec(D_MODEL)] * 6,
        compiler_params=_params(("arbitrary",)),
    )(u3, att, gc, ga, x, w_cb_g, b_cb, w_ab_g, w_out_g, g2, g3, after)


def ffn_up_fwd(h2, w_up_g):
    def body(h_ref, wg_ref, wu_ref, gate_ref, up_ref, act_ref):
        h = h_ref[...]
        gate = _dot(h, wg_ref[0])
        up = _dot(h, wu_ref[0])
        gate_ref[...] = gate.astype(MM)
        up_ref[...] = up.astype(MM)
        act_ref[...] = (gate * _sigmoid(gate) * up).astype(MM)

    tile = pl.BlockSpec((TM, UP_SHARD), lambda n, i: (i, n))
    act = jax.ShapeDtypeStruct((SEQ, D_FF), MM)
    return pl.pallas_call(
        body, name="ffn_up_fwd", grid=(2, SEQ // TM), out_shape=[act, act, act],
        in_specs=[pl.BlockSpec((TM, D_MODEL), lambda n, i: (i, 0)),
                  pl.BlockSpec((1, D_MODEL, UP_SHARD), lambda n, i: (n, 0, 0)),
                  pl.BlockSpec((1, D_MODEL, UP_SHARD), lambda n, i: (n + 2, 0, 0))],
        out_specs=[tile, tile, tile],
        compiler_params=_params(("arbitrary", "arbitrary")),
    )(h2, w_up_g, w_up_g)


def ffn_down_loss(act, w_down_g, x2, target, g4):
    def body(act_ref, wd_ref, x2_ref, t_ref, g_ref, dff_ref, dy_ref, loss_ref, dg_ref):
        ff = _dot(act_ref[...], wd_ref[...])
        n4, r4 = _rms(ff)
        g4v = g_ref[...]
        err = x2_ref[...] + n4 * g4v - t_ref[...]
        row_loss = jnp.mean(err * err, axis=-1, keepdims=True)
        loss_ref[...] = jnp.zeros((8, 128), F32) + 0.5 * jnp.sum(row_loss, axis=0, keepdims=True)
        dy = err * (1.0 / D_MODEL)
        dy_ref[...] = dy
        dff_ref[...] = _rms_bwd(dy * g4v, n4, r4).astype(MM)
        _acc_rows(dg_ref, jnp.sum(dy * n4, axis=0, keepdims=True), pl.program_id(0) == 0)

    nt = SEQ // TM
    vec = _full_spec((1, D_MODEL))
    return pl.pallas_call(
        body, name="ffn_down_loss", grid=(nt,),
        out_shape=(jax.ShapeDtypeStruct((SEQ, D_MODEL), MM), jax.ShapeDtypeStruct((SEQ, D_MODEL), F32),
                   jax.ShapeDtypeStruct((nt * 8, 128), F32), jax.ShapeDtypeStruct((1, D_MODEL), F32)),
        in_specs=[_row_tile_spec(D_FF), _weight_spec(w_down_g.shape), _row_tile_spec(D_MODEL),
                  _row_tile_spec(D_MODEL), vec],
        out_specs=[_row_tile_spec(D_MODEL), _row_tile_spec(D_MODEL),
                   pl.BlockSpec((8, 128), lambda i: (i, 0)), vec],
        compiler_params=_params(("arbitrary",)),
    )(act, w_down_g, x2, target, g4)


def ffn_act_bwd(dff, w_down_g, gate, up):
    def body(dff_ref, wd_ref, gate_ref, up_ref, dgu_ref):
        dact = _dot_nt(dff_ref[...], wd_ref[...])
        gate = gate_ref[...].astype(F32)
        sg = _sigmoid(gate)
        dgu_ref[:, 0:D_FF] = (dact * up_ref[...].astype(F32) * (sg * (1.0 + gate * (1.0 - sg)))).astype(MM)
        dgu_ref[:, D_FF:2 * D_FF] = (dact * (gate * sg)).astype(MM)

    return pl.pallas_call(
        body, name="ffn_act_bwd", grid=(SEQ // TM,),
        out_shape=jax.ShapeDtypeStruct((SEQ, 2 * D_FF), MM),
        in_specs=[_row_tile_spec(D_MODEL), _weight_spec(w_down_g.shape), _row_tile_spec(D_FF), _row_tile_spec(D_FF)],
        out_specs=_row_tile_spec(2 * D_FF),
        compiler_params=_params(("arbitrary",)),
    )(dff, w_down_g, gate, up)


def ffn_in_bwd(dgu, w_up_g, x2, mix, dy, g3, g2):
    def body(dgu_ref, w_ref, x2_ref, mix_ref, dy_ref, g3_ref, g2_ref, dx2_ref, dmix_ref, dg3_ref, dg2_ref):
        dh2 = None
        for j in range(N_CHIPS):
            t = _dot_nt(dgu_ref[:, j * UP_SHARD:(j + 1) * UP_SHARD], w_ref[j])
            dh2 = t if dh2 is None else dh2 + t
        first = pl.program_id(0) == 0
        n3, r3 = _rms(x2_ref[...])
        dx2 = dy_ref[...] + _rms_bwd(dh2 * g3_ref[...], n3, r3)
        dx2_ref[...] = dx2
        _acc_rows(dg3_ref, jnp.sum(dh2 * n3, axis=0, keepdims=True), first)
        n2, r2 = _rms(mix_ref[...])
        dmix_ref[...] = _rms_bwd(dx2 * g2_ref[...], n2, r2).astype(MM)
        _acc_rows(dg2_ref, jnp.sum(dx2 * n2, axis=0, keepdims=True), first)

    vec = _full_spec((1, D_MODEL))
    return pl.pallas_call(
        body, name="ffn_in_bwd", grid=(SEQ // TM,),
        out_shape=(jax.ShapeDtypeStruct((SEQ, D_MODEL), F32), jax.ShapeDtypeStruct((SEQ, D_MODEL), MM),
                   jax.ShapeDtypeStruct((1, D_MODEL), F32), jax.ShapeDtypeStruct((1, D_MODEL), F32)),
        in_specs=[_row_tile_spec(2 * D_FF), _weight_spec(w_up_g.shape), _row_tile_spec(D_MODEL),
                  _row_tile_spec(D_MODEL), _row_tile_spec(D_MODEL), vec, vec],
        out_specs=[_row_tile_spec(D_MODEL), _row_tile_spec(D_MODEL), vec, vec],
        compiler_params=_params(("arbitrary",)),
    )(dgu, w_up_g, x2, mix, dy, g3, g2)


def merge_bwd(dmix, w_out_g, gc, ga, co, ao, w_cb_g, w_ab_g, after):
    def body(dmix_ref, wout_ref, gc_ref, ga_ref, co_ref, ao_ref, wcb_ref, wab_ref, after_ref,
             dco_ref, dao_ref, dg_ref, du3_ref, datt_ref, dbcb_ref):
        dm = _dot_nt(dmix_ref[...], wout_ref[...])
        sgc = _sigmoid(gc_ref[...])
        sga = _sigmoid(ga_ref[...])
        dco = dm * sgc
        dao = dm * sga
        dg_ref[:, 0:D_MODEL] = (dm * co_ref[...].astype(F32) * (sgc * (1.0 - sgc))).astype(MM)
        dg_ref[:, D_MODEL:2 * D_MODEL] = (dm * ao_ref[...].astype(F32) * (sga * (1.0 - sga))).astype(MM)
        _acc_rows(dbcb_ref, jnp.sum(dco, axis=0, keepdims=True), pl.program_id(0) == 0)
        dco_ref[...] = dco.astype(MM)
        dao_ref[...] = dao.astype(MM)
        du3 = None
        datt = None
        for j in range(N_CHIPS):
            cols = slice(j * BR_SHARD, (j + 1) * BR_SHARD)
            t = _dot_nt(dco_ref[:, cols], wcb_ref[j])
            s = _dot_nt(dao_ref[:, cols], wab_ref[j])
            du3 = t if du3 is None else du3 + t
            datt = s if datt is None else datt + s
        du3_ref[...] = du3
        datt_ref[...] = datt.astype(MM)

    wide = _row_tile_spec(D_MODEL)
    return pl.pallas_call(
        body, name="merge_bwd", grid=(SEQ // TM,),
        out_shape=(jax.ShapeDtypeStruct((SEQ, D_MODEL), MM), jax.ShapeDtypeStruct((SEQ, D_MODEL), MM),
                   jax.ShapeDtypeStruct((SEQ, 2 * D_MODEL), MM),
                   jax.ShapeDtypeStruct((SEQ, CONV_DIM), F32), jax.ShapeDtypeStruct((SEQ, ATT_DIM), MM),
                   jax.ShapeDtypeStruct((1, D_MODEL), F32)),
        in_specs=[wide, _weight_spec(w_out_g.shape), wide, wide, wide, wide,
                  _weight_spec(w_cb_g.shape), _weight_spec(w_ab_g.shape), TOKEN_SPEC],
        out_specs=[wide, wide, _row_tile_spec(2 * D_MODEL), _row_tile_spec(CONV_DIM), _row_tile_spec(ATT_DIM),
                   _full_spec((1, D_MODEL))],
        compiler_params=_params(("arbitrary",)),
    )(dmix, w_out_g, gc, ga, co, ao, w_cb_g, w_ab_g, after)


def conv_bwd(du3, u1, ci, w_dw, ln_g, ln_b, after):
    def body(du3_ref, u1_ref, ci_ref, w_ref, g_ref, bb_ref, after_ref,
             dci_ref, dw_ref, dbdw_ref, dg_ref, db_ref, upad_ref, dpad_ref, dwacc_ref, vacc_ref):
        _glu_into(ci_ref, upad_ref)
        dpad_ref[SEQ:SEQ + 32, :] = jnp.zeros((32, CONV_DIM), F32)
        dwacc_ref[...] = jnp.zeros_like(dwacc_ref)
        vacc_ref[...] = jnp.zeros_like(vacc_ref)

        def fold8(t):
            s = t[0:8, :]
            for r in range(1, CONV_TILE // 8):
                s = s + t[8 * r:8 * r + 8, :]
            return s

        def pass1(i, c):
            t0 = pl.multiple_of(i * CONV_TILE, CONV_TILE)
            xh, rstd = _layernorm_parts(u1_ref[pl.ds(t0, CONV_TILE), :])
            gv = g_ref[...]
            u2 = xh * gv + bb_ref[...]
            s2 = _sigmoid(u2)
            du2 = du3_ref[pl.ds(t0, CONV_TILE), :] * (s2 * (1.0 + u2 * (1.0 - s2)))
            wv = du2 * gv
            du1 = rstd * (wv - jnp.mean(wv, axis=-1, keepdims=True)
                          - xh * jnp.mean(wv * xh, axis=-1, keepdims=True))
            dpad_ref[pl.ds(t0, CONV_TILE), :] = du1
            vacc_ref[0] += fold8(du2 * xh)
            vacc_ref[1] += fold8(du2)
            vacc_ref[2] += fold8(du1)
            win = upad_ref[pl.ds(t0, CONV_WIN), :]
            n = win.shape[0]
            for rot in range(8):
                shifted = win if rot == 0 else pltpu.roll(win, n - rot, 0)
                for a in range(5):
                    j = 8 * a + rot - 2
                    if 0 <= j < CONV_WIDTH:
                        dwacc_ref[j] += fold8(du1 * shifted[8 * a:8 * a + CONV_TILE, :])
            return c

        lax.fori_loop(0, SEQ // CONV_TILE, pass1, 0)

        def pass2(i, c):
            t0 = pl.multiple_of(i * CONV_TILE, CONV_TILE)
            win = dpad_ref[pl.ds(t0, CONV_WIN), :]
            du0 = _shifted_sum(win, [(30 - j, w_ref[j:j + 1, :]) for j in range(CONV_WIDTH)])
            a = ci_ref[pl.ds(t0, CONV_TILE), 0:CONV_DIM]
            sb = _sigmoid(ci_ref[pl.ds(t0, CONV_TILE), CONV_DIM:2 * CONV_DIM])
            dci_ref[pl.ds(t0, CONV_TILE), 0:CONV_DIM] = (du0 * sb).astype(MM)
            dci_ref[pl.ds(t0, CONV_TILE), CONV_DIM:2 * CONV_DIM] = (du0 * a * (sb * (1.0 - sb))).astype(MM)
            return c

        lax.fori_loop(0, SEQ // CONV_TILE, pass2, 0)

        for j in range(CONV_WIDTH):
            dw_ref[j:j + 1, :] = jnp.sum(dwacc_ref[j], axis=0, keepdims=True)
        dw_ref[CONV_WIDTH:32, :] = jnp.zeros((32 - CONV_WIDTH, CONV_DIM), F32)
        dg_ref[...] = jnp.sum(vacc_ref[0], axis=0, keepdims=True)
        db_ref[...] = jnp.sum(vacc_ref[1], axis=0, keepdims=True)
        dbdw_ref[...] = jnp.sum(vacc_ref[2], axis=0, keepdims=True)

    vec = jax.ShapeDtypeStruct((1, CONV_DIM), F32)
    return pl.pallas_call(
        body, name="conv_bwd",
        out_shape=(jax.ShapeDtypeStruct((SEQ, 2 * CONV_DIM), MM), jax.ShapeDtypeStruct((32, CONV_DIM), F32),
                   vec, vec, vec),
        in_specs=[VMEM_SPEC] * 7, out_specs=[VMEM_SPEC] * 5,
        scratch_shapes=[pltpu.VMEM((SEQ + 32, CONV_DIM), F32), pltpu.VMEM((SEQ + 32, CONV_DIM), F32),
                        pltpu.VMEM((CONV_WIDTH, 8, CONV_DIM), F32), pltpu.VMEM((3, 8, CONV_DIM), F32)],
        compiler_params=_params(),
    )(du3, u1, ci, w_dw, ln_g, ln_b, after)


def attn_bwd(q, k, v, datt, rc, after):
    def body(q_ref, k_ref, v_ref, do_ref, rc_ref, after_ref, dqkv_ref, dqa_ref, dka_ref, dva_ref, pc_ref, z_ref,
             sig1_ref, sig2_ref, g_ref, spb_ref, gb_ref, ar_ref, dzr_ref, dzc_ref, qm_ref, km_ref, dom_ref):
        lane, row, _ = _head_masks()
        _split_heads(q_ref, qm_ref)
        _split_heads(k_ref, km_ref)
        _split_heads(do_ref, dom_ref)
        for ref in (dqa_ref, dka_ref, dva_ref, pc_ref):
            ref[...] = jnp.zeros_like(ref)
        z_ref[...] = jnp.full(z_ref.shape, NO_SCORE, F32)
        for ref in (sig1_ref, sig2_ref, spb_ref, ar_ref, g_ref, gb_ref, dzr_ref, dzc_ref):
            ref[...] = jnp.zeros_like(ref)
        w_suffix = _cumsum_weights(suffix=True, with_total=False)
        w_prefix = _cumsum_weights(suffix=False, with_total=True)

        def step(pairs):
            (ia, ja), (ib, jb), (ic, jc), (id_, jd) = pairs
            ka, qb_, kb_, kc, qd, kd = (pl.multiple_of(jnp.maximum(b, 0) * TQ, TQ) for b in (ja, ib, jb, jc, id_, jd))
            qa2, qb2, qc2, qd2, kd2 = (pl.multiple_of(jnp.maximum(b, 0) * 2 * TQ, 2 * TQ)
                                       for b in (ia, ib, ic, id_, jd))
            bias_a = _score_bias(lane, row, ia, ja)
            rc_rows = rc_ref[pl.ds(qb_, TQ), :]
            first_c = jc == 0
            for p in range(N_PAIRS):
                cols = slice(128 * p, 128 * (p + 1))
                k_a = k_ref[pl.ds(ka, TQ), cols]
                v_b = v_ref[pl.ds(kb_, TQ), cols]
                dqa_ref[pl.ds(qd, TQ), cols] += _dot(dzc_ref[p], km_ref[p, pl.ds(kd2, 2 * TQ), :])
                dka_ref[pl.ds(kd, TQ), cols] += _dot_tn(dzr_ref[p], qm_ref[p, pl.ds(qd2, 2 * TQ), :])
                dva_ref[pl.ds(kc, TQ), cols] += _dot_tn(ar_ref[p], dom_ref[p, pl.ds(qc2, 2 * TQ), :])
                for h in range(2):
                    hh = 2 * p + h
                    rows = slice(TQ * h, TQ * (h + 1))
                    r = _dot(gb_ref[hh], w_prefix)
                    p_in = jnp.where(first_c, 0.0, pc_ref[hh])
                    dz = (g_ref[hh] - sig2_ref[hh] * (r[:, :128] + p_in)).astype(MM)
                    dzc_ref[p, :, rows] = dz
                    dzr_ref[p, rows, :] = dz
                    pc_ref[hh] = p_in + r[:, 128:]
                    r_in = jnp.sum(jnp.where(lane == 16 * hh + jb, rc_rows, 0.0), axis=1, keepdims=True)
                    a = jnp.exp(z_ref[hh] - (_dot(spb_ref[hh], w_suffix) + r_in))
                    g = _dot_nt(dom_ref[p, pl.ds(qb2 + TQ * h, TQ), :], v_b) * a
                    ar_ref[p, rows, :] = a.astype(MM)
                    g_ref[hh] = g
                    gb_ref[hh] = g.astype(MM)
                    sig2_ref[hh] = sig1_ref[hh]
                    z = _dot_nt(qm_ref[p, pl.ds(qa2 + TQ * h, TQ), :], k_a) + bias_a
                    sp = _softplus(z)
                    sig1_ref[hh] = jnp.exp(z - sp)
                    z_ref[hh] = z
                    spb_ref[hh] = sp.astype(MM)

        _block_pipeline(4, False, step)
        dqkv_ref[:, 0:ATT_DIM] = (dqa_ref[...] * ATT_SCALE).astype(MM)
        dqkv_ref[:, ATT_DIM:2 * ATT_DIM] = dka_ref[...].astype(MM)
        dqkv_ref[:, 2 * ATT_DIM:3 * ATT_DIM] = dva_ref[...].astype(MM)

    split = pltpu.VMEM((N_PAIRS, 2 * SEQ, 128), MM)
    return pl.pallas_call(
        body, name="attn_bwd", out_shape=jax.ShapeDtypeStruct((SEQ, 3 * ATT_DIM), MM),
        in_specs=[VMEM_SPEC] * 6, out_specs=VMEM_SPEC,
        scratch_shapes=[pltpu.VMEM((SEQ, ATT_DIM), F32)] * 3 + [pltpu.VMEM((8, TQ, 128), F32)] * 5
                       + [pltpu.VMEM((8, TQ, 128), MM)] * 2
                       + [pltpu.VMEM((N_PAIRS, 2 * TQ, 128), MM)] * 2 + [pltpu.VMEM((N_PAIRS, TQ, 256), MM)]
                       + [split] * 3,
        compiler_params=_params(),
    )(q, k, v, datt, rc, after)


DPROJ_PIECES = ((0, 1024), (1024, 2560), (2560, 4608))


def _dproj_segments(j):
    g0, g1 = j * IN_SHARD, (j + 1) * IN_SHARD
    segs = []
    for p, (s, e) in enumerate(DPROJ_PIECES):
        lo, hi = max(s, g0), min(e, g1)
        if lo < hi:
            segs.append((p, lo - s, lo - g0, hi - lo))
    return segs


def in_proj_bwd(pieces, w_in_g, x, dx2, g1, after):
    def body(p0_ref, p1_ref, p2_ref, w_ref, x_ref, dx2_ref, g_ref, after_ref, dx_ref, dg_ref):
        p_refs = (p0_ref, p1_ref, p2_ref)
        dh = None
        for j in range(N_CHIPS):
            for p, lo, off, width in _dproj_segments(j):
                t = _dot_nt(p_refs[p][:, lo:lo + width], w_ref[j, :, off:off + width])
                dh = t if dh is None else dh + t
        n1, r1 = _rms(x_ref[...])
        dx_ref[...] = dx2_ref[...] + _rms_bwd(dh * g_ref[...], n1, r1)
        _acc_rows(dg_ref, jnp.sum(dh * n1, axis=0, keepdims=True), pl.program_id(0) == 0)

    vec = _full_spec((1, D_MODEL))
    return pl.pallas_call(
        body, name="in_proj_bwd", grid=(SEQ // TM,),
        out_shape=[jax.ShapeDtypeStruct((SEQ, D_MODEL), F32), jax.ShapeDtypeStruct((1, D_MODEL), F32)],
        in_specs=[_row_tile_spec(p.shape[1]) for p in pieces]
                 + [_weight_spec(w_in_g.shape), _row_tile_spec(D_MODEL), _row_tile_spec(D_MODEL), vec, TOKEN_SPEC],
        out_specs=[_row_tile_spec(D_MODEL), vec],
        compiler_params=_params(("arbitrary",)),
    )(*pieces, w_in_g, x, dx2, g1, after)


def weight_grad_in(h1, pieces):
    kh = D_MODEL // 2

    def body(a_ref, p0_ref, p1_ref, p2_ref, o_ref):
        p_refs = (p0_ref, p1_ref, p2_ref)
        a = a_ref[...]
        for j in range(N_CHIPS):
            @pl.when(pl.program_id(1) == j)
            def _():
                for p, lo, off, width in _dproj_segments(j):
                    o_ref[0, 0, :, off:off + width] = _dot_tn(a, p_refs[p][:, lo:lo + width]).astype(MM)

    return pl.pallas_call(
        body, name="dw_in", grid=(2, N_CHIPS), out_shape=jax.ShapeDtypeStruct((N_CHIPS, 2, kh, IN_SHARD), MM),
        in_specs=[pl.BlockSpec((SEQ, kh), lambda h, j: (0, h))] + [_weight_spec(p.shape) for p in pieces],
        out_specs=pl.BlockSpec((1, 1, kh, IN_SHARD), lambda h, j: (j, h, 0, 0)),
        compiler_params=_params(("arbitrary", "arbitrary")),
    )(h1, *pieces)


def weight_grad(a, b, name, col_sharded, tk=None):
    kin, n = a.shape[1], b.shape[1]

    def body(a_ref, b_ref, o_ref):
        if col_sharded:
            o_ref[0, 0] = _dot_tn(a_ref[...], b_ref[...]).astype(MM)
        else:
            o_ref[...] = _dot_tn(a_ref[...], b_ref[...]).astype(MM)

    if col_sharded:
        kh, ns = kin // 2, n // N_CHIPS
        out = jax.ShapeDtypeStruct((N_CHIPS, 2, kh, ns), MM)
        grid = (2, N_CHIPS)
        in_specs = [pl.BlockSpec((SEQ, kh), lambda h, j: (0, h)), pl.BlockSpec((SEQ, ns), lambda h, j: (0, j))]
        out_spec = pl.BlockSpec((1, 1, kh, ns), lambda h, j: (j, h, 0, 0))
        sem = ("arbitrary", "arbitrary")
    else:
        out = jax.ShapeDtypeStruct((kin, n), MM)
        grid = (kin // tk,)
        in_specs = [pl.BlockSpec((SEQ, tk), lambda r: (0, r)), pl.BlockSpec((SEQ, n), lambda r: (0, 0))]
        out_spec = pl.BlockSpec((tk, n), lambda r: (r, 0))
        sem = ("arbitrary",)
    res = pl.pallas_call(
        body, name=name, grid=grid, out_shape=out, in_specs=in_specs, out_specs=out_spec,
        compiler_params=_params(sem),
    )(a, b)
    if not col_sharded:
        res = res.reshape(N_CHIPS, 2, kin // (2 * N_CHIPS), n)
    return res


def _place():
    x, y, c = lax.axis_index("x"), lax.axis_index("y"), lax.axis_index("c")
    chips = [(1 - x, y), (x, 1 - y), (1 - x, 1 - y)]
    return x, y, c, chips


def _rcopy(src, dst, send_sem, recv_sem, dev):
    return pltpu.make_async_remote_copy(src_ref=src, dst_ref=dst, send_sem=send_sem, recv_sem=recv_sem,
                                        device_id=dev, device_id_type=MESH)


class _Gather:
    N_MOVES = 6

    def __init__(self, shapes, w, o, scratch):
        self.n, self.shapes, self.w, self.o = len(w), shapes, w, o
        self.send, self.recv, self.psend, self.precv, self.loc_in, self.loc_out = scratch[:6]
        self.raw, self.stage = scratch[6:6 + self.n], scratch[6 + self.n:]
        x, y, c, self.chips = _place()
        self.c = c
        self.me, k_x, k_y, k_far = 2 * x + y, 2 * (1 - x) + y, 2 * x + (1 - y), 2 * (1 - x) + (1 - y)
        to_x, to_y = (1 - x, y, c), (x, 1 - y, c)
        self.sib = (x, y, 1 - c)
        self.sent_as = [(self.me, 0, to_x), (self.me, 1, to_y), (self.me, 1, to_x), (self.me, 0, to_y),
                        (k_x, 0, to_y), (k_y, 1, to_x)]
        self.arrives_as = [(k_x, 0, to_x), (k_y, 1, to_y), (k_x, 1, to_x), (k_y, 0, to_y),
                           (k_far, 0, to_y), (k_far, 1, to_x)]
        self.sent_on_after = {0: 4, 1: 5}

    @staticmethod
    def scratch(shards):
        n = len(shards)
        sems = pltpu.SemaphoreType.DMA
        m = _Gather.N_MOVES * n
        return ([sems((m,)), sems((m,)), sems((m,)), sems((m,)), sems((n,)), sems((n,))]
                + [pltpu.VMEM(s.shape, s.dtype) for s in shards] + [pltpu.VMEM(s.shape, MM) for s in shards])

    @staticmethod
    def out_shapes(shards):
        return [jax.ShapeDtypeStruct((N_CHIPS,) + s.shape, MM) for s in shards]

    def _rows(self, t, quarter, cc):
        rq = self.shapes[t][0] // 4
        return pl.ds((2 * cc + quarter) * rq, rq)

    def _chip(self, j):
        cx, cy = self.chips[j]
        return 2 * cx + cy, (cx, cy, self.c)

    def local_in(self, t):
        return pltpu.make_async_copy(self.w[t], self.raw[t], self.loc_in.at[t])

    def local_out(self, t):
        return pltpu.make_async_copy(self.stage[t], self.o[t].at[self.me], self.loc_out.at[t])

    def sent(self, i, t):
        k, quarter, dev = self.sent_as[i]
        rows = self._rows(t, quarter, self.c)
        there = self.o[t].at[k, rows, :]
        return _rcopy(self.stage[t].at[rows, :] if i < 4 else there, there,
                      self.send.at[i * self.n + t], self.recv.at[i * self.n + t], dev)

    def arrived(self, i, t):
        k, quarter, dev = self.arrives_as[i]
        blk = self.o[t].at[k, self._rows(t, quarter, self.c), :]
        return _rcopy(blk, blk, self.send.at[i * self.n + t], self.recv.at[i * self.n + t], dev)

    def passed(self, i, t, cc):
        k, quarter, _ = self.arrives_as[i]
        blk = self.o[t].at[k, self._rows(t, quarter, cc), :]
        return _rcopy(blk, blk, self.psend.at[i * self.n + t], self.precv.at[i * self.n + t], self.sib)

    def start(self):
        for t in range(self.n):
            self.local_in(t).start()
        for t in range(self.n):
            self.local_in(t).wait()
            self.stage[t][...] = self.raw[t][...].astype(MM)
            self.local_out(t).start()
        for i in range(4):
            for t in range(self.n):
                self.sent(i, t).start()

    def forward(self):
        for i in range(self.N_MOVES):
            for t in range(self.n):
                self.arrived(i, t).wait_recv()
                if i in self.sent_on_after:
                    self.sent(self.sent_on_after[i], t).start()
                self.passed(i, t, self.c).start()

    def finish(self):
        for i in range(self.N_MOVES):
            for t in range(self.n):
                self.passed(i, t, 1 - self.c).wait_recv()
        for i in range(self.N_MOVES):
            for t in range(self.n):
                self.sent(i, t).wait_send()
                self.passed(i, t, self.c).wait_send()
        for t in range(self.n):
            self.local_out(t).wait()


def all_gather_weights(shards, small, later):
    n, m = len(shards), len(later)
    shapes = [s.shape for s in shards]

    def body(*refs):
        w = refs[:n]
        sm = refs[n]
        lw = refs[n + 1:n + 1 + m]
        o = refs[n + 1 + m:2 * n + 1 + m]
        osm = refs[2 * n + 1 + m]
        lo = refs[2 * n + 2 + m:2 * n + 2 + 2 * m]
        scratch = refs[2 * n + 2 + 2 * m:]
        ssend, srecv, sloc, lsem_in, lsem_out = scratch[:5]
        lraw, lstage = scratch[5:5 + m], scratch[5 + m:5 + 2 * m]
        g = _Gather(shapes, w, o, scratch[5 + 2 * m:])
        own = pltpu.make_async_copy(sm, osm.at[g.me], sloc)
        own.start()
        loads = [pltpu.make_async_copy(lw[t], lraw[t], lsem_in.at[t]) for t in range(m)]
        for cp in loads:
            cp.start()
        g.start()
        small_cps = [_rcopy(sm, osm.at[g.me], ssend.at[j], srecv.at[j], g._chip(j)[1]) for j in range(3)]
        for cp in small_cps:
            cp.start()
        places = []
        for t in range(m):
            loads[t].wait()
            lstage[t][...] = lraw[t][...].astype(MM)
            places.append(pltpu.make_async_copy(lstage[t], lo[t].at[g.me], lsem_out.at[t]))
            places[t].start()
        g.forward()
        g.finish()
        for j in range(3):
            k, dev = g._chip(j)
            _rcopy(sm, osm.at[k], ssend.at[j], srecv.at[j], dev).wait_recv()
            small_cps[j].wait_send()
        own.wait()
        for cp in places:
            cp.wait()

    out_shape = _Gather.out_shapes(shards)
    out_shape.append(jax.ShapeDtypeStruct((N_CHIPS,) + small.shape, small.dtype))
    out_shape += _Gather.out_shapes(later)
    sems = pltpu.SemaphoreType.DMA
    return pl.pallas_call(
        body, name="all_gather_weights", out_shape=out_shape,
        in_specs=[ANY] * (n + 1 + m), out_specs=[ANY] * (n + 1 + m),
        scratch_shapes=[sems((3,)), sems((3,)), sems, sems((m,)), sems((m,))]
                       + [pltpu.VMEM(s.shape, s.dtype) for s in later] + [pltpu.VMEM(s.shape, MM) for s in later]
                       + _Gather.scratch(shards),
        compiler_params=_params(),
    )(*shards, small, *later)


HBM_SPEC = pl.BlockSpec(memory_space=pltpu.HBM)
SEM_SPEC = pl.BlockSpec(memory_space=pltpu.SEMAPHORE)
DATAFLOW = pltpu.SideEffectType.DATAFLOW_SIDE_EFFECTING


def split_start(name, bufs, n_copies, copies):
    nb = len(bufs)

    def body(*refs):
        for cp in copies(refs[:nb], refs[nb], refs[nb + 1]):
            cp.start()
        token = refs[2 * nb + 2]
        token[...] = jnp.zeros_like(token)

    sems = [pltpu.SemaphoreType.DMA((n_copies,))] * 2
    res = pl.pallas_call(
        body, name=name,
        out_shape=sems + [pltpu.HBM(a.shape, a.dtype) for a in bufs] + [jax.ShapeDtypeStruct((8, 128), F32)],
        in_specs=[HBM_SPEC] * nb, out_specs=[SEM_SPEC] * 2 + [HBM_SPEC] * nb + [VMEM_SPEC],
        input_output_aliases={i: 2 + i for i in range(nb)},
        compiler_params=pltpu.CompilerParams(has_side_effects=DATAFLOW),
    )(*[pltpu.with_memory_space_constraint(a, pltpu.HBM) for a in bufs])
    return res[:-1], res[-1]


def split_wait(name, state, after, copies):
    sems, bufs = state[:2], state[2:]
    nb = len(bufs)

    def body(*refs):
        for cp in copies(refs[:nb], refs[nb], refs[nb + 1]):
            cp.wait_send()
            cp.wait_recv()

    return pl.pallas_call(
        body, name=name, out_shape=[pltpu.HBM(a.shape, a.dtype) for a in bufs],
        in_specs=[HBM_SPEC] * nb + [SEM_SPEC] * 2 + [ANY] * len(after), out_specs=[HBM_SPEC] * nb,
        input_output_aliases={i: i for i in range(nb)},
        compiler_params=pltpu.CompilerParams(has_side_effects=DATAFLOW),
    )(*bufs, *sems, *after)


class _Shifted:
    def __init__(self, sems, first):
        self.sems, self.first = sems, first

    @property
    def at(self):
        return self

    def __getitem__(self, i):
        return self.sems.at[self.first + i]


def _scatter_copies(n):
    def copies(refs, send, recv):
        _, _, c, chips = _place()
        return [_rcopy(refs[t].at[2 * cx + cy], refs[n + t].at[j], send.at[3 * t + j], recv.at[3 * t + j], (cx, cy, c))
                for t in range(n) for j, (cx, cy) in enumerate(chips)]
    return copies


def scatter_start(parts, tag):
    lands = [lax.empty((3,) + p.shape[1:], p.dtype) for p in parts]
    return split_start("scatter_start_" + tag, list(parts) + lands, 3 * len(parts), _scatter_copies(len(parts)))


def scatter_wait(state, after, tag):
    n = (len(state) - 2) // 2
    return split_wait("scatter_wait_" + tag, state, after, _scatter_copies(n))[n:]


def _gather_copies(shapes, level):
    n = len(shapes)

    def copies(refs, send, recv):
        x, y, c, chips = _place()
        out = []
        for t in range(n):
            rh = shapes[t][0] // 2
            for j, (cx, cy) in enumerate(chips):
                k, dev = (2 * x + y, (cx, cy, c)) if level == 1 else (2 * cx + cy, (x, y, 1 - c))
                blk = refs[t].at[k, pl.ds(c * rh, rh), :]
                out.append(_rcopy(blk, blk, send.at[3 * t + j], recv.at[3 * t + j], dev))
        return out
    return copies


def _sibling_copies(n, other_half):
    def copies(refs, send, recv):
        x, y, c, _ = _place()
        return [_rcopy(refs[t].at[:, 1 - c] if other_half else refs[t], refs[n + t], send.at[t], recv.at[t],
                       (x, y, 1 - c)) for t in range(n)]
    return copies


def sibling_start(srcs, other_half, tag):
    lands = [lax.empty((a.shape[0],) + a.shape[2:] if other_half else a.shape, a.dtype) for a in srcs]
    return split_start("sibling_start_" + tag, list(srcs) + lands, len(srcs),
                       _sibling_copies(len(srcs), other_half))


def sibling_wait(state, after, other_half, tag):
    n = (len(state) - 2) // 2
    res = split_wait("sibling_wait_" + tag, state, after, _sibling_copies(n, other_half))
    return res[:n], res[n:]


def small_pack(ddw, v512, v1024, loss_parts):
    rows, width = PACK_ROWS, 512
    n512, n1024 = len(VEC512), len(VEC1024)

    def body(*refs):
        ddw_ref = refs[0]
        a_refs = refs[1:1 + n512]
        b_refs = refs[1 + n512:1 + n512 + n1024]
        lp_ref, o_ref, p_ref = refs[1 + n512 + n1024:]
        p_ref[...] = jnp.zeros_like(p_ref)
        p_ref[0:32, :] = ddw_ref[...]
        p_ref[LOSS_ROW:LOSS_ROW + 1, 0:128] = jnp.sum(lp_ref[...], axis=0, keepdims=True) * 0.125
        for i, r in enumerate(a_refs):
            p_ref[32 + i:33 + i, :] = r[...]
        for i, r in enumerate(b_refs):
            base = 32 + n512 + 2 * i
            p_ref[base:base + 1, :] = r[:, 0:512]
            p_ref[base + 1:base + 2, :] = r[:, 512:1024]
        x, y, c, _ = _place()
        o_ref[4 * x + 2 * y + c] = p_ref[...]

    n_in = 2 + n512 + n1024
    return pl.pallas_call(
        body, name="small_pack", out_shape=jax.ShapeDtypeStruct((8, rows, width), F32),
        in_specs=[VMEM_SPEC] * n_in, out_specs=VMEM_SPEC,
        scratch_shapes=[pltpu.VMEM((rows, width), F32)],
    )(ddw, *[v512[n] for n in VEC512], *[v1024[n] for n in VEC1024], loss_parts)


def _small_copies(refs, send, recv):
    x, y, c, _ = _place()
    mine = refs[0].at[4 * x + 2 * y + c]
    peers = [(1 - x if k & 4 else x, 1 - y if k & 2 else y, 1 - c if k & 1 else c) for k in range(1, 8)]
    return [_rcopy(mine, mine, send.at[i], recv.at[i], dev) for i, dev in enumerate(peers)]


def _row_block(r):
    for tr in (512, 352, 256, 128):
        if r % tr == 0:
            return tr
    return r


def add_halves(g, recv, name):
    _, _, r, w = g.shape
    tr = _row_block(r)

    def body(g_ref, r_ref, ob_ref, own_ref):
        k = pl.program_id(1)
        me = 2 * lax.axis_index("x") + lax.axis_index("y")
        t = g_ref[0, 0].astype(F32) + r_ref[0].astype(F32)
        ob_ref[0] = t.astype(MM)
        mine = jnp.where(k == me, t, 0.0)

        @pl.when(k == 0)
        def _():
            own_ref[...] = mine

        @pl.when(k != 0)
        def _():
            own_ref[...] += mine

    return pl.pallas_call(
        body, name=name, grid=(r // tr, N_CHIPS),
        in_specs=[pl.BlockSpec((1, 1, tr, w), lambda i, k: (k, lax.axis_index("c"), i, 0)),
                  pl.BlockSpec((1, tr, w), lambda i, k: (k, i, 0))],
        out_specs=[pl.BlockSpec((1, tr, w), lambda i, k: (k, i, 0)),
                   pl.BlockSpec((tr, w), lambda i, k: (i, 0))],
        out_shape=(jax.ShapeDtypeStruct((N_CHIPS, r, w), MM), jax.ShapeDtypeStruct((r, w), F32)),
        compiler_params=_params(("arbitrary", "arbitrary")),
    )(g, recv)


def sum_parts(own, rin, after, name):
    _, r, w = rin.shape
    tr = _row_block(r)

    def body(o_ref, r_ref, after_ref, out_ref):
        out_ref[...] = ((o_ref[...] + r_ref[0].astype(F32)) + r_ref[1].astype(F32)) + r_ref[2].astype(F32)

    return pl.pallas_call(
        body, name=name, grid=(r // tr,), out_shape=jax.ShapeDtypeStruct((r, w), F32),
        in_specs=[pl.BlockSpec((tr, w), lambda i: (i, 0)), pl.BlockSpec((3, tr, w), lambda i: (0, i, 0)),
                  _full_spec((8, 128))],
        out_specs=pl.BlockSpec((tr, w), lambda i: (i, 0)),
        compiler_params=_params(("arbitrary",)),
    )(own, rin, after)


def _adamw_math(w, g, m, v):
    mn = ADAM_B1 * m + (1.0 - ADAM_B1) * g
    vn = ADAM_B2 * v + (1.0 - ADAM_B2) * (g * g)
    m_hat = mn / (1.0 - ADAM_B1 ** ADAM_STEP)
    v_hat = vn / (1.0 - ADAM_B2 ** ADAM_STEP)
    return -ADAM_LR * (m_hat / (jnp.sqrt(v_hat) + ADAM_EPS) + ADAM_WD * w), mn, vn


def adamw(w, mine, other, m, v, name):
    r, c = w.shape
    rh = r // 2
    tr = _row_block(rh)
    if c >= 1024 and tr % 512 == 0:
        tr = 256
    nb = rh // tr

    def body(w_ref, a_ref, b_ref, m_ref, v_ref, go_ref, d_ref, mo_ref, vo_ref):
        gv = jnp.where(lax.axis_index("c") == pl.program_id(0), a_ref[...], b_ref[...])
        go_ref[...] = gv
        d_ref[...], mo_ref[...], vo_ref[...] = _adamw_math(w_ref[...], gv, m_ref[...], v_ref[...])

    def half(of_sibling):
        def index(h, i):
            owner = lax.axis_index("c")
            owner = 1 - owner if of_sibling else owner
            return jnp.where(h == owner, i, jnp.where(h < owner, 0, nb - 1)), 0
        return pl.BlockSpec((tr, c), index)

    spec = pl.BlockSpec((tr, c), lambda h, i: (h * nb + i, 0))
    out = jax.ShapeDtypeStruct((r, c), F32)
    return pl.pallas_call(
        body, name=name, grid=(2, nb), out_shape=(out, out, out, out),
        in_specs=[spec, half(False), half(True), spec, spec], out_specs=[spec] * 4,
        compiler_params=_params(("arbitrary", "arbitrary")),
    )(w, mine, other, m, v)


def adamw_small(packs, params, after):
    names = list(params)
    flat = [a for n in names for a in params[n]]

    def body(*refs):
        p_ref = refs[0]
        ins = refs[1:1 + 3 * len(names)]
        g_ref = refs[2 + 3 * len(names)]
        outs = refs[3 + 3 * len(names):]
        total = p_ref[0]
        for d in range(1, 8):
            total = total + p_ref[d]
        g_ref[...] = total
        me = 2 * lax.axis_index("x") + lax.axis_index("y")
        for i, n in enumerate(names):
            w_ref, m_ref, v_ref = ins[3 * i:3 * i + 3]
            go_ref, d_ref, mo_ref, vo_ref = outs[4 * i:4 * i + 4]
            if n == "conv_dw_w":
                gv = jnp.zeros((CONV_WIDTH, 128), F32)
                for k in range(N_CHIPS):
                    gv = gv + jnp.where(me == k, g_ref[0:CONV_WIDTH, 128 * k:128 * (k + 1)], 0.0)
            elif n in VEC512:
                r0 = 32 + VEC512.index(n)
                gv = g_ref[r0:r0 + 1, :]
            else:
                r0 = 32 + len(VEC512) + 2 * VEC1024.index(n)
                gv = jnp.concatenate([g_ref[r0:r0 + 1, :], g_ref[r0 + 1:r0 + 2, :]], axis=1)
            go_ref[...] = gv
            d_ref[...], mo_ref[...], vo_ref[...] = _adamw_math(w_ref[...], gv, m_ref[...], v_ref[...])

    out_shape = [jax.ShapeDtypeStruct(packs.shape[1:], F32)]
    out_shape += [jax.ShapeDtypeStruct(params[n][0].shape, F32) for n in names for _ in range(4)]
    res = pl.pallas_call(
        body, name="adamw_small", out_shape=out_shape,
        in_specs=[VMEM_SPEC] * (2 + len(flat)), out_specs=[VMEM_SPEC] * len(out_shape),
        compiler_params=_params(),
    )(packs, *flat, after)
    return res[0], {n: res[1 + 4 * i:5 + 4 * i] for i, n in enumerate(names)}


REST = ("w_ffn_up", "w_ffn_down", "w_out", "w_conv_branch", "w_att_branch")
VEC512 = ("conv_dw_b", "conv_ln_g", "conv_ln_b")
VEC1024 = ("norm_mix_pre", "b_conv_branch", "norm_mix_post", "norm_ffn_pre", "norm_ffn_post")
PACK_ROWS = 48
LOSS_ROW = 47


def kernel(x, norm_mix_pre, w_in, conv_dw_w, conv_dw_b, conv_ln_g, conv_ln_b, w_conv_branch, b_conv_branch, w_att_branch, w_out, norm_mix_post, norm_ffn_pre, w_ffn_up, w_ffn_down, norm_ffn_post, loss_target, m_norm_mix_pre, m_w_in, m_conv_dw_w, m_conv_dw_b, m_conv_ln_g, m_conv_ln_b, m_w_conv_branch, m_b_conv_branch, m_w_att_branch, m_w_out, m_norm_mix_post, m_norm_ffn_pre, m_w_ffn_up, m_w_ffn_down, m_norm_ffn_post, v_norm_mix_pre, v_w_in, v_conv_dw_w, v_conv_dw_b, v_conv_ln_g, v_conv_ln_b, v_w_conv_branch, v_b_conv_branch, v_w_att_branch, v_w_out, v_norm_mix_post, v_norm_ffn_pre, v_w_ffn_up, v_w_ffn_down, v_norm_ffn_post):
    weights = dict(norm_mix_pre=norm_mix_pre, w_in=w_in, conv_dw_w=conv_dw_w, conv_dw_b=conv_dw_b, conv_ln_g=conv_ln_g, conv_ln_b=conv_ln_b, w_conv_branch=w_conv_branch, b_conv_branch=b_conv_branch, w_att_branch=w_att_branch, w_out=w_out, norm_mix_post=norm_mix_post, norm_ffn_pre=norm_ffn_pre, w_ffn_up=w_ffn_up, w_ffn_down=w_ffn_down, norm_ffn_post=norm_ffn_post)
    mom = dict(norm_mix_pre=m_norm_mix_pre, w_in=m_w_in, conv_dw_w=m_conv_dw_w, conv_dw_b=m_conv_dw_b, conv_ln_g=m_conv_ln_g, conv_ln_b=m_conv_ln_b, w_conv_branch=m_w_conv_branch, b_conv_branch=m_b_conv_branch, w_att_branch=m_w_att_branch, w_out=m_w_out, norm_mix_post=m_norm_mix_post, norm_ffn_pre=m_norm_ffn_pre, w_ffn_up=m_w_ffn_up, w_ffn_down=m_w_ffn_down, norm_ffn_post=m_norm_ffn_post)
    var = dict(norm_mix_pre=v_norm_mix_pre, w_in=v_w_in, conv_dw_w=v_conv_dw_w, conv_dw_b=v_conv_dw_b, conv_ln_g=v_conv_ln_g, conv_ln_b=v_conv_ln_b, w_conv_branch=v_w_conv_branch, b_conv_branch=v_b_conv_branch, w_att_branch=v_w_att_branch, w_out=v_w_out, norm_mix_post=v_norm_mix_post, norm_ffn_pre=v_norm_ffn_pre, w_ffn_up=v_w_ffn_up, w_ffn_down=v_w_ffn_down, norm_ffn_post=v_norm_ffn_post)
    order = list(weights)
    grads, deltas, new_m, new_v = {}, {}, {}, {}
    xs = x.reshape(SEQ, D_MODEL)
    tgt = loss_target.reshape(SEQ, D_MODEL)
    row = lambda a: a.reshape(1, -1)
    g1, g2, g3, g4 = (row(weights[n]) for n in ("norm_mix_pre", "norm_mix_post", "norm_ffn_pre", "norm_ffn_post"))
    ln_g, ln_b = row(conv_ln_g), row(conv_ln_b)

    summed, from_chips = {}, {}

    def core_sums(names, state, after, tag):
        own, from_sibling = sibling_wait(state, after, True, tag)
        for n, g, r in zip(names, own, from_sibling):
            summed[n] = add_halves(g, r, "add_" + n)

    def chip_sums(names, after):
        return [sum_parts(summed[n][1], from_chips[n], after, "sum_" + n) for n in names]

    def optimize(names, state, after, tag):
        mine, other = sibling_wait(state, after, False, tag)
        for n, a, b in zip(names, mine, other):
            grads[n], deltas[n], new_m[n], new_v[n] = adamw(weights[n], a, b, mom[n], var[n], "adamw_" + n)

    w_in_g, dw_g, *rest = all_gather_weights([w_in], conv_dw_w, [weights[n] for n in REST])
    w_dw_full = jnp.concatenate([dw_g[k] for k in range(N_CHIPS)], axis=1)
    rest_shapes = [weights[n].shape for n in REST]
    state, token = split_start("gather_start", rest, 3 * len(REST), _gather_copies(rest_shapes, 1))
    h1, ci, q, k, v, gc, ga = in_proj_fwd(xs, g1, w_in_g, token)
    u1, u3 = conv_fwd(ci, w_dw_full, row(conv_dw_b), ln_g, ln_b)
    att, rc = attn_fwd(q, k, v)
    rest = split_wait("gather_wait", state, [att], _gather_copies(rest_shapes, 1))
    pass_copies = _gather_copies(rest_shapes[2:] + rest_shapes[:2], 2)
    n_first = 3 * len(REST[2:])
    state, token = split_start("pass_start", rest[2:] + rest[:2], 3 * len(REST), pass_copies)
    passed = split_wait("pass_mix_wait", state, [], lambda *a: pass_copies(*a)[:n_first])
    w_out_g, w_cb_g, w_ab_g = passed[:3]
    w_out_g = w_out_g.reshape(D_MODEL, D_MODEL)
    co, ao, merged, mix, x2, h2 = mix_fwd(u3, att, gc, ga, xs, w_cb_g, row(b_conv_branch), w_ab_g, w_out_g,
                                          g2, g3, token)
    w_up_g, w_down_g = split_wait(
        "pass_ffn_wait", list(state[:2]) + list(passed[3:]), [h2],
        lambda refs, send, recv: _gather_copies(rest_shapes[:2], 2)(refs, _Shifted(send, n_first), _Shifted(recv, n_first)))
    w_down_g = w_down_g.reshape(D_FF, D_MODEL)
    gate, up, act = ffn_up_fwd(h2, w_up_g)
    dff, dy, loss_parts, dg4 = ffn_down_loss(act, w_down_g, x2, tgt, g4)

    dgu = ffn_act_bwd(dff, w_down_g, gate, up)
    dx2, dmix, dg3, dg2 = ffn_in_bwd(dgu, w_up_g, x2, mix, dy, g3, g2)
    ffn_grads = [weight_grad(h2, dgu, "dw_ffn_up", True), weight_grad(act, dff, "dw_ffn_down", False, tk=UP_SHARD)]
    to_ffn, token = sibling_start(ffn_grads, True, "dw_ffn")
    dco, dao, dg, du3, datt, dbcb = merge_bwd(dmix, w_out_g, gc, ga, co, ao, w_cb_g, w_ab_g, token)
    mix_grads = [weight_grad(merged, dmix, "dw_out", False, tk=512), weight_grad(u3, dco, "dw_conv_branch", True),
                 weight_grad(att, dao, "dw_att_branch", True)]
    to_mix, token = sibling_start(mix_grads, True, "dw_mix")
    core_sums(REST[:2], to_ffn, [token], "dw_ffn")
    core_sums(REST[2:], to_mix, [summed["w_ffn_down"][1]], "dw_mix")
    state, token = scatter_start([summed[n][0] for n in REST], "rest")
    dci, ddw, dbdw, dlng, dlnb = conv_bwd(du3, u1, ci, w_dw_full, ln_g, ln_b, token)
    dqkv = attn_bwd(q, k, v, datt, rc, token)
    from_chips.update(zip(REST, scatter_wait(state, [dci, dqkv], "rest")))
    dproj = (dci, dqkv, dg)
    to_in, token = sibling_start([weight_grad_in(h1, dproj)], True, "dw_in")
    grad_x, dg1 = in_proj_bwd(dproj, w_in_g, xs, dx2, g1, token)
    v512 = dict(conv_dw_b=dbdw, conv_ln_g=dlng, conv_ln_b=dlnb)
    v1024 = dict(norm_mix_pre=dg1, b_conv_branch=dbcb, norm_mix_post=dg2, norm_ffn_pre=dg3, norm_ffn_post=dg4)
    packs = small_pack(ddw, v512, v1024, loss_parts)
    core_sums(("w_in",), to_in, [packs], "dw_in")
    to_chips = summed["w_in"][0]
    landing = lax.empty((3,) + to_chips.shape[1:], to_chips.dtype)

    def scatter_and_packs(refs, send, recv):
        return (_scatter_copies(1)(refs[:2], send, recv)
                + _small_copies(refs[2:], _Shifted(send, 3), _Shifted(recv, 3)))

    state, token = split_start("scatter_start_w_in", [to_chips, landing, packs], 3 + 7, scatter_and_packs)
    swap_up, token = sibling_start(chip_sums(REST[:1], token), False, "sum_ffn_up")
    swap_rest, token = sibling_start(chip_sums(REST[1:], token), False, "sum_rest")
    optimize(REST[:1], swap_up, [token], "sum_ffn_up")
    optimize(REST[1:], swap_rest, [new_v["w_ffn_up"]], "sum_rest")
    _, from_chips["w_in"], packs = split_wait("scatter_wait_w_in", state, [new_v[n] for n in REST], scatter_and_packs)
    swap_in, token = sibling_start(chip_sums(("w_in",), token), False, "sum_w_in")
    as_rows = lambda n, a: a if n == "conv_dw_w" else a.reshape(1, -1)
    small_names = ("conv_dw_w",) + VEC512 + VEC1024
    gsum, small = adamw_small(packs, {n: tuple(as_rows(n, d[n]) for d in (weights, mom, var)) for n in small_names},
                              token)
    loss = gsum[LOSS_ROW, 0]
    optimize(("w_in",), swap_in, [gsum], "sum_w_in")
    for n in small_names:
        grads[n], deltas[n], new_m[n], new_v[n] = (a.reshape(weights[n].shape) for a in small[n])

    return (loss, grad_x.reshape(1, SEQ, D_MODEL), *[grads[n] for n in order], *[deltas[n] for n in order],
            *[new_m[n] for n in order], *[new_v[n] for n in order])
```

```python
import jax
import jax.numpy as jnp
from jax import lax
from jax.experimental import pallas as pl
from jax.experimental.pallas import tpu as pltpu

F32 = jnp.float32
MM = jnp.bfloat16

SEQ = 2048
D_MODEL = 1024
CONV_DIM = 512
ATT_DIM = 512
CONV_WIDTH = 31
D_FF = 2816
IN_COLS = 2 * CONV_DIM + 3 * ATT_DIM + 2 * D_MODEL
N_CHIPS = 4
IN_SHARD = IN_COLS // N_CHIPS
UP_SHARD = 2 * D_FF // N_CHIPS
BR_SHARD = D_MODEL // N_CHIPS
EPS = 1e-6
ATT_SCALE = 0.125

TM = 256
GLU_ROWS = 256
TQ = 128
CONV_TILE = 64
CONV_WIN = CONV_TILE + 32
VMEM_LIMIT = 56 * 1024 * 1024

ADAM_LR = 0.001
ADAM_B1 = 0.9
ADAM_B2 = 0.999
ADAM_EPS = 1e-08
ADAM_WD = 0.01
ADAM_STEP = 10

MESH = pl.DeviceIdType.MESH
ANY = pl.BlockSpec(memory_space=pl.ANY)
VMEM_SPEC = pl.BlockSpec(memory_space=pltpu.VMEM)

NT_DIMS = (((1,), (1,)), ((), ()))
TN_DIMS = (((0,), (0,)), ((), ()))

IN_PIECES = (("ci", 0, 1024), ("q", 1024, 1536), ("k", 1536, 2048), ("v", 2048, 2560),
             ("gc", 2560, 3584), ("ga", 3584, 4608))


def _params(sem=None, vmem=VMEM_LIMIT):
    return pltpu.CompilerParams(dimension_semantics=sem, vmem_limit_bytes=vmem)


def _dot(a, b):
    return jnp.dot(a, b, preferred_element_type=F32)


def _dot_nt(a, b):
    return lax.dot_general(a, b, NT_DIMS, preferred_element_type=F32)


def _dot_tn(a, b):
    return lax.dot_general(a, b, TN_DIMS, preferred_element_type=F32)


def _sigmoid(x):
    return 1.0 / (1.0 + jnp.exp(-x))


def _rms(x):
    r = lax.rsqrt(jnp.mean(x * x, axis=-1, keepdims=True) + EPS)
    return x * r, r


def _rms_bwd(dy_g, n, r):
    return r * (dy_g - n * jnp.mean(dy_g * n, axis=-1, keepdims=True))


def _row_tile_spec(width, tm=TM):
    return pl.BlockSpec((tm, width), lambda i: (i, 0))


def _full_spec(shape):
    nd = len(shape)
    return pl.BlockSpec(shape, lambda *_: (0,) * nd)


def _weight_spec(shape):
    nd = len(shape)
    return pl.BlockSpec(shape, lambda *_: (0,) * nd, pipeline_mode=pl.Buffered(1))


def _acc_rows(ref, val, first):
    @pl.when(first)
    def _():
        ref[...] = val

    @pl.when(jnp.logical_not(first))
    def _():
        ref[...] += val


TOKEN_SPEC = pl.BlockSpec((8, 128), lambda *_: (0, 0))


def in_proj_fwd(x, g1, w_in_g, after):
    def body(x_ref, g_ref, w_ref, after_ref, h_ref, ci_ref, q_ref, k_ref, v_ref, gc_ref, ga_ref):
        n, _ = _rms(x_ref[...])
        h = (n * g_ref[...]).astype(MM)
        h_ref[...] = h
        outs = dict(ci=ci_ref, q=q_ref, k=k_ref, v=v_ref, gc=gc_ref, ga=ga_ref)
        for j in range(N_CHIPS):
            p = _dot(h, w_ref[j])
            g0 = j * IN_SHARD
            for name, s, e in IN_PIECES:
                lo, hi = max(s, g0), min(e, g0 + IN_SHARD)
                if lo < hi:
                    ref = outs[name]
                    part = p[:, lo - g0:hi - g0]
                    if name == "q":
                        part = part * ATT_SCALE
                    ref[:, lo - s:hi - s] = part.astype(ref.dtype)

    out_shape = [
        jax.ShapeDtypeStruct((SEQ, D_MODEL), MM),
        jax.ShapeDtypeStruct((SEQ, 2 * CONV_DIM), F32),
        jax.ShapeDtypeStruct((SEQ, ATT_DIM), MM),
        jax.ShapeDtypeStruct((SEQ, ATT_DIM), MM),
        jax.ShapeDtypeStruct((SEQ, ATT_DIM), MM),
        jax.ShapeDtypeStruct((SEQ, D_MODEL), F32),
        jax.ShapeDtypeStruct((SEQ, D_MODEL), F32),
    ]
    return pl.pallas_call(
        body, name="in_proj_fwd", grid=(SEQ // TM,), out_shape=out_shape,
        in_specs=[_row_tile_spec(D_MODEL), _full_spec((1, D_MODEL)), _weight_spec(w_in_g.shape), TOKEN_SPEC],
        out_specs=[_row_tile_spec(s.shape[1]) for s in out_shape],
        compiler_params=_params(("arbitrary",)),
    )(x, g1, w_in_g, after)


def _shifted_sum(win, terms):
    by_rot = {}
    for m, coef in terms:
        by_rot.setdefault(m % 8, []).append((m // 8, coef))
    acc = None
    n = win.shape[0]
    for rot in sorted(by_rot):
        shifted = win if rot == 0 else pltpu.roll(win, n - rot, 0)
        for a, coef in by_rot[rot]:
            t = coef * shifted[8 * a:8 * a + CONV_TILE, :]
            acc = t if acc is None else acc + t
    return acc


def _glu_into(ci_ref, upad_ref):
    upad_ref[0:32, :] = jnp.zeros((32, CONV_DIM), F32)

    def step(i, c):
        t0 = pl.multiple_of(i * GLU_ROWS, GLU_ROWS)
        a = ci_ref[pl.ds(t0, GLU_ROWS), 0:CONV_DIM]
        b = ci_ref[pl.ds(t0, GLU_ROWS), CONV_DIM:2 * CONV_DIM]
        upad_ref[pl.ds(t0 + 32, GLU_ROWS), :] = a * _sigmoid(b)
        return c

    lax.fori_loop(0, SEQ // GLU_ROWS, step, 0)


def _layernorm_parts(u1):
    mu = jnp.mean(u1, axis=-1, keepdims=True)
    xc = u1 - mu
    rstd = lax.rsqrt(jnp.mean(xc * xc, axis=-1, keepdims=True) + EPS)
    return xc * rstd, rstd


def conv_fwd(ci, w_dw, b_dw, ln_g, ln_b):
    def body(ci_ref, w_ref, b_ref, g_ref, bb_ref, u1_ref, u3_ref, upad_ref):
        _glu_into(ci_ref, upad_ref)

        def step(i, c):
            t0 = pl.multiple_of(i * CONV_TILE, CONV_TILE)
            win = upad_ref[pl.ds(t0, CONV_WIN), :]
            u1 = _shifted_sum(win, [(j + 2, w_ref[j:j + 1, :]) for j in range(CONV_WIDTH)]) + b_ref[...]
            u1_ref[pl.ds(t0, CONV_TILE), :] = u1
            xh, _ = _layernorm_parts(u1)
            u2 = xh * g_ref[...] + bb_ref[...]
            u3_ref[pl.ds(t0, CONV_TILE), :] = (u2 * _sigmoid(u2)).astype(MM)
            return c

        lax.fori_loop(0, SEQ // CONV_TILE, step, 0)

    return pl.pallas_call(
        body, name="conv_fwd",
        out_shape=[jax.ShapeDtypeStruct((SEQ, CONV_DIM), F32), jax.ShapeDtypeStruct((SEQ, CONV_DIM), MM)],
        in_specs=[VMEM_SPEC] * 5, out_specs=[VMEM_SPEC] * 2,
        scratch_shapes=[pltpu.VMEM((SEQ + 32, CONV_DIM), F32)],
        compiler_params=_params(),
    )(ci, w_dw, b_dw, ln_g, ln_b)


def _softplus(z):
    return jnp.maximum(z, 0.0) + jnp.log(1.0 + jnp.exp(-jnp.abs(z)))


def _cumsum_weights(suffix, with_total):
    n = 256 if with_total else 128
    r = lax.broadcasted_iota(jnp.int32, (128, n), 0)
    c = lax.broadcasted_iota(jnp.int32, (128, n), 1)
    tri = (r >= c) if suffix else (r <= c)
    return jnp.logical_or(tri, c >= 128).astype(MM)


NO_SCORE = -1e30
N_KB = SEQ // TQ


def _score_bias(lane, row, i, j):
    keep = jnp.logical_and(i >= 0, jnp.logical_or(j < i, lane < row))
    return jnp.where(keep, 0.0, NO_SCORE)


def _block_pipeline(n_stages, descending, step, on_query_block=None):
    n_lag = n_stages - 1
    none = jnp.int32(-1)

    def shift(cur, lag):
        step([cur] + [(lag[2 * s], lag[2 * s + 1]) for s in range(n_lag)])
        return (cur[0], cur[1]) + tuple(lag[:-2])

    def outer(i, lag):
        if on_query_block is not None:
            on_query_block(i)

        def inner(n, lag):
            return shift((i, i - n if descending else n), lag)
        return lax.fori_loop(0, i + 1, inner, lag)

    lag = lax.fori_loop(0, N_KB, outer, (none,) * (2 * n_lag))
    lax.fori_loop(0, n_lag, lambda n, lag: shift((none, none), lag), lag)


def _head_masks():
    lane = lax.broadcasted_iota(jnp.int32, (TQ, 128), 1)
    row = lax.broadcasted_iota(jnp.int32, (TQ, 128), 0)
    return lane, row, lane < 64


def _pick_head(x, head0, h):
    zero = jnp.zeros_like(x)
    return jnp.where(head0, x, zero) if h == 0 else jnp.where(head0, zero, x)


N_PAIRS = ATT_DIM // 128


def _split_heads(src_ref, dst_ref):
    _, _, head0 = _head_masks()

    def block(b, c):
        r0 = pl.multiple_of(b * TQ, TQ)
        d0 = pl.multiple_of(b * 2 * TQ, 2 * TQ)
        for p in range(N_PAIRS):
            x = src_ref[pl.ds(r0, TQ), 128 * p:128 * (p + 1)]
            for h in range(2):
                dst_ref[p, pl.ds(d0 + TQ * h, TQ), :] = _pick_head(x, head0, h)
        return c

    lax.fori_loop(0, N_KB, block, 0)


def attn_fwd(q, k, v):
    def body(q_ref, k_ref, v_ref, o_ref, rc_ref, acc_ref, r_ref, z_ref, spb_ref, ab_ref, qm_ref, vm_ref):
        lane, row, _ = _head_masks()
        w = _cumsum_weights(suffix=True, with_total=True)
        _split_heads(q_ref, qm_ref)
        _split_heads(v_ref, vm_ref)
        acc_ref[...] = jnp.zeros_like(acc_ref)
        r_ref[...] = jnp.zeros_like(r_ref)
        rc_ref[...] = jnp.zeros_like(rc_ref)
        z_ref[...] = jnp.full(z_ref.shape, NO_SCORE, F32)
        spb_ref[...] = jnp.zeros_like(spb_ref)
        ab_ref[...] = jnp.zeros_like(ab_ref)

        def step(pairs):
            (i1, j1), (i2, j2), (i3, j3) = pairs
            k1, q2, q3 = (pl.multiple_of(jnp.maximum(b, 0) * TQ, TQ) for b in (j1, i2, i3))
            q1, k3 = (pl.multiple_of(jnp.maximum(b, 0) * 2 * TQ, 2 * TQ) for b in (i1, j3))
            bias1 = _score_bias(lane, row, i1, j1)
            first2 = j2 == i2
            rc_rows = rc_ref[pl.ds(q2, TQ), :]
            for p in range(N_PAIRS):
                cols = slice(128 * p, 128 * (p + 1))
                kb = k_ref[pl.ds(k1, TQ), cols]
                acc_ref[pl.ds(q3, TQ), cols] += _dot(ab_ref[p], vm_ref[p, pl.ds(k3, 2 * TQ), :])
                for h in range(2):
                    hh = 2 * p + h
                    r = _dot(spb_ref[hh], w)
                    r_in = jnp.where(first2, 0.0, r_ref[hh])
                    ab_ref[p, :, 128 * h:128 * (h + 1)] = jnp.exp(z_ref[hh] - (r[:, :128] + r_in)).astype(MM)
                    rc_rows = jnp.where(jnp.logical_and(lane == 16 * hh + j2, i2 >= 0), r_in, rc_rows)
                    r_ref[hh] = r_in + r[:, 128:]
                    z = _dot_nt(qm_ref[p, pl.ds(q1 + TQ * h, TQ), :], kb) + bias1
                    z_ref[hh] = z
                    spb_ref[hh] = _softplus(z).astype(MM)
            rc_ref[pl.ds(q2, TQ), :] = rc_rows

        _block_pipeline(3, True, step)
        o_ref[...] = acc_ref[...].astype(MM)

    return pl.pallas_call(
        body, name="attn_fwd",
        out_shape=[jax.ShapeDtypeStruct((SEQ, ATT_DIM), MM), jax.ShapeDtypeStruct((SEQ, 128), F32)],
        in_specs=[VMEM_SPEC] * 3, out_specs=[VMEM_SPEC] * 2,
        scratch_shapes=[pltpu.VMEM((SEQ, ATT_DIM), F32), pltpu.VMEM((8, TQ, 128), F32),
                        pltpu.VMEM((8, TQ, 128), F32), pltpu.VMEM((8, TQ, 128), MM),
                        pltpu.VMEM((N_PAIRS, TQ, 256), MM), pltpu.VMEM((N_PAIRS, 2 * SEQ, 128), MM),
                        pltpu.VMEM((N_PAIRS, 2 * SEQ, 128), MM)],
        compiler_params=_params(),
    )(q, k, v)


def mix_fwd(u3, att, gc, ga, x, w_cb_g, b_cb, w_ab_g, w_out_g, g2, g3, after):
    def body(u_ref, a_ref, gc_ref, ga_ref, x_ref, wcb_ref, bcb_ref, wab_ref, wout_ref, g2_ref, g3_ref, after_ref,
             co_ref, ao_ref, mg_ref, mix_ref, x2_ref, h2_ref):
        u = u_ref[...]
        a = a_ref[...]
        co = jnp.concatenate([_dot(u, wcb_ref[j]) for j in range(N_CHIPS)], axis=1) + bcb_ref[...]
        ao = jnp.concatenate([_dot(a, wab_ref[j]) for j in range(N_CHIPS)], axis=1)
        co_ref[...] = co.astype(MM)
        ao_ref[...] = ao.astype(MM)
        merged = (_sigmoid(gc_ref[...]) * co + _sigmoid(ga_ref[...]) * ao).astype(MM)
        mg_ref[...] = merged
        mix = _dot(merged, wout_ref[...])
        mix_ref[...] = mix
        n2, _ = _rms(mix)
        x2 = x_ref[...] + n2 * g2_ref[...]
        x2_ref[...] = x2
        n3, _ = _rms(x2)
        h2_ref[...] = (n3 * g3_ref[...]).astype(MM)

    out_shape = [
        jax.ShapeDtypeStruct((SEQ, D_MODEL), MM), jax.ShapeDtypeStruct((SEQ, D_MODEL), MM),
        jax.ShapeDtypeStruct((SEQ, D_MODEL), MM), jax.ShapeDtypeStruct((SEQ, D_MODEL), F32),
        jax.ShapeDtypeStruct((SEQ, D_MODEL), F32), jax.ShapeDtypeStruct((SEQ, D_MODEL), MM),
    ]
    vec = _full_spec((1, D_MODEL))
    return pl.pallas_call(
        body, name="mix_fwd", grid=(SEQ // TM,), out_shape=out_shape,
        in_specs=[_row_tile_spec(CONV_DIM), _row_tile_spec(ATT_DIM), _row_tile_spec(D_MODEL),
                  _row_tile_spec(D_MODEL), _row_tile_spec(D_MODEL), _weight_spec(w_cb_g.shape), vec,
                  _weight_spec(w_ab_g.shape), _weight_spec(w_out_g.shape), vec, vec, TOKEN_SPEC],
        out_specs=[_row_tile_spec(D_MODEL)] * 6,
        compiler_params=_params(("arbitrary",)),
    )(u3, att, gc, ga, x, w_cb_g, b_cb, w_ab_g, w_out_g, g2, g3, after)


def ffn_up_fwd(h2, w_up_g):
    def body(h_ref, wg_ref, wu_ref, gate_ref, up_ref, act_ref):
        h = h_ref[...]
        gate = _dot(h, wg_ref[0])
        up = _dot(h, wu_ref[0])
        gate_ref[...] = gate.astype(MM)
        up_ref[...] = up.astype(MM)
        act_ref[...] = (gate * _sigmoid(gate) * up).astype(MM)

    tile = pl.BlockSpec((TM, UP_SHARD), lambda n, i: (i, n))
    act = jax.ShapeDtypeStruct((SEQ, D_FF), MM)
    return pl.pallas_call(
        body, name="ffn_up_fwd", grid=(2, SEQ // TM), out_shape=[act, act, act],
        in_specs=[pl.BlockSpec((TM, D_MODEL), lambda n, i: (i, 0)),
                  pl.BlockSpec((1, D_MODEL, UP_SHARD), lambda n, i: (n, 0, 0)),
                  pl.BlockSpec((1, D_MODEL, UP_SHARD), lambda n, i: (n + 2, 0, 0))],
        out_specs=[tile, tile, tile],
        compiler_params=_params(("arbitrary", "arbitrary")),
    )(h2, w_up_g, w_up_g)


def ffn_down_loss(act, w_down_g, x2, target, g4):
    def body(act_ref, wd_ref, x2_ref, t_ref, g_ref, dff_ref, dy_ref, loss_ref, dg_ref):
        ff = _dot(act_ref[...], wd_ref[...])
        n4, r4 = _rms(ff)
        g4v = g_ref[...]
        err = x2_ref[...] + n4 * g4v - t_ref[...]
        row_loss = jnp.mean(err * err, axis=-1, keepdims=True)
        loss_ref[...] = jnp.zeros((8, 128), F32) + 0.5 * jnp.sum(row_loss, axis=0, keepdims=True)
        dy = err * (1.0 / D_MODEL)
        dy_ref[...] = dy
        dff_ref[...] = _rms_bwd(dy * g4v, n4, r4).astype(MM)
        _acc_rows(dg_ref, jnp.sum(dy * n4, axis=0, keepdims=True), pl.program_id(0) == 0)

    nt = SEQ // TM
    vec = _full_spec((1, D_MODEL))
    return pl.pallas_call(
        body, name="ffn_down_loss", grid=(nt,),
        out_shape=(jax.ShapeDtypeStruct((SEQ, D_MODEL), MM), jax.ShapeDtypeStruct((SEQ, D_MODEL), F32),
                   jax.ShapeDtypeStruct((nt * 8, 128), F32), jax.ShapeDtypeStruct((1, D_MODEL), F32)),
        in_specs=[_row_tile_spec(D_FF), _weight_spec(w_down_g.shape), _row_tile_spec(D_MODEL),
                  _row_tile_spec(D_MODEL), vec],
        out_specs=[_row_tile_spec(D_MODEL), _row_tile_spec(D_MODEL),
                   pl.BlockSpec((8, 128), lambda i: (i, 0)), vec],
        compiler_params=_params(("arbitrary",)),
    )(act, w_down_g, x2, target, g4)


def ffn_act_bwd(dff, w_down_g, gate, up):
    def body(dff_ref, wd_ref, gate_ref, up_ref, dgu_ref):
        dact = _dot_nt(dff_ref[...], wd_ref[...])
        gate = gate_ref[...].astype(F32)
        sg = _sigmoid(gate)
        dgu_ref[:, 0:D_FF] = (dact * up_ref[...].astype(F32) * (sg * (1.0 + gate * (1.0 - sg)))).astype(MM)
        dgu_ref[:, D_FF:2 * D_FF] = (dact * (gate * sg)).astype(MM)

    return pl.pallas_call(
        body, name="ffn_act_bwd", grid=(SEQ // TM,),
        out_shape=jax.ShapeDtypeStruct((SEQ, 2 * D_FF), MM),
        in_specs=[_row_tile_spec(D_MODEL), _weight_spec(w_down_g.shape), _row_tile_spec(D_FF), _row_tile_spec(D_FF)],
        out_specs=_row_tile_spec(2 * D_FF),
        compiler_params=_params(("arbitrary",)),
    )(dff, w_down_g, gate, up)


def ffn_in_bwd(dgu, w_up_g, x2, mix, dy, g3, g2):
    def body(dgu_ref, w_ref, x2_ref, mix_ref, dy_ref, g3_ref, g2_ref, dx2_ref, dmix_ref, dg3_ref, dg2_ref):
        dh2 = None
        for j in range(N_CHIPS):
            t = _dot_nt(dgu_ref[:, j * UP_SHARD:(j + 1) * UP_SHARD], w_ref[j])
            dh2 = t if dh2 is None else dh2 + t
        first = pl.program_id(0) == 0
        n3, r3 = _rms(x2_ref[...])
        dx2 = dy_ref[...] + _rms_bwd(dh2 * g3_ref[...], n3, r3)
        dx2_ref[...] = dx2
        _acc_rows(dg3_ref, jnp.sum(dh2 * n3, axis=0, keepdims=True), first)
        n2, r2 = _rms(mix_ref[...])
        dmix_ref[...] = _rms_bwd(dx2 * g2_ref[...], n2, r2).astype(MM)
        _acc_rows(dg2_ref, jnp.sum(dx2 * n2, axis=0, keepdims=True), first)

    vec = _full_spec((1, D_MODEL))
    return pl.pallas_call(
        body, name="ffn_in_bwd", grid=(SEQ // TM,),
        out_shape=(jax.ShapeDtypeStruct((SEQ, D_MODEL), F32), jax.ShapeDtypeStruct((SEQ, D_MODEL), MM),
                   jax.ShapeDtypeStruct((1, D_MODEL), F32), jax.ShapeDtypeStruct((1, D_MODEL), F32)),
        in_specs=[_row_tile_spec(2 * D_FF), _weight_spec(w_up_g.shape), _row_tile_spec(D_MODEL),
                  _row_tile_spec(D_MODEL), _row_tile_spec(D_MODEL), vec, vec],
        out_specs=[_row_tile_spec(D_MODEL), _row_tile_spec(D_MODEL), vec, vec],
        compiler_params=_params(("arbitrary",)),
    )(dgu, w_up_g, x2, mix, dy, g3, g2)


def merge_bwd(dmix, w_out_g, gc, ga, co, ao, w_cb_g, w_ab_g, after):
    def body(dmix_ref, wout_ref, gc_ref, ga_ref, co_ref, ao_ref, wcb_ref, wab_ref, after_ref,
             dco_ref, dao_ref, dg_ref, du3_ref, datt_ref, dbcb_ref):
        dm = _dot_nt(dmix_ref[...], wout_ref[...])
        sgc = _sigmoid(gc_ref[...])
        sga = _sigmoid(ga_ref[...])
        dco = dm * sgc
        dao = dm * sga
        dg_ref[:, 0:D_MODEL] = (dm * co_ref[...].astype(F32) * (sgc * (1.0 - sgc))).astype(MM)
        dg_ref[:, D_MODEL:2 * D_MODEL] = (dm * ao_ref[...].astype(F32) * (sga * (1.0 - sga))).astype(MM)
        _acc_rows(dbcb_ref, jnp.sum(dco, axis=0, keepdims=True), pl.program_id(0) == 0)
        dco_ref[...] = dco.astype(MM)
        dao_ref[...] = dao.astype(MM)
        du3 = None
        datt = None
        for j in range(N_CHIPS):
            cols = slice(j * BR_SHARD, (j + 1) * BR_SHARD)
            t = _dot_nt(dco_ref[:, cols], wcb_ref[j])
            s = _dot_nt(dao_ref[:, cols], wab_ref[j])
            du3 = t if du3 is None else du3 + t
            datt = s if datt is None else datt + s
        du3_ref[...] = du3
        datt_ref[...] = datt.astype(MM)

    wide = _row_tile_spec(D_MODEL)
    return pl.pallas_call(
        body, name="merge_bwd", grid=(SEQ // TM,),
        out_shape=(jax.ShapeDtypeStruct((SEQ, D_MODEL), MM), jax.ShapeDtypeStruct((SEQ, D_MODEL), MM),
                   jax.ShapeDtypeStruct((SEQ, 2 * D_MODEL), MM),
                   jax.ShapeDtypeStruct((SEQ, CONV_DIM), F32), jax.ShapeDtypeStruct((SEQ, ATT_DIM), MM),
                   jax.ShapeDtypeStruct((1, D_MODEL), F32)),
        in_specs=[wide, _weight_spec(w_out_g.shape), wide, wide, wide, wide,
                  _weight_spec(w_cb_g.shape), _weight_spec(w_ab_g.shape), TOKEN_SPEC],
        out_specs=[wide, wide, _row_tile_spec(2 * D_MODEL), _row_tile_spec(CONV_DIM), _row_tile_spec(ATT_DIM),
                   _full_spec((1, D_MODEL))],
        compiler_params=_params(("arbitrary",)),
    )(dmix, w_out_g, gc, ga, co, ao, w_cb_g, w_ab_g, after)


def conv_bwd(du3, u1, ci, w_dw, ln_g, ln_b, after):
    def body(du3_ref, u1_ref, ci_ref, w_ref, g_ref, bb_ref, after_ref,
             dci_ref, dw_ref, dbdw_ref, dg_ref, db_ref, upad_ref, dpad_ref, dwacc_ref, vacc_ref):
        _glu_into(ci_ref, upad_ref)
        dpad_ref[SEQ:SEQ + 32, :] = jnp.zeros((32, CONV_DIM), F32)
        dwacc_ref[...] = jnp.zeros_like(dwacc_ref)
        vacc_ref[...] = jnp.zeros_like(vacc_ref)

        def fold8(t):
            s = t[0:8, :]
            for r in range(1, CONV_TILE // 8):
                s = s + t[8 * r:8 * r + 8, :]
            return s

        def pass1(i, c):
            t0 = pl.multiple_of(i * CONV_TILE, CONV_TILE)
            xh, rstd = _layernorm_parts(u1_ref[pl.ds(t0, CONV_TILE), :])
            gv = g_ref[...]
            u2 = xh * gv + bb_ref[...]
            s2 = _sigmoid(u2)
            du2 = du3_ref[pl.ds(t0, CONV_TILE), :] * (s2 * (1.0 + u2 * (1.0 - s2)))
            wv = du2 * gv
            du1 = rstd * (wv - jnp.mean(wv, axis=-1, keepdims=True)
                          - xh * jnp.mean(wv * xh, axis=-1, keepdims=True))
            dpad_ref[pl.ds(t0, CONV_TILE), :] = du1
            vacc_ref[0] += fold8(du2 * xh)
            vacc_ref[1] += fold8(du2)
            vacc_ref[2] += fold8(du1)
            win = upad_ref[pl.ds(t0, CONV_WIN), :]
            n = win.shape[0]
            for rot in range(8):
                shifted = win if rot == 0 else pltpu.roll(win, n - rot, 0)
                for a in range(5):
                    j = 8 * a + rot - 2
                    if 0 <= j < CONV_WIDTH:
                        dwacc_ref[j] += fold8(du1 * shifted[8 * a:8 * a + CONV_TILE, :])
            return c

        lax.fori_loop(0, SEQ // CONV_TILE, pass1, 0)

        def pass2(i, c):
            t0 = pl.multiple_of(i * CONV_TILE, CONV_TILE)
            win = dpad_ref[pl.ds(t0, CONV_WIN), :]
            du0 = _shifted_sum(win, [(30 - j, w_ref[j:j + 1, :]) for j in range(CONV_WIDTH)])
            a = ci_ref[pl.ds(t0, CONV_TILE), 0:CONV_DIM]
            sb = _sigmoid(ci_ref[pl.ds(t0, CONV_TILE), CONV_DIM:2 * CONV_DIM])
            dci_ref[pl.ds(t0, CONV_TILE), 0:CONV_DIM] = (du0 * sb).astype(MM)
            dci_ref[pl.ds(t0, CONV_TILE), CONV_DIM:2 * CONV_DIM] = (du0 * a * (sb * (1.0 - sb))).astype(MM)
            return c

        lax.fori_loop(0, SEQ // CONV_TILE, pass2, 0)

        for j in range(CONV_WIDTH):
            dw_ref[j:j + 1, :] = jnp.sum(dwacc_ref[j], axis=0, keepdims=True)
        dw_ref[CONV_WIDTH:32, :] = jnp.zeros((32 - CONV_WIDTH, CONV_DIM), F32)
        dg_ref[...] = jnp.sum(vacc_ref[0], axis=0, keepdims=True)
        db_ref[...] = jnp.sum(vacc_ref[1], axis=0, keepdims=True)
        dbdw_ref[...] = jnp.sum(vacc_ref[2], axis=0, keepdims=True)

    vec = jax.ShapeDtypeStruct((1, CONV_DIM), F32)
    return pl.pallas_call(
        body, name="conv_bwd",
        out_shape=(jax.ShapeDtypeStruct((SEQ, 2 * CONV_DIM), MM), jax.ShapeDtypeStruct((32, CONV_DIM), F32),
                   vec, vec, vec),
        in_specs=[VMEM_SPEC] * 7, out_specs=[VMEM_SPEC] * 5,
        scratch_shapes=[pltpu.VMEM((SEQ + 32, CONV_DIM), F32), pltpu.VMEM((SEQ + 32, CONV_DIM), F32),
                        pltpu.VMEM((CONV_WIDTH, 8, CONV_DIM), F32), pltpu.VMEM((3, 8, CONV_DIM), F32)],
        compiler_params=_params(),
    )(du3, u1, ci, w_dw, ln_g, ln_b, after)


def attn_bwd(q, k, v, datt, rc, after):
    def body(q_ref, k_ref, v_ref, do_ref, rc_ref, after_ref, dqkv_ref, dqa_ref, dka_ref, dva_ref, pc_ref, z_ref,
             sig1_ref, sig2_ref, g_ref, spb_ref, gb_ref, ar_ref, dzr_ref, dzc_ref, qm_ref, km_ref, dom_ref):
        lane, row, _ = _head_masks()
        _split_heads(q_ref, qm_ref)
        _split_heads(k_ref, km_ref)
        _split_heads(do_ref, dom_ref)
        for ref in (dqa_ref, dka_ref, dva_ref, pc_ref):
            ref[...] = jnp.zeros_like(ref)
        z_ref[...] = jnp.full(z_ref.shape, NO_SCORE, F32)
        for ref in (sig1_ref, sig2_ref, spb_ref, ar_ref, g_ref, gb_ref, dzr_ref, dzc_ref):
            ref[...] = jnp.zeros_like(ref)
        w_suffix = _cumsum_weights(suffix=True, with_total=False)
        w_prefix = _cumsum_weights(suffix=False, with_total=True)

        def step(pairs):
            (ia, ja), (ib, jb), (ic, jc), (id_, jd) = pairs
            ka, qb_, kb_, kc, qd, kd = (pl.multiple_of(jnp.maximum(b, 0) * TQ, TQ) for b in (ja, ib, jb, jc, id_, jd))
            qa2, qb2, qc2, qd2, kd2 = (pl.multiple_of(jnp.maximum(b, 0) * 2 * TQ, 2 * TQ)
                                       for b in (ia, ib, ic, id_, jd))
            bias_a = _score_bias(lane, row, ia, ja)
            rc_rows = rc_ref[pl.ds(qb_, TQ), :]
            first_c = jc == 0
            for p in range(N_PAIRS):
                cols = slice(128 * p, 128 * (p + 1))
                k_a = k_ref[pl.ds(ka, TQ), cols]
                v_b = v_ref[pl.ds(kb_, TQ), cols]
                dqa_ref[pl.ds(qd, TQ), cols] += _dot(dzc_ref[p], km_ref[p, pl.ds(kd2, 2 * TQ), :])
                dka_ref[pl.ds(kd, TQ), cols] += _dot_tn(dzr_ref[p], qm_ref[p, pl.ds(qd2, 2 * TQ), :])
                dva_ref[pl.ds(kc, TQ), cols] += _dot_tn(ar_ref[p], dom_ref[p, pl.ds(qc2, 2 * TQ), :])
                for h in range(2):
                    hh = 2 * p + h
                    rows = slice(TQ * h, TQ * (h + 1))
                    r = _dot(gb_ref[hh], w_prefix)
                    p_in = jnp.where(first_c, 0.0, pc_ref[hh])
                    dz = (g_ref[hh] - sig2_ref[hh] * (r[:, :128] + p_in)).astype(MM)
                    dzc_ref[p, :, rows] = dz
                    dzr_ref[p, rows, :] = dz
                    pc_ref[hh] = p_in + r[:, 128:]
                    r_in = jnp.sum(jnp.where(lane == 16 * hh + jb, rc_rows, 0.0), axis=1, keepdims=True)
                    a = jnp.exp(z_ref[hh] - (_dot(spb_ref[hh], w_suffix) + r_in))
                    g = _dot_nt(dom_ref[p, pl.ds(qb2 + TQ * h, TQ), :], v_b) * a
                    ar_ref[p, rows, :] = a.astype(MM)
                    g_ref[hh] = g
                    gb_ref[hh] = g.astype(MM)
                    sig2_ref[hh] = sig1_ref[hh]
                    z = _dot_nt(qm_ref[p, pl.ds(qa2 + TQ * h, TQ), :], k_a) + bias_a
                    sp = _softplus(z)
                    sig1_ref[hh] = jnp.exp(z - sp)
                    z_ref[hh] = z
                    spb_ref[hh] = sp.astype(MM)

        _block_pipeline(4, False, step)
        dqkv_ref[:, 0:ATT_DIM] = (dqa_ref[...] * ATT_SCALE).astype(MM)
        dqkv_ref[:, ATT_DIM:2 * ATT_DIM] = dka_ref[...].astype(MM)
        dqkv_ref[:, 2 * ATT_DIM:3 * ATT_DIM] = dva_ref[...].astype(MM)

    split = pltpu.VMEM((N_PAIRS, 2 * SEQ, 128), MM)
    return pl.pallas_call(
        body, name="attn_bwd", out_shape=jax.ShapeDtypeStruct((SEQ, 3 * ATT_DIM), MM),
        in_specs=[VMEM_SPEC] * 6, out_specs=VMEM_SPEC,
        scratch_shapes=[pltpu.VMEM((SEQ, ATT_DIM), F32)] * 3 + [pltpu.VMEM((8, TQ, 128), F32)] * 5
                       + [pltpu.VMEM((8, TQ, 128), MM)] * 2
                       + [pltpu.VMEM((N_PAIRS, 2 * TQ, 128), MM)] * 2 + [pltpu.VMEM((N_PAIRS, TQ, 256), MM)]
                       + [split] * 3,
        compiler_params=_params(),
    )(q, k, v, datt, rc, after)


DPROJ_PIECES = ((0, 1024), (1024, 2560), (2560, 4608))


def _dproj_segments(j):
    g0, g1 = j * IN_SHARD, (j + 1) * IN_SHARD
    segs = []
    for p, (s, e) in enumerate(DPROJ_PIECES):
        lo, hi = max(s, g0), min(e, g1)
        if lo < hi:
            segs.append((p, lo - s, lo - g0, hi - lo))
    return segs


def in_proj_bwd(pieces, w_in_g, x, dx2, g1, after):
    def body(p0_ref, p1_ref, p2_ref, w_ref, x_ref, dx2_ref, g_ref, after_ref, dx_ref, dg_ref):
        p_refs = (p0_ref, p1_ref, p2_ref)
        dh = None
        for j in range(N_CHIPS):
            for p, lo, off, width in _dproj_segments(j):
                t = _dot_nt(p_refs[p][:, lo:lo + width], w_ref[j, :, off:off + width])
                dh = t if dh is None else dh + t
        n1, r1 = _rms(x_ref[...])
        dx_ref[...] = dx2_ref[...] + _rms_bwd(dh * g_ref[...], n1, r1)
        _acc_rows(dg_ref, jnp.sum(dh * n1, axis=0, keepdims=True), pl.program_id(0) == 0)

    vec = _full_spec((1, D_MODEL))
    return pl.pallas_call(
        body, name="in_proj_bwd", grid=(SEQ // TM,),
        out_shape=[jax.ShapeDtypeStruct((SEQ, D_MODEL), F32), jax.ShapeDtypeStruct((1, D_MODEL), F32)],
        in_specs=[_row_tile_spec(p.shape[1]) for p in pieces]
                 + [_weight_spec(w_in_g.shape), _row_tile_spec(D_MODEL), _row_tile_spec(D_MODEL), vec, TOKEN_SPEC],
        out_specs=[_row_tile_spec(D_MODEL), vec],
        compiler_params=_params(("arbitrary",)),
    )(*pieces, w_in_g, x, dx2, g1, after)


def weight_grad_in(h1, pieces):
    kh = D_MODEL // 2

    def body(a_ref, p0_ref, p1_ref, p2_ref, o_ref):
        p_refs = (p0_ref, p1_ref, p2_ref)
        a = a_ref[...]
        for j in range(N_CHIPS):
            @pl.when(pl.program_id(1) == j)
            def _():
                for p, lo, off, width in _dproj_segments(j):
                    o_ref[0, 0, :, off:off + width] = _dot_tn(a, p_refs[p][:, lo:lo + width]).astype(MM)

    return pl.pallas_call(
        body, name="dw_in", grid=(2, N_CHIPS), out_shape=jax.ShapeDtypeStruct((N_CHIPS, 2, kh, IN_SHARD), MM),
        in_specs=[pl.BlockSpec((SEQ, kh), lambda h, j: (0, h))] + [_weight_spec(p.shape) for p in pieces],
        out_specs=pl.BlockSpec((1, 1, kh, IN_SHARD), lambda h, j: (j, h, 0, 0)),
        compiler_params=_params(("arbitrary", "arbitrary")),
    )(h1, *pieces)


def weight_grad(a, b, name, col_sharded, tk=None):
    kin, n = a.shape[1], b.shape[1]

    def body(a_ref, b_ref, o_ref):
        if col_sharded:
            o_ref[0, 0] = _dot_tn(a_ref[...], b_ref[...]).astype(MM)
        else:
            o_ref[...] = _dot_tn(a_ref[...], b_ref[...]).astype(MM)

    if col_sharded:
        kh, ns = kin // 2, n // N_CHIPS
        out = jax.ShapeDtypeStruct((N_CHIPS, 2, kh, ns), MM)
        grid = (2, N_CHIPS)
        in_specs = [pl.BlockSpec((SEQ, kh), lambda h, j: (0, h)), pl.BlockSpec((SEQ, ns), lambda h, j: (0, j))]
        out_spec = pl.BlockSpec((1, 1, kh, ns), lambda h, j: (j, h, 0, 0))
        sem = ("arbitrary", "arbitrary")
    else:
        out = jax.ShapeDtypeStruct((kin, n), MM)
        grid = (kin // tk,)
        in_specs = [pl.BlockSpec((SEQ, tk), lambda r: (0, r)), pl.BlockSpec((SEQ, n), lambda r: (0, 0))]
        out_spec = pl.BlockSpec((tk, n), lambda r: (r, 0))
        sem = ("arbitrary",)
    res = pl.pallas_call(
        body, name=name, grid=grid, out_shape=out, in_specs=in_specs, out_specs=out_spec,
        compiler_params=_params(sem),
    )(a, b)
    if not col_sharded:
        res = res.reshape(N_CHIPS, 2, kin // (2 * N_CHIPS), n)
    return res


def weight_grad_mix(merged, dmix, u3, dco, att, dao):
    operands = (merged, dmix, u3, dco, att, dao)
    n_out, n_br = D_MODEL // 2, CONV_DIM // 2

    def body(*refs):
        hbm, (o_out, o_cb, o_ab), bufs, sems = refs[:6], refs[6:9], refs[9:15], refs[15]
        copies = [pltpu.make_async_copy(hbm[i], bufs[i], sems.at[i]) for i in range(6)]
        for cp in copies:
            cp.start()
        m_ref, dm_ref, u_ref, dco_ref, a_ref, dao_ref = bufs
        copies[0].wait()
        copies[1].wait()
        for h in range(2):
            o_out[h * n_out:(h + 1) * n_out, :] = _dot_tn(m_ref[:, h * n_out:(h + 1) * n_out], dm_ref[...]).astype(MM)
        for br, (a, d, o) in enumerate(((u_ref, dco_ref, o_cb), (a_ref, dao_ref, o_ab))):
            copies[2 + 2 * br].wait()
            copies[3 + 2 * br].wait()
            for h in range(2):
                g = _dot_tn(a[:, h * n_br:(h + 1) * n_br], d[...])
                for j in range(N_CHIPS):
                    o[j, h] = g[:, j * BR_SHARD:(j + 1) * BR_SHARD].astype(MM)

    branch = jax.ShapeDtypeStruct((N_CHIPS, 2, n_br, BR_SHARD), MM)
    dw_out, dw_cb, dw_ab = pl.pallas_call(
        body, name="dw_mix", out_shape=[jax.ShapeDtypeStruct((D_MODEL, D_MODEL), MM), branch, branch],
        in_specs=[ANY] * 6, out_specs=[VMEM_SPEC] * 3,
        scratch_shapes=[pltpu.VMEM(a.shape, a.dtype) for a in operands] + [pltpu.SemaphoreType.DMA((6,))],
        compiler_params=_params(),
    )(*operands)
    return dw_out.reshape(N_CHIPS, 2, D_MODEL // (2 * N_CHIPS), D_MODEL), dw_cb, dw_ab


def _place():
    x, y, c = lax.axis_index("x"), lax.axis_index("y"), lax.axis_index("c")
    chips = [(1 - x, y), (x, 1 - y), (1 - x, 1 - y)]
    return x, y, c, chips


def _rcopy(src, dst, send_sem, recv_sem, dev):
    return pltpu.make_async_remote_copy(src_ref=src, dst_ref=dst, send_sem=send_sem, recv_sem=recv_sem,
                                        device_id=dev, device_id_type=MESH)


class _Gather:
    N_MOVES = 6

    def __init__(self, shapes, w, o, scratch):
        self.n, self.shapes, self.w, self.o = len(w), shapes, w, o
        self.send, self.recv, self.psend, self.precv, self.loc_in, self.loc_out = scratch[:6]
        self.raw, self.stage = scratch[6:6 + self.n], scratch[6 + self.n:]
        x, y, c, self.chips = _place()
        self.c = c
        self.me, k_x, k_y, k_far = 2 * x + y, 2 * (1 - x) + y, 2 * x + (1 - y), 2 * (1 - x) + (1 - y)
        to_x, to_y = (1 - x, y, c), (x, 1 - y, c)
        self.sib = (x, y, 1 - c)
        self.sent_as = [(self.me, 0, to_x), (self.me, 1, to_y), (self.me, 1, to_x), (self.me, 0, to_y),
                        (k_x, 0, to_y), (k_y, 1, to_x)]
        self.arrives_as = [(k_x, 0, to_x), (k_y, 1, to_y), (k_x, 1, to_x), (k_y, 0, to_y),
                           (k_far, 0, to_y), (k_far, 1, to_x)]
        self.sent_on_after = {0: 4, 1: 5}

    @staticmethod
    def scratch(shards):
        n = len(shards)
        sems = pltpu.SemaphoreType.DMA
        m = _Gather.N_MOVES * n
        return ([sems((m,)), sems((m,)), sems((m,)), sems((m,)), sems((n,)), sems((n,))]
                + [pltpu.VMEM(s.shape, s.dtype) for s in shards] + [pltpu.VMEM(s.shape, MM) for s in shards])

    @staticmethod
    def out_shapes(shards):
        return [jax.ShapeDtypeStruct((N_CHIPS,) + s.shape, MM) for s in shards]

    def _rows(self, t, quarter, cc):
        rq = self.shapes[t][0] // 4
        return pl.ds((2 * cc + quarter) * rq, rq)

    def _chip(self, j):
        cx, cy = self.chips[j]
        return 2 * cx + cy, (cx, cy, self.c)

    def local_in(self, t):
        return pltpu.make_async_copy(self.w[t], self.raw[t], self.loc_in.at[t])

    def local_out(self, t):
        return pltpu.make_async_copy(self.stage[t], self.o[t].at[self.me], self.loc_out.at[t])

    def sent(self, i, t):
        k, quarter, dev = self.sent_as[i]
        rows = self._rows(t, quarter, self.c)
        there = self.o[t].at[k, rows, :]
        return _rcopy(self.stage[t].at[rows, :] if i < 4 else there, there,
                      self.send.at[i * self.n + t], self.recv.at[i * self.n + t], dev)

    def arrived(self, i, t):
        k, quarter, dev = self.arrives_as[i]
        blk = self.o[t].at[k, self._rows(t, quarter, self.c), :]
        return _rcopy(blk, blk, self.send.at[i * self.n + t], self.recv.at[i * self.n + t], dev)

    def passed(self, i, t, cc):
        k, quarter, _ = self.arrives_as[i]
        blk = self.o[t].at[k, self._rows(t, quarter, cc), :]
        return _rcopy(blk, blk, self.psend.at[i * self.n + t], self.precv.at[i * self.n + t], self.sib)

    def start(self):
        for t in range(self.n):
            self.local_in(t).start()
        for t in range(self.n):
            self.local_in(t).wait()
            self.stage[t][...] = self.raw[t][...].astype(MM)
            self.local_out(t).start()
        for i in range(4):
            for t in range(self.n):
                self.sent(i, t).start()

    def forward(self):
        for i in range(self.N_MOVES):
            for t in range(self.n):
                self.arrived(i, t).wait_recv()
                if i in self.sent_on_after:
                    self.sent(self.sent_on_after[i], t).start()
                self.passed(i, t, self.c).start()

    def finish(self):
        for i in range(self.N_MOVES):
            for t in range(self.n):
                self.passed(i, t, 1 - self.c).wait_recv()
        for i in range(self.N_MOVES):
            for t in range(self.n):
                self.sent(i, t).wait_send()
                self.passed(i, t, self.c).wait_send()
        for t in range(self.n):
            self.local_out(t).wait()


def all_gather_weights(shards, small, later):
    n, m = len(shards), len(later)
    shapes = [s.shape for s in shards]

    def body(*refs):
        w = refs[:n]
        sm = refs[n]
        lw = refs[n + 1:n + 1 + m]
        o = refs[n + 1 + m:2 * n + 1 + m]
        osm = refs[2 * n + 1 + m]
        lo = refs[2 * n + 2 + m:2 * n + 2 + 2 * m]
        scratch = refs[2 * n + 2 + 2 * m:]
        ssend, srecv, sloc, lsem_in, lsem_out = scratch[:5]
        lraw, lstage = scratch[5:5 + m], scratch[5 + m:5 + 2 * m]
        g = _Gather(shapes, w, o, scratch[5 + 2 * m:])
        own = pltpu.make_async_copy(sm, osm.at[g.me], sloc)
        own.start()
        loads = [pltpu.make_async_copy(lw[t], lraw[t], lsem_in.at[t]) for t in range(m)]
        for cp in loads:
            cp.start()
        g.start()
        small_cps = [_rcopy(sm, osm.at[g.me], ssend.at[j], srecv.at[j], g._chip(j)[1]) for j in range(3)]
        for cp in small_cps:
            cp.start()
        places = []
        for t in range(m):
            loads[t].wait()
            lstage[t][...] = lraw[t][...].astype(MM)
            places.append(pltpu.make_async_copy(lstage[t], lo[t].at[g.me], lsem_out.at[t]))
            places[t].start()
        g.forward()
        g.finish()
        for j in range(3):
            k, dev = g._chip(j)
            _rcopy(sm, osm.at[k], ssend.at[j], srecv.at[j], dev).wait_recv()
            small_cps[j].wait_send()
        own.wait()
        for cp in places:
            cp.wait()

    out_shape = _Gather.out_shapes(shards)
    out_shape.append(jax.ShapeDtypeStruct((N_CHIPS,) + small.shape, small.dtype))
    out_shape += _Gather.out_shapes(later)
    sems = pltpu.SemaphoreType.DMA
    return pl.pallas_call(
        body, name="all_gather_weights", out_shape=out_shape,
        in_specs=[ANY] * (n + 1 + m), out_specs=[ANY] * (n + 1 + m),
        scratch_shapes=[sems((3,)), sems((3,)), sems, sems((m,)), sems((m,))]
                       + [pltpu.VMEM(s.shape, s.dtype) for s in later] + [pltpu.VMEM(s.shape, MM) for s in later]
                       + _Gather.scratch(shards),
        compiler_params=_params(),
    )(*shards, small, *later)


HBM_SPEC = pl.BlockSpec(memory_space=pltpu.HBM)
SEM_SPEC = pl.BlockSpec(memory_space=pltpu.SEMAPHORE)
DATAFLOW = pltpu.SideEffectType.DATAFLOW_SIDE_EFFECTING


def split_start(name, bufs, n_copies, copies):
    nb = len(bufs)

    def body(*refs):
        for cp in copies(refs[:nb], refs[nb], refs[nb + 1]):
            cp.start()
        token = refs[2 * nb + 2]
        token[...] = jnp.zeros_like(token)

    sems = [pltpu.SemaphoreType.DMA((n_copies,))] * 2
    res = pl.pallas_call(
        body, name=name,
        out_shape=sems + [pltpu.HBM(a.shape, a.dtype) for a in bufs] + [jax.ShapeDtypeStruct((8, 128), F32)],
        in_specs=[HBM_SPEC] * nb, out_specs=[SEM_SPEC] * 2 + [HBM_SPEC] * nb + [VMEM_SPEC],
        input_output_aliases={i: 2 + i for i in range(nb)},
        compiler_params=pltpu.CompilerParams(has_side_effects=DATAFLOW),
    )(*[pltpu.with_memory_space_constraint(a, pltpu.HBM) for a in bufs])
    return res[:-1], res[-1]


def split_wait(name, state, after, copies):
    sems, bufs = state[:2], state[2:]
    nb = len(bufs)

    def body(*refs):
        for cp in copies(refs[:nb], refs[nb], refs[nb + 1]):
            cp.wait_send()
            cp.wait_recv()

    return pl.pallas_call(
        body, name=name, out_shape=[pltpu.HBM(a.shape, a.dtype) for a in bufs],
        in_specs=[HBM_SPEC] * nb + [SEM_SPEC] * 2 + [ANY] * len(after), out_specs=[HBM_SPEC] * nb,
        input_output_aliases={i: i for i in range(nb)},
        compiler_params=pltpu.CompilerParams(has_side_effects=DATAFLOW),
    )(*bufs, *sems, *after)


class _Shifted:
    def __init__(self, sems, first):
        self.sems, self.first = sems, first

    @property
    def at(self):
        return self

    def __getitem__(self, i):
        return self.sems.at[self.first + i]


def _scatter_copies(n):
    def copies(refs, send, recv):
        _, _, c, chips = _place()
        return [_rcopy(refs[t].at[2 * cx + cy], refs[n + t].at[j], send.at[3 * t + j], recv.at[3 * t + j], (cx, cy, c))
                for t in range(n) for j, (cx, cy) in enumerate(chips)]
    return copies


def scatter_start(parts, tag):
    lands = [lax.empty((3,) + p.shape[1:], p.dtype) for p in parts]
    return split_start("scatter_start_" + tag, list(parts) + lands, 3 * len(parts), _scatter_copies(len(parts)))


def scatter_wait(state, after, tag):
    n = (len(state) - 2) // 2
    return split_wait("scatter_wait_" + tag, state, after, _scatter_copies(n))[n:]


def _gather_copies(shapes, level):
    n = len(shapes)

    def copies(refs, send, recv):
        x, y, c, chips = _place()
        out = []
        for t in range(n):
            rh = shapes[t][0] // 2
            for j, (cx, cy) in enumerate(chips):
                k, dev = (2 * x + y, (cx, cy, c)) if level == 1 else (2 * cx + cy, (x, y, 1 - c))
                blk = refs[t].at[k, pl.ds(c * rh, rh), :]
                out.append(_rcopy(blk, blk, send.at[3 * t + j], recv.at[3 * t + j], dev))
        return out
    return copies


def _sibling_copies(n, other_half):
    def copies(refs, send, recv):
        x, y, c, _ = _place()
        return [_rcopy(refs[t].at[:, 1 - c] if other_half else refs[t], refs[n + t], send.at[t], recv.at[t],
                       (x, y, 1 - c)) for t in range(n)]
    return copies


def sibling_start(srcs, other_half, tag):
    lands = [lax.empty((a.shape[0],) + a.shape[2:] if other_half else a.shape, a.dtype) for a in srcs]
    return split_start("sibling_start_" + tag, list(srcs) + lands, len(srcs),
                       _sibling_copies(len(srcs), other_half))


def sibling_wait(state, after, other_half, tag):
    n = (len(state) - 2) // 2
    res = split_wait("sibling_wait_" + tag, state, after, _sibling_copies(n, other_half))
    return res[:n], res[n:]


def small_pack(ddw, v512, v1024, loss_parts):
    rows, width = PACK_ROWS, 512
    n512, n1024 = len(VEC512), len(VEC1024)

    def body(*refs):
        ddw_ref = refs[0]
        a_refs = refs[1:1 + n512]
        b_refs = refs[1 + n512:1 + n512 + n1024]
        lp_ref, o_ref, p_ref = refs[1 + n512 + n1024:]
        p_ref[...] = jnp.zeros_like(p_ref)
        p_ref[0:32, :] = ddw_ref[...]
        p_ref[LOSS_ROW:LOSS_ROW + 1, 0:128] = jnp.sum(lp_ref[...], axis=0, keepdims=True) * 0.125
        for i, r in enumerate(a_refs):
            p_ref[32 + i:33 + i, :] = r[...]
        for i, r in enumerate(b_refs):
            base = 32 + n512 + 2 * i
            p_ref[base:base + 1, :] = r[:, 0:512]
            p_ref[base + 1:base + 2, :] = r[:, 512:1024]
        x, y, c, _ = _place()
        o_ref[4 * x + 2 * y + c] = p_ref[...]

    n_in = 2 + n512 + n1024
    return pl.pallas_call(
        body, name="small_pack", out_shape=jax.ShapeDtypeStruct((8, rows, width), F32),
        in_specs=[VMEM_SPEC] * n_in, out_specs=VMEM_SPEC,
        scratch_shapes=[pltpu.VMEM((rows, width), F32)],
    )(ddw, *[v512[n] for n in VEC512], *[v1024[n] for n in VEC1024], loss_parts)


def _small_copies(refs, send, recv):
    x, y, c, _ = _place()
    mine = refs[0].at[4 * x + 2 * y + c]
    peers = [(1 - x if k & 4 else x, 1 - y if k & 2 else y, 1 - c if k & 1 else c) for k in range(1, 8)]
    return [_rcopy(mine, mine, send.at[i], recv.at[i], dev) for i, dev in enumerate(peers)]


def _row_block(r):
    for tr in (512, 352, 256, 128):
        if r % tr == 0:
            return tr
    return r


def add_halves(g, recv, name):
    _, _, r, w = g.shape
    tr = _row_block(r)

    def body(g_ref, r_ref, ob_ref, own_ref):
        k = pl.program_id(1)
        me = 2 * lax.axis_index("x") + lax.axis_index("y")
        t = g_ref[0, 0].astype(F32) + r_ref[0].astype(F32)
        ob_ref[0] = t.astype(MM)
        mine = jnp.where(k == me, t, 0.0)

        @pl.when(k == 0)
        def _():
            own_ref[...] = mine

        @pl.when(k != 0)
        def _():
            own_ref[...] += mine

    return pl.pallas_call(
        body, name=name, grid=(r // tr, N_CHIPS),
        in_specs=[pl.BlockSpec((1, 1, tr, w), lambda i, k: (k, lax.axis_index("c"), i, 0)),
                  pl.BlockSpec((1, tr, w), lambda i, k: (k, i, 0))],
        out_specs=[pl.BlockSpec((1, tr, w), lambda i, k: (k, i, 0)),
                   pl.BlockSpec((tr, w), lambda i, k: (i, 0))],
        out_shape=(jax.ShapeDtypeStruct((N_CHIPS, r, w), MM), jax.ShapeDtypeStruct((r, w), F32)),
        compiler_params=_params(("arbitrary", "arbitrary")),
    )(g, recv)


def sum_parts(own, rin, after, name):
    _, r, w = rin.shape
    tr = _row_block(r)

    def body(o_ref, r_ref, after_ref, out_ref):
        out_ref[...] = ((o_ref[...] + r_ref[0].astype(F32)) + r_ref[1].astype(F32)) + r_ref[2].astype(F32)

    return pl.pallas_call(
        body, name=name, grid=(r // tr,), out_shape=jax.ShapeDtypeStruct((r, w), F32),
        in_specs=[pl.BlockSpec((tr, w), lambda i: (i, 0)), pl.BlockSpec((3, tr, w), lambda i: (0, i, 0)),
                  _full_spec((8, 128))],
        out_specs=pl.BlockSpec((tr, w), lambda i: (i, 0)),
        compiler_params=_params(("arbitrary",)),
    )(own, rin, after)


def _adamw_math(w, g, m, v):
    mn = ADAM_B1 * m + (1.0 - ADAM_B1) * g
    vn = ADAM_B2 * v + (1.0 - ADAM_B2) * (g * g)
    m_hat = mn / (1.0 - ADAM_B1 ** ADAM_STEP)
    v_hat = vn / (1.0 - ADAM_B2 ** ADAM_STEP)
    return -ADAM_LR * (m_hat / (jnp.sqrt(v_hat) + ADAM_EPS) + ADAM_WD * w), mn, vn


def adamw(w, mine, other, m, v, name):
    r, c = w.shape
    rh = r // 2
    tr = _row_block(rh)
    if c >= 1024 and tr % 512 == 0:
        tr = 256
    nb = rh // tr

    def body(w_ref, a_ref, b_ref, m_ref, v_ref, go_ref, d_ref, mo_ref, vo_ref):
        gv = jnp.where(lax.axis_index("c") == pl.program_id(0), a_ref[...], b_ref[...])
        go_ref[...] = gv
        d_ref[...], mo_ref[...], vo_ref[...] = _adamw_math(w_ref[...], gv, m_ref[...], v_ref[...])

    def half(of_sibling):
        def index(h, i):
            owner = lax.axis_index("c")
            owner = 1 - owner if of_sibling else owner
            return jnp.where(h == owner, i, jnp.where(h < owner, 0, nb - 1)), 0
        return pl.BlockSpec((tr, c), index)

    spec = pl.BlockSpec((tr, c), lambda h, i: (h * nb + i, 0))
    out = jax.ShapeDtypeStruct((r, c), F32)
    return pl.pallas_call(
        body, name=name, grid=(2, nb), out_shape=(out, out, out, out),
        in_specs=[spec, half(False), half(True), spec, spec], out_specs=[spec] * 4,
        compiler_params=_params(("arbitrary", "arbitrary")),
    )(w, mine, other, m, v)


def adamw_small(packs, params, after):
    names = list(params)
    flat = [a for n in names for a in params[n]]

    def body(*refs):
        p_ref = refs[0]
        ins = refs[1:1 + 3 * len(names)]
        loss_ref, g_ref = refs[2 + 3 * len(names):4 + 3 * len(names)]
        outs = refs[4 + 3 * len(names):]
        total = p_ref[0]
        for d in range(1, 8):
            total = total + p_ref[d]
        g_ref[...] = total
        loss_ref[...] = g_ref[LOSS_ROW:LOSS_ROW + 1, 0:1]
        me = 2 * lax.axis_index("x") + lax.axis_index("y")
        for i, n in enumerate(names):
            w_ref, m_ref, v_ref = ins[3 * i:3 * i + 3]
            go_ref, d_ref, mo_ref, vo_ref = outs[4 * i:4 * i + 4]
            if n == "conv_dw_w":
                gv = jnp.zeros((CONV_WIDTH, 128), F32)
                for k in range(N_CHIPS):
                    gv = gv + jnp.where(me == k, g_ref[0:CONV_WIDTH, 128 * k:128 * (k + 1)], 0.0)
            elif n in VEC512:
                r0 = 32 + VEC512.index(n)
                gv = g_ref[r0:r0 + 1, :]
            else:
                r0 = 32 + len(VEC512) + 2 * VEC1024.index(n)
                gv = jnp.concatenate([g_ref[r0:r0 + 1, :], g_ref[r0 + 1:r0 + 2, :]], axis=1)
            go_ref[...] = gv
            d_ref[...], mo_ref[...], vo_ref[...] = _adamw_math(w_ref[...], gv, m_ref[...], v_ref[...])

    out_shape = [jax.ShapeDtypeStruct((1, 1), F32), jax.ShapeDtypeStruct(packs.shape[1:], F32)]
    out_shape += [jax.ShapeDtypeStruct(params[n][0].shape, F32) for n in names for _ in range(4)]
    res = pl.pallas_call(
        body, name="adamw_small", out_shape=out_shape,
        in_specs=[VMEM_SPEC] * (2 + len(flat)), out_specs=[VMEM_SPEC] * len(out_shape),
        compiler_params=_params(),
    )(packs, *flat, after)
    return res[0], res[1], {n: res[2 + 4 * i:6 + 4 * i] for i, n in enumerate(names)}


REST = ("w_ffn_up", "w_ffn_down", "w_out", "w_conv_branch", "w_att_branch")
VEC512 = ("conv_dw_b", "conv_ln_g", "conv_ln_b")
VEC1024 = ("norm_mix_pre", "b_conv_branch", "norm_mix_post", "norm_ffn_pre", "norm_ffn_post")
PACK_ROWS = 48
LOSS_ROW = 47


def kernel(x, norm_mix_pre, w_in, conv_dw_w, conv_dw_b, conv_ln_g, conv_ln_b, w_conv_branch, b_conv_branch, w_att_branch, w_out, norm_mix_post, norm_ffn_pre, w_ffn_up, w_ffn_down, norm_ffn_post, loss_target, m_norm_mix_pre, m_w_in, m_conv_dw_w, m_conv_dw_b, m_conv_ln_g, m_conv_ln_b, m_w_conv_branch, m_b_conv_branch, m_w_att_branch, m_w_out, m_norm_mix_post, m_norm_ffn_pre, m_w_ffn_up, m_w_ffn_down, m_norm_ffn_post, v_norm_mix_pre, v_w_in, v_conv_dw_w, v_conv_dw_b, v_conv_ln_g, v_conv_ln_b, v_w_conv_branch, v_b_conv_branch, v_w_att_branch, v_w_out, v_norm_mix_post, v_norm_ffn_pre, v_w_ffn_up, v_w_ffn_down, v_norm_ffn_post):
    weights = dict(norm_mix_pre=norm_mix_pre, w_in=w_in, conv_dw_w=conv_dw_w, conv_dw_b=conv_dw_b, conv_ln_g=conv_ln_g, conv_ln_b=conv_ln_b, w_conv_branch=w_conv_branch, b_conv_branch=b_conv_branch, w_att_branch=w_att_branch, w_out=w_out, norm_mix_post=norm_mix_post, norm_ffn_pre=norm_ffn_pre, w_ffn_up=w_ffn_up, w_ffn_down=w_ffn_down, norm_ffn_post=norm_ffn_post)
    mom = dict(norm_mix_pre=m_norm_mix_pre, w_in=m_w_in, conv_dw_w=m_conv_dw_w, conv_dw_b=m_conv_dw_b, conv_ln_g=m_conv_ln_g, conv_ln_b=m_conv_ln_b, w_conv_branch=m_w_conv_branch, b_conv_branch=m_b_conv_branch, w_att_branch=m_w_att_branch, w_out=m_w_out, norm_mix_post=m_norm_mix_post, norm_ffn_pre=m_norm_ffn_pre, w_ffn_up=m_w_ffn_up, w_ffn_down=m_w_ffn_down, norm_ffn_post=m_norm_ffn_post)
    var = dict(norm_mix_pre=v_norm_mix_pre, w_in=v_w_in, conv_dw_w=v_conv_dw_w, conv_dw_b=v_conv_dw_b, conv_ln_g=v_conv_ln_g, conv_ln_b=v_conv_ln_b, w_conv_branch=v_w_conv_branch, b_conv_branch=v_b_conv_branch, w_att_branch=v_w_att_branch, w_out=v_w_out, norm_mix_post=v_norm_mix_post, norm_ffn_pre=v_norm_ffn_pre, w_ffn_up=v_w_ffn_up, w_ffn_down=v_w_ffn_down, norm_ffn_post=v_norm_ffn_post)
    order = list(weights)
    grads, deltas, new_m, new_v = {}, {}, {}, {}
    xs = x.reshape(SEQ, D_MODEL)
    tgt = loss_target.reshape(SEQ, D_MODEL)
    row = lambda a: a.reshape(1, -1)
    g1, g2, g3, g4 = (row(weights[n]) for n in ("norm_mix_pre", "norm_mix_post", "norm_ffn_pre", "norm_ffn_post"))
    ln_g, ln_b = row(conv_ln_g), row(conv_ln_b)

    summed, from_chips = {}, {}

    def core_sums(names, state, after, tag):
        own, from_sibling = sibling_wait(state, after, True, tag)
        for n, g, r in zip(names, own, from_sibling):
            summed[n] = add_halves(g, r, "add_" + n)

    def chip_sums(names, after):
        return [sum_parts(summed[n][1], from_chips[n], after, "sum_" + n) for n in names]

    def optimize(names, state, after, tag):
        mine, other = sibling_wait(state, after, False, tag)
        for n, a, b in zip(names, mine, other):
            grads[n], deltas[n], new_m[n], new_v[n] = adamw(weights[n], a, b, mom[n], var[n], "adamw_" + n)

    w_in_g, dw_g, *rest = all_gather_weights([w_in], conv_dw_w, [weights[n] for n in REST])
    w_dw_full = jnp.concatenate([dw_g[k] for k in range(N_CHIPS)], axis=1)
    rest_shapes = [weights[n].shape for n in REST]
    state, token = split_start("gather_start", rest, 3 * len(REST), _gather_copies(rest_shapes, 1))
    h1, ci, q, k, v, gc, ga = in_proj_fwd(xs, g1, w_in_g, token)
    u1, u3 = conv_fwd(ci, w_dw_full, row(conv_dw_b), ln_g, ln_b)
    att, rc = attn_fwd(q, k, v)
    rest = split_wait("gather_wait", state, [att], _gather_copies(rest_shapes, 1))
    pass_copies = _gather_copies(rest_shapes[2:] + rest_shapes[:2], 2)
    n_first = 3 * len(REST[2:])
    state, token = split_start("pass_start", rest[2:] + rest[:2], 3 * len(REST), pass_copies)
    passed = split_wait("pass_mix_wait", state, [], lambda *a: pass_copies(*a)[:n_first])
    w_out_g, w_cb_g, w_ab_g = passed[:3]
    w_out_g = w_out_g.reshape(D_MODEL, D_MODEL)
    co, ao, merged, mix, x2, h2 = mix_fwd(u3, att, gc, ga, xs, w_cb_g, row(b_conv_branch), w_ab_g, w_out_g,
                                          g2, g3, token)
    w_up_g, w_down_g = split_wait(
        "pass_ffn_wait", list(state[:2]) + list(passed[3:]), [h2],
        lambda refs, send, recv: _gather_copies(rest_shapes[:2], 2)(refs, _Shifted(send, n_first), _Shifted(recv, n_first)))
    w_down_g = w_down_g.reshape(D_FF, D_MODEL)
    gate, up, act = ffn_up_fwd(h2, w_up_g)
    dff, dy, loss_parts, dg4 = ffn_down_loss(act, w_down_g, x2, tgt, g4)

    dgu = ffn_act_bwd(dff, w_down_g, gate, up)
    dx2, dmix, dg3, dg2 = ffn_in_bwd(dgu, w_up_g, x2, mix, dy, g3, g2)
    ffn_grads = [weight_grad(h2, dgu, "dw_ffn_up", True), weight_grad(act, dff, "dw_ffn_down", False, tk=UP_SHARD)]
    to_ffn, token = sibling_start(ffn_grads, True, "dw_ffn")
    dco, dao, dg, du3, datt, dbcb = merge_bwd(dmix, w_out_g, gc, ga, co, ao, w_cb_g, w_ab_g, token)
    to_mix, token = sibling_start(weight_grad_mix(merged, dmix, u3, dco, att, dao), True, "dw_mix")
    core_sums(REST[:2], to_ffn, [token], "dw_ffn")
    core_sums(REST[2:], to_mix, [summed["w_ffn_down"][1]], "dw_mix")
    state, token = scatter_start([summed[n][0] for n in REST], "rest")
    dci, ddw, dbdw, dlng, dlnb = conv_bwd(du3, u1, ci, w_dw_full, ln_g, ln_b, token)
    dqkv = attn_bwd(q, k, v, datt, rc, token)
    from_chips.update(zip(REST, scatter_wait(state, [dci, dqkv], "rest")))
    dproj = (dci, dqkv, dg)
    to_in, token = sibling_start([weight_grad_in(h1, dproj)], True, "dw_in")
    grad_x, dg1 = in_proj_bwd(dproj, w_in_g, xs, dx2, g1, token)
    v512 = dict(conv_dw_b=dbdw, conv_ln_g=dlng, conv_ln_b=dlnb)
    v1024 = dict(norm_mix_pre=dg1, b_conv_branch=dbcb, norm_mix_post=dg2, norm_ffn_pre=dg3, norm_ffn_post=dg4)
    packs = small_pack(ddw, v512, v1024, loss_parts)
    core_sums(("w_in",), to_in, [packs], "dw_in")
    to_chips = summed["w_in"][0]
    landing = lax.empty((3,) + to_chips.shape[1:], to_chips.dtype)

    def scatter_and_packs(refs, send, recv):
        return (_scatter_copies(1)(refs[:2], send, recv)
                + _small_copies(refs[2:], _Shifted(send, 3), _Shifted(recv, 3)))

    state, token = split_start("scatter_start_w_in", [to_chips, landing, packs], 3 + 7, scatter_and_packs)
    swap_up, token = sibling_start(chip_sums(REST[:1], token), False, "sum_ffn_up")
    swap_rest, token = sibling_start(chip_sums(REST[1:], token), False, "sum_rest")
    optimize(REST[:1], swap_up, [token], "sum_ffn_up")
    optimize(REST[1:], swap_rest, [new_v["w_ffn_up"]], "sum_rest")
    _, from_chips["w_in"], packs = split_wait("scatter_wait_w_in", state, [new_v[n] for n in REST], scatter_and_packs)
    swap_in, token = sibling_start(chip_sums(("w_in",), token), False, "sum_w_in")
    as_rows = lambda n, a: a if n == "conv_dw_w" else a.reshape(1, -1)
    small_names = ("conv_dw_w",) + VEC512 + VEC1024
    loss, gsum, small = adamw_small(
        packs, {n: tuple(as_rows(n, d[n]) for d in (weights, mom, var)) for n in small_names}, token)
    optimize(("w_in",), swap_in, [gsum], "sum_w_in")
    for n in small_names:
        grads[n], deltas[n], new_m[n], new_v[n] = (a.reshape(weights[n].shape) for a in small[n])

    return (loss.reshape(()), grad_x.reshape(1, SEQ, D_MODEL),*[grads[n] for n in order], *[deltas[n] for n in order],
            *[new_m[n] for n in order], *[new_v[n] for n in order])
```

```python
import jax
import jax.numpy as jnp
from jax import lax
from jax.experimental import pallas as pl
from jax.experimental.pallas import tpu as pltpu

F32 = jnp.float32
MM = jnp.bfloat16

SEQ = 2048
D_MODEL = 1024
CONV_DIM = 512
ATT_DIM = 512
CONV_WIDTH = 31
D_FF = 2816
IN_COLS = 2 * CONV_DIM + 3 * ATT_DIM + 2 * D_MODEL
N_CHIPS = 4
IN_SHARD = IN_COLS // N_CHIPS
UP_SHARD = 2 * D_FF // N_CHIPS
BR_SHARD = D_MODEL // N_CHIPS
EPS = 1e-6
ATT_SCALE = 0.125

TM = 256
GLU_ROWS = 256
TQ = 128
CONV_TILE = 64
CONV_WIN = CONV_TILE + 32
VMEM_LIMIT = 56 * 1024 * 1024

ADAM_LR = 0.001
ADAM_B1 = 0.9
ADAM_B2 = 0.999
ADAM_EPS = 1e-08
ADAM_WD = 0.01
ADAM_STEP = 10

MESH = pl.DeviceIdType.MESH
ANY = pl.BlockSpec(memory_space=pl.ANY)
VMEM_SPEC = pl.BlockSpec(memory_space=pltpu.VMEM)

NT_DIMS = (((1,), (1,)), ((), ()))
TN_DIMS = (((0,), (0,)), ((), ()))

IN_PIECES = (("ci", 0, 1024), ("q", 1024, 1536), ("k", 1536, 2048), ("v", 2048, 2560),
             ("gc", 2560, 3584), ("ga", 3584, 4608))


def _params(sem=None, vmem=VMEM_LIMIT):
    return pltpu.CompilerParams(dimension_semantics=sem, vmem_limit_bytes=vmem)


def _dot(a, b):
    return jnp.dot(a, b, preferred_element_type=F32)


def _dot_nt(a, b):
    return lax.dot_general(a, b, NT_DIMS, preferred_element_type=F32)


def _dot_tn(a, b):
    return lax.dot_general(a, b, TN_DIMS, preferred_element_type=F32)


def _sigmoid(x):
    return 1.0 / (1.0 + jnp.exp(-x))


def _rms(x):
    r = lax.rsqrt(jnp.mean(x * x, axis=-1, keepdims=True) + EPS)
    return x * r, r


def _rms_bwd(dy_g, n, r):
    return r * (dy_g - n * jnp.mean(dy_g * n, axis=-1, keepdims=True))


def _row_tile_spec(width, tm=TM):
    return pl.BlockSpec((tm, width), lambda i: (i, 0))


def _full_spec(shape):
    nd = len(shape)
    return pl.BlockSpec(shape, lambda *_: (0,) * nd)


def _weight_spec(shape):
    nd = len(shape)
    return pl.BlockSpec(shape, lambda *_: (0,) * nd, pipeline_mode=pl.Buffered(1))


def _load_once(src, dst, sems):
    first = pl.program_id(0) == 0
    copies = [pltpu.make_async_copy(src.at[p], dst.at[p], sems.at[p]) for p in range(src.shape[0])]

    @pl.when(first)
    def _():
        for cp in copies:
            cp.start()

    return lambda p: pl.when(first)(copies[p].wait)


def _load_once_scratch(w):
    return [pltpu.VMEM(w.shape, w.dtype), pltpu.SemaphoreType.DMA((w.shape[0],))]


def _acc_rows(ref, val, first):
    @pl.when(first)
    def _():
        ref[...] = val

    @pl.when(jnp.logical_not(first))
    def _():
        ref[...] += val


TOKEN_SPEC = pl.BlockSpec((8, 128), lambda *_: (0, 0))


def in_proj_fwd(x, g1, w_in_g, after):
    def body(x_ref, g_ref, w_hbm, after_ref, h_ref, ci_ref, q_ref, k_ref, v_ref, gc_ref, ga_ref, w_ref, sems):
        wait_shard = _load_once(w_hbm, w_ref, sems)
        n, _ = _rms(x_ref[...])
        h = (n * g_ref[...]).astype(MM)
        h_ref[...] = h
        outs = dict(ci=ci_ref, q=q_ref, k=k_ref, v=v_ref, gc=gc_ref, ga=ga_ref)
        for j in range(N_CHIPS):
            wait_shard(j)
            p = _dot(h, w_ref[j])
            g0 = j * IN_SHARD
            for name, s, e in IN_PIECES:
                lo, hi = max(s, g0), min(e, g0 + IN_SHARD)
                if lo < hi:
                    ref = outs[name]
                    part = p[:, lo - g0:hi - g0]
                    if name == "q":
                        part = part * ATT_SCALE
                    ref[:, lo - s:hi - s] = part.astype(ref.dtype)

    out_shape = [
        jax.ShapeDtypeStruct((SEQ, D_MODEL), MM),
        jax.ShapeDtypeStruct((SEQ, 2 * CONV_DIM), F32),
        jax.ShapeDtypeStruct((SEQ, ATT_DIM), MM),
        jax.ShapeDtypeStruct((SEQ, ATT_DIM), MM),
        jax.ShapeDtypeStruct((SEQ, ATT_DIM), MM),
        jax.ShapeDtypeStruct((SEQ, D_MODEL), F32),
        jax.ShapeDtypeStruct((SEQ, D_MODEL), F32),
    ]
    return pl.pallas_call(
        body, name="in_proj_fwd", grid=(SEQ // TM,), out_shape=out_shape,
        in_specs=[_row_tile_spec(D_MODEL), _full_spec((1, D_MODEL)), ANY, TOKEN_SPEC],
        out_specs=[_row_tile_spec(s.shape[1]) for s in out_shape],
        scratch_shapes=_load_once_scratch(w_in_g),
        compiler_params=_params(("arbitrary",)),
    )(x, g1, w_in_g, after)


def _shifted_sum(win, terms):
    by_rot = {}
    for m, coef in terms:
        by_rot.setdefault(m % 8, []).append((m // 8, coef))
    acc = None
    n = win.shape[0]
    for rot in sorted(by_rot):
        shifted = win if rot == 0 else pltpu.roll(win, n - rot, 0)
        for a, coef in by_rot[rot]:
            t = coef * shifted[8 * a:8 * a + CONV_TILE, :]
            acc = t if acc is None else acc + t
    return acc


def _glu_into(ci_ref, upad_ref):
    upad_ref[0:32, :] = jnp.zeros((32, CONV_DIM), F32)

    def step(i, c):
        t0 = pl.multiple_of(i * GLU_ROWS, GLU_ROWS)
        a = ci_ref[pl.ds(t0, GLU_ROWS), 0:CONV_DIM]
        b = ci_ref[pl.ds(t0, GLU_ROWS), CONV_DIM:2 * CONV_DIM]
        upad_ref[pl.ds(t0 + 32, GLU_ROWS), :] = a * _sigmoid(b)
        return c

    lax.fori_loop(0, SEQ // GLU_ROWS, step, 0)


def _layernorm_parts(u1):
    mu = jnp.mean(u1, axis=-1, keepdims=True)
    xc = u1 - mu
    rstd = lax.rsqrt(jnp.mean(xc * xc, axis=-1, keepdims=True) + EPS)
    return xc * rstd, rstd


def conv_fwd(ci, w_dw, b_dw, ln_g, ln_b):
    def body(ci_ref, w_ref, b_ref, g_ref, bb_ref, u1_ref, u3_ref, upad_ref):
        _glu_into(ci_ref, upad_ref)

        def step(i, c):
            t0 = pl.multiple_of(i * CONV_TILE, CONV_TILE)
            win = upad_ref[pl.ds(t0, CONV_WIN), :]
            u1 = _shifted_sum(win, [(j + 2, w_ref[j:j + 1, :]) for j in range(CONV_WIDTH)]) + b_ref[...]
            u1_ref[pl.ds(t0, CONV_TILE), :] = u1
            xh, _ = _layernorm_parts(u1)
            u2 = xh * g_ref[...] + bb_ref[...]
            u3_ref[pl.ds(t0, CONV_TILE), :] = (u2 * _sigmoid(u2)).astype(MM)
            return c

        lax.fori_loop(0, SEQ // CONV_TILE, step, 0)

    return pl.pallas_call(
        body, name="conv_fwd",
        out_shape=[jax.ShapeDtypeStruct((SEQ, CONV_DIM), F32), jax.ShapeDtypeStruct((SEQ, CONV_DIM), MM)],
        in_specs=[VMEM_SPEC] * 5, out_specs=[VMEM_SPEC] * 2,
        scratch_shapes=[pltpu.VMEM((SEQ + 32, CONV_DIM), F32)],
        compiler_params=_params(),
    )(ci, w_dw, b_dw, ln_g, ln_b)


def _softplus(z):
    return jnp.maximum(z, 0.0) + jnp.log(1.0 + jnp.exp(-jnp.abs(z)))


def _cumsum_weights(suffix, with_total):
    n = 256 if with_total else 128
    r = lax.broadcasted_iota(jnp.int32, (128, n), 0)
    c = lax.broadcasted_iota(jnp.int32, (128, n), 1)
    tri = (r >= c) if suffix else (r <= c)
    return jnp.logical_or(tri, c >= 128).astype(MM)


NO_SCORE = -1e30
N_KB = SEQ // TQ


def _score_bias(lane, row, i, j):
    keep = jnp.logical_and(i >= 0, jnp.logical_or(j < i, lane < row))
    return jnp.where(keep, 0.0, NO_SCORE)


def _block_pipeline(n_stages, descending, step, on_query_block=None):
    n_lag = n_stages - 1
    none = jnp.int32(-1)

    def shift(cur, lag):
        step([cur] + [(lag[2 * s], lag[2 * s + 1]) for s in range(n_lag)])
        return (cur[0], cur[1]) + tuple(lag[:-2])

    def outer(i, lag):
        if on_query_block is not None:
            on_query_block(i)

        def inner(n, lag):
            return shift((i, i - n if descending else n), lag)
        return lax.fori_loop(0, i + 1, inner, lag)

    lag = lax.fori_loop(0, N_KB, outer, (none,) * (2 * n_lag))
    lax.fori_loop(0, n_lag, lambda n, lag: shift((none, none), lag), lag)


def _head_masks():
    lane = lax.broadcasted_iota(jnp.int32, (TQ, 128), 1)
    row = lax.broadcasted_iota(jnp.int32, (TQ, 128), 0)
    return lane, row, lane < 64


def _pick_head(x, head0, h):
    zero = jnp.zeros_like(x)
    return jnp.where(head0, x, zero) if h == 0 else jnp.where(head0, zero, x)


N_PAIRS = ATT_DIM // 128


def _split_heads(src_ref, dst_ref):
    _, _, head0 = _head_masks()

    def block(b, c):
        r0 = pl.multiple_of(b * TQ, TQ)
        d0 = pl.multiple_of(b * 2 * TQ, 2 * TQ)
        for p in range(N_PAIRS):
            x = src_ref[pl.ds(r0, TQ), 128 * p:128 * (p + 1)]
            for h in range(2):
                dst_ref[p, pl.ds(d0 + TQ * h, TQ), :] = _pick_head(x, head0, h)
        return c

    lax.fori_loop(0, N_KB, block, 0)


def attn_fwd(q, k, v):
    def body(q_ref, k_ref, v_ref, o_ref, rc_ref, acc_ref, r_ref, z_ref, spb_ref, ab_ref, qm_ref, vm_ref):
        lane, row, _ = _head_masks()
        w = _cumsum_weights(suffix=True, with_total=True)
        _split_heads(q_ref, qm_ref)
        _split_heads(v_ref, vm_ref)
        acc_ref[...] = jnp.zeros_like(acc_ref)
        r_ref[...] = jnp.zeros_like(r_ref)
        rc_ref[...] = jnp.zeros_like(rc_ref)
        z_ref[...] = jnp.full(z_ref.shape, NO_SCORE, F32)
        spb_ref[...] = jnp.zeros_like(spb_ref)
        ab_ref[...] = jnp.zeros_like(ab_ref)

        def step(pairs):
            (i1, j1), (i2, j2), (i3, j3) = pairs
            k1, q2, q3 = (pl.multiple_of(jnp.maximum(b, 0) * TQ, TQ) for b in (j1, i2, i3))
            q1, k3 = (pl.multiple_of(jnp.maximum(b, 0) * 2 * TQ, 2 * TQ) for b in (i1, j3))
            bias1 = _score_bias(lane, row, i1, j1)
            first2 = j2 == i2
            rc_rows = rc_ref[pl.ds(q2, TQ), :]
            for p in range(N_PAIRS):
                cols = slice(128 * p, 128 * (p + 1))
                kb = k_ref[pl.ds(k1, TQ), cols]
                acc_ref[pl.ds(q3, TQ), cols] += _dot(ab_ref[p], vm_ref[p, pl.ds(k3, 2 * TQ), :])
                for h in range(2):
                    hh = 2 * p + h
                    r = _dot(spb_ref[hh], w)
                    r_in = jnp.where(first2, 0.0, r_ref[hh])
                    ab_ref[p, :, 128 * h:128 * (h + 1)] = jnp.exp(z_ref[hh] - (r[:, :128] + r_in)).astype(MM)
                    rc_rows = jnp.where(jnp.logical_and(lane == 16 * hh + j2, i2 >= 0), r_in, rc_rows)
                    r_ref[hh] = r_in + r[:, 128:]
                    z = _dot_nt(qm_ref[p, pl.ds(q1 + TQ * h, TQ), :], kb) + bias1
                    z_ref[hh] = z
                    spb_ref[hh] = _softplus(z).astype(MM)
            rc_ref[pl.ds(q2, TQ), :] = rc_rows

        _block_pipeline(3, True, step)
        o_ref[...] = acc_ref[...].astype(MM)

    return pl.pallas_call(
        body, name="attn_fwd",
        out_shape=[jax.ShapeDtypeStruct((SEQ, ATT_DIM), MM), jax.ShapeDtypeStruct((SEQ, 128), F32)],
        in_specs=[VMEM_SPEC] * 3, out_specs=[VMEM_SPEC] * 2,
        scratch_shapes=[pltpu.VMEM((SEQ, ATT_DIM), F32), pltpu.VMEM((8, TQ, 128), F32),
                        pltpu.VMEM((8, TQ, 128), F32), pltpu.VMEM((8, TQ, 128), MM),
                        pltpu.VMEM((N_PAIRS, TQ, 256), MM), pltpu.VMEM((N_PAIRS, 2 * SEQ, 128), MM),
                        pltpu.VMEM((N_PAIRS, 2 * SEQ, 128), MM)],
        compiler_params=_params(),
    )(q, k, v)


def mix_fwd(u3, att, gc, ga, x, w_cb_g, b_cb, w_ab_g, w_out_g, g2, g3, after):
    def body(u_ref, a_ref, gc_ref, ga_ref, x_ref, wcb_ref, bcb_ref, wab_ref, wout_ref, g2_ref, g3_ref, after_ref,
             co_ref, ao_ref, mg_ref, mix_ref, x2_ref, h2_ref):
        u = u_ref[...]
        a = a_ref[...]
        co = jnp.concatenate([_dot(u, wcb_ref[j]) for j in range(N_CHIPS)], axis=1) + bcb_ref[...]
        ao = jnp.concatenate([_dot(a, wab_ref[j]) for j in range(N_CHIPS)], axis=1)
        co_ref[...] = co.astype(MM)
        ao_ref[...] = ao.astype(MM)
        merged = (_sigmoid(gc_ref[...]) * co + _sigmoid(ga_ref[...]) * ao).astype(MM)
        mg_ref[...] = merged
        mix = _dot(merged, wout_ref[...])
        mix_ref[...] = mix
        n2, _ = _rms(mix)
        x2 = x_ref[...] + n2 * g2_ref[...]
        x2_ref[...] = x2
        n3, _ = _rms(x2)
        h2_ref[...] = (n3 * g3_ref[...]).astype(MM)

    out_shape = [
        jax.ShapeDtypeStruct((SEQ, D_MODEL), MM), jax.ShapeDtypeStruct((SEQ, D_MODEL), MM),
        jax.ShapeDtypeStruct((SEQ, D_MODEL), MM), jax.ShapeDtypeStruct((SEQ, D_MODEL), F32),
        jax.ShapeDtypeStruct((SEQ, D_MODEL), F32), jax.ShapeDtypeStruct((SEQ, D_MODEL), MM),
    ]
    vec = _full_spec((1, D_MODEL))
    return pl.pallas_call(
        body, name="mix_fwd", grid=(SEQ // TM,), out_shape=out_shape,
        in_specs=[_row_tile_spec(CONV_DIM), _row_tile_spec(ATT_DIM), _row_tile_spec(D_MODEL),
                  _row_tile_spec(D_MODEL), _row_tile_spec(D_MODEL), _weight_spec(w_cb_g.shape), vec,
                  _weight_spec(w_ab_g.shape), _weight_spec(w_out_g.shape), vec, vec, TOKEN_SPEC],
        out_specs=[_row_tile_spec(D_MODEL)] * 6,
        compiler_params=_params(("arbitrary",)),
    )(u3, att, gc, ga, x, w_cb_g, b_cb, w_ab_g, w_out_g, g2, g3, after)


def ffn_up_fwd(h2, w_up_g):
    def body(h_ref, wg_ref, wu_ref, gate_ref, up_ref, act_ref):
        h = h_ref[...]
        gate = _dot(h, wg_ref[0])
        up = _dot(h, wu_ref[0])
        gate_ref[...] = gate.astype(MM)
        up_ref[...] = up.astype(MM)
        act_ref[...] = (gate * _sigmoid(gate) * up).astype(MM)

    tile = pl.BlockSpec((TM, UP_SHARD), lambda n, i: (i, n))
    act = jax.ShapeDtypeStruct((SEQ, D_FF), MM)
    return pl.pallas_call(
        body, name="ffn_up_fwd", grid=(2, SEQ // TM), out_shape=[act, act, act],
        in_specs=[pl.BlockSpec((TM, D_MODEL), lambda n, i: (i, 0)),
                  pl.BlockSpec((1, D_MODEL, UP_SHARD), lambda n, i: (n, 0, 0)),
                  pl.BlockSpec((1, D_MODEL, UP_SHARD), lambda n, i: (n + 2, 0, 0))],
        out_specs=[tile, tile, tile],
        compiler_params=_params(("arbitrary", "arbitrary")),
    )(h2, w_up_g, w_up_g)


def ffn_down_loss(act, w_down_g, x2, target, g4):
    def body(act_ref, wd_hbm, x2_ref, t_ref, g_ref, dff_ref, dy_ref, loss_ref, dg_ref, wd_ref, sems):
        wait_half = _load_once(wd_hbm, wd_ref, sems)
        ff = None
        for h in range(2):
            wait_half(h)
            t = _dot(act_ref[:, h * UP_SHARD:(h + 1) * UP_SHARD], wd_ref[h])
            ff = t if ff is None else ff + t
        n4, r4 = _rms(ff)
        g4v = g_ref[...]
        err = x2_ref[...] + n4 * g4v - t_ref[...]
        row_loss = jnp.mean(err * err, axis=-1, keepdims=True)
        loss_ref[...] = jnp.zeros((8, 128), F32) + 0.5 * jnp.sum(row_loss, axis=0, keepdims=True)
        dy = err * (1.0 / D_MODEL)
        dy_ref[...] = dy
        dff_ref[...] = _rms_bwd(dy * g4v, n4, r4).astype(MM)
        _acc_rows(dg_ref, jnp.sum(dy * n4, axis=0, keepdims=True), pl.program_id(0) == 0)

    nt = SEQ // TM
    vec = _full_spec((1, D_MODEL))
    return pl.pallas_call(
        body, name="ffn_down_loss", grid=(nt,),
        out_shape=(jax.ShapeDtypeStruct((SEQ, D_MODEL), MM), jax.ShapeDtypeStruct((SEQ, D_MODEL), F32),
                   jax.ShapeDtypeStruct((nt * 8, 128), F32), jax.ShapeDtypeStruct((1, D_MODEL), F32)),
        in_specs=[_row_tile_spec(D_FF), ANY, _row_tile_spec(D_MODEL), _row_tile_spec(D_MODEL), vec],
        out_specs=[_row_tile_spec(D_MODEL), _row_tile_spec(D_MODEL),
                   pl.BlockSpec((8, 128), lambda i: (i, 0)), vec],
        scratch_shapes=_load_once_scratch(w_down_g),
        compiler_params=_params(("arbitrary",)),
    )(act, w_down_g, x2, target, g4)


def ffn_act_bwd(dff, w_down_g, gate, up):
    def body(dff_ref, wd_hbm, gate_ref, up_ref, dgu_ref, wd_ref, sems):
        wait_half = _load_once(wd_hbm, wd_ref, sems)
        for h in range(2):
            wait_half(h)
            cols = slice(h * UP_SHARD, (h + 1) * UP_SHARD)
            dact = _dot_nt(dff_ref[...], wd_ref[h])
            gate = gate_ref[:, cols].astype(F32)
            sg = _sigmoid(gate)
            dgu_ref[:, cols] = (dact * up_ref[:, cols].astype(F32) * (sg * (1.0 + gate * (1.0 - sg)))).astype(MM)
            dgu_ref[:, D_FF + h * UP_SHARD:D_FF + (h + 1) * UP_SHARD] = (dact * (gate * sg)).astype(MM)

    return pl.pallas_call(
        body, name="ffn_act_bwd", grid=(SEQ // TM,),
        out_shape=jax.ShapeDtypeStruct((SEQ, 2 * D_FF), MM),
        in_specs=[_row_tile_spec(D_MODEL), ANY, _row_tile_spec(D_FF), _row_tile_spec(D_FF)],
        out_specs=_row_tile_spec(2 * D_FF),
        scratch_shapes=_load_once_scratch(w_down_g),
        compiler_params=_params(("arbitrary",)),
    )(dff, w_down_g, gate, up)


def ffn_in_bwd(dgu, w_up_g, x2, mix, dy, g3, g2):
    def body(dgu_ref, w_hbm, x2_ref, mix_ref, dy_ref, g3_ref, g2_ref, dx2_ref, dmix_ref, dg3_ref, dg2_ref,
             w_ref, sems):
        wait_shard = _load_once(w_hbm, w_ref, sems)
        dh2 = None
        for j in range(N_CHIPS):
            wait_shard(j)
            t = _dot_nt(dgu_ref[:, j * UP_SHARD:(j + 1) * UP_SHARD], w_ref[j])
            dh2 = t if dh2 is None else dh2 + t
        first = pl.program_id(0) == 0
        n3, r3 = _rms(x2_ref[...])
        dx2 = dy_ref[...] + _rms_bwd(dh2 * g3_ref[...], n3, r3)
        dx2_ref[...] = dx2
        _acc_rows(dg3_ref, jnp.sum(dh2 * n3, axis=0, keepdims=True), first)
        n2, r2 = _rms(mix_ref[...])
        dmix_ref[...] = _rms_bwd(dx2 * g2_ref[...], n2, r2).astype(MM)
        _acc_rows(dg2_ref, jnp.sum(dx2 * n2, axis=0, keepdims=True), first)

    vec = _full_spec((1, D_MODEL))
    return pl.pallas_call(
        body, name="ffn_in_bwd", grid=(SEQ // TM,),
        out_shape=(jax.ShapeDtypeStruct((SEQ, D_MODEL), F32), jax.ShapeDtypeStruct((SEQ, D_MODEL), MM),
                   jax.ShapeDtypeStruct((1, D_MODEL), F32), jax.ShapeDtypeStruct((1, D_MODEL), F32)),
        in_specs=[_row_tile_spec(2 * D_FF), ANY, _row_tile_spec(D_MODEL),
                  _row_tile_spec(D_MODEL), _row_tile_spec(D_MODEL), vec, vec],
        out_specs=[_row_tile_spec(D_MODEL), _row_tile_spec(D_MODEL), vec, vec],
        scratch_shapes=_load_once_scratch(w_up_g),
        compiler_params=_params(("arbitrary",)),
    )(dgu, w_up_g, x2, mix, dy, g3, g2)


def merge_bwd(dmix, w_out_g, gc, ga, co, ao, w_cb_g, w_ab_g, after):
    def body(dmix_ref, wout_ref, gc_ref, ga_ref, co_ref, ao_ref, wcb_ref, wab_ref, after_ref,
             dco_ref, dao_ref, dg_ref, du3_ref, datt_ref, dbcb_ref):
        dm = _dot_nt(dmix_ref[...], wout_ref[...])
        sgc = _sigmoid(gc_ref[...])
        sga = _sigmoid(ga_ref[...])
        dco = dm * sgc
        dao = dm * sga
        dg_ref[:, 0:D_MODEL] = (dm * co_ref[...].astype(F32) * (sgc * (1.0 - sgc))).astype(MM)
        dg_ref[:, D_MODEL:2 * D_MODEL] = (dm * ao_ref[...].astype(F32) * (sga * (1.0 - sga))).astype(MM)
        _acc_rows(dbcb_ref, jnp.sum(dco, axis=0, keepdims=True), pl.program_id(0) == 0)
        dco_ref[...] = dco.astype(MM)
        dao_ref[...] = dao.astype(MM)
        du3 = None
        datt = None
        for j in range(N_CHIPS):
            cols = slice(j * BR_SHARD, (j + 1) * BR_SHARD)
            t = _dot_nt(dco_ref[:, cols], wcb_ref[j])
            s = _dot_nt(dao_ref[:, cols], wab_ref[j])
            du3 = t if du3 is None else du3 + t
            datt = s if datt is None else datt + s
        du3_ref[...] = du3
        datt_ref[...] = datt.astype(MM)

    wide = _row_tile_spec(D_MODEL)
    return pl.pallas_call(
        body, name="merge_bwd", grid=(SEQ // TM,),
        out_shape=(jax.ShapeDtypeStruct((SEQ, D_MODEL), MM), jax.ShapeDtypeStruct((SEQ, D_MODEL), MM),
                   jax.ShapeDtypeStruct((SEQ, 2 * D_MODEL), MM),
                   jax.ShapeDtypeStruct((SEQ, CONV_DIM), F32), jax.ShapeDtypeStruct((SEQ, ATT_DIM), MM),
                   jax.ShapeDtypeStruct((1, D_MODEL), F32)),
        in_specs=[wide, _weight_spec(w_out_g.shape), wide, wide, wide, wide,
                  _weight_spec(w_cb_g.shape), _weight_spec(w_ab_g.shape), TOKEN_SPEC],
        out_specs=[wide, wide, _row_tile_spec(2 * D_MODEL), _row_tile_spec(CONV_DIM), _row_tile_spec(ATT_DIM),
                   _full_spec((1, D_MODEL))],
        compiler_params=_params(("arbitrary",)),
    )(dmix, w_out_g, gc, ga, co, ao, w_cb_g, w_ab_g, after)


def conv_bwd(du3, u1, ci, w_dw, ln_g, ln_b, after):
    def body(du3_ref, u1_ref, ci_ref, w_ref, g_ref, bb_ref, after_ref,
             dci_ref, dw_ref, dbdw_ref, dg_ref, db_ref, upad_ref, dpad_ref, dwacc_ref, vacc_ref):
        _glu_into(ci_ref, upad_ref)
        dpad_ref[SEQ:SEQ + 32, :] = jnp.zeros((32, CONV_DIM), F32)
        dwacc_ref[...] = jnp.zeros_like(dwacc_ref)
        vacc_ref[...] = jnp.zeros_like(vacc_ref)

        def fold8(t):
            s = t[0:8, :]
            for r in range(1, CONV_TILE // 8):
                s = s + t[8 * r:8 * r + 8, :]
            return s

        def pass1(i, c):
            t0 = pl.multiple_of(i * CONV_TILE, CONV_TILE)
            xh, rstd = _layernorm_parts(u1_ref[pl.ds(t0, CONV_TILE), :])
            gv = g_ref[...]
            u2 = xh * gv + bb_ref[...]
            s2 = _sigmoid(u2)
            du2 = du3_ref[pl.ds(t0, CONV_TILE), :] * (s2 * (1.0 + u2 * (1.0 - s2)))
            wv = du2 * gv
            du1 = rstd * (wv - jnp.mean(wv, axis=-1, keepdims=True)
                          - xh * jnp.mean(wv * xh, axis=-1, keepdims=True))
            dpad_ref[pl.ds(t0, CONV_TILE), :] = du1
            vacc_ref[0] += fold8(du2 * xh)
            vacc_ref[1] += fold8(du2)
            vacc_ref[2] += fold8(du1)
            win = upad_ref[pl.ds(t0, CONV_WIN), :]
            n = win.shape[0]
            for rot in range(8):
                shifted = win if rot == 0 else pltpu.roll(win, n - rot, 0)
                for a in range(5):
                    j = 8 * a + rot - 2
                    if 0 <= j < CONV_WIDTH:
                        dwacc_ref[j] += fold8(du1 * shifted[8 * a:8 * a + CONV_TILE, :])
            return c

        lax.fori_loop(0, SEQ // CONV_TILE, pass1, 0)

        def pass2(i, c):
            t0 = pl.multiple_of(i * CONV_TILE, CONV_TILE)
            win = dpad_ref[pl.ds(t0, CONV_WIN), :]
            du0 = _shifted_sum(win, [(30 - j, w_ref[j:j + 1, :]) for j in range(CONV_WIDTH)])
            a = ci_ref[pl.ds(t0, CONV_TILE), 0:CONV_DIM]
            sb = _sigmoid(ci_ref[pl.ds(t0, CONV_TILE), CONV_DIM:2 * CONV_DIM])
            dci_ref[pl.ds(t0, CONV_TILE), 0:CONV_DIM] = (du0 * sb).astype(MM)
            dci_ref[pl.ds(t0, CONV_TILE), CONV_DIM:2 * CONV_DIM] = (du0 * a * (sb * (1.0 - sb))).astype(MM)
            return c

        lax.fori_loop(0, SEQ // CONV_TILE, pass2, 0)

        for j in range(CONV_WIDTH):
            dw_ref[j:j + 1, :] = jnp.sum(dwacc_ref[j], axis=0, keepdims=True)
        dw_ref[CONV_WIDTH:32, :] = jnp.zeros((32 - CONV_WIDTH, CONV_DIM), F32)
        dg_ref[...] = jnp.sum(vacc_ref[0], axis=0, keepdims=True)
        db_ref[...] = jnp.sum(vacc_ref[1], axis=0, keepdims=True)
        dbdw_ref[...] = jnp.sum(vacc_ref[2], axis=0, keepdims=True)

    vec = jax.ShapeDtypeStruct((1, CONV_DIM), F32)
    return pl.pallas_call(
        body, name="conv_bwd",
        out_shape=(jax.ShapeDtypeStruct((SEQ, 2 * CONV_DIM), MM), jax.ShapeDtypeStruct((32, CONV_DIM), F32),
                   vec, vec, vec),
        in_specs=[VMEM_SPEC] * 7, out_specs=[VMEM_SPEC] * 5,
        scratch_shapes=[pltpu.VMEM((SEQ + 32, CONV_DIM), F32), pltpu.VMEM((SEQ + 32, CONV_DIM), F32),
                        pltpu.VMEM((CONV_WIDTH, 8, CONV_DIM), F32), pltpu.VMEM((3, 8, CONV_DIM), F32)],
        compiler_params=_params(),
    )(du3, u1, ci, w_dw, ln_g, ln_b, after)


def attn_bwd(q, k, v, datt, rc, after):
    def body(q_ref, k_ref, v_ref, do_ref, rc_ref, after_ref, dqkv_ref, dqa_ref, dka_ref, dva_ref, pc_ref, z_ref,
             sig1_ref, sig2_ref, g_ref, spb_ref, gb_ref, ar_ref, dzr_ref, dzc_ref, qm_ref, km_ref, dom_ref):
        lane, row, _ = _head_masks()
        _split_heads(q_ref, qm_ref)
        _split_heads(k_ref, km_ref)
        _split_heads(do_ref, dom_ref)
        for ref in (dqa_ref, dka_ref, dva_ref, pc_ref):
            ref[...] = jnp.zeros_like(ref)
        z_ref[...] = jnp.full(z_ref.shape, NO_SCORE, F32)
        for ref in (sig1_ref, sig2_ref, spb_ref, ar_ref, g_ref, gb_ref, dzr_ref, dzc_ref):
            ref[...] = jnp.zeros_like(ref)
        w_suffix = _cumsum_weights(suffix=True, with_total=False)
        w_prefix = _cumsum_weights(suffix=False, with_total=True)

        def step(pairs):
            (ia, ja), (ib, jb), (ic, jc), (id_, jd) = pairs
            ka, qb_, kb_, kc, qd, kd = (pl.multiple_of(jnp.maximum(b, 0) * TQ, TQ) for b in (ja, ib, jb, jc, id_, jd))
            qa2, qb2, qc2, qd2, kd2 = (pl.multiple_of(jnp.maximum(b, 0) * 2 * TQ, 2 * TQ)
                                       for b in (ia, ib, ic, id_, jd))
            bias_a = _score_bias(lane, row, ia, ja)
            rc_rows = rc_ref[pl.ds(qb_, TQ), :]
            first_c = jc == 0
            for p in range(N_PAIRS):
                cols = slice(128 * p, 128 * (p + 1))
                k_a = k_ref[pl.ds(ka, TQ), cols]
                v_b = v_ref[pl.ds(kb_, TQ), cols]
                dqa_ref[pl.ds(qd, TQ), cols] += _dot(dzc_ref[p], km_ref[p, pl.ds(kd2, 2 * TQ), :])
                dka_ref[pl.ds(kd, TQ), cols] += _dot_tn(dzr_ref[p], qm_ref[p, pl.ds(qd2, 2 * TQ), :])
                dva_ref[pl.ds(kc, TQ), cols] += _dot_tn(ar_ref[p], dom_ref[p, pl.ds(qc2, 2 * TQ), :])
                for h in range(2):
                    hh = 2 * p + h
                    rows = slice(TQ * h, TQ * (h + 1))
                    r = _dot(gb_ref[hh], w_prefix)
                    p_in = jnp.where(first_c, 0.0, pc_ref[hh])
                    dz = (g_ref[hh] - sig2_ref[hh] * (r[:, :128] + p_in)).astype(MM)
                    dzc_ref[p, :, rows] = dz
                    dzr_ref[p, rows, :] = dz
                    pc_ref[hh] = p_in + r[:, 128:]
                    r_in = jnp.sum(jnp.where(lane == 16 * hh + jb, rc_rows, 0.0), axis=1, keepdims=True)
                    a = jnp.exp(z_ref[hh] - (_dot(spb_ref[hh], w_suffix) + r_in))
                    g = _dot_nt(dom_ref[p, pl.ds(qb2 + TQ * h, TQ), :], v_b) * a
                    ar_ref[p, rows, :] = a.astype(MM)
                    g_ref[hh] = g
                    gb_ref[hh] = g.astype(MM)
                    sig2_ref[hh] = sig1_ref[hh]
                    z = _dot_nt(qm_ref[p, pl.ds(qa2 + TQ * h, TQ), :], k_a) + bias_a
                    sp = _softplus(z)
                    sig1_ref[hh] = jnp.exp(z - sp)
                    z_ref[hh] = z
                    spb_ref[hh] = sp.astype(MM)

        _block_pipeline(4, False, step)
        dqkv_ref[:, 0:ATT_DIM] = (dqa_ref[...] * ATT_SCALE).astype(MM)
        dqkv_ref[:, ATT_DIM:2 * ATT_DIM] = dka_ref[...].astype(MM)
        dqkv_ref[:, 2 * ATT_DIM:3 * ATT_DIM] = dva_ref[...].astype(MM)

    split = pltpu.VMEM((N_PAIRS, 2 * SEQ, 128), MM)
    return pl.pallas_call(
        body, name="attn_bwd", out_shape=jax.ShapeDtypeStruct((SEQ, 3 * ATT_DIM), MM),
        in_specs=[VMEM_SPEC] * 6, out_specs=VMEM_SPEC,
        scratch_shapes=[pltpu.VMEM((SEQ, ATT_DIM), F32)] * 3 + [pltpu.VMEM((8, TQ, 128), F32)] * 5
                       + [pltpu.VMEM((8, TQ, 128), MM)] * 2
                       + [pltpu.VMEM((N_PAIRS, 2 * TQ, 128), MM)] * 2 + [pltpu.VMEM((N_PAIRS, TQ, 256), MM)]
                       + [split] * 3,
        compiler_params=_params(),
    )(q, k, v, datt, rc, after)


DPROJ_PIECES = ((0, 1024), (1024, 2560), (2560, 4608))


def _dproj_segments(j):
    g0, g1 = j * IN_SHARD, (j + 1) * IN_SHARD
    segs = []
    for p, (s, e) in enumerate(DPROJ_PIECES):
        lo, hi = max(s, g0), min(e, g1)
        if lo < hi:
            segs.append((p, lo - s, lo - g0, hi - lo))
    return segs


def in_proj_bwd(pieces, w_in_g, x, dx2, g1, after):
    def body(p0_ref, p1_ref, p2_ref, w_hbm, x_ref, dx2_ref, g_ref, after_ref, dx_ref, dg_ref, w_ref, sems):
        wait_shard = _load_once(w_hbm, w_ref, sems)
        p_refs = (p0_ref, p1_ref, p2_ref)
        dh = None
        for j in range(N_CHIPS):
            wait_shard(j)
            for p, lo, off, width in _dproj_segments(j):
                t = _dot_nt(p_refs[p][:, lo:lo + width], w_ref[j, :, off:off + width])
                dh = t if dh is None else dh + t
        n1, r1 = _rms(x_ref[...])
        dx_ref[...] = dx2_ref[...] + _rms_bwd(dh * g_ref[...], n1, r1)
        _acc_rows(dg_ref, jnp.sum(dh * n1, axis=0, keepdims=True), pl.program_id(0) == 0)

    vec = _full_spec((1, D_MODEL))
    return pl.pallas_call(
        body, name="in_proj_bwd", grid=(SEQ // TM,),
        out_shape=[jax.ShapeDtypeStruct((SEQ, D_MODEL), F32), jax.ShapeDtypeStruct((1, D_MODEL), F32)],
        in_specs=[_row_tile_spec(p.shape[1]) for p in pieces]
                 + [ANY, _row_tile_spec(D_MODEL), _row_tile_spec(D_MODEL), vec, TOKEN_SPEC],
        out_specs=[_row_tile_spec(D_MODEL), vec],
        scratch_shapes=_load_once_scratch(w_in_g),
        compiler_params=_params(("arbitrary",)),
    )(*pieces, w_in_g, x, dx2, g1, after)


def weight_grad_in(h1, pieces):
    kh = D_MODEL // 2

    def body(a_ref, p0_ref, p1_ref, p2_ref, o_ref):
        p_refs = (p0_ref, p1_ref, p2_ref)
        a = a_ref[...]
        for j in range(N_CHIPS):
            @pl.when(pl.program_id(1) == j)
            def _():
                for p, lo, off, width in _dproj_segments(j):
                    o_ref[0, 0, :, off:off + width] = _dot_tn(a, p_refs[p][:, lo:lo + width]).astype(MM)

    return pl.pallas_call(
        body, name="dw_in", grid=(2, N_CHIPS), out_shape=jax.ShapeDtypeStruct((N_CHIPS, 2, kh, IN_SHARD), MM),
        in_specs=[pl.BlockSpec((SEQ, kh), lambda h, j: (0, h))] + [_weight_spec(p.shape) for p in pieces],
        out_specs=pl.BlockSpec((1, 1, kh, IN_SHARD), lambda h, j: (j, h, 0, 0)),
        compiler_params=_params(("arbitrary", "arbitrary")),
    )(h1, *pieces)


def weight_grad(a, b, name, col_sharded, tk=None):
    kin, n = a.shape[1], b.shape[1]

    def body(a_ref, b_ref, o_ref):
        if col_sharded:
            o_ref[0, 0] = _dot_tn(a_ref[...], b_ref[...]).astype(MM)
        else:
            o_ref[...] = _dot_tn(a_ref[...], b_ref[...]).astype(MM)

    if col_sharded:
        kh, ns = kin // 2, n // N_CHIPS
        out = jax.ShapeDtypeStruct((N_CHIPS, 2, kh, ns), MM)
        grid = (2, N_CHIPS)
        in_specs = [pl.BlockSpec((SEQ, kh), lambda h, j: (0, h)), pl.BlockSpec((SEQ, ns), lambda h, j: (0, j))]
        out_spec = pl.BlockSpec((1, 1, kh, ns), lambda h, j: (j, h, 0, 0))
        sem = ("arbitrary", "arbitrary")
    else:
        out = jax.ShapeDtypeStruct((kin, n), MM)
        grid = (kin // tk,)
        in_specs = [pl.BlockSpec((SEQ, tk), lambda r: (0, r)), pl.BlockSpec((SEQ, n), lambda r: (0, 0))]
        out_spec = pl.BlockSpec((tk, n), lambda r: (r, 0))
        sem = ("arbitrary",)
    res = pl.pallas_call(
        body, name=name, grid=grid, out_shape=out, in_specs=in_specs, out_specs=out_spec,
        compiler_params=_params(sem),
    )(a, b)
    if not col_sharded:
        res = res.reshape(N_CHIPS, 2, kin // (2 * N_CHIPS), n)
    return res


def weight_grad_mix(merged, dmix, u3, dco, att, dao):
    operands = (merged, dmix, u3, dco, att, dao)
    n_out, n_br = D_MODEL // 2, CONV_DIM // 2

    def body(*refs):
        hbm, (o_out, o_cb, o_ab), bufs, sems = refs[:6], refs[6:9], refs[9:15], refs[15]
        copies = [pltpu.make_async_copy(hbm[i], bufs[i], sems.at[i]) for i in range(6)]
        for cp in copies:
            cp.start()
        m_ref, dm_ref, u_ref, dco_ref, a_ref, dao_ref = bufs
        copies[0].wait()
        copies[1].wait()
        for h in range(2):
            o_out[h * n_out:(h + 1) * n_out, :] = _dot_tn(m_ref[:, h * n_out:(h + 1) * n_out], dm_ref[...]).astype(MM)
        for br, (a, d, o) in enumerate(((u_ref, dco_ref, o_cb), (a_ref, dao_ref, o_ab))):
            copies[2 + 2 * br].wait()
            copies[3 + 2 * br].wait()
            for h in range(2):
                g = _dot_tn(a[:, h * n_br:(h + 1) * n_br], d[...])
                for j in range(N_CHIPS):
                    o[j, h] = g[:, j * BR_SHARD:(j + 1) * BR_SHARD].astype(MM)

    branch = jax.ShapeDtypeStruct((N_CHIPS, 2, n_br, BR_SHARD), MM)
    dw_out, dw_cb, dw_ab = pl.pallas_call(
        body, name="dw_mix", out_shape=[jax.ShapeDtypeStruct((D_MODEL, D_MODEL), MM), branch, branch],
        in_specs=[ANY] * 6, out_specs=[VMEM_SPEC] * 3,
        scratch_shapes=[pltpu.VMEM(a.shape, a.dtype) for a in operands] + [pltpu.SemaphoreType.DMA((6,))],
        compiler_params=_params(),
    )(*operands)
    return dw_out.reshape(N_CHIPS, 2, D_MODEL // (2 * N_CHIPS), D_MODEL), dw_cb, dw_ab


def _place():
    x, y, c = lax.axis_index("x"), lax.axis_index("y"), lax.axis_index("c")
    chips = [(1 - x, y), (x, 1 - y), (1 - x, 1 - y)]
    return x, y, c, chips


def _rcopy(src, dst, send_sem, recv_sem, dev):
    return pltpu.make_async_remote_copy(src_ref=src, dst_ref=dst, send_sem=send_sem, recv_sem=recv_sem,
                                        device_id=dev, device_id_type=MESH)


class _Gather:
    N_MOVES = 6

    def __init__(self, shapes, w, o, scratch):
        self.n, self.shapes, self.w, self.o = len(w), shapes, w, o
        self.send, self.recv, self.psend, self.precv, self.loc_in, self.loc_out = scratch[:6]
        self.raw, self.stage = scratch[6:6 + self.n], scratch[6 + self.n:]
        x, y, c, self.chips = _place()
        self.c = c
        self.me, k_x, k_y, k_far = 2 * x + y, 2 * (1 - x) + y, 2 * x + (1 - y), 2 * (1 - x) + (1 - y)
        to_x, to_y = (1 - x, y, c), (x, 1 - y, c)
        self.sib = (x, y, 1 - c)
        self.sent_as = [(self.me, 0, to_x), (self.me, 1, to_y), (self.me, 1, to_x), (self.me, 0, to_y),
                        (k_x, 0, to_y), (k_y, 1, to_x)]
        self.arrives_as = [(k_x, 0, to_x), (k_y, 1, to_y), (k_x, 1, to_x), (k_y, 0, to_y),
                           (k_far, 0, to_y), (k_far, 1, to_x)]
        self.sent_on_after = {0: 4, 1: 5}

    @staticmethod
    def scratch(shards):
        n = len(shards)
        sems = pltpu.SemaphoreType.DMA
        m = _Gather.N_MOVES * n
        return ([sems((m,)), sems((m,)), sems((m,)), sems((m,)), sems((n,)), sems((n,))]
                + [pltpu.VMEM(s.shape, s.dtype) for s in shards] + [pltpu.VMEM(s.shape, MM) for s in shards])

    @staticmethod
    def out_shapes(shards):
        return [jax.ShapeDtypeStruct((N_CHIPS,) + s.shape, MM) for s in shards]

    def _rows(self, t, quarter, cc):
        rq = self.shapes[t][0] // 4
        return pl.ds((2 * cc + quarter) * rq, rq)

    def _chip(self, j):
        cx, cy = self.chips[j]
        return 2 * cx + cy, (cx, cy, self.c)

    def local_in(self, t):
        return pltpu.make_async_copy(self.w[t], self.raw[t], self.loc_in.at[t])

    def local_out(self, t):
        return pltpu.make_async_copy(self.stage[t], self.o[t].at[self.me], self.loc_out.at[t])

    def sent(self, i, t):
        k, quarter, dev = self.sent_as[i]
        rows = self._rows(t, quarter, self.c)
        there = self.o[t].at[k, rows, :]
        return _rcopy(self.stage[t].at[rows, :] if i < 4 else there, there,
                      self.send.at[i * self.n + t], self.recv.at[i * self.n + t], dev)

    def arrived(self, i, t):
        k, quarter, dev = self.arrives_as[i]
        blk = self.o[t].at[k, self._rows(t, quarter, self.c), :]
        return _rcopy(blk, blk, self.send.at[i * self.n + t], self.recv.at[i * self.n + t], dev)

    def passed(self, i, t, cc):
        k, quarter, _ = self.arrives_as[i]
        blk = self.o[t].at[k, self._rows(t, quarter, cc), :]
        return _rcopy(blk, blk, self.psend.at[i * self.n + t], self.precv.at[i * self.n + t], self.sib)

    def start(self):
        for t in range(self.n):
            self.local_in(t).start()
        for t in range(self.n):
            self.local_in(t).wait()
            self.stage[t][...] = self.raw[t][...].astype(MM)
            self.local_out(t).start()
        for i in range(4):
            for t in range(self.n):
                self.sent(i, t).start()

    def forward(self):
        for i in range(self.N_MOVES):
            for t in range(self.n):
                self.arrived(i, t).wait_recv()
                if i in self.sent_on_after:
                    self.sent(self.sent_on_after[i], t).start()
                self.passed(i, t, self.c).start()

    def finish(self):
        for i in range(self.N_MOVES):
            for t in range(self.n):
                self.passed(i, t, 1 - self.c).wait_recv()
        for i in range(self.N_MOVES):
            for t in range(self.n):
                self.sent(i, t).wait_send()
                self.passed(i, t, self.c).wait_send()
        for t in range(self.n):
            self.local_out(t).wait()


def all_gather_weights(shards, small, later):
    n, m = len(shards), len(later)
    shapes = [s.shape for s in shards]

    def body(*refs):
        w = refs[:n]
        sm = refs[n]
        lw = refs[n + 1:n + 1 + m]
        o = refs[n + 1 + m:2 * n + 1 + m]
        osm = refs[2 * n + 1 + m]
        lo = refs[2 * n + 2 + m:2 * n + 2 + 2 * m]
        scratch = refs[2 * n + 2 + 2 * m:]
        ssend, srecv, sloc, lsem_in, lsem_out = scratch[:5]
        lraw, lstage = scratch[5:5 + m], scratch[5 + m:5 + 2 * m]
        g = _Gather(shapes, w, o, scratch[5 + 2 * m:])
        own = pltpu.make_async_copy(sm, osm.at[g.me], sloc)
        own.start()
        loads = [pltpu.make_async_copy(lw[t], lraw[t], lsem_in.at[t]) for t in range(m)]
        for cp in loads:
            cp.start()
        g.start()
        small_cps = [_rcopy(sm, osm.at[g.me], ssend.at[j], srecv.at[j], g._chip(j)[1]) for j in range(3)]
        for cp in small_cps:
            cp.start()
        places = []
        for t in range(m):
            loads[t].wait()
            lstage[t][...] = lraw[t][...].astype(MM)
            places.append(pltpu.make_async_copy(lstage[t], lo[t].at[g.me], lsem_out.at[t]))
            places[t].start()
        g.forward()
        g.finish()
        for j in range(3):
            k, dev = g._chip(j)
            _rcopy(sm, osm.at[k], ssend.at[j], srecv.at[j], dev).wait_recv()
            small_cps[j].wait_send()
        own.wait()
        for cp in places:
            cp.wait()

    out_shape = _Gather.out_shapes(shards)
    out_shape.append(jax.ShapeDtypeStruct((N_CHIPS,) + small.shape, small.dtype))
    out_shape += _Gather.out_shapes(later)
    sems = pltpu.SemaphoreType.DMA
    return pl.pallas_call(
        body, name="all_gather_weights", out_shape=out_shape,
        in_specs=[ANY] * (n + 1 + m), out_specs=[ANY] * (n + 1 + m),
        scratch_shapes=[sems((3,)), sems((3,)), sems, sems((m,)), sems((m,))]
                       + [pltpu.VMEM(s.shape, s.dtype) for s in later] + [pltpu.VMEM(s.shape, MM) for s in later]
                       + _Gather.scratch(shards),
        compiler_params=_params(),
    )(*shards, small, *later)


HBM_SPEC = pl.BlockSpec(memory_space=pltpu.HBM)
SEM_SPEC = pl.BlockSpec(memory_space=pltpu.SEMAPHORE)
DATAFLOW = pltpu.SideEffectType.DATAFLOW_SIDE_EFFECTING


def split_start(name, bufs, n_copies, copies):
    nb = len(bufs)

    def body(*refs):
        for cp in copies(refs[:nb], refs[nb], refs[nb + 1]):
            cp.start()
        token = refs[2 * nb + 2]
        token[...] = jnp.zeros_like(token)

    sems = [pltpu.SemaphoreType.DMA((n_copies,))] * 2
    res = pl.pallas_call(
        body, name=name,
        out_shape=sems + [pltpu.HBM(a.shape, a.dtype) for a in bufs] + [jax.ShapeDtypeStruct((8, 128), F32)],
        in_specs=[HBM_SPEC] * nb, out_specs=[SEM_SPEC] * 2 + [HBM_SPEC] * nb + [VMEM_SPEC],
        input_output_aliases={i: 2 + i for i in range(nb)},
        compiler_params=pltpu.CompilerParams(has_side_effects=DATAFLOW),
    )(*[pltpu.with_memory_space_constraint(a, pltpu.HBM) for a in bufs])
    return res[:-1], res[-1]


def split_wait(name, state, after, copies):
    sems, bufs = state[:2], state[2:]
    nb = len(bufs)

    def body(*refs):
        for cp in copies(refs[:nb], refs[nb], refs[nb + 1]):
            cp.wait_send()
            cp.wait_recv()

    return pl.pallas_call(
        body, name=name, out_shape=[pltpu.HBM(a.shape, a.dtype) for a in bufs],
        in_specs=[HBM_SPEC] * nb + [SEM_SPEC] * 2 + [ANY] * len(after), out_specs=[HBM_SPEC] * nb,
        input_output_aliases={i: i for i in range(nb)},
        compiler_params=pltpu.CompilerParams(has_side_effects=DATAFLOW),
    )(*bufs, *sems, *after)


class _Shifted:
    def __init__(self, sems, first):
        self.sems, self.first = sems, first

    @property
    def at(self):
        return self

    def __getitem__(self, i):
        return self.sems.at[self.first + i]


def _scatter_copies(n):
    def copies(refs, send, recv):
        _, _, c, chips = _place()
        return [_rcopy(refs[t].at[2 * cx + cy], refs[n + t].at[j], send.at[3 * t + j], recv.at[3 * t + j], (cx, cy, c))
                for t in range(n) for j, (cx, cy) in enumerate(chips)]
    return copies


def scatter_start(parts, tag):
    lands = [lax.empty((3,) + p.shape[1:], p.dtype) for p in parts]
    return split_start("scatter_start_" + tag, list(parts) + lands, 3 * len(parts), _scatter_copies(len(parts)))


def scatter_wait(state, after, tag):
    n = (len(state) - 2) // 2
    return split_wait("scatter_wait_" + tag, state, after, _scatter_copies(n))[n:]


def _gather_copies(shapes, level):
    n = len(shapes)

    def copies(refs, send, recv):
        x, y, c, chips = _place()
        out = []
        for t in range(n):
            rh = shapes[t][0] // 2
            for j, (cx, cy) in enumerate(chips):
                k, dev = (2 * x + y, (cx, cy, c)) if level == 1 else (2 * cx + cy, (x, y, 1 - c))
                blk = refs[t].at[k, pl.ds(c * rh, rh), :]
                out.append(_rcopy(blk, blk, send.at[3 * t + j], recv.at[3 * t + j], dev))
        return out
    return copies


def _sibling_copies(n, other_half):
    def copies(refs, send, recv):
        x, y, c, _ = _place()
        return [_rcopy(refs[t].at[:, 1 - c] if other_half else refs[t], refs[n + t], send.at[t], recv.at[t],
                       (x, y, 1 - c)) for t in range(n)]
    return copies


def sibling_start(srcs, other_half, tag):
    lands = [lax.empty((a.shape[0],) + a.shape[2:] if other_half else a.shape, a.dtype) for a in srcs]
    return split_start("sibling_start_" + tag, list(srcs) + lands, len(srcs),
                       _sibling_copies(len(srcs), other_half))


def sibling_wait(state, after, other_half, tag):
    n = (len(state) - 2) // 2
    res = split_wait("sibling_wait_" + tag, state, after, _sibling_copies(n, other_half))
    return res[:n], res[n:]


def small_pack(ddw, v512, v1024, loss_parts):
    rows, width = PACK_ROWS, 512
    n512, n1024 = len(VEC512), len(VEC1024)

    def body(*refs):
        ddw_ref = refs[0]
        a_refs = refs[1:1 + n512]
        b_refs = refs[1 + n512:1 + n512 + n1024]
        lp_ref, o_ref, p_ref = refs[1 + n512 + n1024:]
        p_ref[...] = jnp.zeros_like(p_ref)
        p_ref[0:32, :] = ddw_ref[...]
        p_ref[LOSS_ROW:LOSS_ROW + 1, 0:128] = jnp.sum(lp_ref[...], axis=0, keepdims=True) * 0.125
        for i, r in enumerate(a_refs):
            p_ref[32 + i:33 + i, :] = r[...]
        for i, r in enumerate(b_refs):
            base = 32 + n512 + 2 * i
            p_ref[base:base + 1, :] = r[:, 0:512]
            p_ref[base + 1:base + 2, :] = r[:, 512:1024]
        x, y, c, _ = _place()
        o_ref[4 * x + 2 * y + c] = p_ref[...]

    n_in = 2 + n512 + n1024
    return pl.pallas_call(
        body, name="small_pack", out_shape=jax.ShapeDtypeStruct((8, rows, width), F32),
        in_specs=[VMEM_SPEC] * n_in, out_specs=VMEM_SPEC,
        scratch_shapes=[pltpu.VMEM((rows, width), F32)],
    )(ddw, *[v512[n] for n in VEC512], *[v1024[n] for n in VEC1024], loss_parts)


def _small_copies(refs, send, recv):
    x, y, c, _ = _place()
    mine = refs[0].at[4 * x + 2 * y + c]
    peers = [(1 - x if k & 4 else x, 1 - y if k & 2 else y, 1 - c if k & 1 else c) for k in range(1, 8)]
    return [_rcopy(mine, mine, send.at[i], recv.at[i], dev) for i, dev in enumerate(peers)]


def _row_block(r):
    for tr in (512, 352, 256, 128):
        if r % tr == 0:
            return tr
    return r


def add_halves(g, recv, name):
    _, _, r, w = g.shape
    tr = _row_block(r)

    def body(g_ref, r_ref, ob_ref, own_ref):
        k = pl.program_id(1)
        me = 2 * lax.axis_index("x") + lax.axis_index("y")
        t = g_ref[0, 0].astype(F32) + r_ref[0].astype(F32)
        ob_ref[0] = t.astype(MM)
        mine = jnp.where(k == me, t, 0.0)

        @pl.when(k == 0)
        def _():
            own_ref[...] = mine

        @pl.when(k != 0)
        def _():
            own_ref[...] += mine

    return pl.pallas_call(
        body, name=name, grid=(r // tr, N_CHIPS),
        in_specs=[pl.BlockSpec((1, 1, tr, w), lambda i, k: (k, lax.axis_index("c"), i, 0)),
                  pl.BlockSpec((1, tr, w), lambda i, k: (k, i, 0))],
        out_specs=[pl.BlockSpec((1, tr, w), lambda i, k: (k, i, 0)),
                   pl.BlockSpec((tr, w), lambda i, k: (i, 0))],
        out_shape=(jax.ShapeDtypeStruct((N_CHIPS, r, w), MM), jax.ShapeDtypeStruct((r, w), F32)),
        compiler_params=_params(("arbitrary", "arbitrary")),
    )(g, recv)


def sum_parts(own, rin, after, name):
    _, r, w = rin.shape
    tr = _row_block(r)

    def body(o_ref, r_ref, after_ref, out_ref):
        out_ref[...] = ((o_ref[...] + r_ref[0].astype(F32)) + r_ref[1].astype(F32)) + r_ref[2].astype(F32)

    return pl.pallas_call(
        body, name=name, grid=(r // tr,), out_shape=jax.ShapeDtypeStruct((r, w), F32),
        in_specs=[pl.BlockSpec((tr, w), lambda i: (i, 0)), pl.BlockSpec((3, tr, w), lambda i: (0, i, 0)),
                  _full_spec((8, 128))],
        out_specs=pl.BlockSpec((tr, w), lambda i: (i, 0)),
        compiler_params=_params(("arbitrary",)),
    )(own, rin, after)


def _adamw_math(w, g, m, v):
    mn = ADAM_B1 * m + (1.0 - ADAM_B1) * g
    vn = ADAM_B2 * v + (1.0 - ADAM_B2) * (g * g)
    m_hat = mn / (1.0 - ADAM_B1 ** ADAM_STEP)
    v_hat = vn / (1.0 - ADAM_B2 ** ADAM_STEP)
    return -ADAM_LR * (m_hat / (jnp.sqrt(v_hat) + ADAM_EPS) + ADAM_WD * w), mn, vn


def adamw(w, mine, other, m, v, name):
    r, c = w.shape
    rh = r // 2
    tr = _row_block(rh)
    if c >= 1024 and tr % 512 == 0:
        tr = 256
    nb = rh // tr

    def body(w_ref, a_ref, b_ref, m_ref, v_ref, go_ref, d_ref, mo_ref, vo_ref):
        gv = jnp.where(lax.axis_index("c") == pl.program_id(0), a_ref[...], b_ref[...])
        go_ref[...] = gv
        d_ref[...], mo_ref[...], vo_ref[...] = _adamw_math(w_ref[...], gv, m_ref[...], v_ref[...])

    def half(of_sibling):
        def index(h, i):
            owner = lax.axis_index("c")
            owner = 1 - owner if of_sibling else owner
            return jnp.where(h == owner, i, jnp.where(h < owner, 0, nb - 1)), 0
        return pl.BlockSpec((tr, c), index)

    spec = pl.BlockSpec((tr, c), lambda h, i: (h * nb + i, 0))
    out = jax.ShapeDtypeStruct((r, c), F32)
    return pl.pallas_call(
        body, name=name, grid=(2, nb), out_shape=(out, out, out, out),
        in_specs=[spec, half(False), half(True), spec, spec], out_specs=[spec] * 4,
        compiler_params=_params(("arbitrary", "arbitrary")),
    )(w, mine, other, m, v)


def adamw_small(packs, params, after):
    names = list(params)
    flat = [a for n in names for a in params[n]]

    def body(*refs):
        p_ref = refs[0]
        ins = refs[1:1 + 3 * len(names)]
        loss_ref, g_ref = refs[2 + 3 * len(names):4 + 3 * len(names)]
        outs = refs[4 + 3 * len(names):]
        total = p_ref[0]
        for d in range(1, 8):
            total = total + p_ref[d]
        g_ref[...] = total
        loss_ref[...] = g_ref[LOSS_ROW:LOSS_ROW + 1, 0:1]
        me = 2 * lax.axis_index("x") + lax.axis_index("y")
        for i, n in enumerate(names):
            w_ref, m_ref, v_ref = ins[3 * i:3 * i + 3]
            go_ref, d_ref, mo_ref, vo_ref = outs[4 * i:4 * i + 4]
            if n == "conv_dw_w":
                gv = jnp.zeros((CONV_WIDTH, 128), F32)
                for k in range(N_CHIPS):
                    gv = gv + jnp.where(me == k, g_ref[0:CONV_WIDTH, 128 * k:128 * (k + 1)], 0.0)
            elif n in VEC512:
                r0 = 32 + VEC512.index(n)
                gv = g_ref[r0:r0 + 1, :]
            else:
                r0 = 32 + len(VEC512) + 2 * VEC1024.index(n)
                gv = jnp.concatenate([g_ref[r0:r0 + 1, :], g_ref[r0 + 1:r0 + 2, :]], axis=1)
            go_ref[...] = gv
            d_ref[...], mo_ref[...], vo_ref[...] = _adamw_math(w_ref[...], gv, m_ref[...], v_ref[...])

    out_shape = [jax.ShapeDtypeStruct((1, 1), F32), jax.ShapeDtypeStruct(packs.shape[1:], F32)]
    out_shape += [jax.ShapeDtypeStruct(params[n][0].shape, F32) for n in names for _ in range(4)]
    res = pl.pallas_call(
        body, name="adamw_small", out_shape=out_shape,
        in_specs=[VMEM_SPEC] * (2 + len(flat)), out_specs=[VMEM_SPEC] * len(out_shape),
        compiler_params=_params(),
    )(packs, *flat, after)
    return res[0], res[1], {n: res[2 + 4 * i:6 + 4 * i] for i, n in enumerate(names)}


REST = ("w_ffn_up", "w_ffn_down", "w_out", "w_conv_branch", "w_att_branch")
VEC512 = ("conv_dw_b", "conv_ln_g", "conv_ln_b")
VEC1024 = ("norm_mix_pre", "b_conv_branch", "norm_mix_post", "norm_ffn_pre", "norm_ffn_post")
PACK_ROWS = 48
LOSS_ROW = 47


def kernel(x, norm_mix_pre, w_in, conv_dw_w, conv_dw_b, conv_ln_g, conv_ln_b, w_conv_branch, b_conv_branch, w_att_branch, w_out, norm_mix_post, norm_ffn_pre, w_ffn_up, w_ffn_down, norm_ffn_post, loss_target, m_norm_mix_pre, m_w_in, m_conv_dw_w, m_conv_dw_b, m_conv_ln_g, m_conv_ln_b, m_w_conv_branch, m_b_conv_branch, m_w_att_branch, m_w_out, m_norm_mix_post, m_norm_ffn_pre, m_w_ffn_up, m_w_ffn_down, m_norm_ffn_post, v_norm_mix_pre, v_w_in, v_conv_dw_w, v_conv_dw_b, v_conv_ln_g, v_conv_ln_b, v_w_conv_branch, v_b_conv_branch, v_w_att_branch, v_w_out, v_norm_mix_post, v_norm_ffn_pre, v_w_ffn_up, v_w_ffn_down, v_norm_ffn_post):
    weights = dict(norm_mix_pre=norm_mix_pre, w_in=w_in, conv_dw_w=conv_dw_w, conv_dw_b=conv_dw_b, conv_ln_g=conv_ln_g, conv_ln_b=conv_ln_b, w_conv_branch=w_conv_branch, b_conv_branch=b_conv_branch, w_att_branch=w_att_branch, w_out=w_out, norm_mix_post=norm_mix_post, norm_ffn_pre=norm_ffn_pre, w_ffn_up=w_ffn_up, w_ffn_down=w_ffn_down, norm_ffn_post=norm_ffn_post)
    mom = dict(norm_mix_pre=m_norm_mix_pre, w_in=m_w_in, conv_dw_w=m_conv_dw_w, conv_dw_b=m_conv_dw_b, conv_ln_g=m_conv_ln_g, conv_ln_b=m_conv_ln_b, w_conv_branch=m_w_conv_branch, b_conv_branch=m_b_conv_branch, w_att_branch=m_w_att_branch, w_out=m_w_out, norm_mix_post=m_norm_mix_post, norm_ffn_pre=m_norm_ffn_pre, w_ffn_up=m_w_ffn_up, w_ffn_down=m_w_ffn_down, norm_ffn_post=m_norm_ffn_post)
    var = dict(norm_mix_pre=v_norm_mix_pre, w_in=v_w_in, conv_dw_w=v_conv_dw_w, conv_dw_b=v_conv_dw_b, conv_ln_g=v_conv_ln_g, conv_ln_b=v_conv_ln_b, w_conv_branch=v_w_conv_branch, b_conv_branch=v_b_conv_branch, w_att_branch=v_w_att_branch, w_out=v_w_out, norm_mix_post=v_norm_mix_post, norm_ffn_pre=v_norm_ffn_pre, w_ffn_up=v_w_ffn_up, w_ffn_down=v_w_ffn_down, norm_ffn_post=v_norm_ffn_post)
    order = list(weights)
    grads, deltas, new_m, new_v = {}, {}, {}, {}
    xs = x.reshape(SEQ, D_MODEL)
    tgt = loss_target.reshape(SEQ, D_MODEL)
    row = lambda a: a.reshape(1, -1)
    g1, g2, g3, g4 = (row(weights[n]) for n in ("norm_mix_pre", "norm_mix_post", "norm_ffn_pre", "norm_ffn_post"))
    ln_g, ln_b = row(conv_ln_g), row(conv_ln_b)

    summed, from_chips = {}, {}

    def core_sums(names, state, after, tag):
        own, from_sibling = sibling_wait(state, after, True, tag)
        for n, g, r in zip(names, own, from_sibling):
            summed[n] = add_halves(g, r, "add_" + n)

    def chip_sums(names, after):
        return [sum_parts(summed[n][1], from_chips[n], after, "sum_" + n) for n in names]

    def optimize(names, state, after, tag):
        mine, other = sibling_wait(state, after, False, tag)
        for n, a, b in zip(names, mine, other):
            grads[n], deltas[n], new_m[n], new_v[n] = adamw(weights[n], a, b, mom[n], var[n], "adamw_" + n)

    w_in_g, dw_g, *rest = all_gather_weights([w_in], conv_dw_w, [weights[n] for n in REST])
    w_dw_full = jnp.concatenate([dw_g[k] for k in range(N_CHIPS)], axis=1)
    rest_shapes = [weights[n].shape for n in REST]
    state, token = split_start("gather_start", rest, 3 * len(REST), _gather_copies(rest_shapes, 1))
    h1, ci, q, k, v, gc, ga = in_proj_fwd(xs, g1, w_in_g, token)
    u1, u3 = conv_fwd(ci, w_dw_full, row(conv_dw_b), ln_g, ln_b)
    att, rc = attn_fwd(q, k, v)
    rest = split_wait("gather_wait", state, [att], _gather_copies(rest_shapes, 1))
    pass_copies = _gather_copies(rest_shapes[2:] + rest_shapes[:2], 2)
    n_first = 3 * len(REST[2:])
    state, token = split_start("pass_start", rest[2:] + rest[:2], 3 * len(REST), pass_copies)
    passed = split_wait("pass_mix_wait", state, [], lambda *a: pass_copies(*a)[:n_first])
    w_out_g, w_cb_g, w_ab_g = passed[:3]
    w_out_g = w_out_g.reshape(D_MODEL, D_MODEL)
    co, ao, merged, mix, x2, h2 = mix_fwd(u3, att, gc, ga, xs, w_cb_g, row(b_conv_branch), w_ab_g, w_out_g,
                                          g2, g3, token)
    w_up_g, w_down_g = split_wait(
        "pass_ffn_wait", list(state[:2]) + list(passed[3:]), [h2],
        lambda refs, send, recv: _gather_copies(rest_shapes[:2], 2)(refs, _Shifted(send, n_first), _Shifted(recv, n_first)))
    w_down_g = w_down_g.reshape(2, D_FF // 2, D_MODEL)
    gate, up, act = ffn_up_fwd(h2, w_up_g)
    dff, dy, loss_parts, dg4 = ffn_down_loss(act, w_down_g, x2, tgt, g4)

    dgu = ffn_act_bwd(dff, w_down_g, gate, up)
    dx2, dmix, dg3, dg2 = ffn_in_bwd(dgu, w_up_g, x2, mix, dy, g3, g2)
    ffn_grads = [weight_grad(h2, dgu, "dw_ffn_up", True), weight_grad(act, dff, "dw_ffn_down", False, tk=UP_SHARD)]
    to_ffn, token = sibling_start(ffn_grads, True, "dw_ffn")
    dco, dao, dg, du3, datt, dbcb = merge_bwd(dmix, w_out_g, gc, ga, co, ao, w_cb_g, w_ab_g, token)
    to_mix, token = sibling_start(weight_grad_mix(merged, dmix, u3, dco, att, dao), True, "dw_mix")
    core_sums(REST[:2], to_ffn, [token], "dw_ffn")
    core_sums(REST[2:], to_mix, [summed["w_ffn_down"][1]], "dw_mix")
    state, token = scatter_start([summed[n][0] for n in REST], "rest")
    dci, ddw, dbdw, dlng, dlnb = conv_bwd(du3, u1, ci, w_dw_full, ln_g, ln_b, token)
    dqkv = attn_bwd(q, k, v, datt, rc, token)
    from_chips.update(zip(REST, scatter_wait(state, [dci, dqkv], "rest")))
    dproj = (dci, dqkv, dg)
    to_in, token = sibling_start([weight_grad_in(h1, dproj)], True, "dw_in")
    grad_x, dg1 = in_proj_bwd(dproj, w_in_g, xs, dx2, g1, token)
    v512 = dict(conv_dw_b=dbdw, conv_ln_g=dlng, conv_ln_b=dlnb)
    v1024 = dict(norm_mix_pre=dg1, b_conv_branch=dbcb, norm_mix_post=dg2, norm_ffn_pre=dg3, norm_ffn_post=dg4)
    packs = small_pack(ddw, v512, v1024, loss_parts)
    core_sums(("w_in",), to_in, [packs], "dw_in")
    to_chips = summed["w_in"][0]
    landing = lax.empty((3,) + to_chips.shape[1:], to_chips.dtype)

    def scatter_and_packs(refs, send, recv):
        return (_scatter_copies(1)(refs[:2], send, recv)
                + _small_copies(refs[2:], _Shifted(send, 3), _Shifted(recv, 3)))

    state, token = split_start("scatter_start_w_in", [to_chips, landing, packs], 3 + 7, scatter_and_packs)
    swap_up, token = sibling_start(chip_sums(REST[:1], token), False, "sum_ffn_up")
    swap_rest, token = sibling_start(chip_sums(REST[1:], token), False, "sum_rest")
    optimize(REST[:1], swap_up, [token], "sum_ffn_up")
    optimize(REST[1:], swap_rest, [new_v["w_ffn_up"]], "sum_rest")
    _, from_chips["w_in"], packs = split_wait("scatter_wait_w_in", state, [new_v[n] for n in REST], scatter_and_packs)
    swap_in, token = sibling_start(chip_sums(("w_in",), token), False, "sum_w_in")
    as_rows = lambda n, a: a if n == "conv_dw_w" else a.reshape(1, -1)
    small_names = ("conv_dw_w",) + VEC512 + VEC1024
    loss, gsum, small = adamw_small(
        packs, {n: tuple(as_rows(n, d[n]) for d in (weights, mom, var)) for n in small_names}, token)
    optimize(("w_in",), swap_in, [gsum], "sum_w_in")
    for n in small_names:
        grads[n], deltas[n], new_m[n], new_v[n] = (a.reshape(weights[n].shape) for a in small[n])

    return (loss.reshape(()), grad_x.reshape(1, SEQ, D_MODEL),*[grads[n] for n in order], *[deltas[n] for n in order],
            *[new_m[n] for n in order], *[new_v[n] for n in order])
```

```python
import jax
import jax.numpy as jnp
from jax import lax
from jax.experimental import pallas as pl
from jax.experimental.pallas import tpu as pltpu

F32 = jnp.float32
MM = jnp.bfloat16

SEQ = 2048
D_MODEL = 1024
CONV_DIM = 512
ATT_DIM = 512
CONV_WIDTH = 31
D_FF = 2816
IN_COLS = 2 * CONV_DIM + 3 * ATT_DIM + 2 * D_MODEL
N_CHIPS = 4
IN_SHARD = IN_COLS // N_CHIPS
UP_SHARD = 2 * D_FF // N_CHIPS
BR_SHARD = D_MODEL // N_CHIPS
EPS = 1e-6
ATT_SCALE = 0.125

TM = 256
GLU_ROWS = 256
TQ = 128
CONV_TILE = 64
CONV_WIN = CONV_TILE + 32
VMEM_LIMIT = 56 * 1024 * 1024

ADAM_LR = 0.001
ADAM_B1 = 0.9
ADAM_B2 = 0.999
ADAM_EPS = 1e-08
ADAM_WD = 0.01
ADAM_STEP = 10

MESH = pl.DeviceIdType.MESH
ANY = pl.BlockSpec(memory_space=pl.ANY)
VMEM_SPEC = pl.BlockSpec(memory_space=pltpu.VMEM)

NT_DIMS = (((1,), (1,)), ((), ()))
TN_DIMS = (((0,), (0,)), ((), ()))

IN_PIECES = (("ci", 0, 1024), ("q", 1024, 1536), ("k", 1536, 2048), ("v", 2048, 2560),
             ("gc", 2560, 3584), ("ga", 3584, 4608))


def _params(sem=None, vmem=VMEM_LIMIT):
    return pltpu.CompilerParams(dimension_semantics=sem, vmem_limit_bytes=vmem)


def _dot(a, b):
    return jnp.dot(a, b, preferred_element_type=F32)


def _dot_nt(a, b):
    return lax.dot_general(a, b, NT_DIMS, preferred_element_type=F32)


def _dot_tn(a, b):
    return lax.dot_general(a, b, TN_DIMS, preferred_element_type=F32)


def _sigmoid(x):
    return 1.0 / (1.0 + jnp.exp(-x))


def _rms(x):
    r = lax.rsqrt(jnp.mean(x * x, axis=-1, keepdims=True) + EPS)
    return x * r, r


def _rms_bwd(dy_g, n, r):
    return r * (dy_g - n * jnp.mean(dy_g * n, axis=-1, keepdims=True))


def _row_tile_spec(width, tm=TM):
    return pl.BlockSpec((tm, width), lambda i: (i, 0))


def _full_spec(shape):
    nd = len(shape)
    return pl.BlockSpec(shape, lambda *_: (0,) * nd)


def _weight_spec(shape):
    nd = len(shape)
    return pl.BlockSpec(shape, lambda *_: (0,) * nd, pipeline_mode=pl.Buffered(1))


def _load_once(src, dst, sems, run):
    first = pl.program_id(0) == 0
    copies = [pltpu.make_async_copy(src.at[p], dst.at[p], sems.at[p]) for p in range(src.shape[0])]

    @pl.when(first)
    def _():
        for cp in copies:
            cp.start()
        run(lambda p: copies[p].wait())

    @pl.when(jnp.logical_not(first))
    def _():
        run(lambda p: None)


def _load_once_scratch(w):
    return [pltpu.VMEM(w.shape, w.dtype), pltpu.SemaphoreType.DMA((w.shape[0],))]


def _acc_rows(ref, val, first):
    @pl.when(first)
    def _():
        ref[...] = val

    @pl.when(jnp.logical_not(first))
    def _():
        ref[...] += val


TOKEN_SPEC = pl.BlockSpec((8, 128), lambda *_: (0, 0))


def in_proj_fwd(x, g1, w_in_g, after):
    def body(x_ref, g_ref, w_hbm, after_ref, h_ref, ci_ref, q_ref, k_ref, v_ref, gc_ref, ga_ref, w_ref, sems):
        def run(wait_shard):
            n, _ = _rms(x_ref[...])
            h = (n * g_ref[...]).astype(MM)
            h_ref[...] = h
            outs = dict(ci=ci_ref, q=q_ref, k=k_ref, v=v_ref, gc=gc_ref, ga=ga_ref)
            for j in range(N_CHIPS):
                wait_shard(j)
                p = _dot(h, w_ref[j])
                g0 = j * IN_SHARD
                for name, s, e in IN_PIECES:
                    lo, hi = max(s, g0), min(e, g0 + IN_SHARD)
                    if lo < hi:
                        ref = outs[name]
                        part = p[:, lo - g0:hi - g0]
                        if name == "q":
                            part = part * ATT_SCALE
                        ref[:, lo - s:hi - s] = part.astype(ref.dtype)

        _load_once(w_hbm, w_ref, sems, run)

    out_shape = [
        jax.ShapeDtypeStruct((SEQ, D_MODEL), MM),
        jax.ShapeDtypeStruct((SEQ, 2 * CONV_DIM), F32),
        jax.ShapeDtypeStruct((SEQ, ATT_DIM), MM),
        jax.ShapeDtypeStruct((SEQ, ATT_DIM), MM),
        jax.ShapeDtypeStruct((SEQ, ATT_DIM), MM),
        jax.ShapeDtypeStruct((SEQ, D_MODEL), F32),
        jax.ShapeDtypeStruct((SEQ, D_MODEL), F32),
    ]
    return pl.pallas_call(
        body, name="in_proj_fwd", grid=(SEQ // TM,), out_shape=out_shape,
        in_specs=[_row_tile_spec(D_MODEL), _full_spec((1, D_MODEL)), ANY, TOKEN_SPEC],
        out_specs=[_row_tile_spec(s.shape[1]) for s in out_shape],
        scratch_shapes=_load_once_scratch(w_in_g),
        compiler_params=_params(("arbitrary",)),
    )(x, g1, w_in_g, after)


def _shifted_sum(win, terms):
    by_rot = {}
    for m, coef in terms:
        by_rot.setdefault(m % 8, []).append((m // 8, coef))
    acc = None
    n = win.shape[0]
    for rot in sorted(by_rot):
        shifted = win if rot == 0 else pltpu.roll(win, n - rot, 0)
        for a, coef in by_rot[rot]:
            t = coef * shifted[8 * a:8 * a + CONV_TILE, :]
            acc = t if acc is None else acc + t
    return acc


def _glu_into(ci_ref, upad_ref):
    upad_ref[0:32, :] = jnp.zeros((32, CONV_DIM), F32)

    def step(i, c):
        t0 = pl.multiple_of(i * GLU_ROWS, GLU_ROWS)
        a = ci_ref[pl.ds(t0, GLU_ROWS), 0:CONV_DIM]
        b = ci_ref[pl.ds(t0, GLU_ROWS), CONV_DIM:2 * CONV_DIM]
        upad_ref[pl.ds(t0 + 32, GLU_ROWS), :] = a * _sigmoid(b)
        return c

    lax.fori_loop(0, SEQ // GLU_ROWS, step, 0)


def _layernorm_parts(u1):
    mu = jnp.mean(u1, axis=-1, keepdims=True)
    xc = u1 - mu
    rstd = lax.rsqrt(jnp.mean(xc * xc, axis=-1, keepdims=True) + EPS)
    return xc * rstd, rstd


def conv_fwd(ci, w_dw, b_dw, ln_g, ln_b):
    def body(ci_ref, w_ref, b_ref, g_ref, bb_ref, u1_ref, u3_ref, upad_ref):
        _glu_into(ci_ref, upad_ref)

        def step(i, c):
            t0 = pl.multiple_of(i * CONV_TILE, CONV_TILE)
            win = upad_ref[pl.ds(t0, CONV_WIN), :]
            u1 = _shifted_sum(win, [(j + 2, w_ref[j:j + 1, :]) for j in range(CONV_WIDTH)]) + b_ref[...]
            u1_ref[pl.ds(t0, CONV_TILE), :] = u1
            xh, _ = _layernorm_parts(u1)
            u2 = xh * g_ref[...] + bb_ref[...]
            u3_ref[pl.ds(t0, CONV_TILE), :] = (u2 * _sigmoid(u2)).astype(MM)
            return c

        lax.fori_loop(0, SEQ // CONV_TILE, step, 0)

    return pl.pallas_call(
        body, name="conv_fwd",
        out_shape=[jax.ShapeDtypeStruct((SEQ, CONV_DIM), F32), jax.ShapeDtypeStruct((SEQ, CONV_DIM), MM)],
        in_specs=[VMEM_SPEC] * 5, out_specs=[VMEM_SPEC] * 2,
        scratch_shapes=[pltpu.VMEM((SEQ + 32, CONV_DIM), F32)],
        compiler_params=_params(),
    )(ci, w_dw, b_dw, ln_g, ln_b)


def _softplus(z):
    return jnp.maximum(z, 0.0) + jnp.log(1.0 + jnp.exp(-jnp.abs(z)))


def _cumsum_weights(suffix, with_total):
    n = 256 if with_total else 128
    r = lax.broadcasted_iota(jnp.int32, (128, n), 0)
    c = lax.broadcasted_iota(jnp.int32, (128, n), 1)
    tri = (r >= c) if suffix else (r <= c)
    return jnp.logical_or(tri, c >= 128).astype(MM)


NO_SCORE = -1e30
N_KB = SEQ // TQ


def _score_bias(lane, row, i, j):
    keep = jnp.logical_and(i >= 0, jnp.logical_or(j < i, lane < row))
    return jnp.where(keep, 0.0, NO_SCORE)


def _block_pipeline(n_stages, descending, step, on_query_block=None):
    n_lag = n_stages - 1
    none = jnp.int32(-1)

    def shift(cur, lag):
        step([cur] + [(lag[2 * s], lag[2 * s + 1]) for s in range(n_lag)])
        return (cur[0], cur[1]) + tuple(lag[:-2])

    def outer(i, lag):
        if on_query_block is not None:
            on_query_block(i)

        def inner(n, lag):
            return shift((i, i - n if descending else n), lag)
        return lax.fori_loop(0, i + 1, inner, lag)

    lag = lax.fori_loop(0, N_KB, outer, (none,) * (2 * n_lag))
    lax.fori_loop(0, n_lag, lambda n, lag: shift((none, none), lag), lag)


def _head_masks():
    lane = lax.broadcasted_iota(jnp.int32, (TQ, 128), 1)
    row = lax.broadcasted_iota(jnp.int32, (TQ, 128), 0)
    return lane, row, lane < 64


def _pick_head(x, head0, h):
    zero = jnp.zeros_like(x)
    return jnp.where(head0, x, zero) if h == 0 else jnp.where(head0, zero, x)


N_PAIRS = ATT_DIM // 128


def _split_heads(src_ref, dst_ref):
    _, _, head0 = _head_masks()

    def block(b, c):
        r0 = pl.multiple_of(b * TQ, TQ)
        d0 = pl.multiple_of(b * 2 * TQ, 2 * TQ)
        for p in range(N_PAIRS):
            x = src_ref[pl.ds(r0, TQ), 128 * p:128 * (p + 1)]
            for h in range(2):
                dst_ref[p, pl.ds(d0 + TQ * h, TQ), :] = _pick_head(x, head0, h)
        return c

    lax.fori_loop(0, N_KB, block, 0)


def attn_fwd(q, k, v):
    def body(q_ref, k_ref, v_ref, o_ref, rc_ref, acc_ref, r_ref, z_ref, spb_ref, ab_ref, qm_ref, vm_ref):
        lane, row, _ = _head_masks()
        w = _cumsum_weights(suffix=True, with_total=True)
        _split_heads(q_ref, qm_ref)
        _split_heads(v_ref, vm_ref)
        acc_ref[...] = jnp.zeros_like(acc_ref)
        r_ref[...] = jnp.zeros_like(r_ref)
        rc_ref[...] = jnp.zeros_like(rc_ref)
        z_ref[...] = jnp.full(z_ref.shape, NO_SCORE, F32)
        spb_ref[...] = jnp.zeros_like(spb_ref)
        ab_ref[...] = jnp.zeros_like(ab_ref)

        def step(pairs):
            (i1, j1), (i2, j2), (i3, j3) = pairs
            k1, q2, q3 = (pl.multiple_of(jnp.maximum(b, 0) * TQ, TQ) for b in (j1, i2, i3))
            q1, k3 = (pl.multiple_of(jnp.maximum(b, 0) * 2 * TQ, 2 * TQ) for b in (i1, j3))
            bias1 = _score_bias(lane, row, i1, j1)
            first2 = j2 == i2
            rc_rows = rc_ref[pl.ds(q2, TQ), :]
            for p in range(N_PAIRS):
                cols = slice(128 * p, 128 * (p + 1))
                kb = k_ref[pl.ds(k1, TQ), cols]
                acc_ref[pl.ds(q3, TQ), cols] += _dot(ab_ref[p], vm_ref[p, pl.ds(k3, 2 * TQ), :])
                for h in range(2):
                    hh = 2 * p + h
                    r = _dot(spb_ref[hh], w)
                    r_in = jnp.where(first2, 0.0, r_ref[hh])
                    ab_ref[p, :, 128 * h:128 * (h + 1)] = jnp.exp(z_ref[hh] - (r[:, :128] + r_in)).astype(MM)
                    rc_rows = jnp.where(jnp.logical_and(lane == 16 * hh + j2, i2 >= 0), r_in, rc_rows)
                    r_ref[hh] = r_in + r[:, 128:]
                    z = _dot_nt(qm_ref[p, pl.ds(q1 + TQ * h, TQ), :], kb) + bias1
                    z_ref[hh] = z
                    spb_ref[hh] = _softplus(z).astype(MM)
            rc_ref[pl.ds(q2, TQ), :] = rc_rows

        _block_pipeline(3, True, step)
        o_ref[...] = acc_ref[...].astype(MM)

    return pl.pallas_call(
        body, name="attn_fwd",
        out_shape=[jax.ShapeDtypeStruct((SEQ, ATT_DIM), MM), jax.ShapeDtypeStruct((SEQ, 128), F32)],
        in_specs=[VMEM_SPEC] * 3, out_specs=[VMEM_SPEC] * 2,
        scratch_shapes=[pltpu.VMEM((SEQ, ATT_DIM), F32), pltpu.VMEM((8, TQ, 128), F32),
                        pltpu.VMEM((8, TQ, 128), F32), pltpu.VMEM((8, TQ, 128), MM),
                        pltpu.VMEM((N_PAIRS, TQ, 256), MM), pltpu.VMEM((N_PAIRS, 2 * SEQ, 128), MM),
                        pltpu.VMEM((N_PAIRS, 2 * SEQ, 128), MM)],
        compiler_params=_params(),
    )(q, k, v)


def mix_fwd(u3, att, gc, ga, x, w_cb_g, b_cb, w_ab_g, w_out_g, g2, g3, after):
    def body(u_ref, a_ref, gc_ref, ga_ref, x_ref, wcb_ref, bcb_ref, wab_ref, wout_ref, g2_ref, g3_ref, after_ref,
             co_ref, ao_ref, mg_ref, mix_ref, x2_ref, h2_ref):
        u = u_ref[...]
        a = a_ref[...]
        co = jnp.concatenate([_dot(u, wcb_ref[j]) for j in range(N_CHIPS)], axis=1) + bcb_ref[...]
        ao = jnp.concatenate([_dot(a, wab_ref[j]) for j in range(N_CHIPS)], axis=1)
        co_ref[...] = co.astype(MM)
        ao_ref[...] = ao.astype(MM)
        merged = (_sigmoid(gc_ref[...]) * co + _sigmoid(ga_ref[...]) * ao).astype(MM)
        mg_ref[...] = merged
        mix = _dot(merged, wout_ref[...])
        mix_ref[...] = mix
        n2, _ = _rms(mix)
        x2 = x_ref[...] + n2 * g2_ref[...]
        x2_ref[...] = x2
        n3, _ = _rms(x2)
        h2_ref[...] = (n3 * g3_ref[...]).astype(MM)

    out_shape = [
        jax.ShapeDtypeStruct((SEQ, D_MODEL), MM), jax.ShapeDtypeStruct((SEQ, D_MODEL), MM),
        jax.ShapeDtypeStruct((SEQ, D_MODEL), MM), jax.ShapeDtypeStruct((SEQ, D_MODEL), F32),
        jax.ShapeDtypeStruct((SEQ, D_MODEL), F32), jax.ShapeDtypeStruct((SEQ, D_MODEL), MM),
    ]
    vec = _full_spec((1, D_MODEL))
    return pl.pallas_call(
        body, name="mix_fwd", grid=(SEQ // TM,), out_shape=out_shape,
        in_specs=[_row_tile_spec(CONV_DIM), _row_tile_spec(ATT_DIM), _row_tile_spec(D_MODEL),
                  _row_tile_spec(D_MODEL), _row_tile_spec(D_MODEL), _weight_spec(w_cb_g.shape), vec,
                  _weight_spec(w_ab_g.shape), _weight_spec(w_out_g.shape), vec, vec, TOKEN_SPEC],
        out_specs=[_row_tile_spec(D_MODEL)] * 6,
        compiler_params=_params(("arbitrary",)),
    )(u3, att, gc, ga, x, w_cb_g, b_cb, w_ab_g, w_out_g, g2, g3, after)


def ffn_up_fwd(h2, w_up_g):
    def body(h_ref, wg_ref, wu_ref, gate_ref, up_ref, act_ref):
        h = h_ref[...]
        gate = _dot(h, wg_ref[0])
        up = _dot(h, wu_ref[0])
        gate_ref[...] = gate.astype(MM)
        up_ref[...] = up.astype(MM)
        act_ref[...] = (gate * _sigmoid(gate) * up).astype(MM)

    tile = pl.BlockSpec((TM, UP_SHARD), lambda n, i: (i, n))
    act = jax.ShapeDtypeStruct((SEQ, D_FF), MM)
    return pl.pallas_call(
        body, name="ffn_up_fwd", grid=(2, SEQ // TM), out_shape=[act, act, act],
        in_specs=[pl.BlockSpec((TM, D_MODEL), lambda n, i: (i, 0)),
                  pl.BlockSpec((1, D_MODEL, UP_SHARD), lambda n, i: (n, 0, 0)),
                  pl.BlockSpec((1, D_MODEL, UP_SHARD), lambda n, i: (n + 2, 0, 0))],
        out_specs=[tile, tile, tile],
        compiler_params=_params(("arbitrary", "arbitrary")),
    )(h2, w_up_g, w_up_g)


def ffn_down_loss(act, w_down_g, x2, target, g4):
    def body(act_ref, wd_hbm, x2_ref, t_ref, g_ref, dff_ref, dy_ref, loss_ref, dg_ref, wd_ref, sems):
        def run(wait_half):
            ff = None
            for h in range(2):
                wait_half(h)
                t = _dot(act_ref[:, h * UP_SHARD:(h + 1) * UP_SHARD], wd_ref[h])
                ff = t if ff is None else ff + t
            n4, r4 = _rms(ff)
            g4v = g_ref[...]
            err = x2_ref[...] + n4 * g4v - t_ref[...]
            row_loss = jnp.mean(err * err, axis=-1, keepdims=True)
            loss_ref[...] = jnp.zeros((8, 128), F32) + 0.5 * jnp.sum(row_loss, axis=0, keepdims=True)
            dy = err * (1.0 / D_MODEL)
            dy_ref[...] = dy
            dff_ref[...] = _rms_bwd(dy * g4v, n4, r4).astype(MM)
            _acc_rows(dg_ref, jnp.sum(dy * n4, axis=0, keepdims=True), pl.program_id(0) == 0)

        _load_once(wd_hbm, wd_ref, sems, run)

    nt = SEQ // TM
    vec = _full_spec((1, D_MODEL))
    return pl.pallas_call(
        body, name="ffn_down_loss", grid=(nt,),
        out_shape=(jax.ShapeDtypeStruct((SEQ, D_MODEL), MM), jax.ShapeDtypeStruct((SEQ, D_MODEL), F32),
                   jax.ShapeDtypeStruct((nt * 8, 128), F32), jax.ShapeDtypeStruct((1, D_MODEL), F32)),
        in_specs=[_row_tile_spec(D_FF), ANY, _row_tile_spec(D_MODEL), _row_tile_spec(D_MODEL), vec],
        out_specs=[_row_tile_spec(D_MODEL), _row_tile_spec(D_MODEL),
                   pl.BlockSpec((8, 128), lambda i: (i, 0)), vec],
        scratch_shapes=_load_once_scratch(w_down_g),
        compiler_params=_params(("arbitrary",)),
    )(act, w_down_g, x2, target, g4)


def ffn_act_bwd(dff, w_down_g, gate, up):
    def body(dff_ref, wd_hbm, gate_ref, up_ref, dgu_ref, wd_ref, sems):
        def run(wait_half):
            for h in range(2):
                wait_half(h)
                cols = slice(h * UP_SHARD, (h + 1) * UP_SHARD)
                dact = _dot_nt(dff_ref[...], wd_ref[h])
                gate = gate_ref[:, cols].astype(F32)
                sg = _sigmoid(gate)
                dgu_ref[:, cols] = (dact * up_ref[:, cols].astype(F32) * (sg * (1.0 + gate * (1.0 - sg)))).astype(MM)
                dgu_ref[:, D_FF + h * UP_SHARD:D_FF + (h + 1) * UP_SHARD] = (dact * (gate * sg)).astype(MM)

        _load_once(wd_hbm, wd_ref, sems, run)

    return pl.pallas_call(
        body, name="ffn_act_bwd", grid=(SEQ // TM,),
        out_shape=jax.ShapeDtypeStruct((SEQ, 2 * D_FF), MM),
        in_specs=[_row_tile_spec(D_MODEL), ANY, _row_tile_spec(D_FF), _row_tile_spec(D_FF)],
        out_specs=_row_tile_spec(2 * D_FF),
        scratch_shapes=_load_once_scratch(w_down_g),
        compiler_params=_params(("arbitrary",)),
    )(dff, w_down_g, gate, up)


def ffn_in_bwd(dgu, w_up_g, x2, mix, dy, g3, g2):
    def body(dgu_ref, w_hbm, x2_ref, mix_ref, dy_ref, g3_ref, g2_ref, dx2_ref, dmix_ref, dg3_ref, dg2_ref,
             w_ref, sems):
        def run(wait_shard):
            dh2 = None
            for j in range(N_CHIPS):
                wait_shard(j)
                t = _dot_nt(dgu_ref[:, j * UP_SHARD:(j + 1) * UP_SHARD], w_ref[j])
                dh2 = t if dh2 is None else dh2 + t
            first = pl.program_id(0) == 0
            n3, r3 = _rms(x2_ref[...])
            dx2 = dy_ref[...] + _rms_bwd(dh2 * g3_ref[...], n3, r3)
            dx2_ref[...] = dx2
            _acc_rows(dg3_ref, jnp.sum(dh2 * n3, axis=0, keepdims=True), first)
            n2, r2 = _rms(mix_ref[...])
            dmix_ref[...] = _rms_bwd(dx2 * g2_ref[...], n2, r2).astype(MM)
            _acc_rows(dg2_ref, jnp.sum(dx2 * n2, axis=0, keepdims=True), first)

        _load_once(w_hbm, w_ref, sems, run)

    vec = _full_spec((1, D_MODEL))
    return pl.pallas_call(
        body, name="ffn_in_bwd", grid=(SEQ // TM,),
        out_shape=(jax.ShapeDtypeStruct((SEQ, D_MODEL), F32), jax.ShapeDtypeStruct((SEQ, D_MODEL), MM),
                   jax.ShapeDtypeStruct((1, D_MODEL), F32), jax.ShapeDtypeStruct((1, D_MODEL), F32)),
        in_specs=[_row_tile_spec(2 * D_FF), ANY, _row_tile_spec(D_MODEL),
                  _row_tile_spec(D_MODEL), _row_tile_spec(D_MODEL), vec, vec],
        out_specs=[_row_tile_spec(D_MODEL), _row_tile_spec(D_MODEL), vec, vec],
        scratch_shapes=_load_once_scratch(w_up_g),
        compiler_params=_params(("arbitrary",)),
    )(dgu, w_up_g, x2, mix, dy, g3, g2)


def merge_bwd(dmix, w_out_g, gc, ga, co, ao, w_cb_g, w_ab_g, after):
    def body(dmix_ref, wout_ref, gc_ref, ga_ref, co_ref, ao_ref, wcb_ref, wab_ref, after_ref,
             dco_ref, dao_ref, dg_ref, du3_ref, datt_ref, dbcb_ref):
        dm = _dot_nt(dmix_ref[...], wout_ref[...])
        sgc = _sigmoid(gc_ref[...])
        sga = _sigmoid(ga_ref[...])
        dco = dm * sgc
        dao = dm * sga
        dg_ref[:, 0:D_MODEL] = (dm * co_ref[...].astype(F32) * (sgc * (1.0 - sgc))).astype(MM)
        dg_ref[:, D_MODEL:2 * D_MODEL] = (dm * ao_ref[...].astype(F32) * (sga * (1.0 - sga))).astype(MM)
        _acc_rows(dbcb_ref, jnp.sum(dco, axis=0, keepdims=True), pl.program_id(0) == 0)
        dco_ref[...] = dco.astype(MM)
        dao_ref[...] = dao.astype(MM)
        du3 = None
        datt = None
        for j in range(N_CHIPS):
            cols = slice(j * BR_SHARD, (j + 1) * BR_SHARD)
            t = _dot_nt(dco_ref[:, cols], wcb_ref[j])
            s = _dot_nt(dao_ref[:, cols], wab_ref[j])
            du3 = t if du3 is None else du3 + t
            datt = s if datt is None else datt + s
        du3_ref[...] = du3
        datt_ref[...] = datt.astype(MM)

    wide = _row_tile_spec(D_MODEL)
    return pl.pallas_call(
        body, name="merge_bwd", grid=(SEQ // TM,),
        out_shape=(jax.ShapeDtypeStruct((SEQ, D_MODEL), MM), jax.ShapeDtypeStruct((SEQ, D_MODEL), MM),
                   jax.ShapeDtypeStruct((SEQ, 2 * D_MODEL), MM),
                   jax.ShapeDtypeStruct((SEQ, CONV_DIM), F32), jax.ShapeDtypeStruct((SEQ, ATT_DIM), MM),
                   jax.ShapeDtypeStruct((1, D_MODEL), F32)),
        in_specs=[wide, _weight_spec(w_out_g.shape), wide, wide, wide, wide,
                  _weight_spec(w_cb_g.shape), _weight_spec(w_ab_g.shape), TOKEN_SPEC],
        out_specs=[wide, wide, _row_tile_spec(2 * D_MODEL), _row_tile_spec(CONV_DIM), _row_tile_spec(ATT_DIM),
                   _full_spec((1, D_MODEL))],
        compiler_params=_params(("arbitrary",)),
    )(dmix, w_out_g, gc, ga, co, ao, w_cb_g, w_ab_g, after)


def conv_bwd(du3, u1, ci, w_dw, ln_g, ln_b, after):
    def body(du3_ref, u1_ref, ci_ref, w_ref, g_ref, bb_ref, after_ref,
             dci_ref, dw_ref, dbdw_ref, dg_ref, db_ref, upad_ref, dpad_ref, dwacc_ref, vacc_ref):
        _glu_into(ci_ref, upad_ref)
        dpad_ref[SEQ:SEQ + 32, :] = jnp.zeros((32, CONV_DIM), F32)
        dwacc_ref[...] = jnp.zeros_like(dwacc_ref)
        vacc_ref[...] = jnp.zeros_like(vacc_ref)

        def fold8(t):
            s = t[0:8, :]
            for r in range(1, CONV_TILE // 8):
                s = s + t[8 * r:8 * r + 8, :]
            return s

        def pass1(i, c):
            t0 = pl.multiple_of(i * CONV_TILE, CONV_TILE)
            xh, rstd = _layernorm_parts(u1_ref[pl.ds(t0, CONV_TILE), :])
            gv = g_ref[...]
            u2 = xh * gv + bb_ref[...]
            s2 = _sigmoid(u2)
            du2 = du3_ref[pl.ds(t0, CONV_TILE), :] * (s2 * (1.0 + u2 * (1.0 - s2)))
            wv = du2 * gv
            du1 = rstd * (wv - jnp.mean(wv, axis=-1, keepdims=True)
                          - xh * jnp.mean(wv * xh, axis=-1, keepdims=True))
            dpad_ref[pl.ds(t0, CONV_TILE), :] = du1
            vacc_ref[0] += fold8(du2 * xh)
            vacc_ref[1] += fold8(du2)
            vacc_ref[2] += fold8(du1)
            win = upad_ref[pl.ds(t0, CONV_WIN), :]
            n = win.shape[0]
            for rot in range(8):
                shifted = win if rot == 0 else pltpu.roll(win, n - rot, 0)
                for a in range(5):
                    j = 8 * a + rot - 2
                    if 0 <= j < CONV_WIDTH:
                        dwacc_ref[j] += fold8(du1 * shifted[8 * a:8 * a + CONV_TILE, :])
            return c

        lax.fori_loop(0, SEQ // CONV_TILE, pass1, 0)

        def pass2(i, c):
            t0 = pl.multiple_of(i * CONV_TILE, CONV_TILE)
            win = dpad_ref[pl.ds(t0, CONV_WIN), :]
            du0 = _shifted_sum(win, [(30 - j, w_ref[j:j + 1, :]) for j in range(CONV_WIDTH)])
            a = ci_ref[pl.ds(t0, CONV_TILE), 0:CONV_DIM]
            sb = _sigmoid(ci_ref[pl.ds(t0, CONV_TILE), CONV_DIM:2 * CONV_DIM])
            dci_ref[pl.ds(t0, CONV_TILE), 0:CONV_DIM] = (du0 * sb).astype(MM)
            dci_ref[pl.ds(t0, CONV_TILE), CONV_DIM:2 * CONV_DIM] = (du0 * a * (sb * (1.0 - sb))).astype(MM)
            return c

        lax.fori_loop(0, SEQ // CONV_TILE, pass2, 0)

        for j in range(CONV_WIDTH):
            dw_ref[j:j + 1, :] = jnp.sum(dwacc_ref[j], axis=0, keepdims=True)
        dw_ref[CONV_WIDTH:32, :] = jnp.zeros((32 - CONV_WIDTH, CONV_DIM), F32)
        dg_ref[...] = jnp.sum(vacc_ref[0], axis=0, keepdims=True)
        db_ref[...] = jnp.sum(vacc_ref[1], axis=0, keepdims=True)
        dbdw_ref[...] = jnp.sum(vacc_ref[2], axis=0, keepdims=True)

    vec = jax.ShapeDtypeStruct((1, CONV_DIM), F32)
    return pl.pallas_call(
        body, name="conv_bwd",
        out_shape=(jax.ShapeDtypeStruct((SEQ, 2 * CONV_DIM), MM), jax.ShapeDtypeStruct((32, CONV_DIM), F32),
                   vec, vec, vec),
        in_specs=[VMEM_SPEC] * 7, out_specs=[VMEM_SPEC] * 5,
        scratch_shapes=[pltpu.VMEM((SEQ + 32, CONV_DIM), F32), pltpu.VMEM((SEQ + 32, CONV_DIM), F32),
                        pltpu.VMEM((CONV_WIDTH, 8, CONV_DIM), F32), pltpu.VMEM((3, 8, CONV_DIM), F32)],
        compiler_params=_params(),
    )(du3, u1, ci, w_dw, ln_g, ln_b, after)


def attn_bwd(q, k, v, datt, rc, after):
    def body(q_ref, k_ref, v_ref, do_ref, rc_ref, after_ref, dqkv_ref, dqa_ref, dka_ref, dva_ref, pc_ref, z_ref,
             sig1_ref, sig2_ref, g_ref, spb_ref, gb_ref, ar_ref, dzr_ref, dzc_ref, qm_ref, km_ref, dom_ref):
        lane, row, _ = _head_masks()
        _split_heads(q_ref, qm_ref)
        _split_heads(k_ref, km_ref)
        _split_heads(do_ref, dom_ref)
        for ref in (dqa_ref, dka_ref, dva_ref, pc_ref):
            ref[...] = jnp.zeros_like(ref)
        z_ref[...] = jnp.full(z_ref.shape, NO_SCORE, F32)
        for ref in (sig1_ref, sig2_ref, spb_ref, ar_ref, g_ref, gb_ref, dzr_ref, dzc_ref):
            ref[...] = jnp.zeros_like(ref)
        w_suffix = _cumsum_weights(suffix=True, with_total=False)
        w_prefix = _cumsum_weights(suffix=False, with_total=True)

        def step(pairs):
            (ia, ja), (ib, jb), (ic, jc), (id_, jd) = pairs
            ka, qb_, kb_, kc, qd, kd = (pl.multiple_of(jnp.maximum(b, 0) * TQ, TQ) for b in (ja, ib, jb, jc, id_, jd))
            qa2, qb2, qc2, qd2, kd2 = (pl.multiple_of(jnp.maximum(b, 0) * 2 * TQ, 2 * TQ)
                                       for b in (ia, ib, ic, id_, jd))
            bias_a = _score_bias(lane, row, ia, ja)
            rc_rows = rc_ref[pl.ds(qb_, TQ), :]
            first_c = jc == 0
            for p in range(N_PAIRS):
                cols = slice(128 * p, 128 * (p + 1))
                k_a = k_ref[pl.ds(ka, TQ), cols]
                v_b = v_ref[pl.ds(kb_, TQ), cols]
                dqa_ref[pl.ds(qd, TQ), cols] += _dot(dzc_ref[p], km_ref[p, pl.ds(kd2, 2 * TQ), :])
                dka_ref[pl.ds(kd, TQ), cols] += _dot_tn(dzr_ref[p], qm_ref[p, pl.ds(qd2, 2 * TQ), :])
                dva_ref[pl.ds(kc, TQ), cols] += _dot_tn(ar_ref[p], dom_ref[p, pl.ds(qc2, 2 * TQ), :])
                for h in range(2):
                    hh = 2 * p + h
                    rows = slice(TQ * h, TQ * (h + 1))
                    r = _dot(gb_ref[hh], w_prefix)
                    p_in = jnp.where(first_c, 0.0, pc_ref[hh])
                    dz = (g_ref[hh] - sig2_ref[hh] * (r[:, :128] + p_in)).astype(MM)
                    dzc_ref[p, :, rows] = dz
                    dzr_ref[p, rows, :] = dz
                    pc_ref[hh] = p_in + r[:, 128:]
                    r_in = jnp.sum(jnp.where(lane == 16 * hh + jb, rc_rows, 0.0), axis=1, keepdims=True)
                    a = jnp.exp(z_ref[hh] - (_dot(spb_ref[hh], w_suffix) + r_in))
                    g = _dot_nt(dom_ref[p, pl.ds(qb2 + TQ * h, TQ), :], v_b) * a
                    ar_ref[p, rows, :] = a.astype(MM)
                    g_ref[hh] = g
                    gb_ref[hh] = g.astype(MM)
                    sig2_ref[hh] = sig1_ref[hh]
                    z = _dot_nt(qm_ref[p, pl.ds(qa2 + TQ * h, TQ), :], k_a) + bias_a
                    sp = _softplus(z)
                    sig1_ref[hh] = jnp.exp(z - sp)
                    z_ref[hh] = z
                    spb_ref[hh] = sp.astype(MM)

        _block_pipeline(4, False, step)
        dqkv_ref[:, 0:ATT_DIM] = (dqa_ref[...] * ATT_SCALE).astype(MM)
        dqkv_ref[:, ATT_DIM:2 * ATT_DIM] = dka_ref[...].astype(MM)
        dqkv_ref[:, 2 * ATT_DIM:3 * ATT_DIM] = dva_ref[...].astype(MM)

    split = pltpu.VMEM((N_PAIRS, 2 * SEQ, 128), MM)
    return pl.pallas_call(
        body, name="attn_bwd", out_shape=jax.ShapeDtypeStruct((SEQ, 3 * ATT_DIM), MM),
        in_specs=[VMEM_SPEC] * 6, out_specs=VMEM_SPEC,
        scratch_shapes=[pltpu.VMEM((SEQ, ATT_DIM), F32)] * 3 + [pltpu.VMEM((8, TQ, 128), F32)] * 5
                       + [pltpu.VMEM((8, TQ, 128), MM)] * 2
                       + [pltpu.VMEM((N_PAIRS, 2 * TQ, 128), MM)] * 2 + [pltpu.VMEM((N_PAIRS, TQ, 256), MM)]
                       + [split] * 3,
        compiler_params=_params(),
    )(q, k, v, datt, rc, after)


DPROJ_PIECES = ((0, 1024), (1024, 2560), (2560, 4608))


def _dproj_segments(j):
    g0, g1 = j * IN_SHARD, (j + 1) * IN_SHARD
    segs = []
    for p, (s, e) in enumerate(DPROJ_PIECES):
        lo, hi = max(s, g0), min(e, g1)
        if lo < hi:
            segs.append((p, lo - s, lo - g0, hi - lo))
    return segs


def in_proj_bwd(pieces, w_in_g, x, dx2, g1, after):
    def body(p0_ref, p1_ref, p2_ref, w_hbm, x_ref, dx2_ref, g_ref, after_ref, dx_ref, dg_ref, w_ref, sems):
        p_refs = (p0_ref, p1_ref, p2_ref)

        def run(wait_shard):
            dh = None
            for j in range(N_CHIPS):
                wait_shard(j)
                for p, lo, off, width in _dproj_segments(j):
                    t = _dot_nt(p_refs[p][:, lo:lo + width], w_ref[j, :, off:off + width])
                    dh = t if dh is None else dh + t
            n1, r1 = _rms(x_ref[...])
            dx_ref[...] = dx2_ref[...] + _rms_bwd(dh * g_ref[...], n1, r1)
            _acc_rows(dg_ref, jnp.sum(dh * n1, axis=0, keepdims=True), pl.program_id(0) == 0)

        _load_once(w_hbm, w_ref, sems, run)

    vec = _full_spec((1, D_MODEL))
    return pl.pallas_call(
        body, name="in_proj_bwd", grid=(SEQ // TM,),
        out_shape=[jax.ShapeDtypeStruct((SEQ, D_MODEL), F32), jax.ShapeDtypeStruct((1, D_MODEL), F32)],
        in_specs=[_row_tile_spec(p.shape[1]) for p in pieces]
                 + [ANY, _row_tile_spec(D_MODEL), _row_tile_spec(D_MODEL), vec, TOKEN_SPEC],
        out_specs=[_row_tile_spec(D_MODEL), vec],
        scratch_shapes=_load_once_scratch(w_in_g),
        compiler_params=_params(("arbitrary",)),
    )(*pieces, w_in_g, x, dx2, g1, after)


def weight_grad_in(h1, pieces):
    kh = D_MODEL // 2

    def body(a_ref, p0_ref, p1_ref, p2_ref, o_ref):
        p_refs = (p0_ref, p1_ref, p2_ref)
        a = a_ref[...]
        for j in range(N_CHIPS):
            @pl.when(pl.program_id(1) == j)
            def _():
                for p, lo, off, width in _dproj_segments(j):
                    o_ref[0, 0, :, off:off + width] = _dot_tn(a, p_refs[p][:, lo:lo + width]).astype(MM)

    return pl.pallas_call(
        body, name="dw_in", grid=(2, N_CHIPS), out_shape=jax.ShapeDtypeStruct((N_CHIPS, 2, kh, IN_SHARD), MM),
        in_specs=[pl.BlockSpec((SEQ, kh), lambda h, j: (0, h))] + [_weight_spec(p.shape) for p in pieces],
        out_specs=pl.BlockSpec((1, 1, kh, IN_SHARD), lambda h, j: (j, h, 0, 0)),
        compiler_params=_params(("arbitrary", "arbitrary")),
    )(h1, *pieces)


def weight_grad(a, b, name, col_sharded, tk=None):
    kin, n = a.shape[1], b.shape[1]

    def body(a_ref, b_ref, o_ref):
        if col_sharded:
            o_ref[0, 0] = _dot_tn(a_ref[...], b_ref[...]).astype(MM)
        else:
            o_ref[...] = _dot_tn(a_ref[...], b_ref[...]).astype(MM)

    if col_sharded:
        kh, ns = kin // 2, n // N_CHIPS
        out = jax.ShapeDtypeStruct((N_CHIPS, 2, kh, ns), MM)
        grid = (2, N_CHIPS)
        in_specs = [pl.BlockSpec((SEQ, kh), lambda h, j: (0, h)), pl.BlockSpec((SEQ, ns), lambda h, j: (0, j))]
        out_spec = pl.BlockSpec((1, 1, kh, ns), lambda h, j: (j, h, 0, 0))
        sem = ("arbitrary", "arbitrary")
    else:
        out = jax.ShapeDtypeStruct((kin, n), MM)
        grid = (kin // tk,)
        in_specs = [pl.BlockSpec((SEQ, tk), lambda r: (0, r)), pl.BlockSpec((SEQ, n), lambda r: (0, 0))]
        out_spec = pl.BlockSpec((tk, n), lambda r: (r, 0))
        sem = ("arbitrary",)
    res = pl.pallas_call(
        body, name=name, grid=grid, out_shape=out, in_specs=in_specs, out_specs=out_spec,
        compiler_params=_params(sem),
    )(a, b)
    if not col_sharded:
        res = res.reshape(N_CHIPS, 2, kin // (2 * N_CHIPS), n)
    return res


def weight_grad_mix(merged, dmix, u3, dco, att, dao):
    operands = (merged, dmix, u3, dco, att, dao)
    n_out, n_br = D_MODEL // 2, CONV_DIM // 2

    def body(*refs):
        hbm, (o_out, o_cb, o_ab), bufs, sems = refs[:6], refs[6:9], refs[9:15], refs[15]
        copies = [pltpu.make_async_copy(hbm[i], bufs[i], sems.at[i]) for i in range(6)]
        for cp in copies:
            cp.start()
        m_ref, dm_ref, u_ref, dco_ref, a_ref, dao_ref = bufs
        copies[0].wait()
        copies[1].wait()
        for h in range(2):
            o_out[h * n_out:(h + 1) * n_out, :] = _dot_tn(m_ref[:, h * n_out:(h + 1) * n_out], dm_ref[...]).astype(MM)
        for br, (a, d, o) in enumerate(((u_ref, dco_ref, o_cb), (a_ref, dao_ref, o_ab))):
            copies[2 + 2 * br].wait()
            copies[3 + 2 * br].wait()
            for h in range(2):
                g = _dot_tn(a[:, h * n_br:(h + 1) * n_br], d[...])
                for j in range(N_CHIPS):
                    o[j, h] = g[:, j * BR_SHARD:(j + 1) * BR_SHARD].astype(MM)

    branch = jax.ShapeDtypeStruct((N_CHIPS, 2, n_br, BR_SHARD), MM)
    dw_out, dw_cb, dw_ab = pl.pallas_call(
        body, name="dw_mix", out_shape=[jax.ShapeDtypeStruct((D_MODEL, D_MODEL), MM), branch, branch],
        in_specs=[ANY] * 6, out_specs=[VMEM_SPEC] * 3,
        scratch_shapes=[pltpu.VMEM(a.shape, a.dtype) for a in operands] + [pltpu.SemaphoreType.DMA((6,))],
        compiler_params=_params(),
    )(*operands)
    return dw_out.reshape(N_CHIPS, 2, D_MODEL // (2 * N_CHIPS), D_MODEL), dw_cb, dw_ab


def _place():
    x, y, c = lax.axis_index("x"), lax.axis_index("y"), lax.axis_index("c")
    chips = [(1 - x, y), (x, 1 - y), (1 - x, 1 - y)]
    return x, y, c, chips


def _rcopy(src, dst, send_sem, recv_sem, dev):
    return pltpu.make_async_remote_copy(src_ref=src, dst_ref=dst, send_sem=send_sem, recv_sem=recv_sem,
                                        device_id=dev, device_id_type=MESH)


class _Gather:
    N_MOVES = 6

    def __init__(self, shapes, w, o, scratch):
        self.n, self.shapes, self.w, self.o = len(w), shapes, w, o
        self.send, self.recv, self.psend, self.precv, self.loc_in, self.loc_out = scratch[:6]
        self.raw, self.stage = scratch[6:6 + self.n], scratch[6 + self.n:]
        x, y, c, self.chips = _place()
        self.c = c
        self.me, k_x, k_y, k_far = 2 * x + y, 2 * (1 - x) + y, 2 * x + (1 - y), 2 * (1 - x) + (1 - y)
        to_x, to_y = (1 - x, y, c), (x, 1 - y, c)
        self.sib = (x, y, 1 - c)
        self.sent_as = [(self.me, 0, to_x), (self.me, 1, to_y), (self.me, 1, to_x), (self.me, 0, to_y),
                        (k_x, 0, to_y), (k_y, 1, to_x)]
        self.arrives_as = [(k_x, 0, to_x), (k_y, 1, to_y), (k_x, 1, to_x), (k_y, 0, to_y),
                           (k_far, 0, to_y), (k_far, 1, to_x)]
        self.sent_on_after = {0: 4, 1: 5}

    @staticmethod
    def scratch(shards):
        n = len(shards)
        sems = pltpu.SemaphoreType.DMA
        m = _Gather.N_MOVES * n
        return ([sems((m,)), sems((m,)), sems((m,)), sems((m,)), sems((n,)), sems((n,))]
                + [pltpu.VMEM(s.shape, s.dtype) for s in shards] + [pltpu.VMEM(s.shape, MM) for s in shards])

    @staticmethod
    def out_shapes(shards):
        return [jax.ShapeDtypeStruct((N_CHIPS,) + s.shape, MM) for s in shards]

    def _rows(self, t, quarter, cc):
        rq = self.shapes[t][0] // 4
        return pl.ds((2 * cc + quarter) * rq, rq)

    def _chip(self, j):
        cx, cy = self.chips[j]
        return 2 * cx + cy, (cx, cy, self.c)

    def local_in(self, t):
        return pltpu.make_async_copy(self.w[t], self.raw[t], self.loc_in.at[t])

    def local_out(self, t):
        return pltpu.make_async_copy(self.stage[t], self.o[t].at[self.me], self.loc_out.at[t])

    def sent(self, i, t):
        k, quarter, dev = self.sent_as[i]
        rows = self._rows(t, quarter, self.c)
        there = self.o[t].at[k, rows, :]
        return _rcopy(self.stage[t].at[rows, :] if i < 4 else there, there,
                      self.send.at[i * self.n + t], self.recv.at[i * self.n + t], dev)

    def arrived(self, i, t):
        k, quarter, dev = self.arrives_as[i]
        blk = self.o[t].at[k, self._rows(t, quarter, self.c), :]
        return _rcopy(blk, blk, self.send.at[i * self.n + t], self.recv.at[i * self.n + t], dev)

    def passed(self, i, t, cc):
        k, quarter, _ = self.arrives_as[i]
        blk = self.o[t].at[k, self._rows(t, quarter, cc), :]
        return _rcopy(blk, blk, self.psend.at[i * self.n + t], self.precv.at[i * self.n + t], self.sib)

    def start(self):
        for t in range(self.n):
            self.local_in(t).start()
        for t in range(self.n):
            self.local_in(t).wait()
            self.stage[t][...] = self.raw[t][...].astype(MM)
            self.local_out(t).start()
        for i in range(4):
            for t in range(self.n):
                self.sent(i, t).start()

    def forward(self):
        for i in range(self.N_MOVES):
            for t in range(self.n):
                self.arrived(i, t).wait_recv()
                if i in self.sent_on_after:
                    self.sent(self.sent_on_after[i], t).start()
                self.passed(i, t, self.c).start()

    def finish(self):
        for i in range(self.N_MOVES):
            for t in range(self.n):
                self.passed(i, t, 1 - self.c).wait_recv()
        for i in range(self.N_MOVES):
            for t in range(self.n):
                self.sent(i, t).wait_send()
                self.passed(i, t, self.c).wait_send()
        for t in range(self.n):
            self.local_out(t).wait()


def all_gather_weights(shards, small, later):
    n, m = len(shards), len(later)
    shapes = [s.shape for s in shards]

    def body(*refs):
        w = refs[:n]
        sm = refs[n]
        lw = refs[n + 1:n + 1 + m]
        o = refs[n + 1 + m:2 * n + 1 + m]
        osm = refs[2 * n + 1 + m]
        lo = refs[2 * n + 2 + m:2 * n + 2 + 2 * m]
        scratch = refs[2 * n + 2 + 2 * m:]
        ssend, srecv, sloc, lsem_in, lsem_out = scratch[:5]
        lraw, lstage = scratch[5:5 + m], scratch[5 + m:5 + 2 * m]
        g = _Gather(shapes, w, o, scratch[5 + 2 * m:])
        own = pltpu.make_async_copy(sm, osm.at[g.me], sloc)
        own.start()
        loads = [pltpu.make_async_copy(lw[t], lraw[t], lsem_in.at[t]) for t in range(m)]
        for cp in loads:
            cp.start()
        g.start()
        small_cps = [_rcopy(sm, osm.at[g.me], ssend.at[j], srecv.at[j], g._chip(j)[1]) for j in range(3)]
        for cp in small_cps:
            cp.start()
        places = []
        for t in range(m):
            loads[t].wait()
            lstage[t][...] = lraw[t][...].astype(MM)
            places.append(pltpu.make_async_copy(lstage[t], lo[t].at[g.me], lsem_out.at[t]))
            places[t].start()
        g.forward()
        g.finish()
        for j in range(3):
            k, dev = g._chip(j)
            _rcopy(sm, osm.at[k], ssend.at[j], srecv.at[j], dev).wait_recv()
            small_cps[j].wait_send()
        own.wait()
        for cp in places:
            cp.wait()

    out_shape = _Gather.out_shapes(shards)
    out_shape.append(jax.ShapeDtypeStruct((N_CHIPS,) + small.shape, small.dtype))
    out_shape += _Gather.out_shapes(later)
    sems = pltpu.SemaphoreType.DMA
    return pl.pallas_call(
        body, name="all_gather_weights", out_shape=out_shape,
        in_specs=[ANY] * (n + 1 + m), out_specs=[ANY] * (n + 1 + m),
        scratch_shapes=[sems((3,)), sems((3,)), sems, sems((m,)), sems((m,))]
                       + [pltpu.VMEM(s.shape, s.dtype) for s in later] + [pltpu.VMEM(s.shape, MM) for s in later]
                       + _Gather.scratch(shards),
        compiler_params=_params(),
    )(*shards, small, *later)


HBM_SPEC = pl.BlockSpec(memory_space=pltpu.HBM)
SEM_SPEC = pl.BlockSpec(memory_space=pltpu.SEMAPHORE)
DATAFLOW = pltpu.SideEffectType.DATAFLOW_SIDE_EFFECTING


def split_start(name, bufs, n_copies, copies):
    nb = len(bufs)

    def body(*refs):
        for cp in copies(refs[:nb], refs[nb], refs[nb + 1]):
            cp.start()
        token = refs[2 * nb + 2]
        token[...] = jnp.zeros_like(token)

    sems = [pltpu.SemaphoreType.DMA((n_copies,))] * 2
    res = pl.pallas_call(
        body, name=name,
        out_shape=sems + [pltpu.HBM(a.shape, a.dtype) for a in bufs] + [jax.ShapeDtypeStruct((8, 128), F32)],
        in_specs=[HBM_SPEC] * nb, out_specs=[SEM_SPEC] * 2 + [HBM_SPEC] * nb + [VMEM_SPEC],
        input_output_aliases={i: 2 + i for i in range(nb)},
        compiler_params=pltpu.CompilerParams(has_side_effects=DATAFLOW),
    )(*[pltpu.with_memory_space_constraint(a, pltpu.HBM) for a in bufs])
    return res[:-1], res[-1]


def split_wait(name, state, after, copies):
    sems, bufs = state[:2], state[2:]
    nb = len(bufs)

    def body(*refs):
        for cp in copies(refs[:nb], refs[nb], refs[nb + 1]):
            cp.wait_send()
            cp.wait_recv()

    return pl.pallas_call(
        body, name=name, out_shape=[pltpu.HBM(a.shape, a.dtype) for a in bufs],
        in_specs=[HBM_SPEC] * nb + [SEM_SPEC] * 2 + [ANY] * len(after), out_specs=[HBM_SPEC] * nb,
        input_output_aliases={i: i for i in range(nb)},
        compiler_params=pltpu.CompilerParams(has_side_effects=DATAFLOW),
    )(*bufs, *sems, *after)


class _Shifted:
    def __init__(self, sems, first):
        self.sems, self.first = sems, first

    @property
    def at(self):
        return self

    def __getitem__(self, i):
        return self.sems.at[self.first + i]


def _scatter_copies(n):
    def copies(refs, send, recv):
        _, _, c, chips = _place()
        return [_rcopy(refs[t].at[2 * cx + cy], refs[n + t].at[j], send.at[3 * t + j], recv.at[3 * t + j], (cx, cy, c))
                for t in range(n) for j, (cx, cy) in enumerate(chips)]
    return copies


def scatter_start(parts, tag):
    lands = [lax.empty((3,) + p.shape[1:], p.dtype) for p in parts]
    return split_start("scatter_start_" + tag, list(parts) + lands, 3 * len(parts), _scatter_copies(len(parts)))


def scatter_wait(state, after, tag):
    n = (len(state) - 2) // 2
    return split_wait("scatter_wait_" + tag, state, after, _scatter_copies(n))[n:]


def _gather_copies(shapes, level):
    n = len(shapes)

    def copies(refs, send, recv):
        x, y, c, chips = _place()
        out = []
        for t in range(n):
            rh = shapes[t][0] // 2
            for j, (cx, cy) in enumerate(chips):
                k, dev = (2 * x + y, (cx, cy, c)) if level == 1 else (2 * cx + cy, (x, y, 1 - c))
                blk = refs[t].at[k, pl.ds(c * rh, rh), :]
                out.append(_rcopy(blk, blk, send.at[3 * t + j], recv.at[3 * t + j], dev))
        return out
    return copies


def _sibling_copies(n, other_half):
    def copies(refs, send, recv):
        x, y, c, _ = _place()
        return [_rcopy(refs[t].at[:, 1 - c] if other_half else refs[t], refs[n + t], send.at[t], recv.at[t],
                       (x, y, 1 - c)) for t in range(n)]
    return copies


def sibling_start(srcs, other_half, tag):
    lands = [lax.empty((a.shape[0],) + a.shape[2:] if other_half else a.shape, a.dtype) for a in srcs]
    return split_start("sibling_start_" + tag, list(srcs) + lands, len(srcs),
                       _sibling_copies(len(srcs), other_half))


def sibling_wait(state, after, other_half, tag):
    n = (len(state) - 2) // 2
    res = split_wait("sibling_wait_" + tag, state, after, _sibling_copies(n, other_half))
    return res[:n], res[n:]


def small_pack(ddw, v512, v1024, loss_parts):
    rows, width = PACK_ROWS, 512
    n512, n1024 = len(VEC512), len(VEC1024)

    def body(*refs):
        ddw_ref = refs[0]
        a_refs = refs[1:1 + n512]
        b_refs = refs[1 + n512:1 + n512 + n1024]
        lp_ref, o_ref, p_ref = refs[1 + n512 + n1024:]
        p_ref[...] = jnp.zeros_like(p_ref)
        p_ref[0:32, :] = ddw_ref[...]
        p_ref[LOSS_ROW:LOSS_ROW + 1, 0:128] = jnp.sum(lp_ref[...], axis=0, keepdims=True) * 0.125
        for i, r in enumerate(a_refs):
            p_ref[32 + i:33 + i, :] = r[...]
        for i, r in enumerate(b_refs):
            base = 32 + n512 + 2 * i
            p_ref[base:base + 1, :] = r[:, 0:512]
            p_ref[base + 1:base + 2, :] = r[:, 512:1024]
        x, y, c, _ = _place()
        o_ref[4 * x + 2 * y + c] = p_ref[...]

    n_in = 2 + n512 + n1024
    return pl.pallas_call(
        body, name="small_pack", out_shape=jax.ShapeDtypeStruct((8, rows, width), F32),
        in_specs=[VMEM_SPEC] * n_in, out_specs=VMEM_SPEC,
        scratch_shapes=[pltpu.VMEM((rows, width), F32)],
    )(ddw, *[v512[n] for n in VEC512], *[v1024[n] for n in VEC1024], loss_parts)


def _small_copies(refs, send, recv):
    x, y, c, _ = _place()
    mine = refs[0].at[4 * x + 2 * y + c]
    peers = [(1 - x if k & 4 else x, 1 - y if k & 2 else y, 1 - c if k & 1 else c) for k in range(1, 8)]
    return [_rcopy(mine, mine, send.at[i], recv.at[i], dev) for i, dev in enumerate(peers)]


def _row_block(r):
    for tr in (512, 352, 256, 128):
        if r % tr == 0:
            return tr
    return r


def add_halves(g, recv, name):
    _, _, r, w = g.shape
    tr = _row_block(r)

    def body(g_ref, r_ref, ob_ref, own_ref):
        k = pl.program_id(1)
        me = 2 * lax.axis_index("x") + lax.axis_index("y")
        t = g_ref[0, 0].astype(F32) + r_ref[0].astype(F32)
        ob_ref[0] = t.astype(MM)
        mine = jnp.where(k == me, t, 0.0)

        @pl.when(k == 0)
        def _():
            own_ref[...] = mine

        @pl.when(k != 0)
        def _():
            own_ref[...] += mine

    return pl.pallas_call(
        body, name=name, grid=(r // tr, N_CHIPS),
        in_specs=[pl.BlockSpec((1, 1, tr, w), lambda i, k: (k, lax.axis_index("c"), i, 0)),
                  pl.BlockSpec((1, tr, w), lambda i, k: (k, i, 0))],
        out_specs=[pl.BlockSpec((1, tr, w), lambda i, k: (k, i, 0)),
                   pl.BlockSpec((tr, w), lambda i, k: (i, 0))],
        out_shape=(jax.ShapeDtypeStruct((N_CHIPS, r, w), MM), jax.ShapeDtypeStruct((r, w), F32)),
        compiler_params=_params(("arbitrary", "arbitrary")),
    )(g, recv)


def sum_parts(own, rin, after, name):
    _, r, w = rin.shape
    tr = _row_block(r)

    def body(o_ref, r_ref, after_ref, out_ref):
        out_ref[...] = ((o_ref[...] + r_ref[0].astype(F32)) + r_ref[1].astype(F32)) + r_ref[2].astype(F32)

    return pl.pallas_call(
        body, name=name, grid=(r // tr,), out_shape=jax.ShapeDtypeStruct((r, w), F32),
        in_specs=[pl.BlockSpec((tr, w), lambda i: (i, 0)), pl.BlockSpec((3, tr, w), lambda i: (0, i, 0)),
                  _full_spec((8, 128))],
        out_specs=pl.BlockSpec((tr, w), lambda i: (i, 0)),
        compiler_params=_params(("arbitrary",)),
    )(own, rin, after)


def _adamw_math(w, g, m, v):
    mn = ADAM_B1 * m + (1.0 - ADAM_B1) * g
    vn = ADAM_B2 * v + (1.0 - ADAM_B2) * (g * g)
    m_hat = mn / (1.0 - ADAM_B1 ** ADAM_STEP)
    v_hat = vn / (1.0 - ADAM_B2 ** ADAM_STEP)
    return -ADAM_LR * (m_hat / (jnp.sqrt(v_hat) + ADAM_EPS) + ADAM_WD * w), mn, vn


def adamw(w, mine, other, m, v, name):
    r, c = w.shape
    rh = r // 2
    tr = _row_block(rh)
    if c >= 1024 and tr % 512 == 0:
        tr = 256
    nb = rh // tr

    def body(w_ref, a_ref, b_ref, m_ref, v_ref, go_ref, d_ref, mo_ref, vo_ref):
        gv = jnp.where(lax.axis_index("c") == pl.program_id(0), a_ref[...], b_ref[...])
        go_ref[...] = gv
        d_ref[...], mo_ref[...], vo_ref[...] = _adamw_math(w_ref[...], gv, m_ref[...], v_ref[...])

    def half(of_sibling):
        def index(h, i):
            owner = lax.axis_index("c")
            owner = 1 - owner if of_sibling else owner
            return jnp.where(h == owner, i, jnp.where(h < owner, 0, nb - 1)), 0
        return pl.BlockSpec((tr, c), index)

    spec = pl.BlockSpec((tr, c), lambda h, i: (h * nb + i, 0))
    out = jax.ShapeDtypeStruct((r, c), F32)
    return pl.pallas_call(
        body, name=name, grid=(2, nb), out_shape=(out, out, out, out),
        in_specs=[spec, half(False), half(True), spec, spec], out_specs=[spec] * 4,
        compiler_params=_params(("arbitrary", "arbitrary")),
    )(w, mine, other, m, v)


def adamw_small(packs, params, after):
    names = list(params)
    flat = [a for n in names for a in params[n]]

    def body(*refs):
        p_ref = refs[0]
        ins = refs[1:1 + 3 * len(names)]
        loss_ref, g_ref = refs[2 + 3 * len(names):4 + 3 * len(names)]
        outs = refs[4 + 3 * len(names):]
        total = p_ref[0]
        for d in range(1, 8):
            total = total + p_ref[d]
        g_ref[...] = total
        loss_ref[...] = g_ref[LOSS_ROW:LOSS_ROW + 1, 0:1]
        me = 2 * lax.axis_index("x") + lax.axis_index("y")
        for i, n in enumerate(names):
            w_ref, m_ref, v_ref = ins[3 * i:3 * i + 3]
            go_ref, d_ref, mo_ref, vo_ref = outs[4 * i:4 * i + 4]
            if n == "conv_dw_w":
                gv = jnp.zeros((CONV_WIDTH, 128), F32)
                for k in range(N_CHIPS):
                    gv = gv + jnp.where(me == k, g_ref[0:CONV_WIDTH, 128 * k:128 * (k + 1)], 0.0)
            elif n in VEC512:
                r0 = 32 + VEC512.index(n)
                gv = g_ref[r0:r0 + 1, :]
            else:
                r0 = 32 + len(VEC512) + 2 * VEC1024.index(n)
                gv = jnp.concatenate([g_ref[r0:r0 + 1, :], g_ref[r0 + 1:r0 + 2, :]], axis=1)
            go_ref[...] = gv
            d_ref[...], mo_ref[...], vo_ref[...] = _adamw_math(w_ref[...], gv, m_ref[...], v_ref[...])

    out_shape = [jax.ShapeDtypeStruct((1, 1), F32), jax.ShapeDtypeStruct(packs.shape[1:], F32)]
    out_shape += [jax.ShapeDtypeStruct(params[n][0].shape, F32) for n in names for _ in range(4)]
    res = pl.pallas_call(
        body, name="adamw_small", out_shape=out_shape,
        in_specs=[VMEM_SPEC] * (2 + len(flat)), out_specs=[VMEM_SPEC] * len(out_shape),
        compiler_params=_params(),
    )(packs, *flat, after)
    return res[0], res[1], {n: res[2 + 4 * i:6 + 4 * i] for i, n in enumerate(names)}


REST = ("w_ffn_up", "w_ffn_down", "w_out", "w_conv_branch", "w_att_branch")
VEC512 = ("conv_dw_b", "conv_ln_g", "conv_ln_b")
VEC1024 = ("norm_mix_pre", "b_conv_branch", "norm_mix_post", "norm_ffn_pre", "norm_ffn_post")
PACK_ROWS = 48
LOSS_ROW = 47


def kernel(x, norm_mix_pre, w_in, conv_dw_w, conv_dw_b, conv_ln_g, conv_ln_b, w_conv_branch, b_conv_branch, w_att_branch, w_out, norm_mix_post, norm_ffn_pre, w_ffn_up, w_ffn_down, norm_ffn_post, loss_target, m_norm_mix_pre, m_w_in, m_conv_dw_w, m_conv_dw_b, m_conv_ln_g, m_conv_ln_b, m_w_conv_branch, m_b_conv_branch, m_w_att_branch, m_w_out, m_norm_mix_post, m_norm_ffn_pre, m_w_ffn_up, m_w_ffn_down, m_norm_ffn_post, v_norm_mix_pre, v_w_in, v_conv_dw_w, v_conv_dw_b, v_conv_ln_g, v_conv_ln_b, v_w_conv_branch, v_b_conv_branch, v_w_att_branch, v_w_out, v_norm_mix_post, v_norm_ffn_pre, v_w_ffn_up, v_w_ffn_down, v_norm_ffn_post):
    weights = dict(norm_mix_pre=norm_mix_pre, w_in=w_in, conv_dw_w=conv_dw_w, conv_dw_b=conv_dw_b, conv_ln_g=conv_ln_g, conv_ln_b=conv_ln_b, w_conv_branch=w_conv_branch, b_conv_branch=b_conv_branch, w_att_branch=w_att_branch, w_out=w_out, norm_mix_post=norm_mix_post, norm_ffn_pre=norm_ffn_pre, w_ffn_up=w_ffn_up, w_ffn_down=w_ffn_down, norm_ffn_post=norm_ffn_post)
    mom = dict(norm_mix_pre=m_norm_mix_pre, w_in=m_w_in, conv_dw_w=m_conv_dw_w, conv_dw_b=m_conv_dw_b, conv_ln_g=m_conv_ln_g, conv_ln_b=m_conv_ln_b, w_conv_branch=m_w_conv_branch, b_conv_branch=m_b_conv_branch, w_att_branch=m_w_att_branch, w_out=m_w_out, norm_mix_post=m_norm_mix_post, norm_ffn_pre=m_norm_ffn_pre, w_ffn_up=m_w_ffn_up, w_ffn_down=m_w_ffn_down, norm_ffn_post=m_norm_ffn_post)
    var = dict(norm_mix_pre=v_norm_mix_pre, w_in=v_w_in, conv_dw_w=v_conv_dw_w, conv_dw_b=v_conv_dw_b, conv_ln_g=v_conv_ln_g, conv_ln_b=v_conv_ln_b, w_conv_branch=v_w_conv_branch, b_conv_branch=v_b_conv_branch, w_att_branch=v_w_att_branch, w_out=v_w_out, norm_mix_post=v_norm_mix_post, norm_ffn_pre=v_norm_ffn_pre, w_ffn_up=v_w_ffn_up, w_ffn_down=v_w_ffn_down, norm_ffn_post=v_norm_ffn_post)
    order = list(weights)
    grads, deltas, new_m, new_v = {}, {}, {}, {}
    xs = x.reshape(SEQ, D_MODEL)
    tgt = loss_target.reshape(SEQ, D_MODEL)
    row = lambda a: a.reshape(1, -1)
    g1, g2, g3, g4 = (row(weights[n]) for n in ("norm_mix_pre", "norm_mix_post", "norm_ffn_pre", "norm_ffn_post"))
    ln_g, ln_b = row(conv_ln_g), row(conv_ln_b)

    summed, from_chips = {}, {}

    def core_sums(names, state, after, tag):
        own, from_sibling = sibling_wait(state, after, True, tag)
        for n, g, r in zip(names, own, from_sibling):
            summed[n] = add_halves(g, r, "add_" + n)

    def chip_sums(names, after):
        return [sum_parts(summed[n][1], from_chips[n], after, "sum_" + n) for n in names]

    def optimize(names, state, after, tag):
        mine, other = sibling_wait(state, after, False, tag)
        for n, a, b in zip(names, mine, other):
            grads[n], deltas[n], new_m[n], new_v[n] = adamw(weights[n], a, b, mom[n], var[n], "adamw_" + n)

    w_in_g, dw_g, *rest = all_gather_weights([w_in], conv_dw_w, [weights[n] for n in REST])
    w_dw_full = jnp.concatenate([dw_g[k] for k in range(N_CHIPS)], axis=1)
    rest_shapes = [weights[n].shape for n in REST]
    state, token = split_start("gather_start", rest, 3 * len(REST), _gather_copies(rest_shapes, 1))
    h1, ci, q, k, v, gc, ga = in_proj_fwd(xs, g1, w_in_g, token)
    u1, u3 = conv_fwd(ci, w_dw_full, row(conv_dw_b), ln_g, ln_b)
    att, rc = attn_fwd(q, k, v)
    rest = split_wait("gather_wait", state, [att], _gather_copies(rest_shapes, 1))
    pass_copies = _gather_copies(rest_shapes[2:] + rest_shapes[:2], 2)
    n_first = 3 * len(REST[2:])
    state, token = split_start("pass_start", rest[2:] + rest[:2], 3 * len(REST), pass_copies)
    passed = split_wait("pass_mix_wait", state, [], lambda *a: pass_copies(*a)[:n_first])
    w_out_g, w_cb_g, w_ab_g = passed[:3]
    w_out_g = w_out_g.reshape(D_MODEL, D_MODEL)
    co, ao, merged, mix, x2, h2 = mix_fwd(u3, att, gc, ga, xs, w_cb_g, row(b_conv_branch), w_ab_g, w_out_g,
                                          g2, g3, token)
    w_up_g, w_down_g = split_wait(
        "pass_ffn_wait", list(state[:2]) + list(passed[3:]), [h2],
        lambda refs, send, recv: _gather_copies(rest_shapes[:2], 2)(refs, _Shifted(send, n_first), _Shifted(recv, n_first)))
    w_down_g = w_down_g.reshape(2, D_FF // 2, D_MODEL)
    gate, up, act = ffn_up_fwd(h2, w_up_g)
    dff, dy, loss_parts, dg4 = ffn_down_loss(act, w_down_g, x2, tgt, g4)

    dgu = ffn_act_bwd(dff, w_down_g, gate, up)
    dx2, dmix, dg3, dg2 = ffn_in_bwd(dgu, w_up_g, x2, mix, dy, g3, g2)
    ffn_grads = [weight_grad(h2, dgu, "dw_ffn_up", True), weight_grad(act, dff, "dw_ffn_down", False, tk=UP_SHARD)]
    to_ffn, token = sibling_start(ffn_grads, True, "dw_ffn")
    dco, dao, dg, du3, datt, dbcb = merge_bwd(dmix, w_out_g, gc, ga, co, ao, w_cb_g, w_ab_g, token)
    to_mix, token = sibling_start(weight_grad_mix(merged, dmix, u3, dco, att, dao), True, "dw_mix")
    core_sums(REST[:2], to_ffn, [token], "dw_ffn")
    core_sums(REST[2:], to_mix, [summed["w_ffn_down"][1]], "dw_mix")
    state, token = scatter_start([summed[n][0] for n in REST], "rest")
    dci, ddw, dbdw, dlng, dlnb = conv_bwd(du3, u1, ci, w_dw_full, ln_g, ln_b, token)
    dqkv = attn_bwd(q, k, v, datt, rc, token)
    from_chips.update(zip(REST, scatter_wait(state, [dci, dqkv], "rest")))
    dproj = (dci, dqkv, dg)
    to_in, token = sibling_start([weight_grad_in(h1, dproj)], True, "dw_in")
    grad_x, dg1 = in_proj_bwd(dproj, w_in_g, xs, dx2, g1, token)
    v512 = dict(conv_dw_b=dbdw, conv_ln_g=dlng, conv_ln_b=dlnb)
    v1024 = dict(norm_mix_pre=dg1, b_conv_branch=dbcb, norm_mix_post=dg2, norm_ffn_pre=dg3, norm_ffn_post=dg4)
    packs = small_pack(ddw, v512, v1024, loss_parts)
    core_sums(("w_in",), to_in, [packs], "dw_in")
    to_chips = summed["w_in"][0]
    landing = lax.empty((3,) + to_chips.shape[1:], to_chips.dtype)

    def scatter_and_packs(refs, send, recv):
        return (_scatter_copies(1)(refs[:2], send, recv)
                + _small_copies(refs[2:], _Shifted(send, 3), _Shifted(recv, 3)))

    state, token = split_start("scatter_start_w_in", [to_chips, landing, packs], 3 + 7, scatter_and_packs)
    swap_up, token = sibling_start(chip_sums(REST[:1], token), False, "sum_ffn_up")
    swap_rest, token = sibling_start(chip_sums(REST[1:], token), False, "sum_rest")
    optimize(REST[:1], swap_up, [token], "sum_ffn_up")
    optimize(REST[1:], swap_rest, [new_v["w_ffn_up"]], "sum_rest")
    _, from_chips["w_in"], packs = split_wait("scatter_wait_w_in", state, [new_v[n] for n in REST], scatter_and_packs)
    swap_in, token = sibling_start(chip_sums(("w_in",), token), False, "sum_w_in")
    as_rows = lambda n, a: a if n == "conv_dw_w" else a.reshape(1, -1)
    small_names = ("conv_dw_w",) + VEC512 + VEC1024
    loss, gsum, small = adamw_small(
        packs, {n: tuple(as_rows(n, d[n]) for d in (weights, mom, var)) for n in small_names}, token)
    optimize(("w_in",), swap_in, [gsum], "sum_w_in")
    for n in small_names:
        grads[n], deltas[n], new_m[n], new_v[n] = (a.reshape(weights[n].shape) for a in small[n])

    return (loss.reshape(()), grad_x.reshape(1, SEQ, D_MODEL),*[grads[n] for n in order], *[deltas[n] for n in order],
            *[new_m[n] for n in order], *[new_v[n] for n in order])
```

```python
import jax
import jax.numpy as jnp
from jax import lax
from jax.experimental import pallas as pl
from jax.experimental.pallas import tpu as pltpu

F32 = jnp.float32
MM = jnp.bfloat16

SEQ = 2048
D_MODEL = 1024
CONV_DIM = 512
ATT_DIM = 512
CONV_WIDTH = 31
D_FF = 2816
IN_COLS = 2 * CONV_DIM + 3 * ATT_DIM + 2 * D_MODEL
N_CHIPS = 4
IN_SHARD = IN_COLS // N_CHIPS
UP_SHARD = 2 * D_FF // N_CHIPS
BR_SHARD = D_MODEL // N_CHIPS
EPS = 1e-6
ATT_SCALE = 0.125

TM = 256
GLU_ROWS = 256
TQ = 128
CONV_TILE = 64
CONV_WIN = CONV_TILE + 32
VMEM_LIMIT = 56 * 1024 * 1024

ADAM_LR = 0.001
ADAM_B1 = 0.9
ADAM_B2 = 0.999
ADAM_EPS = 1e-08
ADAM_WD = 0.01
ADAM_STEP = 10

MESH = pl.DeviceIdType.MESH
ANY = pl.BlockSpec(memory_space=pl.ANY)
VMEM_SPEC = pl.BlockSpec(memory_space=pltpu.VMEM)

NT_DIMS = (((1,), (1,)), ((), ()))
TN_DIMS = (((0,), (0,)), ((), ()))

IN_PIECES = (("ci", 0, 1024), ("q", 1024, 1536), ("k", 1536, 2048), ("v", 2048, 2560),
             ("gc", 2560, 3584), ("ga", 3584, 4608))


def _params(sem=None, vmem=VMEM_LIMIT):
    return pltpu.CompilerParams(dimension_semantics=sem, vmem_limit_bytes=vmem)


def _dot(a, b):
    return jnp.dot(a, b, preferred_element_type=F32)


def _dot_nt(a, b):
    return lax.dot_general(a, b, NT_DIMS, preferred_element_type=F32)


def _dot_tn(a, b):
    return lax.dot_general(a, b, TN_DIMS, preferred_element_type=F32)


def _sigmoid(x):
    return 1.0 / (1.0 + jnp.exp(-x))


def _rms(x):
    r = lax.rsqrt(jnp.mean(x * x, axis=-1, keepdims=True) + EPS)
    return x * r, r


def _rms_bwd(dy_g, n, r):
    return r * (dy_g - n * jnp.mean(dy_g * n, axis=-1, keepdims=True))


def _row_tile_spec(width, tm=TM):
    return pl.BlockSpec((tm, width), lambda i: (i, 0))


def _full_spec(shape):
    nd = len(shape)
    return pl.BlockSpec(shape, lambda *_: (0,) * nd)


def _weight_spec(shape):
    nd = len(shape)
    return pl.BlockSpec(shape, lambda *_: (0,) * nd, pipeline_mode=pl.Buffered(1))


def _acc_rows(ref, val, first):
    @pl.when(first)
    def _():
        ref[...] = val

    @pl.when(jnp.logical_not(first))
    def _():
        ref[...] += val


TOKEN_SPEC = pl.BlockSpec((8, 128), lambda *_: (0, 0))


def in_proj_fwd(x, g1, w_in_g, after):
    def body(x_ref, g_ref, w_ref, after_ref, h_ref, ci_ref, q_ref, k_ref, v_ref, gc_ref, ga_ref):
        n, _ = _rms(x_ref[...])
        h = (n * g_ref[...]).astype(MM)
        h_ref[...] = h
        outs = dict(ci=ci_ref, q=q_ref, k=k_ref, v=v_ref, gc=gc_ref, ga=ga_ref)
        for j in range(N_CHIPS):
            p = _dot(h, w_ref[j])
            g0 = j * IN_SHARD
            for name, s, e in IN_PIECES:
                lo, hi = max(s, g0), min(e, g0 + IN_SHARD)
                if lo < hi:
                    ref = outs[name]
                    part = p[:, lo - g0:hi - g0]
                    if name == "q":
                        part = part * ATT_SCALE
                    ref[:, lo - s:hi - s] = part.astype(ref.dtype)

    out_shape = [
        jax.ShapeDtypeStruct((SEQ, D_MODEL), MM),
        jax.ShapeDtypeStruct((SEQ, 2 * CONV_DIM), F32),
        jax.ShapeDtypeStruct((SEQ, ATT_DIM), MM),
        jax.ShapeDtypeStruct((SEQ, ATT_DIM), MM),
        jax.ShapeDtypeStruct((SEQ, ATT_DIM), MM),
        jax.ShapeDtypeStruct((SEQ, D_MODEL), F32),
        jax.ShapeDtypeStruct((SEQ, D_MODEL), F32),
    ]
    return pl.pallas_call(
        body, name="in_proj_fwd", grid=(SEQ // TM,), out_shape=out_shape,
        in_specs=[_row_tile_spec(D_MODEL), _full_spec((1, D_MODEL)), _weight_spec(w_in_g.shape), TOKEN_SPEC],
        out_specs=[_row_tile_spec(s.shape[1]) for s in out_shape],
        compiler_params=_params(("arbitrary",)),
    )(x, g1, w_in_g, after)


LANE_GROUPS = [slice(g, g + 128) for g in range(0, CONV_DIM, 128)]
NORM_ROWS = 16


def _shifted_windows(src_ref, t0, cols, offsets):
    win = src_ref[pl.ds(t0, CONV_WIN), cols]
    for rot in range(8):
        ms = [m for m in offsets if m % 8 == rot]
        if ms:
            shifted = win if rot == 0 else pltpu.roll(win, CONV_WIN - rot, 0)
            for m in ms:
                yield m, shifted[m - rot:m - rot + CONV_TILE, :]


def _shifted_sum(src_ref, t0, cols, w_ref, offset_of_tap):
    tap_at = {offset_of_tap(j): j for j in range(CONV_WIDTH)}
    acc = None
    for m, rows in _shifted_windows(src_ref, t0, cols, sorted(tap_at)):
        t = w_ref[tap_at[m]:tap_at[m] + 1, cols] * rows
        acc = t if acc is None else acc + t
    return acc


def _glu_into(ci_ref, upad_ref):
    upad_ref[0:32, :] = jnp.zeros((32, CONV_DIM), F32)

    def step(i, c):
        t0 = pl.multiple_of(i * GLU_ROWS, GLU_ROWS)
        a = ci_ref[pl.ds(t0, GLU_ROWS), 0:CONV_DIM]
        b = ci_ref[pl.ds(t0, GLU_ROWS), CONV_DIM:2 * CONV_DIM]
        upad_ref[pl.ds(t0 + 32, GLU_ROWS), :] = a * _sigmoid(b)
        return c

    lax.fori_loop(0, SEQ // GLU_ROWS, step, 0)


def _layernorm_parts(u1):
    mu = jnp.mean(u1, axis=-1, keepdims=True)
    xc = u1 - mu
    rstd = lax.rsqrt(jnp.mean(xc * xc, axis=-1, keepdims=True) + EPS)
    return xc * rstd, rstd


def conv_fwd(ci, w_dw, b_dw, ln_g, ln_b):
    def body(ci_ref, w_ref, b_ref, g_ref, bb_ref, u1_ref, u3_ref, upad_ref):
        _glu_into(ci_ref, upad_ref)

        def step(i, c):
            t0 = pl.multiple_of(i * CONV_TILE, CONV_TILE)
            for cols in LANE_GROUPS:
                u1_ref[pl.ds(t0, CONV_TILE), cols] = (_shifted_sum(upad_ref, t0, cols, w_ref, lambda j: j + 2)
                                                      + b_ref[:, cols])
            for r in range(0, CONV_TILE, NORM_ROWS):
                rows = pl.ds(t0 + r, NORM_ROWS)
                xh, _ = _layernorm_parts(u1_ref[rows, :])
                u2 = xh * g_ref[...] + bb_ref[...]
                u3_ref[rows, :] = (u2 * _sigmoid(u2)).astype(MM)
            return c

        lax.fori_loop(0, SEQ // CONV_TILE, step, 0)

    return pl.pallas_call(
        body, name="conv_fwd",
        out_shape=[jax.ShapeDtypeStruct((SEQ, CONV_DIM), F32), jax.ShapeDtypeStruct((SEQ, CONV_DIM), MM)],
        in_specs=[VMEM_SPEC] * 5, out_specs=[VMEM_SPEC] * 2,
        scratch_shapes=[pltpu.VMEM((SEQ + 32, CONV_DIM), F32)],
        compiler_params=_params(),
    )(ci, w_dw, b_dw, ln_g, ln_b)


def _softplus(z):
    return jnp.maximum(z, 0.0) + jnp.log(1.0 + jnp.exp(-jnp.abs(z)))


def _cumsum_weights(suffix, with_total):
    n = 256 if with_total else 128
    r = lax.broadcasted_iota(jnp.int32, (128, n), 0)
    c = lax.broadcasted_iota(jnp.int32, (128, n), 1)
    tri = (r >= c) if suffix else (r <= c)
    return jnp.logical_or(tri, c >= 128).astype(MM)


NO_SCORE = -1e30
N_KB = SEQ // TQ


def _score_bias(lane, row, i, j):
    keep = jnp.logical_and(i >= 0, jnp.logical_or(j < i, lane < row))
    return jnp.where(keep, 0.0, NO_SCORE)


def _block_pipeline(n_stages, descending, step, on_query_block=None):
    n_lag = n_stages - 1
    none = jnp.int32(-1)

    def shift(cur, lag):
        step([cur] + [(lag[2 * s], lag[2 * s + 1]) for s in range(n_lag)])
        return (cur[0], cur[1]) + tuple(lag[:-2])

    def outer(i, lag):
        if on_query_block is not None:
            on_query_block(i)

        def inner(n, lag):
            return shift((i, i - n if descending else n), lag)
        return lax.fori_loop(0, i + 1, inner, lag)

    lag = lax.fori_loop(0, N_KB, outer, (none,) * (2 * n_lag))
    lax.fori_loop(0, n_lag, lambda n, lag: shift((none, none), lag), lag)


def _head_masks():
    lane = lax.broadcasted_iota(jnp.int32, (TQ, 128), 1)
    row = lax.broadcasted_iota(jnp.int32, (TQ, 128), 0)
    return lane, row, lane < 64


def _pick_head(x, head0, h):
    zero = jnp.zeros_like(x)
    return jnp.where(head0, x, zero) if h == 0 else jnp.where(head0, zero, x)


N_PAIRS = ATT_DIM // 128


def _split_heads(src_ref, dst_ref):
    _, _, head0 = _head_masks()

    def block(b, c):
        r0 = pl.multiple_of(b * TQ, TQ)
        d0 = pl.multiple_of(b * 2 * TQ, 2 * TQ)
        for p in range(N_PAIRS):
            x = src_ref[pl.ds(r0, TQ), 128 * p:128 * (p + 1)]
            for h in range(2):
                dst_ref[p, pl.ds(d0 + TQ * h, TQ), :] = _pick_head(x, head0, h)
        return c

    lax.fori_loop(0, N_KB, block, 0)


def attn_fwd(q, k, v):
    def body(q_ref, k_ref, v_ref, o_ref, rc_ref, acc_ref, r_ref, z_ref, spb_ref, ab_ref, qm_ref, vm_ref):
        lane, row, _ = _head_masks()
        w = _cumsum_weights(suffix=True, with_total=True)
        _split_heads(q_ref, qm_ref)
        _split_heads(v_ref, vm_ref)
        acc_ref[...] = jnp.zeros_like(acc_ref)
        r_ref[...] = jnp.zeros_like(r_ref)
        rc_ref[...] = jnp.zeros_like(rc_ref)
        z_ref[...] = jnp.full(z_ref.shape, NO_SCORE, F32)
        spb_ref[...] = jnp.zeros_like(spb_ref)
        ab_ref[...] = jnp.zeros_like(ab_ref)

        def step(pairs):
            (i1, j1), (i2, j2), (i3, j3) = pairs
            k1, q2, q3 = (pl.multiple_of(jnp.maximum(b, 0) * TQ, TQ) for b in (j1, i2, i3))
            q1, k3 = (pl.multiple_of(jnp.maximum(b, 0) * 2 * TQ, 2 * TQ) for b in (i1, j3))
            bias1 = _score_bias(lane, row, i1, j1)
            first2 = j2 == i2
            rc_rows = rc_ref[pl.ds(q2, TQ), :]
            for p in range(N_PAIRS):
                cols = slice(128 * p, 128 * (p + 1))
                kb = k_ref[pl.ds(k1, TQ), cols]
                acc_ref[pl.ds(q3, TQ), cols] += _dot(ab_ref[p], vm_ref[p, pl.ds(k3, 2 * TQ), :])
                for h in range(2):
                    hh = 2 * p + h
                    r = _dot(spb_ref[hh], w)
                    r_in = jnp.where(first2, 0.0, r_ref[hh])
                    ab_ref[p, :, 128 * h:128 * (h + 1)] = jnp.exp(z_ref[hh] - (r[:, :128] + r_in)).astype(MM)
                    rc_rows = jnp.where(jnp.logical_and(lane == 16 * hh + j2, i2 >= 0), r_in, rc_rows)
                    r_ref[hh] = r_in + r[:, 128:]
                    z = _dot_nt(qm_ref[p, pl.ds(q1 + TQ * h, TQ), :], kb) + bias1
                    z_ref[hh] = z
                    spb_ref[hh] = _softplus(z).astype(MM)
            rc_ref[pl.ds(q2, TQ), :] = rc_rows

        _block_pipeline(3, True, step)
        o_ref[...] = acc_ref[...].astype(MM)

    return pl.pallas_call(
        body, name="attn_fwd",
        out_shape=[jax.ShapeDtypeStruct((SEQ, ATT_DIM), MM), jax.ShapeDtypeStruct((SEQ, 128), F32)],
        in_specs=[VMEM_SPEC] * 3, out_specs=[VMEM_SPEC] * 2,
        scratch_shapes=[pltpu.VMEM((SEQ, ATT_DIM), F32), pltpu.VMEM((8, TQ, 128), F32),
                        pltpu.VMEM((8, TQ, 128), F32), pltpu.VMEM((8, TQ, 128), MM),
                        pltpu.VMEM((N_PAIRS, TQ, 256), MM), pltpu.VMEM((N_PAIRS, 2 * SEQ, 128), MM),
                        pltpu.VMEM((N_PAIRS, 2 * SEQ, 128), MM)],
        compiler_params=_params(),
    )(q, k, v)


def mix_fwd(u3, att, gc, ga, x, w_cb_g, b_cb, w_ab_g, w_out_g, g2, g3, after):
    def body(u_ref, a_ref, gc_ref, ga_ref, x_ref, wcb_ref, bcb_ref, wab_ref, wout_ref, g2_ref, g3_ref, after_ref,
             co_ref, ao_ref, mg_ref, mix_ref, x2_ref, h2_ref):
        u = u_ref[...]
        a = a_ref[...]
        co = jnp.concatenate([_dot(u, wcb_ref[j]) for j in range(N_CHIPS)], axis=1) + bcb_ref[...]
        ao = jnp.concatenate([_dot(a, wab_ref[j]) for j in range(N_CHIPS)], axis=1)
        co_ref[...] = co.astype(MM)
        ao_ref[...] = ao.astype(MM)
        merged = (_sigmoid(gc_ref[...]) * co + _sigmoid(ga_ref[...]) * ao).astype(MM)
        mg_ref[...] = merged
        mix = _dot(merged, wout_ref[...])
        mix_ref[...] = mix
        n2, _ = _rms(mix)
        x2 = x_ref[...] + n2 * g2_ref[...]
        x2_ref[...] = x2
        n3, _ = _rms(x2)
        h2_ref[...] = (n3 * g3_ref[...]).astype(MM)

    out_shape = [
        jax.ShapeDtypeStruct((SEQ, D_MODEL), MM), jax.ShapeDtypeStruct((SEQ, D_MODEL), MM),
        jax.ShapeDtypeStruct((SEQ, D_MODEL), MM), jax.ShapeDtypeStruct((SEQ, D_MODEL), F32),
        jax.ShapeDtypeStruct((SEQ, D_MODEL), F32), jax.ShapeDtypeStruct((SEQ, D_MODEL), MM),
    ]
    vec = _full_spec((1, D_MODEL))
    return pl.pallas_call(
        body, name="mix_fwd", grid=(SEQ // TM,), out_shape=out_shape,
        in_specs=[_row_tile_spec(CONV_DIM), _row_tile_spec(ATT_DIM), _row_tile_spec(D_MODEL),
                  _row_tile_spec(D_MODEL), _row_tile_spec(D_MODEL), _weight_spec(w_cb_g.shape), vec,
                  _weight_spec(w_ab_g.shape), _weight_spec(w_out_g.shape), vec, vec, TOKEN_SPEC],
        out_specs=[_row_tile_spec(D_MODEL)] * 6,
        compiler_params=_params(("arbitrary",)),
    )(u3, att, gc, ga, x, w_cb_g, b_cb, w_ab_g, w_out_g, g2, g3, after)


def ffn_up_fwd(h2, w_up_g):
    def body(h_ref, wg_ref, wu_ref, gate_ref, up_ref, act_ref):
        h = h_ref[...]
        gate = _dot(h, wg_ref[0])
        up = _dot(h, wu_ref[0])
        gate_ref[...] = gate.astype(MM)
        up_ref[...] = up.astype(MM)
        act_ref[...] = (gate * _sigmoid(gate) * up).astype(MM)

    tile = pl.BlockSpec((TM, UP_SHARD), lambda n, i: (i, n))
    act = jax.ShapeDtypeStruct((SEQ, D_FF), MM)
    return pl.pallas_call(
        body, name="ffn_up_fwd", grid=(2, SEQ // TM), out_shape=[act, act, act],
        in_specs=[pl.BlockSpec((TM, D_MODEL), lambda n, i: (i, 0)),
                  pl.BlockSpec((1, D_MODEL, UP_SHARD), lambda n, i: (n, 0, 0)),
                  pl.BlockSpec((1, D_MODEL, UP_SHARD), lambda n, i: (n + 2, 0, 0))],
        out_specs=[tile, tile, tile],
        compiler_params=_params(("arbitrary", "arbitrary")),
    )(h2, w_up_g, w_up_g)


def ffn_down_loss(act, w_down_g, x2, target, g4):
    def body(act_ref, wd_ref, x2_ref, t_ref, g_ref, dff_ref, dy_ref, loss_ref, dg_ref):
        ff = _dot(act_ref[...], wd_ref[...])
        n4, r4 = _rms(ff)
        g4v = g_ref[...]
        err = x2_ref[...] + n4 * g4v - t_ref[...]
        row_loss = jnp.mean(err * err, axis=-1, keepdims=True)
        loss_ref[...] = jnp.zeros((8, 128), F32) + 0.5 * jnp.sum(row_loss, axis=0, keepdims=True)
        dy = err * (1.0 / D_MODEL)
        dy_ref[...] = dy
        dff_ref[...] = _rms_bwd(dy * g4v, n4, r4).astype(MM)
        _acc_rows(dg_ref, jnp.sum(dy * n4, axis=0, keepdims=True), pl.program_id(0) == 0)

    nt = SEQ // TM
    vec = _full_spec((1, D_MODEL))
    return pl.pallas_call(
        body, name="ffn_down_loss", grid=(nt,),
        out_shape=(jax.ShapeDtypeStruct((SEQ, D_MODEL), MM), jax.ShapeDtypeStruct((SEQ, D_MODEL), F32),
                   jax.ShapeDtypeStruct((nt * 8, 128), F32), jax.ShapeDtypeStruct((1, D_MODEL), F32)),
        in_specs=[_row_tile_spec(D_FF), _weight_spec(w_down_g.shape), _row_tile_spec(D_MODEL),
                  _row_tile_spec(D_MODEL), vec],
        out_specs=[_row_tile_spec(D_MODEL), _row_tile_spec(D_MODEL),
                   pl.BlockSpec((8, 128), lambda i: (i, 0)), vec],
        compiler_params=_params(("arbitrary",)),
    )(act, w_down_g, x2, target, g4)


def ffn_act_bwd(dff, w_down_g, gate, up):
    def body(dff_ref, wd_ref, gate_ref, up_ref, dgu_ref):
        dact = _dot_nt(dff_ref[...], wd_ref[...])
        gate = gate_ref[...].astype(F32)
        sg = _sigmoid(gate)
        dgu_ref[:, 0:D_FF] = (dact * up_ref[...].astype(F32) * (sg * (1.0 + gate * (1.0 - sg)))).astype(MM)
        dgu_ref[:, D_FF:2 * D_FF] = (dact * (gate * sg)).astype(MM)

    return pl.pallas_call(
        body, name="ffn_act_bwd", grid=(SEQ // TM,),
        out_shape=jax.ShapeDtypeStruct((SEQ, 2 * D_FF), MM),
        in_specs=[_row_tile_spec(D_MODEL), _weight_spec(w_down_g.shape), _row_tile_spec(D_FF), _row_tile_spec(D_FF)],
        out_specs=_row_tile_spec(2 * D_FF),
        compiler_params=_params(("arbitrary",)),
    )(dff, w_down_g, gate, up)


def ffn_in_bwd(dgu, w_up_g, x2, mix, dy, g3, g2):
    def body(dgu_ref, w_ref, x2_ref, mix_ref, dy_ref, g3_ref, g2_ref, dx2_ref, dmix_ref, dg3_ref, dg2_ref):
        dh2 = None
        for j in range(N_CHIPS):
            t = _dot_nt(dgu_ref[:, j * UP_SHARD:(j + 1) * UP_SHARD], w_ref[j])
            dh2 = t if dh2 is None else dh2 + t
        first = pl.program_id(0) == 0
        n3, r3 = _rms(x2_ref[...])
        dx2 = dy_ref[...] + _rms_bwd(dh2 * g3_ref[...], n3, r3)
        dx2_ref[...] = dx2
        _acc_rows(dg3_ref, jnp.sum(dh2 * n3, axis=0, keepdims=True), first)
        n2, r2 = _rms(mix_ref[...])
        dmix_ref[...] = _rms_bwd(dx2 * g2_ref[...], n2, r2).astype(MM)
        _acc_rows(dg2_ref, jnp.sum(dx2 * n2, axis=0, keepdims=True), first)

    vec = _full_spec((1, D_MODEL))
    return pl.pallas_call(
        body, name="ffn_in_bwd", grid=(SEQ // TM,),
        out_shape=(jax.ShapeDtypeStruct((SEQ, D_MODEL), F32), jax.ShapeDtypeStruct((SEQ, D_MODEL), MM),
                   jax.ShapeDtypeStruct((1, D_MODEL), F32), jax.ShapeDtypeStruct((1, D_MODEL), F32)),
        in_specs=[_row_tile_spec(2 * D_FF), _weight_spec(w_up_g.shape), _row_tile_spec(D_MODEL),
                  _row_tile_spec(D_MODEL), _row_tile_spec(D_MODEL), vec, vec],
        out_specs=[_row_tile_spec(D_MODEL), _row_tile_spec(D_MODEL), vec, vec],
        compiler_params=_params(("arbitrary",)),
    )(dgu, w_up_g, x2, mix, dy, g3, g2)


def merge_bwd(dmix, w_out_g, gc, ga, co, ao, w_cb_g, w_ab_g, after):
    def body(dmix_ref, wout_ref, gc_ref, ga_ref, co_ref, ao_ref, wcb_ref, wab_ref, after_ref,
             dco_ref, dao_ref, dg_ref, du3_ref, datt_ref, dbcb_ref):
        dm = _dot_nt(dmix_ref[...], wout_ref[...])
        sgc = _sigmoid(gc_ref[...])
        sga = _sigmoid(ga_ref[...])
        dco = dm * sgc
        dao = dm * sga
        dg_ref[:, 0:D_MODEL] = (dm * co_ref[...].astype(F32) * (sgc * (1.0 - sgc))).astype(MM)
        dg_ref[:, D_MODEL:2 * D_MODEL] = (dm * ao_ref[...].astype(F32) * (sga * (1.0 - sga))).astype(MM)
        _acc_rows(dbcb_ref, jnp.sum(dco, axis=0, keepdims=True), pl.program_id(0) == 0)
        dco_ref[...] = dco.astype(MM)
        dao_ref[...] = dao.astype(MM)
        du3 = None
        datt = None
        for j in range(N_CHIPS):
            cols = slice(j * BR_SHARD, (j + 1) * BR_SHARD)
            t = _dot_nt(dco_ref[:, cols], wcb_ref[j])
            s = _dot_nt(dao_ref[:, cols], wab_ref[j])
            du3 = t if du3 is None else du3 + t
            datt = s if datt is None else datt + s
        du3_ref[...] = du3
        datt_ref[...] = datt.astype(MM)

    wide = _row_tile_spec(D_MODEL)
    return pl.pallas_call(
        body, name="merge_bwd", grid=(SEQ // TM,),
        out_shape=(jax.ShapeDtypeStruct((SEQ, D_MODEL), MM), jax.ShapeDtypeStruct((SEQ, D_MODEL), MM),
                   jax.ShapeDtypeStruct((SEQ, 2 * D_MODEL), MM),
                   jax.ShapeDtypeStruct((SEQ, CONV_DIM), F32), jax.ShapeDtypeStruct((SEQ, ATT_DIM), MM),
                   jax.ShapeDtypeStruct((1, D_MODEL), F32)),
        in_specs=[wide, _weight_spec(w_out_g.shape), wide, wide, wide, wide,
                  _weight_spec(w_cb_g.shape), _weight_spec(w_ab_g.shape), TOKEN_SPEC],
        out_specs=[wide, wide, _row_tile_spec(2 * D_MODEL), _row_tile_spec(CONV_DIM), _row_tile_spec(ATT_DIM),
                   _full_spec((1, D_MODEL))],
        compiler_params=_params(("arbitrary",)),
    )(dmix, w_out_g, gc, ga, co, ao, w_cb_g, w_ab_g, after)


def conv_bwd(du3, u1, ci, w_dw, ln_g, ln_b, after):
    def body(du3_ref, u1_ref, ci_ref, w_ref, g_ref, bb_ref, after_ref,
             dci_ref, dw_ref, dbdw_ref, dg_ref, db_ref, upad_ref, dpad_ref, dwacc_ref, vacc_ref):
        _glu_into(ci_ref, upad_ref)
        dpad_ref[SEQ:SEQ + 32, :] = jnp.zeros((32, CONV_DIM), F32)
        dwacc_ref[...] = jnp.zeros_like(dwacc_ref)
        vacc_ref[...] = jnp.zeros_like(vacc_ref)

        def fold8(t):
            s = t[0:8, :]
            for r in range(8, t.shape[0], 8):
                s = s + t[r:r + 8, :]
            return s

        def pass1(i, c):
            t0 = pl.multiple_of(i * CONV_TILE, CONV_TILE)
            gv = g_ref[...]
            for r in range(0, CONV_TILE, NORM_ROWS):
                rows = pl.ds(t0 + r, NORM_ROWS)
                xh, rstd = _layernorm_parts(u1_ref[rows, :])
                u2 = xh * gv + bb_ref[...]
                s2 = _sigmoid(u2)
                du2 = du3_ref[rows, :] * (s2 * (1.0 + u2 * (1.0 - s2)))
                wv = du2 * gv
                du1 = rstd * (wv - jnp.mean(wv, axis=-1, keepdims=True)
                              - xh * jnp.mean(wv * xh, axis=-1, keepdims=True))
                dpad_ref[rows, :] = du1
                vacc_ref[0] += fold8(du2 * xh)
                vacc_ref[1] += fold8(du2)
                vacc_ref[2] += fold8(du1)
            for cols in LANE_GROUPS:
                du1 = dpad_ref[pl.ds(t0, CONV_TILE), cols]
                for m, rows in _shifted_windows(upad_ref, t0, cols, range(2, CONV_WIDTH + 2)):
                    dwacc_ref[m - 2, :, cols] += fold8(du1 * rows)
            return c

        lax.fori_loop(0, SEQ // CONV_TILE, pass1, 0)

        def pass2(i, c):
            t0 = pl.multiple_of(i * CONV_TILE, CONV_TILE)
            tile = pl.ds(t0, CONV_TILE)
            for cols in LANE_GROUPS:
                gate_cols = slice(cols.start + CONV_DIM, cols.stop + CONV_DIM)
                du0 = _shifted_sum(dpad_ref, t0, cols, w_ref, lambda j: 30 - j)
                a = ci_ref[tile, cols]
                sb = _sigmoid(ci_ref[tile, gate_cols])
                dci_ref[tile, cols] = (du0 * sb).astype(MM)
                dci_ref[tile, gate_cols] = (du0 * a * (sb * (1.0 - sb))).astype(MM)
            return c

        lax.fori_loop(0, SEQ // CONV_TILE, pass2, 0)

        for j in range(CONV_WIDTH):
            dw_ref[j:j + 1, :] = jnp.sum(dwacc_ref[j], axis=0, keepdims=True)
        dw_ref[CONV_WIDTH:32, :] = jnp.zeros((32 - CONV_WIDTH, CONV_DIM), F32)
        dg_ref[...] = jnp.sum(vacc_ref[0], axis=0, keepdims=True)
        db_ref[...] = jnp.sum(vacc_ref[1], axis=0, keepdims=True)
        dbdw_ref[...] = jnp.sum(vacc_ref[2], axis=0, keepdims=True)

    vec = jax.ShapeDtypeStruct((1, CONV_DIM), F32)
    return pl.pallas_call(
        body, name="conv_bwd",
        out_shape=(jax.ShapeDtypeStruct((SEQ, 2 * CONV_DIM), MM), jax.ShapeDtypeStruct((32, CONV_DIM), F32),
                   vec, vec, vec),
        in_specs=[VMEM_SPEC] * 7, out_specs=[VMEM_SPEC] * 5,
        scratch_shapes=[pltpu.VMEM((SEQ + 32, CONV_DIM), F32), pltpu.VMEM((SEQ + 32, CONV_DIM), F32),
                        pltpu.VMEM((CONV_WIDTH, 8, CONV_DIM), F32), pltpu.VMEM((3, 8, CONV_DIM), F32)],
        compiler_params=_params(),
    )(du3, u1, ci, w_dw, ln_g, ln_b, after)


def attn_bwd(q, k, v, datt, rc, after):
    def body(q_ref, k_ref, v_ref, do_ref, rc_ref, after_ref, dqkv_ref, dqa_ref, dka_ref, dva_ref, pc_ref, z_ref,
             sig1_ref, sig2_ref, g_ref, spb_ref, gb_ref, ar_ref, dzr_ref, dzc_ref, qm_ref, km_ref, dom_ref):
        lane, row, _ = _head_masks()
        _split_heads(q_ref, qm_ref)
        _split_heads(k_ref, km_ref)
        _split_heads(do_ref, dom_ref)
        for ref in (dqa_ref, dka_ref, dva_ref, pc_ref):
            ref[...] = jnp.zeros_like(ref)
        z_ref[...] = jnp.full(z_ref.shape, NO_SCORE, F32)
        for ref in (sig1_ref, sig2_ref, spb_ref, ar_ref, g_ref, gb_ref, dzr_ref, dzc_ref):
            ref[...] = jnp.zeros_like(ref)
        w_suffix = _cumsum_weights(suffix=True, with_total=False)
        w_prefix = _cumsum_weights(suffix=False, with_total=True)

        def step(pairs):
            (ia, ja), (ib, jb), (ic, jc), (id_, jd) = pairs
            ka, qb_, kb_, kc, qd, kd = (pl.multiple_of(jnp.maximum(b, 0) * TQ, TQ) for b in (ja, ib, jb, jc, id_, jd))
            qa2, qb2, qc2, qd2, kd2 = (pl.multiple_of(jnp.maximum(b, 0) * 2 * TQ, 2 * TQ)
                                       for b in (ia, ib, ic, id_, jd))
            bias_a = _score_bias(lane, row, ia, ja)
            rc_rows = rc_ref[pl.ds(qb_, TQ), :]
            first_c = jc == 0
            for p in range(N_PAIRS):
                cols = slice(128 * p, 128 * (p + 1))
                k_a = k_ref[pl.ds(ka, TQ), cols]
                v_b = v_ref[pl.ds(kb_, TQ), cols]
                dqa_ref[pl.ds(qd, TQ), cols] += _dot(dzc_ref[p], km_ref[p, pl.ds(kd2, 2 * TQ), :])
                dka_ref[pl.ds(kd, TQ), cols] += _dot_tn(dzr_ref[p], qm_ref[p, pl.ds(qd2, 2 * TQ), :])
                dva_ref[pl.ds(kc, TQ), cols] += _dot_tn(ar_ref[p], dom_ref[p, pl.ds(qc2, 2 * TQ), :])
                for h in range(2):
                    hh = 2 * p + h
                    rows = slice(TQ * h, TQ * (h + 1))
                    r = _dot(gb_ref[hh], w_prefix)
                    p_in = jnp.where(first_c, 0.0, pc_ref[hh])
                    dz = (g_ref[hh] - sig2_ref[hh] * (r[:, :128] + p_in)).astype(MM)
                    dzc_ref[p, :, rows] = dz
                    dzr_ref[p, rows, :] = dz
                    pc_ref[hh] = p_in + r[:, 128:]
                    r_in = jnp.sum(jnp.where(lane == 16 * hh + jb, rc_rows, 0.0), axis=1, keepdims=True)
                    a = jnp.exp(z_ref[hh] - (_dot(spb_ref[hh], w_suffix) + r_in))
                    g = _dot_nt(dom_ref[p, pl.ds(qb2 + TQ * h, TQ), :], v_b) * a
                    ar_ref[p, rows, :] = a.astype(MM)
                    g_ref[hh] = g
                    gb_ref[hh] = g.astype(MM)
                    sig2_ref[hh] = sig1_ref[hh]
                    z = _dot_nt(qm_ref[p, pl.ds(qa2 + TQ * h, TQ), :], k_a) + bias_a
                    sp = _softplus(z)
                    sig1_ref[hh] = jnp.exp(z - sp)
                    z_ref[hh] = z
                    spb_ref[hh] = sp.astype(MM)

        _block_pipeline(4, False, step)
        dqkv_ref[:, 0:ATT_DIM] = (dqa_ref[...] * ATT_SCALE).astype(MM)
        dqkv_ref[:, ATT_DIM:2 * ATT_DIM] = dka_ref[...].astype(MM)
        dqkv_ref[:, 2 * ATT_DIM:3 * ATT_DIM] = dva_ref[...].astype(MM)

    split = pltpu.VMEM((N_PAIRS, 2 * SEQ, 128), MM)
    return pl.pallas_call(
        body, name="attn_bwd", out_shape=jax.ShapeDtypeStruct((SEQ, 3 * ATT_DIM), MM),
        in_specs=[VMEM_SPEC] * 6, out_specs=VMEM_SPEC,
        scratch_shapes=[pltpu.VMEM((SEQ, ATT_DIM), F32)] * 3 + [pltpu.VMEM((8, TQ, 128), F32)] * 5
                       + [pltpu.VMEM((8, TQ, 128), MM)] * 2
                       + [pltpu.VMEM((N_PAIRS, 2 * TQ, 128), MM)] * 2 + [pltpu.VMEM((N_PAIRS, TQ, 256), MM)]
                       + [split] * 3,
        compiler_params=_params(),
    )(q, k, v, datt, rc, after)


DPROJ_PIECES = ((0, 1024), (1024, 2560), (2560, 4608))


def _dproj_segments(j):
    g0, g1 = j * IN_SHARD, (j + 1) * IN_SHARD
    segs = []
    for p, (s, e) in enumerate(DPROJ_PIECES):
        lo, hi = max(s, g0), min(e, g1)
        if lo < hi:
            segs.append((p, lo - s, lo - g0, hi - lo))
    return segs


def in_proj_bwd(pieces, w_in_g, x, dx2, g1, after):
    def body(p0_ref, p1_ref, p2_ref, w_ref, x_ref, dx2_ref, g_ref, after_ref, dx_ref, dg_ref):
        p_refs = (p0_ref, p1_ref, p2_ref)
        dh = None
        for j in range(N_CHIPS):
            for p, lo, off, width in _dproj_segments(j):
                t = _dot_nt(p_refs[p][:, lo:lo + width], w_ref[j, :, off:off + width])
                dh = t if dh is None else dh + t
        n1, r1 = _rms(x_ref[...])
        dx_ref[...] = dx2_ref[...] + _rms_bwd(dh * g_ref[...], n1, r1)
        _acc_rows(dg_ref, jnp.sum(dh * n1, axis=0, keepdims=True), pl.program_id(0) == 0)

    vec = _full_spec((1, D_MODEL))
    return pl.pallas_call(
        body, name="in_proj_bwd", grid=(SEQ // TM,),
        out_shape=[jax.ShapeDtypeStruct((SEQ, D_MODEL), F32), jax.ShapeDtypeStruct((1, D_MODEL), F32)],
        in_specs=[_row_tile_spec(p.shape[1]) for p in pieces]
                 + [_weight_spec(w_in_g.shape), _row_tile_spec(D_MODEL), _row_tile_spec(D_MODEL), vec, TOKEN_SPEC],
        out_specs=[_row_tile_spec(D_MODEL), vec],
        compiler_params=_params(("arbitrary",)),
    )(*pieces, w_in_g, x, dx2, g1, after)


def weight_grad_in(h1, pieces):
    kh = D_MODEL // 2

    def body(a_ref, p0_ref, p1_ref, p2_ref, o_ref):
        p_refs = (p0_ref, p1_ref, p2_ref)
        a = a_ref[...]
        for j in range(N_CHIPS):
            @pl.when(pl.program_id(1) == j)
            def _():
                for p, lo, off, width in _dproj_segments(j):
                    o_ref[0, 0, :, off:off + width] = _dot_tn(a, p_refs[p][:, lo:lo + width]).astype(MM)

    return pl.pallas_call(
        body, name="dw_in", grid=(2, N_CHIPS), out_shape=jax.ShapeDtypeStruct((N_CHIPS, 2, kh, IN_SHARD), MM),
        in_specs=[pl.BlockSpec((SEQ, kh), lambda h, j: (0, h))] + [_weight_spec(p.shape) for p in pieces],
        out_specs=pl.BlockSpec((1, 1, kh, IN_SHARD), lambda h, j: (j, h, 0, 0)),
        compiler_params=_params(("arbitrary", "arbitrary")),
    )(h1, *pieces)


def weight_grad(a, b, name, col_sharded, tk=None):
    kin, n = a.shape[1], b.shape[1]

    def body(a_ref, b_ref, o_ref):
        if col_sharded:
            o_ref[0, 0] = _dot_tn(a_ref[...], b_ref[...]).astype(MM)
        else:
            o_ref[...] = _dot_tn(a_ref[...], b_ref[...]).astype(MM)

    if col_sharded:
        kh, ns = kin // 2, n // N_CHIPS
        out = jax.ShapeDtypeStruct((N_CHIPS, 2, kh, ns), MM)
        grid = (2, N_CHIPS)
        in_specs = [pl.BlockSpec((SEQ, kh), lambda h, j: (0, h)), pl.BlockSpec((SEQ, ns), lambda h, j: (0, j))]
        out_spec = pl.BlockSpec((1, 1, kh, ns), lambda h, j: (j, h, 0, 0))
        sem = ("arbitrary", "arbitrary")
    else:
        out = jax.ShapeDtypeStruct((kin, n), MM)
        grid = (kin // tk,)
        in_specs = [pl.BlockSpec((SEQ, tk), lambda r: (0, r)), pl.BlockSpec((SEQ, n), lambda r: (0, 0))]
        out_spec = pl.BlockSpec((tk, n), lambda r: (r, 0))
        sem = ("arbitrary",)
    res = pl.pallas_call(
        body, name=name, grid=grid, out_shape=out, in_specs=in_specs, out_specs=out_spec,
        compiler_params=_params(sem),
    )(a, b)
    if not col_sharded:
        res = res.reshape(N_CHIPS, 2, kin // (2 * N_CHIPS), n)
    return res


def weight_grad_mix(merged, dmix, u3, dco, att, dao):
    operands = (merged, dmix, u3, dco, att, dao)
    n_out, n_br = D_MODEL // 2, CONV_DIM // 2

    def body(*refs):
        hbm, (o_out, o_cb, o_ab), bufs, sems = refs[:6], refs[6:9], refs[9:15], refs[15]
        copies = [pltpu.make_async_copy(hbm[i], bufs[i], sems.at[i]) for i in range(6)]
        for cp in copies:
            cp.start()
        m_ref, dm_ref, u_ref, dco_ref, a_ref, dao_ref = bufs
        copies[0].wait()
        copies[1].wait()
        for h in range(2):
            o_out[h * n_out:(h + 1) * n_out, :] = _dot_tn(m_ref[:, h * n_out:(h + 1) * n_out], dm_ref[...]).astype(MM)
        for br, (a, d, o) in enumerate(((u_ref, dco_ref, o_cb), (a_ref, dao_ref, o_ab))):
            copies[2 + 2 * br].wait()
            copies[3 + 2 * br].wait()
            for h in range(2):
                g = _dot_tn(a[:, h * n_br:(h + 1) * n_br], d[...])
                for j in range(N_CHIPS):
                    o[j, h] = g[:, j * BR_SHARD:(j + 1) * BR_SHARD].astype(MM)

    branch = jax.ShapeDtypeStruct((N_CHIPS, 2, n_br, BR_SHARD), MM)
    dw_out, dw_cb, dw_ab = pl.pallas_call(
        body, name="dw_mix", out_shape=[jax.ShapeDtypeStruct((D_MODEL, D_MODEL), MM), branch, branch],
        in_specs=[ANY] * 6, out_specs=[VMEM_SPEC] * 3,
        scratch_shapes=[pltpu.VMEM(a.shape, a.dtype) for a in operands] + [pltpu.SemaphoreType.DMA((6,))],
        compiler_params=_params(),
    )(*operands)
    return dw_out.reshape(N_CHIPS, 2, D_MODEL // (2 * N_CHIPS), D_MODEL), dw_cb, dw_ab


def _place():
    x, y, c = lax.axis_index("x"), lax.axis_index("y"), lax.axis_index("c")
    chips = [(1 - x, y), (x, 1 - y), (1 - x, 1 - y)]
    return x, y, c, chips


def _rcopy(src, dst, send_sem, recv_sem, dev):
    return pltpu.make_async_remote_copy(src_ref=src, dst_ref=dst, send_sem=send_sem, recv_sem=recv_sem,
                                        device_id=dev, device_id_type=MESH)


class _Gather:
    N_MOVES = 6

    def __init__(self, shapes, w, o, scratch):
        self.n, self.shapes, self.w, self.o = len(w), shapes, w, o
        self.send, self.recv, self.psend, self.precv, self.loc_in, self.loc_out = scratch[:6]
        self.raw, self.stage = scratch[6:6 + self.n], scratch[6 + self.n:]
        x, y, c, self.chips = _place()
        self.c = c
        self.me, k_x, k_y, k_far = 2 * x + y, 2 * (1 - x) + y, 2 * x + (1 - y), 2 * (1 - x) + (1 - y)
        to_x, to_y = (1 - x, y, c), (x, 1 - y, c)
        self.sib = (x, y, 1 - c)
        self.sent_as = [(self.me, 0, to_x), (self.me, 1, to_y), (self.me, 1, to_x), (self.me, 0, to_y),
                        (k_x, 0, to_y), (k_y, 1, to_x)]
        self.arrives_as = [(k_x, 0, to_x), (k_y, 1, to_y), (k_x, 1, to_x), (k_y, 0, to_y),
                           (k_far, 0, to_y), (k_far, 1, to_x)]
        self.sent_on_after = {0: 4, 1: 5}

    @staticmethod
    def scratch(shards):
        n = len(shards)
        sems = pltpu.SemaphoreType.DMA
        m = _Gather.N_MOVES * n
        return ([sems((m,)), sems((m,)), sems((m,)), sems((m,)), sems((n,)), sems((n,))]
                + [pltpu.VMEM(s.shape, s.dtype) for s in shards] + [pltpu.VMEM(s.shape, MM) for s in shards])

    @staticmethod
    def out_shapes(shards):
        return [jax.ShapeDtypeStruct((N_CHIPS,) + s.shape, MM) for s in shards]

    def _rows(self, t, quarter, cc):
        rq = self.shapes[t][0] // 4
        return pl.ds((2 * cc + quarter) * rq, rq)

    def _chip(self, j):
        cx, cy = self.chips[j]
        return 2 * cx + cy, (cx, cy, self.c)

    def local_in(self, t):
        return pltpu.make_async_copy(self.w[t], self.raw[t], self.loc_in.at[t])

    def local_out(self, t):
        return pltpu.make_async_copy(self.stage[t], self.o[t].at[self.me], self.loc_out.at[t])

    def sent(self, i, t):
        k, quarter, dev = self.sent_as[i]
        rows = self._rows(t, quarter, self.c)
        there = self.o[t].at[k, rows, :]
        return _rcopy(self.stage[t].at[rows, :] if i < 4 else there, there,
                      self.send.at[i * self.n + t], self.recv.at[i * self.n + t], dev)

    def arrived(self, i, t):
        k, quarter, dev = self.arrives_as[i]
        blk = self.o[t].at[k, self._rows(t, quarter, self.c), :]
        return _rcopy(blk, blk, self.send.at[i * self.n + t], self.recv.at[i * self.n + t], dev)

    def passed(self, i, t, cc):
        k, quarter, _ = self.arrives_as[i]
        blk = self.o[t].at[k, self._rows(t, quarter, cc), :]
        return _rcopy(blk, blk, self.psend.at[i * self.n + t], self.precv.at[i * self.n + t], self.sib)

    def start(self):
        for t in range(self.n):
            self.local_in(t).start()
        for t in range(self.n):
            self.local_in(t).wait()
            self.stage[t][...] = self.raw[t][...].astype(MM)
            self.local_out(t).start()
        for i in range(4):
            for t in range(self.n):
                self.sent(i, t).start()

    def forward(self):
        for i in range(self.N_MOVES):
            for t in range(self.n):
                self.arrived(i, t).wait_recv()
                if i in self.sent_on_after:
                    self.sent(self.sent_on_after[i], t).start()
                self.passed(i, t, self.c).start()

    def finish(self):
        for i in range(self.N_MOVES):
            for t in range(self.n):
                self.passed(i, t, 1 - self.c).wait_recv()
        for i in range(self.N_MOVES):
            for t in range(self.n):
                self.sent(i, t).wait_send()
                self.passed(i, t, self.c).wait_send()
        for t in range(self.n):
            self.local_out(t).wait()


def all_gather_weights(shards, small, later):
    n, m = len(shards), len(later)
    shapes = [s.shape for s in shards]

    def body(*refs):
        w = refs[:n]
        sm = refs[n]
        lw = refs[n + 1:n + 1 + m]
        o = refs[n + 1 + m:2 * n + 1 + m]
        osm = refs[2 * n + 1 + m]
        lo = refs[2 * n + 2 + m:2 * n + 2 + 2 * m]
        scratch = refs[2 * n + 2 + 2 * m:]
        ssend, srecv, sloc, lsem_in, lsem_out = scratch[:5]
        lraw, lstage = scratch[5:5 + m], scratch[5 + m:5 + 2 * m]
        g = _Gather(shapes, w, o, scratch[5 + 2 * m:])
        own = pltpu.make_async_copy(sm, osm.at[g.me], sloc)
        own.start()
        loads = [pltpu.make_async_copy(lw[t], lraw[t], lsem_in.at[t]) for t in range(m)]
        for cp in loads:
            cp.start()
        g.start()
        small_cps = [_rcopy(sm, osm.at[g.me], ssend.at[j], srecv.at[j], g._chip(j)[1]) for j in range(3)]
        for cp in small_cps:
            cp.start()
        places = []
        for t in range(m):
            loads[t].wait()
            lstage[t][...] = lraw[t][...].astype(MM)
            places.append(pltpu.make_async_copy(lstage[t], lo[t].at[g.me], lsem_out.at[t]))
            places[t].start()
        g.forward()
        g.finish()
        for j in range(3):
            k, dev = g._chip(j)
            _rcopy(sm, osm.at[k], ssend.at[j], srecv.at[j], dev).wait_recv()
            small_cps[j].wait_send()
        own.wait()
        for cp in places:
            cp.wait()

    out_shape = _Gather.out_shapes(shards)
    out_shape.append(jax.ShapeDtypeStruct((N_CHIPS,) + small.shape, small.dtype))
    out_shape += _Gather.out_shapes(later)
    sems = pltpu.SemaphoreType.DMA
    return pl.pallas_call(
        body, name="all_gather_weights", out_shape=out_shape,
        in_specs=[ANY] * (n + 1 + m), out_specs=[ANY] * (n + 1 + m),
        scratch_shapes=[sems((3,)), sems((3,)), sems, sems((m,)), sems((m,))]
                       + [pltpu.VMEM(s.shape, s.dtype) for s in later] + [pltpu.VMEM(s.shape, MM) for s in later]
                       + _Gather.scratch(shards),
        compiler_params=_params(),
    )(*shards, small, *later)


HBM_SPEC = pl.BlockSpec(memory_space=pltpu.HBM)
SEM_SPEC = pl.BlockSpec(memory_space=pltpu.SEMAPHORE)
DATAFLOW = pltpu.SideEffectType.DATAFLOW_SIDE_EFFECTING


def split_start(name, bufs, n_copies, copies):
    nb = len(bufs)

    def body(*refs):
        for cp in copies(refs[:nb], refs[nb], refs[nb + 1]):
            cp.start()
        token = refs[2 * nb + 2]
        token[...] = jnp.zeros_like(token)

    sems = [pltpu.SemaphoreType.DMA((n_copies,))] * 2
    res = pl.pallas_call(
        body, name=name,
        out_shape=sems + [pltpu.HBM(a.shape, a.dtype) for a in bufs] + [jax.ShapeDtypeStruct((8, 128), F32)],
        in_specs=[HBM_SPEC] * nb, out_specs=[SEM_SPEC] * 2 + [HBM_SPEC] * nb + [VMEM_SPEC],
        input_output_aliases={i: 2 + i for i in range(nb)},
        compiler_params=pltpu.CompilerParams(has_side_effects=DATAFLOW),
    )(*[pltpu.with_memory_space_constraint(a, pltpu.HBM) for a in bufs])
    return res[:-1], res[-1]


def split_wait(name, state, after, copies):
    sems, bufs = state[:2], state[2:]
    nb = len(bufs)

    def body(*refs):
        for cp in copies(refs[:nb], refs[nb], refs[nb + 1]):
            cp.wait_send()
            cp.wait_recv()

    return pl.pallas_call(
        body, name=name, out_shape=[pltpu.HBM(a.shape, a.dtype) for a in bufs],
        in_specs=[HBM_SPEC] * nb + [SEM_SPEC] * 2 + [ANY] * len(after), out_specs=[HBM_SPEC] * nb,
        input_output_aliases={i: i for i in range(nb)},
        compiler_params=pltpu.CompilerParams(has_side_effects=DATAFLOW),
    )(*bufs, *sems, *after)


class _Shifted:
    def __init__(self, sems, first):
        self.sems, self.first = sems, first

    @property
    def at(self):
        return self

    def __getitem__(self, i):
        return self.sems.at[self.first + i]


def _scatter_copies(n):
    def copies(refs, send, recv):
        _, _, c, chips = _place()
        return [_rcopy(refs[t].at[2 * cx + cy], refs[n + t].at[j], send.at[3 * t + j], recv.at[3 * t + j], (cx, cy, c))
                for t in range(n) for j, (cx, cy) in enumerate(chips)]
    return copies


def scatter_start(parts, tag):
    lands = [lax.empty((3,) + p.shape[1:], p.dtype) for p in parts]
    return split_start("scatter_start_" + tag, list(parts) + lands, 3 * len(parts), _scatter_copies(len(parts)))


def scatter_wait(state, after, tag):
    n = (len(state) - 2) // 2
    return split_wait("scatter_wait_" + tag, state, after, _scatter_copies(n))[n:]


def _gather_copies(shapes, level):
    n = len(shapes)

    def copies(refs, send, recv):
        x, y, c, chips = _place()
        out = []
        for t in range(n):
            rh = shapes[t][0] // 2
            for j, (cx, cy) in enumerate(chips):
                k, dev = (2 * x + y, (cx, cy, c)) if level == 1 else (2 * cx + cy, (x, y, 1 - c))
                blk = refs[t].at[k, pl.ds(c * rh, rh), :]
                out.append(_rcopy(blk, blk, send.at[3 * t + j], recv.at[3 * t + j], dev))
        return out
    return copies


def _sibling_copies(n, other_half):
    def copies(refs, send, recv):
        x, y, c, _ = _place()
        return [_rcopy(refs[t].at[:, 1 - c] if other_half else refs[t], refs[n + t], send.at[t], recv.at[t],
                       (x, y, 1 - c)) for t in range(n)]
    return copies


def sibling_start(srcs, other_half, tag):
    lands = [lax.empty((a.shape[0],) + a.shape[2:] if other_half else a.shape, a.dtype) for a in srcs]
    return split_start("sibling_start_" + tag, list(srcs) + lands, len(srcs),
                       _sibling_copies(len(srcs), other_half))


def sibling_wait(state, after, other_half, tag):
    n = (len(state) - 2) // 2
    res = split_wait("sibling_wait_" + tag, state, after, _sibling_copies(n, other_half))
    return res[:n], res[n:]


def small_pack(ddw, v512, v1024, loss_parts):
    rows, width = PACK_ROWS, 512
    n512, n1024 = len(VEC512), len(VEC1024)

    def body(*refs):
        ddw_ref = refs[0]
        a_refs = refs[1:1 + n512]
        b_refs = refs[1 + n512:1 + n512 + n1024]
        lp_ref, o_ref, p_ref = refs[1 + n512 + n1024:]
        p_ref[...] = jnp.zeros_like(p_ref)
        p_ref[0:32, :] = ddw_ref[...]
        p_ref[LOSS_ROW:LOSS_ROW + 1, 0:128] = jnp.sum(lp_ref[...], axis=0, keepdims=True) * 0.125
        for i, r in enumerate(a_refs):
            p_ref[32 + i:33 + i, :] = r[...]
        for i, r in enumerate(b_refs):
            base = 32 + n512 + 2 * i
            p_ref[base:base + 1, :] = r[:, 0:512]
            p_ref[base + 1:base + 2, :] = r[:, 512:1024]
        x, y, c, _ = _place()
        o_ref[4 * x + 2 * y + c] = p_ref[...]

    n_in = 2 + n512 + n1024
    return pl.pallas_call(
        body, name="small_pack", out_shape=jax.ShapeDtypeStruct((8, rows, width), F32),
        in_specs=[VMEM_SPEC] * n_in, out_specs=VMEM_SPEC,
        scratch_shapes=[pltpu.VMEM((rows, width), F32)],
    )(ddw, *[v512[n] for n in VEC512], *[v1024[n] for n in VEC1024], loss_parts)


def _small_copies(refs, send, recv):
    x, y, c, _ = _place()
    mine = refs[0].at[4 * x + 2 * y + c]
    peers = [(1 - x if k & 4 else x, 1 - y if k & 2 else y, 1 - c if k & 1 else c) for k in range(1, 8)]
    return [_rcopy(mine, mine, send.at[i], recv.at[i], dev) for i, dev in enumerate(peers)]


def _row_block(r):
    for tr in (512, 352, 256, 128):
        if r % tr == 0:
            return tr
    return r


def add_halves(g, recv, name):
    _, _, r, w = g.shape
    tr = _row_block(r)

    def body(g_ref, r_ref, ob_ref, own_ref):
        k = pl.program_id(1)
        me = 2 * lax.axis_index("x") + lax.axis_index("y")
        t = g_ref[0, 0].astype(F32) + r_ref[0].astype(F32)
        ob_ref[0] = t.astype(MM)
        mine = jnp.where(k == me, t, 0.0)

        @pl.when(k == 0)
        def _():
            own_ref[...] = mine

        @pl.when(k != 0)
        def _():
            own_ref[...] += mine

    return pl.pallas_call(
        body, name=name, grid=(r // tr, N_CHIPS),
        in_specs=[pl.BlockSpec((1, 1, tr, w), lambda i, k: (k, lax.axis_index("c"), i, 0)),
                  pl.BlockSpec((1, tr, w), lambda i, k: (k, i, 0))],
        out_specs=[pl.BlockSpec((1, tr, w), lambda i, k: (k, i, 0)),
                   pl.BlockSpec((tr, w), lambda i, k: (i, 0))],
        out_shape=(jax.ShapeDtypeStruct((N_CHIPS, r, w), MM), jax.ShapeDtypeStruct((r, w), F32)),
        compiler_params=_params(("arbitrary", "arbitrary")),
    )(g, recv)


def sum_parts(own, rin, after, name):
    _, r, w = rin.shape
    tr = _row_block(r)

    def body(o_ref, r_ref, after_ref, out_ref):
        out_ref[...] = ((o_ref[...] + r_ref[0].astype(F32)) + r_ref[1].astype(F32)) + r_ref[2].astype(F32)

    return pl.pallas_call(
        body, name=name, grid=(r // tr,), out_shape=jax.ShapeDtypeStruct((r, w), F32),
        in_specs=[pl.BlockSpec((tr, w), lambda i: (i, 0)), pl.BlockSpec((3, tr, w), lambda i: (0, i, 0)),
                  _full_spec((8, 128))],
        out_specs=pl.BlockSpec((tr, w), lambda i: (i, 0)),
        compiler_params=_params(("arbitrary",)),
    )(own, rin, after)


def _adamw_math(w, g, m, v):
    mn = ADAM_B1 * m + (1.0 - ADAM_B1) * g
    vn = ADAM_B2 * v + (1.0 - ADAM_B2) * (g * g)
    m_hat = mn / (1.0 - ADAM_B1 ** ADAM_STEP)
    v_hat = vn / (1.0 - ADAM_B2 ** ADAM_STEP)
    return -ADAM_LR * (m_hat / (jnp.sqrt(v_hat) + ADAM_EPS) + ADAM_WD * w), mn, vn


def adamw(w, mine, other, m, v, name):
    r, c = w.shape
    rh = r // 2
    tr = _row_block(rh)
    if c >= 1024 and tr % 512 == 0:
        tr = 256
    nb = rh // tr

    def body(w_ref, a_ref, b_ref, m_ref, v_ref, go_ref, d_ref, mo_ref, vo_ref):
        gv = jnp.where(lax.axis_index("c") == pl.program_id(0), a_ref[...], b_ref[...])
        go_ref[...] = gv
        d_ref[...], mo_ref[...], vo_ref[...] = _adamw_math(w_ref[...], gv, m_ref[...], v_ref[...])

    def half(of_sibling):
        def index(h, i):
            owner = lax.axis_index("c")
            owner = 1 - owner if of_sibling else owner
            return jnp.where(h == owner, i, jnp.where(h < owner, 0, nb - 1)), 0
        return pl.BlockSpec((tr, c), index)

    spec = pl.BlockSpec((tr, c), lambda h, i: (h * nb + i, 0))
    out = jax.ShapeDtypeStruct((r, c), F32)
    return pl.pallas_call(
        body, name=name, grid=(2, nb), out_shape=(out, out, out, out),
        in_specs=[spec, half(False), half(True), spec, spec], out_specs=[spec] * 4,
        compiler_params=_params(("arbitrary", "arbitrary")),
    )(w, mine, other, m, v)


def adamw_small(packs, params, after):
    names = list(params)
    flat = [a for n in names for a in params[n]]

    def body(*refs):
        p_ref = refs[0]
        ins = refs[1:1 + 3 * len(names)]
        loss_ref, g_ref = refs[2 + 3 * len(names):4 + 3 * len(names)]
        outs = refs[4 + 3 * len(names):]
        total = p_ref[0]
        for d in range(1, 8):
            total = total + p_ref[d]
        g_ref[...] = total
        loss_ref[...] = g_ref[LOSS_ROW:LOSS_ROW + 1, 0:1]
        me = 2 * lax.axis_index("x") + lax.axis_index("y")
        for i, n in enumerate(names):
            w_ref, m_ref, v_ref = ins[3 * i:3 * i + 3]
            go_ref, d_ref, mo_ref, vo_ref = outs[4 * i:4 * i + 4]
            if n == "conv_dw_w":
                gv = jnp.zeros((CONV_WIDTH, 128), F32)
                for k in range(N_CHIPS):
                    gv = gv + jnp.where(me == k, g_ref[0:CONV_WIDTH, 128 * k:128 * (k + 1)], 0.0)
            elif n in VEC512:
                r0 = 32 + VEC512.index(n)
                gv = g_ref[r0:r0 + 1, :]
            else:
                r0 = 32 + len(VEC512) + 2 * VEC1024.index(n)
                gv = jnp.concatenate([g_ref[r0:r0 + 1, :], g_ref[r0 + 1:r0 + 2, :]], axis=1)
            go_ref[...] = gv
            d_ref[...], mo_ref[...], vo_ref[...] = _adamw_math(w_ref[...], gv, m_ref[...], v_ref[...])

    out_shape = [jax.ShapeDtypeStruct((1, 1), F32), jax.ShapeDtypeStruct(packs.shape[1:], F32)]
    out_shape += [jax.ShapeDtypeStruct(params[n][0].shape, F32) for n in names for _ in range(4)]
    res = pl.pallas_call(
        body, name="adamw_small", out_shape=out_shape,
        in_specs=[VMEM_SPEC] * (2 + len(flat)), out_specs=[VMEM_SPEC] * len(out_shape),
        compiler_params=_params(),
    )(packs, *flat, after)
    return res[0], res[1], {n: res[2 + 4 * i:6 + 4 * i] for i, n in enumerate(names)}


REST = ("w_ffn_up", "w_ffn_down", "w_out", "w_conv_branch", "w_att_branch")
VEC512 = ("conv_dw_b", "conv_ln_g", "conv_ln_b")
VEC1024 = ("norm_mix_pre", "b_conv_branch", "norm_mix_post", "norm_ffn_pre", "norm_ffn_post")
PACK_ROWS = 48
LOSS_ROW = 47


def kernel(x, norm_mix_pre, w_in, conv_dw_w, conv_dw_b, conv_ln_g, conv_ln_b, w_conv_branch, b_conv_branch, w_att_branch, w_out, norm_mix_post, norm_ffn_pre, w_ffn_up, w_ffn_down, norm_ffn_post, loss_target, m_norm_mix_pre, m_w_in, m_conv_dw_w, m_conv_dw_b, m_conv_ln_g, m_conv_ln_b, m_w_conv_branch, m_b_conv_branch, m_w_att_branch, m_w_out, m_norm_mix_post, m_norm_ffn_pre, m_w_ffn_up, m_w_ffn_down, m_norm_ffn_post, v_norm_mix_pre, v_w_in, v_conv_dw_w, v_conv_dw_b, v_conv_ln_g, v_conv_ln_b, v_w_conv_branch, v_b_conv_branch, v_w_att_branch, v_w_out, v_norm_mix_post, v_norm_ffn_pre, v_w_ffn_up, v_w_ffn_down, v_norm_ffn_post):
    weights = dict(norm_mix_pre=norm_mix_pre, w_in=w_in, conv_dw_w=conv_dw_w, conv_dw_b=conv_dw_b, conv_ln_g=conv_ln_g, conv_ln_b=conv_ln_b, w_conv_branch=w_conv_branch, b_conv_branch=b_conv_branch, w_att_branch=w_att_branch, w_out=w_out, norm_mix_post=norm_mix_post, norm_ffn_pre=norm_ffn_pre, w_ffn_up=w_ffn_up, w_ffn_down=w_ffn_down, norm_ffn_post=norm_ffn_post)
    mom = dict(norm_mix_pre=m_norm_mix_pre, w_in=m_w_in, conv_dw_w=m_conv_dw_w, conv_dw_b=m_conv_dw_b, conv_ln_g=m_conv_ln_g, conv_ln_b=m_conv_ln_b, w_conv_branch=m_w_conv_branch, b_conv_branch=m_b_conv_branch, w_att_branch=m_w_att_branch, w_out=m_w_out, norm_mix_post=m_norm_mix_post, norm_ffn_pre=m_norm_ffn_pre, w_ffn_up=m_w_ffn_up, w_ffn_down=m_w_ffn_down, norm_ffn_post=m_norm_ffn_post)
    var = dict(norm_mix_pre=v_norm_mix_pre, w_in=v_w_in, conv_dw_w=v_conv_dw_w, conv_dw_b=v_conv_dw_b, conv_ln_g=v_conv_ln_g, conv_ln_b=v_conv_ln_b, w_conv_branch=v_w_conv_branch, b_conv_branch=v_b_conv_branch, w_att_branch=v_w_att_branch, w_out=v_w_out, norm_mix_post=v_norm_mix_post, norm_ffn_pre=v_norm_ffn_pre, w_ffn_up=v_w_ffn_up, w_ffn_down=v_w_ffn_down, norm_ffn_post=v_norm_ffn_post)
    order = list(weights)
    grads, deltas, new_m, new_v = {}, {}, {}, {}
    xs = x.reshape(SEQ, D_MODEL)
    tgt = loss_target.reshape(SEQ, D_MODEL)
    row = lambda a: a.reshape(1, -1)
    g1, g2, g3, g4 = (row(weights[n]) for n in ("norm_mix_pre", "norm_mix_post", "norm_ffn_pre", "norm_ffn_post"))
    ln_g, ln_b = row(conv_ln_g), row(conv_ln_b)

    summed, from_chips = {}, {}

    def core_sums(names, state, after, tag):
        own, from_sibling = sibling_wait(state, after, True, tag)
        for n, g, r in zip(names, own, from_sibling):
            summed[n] = add_halves(g, r, "add_" + n)

    def chip_sums(names, after):
        return [sum_parts(summed[n][1], from_chips[n], after, "sum_" + n) for n in names]

    def optimize(names, state, after, tag):
        mine, other = sibling_wait(state, after, False, tag)
        for n, a, b in zip(names, mine, other):
            grads[n], deltas[n], new_m[n], new_v[n] = adamw(weights[n], a, b, mom[n], var[n], "adamw_" + n)

    w_in_g, dw_g, *rest = all_gather_weights([w_in], conv_dw_w, [weights[n] for n in REST])
    w_dw_full = jnp.concatenate([dw_g[k] for k in range(N_CHIPS)], axis=1)
    rest_shapes = [weights[n].shape for n in REST]
    state, token = split_start("gather_start", rest, 3 * len(REST), _gather_copies(rest_shapes, 1))
    h1, ci, q, k, v, gc, ga = in_proj_fwd(xs, g1, w_in_g, token)
    u1, u3 = conv_fwd(ci, w_dw_full, row(conv_dw_b), ln_g, ln_b)
    att, rc = attn_fwd(q, k, v)
    rest = split_wait("gather_wait", state, [att], _gather_copies(rest_shapes, 1))
    pass_copies = _gather_copies(rest_shapes[2:] + rest_shapes[:2], 2)
    n_first = 3 * len(REST[2:])
    state, token = split_start("pass_start", rest[2:] + rest[:2], 3 * len(REST), pass_copies)
    passed = split_wait("pass_mix_wait", state, [], lambda *a: pass_copies(*a)[:n_first])
    w_out_g, w_cb_g, w_ab_g = passed[:3]
    w_out_g = w_out_g.reshape(D_MODEL, D_MODEL)
    co, ao, merged, mix, x2, h2 = mix_fwd(u3, att, gc, ga, xs, w_cb_g, row(b_conv_branch), w_ab_g, w_out_g,
                                          g2, g3, token)
    w_up_g, w_down_g = split_wait(
        "pass_ffn_wait", list(state[:2]) + list(passed[3:]), [h2],
        lambda refs, send, recv: _gather_copies(rest_shapes[:2], 2)(refs, _Shifted(send, n_first), _Shifted(recv, n_first)))
    w_down_g = w_down_g.reshape(D_FF, D_MODEL)
    gate, up, act = ffn_up_fwd(h2, w_up_g)
    dff, dy, loss_parts, dg4 = ffn_down_loss(act, w_down_g, x2, tgt, g4)

    dgu = ffn_act_bwd(dff, w_down_g, gate, up)
    dx2, dmix, dg3, dg2 = ffn_in_bwd(dgu, w_up_g, x2, mix, dy, g3, g2)
    ffn_grads = [weight_grad(h2, dgu, "dw_ffn_up", True), weight_grad(act, dff, "dw_ffn_down", False, tk=UP_SHARD)]
    to_ffn, token = sibling_start(ffn_grads, True, "dw_ffn")
    dco, dao, dg, du3, datt, dbcb = merge_bwd(dmix, w_out_g, gc, ga, co, ao, w_cb_g, w_ab_g, token)
    to_mix, token = sibling_start(weight_grad_mix(merged, dmix, u3, dco, att, dao), True, "dw_mix")
    core_sums(REST[:2], to_ffn, [token], "dw_ffn")
    core_sums(REST[2:], to_mix, [summed["w_ffn_down"][1]], "dw_mix")
    state, token = scatter_start([summed[n][0] for n in REST], "rest")
    dci, ddw, dbdw, dlng, dlnb = conv_bwd(du3, u1, ci, w_dw_full, ln_g, ln_b, token)
    dqkv = attn_bwd(q, k, v, datt, rc, token)
    from_chips.update(zip(REST, scatter_wait(state, [dci, dqkv], "rest")))
    dproj = (dci, dqkv, dg)
    to_in, token = sibling_start([weight_grad_in(h1, dproj)], True, "dw_in")
    grad_x, dg1 = in_proj_bwd(dproj, w_in_g, xs, dx2, g1, token)
    v512 = dict(conv_dw_b=dbdw, conv_ln_g=dlng, conv_ln_b=dlnb)
    v1024 = dict(norm_mix_pre=dg1, b_conv_branch=dbcb, norm_mix_post=dg2, norm_ffn_pre=dg3, norm_ffn_post=dg4)
    packs = small_pack(ddw, v512, v1024, loss_parts)
    core_sums(("w_in",), to_in, [packs], "dw_in")
    to_chips = summed["w_in"][0]
    landing = lax.empty((3,) + to_chips.shape[1:], to_chips.dtype)

    def scatter_and_packs(refs, send, recv):
        return (_scatter_copies(1)(refs[:2], send, recv)
                + _small_copies(refs[2:], _Shifted(send, 3), _Shifted(recv, 3)))

    state, token = split_start("scatter_start_w_in", [to_chips, landing, packs], 3 + 7, scatter_and_packs)
    swap_up, token = sibling_start(chip_sums(REST[:1], token), False, "sum_ffn_up")
    swap_rest, token = sibling_start(chip_sums(REST[1:], token), False, "sum_rest")
    optimize(REST[:1], swap_up, [token], "sum_ffn_up")
    optimize(REST[1:], swap_rest, [new_v["w_ffn_up"]], "sum_rest")
    _, from_chips["w_in"], packs = split_wait("scatter_wait_w_in", state, [new_v[n] for n in REST], scatter_and_packs)
    swap_in, token = sibling_start(chip_sums(("w_in",), token), False, "sum_w_in")
    as_rows = lambda n, a: a if n == "conv_dw_w" else a.reshape(1, -1)
    small_names = ("conv_dw_w",) + VEC512 + VEC1024
    loss, gsum, small = adamw_small(
        packs, {n: tuple(as_rows(n, d[n]) for d in (weights, mom, var)) for n in small_names}, token)
    optimize(("w_in",), swap_in, [gsum], "sum_w_in")
    for n in small_names:
        grads[n], deltas[n], new_m[n], new_v[n] = (a.reshape(weights[n].shape) for a in small[n])

    return (loss.reshape(()), grad_x.reshape(1, SEQ, D_MODEL),*[grads[n] for n in order], *[deltas[n] for n in order],
            *[new_m[n] for n in order], *[new_v[n] for n in order])
```

```python
import jax
import jax.numpy as jnp
from jax import lax
from jax.experimental import pallas as pl
from jax.experimental.pallas import tpu as pltpu

F32 = jnp.float32
MM = jnp.bfloat16

SEQ = 2048
D_MODEL = 1024
CONV_DIM = 512
ATT_DIM = 512
CONV_WIDTH = 31
D_FF = 2816
IN_COLS = 2 * CONV_DIM + 3 * ATT_DIM + 2 * D_MODEL
N_CHIPS = 4
IN_SHARD = IN_COLS // N_CHIPS
UP_SHARD = 2 * D_FF // N_CHIPS
BR_SHARD = D_MODEL // N_CHIPS
EPS = 1e-6
ATT_SCALE = 0.125

TM = 256
GLU_ROWS = 256
TQ = 128
CONV_TILE = 64
CONV_WIN = CONV_TILE + 32
VMEM_LIMIT = 56 * 1024 * 1024

ADAM_LR = 0.001
ADAM_B1 = 0.9
ADAM_B2 = 0.999
ADAM_EPS = 1e-08
ADAM_WD = 0.01
ADAM_STEP = 10

MESH = pl.DeviceIdType.MESH
ANY = pl.BlockSpec(memory_space=pl.ANY)
VMEM_SPEC = pl.BlockSpec(memory_space=pltpu.VMEM)

NT_DIMS = (((1,), (1,)), ((), ()))
TN_DIMS = (((0,), (0,)), ((), ()))

IN_PIECES = (("ci", 0, 1024), ("q", 1024, 1536), ("k", 1536, 2048), ("v", 2048, 2560),
             ("gc", 2560, 3584), ("ga", 3584, 4608))


def _params(sem=None, vmem=VMEM_LIMIT):
    return pltpu.CompilerParams(dimension_semantics=sem, vmem_limit_bytes=vmem)


def _dot(a, b):
    return jnp.dot(a, b, preferred_element_type=F32)


def _dot_nt(a, b):
    return lax.dot_general(a, b, NT_DIMS, preferred_element_type=F32)


def _dot_tn(a, b):
    return lax.dot_general(a, b, TN_DIMS, preferred_element_type=F32)


def _sigmoid(x):
    return 1.0 / (1.0 + jnp.exp(-x))


def _rms(x):
    r = lax.rsqrt(jnp.mean(x * x, axis=-1, keepdims=True) + EPS)
    return x * r, r


def _rms_bwd(dy_g, n, r):
    return r * (dy_g - n * jnp.mean(dy_g * n, axis=-1, keepdims=True))


def _row_tile_spec(width, tm=TM):
    return pl.BlockSpec((tm, width), lambda i: (i, 0))


def _full_spec(shape):
    nd = len(shape)
    return pl.BlockSpec(shape, lambda *_: (0,) * nd)


def _weight_spec(shape):
    nd = len(shape)
    return pl.BlockSpec(shape, lambda *_: (0,) * nd, pipeline_mode=pl.Buffered(1))


def _acc_rows(ref, val, first):
    @pl.when(first)
    def _():
        ref[...] = val

    @pl.when(jnp.logical_not(first))
    def _():
        ref[...] += val


TOKEN_SPEC = pl.BlockSpec((8, 128), lambda *_: (0, 0))


def in_proj_fwd(x, g1, w_in_g, after):
    def body(x_ref, g_ref, w_ref, after_ref, h_ref, ci_ref, q_ref, k_ref, v_ref, gc_ref, ga_ref):
        n, _ = _rms(x_ref[...])
        h = (n * g_ref[...]).astype(MM)
        h_ref[...] = h
        outs = dict(ci=ci_ref, q=q_ref, k=k_ref, v=v_ref, gc=gc_ref, ga=ga_ref)
        for j in range(N_CHIPS):
            p = _dot(h, w_ref[j])
            g0 = j * IN_SHARD
            for name, s, e in IN_PIECES:
                lo, hi = max(s, g0), min(e, g0 + IN_SHARD)
                if lo < hi:
                    ref = outs[name]
                    part = p[:, lo - g0:hi - g0]
                    if name == "q":
                        part = part * ATT_SCALE
                    ref[:, lo - s:hi - s] = part.astype(ref.dtype)

    out_shape = [
        jax.ShapeDtypeStruct((SEQ, D_MODEL), MM),
        jax.ShapeDtypeStruct((SEQ, 2 * CONV_DIM), F32),
        jax.ShapeDtypeStruct((SEQ, ATT_DIM), MM),
        jax.ShapeDtypeStruct((SEQ, ATT_DIM), MM),
        jax.ShapeDtypeStruct((SEQ, ATT_DIM), MM),
        jax.ShapeDtypeStruct((SEQ, D_MODEL), F32),
        jax.ShapeDtypeStruct((SEQ, D_MODEL), F32),
    ]
    return pl.pallas_call(
        body, name="in_proj_fwd", grid=(SEQ // TM,), out_shape=out_shape,
        in_specs=[_row_tile_spec(D_MODEL), _full_spec((1, D_MODEL)), _weight_spec(w_in_g.shape), TOKEN_SPEC],
        out_specs=[_row_tile_spec(s.shape[1]) for s in out_shape],
        compiler_params=_params(("arbitrary",)),
    )(x, g1, w_in_g, after)


LANE_GROUPS = [slice(g, g + 128) for g in range(0, CONV_DIM, 128)]
NORM_ROWS = 16


def _shifted_windows(src_ref, t0, cols, offsets):
    win = src_ref[pl.ds(t0, CONV_WIN), cols]
    for rot in range(8):
        ms = [m for m in offsets if m % 8 == rot]
        if ms:
            shifted = win if rot == 0 else pltpu.roll(win, CONV_WIN - rot, 0)
            for m in ms:
                yield m, shifted[m - rot:m - rot + CONV_TILE, :]


def _shifted_sum(src_ref, t0, cols, w_ref, offset_of_tap):
    tap_at = {offset_of_tap(j): j for j in range(CONV_WIDTH)}
    acc = None
    for m, rows in _shifted_windows(src_ref, t0, cols, sorted(tap_at)):
        t = w_ref[tap_at[m]:tap_at[m] + 1, cols] * rows
        acc = t if acc is None else acc + t
    return acc


def _glu_into(ci_ref, upad_ref):
    upad_ref[0:32, :] = jnp.zeros((32, CONV_DIM), F32)

    def step(i, c):
        t0 = pl.multiple_of(i * GLU_ROWS, GLU_ROWS)
        a = ci_ref[pl.ds(t0, GLU_ROWS), 0:CONV_DIM]
        b = ci_ref[pl.ds(t0, GLU_ROWS), CONV_DIM:2 * CONV_DIM]
        upad_ref[pl.ds(t0 + 32, GLU_ROWS), :] = a * _sigmoid(b)
        return c

    lax.fori_loop(0, SEQ // GLU_ROWS, step, 0)


def _layernorm_parts(u1):
    mu = jnp.mean(u1, axis=-1, keepdims=True)
    xc = u1 - mu
    rstd = lax.rsqrt(jnp.mean(xc * xc, axis=-1, keepdims=True) + EPS)
    return xc * rstd, rstd


def conv_fwd(ci, w_dw, b_dw, ln_g, ln_b):
    def body(ci_ref, w_ref, b_ref, g_ref, bb_ref, u1_ref, u3_ref, upad_ref):
        _glu_into(ci_ref, upad_ref)

        def step(i, c):
            t0 = pl.multiple_of(i * CONV_TILE, CONV_TILE)
            for cols in LANE_GROUPS:
                u1_ref[pl.ds(t0, CONV_TILE), cols] = (_shifted_sum(upad_ref, t0, cols, w_ref, lambda j: j + 2)
                                                      + b_ref[:, cols])
            for r in range(0, CONV_TILE, NORM_ROWS):
                rows = pl.ds(t0 + r, NORM_ROWS)
                xh, _ = _layernorm_parts(u1_ref[rows, :])
                u2 = xh * g_ref[...] + bb_ref[...]
                u3_ref[rows, :] = (u2 * _sigmoid(u2)).astype(MM)
            return c

        lax.fori_loop(0, SEQ // CONV_TILE, step, 0)

    return pl.pallas_call(
        body, name="conv_fwd",
        out_shape=[jax.ShapeDtypeStruct((SEQ, CONV_DIM), F32), jax.ShapeDtypeStruct((SEQ, CONV_DIM), MM)],
        in_specs=[VMEM_SPEC] * 5, out_specs=[VMEM_SPEC] * 2,
        scratch_shapes=[pltpu.VMEM((SEQ + 32, CONV_DIM), F32)],
        compiler_params=_params(),
    )(ci, w_dw, b_dw, ln_g, ln_b)


def _softplus(z):
    return jnp.maximum(z, 0.0) + jnp.log(1.0 + jnp.exp(-jnp.abs(z)))


def _cumsum_weights(suffix, with_total):
    n = 256 if with_total else 128
    r = lax.broadcasted_iota(jnp.int32, (128, n), 0)
    c = lax.broadcasted_iota(jnp.int32, (128, n), 1)
    tri = (r >= c) if suffix else (r <= c)
    return jnp.logical_or(tri, c >= 128).astype(MM)


NO_SCORE = -1e30
N_KB = SEQ // TQ


def _score_bias(lane, row, i, j):
    keep = jnp.logical_and(i >= 0, jnp.logical_or(j < i, lane < row))
    return jnp.where(keep, 0.0, NO_SCORE)


def _block_pipeline(n_stages, descending, step, on_query_block=None):
    n_lag = n_stages - 1
    none = jnp.int32(-1)

    def shift(cur, lag):
        step([cur] + [(lag[2 * s], lag[2 * s + 1]) for s in range(n_lag)])
        return (cur[0], cur[1]) + tuple(lag[:-2])

    def outer(i, lag):
        if on_query_block is not None:
            on_query_block(i)

        def inner(n, lag):
            return shift((i, i - n if descending else n), lag)
        return lax.fori_loop(0, i + 1, inner, lag)

    lag = lax.fori_loop(0, N_KB, outer, (none,) * (2 * n_lag))
    lax.fori_loop(0, n_lag, lambda n, lag: shift((none, none), lag), lag)


def _head_masks():
    lane = lax.broadcasted_iota(jnp.int32, (TQ, 128), 1)
    row = lax.broadcasted_iota(jnp.int32, (TQ, 128), 0)
    return lane, row, lane < 64


def _pick_head(x, head0, h):
    zero = jnp.zeros_like(x)
    return jnp.where(head0, x, zero) if h == 0 else jnp.where(head0, zero, x)


N_PAIRS = ATT_DIM // 128


def _split_heads(src_ref, dst_ref):
    _, _, head0 = _head_masks()

    def block(b, c):
        r0 = pl.multiple_of(b * TQ, TQ)
        d0 = pl.multiple_of(b * 2 * TQ, 2 * TQ)
        for p in range(N_PAIRS):
            x = src_ref[pl.ds(r0, TQ), 128 * p:128 * (p + 1)]
            for h in range(2):
                dst_ref[p, pl.ds(d0 + TQ * h, TQ), :] = _pick_head(x, head0, h)
        return c

    lax.fori_loop(0, N_KB, block, 0)


def attn_fwd(q, k, v):
    def body(q_ref, k_ref, v_ref, o_ref, rc_ref, acc_ref, r_ref, z_ref, spb_ref, ab_ref, qm_ref, vm_ref):
        lane, row, _ = _head_masks()
        w = _cumsum_weights(suffix=True, with_total=True)
        _split_heads(q_ref, qm_ref)
        _split_heads(v_ref, vm_ref)
        acc_ref[...] = jnp.zeros_like(acc_ref)
        r_ref[...] = jnp.zeros_like(r_ref)
        rc_ref[...] = jnp.zeros_like(rc_ref)
        z_ref[...] = jnp.full(z_ref.shape, NO_SCORE, F32)
        spb_ref[...] = jnp.zeros_like(spb_ref)
        ab_ref[...] = jnp.zeros_like(ab_ref)

        def step(pairs):
            (i1, j1), (i2, j2), (i3, j3) = pairs
            k1, q2, q3 = (pl.multiple_of(jnp.maximum(b, 0) * TQ, TQ) for b in (j1, i2, i3))
            q1, k3 = (pl.multiple_of(jnp.maximum(b, 0) * 2 * TQ, 2 * TQ) for b in (i1, j3))
            bias1 = _score_bias(lane, row, i1, j1)
            first2 = j2 == i2
            rc_rows = rc_ref[pl.ds(q2, TQ), :]
            for p in range(N_PAIRS):
                cols = slice(128 * p, 128 * (p + 1))
                kb = k_ref[pl.ds(k1, TQ), cols]
                acc_ref[pl.ds(q3, TQ), cols] += _dot(ab_ref[p], vm_ref[p, pl.ds(k3, 2 * TQ), :])
                for h in range(2):
                    hh = 2 * p + h
                    r = _dot(spb_ref[hh], w)
                    r_in = jnp.where(first2, 0.0, r_ref[hh])
                    ab_ref[p, :, 128 * h:128 * (h + 1)] = jnp.exp(z_ref[hh] - (r[:, :128] + r_in)).astype(MM)
                    rc_rows = jnp.where(jnp.logical_and(lane == 16 * hh + j2, i2 >= 0), r_in, rc_rows)
                    r_ref[hh] = r_in + r[:, 128:]
                    z = _dot_nt(qm_ref[p, pl.ds(q1 + TQ * h, TQ), :], kb) + bias1
                    z_ref[hh] = z
                    spb_ref[hh] = _softplus(z).astype(MM)
            rc_ref[pl.ds(q2, TQ), :] = rc_rows

        _block_pipeline(3, True, step)
        o_ref[...] = acc_ref[...].astype(MM)

    return pl.pallas_call(
        body, name="attn_fwd",
        out_shape=[jax.ShapeDtypeStruct((SEQ, ATT_DIM), MM), jax.ShapeDtypeStruct((SEQ, 128), F32)],
        in_specs=[VMEM_SPEC] * 3, out_specs=[VMEM_SPEC] * 2,
        scratch_shapes=[pltpu.VMEM((SEQ, ATT_DIM), F32), pltpu.VMEM((8, TQ, 128), F32),
                        pltpu.VMEM((8, TQ, 128), F32), pltpu.VMEM((8, TQ, 128), MM),
                        pltpu.VMEM((N_PAIRS, TQ, 256), MM), pltpu.VMEM((N_PAIRS, 2 * SEQ, 128), MM),
                        pltpu.VMEM((N_PAIRS, 2 * SEQ, 128), MM)],
        compiler_params=_params(),
    )(q, k, v)


def mix_fwd(u3, att, gc, ga, x, w_cb_g, b_cb, w_ab_g, w_out_g, g2, g3, after):
    def body(u_ref, a_ref, gc_ref, ga_ref, x_ref, wcb_ref, bcb_ref, wab_ref, wout_ref, g2_ref, g3_ref, after_ref,
             co_ref, ao_ref, mg_ref, mix_ref, x2_ref, h2_ref):
        u = u_ref[...]
        a = a_ref[...]
        co = jnp.concatenate([_dot(u, wcb_ref[j]) for j in range(N_CHIPS)], axis=1) + bcb_ref[...]
        ao = jnp.concatenate([_dot(a, wab_ref[j]) for j in range(N_CHIPS)], axis=1)
        co_ref[...] = co.astype(MM)
        ao_ref[...] = ao.astype(MM)
        merged = (_sigmoid(gc_ref[...]) * co + _sigmoid(ga_ref[...]) * ao).astype(MM)
        mg_ref[...] = merged
        mix = _dot(merged, wout_ref[...])
        mix_ref[...] = mix
        n2, _ = _rms(mix)
        x2 = x_ref[...] + n2 * g2_ref[...]
        x2_ref[...] = x2
        n3, _ = _rms(x2)
        h2_ref[...] = (n3 * g3_ref[...]).astype(MM)

    out_shape = [
        jax.ShapeDtypeStruct((SEQ, D_MODEL), MM), jax.ShapeDtypeStruct((SEQ, D_MODEL), MM),
        jax.ShapeDtypeStruct((SEQ, D_MODEL), MM), jax.ShapeDtypeStruct((SEQ, D_MODEL), F32),
        jax.ShapeDtypeStruct((SEQ, D_MODEL), F32), jax.ShapeDtypeStruct((SEQ, D_MODEL), MM),
    ]
    vec = _full_spec((1, D_MODEL))
    return pl.pallas_call(
        body, name="mix_fwd", grid=(SEQ // TM,), out_shape=out_shape,
        in_specs=[_row_tile_spec(CONV_DIM), _row_tile_spec(ATT_DIM), _row_tile_spec(D_MODEL),
                  _row_tile_spec(D_MODEL), _row_tile_spec(D_MODEL), _weight_spec(w_cb_g.shape), vec,
                  _weight_spec(w_ab_g.shape), _weight_spec(w_out_g.shape), vec, vec, TOKEN_SPEC],
        out_specs=[_row_tile_spec(D_MODEL)] * 6,
        compiler_params=_params(("arbitrary",)),
    )(u3, att, gc, ga, x, w_cb_g, b_cb, w_ab_g, w_out_g, g2, g3, after)


def ffn_up_fwd(h2, w_up_g):
    def body(h_ref, wg_ref, wu_ref, gate_ref, up_ref, act_ref):
        h = h_ref[...]
        gate = _dot(h, wg_ref[0])
        up = _dot(h, wu_ref[0])
        gate_ref[...] = gate.astype(MM)
        up_ref[...] = up.astype(MM)
        act_ref[...] = (gate * _sigmoid(gate) * up).astype(MM)

    tile = pl.BlockSpec((TM, UP_SHARD), lambda n, i: (i, n))
    act = jax.ShapeDtypeStruct((SEQ, D_FF), MM)
    return pl.pallas_call(
        body, name="ffn_up_fwd", grid=(2, SEQ // TM), out_shape=[act, act, act],
        in_specs=[pl.BlockSpec((TM, D_MODEL), lambda n, i: (i, 0)),
                  pl.BlockSpec((1, D_MODEL, UP_SHARD), lambda n, i: (n, 0, 0)),
                  pl.BlockSpec((1, D_MODEL, UP_SHARD), lambda n, i: (n + 2, 0, 0))],
        out_specs=[tile, tile, tile],
        compiler_params=_params(("arbitrary", "arbitrary")),
    )(h2, w_up_g, w_up_g)


def ffn_down_loss(act, w_down_g, x2, target, g4):
    def body(act_ref, wd_ref, x2_ref, t_ref, g_ref, dff_ref, dy_ref, loss_ref, dg_ref):
        ff = _dot(act_ref[...], wd_ref[...])
        n4, r4 = _rms(ff)
        g4v = g_ref[...]
        err = x2_ref[...] + n4 * g4v - t_ref[...]
        row_loss = jnp.mean(err * err, axis=-1, keepdims=True)
        loss_ref[...] = jnp.zeros((8, 128), F32) + 0.5 * jnp.sum(row_loss, axis=0, keepdims=True)
        dy = err * (1.0 / D_MODEL)
        dy_ref[...] = dy
        dff_ref[...] = _rms_bwd(dy * g4v, n4, r4).astype(MM)
        _acc_rows(dg_ref, jnp.sum(dy * n4, axis=0, keepdims=True), pl.program_id(0) == 0)

    nt = SEQ // TM
    vec = _full_spec((1, D_MODEL))
    return pl.pallas_call(
        body, name="ffn_down_loss", grid=(nt,),
        out_shape=(jax.ShapeDtypeStruct((SEQ, D_MODEL), MM), jax.ShapeDtypeStruct((SEQ, D_MODEL), F32),
                   jax.ShapeDtypeStruct((nt * 8, 128), F32), jax.ShapeDtypeStruct((1, D_MODEL), F32)),
        in_specs=[_row_tile_spec(D_FF), _weight_spec(w_down_g.shape), _row_tile_spec(D_MODEL),
                  _row_tile_spec(D_MODEL), vec],
        out_specs=[_row_tile_spec(D_MODEL), _row_tile_spec(D_MODEL),
                   pl.BlockSpec((8, 128), lambda i: (i, 0)), vec],
        compiler_params=_params(("arbitrary",)),
    )(act, w_down_g, x2, target, g4)


def ffn_act_bwd(dff, w_down_g, gate, up):
    def body(dff_ref, wd_ref, gate_ref, up_ref, dgu_ref):
        dact = _dot_nt(dff_ref[...], wd_ref[...])
        gate = gate_ref[...].astype(F32)
        sg = _sigmoid(gate)
        dgu_ref[:, 0:D_FF] = (dact * up_ref[...].astype(F32) * (sg * (1.0 + gate * (1.0 - sg)))).astype(MM)
        dgu_ref[:, D_FF:2 * D_FF] = (dact * (gate * sg)).astype(MM)

    return pl.pallas_call(
        body, name="ffn_act_bwd", grid=(SEQ // TM,),
        out_shape=jax.ShapeDtypeStruct((SEQ, 2 * D_FF), MM),
        in_specs=[_row_tile_spec(D_MODEL), _weight_spec(w_down_g.shape), _row_tile_spec(D_FF), _row_tile_spec(D_FF)],
        out_specs=_row_tile_spec(2 * D_FF),
        compiler_params=_params(("arbitrary",)),
    )(dff, w_down_g, gate, up)


def ffn_in_bwd(dgu, w_up_g, x2, mix, dy, g3, g2):
    def body(dgu_ref, w_ref, x2_ref, mix_ref, dy_ref, g3_ref, g2_ref, dx2_ref, dmix_ref, dg3_ref, dg2_ref):
        dh2 = None
        for j in range(N_CHIPS):
            t = _dot_nt(dgu_ref[:, j * UP_SHARD:(j + 1) * UP_SHARD], w_ref[j])
            dh2 = t if dh2 is None else dh2 + t
        first = pl.program_id(0) == 0
        n3, r3 = _rms(x2_ref[...])
        dx2 = dy_ref[...] + _rms_bwd(dh2 * g3_ref[...], n3, r3)
        dx2_ref[...] = dx2
        _acc_rows(dg3_ref, jnp.sum(dh2 * n3, axis=0, keepdims=True), first)
        n2, r2 = _rms(mix_ref[...])
        dmix_ref[...] = _rms_bwd(dx2 * g2_ref[...], n2, r2).astype(MM)
        _acc_rows(dg2_ref, jnp.sum(dx2 * n2, axis=0, keepdims=True), first)

    vec = _full_spec((1, D_MODEL))
    return pl.pallas_call(
        body, name="ffn_in_bwd", grid=(SEQ // TM,),
        out_shape=(jax.ShapeDtypeStruct((SEQ, D_MODEL), F32), jax.ShapeDtypeStruct((SEQ, D_MODEL), MM),
                   jax.ShapeDtypeStruct((1, D_MODEL), F32), jax.ShapeDtypeStruct((1, D_MODEL), F32)),
        in_specs=[_row_tile_spec(2 * D_FF), _weight_spec(w_up_g.shape), _row_tile_spec(D_MODEL),
                  _row_tile_spec(D_MODEL), _row_tile_spec(D_MODEL), vec, vec],
        out_specs=[_row_tile_spec(D_MODEL), _row_tile_spec(D_MODEL), vec, vec],
        compiler_params=_params(("arbitrary",)),
    )(dgu, w_up_g, x2, mix, dy, g3, g2)


def merge_bwd(dmix, w_out_g, gc, ga, co, ao, w_cb_g, w_ab_g, after):
    def body(dmix_ref, wout_ref, gc_ref, ga_ref, co_ref, ao_ref, wcb_ref, wab_ref, after_ref,
             dco_ref, dao_ref, dg_ref, du3_ref, datt_ref, dbcb_ref):
        dm = _dot_nt(dmix_ref[...], wout_ref[...])
        sgc = _sigmoid(gc_ref[...])
        sga = _sigmoid(ga_ref[...])
        dco = dm * sgc
        dao = dm * sga
        dg_ref[:, 0:D_MODEL] = (dm * co_ref[...].astype(F32) * (sgc * (1.0 - sgc))).astype(MM)
        dg_ref[:, D_MODEL:2 * D_MODEL] = (dm * ao_ref[...].astype(F32) * (sga * (1.0 - sga))).astype(MM)
        _acc_rows(dbcb_ref, jnp.sum(dco, axis=0, keepdims=True), pl.program_id(0) == 0)
        dco_ref[...] = dco.astype(MM)
        dao_ref[...] = dao.astype(MM)
        du3 = None
        datt = None
        for j in range(N_CHIPS):
            cols = slice(j * BR_SHARD, (j + 1) * BR_SHARD)
            t = _dot_nt(dco_ref[:, cols], wcb_ref[j])
            s = _dot_nt(dao_ref[:, cols], wab_ref[j])
            du3 = t if du3 is None else du3 + t
            datt = s if datt is None else datt + s
        du3_ref[...] = du3
        datt_ref[...] = datt.astype(MM)

    wide = _row_tile_spec(D_MODEL)
    return pl.pallas_call(
        body, name="merge_bwd", grid=(SEQ // TM,),
        out_shape=(jax.ShapeDtypeStruct((SEQ, D_MODEL), MM), jax.ShapeDtypeStruct((SEQ, D_MODEL), MM),
                   jax.ShapeDtypeStruct((SEQ, 2 * D_MODEL), MM),
                   jax.ShapeDtypeStruct((SEQ, CONV_DIM), F32), jax.ShapeDtypeStruct((SEQ, ATT_DIM), MM),
                   jax.ShapeDtypeStruct((1, D_MODEL), F32)),
        in_specs=[wide, _weight_spec(w_out_g.shape), wide, wide, wide, wide,
                  _weight_spec(w_cb_g.shape), _weight_spec(w_ab_g.shape), TOKEN_SPEC],
        out_specs=[wide, wide, _row_tile_spec(2 * D_MODEL), _row_tile_spec(CONV_DIM), _row_tile_spec(ATT_DIM),
                   _full_spec((1, D_MODEL))],
        compiler_params=_params(("arbitrary",)),
    )(dmix, w_out_g, gc, ga, co, ao, w_cb_g, w_ab_g, after)


def conv_bwd(du3, u1, ci, w_dw, ln_g, ln_b, after):
    def body(du3_ref, u1_ref, ci_ref, w_ref, g_ref, bb_ref, after_ref,
             dci_ref, dw_ref, dbdw_ref, dg_ref, db_ref, upad_ref, dpad_ref, dwacc_ref, vacc_ref):
        _glu_into(ci_ref, upad_ref)
        dpad_ref[SEQ:SEQ + 32, :] = jnp.zeros((32, CONV_DIM), F32)
        dwacc_ref[...] = jnp.zeros_like(dwacc_ref)
        vacc_ref[...] = jnp.zeros_like(vacc_ref)

        def fold8(t):
            s = t[0:8, :]
            for r in range(8, t.shape[0], 8):
                s = s + t[r:r + 8, :]
            return s

        def pass1(i, c):
            t0 = pl.multiple_of(i * CONV_TILE, CONV_TILE)
            gv = g_ref[...]
            for r in range(0, CONV_TILE, NORM_ROWS):
                rows = pl.ds(t0 + r, NORM_ROWS)
                xh, rstd = _layernorm_parts(u1_ref[rows, :])
                u2 = xh * gv + bb_ref[...]
                s2 = _sigmoid(u2)
                du2 = du3_ref[rows, :] * (s2 * (1.0 + u2 * (1.0 - s2)))
                wv = du2 * gv
                du1 = rstd * (wv - jnp.mean(wv, axis=-1, keepdims=True)
                              - xh * jnp.mean(wv * xh, axis=-1, keepdims=True))
                dpad_ref[rows, :] = du1
                vacc_ref[0] += fold8(du2 * xh)
                vacc_ref[1] += fold8(du2)
                vacc_ref[2] += fold8(du1)
            for cols in LANE_GROUPS:
                du1 = dpad_ref[pl.ds(t0, CONV_TILE), cols]
                for m, rows in _shifted_windows(upad_ref, t0, cols, range(2, CONV_WIDTH + 2)):
                    dwacc_ref[m - 2, :, cols] += fold8(du1 * rows)
            return c

        lax.fori_loop(0, SEQ // CONV_TILE, pass1, 0)

        def pass2(i, c):
            t0 = pl.multiple_of(i * CONV_TILE, CONV_TILE)
            tile = pl.ds(t0, CONV_TILE)
            for cols in LANE_GROUPS:
                gate_cols = slice(cols.start + CONV_DIM, cols.stop + CONV_DIM)
                du0 = _shifted_sum(dpad_ref, t0, cols, w_ref, lambda j: 30 - j)
                a = ci_ref[tile, cols]
                sb = _sigmoid(ci_ref[tile, gate_cols])
                dci_ref[tile, cols] = (du0 * sb).astype(MM)
                dci_ref[tile, gate_cols] = (du0 * a * (sb * (1.0 - sb))).astype(MM)
            return c

        lax.fori_loop(0, SEQ // CONV_TILE, pass2, 0)

        for j in range(CONV_WIDTH):
            dw_ref[j:j + 1, :] = jnp.sum(dwacc_ref[j], axis=0, keepdims=True)
        dw_ref[CONV_WIDTH:32, :] = jnp.zeros((32 - CONV_WIDTH, CONV_DIM), F32)
        dg_ref[...] = jnp.sum(vacc_ref[0], axis=0, keepdims=True)
        db_ref[...] = jnp.sum(vacc_ref[1], axis=0, keepdims=True)
        dbdw_ref[...] = jnp.sum(vacc_ref[2], axis=0, keepdims=True)

    vec = jax.ShapeDtypeStruct((1, CONV_DIM), F32)
    return pl.pallas_call(
        body, name="conv_bwd",
        out_shape=(jax.ShapeDtypeStruct((SEQ, 2 * CONV_DIM), MM), jax.ShapeDtypeStruct((32, CONV_DIM), F32),
                   vec, vec, vec),
        in_specs=[VMEM_SPEC] * 7, out_specs=[VMEM_SPEC] * 5,
        scratch_shapes=[pltpu.VMEM((SEQ + 32, CONV_DIM), F32), pltpu.VMEM((SEQ + 32, CONV_DIM), F32),
                        pltpu.VMEM((CONV_WIDTH, 8, CONV_DIM), F32), pltpu.VMEM((3, 8, CONV_DIM), F32)],
        compiler_params=_params(),
    )(du3, u1, ci, w_dw, ln_g, ln_b, after)


def attn_bwd(q, k, v, datt, rc, after):
    def body(q_ref, k_ref, v_ref, do_ref, rc_ref, after_ref, dqkv_ref, dqa_ref, dka_ref, dva_ref, pc_ref, z_ref,
             sig1_ref, sig2_ref, g_ref, spb_ref, gb_ref, ar_ref, dzr_ref, dzc_ref, qm_ref, km_ref, dom_ref):
        lane, row, _ = _head_masks()
        _split_heads(q_ref, qm_ref)
        _split_heads(k_ref, km_ref)
        _split_heads(do_ref, dom_ref)
        for ref in (dqa_ref, dka_ref, dva_ref, pc_ref):
            ref[...] = jnp.zeros_like(ref)
        z_ref[...] = jnp.full(z_ref.shape, NO_SCORE, F32)
        for ref in (sig1_ref, sig2_ref, spb_ref, ar_ref, g_ref, gb_ref, dzr_ref, dzc_ref):
            ref[...] = jnp.zeros_like(ref)
        w_suffix = _cumsum_weights(suffix=True, with_total=False)
        w_prefix = _cumsum_weights(suffix=False, with_total=True)

        def step(pairs):
            (ia, ja), (ib, jb), (ic, jc), (id_, jd) = pairs
            ka, qb_, kb_, kc, qd, kd = (pl.multiple_of(jnp.maximum(b, 0) * TQ, TQ) for b in (ja, ib, jb, jc, id_, jd))
            qa2, qb2, qc2, qd2, kd2 = (pl.multiple_of(jnp.maximum(b, 0) * 2 * TQ, 2 * TQ)
                                       for b in (ia, ib, ic, id_, jd))
            bias_a = _score_bias(lane, row, ia, ja)
            rc_rows = rc_ref[pl.ds(qb_, TQ), :]
            first_c = jc == 0
            for p in range(N_PAIRS):
                cols = slice(128 * p, 128 * (p + 1))
                k_a = k_ref[pl.ds(ka, TQ), cols]
                v_b = v_ref[pl.ds(kb_, TQ), cols]
                dqa_ref[pl.ds(qd, TQ), cols] += _dot(dzc_ref[p], km_ref[p, pl.ds(kd2, 2 * TQ), :])
                dka_ref[pl.ds(kd, TQ), cols] += _dot_tn(dzr_ref[p], qm_ref[p, pl.ds(qd2, 2 * TQ), :])
                dva_ref[pl.ds(kc, TQ), cols] += _dot_tn(ar_ref[p], dom_ref[p, pl.ds(qc2, 2 * TQ), :])
                for h in range(2):
                    hh = 2 * p + h
                    rows = slice(TQ * h, TQ * (h + 1))
                    r = _dot(gb_ref[hh], w_prefix)
                    p_in = jnp.where(first_c, 0.0, pc_ref[hh])
                    dz = (g_ref[hh] - sig2_ref[hh] * (r[:, :128] + p_in)).astype(MM)
                    dzc_ref[p, :, rows] = dz
                    dzr_ref[p, rows, :] = dz
                    pc_ref[hh] = p_in + r[:, 128:]
                    r_in = jnp.sum(jnp.where(lane == 16 * hh + jb, rc_rows, 0.0), axis=1, keepdims=True)
                    a = jnp.exp(z_ref[hh] - (_dot(spb_ref[hh], w_suffix) + r_in))
                    g = _dot_nt(dom_ref[p, pl.ds(qb2 + TQ * h, TQ), :], v_b) * a
                    ar_ref[p, rows, :] = a.astype(MM)
                    g_ref[hh] = g
                    gb_ref[hh] = g.astype(MM)
                    sig2_ref[hh] = sig1_ref[hh]
                    z = _dot_nt(qm_ref[p, pl.ds(qa2 + TQ * h, TQ), :], k_a) + bias_a
                    sp = _softplus(z)
                    sig1_ref[hh] = jnp.exp(z - sp)
                    z_ref[hh] = z
                    spb_ref[hh] = sp.astype(MM)

        _block_pipeline(4, False, step)
        dqkv_ref[:, 0:ATT_DIM] = (dqa_ref[...] * ATT_SCALE).astype(MM)
        dqkv_ref[:, ATT_DIM:2 * ATT_DIM] = dka_ref[...].astype(MM)
        dqkv_ref[:, 2 * ATT_DIM:3 * ATT_DIM] = dva_ref[...].astype(MM)

    split = pltpu.VMEM((N_PAIRS, 2 * SEQ, 128), MM)
    return pl.pallas_call(
        body, name="attn_bwd", out_shape=jax.ShapeDtypeStruct((SEQ, 3 * ATT_DIM), MM),
        in_specs=[VMEM_SPEC] * 6, out_specs=VMEM_SPEC,
        scratch_shapes=[pltpu.VMEM((SEQ, ATT_DIM), F32)] * 3 + [pltpu.VMEM((8, TQ, 128), F32)] * 5
                       + [pltpu.VMEM((8, TQ, 128), MM)] * 2
                       + [pltpu.VMEM((N_PAIRS, 2 * TQ, 128), MM)] * 2 + [pltpu.VMEM((N_PAIRS, TQ, 256), MM)]
                       + [split] * 3,
        compiler_params=_params(),
    )(q, k, v, datt, rc, after)


DPROJ_PIECES = ((0, 1024), (1024, 2560), (2560, 4608))


def _dproj_segments(j):
    g0, g1 = j * IN_SHARD, (j + 1) * IN_SHARD
    segs = []
    for p, (s, e) in enumerate(DPROJ_PIECES):
        lo, hi = max(s, g0), min(e, g1)
        if lo < hi:
            segs.append((p, lo - s, lo - g0, hi - lo))
    return segs


def in_proj_bwd(pieces, w_in_g, x, dx2, g1, after):
    def body(p0_ref, p1_ref, p2_ref, w_ref, x_ref, dx2_ref, g_ref, after_ref, dx_ref, dg_ref):
        p_refs = (p0_ref, p1_ref, p2_ref)
        dh = None
        for j in range(N_CHIPS):
            for p, lo, off, width in _dproj_segments(j):
                t = _dot_nt(p_refs[p][:, lo:lo + width], w_ref[j, :, off:off + width])
                dh = t if dh is None else dh + t
        n1, r1 = _rms(x_ref[...])
        dx_ref[...] = dx2_ref[...] + _rms_bwd(dh * g_ref[...], n1, r1)
        _acc_rows(dg_ref, jnp.sum(dh * n1, axis=0, keepdims=True), pl.program_id(0) == 0)

    vec = _full_spec((1, D_MODEL))
    return pl.pallas_call(
        body, name="in_proj_bwd", grid=(SEQ // TM,),
        out_shape=[jax.ShapeDtypeStruct((SEQ, D_MODEL), F32), jax.ShapeDtypeStruct((1, D_MODEL), F32)],
        in_specs=[_row_tile_spec(p.shape[1]) for p in pieces]
                 + [_weight_spec(w_in_g.shape), _row_tile_spec(D_MODEL), _row_tile_spec(D_MODEL), vec, TOKEN_SPEC],
        out_specs=[_row_tile_spec(D_MODEL), vec],
        compiler_params=_params(("arbitrary",)),
    )(*pieces, w_in_g, x, dx2, g1, after)


def weight_grad_in(h1, pieces):
    kh = D_MODEL // 2

    def body(a_ref, p0_ref, p1_ref, p2_ref, o_ref):
        p_refs = (p0_ref, p1_ref, p2_ref)
        a = a_ref[...]
        for j in range(N_CHIPS):
            @pl.when(pl.program_id(1) == j)
            def _():
                for p, lo, off, width in _dproj_segments(j):
                    o_ref[0, 0, :, off:off + width] = _dot_tn(a, p_refs[p][:, lo:lo + width]).astype(MM)

    return pl.pallas_call(
        body, name="dw_in", grid=(2, N_CHIPS), out_shape=jax.ShapeDtypeStruct((N_CHIPS, 2, kh, IN_SHARD), MM),
        in_specs=[pl.BlockSpec((SEQ, kh), lambda h, j: (0, h))] + [_weight_spec(p.shape) for p in pieces],
        out_specs=pl.BlockSpec((1, 1, kh, IN_SHARD), lambda h, j: (j, h, 0, 0)),
        compiler_params=_params(("arbitrary", "arbitrary")),
    )(h1, *pieces)


def weight_grad(a, b, name, col_sharded, tk=None):
    kin, n = a.shape[1], b.shape[1]

    def body(a_ref, b_ref, o_ref):
        if col_sharded:
            o_ref[0, 0] = _dot_tn(a_ref[...], b_ref[...]).astype(MM)
        else:
            o_ref[...] = _dot_tn(a_ref[...], b_ref[...]).astype(MM)

    if col_sharded:
        kh, ns = kin // 2, n // N_CHIPS
        out = jax.ShapeDtypeStruct((N_CHIPS, 2, kh, ns), MM)
        grid = (2, N_CHIPS)
        in_specs = [pl.BlockSpec((SEQ, kh), lambda h, j: (0, h)), pl.BlockSpec((SEQ, ns), lambda h, j: (0, j))]
        out_spec = pl.BlockSpec((1, 1, kh, ns), lambda h, j: (j, h, 0, 0))
        sem = ("arbitrary", "arbitrary")
    else:
        out = jax.ShapeDtypeStruct((kin, n), MM)
        grid = (kin // tk,)
        in_specs = [pl.BlockSpec((SEQ, tk), lambda r: (0, r)), pl.BlockSpec((SEQ, n), lambda r: (0, 0))]
        out_spec = pl.BlockSpec((tk, n), lambda r: (r, 0))
        sem = ("arbitrary",)
    res = pl.pallas_call(
        body, name=name, grid=grid, out_shape=out, in_specs=in_specs, out_specs=out_spec,
        compiler_params=_params(sem),
    )(a, b)
    if not col_sharded:
        res = res.reshape(N_CHIPS, 2, kin // (2 * N_CHIPS), n)
    return res


def weight_grad_mix(merged, dmix, u3, dco, att, dao):
    operands = (merged, dmix, u3, dco, att, dao)
    n_out, n_br = D_MODEL // 2, CONV_DIM // 2

    def body(*refs):
        hbm, (o_out, o_cb, o_ab), bufs, sems = refs[:6], refs[6:9], refs[9:15], refs[15]
        copies = [pltpu.make_async_copy(hbm[i], bufs[i], sems.at[i]) for i in range(6)]
        for cp in copies:
            cp.start()
        m_ref, dm_ref, u_ref, dco_ref, a_ref, dao_ref = bufs
        copies[0].wait()
        copies[1].wait()
        for h in range(2):
            o_out[h * n_out:(h + 1) * n_out, :] = _dot_tn(m_ref[:, h * n_out:(h + 1) * n_out], dm_ref[...]).astype(MM)
        for br, (a, d, o) in enumerate(((u_ref, dco_ref, o_cb), (a_ref, dao_ref, o_ab))):
            copies[2 + 2 * br].wait()
            copies[3 + 2 * br].wait()
            for h in range(2):
                g = _dot_tn(a[:, h * n_br:(h + 1) * n_br], d[...])
                for j in range(N_CHIPS):
                    o[j, h] = g[:, j * BR_SHARD:(j + 1) * BR_SHARD].astype(MM)

    branch = jax.ShapeDtypeStruct((N_CHIPS, 2, n_br, BR_SHARD), MM)
    dw_out, dw_cb, dw_ab = pl.pallas_call(
        body, name="dw_mix", out_shape=[jax.ShapeDtypeStruct((D_MODEL, D_MODEL), MM), branch, branch],
        in_specs=[ANY] * 6, out_specs=[VMEM_SPEC] * 3,
        scratch_shapes=[pltpu.VMEM(a.shape, a.dtype) for a in operands] + [pltpu.SemaphoreType.DMA((6,))],
        compiler_params=_params(),
    )(*operands)
    return dw_out.reshape(N_CHIPS, 2, D_MODEL // (2 * N_CHIPS), D_MODEL), dw_cb, dw_ab


def _place():
    x, y, c = lax.axis_index("x"), lax.axis_index("y"), lax.axis_index("c")
    chips = [(1 - x, y), (x, 1 - y), (1 - x, 1 - y)]
    return x, y, c, chips


def _rcopy(src, dst, send_sem, recv_sem, dev):
    return pltpu.make_async_remote_copy(src_ref=src, dst_ref=dst, send_sem=send_sem, recv_sem=recv_sem,
                                        device_id=dev, device_id_type=MESH)


class _Gather:
    N_MOVES = 6

    def __init__(self, shapes, w, o, scratch):
        self.n, self.shapes, self.w, self.o = len(w), shapes, w, o
        self.send, self.recv, self.psend, self.precv, self.loc_in, self.loc_out = scratch[:6]
        self.raw, self.stage = scratch[6:6 + self.n], scratch[6 + self.n:]
        x, y, c, self.chips = _place()
        self.c = c
        self.me, k_x, k_y, k_far = 2 * x + y, 2 * (1 - x) + y, 2 * x + (1 - y), 2 * (1 - x) + (1 - y)
        to_x, to_y = (1 - x, y, c), (x, 1 - y, c)
        self.sib = (x, y, 1 - c)
        self.sent_as = [(self.me, 0, to_x), (self.me, 1, to_y), (self.me, 1, to_x), (self.me, 0, to_y),
                        (k_x, 0, to_y), (k_y, 1, to_x)]
        self.arrives_as = [(k_x, 0, to_x), (k_y, 1, to_y), (k_x, 1, to_x), (k_y, 0, to_y),
                           (k_far, 0, to_y), (k_far, 1, to_x)]
        self.sent_on_after = {0: 4, 1: 5}

    @staticmethod
    def scratch(shards):
        n = len(shards)
        sems = pltpu.SemaphoreType.DMA
        m = _Gather.N_MOVES * n
        return ([sems((m,)), sems((m,)), sems((m,)), sems((m,)), sems((3 * n,)), sems((n,))]
                + [pltpu.VMEM(s.shape, s.dtype) for s in shards] + [pltpu.VMEM(s.shape, MM) for s in shards])

    @staticmethod
    def out_shapes(shards):
        return [jax.ShapeDtypeStruct((N_CHIPS,) + s.shape, MM) for s in shards]

    def _rows(self, t, quarter, cc):
        rq = self.shapes[t][0] // 4
        return pl.ds(pl.multiple_of((2 * cc + quarter) * rq, rq), rq)

    def _own_rows(self, t, piece):
        if piece < 2:
            return self._rows(t, piece, self.c)
        rh = self.shapes[t][0] // 2
        return pl.ds(pl.multiple_of((1 - self.c) * rh, rh), rh)

    def _chip(self, j):
        cx, cy = self.chips[j]
        return 2 * cx + cy, (cx, cy, self.c)

    def local_in(self, t, piece):
        rows = self._own_rows(t, piece)
        return pltpu.make_async_copy(self.w[t].at[rows, :], self.raw[t].at[rows, :], self.loc_in.at[3 * t + piece])

    def local_out(self, t):
        return pltpu.make_async_copy(self.stage[t], self.o[t].at[self.me], self.loc_out.at[t])

    def sent(self, i, t):
        k, quarter, dev = self.sent_as[i]
        rows = self._rows(t, quarter, self.c)
        there = self.o[t].at[k, rows, :]
        return _rcopy(self.stage[t].at[rows, :] if i < 4 else there, there,
                      self.send.at[i * self.n + t], self.recv.at[i * self.n + t], dev)

    def arrived(self, i, t):
        k, quarter, dev = self.arrives_as[i]
        blk = self.o[t].at[k, self._rows(t, quarter, self.c), :]
        return _rcopy(blk, blk, self.send.at[i * self.n + t], self.recv.at[i * self.n + t], dev)

    def passed(self, i, t, cc):
        k, quarter, _ = self.arrives_as[i]
        blk = self.o[t].at[k, self._rows(t, quarter, cc), :]
        return _rcopy(blk, blk, self.psend.at[i * self.n + t], self.precv.at[i * self.n + t], self.sib)

    def start(self):
        for piece in range(3):
            for t in range(self.n):
                self.local_in(t, piece).start()
        for piece, moves in enumerate(((0, 3), (1, 2), ())):
            for t in range(self.n):
                rows = self._own_rows(t, piece)
                self.local_in(t, piece).wait()
                self.stage[t][rows, :] = self.raw[t][rows, :].astype(MM)
                for i in moves:
                    self.sent(i, t).start()
        for t in range(self.n):
            self.local_out(t).start()

    def forward(self):
        for i in range(self.N_MOVES):
            for t in range(self.n):
                self.arrived(i, t).wait_recv()
                if i in self.sent_on_after:
                    self.sent(self.sent_on_after[i], t).start()
                self.passed(i, t, self.c).start()

    def finish(self):
        for i in range(self.N_MOVES):
            for t in range(self.n):
                self.passed(i, t, 1 - self.c).wait_recv()
        for i in range(self.N_MOVES):
            for t in range(self.n):
                self.sent(i, t).wait_send()
                self.passed(i, t, self.c).wait_send()
        for t in range(self.n):
            self.local_out(t).wait()


def all_gather_weights(shards, small, later):
    n, m = len(shards), len(later)
    shapes = [s.shape for s in shards]

    def body(*refs):
        w = refs[:n]
        sm = refs[n]
        lw = refs[n + 1:n + 1 + m]
        o = refs[n + 1 + m:2 * n + 1 + m]
        osm = refs[2 * n + 1 + m]
        lo = refs[2 * n + 2 + m:2 * n + 2 + 2 * m]
        scratch = refs[2 * n + 2 + 2 * m:]
        ssend, srecv, sloc, lsem_in, lsem_out = scratch[:5]
        lraw, lstage = scratch[5:5 + m], scratch[5 + m:5 + 2 * m]
        g = _Gather(shapes, w, o, scratch[5 + 2 * m:])
        own = pltpu.make_async_copy(sm, osm.at[g.me], sloc)
        own.start()
        g.start()
        loads = [pltpu.make_async_copy(lw[t], lraw[t], lsem_in.at[t]) for t in range(m)]
        for cp in loads:
            cp.start()
        small_cps = [_rcopy(sm, osm.at[g.me], ssend.at[j], srecv.at[j], g._chip(j)[1]) for j in range(3)]
        for cp in small_cps:
            cp.start()
        places = []
        for t in range(m):
            loads[t].wait()
            lstage[t][...] = lraw[t][...].astype(MM)
            places.append(pltpu.make_async_copy(lstage[t], lo[t].at[g.me], lsem_out.at[t]))
            places[t].start()
        g.forward()
        g.finish()
        for j in range(3):
            k, dev = g._chip(j)
            _rcopy(sm, osm.at[k], ssend.at[j], srecv.at[j], dev).wait_recv()
            small_cps[j].wait_send()
        own.wait()
        for cp in places:
            cp.wait()

    out_shape = _Gather.out_shapes(shards)
    out_shape.append(jax.ShapeDtypeStruct((N_CHIPS,) + small.shape, small.dtype))
    out_shape += _Gather.out_shapes(later)
    sems = pltpu.SemaphoreType.DMA
    return pl.pallas_call(
        body, name="all_gather_weights", out_shape=out_shape,
        in_specs=[ANY] * (n + 1 + m), out_specs=[ANY] * (n + 1 + m),
        scratch_shapes=[sems((3,)), sems((3,)), sems, sems((m,)), sems((m,))]
                       + [pltpu.VMEM(s.shape, s.dtype) for s in later] + [pltpu.VMEM(s.shape, MM) for s in later]
                       + _Gather.scratch(shards),
        compiler_params=_params(),
    )(*shards, small, *later)


HBM_SPEC = pl.BlockSpec(memory_space=pltpu.HBM)
SEM_SPEC = pl.BlockSpec(memory_space=pltpu.SEMAPHORE)
DATAFLOW = pltpu.SideEffectType.DATAFLOW_SIDE_EFFECTING


def split_start(name, bufs, n_copies, copies):
    nb = len(bufs)

    def body(*refs):
        for cp in copies(refs[:nb], refs[nb], refs[nb + 1]):
            cp.start()
        token = refs[2 * nb + 2]
        token[...] = jnp.zeros_like(token)

    sems = [pltpu.SemaphoreType.DMA((n_copies,))] * 2
    res = pl.pallas_call(
        body, name=name,
        out_shape=sems + [pltpu.HBM(a.shape, a.dtype) for a in bufs] + [jax.ShapeDtypeStruct((8, 128), F32)],
        in_specs=[HBM_SPEC] * nb, out_specs=[SEM_SPEC] * 2 + [HBM_SPEC] * nb + [VMEM_SPEC],
        input_output_aliases={i: 2 + i for i in range(nb)},
        compiler_params=pltpu.CompilerParams(has_side_effects=DATAFLOW),
    )(*[pltpu.with_memory_space_constraint(a, pltpu.HBM) for a in bufs])
    return res[:-1], res[-1]


def split_wait(name, state, after, copies):
    sems, bufs = state[:2], state[2:]
    nb = len(bufs)

    def body(*refs):
        for cp in copies(refs[:nb], refs[nb], refs[nb + 1]):
            cp.wait_send()
            cp.wait_recv()

    return pl.pallas_call(
        body, name=name, out_shape=[pltpu.HBM(a.shape, a.dtype) for a in bufs],
        in_specs=[HBM_SPEC] * nb + [SEM_SPEC] * 2 + [ANY] * len(after), out_specs=[HBM_SPEC] * nb,
        input_output_aliases={i: i for i in range(nb)},
        compiler_params=pltpu.CompilerParams(has_side_effects=DATAFLOW),
    )(*bufs, *sems, *after)


class _Shifted:
    def __init__(self, sems, first):
        self.sems, self.first = sems, first

    @property
    def at(self):
        return self

    def __getitem__(self, i):
        return self.sems.at[self.first + i]


def _scatter_copies(n):
    def copies(refs, send, recv):
        _, _, c, chips = _place()
        return [_rcopy(refs[t].at[2 * cx + cy], refs[n + t].at[j], send.at[3 * t + j], recv.at[3 * t + j], (cx, cy, c))
                for t in range(n) for j, (cx, cy) in enumerate(chips)]
    return copies


def scatter_start(parts, tag):
    lands = [lax.empty((3,) + p.shape[1:], p.dtype) for p in parts]
    return split_start("scatter_start_" + tag, list(parts) + lands, 3 * len(parts), _scatter_copies(len(parts)))


def scatter_wait(state, after, tag):
    n = (len(state) - 2) // 2
    return split_wait("scatter_wait_" + tag, state, after, _scatter_copies(n))[n:]


def _gather_copies(shapes, level):
    n = len(shapes)

    def copies(refs, send, recv):
        x, y, c, chips = _place()
        out = []
        for t in range(n):
            rh = shapes[t][0] // 2
            for j, (cx, cy) in enumerate(chips):
                k, dev = (2 * x + y, (cx, cy, c)) if level == 1 else (2 * cx + cy, (x, y, 1 - c))
                blk = refs[t].at[k, pl.ds(c * rh, rh), :]
                out.append(_rcopy(blk, blk, send.at[3 * t + j], recv.at[3 * t + j], dev))
        return out
    return copies


def _sibling_copies(n, other_half):
    def copies(refs, send, recv):
        x, y, c, _ = _place()
        return [_rcopy(refs[t].at[:, 1 - c] if other_half else refs[t], refs[n + t], send.at[t], recv.at[t],
                       (x, y, 1 - c)) for t in range(n)]
    return copies


def sibling_start(srcs, other_half, tag):
    lands = [lax.empty((a.shape[0],) + a.shape[2:] if other_half else a.shape, a.dtype) for a in srcs]
    return split_start("sibling_start_" + tag, list(srcs) + lands, len(srcs),
                       _sibling_copies(len(srcs), other_half))


def sibling_wait(state, after, other_half, tag):
    n = (len(state) - 2) // 2
    res = split_wait("sibling_wait_" + tag, state, after, _sibling_copies(n, other_half))
    return res[:n], res[n:]


def small_pack(ddw, v512, v1024, loss_parts):
    rows, width = PACK_ROWS, 512
    n512, n1024 = len(VEC512), len(VEC1024)

    def body(*refs):
        ddw_ref = refs[0]
        a_refs = refs[1:1 + n512]
        b_refs = refs[1 + n512:1 + n512 + n1024]
        lp_ref, o_ref, p_ref = refs[1 + n512 + n1024:]
        p_ref[...] = jnp.zeros_like(p_ref)
        p_ref[0:32, :] = ddw_ref[...]
        p_ref[LOSS_ROW:LOSS_ROW + 1, 0:128] = jnp.sum(lp_ref[...], axis=0, keepdims=True) * 0.125
        for i, r in enumerate(a_refs):
            p_ref[32 + i:33 + i, :] = r[...]
        for i, r in enumerate(b_refs):
            base = 32 + n512 + 2 * i
            p_ref[base:base + 1, :] = r[:, 0:512]
            p_ref[base + 1:base + 2, :] = r[:, 512:1024]
        x, y, c, _ = _place()
        o_ref[4 * x + 2 * y + c] = p_ref[...]

    n_in = 2 + n512 + n1024
    return pl.pallas_call(
        body, name="small_pack", out_shape=jax.ShapeDtypeStruct((8, rows, width), F32),
        in_specs=[VMEM_SPEC] * n_in, out_specs=VMEM_SPEC,
        scratch_shapes=[pltpu.VMEM((rows, width), F32)],
    )(ddw, *[v512[n] for n in VEC512], *[v1024[n] for n in VEC1024], loss_parts)


def _small_copies(refs, send, recv):
    x, y, c, _ = _place()
    mine = refs[0].at[4 * x + 2 * y + c]
    peers = [(1 - x if k & 4 else x, 1 - y if k & 2 else y, 1 - c if k & 1 else c) for k in range(1, 8)]
    return [_rcopy(mine, mine, send.at[i], recv.at[i], dev) for i, dev in enumerate(peers)]


def _row_block(r):
    for tr in (512, 352, 256, 128):
        if r % tr == 0:
            return tr
    return r


def add_halves(g, recv, name):
    _, _, r, w = g.shape
    tr = _row_block(r)

    def body(g_ref, r_ref, ob_ref, own_ref):
        k = pl.program_id(1)
        me = 2 * lax.axis_index("x") + lax.axis_index("y")
        t = g_ref[0, 0].astype(F32) + r_ref[0].astype(F32)
        ob_ref[0] = t.astype(MM)
        mine = jnp.where(k == me, t, 0.0)

        @pl.when(k == 0)
        def _():
            own_ref[...] = mine

        @pl.when(k != 0)
        def _():
            own_ref[...] += mine

    return pl.pallas_call(
        body, name=name, grid=(r // tr, N_CHIPS),
        in_specs=[pl.BlockSpec((1, 1, tr, w), lambda i, k: (k, lax.axis_index("c"), i, 0)),
                  pl.BlockSpec((1, tr, w), lambda i, k: (k, i, 0))],
        out_specs=[pl.BlockSpec((1, tr, w), lambda i, k: (k, i, 0)),
                   pl.BlockSpec((tr, w), lambda i, k: (i, 0))],
        out_shape=(jax.ShapeDtypeStruct((N_CHIPS, r, w), MM), jax.ShapeDtypeStruct((r, w), F32)),
        compiler_params=_params(("arbitrary", "arbitrary")),
    )(g, recv)


def sum_parts(own, rin, after, name):
    _, r, w = rin.shape
    tr = _row_block(r)

    def body(o_ref, r_ref, after_ref, out_ref):
        out_ref[...] = ((o_ref[...] + r_ref[0].astype(F32)) + r_ref[1].astype(F32)) + r_ref[2].astype(F32)

    return pl.pallas_call(
        body, name=name, grid=(r // tr,), out_shape=jax.ShapeDtypeStruct((r, w), F32),
        in_specs=[pl.BlockSpec((tr, w), lambda i: (i, 0)), pl.BlockSpec((3, tr, w), lambda i: (0, i, 0)),
                  _full_spec((8, 128))],
        out_specs=pl.BlockSpec((tr, w), lambda i: (i, 0)),
        compiler_params=_params(("arbitrary",)),
    )(own, rin, after)


def _adamw_math(w, g, m, v):
    mn = ADAM_B1 * m + (1.0 - ADAM_B1) * g
    vn = ADAM_B2 * v + (1.0 - ADAM_B2) * (g * g)
    m_hat = mn / (1.0 - ADAM_B1 ** ADAM_STEP)
    v_hat = vn / (1.0 - ADAM_B2 ** ADAM_STEP)
    return -ADAM_LR * (m_hat / (jnp.sqrt(v_hat) + ADAM_EPS) + ADAM_WD * w), mn, vn


def adamw(w, mine, other, m, v, name):
    r, c = w.shape
    rh = r // 2
    tr = _row_block(rh)
    if c >= 1024 and tr % 512 == 0:
        tr = 256
    nb = rh // tr

    def body(w_ref, a_ref, b_ref, m_ref, v_ref, go_ref, d_ref, mo_ref, vo_ref):
        gv = jnp.where(lax.axis_index("c") == pl.program_id(0), a_ref[...], b_ref[...])
        go_ref[...] = gv
        d_ref[...], mo_ref[...], vo_ref[...] = _adamw_math(w_ref[...], gv, m_ref[...], v_ref[...])

    def half(of_sibling):
        def index(h, i):
            owner = lax.axis_index("c")
            owner = 1 - owner if of_sibling else owner
            return jnp.where(h == owner, i, jnp.where(h < owner, 0, nb - 1)), 0
        return pl.BlockSpec((tr, c), index)

    spec = pl.BlockSpec((tr, c), lambda h, i: (h * nb + i, 0))
    out = jax.ShapeDtypeStruct((r, c), F32)
    return pl.pallas_call(
        body, name=name, grid=(2, nb), out_shape=(out, out, out, out),
        in_specs=[spec, half(False), half(True), spec, spec], out_specs=[spec] * 4,
        compiler_params=_params(("arbitrary", "arbitrary")),
    )(w, mine, other, m, v)


def adamw_small(packs, params, after):
    names = list(params)
    flat = [a for n in names for a in params[n]]

    def body(*refs):
        p_ref = refs[0]
        ins = refs[1:1 + 3 * len(names)]
        loss_ref, g_ref = refs[2 + 3 * len(names):4 + 3 * len(names)]
        outs = refs[4 + 3 * len(names):]
        total = p_ref[0]
        for d in range(1, 8):
            total = total + p_ref[d]
        g_ref[...] = total
        loss_ref[...] = g_ref[LOSS_ROW:LOSS_ROW + 1, 0:1]
        me = 2 * lax.axis_index("x") + lax.axis_index("y")
        for i, n in enumerate(names):
            w_ref, m_ref, v_ref = ins[3 * i:3 * i + 3]
            go_ref, d_ref, mo_ref, vo_ref = outs[4 * i:4 * i + 4]
            if n == "conv_dw_w":
                gv = jnp.zeros((CONV_WIDTH, 128), F32)
                for k in range(N_CHIPS):
                    gv = gv + jnp.where(me == k, g_ref[0:CONV_WIDTH, 128 * k:128 * (k + 1)], 0.0)
            elif n in VEC512:
                r0 = 32 + VEC512.index(n)
                gv = g_ref[r0:r0 + 1, :]
            else:
                r0 = 32 + len(VEC512) + 2 * VEC1024.index(n)
                gv = jnp.concatenate([g_ref[r0:r0 + 1, :], g_ref[r0 + 1:r0 + 2, :]], axis=1)
            go_ref[...] = gv
            d_ref[...], mo_ref[...], vo_ref[...] = _adamw_math(w_ref[...], gv, m_ref[...], v_ref[...])

    out_shape = [jax.ShapeDtypeStruct((1, 1), F32), jax.ShapeDtypeStruct(packs.shape[1:], F32)]
    out_shape += [jax.ShapeDtypeStruct(params[n][0].shape, F32) for n in names for _ in range(4)]
    res = pl.pallas_call(
        body, name="adamw_small", out_shape=out_shape,
        in_specs=[VMEM_SPEC] * (2 + len(flat)), out_specs=[VMEM_SPEC] * len(out_shape),
        compiler_params=_params(),
    )(packs, *flat, after)
    return res[0], res[1], {n: res[2 + 4 * i:6 + 4 * i] for i, n in enumerate(names)}


REST = ("w_ffn_up", "w_ffn_down", "w_out", "w_conv_branch", "w_att_branch")
VEC512 = ("conv_dw_b", "conv_ln_g", "conv_ln_b")
VEC1024 = ("norm_mix_pre", "b_conv_branch", "norm_mix_post", "norm_ffn_pre", "norm_ffn_post")
PACK_ROWS = 48
LOSS_ROW = 47


def kernel(x, norm_mix_pre, w_in, conv_dw_w, conv_dw_b, conv_ln_g, conv_ln_b, w_conv_branch, b_conv_branch, w_att_branch, w_out, norm_mix_post, norm_ffn_pre, w_ffn_up, w_ffn_down, norm_ffn_post, loss_target, m_norm_mix_pre, m_w_in, m_conv_dw_w, m_conv_dw_b, m_conv_ln_g, m_conv_ln_b, m_w_conv_branch, m_b_conv_branch, m_w_att_branch, m_w_out, m_norm_mix_post, m_norm_ffn_pre, m_w_ffn_up, m_w_ffn_down, m_norm_ffn_post, v_norm_mix_pre, v_w_in, v_conv_dw_w, v_conv_dw_b, v_conv_ln_g, v_conv_ln_b, v_w_conv_branch, v_b_conv_branch, v_w_att_branch, v_w_out, v_norm_mix_post, v_norm_ffn_pre, v_w_ffn_up, v_w_ffn_down, v_norm_ffn_post):
    weights = dict(norm_mix_pre=norm_mix_pre, w_in=w_in, conv_dw_w=conv_dw_w, conv_dw_b=conv_dw_b, conv_ln_g=conv_ln_g, conv_ln_b=conv_ln_b, w_conv_branch=w_conv_branch, b_conv_branch=b_conv_branch, w_att_branch=w_att_branch, w_out=w_out, norm_mix_post=norm_mix_post, norm_ffn_pre=norm_ffn_pre, w_ffn_up=w_ffn_up, w_ffn_down=w_ffn_down, norm_ffn_post=norm_ffn_post)
    mom = dict(norm_mix_pre=m_norm_mix_pre, w_in=m_w_in, conv_dw_w=m_conv_dw_w, conv_dw_b=m_conv_dw_b, conv_ln_g=m_conv_ln_g, conv_ln_b=m_conv_ln_b, w_conv_branch=m_w_conv_branch, b_conv_branch=m_b_conv_branch, w_att_branch=m_w_att_branch, w_out=m_w_out, norm_mix_post=m_norm_mix_post, norm_ffn_pre=m_norm_ffn_pre, w_ffn_up=m_w_ffn_up, w_ffn_down=m_w_ffn_down, norm_ffn_post=m_norm_ffn_post)
    var = dict(norm_mix_pre=v_norm_mix_pre, w_in=v_w_in, conv_dw_w=v_conv_dw_w, conv_dw_b=v_conv_dw_b, conv_ln_g=v_conv_ln_g, conv_ln_b=v_conv_ln_b, w_conv_branch=v_w_conv_branch, b_conv_branch=v_b_conv_branch, w_att_branch=v_w_att_branch, w_out=v_w_out, norm_mix_post=v_norm_mix_post, norm_ffn_pre=v_norm_ffn_pre, w_ffn_up=v_w_ffn_up, w_ffn_down=v_w_ffn_down, norm_ffn_post=v_norm_ffn_post)
    order = list(weights)
    grads, deltas, new_m, new_v = {}, {}, {}, {}
    xs = x.reshape(SEQ, D_MODEL)
    tgt = loss_target.reshape(SEQ, D_MODEL)
    row = lambda a: a.reshape(1, -1)
    g1, g2, g3, g4 = (row(weights[n]) for n in ("norm_mix_pre", "norm_mix_post", "norm_ffn_pre", "norm_ffn_post"))
    ln_g, ln_b = row(conv_ln_g), row(conv_ln_b)

    summed, from_chips = {}, {}

    def core_sums(names, state, after, tag):
        own, from_sibling = sibling_wait(state, after, True, tag)
        for n, g, r in zip(names, own, from_sibling):
            summed[n] = add_halves(g, r, "add_" + n)

    def chip_sums(names, after):
        return [sum_parts(summed[n][1], from_chips[n], after, "sum_" + n) for n in names]

    def optimize(names, state, after, tag):
        mine, other = sibling_wait(state, after, False, tag)
        for n, a, b in zip(names, mine, other):
            grads[n], deltas[n], new_m[n], new_v[n] = adamw(weights[n], a, b, mom[n], var[n], "adamw_" + n)

    w_in_g, dw_g, *rest = all_gather_weights([w_in], conv_dw_w, [weights[n] for n in REST])
    w_dw_full = jnp.concatenate([dw_g[k] for k in range(N_CHIPS)], axis=1)
    rest_shapes = [weights[n].shape for n in REST]
    state, token = split_start("gather_start", rest, 3 * len(REST), _gather_copies(rest_shapes, 1))
    h1, ci, q, k, v, gc, ga = in_proj_fwd(xs, g1, w_in_g, token)
    u1, u3 = conv_fwd(ci, w_dw_full, row(conv_dw_b), ln_g, ln_b)
    att, rc = attn_fwd(q, k, v)
    rest = split_wait("gather_wait", state, [att], _gather_copies(rest_shapes, 1))
    pass_copies = _gather_copies(rest_shapes[2:] + rest_shapes[:2], 2)
    n_first = 3 * len(REST[2:])
    state, token = split_start("pass_start", rest[2:] + rest[:2], 3 * len(REST), pass_copies)
    passed = split_wait("pass_mix_wait", state, [], lambda *a: pass_copies(*a)[:n_first])
    w_out_g, w_cb_g, w_ab_g = passed[:3]
    w_out_g = w_out_g.reshape(D_MODEL, D_MODEL)
    co, ao, merged, mix, x2, h2 = mix_fwd(u3, att, gc, ga, xs, w_cb_g, row(b_conv_branch), w_ab_g, w_out_g,
                                          g2, g3, token)
    w_up_g, w_down_g = split_wait(
        "pass_ffn_wait", list(state[:2]) + list(passed[3:]), [h2],
        lambda refs, send, recv: _gather_copies(rest_shapes[:2], 2)(refs, _Shifted(send, n_first), _Shifted(recv, n_first)))
    w_down_g = w_down_g.reshape(D_FF, D_MODEL)
    gate, up, act = ffn_up_fwd(h2, w_up_g)
    dff, dy, loss_parts, dg4 = ffn_down_loss(act, w_down_g, x2, tgt, g4)

    dgu = ffn_act_bwd(dff, w_down_g, gate, up)
    dx2, dmix, dg3, dg2 = ffn_in_bwd(dgu, w_up_g, x2, mix, dy, g3, g2)
    ffn_grads = [weight_grad(h2, dgu, "dw_ffn_up", True), weight_grad(act, dff, "dw_ffn_down", False, tk=UP_SHARD)]
    to_ffn, token = sibling_start(ffn_grads, True, "dw_ffn")
    dco, dao, dg, du3, datt, dbcb = merge_bwd(dmix, w_out_g, gc, ga, co, ao, w_cb_g, w_ab_g, token)
    to_mix, token = sibling_start(weight_grad_mix(merged, dmix, u3, dco, att, dao), True, "dw_mix")
    core_sums(REST[:2], to_ffn, [token], "dw_ffn")
    core_sums(REST[2:], to_mix, [summed["w_ffn_down"][1]], "dw_mix")
    state, token = scatter_start([summed[n][0] for n in REST], "rest")
    dci, ddw, dbdw, dlng, dlnb = conv_bwd(du3, u1, ci, w_dw_full, ln_g, ln_b, token)
    dqkv = attn_bwd(q, k, v, datt, rc, token)
    from_chips.update(zip(REST, scatter_wait(state, [dci, dqkv], "rest")))
    dproj = (dci, dqkv, dg)
    to_in, token = sibling_start([weight_grad_in(h1, dproj)], True, "dw_in")
    grad_x, dg1 = in_proj_bwd(dproj, w_in_g, xs, dx2, g1, token)
    v512 = dict(conv_dw_b=dbdw, conv_ln_g=dlng, conv_ln_b=dlnb)
    v1024 = dict(norm_mix_pre=dg1, b_conv_branch=dbcb, norm_mix_post=dg2, norm_ffn_pre=dg3, norm_ffn_post=dg4)
    packs = small_pack(ddw, v512, v1024, loss_parts)
    core_sums(("w_in",), to_in, [packs], "dw_in")
    to_chips = summed["w_in"][0]
    landing = lax.empty((3,) + to_chips.shape[1:], to_chips.dtype)

    def scatter_and_packs(refs, send, recv):
        return (_scatter_copies(1)(refs[:2], send, recv)
                + _small_copies(refs[2:], _Shifted(send, 3), _Shifted(recv, 3)))

    state, token = split_start("scatter_start_w_in", [to_chips, landing, packs], 3 + 7, scatter_and_packs)
    swap_up, token = sibling_start(chip_sums(REST[:1], token), False, "sum_ffn_up")
    swap_rest, token = sibling_start(chip_sums(REST[1:], token), False, "sum_rest")
    optimize(REST[:1], swap_up, [token], "sum_ffn_up")
    optimize(REST[1:], swap_rest, [new_v["w_ffn_up"]], "sum_rest")
    _, from_chips["w_in"], packs = split_wait("scatter_wait_w_in", state, [new_v[n] for n in REST], scatter_and_packs)
    swap_in, token = sibling_start(chip_sums(("w_in",), token), False, "sum_w_in")
    as_rows = lambda n, a: a if n == "conv_dw_w" else a.reshape(1, -1)
    small_names = ("conv_dw_w",) + VEC512 + VEC1024
    loss, gsum, small = adamw_small(
        packs, {n: tuple(as_rows(n, d[n]) for d in (weights, mom, var)) for n in small_names}, token)
    optimize(("w_in",), swap_in, [gsum], "sum_w_in")
    for n in small_names:
        grads[n], deltas[n], new_m[n], new_v[n] = (a.reshape(weights[n].shape) for a in small[n])

    return (loss.reshape(()), grad_x.reshape(1, SEQ, D_MODEL),*[grads[n] for n in order], *[deltas[n] for n in order],
            *[new_m[n] for n in order], *[new_v[n] for n in order])
```

```python
import jax
import jax.numpy as jnp
from jax import lax
from jax.experimental import pallas as pl
from jax.experimental.pallas import tpu as pltpu

F32 = jnp.float32
MM = jnp.bfloat16

SEQ = 2048
D_MODEL = 1024
CONV_DIM = 512
ATT_DIM = 512
CONV_WIDTH = 31
D_FF = 2816
IN_COLS = 2 * CONV_DIM + 3 * ATT_DIM + 2 * D_MODEL
N_CHIPS = 4
IN_SHARD = IN_COLS // N_CHIPS
UP_SHARD = 2 * D_FF // N_CHIPS
BR_SHARD = D_MODEL // N_CHIPS
EPS = 1e-6
ATT_SCALE = 0.125

TM = 256
GLU_ROWS = 256
TQ = 128
CONV_TILE = 64
CONV_WIN = CONV_TILE + 32
VMEM_LIMIT = 56 * 1024 * 1024

ADAM_LR = 0.001
ADAM_B1 = 0.9
ADAM_B2 = 0.999
ADAM_EPS = 1e-08
ADAM_WD = 0.01
ADAM_STEP = 10

MESH = pl.DeviceIdType.MESH
ANY = pl.BlockSpec(memory_space=pl.ANY)
VMEM_SPEC = pl.BlockSpec(memory_space=pltpu.VMEM)

NT_DIMS = (((1,), (1,)), ((), ()))
TN_DIMS = (((0,), (0,)), ((), ()))

IN_PIECES = (("ci", 0, 1024), ("q", 1024, 1536), ("k", 1536, 2048), ("v", 2048, 2560),
             ("gc", 2560, 3584), ("ga", 3584, 4608))


def _params(sem=None, vmem=VMEM_LIMIT):
    return pltpu.CompilerParams(dimension_semantics=sem, vmem_limit_bytes=vmem)


def _dot(a, b):
    return jnp.dot(a, b, preferred_element_type=F32)


def _dot_nt(a, b):
    return lax.dot_general(a, b, NT_DIMS, preferred_element_type=F32)


def _dot_tn(a, b):
    return lax.dot_general(a, b, TN_DIMS, preferred_element_type=F32)


def _sigmoid(x):
    return 1.0 / (1.0 + jnp.exp(-x))


def _rms(x):
    r = lax.rsqrt(jnp.mean(x * x, axis=-1, keepdims=True) + EPS)
    return x * r, r


def _rms_bwd(dy_g, n, r):
    return r * (dy_g - n * jnp.mean(dy_g * n, axis=-1, keepdims=True))


def _row_tile_spec(width, tm=TM):
    return pl.BlockSpec((tm, width), lambda i: (i, 0))


def _full_spec(shape):
    nd = len(shape)
    return pl.BlockSpec(shape, lambda *_: (0,) * nd)


def _weight_spec(shape):
    nd = len(shape)
    return pl.BlockSpec(shape, lambda *_: (0,) * nd, pipeline_mode=pl.Buffered(1))


def _acc_rows(ref, val, first):
    @pl.when(first)
    def _():
        ref[...] = val

    @pl.when(jnp.logical_not(first))
    def _():
        ref[...] += val


TOKEN_SPEC = pl.BlockSpec((8, 128), lambda *_: (0, 0))


def in_proj_fwd(x, g1, w_in_g, after):
    def body(x_ref, g_ref, w_ref, after_ref, h_ref, ci_ref, q_ref, k_ref, v_ref, gc_ref, ga_ref):
        n, _ = _rms(x_ref[...])
        h = (n * g_ref[...]).astype(MM)
        h_ref[...] = h
        outs = dict(ci=ci_ref, q=q_ref, k=k_ref, v=v_ref, gc=gc_ref, ga=ga_ref)
        for j in range(N_CHIPS):
            p = _dot(h, w_ref[j])
            g0 = j * IN_SHARD
            for name, s, e in IN_PIECES:
                lo, hi = max(s, g0), min(e, g0 + IN_SHARD)
                if lo < hi:
                    ref = outs[name]
                    part = p[:, lo - g0:hi - g0]
                    if name == "q":
                        part = part * ATT_SCALE
                    ref[:, lo - s:hi - s] = part.astype(ref.dtype)

    out_shape = [
        jax.ShapeDtypeStruct((SEQ, D_MODEL), MM),
        jax.ShapeDtypeStruct((SEQ, 2 * CONV_DIM), F32),
        jax.ShapeDtypeStruct((SEQ, ATT_DIM), MM),
        jax.ShapeDtypeStruct((SEQ, ATT_DIM), MM),
        jax.ShapeDtypeStruct((SEQ, ATT_DIM), MM),
        jax.ShapeDtypeStruct((SEQ, D_MODEL), F32),
        jax.ShapeDtypeStruct((SEQ, D_MODEL), F32),
    ]
    return pl.pallas_call(
        body, name="in_proj_fwd", grid=(SEQ // TM,), out_shape=out_shape,
        in_specs=[_row_tile_spec(D_MODEL), _full_spec((1, D_MODEL)), _weight_spec(w_in_g.shape), TOKEN_SPEC],
        out_specs=[_row_tile_spec(s.shape[1]) for s in out_shape],
        compiler_params=_params(("arbitrary",)),
    )(x, g1, w_in_g, after)


LANE_GROUPS = [slice(g, g + 128) for g in range(0, CONV_DIM, 128)]
NORM_ROWS = 16


def _shifted_windows(src_ref, t0, cols, offsets):
    win = src_ref[pl.ds(t0, CONV_WIN), cols]
    for rot in range(8):
        ms = [m for m in offsets if m % 8 == rot]
        if ms:
            shifted = win if rot == 0 else pltpu.roll(win, CONV_WIN - rot, 0)
            for m in ms:
                yield m, shifted[m - rot:m - rot + CONV_TILE, :]


def _shifted_sum(src_ref, t0, cols, w_ref, offset_of_tap):
    tap_at = {offset_of_tap(j): j for j in range(CONV_WIDTH)}
    acc = None
    for m, rows in _shifted_windows(src_ref, t0, cols, sorted(tap_at)):
        t = w_ref[tap_at[m]:tap_at[m] + 1, cols] * rows
        acc = t if acc is None else acc + t
    return acc


def _glu_into(ci_ref, upad_ref):
    upad_ref[0:32, :] = jnp.zeros((32, CONV_DIM), F32)

    def step(i, c):
        t0 = pl.multiple_of(i * GLU_ROWS, GLU_ROWS)
        a = ci_ref[pl.ds(t0, GLU_ROWS), 0:CONV_DIM]
        b = ci_ref[pl.ds(t0, GLU_ROWS), CONV_DIM:2 * CONV_DIM]
        upad_ref[pl.ds(t0 + 32, GLU_ROWS), :] = a * _sigmoid(b)
        return c

    lax.fori_loop(0, SEQ // GLU_ROWS, step, 0)


def _layernorm_parts(u1):
    mu = jnp.mean(u1, axis=-1, keepdims=True)
    xc = u1 - mu
    rstd = lax.rsqrt(jnp.mean(xc * xc, axis=-1, keepdims=True) + EPS)
    return xc * rstd, rstd


def conv_fwd(ci, w_dw, b_dw, ln_g, ln_b):
    def body(ci_ref, w_ref, b_ref, g_ref, bb_ref, u1_ref, u3_ref, upad_ref):
        _glu_into(ci_ref, upad_ref)

        def step(i, c):
            t0 = pl.multiple_of(i * CONV_TILE, CONV_TILE)
            for cols in LANE_GROUPS:
                u1_ref[pl.ds(t0, CONV_TILE), cols] = (_shifted_sum(upad_ref, t0, cols, w_ref, lambda j: j + 2)
                                                      + b_ref[:, cols])
            for r in range(0, CONV_TILE, NORM_ROWS):
                rows = pl.ds(t0 + r, NORM_ROWS)
                xh, _ = _layernorm_parts(u1_ref[rows, :])
                u2 = xh * g_ref[...] + bb_ref[...]
                u3_ref[rows, :] = (u2 * _sigmoid(u2)).astype(MM)
            return c

        lax.fori_loop(0, SEQ // CONV_TILE, step, 0)

    return pl.pallas_call(
        body, name="conv_fwd",
        out_shape=[jax.ShapeDtypeStruct((SEQ, CONV_DIM), F32), jax.ShapeDtypeStruct((SEQ, CONV_DIM), MM)],
        in_specs=[VMEM_SPEC] * 5, out_specs=[VMEM_SPEC] * 2,
        scratch_shapes=[pltpu.VMEM((SEQ + 32, CONV_DIM), F32)],
        compiler_params=_params(),
    )(ci, w_dw, b_dw, ln_g, ln_b)


def _softplus(z):
    return jnp.maximum(z, 0.0) + jnp.log(1.0 + jnp.exp(-jnp.abs(z)))


def _cumsum_weights(suffix, with_total):
    n = 256 if with_total else 128
    r = lax.broadcasted_iota(jnp.int32, (128, n), 0)
    c = lax.broadcasted_iota(jnp.int32, (128, n), 1)
    tri = (r >= c) if suffix else (r <= c)
    return jnp.logical_or(tri, c >= 128).astype(MM)


NO_SCORE = -1e30
N_KB = SEQ // TQ


def _score_bias(lane, row, i, j):
    keep = jnp.logical_and(i >= 0, jnp.logical_or(j < i, lane < row))
    return jnp.where(keep, 0.0, NO_SCORE)


def _block_pipeline(n_stages, descending, step, on_query_block=None):
    n_lag = n_stages - 1
    none = jnp.int32(-1)

    def shift(cur, lag):
        step([cur] + [(lag[2 * s], lag[2 * s + 1]) for s in range(n_lag)])
        return (cur[0], cur[1]) + tuple(lag[:-2])

    def outer(i, lag):
        if on_query_block is not None:
            on_query_block(i)

        def inner(n, lag):
            return shift((i, i - n if descending else n), lag)
        return lax.fori_loop(0, i + 1, inner, lag)

    lag = lax.fori_loop(0, N_KB, outer, (none,) * (2 * n_lag))
    lax.fori_loop(0, n_lag, lambda n, lag: shift((none, none), lag), lag)


def _head_masks():
    lane = lax.broadcasted_iota(jnp.int32, (TQ, 128), 1)
    row = lax.broadcasted_iota(jnp.int32, (TQ, 128), 0)
    return lane, row, lane < 64


def _pick_head(x, head0, h):
    zero = jnp.zeros_like(x)
    return jnp.where(head0, x, zero) if h == 0 else jnp.where(head0, zero, x)


N_PAIRS = ATT_DIM // 128


def _split_heads(src_ref, dst_ref):
    _, _, head0 = _head_masks()

    def block(b, c):
        r0 = pl.multiple_of(b * TQ, TQ)
        d0 = pl.multiple_of(b * 2 * TQ, 2 * TQ)
        for p in range(N_PAIRS):
            x = src_ref[pl.ds(r0, TQ), 128 * p:128 * (p + 1)]
            for h in range(2):
                dst_ref[p, pl.ds(d0 + TQ * h, TQ), :] = _pick_head(x, head0, h)
        return c

    lax.fori_loop(0, N_KB, block, 0)


def attn_fwd(q, k, v):
    def body(q_ref, k_ref, v_ref, o_ref, rc_ref, acc_ref, r_ref, z_ref, spb_ref, ab_ref, qm_ref, vm_ref):
        lane, row, _ = _head_masks()
        w = _cumsum_weights(suffix=True, with_total=True)
        _split_heads(q_ref, qm_ref)
        _split_heads(v_ref, vm_ref)
        acc_ref[...] = jnp.zeros_like(acc_ref)
        r_ref[...] = jnp.zeros_like(r_ref)
        rc_ref[...] = jnp.zeros_like(rc_ref)
        z_ref[...] = jnp.full(z_ref.shape, NO_SCORE, F32)
        spb_ref[...] = jnp.zeros_like(spb_ref)
        ab_ref[...] = jnp.zeros_like(ab_ref)

        def step(pairs):
            (i1, j1), (i2, j2), (i3, j3) = pairs
            k1, q2, q3 = (pl.multiple_of(jnp.maximum(b, 0) * TQ, TQ) for b in (j1, i2, i3))
            q1, k3 = (pl.multiple_of(jnp.maximum(b, 0) * 2 * TQ, 2 * TQ) for b in (i1, j3))
            bias1 = _score_bias(lane, row, i1, j1)
            first2 = j2 == i2
            rc_rows = rc_ref[pl.ds(q2, TQ), :]
            for p in range(N_PAIRS):
                cols = slice(128 * p, 128 * (p + 1))
                kb = k_ref[pl.ds(k1, TQ), cols]
                acc_ref[pl.ds(q3, TQ), cols] += _dot(ab_ref[p], vm_ref[p, pl.ds(k3, 2 * TQ), :])
                for h in range(2):
                    hh = 2 * p + h
                    r = _dot(spb_ref[hh], w)
                    r_in = jnp.where(first2, 0.0, r_ref[hh])
                    ab_ref[p, :, 128 * h:128 * (h + 1)] = jnp.exp(z_ref[hh] - (r[:, :128] + r_in)).astype(MM)
                    rc_rows = jnp.where(jnp.logical_and(lane == 16 * hh + j2, i2 >= 0), r_in, rc_rows)
                    r_ref[hh] = r_in + r[:, 128:]
                    z = _dot_nt(qm_ref[p, pl.ds(q1 + TQ * h, TQ), :], kb) + bias1
                    z_ref[hh] = z
                    spb_ref[hh] = _softplus(z).astype(MM)
            rc_ref[pl.ds(q2, TQ), :] = rc_rows

        _block_pipeline(3, True, step)
        o_ref[...] = acc_ref[...].astype(MM)

    return pl.pallas_call(
        body, name="attn_fwd",
        out_shape=[jax.ShapeDtypeStruct((SEQ, ATT_DIM), MM), jax.ShapeDtypeStruct((SEQ, 128), F32)],
        in_specs=[VMEM_SPEC] * 3, out_specs=[VMEM_SPEC] * 2,
        scratch_shapes=[pltpu.VMEM((SEQ, ATT_DIM), F32), pltpu.VMEM((8, TQ, 128), F32),
                        pltpu.VMEM((8, TQ, 128), F32), pltpu.VMEM((8, TQ, 128), MM),
                        pltpu.VMEM((N_PAIRS, TQ, 256), MM), pltpu.VMEM((N_PAIRS, 2 * SEQ, 128), MM),
                        pltpu.VMEM((N_PAIRS, 2 * SEQ, 128), MM)],
        compiler_params=_params(),
    )(q, k, v)


def _branch_outputs(u_ref, a_ref, wcb_ref, bcb_ref, wab_ref):
    u = u_ref[...]
    a = a_ref[...]
    co = jnp.concatenate([_dot(u, wcb_ref[j]) for j in range(N_CHIPS)], axis=1) + bcb_ref[...]
    ao = jnp.concatenate([_dot(a, wab_ref[j]) for j in range(N_CHIPS)], axis=1)
    return co, ao


def mix_fwd(u3, att, gc, ga, x, w_cb_g, b_cb, w_ab_g, w_out_g, g2, g3, after):
    def body(u_ref, a_ref, gc_ref, ga_ref, x_ref, wcb_ref, bcb_ref, wab_ref, wout_ref, g2_ref, g3_ref, after_ref,
             mg_ref, mix_ref, x2_ref, h2_ref):
        co, ao = _branch_outputs(u_ref, a_ref, wcb_ref, bcb_ref, wab_ref)
        merged = (_sigmoid(gc_ref[...]) * co + _sigmoid(ga_ref[...]) * ao).astype(MM)
        mg_ref[...] = merged
        mix = _dot(merged, wout_ref[...])
        mix_ref[...] = mix
        n2, _ = _rms(mix)
        x2 = x_ref[...] + n2 * g2_ref[...]
        x2_ref[...] = x2
        n3, _ = _rms(x2)
        h2_ref[...] = (n3 * g3_ref[...]).astype(MM)

    out_shape = [
        jax.ShapeDtypeStruct((SEQ, D_MODEL), MM), jax.ShapeDtypeStruct((SEQ, D_MODEL), F32),
        jax.ShapeDtypeStruct((SEQ, D_MODEL), F32), jax.ShapeDtypeStruct((SEQ, D_MODEL), MM),
    ]
    vec = _full_spec((1, D_MODEL))
    return pl.pallas_call(
        body, name="mix_fwd", grid=(SEQ // TM,), out_shape=out_shape,
        in_specs=[_row_tile_spec(CONV_DIM), _row_tile_spec(ATT_DIM), _row_tile_spec(D_MODEL),
                  _row_tile_spec(D_MODEL), _row_tile_spec(D_MODEL), _weight_spec(w_cb_g.shape), vec,
                  _weight_spec(w_ab_g.shape), _weight_spec(w_out_g.shape), vec, vec, TOKEN_SPEC],
        out_specs=[_row_tile_spec(D_MODEL)] * 4,
        compiler_params=_params(("arbitrary",)),
    )(u3, att, gc, ga, x, w_cb_g, b_cb, w_ab_g, w_out_g, g2, g3, after)


def ffn_up_fwd(h2, w_up_g):
    def body(h_ref, wg_ref, wu_ref, gate_ref, up_ref, act_ref):
        h = h_ref[...]
        gate = _dot(h, wg_ref[0])
        up = _dot(h, wu_ref[0])
        gate_ref[...] = gate.astype(MM)
        up_ref[...] = up.astype(MM)
        act_ref[...] = (gate * _sigmoid(gate) * up).astype(MM)

    tile = pl.BlockSpec((TM, UP_SHARD), lambda n, i: (i, n))
    act = jax.ShapeDtypeStruct((SEQ, D_FF), MM)
    return pl.pallas_call(
        body, name="ffn_up_fwd", grid=(2, SEQ // TM), out_shape=[act, act, act],
        in_specs=[pl.BlockSpec((TM, D_MODEL), lambda n, i: (i, 0)),
                  pl.BlockSpec((1, D_MODEL, UP_SHARD), lambda n, i: (n, 0, 0)),
                  pl.BlockSpec((1, D_MODEL, UP_SHARD), lambda n, i: (n + 2, 0, 0))],
        out_specs=[tile, tile, tile],
        compiler_params=_params(("arbitrary", "arbitrary")),
    )(h2, w_up_g, w_up_g)


def ffn_down_loss(act, w_down_g, x2, target, g4):
    def body(act_ref, wd_ref, x2_ref, t_ref, g_ref, dff_ref, dy_ref, loss_ref, dg_ref):
        ff = _dot(act_ref[...], wd_ref[...])
        n4, r4 = _rms(ff)
        g4v = g_ref[...]
        err = x2_ref[...] + n4 * g4v - t_ref[...]
        row_loss = jnp.mean(err * err, axis=-1, keepdims=True)
        loss_ref[...] = jnp.zeros((8, 128), F32) + 0.5 * jnp.sum(row_loss, axis=0, keepdims=True)
        dy = err * (1.0 / D_MODEL)
        dy_ref[...] = dy
        dff_ref[...] = _rms_bwd(dy * g4v, n4, r4).astype(MM)
        _acc_rows(dg_ref, jnp.sum(dy * n4, axis=0, keepdims=True), pl.program_id(0) == 0)

    nt = SEQ // TM
    vec = _full_spec((1, D_MODEL))
    return pl.pallas_call(
        body, name="ffn_down_loss", grid=(nt,),
        out_shape=(jax.ShapeDtypeStruct((SEQ, D_MODEL), MM), jax.ShapeDtypeStruct((SEQ, D_MODEL), F32),
                   jax.ShapeDtypeStruct((nt * 8, 128), F32), jax.ShapeDtypeStruct((1, D_MODEL), F32)),
        in_specs=[_row_tile_spec(D_FF), _weight_spec(w_down_g.shape), _row_tile_spec(D_MODEL),
                  _row_tile_spec(D_MODEL), vec],
        out_specs=[_row_tile_spec(D_MODEL), _row_tile_spec(D_MODEL),
                   pl.BlockSpec((8, 128), lambda i: (i, 0)), vec],
        compiler_params=_params(("arbitrary",)),
    )(act, w_down_g, x2, target, g4)


def ffn_act_bwd(dff, w_down_g, gate, up):
    def body(dff_ref, wd_ref, gate_ref, up_ref, dgu_ref):
        dact = _dot_nt(dff_ref[...], wd_ref[...])
        gate = gate_ref[...].astype(F32)
        sg = _sigmoid(gate)
        dgu_ref[:, 0:D_FF] = (dact * up_ref[...].astype(F32) * (sg * (1.0 + gate * (1.0 - sg)))).astype(MM)
        dgu_ref[:, D_FF:2 * D_FF] = (dact * (gate * sg)).astype(MM)

    return pl.pallas_call(
        body, name="ffn_act_bwd", grid=(SEQ // TM,),
        out_shape=jax.ShapeDtypeStruct((SEQ, 2 * D_FF), MM),
        in_specs=[_row_tile_spec(D_MODEL), _weight_spec(w_down_g.shape), _row_tile_spec(D_FF), _row_tile_spec(D_FF)],
        out_specs=_row_tile_spec(2 * D_FF),
        compiler_params=_params(("arbitrary",)),
    )(dff, w_down_g, gate, up)


def ffn_in_bwd(dgu, w_up_g, x2, mix, dy, g3, g2):
    def body(dgu_ref, w_ref, x2_ref, mix_ref, dy_ref, g3_ref, g2_ref, dx2_ref, dmix_ref, dg3_ref, dg2_ref):
        dh2 = None
        for j in range(N_CHIPS):
            t = _dot_nt(dgu_ref[:, j * UP_SHARD:(j + 1) * UP_SHARD], w_ref[j])
            dh2 = t if dh2 is None else dh2 + t
        first = pl.program_id(0) == 0
        n3, r3 = _rms(x2_ref[...])
        dx2 = dy_ref[...] + _rms_bwd(dh2 * g3_ref[...], n3, r3)
        dx2_ref[...] = dx2
        _acc_rows(dg3_ref, jnp.sum(dh2 * n3, axis=0, keepdims=True), first)
        n2, r2 = _rms(mix_ref[...])
        dmix_ref[...] = _rms_bwd(dx2 * g2_ref[...], n2, r2).astype(MM)
        _acc_rows(dg2_ref, jnp.sum(dx2 * n2, axis=0, keepdims=True), first)

    vec = _full_spec((1, D_MODEL))
    return pl.pallas_call(
        body, name="ffn_in_bwd", grid=(SEQ // TM,),
        out_shape=(jax.ShapeDtypeStruct((SEQ, D_MODEL), F32), jax.ShapeDtypeStruct((SEQ, D_MODEL), MM),
                   jax.ShapeDtypeStruct((1, D_MODEL), F32), jax.ShapeDtypeStruct((1, D_MODEL), F32)),
        in_specs=[_row_tile_spec(2 * D_FF), _weight_spec(w_up_g.shape), _row_tile_spec(D_MODEL),
                  _row_tile_spec(D_MODEL), _row_tile_spec(D_MODEL), vec, vec],
        out_specs=[_row_tile_spec(D_MODEL), _row_tile_spec(D_MODEL), vec, vec],
        compiler_params=_params(("arbitrary",)),
    )(dgu, w_up_g, x2, mix, dy, g3, g2)


def merge_bwd(dmix, w_out_g, gc, ga, u3, att, w_cb_g, b_cb, w_ab_g, after):
    def body(dmix_ref, wout_ref, gc_ref, ga_ref, u_ref, a_ref, wcb_ref, bcb_ref, wab_ref, after_ref,
             dco_ref, dao_ref, dg_ref, du3_ref, datt_ref, dbcb_ref):
        dm = _dot_nt(dmix_ref[...], wout_ref[...])
        co, ao = _branch_outputs(u_ref, a_ref, wcb_ref, bcb_ref, wab_ref)
        sgc = _sigmoid(gc_ref[...])
        sga = _sigmoid(ga_ref[...])
        dco = dm * sgc
        dao = dm * sga
        dg_ref[:, 0:D_MODEL] = (dm * co * (sgc * (1.0 - sgc))).astype(MM)
        dg_ref[:, D_MODEL:2 * D_MODEL] = (dm * ao * (sga * (1.0 - sga))).astype(MM)
        _acc_rows(dbcb_ref, jnp.sum(dco, axis=0, keepdims=True), pl.program_id(0) == 0)
        dco_ref[...] = dco.astype(MM)
        dao_ref[...] = dao.astype(MM)
        du3 = None
        datt = None
        for j in range(N_CHIPS):
            cols = slice(j * BR_SHARD, (j + 1) * BR_SHARD)
            t = _dot_nt(dco_ref[:, cols], wcb_ref[j])
            s = _dot_nt(dao_ref[:, cols], wab_ref[j])
            du3 = t if du3 is None else du3 + t
            datt = s if datt is None else datt + s
        du3_ref[...] = du3
        datt_ref[...] = datt.astype(MM)

    wide = _row_tile_spec(D_MODEL)
    return pl.pallas_call(
        body, name="merge_bwd", grid=(SEQ // TM,),
        out_shape=(jax.ShapeDtypeStruct((SEQ, D_MODEL), MM), jax.ShapeDtypeStruct((SEQ, D_MODEL), MM),
                   jax.ShapeDtypeStruct((SEQ, 2 * D_MODEL), MM),
                   jax.ShapeDtypeStruct((SEQ, CONV_DIM), F32), jax.ShapeDtypeStruct((SEQ, ATT_DIM), MM),
                   jax.ShapeDtypeStruct((1, D_MODEL), F32)),
        in_specs=[wide, _weight_spec(w_out_g.shape), wide, wide, _row_tile_spec(CONV_DIM), _row_tile_spec(ATT_DIM),
                  _weight_spec(w_cb_g.shape), _full_spec((1, D_MODEL)), _weight_spec(w_ab_g.shape), TOKEN_SPEC],
        out_specs=[wide, wide, _row_tile_spec(2 * D_MODEL), _row_tile_spec(CONV_DIM), _row_tile_spec(ATT_DIM),
                   _full_spec((1, D_MODEL))],
        compiler_params=_params(("arbitrary",)),
    )(dmix, w_out_g, gc, ga, u3, att, w_cb_g, b_cb, w_ab_g, after)


def conv_bwd(du3, u1, ci, w_dw, ln_g, ln_b, after):
    def body(du3_ref, u1_ref, ci_ref, w_ref, g_ref, bb_ref, after_ref,
             dci_ref, dw_ref, dbdw_ref, dg_ref, db_ref, upad_ref, dpad_ref, dwacc_ref, vacc_ref):
        _glu_into(ci_ref, upad_ref)
        dpad_ref[SEQ:SEQ + 32, :] = jnp.zeros((32, CONV_DIM), F32)
        dwacc_ref[...] = jnp.zeros_like(dwacc_ref)
        vacc_ref[...] = jnp.zeros_like(vacc_ref)

        def fold8(t):
            s = t[0:8, :]
            for r in range(8, t.shape[0], 8):
                s = s + t[r:r + 8, :]
            return s

        def pass1(i, c):
            t0 = pl.multiple_of(i * CONV_TILE, CONV_TILE)
            gv = g_ref[...]
            for r in range(0, CONV_TILE, NORM_ROWS):
                rows = pl.ds(t0 + r, NORM_ROWS)
                xh, rstd = _layernorm_parts(u1_ref[rows, :])
                u2 = xh * gv + bb_ref[...]
                s2 = _sigmoid(u2)
                du2 = du3_ref[rows, :] * (s2 * (1.0 + u2 * (1.0 - s2)))
                wv = du2 * gv
                du1 = rstd * (wv - jnp.mean(wv, axis=-1, keepdims=True)
                              - xh * jnp.mean(wv * xh, axis=-1, keepdims=True))
                dpad_ref[rows, :] = du1
                vacc_ref[0] += fold8(du2 * xh)
                vacc_ref[1] += fold8(du2)
                vacc_ref[2] += fold8(du1)
            for cols in LANE_GROUPS:
                du1 = dpad_ref[pl.ds(t0, CONV_TILE), cols]
                for m, rows in _shifted_windows(upad_ref, t0, cols, range(2, CONV_WIDTH + 2)):
                    dwacc_ref[m - 2, :, cols] += fold8(du1 * rows)
            return c

        lax.fori_loop(0, SEQ // CONV_TILE, pass1, 0)

        def pass2(i, c):
            t0 = pl.multiple_of(i * CONV_TILE, CONV_TILE)
            tile = pl.ds(t0, CONV_TILE)
            for cols in LANE_GROUPS:
                gate_cols = slice(cols.start + CONV_DIM, cols.stop + CONV_DIM)
                du0 = _shifted_sum(dpad_ref, t0, cols, w_ref, lambda j: 30 - j)
                a = ci_ref[tile, cols]
                sb = _sigmoid(ci_ref[tile, gate_cols])
                dci_ref[tile, cols] = (du0 * sb).astype(MM)
                dci_ref[tile, gate_cols] = (du0 * a * (sb * (1.0 - sb))).astype(MM)
            return c

        lax.fori_loop(0, SEQ // CONV_TILE, pass2, 0)

        for j in range(CONV_WIDTH):
            dw_ref[j:j + 1, :] = jnp.sum(dwacc_ref[j], axis=0, keepdims=True)
        dw_ref[CONV_WIDTH:32, :] = jnp.zeros((32 - CONV_WIDTH, CONV_DIM), F32)
        dg_ref[...] = jnp.sum(vacc_ref[0], axis=0, keepdims=True)
        db_ref[...] = jnp.sum(vacc_ref[1], axis=0, keepdims=True)
        dbdw_ref[...] = jnp.sum(vacc_ref[2], axis=0, keepdims=True)

    vec = jax.ShapeDtypeStruct((1, CONV_DIM), F32)
    return pl.pallas_call(
        body, name="conv_bwd",
        out_shape=(jax.ShapeDtypeStruct((SEQ, 2 * CONV_DIM), MM), jax.ShapeDtypeStruct((32, CONV_DIM), F32),
                   vec, vec, vec),
        in_specs=[VMEM_SPEC] * 7, out_specs=[VMEM_SPEC] * 5,
        scratch_shapes=[pltpu.VMEM((SEQ + 32, CONV_DIM), F32), pltpu.VMEM((SEQ + 32, CONV_DIM), F32),
                        pltpu.VMEM((CONV_WIDTH, 8, CONV_DIM), F32), pltpu.VMEM((3, 8, CONV_DIM), F32)],
        compiler_params=_params(),
    )(du3, u1, ci, w_dw, ln_g, ln_b, after)


def attn_bwd(q, k, v, datt, rc, after):
    def body(q_ref, k_ref, v_ref, do_ref, rc_ref, after_ref, dqkv_ref, dqa_ref, dka_ref, dva_ref, pc_ref, z_ref,
             sig1_ref, sig2_ref, g_ref, spb_ref, gb_ref, ar_ref, dzr_ref, dzc_ref, qm_ref, km_ref, dom_ref):
        lane, row, _ = _head_masks()
        _split_heads(q_ref, qm_ref)
        _split_heads(k_ref, km_ref)
        _split_heads(do_ref, dom_ref)
        for ref in (dqa_ref, dka_ref, dva_ref, pc_ref):
            ref[...] = jnp.zeros_like(ref)
        z_ref[...] = jnp.full(z_ref.shape, NO_SCORE, F32)
        for ref in (sig1_ref, sig2_ref, spb_ref, ar_ref, g_ref, gb_ref, dzr_ref, dzc_ref):
            ref[...] = jnp.zeros_like(ref)
        w_suffix = _cumsum_weights(suffix=True, with_total=False)
        w_prefix = _cumsum_weights(suffix=False, with_total=True)

        def step(pairs):
            (ia, ja), (ib, jb), (ic, jc), (id_, jd) = pairs
            ka, qb_, kb_, kc, qd, kd = (pl.multiple_of(jnp.maximum(b, 0) * TQ, TQ) for b in (ja, ib, jb, jc, id_, jd))
            qa2, qb2, qc2, qd2, kd2 = (pl.multiple_of(jnp.maximum(b, 0) * 2 * TQ, 2 * TQ)
                                       for b in (ia, ib, ic, id_, jd))
            bias_a = _score_bias(lane, row, ia, ja)
            rc_rows = rc_ref[pl.ds(qb_, TQ), :]
            first_c = jc == 0
            for p in range(N_PAIRS):
                cols = slice(128 * p, 128 * (p + 1))
                k_a = k_ref[pl.ds(ka, TQ), cols]
                v_b = v_ref[pl.ds(kb_, TQ), cols]
                dqa_ref[pl.ds(qd, TQ), cols] += _dot(dzc_ref[p], km_ref[p, pl.ds(kd2, 2 * TQ), :])
                dka_ref[pl.ds(kd, TQ), cols] += _dot_tn(dzr_ref[p], qm_ref[p, pl.ds(qd2, 2 * TQ), :])
                dva_ref[pl.ds(kc, TQ), cols] += _dot_tn(ar_ref[p], dom_ref[p, pl.ds(qc2, 2 * TQ), :])
                for h in range(2):
                    hh = 2 * p + h
                    rows = slice(TQ * h, TQ * (h + 1))
                    r = _dot(gb_ref[hh], w_prefix)
                    p_in = jnp.where(first_c, 0.0, pc_ref[hh])
                    dz = (g_ref[hh] - sig2_ref[hh] * (r[:, :128] + p_in)).astype(MM)
                    dzc_ref[p, :, rows] = dz
                    dzr_ref[p, rows, :] = dz
                    pc_ref[hh] = p_in + r[:, 128:]
                    r_in = jnp.sum(jnp.where(lane == 16 * hh + jb, rc_rows, 0.0), axis=1, keepdims=True)
                    a = jnp.exp(z_ref[hh] - (_dot(spb_ref[hh], w_suffix) + r_in))
                    g = _dot_nt(dom_ref[p, pl.ds(qb2 + TQ * h, TQ), :], v_b) * a
                    ar_ref[p, rows, :] = a.astype(MM)
                    g_ref[hh] = g
                    gb_ref[hh] = g.astype(MM)
                    sig2_ref[hh] = sig1_ref[hh]
                    z = _dot_nt(qm_ref[p, pl.ds(qa2 + TQ * h, TQ), :], k_a) + bias_a
                    sp = _softplus(z)
                    sig1_ref[hh] = jnp.exp(z - sp)
                    z_ref[hh] = z
                    spb_ref[hh] = sp.astype(MM)

        _block_pipeline(4, False, step)
        dqkv_ref[:, 0:ATT_DIM] = (dqa_ref[...] * ATT_SCALE).astype(MM)
        dqkv_ref[:, ATT_DIM:2 * ATT_DIM] = dka_ref[...].astype(MM)
        dqkv_ref[:, 2 * ATT_DIM:3 * ATT_DIM] = dva_ref[...].astype(MM)

    split = pltpu.VMEM((N_PAIRS, 2 * SEQ, 128), MM)
    return pl.pallas_call(
        body, name="attn_bwd", out_shape=jax.ShapeDtypeStruct((SEQ, 3 * ATT_DIM), MM),
        in_specs=[VMEM_SPEC] * 6, out_specs=VMEM_SPEC,
        scratch_shapes=[pltpu.VMEM((SEQ, ATT_DIM), F32)] * 3 + [pltpu.VMEM((8, TQ, 128), F32)] * 5
                       + [pltpu.VMEM((8, TQ, 128), MM)] * 2
                       + [pltpu.VMEM((N_PAIRS, 2 * TQ, 128), MM)] * 2 + [pltpu.VMEM((N_PAIRS, TQ, 256), MM)]
                       + [split] * 3,
        compiler_params=_params(),
    )(q, k, v, datt, rc, after)


DPROJ_PIECES = ((0, 1024), (1024, 2560), (2560, 4608))


def _dproj_segments(j):
    g0, g1 = j * IN_SHARD, (j + 1) * IN_SHARD
    segs = []
    for p, (s, e) in enumerate(DPROJ_PIECES):
        lo, hi = max(s, g0), min(e, g1)
        if lo < hi:
            segs.append((p, lo - s, lo - g0, hi - lo))
    return segs


def in_proj_bwd(pieces, w_in_g, x, dx2, g1, after):
    def body(p0_ref, p1_ref, p2_ref, w_ref, x_ref, dx2_ref, g_ref, after_ref, dx_ref, dg_ref):
        p_refs = (p0_ref, p1_ref, p2_ref)
        dh = None
        for j in range(N_CHIPS):
            for p, lo, off, width in _dproj_segments(j):
                t = _dot_nt(p_refs[p][:, lo:lo + width], w_ref[j, :, off:off + width])
                dh = t if dh is None else dh + t
        n1, r1 = _rms(x_ref[...])
        dx_ref[...] = dx2_ref[...] + _rms_bwd(dh * g_ref[...], n1, r1)
        _acc_rows(dg_ref, jnp.sum(dh * n1, axis=0, keepdims=True), pl.program_id(0) == 0)

    vec = _full_spec((1, D_MODEL))
    return pl.pallas_call(
        body, name="in_proj_bwd", grid=(SEQ // TM,),
        out_shape=[jax.ShapeDtypeStruct((SEQ, D_MODEL), F32), jax.ShapeDtypeStruct((1, D_MODEL), F32)],
        in_specs=[_row_tile_spec(p.shape[1]) for p in pieces]
                 + [_weight_spec(w_in_g.shape), _row_tile_spec(D_MODEL), _row_tile_spec(D_MODEL), vec, TOKEN_SPEC],
        out_specs=[_row_tile_spec(D_MODEL), vec],
        compiler_params=_params(("arbitrary",)),
    )(*pieces, w_in_g, x, dx2, g1, after)


def weight_grad_in(h1, pieces):
    kh = D_MODEL // 2

    def body(a_ref, p0_ref, p1_ref, p2_ref, o_ref):
        p_refs = (p0_ref, p1_ref, p2_ref)
        a = a_ref[...]
        for j in range(N_CHIPS):
            @pl.when(pl.program_id(1) == j)
            def _():
                for p, lo, off, width in _dproj_segments(j):
                    o_ref[0, 0, :, off:off + width] = _dot_tn(a, p_refs[p][:, lo:lo + width]).astype(MM)

    return pl.pallas_call(
        body, name="dw_in", grid=(2, N_CHIPS), out_shape=jax.ShapeDtypeStruct((N_CHIPS, 2, kh, IN_SHARD), MM),
        in_specs=[pl.BlockSpec((SEQ, kh), lambda h, j: (0, h))] + [_weight_spec(p.shape) for p in pieces],
        out_specs=pl.BlockSpec((1, 1, kh, IN_SHARD), lambda h, j: (j, h, 0, 0)),
        compiler_params=_params(("arbitrary", "arbitrary")),
    )(h1, *pieces)


def weight_grad(a, b, name, col_sharded, tk=None):
    kin, n = a.shape[1], b.shape[1]

    def body(a_ref, b_ref, o_ref):
        if col_sharded:
            o_ref[0, 0] = _dot_tn(a_ref[...], b_ref[...]).astype(MM)
        else:
            o_ref[...] = _dot_tn(a_ref[...], b_ref[...]).astype(MM)

    if col_sharded:
        kh, ns = kin // 2, n // N_CHIPS
        out = jax.ShapeDtypeStruct((N_CHIPS, 2, kh, ns), MM)
        grid = (2, N_CHIPS)
        in_specs = [pl.BlockSpec((SEQ, kh), lambda h, j: (0, h)), pl.BlockSpec((SEQ, ns), lambda h, j: (0, j))]
        out_spec = pl.BlockSpec((1, 1, kh, ns), lambda h, j: (j, h, 0, 0))
        sem = ("arbitrary", "arbitrary")
    else:
        out = jax.ShapeDtypeStruct((kin, n), MM)
        grid = (kin // tk,)
        in_specs = [pl.BlockSpec((SEQ, tk), lambda r: (0, r)), pl.BlockSpec((SEQ, n), lambda r: (0, 0))]
        out_spec = pl.BlockSpec((tk, n), lambda r: (r, 0))
        sem = ("arbitrary",)
    res = pl.pallas_call(
        body, name=name, grid=grid, out_shape=out, in_specs=in_specs, out_specs=out_spec,
        compiler_params=_params(sem),
    )(a, b)
    if not col_sharded:
        res = res.reshape(N_CHIPS, 2, kin // (2 * N_CHIPS), n)
    return res


def weight_grad_mix(merged, dmix, u3, dco, att, dao):
    operands = (merged, dmix, u3, dco, att, dao)
    n_out, n_br = D_MODEL // 2, CONV_DIM // 2

    def body(*refs):
        hbm, (o_out, o_cb, o_ab), bufs, sems = refs[:6], refs[6:9], refs[9:15], refs[15]
        copies = [pltpu.make_async_copy(hbm[i], bufs[i], sems.at[i]) for i in range(6)]
        for cp in copies:
            cp.start()
        m_ref, dm_ref, u_ref, dco_ref, a_ref, dao_ref = bufs
        copies[0].wait()
        copies[1].wait()
        for h in range(2):
            o_out[h * n_out:(h + 1) * n_out, :] = _dot_tn(m_ref[:, h * n_out:(h + 1) * n_out], dm_ref[...]).astype(MM)
        for br, (a, d, o) in enumerate(((u_ref, dco_ref, o_cb), (a_ref, dao_ref, o_ab))):
            copies[2 + 2 * br].wait()
            copies[3 + 2 * br].wait()
            for h in range(2):
                g = _dot_tn(a[:, h * n_br:(h + 1) * n_br], d[...])
                for j in range(N_CHIPS):
                    o[j, h] = g[:, j * BR_SHARD:(j + 1) * BR_SHARD].astype(MM)

    branch = jax.ShapeDtypeStruct((N_CHIPS, 2, n_br, BR_SHARD), MM)
    dw_out, dw_cb, dw_ab = pl.pallas_call(
        body, name="dw_mix", out_shape=[jax.ShapeDtypeStruct((D_MODEL, D_MODEL), MM), branch, branch],
        in_specs=[ANY] * 6, out_specs=[VMEM_SPEC] * 3,
        scratch_shapes=[pltpu.VMEM(a.shape, a.dtype) for a in operands] + [pltpu.SemaphoreType.DMA((6,))],
        compiler_params=_params(),
    )(*operands)
    return dw_out.reshape(N_CHIPS, 2, D_MODEL // (2 * N_CHIPS), D_MODEL), dw_cb, dw_ab


def _place():
    x, y, c = lax.axis_index("x"), lax.axis_index("y"), lax.axis_index("c")
    chips = [(1 - x, y), (x, 1 - y), (1 - x, 1 - y)]
    return x, y, c, chips


def _rcopy(src, dst, send_sem, recv_sem, dev):
    return pltpu.make_async_remote_copy(src_ref=src, dst_ref=dst, send_sem=send_sem, recv_sem=recv_sem,
                                        device_id=dev, device_id_type=MESH)


class _Gather:
    N_MOVES = 6

    def __init__(self, shapes, w, o, scratch):
        self.n, self.shapes, self.w, self.o = len(w), shapes, w, o
        self.send, self.recv, self.psend, self.precv, self.loc_in, self.loc_out = scratch[:6]
        self.raw, self.stage = scratch[6:6 + self.n], scratch[6 + self.n:]
        x, y, c, self.chips = _place()
        self.c = c
        self.me, k_x, k_y, k_far = 2 * x + y, 2 * (1 - x) + y, 2 * x + (1 - y), 2 * (1 - x) + (1 - y)
        to_x, to_y = (1 - x, y, c), (x, 1 - y, c)
        self.sib = (x, y, 1 - c)
        self.sent_as = [(self.me, 0, to_x), (self.me, 1, to_y), (self.me, 1, to_x), (self.me, 0, to_y),
                        (k_x, 0, to_y), (k_y, 1, to_x)]
        self.arrives_as = [(k_x, 0, to_x), (k_y, 1, to_y), (k_x, 1, to_x), (k_y, 0, to_y),
                           (k_far, 0, to_y), (k_far, 1, to_x)]
        self.sent_on_after = {0: 4, 1: 5}

    @staticmethod
    def scratch(shards):
        n = len(shards)
        sems = pltpu.SemaphoreType.DMA
        m = _Gather.N_MOVES * n
        return ([sems((m,)), sems((m,)), sems((m,)), sems((m,)), sems((3 * n,)), sems((n,))]
                + [pltpu.VMEM(s.shape, s.dtype) for s in shards] + [pltpu.VMEM(s.shape, MM) for s in shards])

    @staticmethod
    def out_shapes(shards):
        return [jax.ShapeDtypeStruct((N_CHIPS,) + s.shape, MM) for s in shards]

    def _rows(self, t, quarter, cc):
        rq = self.shapes[t][0] // 4
        return pl.ds(pl.multiple_of((2 * cc + quarter) * rq, rq), rq)

    def _own_rows(self, t, piece):
        if piece < 2:
            return self._rows(t, piece, self.c)
        rh = self.shapes[t][0] // 2
        return pl.ds(pl.multiple_of((1 - self.c) * rh, rh), rh)

    def _chip(self, j):
        cx, cy = self.chips[j]
        return 2 * cx + cy, (cx, cy, self.c)

    def local_in(self, t, piece):
        rows = self._own_rows(t, piece)
        return pltpu.make_async_copy(self.w[t].at[rows, :], self.raw[t].at[rows, :], self.loc_in.at[3 * t + piece])

    def local_out(self, t):
        return pltpu.make_async_copy(self.stage[t], self.o[t].at[self.me], self.loc_out.at[t])

    def sent(self, i, t):
        k, quarter, dev = self.sent_as[i]
        rows = self._rows(t, quarter, self.c)
        there = self.o[t].at[k, rows, :]
        return _rcopy(self.stage[t].at[rows, :] if i < 4 else there, there,
                      self.send.at[i * self.n + t], self.recv.at[i * self.n + t], dev)

    def arrived(self, i, t):
        k, quarter, dev = self.arrives_as[i]
        blk = self.o[t].at[k, self._rows(t, quarter, self.c), :]
        return _rcopy(blk, blk, self.send.at[i * self.n + t], self.recv.at[i * self.n + t], dev)

    def passed(self, i, t, cc):
        k, quarter, _ = self.arrives_as[i]
        blk = self.o[t].at[k, self._rows(t, quarter, cc), :]
        return _rcopy(blk, blk, self.psend.at[i * self.n + t], self.precv.at[i * self.n + t], self.sib)

    def start(self):
        for piece in range(3):
            for t in range(self.n):
                self.local_in(t, piece).start()
        for piece, moves in enumerate(((0, 3), (1, 2), ())):
            for t in range(self.n):
                rows = self._own_rows(t, piece)
                self.local_in(t, piece).wait()
                self.stage[t][rows, :] = self.raw[t][rows, :].astype(MM)
                for i in moves:
                    self.sent(i, t).start()
        for t in range(self.n):
            self.local_out(t).start()

    def forward(self):
        for i in range(self.N_MOVES):
            for t in range(self.n):
                self.arrived(i, t).wait_recv()
                if i in self.sent_on_after:
                    self.sent(self.sent_on_after[i], t).start()
                self.passed(i, t, self.c).start()

    def finish(self):
        for i in range(self.N_MOVES):
            for t in range(self.n):
                self.passed(i, t, 1 - self.c).wait_recv()
        for i in range(self.N_MOVES):
            for t in range(self.n):
                self.sent(i, t).wait_send()
                self.passed(i, t, self.c).wait_send()
        for t in range(self.n):
            self.local_out(t).wait()


def all_gather_weights(shards, small, later):
    n, m = len(shards), len(later)
    shapes = [s.shape for s in shards]

    def body(*refs):
        w = refs[:n]
        sm = refs[n]
        lw = refs[n + 1:n + 1 + m]
        o = refs[n + 1 + m:2 * n + 1 + m]
        osm = refs[2 * n + 1 + m]
        lo = refs[2 * n + 2 + m:2 * n + 2 + 2 * m]
        scratch = refs[2 * n + 2 + 2 * m:]
        ssend, srecv, sloc, lsem_in, lsem_out = scratch[:5]
        lraw, lstage = scratch[5:5 + m], scratch[5 + m:5 + 2 * m]
        g = _Gather(shapes, w, o, scratch[5 + 2 * m:])
        own = pltpu.make_async_copy(sm, osm.at[g.me], sloc)
        own.start()
        g.start()
        loads = [pltpu.make_async_copy(lw[t], lraw[t], lsem_in.at[t]) for t in range(m)]
        for cp in loads:
            cp.start()
        small_cps = [_rcopy(sm, osm.at[g.me], ssend.at[j], srecv.at[j], g._chip(j)[1]) for j in range(3)]
        for cp in small_cps:
            cp.start()
        places = []
        for t in range(m):
            loads[t].wait()
            lstage[t][...] = lraw[t][...].astype(MM)
            places.append(pltpu.make_async_copy(lstage[t], lo[t].at[g.me], lsem_out.at[t]))
            places[t].start()
        g.forward()
        g.finish()
        for j in range(3):
            k, dev = g._chip(j)
            _rcopy(sm, osm.at[k], ssend.at[j], srecv.at[j], dev).wait_recv()
            small_cps[j].wait_send()
        own.wait()
        for cp in places:
            cp.wait()

    out_shape = _Gather.out_shapes(shards)
    out_shape.append(jax.ShapeDtypeStruct((N_CHIPS,) + small.shape, small.dtype))
    out_shape += _Gather.out_shapes(later)
    sems = pltpu.SemaphoreType.DMA
    return pl.pallas_call(
        body, name="all_gather_weights", out_shape=out_shape,
        in_specs=[ANY] * (n + 1 + m), out_specs=[ANY] * (n + 1 + m),
        scratch_shapes=[sems((3,)), sems((3,)), sems, sems((m,)), sems((m,))]
                       + [pltpu.VMEM(s.shape, s.dtype) for s in later] + [pltpu.VMEM(s.shape, MM) for s in later]
                       + _Gather.scratch(shards),
        compiler_params=_params(),
    )(*shards, small, *later)


HBM_SPEC = pl.BlockSpec(memory_space=pltpu.HBM)
SEM_SPEC = pl.BlockSpec(memory_space=pltpu.SEMAPHORE)
DATAFLOW = pltpu.SideEffectType.DATAFLOW_SIDE_EFFECTING


def split_start(name, bufs, n_copies, copies):
    nb = len(bufs)

    def body(*refs):
        for cp in copies(refs[:nb], refs[nb], refs[nb + 1]):
            cp.start()
        token = refs[2 * nb + 2]
        token[...] = jnp.zeros_like(token)

    sems = [pltpu.SemaphoreType.DMA((n_copies,))] * 2
    res = pl.pallas_call(
        body, name=name,
        out_shape=sems + [pltpu.HBM(a.shape, a.dtype) for a in bufs] + [jax.ShapeDtypeStruct((8, 128), F32)],
        in_specs=[HBM_SPEC] * nb, out_specs=[SEM_SPEC] * 2 + [HBM_SPEC] * nb + [VMEM_SPEC],
        input_output_aliases={i: 2 + i for i in range(nb)},
        compiler_params=pltpu.CompilerParams(has_side_effects=DATAFLOW),
    )(*[pltpu.with_memory_space_constraint(a, pltpu.HBM) for a in bufs])
    return res[:-1], res[-1]


def split_wait(name, state, after, copies):
    sems, bufs = state[:2], state[2:]
    nb = len(bufs)

    def body(*refs):
        for cp in copies(refs[:nb], refs[nb], refs[nb + 1]):
            cp.wait_send()
            cp.wait_recv()

    return pl.pallas_call(
        body, name=name, out_shape=[pltpu.HBM(a.shape, a.dtype) for a in bufs],
        in_specs=[HBM_SPEC] * nb + [SEM_SPEC] * 2 + [ANY] * len(after), out_specs=[HBM_SPEC] * nb,
        input_output_aliases={i: i for i in range(nb)},
        compiler_params=pltpu.CompilerParams(has_side_effects=DATAFLOW),
    )(*bufs, *sems, *after)


class _Shifted:
    def __init__(self, sems, first):
        self.sems, self.first = sems, first

    @property
    def at(self):
        return self

    def __getitem__(self, i):
        return self.sems.at[self.first + i]


def _scatter_copies(n):
    def copies(refs, send, recv):
        _, _, c, chips = _place()
        return [_rcopy(refs[t].at[2 * cx + cy], refs[n + t].at[j], send.at[3 * t + j], recv.at[3 * t + j], (cx, cy, c))
                for t in range(n) for j, (cx, cy) in enumerate(chips)]
    return copies


def scatter_start(parts, tag):
    lands = [lax.empty((3,) + p.shape[1:], p.dtype) for p in parts]
    return split_start("scatter_start_" + tag, list(parts) + lands, 3 * len(parts), _scatter_copies(len(parts)))


def scatter_wait(state, after, tag):
    n = (len(state) - 2) // 2
    return split_wait("scatter_wait_" + tag, state, after, _scatter_copies(n))[n:]


def _gather_copies(shapes, level):
    n = len(shapes)

    def copies(refs, send, recv):
        x, y, c, chips = _place()
        out = []
        for t in range(n):
            rh = shapes[t][0] // 2
            for j, (cx, cy) in enumerate(chips):
                k, dev = (2 * x + y, (cx, cy, c)) if level == 1 else (2 * cx + cy, (x, y, 1 - c))
                blk = refs[t].at[k, pl.ds(c * rh, rh), :]
                out.append(_rcopy(blk, blk, send.at[3 * t + j], recv.at[3 * t + j], dev))
        return out
    return copies


def _sibling_copies(n, other_half):
    def copies(refs, send, recv):
        x, y, c, _ = _place()
        return [_rcopy(refs[t].at[:, 1 - c] if other_half else refs[t], refs[n + t], send.at[t], recv.at[t],
                       (x, y, 1 - c)) for t in range(n)]
    return copies


def sibling_start(srcs, other_half, tag):
    lands = [lax.empty((a.shape[0],) + a.shape[2:] if other_half else a.shape, a.dtype) for a in srcs]
    return split_start("sibling_start_" + tag, list(srcs) + lands, len(srcs),
                       _sibling_copies(len(srcs), other_half))


def sibling_wait(state, after, other_half, tag):
    n = (len(state) - 2) // 2
    res = split_wait("sibling_wait_" + tag, state, after, _sibling_copies(n, other_half))
    return res[:n], res[n:]


def small_pack(ddw, v512, v1024, loss_parts):
    rows, width = PACK_ROWS, 512
    n512, n1024 = len(VEC512), len(VEC1024)

    def body(*refs):
        ddw_ref = refs[0]
        a_refs = refs[1:1 + n512]
        b_refs = refs[1 + n512:1 + n512 + n1024]
        lp_ref, o_ref, p_ref = refs[1 + n512 + n1024:]
        p_ref[...] = jnp.zeros_like(p_ref)
        p_ref[0:32, :] = ddw_ref[...]
        p_ref[LOSS_ROW:LOSS_ROW + 1, 0:128] = jnp.sum(lp_ref[...], axis=0, keepdims=True) * 0.125
        for i, r in enumerate(a_refs):
            p_ref[32 + i:33 + i, :] = r[...]
        for i, r in enumerate(b_refs):
            base = 32 + n512 + 2 * i
            p_ref[base:base + 1, :] = r[:, 0:512]
            p_ref[base + 1:base + 2, :] = r[:, 512:1024]
        x, y, c, _ = _place()
        o_ref[4 * x + 2 * y + c] = p_ref[...]

    n_in = 2 + n512 + n1024
    return pl.pallas_call(
        body, name="small_pack", out_shape=jax.ShapeDtypeStruct((8, rows, width), F32),
        in_specs=[VMEM_SPEC] * n_in, out_specs=VMEM_SPEC,
        scratch_shapes=[pltpu.VMEM((rows, width), F32)],
    )(ddw, *[v512[n] for n in VEC512], *[v1024[n] for n in VEC1024], loss_parts)


def _small_copies(refs, send, recv):
    x, y, c, _ = _place()
    mine = refs[0].at[4 * x + 2 * y + c]
    peers = [(1 - x if k & 4 else x, 1 - y if k & 2 else y, 1 - c if k & 1 else c) for k in range(1, 8)]
    return [_rcopy(mine, mine, send.at[i], recv.at[i], dev) for i, dev in enumerate(peers)]


def _row_block(r):
    for tr in (512, 352, 256, 128):
        if r % tr == 0:
            return tr
    return r


def add_halves(g, recv, name):
    _, _, r, w = g.shape
    tr = _row_block(r)

    def body(g_ref, r_ref, ob_ref, own_ref):
        k = pl.program_id(1)
        me = 2 * lax.axis_index("x") + lax.axis_index("y")
        t = g_ref[0, 0].astype(F32) + r_ref[0].astype(F32)
        ob_ref[0] = t.astype(MM)
        mine = jnp.where(k == me, t, 0.0)

        @pl.when(k == 0)
        def _():
            own_ref[...] = mine

        @pl.when(k != 0)
        def _():
            own_ref[...] += mine

    return pl.pallas_call(
        body, name=name, grid=(r // tr, N_CHIPS),
        in_specs=[pl.BlockSpec((1, 1, tr, w), lambda i, k: (k, lax.axis_index("c"), i, 0)),
                  pl.BlockSpec((1, tr, w), lambda i, k: (k, i, 0))],
        out_specs=[pl.BlockSpec((1, tr, w), lambda i, k: (k, i, 0)),
                   pl.BlockSpec((tr, w), lambda i, k: (i, 0))],
        out_shape=(jax.ShapeDtypeStruct((N_CHIPS, r, w), MM), jax.ShapeDtypeStruct((r, w), F32)),
        compiler_params=_params(("arbitrary", "arbitrary")),
    )(g, recv)


def sum_parts(own, rin, after, name):
    _, r, w = rin.shape
    tr = _row_block(r)

    def body(o_ref, r_ref, after_ref, out_ref):
        out_ref[...] = ((o_ref[...] + r_ref[0].astype(F32)) + r_ref[1].astype(F32)) + r_ref[2].astype(F32)

    return pl.pallas_call(
        body, name=name, grid=(r // tr,), out_shape=jax.ShapeDtypeStruct((r, w), F32),
        in_specs=[pl.BlockSpec((tr, w), lambda i: (i, 0)), pl.BlockSpec((3, tr, w), lambda i: (0, i, 0)),
                  _full_spec((8, 128))],
        out_specs=pl.BlockSpec((tr, w), lambda i: (i, 0)),
        compiler_params=_params(("arbitrary",)),
    )(own, rin, after)


def _adamw_math(w, g, m, v):
    mn = ADAM_B1 * m + (1.0 - ADAM_B1) * g
    vn = ADAM_B2 * v + (1.0 - ADAM_B2) * (g * g)
    m_hat = mn / (1.0 - ADAM_B1 ** ADAM_STEP)
    v_hat = vn / (1.0 - ADAM_B2 ** ADAM_STEP)
    return -ADAM_LR * (m_hat / (jnp.sqrt(v_hat) + ADAM_EPS) + ADAM_WD * w), mn, vn


def adamw(w, mine, other, m, v, name):
    r, c = w.shape
    rh = r // 2
    tr = _row_block(rh)
    if c >= 1024 and tr % 512 == 0:
        tr = 256
    nb = rh // tr

    def body(w_ref, a_ref, b_ref, m_ref, v_ref, go_ref, d_ref, mo_ref, vo_ref):
        gv = jnp.where(lax.axis_index("c") == pl.program_id(0), a_ref[...], b_ref[...])
        go_ref[...] = gv
        d_ref[...], mo_ref[...], vo_ref[...] = _adamw_math(w_ref[...], gv, m_ref[...], v_ref[...])

    def half(of_sibling):
        def index(h, i):
            owner = lax.axis_index("c")
            owner = 1 - owner if of_sibling else owner
            return jnp.where(h == owner, i, jnp.where(h < owner, 0, nb - 1)), 0
        return pl.BlockSpec((tr, c), index)

    spec = pl.BlockSpec((tr, c), lambda h, i: (h * nb + i, 0))
    out = jax.ShapeDtypeStruct((r, c), F32)
    return pl.pallas_call(
        body, name=name, grid=(2, nb), out_shape=(out, out, out, out),
        in_specs=[spec, half(False), half(True), spec, spec], out_specs=[spec] * 4,
        compiler_params=_params(("arbitrary", "arbitrary")),
    )(w, mine, other, m, v)


def adamw_small(packs, params, after):
    names = list(params)
    flat = [a for n in names for a in params[n]]

    def body(*refs):
        p_ref = refs[0]
        ins = refs[1:1 + 3 * len(names)]
        loss_ref, g_ref = refs[2 + 3 * len(names):4 + 3 * len(names)]
        outs = refs[4 + 3 * len(names):]
        total = p_ref[0]
        for d in range(1, 8):
            total = total + p_ref[d]
        g_ref[...] = total
        loss_ref[...] = g_ref[LOSS_ROW:LOSS_ROW + 1, 0:1]
        me = 2 * lax.axis_index("x") + lax.axis_index("y")
        for i, n in enumerate(names):
            w_ref, m_ref, v_ref = ins[3 * i:3 * i + 3]
            go_ref, d_ref, mo_ref, vo_ref = outs[4 * i:4 * i + 4]
            if n == "conv_dw_w":
                gv = jnp.zeros((CONV_WIDTH, 128), F32)
                for k in range(N_CHIPS):
                    gv = gv + jnp.where(me == k, g_ref[0:CONV_WIDTH, 128 * k:128 * (k + 1)], 0.0)
            elif n in VEC512:
                r0 = 32 + VEC512.index(n)
                gv = g_ref[r0:r0 + 1, :]
            else:
                r0 = 32 + len(VEC512) + 2 * VEC1024.index(n)
                gv = jnp.concatenate([g_ref[r0:r0 + 1, :], g_ref[r0 + 1:r0 + 2, :]], axis=1)
            go_ref[...] = gv
            d_ref[...], mo_ref[...], vo_ref[...] = _adamw_math(w_ref[...], gv, m_ref[...], v_ref[...])

    out_shape = [jax.ShapeDtypeStruct((1, 1), F32), jax.ShapeDtypeStruct(packs.shape[1:], F32)]
    out_shape += [jax.ShapeDtypeStruct(params[n][0].shape, F32) for n in names for _ in range(4)]
    res = pl.pallas_call(
        body, name="adamw_small", out_shape=out_shape,
        in_specs=[VMEM_SPEC] * (2 + len(flat)), out_specs=[VMEM_SPEC] * len(out_shape),
        compiler_params=_params(),
    )(packs, *flat, after)
    return res[0], res[1], {n: res[2 + 4 * i:6 + 4 * i] for i, n in enumerate(names)}


REST = ("w_ffn_up", "w_ffn_down", "w_out", "w_conv_branch", "w_att_branch")
VEC512 = ("conv_dw_b", "conv_ln_g", "conv_ln_b")
VEC1024 = ("norm_mix_pre", "b_conv_branch", "norm_mix_post", "norm_ffn_pre", "norm_ffn_post")
PACK_ROWS = 48
LOSS_ROW = 47


def kernel(x, norm_mix_pre, w_in, conv_dw_w, conv_dw_b, conv_ln_g, conv_ln_b, w_conv_branch, b_conv_branch, w_att_branch, w_out, norm_mix_post, norm_ffn_pre, w_ffn_up, w_ffn_down, norm_ffn_post, loss_target, m_norm_mix_pre, m_w_in, m_conv_dw_w, m_conv_dw_b, m_conv_ln_g, m_conv_ln_b, m_w_conv_branch, m_b_conv_branch, m_w_att_branch, m_w_out, m_norm_mix_post, m_norm_ffn_pre, m_w_ffn_up, m_w_ffn_down, m_norm_ffn_post, v_norm_mix_pre, v_w_in, v_conv_dw_w, v_conv_dw_b, v_conv_ln_g, v_conv_ln_b, v_w_conv_branch, v_b_conv_branch, v_w_att_branch, v_w_out, v_norm_mix_post, v_norm_ffn_pre, v_w_ffn_up, v_w_ffn_down, v_norm_ffn_post):
    weights = dict(norm_mix_pre=norm_mix_pre, w_in=w_in, conv_dw_w=conv_dw_w, conv_dw_b=conv_dw_b, conv_ln_g=conv_ln_g, conv_ln_b=conv_ln_b, w_conv_branch=w_conv_branch, b_conv_branch=b_conv_branch, w_att_branch=w_att_branch, w_out=w_out, norm_mix_post=norm_mix_post, norm_ffn_pre=norm_ffn_pre, w_ffn_up=w_ffn_up, w_ffn_down=w_ffn_down, norm_ffn_post=norm_ffn_post)
    mom = dict(norm_mix_pre=m_norm_mix_pre, w_in=m_w_in, conv_dw_w=m_conv_dw_w, conv_dw_b=m_conv_dw_b, conv_ln_g=m_conv_ln_g, conv_ln_b=m_conv_ln_b, w_conv_branch=m_w_conv_branch, b_conv_branch=m_b_conv_branch, w_att_branch=m_w_att_branch, w_out=m_w_out, norm_mix_post=m_norm_mix_post, norm_ffn_pre=m_norm_ffn_pre, w_ffn_up=m_w_ffn_up, w_ffn_down=m_w_ffn_down, norm_ffn_post=m_norm_ffn_post)
    var = dict(norm_mix_pre=v_norm_mix_pre, w_in=v_w_in, conv_dw_w=v_conv_dw_w, conv_dw_b=v_conv_dw_b, conv_ln_g=v_conv_ln_g, conv_ln_b=v_conv_ln_b, w_conv_branch=v_w_conv_branch, b_conv_branch=v_b_conv_branch, w_att_branch=v_w_att_branch, w_out=v_w_out, norm_mix_post=v_norm_mix_post, norm_ffn_pre=v_norm_ffn_pre, w_ffn_up=v_w_ffn_up, w_ffn_down=v_w_ffn_down, norm_ffn_post=v_norm_ffn_post)
    order = list(weights)
    grads, deltas, new_m, new_v = {}, {}, {}, {}
    xs = x.reshape(SEQ, D_MODEL)
    tgt = loss_target.reshape(SEQ, D_MODEL)
    row = lambda a: a.reshape(1, -1)
    g1, g2, g3, g4 = (row(weights[n]) for n in ("norm_mix_pre", "norm_mix_post", "norm_ffn_pre", "norm_ffn_post"))
    ln_g, ln_b = row(conv_ln_g), row(conv_ln_b)

    summed, from_chips = {}, {}

    def core_sums(names, state, after, tag):
        own, from_sibling = sibling_wait(state, after, True, tag)
        for n, g, r in zip(names, own, from_sibling):
            summed[n] = add_halves(g, r, "add_" + n)

    def chip_sums(names, after):
        return [sum_parts(summed[n][1], from_chips[n], after, "sum_" + n) for n in names]

    def optimize(names, state, after, tag):
        mine, other = sibling_wait(state, after, False, tag)
        for n, a, b in zip(names, mine, other):
            grads[n], deltas[n], new_m[n], new_v[n] = adamw(weights[n], a, b, mom[n], var[n], "adamw_" + n)

    w_in_g, dw_g, *rest = all_gather_weights([w_in], conv_dw_w, [weights[n] for n in REST])
    w_dw_full = jnp.concatenate([dw_g[k] for k in range(N_CHIPS)], axis=1)
    rest_shapes = [weights[n].shape for n in REST]
    state, token = split_start("gather_start", rest, 3 * len(REST), _gather_copies(rest_shapes, 1))
    h1, ci, q, k, v, gc, ga = in_proj_fwd(xs, g1, w_in_g, token)
    u1, u3 = conv_fwd(ci, w_dw_full, row(conv_dw_b), ln_g, ln_b)
    att, rc = attn_fwd(q, k, v)
    rest = split_wait("gather_wait", state, [att], _gather_copies(rest_shapes, 1))
    pass_copies = _gather_copies(rest_shapes[2:] + rest_shapes[:2], 2)
    n_first = 3 * len(REST[2:])
    state, token = split_start("pass_start", rest[2:] + rest[:2], 3 * len(REST), pass_copies)
    passed = split_wait("pass_mix_wait", state, [], lambda *a: pass_copies(*a)[:n_first])
    w_out_g, w_cb_g, w_ab_g = passed[:3]
    w_out_g = w_out_g.reshape(D_MODEL, D_MODEL)
    merged, mix, x2, h2 = mix_fwd(u3, att, gc, ga, xs, w_cb_g, row(b_conv_branch), w_ab_g, w_out_g, g2, g3, token)
    w_up_g, w_down_g = split_wait(
        "pass_ffn_wait", list(state[:2]) + list(passed[3:]), [h2],
        lambda refs, send, recv: _gather_copies(rest_shapes[:2], 2)(refs, _Shifted(send, n_first), _Shifted(recv, n_first)))
    w_down_g = w_down_g.reshape(D_FF, D_MODEL)
    gate, up, act = ffn_up_fwd(h2, w_up_g)
    dff, dy, loss_parts, dg4 = ffn_down_loss(act, w_down_g, x2, tgt, g4)

    dgu = ffn_act_bwd(dff, w_down_g, gate, up)
    dx2, dmix, dg3, dg2 = ffn_in_bwd(dgu, w_up_g, x2, mix, dy, g3, g2)
    ffn_grads = [weight_grad(h2, dgu, "dw_ffn_up", True), weight_grad(act, dff, "dw_ffn_down", False, tk=UP_SHARD)]
    to_ffn, token = sibling_start(ffn_grads, True, "dw_ffn")
    dco, dao, dg, du3, datt, dbcb = merge_bwd(dmix, w_out_g, gc, ga, u3, att, w_cb_g, row(b_conv_branch), w_ab_g,
                                              token)
    to_mix, token = sibling_start(weight_grad_mix(merged, dmix, u3, dco, att, dao), True, "dw_mix")
    core_sums(REST[:2], to_ffn, [token], "dw_ffn")
    core_sums(REST[2:], to_mix, [summed["w_ffn_down"][1]], "dw_mix")
    state, token = scatter_start([summed[n][0] for n in REST], "rest")
    dci, ddw, dbdw, dlng, dlnb = conv_bwd(du3, u1, ci, w_dw_full, ln_g, ln_b, token)
    dqkv = attn_bwd(q, k, v, datt, rc, token)
    from_chips.update(zip(REST, scatter_wait(state, [dci, dqkv], "rest")))
    dproj = (dci, dqkv, dg)
    to_in, token = sibling_start([weight_grad_in(h1, dproj)], True, "dw_in")
    grad_x, dg1 = in_proj_bwd(dproj, w_in_g, xs, dx2, g1, token)
    v512 = dict(conv_dw_b=dbdw, conv_ln_g=dlng, conv_ln_b=dlnb)
    v1024 = dict(norm_mix_pre=dg1, b_conv_branch=dbcb, norm_mix_post=dg2, norm_ffn_pre=dg3, norm_ffn_post=dg4)
    packs = small_pack(ddw, v512, v1024, loss_parts)
    core_sums(("w_in",), to_in, [packs], "dw_in")
    to_chips = summed["w_in"][0]
    landing = lax.empty((3,) + to_chips.shape[1:], to_chips.dtype)

    def scatter_and_packs(refs, send, recv):
        return (_scatter_copies(1)(refs[:2], send, recv)
                + _small_copies(refs[2:], _Shifted(send, 3), _Shifted(recv, 3)))

    state, token = split_start("scatter_start_w_in", [to_chips, landing, packs], 3 + 7, scatter_and_packs)
    swap_up, token = sibling_start(chip_sums(REST[:1], token), False, "sum_ffn_up")
    swap_rest, token = sibling_start(chip_sums(REST[1:], token), False, "sum_rest")
    optimize(REST[:1], swap_up, [token], "sum_ffn_up")
    optimize(REST[1:], swap_rest, [new_v["w_ffn_up"]], "sum_rest")
    _, from_chips["w_in"], packs = split_wait("scatter_wait_w_in", state, [new_v[n] for n in REST], scatter_and_packs)
    swap_in, token = sibling_start(chip_sums(("w_in",), token), False, "sum_w_in")
    as_rows = lambda n, a: a if n == "conv_dw_w" else a.reshape(1, -1)
    small_names = ("conv_dw_w",) + VEC512 + VEC1024
    loss, gsum, small = adamw_small(
        packs, {n: tuple(as_rows(n, d[n]) for d in (weights, mom, var)) for n in small_names}, token)
    optimize(("w_in",), swap_in, [gsum], "sum_w_in")
    for n in small_names:
        grads[n], deltas[n], new_m[n], new_v[n] = (a.reshape(weights[n].shape) for a in small[n])

    return (loss.reshape(()), grad_x.reshape(1, SEQ, D_MODEL),*[grads[n] for n in order], *[deltas[n] for n in order],
            *[new_m[n] for n in order], *[new_v[n] for n in order])
```

```python
import jax
import jax.numpy as jnp
from jax import lax
from jax.experimental import pallas as pl
from jax.experimental.pallas import tpu as pltpu

F32 = jnp.float32
MM = jnp.bfloat16

SEQ = 2048
D_MODEL = 1024
CONV_DIM = 512
ATT_DIM = 512
CONV_WIDTH = 31
D_FF = 2816
IN_COLS = 2 * CONV_DIM + 3 * ATT_DIM + 2 * D_MODEL
N_CHIPS = 4
IN_SHARD = IN_COLS // N_CHIPS
UP_SHARD = 2 * D_FF // N_CHIPS
BR_SHARD = D_MODEL // N_CHIPS
EPS = 1e-6
ATT_SCALE = 0.125

TM = 256
GLU_ROWS = 256
TQ = 128
CONV_TILE = 64
CONV_WIN = CONV_TILE + 32
VMEM_LIMIT = 56 * 1024 * 1024

ADAM_LR = 0.001
ADAM_B1 = 0.9
ADAM_B2 = 0.999
ADAM_EPS = 1e-08
ADAM_WD = 0.01
ADAM_STEP = 10

MESH = pl.DeviceIdType.MESH
ANY = pl.BlockSpec(memory_space=pl.ANY)
VMEM_SPEC = pl.BlockSpec(memory_space=pltpu.VMEM)

NT_DIMS = (((1,), (1,)), ((), ()))
TN_DIMS = (((0,), (0,)), ((), ()))

IN_PIECES = (("ci", 0, 1024), ("q", 1024, 1536), ("k", 1536, 2048), ("v", 2048, 2560),
             ("gc", 2560, 3584), ("ga", 3584, 4608))


def _params(sem=None, vmem=VMEM_LIMIT):
    return pltpu.CompilerParams(dimension_semantics=sem, vmem_limit_bytes=vmem)


def _dot(a, b):
    return jnp.dot(a, b, preferred_element_type=F32)


def _dot_nt(a, b):
    return lax.dot_general(a, b, NT_DIMS, preferred_element_type=F32)


def _dot_tn(a, b):
    return lax.dot_general(a, b, TN_DIMS, preferred_element_type=F32)


def _sigmoid(x):
    return 1.0 / (1.0 + jnp.exp(-x))


def _rms(x):
    r = lax.rsqrt(jnp.mean(x * x, axis=-1, keepdims=True) + EPS)
    return x * r, r


def _rms_bwd(dy_g, n, r):
    return r * (dy_g - n * jnp.mean(dy_g * n, axis=-1, keepdims=True))


def _row_tile_spec(width, tm=TM):
    return pl.BlockSpec((tm, width), lambda i: (i, 0))


def _full_spec(shape):
    nd = len(shape)
    return pl.BlockSpec(shape, lambda *_: (0,) * nd)


def _weight_spec(shape):
    nd = len(shape)
    return pl.BlockSpec(shape, lambda *_: (0,) * nd, pipeline_mode=pl.Buffered(1))


def _acc_rows(ref, val, first):
    @pl.when(first)
    def _():
        ref[...] = val

    @pl.when(jnp.logical_not(first))
    def _():
        ref[...] += val


TOKEN_SPEC = pl.BlockSpec((8, 128), lambda *_: (0, 0))


def in_proj_fwd(x, g1, w_in_g, after):
    def body(x_ref, g_ref, w_ref, after_ref, h_ref, ci_ref, q_ref, k_ref, v_ref, gc_ref, ga_ref):
        n, _ = _rms(x_ref[...])
        h = (n * g_ref[...]).astype(MM)
        h_ref[...] = h
        outs = dict(ci=ci_ref, q=q_ref, k=k_ref, v=v_ref, gc=gc_ref, ga=ga_ref)
        for j in range(N_CHIPS):
            p = _dot(h, w_ref[j])
            g0 = j * IN_SHARD
            for name, s, e in IN_PIECES:
                lo, hi = max(s, g0), min(e, g0 + IN_SHARD)
                if lo < hi:
                    ref = outs[name]
                    part = p[:, lo - g0:hi - g0]
                    if name == "q":
                        part = part * ATT_SCALE
                    ref[:, lo - s:hi - s] = part.astype(ref.dtype)

    out_shape = [
        jax.ShapeDtypeStruct((SEQ, D_MODEL), MM),
        jax.ShapeDtypeStruct((SEQ, 2 * CONV_DIM), F32),
        jax.ShapeDtypeStruct((SEQ, ATT_DIM), MM),
        jax.ShapeDtypeStruct((SEQ, ATT_DIM), MM),
        jax.ShapeDtypeStruct((SEQ, ATT_DIM), MM),
        jax.ShapeDtypeStruct((SEQ, D_MODEL), F32),
        jax.ShapeDtypeStruct((SEQ, D_MODEL), F32),
    ]
    return pl.pallas_call(
        body, name="in_proj_fwd", grid=(SEQ // TM,), out_shape=out_shape,
        in_specs=[_row_tile_spec(D_MODEL), _full_spec((1, D_MODEL)), _weight_spec(w_in_g.shape), TOKEN_SPEC],
        out_specs=[_row_tile_spec(s.shape[1]) for s in out_shape],
        compiler_params=_params(("arbitrary",)),
    )(x, g1, w_in_g, after)


LANE_GROUPS = [slice(g, g + 128) for g in range(0, CONV_DIM, 128)]
NORM_ROWS = 16


def _shifted_windows(src_ref, t0, cols, offsets):
    win = src_ref[pl.ds(t0, CONV_WIN), cols]
    for rot in range(8):
        ms = [m for m in offsets if m % 8 == rot]
        if ms:
            shifted = win if rot == 0 else pltpu.roll(win, CONV_WIN - rot, 0)
            for m in ms:
                yield m, shifted[m - rot:m - rot + CONV_TILE, :]


def _shifted_sum(src_ref, t0, cols, w_ref, offset_of_tap):
    tap_at = {offset_of_tap(j): j for j in range(CONV_WIDTH)}
    acc = None
    for m, rows in _shifted_windows(src_ref, t0, cols, sorted(tap_at)):
        t = w_ref[tap_at[m]:tap_at[m] + 1, cols] * rows
        acc = t if acc is None else acc + t
    return acc


def _fetch(srcs, dsts, sems):
    copies = [pltpu.make_async_copy(s, d, sems.at[i]) for i, (s, d) in enumerate(zip(srcs, dsts))]
    for cp in copies:
        cp.start()
    return copies


def _row_chunks(src, dst, sems):
    def chunk(i):
        t0 = i * GLU_ROWS
        rows = pl.ds(t0 if isinstance(i, int) else pl.multiple_of(t0, GLU_ROWS), GLU_ROWS)
        return pltpu.make_async_copy(src.at[rows, :], dst.at[rows, :], sems.at[i])

    for i in range(SEQ // GLU_ROWS):
        chunk(i).start()
    return chunk


def _glu_into(ci_chunk, ci_ref, upad_ref):
    upad_ref[0:32, :] = jnp.zeros((32, CONV_DIM), F32)

    def step(i, c):
        ci_chunk(i).wait()
        t0 = pl.multiple_of(i * GLU_ROWS, GLU_ROWS)
        a = ci_ref[pl.ds(t0, GLU_ROWS), 0:CONV_DIM]
        b = ci_ref[pl.ds(t0, GLU_ROWS), CONV_DIM:2 * CONV_DIM]
        upad_ref[pl.ds(t0 + 32, GLU_ROWS), :] = a * _sigmoid(b)
        return c

    lax.fori_loop(0, SEQ // GLU_ROWS, step, 0)


def _layernorm_parts(u1):
    mu = jnp.mean(u1, axis=-1, keepdims=True)
    xc = u1 - mu
    rstd = lax.rsqrt(jnp.mean(xc * xc, axis=-1, keepdims=True) + EPS)
    return xc * rstd, rstd


def conv_fwd(ci, w_dw, b_dw, ln_g, ln_b):
    def body(ci_hbm, w_ref, b_ref, g_ref, bb_ref, u1_ref, u3_ref, upad_ref, ci_ref, sems):
        _glu_into(_row_chunks(ci_hbm, ci_ref, sems), ci_ref, upad_ref)

        def step(i, c):
            t0 = pl.multiple_of(i * CONV_TILE, CONV_TILE)
            for cols in LANE_GROUPS:
                u1_ref[pl.ds(t0, CONV_TILE), cols] = (_shifted_sum(upad_ref, t0, cols, w_ref, lambda j: j + 2)
                                                      + b_ref[:, cols])
            for r in range(0, CONV_TILE, NORM_ROWS):
                rows = pl.ds(t0 + r, NORM_ROWS)
                xh, _ = _layernorm_parts(u1_ref[rows, :])
                u2 = xh * g_ref[...] + bb_ref[...]
                u3_ref[rows, :] = (u2 * _sigmoid(u2)).astype(MM)
            return c

        lax.fori_loop(0, SEQ // CONV_TILE, step, 0)

    return pl.pallas_call(
        body, name="conv_fwd",
        out_shape=[jax.ShapeDtypeStruct((SEQ, CONV_DIM), F32), jax.ShapeDtypeStruct((SEQ, CONV_DIM), MM)],
        in_specs=[ANY] + [VMEM_SPEC] * 4, out_specs=[VMEM_SPEC] * 2,
        scratch_shapes=[pltpu.VMEM((SEQ + 32, CONV_DIM), F32), pltpu.VMEM(ci.shape, ci.dtype),
                        pltpu.SemaphoreType.DMA((SEQ // GLU_ROWS,))],
        compiler_params=_params(),
    )(ci, w_dw, b_dw, ln_g, ln_b)


def _softplus(z):
    return jnp.maximum(z, 0.0) + jnp.log(1.0 + jnp.exp(-jnp.abs(z)))


def _cumsum_weights(suffix, with_total):
    n = 256 if with_total else 128
    r = lax.broadcasted_iota(jnp.int32, (128, n), 0)
    c = lax.broadcasted_iota(jnp.int32, (128, n), 1)
    tri = (r >= c) if suffix else (r <= c)
    return jnp.logical_or(tri, c >= 128).astype(MM)


NO_SCORE = -1e30
N_KB = SEQ // TQ


def _score_bias(lane, row, i, j):
    keep = jnp.logical_and(i >= 0, jnp.logical_or(j < i, lane < row))
    return jnp.where(keep, 0.0, NO_SCORE)


def _block_pipeline(n_stages, descending, step, on_query_block=None):
    n_lag = n_stages - 1
    none = jnp.int32(-1)

    def shift(cur, lag):
        step([cur] + [(lag[2 * s], lag[2 * s + 1]) for s in range(n_lag)])
        return (cur[0], cur[1]) + tuple(lag[:-2])

    def outer(i, lag):
        if on_query_block is not None:
            on_query_block(i)

        def inner(n, lag):
            return shift((i, i - n if descending else n), lag)
        return lax.fori_loop(0, i + 1, inner, lag)

    lag = lax.fori_loop(0, N_KB, outer, (none,) * (2 * n_lag))
    lax.fori_loop(0, n_lag, lambda n, lag: shift((none, none), lag), lag)


def _head_masks():
    lane = lax.broadcasted_iota(jnp.int32, (TQ, 128), 1)
    row = lax.broadcasted_iota(jnp.int32, (TQ, 128), 0)
    return lane, row, lane < 64


def _pick_head(x, head0, h):
    zero = jnp.zeros_like(x)
    return jnp.where(head0, x, zero) if h == 0 else jnp.where(head0, zero, x)


N_PAIRS = ATT_DIM // 128


def _split_heads(src_ref, dst_ref):
    _, _, head0 = _head_masks()

    def block(b, c):
        r0 = pl.multiple_of(b * TQ, TQ)
        d0 = pl.multiple_of(b * 2 * TQ, 2 * TQ)
        for p in range(N_PAIRS):
            x = src_ref[pl.ds(r0, TQ), 128 * p:128 * (p + 1)]
            for h in range(2):
                dst_ref[p, pl.ds(d0 + TQ * h, TQ), :] = _pick_head(x, head0, h)
        return c

    lax.fori_loop(0, N_KB, block, 0)


def attn_fwd(q, k, v):
    def body(q_hbm, k_hbm, v_hbm, o_ref, rc_ref, acc_ref, r_ref, z_ref, spb_ref, ab_ref, qm_ref, vm_ref,
             q_ref, k_ref, v_ref, sems):
        arrive = _fetch((q_hbm, v_hbm, k_hbm), (q_ref, v_ref, k_ref), sems)
        lane, row, _ = _head_masks()
        w = _cumsum_weights(suffix=True, with_total=True)
        acc_ref[...] = jnp.zeros_like(acc_ref)
        r_ref[...] = jnp.zeros_like(r_ref)
        rc_ref[...] = jnp.zeros_like(rc_ref)
        z_ref[...] = jnp.full(z_ref.shape, NO_SCORE, F32)
        spb_ref[...] = jnp.zeros_like(spb_ref)
        ab_ref[...] = jnp.zeros_like(ab_ref)
        arrive[0].wait()
        _split_heads(q_ref, qm_ref)
        arrive[1].wait()
        _split_heads(v_ref, vm_ref)
        arrive[2].wait()

        def step(pairs):
            (i1, j1), (i2, j2), (i3, j3) = pairs
            k1, q2, q3 = (pl.multiple_of(jnp.maximum(b, 0) * TQ, TQ) for b in (j1, i2, i3))
            q1, k3 = (pl.multiple_of(jnp.maximum(b, 0) * 2 * TQ, 2 * TQ) for b in (i1, j3))
            bias1 = _score_bias(lane, row, i1, j1)
            first2 = j2 == i2
            rc_rows = rc_ref[pl.ds(q2, TQ), :]
            for p in range(N_PAIRS):
                cols = slice(128 * p, 128 * (p + 1))
                kb = k_ref[pl.ds(k1, TQ), cols]
                acc_ref[pl.ds(q3, TQ), cols] += _dot(ab_ref[p], vm_ref[p, pl.ds(k3, 2 * TQ), :])
                for h in range(2):
                    hh = 2 * p + h
                    r = _dot(spb_ref[hh], w)
                    r_in = jnp.where(first2, 0.0, r_ref[hh])
                    ab_ref[p, :, 128 * h:128 * (h + 1)] = jnp.exp(z_ref[hh] - (r[:, :128] + r_in)).astype(MM)
                    rc_rows = jnp.where(jnp.logical_and(lane == 16 * hh + j2, i2 >= 0), r_in, rc_rows)
                    r_ref[hh] = r_in + r[:, 128:]
                    z = _dot_nt(qm_ref[p, pl.ds(q1 + TQ * h, TQ), :], kb) + bias1
                    z_ref[hh] = z
                    spb_ref[hh] = _softplus(z).astype(MM)
            rc_ref[pl.ds(q2, TQ), :] = rc_rows

        _block_pipeline(3, True, step)
        o_ref[...] = acc_ref[...].astype(MM)

    return pl.pallas_call(
        body, name="attn_fwd",
        out_shape=[jax.ShapeDtypeStruct((SEQ, ATT_DIM), MM), jax.ShapeDtypeStruct((SEQ, 128), F32)],
        in_specs=[ANY] * 3, out_specs=[VMEM_SPEC] * 2,
        scratch_shapes=[pltpu.VMEM((SEQ, ATT_DIM), F32), pltpu.VMEM((8, TQ, 128), F32),
                        pltpu.VMEM((8, TQ, 128), F32), pltpu.VMEM((8, TQ, 128), MM),
                        pltpu.VMEM((N_PAIRS, TQ, 256), MM), pltpu.VMEM((N_PAIRS, 2 * SEQ, 128), MM),
                        pltpu.VMEM((N_PAIRS, 2 * SEQ, 128), MM)]
                       + [pltpu.VMEM(a.shape, a.dtype) for a in (q, k, v)] + [pltpu.SemaphoreType.DMA((3,))],
        compiler_params=_params(),
    )(q, k, v)


def _branch_outputs(u_ref, a_ref, wcb_ref, bcb_ref, wab_ref):
    u = u_ref[...]
    a = a_ref[...]
    co = jnp.concatenate([_dot(u, wcb_ref[j]) for j in range(N_CHIPS)], axis=1) + bcb_ref[...]
    ao = jnp.concatenate([_dot(a, wab_ref[j]) for j in range(N_CHIPS)], axis=1)
    return co, ao


def mix_fwd(u3, att, gc, ga, x, w_cb_g, b_cb, w_ab_g, w_out_g, g2, g3, after):
    def body(u_ref, a_ref, gc_ref, ga_ref, x_ref, wcb_ref, bcb_ref, wab_ref, wout_ref, g2_ref, g3_ref, after_ref,
             mg_ref, mix_ref, x2_ref, h2_ref):
        co, ao = _branch_outputs(u_ref, a_ref, wcb_ref, bcb_ref, wab_ref)
        merged = (_sigmoid(gc_ref[...]) * co + _sigmoid(ga_ref[...]) * ao).astype(MM)
        mg_ref[...] = merged
        mix = _dot(merged, wout_ref[...])
        mix_ref[...] = mix
        n2, _ = _rms(mix)
        x2 = x_ref[...] + n2 * g2_ref[...]
        x2_ref[...] = x2
        n3, _ = _rms(x2)
        h2_ref[...] = (n3 * g3_ref[...]).astype(MM)

    out_shape = [
        jax.ShapeDtypeStruct((SEQ, D_MODEL), MM), jax.ShapeDtypeStruct((SEQ, D_MODEL), F32),
        jax.ShapeDtypeStruct((SEQ, D_MODEL), F32), jax.ShapeDtypeStruct((SEQ, D_MODEL), MM),
    ]
    vec = _full_spec((1, D_MODEL))
    return pl.pallas_call(
        body, name="mix_fwd", grid=(SEQ // TM,), out_shape=out_shape,
        in_specs=[_row_tile_spec(CONV_DIM), _row_tile_spec(ATT_DIM), _row_tile_spec(D_MODEL),
                  _row_tile_spec(D_MODEL), _row_tile_spec(D_MODEL), _weight_spec(w_cb_g.shape), vec,
                  _weight_spec(w_ab_g.shape), _weight_spec(w_out_g.shape), vec, vec, TOKEN_SPEC],
        out_specs=[_row_tile_spec(D_MODEL)] * 4,
        compiler_params=_params(("arbitrary",)),
    )(u3, att, gc, ga, x, w_cb_g, b_cb, w_ab_g, w_out_g, g2, g3, after)


def ffn_up_fwd(h2, w_up_g):
    def body(h_ref, wg_ref, wu_ref, gate_ref, up_ref, act_ref):
        h = h_ref[...]
        gate = _dot(h, wg_ref[0])
        up = _dot(h, wu_ref[0])
        gate_ref[...] = gate.astype(MM)
        up_ref[...] = up.astype(MM)
        act_ref[...] = (gate * _sigmoid(gate) * up).astype(MM)

    tile = pl.BlockSpec((TM, UP_SHARD), lambda n, i: (i, n))
    act = jax.ShapeDtypeStruct((SEQ, D_FF), MM)
    return pl.pallas_call(
        body, name="ffn_up_fwd", grid=(2, SEQ // TM), out_shape=[act, act, act],
        in_specs=[pl.BlockSpec((TM, D_MODEL), lambda n, i: (i, 0)),
                  pl.BlockSpec((1, D_MODEL, UP_SHARD), lambda n, i: (n, 0, 0)),
                  pl.BlockSpec((1, D_MODEL, UP_SHARD), lambda n, i: (n + 2, 0, 0))],
        out_specs=[tile, tile, tile],
        compiler_params=_params(("arbitrary", "arbitrary")),
    )(h2, w_up_g, w_up_g)


def ffn_down_loss(act, w_down_g, x2, target, g4):
    def body(act_ref, wd_ref, x2_ref, t_ref, g_ref, dff_ref, dy_ref, loss_ref, dg_ref):
        ff = _dot(act_ref[...], wd_ref[...])
        n4, r4 = _rms(ff)
        g4v = g_ref[...]
        err = x2_ref[...] + n4 * g4v - t_ref[...]
        row_loss = jnp.mean(err * err, axis=-1, keepdims=True)
        loss_ref[...] = jnp.zeros((8, 128), F32) + 0.5 * jnp.sum(row_loss, axis=0, keepdims=True)
        dy = err * (1.0 / D_MODEL)
        dy_ref[...] = dy
        dff_ref[...] = _rms_bwd(dy * g4v, n4, r4).astype(MM)
        _acc_rows(dg_ref, jnp.sum(dy * n4, axis=0, keepdims=True), pl.program_id(0) == 0)

    nt = SEQ // TM
    vec = _full_spec((1, D_MODEL))
    return pl.pallas_call(
        body, name="ffn_down_loss", grid=(nt,),
        out_shape=(jax.ShapeDtypeStruct((SEQ, D_MODEL), MM), jax.ShapeDtypeStruct((SEQ, D_MODEL), F32),
                   jax.ShapeDtypeStruct((nt * 8, 128), F32), jax.ShapeDtypeStruct((1, D_MODEL), F32)),
        in_specs=[_row_tile_spec(D_FF), _weight_spec(w_down_g.shape), _row_tile_spec(D_MODEL),
                  _row_tile_spec(D_MODEL), vec],
        out_specs=[_row_tile_spec(D_MODEL), _row_tile_spec(D_MODEL),
                   pl.BlockSpec((8, 128), lambda i: (i, 0)), vec],
        compiler_params=_params(("arbitrary",)),
    )(act, w_down_g, x2, target, g4)


def ffn_act_bwd(dff, w_down_g, gate, up):
    def body(dff_ref, wd_ref, gate_ref, up_ref, dgu_ref):
        dact = _dot_nt(dff_ref[...], wd_ref[...])
        gate = gate_ref[...].astype(F32)
        sg = _sigmoid(gate)
        dgu_ref[:, 0:D_FF] = (dact * up_ref[...].astype(F32) * (sg * (1.0 + gate * (1.0 - sg)))).astype(MM)
        dgu_ref[:, D_FF:2 * D_FF] = (dact * (gate * sg)).astype(MM)

    return pl.pallas_call(
        body, name="ffn_act_bwd", grid=(SEQ // TM,),
        out_shape=jax.ShapeDtypeStruct((SEQ, 2 * D_FF), MM),
        in_specs=[_row_tile_spec(D_MODEL), _weight_spec(w_down_g.shape), _row_tile_spec(D_FF), _row_tile_spec(D_FF)],
        out_specs=_row_tile_spec(2 * D_FF),
        compiler_params=_params(("arbitrary",)),
    )(dff, w_down_g, gate, up)


def ffn_in_bwd(dgu, w_up_g, x2, mix, dy, g3, g2):
    def body(dgu_ref, w_ref, x2_ref, mix_ref, dy_ref, g3_ref, g2_ref, dx2_ref, dmix_ref, dg3_ref, dg2_ref):
        dh2 = None
        for j in range(N_CHIPS):
            t = _dot_nt(dgu_ref[:, j * UP_SHARD:(j + 1) * UP_SHARD], w_ref[j])
            dh2 = t if dh2 is None else dh2 + t
        first = pl.program_id(0) == 0
        n3, r3 = _rms(x2_ref[...])
        dx2 = dy_ref[...] + _rms_bwd(dh2 * g3_ref[...], n3, r3)
        dx2_ref[...] = dx2
        _acc_rows(dg3_ref, jnp.sum(dh2 * n3, axis=0, keepdims=True), first)
        n2, r2 = _rms(mix_ref[...])
        dmix_ref[...] = _rms_bwd(dx2 * g2_ref[...], n2, r2).astype(MM)
        _acc_rows(dg2_ref, jnp.sum(dx2 * n2, axis=0, keepdims=True), first)

    vec = _full_spec((1, D_MODEL))
    return pl.pallas_call(
        body, name="ffn_in_bwd", grid=(SEQ // TM,),
        out_shape=(jax.ShapeDtypeStruct((SEQ, D_MODEL), F32), jax.ShapeDtypeStruct((SEQ, D_MODEL), MM),
                   jax.ShapeDtypeStruct((1, D_MODEL), F32), jax.ShapeDtypeStruct((1, D_MODEL), F32)),
        in_specs=[_row_tile_spec(2 * D_FF), _weight_spec(w_up_g.shape), _row_tile_spec(D_MODEL),
                  _row_tile_spec(D_MODEL), _row_tile_spec(D_MODEL), vec, vec],
        out_specs=[_row_tile_spec(D_MODEL), _row_tile_spec(D_MODEL), vec, vec],
        compiler_params=_params(("arbitrary",)),
    )(dgu, w_up_g, x2, mix, dy, g3, g2)


def merge_bwd(dmix, w_out_g, gc, ga, u3, att, w_cb_g, b_cb, w_ab_g, after):
    def body(dmix_ref, wout_ref, gc_ref, ga_ref, u_ref, a_ref, wcb_ref, bcb_ref, wab_ref, after_ref,
             dco_ref, dao_ref, dg_ref, du3_ref, datt_ref, dbcb_ref):
        dm = _dot_nt(dmix_ref[...], wout_ref[...])
        co, ao = _branch_outputs(u_ref, a_ref, wcb_ref, bcb_ref, wab_ref)
        sgc = _sigmoid(gc_ref[...])
        sga = _sigmoid(ga_ref[...])
        dco = dm * sgc
        dao = dm * sga
        dg_ref[:, 0:D_MODEL] = (dm * co * (sgc * (1.0 - sgc))).astype(MM)
        dg_ref[:, D_MODEL:2 * D_MODEL] = (dm * ao * (sga * (1.0 - sga))).astype(MM)
        _acc_rows(dbcb_ref, jnp.sum(dco, axis=0, keepdims=True), pl.program_id(0) == 0)
        dco_ref[...] = dco.astype(MM)
        dao_ref[...] = dao.astype(MM)
        du3 = None
        datt = None
        for j in range(N_CHIPS):
            cols = slice(j * BR_SHARD, (j + 1) * BR_SHARD)
            t = _dot_nt(dco_ref[:, cols], wcb_ref[j])
            s = _dot_nt(dao_ref[:, cols], wab_ref[j])
            du3 = t if du3 is None else du3 + t
            datt = s if datt is None else datt + s
        du3_ref[...] = du3
        datt_ref[...] = datt.astype(MM)

    wide = _row_tile_spec(D_MODEL)
    return pl.pallas_call(
        body, name="merge_bwd", grid=(SEQ // TM,),
        out_shape=(jax.ShapeDtypeStruct((SEQ, D_MODEL), MM), jax.ShapeDtypeStruct((SEQ, D_MODEL), MM),
                   jax.ShapeDtypeStruct((SEQ, 2 * D_MODEL), MM),
                   jax.ShapeDtypeStruct((SEQ, CONV_DIM), F32), jax.ShapeDtypeStruct((SEQ, ATT_DIM), MM),
                   jax.ShapeDtypeStruct((1, D_MODEL), F32)),
        in_specs=[wide, _weight_spec(w_out_g.shape), wide, wide, _row_tile_spec(CONV_DIM), _row_tile_spec(ATT_DIM),
                  _weight_spec(w_cb_g.shape), _full_spec((1, D_MODEL)), _weight_spec(w_ab_g.shape), TOKEN_SPEC],
        out_specs=[wide, wide, _row_tile_spec(2 * D_MODEL), _row_tile_spec(CONV_DIM), _row_tile_spec(ATT_DIM),
                   _full_spec((1, D_MODEL))],
        compiler_params=_params(("arbitrary",)),
    )(dmix, w_out_g, gc, ga, u3, att, w_cb_g, b_cb, w_ab_g, after)


def conv_bwd(du3, u1, ci, w_dw, ln_g, ln_b, after):
    def body(du3_hbm, u1_hbm, ci_hbm, w_ref, g_ref, bb_ref, after_ref,
             dci_ref, dw_ref, dbdw_ref, dg_ref, db_ref, upad_ref, dpad_ref, dwacc_ref, vacc_ref,
             du3_ref, u1_ref, ci_ref, ci_sems, u1_sems, du3_sems):
        ci_chunk = _row_chunks(ci_hbm, ci_ref, ci_sems)
        u1_chunk = _row_chunks(u1_hbm, u1_ref, u1_sems)
        du3_chunk = _row_chunks(du3_hbm, du3_ref, du3_sems)
        _glu_into(ci_chunk, ci_ref, upad_ref)
        dpad_ref[SEQ:SEQ + 32, :] = jnp.zeros((32, CONV_DIM), F32)
        dwacc_ref[...] = jnp.zeros_like(dwacc_ref)
        vacc_ref[...] = jnp.zeros_like(vacc_ref)

        def fold8(t):
            s = t[0:8, :]
            for r in range(8, t.shape[0], 8):
                s = s + t[r:r + 8, :]
            return s

        def pass1(i, c):
            t0 = pl.multiple_of(i * CONV_TILE, CONV_TILE)
            gv = g_ref[...]
            for r in range(0, CONV_TILE, NORM_ROWS):
                rows = pl.ds(t0 + r, NORM_ROWS)
                xh, rstd = _layernorm_parts(u1_ref[rows, :])
                u2 = xh * gv + bb_ref[...]
                s2 = _sigmoid(u2)
                du2 = du3_ref[rows, :] * (s2 * (1.0 + u2 * (1.0 - s2)))
                wv = du2 * gv
                du1 = rstd * (wv - jnp.mean(wv, axis=-1, keepdims=True)
                              - xh * jnp.mean(wv * xh, axis=-1, keepdims=True))
                dpad_ref[rows, :] = du1
                vacc_ref[0] += fold8(du2 * xh)
                vacc_ref[1] += fold8(du2)
                vacc_ref[2] += fold8(du1)
            for cols in LANE_GROUPS:
                du1 = dpad_ref[pl.ds(t0, CONV_TILE), cols]
                for m, rows in _shifted_windows(upad_ref, t0, cols, range(2, CONV_WIDTH + 2)):
                    dwacc_ref[m - 2, :, cols] += fold8(du1 * rows)
            return c

        tiles_per_chunk = GLU_ROWS // CONV_TILE

        def pass1_chunk(ch, c):
            u1_chunk(ch).wait()
            du3_chunk(ch).wait()
            return lax.fori_loop(ch * tiles_per_chunk, (ch + 1) * tiles_per_chunk, pass1, c)

        lax.fori_loop(0, SEQ // GLU_ROWS, pass1_chunk, 0)

        def pass2(i, c):
            t0 = pl.multiple_of(i * CONV_TILE, CONV_TILE)
            tile = pl.ds(t0, CONV_TILE)
            for cols in LANE_GROUPS:
                gate_cols = slice(cols.start + CONV_DIM, cols.stop + CONV_DIM)
                du0 = _shifted_sum(dpad_ref, t0, cols, w_ref, lambda j: 30 - j)
                a = ci_ref[tile, cols]
                sb = _sigmoid(ci_ref[tile, gate_cols])
                dci_ref[tile, cols] = (du0 * sb).astype(MM)
                dci_ref[tile, gate_cols] = (du0 * a * (sb * (1.0 - sb))).astype(MM)
            return c

        lax.fori_loop(0, SEQ // CONV_TILE, pass2, 0)

        for j in range(CONV_WIDTH):
            dw_ref[j:j + 1, :] = jnp.sum(dwacc_ref[j], axis=0, keepdims=True)
        dw_ref[CONV_WIDTH:32, :] = jnp.zeros((32 - CONV_WIDTH, CONV_DIM), F32)
        dg_ref[...] = jnp.sum(vacc_ref[0], axis=0, keepdims=True)
        db_ref[...] = jnp.sum(vacc_ref[1], axis=0, keepdims=True)
        dbdw_ref[...] = jnp.sum(vacc_ref[2], axis=0, keepdims=True)

    vec = jax.ShapeDtypeStruct((1, CONV_DIM), F32)
    return pl.pallas_call(
        body, name="conv_bwd",
        out_shape=(jax.ShapeDtypeStruct((SEQ, 2 * CONV_DIM), MM), jax.ShapeDtypeStruct((32, CONV_DIM), F32),
                   vec, vec, vec),
        in_specs=[ANY] * 3 + [VMEM_SPEC] * 4, out_specs=[VMEM_SPEC] * 5,
        scratch_shapes=[pltpu.VMEM((SEQ + 32, CONV_DIM), F32), pltpu.VMEM((SEQ + 32, CONV_DIM), F32),
                        pltpu.VMEM((CONV_WIDTH, 8, CONV_DIM), F32), pltpu.VMEM((3, 8, CONV_DIM), F32)]
                       + [pltpu.VMEM(a.shape, a.dtype) for a in (du3, u1, ci)]
                       + [pltpu.SemaphoreType.DMA((SEQ // GLU_ROWS,))] * 3,
        compiler_params=_params(),
    )(du3, u1, ci, w_dw, ln_g, ln_b, after)


def attn_bwd(q, k, v, datt, rc, after):
    def body(q_hbm, k_hbm, v_hbm, do_hbm, rc_hbm, after_ref, dqkv_ref, dqa_ref, dka_ref, dva_ref, pc_ref, z_ref,
             sig1_ref, sig2_ref, g_ref, spb_ref, gb_ref, ar_ref, dzr_ref, dzc_ref, qm_ref, km_ref, dom_ref,
             q_ref, k_ref, v_ref, do_ref, rc_ref, sems):
        arrive = _fetch((q_hbm, k_hbm, do_hbm, v_hbm, rc_hbm), (q_ref, k_ref, do_ref, v_ref, rc_ref), sems)
        lane, row, _ = _head_masks()
        for ref in (dqa_ref, dka_ref, dva_ref, pc_ref):
            ref[...] = jnp.zeros_like(ref)
        z_ref[...] = jnp.full(z_ref.shape, NO_SCORE, F32)
        for ref in (sig1_ref, sig2_ref, spb_ref, ar_ref, g_ref, gb_ref, dzr_ref, dzc_ref):
            ref[...] = jnp.zeros_like(ref)
        for cp, (src_ref, split_ref) in zip(arrive, ((q_ref, qm_ref), (k_ref, km_ref), (do_ref, dom_ref))):
            cp.wait()
            _split_heads(src_ref, split_ref)
        arrive[3].wait()
        arrive[4].wait()
        w_suffix = _cumsum_weights(suffix=True, with_total=False)
        w_prefix = _cumsum_weights(suffix=False, with_total=True)

        def step(pairs):
            (ia, ja), (ib, jb), (ic, jc), (id_, jd) = pairs
            ka, qb_, kb_, kc, qd, kd = (pl.multiple_of(jnp.maximum(b, 0) * TQ, TQ) for b in (ja, ib, jb, jc, id_, jd))
            qa2, qb2, qc2, qd2, kd2 = (pl.multiple_of(jnp.maximum(b, 0) * 2 * TQ, 2 * TQ)
                                       for b in (ia, ib, ic, id_, jd))
            bias_a = _score_bias(lane, row, ia, ja)
            rc_rows = rc_ref[pl.ds(qb_, TQ), :]
            first_c = jc == 0
            for p in range(N_PAIRS):
                cols = slice(128 * p, 128 * (p + 1))
                k_a = k_ref[pl.ds(ka, TQ), cols]
                v_b = v_ref[pl.ds(kb_, TQ), cols]
                dqa_ref[pl.ds(qd, TQ), cols] += _dot(dzc_ref[p], km_ref[p, pl.ds(kd2, 2 * TQ), :])
                dka_ref[pl.ds(kd, TQ), cols] += _dot_tn(dzr_ref[p], qm_ref[p, pl.ds(qd2, 2 * TQ), :])
                dva_ref[pl.ds(kc, TQ), cols] += _dot_tn(ar_ref[p], dom_ref[p, pl.ds(qc2, 2 * TQ), :])
                for h in range(2):
                    hh = 2 * p + h
                    rows = slice(TQ * h, TQ * (h + 1))
                    r = _dot(gb_ref[hh], w_prefix)
                    p_in = jnp.where(first_c, 0.0, pc_ref[hh])
                    dz = (g_ref[hh] - sig2_ref[hh] * (r[:, :128] + p_in)).astype(MM)
                    dzc_ref[p, :, rows] = dz
                    dzr_ref[p, rows, :] = dz
                    pc_ref[hh] = p_in + r[:, 128:]
                    r_in = jnp.sum(jnp.where(lane == 16 * hh + jb, rc_rows, 0.0), axis=1, keepdims=True)
                    a = jnp.exp(z_ref[hh] - (_dot(spb_ref[hh], w_suffix) + r_in))
                    g = _dot_nt(dom_ref[p, pl.ds(qb2 + TQ * h, TQ), :], v_b) * a
                    ar_ref[p, rows, :] = a.astype(MM)
                    g_ref[hh] = g
                    gb_ref[hh] = g.astype(MM)
                    sig2_ref[hh] = sig1_ref[hh]
                    z = _dot_nt(qm_ref[p, pl.ds(qa2 + TQ * h, TQ), :], k_a) + bias_a
                    sp = _softplus(z)
                    sig1_ref[hh] = jnp.exp(z - sp)
                    z_ref[hh] = z
                    spb_ref[hh] = sp.astype(MM)

        _block_pipeline(4, False, step)
        dqkv_ref[:, 0:ATT_DIM] = (dqa_ref[...] * ATT_SCALE).astype(MM)
        dqkv_ref[:, ATT_DIM:2 * ATT_DIM] = dka_ref[...].astype(MM)
        dqkv_ref[:, 2 * ATT_DIM:3 * ATT_DIM] = dva_ref[...].astype(MM)

    split = pltpu.VMEM((N_PAIRS, 2 * SEQ, 128), MM)
    return pl.pallas_call(
        body, name="attn_bwd", out_shape=jax.ShapeDtypeStruct((SEQ, 3 * ATT_DIM), MM),
        in_specs=[ANY] * 5 + [VMEM_SPEC], out_specs=VMEM_SPEC,
        scratch_shapes=[pltpu.VMEM((SEQ, ATT_DIM), F32)] * 3 + [pltpu.VMEM((8, TQ, 128), F32)] * 5
                       + [pltpu.VMEM((8, TQ, 128), MM)] * 2
                       + [pltpu.VMEM((N_PAIRS, 2 * TQ, 128), MM)] * 2 + [pltpu.VMEM((N_PAIRS, TQ, 256), MM)]
                       + [split] * 3
                       + [pltpu.VMEM(a.shape, a.dtype) for a in (q, k, v, datt, rc)] + [pltpu.SemaphoreType.DMA((5,))],
        compiler_params=_params(),
    )(q, k, v, datt, rc, after)


DPROJ_PIECES = ((0, 1024), (1024, 2560), (2560, 4608))


def _dproj_segments(j):
    g0, g1 = j * IN_SHARD, (j + 1) * IN_SHARD
    segs = []
    for p, (s, e) in enumerate(DPROJ_PIECES):
        lo, hi = max(s, g0), min(e, g1)
        if lo < hi:
            segs.append((p, lo - s, lo - g0, hi - lo))
    return segs


def in_proj_bwd(pieces, w_in_g, x, dx2, g1, after):
    def body(p0_ref, p1_ref, p2_ref, w_ref, x_ref, dx2_ref, g_ref, after_ref, dx_ref, dg_ref):
        p_refs = (p0_ref, p1_ref, p2_ref)
        dh = None
        for j in range(N_CHIPS):
            for p, lo, off, width in _dproj_segments(j):
                t = _dot_nt(p_refs[p][:, lo:lo + width], w_ref[j, :, off:off + width])
                dh = t if dh is None else dh + t
        n1, r1 = _rms(x_ref[...])
        dx_ref[...] = dx2_ref[...] + _rms_bwd(dh * g_ref[...], n1, r1)
        _acc_rows(dg_ref, jnp.sum(dh * n1, axis=0, keepdims=True), pl.program_id(0) == 0)

    vec = _full_spec((1, D_MODEL))
    return pl.pallas_call(
        body, name="in_proj_bwd", grid=(SEQ // TM,),
        out_shape=[jax.ShapeDtypeStruct((SEQ, D_MODEL), F32), jax.ShapeDtypeStruct((1, D_MODEL), F32)],
        in_specs=[_row_tile_spec(p.shape[1]) for p in pieces]
                 + [_weight_spec(w_in_g.shape), _row_tile_spec(D_MODEL), _row_tile_spec(D_MODEL), vec, TOKEN_SPEC],
        out_specs=[_row_tile_spec(D_MODEL), vec],
        compiler_params=_params(("arbitrary",)),
    )(*pieces, w_in_g, x, dx2, g1, after)


def weight_grad_in(h1, pieces):
    kh = D_MODEL // 2

    def body(a_ref, p0_ref, p1_ref, p2_ref, o_ref):
        p_refs = (p0_ref, p1_ref, p2_ref)
        a = a_ref[...]
        for j in range(N_CHIPS):
            @pl.when(pl.program_id(1) == j)
            def _():
                for p, lo, off, width in _dproj_segments(j):
                    o_ref[0, 0, :, off:off + width] = _dot_tn(a, p_refs[p][:, lo:lo + width]).astype(MM)

    return pl.pallas_call(
        body, name="dw_in", grid=(2, N_CHIPS), out_shape=jax.ShapeDtypeStruct((N_CHIPS, 2, kh, IN_SHARD), MM),
        in_specs=[pl.BlockSpec((SEQ, kh), lambda h, j: (0, h))] + [_weight_spec(p.shape) for p in pieces],
        out_specs=pl.BlockSpec((1, 1, kh, IN_SHARD), lambda h, j: (j, h, 0, 0)),
        compiler_params=_params(("arbitrary", "arbitrary")),
    )(h1, *pieces)


def weight_grad(a, b, name, col_sharded, tk=None):
    kin, n = a.shape[1], b.shape[1]

    def body(a_ref, b_ref, o_ref):
        if col_sharded:
            o_ref[0, 0] = _dot_tn(a_ref[...], b_ref[...]).astype(MM)
        else:
            o_ref[...] = _dot_tn(a_ref[...], b_ref[...]).astype(MM)

    if col_sharded:
        kh, ns = kin // 2, n // N_CHIPS
        out = jax.ShapeDtypeStruct((N_CHIPS, 2, kh, ns), MM)
        grid = (2, N_CHIPS)
        in_specs = [pl.BlockSpec((SEQ, kh), lambda h, j: (0, h)), pl.BlockSpec((SEQ, ns), lambda h, j: (0, j))]
        out_spec = pl.BlockSpec((1, 1, kh, ns), lambda h, j: (j, h, 0, 0))
        sem = ("arbitrary", "arbitrary")
    else:
        out = jax.ShapeDtypeStruct((kin, n), MM)
        grid = (kin // tk,)
        in_specs = [pl.BlockSpec((SEQ, tk), lambda r: (0, r)), pl.BlockSpec((SEQ, n), lambda r: (0, 0))]
        out_spec = pl.BlockSpec((tk, n), lambda r: (r, 0))
        sem = ("arbitrary",)
    res = pl.pallas_call(
        body, name=name, grid=grid, out_shape=out, in_specs=in_specs, out_specs=out_spec,
        compiler_params=_params(sem),
    )(a, b)
    if not col_sharded:
        res = res.reshape(N_CHIPS, 2, kin // (2 * N_CHIPS), n)
    return res


def weight_grad_mix(merged, dmix, u3, dco, att, dao):
    operands = (merged, dmix, u3, dco, att, dao)
    n_out, n_br = D_MODEL // 2, CONV_DIM // 2

    def body(*refs):
        hbm, (o_out, o_cb, o_ab), bufs, sems = refs[:6], refs[6:9], refs[9:15], refs[15]
        copies = [pltpu.make_async_copy(hbm[i], bufs[i], sems.at[i]) for i in range(6)]
        for cp in copies:
            cp.start()
        m_ref, dm_ref, u_ref, dco_ref, a_ref, dao_ref = bufs
        copies[0].wait()
        copies[1].wait()
        for h in range(2):
            o_out[h * n_out:(h + 1) * n_out, :] = _dot_tn(m_ref[:, h * n_out:(h + 1) * n_out], dm_ref[...]).astype(MM)
        for br, (a, d, o) in enumerate(((u_ref, dco_ref, o_cb), (a_ref, dao_ref, o_ab))):
            copies[2 + 2 * br].wait()
            copies[3 + 2 * br].wait()
            for h in range(2):
                g = _dot_tn(a[:, h * n_br:(h + 1) * n_br], d[...])
                for j in range(N_CHIPS):
                    o[j, h] = g[:, j * BR_SHARD:(j + 1) * BR_SHARD].astype(MM)

    branch = jax.ShapeDtypeStruct((N_CHIPS, 2, n_br, BR_SHARD), MM)
    dw_out, dw_cb, dw_ab = pl.pallas_call(
        body, name="dw_mix", out_shape=[jax.ShapeDtypeStruct((D_MODEL, D_MODEL), MM), branch, branch],
        in_specs=[ANY] * 6, out_specs=[VMEM_SPEC] * 3,
        scratch_shapes=[pltpu.VMEM(a.shape, a.dtype) for a in operands] + [pltpu.SemaphoreType.DMA((6,))],
        compiler_params=_params(),
    )(*operands)
    return dw_out.reshape(N_CHIPS, 2, D_MODEL // (2 * N_CHIPS), D_MODEL), dw_cb, dw_ab


def _place():
    x, y, c = lax.axis_index("x"), lax.axis_index("y"), lax.axis_index("c")
    chips = [(1 - x, y), (x, 1 - y), (1 - x, 1 - y)]
    return x, y, c, chips


def _rcopy(src, dst, send_sem, recv_sem, dev):
    return pltpu.make_async_remote_copy(src_ref=src, dst_ref=dst, send_sem=send_sem, recv_sem=recv_sem,
                                        device_id=dev, device_id_type=MESH)


class _Gather:
    N_MOVES = 6

    def __init__(self, shapes, w, o, scratch):
        self.n, self.shapes, self.w, self.o = len(w), shapes, w, o
        self.send, self.recv, self.psend, self.precv, self.loc_in, self.loc_out = scratch[:6]
        self.raw, self.stage = scratch[6:6 + self.n], scratch[6 + self.n:]
        x, y, c, self.chips = _place()
        self.c = c
        self.me, k_x, k_y, k_far = 2 * x + y, 2 * (1 - x) + y, 2 * x + (1 - y), 2 * (1 - x) + (1 - y)
        to_x, to_y = (1 - x, y, c), (x, 1 - y, c)
        self.sib = (x, y, 1 - c)
        self.sent_as = [(self.me, 0, to_x), (self.me, 1, to_y), (self.me, 1, to_x), (self.me, 0, to_y),
                        (k_x, 0, to_y), (k_y, 1, to_x)]
        self.arrives_as = [(k_x, 0, to_x), (k_y, 1, to_y), (k_x, 1, to_x), (k_y, 0, to_y),
                           (k_far, 0, to_y), (k_far, 1, to_x)]
        self.sent_on_after = {0: 4, 1: 5}

    @staticmethod
    def scratch(shards):
        n = len(shards)
        sems = pltpu.SemaphoreType.DMA
        m = _Gather.N_MOVES * n
        return ([sems((m,)), sems((m,)), sems((m,)), sems((m,)), sems((3 * n,)), sems((n,))]
                + [pltpu.VMEM(s.shape, s.dtype) for s in shards] + [pltpu.VMEM(s.shape, MM) for s in shards])

    @staticmethod
    def out_shapes(shards):
        return [jax.ShapeDtypeStruct((N_CHIPS,) + s.shape, MM) for s in shards]

    def _rows(self, t, quarter, cc):
        rq = self.shapes[t][0] // 4
        return pl.ds(pl.multiple_of((2 * cc + quarter) * rq, rq), rq)

    def _own_rows(self, t, piece):
        if piece < 2:
            return self._rows(t, piece, self.c)
        rh = self.shapes[t][0] // 2
        return pl.ds(pl.multiple_of((1 - self.c) * rh, rh), rh)

    def _chip(self, j):
        cx, cy = self.chips[j]
        return 2 * cx + cy, (cx, cy, self.c)

    def local_in(self, t, piece):
        rows = self._own_rows(t, piece)
        return pltpu.make_async_copy(self.w[t].at[rows, :], self.raw[t].at[rows, :], self.loc_in.at[3 * t + piece])

    def local_out(self, t):
        return pltpu.make_async_copy(self.stage[t], self.o[t].at[self.me], self.loc_out.at[t])

    def sent(self, i, t):
        k, quarter, dev = self.sent_as[i]
        rows = self._rows(t, quarter, self.c)
        there = self.o[t].at[k, rows, :]
        return _rcopy(self.stage[t].at[rows, :] if i < 4 else there, there,
                      self.send.at[i * self.n + t], self.recv.at[i * self.n + t], dev)

    def arrived(self, i, t):
        k, quarter, dev = self.arrives_as[i]
        blk = self.o[t].at[k, self._rows(t, quarter, self.c), :]
        return _rcopy(blk, blk, self.send.at[i * self.n + t], self.recv.at[i * self.n + t], dev)

    def passed(self, i, t, cc):
        k, quarter, _ = self.arrives_as[i]
        blk = self.o[t].at[k, self._rows(t, quarter, cc), :]
        return _rcopy(blk, blk, self.psend.at[i * self.n + t], self.precv.at[i * self.n + t], self.sib)

    def start(self):
        for piece in range(3):
            for t in range(self.n):
                self.local_in(t, piece).start()
        for piece, moves in enumerate(((0, 3), (1, 2), ())):
            for t in range(self.n):
                rows = self._own_rows(t, piece)
                self.local_in(t, piece).wait()
                self.stage[t][rows, :] = self.raw[t][rows, :].astype(MM)
                for i in moves:
                    self.sent(i, t).start()
        for t in range(self.n):
            self.local_out(t).start()

    def forward(self):
        for i in range(self.N_MOVES):
            for t in range(self.n):
                self.arrived(i, t).wait_recv()
                if i in self.sent_on_after:
                    self.sent(self.sent_on_after[i], t).start()
                self.passed(i, t, self.c).start()

    def finish(self):
        for i in range(self.N_MOVES):
            for t in range(self.n):
                self.passed(i, t, 1 - self.c).wait_recv()
        for i in range(self.N_MOVES):
            for t in range(self.n):
                self.sent(i, t).wait_send()
                self.passed(i, t, self.c).wait_send()
        for t in range(self.n):
            self.local_out(t).wait()


def all_gather_weights(shards, small, later):
    n, m = len(shards), len(later)
    shapes = [s.shape for s in shards]

    def body(*refs):
        w = refs[:n]
        sm = refs[n]
        lw = refs[n + 1:n + 1 + m]
        o = refs[n + 1 + m:2 * n + 1 + m]
        osm = refs[2 * n + 1 + m]
        lo = refs[2 * n + 2 + m:2 * n + 2 + 2 * m]
        scratch = refs[2 * n + 2 + 2 * m:]
        ssend, srecv, sloc, lsem_in, lsem_out = scratch[:5]
        lraw, lstage = scratch[5:5 + m], scratch[5 + m:5 + 2 * m]
        g = _Gather(shapes, w, o, scratch[5 + 2 * m:])
        own = pltpu.make_async_copy(sm, osm.at[g.me], sloc)
        own.start()
        g.start()
        loads = [pltpu.make_async_copy(lw[t], lraw[t], lsem_in.at[t]) for t in range(m)]
        for cp in loads:
            cp.start()
        small_cps = [_rcopy(sm, osm.at[g.me], ssend.at[j], srecv.at[j], g._chip(j)[1]) for j in range(3)]
        for cp in small_cps:
            cp.start()
        places = []
        for t in range(m):
            loads[t].wait()
            lstage[t][...] = lraw[t][...].astype(MM)
            places.append(pltpu.make_async_copy(lstage[t], lo[t].at[g.me], lsem_out.at[t]))
            places[t].start()
        g.forward()
        g.finish()
        for j in range(3):
            k, dev = g._chip(j)
            _rcopy(sm, osm.at[k], ssend.at[j], srecv.at[j], dev).wait_recv()
            small_cps[j].wait_send()
        own.wait()
        for cp in places:
            cp.wait()

    out_shape = _Gather.out_shapes(shards)
    out_shape.append(jax.ShapeDtypeStruct((N_CHIPS,) + small.shape, small.dtype))
    out_shape += _Gather.out_shapes(later)
    sems = pltpu.SemaphoreType.DMA
    return pl.pallas_call(
        body, name="all_gather_weights", out_shape=out_shape,
        in_specs=[ANY] * (n + 1 + m), out_specs=[ANY] * (n + 1 + m),
        scratch_shapes=[sems((3,)), sems((3,)), sems, sems((m,)), sems((m,))]
                       + [pltpu.VMEM(s.shape, s.dtype) for s in later] + [pltpu.VMEM(s.shape, MM) for s in later]
                       + _Gather.scratch(shards),
        compiler_params=_params(),
    )(*shards, small, *later)


HBM_SPEC = pl.BlockSpec(memory_space=pltpu.HBM)
SEM_SPEC = pl.BlockSpec(memory_space=pltpu.SEMAPHORE)
DATAFLOW = pltpu.SideEffectType.DATAFLOW_SIDE_EFFECTING


def split_start(name, bufs, n_copies, copies):
    nb = len(bufs)

    def body(*refs):
        for cp in copies(refs[:nb], refs[nb], refs[nb + 1]):
            cp.start()
        token = refs[2 * nb + 2]
        token[...] = jnp.zeros_like(token)

    sems = [pltpu.SemaphoreType.DMA((n_copies,))] * 2
    res = pl.pallas_call(
        body, name=name,
        out_shape=sems + [pltpu.HBM(a.shape, a.dtype) for a in bufs] + [jax.ShapeDtypeStruct((8, 128), F32)],
        in_specs=[HBM_SPEC] * nb, out_specs=[SEM_SPEC] * 2 + [HBM_SPEC] * nb + [VMEM_SPEC],
        input_output_aliases={i: 2 + i for i in range(nb)},
        compiler_params=pltpu.CompilerParams(has_side_effects=DATAFLOW),
    )(*[pltpu.with_memory_space_constraint(a, pltpu.HBM) for a in bufs])
    return res[:-1], res[-1]


def split_wait(name, state, after, copies):
    sems, bufs = state[:2], state[2:]
    nb = len(bufs)

    def body(*refs):
        for cp in copies(refs[:nb], refs[nb], refs[nb + 1]):
            cp.wait_send()
            cp.wait_recv()

    return pl.pallas_call(
        body, name=name, out_shape=[pltpu.HBM(a.shape, a.dtype) for a in bufs],
        in_specs=[HBM_SPEC] * nb + [SEM_SPEC] * 2 + [ANY] * len(after), out_specs=[HBM_SPEC] * nb,
        input_output_aliases={i: i for i in range(nb)},
        compiler_params=pltpu.CompilerParams(has_side_effects=DATAFLOW),
    )(*bufs, *sems, *after)


class _Shifted:
    def __init__(self, sems, first):
        self.sems, self.first = sems, first

    @property
    def at(self):
        return self

    def __getitem__(self, i):
        return self.sems.at[self.first + i]


def _scatter_copies(n):
    def copies(refs, send, recv):
        _, _, c, chips = _place()
        return [_rcopy(refs[t].at[2 * cx + cy], refs[n + t].at[j], send.at[3 * t + j], recv.at[3 * t + j], (cx, cy, c))
                for t in range(n) for j, (cx, cy) in enumerate(chips)]
    return copies


def scatter_start(parts):
    lands = [lax.empty((3,) + p.shape[1:], p.dtype) for p in parts]
    return split_start("scatter_start_rest", list(parts) + lands, 3 * len(parts), _scatter_copies(len(parts)))


def scatter_wait(state, after):
    n = (len(state) - 2) // 2
    return split_wait("scatter_wait_rest", state, after, _scatter_copies(n))[n:]


def _gather_copies(shapes, level):
    n = len(shapes)

    def copies(refs, send, recv):
        x, y, c, chips = _place()
        out = []
        for t in range(n):
            rh = shapes[t][0] // 2
            for j, (cx, cy) in enumerate(chips):
                k, dev = (2 * x + y, (cx, cy, c)) if level == 1 else (2 * cx + cy, (x, y, 1 - c))
                blk = refs[t].at[k, pl.ds(c * rh, rh), :]
                out.append(_rcopy(blk, blk, send.at[3 * t + j], recv.at[3 * t + j], dev))
        return out
    return copies


def _sibling_copies(n, other_half):
    def copies(refs, send, recv):
        x, y, c, _ = _place()
        return [_rcopy(refs[t].at[:, 1 - c] if other_half else refs[t], refs[n + t], send.at[t], recv.at[t],
                       (x, y, 1 - c)) for t in range(n)]
    return copies


def sibling_start(srcs, other_half, tag):
    lands = [lax.empty((a.shape[0],) + a.shape[2:] if other_half else a.shape, a.dtype) for a in srcs]
    return split_start("sibling_start_" + tag, list(srcs) + lands, len(srcs),
                       _sibling_copies(len(srcs), other_half))


def sibling_wait(state, after, other_half, tag):
    n = (len(state) - 2) // 2
    res = split_wait("sibling_wait_" + tag, state, after, _sibling_copies(n, other_half))
    return res[:n], res[n:]


def small_pack(ddw, v512, v1024, loss_parts):
    rows, width = PACK_ROWS, 512
    n512, n1024 = len(VEC512), len(VEC1024)

    def body(*refs):
        ddw_ref = refs[0]
        a_refs = refs[1:1 + n512]
        b_refs = refs[1 + n512:1 + n512 + n1024]
        lp_ref, o_ref, p_ref = refs[1 + n512 + n1024:]
        p_ref[...] = jnp.zeros_like(p_ref)
        p_ref[0:32, :] = ddw_ref[...]
        p_ref[LOSS_ROW:LOSS_ROW + 1, 0:128] = jnp.sum(lp_ref[...], axis=0, keepdims=True) * 0.125
        for i, r in enumerate(a_refs):
            p_ref[32 + i:33 + i, :] = r[...]
        for i, r in enumerate(b_refs):
            base = 32 + n512 + 2 * i
            p_ref[base:base + 1, :] = r[:, 0:512]
            p_ref[base + 1:base + 2, :] = r[:, 512:1024]
        x, y, c, _ = _place()
        o_ref[4 * x + 2 * y + c] = p_ref[...]

    n_in = 2 + n512 + n1024
    return pl.pallas_call(
        body, name="small_pack", out_shape=jax.ShapeDtypeStruct((8, rows, width), F32),
        in_specs=[VMEM_SPEC] * n_in, out_specs=VMEM_SPEC,
        scratch_shapes=[pltpu.VMEM((rows, width), F32)],
    )(ddw, *[v512[n] for n in VEC512], *[v1024[n] for n in VEC1024], loss_parts)


def _small_copies(refs, send, recv):
    x, y, c, _ = _place()
    mine = refs[0].at[4 * x + 2 * y + c]
    peers = [(1 - x if k & 4 else x, 1 - y if k & 2 else y, 1 - c if k & 1 else c) for k in range(1, 8)]
    return [_rcopy(mine, mine, send.at[i], recv.at[i], dev) for i, dev in enumerate(peers)]


def _row_block(r):
    for tr in (512, 352, 256, 128):
        if r % tr == 0:
            return tr
    return r


def add_halves(g, recv, name):
    _, _, r, w = g.shape
    tr = _row_block(r)

    def body(g_ref, r_ref, ob_ref, own_ref):
        k = pl.program_id(1)
        me = 2 * lax.axis_index("x") + lax.axis_index("y")
        t = g_ref[0, 0].astype(F32) + r_ref[0].astype(F32)
        ob_ref[0] = t.astype(MM)
        mine = jnp.where(k == me, t, 0.0)

        @pl.when(k == 0)
        def _():
            own_ref[...] = mine

        @pl.when(k != 0)
        def _():
            own_ref[...] += mine

    return pl.pallas_call(
        body, name=name, grid=(r // tr, N_CHIPS),
        in_specs=[pl.BlockSpec((1, 1, tr, w), lambda i, k: (k, lax.axis_index("c"), i, 0)),
                  pl.BlockSpec((1, tr, w), lambda i, k: (k, i, 0))],
        out_specs=[pl.BlockSpec((1, tr, w), lambda i, k: (k, i, 0)),
                   pl.BlockSpec((tr, w), lambda i, k: (i, 0))],
        out_shape=(jax.ShapeDtypeStruct((N_CHIPS, r, w), MM), jax.ShapeDtypeStruct((r, w), F32)),
        compiler_params=_params(("arbitrary", "arbitrary")),
    )(g, recv)


def sum_parts(own, rin, after, name):
    _, r, w = rin.shape
    tr = _row_block(r)

    def body(o_ref, r_ref, after_ref, out_ref):
        out_ref[...] = ((o_ref[...] + r_ref[0].astype(F32)) + r_ref[1].astype(F32)) + r_ref[2].astype(F32)

    return pl.pallas_call(
        body, name=name, grid=(r // tr,), out_shape=jax.ShapeDtypeStruct((r, w), F32),
        in_specs=[pl.BlockSpec((tr, w), lambda i: (i, 0)), pl.BlockSpec((3, tr, w), lambda i: (0, i, 0)),
                  TOKEN_SPEC],
        out_specs=pl.BlockSpec((tr, w), lambda i: (i, 0)),
        compiler_params=_params(("arbitrary",)),
    )(own, rin, after)


def _adamw_math(w, g, m, v):
    mn = ADAM_B1 * m + (1.0 - ADAM_B1) * g
    vn = ADAM_B2 * v + (1.0 - ADAM_B2) * (g * g)
    m_hat = mn / (1.0 - ADAM_B1 ** ADAM_STEP)
    v_hat = vn / (1.0 - ADAM_B2 ** ADAM_STEP)
    return -ADAM_LR * (m_hat / (jnp.sqrt(v_hat) + ADAM_EPS) + ADAM_WD * w), mn, vn


def adamw(w, mine, other, m, v, name):
    r, c = w.shape
    rh = r // 2
    tr = _row_block(rh)
    if c >= 1024 and tr % 512 == 0:
        tr = 256
    nb = rh // tr

    def body(w_ref, a_ref, b_ref, m_ref, v_ref, go_ref, d_ref, mo_ref, vo_ref):
        gv = jnp.where(lax.axis_index("c") == pl.program_id(0), a_ref[...], b_ref[...])
        go_ref[...] = gv
        d_ref[...], mo_ref[...], vo_ref[...] = _adamw_math(w_ref[...], gv, m_ref[...], v_ref[...])

    def half(of_sibling):
        def index(h, i):
            owner = lax.axis_index("c")
            owner = 1 - owner if of_sibling else owner
            return jnp.where(h == owner, i, jnp.where(h < owner, 0, nb - 1)), 0
        return pl.BlockSpec((tr, c), index)

    spec = pl.BlockSpec((tr, c), lambda h, i: (h * nb + i, 0))
    out = jax.ShapeDtypeStruct((r, c), F32)
    return pl.pallas_call(
        body, name=name, grid=(2, nb), out_shape=(out, out, out, out),
        in_specs=[spec, half(False), half(True), spec, spec], out_specs=[spec] * 4,
        compiler_params=_params(("arbitrary", "arbitrary")),
    )(w, mine, other, m, v)


def adamw_small(packs, params, after):
    names = list(params)
    flat = [a for n in names for a in params[n]]

    def body(*refs):
        p_ref = refs[0]
        ins = refs[1:1 + 3 * len(names)]
        loss_ref, g_ref = refs[2 + 3 * len(names):4 + 3 * len(names)]
        outs = refs[4 + 3 * len(names):]
        total = p_ref[0]
        for d in range(1, 8):
            total = total + p_ref[d]
        g_ref[...] = total
        loss_ref[...] = g_ref[LOSS_ROW:LOSS_ROW + 1, 0:1]
        me = 2 * lax.axis_index("x") + lax.axis_index("y")
        for i, n in enumerate(names):
            w_ref, m_ref, v_ref = ins[3 * i:3 * i + 3]
            go_ref, d_ref, mo_ref, vo_ref = outs[4 * i:4 * i + 4]
            if n == "conv_dw_w":
                gv = jnp.zeros((CONV_WIDTH, 128), F32)
                for k in range(N_CHIPS):
                    gv = gv + jnp.where(me == k, g_ref[0:CONV_WIDTH, 128 * k:128 * (k + 1)], 0.0)
            elif n in VEC512:
                r0 = 32 + VEC512.index(n)
                gv = g_ref[r0:r0 + 1, :]
            else:
                r0 = 32 + len(VEC512) + 2 * VEC1024.index(n)
                gv = jnp.concatenate([g_ref[r0:r0 + 1, :], g_ref[r0 + 1:r0 + 2, :]], axis=1)
            go_ref[...] = gv
            d_ref[...], mo_ref[...], vo_ref[...] = _adamw_math(w_ref[...], gv, m_ref[...], v_ref[...])

    out_shape = [jax.ShapeDtypeStruct((1, 1), F32), jax.ShapeDtypeStruct(packs.shape[1:], F32)]
    out_shape += [jax.ShapeDtypeStruct(params[n][0].shape, F32) for n in names for _ in range(4)]
    res = pl.pallas_call(
        body, name="adamw_small", out_shape=out_shape,
        in_specs=[VMEM_SPEC] * (2 + len(flat)), out_specs=[VMEM_SPEC] * len(out_shape),
        compiler_params=_params(),
    )(packs, *flat, after)
    return res[0], res[1], {n: res[2 + 4 * i:6 + 4 * i] for i, n in enumerate(names)}


REST = ("w_ffn_up", "w_ffn_down", "w_out", "w_conv_branch", "w_att_branch")
VEC512 = ("conv_dw_b", "conv_ln_g", "conv_ln_b")
VEC1024 = ("norm_mix_pre", "b_conv_branch", "norm_mix_post", "norm_ffn_pre", "norm_ffn_post")
PACK_ROWS = 48
LOSS_ROW = 47


def kernel(x, norm_mix_pre, w_in, conv_dw_w, conv_dw_b, conv_ln_g, conv_ln_b, w_conv_branch, b_conv_branch, w_att_branch, w_out, norm_mix_post, norm_ffn_pre, w_ffn_up, w_ffn_down, norm_ffn_post, loss_target, m_norm_mix_pre, m_w_in, m_conv_dw_w, m_conv_dw_b, m_conv_ln_g, m_conv_ln_b, m_w_conv_branch, m_b_conv_branch, m_w_att_branch, m_w_out, m_norm_mix_post, m_norm_ffn_pre, m_w_ffn_up, m_w_ffn_down, m_norm_ffn_post, v_norm_mix_pre, v_w_in, v_conv_dw_w, v_conv_dw_b, v_conv_ln_g, v_conv_ln_b, v_w_conv_branch, v_b_conv_branch, v_w_att_branch, v_w_out, v_norm_mix_post, v_norm_ffn_pre, v_w_ffn_up, v_w_ffn_down, v_norm_ffn_post):
    weights = dict(norm_mix_pre=norm_mix_pre, w_in=w_in, conv_dw_w=conv_dw_w, conv_dw_b=conv_dw_b, conv_ln_g=conv_ln_g, conv_ln_b=conv_ln_b, w_conv_branch=w_conv_branch, b_conv_branch=b_conv_branch, w_att_branch=w_att_branch, w_out=w_out, norm_mix_post=norm_mix_post, norm_ffn_pre=norm_ffn_pre, w_ffn_up=w_ffn_up, w_ffn_down=w_ffn_down, norm_ffn_post=norm_ffn_post)
    mom = dict(norm_mix_pre=m_norm_mix_pre, w_in=m_w_in, conv_dw_w=m_conv_dw_w, conv_dw_b=m_conv_dw_b, conv_ln_g=m_conv_ln_g, conv_ln_b=m_conv_ln_b, w_conv_branch=m_w_conv_branch, b_conv_branch=m_b_conv_branch, w_att_branch=m_w_att_branch, w_out=m_w_out, norm_mix_post=m_norm_mix_post, norm_ffn_pre=m_norm_ffn_pre, w_ffn_up=m_w_ffn_up, w_ffn_down=m_w_ffn_down, norm_ffn_post=m_norm_ffn_post)
    var = dict(norm_mix_pre=v_norm_mix_pre, w_in=v_w_in, conv_dw_w=v_conv_dw_w, conv_dw_b=v_conv_dw_b, conv_ln_g=v_conv_ln_g, conv_ln_b=v_conv_ln_b, w_conv_branch=v_w_conv_branch, b_conv_branch=v_b_conv_branch, w_att_branch=v_w_att_branch, w_out=v_w_out, norm_mix_post=v_norm_mix_post, norm_ffn_pre=v_norm_ffn_pre, w_ffn_up=v_w_ffn_up, w_ffn_down=v_w_ffn_down, norm_ffn_post=v_norm_ffn_post)
    order = list(weights)
    grads, deltas, new_m, new_v = {}, {}, {}, {}
    xs = x.reshape(SEQ, D_MODEL)
    tgt = loss_target.reshape(SEQ, D_MODEL)
    row = lambda a: a.reshape(1, -1)
    g1, g2, g3, g4 = (row(weights[n]) for n in ("norm_mix_pre", "norm_mix_post", "norm_ffn_pre", "norm_ffn_post"))
    ln_g, ln_b = row(conv_ln_g), row(conv_ln_b)

    summed, from_chips = {}, {}

    def core_sums(names, state, after, tag):
        own, from_sibling = sibling_wait(state, after, True, tag)
        for n, g, r in zip(names, own, from_sibling):
            summed[n] = add_halves(g, r, "add_" + n)

    def chip_sums(names, after):
        return [sum_parts(summed[n][1], from_chips[n], after, "sum_" + n) for n in names]

    def optimize(names, state, after, tag):
        mine, other = sibling_wait(state, after, False, tag)
        for n, a, b in zip(names, mine, other):
            grads[n], deltas[n], new_m[n], new_v[n] = adamw(weights[n], a, b, mom[n], var[n], "adamw_" + n)

    w_in_g, dw_g, *rest = all_gather_weights([w_in], conv_dw_w, [weights[n] for n in REST])
    w_dw_full = jnp.concatenate([dw_g[k] for k in range(N_CHIPS)], axis=1)
    rest_shapes = [weights[n].shape for n in REST]
    state, token = split_start("gather_start", rest, 3 * len(REST), _gather_copies(rest_shapes, 1))
    h1, ci, q, k, v, gc, ga = in_proj_fwd(xs, g1, w_in_g, token)
    u1, u3 = conv_fwd(ci, w_dw_full, row(conv_dw_b), ln_g, ln_b)
    att, rc = attn_fwd(q, k, v)
    rest = split_wait("gather_wait", state, [att], _gather_copies(rest_shapes, 1))
    pass_copies = _gather_copies(rest_shapes[2:] + rest_shapes[:2], 2)
    n_first = 3 * len(REST[2:])
    state, token = split_start("pass_start", rest[2:] + rest[:2], 3 * len(REST), pass_copies)
    passed = split_wait("pass_mix_wait", state, [], lambda *a: pass_copies(*a)[:n_first])
    w_out_g, w_cb_g, w_ab_g = passed[:3]
    w_out_g = w_out_g.reshape(D_MODEL, D_MODEL)
    merged, mix, x2, h2 = mix_fwd(u3, att, gc, ga, xs, w_cb_g, row(b_conv_branch), w_ab_g, w_out_g, g2, g3, token)
    w_up_g, w_down_g = split_wait(
        "pass_ffn_wait", list(state[:2]) + list(passed[3:]), [h2],
        lambda refs, send, recv: _gather_copies(rest_shapes[:2], 2)(refs, _Shifted(send, n_first), _Shifted(recv, n_first)))
    w_down_g = w_down_g.reshape(D_FF, D_MODEL)
    gate, up, act = ffn_up_fwd(h2, w_up_g)
    dff, dy, loss_parts, dg4 = ffn_down_loss(act, w_down_g, x2, tgt, g4)

    dgu = ffn_act_bwd(dff, w_down_g, gate, up)
    dx2, dmix, dg3, dg2 = ffn_in_bwd(dgu, w_up_g, x2, mix, dy, g3, g2)
    ffn_grads = [weight_grad(h2, dgu, "dw_ffn_up", True), weight_grad(act, dff, "dw_ffn_down", False, tk=UP_SHARD)]
    to_ffn, token = sibling_start(ffn_grads, True, "dw_ffn")
    dco, dao, dg, du3, datt, dbcb = merge_bwd(dmix, w_out_g, gc, ga, u3, att, w_cb_g, row(b_conv_branch), w_ab_g,
                                              token)
    to_mix, token = sibling_start(weight_grad_mix(merged, dmix, u3, dco, att, dao), True, "dw_mix")
    core_sums(REST[:2], to_ffn, [token], "dw_ffn")
    core_sums(REST[2:], to_mix, [summed["w_ffn_down"][1]], "dw_mix")
    state, token = scatter_start([summed[n][0] for n in REST])
    dci, ddw, dbdw, dlng, dlnb = conv_bwd(du3, u1, ci, w_dw_full, ln_g, ln_b, token)
    dqkv = attn_bwd(q, k, v, datt, rc, token)
    from_chips.update(zip(REST, scatter_wait(state, [dci, dqkv])))
    dproj = (dci, dqkv, dg)
    to_in, token = sibling_start([weight_grad_in(h1, dproj)], True, "dw_in")
    grad_x, dg1 = in_proj_bwd(dproj, w_in_g, xs, dx2, g1, token)
    v512 = dict(conv_dw_b=dbdw, conv_ln_g=dlng, conv_ln_b=dlnb)
    v1024 = dict(norm_mix_pre=dg1, b_conv_branch=dbcb, norm_mix_post=dg2, norm_ffn_pre=dg3, norm_ffn_post=dg4)
    packs = small_pack(ddw, v512, v1024, loss_parts)
    core_sums(("w_in",), to_in, [packs], "dw_in")
    to_chips = summed["w_in"][0]
    landing = lax.empty((3,) + to_chips.shape[1:], to_chips.dtype)

    def scatter_and_packs(refs, send, recv):
        return (_scatter_copies(1)(refs[:2], send, recv)
                + _small_copies(refs[2:], _Shifted(send, 3), _Shifted(recv, 3)))

    state, token = split_start("scatter_start_w_in", [to_chips, landing, packs], 3 + 7, scatter_and_packs)
    swap_up, token = sibling_start(chip_sums(REST[:1], token), False, "sum_ffn_up")
    swap_rest, token = sibling_start(chip_sums(REST[1:], token), False, "sum_rest")
    optimize(REST[:1], swap_up, [token], "sum_ffn_up")
    optimize(REST[1:], swap_rest, [new_v["w_ffn_up"]], "sum_rest")
    _, from_chips["w_in"], packs = split_wait("scatter_wait_w_in", state, [new_v[n] for n in REST], scatter_and_packs)
    swap_in, token = sibling_start(chip_sums(("w_in",), token), False, "sum_w_in")
    as_rows = lambda n, a: a if n == "conv_dw_w" else a.reshape(1, -1)
    small_names = ("conv_dw_w",) + VEC512 + VEC1024
    loss, gsum, small = adamw_small(
        packs, {n: tuple(as_rows(n, d[n]) for d in (weights, mom, var)) for n in small_names}, token)
    optimize(("w_in",), swap_in, [gsum], "sum_w_in")
    for n in small_names:
        grads[n], deltas[n], new_m[n], new_v[n] = (a.reshape(weights[n].shape) for a in small[n])

    return (loss.reshape(()), grad_x.reshape(1, SEQ, D_MODEL),*[grads[n] for n in order], *[deltas[n] for n in order],
            *[new_m[n] for n in order], *[new_v[n] for n in order])
```

```python
import jax
import jax.numpy as jnp
from jax import lax
from jax.experimental import pallas as pl
from jax.experimental.pallas import tpu as pltpu

F32 = jnp.float32
MM = jnp.bfloat16

SEQ = 2048
D_MODEL = 1024
CONV_DIM = 512
ATT_DIM = 512
CONV_WIDTH = 31
D_FF = 2816
IN_COLS = 2 * CONV_DIM + 3 * ATT_DIM + 2 * D_MODEL
N_CHIPS = 4
IN_SHARD = IN_COLS // N_CHIPS
UP_SHARD = 2 * D_FF // N_CHIPS
BR_SHARD = D_MODEL // N_CHIPS
EPS = 1e-6
ATT_SCALE = 0.125

TM = 256
GLU_ROWS = 256
TQ = 128
CONV_TILE = 64
CONV_WIN = CONV_TILE + 32
VMEM_LIMIT = 56 * 1024 * 1024

ADAM_LR = 0.001
ADAM_B1 = 0.9
ADAM_B2 = 0.999
ADAM_EPS = 1e-08
ADAM_WD = 0.01
ADAM_STEP = 10

MESH = pl.DeviceIdType.MESH
ANY = pl.BlockSpec(memory_space=pl.ANY)
VMEM_SPEC = pl.BlockSpec(memory_space=pltpu.VMEM)

NT_DIMS = (((1,), (1,)), ((), ()))
TN_DIMS = (((0,), (0,)), ((), ()))

IN_PIECES = (("ci", 0, 1024), ("q", 1024, 1536), ("k", 1536, 2048), ("v", 2048, 2560),
             ("gc", 2560, 3584), ("ga", 3584, 4608))


def _params(sem=None, vmem=VMEM_LIMIT):
    return pltpu.CompilerParams(dimension_semantics=sem, vmem_limit_bytes=vmem)


def _dot(a, b):
    return jnp.dot(a, b, preferred_element_type=F32)


def _dot_nt(a, b):
    return lax.dot_general(a, b, NT_DIMS, preferred_element_type=F32)


def _dot_tn(a, b):
    return lax.dot_general(a, b, TN_DIMS, preferred_element_type=F32)


def _sigmoid(x):
    return 1.0 / (1.0 + jnp.exp(-x))


def _rms(x):
    r = lax.rsqrt(jnp.mean(x * x, axis=-1, keepdims=True) + EPS)
    return x * r, r


def _rms_bwd(dy_g, n, r):
    return r * (dy_g - n * jnp.mean(dy_g * n, axis=-1, keepdims=True))


def _row_tile_spec(width, tm=TM):
    return pl.BlockSpec((tm, width), lambda i: (i, 0))


def _full_spec(shape):
    nd = len(shape)
    return pl.BlockSpec(shape, lambda *_: (0,) * nd)


def _weight_spec(shape):
    nd = len(shape)
    return pl.BlockSpec(shape, lambda *_: (0,) * nd, pipeline_mode=pl.Buffered(1))


def _acc_rows(ref, val, first):
    @pl.when(first)
    def _():
        ref[...] = val

    @pl.when(jnp.logical_not(first))
    def _():
        ref[...] += val


TOKEN_SPEC = pl.BlockSpec((8, 128), lambda *_: (0, 0))


def in_proj_fwd(x, g1, w_in_g, after):
    def body(x_ref, g_ref, w_ref, after_ref, h_ref, ci_ref, q_ref, k_ref, v_ref, gc_ref, ga_ref):
        n, _ = _rms(x_ref[...])
        h = (n * g_ref[...]).astype(MM)
        h_ref[...] = h
        outs = dict(ci=ci_ref, q=q_ref, k=k_ref, v=v_ref, gc=gc_ref, ga=ga_ref)
        for j in range(N_CHIPS):
            p = _dot(h, w_ref[j])
            g0 = j * IN_SHARD
            for name, s, e in IN_PIECES:
                lo, hi = max(s, g0), min(e, g0 + IN_SHARD)
                if lo < hi:
                    ref = outs[name]
                    part = p[:, lo - g0:hi - g0]
                    if name == "q":
                        part = part * ATT_SCALE
                    ref[:, lo - s:hi - s] = part.astype(ref.dtype)

    out_shape = [
        jax.ShapeDtypeStruct((SEQ, D_MODEL), MM),
        jax.ShapeDtypeStruct((SEQ, 2 * CONV_DIM), F32),
        jax.ShapeDtypeStruct((SEQ, ATT_DIM), MM),
        jax.ShapeDtypeStruct((SEQ, ATT_DIM), MM),
        jax.ShapeDtypeStruct((SEQ, ATT_DIM), MM),
        jax.ShapeDtypeStruct((SEQ, D_MODEL), F32),
        jax.ShapeDtypeStruct((SEQ, D_MODEL), F32),
    ]
    return pl.pallas_call(
        body, name="in_proj_fwd", grid=(SEQ // TM,), out_shape=out_shape,
        in_specs=[_row_tile_spec(D_MODEL), _full_spec((1, D_MODEL)), _weight_spec(w_in_g.shape), TOKEN_SPEC],
        out_specs=[_row_tile_spec(s.shape[1]) for s in out_shape],
        compiler_params=_params(("arbitrary",)),
    )(x, g1, w_in_g, after)


LANE_GROUPS = [slice(g, g + 128) for g in range(0, CONV_DIM, 128)]
NORM_ROWS = 16


def _shifted_windows(src_ref, t0, cols, offsets):
    win = src_ref[pl.ds(t0, CONV_WIN), cols]
    for rot in range(8):
        ms = [m for m in offsets if m % 8 == rot]
        if ms:
            shifted = win if rot == 0 else pltpu.roll(win, CONV_WIN - rot, 0)
            for m in ms:
                yield m, shifted[m - rot:m - rot + CONV_TILE, :]


def _shifted_sum(src_ref, t0, cols, w_ref, offset_of_tap):
    tap_at = {offset_of_tap(j): j for j in range(CONV_WIDTH)}
    acc = None
    for m, rows in _shifted_windows(src_ref, t0, cols, sorted(tap_at)):
        t = w_ref[tap_at[m]:tap_at[m] + 1, cols] * rows
        acc = t if acc is None else acc + t
    return acc


def _fetch(srcs, dsts, sems):
    copies = [pltpu.make_async_copy(s, d, sems.at[i]) for i, (s, d) in enumerate(zip(srcs, dsts))]
    for cp in copies:
        cp.start()
    return copies


def _row_chunks(src, dst, sems):
    def chunk(i):
        t0 = i * GLU_ROWS
        rows = pl.ds(t0 if isinstance(i, int) else pl.multiple_of(t0, GLU_ROWS), GLU_ROWS)
        return pltpu.make_async_copy(src.at[rows, :], dst.at[rows, :], sems.at[i])

    for i in range(SEQ // GLU_ROWS):
        chunk(i).start()
    return chunk


def _glu_into(ci_chunk, ci_ref, upad_ref):
    upad_ref[0:32, :] = jnp.zeros((32, CONV_DIM), F32)

    def step(i, c):
        ci_chunk(i).wait()
        t0 = pl.multiple_of(i * GLU_ROWS, GLU_ROWS)
        a = ci_ref[pl.ds(t0, GLU_ROWS), 0:CONV_DIM]
        b = ci_ref[pl.ds(t0, GLU_ROWS), CONV_DIM:2 * CONV_DIM]
        upad_ref[pl.ds(t0 + 32, GLU_ROWS), :] = a * _sigmoid(b)
        return c

    lax.fori_loop(0, SEQ // GLU_ROWS, step, 0)


def _layernorm_parts(u1):
    mu = jnp.mean(u1, axis=-1, keepdims=True)
    xc = u1 - mu
    rstd = lax.rsqrt(jnp.mean(xc * xc, axis=-1, keepdims=True) + EPS)
    return xc * rstd, rstd


def conv_fwd(ci, w_dw, b_dw, ln_g, ln_b):
    def body(ci_hbm, w_ref, b_ref, g_ref, bb_ref, u1_ref, u3_ref, upad_ref, ci_ref, sems):
        _glu_into(_row_chunks(ci_hbm, ci_ref, sems), ci_ref, upad_ref)

        def step(i, c):
            t0 = pl.multiple_of(i * CONV_TILE, CONV_TILE)
            for cols in LANE_GROUPS:
                u1_ref[pl.ds(t0, CONV_TILE), cols] = (_shifted_sum(upad_ref, t0, cols, w_ref, lambda j: j + 2)
                                                      + b_ref[:, cols])
            for r in range(0, CONV_TILE, NORM_ROWS):
                rows = pl.ds(t0 + r, NORM_ROWS)
                xh, _ = _layernorm_parts(u1_ref[rows, :])
                u2 = xh * g_ref[...] + bb_ref[...]
                u3_ref[rows, :] = (u2 * _sigmoid(u2)).astype(MM)
            return c

        lax.fori_loop(0, SEQ // CONV_TILE, step, 0)

    return pl.pallas_call(
        body, name="conv_fwd",
        out_shape=[jax.ShapeDtypeStruct((SEQ, CONV_DIM), F32), jax.ShapeDtypeStruct((SEQ, CONV_DIM), MM)],
        in_specs=[ANY] + [VMEM_SPEC] * 4, out_specs=[VMEM_SPEC] * 2,
        scratch_shapes=[pltpu.VMEM((SEQ + 32, CONV_DIM), F32), pltpu.VMEM(ci.shape, ci.dtype),
                        pltpu.SemaphoreType.DMA((SEQ // GLU_ROWS,))],
        compiler_params=_params(),
    )(ci, w_dw, b_dw, ln_g, ln_b)


def _softplus(z):
    return jnp.maximum(z, 0.0) + jnp.log(1.0 + jnp.exp(-jnp.abs(z)))


def _cumsum_weights(suffix, with_total):
    n = 256 if with_total else 128
    r = lax.broadcasted_iota(jnp.int32, (128, n), 0)
    c = lax.broadcasted_iota(jnp.int32, (128, n), 1)
    tri = (r >= c) if suffix else (r <= c)
    return jnp.logical_or(tri, c >= 128).astype(MM)


NO_SCORE = -1e30
N_KB = SEQ // TQ


def _score_bias(lane, row, i, j):
    keep = jnp.logical_and(i >= 0, jnp.logical_or(j < i, lane < row))
    return jnp.where(keep, 0.0, NO_SCORE)


def _block_pipeline(n_stages, descending, step, on_query_block=None):
    n_lag = n_stages - 1
    none = jnp.int32(-1)

    def shift(cur, lag):
        step([cur] + [(lag[2 * s], lag[2 * s + 1]) for s in range(n_lag)])
        return (cur[0], cur[1]) + tuple(lag[:-2])

    def outer(i, lag):
        if on_query_block is not None:
            on_query_block(i)

        def inner(n, lag):
            return shift((i, i - n if descending else n), lag)
        return lax.fori_loop(0, i + 1, inner, lag)

    lag = lax.fori_loop(0, N_KB, outer, (none,) * (2 * n_lag))
    lax.fori_loop(0, n_lag, lambda n, lag: shift((none, none), lag), lag)


def _head_masks():
    lane = lax.broadcasted_iota(jnp.int32, (TQ, 128), 1)
    row = lax.broadcasted_iota(jnp.int32, (TQ, 128), 0)
    return lane, row, lane < 64


def _pick_head(x, head0, h):
    zero = jnp.zeros_like(x)
    return jnp.where(head0, x, zero) if h == 0 else jnp.where(head0, zero, x)


N_PAIRS = ATT_DIM // 128


def _split_heads(src_ref, dst_ref):
    _, _, head0 = _head_masks()

    def block(b, c):
        r0 = pl.multiple_of(b * TQ, TQ)
        d0 = pl.multiple_of(b * 2 * TQ, 2 * TQ)
        for p in range(N_PAIRS):
            x = src_ref[pl.ds(r0, TQ), 128 * p:128 * (p + 1)]
            for h in range(2):
                dst_ref[p, pl.ds(d0 + TQ * h, TQ), :] = _pick_head(x, head0, h)
        return c

    lax.fori_loop(0, N_KB, block, 0)


def attn_fwd(q, k, v):
    def body(q_hbm, k_hbm, v_hbm, o_ref, rc_ref, acc_ref, r_ref, z_ref, spb_ref, ab_ref, qm_ref, vm_ref,
             q_ref, k_ref, v_ref, sems):
        arrive = _fetch((q_hbm, v_hbm, k_hbm), (q_ref, v_ref, k_ref), sems)
        lane, row, _ = _head_masks()
        w = _cumsum_weights(suffix=True, with_total=True)
        acc_ref[...] = jnp.zeros_like(acc_ref)
        r_ref[...] = jnp.zeros_like(r_ref)
        rc_ref[...] = jnp.zeros_like(rc_ref)
        z_ref[...] = jnp.full(z_ref.shape, NO_SCORE, F32)
        spb_ref[...] = jnp.zeros_like(spb_ref)
        ab_ref[...] = jnp.zeros_like(ab_ref)
        arrive[0].wait()
        _split_heads(q_ref, qm_ref)
        arrive[1].wait()
        _split_heads(v_ref, vm_ref)
        arrive[2].wait()

        def step(pairs):
            (i1, j1), (i2, j2), (i3, j3) = pairs
            k1, q2, q3 = (pl.multiple_of(jnp.maximum(b, 0) * TQ, TQ) for b in (j1, i2, i3))
            q1, k3 = (pl.multiple_of(jnp.maximum(b, 0) * 2 * TQ, 2 * TQ) for b in (i1, j3))
            bias1 = _score_bias(lane, row, i1, j1)
            first2 = j2 == i2
            rc_rows = rc_ref[pl.ds(q2, TQ), :]
            for p in range(N_PAIRS):
                cols = slice(128 * p, 128 * (p + 1))
                kb = k_ref[pl.ds(k1, TQ), cols]
                acc_ref[pl.ds(q3, TQ), cols] += _dot(ab_ref[p], vm_ref[p, pl.ds(k3, 2 * TQ), :])
                for h in range(2):
                    hh = 2 * p + h
                    r = _dot(spb_ref[hh], w)
                    r_in = jnp.where(first2, 0.0, r_ref[hh])
                    ab_ref[p, :, 128 * h:128 * (h + 1)] = jnp.exp(z_ref[hh] - (r[:, :128] + r_in)).astype(MM)
                    rc_rows = jnp.where(jnp.logical_and(lane == 16 * hh + j2, i2 >= 0), r_in, rc_rows)
                    r_ref[hh] = r_in + r[:, 128:]
                    z = _dot_nt(qm_ref[p, pl.ds(q1 + TQ * h, TQ), :], kb) + bias1
                    z_ref[hh] = z
                    spb_ref[hh] = _softplus(z).astype(MM)
            rc_ref[pl.ds(q2, TQ), :] = rc_rows

        _block_pipeline(3, True, step)
        o_ref[...] = acc_ref[...].astype(MM)

    return pl.pallas_call(
        body, name="attn_fwd",
        out_shape=[jax.ShapeDtypeStruct((SEQ, ATT_DIM), MM), jax.ShapeDtypeStruct((SEQ, 128), F32)],
        in_specs=[ANY] * 3, out_specs=[VMEM_SPEC] * 2,
        scratch_shapes=[pltpu.VMEM((SEQ, ATT_DIM), F32), pltpu.VMEM((8, TQ, 128), F32),
                        pltpu.VMEM((8, TQ, 128), F32), pltpu.VMEM((8, TQ, 128), MM),
                        pltpu.VMEM((N_PAIRS, TQ, 256), MM), pltpu.VMEM((N_PAIRS, 2 * SEQ, 128), MM),
                        pltpu.VMEM((N_PAIRS, 2 * SEQ, 128), MM)]
                       + [pltpu.VMEM(a.shape, a.dtype) for a in (q, k, v)] + [pltpu.SemaphoreType.DMA((3,))],
        compiler_params=_params(),
    )(q, k, v)


def _branch_outputs(u_ref, a_ref, wcb_ref, bcb_ref, wab_ref):
    u = u_ref[...]
    a = a_ref[...]
    co = jnp.concatenate([_dot(u, wcb_ref[j]) for j in range(N_CHIPS)], axis=1) + bcb_ref[...]
    ao = jnp.concatenate([_dot(a, wab_ref[j]) for j in range(N_CHIPS)], axis=1)
    return co, ao


def mix_fwd(u3, att, gc, ga, x, w_cb_g, b_cb, w_ab_g, w_out_g, g2, g3, after):
    def body(u_ref, a_ref, gc_ref, ga_ref, x_ref, wcb_ref, bcb_ref, wab_ref, wout_ref, g2_ref, g3_ref, after_ref,
             mg_ref, mix_ref, x2_ref, h2_ref):
        co, ao = _branch_outputs(u_ref, a_ref, wcb_ref, bcb_ref, wab_ref)
        merged = (_sigmoid(gc_ref[...]) * co + _sigmoid(ga_ref[...]) * ao).astype(MM)
        mg_ref[...] = merged
        mix = _dot(merged, wout_ref[...])
        mix_ref[...] = mix
        n2, _ = _rms(mix)
        x2 = x_ref[...] + n2 * g2_ref[...]
        x2_ref[...] = x2
        n3, _ = _rms(x2)
        h2_ref[...] = (n3 * g3_ref[...]).astype(MM)

    out_shape = [
        jax.ShapeDtypeStruct((SEQ, D_MODEL), MM), jax.ShapeDtypeStruct((SEQ, D_MODEL), F32),
        jax.ShapeDtypeStruct((SEQ, D_MODEL), F32), jax.ShapeDtypeStruct((SEQ, D_MODEL), MM),
    ]
    vec = _full_spec((1, D_MODEL))
    return pl.pallas_call(
        body, name="mix_fwd", grid=(SEQ // TM,), out_shape=out_shape,
        in_specs=[_row_tile_spec(CONV_DIM), _row_tile_spec(ATT_DIM), _row_tile_spec(D_MODEL),
                  _row_tile_spec(D_MODEL), _row_tile_spec(D_MODEL), _weight_spec(w_cb_g.shape), vec,
                  _weight_spec(w_ab_g.shape), _weight_spec(w_out_g.shape), vec, vec, TOKEN_SPEC],
        out_specs=[_row_tile_spec(D_MODEL)] * 4,
        compiler_params=_params(("arbitrary",)),
    )(u3, att, gc, ga, x, w_cb_g, b_cb, w_ab_g, w_out_g, g2, g3, after)


def ffn_up_fwd(h2, w_up_g):
    def body(h_ref, wg_ref, wu_ref, gate_ref, up_ref, act_ref):
        h = h_ref[...]
        gate = _dot(h, wg_ref[0])
        up = _dot(h, wu_ref[0])
        gate_ref[...] = gate.astype(MM)
        up_ref[...] = up.astype(MM)
        act_ref[...] = (gate * _sigmoid(gate) * up).astype(MM)

    tile = pl.BlockSpec((TM, UP_SHARD), lambda n, i: (i, n))
    act = jax.ShapeDtypeStruct((SEQ, D_FF), MM)
    return pl.pallas_call(
        body, name="ffn_up_fwd", grid=(2, SEQ // TM), out_shape=[act, act, act],
        in_specs=[pl.BlockSpec((TM, D_MODEL), lambda n, i: (i, 0)),
                  pl.BlockSpec((1, D_MODEL, UP_SHARD), lambda n, i: (n, 0, 0)),
                  pl.BlockSpec((1, D_MODEL, UP_SHARD), lambda n, i: (n + 2, 0, 0))],
        out_specs=[tile, tile, tile],
        compiler_params=_params(("arbitrary", "arbitrary")),
    )(h2, w_up_g, w_up_g)


def ffn_down_loss(act, w_down_g, x2, target, g4):
    def body(act_ref, wd_ref, x2_ref, t_ref, g_ref, dff_ref, dy_ref, loss_ref, dg_ref):
        ff = _dot(act_ref[...], wd_ref[...])
        n4, r4 = _rms(ff)
        g4v = g_ref[...]
        err = x2_ref[...] + n4 * g4v - t_ref[...]
        row_loss = jnp.mean(err * err, axis=-1, keepdims=True)
        loss_ref[...] = jnp.zeros((8, 128), F32) + 0.5 * jnp.sum(row_loss, axis=0, keepdims=True)
        dy = err * (1.0 / D_MODEL)
        dy_ref[...] = dy
        dff_ref[...] = _rms_bwd(dy * g4v, n4, r4).astype(MM)
        _acc_rows(dg_ref, jnp.sum(dy * n4, axis=0, keepdims=True), pl.program_id(0) == 0)

    nt = SEQ // TM
    vec = _full_spec((1, D_MODEL))
    return pl.pallas_call(
        body, name="ffn_down_loss", grid=(nt,),
        out_shape=(jax.ShapeDtypeStruct((SEQ, D_MODEL), MM), jax.ShapeDtypeStruct((SEQ, D_MODEL), F32),
                   jax.ShapeDtypeStruct((nt * 8, 128), F32), jax.ShapeDtypeStruct((1, D_MODEL), F32)),
        in_specs=[_row_tile_spec(D_FF), _weight_spec(w_down_g.shape), _row_tile_spec(D_MODEL),
                  _row_tile_spec(D_MODEL), vec],
        out_specs=[_row_tile_spec(D_MODEL), _row_tile_spec(D_MODEL),
                   pl.BlockSpec((8, 128), lambda i: (i, 0)), vec],
        compiler_params=_params(("arbitrary",)),
    )(act, w_down_g, x2, target, g4)


def ffn_act_bwd(dff, w_down_g, gate, up):
    def body(dff_ref, wd_ref, gate_ref, up_ref, dgu_ref):
        dact = _dot_nt(dff_ref[...], wd_ref[...])
        gate = gate_ref[...].astype(F32)
        sg = _sigmoid(gate)
        dgu_ref[:, 0:D_FF] = (dact * up_ref[...].astype(F32) * (sg * (1.0 + gate * (1.0 - sg)))).astype(MM)
        dgu_ref[:, D_FF:2 * D_FF] = (dact * (gate * sg)).astype(MM)

    return pl.pallas_call(
        body, name="ffn_act_bwd", grid=(SEQ // TM,),
        out_shape=jax.ShapeDtypeStruct((SEQ, 2 * D_FF), MM),
        in_specs=[_row_tile_spec(D_MODEL), _weight_spec(w_down_g.shape), _row_tile_spec(D_FF), _row_tile_spec(D_FF)],
        out_specs=_row_tile_spec(2 * D_FF),
        compiler_params=_params(("arbitrary",)),
    )(dff, w_down_g, gate, up)


def ffn_in_bwd(dgu, w_up_g, x2, mix, dy, g3, g2):
    def body(dgu_ref, w_ref, x2_ref, mix_ref, dy_ref, g3_ref, g2_ref, dx2_ref, dmix_ref, dg3_ref, dg2_ref):
        dh2 = None
        for j in range(N_CHIPS):
            t = _dot_nt(dgu_ref[:, j * UP_SHARD:(j + 1) * UP_SHARD], w_ref[j])
            dh2 = t if dh2 is None else dh2 + t
        first = pl.program_id(0) == 0
        n3, r3 = _rms(x2_ref[...])
        dx2 = dy_ref[...] + _rms_bwd(dh2 * g3_ref[...], n3, r3)
        dx2_ref[...] = dx2
        _acc_rows(dg3_ref, jnp.sum(dh2 * n3, axis=0, keepdims=True), first)
        n2, r2 = _rms(mix_ref[...])
        dmix_ref[...] = _rms_bwd(dx2 * g2_ref[...], n2, r2).astype(MM)
        _acc_rows(dg2_ref, jnp.sum(dx2 * n2, axis=0, keepdims=True), first)

    vec = _full_spec((1, D_MODEL))
    return pl.pallas_call(
        body, name="ffn_in_bwd", grid=(SEQ // TM,),
        out_shape=(jax.ShapeDtypeStruct((SEQ, D_MODEL), F32), jax.ShapeDtypeStruct((SEQ, D_MODEL), MM),
                   jax.ShapeDtypeStruct((1, D_MODEL), F32), jax.ShapeDtypeStruct((1, D_MODEL), F32)),
        in_specs=[_row_tile_spec(2 * D_FF), _weight_spec(w_up_g.shape), _row_tile_spec(D_MODEL),
                  _row_tile_spec(D_MODEL), _row_tile_spec(D_MODEL), vec, vec],
        out_specs=[_row_tile_spec(D_MODEL), _row_tile_spec(D_MODEL), vec, vec],
        compiler_params=_params(("arbitrary",)),
    )(dgu, w_up_g, x2, mix, dy, g3, g2)


def merge_bwd(dmix, w_out_g, gc, ga, u3, att, w_cb_g, b_cb, w_ab_g, after):
    def body(dmix_ref, wout_ref, gc_ref, ga_ref, u_ref, a_ref, wcb_ref, bcb_ref, wab_ref, after_ref,
             dco_ref, dao_ref, dg_ref, du3_ref, datt_ref, dbcb_ref):
        dm = _dot_nt(dmix_ref[...], wout_ref[...])
        co, ao = _branch_outputs(u_ref, a_ref, wcb_ref, bcb_ref, wab_ref)
        sgc = _sigmoid(gc_ref[...])
        sga = _sigmoid(ga_ref[...])
        dco = dm * sgc
        dao = dm * sga
        dg_ref[:, 0:D_MODEL] = (dm * co * (sgc * (1.0 - sgc))).astype(MM)
        dg_ref[:, D_MODEL:2 * D_MODEL] = (dm * ao * (sga * (1.0 - sga))).astype(MM)
        _acc_rows(dbcb_ref, jnp.sum(dco, axis=0, keepdims=True), pl.program_id(0) == 0)
        dco_ref[...] = dco.astype(MM)
        dao_ref[...] = dao.astype(MM)
        du3 = None
        datt = None
        for j in range(N_CHIPS):
            cols = slice(j * BR_SHARD, (j + 1) * BR_SHARD)
            t = _dot_nt(dco_ref[:, cols], wcb_ref[j])
            s = _dot_nt(dao_ref[:, cols], wab_ref[j])
            du3 = t if du3 is None else du3 + t
            datt = s if datt is None else datt + s
        du3_ref[...] = du3
        datt_ref[...] = datt.astype(MM)

    wide = _row_tile_spec(D_MODEL)
    return pl.pallas_call(
        body, name="merge_bwd", grid=(SEQ // TM,),
        out_shape=(jax.ShapeDtypeStruct((SEQ, D_MODEL), MM), jax.ShapeDtypeStruct((SEQ, D_MODEL), MM),
                   jax.ShapeDtypeStruct((SEQ, 2 * D_MODEL), MM),
                   jax.ShapeDtypeStruct((SEQ, CONV_DIM), F32), jax.ShapeDtypeStruct((SEQ, ATT_DIM), MM),
                   jax.ShapeDtypeStruct((1, D_MODEL), F32)),
        in_specs=[wide, _weight_spec(w_out_g.shape), wide, wide, _row_tile_spec(CONV_DIM), _row_tile_spec(ATT_DIM),
                  _weight_spec(w_cb_g.shape), _full_spec((1, D_MODEL)), _weight_spec(w_ab_g.shape), TOKEN_SPEC],
        out_specs=[wide, wide, _row_tile_spec(2 * D_MODEL), _row_tile_spec(CONV_DIM), _row_tile_spec(ATT_DIM),
                   _full_spec((1, D_MODEL))],
        compiler_params=_params(("arbitrary",)),
    )(dmix, w_out_g, gc, ga, u3, att, w_cb_g, b_cb, w_ab_g, after)


def conv_bwd(du3, u1, ci, w_dw, ln_g, ln_b, after):
    def body(du3_hbm, u1_hbm, ci_hbm, w_ref, g_ref, bb_ref, after_ref,
             dci_ref, dw_ref, dbdw_ref, dg_ref, db_ref, upad_ref, dpad_ref, dwacc_ref, vacc_ref,
             du3_ref, u1_ref, ci_ref, ci_sems, u1_sems, du3_sems):
        ci_chunk = _row_chunks(ci_hbm, ci_ref, ci_sems)
        u1_chunk = _row_chunks(u1_hbm, u1_ref, u1_sems)
        du3_chunk = _row_chunks(du3_hbm, du3_ref, du3_sems)
        _glu_into(ci_chunk, ci_ref, upad_ref)
        dpad_ref[SEQ:SEQ + 32, :] = jnp.zeros((32, CONV_DIM), F32)
        dwacc_ref[...] = jnp.zeros_like(dwacc_ref)
        vacc_ref[...] = jnp.zeros_like(vacc_ref)

        def fold8(t):
            s = t[0:8, :]
            for r in range(8, t.shape[0], 8):
                s = s + t[r:r + 8, :]
            return s

        def pass1(i, c):
            t0 = pl.multiple_of(i * CONV_TILE, CONV_TILE)
            gv = g_ref[...]
            for r in range(0, CONV_TILE, NORM_ROWS):
                rows = pl.ds(t0 + r, NORM_ROWS)
                xh, rstd = _layernorm_parts(u1_ref[rows, :])
                u2 = xh * gv + bb_ref[...]
                s2 = _sigmoid(u2)
                du2 = du3_ref[rows, :] * (s2 * (1.0 + u2 * (1.0 - s2)))
                wv = du2 * gv
                du1 = rstd * (wv - jnp.mean(wv, axis=-1, keepdims=True)
                              - xh * jnp.mean(wv * xh, axis=-1, keepdims=True))
                dpad_ref[rows, :] = du1
                vacc_ref[0] += fold8(du2 * xh)
                vacc_ref[1] += fold8(du2)
                vacc_ref[2] += fold8(du1)
            for cols in LANE_GROUPS:
                du1 = dpad_ref[pl.ds(t0, CONV_TILE), cols]
                for m, rows in _shifted_windows(upad_ref, t0, cols, range(2, CONV_WIDTH + 2)):
                    dwacc_ref[m - 2, :, cols] += fold8(du1 * rows)
            return c

        tiles_per_chunk = GLU_ROWS // CONV_TILE

        def pass1_chunk(ch, c):
            u1_chunk(ch).wait()
            du3_chunk(ch).wait()
            return lax.fori_loop(ch * tiles_per_chunk, (ch + 1) * tiles_per_chunk, pass1, c)

        lax.fori_loop(0, SEQ // GLU_ROWS, pass1_chunk, 0)

        def pass2(i, c):
            t0 = pl.multiple_of(i * CONV_TILE, CONV_TILE)
            tile = pl.ds(t0, CONV_TILE)
            for cols in LANE_GROUPS:
                gate_cols = slice(cols.start + CONV_DIM, cols.stop + CONV_DIM)
                du0 = _shifted_sum(dpad_ref, t0, cols, w_ref, lambda j: 30 - j)
                a = ci_ref[tile, cols]
                sb = _sigmoid(ci_ref[tile, gate_cols])
                dci_ref[tile, cols] = (du0 * sb).astype(MM)
                dci_ref[tile, gate_cols] = (du0 * a * (sb * (1.0 - sb))).astype(MM)
            return c

        lax.fori_loop(0, SEQ // CONV_TILE, pass2, 0)

        for j in range(CONV_WIDTH):
            dw_ref[j:j + 1, :] = jnp.sum(dwacc_ref[j], axis=0, keepdims=True)
        dw_ref[CONV_WIDTH:32, :] = jnp.zeros((32 - CONV_WIDTH, CONV_DIM), F32)
        dg_ref[...] = jnp.sum(vacc_ref[0], axis=0, keepdims=True)
        db_ref[...] = jnp.sum(vacc_ref[1], axis=0, keepdims=True)
        dbdw_ref[...] = jnp.sum(vacc_ref[2], axis=0, keepdims=True)

    vec = jax.ShapeDtypeStruct((1, CONV_DIM), F32)
    return pl.pallas_call(
        body, name="conv_bwd",
        out_shape=(jax.ShapeDtypeStruct((SEQ, 2 * CONV_DIM), MM), jax.ShapeDtypeStruct((32, CONV_DIM), F32),
                   vec, vec, vec),
        in_specs=[ANY] * 3 + [VMEM_SPEC] * 4, out_specs=[VMEM_SPEC] * 5,
        scratch_shapes=[pltpu.VMEM((SEQ + 32, CONV_DIM), F32), pltpu.VMEM((SEQ + 32, CONV_DIM), F32),
                        pltpu.VMEM((CONV_WIDTH, 8, CONV_DIM), F32), pltpu.VMEM((3, 8, CONV_DIM), F32)]
                       + [pltpu.VMEM(a.shape, a.dtype) for a in (du3, u1, ci)]
                       + [pltpu.SemaphoreType.DMA((SEQ // GLU_ROWS,))] * 3,
        compiler_params=_params(),
    )(du3, u1, ci, w_dw, ln_g, ln_b, after)


def attn_bwd(q, k, v, datt, rc, after):
    def body(q_hbm, k_hbm, v_hbm, do_hbm, rc_hbm, after_ref, dqkv_ref, dqa_ref, dka_ref, dva_ref, pc_ref, z_ref,
             sig1_ref, sig2_ref, g_ref, spb_ref, gb_ref, ar_ref, dzr_ref, dzc_ref, qm_ref, km_ref, dom_ref,
             q_ref, k_ref, v_ref, do_ref, rc_ref, sems):
        arrive = _fetch((q_hbm, k_hbm, do_hbm, v_hbm, rc_hbm), (q_ref, k_ref, do_ref, v_ref, rc_ref), sems)
        lane, row, _ = _head_masks()
        for ref in (dqa_ref, dka_ref, dva_ref, pc_ref):
            ref[...] = jnp.zeros_like(ref)
        z_ref[...] = jnp.full(z_ref.shape, NO_SCORE, F32)
        for ref in (sig1_ref, sig2_ref, spb_ref, ar_ref, g_ref, gb_ref, dzr_ref, dzc_ref):
            ref[...] = jnp.zeros_like(ref)
        for cp, (src_ref, split_ref) in zip(arrive, ((q_ref, qm_ref), (k_ref, km_ref), (do_ref, dom_ref))):
            cp.wait()
            _split_heads(src_ref, split_ref)
        arrive[3].wait()
        arrive[4].wait()
        w_suffix = _cumsum_weights(suffix=True, with_total=False)
        w_prefix = _cumsum_weights(suffix=False, with_total=True)

        def step(pairs):
            (ia, ja), (ib, jb), (ic, jc), (id_, jd) = pairs
            ka, qb_, kb_, kc, qd, kd = (pl.multiple_of(jnp.maximum(b, 0) * TQ, TQ) for b in (ja, ib, jb, jc, id_, jd))
            qa2, qb2, qc2, qd2, kd2 = (pl.multiple_of(jnp.maximum(b, 0) * 2 * TQ, 2 * TQ)
                                       for b in (ia, ib, ic, id_, jd))
            bias_a = _score_bias(lane, row, ia, ja)
            rc_rows = rc_ref[pl.ds(qb_, TQ), :]
            first_c = jc == 0
            for p in range(N_PAIRS):
                cols = slice(128 * p, 128 * (p + 1))
                k_a = k_ref[pl.ds(ka, TQ), cols]
                v_b = v_ref[pl.ds(kb_, TQ), cols]
                dqa_ref[pl.ds(qd, TQ), cols] += _dot(dzc_ref[p], km_ref[p, pl.ds(kd2, 2 * TQ), :])
                dka_ref[pl.ds(kd, TQ), cols] += _dot_tn(dzr_ref[p], qm_ref[p, pl.ds(qd2, 2 * TQ), :])
                dva_ref[pl.ds(kc, TQ), cols] += _dot_tn(ar_ref[p], dom_ref[p, pl.ds(qc2, 2 * TQ), :])
                for h in range(2):
                    hh = 2 * p + h
                    rows = slice(TQ * h, TQ * (h + 1))
                    r = _dot(gb_ref[hh], w_prefix)
                    p_in = jnp.where(first_c, 0.0, pc_ref[hh])
                    dz = (g_ref[hh] - sig2_ref[hh] * (r[:, :128] + p_in)).astype(MM)
                    dzc_ref[p, :, rows] = dz
                    dzr_ref[p, rows, :] = dz
                    pc_ref[hh] = p_in + r[:, 128:]
                    r_in = jnp.sum(jnp.where(lane == 16 * hh + jb, rc_rows, 0.0), axis=1, keepdims=True)
                    a = jnp.exp(z_ref[hh] - (_dot(spb_ref[hh], w_suffix) + r_in))
                    g = _dot_nt(dom_ref[p, pl.ds(qb2 + TQ * h, TQ), :], v_b) * a
                    ar_ref[p, rows, :] = a.astype(MM)
                    g_ref[hh] = g
                    gb_ref[hh] = g.astype(MM)
                    sig2_ref[hh] = sig1_ref[hh]
                    z = _dot_nt(qm_ref[p, pl.ds(qa2 + TQ * h, TQ), :], k_a) + bias_a
                    sp = _softplus(z)
                    sig1_ref[hh] = jnp.exp(z - sp)
                    z_ref[hh] = z
                    spb_ref[hh] = sp.astype(MM)

        _block_pipeline(4, False, step)
        dqkv_ref[:, 0:ATT_DIM] = (dqa_ref[...] * ATT_SCALE).astype(MM)
        dqkv_ref[:, ATT_DIM:2 * ATT_DIM] = dka_ref[...].astype(MM)
        dqkv_ref[:, 2 * ATT_DIM:3 * ATT_DIM] = dva_ref[...].astype(MM)

    split = pltpu.VMEM((N_PAIRS, 2 * SEQ, 128), MM)
    return pl.pallas_call(
        body, name="attn_bwd", out_shape=jax.ShapeDtypeStruct((SEQ, 3 * ATT_DIM), MM),
        in_specs=[ANY] * 5 + [VMEM_SPEC], out_specs=VMEM_SPEC,
        scratch_shapes=[pltpu.VMEM((SEQ, ATT_DIM), F32)] * 3 + [pltpu.VMEM((8, TQ, 128), F32)] * 5
                       + [pltpu.VMEM((8, TQ, 128), MM)] * 2
                       + [pltpu.VMEM((N_PAIRS, 2 * TQ, 128), MM)] * 2 + [pltpu.VMEM((N_PAIRS, TQ, 256), MM)]
                       + [split] * 3
                       + [pltpu.VMEM(a.shape, a.dtype) for a in (q, k, v, datt, rc)] + [pltpu.SemaphoreType.DMA((5,))],
        compiler_params=_params(),
    )(q, k, v, datt, rc, after)


DPROJ_PIECES = ((0, 1024), (1024, 2560), (2560, 4608))


def _dproj_segments(j):
    g0, g1 = j * IN_SHARD, (j + 1) * IN_SHARD
    segs = []
    for p, (s, e) in enumerate(DPROJ_PIECES):
        lo, hi = max(s, g0), min(e, g1)
        if lo < hi:
            segs.append((p, lo - s, lo - g0, hi - lo))
    return segs


def in_proj_bwd(pieces, w_in_g, x, dx2, g1, after):
    def body(p0_ref, p1_ref, p2_ref, w_ref, x_ref, dx2_ref, g_ref, after_ref, dx_ref, dg_ref):
        p_refs = (p0_ref, p1_ref, p2_ref)
        dh = None
        for j in range(N_CHIPS):
            for p, lo, off, width in _dproj_segments(j):
                t = _dot_nt(p_refs[p][:, lo:lo + width], w_ref[j, :, off:off + width])
                dh = t if dh is None else dh + t
        n1, r1 = _rms(x_ref[...])
        dx_ref[...] = dx2_ref[...] + _rms_bwd(dh * g_ref[...], n1, r1)
        _acc_rows(dg_ref, jnp.sum(dh * n1, axis=0, keepdims=True), pl.program_id(0) == 0)

    vec = _full_spec((1, D_MODEL))
    return pl.pallas_call(
        body, name="in_proj_bwd", grid=(SEQ // TM,),
        out_shape=[jax.ShapeDtypeStruct((SEQ, D_MODEL), F32), jax.ShapeDtypeStruct((1, D_MODEL), F32)],
        in_specs=[_row_tile_spec(p.shape[1]) for p in pieces]
                 + [_weight_spec(w_in_g.shape), _row_tile_spec(D_MODEL), _row_tile_spec(D_MODEL), vec, TOKEN_SPEC],
        out_specs=[_row_tile_spec(D_MODEL), vec],
        compiler_params=_params(("arbitrary",)),
    )(*pieces, w_in_g, x, dx2, g1, after)


def weight_grad_in(h1, pieces):
    kh = D_MODEL // 2

    def body(a_ref, p0_ref, p1_ref, p2_ref, o_ref):
        p_refs = (p0_ref, p1_ref, p2_ref)
        a = a_ref[...]
        for j in range(N_CHIPS):
            @pl.when(pl.program_id(1) == j)
            def _():
                for p, lo, off, width in _dproj_segments(j):
                    o_ref[0, 0, :, off:off + width] = _dot_tn(a, p_refs[p][:, lo:lo + width]).astype(MM)

    return pl.pallas_call(
        body, name="dw_in", grid=(2, N_CHIPS), out_shape=jax.ShapeDtypeStruct((N_CHIPS, 2, kh, IN_SHARD), MM),
        in_specs=[pl.BlockSpec((SEQ, kh), lambda h, j: (0, h))] + [_weight_spec(p.shape) for p in pieces],
        out_specs=pl.BlockSpec((1, 1, kh, IN_SHARD), lambda h, j: (j, h, 0, 0)),
        compiler_params=_params(("arbitrary", "arbitrary")),
    )(h1, *pieces)


def weight_grad(a, b, name, col_sharded, tk=None):
    kin, n = a.shape[1], b.shape[1]

    def body(a_ref, b_ref, o_ref):
        if col_sharded:
            o_ref[0, 0] = _dot_tn(a_ref[...], b_ref[...]).astype(MM)
        else:
            o_ref[...] = _dot_tn(a_ref[...], b_ref[...]).astype(MM)

    if col_sharded:
        kh, ns = kin // 2, n // N_CHIPS
        out = jax.ShapeDtypeStruct((N_CHIPS, 2, kh, ns), MM)
        grid = (2, N_CHIPS)
        in_specs = [pl.BlockSpec((SEQ, kh), lambda h, j: (0, h)), pl.BlockSpec((SEQ, ns), lambda h, j: (0, j))]
        out_spec = pl.BlockSpec((1, 1, kh, ns), lambda h, j: (j, h, 0, 0))
        sem = ("arbitrary", "arbitrary")
    else:
        out = jax.ShapeDtypeStruct((kin, n), MM)
        grid = (kin // tk,)
        in_specs = [pl.BlockSpec((SEQ, tk), lambda r: (0, r)), pl.BlockSpec((SEQ, n), lambda r: (0, 0))]
        out_spec = pl.BlockSpec((tk, n), lambda r: (r, 0))
        sem = ("arbitrary",)
    res = pl.pallas_call(
        body, name=name, grid=grid, out_shape=out, in_specs=in_specs, out_specs=out_spec,
        compiler_params=_params(sem),
    )(a, b)
    if not col_sharded:
        res = res.reshape(N_CHIPS, 2, kin // (2 * N_CHIPS), n)
    return res


def weight_grad_mix(merged, dmix, u3, dco, att, dao):
    operands = (merged, dmix, u3, dco, att, dao)
    n_out, n_br = D_MODEL // 2, CONV_DIM // 2

    def body(*refs):
        hbm, (o_out, o_cb, o_ab), bufs, sems = refs[:6], refs[6:9], refs[9:15], refs[15]
        copies = [pltpu.make_async_copy(hbm[i], bufs[i], sems.at[i]) for i in range(6)]
        for cp in copies:
            cp.start()
        m_ref, dm_ref, u_ref, dco_ref, a_ref, dao_ref = bufs
        copies[0].wait()
        copies[1].wait()
        for h in range(2):
            o_out[h * n_out:(h + 1) * n_out, :] = _dot_tn(m_ref[:, h * n_out:(h + 1) * n_out], dm_ref[...]).astype(MM)
        for br, (a, d, o) in enumerate(((u_ref, dco_ref, o_cb), (a_ref, dao_ref, o_ab))):
            copies[2 + 2 * br].wait()
            copies[3 + 2 * br].wait()
            for h in range(2):
                g = _dot_tn(a[:, h * n_br:(h + 1) * n_br], d[...])
                for j in range(N_CHIPS):
                    o[j, h] = g[:, j * BR_SHARD:(j + 1) * BR_SHARD].astype(MM)

    branch = jax.ShapeDtypeStruct((N_CHIPS, 2, n_br, BR_SHARD), MM)
    dw_out, dw_cb, dw_ab = pl.pallas_call(
        body, name="dw_mix", out_shape=[jax.ShapeDtypeStruct((D_MODEL, D_MODEL), MM), branch, branch],
        in_specs=[ANY] * 6, out_specs=[VMEM_SPEC] * 3,
        scratch_shapes=[pltpu.VMEM(a.shape, a.dtype) for a in operands] + [pltpu.SemaphoreType.DMA((6,))],
        compiler_params=_params(),
    )(*operands)
    return dw_out.reshape(N_CHIPS, 2, D_MODEL // (2 * N_CHIPS), D_MODEL), dw_cb, dw_ab


def _place():
    x, y, c = lax.axis_index("x"), lax.axis_index("y"), lax.axis_index("c")
    chips = [(1 - x, y), (x, 1 - y), (1 - x, 1 - y)]
    return x, y, c, chips


def _rcopy(src, dst, send_sem, recv_sem, dev):
    return pltpu.make_async_remote_copy(src_ref=src, dst_ref=dst, send_sem=send_sem, recv_sem=recv_sem,
                                        device_id=dev, device_id_type=MESH)


class _Gather:
    N_MOVES = 6

    def __init__(self, shapes, w, o, scratch):
        self.n, self.shapes, self.w, self.o = len(w), shapes, w, o
        self.send, self.recv, self.psend, self.precv, self.loc_in, self.loc_out = scratch[:6]
        self.raw, self.stage = scratch[6:6 + self.n], scratch[6 + self.n:]
        x, y, c, self.chips = _place()
        self.c = c
        self.me, k_x, k_y, k_far = 2 * x + y, 2 * (1 - x) + y, 2 * x + (1 - y), 2 * (1 - x) + (1 - y)
        to_x, to_y = (1 - x, y, c), (x, 1 - y, c)
        self.sib = (x, y, 1 - c)
        self.sent_as = [(self.me, 0, to_x), (self.me, 1, to_y), (self.me, 1, to_x), (self.me, 0, to_y),
                        (k_x, 0, to_y), (k_y, 1, to_x)]
        self.arrives_as = [(k_x, 0, to_x), (k_y, 1, to_y), (k_x, 1, to_x), (k_y, 0, to_y),
                           (k_far, 0, to_y), (k_far, 1, to_x)]
        self.sent_on_after = {0: 4, 1: 5}

    @staticmethod
    def scratch(shards):
        n = len(shards)
        sems = pltpu.SemaphoreType.DMA
        m = _Gather.N_MOVES * n
        return ([sems((m,)), sems((m,)), sems((m,)), sems((m,)), sems((3 * n,)), sems((n,))]
                + [pltpu.VMEM(s.shape, s.dtype) for s in shards] + [pltpu.VMEM(s.shape, MM) for s in shards])

    @staticmethod
    def out_shapes(shards):
        return [jax.ShapeDtypeStruct((N_CHIPS,) + s.shape, MM) for s in shards]

    def _rows(self, t, quarter, cc):
        rq = self.shapes[t][0] // 4
        return pl.ds(pl.multiple_of((2 * cc + quarter) * rq, rq), rq)

    def _own_rows(self, t, piece):
        if piece < 2:
            return self._rows(t, piece, self.c)
        rh = self.shapes[t][0] // 2
        return pl.ds(pl.multiple_of((1 - self.c) * rh, rh), rh)

    def _chip(self, j):
        cx, cy = self.chips[j]
        return 2 * cx + cy, (cx, cy, self.c)

    def local_in(self, t, piece):
        rows = self._own_rows(t, piece)
        return pltpu.make_async_copy(self.w[t].at[rows, :], self.raw[t].at[rows, :], self.loc_in.at[3 * t + piece])

    def local_out(self, t):
        return pltpu.make_async_copy(self.stage[t], self.o[t].at[self.me], self.loc_out.at[t])

    def sent(self, i, t):
        k, quarter, dev = self.sent_as[i]
        rows = self._rows(t, quarter, self.c)
        there = self.o[t].at[k, rows, :]
        return _rcopy(self.stage[t].at[rows, :] if i < 4 else there, there,
                      self.send.at[i * self.n + t], self.recv.at[i * self.n + t], dev)

    def arrived(self, i, t):
        k, quarter, dev = self.arrives_as[i]
        blk = self.o[t].at[k, self._rows(t, quarter, self.c), :]
        return _rcopy(blk, blk, self.send.at[i * self.n + t], self.recv.at[i * self.n + t], dev)

    def passed(self, i, t, cc):
        k, quarter, _ = self.arrives_as[i]
        blk = self.o[t].at[k, self._rows(t, quarter, cc), :]
        return _rcopy(blk, blk, self.psend.at[i * self.n + t], self.precv.at[i * self.n + t], self.sib)

    def start(self):
        for piece in range(3):
            for t in range(self.n):
                self.local_in(t, piece).start()
        for piece, moves in enumerate(((0, 3), (1, 2), ())):
            for t in range(self.n):
                rows = self._own_rows(t, piece)
                self.local_in(t, piece).wait()
                self.stage[t][rows, :] = self.raw[t][rows, :].astype(MM)
                for i in moves:
                    self.sent(i, t).start()
        for t in range(self.n):
            self.local_out(t).start()

    def forward(self):
        for i in range(self.N_MOVES):
            for t in range(self.n):
                self.arrived(i, t).wait_recv()
                if i in self.sent_on_after:
                    self.sent(self.sent_on_after[i], t).start()
                self.passed(i, t, self.c).start()

    def finish(self):
        for i in range(self.N_MOVES):
            for t in range(self.n):
                self.passed(i, t, 1 - self.c).wait_recv()
        for i in range(self.N_MOVES):
            for t in range(self.n):
                self.sent(i, t).wait_send()
                self.passed(i, t, self.c).wait_send()
        for t in range(self.n):
            self.local_out(t).wait()


def all_gather_weights(shards, small, later):
    n, m = len(shards), len(later)
    shapes = [s.shape for s in shards]

    def body(*refs):
        w = refs[:n]
        sm = refs[n]
        lw = refs[n + 1:n + 1 + m]
        o = refs[n + 1 + m:2 * n + 1 + m]
        osm = refs[2 * n + 1 + m]
        lo = refs[2 * n + 2 + m:2 * n + 2 + 2 * m]
        scratch = refs[2 * n + 2 + 2 * m:]
        ssend, srecv, sloc, lsem_in, lsem_out = scratch[:5]
        lraw, lstage = scratch[5:5 + m], scratch[5 + m:5 + 2 * m]
        g = _Gather(shapes, w, o, scratch[5 + 2 * m:])
        own = pltpu.make_async_copy(sm, osm.at[g.me], sloc)
        own.start()
        g.start()
        loads = [pltpu.make_async_copy(lw[t], lraw[t], lsem_in.at[t]) for t in range(m)]
        for cp in loads:
            cp.start()
        small_cps = [_rcopy(sm, osm.at[g.me], ssend.at[j], srecv.at[j], g._chip(j)[1]) for j in range(3)]
        for cp in small_cps:
            cp.start()
        places = []
        for t in range(m):
            loads[t].wait()
            lstage[t][...] = lraw[t][...].astype(MM)
            places.append(pltpu.make_async_copy(lstage[t], lo[t].at[g.me], lsem_out.at[t]))
            places[t].start()
        g.forward()
        g.finish()
        for j in range(3):
            k, dev = g._chip(j)
            _rcopy(sm, osm.at[k], ssend.at[j], srecv.at[j], dev).wait_recv()
            small_cps[j].wait_send()
        own.wait()
        for cp in places:
            cp.wait()

    out_shape = _Gather.out_shapes(shards)
    out_shape.append(jax.ShapeDtypeStruct((N_CHIPS,) + small.shape, small.dtype))
    out_shape += _Gather.out_shapes(later)
    sems = pltpu.SemaphoreType.DMA
    return pl.pallas_call(
        body, name="all_gather_weights", out_shape=out_shape,
        in_specs=[ANY] * (n + 1 + m), out_specs=[ANY] * (n + 1 + m),
        scratch_shapes=[sems((3,)), sems((3,)), sems, sems((m,)), sems((m,))]
                       + [pltpu.VMEM(s.shape, s.dtype) for s in later] + [pltpu.VMEM(s.shape, MM) for s in later]
                       + _Gather.scratch(shards),
        compiler_params=_params(),
    )(*shards, small, *later)


HBM_SPEC = pl.BlockSpec(memory_space=pltpu.HBM)
SEM_SPEC = pl.BlockSpec(memory_space=pltpu.SEMAPHORE)
DATAFLOW = pltpu.SideEffectType.DATAFLOW_SIDE_EFFECTING


def split_start(name, bufs, n_copies, copies):
    nb = len(bufs)

    def body(*refs):
        for cp in copies(refs[:nb], refs[nb], refs[nb + 1]):
            cp.start()
        token = refs[2 * nb + 2]
        token[...] = jnp.zeros_like(token)

    sems = [pltpu.SemaphoreType.DMA((n_copies,))] * 2
    res = pl.pallas_call(
        body, name=name,
        out_shape=sems + [pltpu.HBM(a.shape, a.dtype) for a in bufs] + [jax.ShapeDtypeStruct((8, 128), F32)],
        in_specs=[HBM_SPEC] * nb, out_specs=[SEM_SPEC] * 2 + [HBM_SPEC] * nb + [VMEM_SPEC],
        input_output_aliases={i: 2 + i for i in range(nb)},
        compiler_params=pltpu.CompilerParams(has_side_effects=DATAFLOW),
    )(*[pltpu.with_memory_space_constraint(a, pltpu.HBM) for a in bufs])
    return res[:-1], res[-1]


def split_wait(name, state, after, copies):
    sems, bufs = state[:2], state[2:]
    nb = len(bufs)

    def body(*refs):
        for cp in copies(refs[:nb], refs[nb], refs[nb + 1]):
            cp.wait_send()
            cp.wait_recv()

    return pl.pallas_call(
        body, name=name, out_shape=[pltpu.HBM(a.shape, a.dtype) for a in bufs],
        in_specs=[HBM_SPEC] * nb + [SEM_SPEC] * 2 + [ANY] * len(after), out_specs=[HBM_SPEC] * nb,
        input_output_aliases={i: i for i in range(nb)},
        compiler_params=pltpu.CompilerParams(has_side_effects=DATAFLOW),
    )(*bufs, *sems, *after)


class _Shifted:
    def __init__(self, sems, first):
        self.sems, self.first = sems, first

    @property
    def at(self):
        return self

    def __getitem__(self, i):
        return self.sems.at[self.first + i]


def _scatter_copies(n):
    def copies(refs, send, recv):
        _, _, c, chips = _place()
        return [_rcopy(refs[t].at[2 * cx + cy], refs[n + t].at[j], send.at[3 * t + j], recv.at[3 * t + j], (cx, cy, c))
                for t in range(n) for j, (cx, cy) in enumerate(chips)]
    return copies


def scatter_start(parts):
    lands = [lax.empty((3,) + p.shape[1:], p.dtype) for p in parts]
    return split_start("scatter_start_rest", list(parts) + lands, 3 * len(parts), _scatter_copies(len(parts)))


def scatter_wait(state, after):
    n = (len(state) - 2) // 2
    return split_wait("scatter_wait_rest", state, after, _scatter_copies(n))[n:]


def _gather_copies(shapes, level, to_both_cores=()):
    n = len(shapes)

    def copies(refs, send, recv):
        x, y, c, chips = _place()
        out = []
        for t in list(range(n)) + list(to_both_cores):
            rh = shapes[t][0] // 2
            to_core = c if len(out) < 3 * n else 1 - c
            for cx, cy in chips:
                k, dev = (2 * x + y, (cx, cy, to_core)) if level == 1 else (2 * cx + cy, (x, y, 1 - c))
                blk = refs[t].at[k, pl.ds(c * rh, rh), :]
                out.append(_rcopy(blk, blk, send.at[len(out)], recv.at[len(out)], dev))
        return out
    return copies


def _sibling_copies(n, other_half):
    def copies(refs, send, recv):
        x, y, c, _ = _place()
        return [_rcopy(refs[t].at[:, 1 - c] if other_half else refs[t], refs[n + t], send.at[t], recv.at[t],
                       (x, y, 1 - c)) for t in range(n)]
    return copies


def sibling_start(srcs, other_half, tag):
    lands = [lax.empty((a.shape[0],) + a.shape[2:] if other_half else a.shape, a.dtype) for a in srcs]
    return split_start("sibling_start_" + tag, list(srcs) + lands, len(srcs),
                       _sibling_copies(len(srcs), other_half))


def sibling_wait(state, after, other_half, tag):
    n = (len(state) - 2) // 2
    res = split_wait("sibling_wait_" + tag, state, after, _sibling_copies(n, other_half))
    return res[:n], res[n:]


def small_pack(ddw, v512, v1024, loss_parts):
    rows, width = PACK_ROWS, 512
    n512, n1024 = len(VEC512), len(VEC1024)

    def body(*refs):
        ddw_ref = refs[0]
        a_refs = refs[1:1 + n512]
        b_refs = refs[1 + n512:1 + n512 + n1024]
        lp_ref, o_ref, p_ref = refs[1 + n512 + n1024:]
        p_ref[...] = jnp.zeros_like(p_ref)
        p_ref[0:32, :] = ddw_ref[...]
        p_ref[LOSS_ROW:LOSS_ROW + 1, 0:128] = jnp.sum(lp_ref[...], axis=0, keepdims=True) * 0.125
        for i, r in enumerate(a_refs):
            p_ref[32 + i:33 + i, :] = r[...]
        for i, r in enumerate(b_refs):
            base = 32 + n512 + 2 * i
            p_ref[base:base + 1, :] = r[:, 0:512]
            p_ref[base + 1:base + 2, :] = r[:, 512:1024]
        x, y, c, _ = _place()
        o_ref[4 * x + 2 * y + c] = p_ref[...]

    n_in = 2 + n512 + n1024
    return pl.pallas_call(
        body, name="small_pack", out_shape=jax.ShapeDtypeStruct((8, rows, width), F32),
        in_specs=[VMEM_SPEC] * n_in, out_specs=VMEM_SPEC,
        scratch_shapes=[pltpu.VMEM((rows, width), F32)],
    )(ddw, *[v512[n] for n in VEC512], *[v1024[n] for n in VEC1024], loss_parts)


def _small_copies(refs, send, recv):
    x, y, c, _ = _place()
    mine = refs[0].at[4 * x + 2 * y + c]
    peers = [(1 - x if k & 4 else x, 1 - y if k & 2 else y, 1 - c if k & 1 else c) for k in range(1, 8)]
    return [_rcopy(mine, mine, send.at[i], recv.at[i], dev) for i, dev in enumerate(peers)]


def _row_block(r):
    for tr in (512, 352, 256, 128):
        if r % tr == 0:
            return tr
    return r


def add_halves(g, recv, name):
    _, _, r, w = g.shape
    tr = _row_block(r)

    def body(g_ref, r_ref, ob_ref, own_ref):
        k = pl.program_id(1)
        me = 2 * lax.axis_index("x") + lax.axis_index("y")
        t = g_ref[0, 0].astype(F32) + r_ref[0].astype(F32)
        ob_ref[0] = t.astype(MM)
        mine = jnp.where(k == me, t, 0.0)

        @pl.when(k == 0)
        def _():
            own_ref[...] = mine

        @pl.when(k != 0)
        def _():
            own_ref[...] += mine

    return pl.pallas_call(
        body, name=name, grid=(r // tr, N_CHIPS),
        in_specs=[pl.BlockSpec((1, 1, tr, w), lambda i, k: (k, lax.axis_index("c"), i, 0)),
                  pl.BlockSpec((1, tr, w), lambda i, k: (k, i, 0))],
        out_specs=[pl.BlockSpec((1, tr, w), lambda i, k: (k, i, 0)),
                   pl.BlockSpec((tr, w), lambda i, k: (i, 0))],
        out_shape=(jax.ShapeDtypeStruct((N_CHIPS, r, w), MM), jax.ShapeDtypeStruct((r, w), F32)),
        compiler_params=_params(("arbitrary", "arbitrary")),
    )(g, recv)


def sum_parts(own, rin, after, name):
    _, r, w = rin.shape
    tr = _row_block(r)

    def body(o_ref, r_ref, after_ref, out_ref):
        out_ref[...] = ((o_ref[...] + r_ref[0].astype(F32)) + r_ref[1].astype(F32)) + r_ref[2].astype(F32)

    return pl.pallas_call(
        body, name=name, grid=(r // tr,), out_shape=jax.ShapeDtypeStruct((r, w), F32),
        in_specs=[pl.BlockSpec((tr, w), lambda i: (i, 0)), pl.BlockSpec((3, tr, w), lambda i: (0, i, 0)),
                  TOKEN_SPEC],
        out_specs=pl.BlockSpec((tr, w), lambda i: (i, 0)),
        compiler_params=_params(("arbitrary",)),
    )(own, rin, after)


def _adamw_math(w, g, m, v):
    mn = ADAM_B1 * m + (1.0 - ADAM_B1) * g
    vn = ADAM_B2 * v + (1.0 - ADAM_B2) * (g * g)
    m_hat = mn / (1.0 - ADAM_B1 ** ADAM_STEP)
    v_hat = vn / (1.0 - ADAM_B2 ** ADAM_STEP)
    return -ADAM_LR * (m_hat / (jnp.sqrt(v_hat) + ADAM_EPS) + ADAM_WD * w), mn, vn


def adamw(w, mine, other, m, v, name):
    r, c = w.shape
    rh = r // 2
    tr = _row_block(rh)
    if c >= 1024 and tr % 512 == 0:
        tr = 256
    nb = rh // tr

    def body(w_ref, a_ref, b_ref, m_ref, v_ref, go_ref, d_ref, mo_ref, vo_ref):
        gv = jnp.where(lax.axis_index("c") == pl.program_id(0), a_ref[...], b_ref[...])
        go_ref[...] = gv
        d_ref[...], mo_ref[...], vo_ref[...] = _adamw_math(w_ref[...], gv, m_ref[...], v_ref[...])

    def half(of_sibling):
        def index(h, i):
            owner = lax.axis_index("c")
            owner = 1 - owner if of_sibling else owner
            return jnp.where(h == owner, i, jnp.where(h < owner, 0, nb - 1)), 0
        return pl.BlockSpec((tr, c), index)

    spec = pl.BlockSpec((tr, c), lambda h, i: (h * nb + i, 0))
    out = jax.ShapeDtypeStruct((r, c), F32)
    return pl.pallas_call(
        body, name=name, grid=(2, nb), out_shape=(out, out, out, out),
        in_specs=[spec, half(False), half(True), spec, spec], out_specs=[spec] * 4,
        compiler_params=_params(("arbitrary", "arbitrary")),
    )(w, mine, other, m, v)


def adamw_small(packs, params, after):
    names = list(params)
    flat = [a for n in names for a in params[n]]

    def body(*refs):
        p_ref = refs[0]
        ins = refs[1:1 + 3 * len(names)]
        loss_ref, g_ref = refs[2 + 3 * len(names):4 + 3 * len(names)]
        outs = refs[4 + 3 * len(names):]
        total = p_ref[0]
        for d in range(1, 8):
            total = total + p_ref[d]
        g_ref[...] = total
        loss_ref[...] = g_ref[LOSS_ROW:LOSS_ROW + 1, 0:1]
        me = 2 * lax.axis_index("x") + lax.axis_index("y")
        for i, n in enumerate(names):
            w_ref, m_ref, v_ref = ins[3 * i:3 * i + 3]
            go_ref, d_ref, mo_ref, vo_ref = outs[4 * i:4 * i + 4]
            if n == "conv_dw_w":
                gv = jnp.zeros((CONV_WIDTH, 128), F32)
                for k in range(N_CHIPS):
                    gv = gv + jnp.where(me == k, g_ref[0:CONV_WIDTH, 128 * k:128 * (k + 1)], 0.0)
            elif n in VEC512:
                r0 = 32 + VEC512.index(n)
                gv = g_ref[r0:r0 + 1, :]
            else:
                r0 = 32 + len(VEC512) + 2 * VEC1024.index(n)
                gv = jnp.concatenate([g_ref[r0:r0 + 1, :], g_ref[r0 + 1:r0 + 2, :]], axis=1)
            go_ref[...] = gv
            d_ref[...], mo_ref[...], vo_ref[...] = _adamw_math(w_ref[...], gv, m_ref[...], v_ref[...])

    out_shape = [jax.ShapeDtypeStruct((1, 1), F32), jax.ShapeDtypeStruct(packs.shape[1:], F32)]
    out_shape += [jax.ShapeDtypeStruct(params[n][0].shape, F32) for n in names for _ in range(4)]
    res = pl.pallas_call(
        body, name="adamw_small", out_shape=out_shape,
        in_specs=[VMEM_SPEC] * (2 + len(flat)), out_specs=[VMEM_SPEC] * len(out_shape),
        compiler_params=_params(),
    )(packs, *flat, after)
    return res[0], res[1], {n: res[2 + 4 * i:6 + 4 * i] for i, n in enumerate(names)}


REST = ("w_ffn_up", "w_ffn_down", "w_out", "w_conv_branch", "w_att_branch")
VEC512 = ("conv_dw_b", "conv_ln_g", "conv_ln_b")
VEC1024 = ("norm_mix_pre", "b_conv_branch", "norm_mix_post", "norm_ffn_pre", "norm_ffn_post")
PACK_ROWS = 48
LOSS_ROW = 47


def kernel(x, norm_mix_pre, w_in, conv_dw_w, conv_dw_b, conv_ln_g, conv_ln_b, w_conv_branch, b_conv_branch, w_att_branch, w_out, norm_mix_post, norm_ffn_pre, w_ffn_up, w_ffn_down, norm_ffn_post, loss_target, m_norm_mix_pre, m_w_in, m_conv_dw_w, m_conv_dw_b, m_conv_ln_g, m_conv_ln_b, m_w_conv_branch, m_b_conv_branch, m_w_att_branch, m_w_out, m_norm_mix_post, m_norm_ffn_pre, m_w_ffn_up, m_w_ffn_down, m_norm_ffn_post, v_norm_mix_pre, v_w_in, v_conv_dw_w, v_conv_dw_b, v_conv_ln_g, v_conv_ln_b, v_w_conv_branch, v_b_conv_branch, v_w_att_branch, v_w_out, v_norm_mix_post, v_norm_ffn_pre, v_w_ffn_up, v_w_ffn_down, v_norm_ffn_post):
    weights = dict(norm_mix_pre=norm_mix_pre, w_in=w_in, conv_dw_w=conv_dw_w, conv_dw_b=conv_dw_b, conv_ln_g=conv_ln_g, conv_ln_b=conv_ln_b, w_conv_branch=w_conv_branch, b_conv_branch=b_conv_branch, w_att_branch=w_att_branch, w_out=w_out, norm_mix_post=norm_mix_post, norm_ffn_pre=norm_ffn_pre, w_ffn_up=w_ffn_up, w_ffn_down=w_ffn_down, norm_ffn_post=norm_ffn_post)
    mom = dict(norm_mix_pre=m_norm_mix_pre, w_in=m_w_in, conv_dw_w=m_conv_dw_w, conv_dw_b=m_conv_dw_b, conv_ln_g=m_conv_ln_g, conv_ln_b=m_conv_ln_b, w_conv_branch=m_w_conv_branch, b_conv_branch=m_b_conv_branch, w_att_branch=m_w_att_branch, w_out=m_w_out, norm_mix_post=m_norm_mix_post, norm_ffn_pre=m_norm_ffn_pre, w_ffn_up=m_w_ffn_up, w_ffn_down=m_w_ffn_down, norm_ffn_post=m_norm_ffn_post)
    var = dict(norm_mix_pre=v_norm_mix_pre, w_in=v_w_in, conv_dw_w=v_conv_dw_w, conv_dw_b=v_conv_dw_b, conv_ln_g=v_conv_ln_g, conv_ln_b=v_conv_ln_b, w_conv_branch=v_w_conv_branch, b_conv_branch=v_b_conv_branch, w_att_branch=v_w_att_branch, w_out=v_w_out, norm_mix_post=v_norm_mix_post, norm_ffn_pre=v_norm_ffn_pre, w_ffn_up=v_w_ffn_up, w_ffn_down=v_w_ffn_down, norm_ffn_post=v_norm_ffn_post)
    order = list(weights)
    grads, deltas, new_m, new_v = {}, {}, {}, {}
    xs = x.reshape(SEQ, D_MODEL)
    tgt = loss_target.reshape(SEQ, D_MODEL)
    row = lambda a: a.reshape(1, -1)
    g1, g2, g3, g4 = (row(weights[n]) for n in ("norm_mix_pre", "norm_mix_post", "norm_ffn_pre", "norm_ffn_post"))
    ln_g, ln_b = row(conv_ln_g), row(conv_ln_b)

    summed, from_chips = {}, {}

    def core_sums(names, state, after, tag):
        own, from_sibling = sibling_wait(state, after, True, tag)
        for n, g, r in zip(names, own, from_sibling):
            summed[n] = add_halves(g, r, "add_" + n)

    def chip_sums(names, after):
        return [sum_parts(summed[n][1], from_chips[n], after, "sum_" + n) for n in names]

    def optimize(names, state, after, tag):
        mine, other = sibling_wait(state, after, False, tag)
        for n, a, b in zip(names, mine, other):
            grads[n], deltas[n], new_m[n], new_v[n] = adamw(weights[n], a, b, mom[n], var[n], "adamw_" + n)

    w_in_g, dw_g, *rest = all_gather_weights([w_in], conv_dw_w, [weights[n] for n in REST])
    w_dw_full = jnp.concatenate([dw_g[k] for k in range(N_CHIPS)], axis=1)
    rest_shapes = [weights[n].shape for n in REST]
    over_ici = _gather_copies(rest_shapes, 1, to_both_cores=(2, 3, 4))
    state, token = split_start("gather_start", rest, 3 * (len(REST) + 3), over_ici)
    h1, ci, q, k, v, gc, ga = in_proj_fwd(xs, g1, w_in_g, token)
    u1, u3 = conv_fwd(ci, w_dw_full, row(conv_dw_b), ln_g, ln_b)
    att, rc = attn_fwd(q, k, v)
    rest = split_wait("gather_wait", state, [att], over_ici)
    w_out_g, w_cb_g, w_ab_g = rest[2:]
    w_out_g = w_out_g.reshape(D_MODEL, D_MODEL)
    to_sibling = _gather_copies(rest_shapes[:2], 2)
    state, token = split_start("pass_start", rest[:2], 3 * 2, to_sibling)
    merged, mix, x2, h2 = mix_fwd(u3, att, gc, ga, xs, w_cb_g, row(b_conv_branch), w_ab_g, w_out_g, g2, g3, token)
    w_up_g, w_down_g = split_wait("pass_wait", state, [h2], to_sibling)
    w_down_g = w_down_g.reshape(D_FF, D_MODEL)
    gate, up, act = ffn_up_fwd(h2, w_up_g)
    dff, dy, loss_parts, dg4 = ffn_down_loss(act, w_down_g, x2, tgt, g4)

    dgu = ffn_act_bwd(dff, w_down_g, gate, up)
    dx2, dmix, dg3, dg2 = ffn_in_bwd(dgu, w_up_g, x2, mix, dy, g3, g2)
    ffn_grads = [weight_grad(h2, dgu, "dw_ffn_up", True), weight_grad(act, dff, "dw_ffn_down", False, tk=UP_SHARD)]
    to_ffn, token = sibling_start(ffn_grads, True, "dw_ffn")
    dco, dao, dg, du3, datt, dbcb = merge_bwd(dmix, w_out_g, gc, ga, u3, att, w_cb_g, row(b_conv_branch), w_ab_g,
                                              token)
    to_mix, token = sibling_start(weight_grad_mix(merged, dmix, u3, dco, att, dao), True, "dw_mix")
    core_sums(REST[:2], to_ffn, [token], "dw_ffn")
    core_sums(REST[2:], to_mix, [summed["w_ffn_down"][1]], "dw_mix")
    state, token = scatter_start([summed[n][0] for n in REST])
    dci, ddw, dbdw, dlng, dlnb = conv_bwd(du3, u1, ci, w_dw_full, ln_g, ln_b, token)
    dqkv = attn_bwd(q, k, v, datt, rc, token)
    from_chips.update(zip(REST, scatter_wait(state, [dci, dqkv])))
    dproj = (dci, dqkv, dg)
    to_in, token = sibling_start([weight_grad_in(h1, dproj)], True, "dw_in")
    grad_x, dg1 = in_proj_bwd(dproj, w_in_g, xs, dx2, g1, token)
    v512 = dict(conv_dw_b=dbdw, conv_ln_g=dlng, conv_ln_b=dlnb)
    v1024 = dict(norm_mix_pre=dg1, b_conv_branch=dbcb, norm_mix_post=dg2, norm_ffn_pre=dg3, norm_ffn_post=dg4)
    packs = small_pack(ddw, v512, v1024, loss_parts)
    core_sums(("w_in",), to_in, [packs], "dw_in")
    to_chips = summed["w_in"][0]
    landing = lax.empty((3,) + to_chips.shape[1:], to_chips.dtype)

    def scatter_and_packs(refs, send, recv):
        return (_scatter_copies(1)(refs[:2], send, recv)
                + _small_copies(refs[2:], _Shifted(send, 3), _Shifted(recv, 3)))

    state, token = split_start("scatter_start_w_in", [to_chips, landing, packs], 3 + 7, scatter_and_packs)
    swap_up, token = sibling_start(chip_sums(REST[:1], token), False, "sum_ffn_up")
    swap_rest, token = sibling_start(chip_sums(REST[1:], token), False, "sum_rest")
    optimize(REST[:1], swap_up, [token], "sum_ffn_up")
    optimize(REST[1:], swap_rest, [new_v["w_ffn_up"]], "sum_rest")
    _, from_chips["w_in"], packs = split_wait("scatter_wait_w_in", state, [new_v[n] for n in REST], scatter_and_packs)
    swap_in, token = sibling_start(chip_sums(("w_in",), token), False, "sum_w_in")
    as_rows = lambda n, a: a if n == "conv_dw_w" else a.reshape(1, -1)
    small_names = ("conv_dw_w",) + VEC512 + VEC1024
    loss, gsum, small = adamw_small(
        packs, {n: tuple(as_rows(n, d[n]) for d in (weights, mom, var)) for n in small_names}, token)
    optimize(("w_in",), swap_in, [gsum], "sum_w_in")
    for n in small_names:
        grads[n], deltas[n], new_m[n], new_v[n] = (a.reshape(weights[n].shape) for a in small[n])

    return (loss.reshape(()), grad_x.reshape(1, SEQ, D_MODEL),*[grads[n] for n in order], *[deltas[n] for n in order],
            *[new_m[n] for n in order], *[new_v[n] for n in order])
```

```python
import jax
import jax.numpy as jnp
from jax import lax
from jax.experimental import pallas as pl
from jax.experimental.pallas import tpu as pltpu

F32 = jnp.float32
MM = jnp.bfloat16

SEQ = 2048
D_MODEL = 1024
CONV_DIM = 512
ATT_DIM = 512
CONV_WIDTH = 31
D_FF = 2816
IN_COLS = 2 * CONV_DIM + 3 * ATT_DIM + 2 * D_MODEL
N_CHIPS = 4
IN_SHARD = IN_COLS // N_CHIPS
UP_SHARD = 2 * D_FF // N_CHIPS
BR_SHARD = D_MODEL // N_CHIPS
EPS = 1e-6
ATT_SCALE = 0.125

TM = 256
GLU_ROWS = 256
TQ = 128
CONV_TILE = 64
CONV_WIN = CONV_TILE + 32
VMEM_LIMIT = 56 * 1024 * 1024

ADAM_LR = 0.001
ADAM_B1 = 0.9
ADAM_B2 = 0.999
ADAM_EPS = 1e-08
ADAM_WD = 0.01
ADAM_STEP = 10

MESH = pl.DeviceIdType.MESH
ANY = pl.BlockSpec(memory_space=pl.ANY)
VMEM_SPEC = pl.BlockSpec(memory_space=pltpu.VMEM)

NT_DIMS = (((1,), (1,)), ((), ()))
TN_DIMS = (((0,), (0,)), ((), ()))

IN_PIECES = (("ci", 0, 1024), ("q", 1024, 1536), ("k", 1536, 2048), ("v", 2048, 2560),
             ("gc", 2560, 3584), ("ga", 3584, 4608))


def _params(sem=None, vmem=VMEM_LIMIT):
    return pltpu.CompilerParams(dimension_semantics=sem, vmem_limit_bytes=vmem)


def _dot(a, b):
    return jnp.dot(a, b, preferred_element_type=F32)


def _dot_nt(a, b):
    return lax.dot_general(a, b, NT_DIMS, preferred_element_type=F32)


def _dot_tn(a, b):
    return lax.dot_general(a, b, TN_DIMS, preferred_element_type=F32)


def _sigmoid(x):
    return 1.0 / (1.0 + jnp.exp(-x))


def _rms(x):
    r = lax.rsqrt(jnp.mean(x * x, axis=-1, keepdims=True) + EPS)
    return x * r, r


def _rms_bwd(dy_g, n, r):
    return r * (dy_g - n * jnp.mean(dy_g * n, axis=-1, keepdims=True))


def _row_tile_spec(width, tm=TM):
    return pl.BlockSpec((tm, width), lambda i: (i, 0))


def _full_spec(shape):
    nd = len(shape)
    return pl.BlockSpec(shape, lambda *_: (0,) * nd)


def _weight_spec(shape):
    nd = len(shape)
    return pl.BlockSpec(shape, lambda *_: (0,) * nd, pipeline_mode=pl.Buffered(1))


def _acc_rows(ref, val, first):
    @pl.when(first)
    def _():
        ref[...] = val

    @pl.when(jnp.logical_not(first))
    def _():
        ref[...] += val


TOKEN_SPEC = pl.BlockSpec((8, 128), lambda *_: (0, 0))


def in_proj_fwd(x, g1, w_in_g, after):
    def body(x_ref, g_ref, w_ref, after_ref, h_ref, ci_ref, q_ref, k_ref, v_ref, gc_ref, ga_ref):
        n, _ = _rms(x_ref[...])
        h = (n * g_ref[...]).astype(MM)
        h_ref[...] = h
        outs = dict(ci=ci_ref, q=q_ref, k=k_ref, v=v_ref, gc=gc_ref, ga=ga_ref)
        for j in range(N_CHIPS):
            p = _dot(h, w_ref[j])
            g0 = j * IN_SHARD
            for name, s, e in IN_PIECES:
                lo, hi = max(s, g0), min(e, g0 + IN_SHARD)
                if lo < hi:
                    ref = outs[name]
                    part = p[:, lo - g0:hi - g0]
                    if name == "q":
                        part = part * ATT_SCALE
                    ref[:, lo - s:hi - s] = part.astype(ref.dtype)

    out_shape = [
        jax.ShapeDtypeStruct((SEQ, D_MODEL), MM),
        jax.ShapeDtypeStruct((SEQ, 2 * CONV_DIM), F32),
        jax.ShapeDtypeStruct((SEQ, ATT_DIM), MM),
        jax.ShapeDtypeStruct((SEQ, ATT_DIM), MM),
        jax.ShapeDtypeStruct((SEQ, ATT_DIM), MM),
        jax.ShapeDtypeStruct((SEQ, D_MODEL), F32),
        jax.ShapeDtypeStruct((SEQ, D_MODEL), F32),
    ]
    return pl.pallas_call(
        body, name="in_proj_fwd", grid=(SEQ // TM,), out_shape=out_shape,
        in_specs=[_row_tile_spec(D_MODEL), _full_spec((1, D_MODEL)), _weight_spec(w_in_g.shape), TOKEN_SPEC],
        out_specs=[_row_tile_spec(s.shape[1]) for s in out_shape],
        compiler_params=_params(("arbitrary",)),
    )(x, g1, w_in_g, after)


LANE_GROUPS = [slice(g, g + 128) for g in range(0, CONV_DIM, 128)]
NORM_ROWS = 16


def _shifted_windows(src_ref, t0, cols, offsets):
    win = src_ref[pl.ds(t0, CONV_WIN), cols]
    for rot in range(8):
        ms = [m for m in offsets if m % 8 == rot]
        if ms:
            shifted = win if rot == 0 else pltpu.roll(win, CONV_WIN - rot, 0)
            for m in ms:
                yield m, shifted[m - rot:m - rot + CONV_TILE, :]


def _shifted_sum(src_ref, t0, cols, w_ref, offset_of_tap):
    tap_at = {offset_of_tap(j): j for j in range(CONV_WIDTH)}
    acc = None
    for m, rows in _shifted_windows(src_ref, t0, cols, sorted(tap_at)):
        t = w_ref[tap_at[m]:tap_at[m] + 1, cols] * rows
        acc = t if acc is None else acc + t
    return acc


def _fetch(srcs, dsts, sems):
    copies = [pltpu.make_async_copy(s, d, sems.at[i]) for i, (s, d) in enumerate(zip(srcs, dsts))]
    for cp in copies:
        cp.start()
    return copies


def _row_chunks(src, dst, sems):
    def chunk(i):
        t0 = i * GLU_ROWS
        rows = pl.ds(t0 if isinstance(i, int) else pl.multiple_of(t0, GLU_ROWS), GLU_ROWS)
        return pltpu.make_async_copy(src.at[rows, :], dst.at[rows, :], sems.at[i])

    for i in range(SEQ // GLU_ROWS):
        chunk(i).start()
    return chunk


def _glu_into(ci_chunk, ci_ref, upad_ref):
    upad_ref[0:32, :] = jnp.zeros((32, CONV_DIM), F32)

    def step(i, c):
        ci_chunk(i).wait()
        t0 = pl.multiple_of(i * GLU_ROWS, GLU_ROWS)
        a = ci_ref[pl.ds(t0, GLU_ROWS), 0:CONV_DIM]
        b = ci_ref[pl.ds(t0, GLU_ROWS), CONV_DIM:2 * CONV_DIM]
        upad_ref[pl.ds(t0 + 32, GLU_ROWS), :] = a * _sigmoid(b)
        return c

    lax.fori_loop(0, SEQ // GLU_ROWS, step, 0)


def _layernorm_parts(u1):
    mu = jnp.mean(u1, axis=-1, keepdims=True)
    xc = u1 - mu
    rstd = lax.rsqrt(jnp.mean(xc * xc, axis=-1, keepdims=True) + EPS)
    return xc * rstd, rstd


def conv_fwd(ci, w_dw, b_dw, ln_g, ln_b):
    def body(ci_hbm, w_ref, b_ref, g_ref, bb_ref, u1_ref, u3_ref, upad_ref, ci_ref, sems):
        _glu_into(_row_chunks(ci_hbm, ci_ref, sems), ci_ref, upad_ref)

        def step(i, c):
            t0 = pl.multiple_of(i * CONV_TILE, CONV_TILE)
            for cols in LANE_GROUPS:
                u1_ref[pl.ds(t0, CONV_TILE), cols] = (_shifted_sum(upad_ref, t0, cols, w_ref, lambda j: j + 2)
                                                      + b_ref[:, cols])
            for r in range(0, CONV_TILE, NORM_ROWS):
                rows = pl.ds(t0 + r, NORM_ROWS)
                xh, _ = _layernorm_parts(u1_ref[rows, :])
                u2 = xh * g_ref[...] + bb_ref[...]
                u3_ref[rows, :] = (u2 * _sigmoid(u2)).astype(MM)
            return c

        lax.fori_loop(0, SEQ // CONV_TILE, step, 0)

    return pl.pallas_call(
        body, name="conv_fwd",
        out_shape=[jax.ShapeDtypeStruct((SEQ, CONV_DIM), F32), jax.ShapeDtypeStruct((SEQ, CONV_DIM), MM)],
        in_specs=[ANY] + [VMEM_SPEC] * 4, out_specs=[VMEM_SPEC] * 2,
        scratch_shapes=[pltpu.VMEM((SEQ + 32, CONV_DIM), F32), pltpu.VMEM(ci.shape, ci.dtype),
                        pltpu.SemaphoreType.DMA((SEQ // GLU_ROWS,))],
        compiler_params=_params(),
    )(ci, w_dw, b_dw, ln_g, ln_b)


def _softplus(z):
    return jnp.maximum(z, 0.0) + jnp.log(1.0 + jnp.exp(-jnp.abs(z)))


def _cumsum_weights(suffix, with_total):
    n = 256 if with_total else 128
    r = lax.broadcasted_iota(jnp.int32, (128, n), 0)
    c = lax.broadcasted_iota(jnp.int32, (128, n), 1)
    tri = (r >= c) if suffix else (r <= c)
    return jnp.logical_or(tri, c >= 128).astype(MM)


NO_SCORE = -1e30
N_KB = SEQ // TQ


def _score_bias(lane, row, i, j):
    keep = jnp.logical_and(i >= 0, jnp.logical_or(j < i, lane < row))
    return jnp.where(keep, 0.0, NO_SCORE)


def _block_pipeline(n_stages, descending, step, on_query_block=None):
    n_lag = n_stages - 1
    none = jnp.int32(-1)

    def shift(cur, lag):
        step([cur] + [(lag[2 * s], lag[2 * s + 1]) for s in range(n_lag)])
        return (cur[0], cur[1]) + tuple(lag[:-2])

    def outer(i, lag):
        if on_query_block is not None:
            on_query_block(i)

        def inner(n, lag):
            return shift((i, i - n if descending else n), lag)
        return lax.fori_loop(0, i + 1, inner, lag)

    lag = lax.fori_loop(0, N_KB, outer, (none,) * (2 * n_lag))
    lax.fori_loop(0, n_lag, lambda n, lag: shift((none, none), lag), lag)


def _head_masks():
    lane = lax.broadcasted_iota(jnp.int32, (TQ, 128), 1)
    row = lax.broadcasted_iota(jnp.int32, (TQ, 128), 0)
    return lane, row, lane < 64


def _pick_head(x, head0, h):
    zero = jnp.zeros_like(x)
    return jnp.where(head0, x, zero) if h == 0 else jnp.where(head0, zero, x)


N_PAIRS = ATT_DIM // 128


def _split_heads(src_ref, dst_ref):
    _, _, head0 = _head_masks()

    def block(b, c):
        r0 = pl.multiple_of(b * TQ, TQ)
        d0 = pl.multiple_of(b * 2 * TQ, 2 * TQ)
        for p in range(N_PAIRS):
            x = src_ref[pl.ds(r0, TQ), 128 * p:128 * (p + 1)]
            for h in range(2):
                dst_ref[p, pl.ds(d0 + TQ * h, TQ), :] = _pick_head(x, head0, h)
        return c

    lax.fori_loop(0, N_KB, block, 0)


def attn_fwd(q, k, v):
    def body(q_hbm, k_hbm, v_hbm, o_ref, rc_ref, acc_ref, r_ref, z_ref, spb_ref, ab_ref, qm_ref, vm_ref,
             q_ref, k_ref, v_ref, sems):
        arrive = _fetch((q_hbm, v_hbm, k_hbm), (q_ref, v_ref, k_ref), sems)
        lane, row, _ = _head_masks()
        w = _cumsum_weights(suffix=True, with_total=True)
        acc_ref[...] = jnp.zeros_like(acc_ref)
        r_ref[...] = jnp.zeros_like(r_ref)
        rc_ref[...] = jnp.zeros_like(rc_ref)
        z_ref[...] = jnp.full(z_ref.shape, NO_SCORE, F32)
        spb_ref[...] = jnp.zeros_like(spb_ref)
        ab_ref[...] = jnp.zeros_like(ab_ref)
        arrive[0].wait()
        _split_heads(q_ref, qm_ref)
        arrive[1].wait()
        _split_heads(v_ref, vm_ref)
        arrive[2].wait()

        def step(pairs):
            (i1, j1), (i2, j2), (i3, j3) = pairs
            k1, q2, q3 = (pl.multiple_of(jnp.maximum(b, 0) * TQ, TQ) for b in (j1, i2, i3))
            q1, k3 = (pl.multiple_of(jnp.maximum(b, 0) * 2 * TQ, 2 * TQ) for b in (i1, j3))
            bias1 = _score_bias(lane, row, i1, j1)
            first2 = j2 == i2
            rc_rows = rc_ref[pl.ds(q2, TQ), :]
            for p in range(N_PAIRS):
                cols = slice(128 * p, 128 * (p + 1))
                kb = k_ref[pl.ds(k1, TQ), cols]
                acc_ref[pl.ds(q3, TQ), cols] += _dot(ab_ref[p], vm_ref[p, pl.ds(k3, 2 * TQ), :])
                for h in range(2):
                    hh = 2 * p + h
                    r = _dot(spb_ref[hh], w)
                    r_in = jnp.where(first2, 0.0, r_ref[hh])
                    ab_ref[p, :, 128 * h:128 * (h + 1)] = jnp.exp(z_ref[hh] - (r[:, :128] + r_in)).astype(MM)
                    rc_rows = jnp.where(jnp.logical_and(lane == 16 * hh + j2, i2 >= 0), r_in, rc_rows)
                    r_ref[hh] = r_in + r[:, 128:]
                    z = _dot_nt(qm_ref[p, pl.ds(q1 + TQ * h, TQ), :], kb) + bias1
                    z_ref[hh] = z
                    spb_ref[hh] = _softplus(z).astype(MM)
            rc_ref[pl.ds(q2, TQ), :] = rc_rows

        _block_pipeline(3, True, step)
        o_ref[...] = acc_ref[...].astype(MM)

    return pl.pallas_call(
        body, name="attn_fwd",
        out_shape=[jax.ShapeDtypeStruct((SEQ, ATT_DIM), MM), jax.ShapeDtypeStruct((SEQ, 128), F32)],
        in_specs=[ANY] * 3, out_specs=[VMEM_SPEC] * 2,
        scratch_shapes=[pltpu.VMEM((SEQ, ATT_DIM), F32), pltpu.VMEM((8, TQ, 128), F32),
                        pltpu.VMEM((8, TQ, 128), F32), pltpu.VMEM((8, TQ, 128), MM),
                        pltpu.VMEM((N_PAIRS, TQ, 256), MM), pltpu.VMEM((N_PAIRS, 2 * SEQ, 128), MM),
                        pltpu.VMEM((N_PAIRS, 2 * SEQ, 128), MM)]
                       + [pltpu.VMEM(a.shape, a.dtype) for a in (q, k, v)] + [pltpu.SemaphoreType.DMA((3,))],
        compiler_params=_params(),
    )(q, k, v)


def _branch_outputs(u_ref, a_ref, wcb_ref, bcb_ref, wab_ref):
    u = u_ref[...]
    a = a_ref[...]
    co = jnp.concatenate([_dot(u, wcb_ref[j]) for j in range(N_CHIPS)], axis=1) + bcb_ref[...]
    ao = jnp.concatenate([_dot(a, wab_ref[j]) for j in range(N_CHIPS)], axis=1)
    return co, ao


def mix_fwd(u3, att, gc, ga, x, w_cb_g, b_cb, w_ab_g, w_out_g, g2, g3, after):
    def body(u_ref, a_ref, gc_ref, ga_ref, x_ref, wcb_ref, bcb_ref, wab_ref, wout_ref, g2_ref, g3_ref, after_ref,
             mg_ref, mix_ref, x2_ref, h2_ref):
        co, ao = _branch_outputs(u_ref, a_ref, wcb_ref, bcb_ref, wab_ref)
        merged = (_sigmoid(gc_ref[...]) * co + _sigmoid(ga_ref[...]) * ao).astype(MM)
        mg_ref[...] = merged
        mix = _dot(merged, wout_ref[...])
        mix_ref[...] = mix
        n2, _ = _rms(mix)
        x2 = x_ref[...] + n2 * g2_ref[...]
        x2_ref[...] = x2
        n3, _ = _rms(x2)
        h2_ref[...] = (n3 * g3_ref[...]).astype(MM)

    out_shape = [
        jax.ShapeDtypeStruct((SEQ, D_MODEL), MM), jax.ShapeDtypeStruct((SEQ, D_MODEL), F32),
        jax.ShapeDtypeStruct((SEQ, D_MODEL), F32), jax.ShapeDtypeStruct((SEQ, D_MODEL), MM),
    ]
    vec = _full_spec((1, D_MODEL))
    return pl.pallas_call(
        body, name="mix_fwd", grid=(SEQ // TM,), out_shape=out_shape,
        in_specs=[_row_tile_spec(CONV_DIM), _row_tile_spec(ATT_DIM), _row_tile_spec(D_MODEL),
                  _row_tile_spec(D_MODEL), _row_tile_spec(D_MODEL), _weight_spec(w_cb_g.shape), vec,
                  _weight_spec(w_ab_g.shape), _weight_spec(w_out_g.shape), vec, vec, TOKEN_SPEC],
        out_specs=[_row_tile_spec(D_MODEL)] * 4,
        compiler_params=_params(("arbitrary",)),
    )(u3, att, gc, ga, x, w_cb_g, b_cb, w_ab_g, w_out_g, g2, g3, after)


def ffn_up_fwd(h2, w_up_g):
    def body(h_ref, wg_ref, wu_ref, gate_ref, up_ref, act_ref):
        h = h_ref[...]
        gate = _dot(h, wg_ref[0])
        up = _dot(h, wu_ref[0])
        gate_ref[...] = gate.astype(MM)
        up_ref[...] = up.astype(MM)
        act_ref[...] = (gate * _sigmoid(gate) * up).astype(MM)

    tile = pl.BlockSpec((TM, UP_SHARD), lambda n, i: (i, n))
    act = jax.ShapeDtypeStruct((SEQ, D_FF), MM)
    return pl.pallas_call(
        body, name="ffn_up_fwd", grid=(2, SEQ // TM), out_shape=[act, act, act],
        in_specs=[pl.BlockSpec((TM, D_MODEL), lambda n, i: (i, 0)),
                  pl.BlockSpec((1, D_MODEL, UP_SHARD), lambda n, i: (n, 0, 0)),
                  pl.BlockSpec((1, D_MODEL, UP_SHARD), lambda n, i: (n + 2, 0, 0))],
        out_specs=[tile, tile, tile],
        compiler_params=_params(("arbitrary", "arbitrary")),
    )(h2, w_up_g, w_up_g)


def ffn_down_loss(act, w_down_g, x2, target, g4):
    def body(act_ref, wd_ref, x2_ref, t_ref, g_ref, dff_ref, dy_ref, loss_ref, dg_ref):
        ff = _dot(act_ref[...], wd_ref[...])
        n4, r4 = _rms(ff)
        g4v = g_ref[...]
        err = x2_ref[...] + n4 * g4v - t_ref[...]
        row_loss = jnp.mean(err * err, axis=-1, keepdims=True)
        loss_ref[...] = jnp.zeros((8, 128), F32) + 0.5 * jnp.sum(row_loss, axis=0, keepdims=True)
        dy = err * (1.0 / D_MODEL)
        dy_ref[...] = dy
        dff_ref[...] = _rms_bwd(dy * g4v, n4, r4).astype(MM)
        _acc_rows(dg_ref, jnp.sum(dy * n4, axis=0, keepdims=True), pl.program_id(0) == 0)

    nt = SEQ // TM
    vec = _full_spec((1, D_MODEL))
    return pl.pallas_call(
        body, name="ffn_down_loss", grid=(nt,),
        out_shape=(jax.ShapeDtypeStruct((SEQ, D_MODEL), MM), jax.ShapeDtypeStruct((SEQ, D_MODEL), F32),
                   jax.ShapeDtypeStruct((nt * 8, 128), F32), jax.ShapeDtypeStruct((1, D_MODEL), F32)),
        in_specs=[_row_tile_spec(D_FF), _weight_spec(w_down_g.shape), _row_tile_spec(D_MODEL),
                  _row_tile_spec(D_MODEL), vec],
        out_specs=[_row_tile_spec(D_MODEL), _row_tile_spec(D_MODEL),
                   pl.BlockSpec((8, 128), lambda i: (i, 0)), vec],
        compiler_params=_params(("arbitrary",)),
    )(act, w_down_g, x2, target, g4)


def ffn_act_bwd(dff, w_down_g, gate, up):
    def body(dff_ref, wd_ref, gate_ref, up_ref, dgu_ref):
        dact = _dot_nt(dff_ref[...], wd_ref[...])
        gate = gate_ref[...].astype(F32)
        sg = _sigmoid(gate)
        dgu_ref[:, 0:D_FF] = (dact * up_ref[...].astype(F32) * (sg * (1.0 + gate * (1.0 - sg)))).astype(MM)
        dgu_ref[:, D_FF:2 * D_FF] = (dact * (gate * sg)).astype(MM)

    return pl.pallas_call(
        body, name="ffn_act_bwd", grid=(SEQ // TM,),
        out_shape=jax.ShapeDtypeStruct((SEQ, 2 * D_FF), MM),
        in_specs=[_row_tile_spec(D_MODEL), _weight_spec(w_down_g.shape), _row_tile_spec(D_FF), _row_tile_spec(D_FF)],
        out_specs=_row_tile_spec(2 * D_FF),
        compiler_params=_params(("arbitrary",)),
    )(dff, w_down_g, gate, up)


def ffn_in_bwd(dgu, w_up_g, x2, mix, dy, g3, g2):
    def body(dgu_ref, w_ref, x2_ref, mix_ref, dy_ref, g3_ref, g2_ref, dx2_ref, dmix_ref, dg3_ref, dg2_ref):
        dh2 = None
        for j in range(N_CHIPS):
            t = _dot_nt(dgu_ref[:, j * UP_SHARD:(j + 1) * UP_SHARD], w_ref[j])
            dh2 = t if dh2 is None else dh2 + t
        first = pl.program_id(0) == 0
        n3, r3 = _rms(x2_ref[...])
        dx2 = dy_ref[...] + _rms_bwd(dh2 * g3_ref[...], n3, r3)
        dx2_ref[...] = dx2
        _acc_rows(dg3_ref, jnp.sum(dh2 * n3, axis=0, keepdims=True), first)
        n2, r2 = _rms(mix_ref[...])
        dmix_ref[...] = _rms_bwd(dx2 * g2_ref[...], n2, r2).astype(MM)
        _acc_rows(dg2_ref, jnp.sum(dx2 * n2, axis=0, keepdims=True), first)

    vec = _full_spec((1, D_MODEL))
    return pl.pallas_call(
        body, name="ffn_in_bwd", grid=(SEQ // TM,),
        out_shape=(jax.ShapeDtypeStruct((SEQ, D_MODEL), F32), jax.ShapeDtypeStruct((SEQ, D_MODEL), MM),
                   jax.ShapeDtypeStruct((1, D_MODEL), F32), jax.ShapeDtypeStruct((1, D_MODEL), F32)),
        in_specs=[_row_tile_spec(2 * D_FF), _weight_spec(w_up_g.shape), _row_tile_spec(D_MODEL),
                  _row_tile_spec(D_MODEL), _row_tile_spec(D_MODEL), vec, vec],
        out_specs=[_row_tile_spec(D_MODEL), _row_tile_spec(D_MODEL), vec, vec],
        compiler_params=_params(("arbitrary",)),
    )(dgu, w_up_g, x2, mix, dy, g3, g2)


def merge_bwd(dmix, w_out_g, gc, ga, u3, att, w_cb_g, b_cb, w_ab_g, after):
    def body(dmix_ref, wout_ref, gc_ref, ga_ref, u_ref, a_ref, wcb_ref, bcb_ref, wab_ref, after_ref,
             dco_ref, dao_ref, dg_ref, du3_ref, datt_ref, dbcb_ref):
        dm = _dot_nt(dmix_ref[...], wout_ref[...])
        co, ao = _branch_outputs(u_ref, a_ref, wcb_ref, bcb_ref, wab_ref)
        sgc = _sigmoid(gc_ref[...])
        sga = _sigmoid(ga_ref[...])
        dco = dm * sgc
        dao = dm * sga
        dg_ref[:, 0:D_MODEL] = (dm * co * (sgc * (1.0 - sgc))).astype(MM)
        dg_ref[:, D_MODEL:2 * D_MODEL] = (dm * ao * (sga * (1.0 - sga))).astype(MM)
        _acc_rows(dbcb_ref, jnp.sum(dco, axis=0, keepdims=True), pl.program_id(0) == 0)
        dco_ref[...] = dco.astype(MM)
        dao_ref[...] = dao.astype(MM)
        du3 = None
        datt = None
        for j in range(N_CHIPS):
            cols = slice(j * BR_SHARD, (j + 1) * BR_SHARD)
            t = _dot_nt(dco_ref[:, cols], wcb_ref[j])
            s = _dot_nt(dao_ref[:, cols], wab_ref[j])
            du3 = t if du3 is None else du3 + t
            datt = s if datt is None else datt + s
        du3_ref[...] = du3
        datt_ref[...] = datt.astype(MM)

    wide = _row_tile_spec(D_MODEL)
    return pl.pallas_call(
        body, name="merge_bwd", grid=(SEQ // TM,),
        out_shape=(jax.ShapeDtypeStruct((SEQ, D_MODEL), MM), jax.ShapeDtypeStruct((SEQ, D_MODEL), MM),
                   jax.ShapeDtypeStruct((SEQ, 2 * D_MODEL), MM),
                   jax.ShapeDtypeStruct((SEQ, CONV_DIM), F32), jax.ShapeDtypeStruct((SEQ, ATT_DIM), MM),
                   jax.ShapeDtypeStruct((1, D_MODEL), F32)),
        in_specs=[wide, _weight_spec(w_out_g.shape), wide, wide, _row_tile_spec(CONV_DIM), _row_tile_spec(ATT_DIM),
                  _weight_spec(w_cb_g.shape), _full_spec((1, D_MODEL)), _weight_spec(w_ab_g.shape), TOKEN_SPEC],
        out_specs=[wide, wide, _row_tile_spec(2 * D_MODEL), _row_tile_spec(CONV_DIM), _row_tile_spec(ATT_DIM),
                   _full_spec((1, D_MODEL))],
        compiler_params=_params(("arbitrary",)),
    )(dmix, w_out_g, gc, ga, u3, att, w_cb_g, b_cb, w_ab_g, after)


def conv_bwd(du3, u1, ci, w_dw, ln_g, ln_b, after):
    def body(du3_hbm, u1_hbm, ci_hbm, w_ref, g_ref, bb_ref, after_ref,
             dci_ref, dw_ref, dbdw_ref, dg_ref, db_ref, upad_ref, dpad_ref, dwacc_ref, vacc_ref,
             du3_ref, u1_ref, ci_ref, ci_sems, u1_sems, du3_sems):
        ci_chunk = _row_chunks(ci_hbm, ci_ref, ci_sems)
        u1_chunk = _row_chunks(u1_hbm, u1_ref, u1_sems)
        du3_chunk = _row_chunks(du3_hbm, du3_ref, du3_sems)
        _glu_into(ci_chunk, ci_ref, upad_ref)
        dpad_ref[SEQ:SEQ + 32, :] = jnp.zeros((32, CONV_DIM), F32)
        dwacc_ref[...] = jnp.zeros_like(dwacc_ref)
        vacc_ref[...] = jnp.zeros_like(vacc_ref)

        def fold8(t):
            s = t[0:8, :]
            for r in range(8, t.shape[0], 8):
                s = s + t[r:r + 8, :]
            return s

        def pass1(i, c):
            t0 = pl.multiple_of(i * CONV_TILE, CONV_TILE)
            gv = g_ref[...]
            for r in range(0, CONV_TILE, NORM_ROWS):
                rows = pl.ds(t0 + r, NORM_ROWS)
                xh, rstd = _layernorm_parts(u1_ref[rows, :])
                u2 = xh * gv + bb_ref[...]
                s2 = _sigmoid(u2)
                du2 = du3_ref[rows, :] * (s2 * (1.0 + u2 * (1.0 - s2)))
                wv = du2 * gv
                du1 = rstd * (wv - jnp.mean(wv, axis=-1, keepdims=True)
                              - xh * jnp.mean(wv * xh, axis=-1, keepdims=True))
                dpad_ref[rows, :] = du1
                vacc_ref[0] += fold8(du2 * xh)
                vacc_ref[1] += fold8(du2)
                vacc_ref[2] += fold8(du1)
            for cols in LANE_GROUPS:
                du1 = dpad_ref[pl.ds(t0, CONV_TILE), cols]
                for m, rows in _shifted_windows(upad_ref, t0, cols, range(2, CONV_WIDTH + 2)):
                    dwacc_ref[m - 2, :, cols] += fold8(du1 * rows)
            return c

        tiles_per_chunk = GLU_ROWS // CONV_TILE

        def pass1_chunk(ch, c):
            u1_chunk(ch).wait()
            du3_chunk(ch).wait()
            return lax.fori_loop(ch * tiles_per_chunk, (ch + 1) * tiles_per_chunk, pass1, c)

        lax.fori_loop(0, SEQ // GLU_ROWS, pass1_chunk, 0)

        def pass2(i, c):
            t0 = pl.multiple_of(i * CONV_TILE, CONV_TILE)
            tile = pl.ds(t0, CONV_TILE)
            for cols in LANE_GROUPS:
                gate_cols = slice(cols.start + CONV_DIM, cols.stop + CONV_DIM)
                du0 = _shifted_sum(dpad_ref, t0, cols, w_ref, lambda j: 30 - j)
                a = ci_ref[tile, cols]
                sb = _sigmoid(ci_ref[tile, gate_cols])
                dci_ref[tile, cols] = (du0 * sb).astype(MM)
                dci_ref[tile, gate_cols] = (du0 * a * (sb * (1.0 - sb))).astype(MM)
            return c

        lax.fori_loop(0, SEQ // CONV_TILE, pass2, 0)

        for j in range(CONV_WIDTH):
            dw_ref[j:j + 1, :] = jnp.sum(dwacc_ref[j], axis=0, keepdims=True)
        dw_ref[CONV_WIDTH:32, :] = jnp.zeros((32 - CONV_WIDTH, CONV_DIM), F32)
        dg_ref[...] = jnp.sum(vacc_ref[0], axis=0, keepdims=True)
        db_ref[...] = jnp.sum(vacc_ref[1], axis=0, keepdims=True)
        dbdw_ref[...] = jnp.sum(vacc_ref[2], axis=0, keepdims=True)

    vec = jax.ShapeDtypeStruct((1, CONV_DIM), F32)
    return pl.pallas_call(
        body, name="conv_bwd",
        out_shape=(jax.ShapeDtypeStruct((SEQ, 2 * CONV_DIM), MM), jax.ShapeDtypeStruct((32, CONV_DIM), F32),
                   vec, vec, vec),
        in_specs=[ANY] * 3 + [VMEM_SPEC] * 4, out_specs=[VMEM_SPEC] * 5,
        scratch_shapes=[pltpu.VMEM((SEQ + 32, CONV_DIM), F32), pltpu.VMEM((SEQ + 32, CONV_DIM), F32),
                        pltpu.VMEM((CONV_WIDTH, 8, CONV_DIM), F32), pltpu.VMEM((3, 8, CONV_DIM), F32)]
                       + [pltpu.VMEM(a.shape, a.dtype) for a in (du3, u1, ci)]
                       + [pltpu.SemaphoreType.DMA((SEQ // GLU_ROWS,))] * 3,
        compiler_params=_params(),
    )(du3, u1, ci, w_dw, ln_g, ln_b, after)


def attn_bwd(q, k, v, datt, rc, after):
    def body(q_hbm, k_hbm, v_hbm, do_hbm, rc_hbm, after_ref, dqkv_ref, dqa_ref, dka_ref, dva_ref, pc_ref, z_ref,
             sig1_ref, sig2_ref, g_ref, spb_ref, gb_ref, ar_ref, dzr_ref, dzc_ref, qm_ref, km_ref, dom_ref,
             q_ref, k_ref, v_ref, do_ref, rc_ref, sems):
        arrive = _fetch((q_hbm, k_hbm, do_hbm, v_hbm, rc_hbm), (q_ref, k_ref, do_ref, v_ref, rc_ref), sems)
        lane, row, _ = _head_masks()
        for ref in (dqa_ref, dka_ref, dva_ref, pc_ref):
            ref[...] = jnp.zeros_like(ref)
        z_ref[...] = jnp.full(z_ref.shape, NO_SCORE, F32)
        for ref in (sig1_ref, sig2_ref, spb_ref, ar_ref, g_ref, gb_ref, dzr_ref, dzc_ref):
            ref[...] = jnp.zeros_like(ref)
        for cp, (src_ref, split_ref) in zip(arrive, ((q_ref, qm_ref), (k_ref, km_ref), (do_ref, dom_ref))):
            cp.wait()
            _split_heads(src_ref, split_ref)
        arrive[3].wait()
        arrive[4].wait()
        w_suffix = _cumsum_weights(suffix=True, with_total=False)
        w_prefix = _cumsum_weights(suffix=False, with_total=True)

        def step(pairs):
            (ia, ja), (ib, jb), (ic, jc), (id_, jd) = pairs
            ka, qb_, kb_, kc, qd, kd = (pl.multiple_of(jnp.maximum(b, 0) * TQ, TQ) for b in (ja, ib, jb, jc, id_, jd))
            qa2, qb2, qc2, qd2, kd2 = (pl.multiple_of(jnp.maximum(b, 0) * 2 * TQ, 2 * TQ)
                                       for b in (ia, ib, ic, id_, jd))
            bias_a = _score_bias(lane, row, ia, ja)
            rc_rows = rc_ref[pl.ds(qb_, TQ), :]
            first_c = jc == 0
            for p in range(N_PAIRS):
                cols = slice(128 * p, 128 * (p + 1))
                k_a = k_ref[pl.ds(ka, TQ), cols]
                v_b = v_ref[pl.ds(kb_, TQ), cols]
                dqa_ref[pl.ds(qd, TQ), cols] += _dot(dzc_ref[p], km_ref[p, pl.ds(kd2, 2 * TQ), :])
                dka_ref[pl.ds(kd, TQ), cols] += _dot_tn(dzr_ref[p], qm_ref[p, pl.ds(qd2, 2 * TQ), :])
                dva_ref[pl.ds(kc, TQ), cols] += _dot_tn(ar_ref[p], dom_ref[p, pl.ds(qc2, 2 * TQ), :])
                for h in range(2):
                    hh = 2 * p + h
                    rows = slice(TQ * h, TQ * (h + 1))
                    r = _dot(gb_ref[hh], w_prefix)
                    p_in = jnp.where(first_c, 0.0, pc_ref[hh])
                    dz = (g_ref[hh] - sig2_ref[hh] * (r[:, :128] + p_in)).astype(MM)
                    dzc_ref[p, :, rows] = dz
                    dzr_ref[p, rows, :] = dz
                    pc_ref[hh] = p_in + r[:, 128:]
                    r_in = jnp.sum(jnp.where(lane == 16 * hh + jb, rc_rows, 0.0), axis=1, keepdims=True)
                    a = jnp.exp(z_ref[hh] - (_dot(spb_ref[hh], w_suffix) + r_in))
                    g = _dot_nt(dom_ref[p, pl.ds(qb2 + TQ * h, TQ), :], v_b) * a
                    ar_ref[p, rows, :] = a.astype(MM)
                    g_ref[hh] = g
                    gb_ref[hh] = g.astype(MM)
                    sig2_ref[hh] = sig1_ref[hh]
                    z = _dot_nt(qm_ref[p, pl.ds(qa2 + TQ * h, TQ), :], k_a) + bias_a
                    sp = _softplus(z)
                    sig1_ref[hh] = jnp.exp(z - sp)
                    z_ref[hh] = z
                    spb_ref[hh] = sp.astype(MM)

        _block_pipeline(4, False, step)
        dqkv_ref[:, 0:ATT_DIM] = (dqa_ref[...] * ATT_SCALE).astype(MM)
        dqkv_ref[:, ATT_DIM:2 * ATT_DIM] = dka_ref[...].astype(MM)
        dqkv_ref[:, 2 * ATT_DIM:3 * ATT_DIM] = dva_ref[...].astype(MM)

    split = pltpu.VMEM((N_PAIRS, 2 * SEQ, 128), MM)
    return pl.pallas_call(
        body, name="attn_bwd", out_shape=jax.ShapeDtypeStruct((SEQ, 3 * ATT_DIM), MM),
        in_specs=[ANY] * 5 + [VMEM_SPEC], out_specs=VMEM_SPEC,
        scratch_shapes=[pltpu.VMEM((SEQ, ATT_DIM), F32)] * 3 + [pltpu.VMEM((8, TQ, 128), F32)] * 5
                       + [pltpu.VMEM((8, TQ, 128), MM)] * 2
                       + [pltpu.VMEM((N_PAIRS, 2 * TQ, 128), MM)] * 2 + [pltpu.VMEM((N_PAIRS, TQ, 256), MM)]
                       + [split] * 3
                       + [pltpu.VMEM(a.shape, a.dtype) for a in (q, k, v, datt, rc)] + [pltpu.SemaphoreType.DMA((5,))],
        compiler_params=_params(),
    )(q, k, v, datt, rc, after)


DPROJ_PIECES = ((0, 1024), (1024, 2560), (2560, 4608))


def _dproj_segments(j):
    g0, g1 = j * IN_SHARD, (j + 1) * IN_SHARD
    segs = []
    for p, (s, e) in enumerate(DPROJ_PIECES):
        lo, hi = max(s, g0), min(e, g1)
        if lo < hi:
            segs.append((p, lo - s, lo - g0, hi - lo))
    return segs


def in_proj_bwd(pieces, w_in_g, x, dx2, g1, after):
    def body(p0_ref, p1_ref, p2_ref, w_ref, x_ref, dx2_ref, g_ref, after_ref, dx_ref, dg_ref):
        p_refs = (p0_ref, p1_ref, p2_ref)
        dh = None
        for j in range(N_CHIPS):
            for p, lo, off, width in _dproj_segments(j):
                t = _dot_nt(p_refs[p][:, lo:lo + width], w_ref[j, :, off:off + width])
                dh = t if dh is None else dh + t
        n1, r1 = _rms(x_ref[...])
        dx_ref[...] = dx2_ref[...] + _rms_bwd(dh * g_ref[...], n1, r1)
        _acc_rows(dg_ref, jnp.sum(dh * n1, axis=0, keepdims=True), pl.program_id(0) == 0)

    vec = _full_spec((1, D_MODEL))
    return pl.pallas_call(
        body, name="in_proj_bwd", grid=(SEQ // TM,),
        out_shape=[jax.ShapeDtypeStruct((SEQ, D_MODEL), F32), jax.ShapeDtypeStruct((1, D_MODEL), F32)],
        in_specs=[_row_tile_spec(p.shape[1]) for p in pieces]
                 + [_weight_spec(w_in_g.shape), _row_tile_spec(D_MODEL), _row_tile_spec(D_MODEL), vec, TOKEN_SPEC],
        out_specs=[_row_tile_spec(D_MODEL), vec],
        compiler_params=_params(("arbitrary",)),
    )(*pieces, w_in_g, x, dx2, g1, after)


def weight_grad_in(h1, pieces):
    kh = D_MODEL // 2

    def body(a_ref, p0_ref, p1_ref, p2_ref, o_ref):
        p_refs = (p0_ref, p1_ref, p2_ref)
        a = a_ref[...]
        for j in range(N_CHIPS):
            @pl.when(pl.program_id(1) == j)
            def _():
                for p, lo, off, width in _dproj_segments(j):
                    o_ref[0, 0, :, off:off + width] = _dot_tn(a, p_refs[p][:, lo:lo + width]).astype(MM)

    return pl.pallas_call(
        body, name="dw_in", grid=(2, N_CHIPS), out_shape=jax.ShapeDtypeStruct((N_CHIPS, 2, kh, IN_SHARD), MM),
        in_specs=[pl.BlockSpec((SEQ, kh), lambda h, j: (0, h))] + [_weight_spec(p.shape) for p in pieces],
        out_specs=pl.BlockSpec((1, 1, kh, IN_SHARD), lambda h, j: (j, h, 0, 0)),
        compiler_params=_params(("arbitrary", "arbitrary")),
    )(h1, *pieces)


def weight_grad(a, b, name, col_sharded, tk=None):
    kin, n = a.shape[1], b.shape[1]

    def body(a_ref, b_ref, o_ref):
        if col_sharded:
            o_ref[0, 0] = _dot_tn(a_ref[...], b_ref[...]).astype(MM)
        else:
            o_ref[...] = _dot_tn(a_ref[...], b_ref[...]).astype(MM)

    if col_sharded:
        kh, ns = kin // 2, n // N_CHIPS
        out = jax.ShapeDtypeStruct((N_CHIPS, 2, kh, ns), MM)
        grid = (2, N_CHIPS)
        in_specs = [pl.BlockSpec((SEQ, kh), lambda h, j: (0, h)), pl.BlockSpec((SEQ, ns), lambda h, j: (0, j))]
        out_spec = pl.BlockSpec((1, 1, kh, ns), lambda h, j: (j, h, 0, 0))
        sem = ("arbitrary", "arbitrary")
    else:
        out = jax.ShapeDtypeStruct((kin, n), MM)
        grid = (kin // tk,)
        in_specs = [pl.BlockSpec((SEQ, tk), lambda r: (0, r)), pl.BlockSpec((SEQ, n), lambda r: (0, 0))]
        out_spec = pl.BlockSpec((tk, n), lambda r: (r, 0))
        sem = ("arbitrary",)
    res = pl.pallas_call(
        body, name=name, grid=grid, out_shape=out, in_specs=in_specs, out_specs=out_spec,
        compiler_params=_params(sem),
    )(a, b)
    if not col_sharded:
        res = res.reshape(N_CHIPS, 2, kin // (2 * N_CHIPS), n)
    return res


def weight_grad_mix(merged, dmix, u3, dco, att, dao):
    operands = (merged, dmix, u3, dco, att, dao)
    n_out, n_br = D_MODEL // 2, CONV_DIM // 2

    def body(*refs):
        hbm, (o_out, o_cb, o_ab), bufs, sems = refs[:6], refs[6:9], refs[9:15], refs[15]
        copies = [pltpu.make_async_copy(hbm[i], bufs[i], sems.at[i]) for i in range(6)]
        for cp in copies:
            cp.start()
        m_ref, dm_ref, u_ref, dco_ref, a_ref, dao_ref = bufs
        copies[0].wait()
        copies[1].wait()
        for h in range(2):
            o_out[h * n_out:(h + 1) * n_out, :] = _dot_tn(m_ref[:, h * n_out:(h + 1) * n_out], dm_ref[...]).astype(MM)
        for br, (a, d, o) in enumerate(((u_ref, dco_ref, o_cb), (a_ref, dao_ref, o_ab))):
            copies[2 + 2 * br].wait()
            copies[3 + 2 * br].wait()
            for h in range(2):
                g = _dot_tn(a[:, h * n_br:(h + 1) * n_br], d[...])
                for j in range(N_CHIPS):
                    o[j, h] = g[:, j * BR_SHARD:(j + 1) * BR_SHARD].astype(MM)

    branch = jax.ShapeDtypeStruct((N_CHIPS, 2, n_br, BR_SHARD), MM)
    dw_out, dw_cb, dw_ab = pl.pallas_call(
        body, name="dw_mix", out_shape=[jax.ShapeDtypeStruct((D_MODEL, D_MODEL), MM), branch, branch],
        in_specs=[ANY] * 6, out_specs=[VMEM_SPEC] * 3,
        scratch_shapes=[pltpu.VMEM(a.shape, a.dtype) for a in operands] + [pltpu.SemaphoreType.DMA((6,))],
        compiler_params=_params(),
    )(*operands)
    return dw_out.reshape(N_CHIPS, 2, D_MODEL // (2 * N_CHIPS), D_MODEL), dw_cb, dw_ab


def _place():
    x, y, c = lax.axis_index("x"), lax.axis_index("y"), lax.axis_index("c")
    chips = [(1 - x, y), (x, 1 - y), (1 - x, 1 - y)]
    return x, y, c, chips


def _rcopy(src, dst, send_sem, recv_sem, dev):
    return pltpu.make_async_remote_copy(src_ref=src, dst_ref=dst, send_sem=send_sem, recv_sem=recv_sem,
                                        device_id=dev, device_id_type=MESH)


class _Gather:
    N_MOVES = 6

    def __init__(self, shapes, w, o, scratch):
        self.n, self.shapes, self.w, self.o = len(w), shapes, w, o
        self.send, self.recv, self.psend, self.precv, self.loc_in, self.loc_out = scratch[:6]
        self.raw, self.stage = scratch[6:6 + self.n], scratch[6 + self.n:]
        x, y, c, self.chips = _place()
        self.c = c
        self.me, k_x, k_y, k_far = 2 * x + y, 2 * (1 - x) + y, 2 * x + (1 - y), 2 * (1 - x) + (1 - y)
        to_x, to_y = (1 - x, y, c), (x, 1 - y, c)
        self.sib = (x, y, 1 - c)
        self.sent_as = [(self.me, 0, to_x), (self.me, 1, to_y), (self.me, 1, to_x), (self.me, 0, to_y),
                        (k_x, 0, to_y), (k_y, 1, to_x)]
        self.arrives_as = [(k_x, 0, to_x), (k_y, 1, to_y), (k_x, 1, to_x), (k_y, 0, to_y),
                           (k_far, 0, to_y), (k_far, 1, to_x)]
        self.sent_on_after = {0: 4, 1: 5}

    @staticmethod
    def scratch(shards):
        n = len(shards)
        sems = pltpu.SemaphoreType.DMA
        m = _Gather.N_MOVES * n
        return ([sems((m,)), sems((m,)), sems((m,)), sems((m,)), sems((3 * n,)), sems((n,))]
                + [pltpu.VMEM(s.shape, s.dtype) for s in shards] + [pltpu.VMEM(s.shape, MM) for s in shards])

    @staticmethod
    def out_shapes(shards):
        return [jax.ShapeDtypeStruct((N_CHIPS,) + s.shape, MM) for s in shards]

    def _rows(self, t, quarter, cc):
        rq = self.shapes[t][0] // 4
        return pl.ds(pl.multiple_of((2 * cc + quarter) * rq, rq), rq)

    def _own_rows(self, t, piece):
        if piece < 2:
            return self._rows(t, piece, self.c)
        rh = self.shapes[t][0] // 2
        return pl.ds(pl.multiple_of((1 - self.c) * rh, rh), rh)

    def _chip(self, j):
        cx, cy = self.chips[j]
        return 2 * cx + cy, (cx, cy, self.c)

    def local_in(self, t, piece):
        rows = self._own_rows(t, piece)
        return pltpu.make_async_copy(self.w[t].at[rows, :], self.raw[t].at[rows, :], self.loc_in.at[3 * t + piece])

    def local_out(self, t):
        return pltpu.make_async_copy(self.stage[t], self.o[t].at[self.me], self.loc_out.at[t])

    def sent(self, i, t):
        k, quarter, dev = self.sent_as[i]
        rows = self._rows(t, quarter, self.c)
        there = self.o[t].at[k, rows, :]
        return _rcopy(self.stage[t].at[rows, :] if i < 4 else there, there,
                      self.send.at[i * self.n + t], self.recv.at[i * self.n + t], dev)

    def arrived(self, i, t):
        k, quarter, dev = self.arrives_as[i]
        blk = self.o[t].at[k, self._rows(t, quarter, self.c), :]
        return _rcopy(blk, blk, self.send.at[i * self.n + t], self.recv.at[i * self.n + t], dev)

    def passed(self, i, t, cc):
        k, quarter, _ = self.arrives_as[i]
        blk = self.o[t].at[k, self._rows(t, quarter, cc), :]
        return _rcopy(blk, blk, self.psend.at[i * self.n + t], self.precv.at[i * self.n + t], self.sib)

    def start(self):
        for piece in range(3):
            for t in range(self.n):
                self.local_in(t, piece).start()
        for piece, moves in enumerate(((0, 3), (1, 2), ())):
            for t in range(self.n):
                rows = self._own_rows(t, piece)
                self.local_in(t, piece).wait()
                self.stage[t][rows, :] = self.raw[t][rows, :].astype(MM)
                for i in moves:
                    self.sent(i, t).start()
        for t in range(self.n):
            self.local_out(t).start()

    def forward(self):
        for i in range(self.N_MOVES):
            for t in range(self.n):
                self.arrived(i, t).wait_recv()
                if i in self.sent_on_after:
                    self.sent(self.sent_on_after[i], t).start()
                self.passed(i, t, self.c).start()

    def finish(self):
        for i in range(self.N_MOVES):
            for t in range(self.n):
                self.passed(i, t, 1 - self.c).wait_recv()
        for i in range(self.N_MOVES):
            for t in range(self.n):
                self.sent(i, t).wait_send()
                self.passed(i, t, self.c).wait_send()
        for t in range(self.n):
            self.local_out(t).wait()


def all_gather_weights(shards, small, later):
    n, m = len(shards), len(later)
    shapes = [s.shape for s in shards]

    def body(*refs):
        w = refs[:n]
        sm = refs[n]
        lw = refs[n + 1:n + 1 + m]
        o = refs[n + 1 + m:2 * n + 1 + m]
        osm = refs[2 * n + 1 + m]
        lo = refs[2 * n + 2 + m:2 * n + 2 + 2 * m]
        scratch = refs[2 * n + 2 + 2 * m:]
        ssend, srecv, sloc, lsem_in, lsem_out = scratch[:5]
        lraw, lstage = scratch[5:5 + m], scratch[5 + m:5 + 2 * m]
        g = _Gather(shapes, w, o, scratch[5 + 2 * m:])
        own = pltpu.make_async_copy(sm, osm.at[g.me], sloc)
        own.start()
        g.start()
        loads = [pltpu.make_async_copy(lw[t], lraw[t], lsem_in.at[t]) for t in range(m)]
        for cp in loads:
            cp.start()
        small_cps = [_rcopy(sm, osm.at[g.me], ssend.at[j], srecv.at[j], g._chip(j)[1]) for j in range(3)]
        for cp in small_cps:
            cp.start()
        places = []
        for t in range(m):
            loads[t].wait()
            lstage[t][...] = lraw[t][...].astype(MM)
            places.append(pltpu.make_async_copy(lstage[t], lo[t].at[g.me], lsem_out.at[t]))
            places[t].start()
        g.forward()
        g.finish()
        for j in range(3):
            k, dev = g._chip(j)
            _rcopy(sm, osm.at[k], ssend.at[j], srecv.at[j], dev).wait_recv()
            small_cps[j].wait_send()
        own.wait()
        for cp in places:
            cp.wait()

    out_shape = _Gather.out_shapes(shards)
    out_shape.append(jax.ShapeDtypeStruct((N_CHIPS,) + small.shape, small.dtype))
    out_shape += _Gather.out_shapes(later)
    sems = pltpu.SemaphoreType.DMA
    return pl.pallas_call(
        body, name="all_gather_weights", out_shape=out_shape,
        in_specs=[ANY] * (n + 1 + m), out_specs=[ANY] * (n + 1 + m),
        scratch_shapes=[sems((3,)), sems((3,)), sems, sems((m,)), sems((m,))]
                       + [pltpu.VMEM(s.shape, s.dtype) for s in later] + [pltpu.VMEM(s.shape, MM) for s in later]
                       + _Gather.scratch(shards),
        compiler_params=_params(),
    )(*shards, small, *later)


HBM_SPEC = pl.BlockSpec(memory_space=pltpu.HBM)
SEM_SPEC = pl.BlockSpec(memory_space=pltpu.SEMAPHORE)
DATAFLOW = pltpu.SideEffectType.DATAFLOW_SIDE_EFFECTING


def split_start(name, bufs, n_copies, copies):
    nb = len(bufs)

    def body(*refs):
        for cp in copies(refs[:nb], refs[nb], refs[nb + 1]):
            cp.start()
        token = refs[2 * nb + 2]
        token[...] = jnp.zeros_like(token)

    sems = [pltpu.SemaphoreType.DMA((n_copies,))] * 2
    res = pl.pallas_call(
        body, name=name,
        out_shape=sems + [pltpu.HBM(a.shape, a.dtype) for a in bufs] + [jax.ShapeDtypeStruct((8, 128), F32)],
        in_specs=[HBM_SPEC] * nb, out_specs=[SEM_SPEC] * 2 + [HBM_SPEC] * nb + [VMEM_SPEC],
        input_output_aliases={i: 2 + i for i in range(nb)},
        compiler_params=pltpu.CompilerParams(has_side_effects=DATAFLOW),
    )(*[pltpu.with_memory_space_constraint(a, pltpu.HBM) for a in bufs])
    return res[:-1], res[-1]


def split_wait(name, state, after, copies):
    sems, bufs = state[:2], state[2:]
    nb = len(bufs)

    def body(*refs):
        for cp in copies(refs[:nb], refs[nb], refs[nb + 1]):
            cp.wait_send()
            cp.wait_recv()

    return pl.pallas_call(
        body, name=name, out_shape=[pltpu.HBM(a.shape, a.dtype) for a in bufs],
        in_specs=[HBM_SPEC] * nb + [SEM_SPEC] * 2 + [ANY] * len(after), out_specs=[HBM_SPEC] * nb,
        input_output_aliases={i: i for i in range(nb)},
        compiler_params=pltpu.CompilerParams(has_side_effects=DATAFLOW),
    )(*bufs, *sems, *after)


class _Shifted:
    def __init__(self, sems, first):
        self.sems, self.first = sems, first

    @property
    def at(self):
        return self

    def __getitem__(self, i):
        return self.sems.at[self.first + i]


def _scatter_copies(n):
    def copies(refs, send, recv):
        _, _, c, chips = _place()
        return [_rcopy(refs[t].at[2 * cx + cy], refs[n + t].at[j], send.at[3 * t + j], recv.at[3 * t + j], (cx, cy, c))
                for t in range(n) for j, (cx, cy) in enumerate(chips)]
    return copies


def _direct_copies(n):
    def copies(refs, send, recv):
        x, y, c, chips = _place()
        out = []
        for t in range(n):
            src, land = refs[t], refs[n + t]
            for to_core, first in ((c, 0), (1 - c, 3)):
                for j, (cx, cy) in enumerate(chips):
                    i = 7 * t + first + j
                    out.append(_rcopy(src.at[2 * cx + cy, to_core], land.at[first + j], send.at[i], recv.at[i],
                                      (cx, cy, to_core)))
            i = 7 * t + 6
            out.append(_rcopy(src.at[2 * x + y, 1 - c], land.at[6], send.at[i], recv.at[i], (x, y, 1 - c)))
        return out
    return copies


def _scatter_and_direct(n, m):
    def copies(refs, send, recv):
        return (_scatter_copies(n)(refs[:2 * n], send, recv)
                + _direct_copies(m)(refs[2 * n:], _Shifted(send, 3 * n), _Shifted(recv, 3 * n)))
    return copies


def scatter_start(parts, partials):
    n, m = len(parts), len(partials)
    lands = [lax.empty((3,) + p.shape[1:], p.dtype) for p in parts]
    direct_lands = [lax.empty((7,) + p.shape[2:], p.dtype) for p in partials]
    return split_start("scatter_start_rest", list(parts) + lands + list(partials) + direct_lands, 3 * n + 7 * m,
                       _scatter_and_direct(n, m))


def scatter_wait(state, after, n, m):
    res = split_wait("scatter_wait_rest", state, after, _scatter_and_direct(n, m))
    return res[n:2 * n], res[2 * n:2 * n + m], res[2 * n + m:]


def _gather_copies(shapes, level, to_both_cores=()):
    n = len(shapes)

    def copies(refs, send, recv):
        x, y, c, chips = _place()
        out = []
        for t in list(range(n)) + list(to_both_cores):
            rh = shapes[t][0] // 2
            to_core = c if len(out) < 3 * n else 1 - c
            for cx, cy in chips:
                k, dev = (2 * x + y, (cx, cy, to_core)) if level == 1 else (2 * cx + cy, (x, y, 1 - c))
                blk = refs[t].at[k, pl.ds(c * rh, rh), :]
                out.append(_rcopy(blk, blk, send.at[len(out)], recv.at[len(out)], dev))
        return out
    return copies


def _sibling_copies(n, other_half):
    def copies(refs, send, recv):
        x, y, c, _ = _place()
        return [_rcopy(refs[t].at[:, 1 - c] if other_half else refs[t], refs[n + t], send.at[t], recv.at[t],
                       (x, y, 1 - c)) for t in range(n)]
    return copies


def sibling_start(srcs, other_half, tag):
    lands = [lax.empty((a.shape[0],) + a.shape[2:] if other_half else a.shape, a.dtype) for a in srcs]
    return split_start("sibling_start_" + tag, list(srcs) + lands, len(srcs),
                       _sibling_copies(len(srcs), other_half))


def sibling_wait(state, after, other_half, tag):
    n = (len(state) - 2) // 2
    res = split_wait("sibling_wait_" + tag, state, after, _sibling_copies(n, other_half))
    return res[:n], res[n:]


def small_pack(ddw, v512, v1024, loss_parts):
    rows, width = PACK_ROWS, 512
    n512, n1024 = len(VEC512), len(VEC1024)

    def body(*refs):
        ddw_ref = refs[0]
        a_refs = refs[1:1 + n512]
        b_refs = refs[1 + n512:1 + n512 + n1024]
        lp_ref, o_ref, p_ref = refs[1 + n512 + n1024:]
        p_ref[...] = jnp.zeros_like(p_ref)
        p_ref[0:32, :] = ddw_ref[...]
        p_ref[LOSS_ROW:LOSS_ROW + 1, 0:128] = jnp.sum(lp_ref[...], axis=0, keepdims=True) * 0.125
        for i, r in enumerate(a_refs):
            p_ref[32 + i:33 + i, :] = r[...]
        for i, r in enumerate(b_refs):
            base = 32 + n512 + 2 * i
            p_ref[base:base + 1, :] = r[:, 0:512]
            p_ref[base + 1:base + 2, :] = r[:, 512:1024]
        x, y, c, _ = _place()
        o_ref[4 * x + 2 * y + c] = p_ref[...]

    n_in = 2 + n512 + n1024
    return pl.pallas_call(
        body, name="small_pack", out_shape=jax.ShapeDtypeStruct((8, rows, width), F32),
        in_specs=[VMEM_SPEC] * n_in, out_specs=VMEM_SPEC,
        scratch_shapes=[pltpu.VMEM((rows, width), F32)],
    )(ddw, *[v512[n] for n in VEC512], *[v1024[n] for n in VEC1024], loss_parts)


def _small_copies(refs, send, recv):
    x, y, c, _ = _place()
    mine = refs[0].at[4 * x + 2 * y + c]
    peers = [(1 - x if k & 4 else x, 1 - y if k & 2 else y, 1 - c if k & 1 else c) for k in range(1, 8)]
    return [_rcopy(mine, mine, send.at[i], recv.at[i], dev) for i, dev in enumerate(peers)]


def _row_block(r):
    for tr in (512, 352, 256, 128):
        if r % tr == 0:
            return tr
    return r


def add_halves(g, recv, name):
    _, _, r, w = g.shape
    tr = _row_block(r)

    def body(g_ref, r_ref, ob_ref, own_ref):
        k = pl.program_id(1)
        me = 2 * lax.axis_index("x") + lax.axis_index("y")
        t = g_ref[0, 0].astype(F32) + r_ref[0].astype(F32)
        ob_ref[0] = t.astype(MM)
        mine = jnp.where(k == me, t, 0.0)

        @pl.when(k == 0)
        def _():
            own_ref[...] = mine

        @pl.when(k != 0)
        def _():
            own_ref[...] += mine

    return pl.pallas_call(
        body, name=name, grid=(r // tr, N_CHIPS),
        in_specs=[pl.BlockSpec((1, 1, tr, w), lambda i, k: (k, lax.axis_index("c"), i, 0)),
                  pl.BlockSpec((1, tr, w), lambda i, k: (k, i, 0))],
        out_specs=[pl.BlockSpec((1, tr, w), lambda i, k: (k, i, 0)),
                   pl.BlockSpec((tr, w), lambda i, k: (i, 0))],
        out_shape=(jax.ShapeDtypeStruct((N_CHIPS, r, w), MM), jax.ShapeDtypeStruct((r, w), F32)),
        compiler_params=_params(("arbitrary", "arbitrary")),
    )(g, recv)


def sum_parts(own, rin, after, name):
    _, r, w = rin.shape
    tr = _row_block(r)

    def body(o_ref, r_ref, after_ref, out_ref):
        out_ref[...] = ((o_ref[...] + r_ref[0].astype(F32)) + r_ref[1].astype(F32)) + r_ref[2].astype(F32)

    return pl.pallas_call(
        body, name=name, grid=(r // tr,), out_shape=jax.ShapeDtypeStruct((r, w), F32),
        in_specs=[pl.BlockSpec((tr, w), lambda i: (i, 0)), pl.BlockSpec((3, tr, w), lambda i: (0, i, 0)),
                  TOKEN_SPEC],
        out_specs=pl.BlockSpec((tr, w), lambda i: (i, 0)),
        compiler_params=_params(("arbitrary",)),
    )(own, rin, after)


def sum_partials(p, land, after, name):
    _, _, r, w = p.shape
    tr = _row_block(r)

    def body(p_ref, l_ref, after_ref, out_ref):
        total = p_ref[0, 0].astype(F32)
        for slot in (6, 0, 3, 1, 4, 2, 5):
            total = total + l_ref[slot].astype(F32)
        out_ref[...] = total

    def own(i):
        return 2 * lax.axis_index("x") + lax.axis_index("y"), lax.axis_index("c"), i, 0

    return pl.pallas_call(
        body, name=name, grid=(r // tr,), out_shape=jax.ShapeDtypeStruct((r, w), F32),
        in_specs=[pl.BlockSpec((1, 1, tr, w), own), pl.BlockSpec((7, tr, w), lambda i: (0, i, 0)), TOKEN_SPEC],
        out_specs=pl.BlockSpec((tr, w), lambda i: (i, 0)),
        compiler_params=_params(("arbitrary",)),
    )(p, land, after)


def _adamw_math(w, g, m, v):
    mn = ADAM_B1 * m + (1.0 - ADAM_B1) * g
    vn = ADAM_B2 * v + (1.0 - ADAM_B2) * (g * g)
    m_hat = mn / (1.0 - ADAM_B1 ** ADAM_STEP)
    v_hat = vn / (1.0 - ADAM_B2 ** ADAM_STEP)
    return -ADAM_LR * (m_hat / (jnp.sqrt(v_hat) + ADAM_EPS) + ADAM_WD * w), mn, vn


def adamw(w, mine, other, m, v, name):
    r, c = w.shape
    rh = r // 2
    tr = _row_block(rh)
    if c >= 1024 and tr % 512 == 0:
        tr = 256
    nb = rh // tr

    def body(w_ref, a_ref, b_ref, m_ref, v_ref, go_ref, d_ref, mo_ref, vo_ref):
        gv = jnp.where(lax.axis_index("c") == pl.program_id(0), a_ref[...], b_ref[...])
        go_ref[...] = gv
        d_ref[...], mo_ref[...], vo_ref[...] = _adamw_math(w_ref[...], gv, m_ref[...], v_ref[...])

    def half(of_sibling):
        def index(h, i):
            owner = lax.axis_index("c")
            owner = 1 - owner if of_sibling else owner
            return jnp.where(h == owner, i, jnp.where(h < owner, 0, nb - 1)), 0
        return pl.BlockSpec((tr, c), index)

    spec = pl.BlockSpec((tr, c), lambda h, i: (h * nb + i, 0))
    out = jax.ShapeDtypeStruct((r, c), F32)
    return pl.pallas_call(
        body, name=name, grid=(2, nb), out_shape=(out, out, out, out),
        in_specs=[spec, half(False), half(True), spec, spec], out_specs=[spec] * 4,
        compiler_params=_params(("arbitrary", "arbitrary")),
    )(w, mine, other, m, v)


def adamw_small(packs, params, after):
    names = list(params)
    flat = [a for n in names for a in params[n]]

    def body(*refs):
        p_ref = refs[0]
        ins = refs[1:1 + 3 * len(names)]
        loss_ref, g_ref = refs[2 + 3 * len(names):4 + 3 * len(names)]
        outs = refs[4 + 3 * len(names):]
        total = p_ref[0]
        for d in range(1, 8):
            total = total + p_ref[d]
        g_ref[...] = total
        loss_ref[...] = g_ref[LOSS_ROW:LOSS_ROW + 1, 0:1]
        me = 2 * lax.axis_index("x") + lax.axis_index("y")
        for i, n in enumerate(names):
            w_ref, m_ref, v_ref = ins[3 * i:3 * i + 3]
            go_ref, d_ref, mo_ref, vo_ref = outs[4 * i:4 * i + 4]
            if n == "conv_dw_w":
                gv = jnp.zeros((CONV_WIDTH, 128), F32)
                for k in range(N_CHIPS):
                    gv = gv + jnp.where(me == k, g_ref[0:CONV_WIDTH, 128 * k:128 * (k + 1)], 0.0)
            elif n in VEC512:
                r0 = 32 + VEC512.index(n)
                gv = g_ref[r0:r0 + 1, :]
            else:
                r0 = 32 + len(VEC512) + 2 * VEC1024.index(n)
                gv = jnp.concatenate([g_ref[r0:r0 + 1, :], g_ref[r0 + 1:r0 + 2, :]], axis=1)
            go_ref[...] = gv
            d_ref[...], mo_ref[...], vo_ref[...] = _adamw_math(w_ref[...], gv, m_ref[...], v_ref[...])

    out_shape = [jax.ShapeDtypeStruct((1, 1), F32), jax.ShapeDtypeStruct(packs.shape[1:], F32)]
    out_shape += [jax.ShapeDtypeStruct(params[n][0].shape, F32) for n in names for _ in range(4)]
    res = pl.pallas_call(
        body, name="adamw_small", out_shape=out_shape,
        in_specs=[VMEM_SPEC] * (2 + len(flat)), out_specs=[VMEM_SPEC] * len(out_shape),
        compiler_params=_params(),
    )(packs, *flat, after)
    return res[0], res[1], {n: res[2 + 4 * i:6 + 4 * i] for i, n in enumerate(names)}


REST = ("w_ffn_up", "w_ffn_down", "w_out", "w_conv_branch", "w_att_branch")
VEC512 = ("conv_dw_b", "conv_ln_g", "conv_ln_b")
VEC1024 = ("norm_mix_pre", "b_conv_branch", "norm_mix_post", "norm_ffn_pre", "norm_ffn_post")
PACK_ROWS = 48
LOSS_ROW = 47


def kernel(x, norm_mix_pre, w_in, conv_dw_w, conv_dw_b, conv_ln_g, conv_ln_b, w_conv_branch, b_conv_branch, w_att_branch, w_out, norm_mix_post, norm_ffn_pre, w_ffn_up, w_ffn_down, norm_ffn_post, loss_target, m_norm_mix_pre, m_w_in, m_conv_dw_w, m_conv_dw_b, m_conv_ln_g, m_conv_ln_b, m_w_conv_branch, m_b_conv_branch, m_w_att_branch, m_w_out, m_norm_mix_post, m_norm_ffn_pre, m_w_ffn_up, m_w_ffn_down, m_norm_ffn_post, v_norm_mix_pre, v_w_in, v_conv_dw_w, v_conv_dw_b, v_conv_ln_g, v_conv_ln_b, v_w_conv_branch, v_b_conv_branch, v_w_att_branch, v_w_out, v_norm_mix_post, v_norm_ffn_pre, v_w_ffn_up, v_w_ffn_down, v_norm_ffn_post):
    weights = dict(norm_mix_pre=norm_mix_pre, w_in=w_in, conv_dw_w=conv_dw_w, conv_dw_b=conv_dw_b, conv_ln_g=conv_ln_g, conv_ln_b=conv_ln_b, w_conv_branch=w_conv_branch, b_conv_branch=b_conv_branch, w_att_branch=w_att_branch, w_out=w_out, norm_mix_post=norm_mix_post, norm_ffn_pre=norm_ffn_pre, w_ffn_up=w_ffn_up, w_ffn_down=w_ffn_down, norm_ffn_post=norm_ffn_post)
    mom = dict(norm_mix_pre=m_norm_mix_pre, w_in=m_w_in, conv_dw_w=m_conv_dw_w, conv_dw_b=m_conv_dw_b, conv_ln_g=m_conv_ln_g, conv_ln_b=m_conv_ln_b, w_conv_branch=m_w_conv_branch, b_conv_branch=m_b_conv_branch, w_att_branch=m_w_att_branch, w_out=m_w_out, norm_mix_post=m_norm_mix_post, norm_ffn_pre=m_norm_ffn_pre, w_ffn_up=m_w_ffn_up, w_ffn_down=m_w_ffn_down, norm_ffn_post=m_norm_ffn_post)
    var = dict(norm_mix_pre=v_norm_mix_pre, w_in=v_w_in, conv_dw_w=v_conv_dw_w, conv_dw_b=v_conv_dw_b, conv_ln_g=v_conv_ln_g, conv_ln_b=v_conv_ln_b, w_conv_branch=v_w_conv_branch, b_conv_branch=v_b_conv_branch, w_att_branch=v_w_att_branch, w_out=v_w_out, norm_mix_post=v_norm_mix_post, norm_ffn_pre=v_norm_ffn_pre, w_ffn_up=v_w_ffn_up, w_ffn_down=v_w_ffn_down, norm_ffn_post=v_norm_ffn_post)
    order = list(weights)
    grads, deltas, new_m, new_v = {}, {}, {}, {}
    xs = x.reshape(SEQ, D_MODEL)
    tgt = loss_target.reshape(SEQ, D_MODEL)
    row = lambda a: a.reshape(1, -1)
    g1, g2, g3, g4 = (row(weights[n]) for n in ("norm_mix_pre", "norm_mix_post", "norm_ffn_pre", "norm_ffn_post"))
    ln_g, ln_b = row(conv_ln_g), row(conv_ln_b)

    summed, from_chips = {}, {}

    def core_sums(names, state, after, tag):
        own, from_sibling = sibling_wait(state, after, True, tag)
        for n, g, r in zip(names, own, from_sibling):
            summed[n] = add_halves(g, r, "add_" + n)

    def chip_sums(names, after):
        return [sum_parts(summed[n][1], from_chips[n], after, "sum_" + n) for n in names]

    def optimize(names, state, after, tag):
        mine, other = sibling_wait(state, after, False, tag)
        for n, a, b in zip(names, mine, other):
            grads[n], deltas[n], new_m[n], new_v[n] = adamw(weights[n], a, b, mom[n], var[n], "adamw_" + n)

    w_in_g, dw_g, *rest = all_gather_weights([w_in], conv_dw_w, [weights[n] for n in REST])
    w_dw_full = jnp.concatenate([dw_g[k] for k in range(N_CHIPS)], axis=1)
    rest_shapes = [weights[n].shape for n in REST]
    over_ici = _gather_copies(rest_shapes, 1, to_both_cores=(2, 3, 4))
    state, token = split_start("gather_start", rest, 3 * (len(REST) + 3), over_ici)
    h1, ci, q, k, v, gc, ga = in_proj_fwd(xs, g1, w_in_g, token)
    u1, u3 = conv_fwd(ci, w_dw_full, row(conv_dw_b), ln_g, ln_b)
    att, rc = attn_fwd(q, k, v)
    rest = split_wait("gather_wait", state, [att], over_ici)
    w_out_g, w_cb_g, w_ab_g = rest[2:]
    w_out_g = w_out_g.reshape(D_MODEL, D_MODEL)
    to_sibling = _gather_copies(rest_shapes[:2], 2)
    state, token = split_start("pass_start", rest[:2], 3 * 2, to_sibling)
    merged, mix, x2, h2 = mix_fwd(u3, att, gc, ga, xs, w_cb_g, row(b_conv_branch), w_ab_g, w_out_g, g2, g3, token)
    w_up_g, w_down_g = split_wait("pass_wait", state, [h2], to_sibling)
    w_down_g = w_down_g.reshape(D_FF, D_MODEL)
    gate, up, act = ffn_up_fwd(h2, w_up_g)
    dff, dy, loss_parts, dg4 = ffn_down_loss(act, w_down_g, x2, tgt, g4)

    dgu = ffn_act_bwd(dff, w_down_g, gate, up)
    dx2, dmix, dg3, dg2 = ffn_in_bwd(dgu, w_up_g, x2, mix, dy, g3, g2)
    ffn_grads = [weight_grad(h2, dgu, "dw_ffn_up", True), weight_grad(act, dff, "dw_ffn_down", False, tk=UP_SHARD)]
    to_ffn, token = sibling_start(ffn_grads, True, "dw_ffn")
    dco, dao, dg, du3, datt, dbcb = merge_bwd(dmix, w_out_g, gc, ga, u3, att, w_cb_g, row(b_conv_branch), w_ab_g,
                                              token)
    mix_grads = weight_grad_mix(merged, dmix, u3, dco, att, dao)
    core_sums(REST[:2], to_ffn, [mix_grads[0]], "dw_ffn")
    state, token = scatter_start([summed[n][0] for n in REST[:2]], mix_grads)
    dci, ddw, dbdw, dlng, dlnb = conv_bwd(du3, u1, ci, w_dw_full, ln_g, ln_b, token)
    dqkv = attn_bwd(q, k, v, datt, rc, token)
    ffn_from_chips, mix_grads, mix_from_all = scatter_wait(state, [dci, dqkv], 2, len(mix_grads))
    from_chips.update(zip(REST[:2], ffn_from_chips))
    dproj = (dci, dqkv, dg)
    to_in, token = sibling_start([weight_grad_in(h1, dproj)], True, "dw_in")
    grad_x, dg1 = in_proj_bwd(dproj, w_in_g, xs, dx2, g1, token)
    v512 = dict(conv_dw_b=dbdw, conv_ln_g=dlng, conv_ln_b=dlnb)
    v1024 = dict(norm_mix_pre=dg1, b_conv_branch=dbcb, norm_mix_post=dg2, norm_ffn_pre=dg3, norm_ffn_post=dg4)
    packs = small_pack(ddw, v512, v1024, loss_parts)
    core_sums(("w_in",), to_in, [packs], "dw_in")
    to_chips = summed["w_in"][0]
    landing = lax.empty((3,) + to_chips.shape[1:], to_chips.dtype)

    def scatter_and_packs(refs, send, recv):
        return (_scatter_copies(1)(refs[:2], send, recv)
                + _small_copies(refs[2:], _Shifted(send, 3), _Shifted(recv, 3)))

    state, token = split_start("scatter_start_w_in", [to_chips, landing, packs], 3 + 7, scatter_and_packs)
    swap_up, token = sibling_start(chip_sums(REST[:1], token), False, "sum_ffn_up")
    rest_sums = chip_sums(REST[1:2], token) + [sum_partials(p, r, token, "sum_" + n)
                                               for n, p, r in zip(REST[2:], mix_grads, mix_from_all)]
    swap_rest, token = sibling_start(rest_sums, False, "sum_rest")
    optimize(REST[:1], swap_up, [token], "sum_ffn_up")
    optimize(REST[1:], swap_rest, [new_v["w_ffn_up"]], "sum_rest")
    _, from_chips["w_in"], packs = split_wait("scatter_wait_w_in", state, [new_v[n] for n in REST], scatter_and_packs)
    swap_in, token = sibling_start(chip_sums(("w_in",), token), False, "sum_w_in")
    as_rows = lambda n, a: a if n == "conv_dw_w" else a.reshape(1, -1)
    small_names = ("conv_dw_w",) + VEC512 + VEC1024
    loss, gsum, small = adamw_small(
        packs, {n: tuple(as_rows(n, d[n]) for d in (weights, mom, var)) for n in small_names}, token)
    optimize(("w_in",), swap_in, [gsum], "sum_w_in")
    for n in small_names:
        grads[n], deltas[n], new_m[n], new_v[n] = (a.reshape(weights[n].shape) for a in small[n])

    return (loss.reshape(()), grad_x.reshape(1, SEQ, D_MODEL),*[grads[n] for n in order], *[deltas[n] for n in order],
            *[new_m[n] for n in order], *[new_v[n] for n in order])
```

```python
import jax
import jax.numpy as jnp
from jax import lax
from jax.experimental import pallas as pl
from jax.experimental.pallas import tpu as pltpu

F32 = jnp.float32
MM = jnp.bfloat16

SEQ = 2048
D_MODEL = 1024
CONV_DIM = 512
ATT_DIM = 512
CONV_WIDTH = 31
D_FF = 2816
IN_COLS = 2 * CONV_DIM + 3 * ATT_DIM + 2 * D_MODEL
N_CHIPS = 4
IN_SHARD = IN_COLS // N_CHIPS
UP_SHARD = 2 * D_FF // N_CHIPS
BR_SHARD = D_MODEL // N_CHIPS
EPS = 1e-6
ATT_SCALE = 0.125

TM = 256
GLU_ROWS = 256
TQ = 128
CONV_TILE = 64
CONV_WIN = CONV_TILE + 32
VMEM_LIMIT = 56 * 1024 * 1024

ADAM_LR = 0.001
ADAM_B1 = 0.9
ADAM_B2 = 0.999
ADAM_EPS = 1e-08
ADAM_WD = 0.01
ADAM_STEP = 10

MESH = pl.DeviceIdType.MESH
ANY = pl.BlockSpec(memory_space=pl.ANY)
VMEM_SPEC = pl.BlockSpec(memory_space=pltpu.VMEM)

NT_DIMS = (((1,), (1,)), ((), ()))
TN_DIMS = (((0,), (0,)), ((), ()))

IN_PIECES = (("ci", 0, 1024), ("q", 1024, 1536), ("k", 1536, 2048), ("v", 2048, 2560),
             ("gc", 2560, 3584), ("ga", 3584, 4608))


def _params(sem=None, vmem=VMEM_LIMIT):
    return pltpu.CompilerParams(dimension_semantics=sem, vmem_limit_bytes=vmem)


def _dot(a, b):
    return jnp.dot(a, b, preferred_element_type=F32)


def _dot_nt(a, b):
    return lax.dot_general(a, b, NT_DIMS, preferred_element_type=F32)


def _dot_tn(a, b):
    return lax.dot_general(a, b, TN_DIMS, preferred_element_type=F32)


def _sigmoid(x):
    return 1.0 / (1.0 + jnp.exp(-x))


def _rms(x):
    r = lax.rsqrt(jnp.mean(x * x, axis=-1, keepdims=True) + EPS)
    return x * r, r


def _rms_bwd(dy_g, n, r):
    return r * (dy_g - n * jnp.mean(dy_g * n, axis=-1, keepdims=True))


def _row_tile_spec(width, tm=TM):
    return pl.BlockSpec((tm, width), lambda i: (i, 0))


def _full_spec(shape):
    nd = len(shape)
    return pl.BlockSpec(shape, lambda *_: (0,) * nd)


def _weight_spec(shape):
    nd = len(shape)
    return pl.BlockSpec(shape, lambda *_: (0,) * nd, pipeline_mode=pl.Buffered(1))


def _acc_rows(ref, val, first):
    @pl.when(first)
    def _():
        ref[...] = val

    @pl.when(jnp.logical_not(first))
    def _():
        ref[...] += val


TOKEN_SPEC = pl.BlockSpec((8, 128), lambda *_: (0, 0))


def in_proj_fwd(x, g1, w_in_g, after):
    def body(x_ref, g_ref, w_ref, after_ref, h_ref, ci_ref, q_ref, k_ref, v_ref, gc_ref, ga_ref):
        n, _ = _rms(x_ref[...])
        h = (n * g_ref[...]).astype(MM)
        h_ref[...] = h
        outs = dict(ci=ci_ref, q=q_ref, k=k_ref, v=v_ref, gc=gc_ref, ga=ga_ref)
        for j in range(N_CHIPS):
            p = _dot(h, w_ref[j])
            g0 = j * IN_SHARD
            for name, s, e in IN_PIECES:
                lo, hi = max(s, g0), min(e, g0 + IN_SHARD)
                if lo < hi:
                    ref = outs[name]
                    part = p[:, lo - g0:hi - g0]
                    if name == "q":
                        part = part * ATT_SCALE
                    ref[:, lo - s:hi - s] = part.astype(ref.dtype)

    out_shape = [
        jax.ShapeDtypeStruct((SEQ, D_MODEL), MM),
        jax.ShapeDtypeStruct((SEQ, 2 * CONV_DIM), F32),
        jax.ShapeDtypeStruct((SEQ, ATT_DIM), MM),
        jax.ShapeDtypeStruct((SEQ, ATT_DIM), MM),
        jax.ShapeDtypeStruct((SEQ, ATT_DIM), MM),
        jax.ShapeDtypeStruct((SEQ, D_MODEL), F32),
        jax.ShapeDtypeStruct((SEQ, D_MODEL), F32),
    ]
    return pl.pallas_call(
        body, name="in_proj_fwd", grid=(SEQ // TM,), out_shape=out_shape,
        in_specs=[_row_tile_spec(D_MODEL), _full_spec((1, D_MODEL)), _weight_spec(w_in_g.shape), TOKEN_SPEC],
        out_specs=[_row_tile_spec(s.shape[1]) for s in out_shape],
        compiler_params=_params(("arbitrary",)),
    )(x, g1, w_in_g, after)


LANE_GROUPS = [slice(g, g + 128) for g in range(0, CONV_DIM, 128)]
NORM_ROWS = 16


def _shifted_windows(src_ref, t0, cols, offsets):
    win = src_ref[pl.ds(t0, CONV_WIN), cols]
    for rot in range(8):
        ms = [m for m in offsets if m % 8 == rot]
        if ms:
            shifted = win if rot == 0 else pltpu.roll(win, CONV_WIN - rot, 0)
            for m in ms:
                yield m, shifted[m - rot:m - rot + CONV_TILE, :]


def _shifted_sum(src_ref, t0, cols, w_ref, offset_of_tap):
    tap_at = {offset_of_tap(j): j for j in range(CONV_WIDTH)}
    acc = None
    for m, rows in _shifted_windows(src_ref, t0, cols, sorted(tap_at)):
        t = w_ref[tap_at[m]:tap_at[m] + 1, cols] * rows
        acc = t if acc is None else acc + t
    return acc


def _fetch(srcs, dsts, sems):
    copies = [pltpu.make_async_copy(s, d, sems.at[i]) for i, (s, d) in enumerate(zip(srcs, dsts))]
    for cp in copies:
        cp.start()
    return copies


def _row_chunks(src, dst, sems):
    def chunk(i):
        t0 = i * GLU_ROWS
        rows = pl.ds(t0 if isinstance(i, int) else pl.multiple_of(t0, GLU_ROWS), GLU_ROWS)
        return pltpu.make_async_copy(src.at[rows, :], dst.at[rows, :], sems.at[i])

    for i in range(SEQ // GLU_ROWS):
        chunk(i).start()
    return chunk


def _glu_into(ci_chunk, ci_ref, upad_ref):
    upad_ref[0:32, :] = jnp.zeros((32, CONV_DIM), F32)

    def step(i, c):
        ci_chunk(i).wait()
        t0 = pl.multiple_of(i * GLU_ROWS, GLU_ROWS)
        a = ci_ref[pl.ds(t0, GLU_ROWS), 0:CONV_DIM]
        b = ci_ref[pl.ds(t0, GLU_ROWS), CONV_DIM:2 * CONV_DIM]
        upad_ref[pl.ds(t0 + 32, GLU_ROWS), :] = a * _sigmoid(b)
        return c

    lax.fori_loop(0, SEQ // GLU_ROWS, step, 0)


def _layernorm_parts(u1):
    mu = jnp.mean(u1, axis=-1, keepdims=True)
    xc = u1 - mu
    rstd = lax.rsqrt(jnp.mean(xc * xc, axis=-1, keepdims=True) + EPS)
    return xc * rstd, rstd


def conv_fwd(ci, w_dw, b_dw, ln_g, ln_b):
    def body(ci_hbm, w_ref, b_ref, g_ref, bb_ref, u1_ref, u3_ref, upad_ref, ci_ref, sems):
        _glu_into(_row_chunks(ci_hbm, ci_ref, sems), ci_ref, upad_ref)

        def step(i, c):
            t0 = pl.multiple_of(i * CONV_TILE, CONV_TILE)
            for cols in LANE_GROUPS:
                u1_ref[pl.ds(t0, CONV_TILE), cols] = (_shifted_sum(upad_ref, t0, cols, w_ref, lambda j: j + 2)
                                                      + b_ref[:, cols])
            for r in range(0, CONV_TILE, NORM_ROWS):
                rows = pl.ds(t0 + r, NORM_ROWS)
                xh, _ = _layernorm_parts(u1_ref[rows, :])
                u2 = xh * g_ref[...] + bb_ref[...]
                u3_ref[rows, :] = (u2 * _sigmoid(u2)).astype(MM)
            return c

        lax.fori_loop(0, SEQ // CONV_TILE, step, 0)

    return pl.pallas_call(
        body, name="conv_fwd",
        out_shape=[jax.ShapeDtypeStruct((SEQ, CONV_DIM), F32), jax.ShapeDtypeStruct((SEQ, CONV_DIM), MM)],
        in_specs=[ANY] + [VMEM_SPEC] * 4, out_specs=[VMEM_SPEC] * 2,
        scratch_shapes=[pltpu.VMEM((SEQ + 32, CONV_DIM), F32), pltpu.VMEM(ci.shape, ci.dtype),
                        pltpu.SemaphoreType.DMA((SEQ // GLU_ROWS,))],
        compiler_params=_params(),
    )(ci, w_dw, b_dw, ln_g, ln_b)


def _softplus(z):
    return jnp.maximum(z, 0.0) + jnp.log(1.0 + jnp.exp(-jnp.abs(z)))


def _cumsum_weights(suffix, with_total):
    n = 256 if with_total else 128
    r = lax.broadcasted_iota(jnp.int32, (128, n), 0)
    c = lax.broadcasted_iota(jnp.int32, (128, n), 1)
    tri = (r >= c) if suffix else (r <= c)
    return jnp.logical_or(tri, c >= 128).astype(MM)


NO_SCORE = -1e30
N_KB = SEQ // TQ


def _score_bias(lane, row, i, j):
    keep = jnp.logical_and(i >= 0, jnp.logical_or(j < i, lane < row))
    return jnp.where(keep, 0.0, NO_SCORE)


def _block_pipeline(n_stages, descending, step, on_query_block=None):
    n_lag = n_stages - 1
    none = jnp.int32(-1)

    def shift(cur, lag):
        step([cur] + [(lag[2 * s], lag[2 * s + 1]) for s in range(n_lag)])
        return (cur[0], cur[1]) + tuple(lag[:-2])

    def outer(i, lag):
        if on_query_block is not None:
            on_query_block(i)

        def inner(n, lag):
            return shift((i, i - n if descending else n), lag)
        return lax.fori_loop(0, i + 1, inner, lag)

    lag = lax.fori_loop(0, N_KB, outer, (none,) * (2 * n_lag))
    lax.fori_loop(0, n_lag, lambda n, lag: shift((none, none), lag), lag)


def _head_masks():
    lane = lax.broadcasted_iota(jnp.int32, (TQ, 128), 1)
    row = lax.broadcasted_iota(jnp.int32, (TQ, 128), 0)
    return lane, row, lane < 64


def _pick_head(x, head0, h):
    zero = jnp.zeros_like(x)
    return jnp.where(head0, x, zero) if h == 0 else jnp.where(head0, zero, x)


N_PAIRS = ATT_DIM // 128


def _split_heads(src_ref, dst_ref):
    _, _, head0 = _head_masks()

    def block(b, c):
        r0 = pl.multiple_of(b * TQ, TQ)
        d0 = pl.multiple_of(b * 2 * TQ, 2 * TQ)
        for p in range(N_PAIRS):
            x = src_ref[pl.ds(r0, TQ), 128 * p:128 * (p + 1)]
            for h in range(2):
                dst_ref[p, pl.ds(d0 + TQ * h, TQ), :] = _pick_head(x, head0, h)
        return c

    lax.fori_loop(0, N_KB, block, 0)


def attn_fwd(q, k, v):
    def body(q_hbm, k_hbm, v_hbm, o_ref, rc_ref, acc_ref, r_ref, z_ref, spb_ref, ab_ref, qm_ref, vm_ref,
             q_ref, k_ref, v_ref, sems):
        arrive = _fetch((q_hbm, v_hbm, k_hbm), (q_ref, v_ref, k_ref), sems)
        lane, row, _ = _head_masks()
        w = _cumsum_weights(suffix=True, with_total=True)
        acc_ref[...] = jnp.zeros_like(acc_ref)
        r_ref[...] = jnp.zeros_like(r_ref)
        rc_ref[...] = jnp.zeros_like(rc_ref)
        z_ref[...] = jnp.full(z_ref.shape, NO_SCORE, F32)
        spb_ref[...] = jnp.zeros_like(spb_ref)
        ab_ref[...] = jnp.zeros_like(ab_ref)
        arrive[0].wait()
        _split_heads(q_ref, qm_ref)
        arrive[1].wait()
        _split_heads(v_ref, vm_ref)
        arrive[2].wait()

        def step(pairs):
            (i1, j1), (i2, j2), (i3, j3) = pairs
            k1, q2, q3 = (pl.multiple_of(jnp.maximum(b, 0) * TQ, TQ) for b in (j1, i2, i3))
            q1, k3 = (pl.multiple_of(jnp.maximum(b, 0) * 2 * TQ, 2 * TQ) for b in (i1, j3))
            bias1 = _score_bias(lane, row, i1, j1)
            first2 = j2 == i2
            rc_rows = rc_ref[pl.ds(q2, TQ), :]
            for p in range(N_PAIRS):
                cols = slice(128 * p, 128 * (p + 1))
                kb = k_ref[pl.ds(k1, TQ), cols]
                acc_ref[pl.ds(q3, TQ), cols] += _dot(ab_ref[p], vm_ref[p, pl.ds(k3, 2 * TQ), :])
                for h in range(2):
                    hh = 2 * p + h
                    r = _dot(spb_ref[hh], w)
                    r_in = jnp.where(first2, 0.0, r_ref[hh])
                    ab_ref[p, :, 128 * h:128 * (h + 1)] = jnp.exp(z_ref[hh] - (r[:, :128] + r_in)).astype(MM)
                    rc_rows = jnp.where(jnp.logical_and(lane == 16 * hh + j2, i2 >= 0), r_in, rc_rows)
                    r_ref[hh] = r_in + r[:, 128:]
                    z = _dot_nt(qm_ref[p, pl.ds(q1 + TQ * h, TQ), :], kb) + bias1
                    z_ref[hh] = z
                    spb_ref[hh] = _softplus(z).astype(MM)
            rc_ref[pl.ds(q2, TQ), :] = rc_rows

        _block_pipeline(3, True, step)
        o_ref[...] = acc_ref[...].astype(MM)

    return pl.pallas_call(
        body, name="attn_fwd",
        out_shape=[jax.ShapeDtypeStruct((SEQ, ATT_DIM), MM), jax.ShapeDtypeStruct((SEQ, 128), F32)],
        in_specs=[ANY] * 3, out_specs=[VMEM_SPEC] * 2,
        scratch_shapes=[pltpu.VMEM((SEQ, ATT_DIM), F32), pltpu.VMEM((8, TQ, 128), F32),
                        pltpu.VMEM((8, TQ, 128), F32), pltpu.VMEM((8, TQ, 128), MM),
                        pltpu.VMEM((N_PAIRS, TQ, 256), MM), pltpu.VMEM((N_PAIRS, 2 * SEQ, 128), MM),
                        pltpu.VMEM((N_PAIRS, 2 * SEQ, 128), MM)]
                       + [pltpu.VMEM(a.shape, a.dtype) for a in (q, k, v)] + [pltpu.SemaphoreType.DMA((3,))],
        compiler_params=_params(),
    )(q, k, v)


def _branch_outputs(u_ref, a_ref, wcb_ref, bcb_ref, wab_ref):
    u = u_ref[...]
    a = a_ref[...]
    co = jnp.concatenate([_dot(u, wcb_ref[j]) for j in range(N_CHIPS)], axis=1) + bcb_ref[...]
    ao = jnp.concatenate([_dot(a, wab_ref[j]) for j in range(N_CHIPS)], axis=1)
    return co, ao


def mix_fwd(u3, att, gc, ga, x, w_cb_g, b_cb, w_ab_g, w_out_g, g2, g3, after):
    def body(u_ref, a_ref, gc_ref, ga_ref, x_ref, wcb_ref, bcb_ref, wab_ref, wout_ref, g2_ref, g3_ref, after_ref,
             mg_ref, mix_ref, x2_ref, h2_ref):
        co, ao = _branch_outputs(u_ref, a_ref, wcb_ref, bcb_ref, wab_ref)
        merged = (_sigmoid(gc_ref[...]) * co + _sigmoid(ga_ref[...]) * ao).astype(MM)
        mg_ref[...] = merged
        mix = _dot(merged, wout_ref[...])
        mix_ref[...] = mix
        n2, _ = _rms(mix)
        x2 = x_ref[...] + n2 * g2_ref[...]
        x2_ref[...] = x2
        n3, _ = _rms(x2)
        h2_ref[...] = (n3 * g3_ref[...]).astype(MM)

    out_shape = [
        jax.ShapeDtypeStruct((SEQ, D_MODEL), MM), jax.ShapeDtypeStruct((SEQ, D_MODEL), F32),
        jax.ShapeDtypeStruct((SEQ, D_MODEL), F32), jax.ShapeDtypeStruct((SEQ, D_MODEL), MM),
    ]
    vec = _full_spec((1, D_MODEL))
    return pl.pallas_call(
        body, name="mix_fwd", grid=(SEQ // TM,), out_shape=out_shape,
        in_specs=[_row_tile_spec(CONV_DIM), _row_tile_spec(ATT_DIM), _row_tile_spec(D_MODEL),
                  _row_tile_spec(D_MODEL), _row_tile_spec(D_MODEL), _weight_spec(w_cb_g.shape), vec,
                  _weight_spec(w_ab_g.shape), _weight_spec(w_out_g.shape), vec, vec, TOKEN_SPEC],
        out_specs=[_row_tile_spec(D_MODEL)] * 4,
        compiler_params=_params(("arbitrary",)),
    )(u3, att, gc, ga, x, w_cb_g, b_cb, w_ab_g, w_out_g, g2, g3, after)


def ffn_up_fwd(h2, w_up_g):
    def body(h_ref, wg_ref, wu_ref, gate_ref, up_ref, act_ref):
        h = h_ref[...]
        gate = _dot(h, wg_ref[0])
        up = _dot(h, wu_ref[0])
        gate_ref[...] = gate.astype(MM)
        up_ref[...] = up.astype(MM)
        act_ref[...] = (gate * _sigmoid(gate) * up).astype(MM)

    tile = pl.BlockSpec((TM, UP_SHARD), lambda n, i: (i, n))
    act = jax.ShapeDtypeStruct((SEQ, D_FF), MM)
    return pl.pallas_call(
        body, name="ffn_up_fwd", grid=(2, SEQ // TM), out_shape=[act, act, act],
        in_specs=[pl.BlockSpec((TM, D_MODEL), lambda n, i: (i, 0)),
                  pl.BlockSpec((1, D_MODEL, UP_SHARD), lambda n, i: (n, 0, 0)),
                  pl.BlockSpec((1, D_MODEL, UP_SHARD), lambda n, i: (n + 2, 0, 0))],
        out_specs=[tile, tile, tile],
        compiler_params=_params(("arbitrary", "arbitrary")),
    )(h2, w_up_g, w_up_g)


def ffn_down_loss(act, w_down_g, x2, target, g4):
    def body(act_ref, wd_ref, x2_ref, t_ref, g_ref, dff_ref, dy_ref, loss_ref, dg_ref):
        ff = _dot(act_ref[...], wd_ref[...])
        n4, r4 = _rms(ff)
        g4v = g_ref[...]
        err = x2_ref[...] + n4 * g4v - t_ref[...]
        row_loss = jnp.mean(err * err, axis=-1, keepdims=True)
        loss_ref[...] = jnp.zeros((8, 128), F32) + 0.5 * jnp.sum(row_loss, axis=0, keepdims=True)
        dy = err * (1.0 / D_MODEL)
        dy_ref[...] = dy
        dff_ref[...] = _rms_bwd(dy * g4v, n4, r4).astype(MM)
        _acc_rows(dg_ref, jnp.sum(dy * n4, axis=0, keepdims=True), pl.program_id(0) == 0)

    nt = SEQ // TM
    vec = _full_spec((1, D_MODEL))
    return pl.pallas_call(
        body, name="ffn_down_loss", grid=(nt,),
        out_shape=(jax.ShapeDtypeStruct((SEQ, D_MODEL), MM), jax.ShapeDtypeStruct((SEQ, D_MODEL), F32),
                   jax.ShapeDtypeStruct((nt * 8, 128), F32), jax.ShapeDtypeStruct((1, D_MODEL), F32)),
        in_specs=[_row_tile_spec(D_FF), _weight_spec(w_down_g.shape), _row_tile_spec(D_MODEL),
                  _row_tile_spec(D_MODEL), vec],
        out_specs=[_row_tile_spec(D_MODEL), _row_tile_spec(D_MODEL),
                   pl.BlockSpec((8, 128), lambda i: (i, 0)), vec],
        compiler_params=_params(("arbitrary",)),
    )(act, w_down_g, x2, target, g4)


def ffn_act_bwd(dff, w_down_g, gate, up):
    def body(dff_ref, wd_ref, gate_ref, up_ref, dgu_ref):
        dact = _dot_nt(dff_ref[...], wd_ref[...])
        gate = gate_ref[...].astype(F32)
        sg = _sigmoid(gate)
        dgu_ref[:, 0:D_FF] = (dact * up_ref[...].astype(F32) * (sg * (1.0 + gate * (1.0 - sg)))).astype(MM)
        dgu_ref[:, D_FF:2 * D_FF] = (dact * (gate * sg)).astype(MM)

    return pl.pallas_call(
        body, name="ffn_act_bwd", grid=(SEQ // TM,),
        out_shape=jax.ShapeDtypeStruct((SEQ, 2 * D_FF), MM),
        in_specs=[_row_tile_spec(D_MODEL), _weight_spec(w_down_g.shape), _row_tile_spec(D_FF), _row_tile_spec(D_FF)],
        out_specs=_row_tile_spec(2 * D_FF),
        compiler_params=_params(("arbitrary",)),
    )(dff, w_down_g, gate, up)


def ffn_in_bwd(dgu, w_up_g, x2, mix, dy, g3, g2):
    def body(dgu_ref, w_ref, x2_ref, mix_ref, dy_ref, g3_ref, g2_ref, dx2_ref, dmix_ref, dg3_ref, dg2_ref):
        dh2 = None
        for j in range(N_CHIPS):
            t = _dot_nt(dgu_ref[:, j * UP_SHARD:(j + 1) * UP_SHARD], w_ref[j])
            dh2 = t if dh2 is None else dh2 + t
        first = pl.program_id(0) == 0
        n3, r3 = _rms(x2_ref[...])
        dx2 = dy_ref[...] + _rms_bwd(dh2 * g3_ref[...], n3, r3)
        dx2_ref[...] = dx2
        _acc_rows(dg3_ref, jnp.sum(dh2 * n3, axis=0, keepdims=True), first)
        n2, r2 = _rms(mix_ref[...])
        dmix_ref[...] = _rms_bwd(dx2 * g2_ref[...], n2, r2).astype(MM)
        _acc_rows(dg2_ref, jnp.sum(dx2 * n2, axis=0, keepdims=True), first)

    vec = _full_spec((1, D_MODEL))
    return pl.pallas_call(
        body, name="ffn_in_bwd", grid=(SEQ // TM,),
        out_shape=(jax.ShapeDtypeStruct((SEQ, D_MODEL), F32), jax.ShapeDtypeStruct((SEQ, D_MODEL), MM),
                   jax.ShapeDtypeStruct((1, D_MODEL), F32), jax.ShapeDtypeStruct((1, D_MODEL), F32)),
        in_specs=[_row_tile_spec(2 * D_FF), _weight_spec(w_up_g.shape), _row_tile_spec(D_MODEL),
                  _row_tile_spec(D_MODEL), _row_tile_spec(D_MODEL), vec, vec],
        out_specs=[_row_tile_spec(D_MODEL), _row_tile_spec(D_MODEL), vec, vec],
        compiler_params=_params(("arbitrary",)),
    )(dgu, w_up_g, x2, mix, dy, g3, g2)


def merge_bwd(dmix, w_out_g, gc, ga, u3, att, w_cb_g, b_cb, w_ab_g, after):
    def body(dmix_ref, wout_ref, gc_ref, ga_ref, u_ref, a_ref, wcb_ref, bcb_ref, wab_ref, after_ref,
             dco_ref, dao_ref, dg_ref, du3_ref, datt_ref, dbcb_ref):
        dm = _dot_nt(dmix_ref[...], wout_ref[...])
        co, ao = _branch_outputs(u_ref, a_ref, wcb_ref, bcb_ref, wab_ref)
        sgc = _sigmoid(gc_ref[...])
        sga = _sigmoid(ga_ref[...])
        dco = dm * sgc
        dao = dm * sga
        dg_ref[:, 0:D_MODEL] = (dm * co * (sgc * (1.0 - sgc))).astype(MM)
        dg_ref[:, D_MODEL:2 * D_MODEL] = (dm * ao * (sga * (1.0 - sga))).astype(MM)
        _acc_rows(dbcb_ref, jnp.sum(dco, axis=0, keepdims=True), pl.program_id(0) == 0)
        dco_ref[...] = dco.astype(MM)
        dao_ref[...] = dao.astype(MM)
        du3 = None
        datt = None
        for j in range(N_CHIPS):
            cols = slice(j * BR_SHARD, (j + 1) * BR_SHARD)
            t = _dot_nt(dco_ref[:, cols], wcb_ref[j])
            s = _dot_nt(dao_ref[:, cols], wab_ref[j])
            du3 = t if du3 is None else du3 + t
            datt = s if datt is None else datt + s
        du3_ref[...] = du3
        datt_ref[...] = datt.astype(MM)

    wide = _row_tile_spec(D_MODEL)
    return pl.pallas_call(
        body, name="merge_bwd", grid=(SEQ // TM,),
        out_shape=(jax.ShapeDtypeStruct((SEQ, D_MODEL), MM), jax.ShapeDtypeStruct((SEQ, D_MODEL), MM),
                   jax.ShapeDtypeStruct((SEQ, 2 * D_MODEL), MM),
                   jax.ShapeDtypeStruct((SEQ, CONV_DIM), F32), jax.ShapeDtypeStruct((SEQ, ATT_DIM), MM),
                   jax.ShapeDtypeStruct((1, D_MODEL), F32)),
        in_specs=[wide, _weight_spec(w_out_g.shape), wide, wide, _row_tile_spec(CONV_DIM), _row_tile_spec(ATT_DIM),
                  _weight_spec(w_cb_g.shape), _full_spec((1, D_MODEL)), _weight_spec(w_ab_g.shape), TOKEN_SPEC],
        out_specs=[wide, wide, _row_tile_spec(2 * D_MODEL), _row_tile_spec(CONV_DIM), _row_tile_spec(ATT_DIM),
                   _full_spec((1, D_MODEL))],
        compiler_params=_params(("arbitrary",)),
    )(dmix, w_out_g, gc, ga, u3, att, w_cb_g, b_cb, w_ab_g, after)


def conv_bwd(du3, u1, ci, w_dw, ln_g, ln_b, after):
    def body(du3_hbm, u1_hbm, ci_hbm, w_ref, g_ref, bb_ref, after_ref,
             dci_ref, dw_ref, dbdw_ref, dg_ref, db_ref, upad_ref, dpad_ref, dwacc_ref, vacc_ref,
             du3_ref, u1_ref, ci_ref, ci_sems, u1_sems, du3_sems):
        ci_chunk = _row_chunks(ci_hbm, ci_ref, ci_sems)
        u1_chunk = _row_chunks(u1_hbm, u1_ref, u1_sems)
        du3_chunk = _row_chunks(du3_hbm, du3_ref, du3_sems)
        _glu_into(ci_chunk, ci_ref, upad_ref)
        dpad_ref[SEQ:SEQ + 32, :] = jnp.zeros((32, CONV_DIM), F32)
        dwacc_ref[...] = jnp.zeros_like(dwacc_ref)
        vacc_ref[...] = jnp.zeros_like(vacc_ref)

        def fold8(t):
            s = t[0:8, :]
            for r in range(8, t.shape[0], 8):
                s = s + t[r:r + 8, :]
            return s

        def pass1(i, c):
            t0 = pl.multiple_of(i * CONV_TILE, CONV_TILE)
            gv = g_ref[...]
            for r in range(0, CONV_TILE, NORM_ROWS):
                rows = pl.ds(t0 + r, NORM_ROWS)
                xh, rstd = _layernorm_parts(u1_ref[rows, :])
                u2 = xh * gv + bb_ref[...]
                s2 = _sigmoid(u2)
                du2 = du3_ref[rows, :] * (s2 * (1.0 + u2 * (1.0 - s2)))
                wv = du2 * gv
                du1 = rstd * (wv - jnp.mean(wv, axis=-1, keepdims=True)
                              - xh * jnp.mean(wv * xh, axis=-1, keepdims=True))
                dpad_ref[rows, :] = du1
                vacc_ref[0] += fold8(du2 * xh)
                vacc_ref[1] += fold8(du2)
                vacc_ref[2] += fold8(du1)
            for cols in LANE_GROUPS:
                du1 = dpad_ref[pl.ds(t0, CONV_TILE), cols]
                for m, rows in _shifted_windows(upad_ref, t0, cols, range(2, CONV_WIDTH + 2)):
                    dwacc_ref[m - 2, :, cols] += fold8(du1 * rows)
            return c

        tiles_per_chunk = GLU_ROWS // CONV_TILE

        def pass1_chunk(ch, c):
            u1_chunk(ch).wait()
            du3_chunk(ch).wait()
            return lax.fori_loop(ch * tiles_per_chunk, (ch + 1) * tiles_per_chunk, pass1, c)

        lax.fori_loop(0, SEQ // GLU_ROWS, pass1_chunk, 0)

        def pass2(i, c):
            t0 = pl.multiple_of(i * CONV_TILE, CONV_TILE)
            tile = pl.ds(t0, CONV_TILE)
            for cols in LANE_GROUPS:
                gate_cols = slice(cols.start + CONV_DIM, cols.stop + CONV_DIM)
                du0 = _shifted_sum(dpad_ref, t0, cols, w_ref, lambda j: 30 - j)
                a = ci_ref[tile, cols]
                sb = _sigmoid(ci_ref[tile, gate_cols])
                dci_ref[tile, cols] = (du0 * sb).astype(MM)
                dci_ref[tile, gate_cols] = (du0 * a * (sb * (1.0 - sb))).astype(MM)
            return c

        lax.fori_loop(0, SEQ // CONV_TILE, pass2, 0)

        for j in range(CONV_WIDTH):
            dw_ref[j:j + 1, :] = jnp.sum(dwacc_ref[j], axis=0, keepdims=True)
        dw_ref[CONV_WIDTH:32, :] = jnp.zeros((32 - CONV_WIDTH, CONV_DIM), F32)
        dg_ref[...] = jnp.sum(vacc_ref[0], axis=0, keepdims=True)
        db_ref[...] = jnp.sum(vacc_ref[1], axis=0, keepdims=True)
        dbdw_ref[...] = jnp.sum(vacc_ref[2], axis=0, keepdims=True)

    vec = jax.ShapeDtypeStruct((1, CONV_DIM), F32)
    return pl.pallas_call(
        body, name="conv_bwd",
        out_shape=(jax.ShapeDtypeStruct((SEQ, 2 * CONV_DIM), MM), jax.ShapeDtypeStruct((32, CONV_DIM), F32),
                   vec, vec, vec),
        in_specs=[ANY] * 3 + [VMEM_SPEC] * 4, out_specs=[VMEM_SPEC] * 5,
        scratch_shapes=[pltpu.VMEM((SEQ + 32, CONV_DIM), F32), pltpu.VMEM((SEQ + 32, CONV_DIM), F32),
                        pltpu.VMEM((CONV_WIDTH, 8, CONV_DIM), F32), pltpu.VMEM((3, 8, CONV_DIM), F32)]
                       + [pltpu.VMEM(a.shape, a.dtype) for a in (du3, u1, ci)]
                       + [pltpu.SemaphoreType.DMA((SEQ // GLU_ROWS,))] * 3,
        compiler_params=_params(),
    )(du3, u1, ci, w_dw, ln_g, ln_b, after)


def attn_bwd(q, k, v, datt, rc, after):
    def body(q_hbm, k_hbm, v_hbm, do_hbm, rc_hbm, after_ref, dqkv_ref, dqa_ref, dka_ref, dva_ref, pc_ref, z_ref,
             sig1_ref, sig2_ref, g_ref, spb_ref, gb_ref, ar_ref, dzr_ref, dzc_ref, qm_ref, km_ref, dom_ref,
             q_ref, k_ref, v_ref, do_ref, rc_ref, sems):
        arrive = _fetch((q_hbm, k_hbm, do_hbm, v_hbm, rc_hbm), (q_ref, k_ref, do_ref, v_ref, rc_ref), sems)
        lane, row, _ = _head_masks()
        for ref in (dqa_ref, dka_ref, dva_ref, pc_ref):
            ref[...] = jnp.zeros_like(ref)
        z_ref[...] = jnp.full(z_ref.shape, NO_SCORE, F32)
        for ref in (sig1_ref, sig2_ref, spb_ref, ar_ref, g_ref, gb_ref, dzr_ref, dzc_ref):
            ref[...] = jnp.zeros_like(ref)
        for cp, (src_ref, split_ref) in zip(arrive, ((q_ref, qm_ref), (k_ref, km_ref), (do_ref, dom_ref))):
            cp.wait()
            _split_heads(src_ref, split_ref)
        arrive[3].wait()
        arrive[4].wait()
        w_suffix = _cumsum_weights(suffix=True, with_total=False)
        w_prefix = _cumsum_weights(suffix=False, with_total=True)

        def step(pairs):
            (ia, ja), (ib, jb), (ic, jc), (id_, jd) = pairs
            ka, qb_, kb_, kc, qd, kd = (pl.multiple_of(jnp.maximum(b, 0) * TQ, TQ) for b in (ja, ib, jb, jc, id_, jd))
            qa2, qb2, qc2, qd2, kd2 = (pl.multiple_of(jnp.maximum(b, 0) * 2 * TQ, 2 * TQ)
                                       for b in (ia, ib, ic, id_, jd))
            bias_a = _score_bias(lane, row, ia, ja)
            rc_rows = rc_ref[pl.ds(qb_, TQ), :]
            first_c = jc == 0
            for p in range(N_PAIRS):
                cols = slice(128 * p, 128 * (p + 1))
                k_a = k_ref[pl.ds(ka, TQ), cols]
                v_b = v_ref[pl.ds(kb_, TQ), cols]
                dqa_ref[pl.ds(qd, TQ), cols] += _dot(dzc_ref[p], km_ref[p, pl.ds(kd2, 2 * TQ), :])
                dka_ref[pl.ds(kd, TQ), cols] += _dot_tn(dzr_ref[p], qm_ref[p, pl.ds(qd2, 2 * TQ), :])
                dva_ref[pl.ds(kc, TQ), cols] += _dot_tn(ar_ref[p], dom_ref[p, pl.ds(qc2, 2 * TQ), :])
                for h in range(2):
                    hh = 2 * p + h
                    rows = slice(TQ * h, TQ * (h + 1))
                    r = _dot(gb_ref[hh], w_prefix)
                    p_in = jnp.where(first_c, 0.0, pc_ref[hh])
                    dz = (g_ref[hh] - sig2_ref[hh] * (r[:, :128] + p_in)).astype(MM)
                    dzc_ref[p, :, rows] = dz
                    dzr_ref[p, rows, :] = dz
                    pc_ref[hh] = p_in + r[:, 128:]
                    r_in = jnp.sum(jnp.where(lane == 16 * hh + jb, rc_rows, 0.0), axis=1, keepdims=True)
                    a = jnp.exp(z_ref[hh] - (_dot(spb_ref[hh], w_suffix) + r_in))
                    g = _dot_nt(dom_ref[p, pl.ds(qb2 + TQ * h, TQ), :], v_b) * a
                    ar_ref[p, rows, :] = a.astype(MM)
                    g_ref[hh] = g
                    gb_ref[hh] = g.astype(MM)
                    sig2_ref[hh] = sig1_ref[hh]
                    z = _dot_nt(qm_ref[p, pl.ds(qa2 + TQ * h, TQ), :], k_a) + bias_a
                    sp = _softplus(z)
                    sig1_ref[hh] = jnp.exp(z - sp)
                    z_ref[hh] = z
                    spb_ref[hh] = sp.astype(MM)

        _block_pipeline(4, False, step)
        dqkv_ref[:, 0:ATT_DIM] = (dqa_ref[...] * ATT_SCALE).astype(MM)
        dqkv_ref[:, ATT_DIM:2 * ATT_DIM] = dka_ref[...].astype(MM)
        dqkv_ref[:, 2 * ATT_DIM:3 * ATT_DIM] = dva_ref[...].astype(MM)

    split = pltpu.VMEM((N_PAIRS, 2 * SEQ, 128), MM)
    return pl.pallas_call(
        body, name="attn_bwd", out_shape=jax.ShapeDtypeStruct((SEQ, 3 * ATT_DIM), MM),
        in_specs=[ANY] * 5 + [VMEM_SPEC], out_specs=VMEM_SPEC,
        scratch_shapes=[pltpu.VMEM((SEQ, ATT_DIM), F32)] * 3 + [pltpu.VMEM((8, TQ, 128), F32)] * 5
                       + [pltpu.VMEM((8, TQ, 128), MM)] * 2
                       + [pltpu.VMEM((N_PAIRS, 2 * TQ, 128), MM)] * 2 + [pltpu.VMEM((N_PAIRS, TQ, 256), MM)]
                       + [split] * 3
                       + [pltpu.VMEM(a.shape, a.dtype) for a in (q, k, v, datt, rc)] + [pltpu.SemaphoreType.DMA((5,))],
        compiler_params=_params(),
    )(q, k, v, datt, rc, after)


DPROJ_PIECES = ((0, 1024), (1024, 2560), (2560, 4608))


def _dproj_segments(j):
    g0, g1 = j * IN_SHARD, (j + 1) * IN_SHARD
    segs = []
    for p, (s, e) in enumerate(DPROJ_PIECES):
        lo, hi = max(s, g0), min(e, g1)
        if lo < hi:
            segs.append((p, lo - s, lo - g0, hi - lo))
    return segs


def in_proj_bwd(pieces, w_in_g, x, dx2, g1, after):
    def body(p0_ref, p1_ref, p2_ref, w_ref, x_ref, dx2_ref, g_ref, after_ref, dx_ref, dg_ref):
        p_refs = (p0_ref, p1_ref, p2_ref)
        dh = None
        for j in range(N_CHIPS):
            for p, lo, off, width in _dproj_segments(j):
                t = _dot_nt(p_refs[p][:, lo:lo + width], w_ref[j, :, off:off + width])
                dh = t if dh is None else dh + t
        n1, r1 = _rms(x_ref[...])
        dx_ref[...] = dx2_ref[...] + _rms_bwd(dh * g_ref[...], n1, r1)
        _acc_rows(dg_ref, jnp.sum(dh * n1, axis=0, keepdims=True), pl.program_id(0) == 0)

    vec = _full_spec((1, D_MODEL))
    return pl.pallas_call(
        body, name="in_proj_bwd", grid=(SEQ // TM,),
        out_shape=[jax.ShapeDtypeStruct((SEQ, D_MODEL), F32), jax.ShapeDtypeStruct((1, D_MODEL), F32)],
        in_specs=[_row_tile_spec(p.shape[1]) for p in pieces]
                 + [_weight_spec(w_in_g.shape), _row_tile_spec(D_MODEL), _row_tile_spec(D_MODEL), vec, TOKEN_SPEC],
        out_specs=[_row_tile_spec(D_MODEL), vec],
        compiler_params=_params(("arbitrary",)),
    )(*pieces, w_in_g, x, dx2, g1, after)


def weight_grad_in(h1, pieces):
    kh = D_MODEL // 2

    def body(a_ref, p0_ref, p1_ref, p2_ref, o_ref):
        p_refs = (p0_ref, p1_ref, p2_ref)
        a = a_ref[...]
        for j in range(N_CHIPS):
            @pl.when(pl.program_id(1) == j)
            def _():
                for p, lo, off, width in _dproj_segments(j):
                    o_ref[0, 0, :, off:off + width] = _dot_tn(a, p_refs[p][:, lo:lo + width]).astype(MM)

    return pl.pallas_call(
        body, name="dw_in", grid=(2, N_CHIPS), out_shape=jax.ShapeDtypeStruct((N_CHIPS, 2, kh, IN_SHARD), MM),
        in_specs=[pl.BlockSpec((SEQ, kh), lambda h, j: (0, h))] + [_weight_spec(p.shape) for p in pieces],
        out_specs=pl.BlockSpec((1, 1, kh, IN_SHARD), lambda h, j: (j, h, 0, 0)),
        compiler_params=_params(("arbitrary", "arbitrary")),
    )(h1, *pieces)


def weight_grad(a, b, name, col_sharded, tk=None):
    kin, n = a.shape[1], b.shape[1]

    def body(a_ref, b_ref, o_ref):
        if col_sharded:
            o_ref[0, 0] = _dot_tn(a_ref[...], b_ref[...]).astype(MM)
        else:
            o_ref[...] = _dot_tn(a_ref[...], b_ref[...]).astype(MM)

    if col_sharded:
        kh, ns = kin // 2, n // N_CHIPS
        out = jax.ShapeDtypeStruct((N_CHIPS, 2, kh, ns), MM)
        grid = (2, N_CHIPS)
        in_specs = [pl.BlockSpec((SEQ, kh), lambda h, j: (0, h)), pl.BlockSpec((SEQ, ns), lambda h, j: (0, j))]
        out_spec = pl.BlockSpec((1, 1, kh, ns), lambda h, j: (j, h, 0, 0))
        sem = ("arbitrary", "arbitrary")
    else:
        out = jax.ShapeDtypeStruct((kin, n), MM)
        grid = (kin // tk,)
        in_specs = [pl.BlockSpec((SEQ, tk), lambda r: (0, r)), pl.BlockSpec((SEQ, n), lambda r: (0, 0))]
        out_spec = pl.BlockSpec((tk, n), lambda r: (r, 0))
        sem = ("arbitrary",)
    res = pl.pallas_call(
        body, name=name, grid=grid, out_shape=out, in_specs=in_specs, out_specs=out_spec,
        compiler_params=_params(sem),
    )(a, b)
    if not col_sharded:
        res = res.reshape(N_CHIPS, 2, kin // (2 * N_CHIPS), n)
    return res


def weight_grad_mix(merged, dmix, u3, dco, att, dao):
    operands = (merged, dmix, u3, dco, att, dao)
    n_out, n_br = D_MODEL // 2, CONV_DIM // 2

    def body(*refs):
        hbm, (o_out, o_cb, o_ab), bufs, sems = refs[:6], refs[6:9], refs[9:15], refs[15]
        copies = [pltpu.make_async_copy(hbm[i], bufs[i], sems.at[i]) for i in range(6)]
        for cp in copies:
            cp.start()
        m_ref, dm_ref, u_ref, dco_ref, a_ref, dao_ref = bufs
        copies[0].wait()
        copies[1].wait()
        for h in range(2):
            o_out[h * n_out:(h + 1) * n_out, :] = _dot_tn(m_ref[:, h * n_out:(h + 1) * n_out], dm_ref[...]).astype(MM)
        for br, (a, d, o) in enumerate(((u_ref, dco_ref, o_cb), (a_ref, dao_ref, o_ab))):
            copies[2 + 2 * br].wait()
            copies[3 + 2 * br].wait()
            for h in range(2):
                g = _dot_tn(a[:, h * n_br:(h + 1) * n_br], d[...])
                for j in range(N_CHIPS):
                    o[j, h] = g[:, j * BR_SHARD:(j + 1) * BR_SHARD].astype(MM)

    branch = jax.ShapeDtypeStruct((N_CHIPS, 2, n_br, BR_SHARD), MM)
    dw_out, dw_cb, dw_ab = pl.pallas_call(
        body, name="dw_mix", out_shape=[jax.ShapeDtypeStruct((D_MODEL, D_MODEL), MM), branch, branch],
        in_specs=[ANY] * 6, out_specs=[VMEM_SPEC] * 3,
        scratch_shapes=[pltpu.VMEM(a.shape, a.dtype) for a in operands] + [pltpu.SemaphoreType.DMA((6,))],
        compiler_params=_params(),
    )(*operands)
    return dw_out.reshape(N_CHIPS, 2, D_MODEL // (2 * N_CHIPS), D_MODEL), dw_cb, dw_ab


def _place():
    x, y, c = lax.axis_index("x"), lax.axis_index("y"), lax.axis_index("c")
    chips = [(1 - x, y), (x, 1 - y), (1 - x, 1 - y)]
    return x, y, c, chips


def _rcopy(src, dst, send_sem, recv_sem, dev):
    return pltpu.make_async_remote_copy(src_ref=src, dst_ref=dst, send_sem=send_sem, recv_sem=recv_sem,
                                        device_id=dev, device_id_type=MESH)


class _Gather:
    N_MOVES = 6

    def __init__(self, shapes, w, o, scratch):
        self.n, self.shapes, self.w, self.o = len(w), shapes, w, o
        self.send, self.recv, self.psend, self.precv, self.loc_in, self.loc_out = scratch[:6]
        self.raw, self.stage = scratch[6:6 + self.n], scratch[6 + self.n:]
        x, y, c, self.chips = _place()
        self.c = c
        self.me, k_x, k_y, k_far = 2 * x + y, 2 * (1 - x) + y, 2 * x + (1 - y), 2 * (1 - x) + (1 - y)
        to_x, to_y = (1 - x, y, c), (x, 1 - y, c)
        self.sib = (x, y, 1 - c)
        self.sent_as = [(self.me, 0, to_x), (self.me, 1, to_y), (self.me, 1, to_x), (self.me, 0, to_y),
                        (k_x, 0, to_y), (k_y, 1, to_x)]
        self.arrives_as = [(k_x, 0, to_x), (k_y, 1, to_y), (k_x, 1, to_x), (k_y, 0, to_y),
                           (k_far, 0, to_y), (k_far, 1, to_x)]
        self.sent_on_after = {0: 4, 1: 5}

    @staticmethod
    def scratch(shards):
        n = len(shards)
        sems = pltpu.SemaphoreType.DMA
        m = _Gather.N_MOVES * n
        return ([sems((m,)), sems((m,)), sems((m,)), sems((m,)), sems((3 * n,)), sems((n,))]
                + [pltpu.VMEM(s.shape, s.dtype) for s in shards] + [pltpu.VMEM(s.shape, MM) for s in shards])

    @staticmethod
    def out_shapes(shards):
        return [jax.ShapeDtypeStruct((N_CHIPS,) + s.shape, MM) for s in shards]

    def _rows(self, t, quarter, cc):
        rq = self.shapes[t][0] // 4
        return pl.ds(pl.multiple_of((2 * cc + quarter) * rq, rq), rq)

    def _own_rows(self, t, piece):
        if piece < 2:
            return self._rows(t, piece, self.c)
        rh = self.shapes[t][0] // 2
        return pl.ds(pl.multiple_of((1 - self.c) * rh, rh), rh)

    def _chip(self, j):
        cx, cy = self.chips[j]
        return 2 * cx + cy, (cx, cy, self.c)

    def local_in(self, t, piece):
        rows = self._own_rows(t, piece)
        return pltpu.make_async_copy(self.w[t].at[rows, :], self.raw[t].at[rows, :], self.loc_in.at[3 * t + piece])

    def local_out(self, t):
        return pltpu.make_async_copy(self.stage[t], self.o[t].at[self.me], self.loc_out.at[t])

    def sent(self, i, t):
        k, quarter, dev = self.sent_as[i]
        rows = self._rows(t, quarter, self.c)
        there = self.o[t].at[k, rows, :]
        return _rcopy(self.stage[t].at[rows, :] if i < 4 else there, there,
                      self.send.at[i * self.n + t], self.recv.at[i * self.n + t], dev)

    def arrived(self, i, t):
        k, quarter, dev = self.arrives_as[i]
        blk = self.o[t].at[k, self._rows(t, quarter, self.c), :]
        return _rcopy(blk, blk, self.send.at[i * self.n + t], self.recv.at[i * self.n + t], dev)

    def passed(self, i, t, cc):
        k, quarter, _ = self.arrives_as[i]
        blk = self.o[t].at[k, self._rows(t, quarter, cc), :]
        return _rcopy(blk, blk, self.psend.at[i * self.n + t], self.precv.at[i * self.n + t], self.sib)

    def start(self):
        for piece in range(3):
            for t in range(self.n):
                self.local_in(t, piece).start()
        for piece, moves in enumerate(((0, 3), (1, 2), ())):
            for t in range(self.n):
                rows = self._own_rows(t, piece)
                self.local_in(t, piece).wait()
                self.stage[t][rows, :] = self.raw[t][rows, :].astype(MM)
                for i in moves:
                    self.sent(i, t).start()
        for t in range(self.n):
            self.local_out(t).start()

    def forward(self):
        for i in range(self.N_MOVES):
            for t in range(self.n):
                self.arrived(i, t).wait_recv()
                if i in self.sent_on_after:
                    self.sent(self.sent_on_after[i], t).start()
                self.passed(i, t, self.c).start()

    def finish(self):
        for i in range(self.N_MOVES):
            for t in range(self.n):
                self.passed(i, t, 1 - self.c).wait_recv()
        for i in range(self.N_MOVES):
            for t in range(self.n):
                self.sent(i, t).wait_send()
                self.passed(i, t, self.c).wait_send()
        for t in range(self.n):
            self.local_out(t).wait()


def all_gather_weights(shards, small, later):
    n, m = len(shards), len(later)
    shapes = [s.shape for s in shards]

    def body(*refs):
        w = refs[:n]
        sm = refs[n]
        lw = refs[n + 1:n + 1 + m]
        o = refs[n + 1 + m:2 * n + 1 + m]
        osm = refs[2 * n + 1 + m]
        lo = refs[2 * n + 2 + m:2 * n + 2 + 2 * m]
        scratch = refs[2 * n + 2 + 2 * m:]
        ssend, srecv, sloc, lsem_in, lsem_out = scratch[:5]
        lraw, lstage = scratch[5:5 + m], scratch[5 + m:5 + 2 * m]
        g = _Gather(shapes, w, o, scratch[5 + 2 * m:])
        own = pltpu.make_async_copy(sm, osm.at[g.me], sloc)
        own.start()
        g.start()
        loads = [pltpu.make_async_copy(lw[t], lraw[t], lsem_in.at[t]) for t in range(m)]
        for cp in loads:
            cp.start()
        small_cps = [_rcopy(sm, osm.at[g.me], ssend.at[j], srecv.at[j], g._chip(j)[1]) for j in range(3)]
        for cp in small_cps:
            cp.start()
        places = []
        for t in range(m):
            loads[t].wait()
            lstage[t][...] = lraw[t][...].astype(MM)
            places.append(pltpu.make_async_copy(lstage[t], lo[t].at[g.me], lsem_out.at[t]))
            places[t].start()
        g.forward()
        g.finish()
        for j in range(3):
            k, dev = g._chip(j)
            _rcopy(sm, osm.at[k], ssend.at[j], srecv.at[j], dev).wait_recv()
            small_cps[j].wait_send()
        own.wait()
        for cp in places:
            cp.wait()

    out_shape = _Gather.out_shapes(shards)
    out_shape.append(jax.ShapeDtypeStruct((N_CHIPS,) + small.shape, small.dtype))
    out_shape += _Gather.out_shapes(later)
    sems = pltpu.SemaphoreType.DMA
    return pl.pallas_call(
        body, name="all_gather_weights", out_shape=out_shape,
        in_specs=[ANY] * (n + 1 + m), out_specs=[ANY] * (n + 1 + m),
        scratch_shapes=[sems((3,)), sems((3,)), sems, sems((m,)), sems((m,))]
                       + [pltpu.VMEM(s.shape, s.dtype) for s in later] + [pltpu.VMEM(s.shape, MM) for s in later]
                       + _Gather.scratch(shards),
        compiler_params=_params(),
    )(*shards, small, *later)


HBM_SPEC = pl.BlockSpec(memory_space=pltpu.HBM)
SEM_SPEC = pl.BlockSpec(memory_space=pltpu.SEMAPHORE)
DATAFLOW = pltpu.SideEffectType.DATAFLOW_SIDE_EFFECTING


def split_start(name, bufs, n_copies, copies):
    nb = len(bufs)

    def body(*refs):
        for cp in copies(refs[:nb], refs[nb], refs[nb + 1]):
            cp.start()
        token = refs[2 * nb + 2]
        token[...] = jnp.zeros_like(token)

    sems = [pltpu.SemaphoreType.DMA((n_copies,))] * 2
    res = pl.pallas_call(
        body, name=name,
        out_shape=sems + [pltpu.HBM(a.shape, a.dtype) for a in bufs] + [jax.ShapeDtypeStruct((8, 128), F32)],
        in_specs=[HBM_SPEC] * nb, out_specs=[SEM_SPEC] * 2 + [HBM_SPEC] * nb + [VMEM_SPEC],
        input_output_aliases={i: 2 + i for i in range(nb)},
        compiler_params=pltpu.CompilerParams(has_side_effects=DATAFLOW),
    )(*[pltpu.with_memory_space_constraint(a, pltpu.HBM) for a in bufs])
    return res[:-1], res[-1]


def split_wait(name, state, after, copies):
    sems, bufs = state[:2], state[2:]
    nb = len(bufs)

    def body(*refs):
        for cp in copies(refs[:nb], refs[nb], refs[nb + 1]):
            cp.wait_send()
            cp.wait_recv()

    return pl.pallas_call(
        body, name=name, out_shape=[pltpu.HBM(a.shape, a.dtype) for a in bufs],
        in_specs=[HBM_SPEC] * nb + [SEM_SPEC] * 2 + [ANY] * len(after), out_specs=[HBM_SPEC] * nb,
        input_output_aliases={i: i for i in range(nb)},
        compiler_params=pltpu.CompilerParams(has_side_effects=DATAFLOW),
    )(*bufs, *sems, *after)


class _Shifted:
    def __init__(self, sems, first):
        self.sems, self.first = sems, first

    @property
    def at(self):
        return self

    def __getitem__(self, i):
        return self.sems.at[self.first + i]


def _scatter_copies(n):
    def copies(refs, send, recv):
        _, _, c, chips = _place()
        return [_rcopy(refs[t].at[2 * cx + cy], refs[n + t].at[j], send.at[3 * t + j], recv.at[3 * t + j], (cx, cy, c))
                for t in range(n) for j, (cx, cy) in enumerate(chips)]
    return copies


def _direct_copies(n):
    def copies(refs, send, recv):
        x, y, c, chips = _place()
        out = []
        for t in range(n):
            src, land = refs[t], refs[n + t]
            for to_core, first in ((c, 0), (1 - c, 3)):
                for j, (cx, cy) in enumerate(chips):
                    i = 7 * t + first + j
                    out.append(_rcopy(src.at[2 * cx + cy, to_core], land.at[first + j], send.at[i], recv.at[i],
                                      (cx, cy, to_core)))
            i = 7 * t + 6
            out.append(_rcopy(src.at[2 * x + y, 1 - c], land.at[6], send.at[i], recv.at[i], (x, y, 1 - c)))
        return out
    return copies


def _scatter_and_direct(n, m):
    def copies(refs, send, recv):
        return (_scatter_copies(n)(refs[:2 * n], send, recv)
                + _direct_copies(m)(refs[2 * n:], _Shifted(send, 3 * n), _Shifted(recv, 3 * n)))
    return copies


def scatter_start(parts, partials):
    n, m = len(parts), len(partials)
    lands = [lax.empty((3,) + p.shape[1:], p.dtype) for p in parts]
    direct_lands = [lax.empty((7,) + p.shape[2:], p.dtype) for p in partials]
    return split_start("scatter_start_rest", list(parts) + lands + list(partials) + direct_lands, 3 * n + 7 * m,
                       _scatter_and_direct(n, m))


def scatter_wait(state, after, n, m):
    res = split_wait("scatter_wait_rest", state, after, _scatter_and_direct(n, m))
    return res[n:2 * n], res[2 * n:2 * n + m], res[2 * n + m:]


def _gather_copies(shapes, level, to_both_cores=()):
    n = len(shapes)

    def copies(refs, send, recv):
        x, y, c, chips = _place()
        out = []
        for t in list(range(n)) + list(to_both_cores):
            rh = shapes[t][0] // 2
            to_core = c if len(out) < 3 * n else 1 - c
            for cx, cy in chips:
                k, dev = (2 * x + y, (cx, cy, to_core)) if level == 1 else (2 * cx + cy, (x, y, 1 - c))
                blk = refs[t].at[k, pl.ds(c * rh, rh), :]
                out.append(_rcopy(blk, blk, send.at[len(out)], recv.at[len(out)], dev))
        return out
    return copies


def _sibling_copies(n, other_half):
    def copies(refs, send, recv):
        x, y, c, _ = _place()
        return [_rcopy(refs[t].at[:, 1 - c] if other_half else refs[t], refs[n + t], send.at[t], recv.at[t],
                       (x, y, 1 - c)) for t in range(n)]
    return copies


def sibling_start(srcs, other_half, tag):
    lands = [lax.empty((a.shape[0],) + a.shape[2:] if other_half else a.shape, a.dtype) for a in srcs]
    return split_start("sibling_start_" + tag, list(srcs) + lands, len(srcs),
                       _sibling_copies(len(srcs), other_half))


def sibling_wait(state, after, other_half, tag):
    n = (len(state) - 2) // 2
    res = split_wait("sibling_wait_" + tag, state, after, _sibling_copies(n, other_half))
    return res[:n], res[n:]


def small_pack(ddw, v512, v1024, loss_parts):
    rows, width = PACK_ROWS, 512
    n512, n1024 = len(VEC512), len(VEC1024)

    def body(*refs):
        ddw_ref = refs[0]
        a_refs = refs[1:1 + n512]
        b_refs = refs[1 + n512:1 + n512 + n1024]
        lp_ref, o_ref, p_ref = refs[1 + n512 + n1024:]
        p_ref[...] = jnp.zeros_like(p_ref)
        p_ref[0:32, :] = ddw_ref[...]
        p_ref[LOSS_ROW:LOSS_ROW + 1, 0:128] = jnp.sum(lp_ref[...], axis=0, keepdims=True) * 0.125
        for i, r in enumerate(a_refs):
            p_ref[32 + i:33 + i, :] = r[...]
        for i, r in enumerate(b_refs):
            base = 32 + n512 + 2 * i
            p_ref[base:base + 1, :] = r[:, 0:512]
            p_ref[base + 1:base + 2, :] = r[:, 512:1024]
        x, y, c, _ = _place()
        o_ref[4 * x + 2 * y + c] = p_ref[...]

    n_in = 2 + n512 + n1024
    return pl.pallas_call(
        body, name="small_pack", out_shape=jax.ShapeDtypeStruct((8, rows, width), F32),
        in_specs=[VMEM_SPEC] * n_in, out_specs=VMEM_SPEC,
        scratch_shapes=[pltpu.VMEM((rows, width), F32)],
    )(ddw, *[v512[n] for n in VEC512], *[v1024[n] for n in VEC1024], loss_parts)


def _small_copies(refs, send, recv):
    x, y, c, _ = _place()
    mine = refs[0].at[4 * x + 2 * y + c]
    peers = [(1 - x if k & 4 else x, 1 - y if k & 2 else y, 1 - c if k & 1 else c) for k in range(1, 8)]
    return [_rcopy(mine, mine, send.at[i], recv.at[i], dev) for i, dev in enumerate(peers)]


def _row_block(r):
    for tr in (512, 352, 256, 128):
        if r % tr == 0:
            return tr
    return r


def add_halves(g, recv, name):
    _, _, r, w = g.shape
    tr = _row_block(r)

    def body(g_ref, r_ref, ob_ref, own_ref):
        k = pl.program_id(1)
        me = 2 * lax.axis_index("x") + lax.axis_index("y")
        t = g_ref[0, 0].astype(F32) + r_ref[0].astype(F32)
        ob_ref[0] = t.astype(MM)
        mine = jnp.where(k == me, t, 0.0)

        @pl.when(k == 0)
        def _():
            own_ref[...] = mine

        @pl.when(k != 0)
        def _():
            own_ref[...] += mine

    return pl.pallas_call(
        body, name=name, grid=(r // tr, N_CHIPS),
        in_specs=[pl.BlockSpec((1, 1, tr, w), lambda i, k: (k, lax.axis_index("c"), i, 0)),
                  pl.BlockSpec((1, tr, w), lambda i, k: (k, i, 0))],
        out_specs=[pl.BlockSpec((1, tr, w), lambda i, k: (k, i, 0)),
                   pl.BlockSpec((tr, w), lambda i, k: (i, 0))],
        out_shape=(jax.ShapeDtypeStruct((N_CHIPS, r, w), MM), jax.ShapeDtypeStruct((r, w), F32)),
        compiler_params=_params(("arbitrary", "arbitrary")),
    )(g, recv)


def sum_parts(own, rin, after, name):
    _, r, w = rin.shape
    tr = _row_block(r)

    def body(o_ref, r_ref, after_ref, out_ref):
        out_ref[...] = ((o_ref[...] + r_ref[0].astype(F32)) + r_ref[1].astype(F32)) + r_ref[2].astype(F32)

    return pl.pallas_call(
        body, name=name, grid=(r // tr,), out_shape=jax.ShapeDtypeStruct((r, w), F32),
        in_specs=[pl.BlockSpec((tr, w), lambda i: (i, 0)), pl.BlockSpec((3, tr, w), lambda i: (0, i, 0)),
                  TOKEN_SPEC],
        out_specs=pl.BlockSpec((tr, w), lambda i: (i, 0)),
        compiler_params=_params(("arbitrary",)),
    )(own, rin, after)


def sum_partials(p, land, after, name):
    _, _, r, w = p.shape
    tr = _row_block(r)

    def body(p_ref, l_ref, after_ref, out_ref):
        total = p_ref[0, 0].astype(F32)
        for slot in (6, 0, 3, 1, 4, 2, 5):
            total = total + l_ref[slot].astype(F32)
        out_ref[...] = total

    def own(i):
        return 2 * lax.axis_index("x") + lax.axis_index("y"), lax.axis_index("c"), i, 0

    return pl.pallas_call(
        body, name=name, grid=(r // tr,), out_shape=jax.ShapeDtypeStruct((r, w), F32),
        in_specs=[pl.BlockSpec((1, 1, tr, w), own), pl.BlockSpec((7, tr, w), lambda i: (0, i, 0)), TOKEN_SPEC],
        out_specs=pl.BlockSpec((tr, w), lambda i: (i, 0)),
        compiler_params=_params(("arbitrary",)),
    )(p, land, after)


def _adamw_math(w, g, m, v):
    mn = ADAM_B1 * m + (1.0 - ADAM_B1) * g
    vn = ADAM_B2 * v + (1.0 - ADAM_B2) * (g * g)
    m_hat = mn / (1.0 - ADAM_B1 ** ADAM_STEP)
    v_hat = vn / (1.0 - ADAM_B2 ** ADAM_STEP)
    return -ADAM_LR * (m_hat / (jnp.sqrt(v_hat) + ADAM_EPS) + ADAM_WD * w), mn, vn


def adamw(w, mine, other, m, v, name):
    r, c = w.shape
    rh = r // 2
    tr = _row_block(rh)
    if c >= 1024 and tr % 512 == 0:
        tr = 256
    nb = rh // tr

    def body(w_ref, a_ref, b_ref, m_ref, v_ref, go_ref, d_ref, mo_ref, vo_ref):
        gv = jnp.where(lax.axis_index("c") == pl.program_id(0), a_ref[...], b_ref[...])
        go_ref[...] = gv
        d_ref[...], mo_ref[...], vo_ref[...] = _adamw_math(w_ref[...], gv, m_ref[...], v_ref[...])

    def half(of_sibling):
        def index(h, i):
            owner = lax.axis_index("c")
            owner = 1 - owner if of_sibling else owner
            return jnp.where(h == owner, i, jnp.where(h < owner, 0, nb - 1)), 0
        return pl.BlockSpec((tr, c), index)

    spec = pl.BlockSpec((tr, c), lambda h, i: (h * nb + i, 0))
    out = jax.ShapeDtypeStruct((r, c), F32)
    return pl.pallas_call(
        body, name=name, grid=(2, nb), out_shape=(out, out, out, out),
        in_specs=[spec, half(False), half(True), spec, spec], out_specs=[spec] * 4,
        compiler_params=_params(("arbitrary", "arbitrary")),
    )(w, mine, other, m, v)


def adamw_small(packs, params, after):
    names = list(params)
    flat = [a for n in names for a in params[n]]

    def body(*refs):
        p_ref = refs[0]
        ins = refs[1:1 + 3 * len(names)]
        loss_ref, g_ref = refs[2 + 3 * len(names):4 + 3 * len(names)]
        outs = refs[4 + 3 * len(names):]
        total = p_ref[0]
        for d in range(1, 8):
            total = total + p_ref[d]
        g_ref[...] = total
        loss_ref[...] = g_ref[LOSS_ROW:LOSS_ROW + 1, 0:1]
        me = 2 * lax.axis_index("x") + lax.axis_index("y")
        for i, n in enumerate(names):
            w_ref, m_ref, v_ref = ins[3 * i:3 * i + 3]
            go_ref, d_ref, mo_ref, vo_ref = outs[4 * i:4 * i + 4]
            if n == "conv_dw_w":
                gv = jnp.zeros((CONV_WIDTH, 128), F32)
                for k in range(N_CHIPS):
                    gv = gv + jnp.where(me == k, g_ref[0:CONV_WIDTH, 128 * k:128 * (k + 1)], 0.0)
            elif n in VEC512:
                r0 = 32 + VEC512.index(n)
                gv = g_ref[r0:r0 + 1, :]
            else:
                r0 = 32 + len(VEC512) + 2 * VEC1024.index(n)
                gv = jnp.concatenate([g_ref[r0:r0 + 1, :], g_ref[r0 + 1:r0 + 2, :]], axis=1)
            go_ref[...] = gv
            d_ref[...], mo_ref[...], vo_ref[...] = _adamw_math(w_ref[...], gv, m_ref[...], v_ref[...])

    out_shape = [jax.ShapeDtypeStruct((1, 1), F32), jax.ShapeDtypeStruct(packs.shape[1:], F32)]
    out_shape += [jax.ShapeDtypeStruct(params[n][0].shape, F32) for n in names for _ in range(4)]
    res = pl.pallas_call(
        body, name="adamw_small", out_shape=out_shape,
        in_specs=[VMEM_SPEC] * (2 + len(flat)), out_specs=[VMEM_SPEC] * len(out_shape),
        compiler_params=_params(),
    )(packs, *flat, after)
    return res[0], res[1], {n: res[2 + 4 * i:6 + 4 * i] for i, n in enumerate(names)}


REST = ("w_ffn_up", "w_ffn_down", "w_out", "w_conv_branch", "w_att_branch")
VEC512 = ("conv_dw_b", "conv_ln_g", "conv_ln_b")
VEC1024 = ("norm_mix_pre", "b_conv_branch", "norm_mix_post", "norm_ffn_pre", "norm_ffn_post")
PACK_ROWS = 48
LOSS_ROW = 47


def kernel(x, norm_mix_pre, w_in, conv_dw_w, conv_dw_b, conv_ln_g, conv_ln_b, w_conv_branch, b_conv_branch, w_att_branch, w_out, norm_mix_post, norm_ffn_pre, w_ffn_up, w_ffn_down, norm_ffn_post, loss_target, m_norm_mix_pre, m_w_in, m_conv_dw_w, m_conv_dw_b, m_conv_ln_g, m_conv_ln_b, m_w_conv_branch, m_b_conv_branch, m_w_att_branch, m_w_out, m_norm_mix_post, m_norm_ffn_pre, m_w_ffn_up, m_w_ffn_down, m_norm_ffn_post, v_norm_mix_pre, v_w_in, v_conv_dw_w, v_conv_dw_b, v_conv_ln_g, v_conv_ln_b, v_w_conv_branch, v_b_conv_branch, v_w_att_branch, v_w_out, v_norm_mix_post, v_norm_ffn_pre, v_w_ffn_up, v_w_ffn_down, v_norm_ffn_post):
    weights = dict(norm_mix_pre=norm_mix_pre, w_in=w_in, conv_dw_w=conv_dw_w, conv_dw_b=conv_dw_b, conv_ln_g=conv_ln_g, conv_ln_b=conv_ln_b, w_conv_branch=w_conv_branch, b_conv_branch=b_conv_branch, w_att_branch=w_att_branch, w_out=w_out, norm_mix_post=norm_mix_post, norm_ffn_pre=norm_ffn_pre, w_ffn_up=w_ffn_up, w_ffn_down=w_ffn_down, norm_ffn_post=norm_ffn_post)
    mom = dict(norm_mix_pre=m_norm_mix_pre, w_in=m_w_in, conv_dw_w=m_conv_dw_w, conv_dw_b=m_conv_dw_b, conv_ln_g=m_conv_ln_g, conv_ln_b=m_conv_ln_b, w_conv_branch=m_w_conv_branch, b_conv_branch=m_b_conv_branch, w_att_branch=m_w_att_branch, w_out=m_w_out, norm_mix_post=m_norm_mix_post, norm_ffn_pre=m_norm_ffn_pre, w_ffn_up=m_w_ffn_up, w_ffn_down=m_w_ffn_down, norm_ffn_post=m_norm_ffn_post)
    var = dict(norm_mix_pre=v_norm_mix_pre, w_in=v_w_in, conv_dw_w=v_conv_dw_w, conv_dw_b=v_conv_dw_b, conv_ln_g=v_conv_ln_g, conv_ln_b=v_conv_ln_b, w_conv_branch=v_w_conv_branch, b_conv_branch=v_b_conv_branch, w_att_branch=v_w_att_branch, w_out=v_w_out, norm_mix_post=v_norm_mix_post, norm_ffn_pre=v_norm_ffn_pre, w_ffn_up=v_w_ffn_up, w_ffn_down=v_w_ffn_down, norm_ffn_post=v_norm_ffn_post)
    order = list(weights)
    grads, deltas, new_m, new_v = {}, {}, {}, {}
    xs = x.reshape(SEQ, D_MODEL)
    tgt = loss_target.reshape(SEQ, D_MODEL)
    row = lambda a: a.reshape(1, -1)
    g1, g2, g3, g4 = (row(weights[n]) for n in ("norm_mix_pre", "norm_mix_post", "norm_ffn_pre", "norm_ffn_post"))
    ln_g, ln_b = row(conv_ln_g), row(conv_ln_b)

    summed, from_chips = {}, {}

    def core_sums(names, state, after, tag):
        own, from_sibling = sibling_wait(state, after, True, tag)
        for n, g, r in zip(names, own, from_sibling):
            summed[n] = add_halves(g, r, "add_" + n)

    def chip_sums(names, after):
        return [sum_parts(summed[n][1], from_chips[n], after, "sum_" + n) for n in names]

    def optimize(names, state, after, tag):
        mine, other = sibling_wait(state, after, False, tag)
        for n, a, b in zip(names, mine, other):
            grads[n], deltas[n], new_m[n], new_v[n] = adamw(weights[n], a, b, mom[n], var[n], "adamw_" + n)

    w_in_g, dw_g, *rest = all_gather_weights([w_in], conv_dw_w, [weights[n] for n in REST])
    w_dw_full = jnp.concatenate([dw_g[k] for k in range(N_CHIPS)], axis=1)
    rest_shapes = [weights[n].shape for n in REST]
    over_ici = _gather_copies(rest_shapes, 1, to_both_cores=(2, 3, 4))
    state, token = split_start("gather_start", rest, 3 * (len(REST) + 3), over_ici)
    h1, ci, q, k, v, gc, ga = in_proj_fwd(xs, g1, w_in_g, token)
    u1, u3 = conv_fwd(ci, w_dw_full, row(conv_dw_b), ln_g, ln_b)
    att, rc = attn_fwd(q, k, v)
    rest = split_wait("gather_wait", state, [att], over_ici)
    w_out_g, w_cb_g, w_ab_g = rest[2:]
    w_out_g = w_out_g.reshape(D_MODEL, D_MODEL)
    to_sibling = _gather_copies(rest_shapes[:2], 2)
    state, token = split_start("pass_start", rest[:2], 3 * 2, to_sibling)
    merged, mix, x2, h2 = mix_fwd(u3, att, gc, ga, xs, w_cb_g, row(b_conv_branch), w_ab_g, w_out_g, g2, g3, token)
    w_up_g, w_down_g = split_wait("pass_wait", state, [h2], to_sibling)
    w_down_g = w_down_g.reshape(D_FF, D_MODEL)
    gate, up, act = ffn_up_fwd(h2, w_up_g)
    dff, dy, loss_parts, dg4 = ffn_down_loss(act, w_down_g, x2, tgt, g4)

    dgu = ffn_act_bwd(dff, w_down_g, gate, up)
    dx2, dmix, dg3, dg2 = ffn_in_bwd(dgu, w_up_g, x2, mix, dy, g3, g2)
    to_ffn, token = sibling_start([weight_grad(h2, dgu, "dw_ffn_up", True)], True, "dw_ffn")
    dco, dao, dg, du3, datt, dbcb = merge_bwd(dmix, w_out_g, gc, ga, u3, att, w_cb_g, row(b_conv_branch), w_ab_g,
                                              token)
    direct_grads = [weight_grad(act, dff, "dw_ffn_down", False, tk=UP_SHARD),
                    *weight_grad_mix(merged, dmix, u3, dco, att, dao)]
    core_sums(REST[:1], to_ffn, [direct_grads[1]], "dw_ffn")
    state, token = scatter_start([summed["w_ffn_up"][0]], direct_grads)
    dci, ddw, dbdw, dlng, dlnb = conv_bwd(du3, u1, ci, w_dw_full, ln_g, ln_b, token)
    dqkv = attn_bwd(q, k, v, datt, rc, token)
    up_from_chips, direct_grads, direct_from_all = scatter_wait(state, [dci, dqkv], 1, len(direct_grads))
    from_chips["w_ffn_up"], = up_from_chips
    dproj = (dci, dqkv, dg)
    to_in, token = sibling_start([weight_grad_in(h1, dproj)], True, "dw_in")
    grad_x, dg1 = in_proj_bwd(dproj, w_in_g, xs, dx2, g1, token)
    v512 = dict(conv_dw_b=dbdw, conv_ln_g=dlng, conv_ln_b=dlnb)
    v1024 = dict(norm_mix_pre=dg1, b_conv_branch=dbcb, norm_mix_post=dg2, norm_ffn_pre=dg3, norm_ffn_post=dg4)
    packs = small_pack(ddw, v512, v1024, loss_parts)
    core_sums(("w_in",), to_in, [packs], "dw_in")
    to_chips = summed["w_in"][0]
    landing = lax.empty((3,) + to_chips.shape[1:], to_chips.dtype)

    def scatter_and_packs(refs, send, recv):
        return (_scatter_copies(1)(refs[:2], send, recv)
                + _small_copies(refs[2:], _Shifted(send, 3), _Shifted(recv, 3)))

    state, token = split_start("scatter_start_w_in", [to_chips, landing, packs], 3 + 7, scatter_and_packs)
    swap_up, token = sibling_start(chip_sums(REST[:1], token), False, "sum_ffn_up")
    rest_sums = [sum_partials(p, r, token, "sum_" + n) for n, p, r in zip(REST[1:], direct_grads, direct_from_all)]
    swap_rest, token = sibling_start(rest_sums, False, "sum_rest")
    optimize(REST[:1], swap_up, [token], "sum_ffn_up")
    optimize(REST[1:], swap_rest, [new_v["w_ffn_up"]], "sum_rest")
    _, from_chips["w_in"], packs = split_wait("scatter_wait_w_in", state, [new_v[n] for n in REST], scatter_and_packs)
    swap_in, token = sibling_start(chip_sums(("w_in",), token), False, "sum_w_in")
    as_rows = lambda n, a: a if n == "conv_dw_w" else a.reshape(1, -1)
    small_names = ("conv_dw_w",) + VEC512 + VEC1024
    loss, gsum, small = adamw_small(
        packs, {n: tuple(as_rows(n, d[n]) for d in (weights, mom, var)) for n in small_names}, token)
    optimize(("w_in",), swap_in, [gsum], "sum_w_in")
    for n in small_names:
        grads[n], deltas[n], new_m[n], new_v[n] = (a.reshape(weights[n].shape) for a in small[n])

    return (loss.reshape(()), grad_x.reshape(1, SEQ, D_MODEL),*[grads[n] for n in order], *[deltas[n] for n in order],
            *[new_m[n] for n in order], *[new_v[n] for n in order])
```

```python
import jax
import jax.numpy as jnp
from jax import lax
from jax.experimental import pallas as pl
from jax.experimental.pallas import tpu as pltpu

F32 = jnp.float32
MM = jnp.bfloat16

SEQ = 2048
D_MODEL = 1024
CONV_DIM = 512
ATT_DIM = 512
CONV_WIDTH = 31
D_FF = 2816
IN_COLS = 2 * CONV_DIM + 3 * ATT_DIM + 2 * D_MODEL
N_CHIPS = 4
IN_SHARD = IN_COLS // N_CHIPS
UP_SHARD = 2 * D_FF // N_CHIPS
BR_SHARD = D_MODEL // N_CHIPS
EPS = 1e-6
ATT_SCALE = 0.125

TM = 256
GLU_ROWS = 256
TQ = 128
CONV_TILE = 64
CONV_WIN = CONV_TILE + 32
VMEM_LIMIT = 56 * 1024 * 1024

ADAM_LR = 0.001
ADAM_B1 = 0.9
ADAM_B2 = 0.999
ADAM_EPS = 1e-08
ADAM_WD = 0.01
ADAM_STEP = 10

MESH = pl.DeviceIdType.MESH
ANY = pl.BlockSpec(memory_space=pl.ANY)
VMEM_SPEC = pl.BlockSpec(memory_space=pltpu.VMEM)

NT_DIMS = (((1,), (1,)), ((), ()))
TN_DIMS = (((0,), (0,)), ((), ()))

IN_PIECES = (("ci", 0, 1024), ("q", 1024, 1536), ("k", 1536, 2048), ("v", 2048, 2560),
             ("gc", 2560, 3584), ("ga", 3584, 4608))


def _params(sem=None, vmem=VMEM_LIMIT):
    return pltpu.CompilerParams(dimension_semantics=sem, vmem_limit_bytes=vmem)


def _dot(a, b):
    return jnp.dot(a, b, preferred_element_type=F32)


def _dot_nt(a, b):
    return lax.dot_general(a, b, NT_DIMS, preferred_element_type=F32)


def _dot_tn(a, b):
    return lax.dot_general(a, b, TN_DIMS, preferred_element_type=F32)


def _sigmoid(x):
    return 1.0 / (1.0 + jnp.exp(-x))


def _rms(x):
    r = lax.rsqrt(jnp.mean(x * x, axis=-1, keepdims=True) + EPS)
    return x * r, r


def _rms_bwd(dy_g, n, r):
    return r * (dy_g - n * jnp.mean(dy_g * n, axis=-1, keepdims=True))


def _row_tile_spec(width, tm=TM):
    return pl.BlockSpec((tm, width), lambda i: (i, 0))


def _full_spec(shape):
    nd = len(shape)
    return pl.BlockSpec(shape, lambda *_: (0,) * nd)


def _weight_spec(shape):
    nd = len(shape)
    return pl.BlockSpec(shape, lambda *_: (0,) * nd, pipeline_mode=pl.Buffered(1))


def _acc_rows(ref, val, first):
    @pl.when(first)
    def _():
        ref[...] = val

    @pl.when(jnp.logical_not(first))
    def _():
        ref[...] += val


TOKEN_SPEC = pl.BlockSpec((8, 128), lambda *_: (0, 0))


def in_proj_fwd(x, g1, w_in_g, after):
    def body(x_ref, g_ref, w_ref, after_ref, h_ref, ci_ref, q_ref, k_ref, v_ref, gc_ref, ga_ref):
        n, _ = _rms(x_ref[...])
        h = (n * g_ref[...]).astype(MM)
        h_ref[...] = h
        outs = dict(ci=ci_ref, q=q_ref, k=k_ref, v=v_ref, gc=gc_ref, ga=ga_ref)
        for j in range(N_CHIPS):
            p = _dot(h, w_ref[j])
            g0 = j * IN_SHARD
            for name, s, e in IN_PIECES:
                lo, hi = max(s, g0), min(e, g0 + IN_SHARD)
                if lo < hi:
                    ref = outs[name]
                    part = p[:, lo - g0:hi - g0]
                    if name == "q":
                        part = part * ATT_SCALE
                    ref[:, lo - s:hi - s] = part.astype(ref.dtype)

    out_shape = [
        jax.ShapeDtypeStruct((SEQ, D_MODEL), MM),
        jax.ShapeDtypeStruct((SEQ, 2 * CONV_DIM), F32),
        jax.ShapeDtypeStruct((SEQ, ATT_DIM), MM),
        jax.ShapeDtypeStruct((SEQ, ATT_DIM), MM),
        jax.ShapeDtypeStruct((SEQ, ATT_DIM), MM),
        jax.ShapeDtypeStruct((SEQ, D_MODEL), F32),
        jax.ShapeDtypeStruct((SEQ, D_MODEL), F32),
    ]
    return pl.pallas_call(
        body, name="in_proj_fwd", grid=(SEQ // TM,), out_shape=out_shape,
        in_specs=[_row_tile_spec(D_MODEL), _full_spec((1, D_MODEL)), _weight_spec(w_in_g.shape), TOKEN_SPEC],
        out_specs=[_row_tile_spec(s.shape[1]) for s in out_shape],
        compiler_params=_params(("arbitrary",)),
    )(x, g1, w_in_g, after)


LANE_GROUPS = [slice(g, g + 128) for g in range(0, CONV_DIM, 128)]
NORM_ROWS = 16


def _shifted_windows(src_ref, t0, cols, offsets):
    win = src_ref[pl.ds(t0, CONV_WIN), cols]
    for rot in range(8):
        ms = [m for m in offsets if m % 8 == rot]
        if ms:
            shifted = win if rot == 0 else pltpu.roll(win, CONV_WIN - rot, 0)
            for m in ms:
                yield m, shifted[m - rot:m - rot + CONV_TILE, :]


def _shifted_sum(src_ref, t0, cols, w_ref, offset_of_tap):
    tap_at = {offset_of_tap(j): j for j in range(CONV_WIDTH)}
    acc = None
    for m, rows in _shifted_windows(src_ref, t0, cols, sorted(tap_at)):
        t = w_ref[tap_at[m]:tap_at[m] + 1, cols] * rows
        acc = t if acc is None else acc + t
    return acc


def _fetch(srcs, dsts, sems):
    copies = [pltpu.make_async_copy(s, d, sems.at[i]) for i, (s, d) in enumerate(zip(srcs, dsts))]
    for cp in copies:
        cp.start()
    return copies


def _row_chunks(src, dst, sems):
    def chunk(i):
        t0 = i * GLU_ROWS
        rows = pl.ds(t0 if isinstance(i, int) else pl.multiple_of(t0, GLU_ROWS), GLU_ROWS)
        return pltpu.make_async_copy(src.at[rows, :], dst.at[rows, :], sems.at[i])

    for i in range(SEQ // GLU_ROWS):
        chunk(i).start()
    return chunk


def _glu_into(ci_chunk, ci_ref, upad_ref):
    upad_ref[0:32, :] = jnp.zeros((32, CONV_DIM), F32)

    def step(i, c):
        ci_chunk(i).wait()
        t0 = pl.multiple_of(i * GLU_ROWS, GLU_ROWS)
        a = ci_ref[pl.ds(t0, GLU_ROWS), 0:CONV_DIM]
        b = ci_ref[pl.ds(t0, GLU_ROWS), CONV_DIM:2 * CONV_DIM]
        upad_ref[pl.ds(t0 + 32, GLU_ROWS), :] = a * _sigmoid(b)
        return c

    lax.fori_loop(0, SEQ // GLU_ROWS, step, 0)


def _layernorm_parts(u1):
    mu = jnp.mean(u1, axis=-1, keepdims=True)
    xc = u1 - mu
    rstd = lax.rsqrt(jnp.mean(xc * xc, axis=-1, keepdims=True) + EPS)
    return xc * rstd, rstd


def conv_fwd(ci, w_dw, b_dw, ln_g, ln_b):
    def body(ci_hbm, w_ref, b_ref, g_ref, bb_ref, u1_ref, u3_ref, upad_ref, ci_ref, sems):
        _glu_into(_row_chunks(ci_hbm, ci_ref, sems), ci_ref, upad_ref)

        def step(i, c):
            t0 = pl.multiple_of(i * CONV_TILE, CONV_TILE)
            for cols in LANE_GROUPS:
                u1_ref[pl.ds(t0, CONV_TILE), cols] = (_shifted_sum(upad_ref, t0, cols, w_ref, lambda j: j + 2)
                                                      + b_ref[:, cols])
            for r in range(0, CONV_TILE, NORM_ROWS):
                rows = pl.ds(t0 + r, NORM_ROWS)
                xh, _ = _layernorm_parts(u1_ref[rows, :])
                u2 = xh * g_ref[...] + bb_ref[...]
                u3_ref[rows, :] = (u2 * _sigmoid(u2)).astype(MM)
            return c

        lax.fori_loop(0, SEQ // CONV_TILE, step, 0)

    return pl.pallas_call(
        body, name="conv_fwd",
        out_shape=[jax.ShapeDtypeStruct((SEQ, CONV_DIM), F32), jax.ShapeDtypeStruct((SEQ, CONV_DIM), MM)],
        in_specs=[ANY] + [VMEM_SPEC] * 4, out_specs=[VMEM_SPEC] * 2,
        scratch_shapes=[pltpu.VMEM((SEQ + 32, CONV_DIM), F32), pltpu.VMEM(ci.shape, ci.dtype),
                        pltpu.SemaphoreType.DMA((SEQ // GLU_ROWS,))],
        compiler_params=_params(),
    )(ci, w_dw, b_dw, ln_g, ln_b)


def _softplus(z):
    return jnp.maximum(z, 0.0) + jnp.log(1.0 + jnp.exp(-jnp.abs(z)))


def _cumsum_weights(suffix, with_total):
    n = 256 if with_total else 128
    r = lax.broadcasted_iota(jnp.int32, (128, n), 0)
    c = lax.broadcasted_iota(jnp.int32, (128, n), 1)
    tri = (r >= c) if suffix else (r <= c)
    return jnp.logical_or(tri, c >= 128).astype(MM)


NO_SCORE = -1e30
N_KB = SEQ // TQ


def _score_bias(lane, row, i, j):
    keep = jnp.logical_and(i >= 0, jnp.logical_or(j < i, lane < row))
    return jnp.where(keep, 0.0, NO_SCORE)


def _block_pipeline(n_stages, descending, step, on_query_block=None):
    n_lag = n_stages - 1
    none = jnp.int32(-1)

    def shift(cur, lag):
        step([cur] + [(lag[2 * s], lag[2 * s + 1]) for s in range(n_lag)])
        return (cur[0], cur[1]) + tuple(lag[:-2])

    def outer(i, lag):
        if on_query_block is not None:
            on_query_block(i)

        def inner(n, lag):
            return shift((i, i - n if descending else n), lag)
        return lax.fori_loop(0, i + 1, inner, lag)

    lag = lax.fori_loop(0, N_KB, outer, (none,) * (2 * n_lag))
    lax.fori_loop(0, n_lag, lambda n, lag: shift((none, none), lag), lag)


def _head_masks():
    lane = lax.broadcasted_iota(jnp.int32, (TQ, 128), 1)
    row = lax.broadcasted_iota(jnp.int32, (TQ, 128), 0)
    return lane, row, lane < 64


def _pick_head(x, head0, h):
    zero = jnp.zeros_like(x)
    return jnp.where(head0, x, zero) if h == 0 else jnp.where(head0, zero, x)


N_PAIRS = ATT_DIM // 128


def _split_heads(src_ref, dst_ref):
    _, _, head0 = _head_masks()

    def block(b, c):
        r0 = pl.multiple_of(b * TQ, TQ)
        d0 = pl.multiple_of(b * 2 * TQ, 2 * TQ)
        for p in range(N_PAIRS):
            x = src_ref[pl.ds(r0, TQ), 128 * p:128 * (p + 1)]
            for h in range(2):
                dst_ref[p, pl.ds(d0 + TQ * h, TQ), :] = _pick_head(x, head0, h)
        return c

    lax.fori_loop(0, N_KB, block, 0)


def attn_fwd(q, k, v):
    def body(q_hbm, k_hbm, v_hbm, o_ref, rc_ref, acc_ref, r_ref, z_ref, spb_ref, ab_ref, qm_ref, vm_ref,
             q_ref, k_ref, v_ref, sems):
        arrive = _fetch((q_hbm, v_hbm, k_hbm), (q_ref, v_ref, k_ref), sems)
        lane, row, _ = _head_masks()
        w = _cumsum_weights(suffix=True, with_total=True)
        acc_ref[...] = jnp.zeros_like(acc_ref)
        r_ref[...] = jnp.zeros_like(r_ref)
        rc_ref[...] = jnp.zeros_like(rc_ref)
        z_ref[...] = jnp.full(z_ref.shape, NO_SCORE, F32)
        spb_ref[...] = jnp.zeros_like(spb_ref)
        ab_ref[...] = jnp.zeros_like(ab_ref)
        arrive[0].wait()
        _split_heads(q_ref, qm_ref)
        arrive[1].wait()
        _split_heads(v_ref, vm_ref)
        arrive[2].wait()

        def step(pairs):
            (i1, j1), (i2, j2), (i3, j3) = pairs
            k1, q2, q3 = (pl.multiple_of(jnp.maximum(b, 0) * TQ, TQ) for b in (j1, i2, i3))
            q1, k3 = (pl.multiple_of(jnp.maximum(b, 0) * 2 * TQ, 2 * TQ) for b in (i1, j3))
            bias1 = _score_bias(lane, row, i1, j1)
            first2 = j2 == i2
            rc_rows = rc_ref[pl.ds(q2, TQ), :]
            for p in range(N_PAIRS):
                cols = slice(128 * p, 128 * (p + 1))
                kb = k_ref[pl.ds(k1, TQ), cols]
                acc_ref[pl.ds(q3, TQ), cols] += _dot(ab_ref[p], vm_ref[p, pl.ds(k3, 2 * TQ), :])
                for h in range(2):
                    hh = 2 * p + h
                    r = _dot(spb_ref[hh], w)
                    r_in = jnp.where(first2, 0.0, r_ref[hh])
                    ab_ref[p, :, 128 * h:128 * (h + 1)] = jnp.exp(z_ref[hh] - (r[:, :128] + r_in)).astype(MM)
                    rc_rows = jnp.where(jnp.logical_and(lane == 16 * hh + j2, i2 >= 0), r_in, rc_rows)
                    r_ref[hh] = r_in + r[:, 128:]
                    z = _dot_nt(qm_ref[p, pl.ds(q1 + TQ * h, TQ), :], kb) + bias1
                    z_ref[hh] = z
                    spb_ref[hh] = _softplus(z).astype(MM)
            rc_ref[pl.ds(q2, TQ), :] = rc_rows

        _block_pipeline(3, True, step)
        o_ref[...] = acc_ref[...].astype(MM)

    return pl.pallas_call(
        body, name="attn_fwd",
        out_shape=[jax.ShapeDtypeStruct((SEQ, ATT_DIM), MM), jax.ShapeDtypeStruct((SEQ, 128), F32)],
        in_specs=[ANY] * 3, out_specs=[VMEM_SPEC] * 2,
        scratch_shapes=[pltpu.VMEM((SEQ, ATT_DIM), F32), pltpu.VMEM((8, TQ, 128), F32),
                        pltpu.VMEM((8, TQ, 128), F32), pltpu.VMEM((8, TQ, 128), MM),
                        pltpu.VMEM((N_PAIRS, TQ, 256), MM), pltpu.VMEM((N_PAIRS, 2 * SEQ, 128), MM),
                        pltpu.VMEM((N_PAIRS, 2 * SEQ, 128), MM)]
                       + [pltpu.VMEM(a.shape, a.dtype) for a in (q, k, v)] + [pltpu.SemaphoreType.DMA((3,))],
        compiler_params=_params(),
    )(q, k, v)


def _branch_outputs(u_ref, a_ref, wcb_ref, bcb_ref, wab_ref):
    u = u_ref[...]
    a = a_ref[...]
    co = jnp.concatenate([_dot(u, wcb_ref[j]) for j in range(N_CHIPS)], axis=1) + bcb_ref[...]
    ao = jnp.concatenate([_dot(a, wab_ref[j]) for j in range(N_CHIPS)], axis=1)
    return co, ao


def mix_fwd(u3, att, gc, ga, x, w_cb_g, b_cb, w_ab_g, w_out_g, g2, g3, after):
    def body(u_ref, a_ref, gc_ref, ga_ref, x_ref, wcb_ref, bcb_ref, wab_ref, wout_ref, g2_ref, g3_ref, after_ref,
             mg_ref, mix_ref, x2_ref, h2_ref):
        co, ao = _branch_outputs(u_ref, a_ref, wcb_ref, bcb_ref, wab_ref)
        merged = (_sigmoid(gc_ref[...]) * co + _sigmoid(ga_ref[...]) * ao).astype(MM)
        mg_ref[...] = merged
        mix = _dot(merged, wout_ref[...])
        mix_ref[...] = mix
        n2, _ = _rms(mix)
        x2 = x_ref[...] + n2 * g2_ref[...]
        x2_ref[...] = x2
        n3, _ = _rms(x2)
        h2_ref[...] = (n3 * g3_ref[...]).astype(MM)

    out_shape = [
        jax.ShapeDtypeStruct((SEQ, D_MODEL), MM), jax.ShapeDtypeStruct((SEQ, D_MODEL), F32),
        jax.ShapeDtypeStruct((SEQ, D_MODEL), F32), jax.ShapeDtypeStruct((SEQ, D_MODEL), MM),
    ]
    vec = _full_spec((1, D_MODEL))
    return pl.pallas_call(
        body, name="mix_fwd", grid=(SEQ // TM,), out_shape=out_shape,
        in_specs=[_row_tile_spec(CONV_DIM), _row_tile_spec(ATT_DIM), _row_tile_spec(D_MODEL),
                  _row_tile_spec(D_MODEL), _row_tile_spec(D_MODEL), _weight_spec(w_cb_g.shape), vec,
                  _weight_spec(w_ab_g.shape), _weight_spec(w_out_g.shape), vec, vec, TOKEN_SPEC],
        out_specs=[_row_tile_spec(D_MODEL)] * 4,
        compiler_params=_params(("arbitrary",)),
    )(u3, att, gc, ga, x, w_cb_g, b_cb, w_ab_g, w_out_g, g2, g3, after)


def ffn_up_fwd(h2, w_up_g):
    def body(h_ref, wg_ref, wu_ref, gate_ref, up_ref, act_ref):
        h = h_ref[...]
        gate = _dot(h, wg_ref[0])
        up = _dot(h, wu_ref[0])
        gate_ref[...] = gate.astype(MM)
        up_ref[...] = up.astype(MM)
        act_ref[...] = (gate * _sigmoid(gate) * up).astype(MM)

    tile = pl.BlockSpec((TM, UP_SHARD), lambda n, i: (i, n))
    act = jax.ShapeDtypeStruct((SEQ, D_FF), MM)
    return pl.pallas_call(
        body, name="ffn_up_fwd", grid=(2, SEQ // TM), out_shape=[act, act, act],
        in_specs=[pl.BlockSpec((TM, D_MODEL), lambda n, i: (i, 0)),
                  pl.BlockSpec((1, D_MODEL, UP_SHARD), lambda n, i: (n, 0, 0)),
                  pl.BlockSpec((1, D_MODEL, UP_SHARD), lambda n, i: (n + 2, 0, 0))],
        out_specs=[tile, tile, tile],
        compiler_params=_params(("arbitrary", "arbitrary")),
    )(h2, w_up_g, w_up_g)


def ffn_down_loss(act, w_down_g, x2, target, g4):
    def body(act_ref, wd_ref, x2_ref, t_ref, g_ref, dff_ref, dy_ref, loss_ref, dg_ref):
        ff = _dot(act_ref[...], wd_ref[...])
        n4, r4 = _rms(ff)
        g4v = g_ref[...]
        err = x2_ref[...] + n4 * g4v - t_ref[...]
        row_loss = jnp.mean(err * err, axis=-1, keepdims=True)
        loss_ref[...] = jnp.zeros((8, 128), F32) + 0.5 * jnp.sum(row_loss, axis=0, keepdims=True)
        dy = err * (1.0 / D_MODEL)
        dy_ref[...] = dy
        dff_ref[...] = _rms_bwd(dy * g4v, n4, r4).astype(MM)
        _acc_rows(dg_ref, jnp.sum(dy * n4, axis=0, keepdims=True), pl.program_id(0) == 0)

    nt = SEQ // TM
    vec = _full_spec((1, D_MODEL))
    return pl.pallas_call(
        body, name="ffn_down_loss", grid=(nt,),
        out_shape=(jax.ShapeDtypeStruct((SEQ, D_MODEL), MM), jax.ShapeDtypeStruct((SEQ, D_MODEL), F32),
                   jax.ShapeDtypeStruct((nt * 8, 128), F32), jax.ShapeDtypeStruct((1, D_MODEL), F32)),
        in_specs=[_row_tile_spec(D_FF), _weight_spec(w_down_g.shape), _row_tile_spec(D_MODEL),
                  _row_tile_spec(D_MODEL), vec],
        out_specs=[_row_tile_spec(D_MODEL), _row_tile_spec(D_MODEL),
                   pl.BlockSpec((8, 128), lambda i: (i, 0)), vec],
        compiler_params=_params(("arbitrary",)),
    )(act, w_down_g, x2, target, g4)


def ffn_act_bwd(dff, w_down_g, gate, up):
    def body(dff_ref, wd_ref, gate_ref, up_ref, dgu_ref):
        dact = _dot_nt(dff_ref[...], wd_ref[...])
        gate = gate_ref[...].astype(F32)
        sg = _sigmoid(gate)
        dgu_ref[:, 0:D_FF] = (dact * up_ref[...].astype(F32) * (sg * (1.0 + gate * (1.0 - sg)))).astype(MM)
        dgu_ref[:, D_FF:2 * D_FF] = (dact * (gate * sg)).astype(MM)

    return pl.pallas_call(
        body, name="ffn_act_bwd", grid=(SEQ // TM,),
        out_shape=jax.ShapeDtypeStruct((SEQ, 2 * D_FF), MM),
        in_specs=[_row_tile_spec(D_MODEL), _weight_spec(w_down_g.shape), _row_tile_spec(D_FF), _row_tile_spec(D_FF)],
        out_specs=_row_tile_spec(2 * D_FF),
        compiler_params=_params(("arbitrary",)),
    )(dff, w_down_g, gate, up)


def ffn_in_bwd(dgu, w_up_g, x2, mix, dy, g3, g2):
    def body(dgu_ref, w_ref, x2_ref, mix_ref, dy_ref, g3_ref, g2_ref, dx2_ref, dmix_ref, dg3_ref, dg2_ref):
        dh2 = None
        for j in range(N_CHIPS):
            t = _dot_nt(dgu_ref[:, j * UP_SHARD:(j + 1) * UP_SHARD], w_ref[j])
            dh2 = t if dh2 is None else dh2 + t
        first = pl.program_id(0) == 0
        n3, r3 = _rms(x2_ref[...])
        dx2 = dy_ref[...] + _rms_bwd(dh2 * g3_ref[...], n3, r3)
        dx2_ref[...] = dx2
        _acc_rows(dg3_ref, jnp.sum(dh2 * n3, axis=0, keepdims=True), first)
        n2, r2 = _rms(mix_ref[...])
        dmix_ref[...] = _rms_bwd(dx2 * g2_ref[...], n2, r2).astype(MM)
        _acc_rows(dg2_ref, jnp.sum(dx2 * n2, axis=0, keepdims=True), first)

    vec = _full_spec((1, D_MODEL))
    return pl.pallas_call(
        body, name="ffn_in_bwd", grid=(SEQ // TM,),
        out_shape=(jax.ShapeDtypeStruct((SEQ, D_MODEL), F32), jax.ShapeDtypeStruct((SEQ, D_MODEL), MM),
                   jax.ShapeDtypeStruct((1, D_MODEL), F32), jax.ShapeDtypeStruct((1, D_MODEL), F32)),
        in_specs=[_row_tile_spec(2 * D_FF), _weight_spec(w_up_g.shape), _row_tile_spec(D_MODEL),
                  _row_tile_spec(D_MODEL), _row_tile_spec(D_MODEL), vec, vec],
        out_specs=[_row_tile_spec(D_MODEL), _row_tile_spec(D_MODEL), vec, vec],
        compiler_params=_params(("arbitrary",)),
    )(dgu, w_up_g, x2, mix, dy, g3, g2)


def merge_bwd(dmix, w_out_g, gc, ga, u3, att, w_cb_g, b_cb, w_ab_g, after):
    def body(dmix_ref, wout_ref, gc_ref, ga_ref, u_ref, a_ref, wcb_ref, bcb_ref, wab_ref, after_ref,
             dco_ref, dao_ref, dg_ref, du3_ref, datt_ref, dbcb_ref):
        dm = _dot_nt(dmix_ref[...], wout_ref[...])
        co, ao = _branch_outputs(u_ref, a_ref, wcb_ref, bcb_ref, wab_ref)
        sgc = _sigmoid(gc_ref[...])
        sga = _sigmoid(ga_ref[...])
        dco = dm * sgc
        dao = dm * sga
        dg_ref[:, 0:D_MODEL] = (dm * co * (sgc * (1.0 - sgc))).astype(MM)
        dg_ref[:, D_MODEL:2 * D_MODEL] = (dm * ao * (sga * (1.0 - sga))).astype(MM)
        _acc_rows(dbcb_ref, jnp.sum(dco, axis=0, keepdims=True), pl.program_id(0) == 0)
        dco_ref[...] = dco.astype(MM)
        dao_ref[...] = dao.astype(MM)
        du3 = None
        datt = None
        for j in range(N_CHIPS):
            cols = slice(j * BR_SHARD, (j + 1) * BR_SHARD)
            t = _dot_nt(dco_ref[:, cols], wcb_ref[j])
            s = _dot_nt(dao_ref[:, cols], wab_ref[j])
            du3 = t if du3 is None else du3 + t
            datt = s if datt is None else datt + s
        du3_ref[...] = du3
        datt_ref[...] = datt.astype(MM)

    wide = _row_tile_spec(D_MODEL)
    return pl.pallas_call(
        body, name="merge_bwd", grid=(SEQ // TM,),
        out_shape=(jax.ShapeDtypeStruct((SEQ, D_MODEL), MM), jax.ShapeDtypeStruct((SEQ, D_MODEL), MM),
                   jax.ShapeDtypeStruct((SEQ, 2 * D_MODEL), MM),
                   jax.ShapeDtypeStruct((SEQ, CONV_DIM), F32), jax.ShapeDtypeStruct((SEQ, ATT_DIM), MM),
                   jax.ShapeDtypeStruct((1, D_MODEL), F32)),
        in_specs=[wide, _weight_spec(w_out_g.shape), wide, wide, _row_tile_spec(CONV_DIM), _row_tile_spec(ATT_DIM),
                  _weight_spec(w_cb_g.shape), _full_spec((1, D_MODEL)), _weight_spec(w_ab_g.shape), TOKEN_SPEC],
        out_specs=[wide, wide, _row_tile_spec(2 * D_MODEL), _row_tile_spec(CONV_DIM), _row_tile_spec(ATT_DIM),
                   _full_spec((1, D_MODEL))],
        compiler_params=_params(("arbitrary",)),
    )(dmix, w_out_g, gc, ga, u3, att, w_cb_g, b_cb, w_ab_g, after)


def conv_bwd(du3, u1, ci, w_dw, ln_g, ln_b, after):
    def body(du3_hbm, u1_hbm, ci_hbm, w_ref, g_ref, bb_ref, after_ref,
             dci_ref, dw_ref, dbdw_ref, dg_ref, db_ref, upad_ref, dpad_ref, dwacc_ref, vacc_ref,
             du3_ref, u1_ref, ci_ref, ci_sems, u1_sems, du3_sems):
        ci_chunk = _row_chunks(ci_hbm, ci_ref, ci_sems)
        u1_chunk = _row_chunks(u1_hbm, u1_ref, u1_sems)
        du3_chunk = _row_chunks(du3_hbm, du3_ref, du3_sems)
        _glu_into(ci_chunk, ci_ref, upad_ref)
        dpad_ref[SEQ:SEQ + 32, :] = jnp.zeros((32, CONV_DIM), F32)
        dwacc_ref[...] = jnp.zeros_like(dwacc_ref)
        vacc_ref[...] = jnp.zeros_like(vacc_ref)

        def fold8(t):
            s = t[0:8, :]
            for r in range(8, t.shape[0], 8):
                s = s + t[r:r + 8, :]
            return s

        def pass1(i, c):
            t0 = pl.multiple_of(i * CONV_TILE, CONV_TILE)
            gv = g_ref[...]
            for r in range(0, CONV_TILE, NORM_ROWS):
                rows = pl.ds(t0 + r, NORM_ROWS)
                xh, rstd = _layernorm_parts(u1_ref[rows, :])
                u2 = xh * gv + bb_ref[...]
                s2 = _sigmoid(u2)
                du2 = du3_ref[rows, :] * (s2 * (1.0 + u2 * (1.0 - s2)))
                wv = du2 * gv
                du1 = rstd * (wv - jnp.mean(wv, axis=-1, keepdims=True)
                              - xh * jnp.mean(wv * xh, axis=-1, keepdims=True))
                dpad_ref[rows, :] = du1
                vacc_ref[0] += fold8(du2 * xh)
                vacc_ref[1] += fold8(du2)
                vacc_ref[2] += fold8(du1)
            for cols in LANE_GROUPS:
                du1 = dpad_ref[pl.ds(t0, CONV_TILE), cols]
                for m, rows in _shifted_windows(upad_ref, t0, cols, range(2, CONV_WIDTH + 2)):
                    dwacc_ref[m - 2, :, cols] += fold8(du1 * rows)
            return c

        tiles_per_chunk = GLU_ROWS // CONV_TILE

        def pass1_chunk(ch, c):
            u1_chunk(ch).wait()
            du3_chunk(ch).wait()
            return lax.fori_loop(ch * tiles_per_chunk, (ch + 1) * tiles_per_chunk, pass1, c)

        lax.fori_loop(0, SEQ // GLU_ROWS, pass1_chunk, 0)

        def pass2(i, c):
            t0 = pl.multiple_of(i * CONV_TILE, CONV_TILE)
            tile = pl.ds(t0, CONV_TILE)
            for cols in LANE_GROUPS:
                gate_cols = slice(cols.start + CONV_DIM, cols.stop + CONV_DIM)
                du0 = _shifted_sum(dpad_ref, t0, cols, w_ref, lambda j: 30 - j)
                a = ci_ref[tile, cols]
                sb = _sigmoid(ci_ref[tile, gate_cols])
                dci_ref[tile, cols] = (du0 * sb).astype(MM)
                dci_ref[tile, gate_cols] = (du0 * a * (sb * (1.0 - sb))).astype(MM)
            return c

        lax.fori_loop(0, SEQ // CONV_TILE, pass2, 0)

        for j in range(CONV_WIDTH):
            dw_ref[j:j + 1, :] = jnp.sum(dwacc_ref[j], axis=0, keepdims=True)
        dw_ref[CONV_WIDTH:32, :] = jnp.zeros((32 - CONV_WIDTH, CONV_DIM), F32)
        dg_ref[...] = jnp.sum(vacc_ref[0], axis=0, keepdims=True)
        db_ref[...] = jnp.sum(vacc_ref[1], axis=0, keepdims=True)
        dbdw_ref[...] = jnp.sum(vacc_ref[2], axis=0, keepdims=True)

    vec = jax.ShapeDtypeStruct((1, CONV_DIM), F32)
    return pl.pallas_call(
        body, name="conv_bwd",
        out_shape=(jax.ShapeDtypeStruct((SEQ, 2 * CONV_DIM), MM), jax.ShapeDtypeStruct((32, CONV_DIM), F32),
                   vec, vec, vec),
        in_specs=[ANY] * 3 + [VMEM_SPEC] * 4, out_specs=[VMEM_SPEC] * 5,
        scratch_shapes=[pltpu.VMEM((SEQ + 32, CONV_DIM), F32), pltpu.VMEM((SEQ + 32, CONV_DIM), F32),
                        pltpu.VMEM((CONV_WIDTH, 8, CONV_DIM), F32), pltpu.VMEM((3, 8, CONV_DIM), F32)]
                       + [pltpu.VMEM(a.shape, a.dtype) for a in (du3, u1, ci)]
                       + [pltpu.SemaphoreType.DMA((SEQ // GLU_ROWS,))] * 3,
        compiler_params=_params(),
    )(du3, u1, ci, w_dw, ln_g, ln_b, after)


def attn_bwd(q, k, v, datt, rc, after):
    def body(q_hbm, k_hbm, v_hbm, do_hbm, rc_hbm, after_ref, dqkv_ref, dqa_ref, dka_ref, dva_ref, pc_ref, z_ref,
             sig1_ref, sig2_ref, g_ref, spb_ref, gb_ref, ar_ref, dzr_ref, dzc_ref, qm_ref, km_ref, dom_ref,
             q_ref, k_ref, v_ref, do_ref, rc_ref, sems):
        arrive = _fetch((q_hbm, k_hbm, do_hbm, v_hbm, rc_hbm), (q_ref, k_ref, do_ref, v_ref, rc_ref), sems)
        lane, row, _ = _head_masks()
        for ref in (dqa_ref, dka_ref, dva_ref, pc_ref):
            ref[...] = jnp.zeros_like(ref)
        z_ref[...] = jnp.full(z_ref.shape, NO_SCORE, F32)
        for ref in (sig1_ref, sig2_ref, spb_ref, ar_ref, g_ref, gb_ref, dzr_ref, dzc_ref):
            ref[...] = jnp.zeros_like(ref)
        for cp, (src_ref, split_ref) in zip(arrive, ((q_ref, qm_ref), (k_ref, km_ref), (do_ref, dom_ref))):
            cp.wait()
            _split_heads(src_ref, split_ref)
        arrive[3].wait()
        arrive[4].wait()
        w_suffix = _cumsum_weights(suffix=True, with_total=False)
        w_prefix = _cumsum_weights(suffix=False, with_total=True)

        def step(pairs):
            (ia, ja), (ib, jb), (ic, jc), (id_, jd) = pairs
            ka, qb_, kb_, kc, qd, kd = (pl.multiple_of(jnp.maximum(b, 0) * TQ, TQ) for b in (ja, ib, jb, jc, id_, jd))
            qa2, qb2, qc2, qd2, kd2 = (pl.multiple_of(jnp.maximum(b, 0) * 2 * TQ, 2 * TQ)
                                       for b in (ia, ib, ic, id_, jd))
            bias_a = _score_bias(lane, row, ia, ja)
            rc_rows = rc_ref[pl.ds(qb_, TQ), :]
            first_c = jc == 0
            for p in range(N_PAIRS):
                cols = slice(128 * p, 128 * (p + 1))
                k_a = k_ref[pl.ds(ka, TQ), cols]
                v_b = v_ref[pl.ds(kb_, TQ), cols]
                dqa_ref[pl.ds(qd, TQ), cols] += _dot(dzc_ref[p], km_ref[p, pl.ds(kd2, 2 * TQ), :])
                dka_ref[pl.ds(kd, TQ), cols] += _dot_tn(dzr_ref[p], qm_ref[p, pl.ds(qd2, 2 * TQ), :])
                dva_ref[pl.ds(kc, TQ), cols] += _dot_tn(ar_ref[p], dom_ref[p, pl.ds(qc2, 2 * TQ), :])
                for h in range(2):
                    hh = 2 * p + h
                    rows = slice(TQ * h, TQ * (h + 1))
                    r = _dot(gb_ref[hh], w_prefix)
                    p_in = jnp.where(first_c, 0.0, pc_ref[hh])
                    dz = (g_ref[hh] - sig2_ref[hh] * (r[:, :128] + p_in)).astype(MM)
                    dzc_ref[p, :, rows] = dz
                    dzr_ref[p, rows, :] = dz
                    pc_ref[hh] = p_in + r[:, 128:]
                    r_in = jnp.sum(jnp.where(lane == 16 * hh + jb, rc_rows, 0.0), axis=1, keepdims=True)
                    a = jnp.exp(z_ref[hh] - (_dot(spb_ref[hh], w_suffix) + r_in))
                    g = _dot_nt(dom_ref[p, pl.ds(qb2 + TQ * h, TQ), :], v_b) * a
                    ar_ref[p, rows, :] = a.astype(MM)
                    g_ref[hh] = g
                    gb_ref[hh] = g.astype(MM)
                    sig2_ref[hh] = sig1_ref[hh]
                    z = _dot_nt(qm_ref[p, pl.ds(qa2 + TQ * h, TQ), :], k_a) + bias_a
                    sp = _softplus(z)
                    sig1_ref[hh] = jnp.exp(z - sp)
                    z_ref[hh] = z
                    spb_ref[hh] = sp.astype(MM)

        _block_pipeline(4, False, step)
        dqkv_ref[:, 0:ATT_DIM] = (dqa_ref[...] * ATT_SCALE).astype(MM)
        dqkv_ref[:, ATT_DIM:2 * ATT_DIM] = dka_ref[...].astype(MM)
        dqkv_ref[:, 2 * ATT_DIM:3 * ATT_DIM] = dva_ref[...].astype(MM)

    split = pltpu.VMEM((N_PAIRS, 2 * SEQ, 128), MM)
    return pl.pallas_call(
        body, name="attn_bwd", out_shape=jax.ShapeDtypeStruct((SEQ, 3 * ATT_DIM), MM),
        in_specs=[ANY] * 5 + [VMEM_SPEC], out_specs=VMEM_SPEC,
        scratch_shapes=[pltpu.VMEM((SEQ, ATT_DIM), F32)] * 3 + [pltpu.VMEM((8, TQ, 128), F32)] * 5
                       + [pltpu.VMEM((8, TQ, 128), MM)] * 2
                       + [pltpu.VMEM((N_PAIRS, 2 * TQ, 128), MM)] * 2 + [pltpu.VMEM((N_PAIRS, TQ, 256), MM)]
                       + [split] * 3
                       + [pltpu.VMEM(a.shape, a.dtype) for a in (q, k, v, datt, rc)] + [pltpu.SemaphoreType.DMA((5,))],
        compiler_params=_params(),
    )(q, k, v, datt, rc, after)


DPROJ_PIECES = ((0, 1024), (1024, 2560), (2560, 4608))


def _dproj_segments(j):
    g0, g1 = j * IN_SHARD, (j + 1) * IN_SHARD
    segs = []
    for p, (s, e) in enumerate(DPROJ_PIECES):
        lo, hi = max(s, g0), min(e, g1)
        if lo < hi:
            segs.append((p, lo - s, lo - g0, hi - lo))
    return segs


def in_proj_bwd(pieces, w_in_g, x, dx2, g1, after):
    def body(p0_ref, p1_ref, p2_ref, w_ref, x_ref, dx2_ref, g_ref, after_ref, dx_ref, dg_ref):
        p_refs = (p0_ref, p1_ref, p2_ref)
        dh = None
        for j in range(N_CHIPS):
            for p, lo, off, width in _dproj_segments(j):
                t = _dot_nt(p_refs[p][:, lo:lo + width], w_ref[j, :, off:off + width])
                dh = t if dh is None else dh + t
        n1, r1 = _rms(x_ref[...])
        dx_ref[...] = dx2_ref[...] + _rms_bwd(dh * g_ref[...], n1, r1)
        _acc_rows(dg_ref, jnp.sum(dh * n1, axis=0, keepdims=True), pl.program_id(0) == 0)

    vec = _full_spec((1, D_MODEL))
    return pl.pallas_call(
        body, name="in_proj_bwd", grid=(SEQ // TM,),
        out_shape=[jax.ShapeDtypeStruct((SEQ, D_MODEL), F32), jax.ShapeDtypeStruct((1, D_MODEL), F32)],
        in_specs=[_row_tile_spec(p.shape[1]) for p in pieces]
                 + [_weight_spec(w_in_g.shape), _row_tile_spec(D_MODEL), _row_tile_spec(D_MODEL), vec, TOKEN_SPEC],
        out_specs=[_row_tile_spec(D_MODEL), vec],
        compiler_params=_params(("arbitrary",)),
    )(*pieces, w_in_g, x, dx2, g1, after)


def weight_grad_in(h1, pieces):
    kh = D_MODEL // 2
    operands = (h1,) + tuple(pieces)
    out = jax.ShapeDtypeStruct((N_CHIPS, 2, kh, IN_SHARD), MM)

    def body(*refs):
        hbm, o_hbm, bufs, stage_ref, in_sems, out_sems = refs[:4], refs[4], refs[5:9], refs[9], refs[10], refs[11]
        arrive = _fetch(hbm, bufs, in_sems)
        a_ref, p_refs = bufs[0], bufs[1:]
        arrive[0].wait()
        here, leaving = set(), []
        for j in range(N_CHIPS):
            for p in sorted({seg[0] for seg in _dproj_segments(j)} - here):
                arrive[1 + p].wait()
                here.add(p)
            for h in range(2):
                a = a_ref[:, h * kh:(h + 1) * kh]
                for p, lo, off, width in _dproj_segments(j):
                    stage_ref[j, h, :, off:off + width] = _dot_tn(a, p_refs[p][:, lo:lo + width]).astype(MM)
                leaving.append(pltpu.make_async_copy(stage_ref.at[j, h], o_hbm.at[j, h], out_sems.at[2 * j + h]))
                leaving[-1].start()
        for cp in leaving:
            cp.wait()

    return pl.pallas_call(
        body, name="dw_in", out_shape=out, in_specs=[ANY] * 4, out_specs=ANY,
        scratch_shapes=[pltpu.VMEM(a.shape, a.dtype) for a in operands]
                       + [pltpu.VMEM(out.shape, out.dtype), pltpu.SemaphoreType.DMA((4,)),
                          pltpu.SemaphoreType.DMA((2 * N_CHIPS,))],
        compiler_params=_params(),
    )(*operands)


def weight_grad(a, b, name, col_sharded, tk=None):
    kin, n = a.shape[1], b.shape[1]

    def body(a_ref, b_ref, o_ref):
        if col_sharded:
            o_ref[0, 0] = _dot_tn(a_ref[...], b_ref[...]).astype(MM)
        else:
            o_ref[...] = _dot_tn(a_ref[...], b_ref[...]).astype(MM)

    if col_sharded:
        kh, ns = kin // 2, n // N_CHIPS
        out = jax.ShapeDtypeStruct((N_CHIPS, 2, kh, ns), MM)
        grid = (2, N_CHIPS)
        in_specs = [pl.BlockSpec((SEQ, kh), lambda h, j: (0, h)), pl.BlockSpec((SEQ, ns), lambda h, j: (0, j))]
        out_spec = pl.BlockSpec((1, 1, kh, ns), lambda h, j: (j, h, 0, 0))
        sem = ("arbitrary", "arbitrary")
    else:
        out = jax.ShapeDtypeStruct((kin, n), MM)
        grid = (kin // tk,)
        in_specs = [pl.BlockSpec((SEQ, tk), lambda r: (0, r)), pl.BlockSpec((SEQ, n), lambda r: (0, 0))]
        out_spec = pl.BlockSpec((tk, n), lambda r: (r, 0))
        sem = ("arbitrary",)
    res = pl.pallas_call(
        body, name=name, grid=grid, out_shape=out, in_specs=in_specs, out_specs=out_spec,
        compiler_params=_params(sem),
    )(a, b)
    if not col_sharded:
        res = res.reshape(N_CHIPS, 2, kin // (2 * N_CHIPS), n)
    return res


def weight_grad_mix(merged, dmix, u3, dco, att, dao):
    operands = (merged, dmix, u3, dco, att, dao)
    n_out, n_br = D_MODEL // 2, CONV_DIM // 2

    def body(*refs):
        hbm, (o_out, o_cb, o_ab), bufs, sems = refs[:6], refs[6:9], refs[9:15], refs[15]
        copies = [pltpu.make_async_copy(hbm[i], bufs[i], sems.at[i]) for i in range(6)]
        for cp in copies:
            cp.start()
        m_ref, dm_ref, u_ref, dco_ref, a_ref, dao_ref = bufs
        copies[0].wait()
        copies[1].wait()
        for h in range(2):
            o_out[h * n_out:(h + 1) * n_out, :] = _dot_tn(m_ref[:, h * n_out:(h + 1) * n_out], dm_ref[...]).astype(MM)
        for br, (a, d, o) in enumerate(((u_ref, dco_ref, o_cb), (a_ref, dao_ref, o_ab))):
            copies[2 + 2 * br].wait()
            copies[3 + 2 * br].wait()
            for h in range(2):
                g = _dot_tn(a[:, h * n_br:(h + 1) * n_br], d[...])
                for j in range(N_CHIPS):
                    o[j, h] = g[:, j * BR_SHARD:(j + 1) * BR_SHARD].astype(MM)

    branch = jax.ShapeDtypeStruct((N_CHIPS, 2, n_br, BR_SHARD), MM)
    dw_out, dw_cb, dw_ab = pl.pallas_call(
        body, name="dw_mix", out_shape=[jax.ShapeDtypeStruct((D_MODEL, D_MODEL), MM), branch, branch],
        in_specs=[ANY] * 6, out_specs=[VMEM_SPEC] * 3,
        scratch_shapes=[pltpu.VMEM(a.shape, a.dtype) for a in operands] + [pltpu.SemaphoreType.DMA((6,))],
        compiler_params=_params(),
    )(*operands)
    return dw_out.reshape(N_CHIPS, 2, D_MODEL // (2 * N_CHIPS), D_MODEL), dw_cb, dw_ab


def _place():
    x, y, c = lax.axis_index("x"), lax.axis_index("y"), lax.axis_index("c")
    chips = [(1 - x, y), (x, 1 - y), (1 - x, 1 - y)]
    return x, y, c, chips


def _rcopy(src, dst, send_sem, recv_sem, dev):
    return pltpu.make_async_remote_copy(src_ref=src, dst_ref=dst, send_sem=send_sem, recv_sem=recv_sem,
                                        device_id=dev, device_id_type=MESH)


class _Gather:
    N_MOVES = 6

    def __init__(self, shapes, w, o, scratch):
        self.n, self.shapes, self.w, self.o = len(w), shapes, w, o
        self.send, self.recv, self.psend, self.precv, self.loc_in, self.loc_out = scratch[:6]
        self.raw, self.stage = scratch[6:6 + self.n], scratch[6 + self.n:]
        x, y, c, self.chips = _place()
        self.c = c
        self.me, k_x, k_y, k_far = 2 * x + y, 2 * (1 - x) + y, 2 * x + (1 - y), 2 * (1 - x) + (1 - y)
        to_x, to_y = (1 - x, y, c), (x, 1 - y, c)
        self.sib = (x, y, 1 - c)
        self.sent_as = [(self.me, 0, to_x), (self.me, 1, to_y), (self.me, 1, to_x), (self.me, 0, to_y),
                        (k_x, 0, to_y), (k_y, 1, to_x)]
        self.arrives_as = [(k_x, 0, to_x), (k_y, 1, to_y), (k_x, 1, to_x), (k_y, 0, to_y),
                           (k_far, 0, to_y), (k_far, 1, to_x)]
        self.sent_on_after = {0: 4, 1: 5}

    @staticmethod
    def scratch(shards):
        n = len(shards)
        sems = pltpu.SemaphoreType.DMA
        m = _Gather.N_MOVES * n
        return ([sems((m,)), sems((m,)), sems((m,)), sems((m,)), sems((3 * n,)), sems((n,))]
                + [pltpu.VMEM(s.shape, s.dtype) for s in shards] + [pltpu.VMEM(s.shape, MM) for s in shards])

    @staticmethod
    def out_shapes(shards):
        return [jax.ShapeDtypeStruct((N_CHIPS,) + s.shape, MM) for s in shards]

    def _rows(self, t, quarter, cc):
        rq = self.shapes[t][0] // 4
        return pl.ds(pl.multiple_of((2 * cc + quarter) * rq, rq), rq)

    def _own_rows(self, t, piece):
        if piece < 2:
            return self._rows(t, piece, self.c)
        rh = self.shapes[t][0] // 2
        return pl.ds(pl.multiple_of((1 - self.c) * rh, rh), rh)

    def _chip(self, j):
        cx, cy = self.chips[j]
        return 2 * cx + cy, (cx, cy, self.c)

    def local_in(self, t, piece):
        rows = self._own_rows(t, piece)
        return pltpu.make_async_copy(self.w[t].at[rows, :], self.raw[t].at[rows, :], self.loc_in.at[3 * t + piece])

    def local_out(self, t):
        return pltpu.make_async_copy(self.stage[t], self.o[t].at[self.me], self.loc_out.at[t])

    def sent(self, i, t):
        k, quarter, dev = self.sent_as[i]
        rows = self._rows(t, quarter, self.c)
        there = self.o[t].at[k, rows, :]
        return _rcopy(self.stage[t].at[rows, :] if i < 4 else there, there,
                      self.send.at[i * self.n + t], self.recv.at[i * self.n + t], dev)

    def arrived(self, i, t):
        k, quarter, dev = self.arrives_as[i]
        blk = self.o[t].at[k, self._rows(t, quarter, self.c), :]
        return _rcopy(blk, blk, self.send.at[i * self.n + t], self.recv.at[i * self.n + t], dev)

    def passed(self, i, t, cc):
        k, quarter, _ = self.arrives_as[i]
        blk = self.o[t].at[k, self._rows(t, quarter, cc), :]
        return _rcopy(blk, blk, self.psend.at[i * self.n + t], self.precv.at[i * self.n + t], self.sib)

    def start(self):
        for piece in range(3):
            for t in range(self.n):
                self.local_in(t, piece).start()
        for piece, moves in enumerate(((0, 3), (1, 2), ())):
            for t in range(self.n):
                rows = self._own_rows(t, piece)
                self.local_in(t, piece).wait()
                self.stage[t][rows, :] = self.raw[t][rows, :].astype(MM)
                for i in moves:
                    self.sent(i, t).start()
        for t in range(self.n):
            self.local_out(t).start()

    def forward(self):
        for i in range(self.N_MOVES):
            for t in range(self.n):
                self.arrived(i, t).wait_recv()
                if i in self.sent_on_after:
                    self.sent(self.sent_on_after[i], t).start()
                self.passed(i, t, self.c).start()

    def finish(self):
        for i in range(self.N_MOVES):
            for t in range(self.n):
                self.passed(i, t, 1 - self.c).wait_recv()
        for i in range(self.N_MOVES):
            for t in range(self.n):
                self.sent(i, t).wait_send()
                self.passed(i, t, self.c).wait_send()
        for t in range(self.n):
            self.local_out(t).wait()


def all_gather_weights(shards, small, later):
    n, m = len(shards), len(later)
    shapes = [s.shape for s in shards]

    def body(*refs):
        w = refs[:n]
        sm = refs[n]
        lw = refs[n + 1:n + 1 + m]
        o = refs[n + 1 + m:2 * n + 1 + m]
        osm = refs[2 * n + 1 + m]
        lo = refs[2 * n + 2 + m:2 * n + 2 + 2 * m]
        scratch = refs[2 * n + 2 + 2 * m:]
        ssend, srecv, sloc, lsem_in, lsem_out = scratch[:5]
        lraw, lstage = scratch[5:5 + m], scratch[5 + m:5 + 2 * m]
        g = _Gather(shapes, w, o, scratch[5 + 2 * m:])
        own = pltpu.make_async_copy(sm, osm.at[g.me], sloc)
        own.start()
        g.start()
        loads = [pltpu.make_async_copy(lw[t], lraw[t], lsem_in.at[t]) for t in range(m)]
        for cp in loads:
            cp.start()
        small_cps = [_rcopy(sm, osm.at[g.me], ssend.at[j], srecv.at[j], g._chip(j)[1]) for j in range(3)]
        for cp in small_cps:
            cp.start()
        places = []
        for t in range(m):
            loads[t].wait()
            lstage[t][...] = lraw[t][...].astype(MM)
            places.append(pltpu.make_async_copy(lstage[t], lo[t].at[g.me], lsem_out.at[t]))
            places[t].start()
        g.forward()
        g.finish()
        for j in range(3):
            k, dev = g._chip(j)
            _rcopy(sm, osm.at[k], ssend.at[j], srecv.at[j], dev).wait_recv()
            small_cps[j].wait_send()
        own.wait()
        for cp in places:
            cp.wait()

    out_shape = _Gather.out_shapes(shards)
    out_shape.append(jax.ShapeDtypeStruct((N_CHIPS,) + small.shape, small.dtype))
    out_shape += _Gather.out_shapes(later)
    sems = pltpu.SemaphoreType.DMA
    return pl.pallas_call(
        body, name="all_gather_weights", out_shape=out_shape,
        in_specs=[ANY] * (n + 1 + m), out_specs=[ANY] * (n + 1 + m),
        scratch_shapes=[sems((3,)), sems((3,)), sems, sems((m,)), sems((m,))]
                       + [pltpu.VMEM(s.shape, s.dtype) for s in later] + [pltpu.VMEM(s.shape, MM) for s in later]
                       + _Gather.scratch(shards),
        compiler_params=_params(),
    )(*shards, small, *later)


HBM_SPEC = pl.BlockSpec(memory_space=pltpu.HBM)
SEM_SPEC = pl.BlockSpec(memory_space=pltpu.SEMAPHORE)
DATAFLOW = pltpu.SideEffectType.DATAFLOW_SIDE_EFFECTING


def split_start(name, bufs, n_copies, copies):
    nb = len(bufs)

    def body(*refs):
        for cp in copies(refs[:nb], refs[nb], refs[nb + 1]):
            cp.start()
        token = refs[2 * nb + 2]
        token[...] = jnp.zeros_like(token)

    sems = [pltpu.SemaphoreType.DMA((n_copies,))] * 2
    res = pl.pallas_call(
        body, name=name,
        out_shape=sems + [pltpu.HBM(a.shape, a.dtype) for a in bufs] + [jax.ShapeDtypeStruct((8, 128), F32)],
        in_specs=[HBM_SPEC] * nb, out_specs=[SEM_SPEC] * 2 + [HBM_SPEC] * nb + [VMEM_SPEC],
        input_output_aliases={i: 2 + i for i in range(nb)},
        compiler_params=pltpu.CompilerParams(has_side_effects=DATAFLOW),
    )(*[pltpu.with_memory_space_constraint(a, pltpu.HBM) for a in bufs])
    return res[:-1], res[-1]


def split_wait(name, state, after, copies):
    sems, bufs = state[:2], state[2:]
    nb = len(bufs)

    def body(*refs):
        for cp in copies(refs[:nb], refs[nb], refs[nb + 1]):
            cp.wait_send()
            cp.wait_recv()

    return pl.pallas_call(
        body, name=name, out_shape=[pltpu.HBM(a.shape, a.dtype) for a in bufs],
        in_specs=[HBM_SPEC] * nb + [SEM_SPEC] * 2 + [ANY] * len(after), out_specs=[HBM_SPEC] * nb,
        input_output_aliases={i: i for i in range(nb)},
        compiler_params=pltpu.CompilerParams(has_side_effects=DATAFLOW),
    )(*bufs, *sems, *after)


class _Shifted:
    def __init__(self, sems, first):
        self.sems, self.first = sems, first

    @property
    def at(self):
        return self

    def __getitem__(self, i):
        return self.sems.at[self.first + i]


def _scatter_copies(n):
    def copies(refs, send, recv):
        _, _, c, chips = _place()
        return [_rcopy(refs[t].at[2 * cx + cy], refs[n + t].at[j], send.at[3 * t + j], recv.at[3 * t + j], (cx, cy, c))
                for t in range(n) for j, (cx, cy) in enumerate(chips)]
    return copies


def _direct_copies(n):
    def copies(refs, send, recv):
        x, y, c, chips = _place()
        out = []
        for t in range(n):
            src, land = refs[t], refs[n + t]
            for to_core, first in ((c, 0), (1 - c, 3)):
                for j, (cx, cy) in enumerate(chips):
                    i = 7 * t + first + j
                    out.append(_rcopy(src.at[2 * cx + cy, to_core], land.at[first + j], send.at[i], recv.at[i],
                                      (cx, cy, to_core)))
            i = 7 * t + 6
            out.append(_rcopy(src.at[2 * x + y, 1 - c], land.at[6], send.at[i], recv.at[i], (x, y, 1 - c)))
        return out
    return copies


def _scatter_and_direct(n, m):
    def copies(refs, send, recv):
        return (_scatter_copies(n)(refs[:2 * n], send, recv)
                + _direct_copies(m)(refs[2 * n:], _Shifted(send, 3 * n), _Shifted(recv, 3 * n)))
    return copies


def scatter_start(parts, partials):
    n, m = len(parts), len(partials)
    lands = [lax.empty((3,) + p.shape[1:], p.dtype) for p in parts]
    direct_lands = [lax.empty((7,) + p.shape[2:], p.dtype) for p in partials]
    return split_start("scatter_start_rest", list(parts) + lands + list(partials) + direct_lands, 3 * n + 7 * m,
                       _scatter_and_direct(n, m))


def scatter_wait(state, after, n, m):
    res = split_wait("scatter_wait_rest", state, after, _scatter_and_direct(n, m))
    return res[n:2 * n], res[2 * n:2 * n + m], res[2 * n + m:]


def _gather_copies(shapes, level, to_both_cores=()):
    n = len(shapes)

    def copies(refs, send, recv):
        x, y, c, chips = _place()
        out = []
        for t in list(range(n)) + list(to_both_cores):
            rh = shapes[t][0] // 2
            to_core = c if len(out) < 3 * n else 1 - c
            for cx, cy in chips:
                k, dev = (2 * x + y, (cx, cy, to_core)) if level == 1 else (2 * cx + cy, (x, y, 1 - c))
                blk = refs[t].at[k, pl.ds(c * rh, rh), :]
                out.append(_rcopy(blk, blk, send.at[len(out)], recv.at[len(out)], dev))
        return out
    return copies


def _sibling_copies(n, other_half):
    def copies(refs, send, recv):
        x, y, c, _ = _place()
        return [_rcopy(refs[t].at[:, 1 - c] if other_half else refs[t], refs[n + t], send.at[t], recv.at[t],
                       (x, y, 1 - c)) for t in range(n)]
    return copies


def sibling_start(srcs, other_half, tag):
    lands = [lax.empty((a.shape[0],) + a.shape[2:] if other_half else a.shape, a.dtype) for a in srcs]
    return split_start("sibling_start_" + tag, list(srcs) + lands, len(srcs),
                       _sibling_copies(len(srcs), other_half))


def sibling_wait(state, after, other_half, tag):
    n = (len(state) - 2) // 2
    res = split_wait("sibling_wait_" + tag, state, after, _sibling_copies(n, other_half))
    return res[:n], res[n:]


def small_pack(ddw, v512, v1024, loss_parts):
    rows, width = PACK_ROWS, 512
    n512, n1024 = len(VEC512), len(VEC1024)

    def body(*refs):
        ddw_ref = refs[0]
        a_refs = refs[1:1 + n512]
        b_refs = refs[1 + n512:1 + n512 + n1024]
        lp_ref, o_ref, p_ref = refs[1 + n512 + n1024:]
        p_ref[...] = jnp.zeros_like(p_ref)
        p_ref[0:32, :] = ddw_ref[...]
        p_ref[LOSS_ROW:LOSS_ROW + 1, 0:128] = jnp.sum(lp_ref[...], axis=0, keepdims=True) * 0.125
        for i, r in enumerate(a_refs):
            p_ref[32 + i:33 + i, :] = r[...]
        for i, r in enumerate(b_refs):
            base = 32 + n512 + 2 * i
            p_ref[base:base + 1, :] = r[:, 0:512]
            p_ref[base + 1:base + 2, :] = r[:, 512:1024]
        x, y, c, _ = _place()
        o_ref[4 * x + 2 * y + c] = p_ref[...]

    n_in = 2 + n512 + n1024
    return pl.pallas_call(
        body, name="small_pack", out_shape=jax.ShapeDtypeStruct((8, rows, width), F32),
        in_specs=[VMEM_SPEC] * n_in, out_specs=VMEM_SPEC,
        scratch_shapes=[pltpu.VMEM((rows, width), F32)],
    )(ddw, *[v512[n] for n in VEC512], *[v1024[n] for n in VEC1024], loss_parts)


def _small_copies(refs, send, recv):
    x, y, c, _ = _place()
    mine = refs[0].at[4 * x + 2 * y + c]
    peers = [(1 - x if k & 4 else x, 1 - y if k & 2 else y, 1 - c if k & 1 else c) for k in range(1, 8)]
    return [_rcopy(mine, mine, send.at[i], recv.at[i], dev) for i, dev in enumerate(peers)]


def _row_block(r):
    for tr in (512, 352, 256, 128):
        if r % tr == 0:
            return tr
    return r


def add_halves(g, recv, name):
    _, _, r, w = g.shape
    tr = _row_block(r)

    def body(g_ref, r_ref, ob_ref, own_ref):
        k = pl.program_id(1)
        me = 2 * lax.axis_index("x") + lax.axis_index("y")
        t = g_ref[0, 0].astype(F32) + r_ref[0].astype(F32)
        ob_ref[0] = t.astype(MM)
        mine = jnp.where(k == me, t, 0.0)

        @pl.when(k == 0)
        def _():
            own_ref[...] = mine

        @pl.when(k != 0)
        def _():
            own_ref[...] += mine

    return pl.pallas_call(
        body, name=name, grid=(r // tr, N_CHIPS),
        in_specs=[pl.BlockSpec((1, 1, tr, w), lambda i, k: (k, lax.axis_index("c"), i, 0)),
                  pl.BlockSpec((1, tr, w), lambda i, k: (k, i, 0))],
        out_specs=[pl.BlockSpec((1, tr, w), lambda i, k: (k, i, 0)),
                   pl.BlockSpec((tr, w), lambda i, k: (i, 0))],
        out_shape=(jax.ShapeDtypeStruct((N_CHIPS, r, w), MM), jax.ShapeDtypeStruct((r, w), F32)),
        compiler_params=_params(("arbitrary", "arbitrary")),
    )(g, recv)


def sum_parts(own, rin, after, name):
    _, r, w = rin.shape
    tr = _row_block(r)

    def body(o_ref, r_ref, after_ref, out_ref):
        out_ref[...] = ((o_ref[...] + r_ref[0].astype(F32)) + r_ref[1].astype(F32)) + r_ref[2].astype(F32)

    return pl.pallas_call(
        body, name=name, grid=(r // tr,), out_shape=jax.ShapeDtypeStruct((r, w), F32),
        in_specs=[pl.BlockSpec((tr, w), lambda i: (i, 0)), pl.BlockSpec((3, tr, w), lambda i: (0, i, 0)),
                  TOKEN_SPEC],
        out_specs=pl.BlockSpec((tr, w), lambda i: (i, 0)),
        compiler_params=_params(("arbitrary",)),
    )(own, rin, after)


def sum_partials(p, land, after, name):
    _, _, r, w = p.shape
    tr = _row_block(r)

    def body(p_ref, l_ref, after_ref, out_ref):
        total = p_ref[0, 0].astype(F32)
        for slot in (6, 0, 3, 1, 4, 2, 5):
            total = total + l_ref[slot].astype(F32)
        out_ref[...] = total

    def own(i):
        return 2 * lax.axis_index("x") + lax.axis_index("y"), lax.axis_index("c"), i, 0

    return pl.pallas_call(
        body, name=name, grid=(r // tr,), out_shape=jax.ShapeDtypeStruct((r, w), F32),
        in_specs=[pl.BlockSpec((1, 1, tr, w), own), pl.BlockSpec((7, tr, w), lambda i: (0, i, 0)), TOKEN_SPEC],
        out_specs=pl.BlockSpec((tr, w), lambda i: (i, 0)),
        compiler_params=_params(("arbitrary",)),
    )(p, land, after)


def _adamw_math(w, g, m, v):
    mn = ADAM_B1 * m + (1.0 - ADAM_B1) * g
    vn = ADAM_B2 * v + (1.0 - ADAM_B2) * (g * g)
    m_hat = mn / (1.0 - ADAM_B1 ** ADAM_STEP)
    v_hat = vn / (1.0 - ADAM_B2 ** ADAM_STEP)
    return -ADAM_LR * (m_hat / (jnp.sqrt(v_hat) + ADAM_EPS) + ADAM_WD * w), mn, vn


def adamw(w, mine, other, m, v, name):
    r, c = w.shape
    rh = r // 2
    tr = _row_block(rh)
    if c >= 1024 and tr % 512 == 0:
        tr = 256
    nb = rh // tr

    def body(w_ref, a_ref, b_ref, m_ref, v_ref, go_ref, d_ref, mo_ref, vo_ref):
        gv = jnp.where(lax.axis_index("c") == pl.program_id(0), a_ref[...], b_ref[...])
        go_ref[...] = gv
        d_ref[...], mo_ref[...], vo_ref[...] = _adamw_math(w_ref[...], gv, m_ref[...], v_ref[...])

    def half(of_sibling):
        def index(h, i):
            owner = lax.axis_index("c")
            owner = 1 - owner if of_sibling else owner
            return jnp.where(h == owner, i, jnp.where(h < owner, 0, nb - 1)), 0
        return pl.BlockSpec((tr, c), index)

    spec = pl.BlockSpec((tr, c), lambda h, i: (h * nb + i, 0))
    out = jax.ShapeDtypeStruct((r, c), F32)
    return pl.pallas_call(
        body, name=name, grid=(2, nb), out_shape=(out, out, out, out),
        in_specs=[spec, half(False), half(True), spec, spec], out_specs=[spec] * 4,
        compiler_params=_params(("arbitrary", "arbitrary")),
    )(w, mine, other, m, v)


def adamw_small(packs, params, after):
    names = list(params)
    flat = [a for n in names for a in params[n]]

    def body(*refs):
        p_ref = refs[0]
        ins = refs[1:1 + 3 * len(names)]
        loss_ref, g_ref = refs[2 + 3 * len(names):4 + 3 * len(names)]
        outs = refs[4 + 3 * len(names):]
        total = p_ref[0]
        for d in range(1, 8):
            total = total + p_ref[d]
        g_ref[...] = total
        loss_ref[...] = g_ref[LOSS_ROW:LOSS_ROW + 1, 0:1]
        me = 2 * lax.axis_index("x") + lax.axis_index("y")
        for i, n in enumerate(names):
            w_ref, m_ref, v_ref = ins[3 * i:3 * i + 3]
            go_ref, d_ref, mo_ref, vo_ref = outs[4 * i:4 * i + 4]
            if n == "conv_dw_w":
                gv = jnp.zeros((CONV_WIDTH, 128), F32)
                for k in range(N_CHIPS):
                    gv = gv + jnp.where(me == k, g_ref[0:CONV_WIDTH, 128 * k:128 * (k + 1)], 0.0)
            elif n in VEC512:
                r0 = 32 + VEC512.index(n)
                gv = g_ref[r0:r0 + 1, :]
            else:
                r0 = 32 + len(VEC512) + 2 * VEC1024.index(n)
                gv = jnp.concatenate([g_ref[r0:r0 + 1, :], g_ref[r0 + 1:r0 + 2, :]], axis=1)
            go_ref[...] = gv
            d_ref[...], mo_ref[...], vo_ref[...] = _adamw_math(w_ref[...], gv, m_ref[...], v_ref[...])

    out_shape = [jax.ShapeDtypeStruct((1, 1), F32), jax.ShapeDtypeStruct(packs.shape[1:], F32)]
    out_shape += [jax.ShapeDtypeStruct(params[n][0].shape, F32) for n in names for _ in range(4)]
    res = pl.pallas_call(
        body, name="adamw_small", out_shape=out_shape,
        in_specs=[VMEM_SPEC] * (2 + len(flat)), out_specs=[VMEM_SPEC] * len(out_shape),
        compiler_params=_params(),
    )(packs, *flat, after)
    return res[0], res[1], {n: res[2 + 4 * i:6 + 4 * i] for i, n in enumerate(names)}


REST = ("w_ffn_up", "w_ffn_down", "w_out", "w_conv_branch", "w_att_branch")
VEC512 = ("conv_dw_b", "conv_ln_g", "conv_ln_b")
VEC1024 = ("norm_mix_pre", "b_conv_branch", "norm_mix_post", "norm_ffn_pre", "norm_ffn_post")
PACK_ROWS = 48
LOSS_ROW = 47


def kernel(x, norm_mix_pre, w_in, conv_dw_w, conv_dw_b, conv_ln_g, conv_ln_b, w_conv_branch, b_conv_branch, w_att_branch, w_out, norm_mix_post, norm_ffn_pre, w_ffn_up, w_ffn_down, norm_ffn_post, loss_target, m_norm_mix_pre, m_w_in, m_conv_dw_w, m_conv_dw_b, m_conv_ln_g, m_conv_ln_b, m_w_conv_branch, m_b_conv_branch, m_w_att_branch, m_w_out, m_norm_mix_post, m_norm_ffn_pre, m_w_ffn_up, m_w_ffn_down, m_norm_ffn_post, v_norm_mix_pre, v_w_in, v_conv_dw_w, v_conv_dw_b, v_conv_ln_g, v_conv_ln_b, v_w_conv_branch, v_b_conv_branch, v_w_att_branch, v_w_out, v_norm_mix_post, v_norm_ffn_pre, v_w_ffn_up, v_w_ffn_down, v_norm_ffn_post):
    weights = dict(norm_mix_pre=norm_mix_pre, w_in=w_in, conv_dw_w=conv_dw_w, conv_dw_b=conv_dw_b, conv_ln_g=conv_ln_g, conv_ln_b=conv_ln_b, w_conv_branch=w_conv_branch, b_conv_branch=b_conv_branch, w_att_branch=w_att_branch, w_out=w_out, norm_mix_post=norm_mix_post, norm_ffn_pre=norm_ffn_pre, w_ffn_up=w_ffn_up, w_ffn_down=w_ffn_down, norm_ffn_post=norm_ffn_post)
    mom = dict(norm_mix_pre=m_norm_mix_pre, w_in=m_w_in, conv_dw_w=m_conv_dw_w, conv_dw_b=m_conv_dw_b, conv_ln_g=m_conv_ln_g, conv_ln_b=m_conv_ln_b, w_conv_branch=m_w_conv_branch, b_conv_branch=m_b_conv_branch, w_att_branch=m_w_att_branch, w_out=m_w_out, norm_mix_post=m_norm_mix_post, norm_ffn_pre=m_norm_ffn_pre, w_ffn_up=m_w_ffn_up, w_ffn_down=m_w_ffn_down, norm_ffn_post=m_norm_ffn_post)
    var = dict(norm_mix_pre=v_norm_mix_pre, w_in=v_w_in, conv_dw_w=v_conv_dw_w, conv_dw_b=v_conv_dw_b, conv_ln_g=v_conv_ln_g, conv_ln_b=v_conv_ln_b, w_conv_branch=v_w_conv_branch, b_conv_branch=v_b_conv_branch, w_att_branch=v_w_att_branch, w_out=v_w_out, norm_mix_post=v_norm_mix_post, norm_ffn_pre=v_norm_ffn_pre, w_ffn_up=v_w_ffn_up, w_ffn_down=v_w_ffn_down, norm_ffn_post=v_norm_ffn_post)
    order = list(weights)
    grads, deltas, new_m, new_v = {}, {}, {}, {}
    xs = x.reshape(SEQ, D_MODEL)
    tgt = loss_target.reshape(SEQ, D_MODEL)
    row = lambda a: a.reshape(1, -1)
    g1, g2, g3, g4 = (row(weights[n]) for n in ("norm_mix_pre", "norm_mix_post", "norm_ffn_pre", "norm_ffn_post"))
    ln_g, ln_b = row(conv_ln_g), row(conv_ln_b)

    summed, from_chips = {}, {}

    def core_sums(names, state, after, tag):
        own, from_sibling = sibling_wait(state, after, True, tag)
        for n, g, r in zip(names, own, from_sibling):
            summed[n] = add_halves(g, r, "add_" + n)

    def chip_sums(names, after):
        return [sum_parts(summed[n][1], from_chips[n], after, "sum_" + n) for n in names]

    def optimize(names, state, after, tag):
        mine, other = sibling_wait(state, after, False, tag)
        for n, a, b in zip(names, mine, other):
            grads[n], deltas[n], new_m[n], new_v[n] = adamw(weights[n], a, b, mom[n], var[n], "adamw_" + n)

    w_in_g, dw_g, *rest = all_gather_weights([w_in], conv_dw_w, [weights[n] for n in REST])
    w_dw_full = jnp.concatenate([dw_g[k] for k in range(N_CHIPS)], axis=1)
    rest_shapes = [weights[n].shape for n in REST]
    over_ici = _gather_copies(rest_shapes, 1, to_both_cores=(2, 3, 4))
    state, token = split_start("gather_start", rest, 3 * (len(REST) + 3), over_ici)
    h1, ci, q, k, v, gc, ga = in_proj_fwd(xs, g1, w_in_g, token)
    u1, u3 = conv_fwd(ci, w_dw_full, row(conv_dw_b), ln_g, ln_b)
    att, rc = attn_fwd(q, k, v)
    rest = split_wait("gather_wait", state, [att], over_ici)
    w_out_g, w_cb_g, w_ab_g = rest[2:]
    w_out_g = w_out_g.reshape(D_MODEL, D_MODEL)
    to_sibling = _gather_copies(rest_shapes[:2], 2)
    state, token = split_start("pass_start", rest[:2], 3 * 2, to_sibling)
    merged, mix, x2, h2 = mix_fwd(u3, att, gc, ga, xs, w_cb_g, row(b_conv_branch), w_ab_g, w_out_g, g2, g3, token)
    w_up_g, w_down_g = split_wait("pass_wait", state, [h2], to_sibling)
    w_down_g = w_down_g.reshape(D_FF, D_MODEL)
    gate, up, act = ffn_up_fwd(h2, w_up_g)
    dff, dy, loss_parts, dg4 = ffn_down_loss(act, w_down_g, x2, tgt, g4)

    dgu = ffn_act_bwd(dff, w_down_g, gate, up)
    dx2, dmix, dg3, dg2 = ffn_in_bwd(dgu, w_up_g, x2, mix, dy, g3, g2)
    ffn_grads = [weight_grad(h2, dgu, "dw_ffn_up", True), weight_grad(act, dff, "dw_ffn_down", False, tk=UP_SHARD)]
    to_ffn, token = sibling_start(ffn_grads, True, "dw_ffn")
    dco, dao, dg, du3, datt, dbcb = merge_bwd(dmix, w_out_g, gc, ga, u3, att, w_cb_g, row(b_conv_branch), w_ab_g,
                                              token)
    mix_grads = weight_grad_mix(merged, dmix, u3, dco, att, dao)
    core_sums(REST[:2], to_ffn, [mix_grads[0]], "dw_ffn")
    state, token = scatter_start([summed[n][0] for n in REST[:2]], mix_grads)
    dci, ddw, dbdw, dlng, dlnb = conv_bwd(du3, u1, ci, w_dw_full, ln_g, ln_b, token)
    dqkv = attn_bwd(q, k, v, datt, rc, token)
    ffn_from_chips, mix_grads, mix_from_all = scatter_wait(state, [dci, dqkv], 2, len(mix_grads))
    from_chips.update(zip(REST[:2], ffn_from_chips))
    dproj = (dci, dqkv, dg)
    to_in, token = sibling_start([weight_grad_in(h1, dproj)], True, "dw_in")
    grad_x, dg1 = in_proj_bwd(dproj, w_in_g, xs, dx2, g1, token)
    v512 = dict(conv_dw_b=dbdw, conv_ln_g=dlng, conv_ln_b=dlnb)
    v1024 = dict(norm_mix_pre=dg1, b_conv_branch=dbcb, norm_mix_post=dg2, norm_ffn_pre=dg3, norm_ffn_post=dg4)
    packs = small_pack(ddw, v512, v1024, loss_parts)
    core_sums(("w_in",), to_in, [packs], "dw_in")
    to_chips = summed["w_in"][0]
    landing = lax.empty((3,) + to_chips.shape[1:], to_chips.dtype)

    def scatter_and_packs(refs, send, recv):
        return (_scatter_copies(1)(refs[:2], send, recv)
                + _small_copies(refs[2:], _Shifted(send, 3), _Shifted(recv, 3)))

    state, token = split_start("scatter_start_w_in", [to_chips, landing, packs], 3 + 7, scatter_and_packs)
    swap_up, token = sibling_start(chip_sums(REST[:1], token), False, "sum_ffn_up")
    rest_sums = chip_sums(REST[1:2], token) + [sum_partials(p, r, token, "sum_" + n)
                                               for n, p, r in zip(REST[2:], mix_grads, mix_from_all)]
    swap_rest, token = sibling_start(rest_sums, False, "sum_rest")
    optimize(REST[:1], swap_up, [token], "sum_ffn_up")
    optimize(REST[1:], swap_rest, [new_v["w_ffn_up"]], "sum_rest")
    _, from_chips["w_in"], packs = split_wait("scatter_wait_w_in", state, [new_v[n] for n in REST], scatter_and_packs)
    swap_in, token = sibling_start(chip_sums(("w_in",), token), False, "sum_w_in")
    as_rows = lambda n, a: a if n == "conv_dw_w" else a.reshape(1, -1)
    small_names = ("conv_dw_w",) + VEC512 + VEC1024
    loss, gsum, small = adamw_small(
        packs, {n: tuple(as_rows(n, d[n]) for d in (weights, mom, var)) for n in small_names}, token)
    optimize(("w_in",), swap_in, [gsum], "sum_w_in")
    for n in small_names:
        grads[n], deltas[n], new_m[n], new_v[n] = (a.reshape(weights[n].shape) for a in small[n])

    return (loss.reshape(()), grad_x.reshape(1, SEQ, D_MODEL),*[grads[n] for n in order], *[deltas[n] for n in order],
            *[new_m[n] for n in order], *[new_v[n] for n in order])
```

```python
import jax
import jax.numpy as jnp
from jax import lax
from jax.experimental import pallas as pl
from jax.experimental.pallas import tpu as pltpu

F32 = jnp.float32
MM = jnp.bfloat16

SEQ = 2048
D_MODEL = 1024
CONV_DIM = 512
ATT_DIM = 512
CONV_WIDTH = 31
D_FF = 2816
IN_COLS = 2 * CONV_DIM + 3 * ATT_DIM + 2 * D_MODEL
N_CHIPS = 4
IN_SHARD = IN_COLS // N_CHIPS
UP_SHARD = 2 * D_FF // N_CHIPS
BR_SHARD = D_MODEL // N_CHIPS
EPS = 1e-6
ATT_SCALE = 0.125

TM = 256
GLU_ROWS = 256
DW_CHUNK = 256
TQ = 128
CONV_TILE = 64
CONV_WIN = CONV_TILE + 32
VMEM_LIMIT = 56 * 1024 * 1024

ADAM_LR = 0.001
ADAM_B1 = 0.9
ADAM_B2 = 0.999
ADAM_EPS = 1e-08
ADAM_WD = 0.01
ADAM_STEP = 10

MESH = pl.DeviceIdType.MESH
ANY = pl.BlockSpec(memory_space=pl.ANY)
VMEM_SPEC = pl.BlockSpec(memory_space=pltpu.VMEM)

NT_DIMS = (((1,), (1,)), ((), ()))
TN_DIMS = (((0,), (0,)), ((), ()))

IN_PIECES = (("ci", 0, 1024), ("q", 1024, 1536), ("k", 1536, 2048), ("v", 2048, 2560),
             ("gc", 2560, 3584), ("ga", 3584, 4608))


def _params(sem=None, vmem=VMEM_LIMIT):
    return pltpu.CompilerParams(dimension_semantics=sem, vmem_limit_bytes=vmem)


def _dot(a, b):
    return jnp.dot(a, b, preferred_element_type=F32)


def _dot_nt(a, b):
    return lax.dot_general(a, b, NT_DIMS, preferred_element_type=F32)


def _dot_tn(a, b):
    return lax.dot_general(a, b, TN_DIMS, preferred_element_type=F32)


def _sigmoid(x):
    return 1.0 / (1.0 + jnp.exp(-x))


def _rms(x):
    r = lax.rsqrt(jnp.mean(x * x, axis=-1, keepdims=True) + EPS)
    return x * r, r


def _rms_bwd(dy_g, n, r):
    return r * (dy_g - n * jnp.mean(dy_g * n, axis=-1, keepdims=True))


def _row_tile_spec(width, tm=TM):
    return pl.BlockSpec((tm, width), lambda i: (i, 0))


def _full_spec(shape):
    nd = len(shape)
    return pl.BlockSpec(shape, lambda *_: (0,) * nd)


def _weight_spec(shape):
    nd = len(shape)
    return pl.BlockSpec(shape, lambda *_: (0,) * nd, pipeline_mode=pl.Buffered(1))


def _acc_rows(ref, val, first):
    @pl.when(first)
    def _():
        ref[...] = val

    @pl.when(jnp.logical_not(first))
    def _():
        ref[...] += val


TOKEN_SPEC = pl.BlockSpec((8, 128), lambda *_: (0, 0))


def in_proj_fwd(x, g1, w_in_g, after):
    def body(x_ref, g_ref, w_ref, after_ref, h_ref, ci_ref, q_ref, k_ref, v_ref, gc_ref, ga_ref):
        n, _ = _rms(x_ref[...])
        h = (n * g_ref[...]).astype(MM)
        h_ref[...] = h
        outs = dict(ci=ci_ref, q=q_ref, k=k_ref, v=v_ref, gc=gc_ref, ga=ga_ref)
        for j in range(N_CHIPS):
            p = _dot(h, w_ref[j])
            g0 = j * IN_SHARD
            for name, s, e in IN_PIECES:
                lo, hi = max(s, g0), min(e, g0 + IN_SHARD)
                if lo < hi:
                    ref = outs[name]
                    part = p[:, lo - g0:hi - g0]
                    if name == "q":
                        part = part * ATT_SCALE
                    ref[:, lo - s:hi - s] = part.astype(ref.dtype)

    out_shape = [
        jax.ShapeDtypeStruct((SEQ, D_MODEL), MM),
        jax.ShapeDtypeStruct((SEQ, 2 * CONV_DIM), F32),
        jax.ShapeDtypeStruct((SEQ, ATT_DIM), MM),
        jax.ShapeDtypeStruct((SEQ, ATT_DIM), MM),
        jax.ShapeDtypeStruct((SEQ, ATT_DIM), MM),
        jax.ShapeDtypeStruct((SEQ, D_MODEL), F32),
        jax.ShapeDtypeStruct((SEQ, D_MODEL), F32),
    ]
    return pl.pallas_call(
        body, name="in_proj_fwd", grid=(SEQ // TM,), out_shape=out_shape,
        in_specs=[_row_tile_spec(D_MODEL), _full_spec((1, D_MODEL)), _weight_spec(w_in_g.shape), TOKEN_SPEC],
        out_specs=[_row_tile_spec(s.shape[1]) for s in out_shape],
        compiler_params=_params(("arbitrary",)),
    )(x, g1, w_in_g, after)


LANE_GROUPS = [slice(g, g + 128) for g in range(0, CONV_DIM, 128)]
NORM_ROWS = 16


def _shifted_windows(src_ref, t0, cols, offsets):
    win = src_ref[pl.ds(t0, CONV_WIN), cols]
    for rot in range(8):
        ms = [m for m in offsets if m % 8 == rot]
        if ms:
            shifted = win if rot == 0 else pltpu.roll(win, CONV_WIN - rot, 0)
            for m in ms:
                yield m, shifted[m - rot:m - rot + CONV_TILE, :]


def _shifted_sum(src_ref, t0, cols, w_ref, offset_of_tap):
    tap_at = {offset_of_tap(j): j for j in range(CONV_WIDTH)}
    acc = None
    for m, rows in _shifted_windows(src_ref, t0, cols, sorted(tap_at)):
        t = w_ref[tap_at[m]:tap_at[m] + 1, cols] * rows
        acc = t if acc is None else acc + t
    return acc


def _fetch(srcs, dsts, sems):
    copies = [pltpu.make_async_copy(s, d, sems.at[i]) for i, (s, d) in enumerate(zip(srcs, dsts))]
    for cp in copies:
        cp.start()
    return copies


def _row_chunks(src, dst, sems):
    def chunk(i):
        t0 = i * GLU_ROWS
        rows = pl.ds(t0 if isinstance(i, int) else pl.multiple_of(t0, GLU_ROWS), GLU_ROWS)
        return pltpu.make_async_copy(src.at[rows, :], dst.at[rows, :], sems.at[i])

    for i in range(SEQ // GLU_ROWS):
        chunk(i).start()
    return chunk


def _glu_into(ci_chunk, ci_ref, upad_ref):
    upad_ref[0:32, :] = jnp.zeros((32, CONV_DIM), F32)

    def step(i, c):
        ci_chunk(i).wait()
        t0 = pl.multiple_of(i * GLU_ROWS, GLU_ROWS)
        a = ci_ref[pl.ds(t0, GLU_ROWS), 0:CONV_DIM]
        b = ci_ref[pl.ds(t0, GLU_ROWS), CONV_DIM:2 * CONV_DIM]
        upad_ref[pl.ds(t0 + 32, GLU_ROWS), :] = a * _sigmoid(b)
        return c

    lax.fori_loop(0, SEQ // GLU_ROWS, step, 0)


def _layernorm_parts(u1):
    mu = jnp.mean(u1, axis=-1, keepdims=True)
    xc = u1 - mu
    rstd = lax.rsqrt(jnp.mean(xc * xc, axis=-1, keepdims=True) + EPS)
    return xc * rstd, rstd


def conv_fwd(ci, w_dw, b_dw, ln_g, ln_b):
    def body(ci_hbm, w_ref, b_ref, g_ref, bb_ref, u1_ref, u3_ref, upad_ref, ci_ref, sems):
        _glu_into(_row_chunks(ci_hbm, ci_ref, sems), ci_ref, upad_ref)

        def step(i, c):
            t0 = pl.multiple_of(i * CONV_TILE, CONV_TILE)
            for cols in LANE_GROUPS:
                u1_ref[pl.ds(t0, CONV_TILE), cols] = (_shifted_sum(upad_ref, t0, cols, w_ref, lambda j: j + 2)
                                                      + b_ref[:, cols])
            for r in range(0, CONV_TILE, NORM_ROWS):
                rows = pl.ds(t0 + r, NORM_ROWS)
                xh, _ = _layernorm_parts(u1_ref[rows, :])
                u2 = xh * g_ref[...] + bb_ref[...]
                u3_ref[rows, :] = (u2 * _sigmoid(u2)).astype(MM)
            return c

        lax.fori_loop(0, SEQ // CONV_TILE, step, 0)

    return pl.pallas_call(
        body, name="conv_fwd",
        out_shape=[jax.ShapeDtypeStruct((SEQ, CONV_DIM), F32), jax.ShapeDtypeStruct((SEQ, CONV_DIM), MM)],
        in_specs=[ANY] + [VMEM_SPEC] * 4, out_specs=[VMEM_SPEC] * 2,
        scratch_shapes=[pltpu.VMEM((SEQ + 32, CONV_DIM), F32), pltpu.VMEM(ci.shape, ci.dtype),
                        pltpu.SemaphoreType.DMA((SEQ // GLU_ROWS,))],
        compiler_params=_params(),
    )(ci, w_dw, b_dw, ln_g, ln_b)


def _softplus(z):
    return jnp.maximum(z, 0.0) + jnp.log(1.0 + jnp.exp(-jnp.abs(z)))


def _cumsum_weights(suffix, with_total):
    n = 256 if with_total else 128
    r = lax.broadcasted_iota(jnp.int32, (128, n), 0)
    c = lax.broadcasted_iota(jnp.int32, (128, n), 1)
    tri = (r >= c) if suffix else (r <= c)
    return jnp.logical_or(tri, c >= 128).astype(MM)


NO_SCORE = -1e30
N_KB = SEQ // TQ


def _score_bias(lane, row, i, j):
    keep = jnp.logical_and(i >= 0, jnp.logical_or(j < i, lane < row))
    return jnp.where(keep, 0.0, NO_SCORE)


def _block_pipeline(n_stages, descending, step, on_query_block=None):
    n_lag = n_stages - 1
    none = jnp.int32(-1)

    def shift(cur, lag):
        step([cur] + [(lag[2 * s], lag[2 * s + 1]) for s in range(n_lag)])
        return (cur[0], cur[1]) + tuple(lag[:-2])

    def outer(i, lag):
        if on_query_block is not None:
            on_query_block(i)

        def inner(n, lag):
            return shift((i, i - n if descending else n), lag)
        return lax.fori_loop(0, i + 1, inner, lag)

    lag = lax.fori_loop(0, N_KB, outer, (none,) * (2 * n_lag))
    lax.fori_loop(0, n_lag, lambda n, lag: shift((none, none), lag), lag)


def _head_masks():
    lane = lax.broadcasted_iota(jnp.int32, (TQ, 128), 1)
    row = lax.broadcasted_iota(jnp.int32, (TQ, 128), 0)
    return lane, row, lane < 64


def _pick_head(x, head0, h):
    zero = jnp.zeros_like(x)
    return jnp.where(head0, x, zero) if h == 0 else jnp.where(head0, zero, x)


N_PAIRS = ATT_DIM // 128


def _split_heads(src_ref, dst_ref):
    _, _, head0 = _head_masks()

    def block(b, c):
        r0 = pl.multiple_of(b * TQ, TQ)
        d0 = pl.multiple_of(b * 2 * TQ, 2 * TQ)
        for p in range(N_PAIRS):
            x = src_ref[pl.ds(r0, TQ), 128 * p:128 * (p + 1)]
            for h in range(2):
                dst_ref[p, pl.ds(d0 + TQ * h, TQ), :] = _pick_head(x, head0, h)
        return c

    lax.fori_loop(0, N_KB, block, 0)


def attn_fwd(q, k, v):
    def body(q_hbm, k_hbm, v_hbm, o_ref, rc_ref, acc_ref, r_ref, z_ref, spb_ref, ab_ref, qm_ref, vm_ref,
             q_ref, k_ref, v_ref, sems):
        arrive = _fetch((q_hbm, v_hbm, k_hbm), (q_ref, v_ref, k_ref), sems)
        lane, row, _ = _head_masks()
        w = _cumsum_weights(suffix=True, with_total=True)
        acc_ref[...] = jnp.zeros_like(acc_ref)
        r_ref[...] = jnp.zeros_like(r_ref)
        rc_ref[...] = jnp.zeros_like(rc_ref)
        z_ref[...] = jnp.full(z_ref.shape, NO_SCORE, F32)
        spb_ref[...] = jnp.zeros_like(spb_ref)
        ab_ref[...] = jnp.zeros_like(ab_ref)
        arrive[0].wait()
        _split_heads(q_ref, qm_ref)
        arrive[1].wait()
        _split_heads(v_ref, vm_ref)
        arrive[2].wait()

        def step(pairs):
            (i1, j1), (i2, j2), (i3, j3) = pairs
            k1, q2, q3 = (pl.multiple_of(jnp.maximum(b, 0) * TQ, TQ) for b in (j1, i2, i3))
            q1, k3 = (pl.multiple_of(jnp.maximum(b, 0) * 2 * TQ, 2 * TQ) for b in (i1, j3))
            bias1 = _score_bias(lane, row, i1, j1)
            first2 = j2 == i2
            rc_rows = rc_ref[pl.ds(q2, TQ), :]
            for p in range(N_PAIRS):
                cols = slice(128 * p, 128 * (p + 1))
                kb = k_ref[pl.ds(k1, TQ), cols]
                acc_ref[pl.ds(q3, TQ), cols] += _dot(ab_ref[p], vm_ref[p, pl.ds(k3, 2 * TQ), :])
                for h in range(2):
                    hh = 2 * p + h
                    r = _dot(spb_ref[hh], w)
                    r_in = jnp.where(first2, 0.0, r_ref[hh])
                    ab_ref[p, :, 128 * h:128 * (h + 1)] = jnp.exp(z_ref[hh] - (r[:, :128] + r_in)).astype(MM)
                    rc_rows = jnp.where(jnp.logical_and(lane == 16 * hh + j2, i2 >= 0), r_in, rc_rows)
                    r_ref[hh] = r_in + r[:, 128:]
                    z = _dot_nt(qm_ref[p, pl.ds(q1 + TQ * h, TQ), :], kb) + bias1
                    z_ref[hh] = z
                    spb_ref[hh] = _softplus(z).astype(MM)
            rc_ref[pl.ds(q2, TQ), :] = rc_rows

        _block_pipeline(3, True, step)
        o_ref[...] = acc_ref[...].astype(MM)

    return pl.pallas_call(
        body, name="attn_fwd",
        out_shape=[jax.ShapeDtypeStruct((SEQ, ATT_DIM), MM), jax.ShapeDtypeStruct((SEQ, 128), F32)],
        in_specs=[ANY] * 3, out_specs=[VMEM_SPEC] * 2,
        scratch_shapes=[pltpu.VMEM((SEQ, ATT_DIM), F32), pltpu.VMEM((8, TQ, 128), F32),
                        pltpu.VMEM((8, TQ, 128), F32), pltpu.VMEM((8, TQ, 128), MM),
                        pltpu.VMEM((N_PAIRS, TQ, 256), MM), pltpu.VMEM((N_PAIRS, 2 * SEQ, 128), MM),
                        pltpu.VMEM((N_PAIRS, 2 * SEQ, 128), MM)]
                       + [pltpu.VMEM(a.shape, a.dtype) for a in (q, k, v)] + [pltpu.SemaphoreType.DMA((3,))],
        compiler_params=_params(),
    )(q, k, v)


def _branch_outputs(u_ref, a_ref, wcb_ref, bcb_ref, wab_ref):
    u = u_ref[...]
    a = a_ref[...]
    co = jnp.concatenate([_dot(u, wcb_ref[j]) for j in range(N_CHIPS)], axis=1) + bcb_ref[...]
    ao = jnp.concatenate([_dot(a, wab_ref[j]) for j in range(N_CHIPS)], axis=1)
    return co, ao


def mix_fwd(u3, att, gc, ga, x, w_cb_g, b_cb, w_ab_g, w_out_g, g2, g3, after):
    def body(u_ref, a_ref, gc_ref, ga_ref, x_ref, wcb_ref, bcb_ref, wab_ref, wout_ref, g2_ref, g3_ref, after_ref,
             mg_ref, mix_ref, x2_ref, h2_ref):
        co, ao = _branch_outputs(u_ref, a_ref, wcb_ref, bcb_ref, wab_ref)
        merged = (_sigmoid(gc_ref[...]) * co + _sigmoid(ga_ref[...]) * ao).astype(MM)
        mg_ref[...] = merged
        mix = _dot(merged, wout_ref[...])
        mix_ref[...] = mix
        n2, _ = _rms(mix)
        x2 = x_ref[...] + n2 * g2_ref[...]
        x2_ref[...] = x2
        n3, _ = _rms(x2)
        h2_ref[...] = (n3 * g3_ref[...]).astype(MM)

    out_shape = [
        jax.ShapeDtypeStruct((SEQ, D_MODEL), MM), jax.ShapeDtypeStruct((SEQ, D_MODEL), F32),
        jax.ShapeDtypeStruct((SEQ, D_MODEL), F32), jax.ShapeDtypeStruct((SEQ, D_MODEL), MM),
    ]
    vec = _full_spec((1, D_MODEL))
    return pl.pallas_call(
        body, name="mix_fwd", grid=(SEQ // TM,), out_shape=out_shape,
        in_specs=[_row_tile_spec(CONV_DIM), _row_tile_spec(ATT_DIM), _row_tile_spec(D_MODEL),
                  _row_tile_spec(D_MODEL), _row_tile_spec(D_MODEL), _weight_spec(w_cb_g.shape), vec,
                  _weight_spec(w_ab_g.shape), _weight_spec(w_out_g.shape), vec, vec, TOKEN_SPEC],
        out_specs=[_row_tile_spec(D_MODEL)] * 4,
        compiler_params=_params(("arbitrary",)),
    )(u3, att, gc, ga, x, w_cb_g, b_cb, w_ab_g, w_out_g, g2, g3, after)


def ffn_up_fwd(h2, w_up_g):
    def body(h_ref, wg_ref, wu_ref, gate_ref, up_ref, act_ref):
        h = h_ref[...]
        gate = _dot(h, wg_ref[0])
        up = _dot(h, wu_ref[0])
        gate_ref[...] = gate.astype(MM)
        up_ref[...] = up.astype(MM)
        act_ref[...] = (gate * _sigmoid(gate) * up).astype(MM)

    tile = pl.BlockSpec((TM, UP_SHARD), lambda n, i: (i, n))
    act = jax.ShapeDtypeStruct((SEQ, D_FF), MM)
    return pl.pallas_call(
        body, name="ffn_up_fwd", grid=(2, SEQ // TM), out_shape=[act, act, act],
        in_specs=[pl.BlockSpec((TM, D_MODEL), lambda n, i: (i, 0)),
                  pl.BlockSpec((1, D_MODEL, UP_SHARD), lambda n, i: (n, 0, 0)),
                  pl.BlockSpec((1, D_MODEL, UP_SHARD), lambda n, i: (n + 2, 0, 0))],
        out_specs=[tile, tile, tile],
        compiler_params=_params(("arbitrary", "arbitrary")),
    )(h2, w_up_g, w_up_g)


def ffn_down_loss(act, w_down_g, x2, target, g4):
    def body(act_ref, wd_ref, x2_ref, t_ref, g_ref, dff_ref, dy_ref, loss_ref, dg_ref):
        ff = _dot(act_ref[...], wd_ref[...])
        n4, r4 = _rms(ff)
        g4v = g_ref[...]
        err = x2_ref[...] + n4 * g4v - t_ref[...]
        row_loss = jnp.mean(err * err, axis=-1, keepdims=True)
        loss_ref[...] = jnp.zeros((8, 128), F32) + 0.5 * jnp.sum(row_loss, axis=0, keepdims=True)
        dy = err * (1.0 / D_MODEL)
        dy_ref[...] = dy
        dff_ref[...] = _rms_bwd(dy * g4v, n4, r4).astype(MM)
        _acc_rows(dg_ref, jnp.sum(dy * n4, axis=0, keepdims=True), pl.program_id(0) == 0)

    nt = SEQ // TM
    vec = _full_spec((1, D_MODEL))
    return pl.pallas_call(
        body, name="ffn_down_loss", grid=(nt,),
        out_shape=(jax.ShapeDtypeStruct((SEQ, D_MODEL), MM), jax.ShapeDtypeStruct((SEQ, D_MODEL), F32),
                   jax.ShapeDtypeStruct((nt * 8, 128), F32), jax.ShapeDtypeStruct((1, D_MODEL), F32)),
        in_specs=[_row_tile_spec(D_FF), _weight_spec(w_down_g.shape), _row_tile_spec(D_MODEL),
                  _row_tile_spec(D_MODEL), vec],
        out_specs=[_row_tile_spec(D_MODEL), _row_tile_spec(D_MODEL),
                   pl.BlockSpec((8, 128), lambda i: (i, 0)), vec],
        compiler_params=_params(("arbitrary",)),
    )(act, w_down_g, x2, target, g4)


def ffn_act_bwd(dff, w_down_g, gate, up):
    def body(dff_ref, wd_ref, gate_ref, up_ref, dgu_ref):
        dact = _dot_nt(dff_ref[...], wd_ref[...])
        gate = gate_ref[...].astype(F32)
        sg = _sigmoid(gate)
        dgu_ref[:, 0:D_FF] = (dact * up_ref[...].astype(F32) * (sg * (1.0 + gate * (1.0 - sg)))).astype(MM)
        dgu_ref[:, D_FF:2 * D_FF] = (dact * (gate * sg)).astype(MM)

    return pl.pallas_call(
        body, name="ffn_act_bwd", grid=(SEQ // TM,),
        out_shape=jax.ShapeDtypeStruct((SEQ, 2 * D_FF), MM),
        in_specs=[_row_tile_spec(D_MODEL), _weight_spec(w_down_g.shape), _row_tile_spec(D_FF), _row_tile_spec(D_FF)],
        out_specs=_row_tile_spec(2 * D_FF),
        compiler_params=_params(("arbitrary",)),
    )(dff, w_down_g, gate, up)


def ffn_in_bwd(dgu, w_up_g, x2, mix, dy, g3, g2):
    def body(dgu_ref, w_ref, x2_ref, mix_ref, dy_ref, g3_ref, g2_ref, dx2_ref, dmix_ref, dg3_ref, dg2_ref):
        dh2 = None
        for j in range(N_CHIPS):
            t = _dot_nt(dgu_ref[:, j * UP_SHARD:(j + 1) * UP_SHARD], w_ref[j])
            dh2 = t if dh2 is None else dh2 + t
        first = pl.program_id(0) == 0
        n3, r3 = _rms(x2_ref[...])
        dx2 = dy_ref[...] + _rms_bwd(dh2 * g3_ref[...], n3, r3)
        dx2_ref[...] = dx2
        _acc_rows(dg3_ref, jnp.sum(dh2 * n3, axis=0, keepdims=True), first)
        n2, r2 = _rms(mix_ref[...])
        dmix_ref[...] = _rms_bwd(dx2 * g2_ref[...], n2, r2).astype(MM)
        _acc_rows(dg2_ref, jnp.sum(dx2 * n2, axis=0, keepdims=True), first)

    vec = _full_spec((1, D_MODEL))
    return pl.pallas_call(
        body, name="ffn_in_bwd", grid=(SEQ // TM,),
        out_shape=(jax.ShapeDtypeStruct((SEQ, D_MODEL), F32), jax.ShapeDtypeStruct((SEQ, D_MODEL), MM),
                   jax.ShapeDtypeStruct((1, D_MODEL), F32), jax.ShapeDtypeStruct((1, D_MODEL), F32)),
        in_specs=[_row_tile_spec(2 * D_FF), _weight_spec(w_up_g.shape), _row_tile_spec(D_MODEL),
                  _row_tile_spec(D_MODEL), _row_tile_spec(D_MODEL), vec, vec],
        out_specs=[_row_tile_spec(D_MODEL), _row_tile_spec(D_MODEL), vec, vec],
        compiler_params=_params(("arbitrary",)),
    )(dgu, w_up_g, x2, mix, dy, g3, g2)


def merge_bwd(dmix, w_out_g, gc, ga, u3, att, w_cb_g, b_cb, w_ab_g, after):
    def body(dmix_ref, wout_ref, gc_ref, ga_ref, u_ref, a_ref, wcb_ref, bcb_ref, wab_ref, after_ref,
             dco_ref, dao_ref, dg_ref, du3_ref, datt_ref, dbcb_ref):
        dm = _dot_nt(dmix_ref[...], wout_ref[...])
        co, ao = _branch_outputs(u_ref, a_ref, wcb_ref, bcb_ref, wab_ref)
        sgc = _sigmoid(gc_ref[...])
        sga = _sigmoid(ga_ref[...])
        dco = dm * sgc
        dao = dm * sga
        dg_ref[:, 0:D_MODEL] = (dm * co * (sgc * (1.0 - sgc))).astype(MM)
        dg_ref[:, D_MODEL:2 * D_MODEL] = (dm * ao * (sga * (1.0 - sga))).astype(MM)
        _acc_rows(dbcb_ref, jnp.sum(dco, axis=0, keepdims=True), pl.program_id(0) == 0)
        dco_ref[...] = dco.astype(MM)
        dao_ref[...] = dao.astype(MM)
        du3 = None
        datt = None
        for j in range(N_CHIPS):
            cols = slice(j * BR_SHARD, (j + 1) * BR_SHARD)
            t = _dot_nt(dco_ref[:, cols], wcb_ref[j])
            s = _dot_nt(dao_ref[:, cols], wab_ref[j])
            du3 = t if du3 is None else du3 + t
            datt = s if datt is None else datt + s
        du3_ref[...] = du3
        datt_ref[...] = datt.astype(MM)

    wide = _row_tile_spec(D_MODEL)
    return pl.pallas_call(
        body, name="merge_bwd", grid=(SEQ // TM,),
        out_shape=(jax.ShapeDtypeStruct((SEQ, D_MODEL), MM), jax.ShapeDtypeStruct((SEQ, D_MODEL), MM),
                   jax.ShapeDtypeStruct((SEQ, 2 * D_MODEL), MM),
                   jax.ShapeDtypeStruct((SEQ, CONV_DIM), F32), jax.ShapeDtypeStruct((SEQ, ATT_DIM), MM),
                   jax.ShapeDtypeStruct((1, D_MODEL), F32)),
        in_specs=[wide, _weight_spec(w_out_g.shape), wide, wide, _row_tile_spec(CONV_DIM), _row_tile_spec(ATT_DIM),
                  _weight_spec(w_cb_g.shape), _full_spec((1, D_MODEL)), _weight_spec(w_ab_g.shape), TOKEN_SPEC],
        out_specs=[wide, wide, _row_tile_spec(2 * D_MODEL), _row_tile_spec(CONV_DIM), _row_tile_spec(ATT_DIM),
                   _full_spec((1, D_MODEL))],
        compiler_params=_params(("arbitrary",)),
    )(dmix, w_out_g, gc, ga, u3, att, w_cb_g, b_cb, w_ab_g, after)


def conv_bwd(du3, u1, ci, w_dw, ln_g, ln_b, after):
    def body(du3_hbm, u1_hbm, ci_hbm, w_ref, g_ref, bb_ref, after_ref,
             dci_ref, dw_ref, dbdw_ref, dg_ref, db_ref, upad_ref, dpad_ref, dwacc_ref, vacc_ref,
             du3_ref, u1_ref, ci_ref, ci_sems, u1_sems, du3_sems):
        ci_chunk = _row_chunks(ci_hbm, ci_ref, ci_sems)
        u1_chunk = _row_chunks(u1_hbm, u1_ref, u1_sems)
        du3_chunk = _row_chunks(du3_hbm, du3_ref, du3_sems)
        _glu_into(ci_chunk, ci_ref, upad_ref)
        dpad_ref[SEQ:SEQ + 32, :] = jnp.zeros((32, CONV_DIM), F32)
        dwacc_ref[...] = jnp.zeros_like(dwacc_ref)
        vacc_ref[...] = jnp.zeros_like(vacc_ref)

        def fold8(t):
            s = t[0:8, :]
            for r in range(8, t.shape[0], 8):
                s = s + t[r:r + 8, :]
            return s

        def pass1(i, c):
            t0 = pl.multiple_of(i * CONV_TILE, CONV_TILE)
            gv = g_ref[...]
            for r in range(0, CONV_TILE, NORM_ROWS):
                rows = pl.ds(t0 + r, NORM_ROWS)
                xh, rstd = _layernorm_parts(u1_ref[rows, :])
                u2 = xh * gv + bb_ref[...]
                s2 = _sigmoid(u2)
                du2 = du3_ref[rows, :] * (s2 * (1.0 + u2 * (1.0 - s2)))
                wv = du2 * gv
                du1 = rstd * (wv - jnp.mean(wv, axis=-1, keepdims=True)
                              - xh * jnp.mean(wv * xh, axis=-1, keepdims=True))
                dpad_ref[rows, :] = du1
                vacc_ref[0] += fold8(du2 * xh)
                vacc_ref[1] += fold8(du2)
                vacc_ref[2] += fold8(du1)
            for cols in LANE_GROUPS:
                du1 = dpad_ref[pl.ds(t0, CONV_TILE), cols]
                for m, rows in _shifted_windows(upad_ref, t0, cols, range(2, CONV_WIDTH + 2)):
                    dwacc_ref[m - 2, :, cols] += fold8(du1 * rows)
            return c

        tiles_per_chunk = GLU_ROWS // CONV_TILE

        def pass1_chunk(ch, c):
            u1_chunk(ch).wait()
            du3_chunk(ch).wait()
            return lax.fori_loop(ch * tiles_per_chunk, (ch + 1) * tiles_per_chunk, pass1, c)

        lax.fori_loop(0, SEQ // GLU_ROWS, pass1_chunk, 0)

        def pass2(i, c):
            t0 = pl.multiple_of(i * CONV_TILE, CONV_TILE)
            tile = pl.ds(t0, CONV_TILE)
            for cols in LANE_GROUPS:
                gate_cols = slice(cols.start + CONV_DIM, cols.stop + CONV_DIM)
                du0 = _shifted_sum(dpad_ref, t0, cols, w_ref, lambda j: 30 - j)
                a = ci_ref[tile, cols]
                sb = _sigmoid(ci_ref[tile, gate_cols])
                dci_ref[tile, cols] = (du0 * sb).astype(MM)
                dci_ref[tile, gate_cols] = (du0 * a * (sb * (1.0 - sb))).astype(MM)
            return c

        lax.fori_loop(0, SEQ // CONV_TILE, pass2, 0)

        for j in range(CONV_WIDTH):
            dw_ref[j:j + 1, :] = jnp.sum(dwacc_ref[j], axis=0, keepdims=True)
        dw_ref[CONV_WIDTH:32, :] = jnp.zeros((32 - CONV_WIDTH, CONV_DIM), F32)
        dg_ref[...] = jnp.sum(vacc_ref[0], axis=0, keepdims=True)
        db_ref[...] = jnp.sum(vacc_ref[1], axis=0, keepdims=True)
        dbdw_ref[...] = jnp.sum(vacc_ref[2], axis=0, keepdims=True)

    vec = jax.ShapeDtypeStruct((1, CONV_DIM), F32)
    return pl.pallas_call(
        body, name="conv_bwd",
        out_shape=(jax.ShapeDtypeStruct((SEQ, 2 * CONV_DIM), MM), jax.ShapeDtypeStruct((32, CONV_DIM), F32),
                   vec, vec, vec),
        in_specs=[ANY] * 3 + [VMEM_SPEC] * 4, out_specs=[VMEM_SPEC] * 5,
        scratch_shapes=[pltpu.VMEM((SEQ + 32, CONV_DIM), F32), pltpu.VMEM((SEQ + 32, CONV_DIM), F32),
                        pltpu.VMEM((CONV_WIDTH, 8, CONV_DIM), F32), pltpu.VMEM((3, 8, CONV_DIM), F32)]
                       + [pltpu.VMEM(a.shape, a.dtype) for a in (du3, u1, ci)]
                       + [pltpu.SemaphoreType.DMA((SEQ // GLU_ROWS,))] * 3,
        compiler_params=_params(),
    )(du3, u1, ci, w_dw, ln_g, ln_b, after)


def attn_bwd(q, k, v, datt, rc, after):
    def body(q_hbm, k_hbm, v_hbm, do_hbm, rc_hbm, after_ref, dqkv_ref, dqa_ref, dka_ref, dva_ref, pc_ref, z_ref,
             sig1_ref, sig2_ref, g_ref, spb_ref, gb_ref, ar_ref, dzr_ref, dzc_ref, qm_ref, km_ref, dom_ref,
             q_ref, k_ref, v_ref, do_ref, rc_ref, sems):
        arrive = _fetch((q_hbm, k_hbm, do_hbm, v_hbm, rc_hbm), (q_ref, k_ref, do_ref, v_ref, rc_ref), sems)
        lane, row, _ = _head_masks()
        for ref in (dqa_ref, dka_ref, dva_ref, pc_ref):
            ref[...] = jnp.zeros_like(ref)
        z_ref[...] = jnp.full(z_ref.shape, NO_SCORE, F32)
        for ref in (sig1_ref, sig2_ref, spb_ref, ar_ref, g_ref, gb_ref, dzr_ref, dzc_ref):
            ref[...] = jnp.zeros_like(ref)
        for cp, (src_ref, split_ref) in zip(arrive, ((q_ref, qm_ref), (k_ref, km_ref), (do_ref, dom_ref))):
            cp.wait()
            _split_heads(src_ref, split_ref)
        arrive[3].wait()
        arrive[4].wait()
        w_suffix = _cumsum_weights(suffix=True, with_total=False)
        w_prefix = _cumsum_weights(suffix=False, with_total=True)

        def step(pairs):
            (ia, ja), (ib, jb), (ic, jc), (id_, jd) = pairs
            ka, qb_, kb_, kc, qd, kd = (pl.multiple_of(jnp.maximum(b, 0) * TQ, TQ) for b in (ja, ib, jb, jc, id_, jd))
            qa2, qb2, qc2, qd2, kd2 = (pl.multiple_of(jnp.maximum(b, 0) * 2 * TQ, 2 * TQ)
                                       for b in (ia, ib, ic, id_, jd))
            bias_a = _score_bias(lane, row, ia, ja)
            rc_rows = rc_ref[pl.ds(qb_, TQ), :]
            first_c = jc == 0
            for p in range(N_PAIRS):
                cols = slice(128 * p, 128 * (p + 1))
                k_a = k_ref[pl.ds(ka, TQ), cols]
                v_b = v_ref[pl.ds(kb_, TQ), cols]
                dqa_ref[pl.ds(qd, TQ), cols] += _dot(dzc_ref[p], km_ref[p, pl.ds(kd2, 2 * TQ), :])
                dka_ref[pl.ds(kd, TQ), cols] += _dot_tn(dzr_ref[p], qm_ref[p, pl.ds(qd2, 2 * TQ), :])
                dva_ref[pl.ds(kc, TQ), cols] += _dot_tn(ar_ref[p], dom_ref[p, pl.ds(qc2, 2 * TQ), :])
                for h in range(2):
                    hh = 2 * p + h
                    rows = slice(TQ * h, TQ * (h + 1))
                    r = _dot(gb_ref[hh], w_prefix)
                    p_in = jnp.where(first_c, 0.0, pc_ref[hh])
                    dz = (g_ref[hh] - sig2_ref[hh] * (r[:, :128] + p_in)).astype(MM)
                    dzc_ref[p, :, rows] = dz
                    dzr_ref[p, rows, :] = dz
                    pc_ref[hh] = p_in + r[:, 128:]
                    r_in = jnp.sum(jnp.where(lane == 16 * hh + jb, rc_rows, 0.0), axis=1, keepdims=True)
                    a = jnp.exp(z_ref[hh] - (_dot(spb_ref[hh], w_suffix) + r_in))
                    g = _dot_nt(dom_ref[p, pl.ds(qb2 + TQ * h, TQ), :], v_b) * a
                    ar_ref[p, rows, :] = a.astype(MM)
                    g_ref[hh] = g
                    gb_ref[hh] = g.astype(MM)
                    sig2_ref[hh] = sig1_ref[hh]
                    z = _dot_nt(qm_ref[p, pl.ds(qa2 + TQ * h, TQ), :], k_a) + bias_a
                    sp = _softplus(z)
                    sig1_ref[hh] = jnp.exp(z - sp)
                    z_ref[hh] = z
                    spb_ref[hh] = sp.astype(MM)

        _block_pipeline(4, False, step)
        dqkv_ref[:, 0:ATT_DIM] = (dqa_ref[...] * ATT_SCALE).astype(MM)
        dqkv_ref[:, ATT_DIM:2 * ATT_DIM] = dka_ref[...].astype(MM)
        dqkv_ref[:, 2 * ATT_DIM:3 * ATT_DIM] = dva_ref[...].astype(MM)

    split = pltpu.VMEM((N_PAIRS, 2 * SEQ, 128), MM)
    return pl.pallas_call(
        body, name="attn_bwd", out_shape=jax.ShapeDtypeStruct((SEQ, 3 * ATT_DIM), MM),
        in_specs=[ANY] * 5 + [VMEM_SPEC], out_specs=VMEM_SPEC,
        scratch_shapes=[pltpu.VMEM((SEQ, ATT_DIM), F32)] * 3 + [pltpu.VMEM((8, TQ, 128), F32)] * 5
                       + [pltpu.VMEM((8, TQ, 128), MM)] * 2
                       + [pltpu.VMEM((N_PAIRS, 2 * TQ, 128), MM)] * 2 + [pltpu.VMEM((N_PAIRS, TQ, 256), MM)]
                       + [split] * 3
                       + [pltpu.VMEM(a.shape, a.dtype) for a in (q, k, v, datt, rc)] + [pltpu.SemaphoreType.DMA((5,))],
        compiler_params=_params(),
    )(q, k, v, datt, rc, after)


DPROJ_PIECES = ((0, 1024), (1024, 2560), (2560, 4608))


def _dproj_segments(j):
    g0, g1 = j * IN_SHARD, (j + 1) * IN_SHARD
    segs = []
    for p, (s, e) in enumerate(DPROJ_PIECES):
        lo, hi = max(s, g0), min(e, g1)
        if lo < hi:
            segs.append((p, lo - s, lo - g0, hi - lo))
    return segs


def in_proj_bwd(pieces, w_in_g, x, dx2, g1, after):
    def body(p0_ref, p1_ref, p2_ref, w_ref, x_ref, dx2_ref, g_ref, after_ref, dx_ref, dg_ref):
        p_refs = (p0_ref, p1_ref, p2_ref)
        dh = None
        for j in range(N_CHIPS):
            for p, lo, off, width in _dproj_segments(j):
                t = _dot_nt(p_refs[p][:, lo:lo + width], w_ref[j, :, off:off + width])
                dh = t if dh is None else dh + t
        n1, r1 = _rms(x_ref[...])
        dx_ref[...] = dx2_ref[...] + _rms_bwd(dh * g_ref[...], n1, r1)
        _acc_rows(dg_ref, jnp.sum(dh * n1, axis=0, keepdims=True), pl.program_id(0) == 0)

    vec = _full_spec((1, D_MODEL))
    return pl.pallas_call(
        body, name="in_proj_bwd", grid=(SEQ // TM,),
        out_shape=[jax.ShapeDtypeStruct((SEQ, D_MODEL), F32), jax.ShapeDtypeStruct((1, D_MODEL), F32)],
        in_specs=[_row_tile_spec(p.shape[1]) for p in pieces]
                 + [_weight_spec(w_in_g.shape), _row_tile_spec(D_MODEL), _row_tile_spec(D_MODEL), vec, TOKEN_SPEC],
        out_specs=[_row_tile_spec(D_MODEL), vec],
        compiler_params=_params(("arbitrary",)),
    )(*pieces, w_in_g, x, dx2, g1, after)


def weight_grad_in(h1, pieces):
    kh = D_MODEL // 2
    operands = (h1,) + tuple(pieces)
    out = jax.ShapeDtypeStruct((N_CHIPS, 2, kh, IN_SHARD), MM)

    def body(*refs):
        hbm, o_hbm, bufs, stage_ref, in_sems, out_sems = refs[:4], refs[4], refs[5:9], refs[9], refs[10], refs[11]
        arrive = _fetch(hbm, bufs, in_sems)
        a_ref, p_refs = bufs[0], bufs[1:]
        arrive[0].wait()
        here, leaving = set(), []
        for j in range(N_CHIPS):
            for p in sorted({seg[0] for seg in _dproj_segments(j)} - here):
                arrive[1 + p].wait()
                here.add(p)
            for h in range(2):
                a = a_ref[:, h * kh:(h + 1) * kh]
                for p, lo, off, width in _dproj_segments(j):
                    stage_ref[j, h, :, off:off + width] = _dot_tn(a, p_refs[p][:, lo:lo + width]).astype(MM)
                leaving.append(pltpu.make_async_copy(stage_ref.at[j, h], o_hbm.at[j, h], out_sems.at[2 * j + h]))
                leaving[-1].start()
        for cp in leaving:
            cp.wait()

    return pl.pallas_call(
        body, name="dw_in", out_shape=out, in_specs=[ANY] * 4, out_specs=ANY,
        scratch_shapes=[pltpu.VMEM(a.shape, a.dtype) for a in operands]
                       + [pltpu.VMEM(out.shape, out.dtype), pltpu.SemaphoreType.DMA((4,)),
                          pltpu.SemaphoreType.DMA((2 * N_CHIPS,))],
        compiler_params=_params(),
    )(*operands)


def weight_grad(a, b, name, col_sharded, tk=None):
    kin, n = a.shape[1], b.shape[1]
    kh = kin // 2
    n_chunks = N_CHIPS if col_sharded else kin // tk
    out = jax.ShapeDtypeStruct((N_CHIPS, 2, kh, n // N_CHIPS) if col_sharded else (kin, n), MM)

    def body(a_hbm, b_hbm, o_hbm, a_ref, b_ref, stage_ref, whole_sem, chunk_sems, out_sems):
        (src, dst), whole = ((b_hbm, b_ref), (a_hbm, a_ref)) if col_sharded else ((a_hbm, a_ref), (b_hbm, b_ref))
        width = src.shape[1] // n_chunks
        first = pltpu.make_async_copy(*whole, whole_sem)
        first.start()
        chunks = [pltpu.make_async_copy(src.at[:, c * width:(c + 1) * width], dst.at[:, c * width:(c + 1) * width],
                                        chunk_sems.at[c]) for c in range(n_chunks)]
        for cp in chunks:
            cp.start()
        first.wait()
        leaving = []

        def leave(block, to):
            leaving.append(pltpu.make_async_copy(block, to, out_sems.at[len(leaving)]))
            leaving[-1].start()

        for c in range(n_chunks):
            chunks[c].wait()
            cols = slice(c * width, (c + 1) * width)
            if col_sharded:
                for h in range(2):
                    stage_ref[c, h] = _dot_tn(a_ref[:, h * kh:(h + 1) * kh], b_ref[:, cols]).astype(MM)
                    leave(stage_ref.at[c, h], o_hbm.at[c, h])
            else:
                stage_ref[cols, :] = _dot_tn(a_ref[:, cols], b_ref[...]).astype(MM)
                leave(stage_ref.at[cols, :], o_hbm.at[cols, :])
        for cp in leaving:
            cp.wait()

    sems = pltpu.SemaphoreType.DMA
    res = pl.pallas_call(
        body, name=name, out_shape=out, in_specs=[ANY] * 2, out_specs=ANY,
        scratch_shapes=[pltpu.VMEM(a.shape, a.dtype), pltpu.VMEM(b.shape, b.dtype), pltpu.VMEM(out.shape, out.dtype),
                        sems, sems((n_chunks,)), sems((2 * n_chunks if col_sharded else n_chunks,))],
        compiler_params=_params(),
    )(a, b)
    if not col_sharded:
        res = res.reshape(N_CHIPS, 2, kin // (2 * N_CHIPS), n)
    return res


def weight_grad_mix(merged, dmix, u3, dco, att, dao):
    operands = (merged, dmix, u3, dco, att, dao)
    n_out, n_br = D_MODEL // 2, CONV_DIM // 2

    def body(*refs):
        hbm, (o_out, o_cb, o_ab), bufs, sems = refs[:6], refs[6:9], refs[9:15], refs[15]
        copies = [pltpu.make_async_copy(hbm[i], bufs[i], sems.at[i]) for i in range(6)]
        for cp in copies:
            cp.start()
        m_ref, dm_ref, u_ref, dco_ref, a_ref, dao_ref = bufs
        copies[0].wait()
        copies[1].wait()
        for h in range(2):
            o_out[h * n_out:(h + 1) * n_out, :] = _dot_tn(m_ref[:, h * n_out:(h + 1) * n_out], dm_ref[...]).astype(MM)
        for br, (a, d, o) in enumerate(((u_ref, dco_ref, o_cb), (a_ref, dao_ref, o_ab))):
            copies[2 + 2 * br].wait()
            copies[3 + 2 * br].wait()
            for h in range(2):
                g = _dot_tn(a[:, h * n_br:(h + 1) * n_br], d[...])
                for j in range(N_CHIPS):
                    o[j, h] = g[:, j * BR_SHARD:(j + 1) * BR_SHARD].astype(MM)

    branch = jax.ShapeDtypeStruct((N_CHIPS, 2, n_br, BR_SHARD), MM)
    dw_out, dw_cb, dw_ab = pl.pallas_call(
        body, name="dw_mix", out_shape=[jax.ShapeDtypeStruct((D_MODEL, D_MODEL), MM), branch, branch],
        in_specs=[ANY] * 6, out_specs=[VMEM_SPEC] * 3,
        scratch_shapes=[pltpu.VMEM(a.shape, a.dtype) for a in operands] + [pltpu.SemaphoreType.DMA((6,))],
        compiler_params=_params(),
    )(*operands)
    return dw_out.reshape(N_CHIPS, 2, D_MODEL // (2 * N_CHIPS), D_MODEL), dw_cb, dw_ab


def _place():
    x, y, c = lax.axis_index("x"), lax.axis_index("y"), lax.axis_index("c")
    chips = [(1 - x, y), (x, 1 - y), (1 - x, 1 - y)]
    return x, y, c, chips


def _rcopy(src, dst, send_sem, recv_sem, dev):
    return pltpu.make_async_remote_copy(src_ref=src, dst_ref=dst, send_sem=send_sem, recv_sem=recv_sem,
                                        device_id=dev, device_id_type=MESH)


class _Gather:
    N_MOVES = 6

    def __init__(self, shapes, w, o, scratch):
        self.n, self.shapes, self.w, self.o = len(w), shapes, w, o
        self.send, self.recv, self.psend, self.precv, self.loc_in, self.loc_out = scratch[:6]
        self.raw, self.stage = scratch[6:6 + self.n], scratch[6 + self.n:]
        x, y, c, self.chips = _place()
        self.c = c
        self.me, k_x, k_y, k_far = 2 * x + y, 2 * (1 - x) + y, 2 * x + (1 - y), 2 * (1 - x) + (1 - y)
        to_x, to_y = (1 - x, y, c), (x, 1 - y, c)
        self.sib = (x, y, 1 - c)
        self.sent_as = [(self.me, 0, to_x), (self.me, 1, to_y), (self.me, 1, to_x), (self.me, 0, to_y),
                        (k_x, 0, to_y), (k_y, 1, to_x)]
        self.arrives_as = [(k_x, 0, to_x), (k_y, 1, to_y), (k_x, 1, to_x), (k_y, 0, to_y),
                           (k_far, 0, to_y), (k_far, 1, to_x)]
        self.sent_on_after = {0: 4, 1: 5}

    @staticmethod
    def scratch(shards):
        n = len(shards)
        sems = pltpu.SemaphoreType.DMA
        m = _Gather.N_MOVES * n
        return ([sems((m,)), sems((m,)), sems((m,)), sems((m,)), sems((3 * n,)), sems((n,))]
                + [pltpu.VMEM(s.shape, s.dtype) for s in shards] + [pltpu.VMEM(s.shape, MM) for s in shards])

    @staticmethod
    def out_shapes(shards):
        return [jax.ShapeDtypeStruct((N_CHIPS,) + s.shape, MM) for s in shards]

    def _rows(self, t, quarter, cc):
        rq = self.shapes[t][0] // 4
        return pl.ds(pl.multiple_of((2 * cc + quarter) * rq, rq), rq)

    def _own_rows(self, t, piece):
        if piece < 2:
            return self._rows(t, piece, self.c)
        rh = self.shapes[t][0] // 2
        return pl.ds(pl.multiple_of((1 - self.c) * rh, rh), rh)

    def _chip(self, j):
        cx, cy = self.chips[j]
        return 2 * cx + cy, (cx, cy, self.c)

    def local_in(self, t, piece):
        rows = self._own_rows(t, piece)
        return pltpu.make_async_copy(self.w[t].at[rows, :], self.raw[t].at[rows, :], self.loc_in.at[3 * t + piece])

    def local_out(self, t):
        return pltpu.make_async_copy(self.stage[t], self.o[t].at[self.me], self.loc_out.at[t])

    def sent(self, i, t):
        k, quarter, dev = self.sent_as[i]
        rows = self._rows(t, quarter, self.c)
        there = self.o[t].at[k, rows, :]
        return _rcopy(self.stage[t].at[rows, :] if i < 4 else there, there,
                      self.send.at[i * self.n + t], self.recv.at[i * self.n + t], dev)

    def arrived(self, i, t):
        k, quarter, dev = self.arrives_as[i]
        blk = self.o[t].at[k, self._rows(t, quarter, self.c), :]
        return _rcopy(blk, blk, self.send.at[i * self.n + t], self.recv.at[i * self.n + t], dev)

    def passed(self, i, t, cc):
        k, quarter, _ = self.arrives_as[i]
        blk = self.o[t].at[k, self._rows(t, quarter, cc), :]
        return _rcopy(blk, blk, self.psend.at[i * self.n + t], self.precv.at[i * self.n + t], self.sib)

    def start(self):
        for piece in range(3):
            for t in range(self.n):
                self.local_in(t, piece).start()
        for piece, moves in enumerate(((0, 3), (1, 2), ())):
            for t in range(self.n):
                rows = self._own_rows(t, piece)
                self.local_in(t, piece).wait()
                self.stage[t][rows, :] = self.raw[t][rows, :].astype(MM)
                for i in moves:
                    self.sent(i, t).start()
        for t in range(self.n):
            self.local_out(t).start()

    def forward(self):
        for i in range(self.N_MOVES):
            for t in range(self.n):
                self.arrived(i, t).wait_recv()
                if i in self.sent_on_after:
                    self.sent(self.sent_on_after[i], t).start()
                self.passed(i, t, self.c).start()

    def finish(self):
        for i in range(self.N_MOVES):
            for t in range(self.n):
                self.passed(i, t, 1 - self.c).wait_recv()
        for i in range(self.N_MOVES):
            for t in range(self.n):
                self.sent(i, t).wait_send()
                self.passed(i, t, self.c).wait_send()
        for t in range(self.n):
            self.local_out(t).wait()


def all_gather_weights(shards, small, later):
    n, m = len(shards), len(later)
    shapes = [s.shape for s in shards]

    def body(*refs):
        w = refs[:n]
        sm = refs[n]
        lw = refs[n + 1:n + 1 + m]
        o = refs[n + 1 + m:2 * n + 1 + m]
        osm = refs[2 * n + 1 + m]
        lo = refs[2 * n + 2 + m:2 * n + 2 + 2 * m]
        scratch = refs[2 * n + 2 + 2 * m:]
        ssend, srecv, sloc, lsem_in, lsem_out = scratch[:5]
        lraw, lstage = scratch[5:5 + m], scratch[5 + m:5 + 2 * m]
        g = _Gather(shapes, w, o, scratch[5 + 2 * m:])
        own = pltpu.make_async_copy(sm, osm.at[g.me], sloc)
        own.start()
        g.start()
        loads = [pltpu.make_async_copy(lw[t], lraw[t], lsem_in.at[t]) for t in range(m)]
        for cp in loads:
            cp.start()
        small_cps = [_rcopy(sm, osm.at[g.me], ssend.at[j], srecv.at[j], g._chip(j)[1]) for j in range(3)]
        for cp in small_cps:
            cp.start()
        places = []
        for t in range(m):
            loads[t].wait()
            lstage[t][...] = lraw[t][...].astype(MM)
            places.append(pltpu.make_async_copy(lstage[t], lo[t].at[g.me], lsem_out.at[t]))
            places[t].start()
        g.forward()
        g.finish()
        for j in range(3):
            k, dev = g._chip(j)
            _rcopy(sm, osm.at[k], ssend.at[j], srecv.at[j], dev).wait_recv()
            small_cps[j].wait_send()
        own.wait()
        for cp in places:
            cp.wait()

    out_shape = _Gather.out_shapes(shards)
    out_shape.append(jax.ShapeDtypeStruct((N_CHIPS,) + small.shape, small.dtype))
    out_shape += _Gather.out_shapes(later)
    sems = pltpu.SemaphoreType.DMA
    return pl.pallas_call(
        body, name="all_gather_weights", out_shape=out_shape,
        in_specs=[ANY] * (n + 1 + m), out_specs=[ANY] * (n + 1 + m),
        scratch_shapes=[sems((3,)), sems((3,)), sems, sems((m,)), sems((m,))]
                       + [pltpu.VMEM(s.shape, s.dtype) for s in later] + [pltpu.VMEM(s.shape, MM) for s in later]
                       + _Gather.scratch(shards),
        compiler_params=_params(),
    )(*shards, small, *later)


HBM_SPEC = pl.BlockSpec(memory_space=pltpu.HBM)
SEM_SPEC = pl.BlockSpec(memory_space=pltpu.SEMAPHORE)
DATAFLOW = pltpu.SideEffectType.DATAFLOW_SIDE_EFFECTING


def split_start(name, bufs, n_copies, copies):
    nb = len(bufs)

    def body(*refs):
        for cp in copies(refs[:nb], refs[nb], refs[nb + 1]):
            cp.start()
        token = refs[2 * nb + 2]
        token[...] = jnp.zeros_like(token)

    sems = [pltpu.SemaphoreType.DMA((n_copies,))] * 2
    res = pl.pallas_call(
        body, name=name,
        out_shape=sems + [pltpu.HBM(a.shape, a.dtype) for a in bufs] + [jax.ShapeDtypeStruct((8, 128), F32)],
        in_specs=[HBM_SPEC] * nb, out_specs=[SEM_SPEC] * 2 + [HBM_SPEC] * nb + [VMEM_SPEC],
        input_output_aliases={i: 2 + i for i in range(nb)},
        compiler_params=pltpu.CompilerParams(has_side_effects=DATAFLOW),
    )(*[pltpu.with_memory_space_constraint(a, pltpu.HBM) for a in bufs])
    return res[:-1], res[-1]


def split_wait(name, state, after, copies):
    sems, bufs = state[:2], state[2:]
    nb = len(bufs)

    def body(*refs):
        for cp in copies(refs[:nb], refs[nb], refs[nb + 1]):
            cp.wait_send()
            cp.wait_recv()

    return pl.pallas_call(
        body, name=name, out_shape=[pltpu.HBM(a.shape, a.dtype) for a in bufs],
        in_specs=[HBM_SPEC] * nb + [SEM_SPEC] * 2 + [ANY] * len(after), out_specs=[HBM_SPEC] * nb,
        input_output_aliases={i: i for i in range(nb)},
        compiler_params=pltpu.CompilerParams(has_side_effects=DATAFLOW),
    )(*bufs, *sems, *after)


class _Shifted:
    def __init__(self, sems, first):
        self.sems, self.first = sems, first

    @property
    def at(self):
        return self

    def __getitem__(self, i):
        return self.sems.at[self.first + i]


def _scatter_copies(n):
    def copies(refs, send, recv):
        _, _, c, chips = _place()
        return [_rcopy(refs[t].at[2 * cx + cy], refs[n + t].at[j], send.at[3 * t + j], recv.at[3 * t + j], (cx, cy, c))
                for t in range(n) for j, (cx, cy) in enumerate(chips)]
    return copies


def _direct_copies(n):
    def copies(refs, send, recv):
        x, y, c, chips = _place()
        out = []
        for t in range(n):
            src, land = refs[t], refs[n + t]
            for to_core, first in ((c, 0), (1 - c, 3)):
                for j, (cx, cy) in enumerate(chips):
                    i = 7 * t + first + j
                    out.append(_rcopy(src.at[2 * cx + cy, to_core], land.at[first + j], send.at[i], recv.at[i],
                                      (cx, cy, to_core)))
            i = 7 * t + 6
            out.append(_rcopy(src.at[2 * x + y, 1 - c], land.at[6], send.at[i], recv.at[i], (x, y, 1 - c)))
        return out
    return copies


def _scatter_and_direct(n, m):
    def copies(refs, send, recv):
        return (_scatter_copies(n)(refs[:2 * n], send, recv)
                + _direct_copies(m)(refs[2 * n:], _Shifted(send, 3 * n), _Shifted(recv, 3 * n)))
    return copies


def scatter_start(parts, partials):
    n, m = len(parts), len(partials)
    lands = [lax.empty((3,) + p.shape[1:], p.dtype) for p in parts]
    direct_lands = [lax.empty((7,) + p.shape[2:], p.dtype) for p in partials]
    return split_start("scatter_start_rest", list(parts) + lands + list(partials) + direct_lands, 3 * n + 7 * m,
                       _scatter_and_direct(n, m))


def scatter_wait(state, after, n, m):
    res = split_wait("scatter_wait_rest", state, after, _scatter_and_direct(n, m))
    return res[n:2 * n], res[2 * n:2 * n + m], res[2 * n + m:]


def _gather_copies(shapes, level, to_both_cores=()):
    n = len(shapes)

    def copies(refs, send, recv):
        x, y, c, chips = _place()
        out = []
        for t in list(range(n)) + list(to_both_cores):
            rh = shapes[t][0] // 2
            to_core = c if len(out) < 3 * n else 1 - c
            for cx, cy in chips:
                k, dev = (2 * x + y, (cx, cy, to_core)) if level == 1 else (2 * cx + cy, (x, y, 1 - c))
                blk = refs[t].at[k, pl.ds(c * rh, rh), :]
                out.append(_rcopy(blk, blk, send.at[len(out)], recv.at[len(out)], dev))
        return out
    return copies


def _sibling_copies(n, other_half):
    def copies(refs, send, recv):
        x, y, c, _ = _place()
        return [_rcopy(refs[t].at[:, 1 - c] if other_half else refs[t], refs[n + t], send.at[t], recv.at[t],
                       (x, y, 1 - c)) for t in range(n)]
    return copies


def sibling_start(srcs, other_half, tag):
    lands = [lax.empty((a.shape[0],) + a.shape[2:] if other_half else a.shape, a.dtype) for a in srcs]
    return split_start("sibling_start_" + tag, list(srcs) + lands, len(srcs),
                       _sibling_copies(len(srcs), other_half))


def sibling_wait(state, after, other_half, tag):
    n = (len(state) - 2) // 2
    res = split_wait("sibling_wait_" + tag, state, after, _sibling_copies(n, other_half))
    return res[:n], res[n:]


def small_pack(ddw, v512, v1024, loss_parts):
    rows, width = PACK_ROWS, 512
    n512, n1024 = len(VEC512), len(VEC1024)

    def body(*refs):
        ddw_ref = refs[0]
        a_refs = refs[1:1 + n512]
        b_refs = refs[1 + n512:1 + n512 + n1024]
        lp_ref, o_ref, p_ref = refs[1 + n512 + n1024:]
        p_ref[...] = jnp.zeros_like(p_ref)
        p_ref[0:32, :] = ddw_ref[...]
        p_ref[LOSS_ROW:LOSS_ROW + 1, 0:128] = jnp.sum(lp_ref[...], axis=0, keepdims=True) * 0.125
        for i, r in enumerate(a_refs):
            p_ref[32 + i:33 + i, :] = r[...]
        for i, r in enumerate(b_refs):
            base = 32 + n512 + 2 * i
            p_ref[base:base + 1, :] = r[:, 0:512]
            p_ref[base + 1:base + 2, :] = r[:, 512:1024]
        x, y, c, _ = _place()
        o_ref[4 * x + 2 * y + c] = p_ref[...]

    n_in = 2 + n512 + n1024
    return pl.pallas_call(
        body, name="small_pack", out_shape=jax.ShapeDtypeStruct((8, rows, width), F32),
        in_specs=[VMEM_SPEC] * n_in, out_specs=VMEM_SPEC,
        scratch_shapes=[pltpu.VMEM((rows, width), F32)],
    )(ddw, *[v512[n] for n in VEC512], *[v1024[n] for n in VEC1024], loss_parts)


def _small_copies(refs, send, recv):
    x, y, c, _ = _place()
    mine = refs[0].at[4 * x + 2 * y + c]
    peers = [(1 - x if k & 4 else x, 1 - y if k & 2 else y, 1 - c if k & 1 else c) for k in range(1, 8)]
    return [_rcopy(mine, mine, send.at[i], recv.at[i], dev) for i, dev in enumerate(peers)]


def _row_block(r):
    for tr in (512, 352, 256, 128):
        if r % tr == 0:
            return tr
    return r


def add_halves(g, recv, name):
    _, _, r, w = g.shape
    tr = _row_block(r)

    def body(g_ref, r_ref, ob_ref, own_ref):
        k = pl.program_id(1)
        me = 2 * lax.axis_index("x") + lax.axis_index("y")
        t = g_ref[0, 0].astype(F32) + r_ref[0].astype(F32)
        ob_ref[0] = t.astype(MM)
        mine = jnp.where(k == me, t, 0.0)

        @pl.when(k == 0)
        def _():
            own_ref[...] = mine

        @pl.when(k != 0)
        def _():
            own_ref[...] += mine

    return pl.pallas_call(
        body, name=name, grid=(r // tr, N_CHIPS),
        in_specs=[pl.BlockSpec((1, 1, tr, w), lambda i, k: (k, lax.axis_index("c"), i, 0)),
                  pl.BlockSpec((1, tr, w), lambda i, k: (k, i, 0))],
        out_specs=[pl.BlockSpec((1, tr, w), lambda i, k: (k, i, 0)),
                   pl.BlockSpec((tr, w), lambda i, k: (i, 0))],
        out_shape=(jax.ShapeDtypeStruct((N_CHIPS, r, w), MM), jax.ShapeDtypeStruct((r, w), F32)),
        compiler_params=_params(("arbitrary", "arbitrary")),
    )(g, recv)


def sum_parts(own, rin, after, name):
    _, r, w = rin.shape
    tr = _row_block(r)

    def body(o_ref, r_ref, after_ref, out_ref):
        out_ref[...] = ((o_ref[...] + r_ref[0].astype(F32)) + r_ref[1].astype(F32)) + r_ref[2].astype(F32)

    return pl.pallas_call(
        body, name=name, grid=(r // tr,), out_shape=jax.ShapeDtypeStruct((r, w), F32),
        in_specs=[pl.BlockSpec((tr, w), lambda i: (i, 0)), pl.BlockSpec((3, tr, w), lambda i: (0, i, 0)),
                  TOKEN_SPEC],
        out_specs=pl.BlockSpec((tr, w), lambda i: (i, 0)),
        compiler_params=_params(("arbitrary",)),
    )(own, rin, after)


def sum_partials(p, land, after, name):
    _, _, r, w = p.shape
    tr = _row_block(r)

    def body(p_ref, l_ref, after_ref, out_ref):
        total = p_ref[0, 0].astype(F32)
        for slot in (6, 0, 3, 1, 4, 2, 5):
            total = total + l_ref[slot].astype(F32)
        out_ref[...] = total

    def own(i):
        return 2 * lax.axis_index("x") + lax.axis_index("y"), lax.axis_index("c"), i, 0

    return pl.pallas_call(
        body, name=name, grid=(r // tr,), out_shape=jax.ShapeDtypeStruct((r, w), F32),
        in_specs=[pl.BlockSpec((1, 1, tr, w), own), pl.BlockSpec((7, tr, w), lambda i: (0, i, 0)), TOKEN_SPEC],
        out_specs=pl.BlockSpec((tr, w), lambda i: (i, 0)),
        compiler_params=_params(("arbitrary",)),
    )(p, land, after)


def _adamw_math(w, g, m, v):
    mn = ADAM_B1 * m + (1.0 - ADAM_B1) * g
    vn = ADAM_B2 * v + (1.0 - ADAM_B2) * (g * g)
    m_hat = mn / (1.0 - ADAM_B1 ** ADAM_STEP)
    v_hat = vn / (1.0 - ADAM_B2 ** ADAM_STEP)
    return -ADAM_LR * (m_hat / (jnp.sqrt(v_hat) + ADAM_EPS) + ADAM_WD * w), mn, vn


def adamw(w, mine, other, m, v, name):
    r, c = w.shape
    rh = r // 2
    tr = _row_block(rh)
    if c >= 1024 and tr % 512 == 0:
        tr = 256
    nb = rh // tr

    def body(w_ref, a_ref, b_ref, m_ref, v_ref, go_ref, d_ref, mo_ref, vo_ref):
        gv = jnp.where(lax.axis_index("c") == pl.program_id(0), a_ref[...], b_ref[...])
        go_ref[...] = gv
        d_ref[...], mo_ref[...], vo_ref[...] = _adamw_math(w_ref[...], gv, m_ref[...], v_ref[...])

    def half(of_sibling):
        def index(h, i):
            owner = lax.axis_index("c")
            owner = 1 - owner if of_sibling else owner
            return jnp.where(h == owner, i, jnp.where(h < owner, 0, nb - 1)), 0
        return pl.BlockSpec((tr, c), index)

    spec = pl.BlockSpec((tr, c), lambda h, i: (h * nb + i, 0))
    out = jax.ShapeDtypeStruct((r, c), F32)
    return pl.pallas_call(
        body, name=name, grid=(2, nb), out_shape=(out, out, out, out),
        in_specs=[spec, half(False), half(True), spec, spec], out_specs=[spec] * 4,
        compiler_params=_params(("arbitrary", "arbitrary")),
    )(w, mine, other, m, v)


def adamw_small(packs, params, after):
    names = list(params)
    flat = [a for n in names for a in params[n]]

    def body(*refs):
        p_ref = refs[0]
        ins = refs[1:1 + 3 * len(names)]
        loss_ref, g_ref = refs[2 + 3 * len(names):4 + 3 * len(names)]
        outs = refs[4 + 3 * len(names):]
        total = p_ref[0]
        for d in range(1, 8):
            total = total + p_ref[d]
        g_ref[...] = total
        loss_ref[...] = g_ref[LOSS_ROW:LOSS_ROW + 1, 0:1]
        me = 2 * lax.axis_index("x") + lax.axis_index("y")
        for i, n in enumerate(names):
            w_ref, m_ref, v_ref = ins[3 * i:3 * i + 3]
            go_ref, d_ref, mo_ref, vo_ref = outs[4 * i:4 * i + 4]
            if n == "conv_dw_w":
                gv = jnp.zeros((CONV_WIDTH, 128), F32)
                for k in range(N_CHIPS):
                    gv = gv + jnp.where(me == k, g_ref[0:CONV_WIDTH, 128 * k:128 * (k + 1)], 0.0)
            elif n in VEC512:
                r0 = 32 + VEC512.index(n)
                gv = g_ref[r0:r0 + 1, :]
            else:
                r0 = 32 + len(VEC512) + 2 * VEC1024.index(n)
                gv = jnp.concatenate([g_ref[r0:r0 + 1, :], g_ref[r0 + 1:r0 + 2, :]], axis=1)
            go_ref[...] = gv
            d_ref[...], mo_ref[...], vo_ref[...] = _adamw_math(w_ref[...], gv, m_ref[...], v_ref[...])

    out_shape = [jax.ShapeDtypeStruct((1, 1), F32), jax.ShapeDtypeStruct(packs.shape[1:], F32)]
    out_shape += [jax.ShapeDtypeStruct(params[n][0].shape, F32) for n in names for _ in range(4)]
    res = pl.pallas_call(
        body, name="adamw_small", out_shape=out_shape,
        in_specs=[VMEM_SPEC] * (2 + len(flat)), out_specs=[VMEM_SPEC] * len(out_shape),
        compiler_params=_params(),
    )(packs, *flat, after)
    return res[0], res[1], {n: res[2 + 4 * i:6 + 4 * i] for i, n in enumerate(names)}


REST = ("w_ffn_up", "w_ffn_down", "w_out", "w_conv_branch", "w_att_branch")
VEC512 = ("conv_dw_b", "conv_ln_g", "conv_ln_b")
VEC1024 = ("norm_mix_pre", "b_conv_branch", "norm_mix_post", "norm_ffn_pre", "norm_ffn_post")
PACK_ROWS = 48
LOSS_ROW = 47


def kernel(x, norm_mix_pre, w_in, conv_dw_w, conv_dw_b, conv_ln_g, conv_ln_b, w_conv_branch, b_conv_branch, w_att_branch, w_out, norm_mix_post, norm_ffn_pre, w_ffn_up, w_ffn_down, norm_ffn_post, loss_target, m_norm_mix_pre, m_w_in, m_conv_dw_w, m_conv_dw_b, m_conv_ln_g, m_conv_ln_b, m_w_conv_branch, m_b_conv_branch, m_w_att_branch, m_w_out, m_norm_mix_post, m_norm_ffn_pre, m_w_ffn_up, m_w_ffn_down, m_norm_ffn_post, v_norm_mix_pre, v_w_in, v_conv_dw_w, v_conv_dw_b, v_conv_ln_g, v_conv_ln_b, v_w_conv_branch, v_b_conv_branch, v_w_att_branch, v_w_out, v_norm_mix_post, v_norm_ffn_pre, v_w_ffn_up, v_w_ffn_down, v_norm_ffn_post):
    weights = dict(norm_mix_pre=norm_mix_pre, w_in=w_in, conv_dw_w=conv_dw_w, conv_dw_b=conv_dw_b, conv_ln_g=conv_ln_g, conv_ln_b=conv_ln_b, w_conv_branch=w_conv_branch, b_conv_branch=b_conv_branch, w_att_branch=w_att_branch, w_out=w_out, norm_mix_post=norm_mix_post, norm_ffn_pre=norm_ffn_pre, w_ffn_up=w_ffn_up, w_ffn_down=w_ffn_down, norm_ffn_post=norm_ffn_post)
    mom = dict(norm_mix_pre=m_norm_mix_pre, w_in=m_w_in, conv_dw_w=m_conv_dw_w, conv_dw_b=m_conv_dw_b, conv_ln_g=m_conv_ln_g, conv_ln_b=m_conv_ln_b, w_conv_branch=m_w_conv_branch, b_conv_branch=m_b_conv_branch, w_att_branch=m_w_att_branch, w_out=m_w_out, norm_mix_post=m_norm_mix_post, norm_ffn_pre=m_norm_ffn_pre, w_ffn_up=m_w_ffn_up, w_ffn_down=m_w_ffn_down, norm_ffn_post=m_norm_ffn_post)
    var = dict(norm_mix_pre=v_norm_mix_pre, w_in=v_w_in, conv_dw_w=v_conv_dw_w, conv_dw_b=v_conv_dw_b, conv_ln_g=v_conv_ln_g, conv_ln_b=v_conv_ln_b, w_conv_branch=v_w_conv_branch, b_conv_branch=v_b_conv_branch, w_att_branch=v_w_att_branch, w_out=v_w_out, norm_mix_post=v_norm_mix_post, norm_ffn_pre=v_norm_ffn_pre, w_ffn_up=v_w_ffn_up, w_ffn_down=v_w_ffn_down, norm_ffn_post=v_norm_ffn_post)
    order = list(weights)
    grads, deltas, new_m, new_v = {}, {}, {}, {}
    xs = x.reshape(SEQ, D_MODEL)
    tgt = loss_target.reshape(SEQ, D_MODEL)
    row = lambda a: a.reshape(1, -1)
    g1, g2, g3, g4 = (row(weights[n]) for n in ("norm_mix_pre", "norm_mix_post", "norm_ffn_pre", "norm_ffn_post"))
    ln_g, ln_b = row(conv_ln_g), row(conv_ln_b)

    summed, from_chips = {}, {}

    def core_sums(names, state, after, tag):
        own, from_sibling = sibling_wait(state, after, True, tag)
        for n, g, r in zip(names, own, from_sibling):
            summed[n] = add_halves(g, r, "add_" + n)

    def chip_sums(names, after):
        return [sum_parts(summed[n][1], from_chips[n], after, "sum_" + n) for n in names]

    def optimize(names, state, after, tag):
        mine, other = sibling_wait(state, after, False, tag)
        for n, a, b in zip(names, mine, other):
            grads[n], deltas[n], new_m[n], new_v[n] = adamw(weights[n], a, b, mom[n], var[n], "adamw_" + n)

    w_in_g, dw_g, *rest = all_gather_weights([w_in], conv_dw_w, [weights[n] for n in REST])
    w_dw_full = jnp.concatenate([dw_g[k] for k in range(N_CHIPS)], axis=1)
    rest_shapes = [weights[n].shape for n in REST]
    over_ici = _gather_copies(rest_shapes, 1, to_both_cores=(2, 3, 4))
    state, token = split_start("gather_start", rest, 3 * (len(REST) + 3), over_ici)
    h1, ci, q, k, v, gc, ga = in_proj_fwd(xs, g1, w_in_g, token)
    u1, u3 = conv_fwd(ci, w_dw_full, row(conv_dw_b), ln_g, ln_b)
    att, rc = attn_fwd(q, k, v)
    rest = split_wait("gather_wait", state, [att], over_ici)
    w_out_g, w_cb_g, w_ab_g = rest[2:]
    w_out_g = w_out_g.reshape(D_MODEL, D_MODEL)
    to_sibling = _gather_copies(rest_shapes[:2], 2)
    state, token = split_start("pass_start", rest[:2], 3 * 2, to_sibling)
    merged, mix, x2, h2 = mix_fwd(u3, att, gc, ga, xs, w_cb_g, row(b_conv_branch), w_ab_g, w_out_g, g2, g3, token)
    w_up_g, w_down_g = split_wait("pass_wait", state, [h2], to_sibling)
    w_down_g = w_down_g.reshape(D_FF, D_MODEL)
    gate, up, act = ffn_up_fwd(h2, w_up_g)
    dff, dy, loss_parts, dg4 = ffn_down_loss(act, w_down_g, x2, tgt, g4)

    dgu = ffn_act_bwd(dff, w_down_g, gate, up)
    dx2, dmix, dg3, dg2 = ffn_in_bwd(dgu, w_up_g, x2, mix, dy, g3, g2)
    ffn_grads = [weight_grad(h2, dgu, "dw_ffn_up", True), weight_grad(act, dff, "dw_ffn_down", False, tk=DW_CHUNK)]
    to_ffn, token = sibling_start(ffn_grads, True, "dw_ffn")
    dco, dao, dg, du3, datt, dbcb = merge_bwd(dmix, w_out_g, gc, ga, u3, att, w_cb_g, row(b_conv_branch), w_ab_g,
                                              token)
    mix_grads = weight_grad_mix(merged, dmix, u3, dco, att, dao)
    core_sums(REST[:2], to_ffn, [mix_grads[0]], "dw_ffn")
    state, token = scatter_start([summed[n][0] for n in REST[:2]], mix_grads)
    dci, ddw, dbdw, dlng, dlnb = conv_bwd(du3, u1, ci, w_dw_full, ln_g, ln_b, token)
    dqkv = attn_bwd(q, k, v, datt, rc, token)
    ffn_from_chips, mix_grads, mix_from_all = scatter_wait(state, [dci, dqkv], 2, len(mix_grads))
    from_chips.update(zip(REST[:2], ffn_from_chips))
    dproj = (dci, dqkv, dg)
    to_in, token = sibling_start([weight_grad_in(h1, dproj)], True, "dw_in")
    grad_x, dg1 = in_proj_bwd(dproj, w_in_g, xs, dx2, g1, token)
    v512 = dict(conv_dw_b=dbdw, conv_ln_g=dlng, conv_ln_b=dlnb)
    v1024 = dict(norm_mix_pre=dg1, b_conv_branch=dbcb, norm_mix_post=dg2, norm_ffn_pre=dg3, norm_ffn_post=dg4)
    packs = small_pack(ddw, v512, v1024, loss_parts)
    core_sums(("w_in",), to_in, [packs], "dw_in")
    to_chips = summed["w_in"][0]
    landing = lax.empty((3,) + to_chips.shape[1:], to_chips.dtype)

    def scatter_and_packs(refs, send, recv):
        return (_scatter_copies(1)(refs[:2], send, recv)
                + _small_copies(refs[2:], _Shifted(send, 3), _Shifted(recv, 3)))

    state, token = split_start("scatter_start_w_in", [to_chips, landing, packs], 3 + 7, scatter_and_packs)
    swap_up, token = sibling_start(chip_sums(REST[:1], token), False, "sum_ffn_up")
    rest_sums = chip_sums(REST[1:2], token) + [sum_partials(p, r, token, "sum_" + n)
                                               for n, p, r in zip(REST[2:], mix_grads, mix_from_all)]
    swap_rest, token = sibling_start(rest_sums, False, "sum_rest")
    optimize(REST[:1], swap_up, [token], "sum_ffn_up")
    optimize(REST[1:], swap_rest, [new_v["w_ffn_up"]], "sum_rest")
    _, from_chips["w_in"], packs = split_wait("scatter_wait_w_in", state, [new_v[n] for n in REST], scatter_and_packs)
    swap_in, token = sibling_start(chip_sums(("w_in",), token), False, "sum_w_in")
    as_rows = lambda n, a: a if n == "conv_dw_w" else a.reshape(1, -1)
    small_names = ("conv_dw_w",) + VEC512 + VEC1024
    loss, gsum, small = adamw_small(
        packs, {n: tuple(as_rows(n, d[n]) for d in (weights, mom, var)) for n in small_names}, token)
    optimize(("w_in",), swap_in, [gsum], "sum_w_in")
    for n in small_names:
        grads[n], deltas[n], new_m[n], new_v[n] = (a.reshape(weights[n].shape) for a in small[n])

    return (loss.reshape(()), grad_x.reshape(1, SEQ, D_MODEL),*[grads[n] for n in order], *[deltas[n] for n in order],
            *[new_m[n] for n in order], *[new_v[n] for n in order])
```

```python
import jax
import jax.numpy as jnp
from jax import lax
from jax.experimental import pallas as pl
from jax.experimental.pallas import tpu as pltpu

F32 = jnp.float32
MM = jnp.bfloat16

SEQ = 2048
D_MODEL = 1024
CONV_DIM = 512
ATT_DIM = 512
CONV_WIDTH = 31
D_FF = 2816
IN_COLS = 2 * CONV_DIM + 3 * ATT_DIM + 2 * D_MODEL
N_CHIPS = 4
IN_SHARD = IN_COLS // N_CHIPS
UP_SHARD = 2 * D_FF // N_CHIPS
BR_SHARD = D_MODEL // N_CHIPS
EPS = 1e-6
ATT_SCALE = 0.125

TM = 256
GLU_ROWS = 256
TQ = 128
CONV_TILE = 64
CONV_WIN = CONV_TILE + 32
VMEM_LIMIT = 56 * 1024 * 1024

ADAM_LR = 0.001
ADAM_B1 = 0.9
ADAM_B2 = 0.999
ADAM_EPS = 1e-08
ADAM_WD = 0.01
ADAM_STEP = 10

MESH = pl.DeviceIdType.MESH
ANY = pl.BlockSpec(memory_space=pl.ANY)
VMEM_SPEC = pl.BlockSpec(memory_space=pltpu.VMEM)

NT_DIMS = (((1,), (1,)), ((), ()))
TN_DIMS = (((0,), (0,)), ((), ()))

IN_PIECES = (("ci", 0, 1024), ("q", 1024, 1536), ("k", 1536, 2048), ("v", 2048, 2560),
             ("gc", 2560, 3584), ("ga", 3584, 4608))


def _params(sem=None, vmem=VMEM_LIMIT):
    return pltpu.CompilerParams(dimension_semantics=sem, vmem_limit_bytes=vmem)


def _dot(a, b):
    return jnp.dot(a, b, preferred_element_type=F32)


def _dot_nt(a, b):
    return lax.dot_general(a, b, NT_DIMS, preferred_element_type=F32)


def _dot_tn(a, b):
    return lax.dot_general(a, b, TN_DIMS, preferred_element_type=F32)


def _sigmoid(x):
    return 1.0 / (1.0 + jnp.exp(-x))


def _rms(x):
    r = lax.rsqrt(jnp.mean(x * x, axis=-1, keepdims=True) + EPS)
    return x * r, r


def _rms_bwd(dy_g, n, r):
    return r * (dy_g - n * jnp.mean(dy_g * n, axis=-1, keepdims=True))


def _row_tile_spec(width, tm=TM):
    return pl.BlockSpec((tm, width), lambda i: (i, 0))


def _full_spec(shape):
    nd = len(shape)
    return pl.BlockSpec(shape, lambda *_: (0,) * nd)


def _weight_spec(shape):
    nd = len(shape)
    return pl.BlockSpec(shape, lambda *_: (0,) * nd, pipeline_mode=pl.Buffered(1))


def _acc_rows(ref, val, first):
    @pl.when(first)
    def _():
        ref[...] = val

    @pl.when(jnp.logical_not(first))
    def _():
        ref[...] += val


TOKEN_SPEC = pl.BlockSpec((8, 128), lambda *_: (0, 0))


def in_proj_fwd(x, g1, w_in_g, after):
    def body(x_ref, g_ref, w_ref, after_ref, h_ref, ci_ref, q_ref, k_ref, v_ref, gc_ref, ga_ref):
        n, _ = _rms(x_ref[...])
        h = (n * g_ref[...]).astype(MM)
        h_ref[...] = h
        outs = dict(ci=ci_ref, q=q_ref, k=k_ref, v=v_ref, gc=gc_ref, ga=ga_ref)
        for j in range(N_CHIPS):
            p = _dot(h, w_ref[j])
            g0 = j * IN_SHARD
            for name, s, e in IN_PIECES:
                lo, hi = max(s, g0), min(e, g0 + IN_SHARD)
                if lo < hi:
                    ref = outs[name]
                    part = p[:, lo - g0:hi - g0]
                    if name == "q":
                        part = part * ATT_SCALE
                    ref[:, lo - s:hi - s] = part.astype(ref.dtype)

    out_shape = [
        jax.ShapeDtypeStruct((SEQ, D_MODEL), MM),
        jax.ShapeDtypeStruct((SEQ, 2 * CONV_DIM), F32),
        jax.ShapeDtypeStruct((SEQ, ATT_DIM), MM),
        jax.ShapeDtypeStruct((SEQ, ATT_DIM), MM),
        jax.ShapeDtypeStruct((SEQ, ATT_DIM), MM),
        jax.ShapeDtypeStruct((SEQ, D_MODEL), F32),
        jax.ShapeDtypeStruct((SEQ, D_MODEL), F32),
    ]
    return pl.pallas_call(
        body, name="in_proj_fwd", grid=(SEQ // TM,), out_shape=out_shape,
        in_specs=[_row_tile_spec(D_MODEL), _full_spec((1, D_MODEL)), _weight_spec(w_in_g.shape), TOKEN_SPEC],
        out_specs=[_row_tile_spec(s.shape[1]) for s in out_shape],
        compiler_params=_params(("arbitrary",)),
    )(x, g1, w_in_g, after)


LANE_GROUPS = [slice(g, g + 128) for g in range(0, CONV_DIM, 128)]
NORM_ROWS = 16


def _shifted_windows(src_ref, t0, cols, offsets):
    win = src_ref[pl.ds(t0, CONV_WIN), cols]
    for rot in range(8):
        ms = [m for m in offsets if m % 8 == rot]
        if ms:
            shifted = win if rot == 0 else pltpu.roll(win, CONV_WIN - rot, 0)
            for m in ms:
                yield m, shifted[m - rot:m - rot + CONV_TILE, :]


def _shifted_sum(src_ref, t0, cols, w_ref, offset_of_tap):
    tap_at = {offset_of_tap(j): j for j in range(CONV_WIDTH)}
    acc = None
    for m, rows in _shifted_windows(src_ref, t0, cols, sorted(tap_at)):
        t = w_ref[tap_at[m]:tap_at[m] + 1, cols] * rows
        acc = t if acc is None else acc + t
    return acc


def _fetch(srcs, dsts, sems):
    copies = [pltpu.make_async_copy(s, d, sems.at[i]) for i, (s, d) in enumerate(zip(srcs, dsts))]
    for cp in copies:
        cp.start()
    return copies


def _row_chunks(src, dst, sems):
    def chunk(i):
        t0 = i * GLU_ROWS
        rows = pl.ds(t0 if isinstance(i, int) else pl.multiple_of(t0, GLU_ROWS), GLU_ROWS)
        return pltpu.make_async_copy(src.at[rows, :], dst.at[rows, :], sems.at[i])

    for i in range(SEQ // GLU_ROWS):
        chunk(i).start()
    return chunk


def _glu_into(ci_chunk, ci_ref, upad_ref):
    upad_ref[0:32, :] = jnp.zeros((32, CONV_DIM), F32)

    def step(i, c):
        ci_chunk(i).wait()
        t0 = pl.multiple_of(i * GLU_ROWS, GLU_ROWS)
        a = ci_ref[pl.ds(t0, GLU_ROWS), 0:CONV_DIM]
        b = ci_ref[pl.ds(t0, GLU_ROWS), CONV_DIM:2 * CONV_DIM]
        upad_ref[pl.ds(t0 + 32, GLU_ROWS), :] = a * _sigmoid(b)
        return c

    lax.fori_loop(0, SEQ // GLU_ROWS, step, 0)


def _layernorm_parts(u1):
    mu = jnp.mean(u1, axis=-1, keepdims=True)
    xc = u1 - mu
    rstd = lax.rsqrt(jnp.mean(xc * xc, axis=-1, keepdims=True) + EPS)
    return xc * rstd, rstd


def conv_fwd(ci, w_dw, b_dw, ln_g, ln_b):
    def body(ci_hbm, w_ref, b_ref, g_ref, bb_ref, u1_ref, u3_ref, upad_ref, ci_ref, sems):
        _glu_into(_row_chunks(ci_hbm, ci_ref, sems), ci_ref, upad_ref)

        def step(i, c):
            t0 = pl.multiple_of(i * CONV_TILE, CONV_TILE)
            for cols in LANE_GROUPS:
                u1_ref[pl.ds(t0, CONV_TILE), cols] = (_shifted_sum(upad_ref, t0, cols, w_ref, lambda j: j + 2)
                                                      + b_ref[:, cols])
            for r in range(0, CONV_TILE, NORM_ROWS):
                rows = pl.ds(t0 + r, NORM_ROWS)
                xh, _ = _layernorm_parts(u1_ref[rows, :])
                u2 = xh * g_ref[...] + bb_ref[...]
                u3_ref[rows, :] = (u2 * _sigmoid(u2)).astype(MM)
            return c

        lax.fori_loop(0, SEQ // CONV_TILE, step, 0)

    return pl.pallas_call(
        body, name="conv_fwd",
        out_shape=[jax.ShapeDtypeStruct((SEQ, CONV_DIM), F32), jax.ShapeDtypeStruct((SEQ, CONV_DIM), MM)],
        in_specs=[ANY] + [VMEM_SPEC] * 4, out_specs=[VMEM_SPEC] * 2,
        scratch_shapes=[pltpu.VMEM((SEQ + 32, CONV_DIM), F32), pltpu.VMEM(ci.shape, ci.dtype),
                        pltpu.SemaphoreType.DMA((SEQ // GLU_ROWS,))],
        compiler_params=_params(),
    )(ci, w_dw, b_dw, ln_g, ln_b)


def _softplus(z):
    return jnp.maximum(z, 0.0) + jnp.log(1.0 + jnp.exp(-jnp.abs(z)))


def _cumsum_weights(suffix, with_total):
    n = 256 if with_total else 128
    r = lax.broadcasted_iota(jnp.int32, (128, n), 0)
    c = lax.broadcasted_iota(jnp.int32, (128, n), 1)
    tri = (r >= c) if suffix else (r <= c)
    return jnp.logical_or(tri, c >= 128).astype(MM)


NO_SCORE = -1e30
N_KB = SEQ // TQ


def _score_bias(lane, row, i, j):
    keep = jnp.logical_and(i >= 0, jnp.logical_or(j < i, lane < row))
    return jnp.where(keep, 0.0, NO_SCORE)


def _block_pipeline(n_stages, descending, step, on_query_block=None):
    n_lag = n_stages - 1
    none = jnp.int32(-1)

    def shift(cur, lag):
        step([cur] + [(lag[2 * s], lag[2 * s + 1]) for s in range(n_lag)])
        return (cur[0], cur[1]) + tuple(lag[:-2])

    def outer(i, lag):
        if on_query_block is not None:
            on_query_block(i)

        def inner(n, lag):
            return shift((i, i - n if descending else n), lag)
        return lax.fori_loop(0, i + 1, inner, lag)

    lag = lax.fori_loop(0, N_KB, outer, (none,) * (2 * n_lag))
    lax.fori_loop(0, n_lag, lambda n, lag: shift((none, none), lag), lag)


def _head_masks():
    lane = lax.broadcasted_iota(jnp.int32, (TQ, 128), 1)
    row = lax.broadcasted_iota(jnp.int32, (TQ, 128), 0)
    return lane, row, lane < 64


def _pick_head(x, head0, h):
    zero = jnp.zeros_like(x)
    return jnp.where(head0, x, zero) if h == 0 else jnp.where(head0, zero, x)


N_PAIRS = ATT_DIM // 128


def _split_heads(src_ref, dst_ref):
    _, _, head0 = _head_masks()

    def block(b, c):
        r0 = pl.multiple_of(b * TQ, TQ)
        d0 = pl.multiple_of(b * 2 * TQ, 2 * TQ)
        for p in range(N_PAIRS):
            x = src_ref[pl.ds(r0, TQ), 128 * p:128 * (p + 1)]
            for h in range(2):
                dst_ref[p, pl.ds(d0 + TQ * h, TQ), :] = _pick_head(x, head0, h)
        return c

    lax.fori_loop(0, N_KB, block, 0)


def attn_fwd(q, k, v):
    def body(q_hbm, k_hbm, v_hbm, o_ref, rc_ref, acc_ref, r_ref, z_ref, spb_ref, ab_ref, qm_ref, vm_ref,
             q_ref, k_ref, v_ref, sems):
        arrive = _fetch((q_hbm, v_hbm, k_hbm), (q_ref, v_ref, k_ref), sems)
        lane, row, _ = _head_masks()
        w = _cumsum_weights(suffix=True, with_total=True)
        acc_ref[...] = jnp.zeros_like(acc_ref)
        r_ref[...] = jnp.zeros_like(r_ref)
        rc_ref[...] = jnp.zeros_like(rc_ref)
        z_ref[...] = jnp.full(z_ref.shape, NO_SCORE, F32)
        spb_ref[...] = jnp.zeros_like(spb_ref)
        ab_ref[...] = jnp.zeros_like(ab_ref)
        arrive[0].wait()
        _split_heads(q_ref, qm_ref)
        arrive[1].wait()
        _split_heads(v_ref, vm_ref)
        arrive[2].wait()

        def step(pairs):
            (i1, j1), (i2, j2), (i3, j3) = pairs
            k1, q2, q3 = (pl.multiple_of(jnp.maximum(b, 0) * TQ, TQ) for b in (j1, i2, i3))
            q1, k3 = (pl.multiple_of(jnp.maximum(b, 0) * 2 * TQ, 2 * TQ) for b in (i1, j3))
            bias1 = _score_bias(lane, row, i1, j1)
            first2 = j2 == i2
            rc_rows = rc_ref[pl.ds(q2, TQ), :]
            for p in range(N_PAIRS):
                cols = slice(128 * p, 128 * (p + 1))
                kb = k_ref[pl.ds(k1, TQ), cols]
                acc_ref[pl.ds(q3, TQ), cols] += _dot(ab_ref[p], vm_ref[p, pl.ds(k3, 2 * TQ), :])
                for h in range(2):
                    hh = 2 * p + h
                    r = _dot(spb_ref[hh], w)
                    r_in = jnp.where(first2, 0.0, r_ref[hh])
                    ab_ref[p, :, 128 * h:128 * (h + 1)] = jnp.exp(z_ref[hh] - (r[:, :128] + r_in)).astype(MM)
                    rc_rows = jnp.where(jnp.logical_and(lane == 16 * hh + j2, i2 >= 0), r_in, rc_rows)
                    r_ref[hh] = r_in + r[:, 128:]
                    z = _dot_nt(qm_ref[p, pl.ds(q1 + TQ * h, TQ), :], kb) + bias1
                    z_ref[hh] = z
                    spb_ref[hh] = _softplus(z).astype(MM)
            rc_ref[pl.ds(q2, TQ), :] = rc_rows

        _block_pipeline(3, True, step)
        o_ref[...] = acc_ref[...].astype(MM)

    return pl.pallas_call(
        body, name="attn_fwd",
        out_shape=[jax.ShapeDtypeStruct((SEQ, ATT_DIM), MM), jax.ShapeDtypeStruct((SEQ, 128), F32)],
        in_specs=[ANY] * 3, out_specs=[VMEM_SPEC] * 2,
        scratch_shapes=[pltpu.VMEM((SEQ, ATT_DIM), F32), pltpu.VMEM((8, TQ, 128), F32),
                        pltpu.VMEM((8, TQ, 128), F32), pltpu.VMEM((8, TQ, 128), MM),
                        pltpu.VMEM((N_PAIRS, TQ, 256), MM), pltpu.VMEM((N_PAIRS, 2 * SEQ, 128), MM),
                        pltpu.VMEM((N_PAIRS, 2 * SEQ, 128), MM)]
                       + [pltpu.VMEM(a.shape, a.dtype) for a in (q, k, v)] + [pltpu.SemaphoreType.DMA((3,))],
        compiler_params=_params(),
    )(q, k, v)


def _branch_outputs(u_ref, a_ref, wcb_ref, bcb_ref, wab_ref):
    u = u_ref[...]
    a = a_ref[...]
    co = jnp.concatenate([_dot(u, wcb_ref[j]) for j in range(N_CHIPS)], axis=1) + bcb_ref[...]
    ao = jnp.concatenate([_dot(a, wab_ref[j]) for j in range(N_CHIPS)], axis=1)
    return co, ao


def mix_fwd(u3, att, gc, ga, x, w_cb_g, b_cb, w_ab_g, w_out_g, g2, g3, after):
    def body(u_ref, a_ref, gc_ref, ga_ref, x_ref, wcb_ref, bcb_ref, wab_ref, wout_ref, g2_ref, g3_ref, after_ref,
             mg_ref, mix_ref, x2_ref, h2_ref):
        co, ao = _branch_outputs(u_ref, a_ref, wcb_ref, bcb_ref, wab_ref)
        merged = (_sigmoid(gc_ref[...]) * co + _sigmoid(ga_ref[...]) * ao).astype(MM)
        mg_ref[...] = merged
        mix = _dot(merged, wout_ref[...])
        mix_ref[...] = mix
        n2, _ = _rms(mix)
        x2 = x_ref[...] + n2 * g2_ref[...]
        x2_ref[...] = x2
        n3, _ = _rms(x2)
        h2_ref[...] = (n3 * g3_ref[...]).astype(MM)

    out_shape = [
        jax.ShapeDtypeStruct((SEQ, D_MODEL), MM), jax.ShapeDtypeStruct((SEQ, D_MODEL), F32),
        jax.ShapeDtypeStruct((SEQ, D_MODEL), F32), jax.ShapeDtypeStruct((SEQ, D_MODEL), MM),
    ]
    vec = _full_spec((1, D_MODEL))
    return pl.pallas_call(
        body, name="mix_fwd", grid=(SEQ // TM,), out_shape=out_shape,
        in_specs=[_row_tile_spec(CONV_DIM), _row_tile_spec(ATT_DIM), _row_tile_spec(D_MODEL),
                  _row_tile_spec(D_MODEL), _row_tile_spec(D_MODEL), _weight_spec(w_cb_g.shape), vec,
                  _weight_spec(w_ab_g.shape), _weight_spec(w_out_g.shape), vec, vec, TOKEN_SPEC],
        out_specs=[_row_tile_spec(D_MODEL)] * 4,
        compiler_params=_params(("arbitrary",)),
    )(u3, att, gc, ga, x, w_cb_g, b_cb, w_ab_g, w_out_g, g2, g3, after)


def ffn_up_fwd(h2, w_up_g):
    def body(h_ref, wg_ref, wu_ref, gate_ref, up_ref, act_ref):
        h = h_ref[...]
        gate = _dot(h, wg_ref[0])
        up = _dot(h, wu_ref[0])
        gate_ref[...] = gate.astype(MM)
        up_ref[...] = up.astype(MM)
        act_ref[...] = (gate * _sigmoid(gate) * up).astype(MM)

    tile = pl.BlockSpec((TM, UP_SHARD), lambda n, i: (i, n))
    act = jax.ShapeDtypeStruct((SEQ, D_FF), MM)
    return pl.pallas_call(
        body, name="ffn_up_fwd", grid=(2, SEQ // TM), out_shape=[act, act, act],
        in_specs=[pl.BlockSpec((TM, D_MODEL), lambda n, i: (i, 0)),
                  pl.BlockSpec((1, D_MODEL, UP_SHARD), lambda n, i: (n, 0, 0)),
                  pl.BlockSpec((1, D_MODEL, UP_SHARD), lambda n, i: (n + 2, 0, 0))],
        out_specs=[tile, tile, tile],
        compiler_params=_params(("arbitrary", "arbitrary")),
    )(h2, w_up_g, w_up_g)


def ffn_down_loss(act, w_down_g, x2, target, g4, gate, up):
    def body(act_ref, wd_ref, x2_ref, t_ref, g_ref, gate_ref, up_ref, dff_ref, dy_ref, loss_ref, dg_ref, dgu_ref):
        ff = _dot(act_ref[...], wd_ref[...])
        n4, r4 = _rms(ff)
        g4v = g_ref[...]
        err = x2_ref[...] + n4 * g4v - t_ref[...]
        row_loss = jnp.mean(err * err, axis=-1, keepdims=True)
        loss_ref[...] = jnp.zeros((8, 128), F32) + 0.5 * jnp.sum(row_loss, axis=0, keepdims=True)
        dy = err * (1.0 / D_MODEL)
        dy_ref[...] = dy
        dff = _rms_bwd(dy * g4v, n4, r4).astype(MM)
        dff_ref[...] = dff
        _acc_rows(dg_ref, jnp.sum(dy * n4, axis=0, keepdims=True), pl.program_id(0) == 0)
        dact = _dot_nt(dff, wd_ref[...])
        gate = gate_ref[...].astype(F32)
        sg = _sigmoid(gate)
        dgu_ref[:, 0:D_FF] = (dact * up_ref[...].astype(F32) * (sg * (1.0 + gate * (1.0 - sg)))).astype(MM)
        dgu_ref[:, D_FF:2 * D_FF] = (dact * (gate * sg)).astype(MM)

    nt = SEQ // TM
    vec = _full_spec((1, D_MODEL))
    return pl.pallas_call(
        body, name="ffn_down_loss", grid=(nt,),
        out_shape=(jax.ShapeDtypeStruct((SEQ, D_MODEL), MM), jax.ShapeDtypeStruct((SEQ, D_MODEL), F32),
                   jax.ShapeDtypeStruct((nt * 8, 128), F32), jax.ShapeDtypeStruct((1, D_MODEL), F32),
                   jax.ShapeDtypeStruct((SEQ, 2 * D_FF), MM)),
        in_specs=[_row_tile_spec(D_FF), _weight_spec(w_down_g.shape), _row_tile_spec(D_MODEL),
                  _row_tile_spec(D_MODEL), vec, _row_tile_spec(D_FF), _row_tile_spec(D_FF)],
        out_specs=[_row_tile_spec(D_MODEL), _row_tile_spec(D_MODEL),
                   pl.BlockSpec((8, 128), lambda i: (i, 0)), vec, _row_tile_spec(2 * D_FF)],
        compiler_params=_params(("arbitrary",)),
    )(act, w_down_g, x2, target, g4, gate, up)


def ffn_in_bwd(dgu, w_up_g, x2, mix, dy, g3, g2):
    def body(dgu_ref, w_ref, x2_ref, mix_ref, dy_ref, g3_ref, g2_ref, dx2_ref, dmix_ref, dg3_ref, dg2_ref):
        dh2 = None
        for j in range(N_CHIPS):
            t = _dot_nt(dgu_ref[:, j * UP_SHARD:(j + 1) * UP_SHARD], w_ref[j])
            dh2 = t if dh2 is None else dh2 + t
        first = pl.program_id(0) == 0
        n3, r3 = _rms(x2_ref[...])
        dx2 = dy_ref[...] + _rms_bwd(dh2 * g3_ref[...], n3, r3)
        dx2_ref[...] = dx2
        _acc_rows(dg3_ref, jnp.sum(dh2 * n3, axis=0, keepdims=True), first)
        n2, r2 = _rms(mix_ref[...])
        dmix_ref[...] = _rms_bwd(dx2 * g2_ref[...], n2, r2).astype(MM)
        _acc_rows(dg2_ref, jnp.sum(dx2 * n2, axis=0, keepdims=True), first)

    vec = _full_spec((1, D_MODEL))
    return pl.pallas_call(
        body, name="ffn_in_bwd", grid=(SEQ // TM,),
        out_shape=(jax.ShapeDtypeStruct((SEQ, D_MODEL), F32), jax.ShapeDtypeStruct((SEQ, D_MODEL), MM),
                   jax.ShapeDtypeStruct((1, D_MODEL), F32), jax.ShapeDtypeStruct((1, D_MODEL), F32)),
        in_specs=[_row_tile_spec(2 * D_FF), _weight_spec(w_up_g.shape), _row_tile_spec(D_MODEL),
                  _row_tile_spec(D_MODEL), _row_tile_spec(D_MODEL), vec, vec],
        out_specs=[_row_tile_spec(D_MODEL), _row_tile_spec(D_MODEL), vec, vec],
        compiler_params=_params(("arbitrary",)),
    )(dgu, w_up_g, x2, mix, dy, g3, g2)


def merge_bwd(dmix, w_out_g, gc, ga, u3, att, w_cb_g, b_cb, w_ab_g, after):
    def body(dmix_ref, wout_ref, gc_ref, ga_ref, u_ref, a_ref, wcb_ref, bcb_ref, wab_ref, after_ref,
             dco_ref, dao_ref, dg_ref, du3_ref, datt_ref, dbcb_ref):
        dm = _dot_nt(dmix_ref[...], wout_ref[...])
        co, ao = _branch_outputs(u_ref, a_ref, wcb_ref, bcb_ref, wab_ref)
        sgc = _sigmoid(gc_ref[...])
        sga = _sigmoid(ga_ref[...])
        dco = dm * sgc
        dao = dm * sga
        dg_ref[:, 0:D_MODEL] = (dm * co * (sgc * (1.0 - sgc))).astype(MM)
        dg_ref[:, D_MODEL:2 * D_MODEL] = (dm * ao * (sga * (1.0 - sga))).astype(MM)
        _acc_rows(dbcb_ref, jnp.sum(dco, axis=0, keepdims=True), pl.program_id(0) == 0)
        dco_ref[...] = dco.astype(MM)
        dao_ref[...] = dao.astype(MM)
        du3 = None
        datt = None
        for j in range(N_CHIPS):
            cols = slice(j * BR_SHARD, (j + 1) * BR_SHARD)
            t = _dot_nt(dco_ref[:, cols], wcb_ref[j])
            s = _dot_nt(dao_ref[:, cols], wab_ref[j])
            du3 = t if du3 is None else du3 + t
            datt = s if datt is None else datt + s
        du3_ref[...] = du3
        datt_ref[...] = datt.astype(MM)

    wide = _row_tile_spec(D_MODEL)
    return pl.pallas_call(
        body, name="merge_bwd", grid=(SEQ // TM,),
        out_shape=(jax.ShapeDtypeStruct((SEQ, D_MODEL), MM), jax.ShapeDtypeStruct((SEQ, D_MODEL), MM),
                   jax.ShapeDtypeStruct((SEQ, 2 * D_MODEL), MM),
                   jax.ShapeDtypeStruct((SEQ, CONV_DIM), F32), jax.ShapeDtypeStruct((SEQ, ATT_DIM), MM),
                   jax.ShapeDtypeStruct((1, D_MODEL), F32)),
        in_specs=[wide, _weight_spec(w_out_g.shape), wide, wide, _row_tile_spec(CONV_DIM), _row_tile_spec(ATT_DIM),
                  _weight_spec(w_cb_g.shape), _full_spec((1, D_MODEL)), _weight_spec(w_ab_g.shape), TOKEN_SPEC],
        out_specs=[wide, wide, _row_tile_spec(2 * D_MODEL), _row_tile_spec(CONV_DIM), _row_tile_spec(ATT_DIM),
                   _full_spec((1, D_MODEL))],
        compiler_params=_params(("arbitrary",)),
    )(dmix, w_out_g, gc, ga, u3, att, w_cb_g, b_cb, w_ab_g, after)


def conv_bwd(du3, u1, ci, w_dw, ln_g, ln_b, after):
    def body(du3_hbm, u1_hbm, ci_hbm, w_ref, g_ref, bb_ref, after_ref,
             dci_ref, dw_ref, dbdw_ref, dg_ref, db_ref, upad_ref, dpad_ref, dwacc_ref, vacc_ref,
             du3_ref, u1_ref, ci_ref, ci_sems, u1_sems, du3_sems):
        ci_chunk = _row_chunks(ci_hbm, ci_ref, ci_sems)
        u1_chunk = _row_chunks(u1_hbm, u1_ref, u1_sems)
        du3_chunk = _row_chunks(du3_hbm, du3_ref, du3_sems)
        _glu_into(ci_chunk, ci_ref, upad_ref)
        dpad_ref[SEQ:SEQ + 32, :] = jnp.zeros((32, CONV_DIM), F32)
        dwacc_ref[...] = jnp.zeros_like(dwacc_ref)
        vacc_ref[...] = jnp.zeros_like(vacc_ref)

        def fold8(t):
            s = t[0:8, :]
            for r in range(8, t.shape[0], 8):
                s = s + t[r:r + 8, :]
            return s

        def pass1(i, c):
            t0 = pl.multiple_of(i * CONV_TILE, CONV_TILE)
            gv = g_ref[...]
            for r in range(0, CONV_TILE, NORM_ROWS):
                rows = pl.ds(t0 + r, NORM_ROWS)
                xh, rstd = _layernorm_parts(u1_ref[rows, :])
                u2 = xh * gv + bb_ref[...]
                s2 = _sigmoid(u2)
                du2 = du3_ref[rows, :] * (s2 * (1.0 + u2 * (1.0 - s2)))
                wv = du2 * gv
                du1 = rstd * (wv - jnp.mean(wv, axis=-1, keepdims=True)
                              - xh * jnp.mean(wv * xh, axis=-1, keepdims=True))
                dpad_ref[rows, :] = du1
                vacc_ref[0] += fold8(du2 * xh)
                vacc_ref[1] += fold8(du2)
                vacc_ref[2] += fold8(du1)
            for cols in LANE_GROUPS:
                du1 = dpad_ref[pl.ds(t0, CONV_TILE), cols]
                for m, rows in _shifted_windows(upad_ref, t0, cols, range(2, CONV_WIDTH + 2)):
                    dwacc_ref[m - 2, :, cols] += fold8(du1 * rows)
            return c

        tiles_per_chunk = GLU_ROWS // CONV_TILE

        def pass1_chunk(ch, c):
            u1_chunk(ch).wait()
            du3_chunk(ch).wait()
            return lax.fori_loop(ch * tiles_per_chunk, (ch + 1) * tiles_per_chunk, pass1, c)

        lax.fori_loop(0, SEQ // GLU_ROWS, pass1_chunk, 0)

        def pass2(i, c):
            t0 = pl.multiple_of(i * CONV_TILE, CONV_TILE)
            tile = pl.ds(t0, CONV_TILE)
            for cols in LANE_GROUPS:
                gate_cols = slice(cols.start + CONV_DIM, cols.stop + CONV_DIM)
                du0 = _shifted_sum(dpad_ref, t0, cols, w_ref, lambda j: 30 - j)
                a = ci_ref[tile, cols]
                sb = _sigmoid(ci_ref[tile, gate_cols])
                dci_ref[tile, cols] = (du0 * sb).astype(MM)
                dci_ref[tile, gate_cols] = (du0 * a * (sb * (1.0 - sb))).astype(MM)
            return c

        lax.fori_loop(0, SEQ // CONV_TILE, pass2, 0)

        for j in range(CONV_WIDTH):
            dw_ref[j:j + 1, :] = jnp.sum(dwacc_ref[j], axis=0, keepdims=True)
        dw_ref[CONV_WIDTH:32, :] = jnp.zeros((32 - CONV_WIDTH, CONV_DIM), F32)
        dg_ref[...] = jnp.sum(vacc_ref[0], axis=0, keepdims=True)
        db_ref[...] = jnp.sum(vacc_ref[1], axis=0, keepdims=True)
        dbdw_ref[...] = jnp.sum(vacc_ref[2], axis=0, keepdims=True)

    vec = jax.ShapeDtypeStruct((1, CONV_DIM), F32)
    return pl.pallas_call(
        body, name="conv_bwd",
        out_shape=(jax.ShapeDtypeStruct((SEQ, 2 * CONV_DIM), MM), jax.ShapeDtypeStruct((32, CONV_DIM), F32),
                   vec, vec, vec),
        in_specs=[ANY] * 3 + [VMEM_SPEC] * 4, out_specs=[VMEM_SPEC] * 5,
        scratch_shapes=[pltpu.VMEM((SEQ + 32, CONV_DIM), F32), pltpu.VMEM((SEQ + 32, CONV_DIM), F32),
                        pltpu.VMEM((CONV_WIDTH, 8, CONV_DIM), F32), pltpu.VMEM((3, 8, CONV_DIM), F32)]
                       + [pltpu.VMEM(a.shape, a.dtype) for a in (du3, u1, ci)]
                       + [pltpu.SemaphoreType.DMA((SEQ // GLU_ROWS,))] * 3,
        compiler_params=_params(),
    )(du3, u1, ci, w_dw, ln_g, ln_b, after)


def attn_bwd(q, k, v, datt, rc, after):
    def body(q_hbm, k_hbm, v_hbm, do_hbm, rc_hbm, after_ref, dqkv_ref, dqa_ref, dka_ref, dva_ref, pc_ref, z_ref,
             sig1_ref, sig2_ref, g_ref, spb_ref, gb_ref, ar_ref, dzr_ref, dzc_ref, qm_ref, km_ref, dom_ref,
             q_ref, k_ref, v_ref, do_ref, rc_ref, sems):
        arrive = _fetch((q_hbm, k_hbm, do_hbm, v_hbm, rc_hbm), (q_ref, k_ref, do_ref, v_ref, rc_ref), sems)
        lane, row, _ = _head_masks()
        for ref in (dqa_ref, dka_ref, dva_ref, pc_ref):
            ref[...] = jnp.zeros_like(ref)
        z_ref[...] = jnp.full(z_ref.shape, NO_SCORE, F32)
        for ref in (sig1_ref, sig2_ref, spb_ref, ar_ref, g_ref, gb_ref, dzr_ref, dzc_ref):
            ref[...] = jnp.zeros_like(ref)
        for cp, (src_ref, split_ref) in zip(arrive, ((q_ref, qm_ref), (k_ref, km_ref), (do_ref, dom_ref))):
            cp.wait()
            _split_heads(src_ref, split_ref)
        arrive[3].wait()
        arrive[4].wait()
        w_suffix = _cumsum_weights(suffix=True, with_total=False)
        w_prefix = _cumsum_weights(suffix=False, with_total=True)

        def step(pairs):
            (ia, ja), (ib, jb), (ic, jc), (id_, jd) = pairs
            ka, qb_, kb_, kc, qd, kd = (pl.multiple_of(jnp.maximum(b, 0) * TQ, TQ) for b in (ja, ib, jb, jc, id_, jd))
            qa2, qb2, qc2, qd2, kd2 = (pl.multiple_of(jnp.maximum(b, 0) * 2 * TQ, 2 * TQ)
                                       for b in (ia, ib, ic, id_, jd))
            bias_a = _score_bias(lane, row, ia, ja)
            rc_rows = rc_ref[pl.ds(qb_, TQ), :]
            first_c = jc == 0
            for p in range(N_PAIRS):
                cols = slice(128 * p, 128 * (p + 1))
                k_a = k_ref[pl.ds(ka, TQ), cols]
                v_b = v_ref[pl.ds(kb_, TQ), cols]
                dqa_ref[pl.ds(qd, TQ), cols] += _dot(dzc_ref[p], km_ref[p, pl.ds(kd2, 2 * TQ), :])
                dka_ref[pl.ds(kd, TQ), cols] += _dot_tn(dzr_ref[p], qm_ref[p, pl.ds(qd2, 2 * TQ), :])
                dva_ref[pl.ds(kc, TQ), cols] += _dot_tn(ar_ref[p], dom_ref[p, pl.ds(qc2, 2 * TQ), :])
                for h in range(2):
                    hh = 2 * p + h
                    rows = slice(TQ * h, TQ * (h + 1))
                    r = _dot(gb_ref[hh], w_prefix)
                    p_in = jnp.where(first_c, 0.0, pc_ref[hh])
                    dz = (g_ref[hh] - sig2_ref[hh] * (r[:, :128] + p_in)).astype(MM)
                    dzc_ref[p, :, rows] = dz
                    dzr_ref[p, rows, :] = dz
                    pc_ref[hh] = p_in + r[:, 128:]
                    r_in = jnp.sum(jnp.where(lane == 16 * hh + jb, rc_rows, 0.0), axis=1, keepdims=True)
                    a = jnp.exp(z_ref[hh] - (_dot(spb_ref[hh], w_suffix) + r_in))
                    g = _dot_nt(dom_ref[p, pl.ds(qb2 + TQ * h, TQ), :], v_b) * a
                    ar_ref[p, rows, :] = a.astype(MM)
                    g_ref[hh] = g
                    gb_ref[hh] = g.astype(MM)
                    sig2_ref[hh] = sig1_ref[hh]
                    z = _dot_nt(qm_ref[p, pl.ds(qa2 + TQ * h, TQ), :], k_a) + bias_a
                    sp = _softplus(z)
                    sig1_ref[hh] = jnp.exp(z - sp)
                    z_ref[hh] = z
                    spb_ref[hh] = sp.astype(MM)

        _block_pipeline(4, False, step)
        dqkv_ref[:, 0:ATT_DIM] = (dqa_ref[...] * ATT_SCALE).astype(MM)
        dqkv_ref[:, ATT_DIM:2 * ATT_DIM] = dka_ref[...].astype(MM)
        dqkv_ref[:, 2 * ATT_DIM:3 * ATT_DIM] = dva_ref[...].astype(MM)

    split = pltpu.VMEM((N_PAIRS, 2 * SEQ, 128), MM)
    return pl.pallas_call(
        body, name="attn_bwd", out_shape=jax.ShapeDtypeStruct((SEQ, 3 * ATT_DIM), MM),
        in_specs=[ANY] * 5 + [VMEM_SPEC], out_specs=VMEM_SPEC,
        scratch_shapes=[pltpu.VMEM((SEQ, ATT_DIM), F32)] * 3 + [pltpu.VMEM((8, TQ, 128), F32)] * 5
                       + [pltpu.VMEM((8, TQ, 128), MM)] * 2
                       + [pltpu.VMEM((N_PAIRS, 2 * TQ, 128), MM)] * 2 + [pltpu.VMEM((N_PAIRS, TQ, 256), MM)]
                       + [split] * 3
                       + [pltpu.VMEM(a.shape, a.dtype) for a in (q, k, v, datt, rc)] + [pltpu.SemaphoreType.DMA((5,))],
        compiler_params=_params(),
    )(q, k, v, datt, rc, after)


DPROJ_PIECES = ((0, 1024), (1024, 2560), (2560, 4608))


def _dproj_segments(j):
    g0, g1 = j * IN_SHARD, (j + 1) * IN_SHARD
    segs = []
    for p, (s, e) in enumerate(DPROJ_PIECES):
        lo, hi = max(s, g0), min(e, g1)
        if lo < hi:
            segs.append((p, lo - s, lo - g0, hi - lo))
    return segs


def in_proj_bwd(pieces, w_in_g, x, dx2, g1, after):
    def body(p0_ref, p1_ref, p2_ref, w_ref, x_ref, dx2_ref, g_ref, after_ref, dx_ref, dg_ref):
        p_refs = (p0_ref, p1_ref, p2_ref)
        dh = None
        for j in range(N_CHIPS):
            for p, lo, off, width in _dproj_segments(j):
                t = _dot_nt(p_refs[p][:, lo:lo + width], w_ref[j, :, off:off + width])
                dh = t if dh is None else dh + t
        n1, r1 = _rms(x_ref[...])
        dx_ref[...] = dx2_ref[...] + _rms_bwd(dh * g_ref[...], n1, r1)
        _acc_rows(dg_ref, jnp.sum(dh * n1, axis=0, keepdims=True), pl.program_id(0) == 0)

    vec = _full_spec((1, D_MODEL))
    return pl.pallas_call(
        body, name="in_proj_bwd", grid=(SEQ // TM,),
        out_shape=[jax.ShapeDtypeStruct((SEQ, D_MODEL), F32), jax.ShapeDtypeStruct((1, D_MODEL), F32)],
        in_specs=[_row_tile_spec(p.shape[1]) for p in pieces]
                 + [_weight_spec(w_in_g.shape), _row_tile_spec(D_MODEL), _row_tile_spec(D_MODEL), vec, TOKEN_SPEC],
        out_specs=[_row_tile_spec(D_MODEL), vec],
        compiler_params=_params(("arbitrary",)),
    )(*pieces, w_in_g, x, dx2, g1, after)


def weight_grad_in(h1, pieces):
    kh = D_MODEL // 2
    operands = (h1,) + tuple(pieces)
    out = jax.ShapeDtypeStruct((N_CHIPS, 2, kh, IN_SHARD), MM)

    def body(*refs):
        hbm, o_hbm, bufs, stage_ref, in_sems, out_sems = refs[:4], refs[4], refs[5:9], refs[9], refs[10], refs[11]
        arrive = _fetch(hbm, bufs, in_sems)
        a_ref, p_refs = bufs[0], bufs[1:]
        arrive[0].wait()
        here, leaving = set(), []
        for j in range(N_CHIPS):
            for p in sorted({seg[0] for seg in _dproj_segments(j)} - here):
                arrive[1 + p].wait()
                here.add(p)
            for h in range(2):
                a = a_ref[:, h * kh:(h + 1) * kh]
                for p, lo, off, width in _dproj_segments(j):
                    stage_ref[j, h, :, off:off + width] = _dot_tn(a, p_refs[p][:, lo:lo + width]).astype(MM)
                leaving.append(pltpu.make_async_copy(stage_ref.at[j, h], o_hbm.at[j, h], out_sems.at[2 * j + h]))
                leaving[-1].start()
        for cp in leaving:
            cp.wait()

    return pl.pallas_call(
        body, name="dw_in", out_shape=out, in_specs=[ANY] * 4, out_specs=ANY,
        scratch_shapes=[pltpu.VMEM(a.shape, a.dtype) for a in operands]
                       + [pltpu.VMEM(out.shape, out.dtype), pltpu.SemaphoreType.DMA((4,)),
                          pltpu.SemaphoreType.DMA((2 * N_CHIPS,))],
        compiler_params=_params(),
    )(*operands)


def weight_grad(a, b, name, col_sharded, tk=None):
    kin, n = a.shape[1], b.shape[1]

    def body(a_ref, b_ref, o_ref):
        if col_sharded:
            o_ref[0, 0] = _dot_tn(a_ref[...], b_ref[...]).astype(MM)
        else:
            o_ref[...] = _dot_tn(a_ref[...], b_ref[...]).astype(MM)

    if col_sharded:
        kh, ns = kin // 2, n // N_CHIPS
        out = jax.ShapeDtypeStruct((N_CHIPS, 2, kh, ns), MM)
        grid = (2, N_CHIPS)
        in_specs = [pl.BlockSpec((SEQ, kh), lambda h, j: (0, h)), pl.BlockSpec((SEQ, ns), lambda h, j: (0, j))]
        out_spec = pl.BlockSpec((1, 1, kh, ns), lambda h, j: (j, h, 0, 0))
        sem = ("arbitrary", "arbitrary")
    else:
        out = jax.ShapeDtypeStruct((kin, n), MM)
        grid = (kin // tk,)
        in_specs = [pl.BlockSpec((SEQ, tk), lambda r: (0, r)), pl.BlockSpec((SEQ, n), lambda r: (0, 0))]
        out_spec = pl.BlockSpec((tk, n), lambda r: (r, 0))
        sem = ("arbitrary",)
    res = pl.pallas_call(
        body, name=name, grid=grid, out_shape=out, in_specs=in_specs, out_specs=out_spec,
        compiler_params=_params(sem),
    )(a, b)
    if not col_sharded:
        res = res.reshape(N_CHIPS, 2, kin // (2 * N_CHIPS), n)
    return res


def weight_grad_mix(merged, dmix, u3, dco, att, dao):
    operands = (merged, dmix, u3, dco, att, dao)
    n_out, n_br = D_MODEL // 2, CONV_DIM // 2

    def body(*refs):
        hbm, (o_out, o_cb, o_ab), bufs, sems = refs[:6], refs[6:9], refs[9:15], refs[15]
        copies = [pltpu.make_async_copy(hbm[i], bufs[i], sems.at[i]) for i in range(6)]
        for cp in copies:
            cp.start()
        m_ref, dm_ref, u_ref, dco_ref, a_ref, dao_ref = bufs
        copies[0].wait()
        copies[1].wait()
        for h in range(2):
            o_out[h * n_out:(h + 1) * n_out, :] = _dot_tn(m_ref[:, h * n_out:(h + 1) * n_out], dm_ref[...]).astype(MM)
        for br, (a, d, o) in enumerate(((u_ref, dco_ref, o_cb), (a_ref, dao_ref, o_ab))):
            copies[2 + 2 * br].wait()
            copies[3 + 2 * br].wait()
            for h in range(2):
                g = _dot_tn(a[:, h * n_br:(h + 1) * n_br], d[...])
                for j in range(N_CHIPS):
                    o[j, h] = g[:, j * BR_SHARD:(j + 1) * BR_SHARD].astype(MM)

    branch = jax.ShapeDtypeStruct((N_CHIPS, 2, n_br, BR_SHARD), MM)
    dw_out, dw_cb, dw_ab = pl.pallas_call(
        body, name="dw_mix", out_shape=[jax.ShapeDtypeStruct((D_MODEL, D_MODEL), MM), branch, branch],
        in_specs=[ANY] * 6, out_specs=[VMEM_SPEC] * 3,
        scratch_shapes=[pltpu.VMEM(a.shape, a.dtype) for a in operands] + [pltpu.SemaphoreType.DMA((6,))],
        compiler_params=_params(),
    )(*operands)
    return dw_out.reshape(N_CHIPS, 2, D_MODEL // (2 * N_CHIPS), D_MODEL), dw_cb, dw_ab


def _place():
    x, y, c = lax.axis_index("x"), lax.axis_index("y"), lax.axis_index("c")
    chips = [(1 - x, y), (x, 1 - y), (1 - x, 1 - y)]
    return x, y, c, chips


def _rcopy(src, dst, send_sem, recv_sem, dev):
    return pltpu.make_async_remote_copy(src_ref=src, dst_ref=dst, send_sem=send_sem, recv_sem=recv_sem,
                                        device_id=dev, device_id_type=MESH)


class _Gather:
    N_MOVES = 6

    def __init__(self, shapes, w, o, scratch):
        self.n, self.shapes, self.w, self.o = len(w), shapes, w, o
        self.send, self.recv, self.psend, self.precv, self.loc_in, self.loc_out = scratch[:6]
        self.raw, self.stage = scratch[6:6 + self.n], scratch[6 + self.n:]
        x, y, c, self.chips = _place()
        self.c = c
        self.me, k_x, k_y, k_far = 2 * x + y, 2 * (1 - x) + y, 2 * x + (1 - y), 2 * (1 - x) + (1 - y)
        to_x, to_y = (1 - x, y, c), (x, 1 - y, c)
        self.sib = (x, y, 1 - c)
        self.sent_as = [(self.me, 0, to_x), (self.me, 1, to_y), (self.me, 1, to_x), (self.me, 0, to_y),
                        (k_x, 0, to_y), (k_y, 1, to_x)]
        self.arrives_as = [(k_x, 0, to_x), (k_y, 1, to_y), (k_x, 1, to_x), (k_y, 0, to_y),
                           (k_far, 0, to_y), (k_far, 1, to_x)]
        self.sent_on_after = {0: 4, 1: 5}

    @staticmethod
    def scratch(shards):
        n = len(shards)
        sems = pltpu.SemaphoreType.DMA
        m = _Gather.N_MOVES * n
        return ([sems((m,)), sems((m,)), sems((m,)), sems((m,)), sems((3 * n,)), sems((n,))]
                + [pltpu.VMEM(s.shape, s.dtype) for s in shards] + [pltpu.VMEM(s.shape, MM) for s in shards])

    @staticmethod
    def out_shapes(shards):
        return [jax.ShapeDtypeStruct((N_CHIPS,) + s.shape, MM) for s in shards]

    def _rows(self, t, quarter, cc):
        rq = self.shapes[t][0] // 4
        return pl.ds(pl.multiple_of((2 * cc + quarter) * rq, rq), rq)

    def _own_rows(self, t, piece):
        if piece < 2:
            return self._rows(t, piece, self.c)
        rh = self.shapes[t][0] // 2
        return pl.ds(pl.multiple_of((1 - self.c) * rh, rh), rh)

    def _chip(self, j):
        cx, cy = self.chips[j]
        return 2 * cx + cy, (cx, cy, self.c)

    def local_in(self, t, piece):
        rows = self._own_rows(t, piece)
        return pltpu.make_async_copy(self.w[t].at[rows, :], self.raw[t].at[rows, :], self.loc_in.at[3 * t + piece])

    def local_out(self, t):
        return pltpu.make_async_copy(self.stage[t], self.o[t].at[self.me], self.loc_out.at[t])

    def sent(self, i, t):
        k, quarter, dev = self.sent_as[i]
        rows = self._rows(t, quarter, self.c)
        there = self.o[t].at[k, rows, :]
        return _rcopy(self.stage[t].at[rows, :] if i < 4 else there, there,
                      self.send.at[i * self.n + t], self.recv.at[i * self.n + t], dev)

    def arrived(self, i, t):
        k, quarter, dev = self.arrives_as[i]
        blk = self.o[t].at[k, self._rows(t, quarter, self.c), :]
        return _rcopy(blk, blk, self.send.at[i * self.n + t], self.recv.at[i * self.n + t], dev)

    def passed(self, i, t, cc):
        k, quarter, _ = self.arrives_as[i]
        blk = self.o[t].at[k, self._rows(t, quarter, cc), :]
        return _rcopy(blk, blk, self.psend.at[i * self.n + t], self.precv.at[i * self.n + t], self.sib)

    def start(self):
        for piece in range(3):
            for t in range(self.n):
                self.local_in(t, piece).start()
        for piece, moves in enumerate(((0, 3), (1, 2), ())):
            for t in range(self.n):
                rows = self._own_rows(t, piece)
                self.local_in(t, piece).wait()
                self.stage[t][rows, :] = self.raw[t][rows, :].astype(MM)
                for i in moves:
                    self.sent(i, t).start()
        for t in range(self.n):
            self.local_out(t).start()

    def forward(self):
        for i in range(self.N_MOVES):
            for t in range(self.n):
                self.arrived(i, t).wait_recv()
                if i in self.sent_on_after:
                    self.sent(self.sent_on_after[i], t).start()
                self.passed(i, t, self.c).start()

    def finish(self):
        for i in range(self.N_MOVES):
            for t in range(self.n):
                self.passed(i, t, 1 - self.c).wait_recv()
        for i in range(self.N_MOVES):
            for t in range(self.n):
                self.sent(i, t).wait_send()
                self.passed(i, t, self.c).wait_send()
        for t in range(self.n):
            self.local_out(t).wait()


def all_gather_weights(shards, small, later):
    n, m = len(shards), len(later)
    shapes = [s.shape for s in shards]

    def body(*refs):
        w = refs[:n]
        sm = refs[n]
        lw = refs[n + 1:n + 1 + m]
        o = refs[n + 1 + m:2 * n + 1 + m]
        osm = refs[2 * n + 1 + m]
        lo = refs[2 * n + 2 + m:2 * n + 2 + 2 * m]
        scratch = refs[2 * n + 2 + 2 * m:]
        ssend, srecv, sloc, lsem_in, lsem_out = scratch[:5]
        lraw, lstage = scratch[5:5 + m], scratch[5 + m:5 + 2 * m]
        g = _Gather(shapes, w, o, scratch[5 + 2 * m:])
        own = pltpu.make_async_copy(sm, osm.at[g.me], sloc)
        own.start()
        g.start()
        loads = [pltpu.make_async_copy(lw[t], lraw[t], lsem_in.at[t]) for t in range(m)]
        for cp in loads:
            cp.start()
        small_cps = [_rcopy(sm, osm.at[g.me], ssend.at[j], srecv.at[j], g._chip(j)[1]) for j in range(3)]
        for cp in small_cps:
            cp.start()
        places = []
        for t in range(m):
            loads[t].wait()
            lstage[t][...] = lraw[t][...].astype(MM)
            places.append(pltpu.make_async_copy(lstage[t], lo[t].at[g.me], lsem_out.at[t]))
            places[t].start()
        g.forward()
        g.finish()
        for j in range(3):
            k, dev = g._chip(j)
            _rcopy(sm, osm.at[k], ssend.at[j], srecv.at[j], dev).wait_recv()
            small_cps[j].wait_send()
        own.wait()
        for cp in places:
            cp.wait()

    out_shape = _Gather.out_shapes(shards)
    out_shape.append(jax.ShapeDtypeStruct((N_CHIPS,) + small.shape, small.dtype))
    out_shape += _Gather.out_shapes(later)
    sems = pltpu.SemaphoreType.DMA
    return pl.pallas_call(
        body, name="all_gather_weights", out_shape=out_shape,
        in_specs=[ANY] * (n + 1 + m), out_specs=[ANY] * (n + 1 + m),
        scratch_shapes=[sems((3,)), sems((3,)), sems, sems((m,)), sems((m,))]
                       + [pltpu.VMEM(s.shape, s.dtype) for s in later] + [pltpu.VMEM(s.shape, MM) for s in later]
                       + _Gather.scratch(shards),
        compiler_params=_params(),
    )(*shards, small, *later)


HBM_SPEC = pl.BlockSpec(memory_space=pltpu.HBM)
SEM_SPEC = pl.BlockSpec(memory_space=pltpu.SEMAPHORE)
DATAFLOW = pltpu.SideEffectType.DATAFLOW_SIDE_EFFECTING


def split_start(name, bufs, n_copies, copies):
    nb = len(bufs)

    def body(*refs):
        for cp in copies(refs[:nb], refs[nb], refs[nb + 1]):
            cp.start()
        token = refs[2 * nb + 2]
        token[...] = jnp.zeros_like(token)

    sems = [pltpu.SemaphoreType.DMA((n_copies,))] * 2
    res = pl.pallas_call(
        body, name=name,
        out_shape=sems + [pltpu.HBM(a.shape, a.dtype) for a in bufs] + [jax.ShapeDtypeStruct((8, 128), F32)],
        in_specs=[HBM_SPEC] * nb, out_specs=[SEM_SPEC] * 2 + [HBM_SPEC] * nb + [VMEM_SPEC],
        input_output_aliases={i: 2 + i for i in range(nb)},
        compiler_params=pltpu.CompilerParams(has_side_effects=DATAFLOW),
    )(*[pltpu.with_memory_space_constraint(a, pltpu.HBM) for a in bufs])
    return res[:-1], res[-1]


def split_wait(name, state, after, copies):
    sems, bufs = state[:2], state[2:]
    nb = len(bufs)

    def body(*refs):
        for cp in copies(refs[:nb], refs[nb], refs[nb + 1]):
            cp.wait_send()
            cp.wait_recv()

    return pl.pallas_call(
        body, name=name, out_shape=[pltpu.HBM(a.shape, a.dtype) for a in bufs],
        in_specs=[HBM_SPEC] * nb + [SEM_SPEC] * 2 + [ANY] * len(after), out_specs=[HBM_SPEC] * nb,
        input_output_aliases={i: i for i in range(nb)},
        compiler_params=pltpu.CompilerParams(has_side_effects=DATAFLOW),
    )(*bufs, *sems, *after)


class _Shifted:
    def __init__(self, sems, first):
        self.sems, self.first = sems, first

    @property
    def at(self):
        return self

    def __getitem__(self, i):
        return self.sems.at[self.first + i]


def _scatter_copies(n):
    def copies(refs, send, recv):
        _, _, c, chips = _place()
        return [_rcopy(refs[t].at[2 * cx + cy], refs[n + t].at[j], send.at[3 * t + j], recv.at[3 * t + j], (cx, cy, c))
                for t in range(n) for j, (cx, cy) in enumerate(chips)]
    return copies


def _direct_copies(n):
    def copies(refs, send, recv):
        x, y, c, chips = _place()
        out = []
        for t in range(n):
            src, land = refs[t], refs[n + t]
            for to_core, first in ((c, 0), (1 - c, 3)):
                for j, (cx, cy) in enumerate(chips):
                    i = 7 * t + first + j
                    out.append(_rcopy(src.at[2 * cx + cy, to_core], land.at[first + j], send.at[i], recv.at[i],
                                      (cx, cy, to_core)))
            i = 7 * t + 6
            out.append(_rcopy(src.at[2 * x + y, 1 - c], land.at[6], send.at[i], recv.at[i], (x, y, 1 - c)))
        return out
    return copies


def _scatter_and_direct(n, m):
    def copies(refs, send, recv):
        return (_scatter_copies(n)(refs[:2 * n], send, recv)
                + _direct_copies(m)(refs[2 * n:], _Shifted(send, 3 * n), _Shifted(recv, 3 * n)))
    return copies


def scatter_start(parts, partials):
    n, m = len(parts), len(partials)
    lands = [lax.empty((3,) + p.shape[1:], p.dtype) for p in parts]
    direct_lands = [lax.empty((7,) + p.shape[2:], p.dtype) for p in partials]
    return split_start("scatter_start_rest", list(parts) + lands + list(partials) + direct_lands, 3 * n + 7 * m,
                       _scatter_and_direct(n, m))


def scatter_wait(state, after, n, m):
    res = split_wait("scatter_wait_rest", state, after, _scatter_and_direct(n, m))
    return res[n:2 * n], res[2 * n:2 * n + m], res[2 * n + m:]


def _gather_copies(shapes, level, to_both_cores=()):
    n = len(shapes)

    def copies(refs, send, recv):
        x, y, c, chips = _place()
        out = []
        for t in list(range(n)) + list(to_both_cores):
            rh = shapes[t][0] // 2
            to_core = c if len(out) < 3 * n else 1 - c
            for cx, cy in chips:
                k, dev = (2 * x + y, (cx, cy, to_core)) if level == 1 else (2 * cx + cy, (x, y, 1 - c))
                blk = refs[t].at[k, pl.ds(c * rh, rh), :]
                out.append(_rcopy(blk, blk, send.at[len(out)], recv.at[len(out)], dev))
        return out
    return copies


def _sibling_copies(n, other_half):
    def copies(refs, send, recv):
        x, y, c, _ = _place()
        return [_rcopy(refs[t].at[:, 1 - c] if other_half else refs[t], refs[n + t], send.at[t], recv.at[t],
                       (x, y, 1 - c)) for t in range(n)]
    return copies


def sibling_start(srcs, other_half, tag):
    lands = [lax.empty((a.shape[0],) + a.shape[2:] if other_half else a.shape, a.dtype) for a in srcs]
    return split_start("sibling_start_" + tag, list(srcs) + lands, len(srcs),
                       _sibling_copies(len(srcs), other_half))


def sibling_wait(state, after, other_half, tag):
    n = (len(state) - 2) // 2
    res = split_wait("sibling_wait_" + tag, state, after, _sibling_copies(n, other_half))
    return res[:n], res[n:]


def small_pack(ddw, v512, v1024, loss_parts):
    rows, width = PACK_ROWS, 512
    n512, n1024 = len(VEC512), len(VEC1024)

    def body(*refs):
        ddw_ref = refs[0]
        a_refs = refs[1:1 + n512]
        b_refs = refs[1 + n512:1 + n512 + n1024]
        lp_ref, o_ref, p_ref = refs[1 + n512 + n1024:]
        p_ref[...] = jnp.zeros_like(p_ref)
        p_ref[0:32, :] = ddw_ref[...]
        p_ref[LOSS_ROW:LOSS_ROW + 1, 0:128] = jnp.sum(lp_ref[...], axis=0, keepdims=True) * 0.125
        for i, r in enumerate(a_refs):
            p_ref[32 + i:33 + i, :] = r[...]
        for i, r in enumerate(b_refs):
            base = 32 + n512 + 2 * i
            p_ref[base:base + 1, :] = r[:, 0:512]
            p_ref[base + 1:base + 2, :] = r[:, 512:1024]
        x, y, c, _ = _place()
        o_ref[4 * x + 2 * y + c] = p_ref[...]

    n_in = 2 + n512 + n1024
    return pl.pallas_call(
        body, name="small_pack", out_shape=jax.ShapeDtypeStruct((8, rows, width), F32),
        in_specs=[VMEM_SPEC] * n_in, out_specs=VMEM_SPEC,
        scratch_shapes=[pltpu.VMEM((rows, width), F32)],
    )(ddw, *[v512[n] for n in VEC512], *[v1024[n] for n in VEC1024], loss_parts)


def _small_copies(refs, send, recv):
    x, y, c, _ = _place()
    mine = refs[0].at[4 * x + 2 * y + c]
    peers = [(1 - x if k & 4 else x, 1 - y if k & 2 else y, 1 - c if k & 1 else c) for k in range(1, 8)]
    return [_rcopy(mine, mine, send.at[i], recv.at[i], dev) for i, dev in enumerate(peers)]


def _row_block(r):
    for tr in (512, 352, 256, 128):
        if r % tr == 0:
            return tr
    return r


def add_halves(g, recv, name):
    _, _, r, w = g.shape
    tr = _row_block(r)

    def body(g_ref, r_ref, ob_ref, own_ref):
        k = pl.program_id(1)
        me = 2 * lax.axis_index("x") + lax.axis_index("y")
        t = g_ref[0, 0].astype(F32) + r_ref[0].astype(F32)
        ob_ref[0] = t.astype(MM)
        mine = jnp.where(k == me, t, 0.0)

        @pl.when(k == 0)
        def _():
            own_ref[...] = mine

        @pl.when(k != 0)
        def _():
            own_ref[...] += mine

    return pl.pallas_call(
        body, name=name, grid=(r // tr, N_CHIPS),
        in_specs=[pl.BlockSpec((1, 1, tr, w), lambda i, k: (k, lax.axis_index("c"), i, 0)),
                  pl.BlockSpec((1, tr, w), lambda i, k: (k, i, 0))],
        out_specs=[pl.BlockSpec((1, tr, w), lambda i, k: (k, i, 0)),
                   pl.BlockSpec((tr, w), lambda i, k: (i, 0))],
        out_shape=(jax.ShapeDtypeStruct((N_CHIPS, r, w), MM), jax.ShapeDtypeStruct((r, w), F32)),
        compiler_params=_params(("arbitrary", "arbitrary")),
    )(g, recv)


def sum_parts(own, rin, after, name):
    _, r, w = rin.shape
    tr = _row_block(r)

    def body(o_ref, r_ref, after_ref, out_ref):
        out_ref[...] = ((o_ref[...] + r_ref[0].astype(F32)) + r_ref[1].astype(F32)) + r_ref[2].astype(F32)

    return pl.pallas_call(
        body, name=name, grid=(r // tr,), out_shape=jax.ShapeDtypeStruct((r, w), F32),
        in_specs=[pl.BlockSpec((tr, w), lambda i: (i, 0)), pl.BlockSpec((3, tr, w), lambda i: (0, i, 0)),
                  TOKEN_SPEC],
        out_specs=pl.BlockSpec((tr, w), lambda i: (i, 0)),
        compiler_params=_params(("arbitrary",)),
    )(own, rin, after)


def sum_partials(p, land, after, name):
    _, _, r, w = p.shape
    tr = _row_block(r)

    def body(p_ref, l_ref, after_ref, out_ref):
        total = p_ref[0, 0].astype(F32)
        for slot in (6, 0, 3, 1, 4, 2, 5):
            total = total + l_ref[slot].astype(F32)
        out_ref[...] = total

    def own(i):
        return 2 * lax.axis_index("x") + lax.axis_index("y"), lax.axis_index("c"), i, 0

    return pl.pallas_call(
        body, name=name, grid=(r // tr,), out_shape=jax.ShapeDtypeStruct((r, w), F32),
        in_specs=[pl.BlockSpec((1, 1, tr, w), own), pl.BlockSpec((7, tr, w), lambda i: (0, i, 0)), TOKEN_SPEC],
        out_specs=pl.BlockSpec((tr, w), lambda i: (i, 0)),
        compiler_params=_params(("arbitrary",)),
    )(p, land, after)


def _adamw_math(w, g, m, v):
    mn = ADAM_B1 * m + (1.0 - ADAM_B1) * g
    vn = ADAM_B2 * v + (1.0 - ADAM_B2) * (g * g)
    m_hat = mn / (1.0 - ADAM_B1 ** ADAM_STEP)
    v_hat = vn / (1.0 - ADAM_B2 ** ADAM_STEP)
    return -ADAM_LR * (m_hat / (jnp.sqrt(v_hat) + ADAM_EPS) + ADAM_WD * w), mn, vn


def adamw(w, mine, other, m, v, name):
    r, c = w.shape
    rh = r // 2
    tr = _row_block(rh)
    if c >= 1024 and tr % 512 == 0:
        tr = 256
    nb = rh // tr

    def body(w_ref, a_ref, b_ref, m_ref, v_ref, go_ref, d_ref, mo_ref, vo_ref):
        gv = jnp.where(lax.axis_index("c") == pl.program_id(0), a_ref[...], b_ref[...])
        go_ref[...] = gv
        d_ref[...], mo_ref[...], vo_ref[...] = _adamw_math(w_ref[...], gv, m_ref[...], v_ref[...])

    def half(of_sibling):
        def index(h, i):
            owner = lax.axis_index("c")
            owner = 1 - owner if of_sibling else owner
            return jnp.where(h == owner, i, jnp.where(h < owner, 0, nb - 1)), 0
        return pl.BlockSpec((tr, c), index)

    spec = pl.BlockSpec((tr, c), lambda h, i: (h * nb + i, 0))
    out = jax.ShapeDtypeStruct((r, c), F32)
    return pl.pallas_call(
        body, name=name, grid=(2, nb), out_shape=(out, out, out, out),
        in_specs=[spec, half(False), half(True), spec, spec], out_specs=[spec] * 4,
        compiler_params=_params(("arbitrary", "arbitrary")),
    )(w, mine, other, m, v)


def adamw_small(packs, params, after):
    names = list(params)
    flat = [a for n in names for a in params[n]]

    def body(*refs):
        p_ref = refs[0]
        ins = refs[1:1 + 3 * len(names)]
        loss_ref, g_ref = refs[2 + 3 * len(names):4 + 3 * len(names)]
        outs = refs[4 + 3 * len(names):]
        total = p_ref[0]
        for d in range(1, 8):
            total = total + p_ref[d]
        g_ref[...] = total
        loss_ref[...] = g_ref[LOSS_ROW:LOSS_ROW + 1, 0:1]
        me = 2 * lax.axis_index("x") + lax.axis_index("y")
        for i, n in enumerate(names):
            w_ref, m_ref, v_ref = ins[3 * i:3 * i + 3]
            go_ref, d_ref, mo_ref, vo_ref = outs[4 * i:4 * i + 4]
            if n == "conv_dw_w":
                gv = jnp.zeros((CONV_WIDTH, 128), F32)
                for k in range(N_CHIPS):
                    gv = gv + jnp.where(me == k, g_ref[0:CONV_WIDTH, 128 * k:128 * (k + 1)], 0.0)
            elif n in VEC512:
                r0 = 32 + VEC512.index(n)
                gv = g_ref[r0:r0 + 1, :]
            else:
                r0 = 32 + len(VEC512) + 2 * VEC1024.index(n)
                gv = jnp.concatenate([g_ref[r0:r0 + 1, :], g_ref[r0 + 1:r0 + 2, :]], axis=1)
            go_ref[...] = gv
            d_ref[...], mo_ref[...], vo_ref[...] = _adamw_math(w_ref[...], gv, m_ref[...], v_ref[...])

    out_shape = [jax.ShapeDtypeStruct((1, 1), F32), jax.ShapeDtypeStruct(packs.shape[1:], F32)]
    out_shape += [jax.ShapeDtypeStruct(params[n][0].shape, F32) for n in names for _ in range(4)]
    res = pl.pallas_call(
        body, name="adamw_small", out_shape=out_shape,
        in_specs=[VMEM_SPEC] * (2 + len(flat)), out_specs=[VMEM_SPEC] * len(out_shape),
        compiler_params=_params(),
    )(packs, *flat, after)
    return res[0], res[1], {n: res[2 + 4 * i:6 + 4 * i] for i, n in enumerate(names)}


REST = ("w_ffn_up", "w_ffn_down", "w_out", "w_conv_branch", "w_att_branch")
VEC512 = ("conv_dw_b", "conv_ln_g", "conv_ln_b")
VEC1024 = ("norm_mix_pre", "b_conv_branch", "norm_mix_post", "norm_ffn_pre", "norm_ffn_post")
PACK_ROWS = 48
LOSS_ROW = 47


def kernel(x, norm_mix_pre, w_in, conv_dw_w, conv_dw_b, conv_ln_g, conv_ln_b, w_conv_branch, b_conv_branch, w_att_branch, w_out, norm_mix_post, norm_ffn_pre, w_ffn_up, w_ffn_down, norm_ffn_post, loss_target, m_norm_mix_pre, m_w_in, m_conv_dw_w, m_conv_dw_b, m_conv_ln_g, m_conv_ln_b, m_w_conv_branch, m_b_conv_branch, m_w_att_branch, m_w_out, m_norm_mix_post, m_norm_ffn_pre, m_w_ffn_up, m_w_ffn_down, m_norm_ffn_post, v_norm_mix_pre, v_w_in, v_conv_dw_w, v_conv_dw_b, v_conv_ln_g, v_conv_ln_b, v_w_conv_branch, v_b_conv_branch, v_w_att_branch, v_w_out, v_norm_mix_post, v_norm_ffn_pre, v_w_ffn_up, v_w_ffn_down, v_norm_ffn_post):
    weights = dict(norm_mix_pre=norm_mix_pre, w_in=w_in, conv_dw_w=conv_dw_w, conv_dw_b=conv_dw_b, conv_ln_g=conv_ln_g, conv_ln_b=conv_ln_b, w_conv_branch=w_conv_branch, b_conv_branch=b_conv_branch, w_att_branch=w_att_branch, w_out=w_out, norm_mix_post=norm_mix_post, norm_ffn_pre=norm_ffn_pre, w_ffn_up=w_ffn_up, w_ffn_down=w_ffn_down, norm_ffn_post=norm_ffn_post)
    mom = dict(norm_mix_pre=m_norm_mix_pre, w_in=m_w_in, conv_dw_w=m_conv_dw_w, conv_dw_b=m_conv_dw_b, conv_ln_g=m_conv_ln_g, conv_ln_b=m_conv_ln_b, w_conv_branch=m_w_conv_branch, b_conv_branch=m_b_conv_branch, w_att_branch=m_w_att_branch, w_out=m_w_out, norm_mix_post=m_norm_mix_post, norm_ffn_pre=m_norm_ffn_pre, w_ffn_up=m_w_ffn_up, w_ffn_down=m_w_ffn_down, norm_ffn_post=m_norm_ffn_post)
    var = dict(norm_mix_pre=v_norm_mix_pre, w_in=v_w_in, conv_dw_w=v_conv_dw_w, conv_dw_b=v_conv_dw_b, conv_ln_g=v_conv_ln_g, conv_ln_b=v_conv_ln_b, w_conv_branch=v_w_conv_branch, b_conv_branch=v_b_conv_branch, w_att_branch=v_w_att_branch, w_out=v_w_out, norm_mix_post=v_norm_mix_post, norm_ffn_pre=v_norm_ffn_pre, w_ffn_up=v_w_ffn_up, w_ffn_down=v_w_ffn_down, norm_ffn_post=v_norm_ffn_post)
    order = list(weights)
    grads, deltas, new_m, new_v = {}, {}, {}, {}
    xs = x.reshape(SEQ, D_MODEL)
    tgt = loss_target.reshape(SEQ, D_MODEL)
    row = lambda a: a.reshape(1, -1)
    g1, g2, g3, g4 = (row(weights[n]) for n in ("norm_mix_pre", "norm_mix_post", "norm_ffn_pre", "norm_ffn_post"))
    ln_g, ln_b = row(conv_ln_g), row(conv_ln_b)

    summed, from_chips = {}, {}

    def core_sums(names, state, after, tag):
        own, from_sibling = sibling_wait(state, after, True, tag)
        for n, g, r in zip(names, own, from_sibling):
            summed[n] = add_halves(g, r, "add_" + n)

    def chip_sums(names, after):
        return [sum_parts(summed[n][1], from_chips[n], after, "sum_" + n) for n in names]

    def optimize(names, state, after, tag):
        mine, other = sibling_wait(state, after, False, tag)
        for n, a, b in zip(names, mine, other):
            grads[n], deltas[n], new_m[n], new_v[n] = adamw(weights[n], a, b, mom[n], var[n], "adamw_" + n)

    w_in_g, dw_g, *rest = all_gather_weights([w_in], conv_dw_w, [weights[n] for n in REST])
    w_dw_full = jnp.concatenate([dw_g[k] for k in range(N_CHIPS)], axis=1)
    rest_shapes = [weights[n].shape for n in REST]
    over_ici = _gather_copies(rest_shapes, 1, to_both_cores=(2, 3, 4))
    state, token = split_start("gather_start", rest, 3 * (len(REST) + 3), over_ici)
    h1, ci, q, k, v, gc, ga = in_proj_fwd(xs, g1, w_in_g, token)
    u1, u3 = conv_fwd(ci, w_dw_full, row(conv_dw_b), ln_g, ln_b)
    att, rc = attn_fwd(q, k, v)
    rest = split_wait("gather_wait", state, [att], over_ici)
    w_out_g, w_cb_g, w_ab_g = rest[2:]
    w_out_g = w_out_g.reshape(D_MODEL, D_MODEL)
    to_sibling = _gather_copies(rest_shapes[:2], 2)
    state, token = split_start("pass_start", rest[:2], 3 * 2, to_sibling)
    merged, mix, x2, h2 = mix_fwd(u3, att, gc, ga, xs, w_cb_g, row(b_conv_branch), w_ab_g, w_out_g, g2, g3, token)
    w_up_g, w_down_g = split_wait("pass_wait", state, [h2], to_sibling)
    w_down_g = w_down_g.reshape(D_FF, D_MODEL)
    gate, up, act = ffn_up_fwd(h2, w_up_g)
    dff, dy, loss_parts, dg4, dgu = ffn_down_loss(act, w_down_g, x2, tgt, g4, gate, up)

    dx2, dmix, dg3, dg2 = ffn_in_bwd(dgu, w_up_g, x2, mix, dy, g3, g2)
    ffn_grads = [weight_grad(h2, dgu, "dw_ffn_up", True), weight_grad(act, dff, "dw_ffn_down", False, tk=UP_SHARD)]
    to_ffn, token = sibling_start(ffn_grads, True, "dw_ffn")
    dco, dao, dg, du3, datt, dbcb = merge_bwd(dmix, w_out_g, gc, ga, u3, att, w_cb_g, row(b_conv_branch), w_ab_g,
                                              token)
    mix_grads = weight_grad_mix(merged, dmix, u3, dco, att, dao)
    core_sums(REST[:2], to_ffn, [mix_grads[0]], "dw_ffn")
    state, token = scatter_start([summed[n][0] for n in REST[:2]], mix_grads)
    dci, ddw, dbdw, dlng, dlnb = conv_bwd(du3, u1, ci, w_dw_full, ln_g, ln_b, token)
    dqkv = attn_bwd(q, k, v, datt, rc, token)
    ffn_from_chips, mix_grads, mix_from_all = scatter_wait(state, [dci, dqkv], 2, len(mix_grads))
    from_chips.update(zip(REST[:2], ffn_from_chips))
    dproj = (dci, dqkv, dg)
    to_in, token = sibling_start([weight_grad_in(h1, dproj)], True, "dw_in")
    grad_x, dg1 = in_proj_bwd(dproj, w_in_g, xs, dx2, g1, token)
    v512 = dict(conv_dw_b=dbdw, conv_ln_g=dlng, conv_ln_b=dlnb)
    v1024 = dict(norm_mix_pre=dg1, b_conv_branch=dbcb, norm_mix_post=dg2, norm_ffn_pre=dg3, norm_ffn_post=dg4)
    packs = small_pack(ddw, v512, v1024, loss_parts)
    core_sums(("w_in",), to_in, [packs], "dw_in")
    to_chips = summed["w_in"][0]
    landing = lax.empty((3,) + to_chips.shape[1:], to_chips.dtype)

    def scatter_and_packs(refs, send, recv):
        return (_scatter_copies(1)(refs[:2], send, recv)
                + _small_copies(refs[2:], _Shifted(send, 3), _Shifted(recv, 3)))

    state, token = split_start("scatter_start_w_in", [to_chips, landing, packs], 3 + 7, scatter_and_packs)
    swap_up, token = sibling_start(chip_sums(REST[:1], token), False, "sum_ffn_up")
    rest_sums = chip_sums(REST[1:2], token) + [sum_partials(p, r, token, "sum_" + n)
                                               for n, p, r in zip(REST[2:], mix_grads, mix_from_all)]
    swap_rest, token = sibling_start(rest_sums, False, "sum_rest")
    optimize(REST[:1], swap_up, [token], "sum_ffn_up")
    optimize(REST[1:], swap_rest, [new_v["w_ffn_up"]], "sum_rest")
    _, from_chips["w_in"], packs = split_wait("scatter_wait_w_in", state, [new_v[n] for n in REST], scatter_and_packs)
    swap_in, token = sibling_start(chip_sums(("w_in",), token), False, "sum_w_in")
    as_rows = lambda n, a: a if n == "conv_dw_w" else a.reshape(1, -1)
    small_names = ("conv_dw_w",) + VEC512 + VEC1024
    loss, gsum, small = adamw_small(
        packs, {n: tuple(as_rows(n, d[n]) for d in (weights, mom, var)) for n in small_names}, token)
    optimize(("w_in",), swap_in, [gsum], "sum_w_in")
    for n in small_names:
        grads[n], deltas[n], new_m[n], new_v[n] = (a.reshape(weights[n].shape) for a in small[n])

    return (loss.reshape(()), grad_x.reshape(1, SEQ, D_MODEL),*[grads[n] for n in order], *[deltas[n] for n in order],
            *[new_m[n] for n in order], *[new_v[n] for n in order])
```

```python
import jax
import jax.numpy as jnp
from jax import lax
from jax.experimental import pallas as pl
from jax.experimental.pallas import tpu as pltpu

F32 = jnp.float32
MM = jnp.bfloat16

SEQ = 2048
D_MODEL = 1024
CONV_DIM = 512
ATT_DIM = 512
CONV_WIDTH = 31
D_FF = 2816
IN_COLS = 2 * CONV_DIM + 3 * ATT_DIM + 2 * D_MODEL
N_CHIPS = 4
IN_SHARD = IN_COLS // N_CHIPS
UP_SHARD = 2 * D_FF // N_CHIPS
BR_SHARD = D_MODEL // N_CHIPS
EPS = 1e-6
ATT_SCALE = 0.125

TM = 256
GLU_ROWS = 256
TQ = 128
CONV_TILE = 64
CONV_WIN = CONV_TILE + 32
VMEM_LIMIT = 56 * 1024 * 1024

ADAM_LR = 0.001
ADAM_B1 = 0.9
ADAM_B2 = 0.999
ADAM_EPS = 1e-08
ADAM_WD = 0.01
ADAM_STEP = 10

MESH = pl.DeviceIdType.MESH
ANY = pl.BlockSpec(memory_space=pl.ANY)
VMEM_SPEC = pl.BlockSpec(memory_space=pltpu.VMEM)

NT_DIMS = (((1,), (1,)), ((), ()))
TN_DIMS = (((0,), (0,)), ((), ()))

IN_PIECES = (("ci", 0, 1024), ("q", 1024, 1536), ("k", 1536, 2048), ("v", 2048, 2560),
             ("gc", 2560, 3584), ("ga", 3584, 4608))


def _params(sem=None, vmem=VMEM_LIMIT):
    return pltpu.CompilerParams(dimension_semantics=sem, vmem_limit_bytes=vmem)


def _dot(a, b):
    return jnp.dot(a, b, preferred_element_type=F32)


def _dot_nt(a, b):
    return lax.dot_general(a, b, NT_DIMS, preferred_element_type=F32)


def _dot_tn(a, b):
    return lax.dot_general(a, b, TN_DIMS, preferred_element_type=F32)


def _sigmoid(x):
    return 1.0 / (1.0 + jnp.exp(-x))


def _rms(x):
    r = lax.rsqrt(jnp.mean(x * x, axis=-1, keepdims=True) + EPS)
    return x * r, r


def _rms_bwd(dy_g, n, r):
    return r * (dy_g - n * jnp.mean(dy_g * n, axis=-1, keepdims=True))


def _row_tile_spec(width, tm=TM):
    return pl.BlockSpec((tm, width), lambda i: (i, 0))


def _full_spec(shape):
    nd = len(shape)
    return pl.BlockSpec(shape, lambda *_: (0,) * nd)


def _weight_spec(shape):
    nd = len(shape)
    return pl.BlockSpec(shape, lambda *_: (0,) * nd, pipeline_mode=pl.Buffered(1))


def _acc_rows(ref, val, first):
    @pl.when(first)
    def _():
        ref[...] = val

    @pl.when(jnp.logical_not(first))
    def _():
        ref[...] += val


TOKEN_SPEC = pl.BlockSpec((8, 128), lambda *_: (0, 0))


def in_proj_fwd(x, g1, w_in_g, after):
    def body(x_ref, g_ref, w_ref, after_ref, h_ref, ci_ref, q_ref, k_ref, v_ref, gc_ref, ga_ref):
        n, _ = _rms(x_ref[...])
        h = (n * g_ref[...]).astype(MM)
        h_ref[...] = h
        outs = dict(ci=ci_ref, q=q_ref, k=k_ref, v=v_ref, gc=gc_ref, ga=ga_ref)
        for j in range(N_CHIPS):
            p = _dot(h, w_ref[j])
            g0 = j * IN_SHARD
            for name, s, e in IN_PIECES:
                lo, hi = max(s, g0), min(e, g0 + IN_SHARD)
                if lo < hi:
                    ref = outs[name]
                    part = p[:, lo - g0:hi - g0]
                    if name == "q":
                        part = part * ATT_SCALE
                    ref[:, lo - s:hi - s] = part.astype(ref.dtype)

    out_shape = [
        jax.ShapeDtypeStruct((SEQ, D_MODEL), MM),
        jax.ShapeDtypeStruct((SEQ, 2 * CONV_DIM), F32),
        jax.ShapeDtypeStruct((SEQ, ATT_DIM), MM),
        jax.ShapeDtypeStruct((SEQ, ATT_DIM), MM),
        jax.ShapeDtypeStruct((SEQ, ATT_DIM), MM),
        jax.ShapeDtypeStruct((SEQ, D_MODEL), F32),
        jax.ShapeDtypeStruct((SEQ, D_MODEL), F32),
    ]
    return pl.pallas_call(
        body, name="in_proj_fwd", grid=(SEQ // TM,), out_shape=out_shape,
        in_specs=[_row_tile_spec(D_MODEL), _full_spec((1, D_MODEL)), _weight_spec(w_in_g.shape), TOKEN_SPEC],
        out_specs=[_row_tile_spec(s.shape[1]) for s in out_shape],
        compiler_params=_params(("arbitrary",)),
    )(x, g1, w_in_g, after)


LANE_GROUPS = [slice(g, g + 128) for g in range(0, CONV_DIM, 128)]
NORM_ROWS = 16


def _shifted_windows(src_ref, t0, cols, offsets):
    win = src_ref[pl.ds(t0, CONV_WIN), cols]
    for rot in range(8):
        ms = [m for m in offsets if m % 8 == rot]
        if ms:
            shifted = win if rot == 0 else pltpu.roll(win, CONV_WIN - rot, 0)
            for m in ms:
                yield m, shifted[m - rot:m - rot + CONV_TILE, :]


def _shifted_sum(src_ref, t0, cols, w_ref, offset_of_tap):
    tap_at = {offset_of_tap(j): j for j in range(CONV_WIDTH)}
    acc = None
    for m, rows in _shifted_windows(src_ref, t0, cols, sorted(tap_at)):
        t = w_ref[tap_at[m]:tap_at[m] + 1, cols] * rows
        acc = t if acc is None else acc + t
    return acc


def _fetch(srcs, dsts, sems):
    copies = [pltpu.make_async_copy(s, d, sems.at[i]) for i, (s, d) in enumerate(zip(srcs, dsts))]
    for cp in copies:
        cp.start()
    return copies


def _row_chunks(src, dst, sems):
    def chunk(i):
        t0 = i * GLU_ROWS
        rows = pl.ds(t0 if isinstance(i, int) else pl.multiple_of(t0, GLU_ROWS), GLU_ROWS)
        return pltpu.make_async_copy(src.at[rows, :], dst.at[rows, :], sems.at[i])

    for i in range(SEQ // GLU_ROWS):
        chunk(i).start()
    return chunk


def _glu_into(ci_chunk, ci_ref, upad_ref):
    upad_ref[0:32, :] = jnp.zeros((32, CONV_DIM), F32)

    def step(i, c):
        ci_chunk(i).wait()
        t0 = pl.multiple_of(i * GLU_ROWS, GLU_ROWS)
        a = ci_ref[pl.ds(t0, GLU_ROWS), 0:CONV_DIM]
        b = ci_ref[pl.ds(t0, GLU_ROWS), CONV_DIM:2 * CONV_DIM]
        upad_ref[pl.ds(t0 + 32, GLU_ROWS), :] = a * _sigmoid(b)
        return c

    lax.fori_loop(0, SEQ // GLU_ROWS, step, 0)


def _layernorm_parts(u1):
    mu = jnp.mean(u1, axis=-1, keepdims=True)
    xc = u1 - mu
    rstd = lax.rsqrt(jnp.mean(xc * xc, axis=-1, keepdims=True) + EPS)
    return xc * rstd, rstd


def conv_fwd(ci, w_dw, b_dw, ln_g, ln_b):
    def body(ci_hbm, w_ref, b_ref, g_ref, bb_ref, u1_ref, u3_ref, upad_ref, ci_ref, sems):
        _glu_into(_row_chunks(ci_hbm, ci_ref, sems), ci_ref, upad_ref)

        def step(i, c):
            t0 = pl.multiple_of(i * CONV_TILE, CONV_TILE)
            for cols in LANE_GROUPS:
                u1_ref[pl.ds(t0, CONV_TILE), cols] = (_shifted_sum(upad_ref, t0, cols, w_ref, lambda j: j + 2)
                                                      + b_ref[:, cols])
            for r in range(0, CONV_TILE, NORM_ROWS):
                rows = pl.ds(t0 + r, NORM_ROWS)
                xh, _ = _layernorm_parts(u1_ref[rows, :])
                u2 = xh * g_ref[...] + bb_ref[...]
                u3_ref[rows, :] = (u2 * _sigmoid(u2)).astype(MM)
            return c

        lax.fori_loop(0, SEQ // CONV_TILE, step, 0)

    return pl.pallas_call(
        body, name="conv_fwd",
        out_shape=[jax.ShapeDtypeStruct((SEQ, CONV_DIM), F32), jax.ShapeDtypeStruct((SEQ, CONV_DIM), MM)],
        in_specs=[ANY] + [VMEM_SPEC] * 4, out_specs=[VMEM_SPEC] * 2,
        scratch_shapes=[pltpu.VMEM((SEQ + 32, CONV_DIM), F32), pltpu.VMEM(ci.shape, ci.dtype),
                        pltpu.SemaphoreType.DMA((SEQ // GLU_ROWS,))],
        compiler_params=_params(),
    )(ci, w_dw, b_dw, ln_g, ln_b)


def _softplus(z):
    return jnp.maximum(z, 0.0) + jnp.log(1.0 + jnp.exp(-jnp.abs(z)))


def _cumsum_weights(suffix, with_total):
    n = 256 if with_total else 128
    r = lax.broadcasted_iota(jnp.int32, (128, n), 0)
    c = lax.broadcasted_iota(jnp.int32, (128, n), 1)
    tri = (r >= c) if suffix else (r <= c)
    return jnp.logical_or(tri, c >= 128).astype(MM)


NO_SCORE = -1e30
N_KB = SEQ // TQ


def _score_bias(lane, row, i, j):
    keep = jnp.logical_and(i >= 0, jnp.logical_or(j < i, lane < row))
    return jnp.where(keep, 0.0, NO_SCORE)


def _block_pipeline(n_stages, descending, step, on_query_block=None):
    n_lag = n_stages - 1
    none = jnp.int32(-1)

    def shift(cur, lag):
        step([cur] + [(lag[2 * s], lag[2 * s + 1]) for s in range(n_lag)])
        return (cur[0], cur[1]) + tuple(lag[:-2])

    def outer(i, lag):
        if on_query_block is not None:
            on_query_block(i)

        def inner(n, lag):
            return shift((i, i - n if descending else n), lag)
        return lax.fori_loop(0, i + 1, inner, lag)

    lag = lax.fori_loop(0, N_KB, outer, (none,) * (2 * n_lag))
    lax.fori_loop(0, n_lag, lambda n, lag: shift((none, none), lag), lag)


def _head_masks():
    lane = lax.broadcasted_iota(jnp.int32, (TQ, 128), 1)
    row = lax.broadcasted_iota(jnp.int32, (TQ, 128), 0)
    return lane, row, lane < 64


def _pick_head(x, head0, h):
    zero = jnp.zeros_like(x)
    return jnp.where(head0, x, zero) if h == 0 else jnp.where(head0, zero, x)


N_PAIRS = ATT_DIM // 128


def _split_heads(src_ref, dst_ref):
    _, _, head0 = _head_masks()

    def block(b, c):
        r0 = pl.multiple_of(b * TQ, TQ)
        d0 = pl.multiple_of(b * 2 * TQ, 2 * TQ)
        for p in range(N_PAIRS):
            x = src_ref[pl.ds(r0, TQ), 128 * p:128 * (p + 1)]
            for h in range(2):
                dst_ref[p, pl.ds(d0 + TQ * h, TQ), :] = _pick_head(x, head0, h)
        return c

    lax.fori_loop(0, N_KB, block, 0)


def attn_fwd(q, k, v):
    def body(q_hbm, k_hbm, v_hbm, o_ref, rc_ref, acc_ref, r_ref, z_ref, spb_ref, ab_ref, qm_ref, vm_ref,
             q_ref, k_ref, v_ref, sems):
        arrive = _fetch((q_hbm, v_hbm, k_hbm), (q_ref, v_ref, k_ref), sems)
        lane, row, _ = _head_masks()
        w = _cumsum_weights(suffix=True, with_total=True)
        acc_ref[...] = jnp.zeros_like(acc_ref)
        r_ref[...] = jnp.zeros_like(r_ref)
        rc_ref[...] = jnp.zeros_like(rc_ref)
        z_ref[...] = jnp.full(z_ref.shape, NO_SCORE, F32)
        spb_ref[...] = jnp.zeros_like(spb_ref)
        ab_ref[...] = jnp.zeros_like(ab_ref)
        arrive[0].wait()
        _split_heads(q_ref, qm_ref)
        arrive[1].wait()
        _split_heads(v_ref, vm_ref)
        arrive[2].wait()

        def step(pairs):
            (i1, j1), (i2, j2), (i3, j3) = pairs
            k1, q2, q3 = (pl.multiple_of(jnp.maximum(b, 0) * TQ, TQ) for b in (j1, i2, i3))
            q1, k3 = (pl.multiple_of(jnp.maximum(b, 0) * 2 * TQ, 2 * TQ) for b in (i1, j3))
            bias1 = _score_bias(lane, row, i1, j1)
            first2 = j2 == i2
            rc_rows = rc_ref[pl.ds(q2, TQ), :]
            for p in range(N_PAIRS):
                cols = slice(128 * p, 128 * (p + 1))
                kb = k_ref[pl.ds(k1, TQ), cols]
                acc_ref[pl.ds(q3, TQ), cols] += _dot(ab_ref[p], vm_ref[p, pl.ds(k3, 2 * TQ), :])
                for h in range(2):
                    hh = 2 * p + h
                    r = _dot(spb_ref[hh], w)
                    r_in = jnp.where(first2, 0.0, r_ref[hh])
                    ab_ref[p, :, 128 * h:128 * (h + 1)] = jnp.exp(z_ref[hh] - (r[:, :128] + r_in)).astype(MM)
                    rc_rows = jnp.where(jnp.logical_and(lane == 16 * hh + j2, i2 >= 0), r_in, rc_rows)
                    r_ref[hh] = r_in + r[:, 128:]
                    z = _dot_nt(qm_ref[p, pl.ds(q1 + TQ * h, TQ), :], kb) + bias1
                    z_ref[hh] = z
                    spb_ref[hh] = _softplus(z).astype(MM)
            rc_ref[pl.ds(q2, TQ), :] = rc_rows

        _block_pipeline(3, True, step)
        o_ref[...] = acc_ref[...].astype(MM)

    return pl.pallas_call(
        body, name="attn_fwd",
        out_shape=[jax.ShapeDtypeStruct((SEQ, ATT_DIM), MM), jax.ShapeDtypeStruct((SEQ, 128), F32)],
        in_specs=[ANY] * 3, out_specs=[VMEM_SPEC] * 2,
        scratch_shapes=[pltpu.VMEM((SEQ, ATT_DIM), F32), pltpu.VMEM((8, TQ, 128), F32),
                        pltpu.VMEM((8, TQ, 128), F32), pltpu.VMEM((8, TQ, 128), MM),
                        pltpu.VMEM((N_PAIRS, TQ, 256), MM), pltpu.VMEM((N_PAIRS, 2 * SEQ, 128), MM),
                        pltpu.VMEM((N_PAIRS, 2 * SEQ, 128), MM)]
                       + [pltpu.VMEM(a.shape, a.dtype) for a in (q, k, v)] + [pltpu.SemaphoreType.DMA((3,))],
        compiler_params=_params(),
    )(q, k, v)


def _branch_outputs(u_ref, a_ref, wcb_ref, bcb_ref, wab_ref):
    u = u_ref[...]
    a = a_ref[...]
    co = jnp.concatenate([_dot(u, wcb_ref[j]) for j in range(N_CHIPS)], axis=1) + bcb_ref[...]
    ao = jnp.concatenate([_dot(a, wab_ref[j]) for j in range(N_CHIPS)], axis=1)
    return co, ao


def mix_fwd(u3, att, gc, ga, x, w_cb_g, b_cb, w_ab_g, w_out_g, g2, g3, after):
    def body(u_ref, a_ref, gc_ref, ga_ref, x_ref, wcb_ref, bcb_ref, wab_ref, wout_ref, g2_ref, g3_ref, after_ref,
             mg_ref, mix_ref, x2_ref, h2_ref):
        co, ao = _branch_outputs(u_ref, a_ref, wcb_ref, bcb_ref, wab_ref)
        merged = (_sigmoid(gc_ref[...]) * co + _sigmoid(ga_ref[...]) * ao).astype(MM)
        mg_ref[...] = merged
        mix = _dot(merged, wout_ref[...])
        mix_ref[...] = mix
        n2, _ = _rms(mix)
        x2 = x_ref[...] + n2 * g2_ref[...]
        x2_ref[...] = x2
        n3, _ = _rms(x2)
        h2_ref[...] = (n3 * g3_ref[...]).astype(MM)

    out_shape = [
        jax.ShapeDtypeStruct((SEQ, D_MODEL), MM), jax.ShapeDtypeStruct((SEQ, D_MODEL), F32),
        jax.ShapeDtypeStruct((SEQ, D_MODEL), F32), jax.ShapeDtypeStruct((SEQ, D_MODEL), MM),
    ]
    vec = _full_spec((1, D_MODEL))
    return pl.pallas_call(
        body, name="mix_fwd", grid=(SEQ // TM,), out_shape=out_shape,
        in_specs=[_row_tile_spec(CONV_DIM), _row_tile_spec(ATT_DIM), _row_tile_spec(D_MODEL),
                  _row_tile_spec(D_MODEL), _row_tile_spec(D_MODEL), _weight_spec(w_cb_g.shape), vec,
                  _weight_spec(w_ab_g.shape), _weight_spec(w_out_g.shape), vec, vec, TOKEN_SPEC],
        out_specs=[_row_tile_spec(D_MODEL)] * 4,
        compiler_params=_params(("arbitrary",)),
    )(u3, att, gc, ga, x, w_cb_g, b_cb, w_ab_g, w_out_g, g2, g3, after)


def ffn_up_fwd(h2, w_up_g):
    def body(h_ref, wg_ref, wu_ref, gate_ref, up_ref, act_ref):
        h = h_ref[...]
        gate = _dot(h, wg_ref[0])
        up = _dot(h, wu_ref[0])
        gate_ref[...] = gate.astype(MM)
        up_ref[...] = up.astype(MM)
        act_ref[...] = (gate * _sigmoid(gate) * up).astype(MM)

    tile = pl.BlockSpec((TM, UP_SHARD), lambda n, i: (i, n))
    act = jax.ShapeDtypeStruct((SEQ, D_FF), MM)
    return pl.pallas_call(
        body, name="ffn_up_fwd", grid=(2, SEQ // TM), out_shape=[act, act, act],
        in_specs=[pl.BlockSpec((TM, D_MODEL), lambda n, i: (i, 0)),
                  pl.BlockSpec((1, D_MODEL, UP_SHARD), lambda n, i: (n, 0, 0)),
                  pl.BlockSpec((1, D_MODEL, UP_SHARD), lambda n, i: (n + 2, 0, 0))],
        out_specs=[tile, tile, tile],
        compiler_params=_params(("arbitrary", "arbitrary")),
    )(h2, w_up_g, w_up_g)


def ffn_down_loss(act, w_down_g, x2, target, g4, gate, up):
    def body(act_ref, wd_ref, x2_ref, t_ref, g_ref, gate_ref, up_ref, dff_ref, dy_ref, loss_ref, dg_ref, dgu_ref):
        ff = _dot(act_ref[...], wd_ref[...])
        n4, r4 = _rms(ff)
        g4v = g_ref[...]
        err = x2_ref[...] + n4 * g4v - t_ref[...]
        row_loss = jnp.mean(err * err, axis=-1, keepdims=True)
        loss_ref[...] = jnp.zeros((8, 128), F32) + 0.5 * jnp.sum(row_loss, axis=0, keepdims=True)
        dy = err * (1.0 / D_MODEL)
        dy_ref[...] = dy
        dff = _rms_bwd(dy * g4v, n4, r4).astype(MM)
        dff_ref[...] = dff
        _acc_rows(dg_ref, jnp.sum(dy * n4, axis=0, keepdims=True), pl.program_id(0) == 0)
        dact = _dot_nt(dff, wd_ref[...])
        gate = gate_ref[...].astype(F32)
        sg = _sigmoid(gate)
        dgu_ref[:, 0:D_FF] = (dact * up_ref[...].astype(F32) * (sg * (1.0 + gate * (1.0 - sg)))).astype(MM)
        dgu_ref[:, D_FF:2 * D_FF] = (dact * (gate * sg)).astype(MM)

    nt = SEQ // TM
    vec = _full_spec((1, D_MODEL))
    return pl.pallas_call(
        body, name="ffn_down_loss", grid=(nt,),
        out_shape=(jax.ShapeDtypeStruct((SEQ, D_MODEL), MM), jax.ShapeDtypeStruct((SEQ, D_MODEL), F32),
                   jax.ShapeDtypeStruct((nt * 8, 128), F32), jax.ShapeDtypeStruct((1, D_MODEL), F32),
                   jax.ShapeDtypeStruct((SEQ, 2 * D_FF), MM)),
        in_specs=[_row_tile_spec(D_FF), _weight_spec(w_down_g.shape), _row_tile_spec(D_MODEL),
                  _row_tile_spec(D_MODEL), vec, _row_tile_spec(D_FF), _row_tile_spec(D_FF)],
        out_specs=[_row_tile_spec(D_MODEL), _row_tile_spec(D_MODEL),
                   pl.BlockSpec((8, 128), lambda i: (i, 0)), vec, _row_tile_spec(2 * D_FF)],
        compiler_params=_params(("arbitrary",)),
    )(act, w_down_g, x2, target, g4, gate, up)


def ffn_merge_bwd(dgu, w_up_g, x2, mix, dy, g3, g2, w_out_g, gc, ga, u3, att, w_cb_g, b_cb, w_ab_g, after):
    def body(dgu_ref, w_ref, x2_ref, mix_ref, dy_ref, g3_ref, g2_ref,
             wout_ref, gc_ref, ga_ref, u_ref, a_ref, wcb_ref, bcb_ref, wab_ref, after_ref,
             dx2_ref, dmix_ref, dg3_ref, dg2_ref, dco_ref, dao_ref, dg_ref, du3_ref, datt_ref, dbcb_ref):
        dh2 = None
        for j in range(N_CHIPS):
            t = _dot_nt(dgu_ref[:, j * UP_SHARD:(j + 1) * UP_SHARD], w_ref[j])
            dh2 = t if dh2 is None else dh2 + t
        first = pl.program_id(0) == 0
        n3, r3 = _rms(x2_ref[...])
        dx2 = dy_ref[...] + _rms_bwd(dh2 * g3_ref[...], n3, r3)
        dx2_ref[...] = dx2
        _acc_rows(dg3_ref, jnp.sum(dh2 * n3, axis=0, keepdims=True), first)
        n2, r2 = _rms(mix_ref[...])
        dmix = _rms_bwd(dx2 * g2_ref[...], n2, r2).astype(MM)
        dmix_ref[...] = dmix
        _acc_rows(dg2_ref, jnp.sum(dx2 * n2, axis=0, keepdims=True), first)

        dm = _dot_nt(dmix, wout_ref[...])
        co, ao = _branch_outputs(u_ref, a_ref, wcb_ref, bcb_ref, wab_ref)
        sgc = _sigmoid(gc_ref[...])
        sga = _sigmoid(ga_ref[...])
        dco = dm * sgc
        dao = dm * sga
        dg_ref[:, 0:D_MODEL] = (dm * co * (sgc * (1.0 - sgc))).astype(MM)
        dg_ref[:, D_MODEL:2 * D_MODEL] = (dm * ao * (sga * (1.0 - sga))).astype(MM)
        _acc_rows(dbcb_ref, jnp.sum(dco, axis=0, keepdims=True), pl.program_id(0) == 0)
        dco_ref[...] = dco.astype(MM)
        dao_ref[...] = dao.astype(MM)
        du3 = None
        datt = None
        for j in range(N_CHIPS):
            cols = slice(j * BR_SHARD, (j + 1) * BR_SHARD)
            t = _dot_nt(dco_ref[:, cols], wcb_ref[j])
            s = _dot_nt(dao_ref[:, cols], wab_ref[j])
            du3 = t if du3 is None else du3 + t
            datt = s if datt is None else datt + s
        du3_ref[...] = du3
        datt_ref[...] = datt.astype(MM)

    wide = _row_tile_spec(D_MODEL)
    vec = _full_spec((1, D_MODEL))
    wide_f32, wide_mm = jax.ShapeDtypeStruct((SEQ, D_MODEL), F32), jax.ShapeDtypeStruct((SEQ, D_MODEL), MM)
    vec_f32 = jax.ShapeDtypeStruct((1, D_MODEL), F32)
    return pl.pallas_call(
        body, name="ffn_merge_bwd", grid=(SEQ // TM,),
        out_shape=(wide_f32, wide_mm, vec_f32, vec_f32, wide_mm, wide_mm,
                   jax.ShapeDtypeStruct((SEQ, 2 * D_MODEL), MM),
                   jax.ShapeDtypeStruct((SEQ, CONV_DIM), F32), jax.ShapeDtypeStruct((SEQ, ATT_DIM), MM), vec_f32),
        in_specs=[_row_tile_spec(2 * D_FF), _weight_spec(w_up_g.shape), wide, wide, wide, vec, vec,
                  _weight_spec(w_out_g.shape), wide, wide, _row_tile_spec(CONV_DIM), _row_tile_spec(ATT_DIM),
                  _weight_spec(w_cb_g.shape), vec, _weight_spec(w_ab_g.shape), TOKEN_SPEC],
        out_specs=[wide, wide, vec, vec, wide, wide, _row_tile_spec(2 * D_MODEL), _row_tile_spec(CONV_DIM),
                   _row_tile_spec(ATT_DIM), vec],
        compiler_params=_params(("arbitrary",)),
    )(dgu, w_up_g, x2, mix, dy, g3, g2, w_out_g, gc, ga, u3, att, w_cb_g, b_cb, w_ab_g, after)


def conv_bwd(du3, u1, ci, w_dw, ln_g, ln_b, after):
    def body(du3_hbm, u1_hbm, ci_hbm, w_ref, g_ref, bb_ref, after_ref,
             dci_ref, dw_ref, dbdw_ref, dg_ref, db_ref, upad_ref, dpad_ref, dwacc_ref, vacc_ref,
             du3_ref, u1_ref, ci_ref, ci_sems, u1_sems, du3_sems):
        ci_chunk = _row_chunks(ci_hbm, ci_ref, ci_sems)
        u1_chunk = _row_chunks(u1_hbm, u1_ref, u1_sems)
        du3_chunk = _row_chunks(du3_hbm, du3_ref, du3_sems)
        _glu_into(ci_chunk, ci_ref, upad_ref)
        dpad_ref[SEQ:SEQ + 32, :] = jnp.zeros((32, CONV_DIM), F32)
        dwacc_ref[...] = jnp.zeros_like(dwacc_ref)
        vacc_ref[...] = jnp.zeros_like(vacc_ref)

        def fold8(t):
            s = t[0:8, :]
            for r in range(8, t.shape[0], 8):
                s = s + t[r:r + 8, :]
            return s

        def pass1(i, c):
            t0 = pl.multiple_of(i * CONV_TILE, CONV_TILE)
            gv = g_ref[...]
            for r in range(0, CONV_TILE, NORM_ROWS):
                rows = pl.ds(t0 + r, NORM_ROWS)
                xh, rstd = _layernorm_parts(u1_ref[rows, :])
                u2 = xh * gv + bb_ref[...]
                s2 = _sigmoid(u2)
                du2 = du3_ref[rows, :] * (s2 * (1.0 + u2 * (1.0 - s2)))
                wv = du2 * gv
                du1 = rstd * (wv - jnp.mean(wv, axis=-1, keepdims=True)
                              - xh * jnp.mean(wv * xh, axis=-1, keepdims=True))
                dpad_ref[rows, :] = du1
                vacc_ref[0] += fold8(du2 * xh)
                vacc_ref[1] += fold8(du2)
                vacc_ref[2] += fold8(du1)
            for cols in LANE_GROUPS:
                du1 = dpad_ref[pl.ds(t0, CONV_TILE), cols]
                for m, rows in _shifted_windows(upad_ref, t0, cols, range(2, CONV_WIDTH + 2)):
                    dwacc_ref[m - 2, :, cols] += fold8(du1 * rows)
            return c

        tiles_per_chunk = GLU_ROWS // CONV_TILE

        def pass1_chunk(ch, c):
            u1_chunk(ch).wait()
            du3_chunk(ch).wait()
            return lax.fori_loop(ch * tiles_per_chunk, (ch + 1) * tiles_per_chunk, pass1, c)

        lax.fori_loop(0, SEQ // GLU_ROWS, pass1_chunk, 0)

        def pass2(i, c):
            t0 = pl.multiple_of(i * CONV_TILE, CONV_TILE)
            tile = pl.ds(t0, CONV_TILE)
            for cols in LANE_GROUPS:
                gate_cols = slice(cols.start + CONV_DIM, cols.stop + CONV_DIM)
                du0 = _shifted_sum(dpad_ref, t0, cols, w_ref, lambda j: 30 - j)
                a = ci_ref[tile, cols]
                sb = _sigmoid(ci_ref[tile, gate_cols])
                dci_ref[tile, cols] = (du0 * sb).astype(MM)
                dci_ref[tile, gate_cols] = (du0 * a * (sb * (1.0 - sb))).astype(MM)
            return c

        lax.fori_loop(0, SEQ // CONV_TILE, pass2, 0)

        for j in range(CONV_WIDTH):
            dw_ref[j:j + 1, :] = jnp.sum(dwacc_ref[j], axis=0, keepdims=True)
        dw_ref[CONV_WIDTH:32, :] = jnp.zeros((32 - CONV_WIDTH, CONV_DIM), F32)
        dg_ref[...] = jnp.sum(vacc_ref[0], axis=0, keepdims=True)
        db_ref[...] = jnp.sum(vacc_ref[1], axis=0, keepdims=True)
        dbdw_ref[...] = jnp.sum(vacc_ref[2], axis=0, keepdims=True)

    vec = jax.ShapeDtypeStruct((1, CONV_DIM), F32)
    return pl.pallas_call(
        body, name="conv_bwd",
        out_shape=(jax.ShapeDtypeStruct((SEQ, 2 * CONV_DIM), MM), jax.ShapeDtypeStruct((32, CONV_DIM), F32),
                   vec, vec, vec),
        in_specs=[ANY] * 3 + [VMEM_SPEC] * 4, out_specs=[VMEM_SPEC] * 5,
        scratch_shapes=[pltpu.VMEM((SEQ + 32, CONV_DIM), F32), pltpu.VMEM((SEQ + 32, CONV_DIM), F32),
                        pltpu.VMEM((CONV_WIDTH, 8, CONV_DIM), F32), pltpu.VMEM((3, 8, CONV_DIM), F32)]
                       + [pltpu.VMEM(a.shape, a.dtype) for a in (du3, u1, ci)]
                       + [pltpu.SemaphoreType.DMA((SEQ // GLU_ROWS,))] * 3,
        compiler_params=_params(),
    )(du3, u1, ci, w_dw, ln_g, ln_b, after)


def attn_bwd(q, k, v, datt, rc, after):
    def body(q_hbm, k_hbm, v_hbm, do_hbm, rc_hbm, after_ref, dqkv_ref, dqa_ref, dka_ref, dva_ref, pc_ref, z_ref,
             sig1_ref, sig2_ref, g_ref, spb_ref, gb_ref, ar_ref, dzr_ref, dzc_ref, qm_ref, km_ref, dom_ref,
             q_ref, k_ref, v_ref, do_ref, rc_ref, sems):
        arrive = _fetch((q_hbm, k_hbm, do_hbm, v_hbm, rc_hbm), (q_ref, k_ref, do_ref, v_ref, rc_ref), sems)
        lane, row, _ = _head_masks()
        for ref in (dqa_ref, dka_ref, dva_ref, pc_ref):
            ref[...] = jnp.zeros_like(ref)
        z_ref[...] = jnp.full(z_ref.shape, NO_SCORE, F32)
        for ref in (sig1_ref, sig2_ref, spb_ref, ar_ref, g_ref, gb_ref, dzr_ref, dzc_ref):
            ref[...] = jnp.zeros_like(ref)
        for cp, (src_ref, split_ref) in zip(arrive, ((q_ref, qm_ref), (k_ref, km_ref), (do_ref, dom_ref))):
            cp.wait()
            _split_heads(src_ref, split_ref)
        arrive[3].wait()
        arrive[4].wait()
        w_suffix = _cumsum_weights(suffix=True, with_total=False)
        w_prefix = _cumsum_weights(suffix=False, with_total=True)

        def step(pairs):
            (ia, ja), (ib, jb), (ic, jc), (id_, jd) = pairs
            ka, qb_, kb_, kc, qd, kd = (pl.multiple_of(jnp.maximum(b, 0) * TQ, TQ) for b in (ja, ib, jb, jc, id_, jd))
            qa2, qb2, qc2, qd2, kd2 = (pl.multiple_of(jnp.maximum(b, 0) * 2 * TQ, 2 * TQ)
                                       for b in (ia, ib, ic, id_, jd))
            bias_a = _score_bias(lane, row, ia, ja)
            rc_rows = rc_ref[pl.ds(qb_, TQ), :]
            first_c = jc == 0
            for p in range(N_PAIRS):
                cols = slice(128 * p, 128 * (p + 1))
                k_a = k_ref[pl.ds(ka, TQ), cols]
                v_b = v_ref[pl.ds(kb_, TQ), cols]
                dqa_ref[pl.ds(qd, TQ), cols] += _dot(dzc_ref[p], km_ref[p, pl.ds(kd2, 2 * TQ), :])
                dka_ref[pl.ds(kd, TQ), cols] += _dot_tn(dzr_ref[p], qm_ref[p, pl.ds(qd2, 2 * TQ), :])
                dva_ref[pl.ds(kc, TQ), cols] += _dot_tn(ar_ref[p], dom_ref[p, pl.ds(qc2, 2 * TQ), :])
                for h in range(2):
                    hh = 2 * p + h
                    rows = slice(TQ * h, TQ * (h + 1))
                    r = _dot(gb_ref[hh], w_prefix)
                    p_in = jnp.where(first_c, 0.0, pc_ref[hh])
                    dz = (g_ref[hh] - sig2_ref[hh] * (r[:, :128] + p_in)).astype(MM)
                    dzc_ref[p, :, rows] = dz
                    dzr_ref[p, rows, :] = dz
                    pc_ref[hh] = p_in + r[:, 128:]
                    r_in = jnp.sum(jnp.where(lane == 16 * hh + jb, rc_rows, 0.0), axis=1, keepdims=True)
                    a = jnp.exp(z_ref[hh] - (_dot(spb_ref[hh], w_suffix) + r_in))
                    g = _dot_nt(dom_ref[p, pl.ds(qb2 + TQ * h, TQ), :], v_b) * a
                    ar_ref[p, rows, :] = a.astype(MM)
                    g_ref[hh] = g
                    gb_ref[hh] = g.astype(MM)
                    sig2_ref[hh] = sig1_ref[hh]
                    z = _dot_nt(qm_ref[p, pl.ds(qa2 + TQ * h, TQ), :], k_a) + bias_a
                    sp = _softplus(z)
                    sig1_ref[hh] = jnp.exp(z - sp)
                    z_ref[hh] = z
                    spb_ref[hh] = sp.astype(MM)

        _block_pipeline(4, False, step)
        dqkv_ref[:, 0:ATT_DIM] = (dqa_ref[...] * ATT_SCALE).astype(MM)
        dqkv_ref[:, ATT_DIM:2 * ATT_DIM] = dka_ref[...].astype(MM)
        dqkv_ref[:, 2 * ATT_DIM:3 * ATT_DIM] = dva_ref[...].astype(MM)

    split = pltpu.VMEM((N_PAIRS, 2 * SEQ, 128), MM)
    return pl.pallas_call(
        body, name="attn_bwd", out_shape=jax.ShapeDtypeStruct((SEQ, 3 * ATT_DIM), MM),
        in_specs=[ANY] * 5 + [VMEM_SPEC], out_specs=VMEM_SPEC,
        scratch_shapes=[pltpu.VMEM((SEQ, ATT_DIM), F32)] * 3 + [pltpu.VMEM((8, TQ, 128), F32)] * 5
                       + [pltpu.VMEM((8, TQ, 128), MM)] * 2
                       + [pltpu.VMEM((N_PAIRS, 2 * TQ, 128), MM)] * 2 + [pltpu.VMEM((N_PAIRS, TQ, 256), MM)]
                       + [split] * 3
                       + [pltpu.VMEM(a.shape, a.dtype) for a in (q, k, v, datt, rc)] + [pltpu.SemaphoreType.DMA((5,))],
        compiler_params=_params(),
    )(q, k, v, datt, rc, after)


DPROJ_PIECES = ((0, 1024), (1024, 2560), (2560, 4608))


def _dproj_segments(j):
    g0, g1 = j * IN_SHARD, (j + 1) * IN_SHARD
    segs = []
    for p, (s, e) in enumerate(DPROJ_PIECES):
        lo, hi = max(s, g0), min(e, g1)
        if lo < hi:
            segs.append((p, lo - s, lo - g0, hi - lo))
    return segs


def in_proj_bwd(pieces, w_in_g, x, dx2, g1, after):
    def body(p0_ref, p1_ref, p2_ref, w_ref, x_ref, dx2_ref, g_ref, after_ref, dx_ref, dg_ref):
        p_refs = (p0_ref, p1_ref, p2_ref)
        dh = None
        for j in range(N_CHIPS):
            for p, lo, off, width in _dproj_segments(j):
                t = _dot_nt(p_refs[p][:, lo:lo + width], w_ref[j, :, off:off + width])
                dh = t if dh is None else dh + t
        n1, r1 = _rms(x_ref[...])
        dx_ref[...] = dx2_ref[...] + _rms_bwd(dh * g_ref[...], n1, r1)
        _acc_rows(dg_ref, jnp.sum(dh * n1, axis=0, keepdims=True), pl.program_id(0) == 0)

    vec = _full_spec((1, D_MODEL))
    return pl.pallas_call(
        body, name="in_proj_bwd", grid=(SEQ // TM,),
        out_shape=[jax.ShapeDtypeStruct((SEQ, D_MODEL), F32), jax.ShapeDtypeStruct((1, D_MODEL), F32)],
        in_specs=[_row_tile_spec(p.shape[1]) for p in pieces]
                 + [_weight_spec(w_in_g.shape), _row_tile_spec(D_MODEL), _row_tile_spec(D_MODEL), vec, TOKEN_SPEC],
        out_specs=[_row_tile_spec(D_MODEL), vec],
        compiler_params=_params(("arbitrary",)),
    )(*pieces, w_in_g, x, dx2, g1, after)


def weight_grad_in(h1, pieces):
    kh = D_MODEL // 2
    operands = (h1,) + tuple(pieces)
    out = jax.ShapeDtypeStruct((N_CHIPS, 2, kh, IN_SHARD), MM)

    def body(*refs):
        hbm, o_hbm, bufs, stage_ref, in_sems, out_sems = refs[:4], refs[4], refs[5:9], refs[9], refs[10], refs[11]
        arrive = _fetch(hbm, bufs, in_sems)
        a_ref, p_refs = bufs[0], bufs[1:]
        arrive[0].wait()
        here, leaving = set(), []
        for j in range(N_CHIPS):
            for p in sorted({seg[0] for seg in _dproj_segments(j)} - here):
                arrive[1 + p].wait()
                here.add(p)
            for h in range(2):
                a = a_ref[:, h * kh:(h + 1) * kh]
                for p, lo, off, width in _dproj_segments(j):
                    stage_ref[j, h, :, off:off + width] = _dot_tn(a, p_refs[p][:, lo:lo + width]).astype(MM)
                leaving.append(pltpu.make_async_copy(stage_ref.at[j, h], o_hbm.at[j, h], out_sems.at[2 * j + h]))
                leaving[-1].start()
        for cp in leaving:
            cp.wait()

    return pl.pallas_call(
        body, name="dw_in", out_shape=out, in_specs=[ANY] * 4, out_specs=ANY,
        scratch_shapes=[pltpu.VMEM(a.shape, a.dtype) for a in operands]
                       + [pltpu.VMEM(out.shape, out.dtype), pltpu.SemaphoreType.DMA((4,)),
                          pltpu.SemaphoreType.DMA((2 * N_CHIPS,))],
        compiler_params=_params(),
    )(*operands)


def weight_grad(a, b, name, col_sharded, tk=None):
    kin, n = a.shape[1], b.shape[1]

    def body(a_ref, b_ref, o_ref):
        if col_sharded:
            o_ref[0, 0] = _dot_tn(a_ref[...], b_ref[...]).astype(MM)
        else:
            o_ref[...] = _dot_tn(a_ref[...], b_ref[...]).astype(MM)

    if col_sharded:
        kh, ns = kin // 2, n // N_CHIPS
        out = jax.ShapeDtypeStruct((N_CHIPS, 2, kh, ns), MM)
        grid = (2, N_CHIPS)
        in_specs = [pl.BlockSpec((SEQ, kh), lambda h, j: (0, h)), pl.BlockSpec((SEQ, ns), lambda h, j: (0, j))]
        out_spec = pl.BlockSpec((1, 1, kh, ns), lambda h, j: (j, h, 0, 0))
        sem = ("arbitrary", "arbitrary")
    else:
        out = jax.ShapeDtypeStruct((kin, n), MM)
        grid = (kin // tk,)
        in_specs = [pl.BlockSpec((SEQ, tk), lambda r: (0, r)), pl.BlockSpec((SEQ, n), lambda r: (0, 0))]
        out_spec = pl.BlockSpec((tk, n), lambda r: (r, 0))
        sem = ("arbitrary",)
    res = pl.pallas_call(
        body, name=name, grid=grid, out_shape=out, in_specs=in_specs, out_specs=out_spec,
        compiler_params=_params(sem),
    )(a, b)
    if not col_sharded:
        res = res.reshape(N_CHIPS, 2, kin // (2 * N_CHIPS), n)
    return res


def weight_grad_mix(merged, dmix, u3, dco, att, dao):
    operands = (merged, dmix, u3, dco, att, dao)
    n_out, n_br = D_MODEL // 2, CONV_DIM // 2

    def body(*refs):
        hbm, (o_out, o_cb, o_ab), bufs, sems = refs[:6], refs[6:9], refs[9:15], refs[15]
        copies = [pltpu.make_async_copy(hbm[i], bufs[i], sems.at[i]) for i in range(6)]
        for cp in copies:
            cp.start()
        m_ref, dm_ref, u_ref, dco_ref, a_ref, dao_ref = bufs
        copies[0].wait()
        copies[1].wait()
        for h in range(2):
            o_out[h * n_out:(h + 1) * n_out, :] = _dot_tn(m_ref[:, h * n_out:(h + 1) * n_out], dm_ref[...]).astype(MM)
        for br, (a, d, o) in enumerate(((u_ref, dco_ref, o_cb), (a_ref, dao_ref, o_ab))):
            copies[2 + 2 * br].wait()
            copies[3 + 2 * br].wait()
            for h in range(2):
                g = _dot_tn(a[:, h * n_br:(h + 1) * n_br], d[...])
                for j in range(N_CHIPS):
                    o[j, h] = g[:, j * BR_SHARD:(j + 1) * BR_SHARD].astype(MM)

    branch = jax.ShapeDtypeStruct((N_CHIPS, 2, n_br, BR_SHARD), MM)
    dw_out, dw_cb, dw_ab = pl.pallas_call(
        body, name="dw_mix", out_shape=[jax.ShapeDtypeStruct((D_MODEL, D_MODEL), MM), branch, branch],
        in_specs=[ANY] * 6, out_specs=[VMEM_SPEC] * 3,
        scratch_shapes=[pltpu.VMEM(a.shape, a.dtype) for a in operands] + [pltpu.SemaphoreType.DMA((6,))],
        compiler_params=_params(),
    )(*operands)
    return dw_out.reshape(N_CHIPS, 2, D_MODEL // (2 * N_CHIPS), D_MODEL), dw_cb, dw_ab


def _place():
    x, y, c = lax.axis_index("x"), lax.axis_index("y"), lax.axis_index("c")
    chips = [(1 - x, y), (x, 1 - y), (1 - x, 1 - y)]
    return x, y, c, chips


def _rcopy(src, dst, send_sem, recv_sem, dev):
    return pltpu.make_async_remote_copy(src_ref=src, dst_ref=dst, send_sem=send_sem, recv_sem=recv_sem,
                                        device_id=dev, device_id_type=MESH)


class _Gather:
    N_MOVES = 6

    def __init__(self, shapes, w, o, scratch):
        self.n, self.shapes, self.w, self.o = len(w), shapes, w, o
        self.send, self.recv, self.psend, self.precv, self.loc_in, self.loc_out = scratch[:6]
        self.raw, self.stage = scratch[6:6 + self.n], scratch[6 + self.n:]
        x, y, c, self.chips = _place()
        self.c = c
        self.me, k_x, k_y, k_far = 2 * x + y, 2 * (1 - x) + y, 2 * x + (1 - y), 2 * (1 - x) + (1 - y)
        to_x, to_y = (1 - x, y, c), (x, 1 - y, c)
        self.sib = (x, y, 1 - c)
        self.sent_as = [(self.me, 0, to_x), (self.me, 1, to_y), (self.me, 1, to_x), (self.me, 0, to_y),
                        (k_x, 0, to_y), (k_y, 1, to_x)]
        self.arrives_as = [(k_x, 0, to_x), (k_y, 1, to_y), (k_x, 1, to_x), (k_y, 0, to_y),
                           (k_far, 0, to_y), (k_far, 1, to_x)]
        self.sent_on_after = {0: 4, 1: 5}

    @staticmethod
    def scratch(shards):
        n = len(shards)
        sems = pltpu.SemaphoreType.DMA
        m = _Gather.N_MOVES * n
        return ([sems((m,)), sems((m,)), sems((m,)), sems((m,)), sems((3 * n,)), sems((n,))]
                + [pltpu.VMEM(s.shape, s.dtype) for s in shards] + [pltpu.VMEM(s.shape, MM) for s in shards])

    @staticmethod
    def out_shapes(shards):
        return [jax.ShapeDtypeStruct((N_CHIPS,) + s.shape, MM) for s in shards]

    def _rows(self, t, quarter, cc):
        rq = self.shapes[t][0] // 4
        return pl.ds(pl.multiple_of((2 * cc + quarter) * rq, rq), rq)

    def _own_rows(self, t, piece):
        if piece < 2:
            return self._rows(t, piece, self.c)
        rh = self.shapes[t][0] // 2
        return pl.ds(pl.multiple_of((1 - self.c) * rh, rh), rh)

    def _chip(self, j):
        cx, cy = self.chips[j]
        return 2 * cx + cy, (cx, cy, self.c)

    def local_in(self, t, piece):
        rows = self._own_rows(t, piece)
        return pltpu.make_async_copy(self.w[t].at[rows, :], self.raw[t].at[rows, :], self.loc_in.at[3 * t + piece])

    def local_out(self, t):
        return pltpu.make_async_copy(self.stage[t], self.o[t].at[self.me], self.loc_out.at[t])

    def sent(self, i, t):
        k, quarter, dev = self.sent_as[i]
        rows = self._rows(t, quarter, self.c)
        there = self.o[t].at[k, rows, :]
        return _rcopy(self.stage[t].at[rows, :] if i < 4 else there, there,
                      self.send.at[i * self.n + t], self.recv.at[i * self.n + t], dev)

    def arrived(self, i, t):
        k, quarter, dev = self.arrives_as[i]
        blk = self.o[t].at[k, self._rows(t, quarter, self.c), :]
        return _rcopy(blk, blk, self.send.at[i * self.n + t], self.recv.at[i * self.n + t], dev)

    def passed(self, i, t, cc):
        k, quarter, _ = self.arrives_as[i]
        blk = self.o[t].at[k, self._rows(t, quarter, cc), :]
        return _rcopy(blk, blk, self.psend.at[i * self.n + t], self.precv.at[i * self.n + t], self.sib)

    def start(self):
        for piece in range(3):
            for t in range(self.n):
                self.local_in(t, piece).start()
        for piece, moves in enumerate(((0, 3), (1, 2), ())):
            for t in range(self.n):
                rows = self._own_rows(t, piece)
                self.local_in(t, piece).wait()
                self.stage[t][rows, :] = self.raw[t][rows, :].astype(MM)
                for i in moves:
                    self.sent(i, t).start()
        for t in range(self.n):
            self.local_out(t).start()

    def forward(self):
        for i in range(self.N_MOVES):
            for t in range(self.n):
                self.arrived(i, t).wait_recv()
                if i in self.sent_on_after:
                    self.sent(self.sent_on_after[i], t).start()
                self.passed(i, t, self.c).start()

    def finish(self):
        for i in range(self.N_MOVES):
            for t in range(self.n):
                self.passed(i, t, 1 - self.c).wait_recv()
        for i in range(self.N_MOVES):
            for t in range(self.n):
                self.sent(i, t).wait_send()
                self.passed(i, t, self.c).wait_send()
        for t in range(self.n):
            self.local_out(t).wait()


def all_gather_weights(shards, small, later):
    n, m = len(shards), len(later)
    shapes = [s.shape for s in shards]

    def body(*refs):
        w = refs[:n]
        sm = refs[n]
        lw = refs[n + 1:n + 1 + m]
        o = refs[n + 1 + m:2 * n + 1 + m]
        osm = refs[2 * n + 1 + m]
        lo = refs[2 * n + 2 + m:2 * n + 2 + 2 * m]
        scratch = refs[2 * n + 2 + 2 * m:]
        ssend, srecv, sloc, lsem_in, lsem_out = scratch[:5]
        lraw, lstage = scratch[5:5 + m], scratch[5 + m:5 + 2 * m]
        g = _Gather(shapes, w, o, scratch[5 + 2 * m:])
        own = pltpu.make_async_copy(sm, osm.at[g.me], sloc)
        own.start()
        g.start()
        loads = [pltpu.make_async_copy(lw[t], lraw[t], lsem_in.at[t]) for t in range(m)]
        for cp in loads:
            cp.start()
        small_cps = [_rcopy(sm, osm.at[g.me], ssend.at[j], srecv.at[j], g._chip(j)[1]) for j in range(3)]
        for cp in small_cps:
            cp.start()
        places = []
        for t in range(m):
            loads[t].wait()
            lstage[t][...] = lraw[t][...].astype(MM)
            places.append(pltpu.make_async_copy(lstage[t], lo[t].at[g.me], lsem_out.at[t]))
            places[t].start()
        g.forward()
        g.finish()
        for j in range(3):
            k, dev = g._chip(j)
            _rcopy(sm, osm.at[k], ssend.at[j], srecv.at[j], dev).wait_recv()
            small_cps[j].wait_send()
        own.wait()
        for cp in places:
            cp.wait()

    out_shape = _Gather.out_shapes(shards)
    out_shape.append(jax.ShapeDtypeStruct((N_CHIPS,) + small.shape, small.dtype))
    out_shape += _Gather.out_shapes(later)
    sems = pltpu.SemaphoreType.DMA
    return pl.pallas_call(
        body, name="all_gather_weights", out_shape=out_shape,
        in_specs=[ANY] * (n + 1 + m), out_specs=[ANY] * (n + 1 + m),
        scratch_shapes=[sems((3,)), sems((3,)), sems, sems((m,)), sems((m,))]
                       + [pltpu.VMEM(s.shape, s.dtype) for s in later] + [pltpu.VMEM(s.shape, MM) for s in later]
                       + _Gather.scratch(shards),
        compiler_params=_params(),
    )(*shards, small, *later)


HBM_SPEC = pl.BlockSpec(memory_space=pltpu.HBM)
SEM_SPEC = pl.BlockSpec(memory_space=pltpu.SEMAPHORE)
DATAFLOW = pltpu.SideEffectType.DATAFLOW_SIDE_EFFECTING


def split_start(name, bufs, n_copies, copies):
    nb = len(bufs)

    def body(*refs):
        for cp in copies(refs[:nb], refs[nb], refs[nb + 1]):
            cp.start()
        token = refs[2 * nb + 2]
        token[...] = jnp.zeros_like(token)

    sems = [pltpu.SemaphoreType.DMA((n_copies,))] * 2
    res = pl.pallas_call(
        body, name=name,
        out_shape=sems + [pltpu.HBM(a.shape, a.dtype) for a in bufs] + [jax.ShapeDtypeStruct((8, 128), F32)],
        in_specs=[HBM_SPEC] * nb, out_specs=[SEM_SPEC] * 2 + [HBM_SPEC] * nb + [VMEM_SPEC],
        input_output_aliases={i: 2 + i for i in range(nb)},
        compiler_params=pltpu.CompilerParams(has_side_effects=DATAFLOW),
    )(*[pltpu.with_memory_space_constraint(a, pltpu.HBM) for a in bufs])
    return res[:-1], res[-1]


def split_wait(name, state, after, copies):
    sems, bufs = state[:2], state[2:]
    nb = len(bufs)

    def body(*refs):
        for cp in copies(refs[:nb], refs[nb], refs[nb + 1]):
            cp.wait_send()
            cp.wait_recv()

    return pl.pallas_call(
        body, name=name, out_shape=[pltpu.HBM(a.shape, a.dtype) for a in bufs],
        in_specs=[HBM_SPEC] * nb + [SEM_SPEC] * 2 + [ANY] * len(after), out_specs=[HBM_SPEC] * nb,
        input_output_aliases={i: i for i in range(nb)},
        compiler_params=pltpu.CompilerParams(has_side_effects=DATAFLOW),
    )(*bufs, *sems, *after)


class _Shifted:
    def __init__(self, sems, first):
        self.sems, self.first = sems, first

    @property
    def at(self):
        return self

    def __getitem__(self, i):
        return self.sems.at[self.first + i]


def _scatter_copies(n):
    def copies(refs, send, recv):
        _, _, c, chips = _place()
        return [_rcopy(refs[t].at[2 * cx + cy], refs[n + t].at[j], send.at[3 * t + j], recv.at[3 * t + j], (cx, cy, c))
                for t in range(n) for j, (cx, cy) in enumerate(chips)]
    return copies


def _direct_copies(n):
    def copies(refs, send, recv):
        x, y, c, chips = _place()
        out = []
        for t in range(n):
            src, land = refs[t], refs[n + t]
            for to_core, first in ((c, 0), (1 - c, 3)):
                for j, (cx, cy) in enumerate(chips):
                    i = 7 * t + first + j
                    out.append(_rcopy(src.at[2 * cx + cy, to_core], land.at[first + j], send.at[i], recv.at[i],
                                      (cx, cy, to_core)))
            i = 7 * t + 6
            out.append(_rcopy(src.at[2 * x + y, 1 - c], land.at[6], send.at[i], recv.at[i], (x, y, 1 - c)))
        return out
    return copies


def _scatter_and_direct(n, m):
    def copies(refs, send, recv):
        return (_scatter_copies(n)(refs[:2 * n], send, recv)
                + _direct_copies(m)(refs[2 * n:], _Shifted(send, 3 * n), _Shifted(recv, 3 * n)))
    return copies


def scatter_start(parts, partials):
    n, m = len(parts), len(partials)
    lands = [lax.empty((3,) + p.shape[1:], p.dtype) for p in parts]
    direct_lands = [lax.empty((7,) + p.shape[2:], p.dtype) for p in partials]
    return split_start("scatter_start_rest", list(parts) + lands + list(partials) + direct_lands, 3 * n + 7 * m,
                       _scatter_and_direct(n, m))


def scatter_wait(state, after, n, m):
    res = split_wait("scatter_wait_rest", state, after, _scatter_and_direct(n, m))
    return res[n:2 * n], res[2 * n:2 * n + m], res[2 * n + m:]


def _gather_copies(shapes, level, to_both_cores=()):
    n = len(shapes)

    def copies(refs, send, recv):
        x, y, c, chips = _place()
        out = []
        for t in list(range(n)) + list(to_both_cores):
            rh = shapes[t][0] // 2
            to_core = c if len(out) < 3 * n else 1 - c
            for cx, cy in chips:
                k, dev = (2 * x + y, (cx, cy, to_core)) if level == 1 else (2 * cx + cy, (x, y, 1 - c))
                blk = refs[t].at[k, pl.ds(c * rh, rh), :]
                out.append(_rcopy(blk, blk, send.at[len(out)], recv.at[len(out)], dev))
        return out
    return copies


def _sibling_copies(n, other_half):
    def copies(refs, send, recv):
        x, y, c, _ = _place()
        return [_rcopy(refs[t].at[:, 1 - c] if other_half else refs[t], refs[n + t], send.at[t], recv.at[t],
                       (x, y, 1 - c)) for t in range(n)]
    return copies


def sibling_start(srcs, other_half, tag):
    lands = [lax.empty((a.shape[0],) + a.shape[2:] if other_half else a.shape, a.dtype) for a in srcs]
    return split_start("sibling_start_" + tag, list(srcs) + lands, len(srcs),
                       _sibling_copies(len(srcs), other_half))


def sibling_wait(state, after, other_half, tag):
    n = (len(state) - 2) // 2
    res = split_wait("sibling_wait_" + tag, state, after, _sibling_copies(n, other_half))
    return res[:n], res[n:]


def small_pack(ddw, v512, v1024, loss_parts):
    rows, width = PACK_ROWS, 512
    n512, n1024 = len(VEC512), len(VEC1024)

    def body(*refs):
        ddw_ref = refs[0]
        a_refs = refs[1:1 + n512]
        b_refs = refs[1 + n512:1 + n512 + n1024]
        lp_ref, o_ref, p_ref = refs[1 + n512 + n1024:]
        p_ref[...] = jnp.zeros_like(p_ref)
        p_ref[0:32, :] = ddw_ref[...]
        p_ref[LOSS_ROW:LOSS_ROW + 1, 0:128] = jnp.sum(lp_ref[...], axis=0, keepdims=True) * 0.125
        for i, r in enumerate(a_refs):
            p_ref[32 + i:33 + i, :] = r[...]
        for i, r in enumerate(b_refs):
            base = 32 + n512 + 2 * i
            p_ref[base:base + 1, :] = r[:, 0:512]
            p_ref[base + 1:base + 2, :] = r[:, 512:1024]
        x, y, c, _ = _place()
        o_ref[4 * x + 2 * y + c] = p_ref[...]

    n_in = 2 + n512 + n1024
    return pl.pallas_call(
        body, name="small_pack", out_shape=jax.ShapeDtypeStruct((8, rows, width), F32),
        in_specs=[VMEM_SPEC] * n_in, out_specs=VMEM_SPEC,
        scratch_shapes=[pltpu.VMEM((rows, width), F32)],
    )(ddw, *[v512[n] for n in VEC512], *[v1024[n] for n in VEC1024], loss_parts)


def _small_copies(refs, send, recv):
    x, y, c, _ = _place()
    mine = refs[0].at[4 * x + 2 * y + c]
    peers = [(1 - x if k & 4 else x, 1 - y if k & 2 else y, 1 - c if k & 1 else c) for k in range(1, 8)]
    return [_rcopy(mine, mine, send.at[i], recv.at[i], dev) for i, dev in enumerate(peers)]


def _row_block(r):
    for tr in (512, 352, 256, 128):
        if r % tr == 0:
            return tr
    return r


def add_halves(g, recv, name):
    _, _, r, w = g.shape
    tr = _row_block(r)

    def body(g_ref, r_ref, ob_ref, own_ref):
        k = pl.program_id(1)
        me = 2 * lax.axis_index("x") + lax.axis_index("y")
        t = g_ref[0, 0].astype(F32) + r_ref[0].astype(F32)
        ob_ref[0] = t.astype(MM)
        mine = jnp.where(k == me, t, 0.0)

        @pl.when(k == 0)
        def _():
            own_ref[...] = mine

        @pl.when(k != 0)
        def _():
            own_ref[...] += mine

    return pl.pallas_call(
        body, name=name, grid=(r // tr, N_CHIPS),
        in_specs=[pl.BlockSpec((1, 1, tr, w), lambda i, k: (k, lax.axis_index("c"), i, 0)),
                  pl.BlockSpec((1, tr, w), lambda i, k: (k, i, 0))],
        out_specs=[pl.BlockSpec((1, tr, w), lambda i, k: (k, i, 0)),
                   pl.BlockSpec((tr, w), lambda i, k: (i, 0))],
        out_shape=(jax.ShapeDtypeStruct((N_CHIPS, r, w), MM), jax.ShapeDtypeStruct((r, w), F32)),
        compiler_params=_params(("arbitrary", "arbitrary")),
    )(g, recv)


def sum_parts(own, rin, after, name):
    _, r, w = rin.shape
    tr = _row_block(r)

    def body(o_ref, r_ref, after_ref, out_ref):
        out_ref[...] = ((o_ref[...] + r_ref[0].astype(F32)) + r_ref[1].astype(F32)) + r_ref[2].astype(F32)

    return pl.pallas_call(
        body, name=name, grid=(r // tr,), out_shape=jax.ShapeDtypeStruct((r, w), F32),
        in_specs=[pl.BlockSpec((tr, w), lambda i: (i, 0)), pl.BlockSpec((3, tr, w), lambda i: (0, i, 0)),
                  TOKEN_SPEC],
        out_specs=pl.BlockSpec((tr, w), lambda i: (i, 0)),
        compiler_params=_params(("arbitrary",)),
    )(own, rin, after)


def sum_partials(p, land, after, name):
    _, _, r, w = p.shape
    tr = _row_block(r)

    def body(p_ref, l_ref, after_ref, out_ref):
        total = p_ref[0, 0].astype(F32)
        for slot in (6, 0, 3, 1, 4, 2, 5):
            total = total + l_ref[slot].astype(F32)
        out_ref[...] = total

    def own(i):
        return 2 * lax.axis_index("x") + lax.axis_index("y"), lax.axis_index("c"), i, 0

    return pl.pallas_call(
        body, name=name, grid=(r // tr,), out_shape=jax.ShapeDtypeStruct((r, w), F32),
        in_specs=[pl.BlockSpec((1, 1, tr, w), own), pl.BlockSpec((7, tr, w), lambda i: (0, i, 0)), TOKEN_SPEC],
        out_specs=pl.BlockSpec((tr, w), lambda i: (i, 0)),
        compiler_params=_params(("arbitrary",)),
    )(p, land, after)


def _adamw_math(w, g, m, v):
    mn = ADAM_B1 * m + (1.0 - ADAM_B1) * g
    vn = ADAM_B2 * v + (1.0 - ADAM_B2) * (g * g)
    m_hat = mn / (1.0 - ADAM_B1 ** ADAM_STEP)
    v_hat = vn / (1.0 - ADAM_B2 ** ADAM_STEP)
    return -ADAM_LR * (m_hat / (jnp.sqrt(v_hat) + ADAM_EPS) + ADAM_WD * w), mn, vn


def adamw(w, mine, other, m, v, name):
    r, c = w.shape
    rh = r // 2
    tr = _row_block(rh)
    if c >= 1024 and tr % 512 == 0:
        tr = 256
    nb = rh // tr

    def body(w_ref, a_ref, b_ref, m_ref, v_ref, go_ref, d_ref, mo_ref, vo_ref):
        gv = jnp.where(lax.axis_index("c") == pl.program_id(0), a_ref[...], b_ref[...])
        go_ref[...] = gv
        d_ref[...], mo_ref[...], vo_ref[...] = _adamw_math(w_ref[...], gv, m_ref[...], v_ref[...])

    def half(of_sibling):
        def index(h, i):
            owner = lax.axis_index("c")
            owner = 1 - owner if of_sibling else owner
            return jnp.where(h == owner, i, jnp.where(h < owner, 0, nb - 1)), 0
        return pl.BlockSpec((tr, c), index)

    spec = pl.BlockSpec((tr, c), lambda h, i: (h * nb + i, 0))
    out = jax.ShapeDtypeStruct((r, c), F32)
    return pl.pallas_call(
        body, name=name, grid=(2, nb), out_shape=(out, out, out, out),
        in_specs=[spec, half(False), half(True), spec, spec], out_specs=[spec] * 4,
        compiler_params=_params(("arbitrary", "arbitrary")),
    )(w, mine, other, m, v)


def adamw_small(packs, params, after):
    names = list(params)
    flat = [a for n in names for a in params[n]]

    def body(*refs):
        p_ref = refs[0]
        ins = refs[1:1 + 3 * len(names)]
        loss_ref, g_ref = refs[2 + 3 * len(names):4 + 3 * len(names)]
        outs = refs[4 + 3 * len(names):]
        total = p_ref[0]
        for d in range(1, 8):
            total = total + p_ref[d]
        g_ref[...] = total
        loss_ref[...] = g_ref[LOSS_ROW:LOSS_ROW + 1, 0:1]
        me = 2 * lax.axis_index("x") + lax.axis_index("y")
        for i, n in enumerate(names):
            w_ref, m_ref, v_ref = ins[3 * i:3 * i + 3]
            go_ref, d_ref, mo_ref, vo_ref = outs[4 * i:4 * i + 4]
            if n == "conv_dw_w":
                gv = jnp.zeros((CONV_WIDTH, 128), F32)
                for k in range(N_CHIPS):
                    gv = gv + jnp.where(me == k, g_ref[0:CONV_WIDTH, 128 * k:128 * (k + 1)], 0.0)
            elif n in VEC512:
                r0 = 32 + VEC512.index(n)
                gv = g_ref[r0:r0 + 1, :]
            else:
                r0 = 32 + len(VEC512) + 2 * VEC1024.index(n)
                gv = jnp.concatenate([g_ref[r0:r0 + 1, :], g_ref[r0 + 1:r0 + 2, :]], axis=1)
            go_ref[...] = gv
            d_ref[...], mo_ref[...], vo_ref[...] = _adamw_math(w_ref[...], gv, m_ref[...], v_ref[...])

    out_shape = [jax.ShapeDtypeStruct((1, 1), F32), jax.ShapeDtypeStruct(packs.shape[1:], F32)]
    out_shape += [jax.ShapeDtypeStruct(params[n][0].shape, F32) for n in names for _ in range(4)]
    res = pl.pallas_call(
        body, name="adamw_small", out_shape=out_shape,
        in_specs=[VMEM_SPEC] * (2 + len(flat)), out_specs=[VMEM_SPEC] * len(out_shape),
        compiler_params=_params(),
    )(packs, *flat, after)
    return res[0], res[1], {n: res[2 + 4 * i:6 + 4 * i] for i, n in enumerate(names)}


REST = ("w_ffn_up", "w_ffn_down", "w_out", "w_conv_branch", "w_att_branch")
VEC512 = ("conv_dw_b", "conv_ln_g", "conv_ln_b")
VEC1024 = ("norm_mix_pre", "b_conv_branch", "norm_mix_post", "norm_ffn_pre", "norm_ffn_post")
PACK_ROWS = 48
LOSS_ROW = 47


def kernel(x, norm_mix_pre, w_in, conv_dw_w, conv_dw_b, conv_ln_g, conv_ln_b, w_conv_branch, b_conv_branch, w_att_branch, w_out, norm_mix_post, norm_ffn_pre, w_ffn_up, w_ffn_down, norm_ffn_post, loss_target, m_norm_mix_pre, m_w_in, m_conv_dw_w, m_conv_dw_b, m_conv_ln_g, m_conv_ln_b, m_w_conv_branch, m_b_conv_branch, m_w_att_branch, m_w_out, m_norm_mix_post, m_norm_ffn_pre, m_w_ffn_up, m_w_ffn_down, m_norm_ffn_post, v_norm_mix_pre, v_w_in, v_conv_dw_w, v_conv_dw_b, v_conv_ln_g, v_conv_ln_b, v_w_conv_branch, v_b_conv_branch, v_w_att_branch, v_w_out, v_norm_mix_post, v_norm_ffn_pre, v_w_ffn_up, v_w_ffn_down, v_norm_ffn_post):
    weights = dict(norm_mix_pre=norm_mix_pre, w_in=w_in, conv_dw_w=conv_dw_w, conv_dw_b=conv_dw_b, conv_ln_g=conv_ln_g, conv_ln_b=conv_ln_b, w_conv_branch=w_conv_branch, b_conv_branch=b_conv_branch, w_att_branch=w_att_branch, w_out=w_out, norm_mix_post=norm_mix_post, norm_ffn_pre=norm_ffn_pre, w_ffn_up=w_ffn_up, w_ffn_down=w_ffn_down, norm_ffn_post=norm_ffn_post)
    mom = dict(norm_mix_pre=m_norm_mix_pre, w_in=m_w_in, conv_dw_w=m_conv_dw_w, conv_dw_b=m_conv_dw_b, conv_ln_g=m_conv_ln_g, conv_ln_b=m_conv_ln_b, w_conv_branch=m_w_conv_branch, b_conv_branch=m_b_conv_branch, w_att_branch=m_w_att_branch, w_out=m_w_out, norm_mix_post=m_norm_mix_post, norm_ffn_pre=m_norm_ffn_pre, w_ffn_up=m_w_ffn_up, w_ffn_down=m_w_ffn_down, norm_ffn_post=m_norm_ffn_post)
    var = dict(norm_mix_pre=v_norm_mix_pre, w_in=v_w_in, conv_dw_w=v_conv_dw_w, conv_dw_b=v_conv_dw_b, conv_ln_g=v_conv_ln_g, conv_ln_b=v_conv_ln_b, w_conv_branch=v_w_conv_branch, b_conv_branch=v_b_conv_branch, w_att_branch=v_w_att_branch, w_out=v_w_out, norm_mix_post=v_norm_mix_post, norm_ffn_pre=v_norm_ffn_pre, w_ffn_up=v_w_ffn_up, w_ffn_down=v_w_ffn_down, norm_ffn_post=v_norm_ffn_post)
    order = list(weights)
    grads, deltas, new_m, new_v = {}, {}, {}, {}
    xs = x.reshape(SEQ, D_MODEL)
    tgt = loss_target.reshape(SEQ, D_MODEL)
    row = lambda a: a.reshape(1, -1)
    g1, g2, g3, g4 = (row(weights[n]) for n in ("norm_mix_pre", "norm_mix_post", "norm_ffn_pre", "norm_ffn_post"))
    ln_g, ln_b = row(conv_ln_g), row(conv_ln_b)

    summed, from_chips = {}, {}

    def core_sums(names, state, after, tag):
        own, from_sibling = sibling_wait(state, after, True, tag)
        for n, g, r in zip(names, own, from_sibling):
            summed[n] = add_halves(g, r, "add_" + n)

    def chip_sums(names, after):
        return [sum_parts(summed[n][1], from_chips[n], after, "sum_" + n) for n in names]

    def optimize(names, state, after, tag):
        mine, other = sibling_wait(state, after, False, tag)
        for n, a, b in zip(names, mine, other):
            grads[n], deltas[n], new_m[n], new_v[n] = adamw(weights[n], a, b, mom[n], var[n], "adamw_" + n)

    w_in_g, dw_g, *rest = all_gather_weights([w_in], conv_dw_w, [weights[n] for n in REST])
    w_dw_full = jnp.concatenate([dw_g[k] for k in range(N_CHIPS)], axis=1)
    rest_shapes = [weights[n].shape for n in REST]
    over_ici = _gather_copies(rest_shapes, 1, to_both_cores=(2, 3, 4))
    state, token = split_start("gather_start", rest, 3 * (len(REST) + 3), over_ici)
    h1, ci, q, k, v, gc, ga = in_proj_fwd(xs, g1, w_in_g, token)
    u1, u3 = conv_fwd(ci, w_dw_full, row(conv_dw_b), ln_g, ln_b)
    att, rc = attn_fwd(q, k, v)
    rest = split_wait("gather_wait", state, [att], over_ici)
    w_out_g, w_cb_g, w_ab_g = rest[2:]
    w_out_g = w_out_g.reshape(D_MODEL, D_MODEL)
    to_sibling = _gather_copies(rest_shapes[:2], 2)
    state, token = split_start("pass_start", rest[:2], 3 * 2, to_sibling)
    merged, mix, x2, h2 = mix_fwd(u3, att, gc, ga, xs, w_cb_g, row(b_conv_branch), w_ab_g, w_out_g, g2, g3, token)
    w_up_g, w_down_g = split_wait("pass_wait", state, [h2], to_sibling)
    w_down_g = w_down_g.reshape(D_FF, D_MODEL)
    gate, up, act = ffn_up_fwd(h2, w_up_g)
    dff, dy, loss_parts, dg4, dgu = ffn_down_loss(act, w_down_g, x2, tgt, g4, gate, up)

    ffn_grads = [weight_grad(h2, dgu, "dw_ffn_up", True), weight_grad(act, dff, "dw_ffn_down", False, tk=UP_SHARD)]
    to_ffn, token = sibling_start(ffn_grads, True, "dw_ffn")
    dx2, dmix, dg3, dg2, dco, dao, dg, du3, datt, dbcb = ffn_merge_bwd(
        dgu, w_up_g, x2, mix, dy, g3, g2, w_out_g, gc, ga, u3, att, w_cb_g, row(b_conv_branch), w_ab_g, token)
    mix_grads = weight_grad_mix(merged, dmix, u3, dco, att, dao)
    core_sums(REST[:2], to_ffn, [mix_grads[0]], "dw_ffn")
    state, token = scatter_start([summed[n][0] for n in REST[:2]], mix_grads)
    dci, ddw, dbdw, dlng, dlnb = conv_bwd(du3, u1, ci, w_dw_full, ln_g, ln_b, token)
    dqkv = attn_bwd(q, k, v, datt, rc, token)
    ffn_from_chips, mix_grads, mix_from_all = scatter_wait(state, [dci, dqkv], 2, len(mix_grads))
    from_chips.update(zip(REST[:2], ffn_from_chips))
    dproj = (dci, dqkv, dg)
    to_in, token = sibling_start([weight_grad_in(h1, dproj)], True, "dw_in")
    grad_x, dg1 = in_proj_bwd(dproj, w_in_g, xs, dx2, g1, token)
    v512 = dict(conv_dw_b=dbdw, conv_ln_g=dlng, conv_ln_b=dlnb)
    v1024 = dict(norm_mix_pre=dg1, b_conv_branch=dbcb, norm_mix_post=dg2, norm_ffn_pre=dg3, norm_ffn_post=dg4)
    packs = small_pack(ddw, v512, v1024, loss_parts)
    core_sums(("w_in",), to_in, [packs], "dw_in")
    to_chips = summed["w_in"][0]
    landing = lax.empty((3,) + to_chips.shape[1:], to_chips.dtype)

    def scatter_and_packs(refs, send, recv):
        return (_scatter_copies(1)(refs[:2], send, recv)
                + _small_copies(refs[2:], _Shifted(send, 3), _Shifted(recv, 3)))

    state, token = split_start("scatter_start_w_in", [to_chips, landing, packs], 3 + 7, scatter_and_packs)
    swap_up, token = sibling_start(chip_sums(REST[:1], token), False, "sum_ffn_up")
    rest_sums = chip_sums(REST[1:2], token) + [sum_partials(p, r, token, "sum_" + n)
                                               for n, p, r in zip(REST[2:], mix_grads, mix_from_all)]
    swap_rest, token = sibling_start(rest_sums, False, "sum_rest")
    optimize(REST[:1], swap_up, [token], "sum_ffn_up")
    optimize(REST[1:], swap_rest, [new_v["w_ffn_up"]], "sum_rest")
    _, from_chips["w_in"], packs = split_wait("scatter_wait_w_in", state, [new_v[n] for n in REST], scatter_and_packs)
    swap_in, token = sibling_start(chip_sums(("w_in",), token), False, "sum_w_in")
    as_rows = lambda n, a: a if n == "conv_dw_w" else a.reshape(1, -1)
    small_names = ("conv_dw_w",) + VEC512 + VEC1024
    loss, gsum, small = adamw_small(
        packs, {n: tuple(as_rows(n, d[n]) for d in (weights, mom, var)) for n in small_names}, token)
    optimize(("w_in",), swap_in, [gsum], "sum_w_in")
    for n in small_names:
        grads[n], deltas[n], new_m[n], new_v[n] = (a.reshape(weights[n].shape) for a in small[n])

    return (loss.reshape(()), grad_x.reshape(1, SEQ, D_MODEL),*[grads[n] for n in order], *[deltas[n] for n in order],
            *[new_m[n] for n in order], *[new_v[n] for n in order])
```

```python
import jax
import jax.numpy as jnp
from jax import lax
from jax.experimental import pallas as pl
from jax.experimental.pallas import tpu as pltpu

F32 = jnp.float32
MM = jnp.bfloat16

SEQ = 2048
D_MODEL = 1024
CONV_DIM = 512
ATT_DIM = 512
CONV_WIDTH = 31
D_FF = 2816
IN_COLS = 2 * CONV_DIM + 3 * ATT_DIM + 2 * D_MODEL
N_CHIPS = 4
IN_SHARD = IN_COLS // N_CHIPS
UP_SHARD = 2 * D_FF // N_CHIPS
BR_SHARD = D_MODEL // N_CHIPS
EPS = 1e-6
ATT_SCALE = 0.125

TM = 256
GLU_ROWS = 256
TQ = 128
CONV_TILE = 64
CONV_WIN = CONV_TILE + 32
VMEM_LIMIT = 56 * 1024 * 1024

ADAM_LR = 0.001
ADAM_B1 = 0.9
ADAM_B2 = 0.999
ADAM_EPS = 1e-08
ADAM_WD = 0.01
ADAM_STEP = 10

MESH = pl.DeviceIdType.MESH
ANY = pl.BlockSpec(memory_space=pl.ANY)
VMEM_SPEC = pl.BlockSpec(memory_space=pltpu.VMEM)

NT_DIMS = (((1,), (1,)), ((), ()))
TN_DIMS = (((0,), (0,)), ((), ()))

IN_PIECES = (("ci", 0, 1024), ("q", 1024, 1536), ("k", 1536, 2048), ("v", 2048, 2560),
             ("gc", 2560, 3584), ("ga", 3584, 4608))


def _params(sem=None, vmem=VMEM_LIMIT):
    return pltpu.CompilerParams(dimension_semantics=sem, vmem_limit_bytes=vmem)


def _dot(a, b):
    return jnp.dot(a, b, preferred_element_type=F32)


def _dot_nt(a, b):
    return lax.dot_general(a, b, NT_DIMS, preferred_element_type=F32)


def _dot_tn(a, b):
    return lax.dot_general(a, b, TN_DIMS, preferred_element_type=F32)


def _sigmoid(x):
    return 1.0 / (1.0 + jnp.exp(-x))


def _rms(x):
    r = lax.rsqrt(jnp.mean(x * x, axis=-1, keepdims=True) + EPS)
    return x * r, r


def _rms_bwd(dy_g, n, r):
    return r * (dy_g - n * jnp.mean(dy_g * n, axis=-1, keepdims=True))


def _row_tile_spec(width, tm=TM):
    return pl.BlockSpec((tm, width), lambda i: (i, 0))


def _full_spec(shape):
    nd = len(shape)
    return pl.BlockSpec(shape, lambda *_: (0,) * nd)


def _weight_spec(shape):
    nd = len(shape)
    return pl.BlockSpec(shape, lambda *_: (0,) * nd, pipeline_mode=pl.Buffered(1))


def _acc_rows(ref, val, first):
    @pl.when(first)
    def _():
        ref[...] = val

    @pl.when(jnp.logical_not(first))
    def _():
        ref[...] += val


TOKEN_SPEC = pl.BlockSpec((8, 128), lambda *_: (0, 0))


def in_proj_fwd(x, g1, w_in_g, after):
    def body(x_ref, g_ref, w_ref, after_ref, h_ref, ci_ref, q_ref, k_ref, v_ref, gc_ref, ga_ref):
        n, _ = _rms(x_ref[...])
        h = (n * g_ref[...]).astype(MM)
        h_ref[...] = h
        outs = dict(ci=ci_ref, q=q_ref, k=k_ref, v=v_ref, gc=gc_ref, ga=ga_ref)
        for j in range(N_CHIPS):
            p = _dot(h, w_ref[j])
            g0 = j * IN_SHARD
            for name, s, e in IN_PIECES:
                lo, hi = max(s, g0), min(e, g0 + IN_SHARD)
                if lo < hi:
                    ref = outs[name]
                    part = p[:, lo - g0:hi - g0]
                    if name == "q":
                        part = part * ATT_SCALE
                    ref[:, lo - s:hi - s] = part.astype(ref.dtype)

    out_shape = [
        jax.ShapeDtypeStruct((SEQ, D_MODEL), MM),
        jax.ShapeDtypeStruct((SEQ, 2 * CONV_DIM), F32),
        jax.ShapeDtypeStruct((SEQ, ATT_DIM), MM),
        jax.ShapeDtypeStruct((SEQ, ATT_DIM), MM),
        jax.ShapeDtypeStruct((SEQ, ATT_DIM), MM),
        jax.ShapeDtypeStruct((SEQ, D_MODEL), F32),
        jax.ShapeDtypeStruct((SEQ, D_MODEL), F32),
    ]
    return pl.pallas_call(
        body, name="in_proj_fwd", grid=(SEQ // TM,), out_shape=out_shape,
        in_specs=[_row_tile_spec(D_MODEL), _full_spec((1, D_MODEL)), _weight_spec(w_in_g.shape), TOKEN_SPEC],
        out_specs=[_row_tile_spec(s.shape[1]) for s in out_shape],
        compiler_params=_params(("arbitrary",)),
    )(x, g1, w_in_g, after)


LANE_GROUPS = [slice(g, g + 128) for g in range(0, CONV_DIM, 128)]
NORM_ROWS = 16


def _shifted_windows(src_ref, t0, cols, offsets):
    win = src_ref[pl.ds(t0, CONV_WIN), cols]
    for rot in range(8):
        ms = [m for m in offsets if m % 8 == rot]
        if ms:
            shifted = win if rot == 0 else pltpu.roll(win, CONV_WIN - rot, 0)
            for m in ms:
                yield m, shifted[m - rot:m - rot + CONV_TILE, :]


def _shifted_sum(src_ref, t0, cols, w_ref, offset_of_tap):
    tap_at = {offset_of_tap(j): j for j in range(CONV_WIDTH)}
    acc = None
    for m, rows in _shifted_windows(src_ref, t0, cols, sorted(tap_at)):
        t = w_ref[tap_at[m]:tap_at[m] + 1, cols] * rows
        acc = t if acc is None else acc + t
    return acc


def _fetch(srcs, dsts, sems):
    copies = [pltpu.make_async_copy(s, d, sems.at[i]) for i, (s, d) in enumerate(zip(srcs, dsts))]
    for cp in copies:
        cp.start()
    return copies


def _row_chunks(src, dst, sems):
    def chunk(i):
        t0 = i * GLU_ROWS
        rows = pl.ds(t0 if isinstance(i, int) else pl.multiple_of(t0, GLU_ROWS), GLU_ROWS)
        return pltpu.make_async_copy(src.at[rows, :], dst.at[rows, :], sems.at[i])

    for i in range(SEQ // GLU_ROWS):
        chunk(i).start()
    return chunk


def _glu_into(ci_chunk, ci_ref, upad_ref):
    upad_ref[0:32, :] = jnp.zeros((32, CONV_DIM), F32)

    def step(i, c):
        ci_chunk(i).wait()
        t0 = pl.multiple_of(i * GLU_ROWS, GLU_ROWS)
        a = ci_ref[pl.ds(t0, GLU_ROWS), 0:CONV_DIM]
        b = ci_ref[pl.ds(t0, GLU_ROWS), CONV_DIM:2 * CONV_DIM]
        upad_ref[pl.ds(t0 + 32, GLU_ROWS), :] = a * _sigmoid(b)
        return c

    lax.fori_loop(0, SEQ // GLU_ROWS, step, 0)


def _layernorm_parts(u1):
    mu = jnp.mean(u1, axis=-1, keepdims=True)
    xc = u1 - mu
    rstd = lax.rsqrt(jnp.mean(xc * xc, axis=-1, keepdims=True) + EPS)
    return xc * rstd, rstd


def conv_fwd(ci, w_dw, b_dw, ln_g, ln_b):
    def body(ci_hbm, w_ref, b_ref, g_ref, bb_ref, u1_ref, u3_ref, upad_ref, ci_ref, sems):
        _glu_into(_row_chunks(ci_hbm, ci_ref, sems), ci_ref, upad_ref)

        def step(i, c):
            t0 = pl.multiple_of(i * CONV_TILE, CONV_TILE)
            for cols in LANE_GROUPS:
                u1_ref[pl.ds(t0, CONV_TILE), cols] = (_shifted_sum(upad_ref, t0, cols, w_ref, lambda j: j + 2)
                                                      + b_ref[:, cols])
            for r in range(0, CONV_TILE, NORM_ROWS):
                rows = pl.ds(t0 + r, NORM_ROWS)
                xh, _ = _layernorm_parts(u1_ref[rows, :])
                u2 = xh * g_ref[...] + bb_ref[...]
                u3_ref[rows, :] = (u2 * _sigmoid(u2)).astype(MM)
            return c

        lax.fori_loop(0, SEQ // CONV_TILE, step, 0)

    return pl.pallas_call(
        body, name="conv_fwd",
        out_shape=[jax.ShapeDtypeStruct((SEQ, CONV_DIM), F32), jax.ShapeDtypeStruct((SEQ, CONV_DIM), MM)],
        in_specs=[ANY] + [VMEM_SPEC] * 4, out_specs=[VMEM_SPEC] * 2,
        scratch_shapes=[pltpu.VMEM((SEQ + 32, CONV_DIM), F32), pltpu.VMEM(ci.shape, ci.dtype),
                        pltpu.SemaphoreType.DMA((SEQ // GLU_ROWS,))],
        compiler_params=_params(),
    )(ci, w_dw, b_dw, ln_g, ln_b)


def _softplus(z):
    return jnp.maximum(z, 0.0) + jnp.log(1.0 + jnp.exp(-jnp.abs(z)))


def _cumsum_weights(suffix, with_total):
    n = 256 if with_total else 128
    r = lax.broadcasted_iota(jnp.int32, (128, n), 0)
    c = lax.broadcasted_iota(jnp.int32, (128, n), 1)
    tri = (r >= c) if suffix else (r <= c)
    return jnp.logical_or(tri, c >= 128).astype(MM)


NO_SCORE = -1e30
N_KB = SEQ // TQ


def _score_bias(lane, row, i, j):
    keep = jnp.logical_and(i >= 0, jnp.logical_or(j < i, lane < row))
    return jnp.where(keep, 0.0, NO_SCORE)


def _block_pipeline(n_stages, descending, step, on_query_block=None):
    n_lag = n_stages - 1
    none = jnp.int32(-1)

    def shift(cur, lag):
        step([cur] + [(lag[2 * s], lag[2 * s + 1]) for s in range(n_lag)])
        return (cur[0], cur[1]) + tuple(lag[:-2])

    def outer(i, lag):
        if on_query_block is not None:
            on_query_block(i)

        def inner(n, lag):
            return shift((i, i - n if descending else n), lag)
        return lax.fori_loop(0, i + 1, inner, lag)

    lag = lax.fori_loop(0, N_KB, outer, (none,) * (2 * n_lag))
    lax.fori_loop(0, n_lag, lambda n, lag: shift((none, none), lag), lag)


def _head_masks():
    lane = lax.broadcasted_iota(jnp.int32, (TQ, 128), 1)
    row = lax.broadcasted_iota(jnp.int32, (TQ, 128), 0)
    return lane, row, lane < 64


def _pick_head(x, head0, h):
    zero = jnp.zeros_like(x)
    return jnp.where(head0, x, zero) if h == 0 else jnp.where(head0, zero, x)


N_PAIRS = ATT_DIM // 128


def _split_heads(src_ref, dst_ref):
    _, _, head0 = _head_masks()

    def block(b, c):
        r0 = pl.multiple_of(b * TQ, TQ)
        d0 = pl.multiple_of(b * 2 * TQ, 2 * TQ)
        for p in range(N_PAIRS):
            x = src_ref[pl.ds(r0, TQ), 128 * p:128 * (p + 1)]
            for h in range(2):
                dst_ref[p, pl.ds(d0 + TQ * h, TQ), :] = _pick_head(x, head0, h)
        return c

    lax.fori_loop(0, N_KB, block, 0)


def attn_fwd(q, k, v):
    def body(q_hbm, k_hbm, v_hbm, o_ref, rc_ref, acc_ref, r_ref, z_ref, spb_ref, ab_ref, qm_ref, vm_ref,
             q_ref, k_ref, v_ref, sems):
        arrive = _fetch((q_hbm, v_hbm, k_hbm), (q_ref, v_ref, k_ref), sems)
        lane, row, _ = _head_masks()
        w = _cumsum_weights(suffix=True, with_total=True)
        acc_ref[...] = jnp.zeros_like(acc_ref)
        r_ref[...] = jnp.zeros_like(r_ref)
        rc_ref[...] = jnp.zeros_like(rc_ref)
        z_ref[...] = jnp.full(z_ref.shape, NO_SCORE, F32)
        spb_ref[...] = jnp.zeros_like(spb_ref)
        ab_ref[...] = jnp.zeros_like(ab_ref)
        arrive[0].wait()
        _split_heads(q_ref, qm_ref)
        arrive[1].wait()
        _split_heads(v_ref, vm_ref)
        arrive[2].wait()

        def step(pairs):
            (i1, j1), (i2, j2), (i3, j3) = pairs
            k1, q2, q3 = (pl.multiple_of(jnp.maximum(b, 0) * TQ, TQ) for b in (j1, i2, i3))
            q1, k3 = (pl.multiple_of(jnp.maximum(b, 0) * 2 * TQ, 2 * TQ) for b in (i1, j3))
            bias1 = _score_bias(lane, row, i1, j1)
            first2 = j2 == i2
            rc_rows = rc_ref[pl.ds(q2, TQ), :]
            for p in range(N_PAIRS):
                cols = slice(128 * p, 128 * (p + 1))
                kb = k_ref[pl.ds(k1, TQ), cols]
                acc_ref[pl.ds(q3, TQ), cols] += _dot(ab_ref[p], vm_ref[p, pl.ds(k3, 2 * TQ), :])
                for h in range(2):
                    hh = 2 * p + h
                    r = _dot(spb_ref[hh], w)
                    r_in = jnp.where(first2, 0.0, r_ref[hh])
                    ab_ref[p, :, 128 * h:128 * (h + 1)] = jnp.exp(z_ref[hh] - (r[:, :128] + r_in)).astype(MM)
                    rc_rows = jnp.where(jnp.logical_and(lane == 16 * hh + j2, i2 >= 0), r_in, rc_rows)
                    r_ref[hh] = r_in + r[:, 128:]
                    z = _dot_nt(qm_ref[p, pl.ds(q1 + TQ * h, TQ), :], kb) + bias1
                    z_ref[hh] = z
                    spb_ref[hh] = _softplus(z).astype(MM)
            rc_ref[pl.ds(q2, TQ), :] = rc_rows

        _block_pipeline(3, True, step)
        o_ref[...] = acc_ref[...].astype(MM)

    return pl.pallas_call(
        body, name="attn_fwd",
        out_shape=[jax.ShapeDtypeStruct((SEQ, ATT_DIM), MM), jax.ShapeDtypeStruct((SEQ, 128), F32)],
        in_specs=[ANY] * 3, out_specs=[VMEM_SPEC] * 2,
        scratch_shapes=[pltpu.VMEM((SEQ, ATT_DIM), F32), pltpu.VMEM((8, TQ, 128), F32),
                        pltpu.VMEM((8, TQ, 128), F32), pltpu.VMEM((8, TQ, 128), MM),
                        pltpu.VMEM((N_PAIRS, TQ, 256), MM), pltpu.VMEM((N_PAIRS, 2 * SEQ, 128), MM),
                        pltpu.VMEM((N_PAIRS, 2 * SEQ, 128), MM)]
                       + [pltpu.VMEM(a.shape, a.dtype) for a in (q, k, v)] + [pltpu.SemaphoreType.DMA((3,))],
        compiler_params=_params(),
    )(q, k, v)


def _branch_outputs(u_ref, a_ref, wcb_ref, bcb_ref, wab_ref):
    u = u_ref[...]
    a = a_ref[...]
    co = jnp.concatenate([_dot(u, wcb_ref[j]) for j in range(N_CHIPS)], axis=1) + bcb_ref[...]
    ao = jnp.concatenate([_dot(a, wab_ref[j]) for j in range(N_CHIPS)], axis=1)
    return co, ao


def mix_fwd(u3, att, gc, ga, x, w_cb_g, b_cb, w_ab_g, w_out_g, g2, g3, after):
    def body(u_ref, a_ref, gc_ref, ga_ref, x_ref, wcb_ref, bcb_ref, wab_ref, wout_ref, g2_ref, g3_ref, after_ref,
             mg_ref, mix_ref, x2_ref, h2_ref):
        co, ao = _branch_outputs(u_ref, a_ref, wcb_ref, bcb_ref, wab_ref)
        merged = (_sigmoid(gc_ref[...]) * co + _sigmoid(ga_ref[...]) * ao).astype(MM)
        mg_ref[...] = merged
        mix = _dot(merged, wout_ref[...])
        mix_ref[...] = mix
        n2, _ = _rms(mix)
        x2 = x_ref[...] + n2 * g2_ref[...]
        x2_ref[...] = x2
        n3, _ = _rms(x2)
        h2_ref[...] = (n3 * g3_ref[...]).astype(MM)

    out_shape = [
        jax.ShapeDtypeStruct((SEQ, D_MODEL), MM), jax.ShapeDtypeStruct((SEQ, D_MODEL), F32),
        jax.ShapeDtypeStruct((SEQ, D_MODEL), F32), jax.ShapeDtypeStruct((SEQ, D_MODEL), MM),
    ]
    vec = _full_spec((1, D_MODEL))
    return pl.pallas_call(
        body, name="mix_fwd", grid=(SEQ // TM,), out_shape=out_shape,
        in_specs=[_row_tile_spec(CONV_DIM), _row_tile_spec(ATT_DIM), _row_tile_spec(D_MODEL),
                  _row_tile_spec(D_MODEL), _row_tile_spec(D_MODEL), _weight_spec(w_cb_g.shape), vec,
                  _weight_spec(w_ab_g.shape), _weight_spec(w_out_g.shape), vec, vec, TOKEN_SPEC],
        out_specs=[_row_tile_spec(D_MODEL)] * 4,
        compiler_params=_params(("arbitrary",)),
    )(u3, att, gc, ga, x, w_cb_g, b_cb, w_ab_g, w_out_g, g2, g3, after)


def ffn_loss(h2, w_up_g, w_down_g, x2, target, g4):
    def body(h_ref, wu_ref, wd_ref, x2_ref, t_ref, g_ref, act_ref, dff_ref, dy_ref, loss_ref, dg_ref, dgu_ref):
        h = h_ref[...]
        gate = jnp.concatenate([_dot(h, wu_ref[0]), _dot(h, wu_ref[1])], axis=1)
        up = jnp.concatenate([_dot(h, wu_ref[2]), _dot(h, wu_ref[3])], axis=1)
        sg = _sigmoid(gate)
        act = (gate * sg * up).astype(MM)
        act_ref[...] = act
        ff = _dot(act, wd_ref[...])
        n4, r4 = _rms(ff)
        g4v = g_ref[...]
        err = x2_ref[...] + n4 * g4v - t_ref[...]
        row_loss = jnp.mean(err * err, axis=-1, keepdims=True)
        loss_ref[...] = jnp.zeros((8, 128), F32) + 0.5 * jnp.sum(row_loss, axis=0, keepdims=True)
        dy = err * (1.0 / D_MODEL)
        dy_ref[...] = dy
        dff = _rms_bwd(dy * g4v, n4, r4).astype(MM)
        dff_ref[...] = dff
        _acc_rows(dg_ref, jnp.sum(dy * n4, axis=0, keepdims=True), pl.program_id(0) == 0)
        dact = _dot_nt(dff, wd_ref[...])
        dgu_ref[:, 0:D_FF] = (dact * up * (sg * (1.0 + gate * (1.0 - sg)))).astype(MM)
        dgu_ref[:, D_FF:2 * D_FF] = (dact * (gate * sg)).astype(MM)

    nt = SEQ // TM
    vec = _full_spec((1, D_MODEL))
    return pl.pallas_call(
        body, name="ffn_loss", grid=(nt,),
        out_shape=(jax.ShapeDtypeStruct((SEQ, D_FF), MM),
                   jax.ShapeDtypeStruct((SEQ, D_MODEL), MM), jax.ShapeDtypeStruct((SEQ, D_MODEL), F32),
                   jax.ShapeDtypeStruct((nt * 8, 128), F32), jax.ShapeDtypeStruct((1, D_MODEL), F32),
                   jax.ShapeDtypeStruct((SEQ, 2 * D_FF), MM)),
        in_specs=[_row_tile_spec(D_MODEL), _weight_spec(w_up_g.shape), _weight_spec(w_down_g.shape),
                  _row_tile_spec(D_MODEL), _row_tile_spec(D_MODEL), vec],
        out_specs=[_row_tile_spec(D_FF), _row_tile_spec(D_MODEL), _row_tile_spec(D_MODEL),
                   pl.BlockSpec((8, 128), lambda i: (i, 0)), vec, _row_tile_spec(2 * D_FF)],
        compiler_params=_params(("arbitrary",)),
    )(h2, w_up_g, w_down_g, x2, target, g4)


def ffn_merge_bwd(dgu, w_up_g, x2, mix, dy, g3, g2, w_out_g, gc, ga, u3, att, w_cb_g, b_cb, w_ab_g, after):
    def body(dgu_ref, w_ref, x2_ref, mix_ref, dy_ref, g3_ref, g2_ref,
             wout_ref, gc_ref, ga_ref, u_ref, a_ref, wcb_ref, bcb_ref, wab_ref, after_ref,
             dx2_ref, dmix_ref, dg3_ref, dg2_ref, dco_ref, dao_ref, dg_ref, du3_ref, datt_ref, dbcb_ref):
        dh2 = None
        for j in range(N_CHIPS):
            t = _dot_nt(dgu_ref[:, j * UP_SHARD:(j + 1) * UP_SHARD], w_ref[j])
            dh2 = t if dh2 is None else dh2 + t
        first = pl.program_id(0) == 0
        n3, r3 = _rms(x2_ref[...])
        dx2 = dy_ref[...] + _rms_bwd(dh2 * g3_ref[...], n3, r3)
        dx2_ref[...] = dx2
        _acc_rows(dg3_ref, jnp.sum(dh2 * n3, axis=0, keepdims=True), first)
        n2, r2 = _rms(mix_ref[...])
        dmix = _rms_bwd(dx2 * g2_ref[...], n2, r2).astype(MM)
        dmix_ref[...] = dmix
        _acc_rows(dg2_ref, jnp.sum(dx2 * n2, axis=0, keepdims=True), first)

        dm = _dot_nt(dmix, wout_ref[...])
        co, ao = _branch_outputs(u_ref, a_ref, wcb_ref, bcb_ref, wab_ref)
        sgc = _sigmoid(gc_ref[...])
        sga = _sigmoid(ga_ref[...])
        dco = dm * sgc
        dao = dm * sga
        dg_ref[:, 0:D_MODEL] = (dm * co * (sgc * (1.0 - sgc))).astype(MM)
        dg_ref[:, D_MODEL:2 * D_MODEL] = (dm * ao * (sga * (1.0 - sga))).astype(MM)
        _acc_rows(dbcb_ref, jnp.sum(dco, axis=0, keepdims=True), pl.program_id(0) == 0)
        dco_ref[...] = dco.astype(MM)
        dao_ref[...] = dao.astype(MM)
        du3 = None
        datt = None
        for j in range(N_CHIPS):
            cols = slice(j * BR_SHARD, (j + 1) * BR_SHARD)
            t = _dot_nt(dco_ref[:, cols], wcb_ref[j])
            s = _dot_nt(dao_ref[:, cols], wab_ref[j])
            du3 = t if du3 is None else du3 + t
            datt = s if datt is None else datt + s
        du3_ref[...] = du3
        datt_ref[...] = datt.astype(MM)

    wide = _row_tile_spec(D_MODEL)
    vec = _full_spec((1, D_MODEL))
    wide_f32, wide_mm = jax.ShapeDtypeStruct((SEQ, D_MODEL), F32), jax.ShapeDtypeStruct((SEQ, D_MODEL), MM)
    vec_f32 = jax.ShapeDtypeStruct((1, D_MODEL), F32)
    return pl.pallas_call(
        body, name="ffn_merge_bwd", grid=(SEQ // TM,),
        out_shape=(wide_f32, wide_mm, vec_f32, vec_f32, wide_mm, wide_mm,
                   jax.ShapeDtypeStruct((SEQ, 2 * D_MODEL), MM),
                   jax.ShapeDtypeStruct((SEQ, CONV_DIM), F32), jax.ShapeDtypeStruct((SEQ, ATT_DIM), MM), vec_f32),
        in_specs=[_row_tile_spec(2 * D_FF), _weight_spec(w_up_g.shape), wide, wide, wide, vec, vec,
                  _weight_spec(w_out_g.shape), wide, wide, _row_tile_spec(CONV_DIM), _row_tile_spec(ATT_DIM),
                  _weight_spec(w_cb_g.shape), vec, _weight_spec(w_ab_g.shape), TOKEN_SPEC],
        out_specs=[wide, wide, vec, vec, wide, wide, _row_tile_spec(2 * D_MODEL), _row_tile_spec(CONV_DIM),
                   _row_tile_spec(ATT_DIM), vec],
        compiler_params=_params(("arbitrary",)),
    )(dgu, w_up_g, x2, mix, dy, g3, g2, w_out_g, gc, ga, u3, att, w_cb_g, b_cb, w_ab_g, after)


def conv_bwd(du3, u1, ci, w_dw, ln_g, ln_b, after):
    def body(du3_hbm, u1_hbm, ci_hbm, w_ref, g_ref, bb_ref, after_ref,
             dci_ref, dw_ref, dbdw_ref, dg_ref, db_ref, upad_ref, dpad_ref, dwacc_ref, vacc_ref,
             du3_ref, u1_ref, ci_ref, ci_sems, u1_sems, du3_sems):
        ci_chunk = _row_chunks(ci_hbm, ci_ref, ci_sems)
        u1_chunk = _row_chunks(u1_hbm, u1_ref, u1_sems)
        du3_chunk = _row_chunks(du3_hbm, du3_ref, du3_sems)
        _glu_into(ci_chunk, ci_ref, upad_ref)
        dpad_ref[SEQ:SEQ + 32, :] = jnp.zeros((32, CONV_DIM), F32)
        dwacc_ref[...] = jnp.zeros_like(dwacc_ref)
        vacc_ref[...] = jnp.zeros_like(vacc_ref)

        def fold8(t):
            s = t[0:8, :]
            for r in range(8, t.shape[0], 8):
                s = s + t[r:r + 8, :]
            return s

        def pass1(i, c):
            t0 = pl.multiple_of(i * CONV_TILE, CONV_TILE)
            gv = g_ref[...]
            for r in range(0, CONV_TILE, NORM_ROWS):
                rows = pl.ds(t0 + r, NORM_ROWS)
                xh, rstd = _layernorm_parts(u1_ref[rows, :])
                u2 = xh * gv + bb_ref[...]
                s2 = _sigmoid(u2)
                du2 = du3_ref[rows, :] * (s2 * (1.0 + u2 * (1.0 - s2)))
                wv = du2 * gv
                du1 = rstd * (wv - jnp.mean(wv, axis=-1, keepdims=True)
                              - xh * jnp.mean(wv * xh, axis=-1, keepdims=True))
                dpad_ref[rows, :] = du1
                vacc_ref[0] += fold8(du2 * xh)
                vacc_ref[1] += fold8(du2)
                vacc_ref[2] += fold8(du1)
            for cols in LANE_GROUPS:
                du1 = dpad_ref[pl.ds(t0, CONV_TILE), cols]
                for m, rows in _shifted_windows(upad_ref, t0, cols, range(2, CONV_WIDTH + 2)):
                    dwacc_ref[m - 2, :, cols] += fold8(du1 * rows)
            return c

        tiles_per_chunk = GLU_ROWS // CONV_TILE

        def pass1_chunk(ch, c):
            u1_chunk(ch).wait()
            du3_chunk(ch).wait()
            return lax.fori_loop(ch * tiles_per_chunk, (ch + 1) * tiles_per_chunk, pass1, c)

        lax.fori_loop(0, SEQ // GLU_ROWS, pass1_chunk, 0)

        def pass2(i, c):
            t0 = pl.multiple_of(i * CONV_TILE, CONV_TILE)
            tile = pl.ds(t0, CONV_TILE)
            for cols in LANE_GROUPS:
                gate_cols = slice(cols.start + CONV_DIM, cols.stop + CONV_DIM)
                du0 = _shifted_sum(dpad_ref, t0, cols, w_ref, lambda j: 30 - j)
                a = ci_ref[tile, cols]
                sb = _sigmoid(ci_ref[tile, gate_cols])
                dci_ref[tile, cols] = (du0 * sb).astype(MM)
                dci_ref[tile, gate_cols] = (du0 * a * (sb * (1.0 - sb))).astype(MM)
            return c

        lax.fori_loop(0, SEQ // CONV_TILE, pass2, 0)

        for j in range(CONV_WIDTH):
            dw_ref[j:j + 1, :] = jnp.sum(dwacc_ref[j], axis=0, keepdims=True)
        dw_ref[CONV_WIDTH:32, :] = jnp.zeros((32 - CONV_WIDTH, CONV_DIM), F32)
        dg_ref[...] = jnp.sum(vacc_ref[0], axis=0, keepdims=True)
        db_ref[...] = jnp.sum(vacc_ref[1], axis=0, keepdims=True)
        dbdw_ref[...] = jnp.sum(vacc_ref[2], axis=0, keepdims=True)

    vec = jax.ShapeDtypeStruct((1, CONV_DIM), F32)
    return pl.pallas_call(
        body, name="conv_bwd",
        out_shape=(jax.ShapeDtypeStruct((SEQ, 2 * CONV_DIM), MM), jax.ShapeDtypeStruct((32, CONV_DIM), F32),
                   vec, vec, vec),
        in_specs=[ANY] * 3 + [VMEM_SPEC] * 4, out_specs=[VMEM_SPEC] * 5,
        scratch_shapes=[pltpu.VMEM((SEQ + 32, CONV_DIM), F32), pltpu.VMEM((SEQ + 32, CONV_DIM), F32),
                        pltpu.VMEM((CONV_WIDTH, 8, CONV_DIM), F32), pltpu.VMEM((3, 8, CONV_DIM), F32)]
                       + [pltpu.VMEM(a.shape, a.dtype) for a in (du3, u1, ci)]
                       + [pltpu.SemaphoreType.DMA((SEQ // GLU_ROWS,))] * 3,
        compiler_params=_params(),
    )(du3, u1, ci, w_dw, ln_g, ln_b, after)


def attn_bwd(q, k, v, datt, rc, after):
    def body(q_hbm, k_hbm, v_hbm, do_hbm, rc_hbm, after_ref, dqkv_ref, dqa_ref, dka_ref, dva_ref, pc_ref, z_ref,
             sig1_ref, sig2_ref, g_ref, spb_ref, gb_ref, ar_ref, dzr_ref, dzc_ref, qm_ref, km_ref, dom_ref,
             q_ref, k_ref, v_ref, do_ref, rc_ref, sems):
        arrive = _fetch((q_hbm, k_hbm, do_hbm, v_hbm, rc_hbm), (q_ref, k_ref, do_ref, v_ref, rc_ref), sems)
        lane, row, _ = _head_masks()
        for ref in (dqa_ref, dka_ref, dva_ref, pc_ref):
            ref[...] = jnp.zeros_like(ref)
        z_ref[...] = jnp.full(z_ref.shape, NO_SCORE, F32)
        for ref in (sig1_ref, sig2_ref, spb_ref, ar_ref, g_ref, gb_ref, dzr_ref, dzc_ref):
            ref[...] = jnp.zeros_like(ref)
        for cp, (src_ref, split_ref) in zip(arrive, ((q_ref, qm_ref), (k_ref, km_ref), (do_ref, dom_ref))):
            cp.wait()
            _split_heads(src_ref, split_ref)
        arrive[3].wait()
        arrive[4].wait()
        w_suffix = _cumsum_weights(suffix=True, with_total=False)
        w_prefix = _cumsum_weights(suffix=False, with_total=True)

        def step(pairs):
            (ia, ja), (ib, jb), (ic, jc), (id_, jd) = pairs
            ka, qb_, kb_, kc, qd, kd = (pl.multiple_of(jnp.maximum(b, 0) * TQ, TQ) for b in (ja, ib, jb, jc, id_, jd))
            qa2, qb2, qc2, qd2, kd2 = (pl.multiple_of(jnp.maximum(b, 0) * 2 * TQ, 2 * TQ)
                                       for b in (ia, ib, ic, id_, jd))
            bias_a = _score_bias(lane, row, ia, ja)
            rc_rows = rc_ref[pl.ds(qb_, TQ), :]
            first_c = jc == 0
            for p in range(N_PAIRS):
                cols = slice(128 * p, 128 * (p + 1))
                k_a = k_ref[pl.ds(ka, TQ), cols]
                v_b = v_ref[pl.ds(kb_, TQ), cols]
                dqa_ref[pl.ds(qd, TQ), cols] += _dot(dzc_ref[p], km_ref[p, pl.ds(kd2, 2 * TQ), :])
                dka_ref[pl.ds(kd, TQ), cols] += _dot_tn(dzr_ref[p], qm_ref[p, pl.ds(qd2, 2 * TQ), :])
                dva_ref[pl.ds(kc, TQ), cols] += _dot_tn(ar_ref[p], dom_ref[p, pl.ds(qc2, 2 * TQ), :])
                for h in range(2):
                    hh = 2 * p + h
                    rows = slice(TQ * h, TQ * (h + 1))
                    r = _dot(gb_ref[hh], w_prefix)
                    p_in = jnp.where(first_c, 0.0, pc_ref[hh])
                    dz = (g_ref[hh] - sig2_ref[hh] * (r[:, :128] + p_in)).astype(MM)
                    dzc_ref[p, :, rows] = dz
                    dzr_ref[p, rows, :] = dz
                    pc_ref[hh] = p_in + r[:, 128:]
                    r_in = jnp.sum(jnp.where(lane == 16 * hh + jb, rc_rows, 0.0), axis=1, keepdims=True)
                    a = jnp.exp(z_ref[hh] - (_dot(spb_ref[hh], w_suffix) + r_in))
                    g = _dot_nt(dom_ref[p, pl.ds(qb2 + TQ * h, TQ), :], v_b) * a
                    ar_ref[p, rows, :] = a.astype(MM)
                    g_ref[hh] = g
                    gb_ref[hh] = g.astype(MM)
                    sig2_ref[hh] = sig1_ref[hh]
                    z = _dot_nt(qm_ref[p, pl.ds(qa2 + TQ * h, TQ), :], k_a) + bias_a
                    sp = _softplus(z)
                    sig1_ref[hh] = jnp.exp(z - sp)
                    z_ref[hh] = z
                    spb_ref[hh] = sp.astype(MM)

        _block_pipeline(4, False, step)
        dqkv_ref[:, 0:ATT_DIM] = (dqa_ref[...] * ATT_SCALE).astype(MM)
        dqkv_ref[:, ATT_DIM:2 * ATT_DIM] = dka_ref[...].astype(MM)
        dqkv_ref[:, 2 * ATT_DIM:3 * ATT_DIM] = dva_ref[...].astype(MM)

    split = pltpu.VMEM((N_PAIRS, 2 * SEQ, 128), MM)
    return pl.pallas_call(
        body, name="attn_bwd", out_shape=jax.ShapeDtypeStruct((SEQ, 3 * ATT_DIM), MM),
        in_specs=[ANY] * 5 + [VMEM_SPEC], out_specs=VMEM_SPEC,
        scratch_shapes=[pltpu.VMEM((SEQ, ATT_DIM), F32)] * 3 + [pltpu.VMEM((8, TQ, 128), F32)] * 5
                       + [pltpu.VMEM((8, TQ, 128), MM)] * 2
                       + [pltpu.VMEM((N_PAIRS, 2 * TQ, 128), MM)] * 2 + [pltpu.VMEM((N_PAIRS, TQ, 256), MM)]
                       + [split] * 3
                       + [pltpu.VMEM(a.shape, a.dtype) for a in (q, k, v, datt, rc)] + [pltpu.SemaphoreType.DMA((5,))],
        compiler_params=_params(),
    )(q, k, v, datt, rc, after)


DPROJ_PIECES = ((0, 1024), (1024, 2560), (2560, 4608))


def _dproj_segments(j):
    g0, g1 = j * IN_SHARD, (j + 1) * IN_SHARD
    segs = []
    for p, (s, e) in enumerate(DPROJ_PIECES):
        lo, hi = max(s, g0), min(e, g1)
        if lo < hi:
            segs.append((p, lo - s, lo - g0, hi - lo))
    return segs


def in_proj_bwd(pieces, w_in_g, x, dx2, g1, after):
    def body(p0_ref, p1_ref, p2_ref, w_ref, x_ref, dx2_ref, g_ref, after_ref, dx_ref, dg_ref):
        p_refs = (p0_ref, p1_ref, p2_ref)
        dh = None
        for j in range(N_CHIPS):
            for p, lo, off, width in _dproj_segments(j):
                t = _dot_nt(p_refs[p][:, lo:lo + width], w_ref[j, :, off:off + width])
                dh = t if dh is None else dh + t
        n1, r1 = _rms(x_ref[...])
        dx_ref[...] = dx2_ref[...] + _rms_bwd(dh * g_ref[...], n1, r1)
        _acc_rows(dg_ref, jnp.sum(dh * n1, axis=0, keepdims=True), pl.program_id(0) == 0)

    vec = _full_spec((1, D_MODEL))
    return pl.pallas_call(
        body, name="in_proj_bwd", grid=(SEQ // TM,),
        out_shape=[jax.ShapeDtypeStruct((SEQ, D_MODEL), F32), jax.ShapeDtypeStruct((1, D_MODEL), F32)],
        in_specs=[_row_tile_spec(p.shape[1]) for p in pieces]
                 + [_weight_spec(w_in_g.shape), _row_tile_spec(D_MODEL), _row_tile_spec(D_MODEL), vec, TOKEN_SPEC],
        out_specs=[_row_tile_spec(D_MODEL), vec],
        compiler_params=_params(("arbitrary",)),
    )(*pieces, w_in_g, x, dx2, g1, after)


def weight_grad_in(h1, pieces):
    kh = D_MODEL // 2
    operands = (h1,) + tuple(pieces)
    out = jax.ShapeDtypeStruct((N_CHIPS, 2, kh, IN_SHARD), MM)

    def body(*refs):
        hbm, o_hbm, bufs, stage_ref, in_sems, out_sems = refs[:4], refs[4], refs[5:9], refs[9], refs[10], refs[11]
        arrive = _fetch(hbm, bufs, in_sems)
        a_ref, p_refs = bufs[0], bufs[1:]
        arrive[0].wait()
        here, leaving = set(), []
        for j in range(N_CHIPS):
            for p in sorted({seg[0] for seg in _dproj_segments(j)} - here):
                arrive[1 + p].wait()
                here.add(p)
            for h in range(2):
                a = a_ref[:, h * kh:(h + 1) * kh]
                for p, lo, off, width in _dproj_segments(j):
                    stage_ref[j, h, :, off:off + width] = _dot_tn(a, p_refs[p][:, lo:lo + width]).astype(MM)
                leaving.append(pltpu.make_async_copy(stage_ref.at[j, h], o_hbm.at[j, h], out_sems.at[2 * j + h]))
                leaving[-1].start()
        for cp in leaving:
            cp.wait()

    return pl.pallas_call(
        body, name="dw_in", out_shape=out, in_specs=[ANY] * 4, out_specs=ANY,
        scratch_shapes=[pltpu.VMEM(a.shape, a.dtype) for a in operands]
                       + [pltpu.VMEM(out.shape, out.dtype), pltpu.SemaphoreType.DMA((4,)),
                          pltpu.SemaphoreType.DMA((2 * N_CHIPS,))],
        compiler_params=_params(),
    )(*operands)


def weight_grad(a, b, name, col_sharded, tk=None):
    kin, n = a.shape[1], b.shape[1]

    def body(a_ref, b_ref, o_ref):
        if col_sharded:
            o_ref[0, 0] = _dot_tn(a_ref[...], b_ref[...]).astype(MM)
        else:
            o_ref[...] = _dot_tn(a_ref[...], b_ref[...]).astype(MM)

    if col_sharded:
        kh, ns = kin // 2, n // N_CHIPS
        out = jax.ShapeDtypeStruct((N_CHIPS, 2, kh, ns), MM)
        grid = (2, N_CHIPS)
        in_specs = [pl.BlockSpec((SEQ, kh), lambda h, j: (0, h)), pl.BlockSpec((SEQ, ns), lambda h, j: (0, j))]
        out_spec = pl.BlockSpec((1, 1, kh, ns), lambda h, j: (j, h, 0, 0))
        sem = ("arbitrary", "arbitrary")
    else:
        out = jax.ShapeDtypeStruct((kin, n), MM)
        grid = (kin // tk,)
        in_specs = [pl.BlockSpec((SEQ, tk), lambda r: (0, r)), pl.BlockSpec((SEQ, n), lambda r: (0, 0))]
        out_spec = pl.BlockSpec((tk, n), lambda r: (r, 0))
        sem = ("arbitrary",)
    res = pl.pallas_call(
        body, name=name, grid=grid, out_shape=out, in_specs=in_specs, out_specs=out_spec,
        compiler_params=_params(sem),
    )(a, b)
    if not col_sharded:
        res = res.reshape(N_CHIPS, 2, kin // (2 * N_CHIPS), n)
    return res


def weight_grad_mix(merged, dmix, u3, dco, att, dao):
    operands = (merged, dmix, u3, dco, att, dao)
    n_out, n_br = D_MODEL // 2, CONV_DIM // 2

    def body(*refs):
        hbm, (o_out, o_cb, o_ab), bufs, sems = refs[:6], refs[6:9], refs[9:15], refs[15]
        copies = [pltpu.make_async_copy(hbm[i], bufs[i], sems.at[i]) for i in range(6)]
        for cp in copies:
            cp.start()
        m_ref, dm_ref, u_ref, dco_ref, a_ref, dao_ref = bufs
        copies[0].wait()
        copies[1].wait()
        for h in range(2):
            o_out[h * n_out:(h + 1) * n_out, :] = _dot_tn(m_ref[:, h * n_out:(h + 1) * n_out], dm_ref[...]).astype(MM)
        for br, (a, d, o) in enumerate(((u_ref, dco_ref, o_cb), (a_ref, dao_ref, o_ab))):
            copies[2 + 2 * br].wait()
            copies[3 + 2 * br].wait()
            for h in range(2):
                g = _dot_tn(a[:, h * n_br:(h + 1) * n_br], d[...])
                for j in range(N_CHIPS):
                    o[j, h] = g[:, j * BR_SHARD:(j + 1) * BR_SHARD].astype(MM)

    branch = jax.ShapeDtypeStruct((N_CHIPS, 2, n_br, BR_SHARD), MM)
    dw_out, dw_cb, dw_ab = pl.pallas_call(
        body, name="dw_mix", out_shape=[jax.ShapeDtypeStruct((D_MODEL, D_MODEL), MM), branch, branch],
        in_specs=[ANY] * 6, out_specs=[VMEM_SPEC] * 3,
        scratch_shapes=[pltpu.VMEM(a.shape, a.dtype) for a in operands] + [pltpu.SemaphoreType.DMA((6,))],
        compiler_params=_params(),
    )(*operands)
    return dw_out.reshape(N_CHIPS, 2, D_MODEL // (2 * N_CHIPS), D_MODEL), dw_cb, dw_ab


def _place():
    x, y, c = lax.axis_index("x"), lax.axis_index("y"), lax.axis_index("c")
    chips = [(1 - x, y), (x, 1 - y), (1 - x, 1 - y)]
    return x, y, c, chips


def _rcopy(src, dst, send_sem, recv_sem, dev):
    return pltpu.make_async_remote_copy(src_ref=src, dst_ref=dst, send_sem=send_sem, recv_sem=recv_sem,
                                        device_id=dev, device_id_type=MESH)


class _Gather:
    N_MOVES = 6

    def __init__(self, shapes, w, o, scratch):
        self.n, self.shapes, self.w, self.o = len(w), shapes, w, o
        self.send, self.recv, self.psend, self.precv, self.loc_in, self.loc_out = scratch[:6]
        self.raw, self.stage = scratch[6:6 + self.n], scratch[6 + self.n:]
        x, y, c, self.chips = _place()
        self.c = c
        self.me, k_x, k_y, k_far = 2 * x + y, 2 * (1 - x) + y, 2 * x + (1 - y), 2 * (1 - x) + (1 - y)
        to_x, to_y = (1 - x, y, c), (x, 1 - y, c)
        self.sib = (x, y, 1 - c)
        self.sent_as = [(self.me, 0, to_x), (self.me, 1, to_y), (self.me, 1, to_x), (self.me, 0, to_y),
                        (k_x, 0, to_y), (k_y, 1, to_x)]
        self.arrives_as = [(k_x, 0, to_x), (k_y, 1, to_y), (k_x, 1, to_x), (k_y, 0, to_y),
                           (k_far, 0, to_y), (k_far, 1, to_x)]
        self.sent_on_after = {0: 4, 1: 5}

    @staticmethod
    def scratch(shards):
        n = len(shards)
        sems = pltpu.SemaphoreType.DMA
        m = _Gather.N_MOVES * n
        return ([sems((m,)), sems((m,)), sems((m,)), sems((m,)), sems((3 * n,)), sems((n,))]
                + [pltpu.VMEM(s.shape, s.dtype) for s in shards] + [pltpu.VMEM(s.shape, MM) for s in shards])

    @staticmethod
    def out_shapes(shards):
        return [jax.ShapeDtypeStruct((N_CHIPS,) + s.shape, MM) for s in shards]

    def _rows(self, t, quarter, cc):
        rq = self.shapes[t][0] // 4
        return pl.ds(pl.multiple_of((2 * cc + quarter) * rq, rq), rq)

    def _own_rows(self, t, piece):
        if piece < 2:
            return self._rows(t, piece, self.c)
        rh = self.shapes[t][0] // 2
        return pl.ds(pl.multiple_of((1 - self.c) * rh, rh), rh)

    def _chip(self, j):
        cx, cy = self.chips[j]
        return 2 * cx + cy, (cx, cy, self.c)

    def local_in(self, t, piece):
        rows = self._own_rows(t, piece)
        return pltpu.make_async_copy(self.w[t].at[rows, :], self.raw[t].at[rows, :], self.loc_in.at[3 * t + piece])

    def local_out(self, t):
        return pltpu.make_async_copy(self.stage[t], self.o[t].at[self.me], self.loc_out.at[t])

    def sent(self, i, t):
        k, quarter, dev = self.sent_as[i]
        rows = self._rows(t, quarter, self.c)
        there = self.o[t].at[k, rows, :]
        return _rcopy(self.stage[t].at[rows, :] if i < 4 else there, there,
                      self.send.at[i * self.n + t], self.recv.at[i * self.n + t], dev)

    def arrived(self, i, t):
        k, quarter, dev = self.arrives_as[i]
        blk = self.o[t].at[k, self._rows(t, quarter, self.c), :]
        return _rcopy(blk, blk, self.send.at[i * self.n + t], self.recv.at[i * self.n + t], dev)

    def passed(self, i, t, cc):
        k, quarter, _ = self.arrives_as[i]
        blk = self.o[t].at[k, self._rows(t, quarter, cc), :]
        return _rcopy(blk, blk, self.psend.at[i * self.n + t], self.precv.at[i * self.n + t], self.sib)

    def start(self):
        for piece in range(3):
            for t in range(self.n):
                self.local_in(t, piece).start()
        for piece, moves in enumerate(((0, 3), (1, 2), ())):
            for t in range(self.n):
                rows = self._own_rows(t, piece)
                self.local_in(t, piece).wait()
                self.stage[t][rows, :] = self.raw[t][rows, :].astype(MM)
                for i in moves:
                    self.sent(i, t).start()
        for t in range(self.n):
            self.local_out(t).start()

    def forward(self):
        for i in range(self.N_MOVES):
            for t in range(self.n):
                self.arrived(i, t).wait_recv()
                if i in self.sent_on_after:
                    self.sent(self.sent_on_after[i], t).start()
                self.passed(i, t, self.c).start()

    def finish(self):
        for i in range(self.N_MOVES):
            for t in range(self.n):
                self.passed(i, t, 1 - self.c).wait_recv()
        for i in range(self.N_MOVES):
            for t in range(self.n):
                self.sent(i, t).wait_send()
                self.passed(i, t, self.c).wait_send()
        for t in range(self.n):
            self.local_out(t).wait()


def all_gather_weights(shards, small, later):
    n, m = len(shards), len(later)
    shapes = [s.shape for s in shards]

    def body(*refs):
        w = refs[:n]
        sm = refs[n]
        lw = refs[n + 1:n + 1 + m]
        o = refs[n + 1 + m:2 * n + 1 + m]
        osm = refs[2 * n + 1 + m]
        lo = refs[2 * n + 2 + m:2 * n + 2 + 2 * m]
        scratch = refs[2 * n + 2 + 2 * m:]
        ssend, srecv, sloc, lsem_in, lsem_out = scratch[:5]
        lraw, lstage = scratch[5:5 + m], scratch[5 + m:5 + 2 * m]
        g = _Gather(shapes, w, o, scratch[5 + 2 * m:])
        own = pltpu.make_async_copy(sm, osm.at[g.me], sloc)
        own.start()
        g.start()
        loads = [pltpu.make_async_copy(lw[t], lraw[t], lsem_in.at[t]) for t in range(m)]
        for cp in loads:
            cp.start()
        small_cps = [_rcopy(sm, osm.at[g.me], ssend.at[j], srecv.at[j], g._chip(j)[1]) for j in range(3)]
        for cp in small_cps:
            cp.start()
        places = []
        for t in range(m):
            loads[t].wait()
            lstage[t][...] = lraw[t][...].astype(MM)
            places.append(pltpu.make_async_copy(lstage[t], lo[t].at[g.me], lsem_out.at[t]))
            places[t].start()
        g.forward()
        g.finish()
        for j in range(3):
            k, dev = g._chip(j)
            _rcopy(sm, osm.at[k], ssend.at[j], srecv.at[j], dev).wait_recv()
            small_cps[j].wait_send()
        own.wait()
        for cp in places:
            cp.wait()

    out_shape = _Gather.out_shapes(shards)
    out_shape.append(jax.ShapeDtypeStruct((N_CHIPS,) + small.shape, small.dtype))
    out_shape += _Gather.out_shapes(later)
    sems = pltpu.SemaphoreType.DMA
    return pl.pallas_call(
        body, name="all_gather_weights", out_shape=out_shape,
        in_specs=[ANY] * (n + 1 + m), out_specs=[ANY] * (n + 1 + m),
        scratch_shapes=[sems((3,)), sems((3,)), sems, sems((m,)), sems((m,))]
                       + [pltpu.VMEM(s.shape, s.dtype) for s in later] + [pltpu.VMEM(s.shape, MM) for s in later]
                       + _Gather.scratch(shards),
        compiler_params=_params(),
    )(*shards, small, *later)


HBM_SPEC = pl.BlockSpec(memory_space=pltpu.HBM)
SEM_SPEC = pl.BlockSpec(memory_space=pltpu.SEMAPHORE)
DATAFLOW = pltpu.SideEffectType.DATAFLOW_SIDE_EFFECTING


def split_start(name, bufs, n_copies, copies):
    nb = len(bufs)

    def body(*refs):
        for cp in copies(refs[:nb], refs[nb], refs[nb + 1]):
            cp.start()
        token = refs[2 * nb + 2]
        token[...] = jnp.zeros_like(token)

    sems = [pltpu.SemaphoreType.DMA((n_copies,))] * 2
    res = pl.pallas_call(
        body, name=name,
        out_shape=sems + [pltpu.HBM(a.shape, a.dtype) for a in bufs] + [jax.ShapeDtypeStruct((8, 128), F32)],
        in_specs=[HBM_SPEC] * nb, out_specs=[SEM_SPEC] * 2 + [HBM_SPEC] * nb + [VMEM_SPEC],
        input_output_aliases={i: 2 + i for i in range(nb)},
        compiler_params=pltpu.CompilerParams(has_side_effects=DATAFLOW),
    )(*[pltpu.with_memory_space_constraint(a, pltpu.HBM) for a in bufs])
    return res[:-1], res[-1]


def split_wait(name, state, after, copies):
    sems, bufs = state[:2], state[2:]
    nb = len(bufs)

    def body(*refs):
        for cp in copies(refs[:nb], refs[nb], refs[nb + 1]):
            cp.wait_send()
            cp.wait_recv()

    return pl.pallas_call(
        body, name=name, out_shape=[pltpu.HBM(a.shape, a.dtype) for a in bufs],
        in_specs=[HBM_SPEC] * nb + [SEM_SPEC] * 2 + [ANY] * len(after), out_specs=[HBM_SPEC] * nb,
        input_output_aliases={i: i for i in range(nb)},
        compiler_params=pltpu.CompilerParams(has_side_effects=DATAFLOW),
    )(*bufs, *sems, *after)


class _Shifted:
    def __init__(self, sems, first):
        self.sems, self.first = sems, first

    @property
    def at(self):
        return self

    def __getitem__(self, i):
        return self.sems.at[self.first + i]


def _scatter_copies(n):
    def copies(refs, send, recv):
        _, _, c, chips = _place()
        return [_rcopy(refs[t].at[2 * cx + cy], refs[n + t].at[j], send.at[3 * t + j], recv.at[3 * t + j], (cx, cy, c))
                for t in range(n) for j, (cx, cy) in enumerate(chips)]
    return copies


def _direct_copies(n):
    def copies(refs, send, recv):
        x, y, c, chips = _place()
        out = []
        for t in range(n):
            src, land = refs[t], refs[n + t]
            for to_core, first in ((c, 0), (1 - c, 3)):
                for j, (cx, cy) in enumerate(chips):
                    i = 7 * t + first + j
                    out.append(_rcopy(src.at[2 * cx + cy, to_core], land.at[first + j], send.at[i], recv.at[i],
                                      (cx, cy, to_core)))
            i = 7 * t + 6
            out.append(_rcopy(src.at[2 * x + y, 1 - c], land.at[6], send.at[i], recv.at[i], (x, y, 1 - c)))
        return out
    return copies


def _scatter_and_direct(n, m):
    def copies(refs, send, recv):
        return (_scatter_copies(n)(refs[:2 * n], send, recv)
                + _direct_copies(m)(refs[2 * n:], _Shifted(send, 3 * n), _Shifted(recv, 3 * n)))
    return copies


def scatter_start(parts, partials):
    n, m = len(parts), len(partials)
    lands = [lax.empty((3,) + p.shape[1:], p.dtype) for p in parts]
    direct_lands = [lax.empty((7,) + p.shape[2:], p.dtype) for p in partials]
    return split_start("scatter_start_rest", list(parts) + lands + list(partials) + direct_lands, 3 * n + 7 * m,
                       _scatter_and_direct(n, m))


def scatter_wait(state, after, n, m):
    res = split_wait("scatter_wait_rest", state, after, _scatter_and_direct(n, m))
    return res[n:2 * n], res[2 * n:2 * n + m], res[2 * n + m:]


def _gather_copies(shapes, level, to_both_cores=()):
    n = len(shapes)

    def copies(refs, send, recv):
        x, y, c, chips = _place()
        out = []
        for t in list(range(n)) + list(to_both_cores):
            rh = shapes[t][0] // 2
            to_core = c if len(out) < 3 * n else 1 - c
            for cx, cy in chips:
                k, dev = (2 * x + y, (cx, cy, to_core)) if level == 1 else (2 * cx + cy, (x, y, 1 - c))
                blk = refs[t].at[k, pl.ds(c * rh, rh), :]
                out.append(_rcopy(blk, blk, send.at[len(out)], recv.at[len(out)], dev))
        return out
    return copies


def _sibling_copies(n, other_half):
    def copies(refs, send, recv):
        x, y, c, _ = _place()
        return [_rcopy(refs[t].at[:, 1 - c] if other_half else refs[t], refs[n + t], send.at[t], recv.at[t],
                       (x, y, 1 - c)) for t in range(n)]
    return copies


def sibling_start(srcs, other_half, tag):
    lands = [lax.empty((a.shape[0],) + a.shape[2:] if other_half else a.shape, a.dtype) for a in srcs]
    return split_start("sibling_start_" + tag, list(srcs) + lands, len(srcs),
                       _sibling_copies(len(srcs), other_half))


def sibling_wait(state, after, other_half, tag):
    n = (len(state) - 2) // 2
    res = split_wait("sibling_wait_" + tag, state, after, _sibling_copies(n, other_half))
    return res[:n], res[n:]


def small_pack(ddw, v512, v1024, loss_parts):
    rows, width = PACK_ROWS, 512
    n512, n1024 = len(VEC512), len(VEC1024)

    def body(*refs):
        ddw_ref = refs[0]
        a_refs = refs[1:1 + n512]
        b_refs = refs[1 + n512:1 + n512 + n1024]
        lp_ref, o_ref, p_ref = refs[1 + n512 + n1024:]
        p_ref[...] = jnp.zeros_like(p_ref)
        p_ref[0:32, :] = ddw_ref[...]
        p_ref[LOSS_ROW:LOSS_ROW + 1, 0:128] = jnp.sum(lp_ref[...], axis=0, keepdims=True) * 0.125
        for i, r in enumerate(a_refs):
            p_ref[32 + i:33 + i, :] = r[...]
        for i, r in enumerate(b_refs):
            base = 32 + n512 + 2 * i
            p_ref[base:base + 1, :] = r[:, 0:512]
            p_ref[base + 1:base + 2, :] = r[:, 512:1024]
        x, y, c, _ = _place()
        o_ref[4 * x + 2 * y + c] = p_ref[...]

    n_in = 2 + n512 + n1024
    return pl.pallas_call(
        body, name="small_pack", out_shape=jax.ShapeDtypeStruct((8, rows, width), F32),
        in_specs=[VMEM_SPEC] * n_in, out_specs=VMEM_SPEC,
        scratch_shapes=[pltpu.VMEM((rows, width), F32)],
    )(ddw, *[v512[n] for n in VEC512], *[v1024[n] for n in VEC1024], loss_parts)


def _small_copies(refs, send, recv):
    x, y, c, _ = _place()
    mine = refs[0].at[4 * x + 2 * y + c]
    peers = [(1 - x if k & 4 else x, 1 - y if k & 2 else y, 1 - c if k & 1 else c) for k in range(1, 8)]
    return [_rcopy(mine, mine, send.at[i], recv.at[i], dev) for i, dev in enumerate(peers)]


def _row_block(r):
    for tr in (512, 352, 256, 128):
        if r % tr == 0:
            return tr
    return r


def add_halves(g, recv, name):
    _, _, r, w = g.shape
    tr = _row_block(r)

    def body(g_ref, r_ref, ob_ref, own_ref):
        k = pl.program_id(1)
        me = 2 * lax.axis_index("x") + lax.axis_index("y")
        t = g_ref[0, 0].astype(F32) + r_ref[0].astype(F32)
        ob_ref[0] = t.astype(MM)
        mine = jnp.where(k == me, t, 0.0)

        @pl.when(k == 0)
        def _():
            own_ref[...] = mine

        @pl.when(k != 0)
        def _():
            own_ref[...] += mine

    return pl.pallas_call(
        body, name=name, grid=(r // tr, N_CHIPS),
        in_specs=[pl.BlockSpec((1, 1, tr, w), lambda i, k: (k, lax.axis_index("c"), i, 0)),
                  pl.BlockSpec((1, tr, w), lambda i, k: (k, i, 0))],
        out_specs=[pl.BlockSpec((1, tr, w), lambda i, k: (k, i, 0)),
                   pl.BlockSpec((tr, w), lambda i, k: (i, 0))],
        out_shape=(jax.ShapeDtypeStruct((N_CHIPS, r, w), MM), jax.ShapeDtypeStruct((r, w), F32)),
        compiler_params=_params(("arbitrary", "arbitrary")),
    )(g, recv)


def sum_parts(own, rin, after, name):
    _, r, w = rin.shape
    tr = _row_block(r)

    def body(o_ref, r_ref, after_ref, out_ref):
        out_ref[...] = ((o_ref[...] + r_ref[0].astype(F32)) + r_ref[1].astype(F32)) + r_ref[2].astype(F32)

    return pl.pallas_call(
        body, name=name, grid=(r // tr,), out_shape=jax.ShapeDtypeStruct((r, w), F32),
        in_specs=[pl.BlockSpec((tr, w), lambda i: (i, 0)), pl.BlockSpec((3, tr, w), lambda i: (0, i, 0)),
                  TOKEN_SPEC],
        out_specs=pl.BlockSpec((tr, w), lambda i: (i, 0)),
        compiler_params=_params(("arbitrary",)),
    )(own, rin, after)


def sum_partials(p, land, after, name):
    _, _, r, w = p.shape
    tr = _row_block(r)

    def body(p_ref, l_ref, after_ref, out_ref):
        total = p_ref[0, 0].astype(F32)
        for slot in (6, 0, 3, 1, 4, 2, 5):
            total = total + l_ref[slot].astype(F32)
        out_ref[...] = total

    def own(i):
        return 2 * lax.axis_index("x") + lax.axis_index("y"), lax.axis_index("c"), i, 0

    return pl.pallas_call(
        body, name=name, grid=(r // tr,), out_shape=jax.ShapeDtypeStruct((r, w), F32),
        in_specs=[pl.BlockSpec((1, 1, tr, w), own), pl.BlockSpec((7, tr, w), lambda i: (0, i, 0)), TOKEN_SPEC],
        out_specs=pl.BlockSpec((tr, w), lambda i: (i, 0)),
        compiler_params=_params(("arbitrary",)),
    )(p, land, after)


def _adamw_math(w, g, m, v):
    mn = ADAM_B1 * m + (1.0 - ADAM_B1) * g
    vn = ADAM_B2 * v + (1.0 - ADAM_B2) * (g * g)
    m_hat = mn / (1.0 - ADAM_B1 ** ADAM_STEP)
    v_hat = vn / (1.0 - ADAM_B2 ** ADAM_STEP)
    return -ADAM_LR * (m_hat / (jnp.sqrt(v_hat) + ADAM_EPS) + ADAM_WD * w), mn, vn


def adamw(w, mine, other, m, v, name):
    r, c = w.shape
    rh = r // 2
    tr = _row_block(rh)
    if c >= 1024 and tr % 512 == 0:
        tr = 256
    nb = rh // tr

    def body(w_ref, a_ref, b_ref, m_ref, v_ref, go_ref, d_ref, mo_ref, vo_ref):
        gv = jnp.where(lax.axis_index("c") == pl.program_id(0), a_ref[...], b_ref[...])
        go_ref[...] = gv
        d_ref[...], mo_ref[...], vo_ref[...] = _adamw_math(w_ref[...], gv, m_ref[...], v_ref[...])

    def half(of_sibling):
        def index(h, i):
            owner = lax.axis_index("c")
            owner = 1 - owner if of_sibling else owner
            return jnp.where(h == owner, i, jnp.where(h < owner, 0, nb - 1)), 0
        return pl.BlockSpec((tr, c), index)

    spec = pl.BlockSpec((tr, c), lambda h, i: (h * nb + i, 0))
    out = jax.ShapeDtypeStruct((r, c), F32)
    return pl.pallas_call(
        body, name=name, grid=(2, nb), out_shape=(out, out, out, out),
        in_specs=[spec, half(False), half(True), spec, spec], out_specs=[spec] * 4,
        compiler_params=_params(("arbitrary", "arbitrary")),
    )(w, mine, other, m, v)


def adamw_small(packs, params, after):
    names = list(params)
    flat = [a for n in names for a in params[n]]

    def body(*refs):
        p_ref = refs[0]
        ins = refs[1:1 + 3 * len(names)]
        loss_ref, g_ref = refs[2 + 3 * len(names):4 + 3 * len(names)]
        outs = refs[4 + 3 * len(names):]
        total = p_ref[0]
        for d in range(1, 8):
            total = total + p_ref[d]
        g_ref[...] = total
        loss_ref[...] = g_ref[LOSS_ROW:LOSS_ROW + 1, 0:1]
        me = 2 * lax.axis_index("x") + lax.axis_index("y")
        for i, n in enumerate(names):
            w_ref, m_ref, v_ref = ins[3 * i:3 * i + 3]
            go_ref, d_ref, mo_ref, vo_ref = outs[4 * i:4 * i + 4]
            if n == "conv_dw_w":
                gv = jnp.zeros((CONV_WIDTH, 128), F32)
                for k in range(N_CHIPS):
                    gv = gv + jnp.where(me == k, g_ref[0:CONV_WIDTH, 128 * k:128 * (k + 1)], 0.0)
            elif n in VEC512:
                r0 = 32 + VEC512.index(n)
                gv = g_ref[r0:r0 + 1, :]
            else:
                r0 = 32 + len(VEC512) + 2 * VEC1024.index(n)
                gv = jnp.concatenate([g_ref[r0:r0 + 1, :], g_ref[r0 + 1:r0 + 2, :]], axis=1)
            go_ref[...] = gv
            d_ref[...], mo_ref[...], vo_ref[...] = _adamw_math(w_ref[...], gv, m_ref[...], v_ref[...])

    out_shape = [jax.ShapeDtypeStruct((1, 1), F32), jax.ShapeDtypeStruct(packs.shape[1:], F32)]
    out_shape += [jax.ShapeDtypeStruct(params[n][0].shape, F32) for n in names for _ in range(4)]
    res = pl.pallas_call(
        body, name="adamw_small", out_shape=out_shape,
        in_specs=[VMEM_SPEC] * (2 + len(flat)), out_specs=[VMEM_SPEC] * len(out_shape),
        compiler_params=_params(),
    )(packs, *flat, after)
    return res[0], res[1], {n: res[2 + 4 * i:6 + 4 * i] for i, n in enumerate(names)}


REST = ("w_ffn_up", "w_ffn_down", "w_out", "w_conv_branch", "w_att_branch")
VEC512 = ("conv_dw_b", "conv_ln_g", "conv_ln_b")
VEC1024 = ("norm_mix_pre", "b_conv_branch", "norm_mix_post", "norm_ffn_pre", "norm_ffn_post")
PACK_ROWS = 48
LOSS_ROW = 47


def kernel(x, norm_mix_pre, w_in, conv_dw_w, conv_dw_b, conv_ln_g, conv_ln_b, w_conv_branch, b_conv_branch, w_att_branch, w_out, norm_mix_post, norm_ffn_pre, w_ffn_up, w_ffn_down, norm_ffn_post, loss_target, m_norm_mix_pre, m_w_in, m_conv_dw_w, m_conv_dw_b, m_conv_ln_g, m_conv_ln_b, m_w_conv_branch, m_b_conv_branch, m_w_att_branch, m_w_out, m_norm_mix_post, m_norm_ffn_pre, m_w_ffn_up, m_w_ffn_down, m_norm_ffn_post, v_norm_mix_pre, v_w_in, v_conv_dw_w, v_conv_dw_b, v_conv_ln_g, v_conv_ln_b, v_w_conv_branch, v_b_conv_branch, v_w_att_branch, v_w_out, v_norm_mix_post, v_norm_ffn_pre, v_w_ffn_up, v_w_ffn_down, v_norm_ffn_post):
    weights = dict(norm_mix_pre=norm_mix_pre, w_in=w_in, conv_dw_w=conv_dw_w, conv_dw_b=conv_dw_b, conv_ln_g=conv_ln_g, conv_ln_b=conv_ln_b, w_conv_branch=w_conv_branch, b_conv_branch=b_conv_branch, w_att_branch=w_att_branch, w_out=w_out, norm_mix_post=norm_mix_post, norm_ffn_pre=norm_ffn_pre, w_ffn_up=w_ffn_up, w_ffn_down=w_ffn_down, norm_ffn_post=norm_ffn_post)
    mom = dict(norm_mix_pre=m_norm_mix_pre, w_in=m_w_in, conv_dw_w=m_conv_dw_w, conv_dw_b=m_conv_dw_b, conv_ln_g=m_conv_ln_g, conv_ln_b=m_conv_ln_b, w_conv_branch=m_w_conv_branch, b_conv_branch=m_b_conv_branch, w_att_branch=m_w_att_branch, w_out=m_w_out, norm_mix_post=m_norm_mix_post, norm_ffn_pre=m_norm_ffn_pre, w_ffn_up=m_w_ffn_up, w_ffn_down=m_w_ffn_down, norm_ffn_post=m_norm_ffn_post)
    var = dict(norm_mix_pre=v_norm_mix_pre, w_in=v_w_in, conv_dw_w=v_conv_dw_w, conv_dw_b=v_conv_dw_b, conv_ln_g=v_conv_ln_g, conv_ln_b=v_conv_ln_b, w_conv_branch=v_w_conv_branch, b_conv_branch=v_b_conv_branch, w_att_branch=v_w_att_branch, w_out=v_w_out, norm_mix_post=v_norm_mix_post, norm_ffn_pre=v_norm_ffn_pre, w_ffn_up=v_w_ffn_up, w_ffn_down=v_w_ffn_down, norm_ffn_post=v_norm_ffn_post)
    order = list(weights)
    grads, deltas, new_m, new_v = {}, {}, {}, {}
    xs = x.reshape(SEQ, D_MODEL)
    tgt = loss_target.reshape(SEQ, D_MODEL)
    row = lambda a: a.reshape(1, -1)
    g1, g2, g3, g4 = (row(weights[n]) for n in ("norm_mix_pre", "norm_mix_post", "norm_ffn_pre", "norm_ffn_post"))
    ln_g, ln_b = row(conv_ln_g), row(conv_ln_b)

    summed, from_chips = {}, {}

    def core_sums(names, state, after, tag):
        own, from_sibling = sibling_wait(state, after, True, tag)
        for n, g, r in zip(names, own, from_sibling):
            summed[n] = add_halves(g, r, "add_" + n)

    def chip_sums(names, after):
        return [sum_parts(summed[n][1], from_chips[n], after, "sum_" + n) for n in names]

    def optimize(names, state, after, tag):
        mine, other = sibling_wait(state, after, False, tag)
        for n, a, b in zip(names, mine, other):
            grads[n], deltas[n], new_m[n], new_v[n] = adamw(weights[n], a, b, mom[n], var[n], "adamw_" + n)

    w_in_g, dw_g, *rest = all_gather_weights([w_in], conv_dw_w, [weights[n] for n in REST])
    w_dw_full = jnp.concatenate([dw_g[k] for k in range(N_CHIPS)], axis=1)
    rest_shapes = [weights[n].shape for n in REST]
    over_ici = _gather_copies(rest_shapes, 1, to_both_cores=(2, 3, 4))
    state, token = split_start("gather_start", rest, 3 * (len(REST) + 3), over_ici)
    h1, ci, q, k, v, gc, ga = in_proj_fwd(xs, g1, w_in_g, token)
    u1, u3 = conv_fwd(ci, w_dw_full, row(conv_dw_b), ln_g, ln_b)
    att, rc = attn_fwd(q, k, v)
    rest = split_wait("gather_wait", state, [att], over_ici)
    w_out_g, w_cb_g, w_ab_g = rest[2:]
    w_out_g = w_out_g.reshape(D_MODEL, D_MODEL)
    to_sibling = _gather_copies(rest_shapes[:2], 2)
    state, token = split_start("pass_start", rest[:2], 3 * 2, to_sibling)
    merged, mix, x2, h2 = mix_fwd(u3, att, gc, ga, xs, w_cb_g, row(b_conv_branch), w_ab_g, w_out_g, g2, g3, token)
    w_up_g, w_down_g = split_wait("pass_wait", state, [h2], to_sibling)
    w_down_g = w_down_g.reshape(D_FF, D_MODEL)
    act, dff, dy, loss_parts, dg4, dgu = ffn_loss(h2, w_up_g, w_down_g, x2, tgt, g4)

    ffn_grads = [weight_grad(h2, dgu, "dw_ffn_up", True), weight_grad(act, dff, "dw_ffn_down", False, tk=UP_SHARD)]
    to_ffn, token = sibling_start(ffn_grads, True, "dw_ffn")
    dx2, dmix, dg3, dg2, dco, dao, dg, du3, datt, dbcb = ffn_merge_bwd(
        dgu, w_up_g, x2, mix, dy, g3, g2, w_out_g, gc, ga, u3, att, w_cb_g, row(b_conv_branch), w_ab_g, token)
    mix_grads = weight_grad_mix(merged, dmix, u3, dco, att, dao)
    core_sums(REST[:2], to_ffn, [mix_grads[0]], "dw_ffn")
    state, token = scatter_start([summed[n][0] for n in REST[:2]], mix_grads)
    dci, ddw, dbdw, dlng, dlnb = conv_bwd(du3, u1, ci, w_dw_full, ln_g, ln_b, token)
    dqkv = attn_bwd(q, k, v, datt, rc, token)
    ffn_from_chips, mix_grads, mix_from_all = scatter_wait(state, [dci, dqkv], 2, len(mix_grads))
    from_chips.update(zip(REST[:2], ffn_from_chips))
    dproj = (dci, dqkv, dg)
    to_in, token = sibling_start([weight_grad_in(h1, dproj)], True, "dw_in")
    grad_x, dg1 = in_proj_bwd(dproj, w_in_g, xs, dx2, g1, token)
    v512 = dict(conv_dw_b=dbdw, conv_ln_g=dlng, conv_ln_b=dlnb)
    v1024 = dict(norm_mix_pre=dg1, b_conv_branch=dbcb, norm_mix_post=dg2, norm_ffn_pre=dg3, norm_ffn_post=dg4)
    packs = small_pack(ddw, v512, v1024, loss_parts)
    core_sums(("w_in",), to_in, [packs], "dw_in")
    to_chips = summed["w_in"][0]
    landing = lax.empty((3,) + to_chips.shape[1:], to_chips.dtype)

    def scatter_and_packs(refs, send, recv):
        return (_scatter_copies(1)(refs[:2], send, recv)
                + _small_copies(refs[2:], _Shifted(send, 3), _Shifted(recv, 3)))

    state, token = split_start("scatter_start_w_in", [to_chips, landing, packs], 3 + 7, scatter_and_packs)
    swap_up, token = sibling_start(chip_sums(REST[:1], token), False, "sum_ffn_up")
    rest_sums = chip_sums(REST[1:2], token) + [sum_partials(p, r, token, "sum_" + n)
                                               for n, p, r in zip(REST[2:], mix_grads, mix_from_all)]
    swap_rest, token = sibling_start(rest_sums, False, "sum_rest")
    optimize(REST[:1], swap_up, [token], "sum_ffn_up")
    optimize(REST[1:], swap_rest, [new_v["w_ffn_up"]], "sum_rest")
    _, from_chips["w_in"], packs = split_wait("scatter_wait_w_in", state, [new_v[n] for n in REST], scatter_and_packs)
    swap_in, token = sibling_start(chip_sums(("w_in",), token), False, "sum_w_in")
    as_rows = lambda n, a: a if n == "conv_dw_w" else a.reshape(1, -1)
    small_names = ("conv_dw_w",) + VEC512 + VEC1024
    loss, gsum, small = adamw_small(
        packs, {n: tuple(as_rows(n, d[n]) for d in (weights, mom, var)) for n in small_names}, token)
    optimize(("w_in",), swap_in, [gsum], "sum_w_in")
    for n in small_names:
        grads[n], deltas[n], new_m[n], new_v[n] = (a.reshape(weights[n].shape) for a in small[n])

    return (loss.reshape(()), grad_x.reshape(1, SEQ, D_MODEL),*[grads[n] for n in order], *[deltas[n] for n in order],
            *[new_m[n] for n in order], *[new_v[n] for n in order])
```

```python
import jax
import jax.numpy as jnp
from jax import lax
from jax.experimental import pallas as pl
from jax.experimental.pallas import tpu as pltpu

F32 = jnp.float32
MM = jnp.bfloat16

SEQ = 2048
D_MODEL = 1024
CONV_DIM = 512
ATT_DIM = 512
CONV_WIDTH = 31
D_FF = 2816
IN_COLS = 2 * CONV_DIM + 3 * ATT_DIM + 2 * D_MODEL
N_CHIPS = 4
IN_SHARD = IN_COLS // N_CHIPS
UP_SHARD = 2 * D_FF // N_CHIPS
BR_SHARD = D_MODEL // N_CHIPS
EPS = 1e-6
ATT_SCALE = 0.125

TM = 256
GLU_ROWS = 256
TQ = 128
CONV_TILE = 64
CONV_WIN = CONV_TILE + 32
VMEM_LIMIT = 56 * 1024 * 1024

ADAM_LR = 0.001
ADAM_B1 = 0.9
ADAM_B2 = 0.999
ADAM_EPS = 1e-08
ADAM_WD = 0.01
ADAM_STEP = 10

MESH = pl.DeviceIdType.MESH
ANY = pl.BlockSpec(memory_space=pl.ANY)
VMEM_SPEC = pl.BlockSpec(memory_space=pltpu.VMEM)

NT_DIMS = (((1,), (1,)), ((), ()))
TN_DIMS = (((0,), (0,)), ((), ()))

IN_PIECES = (("ci", 0, 1024), ("q", 1024, 1536), ("k", 1536, 2048), ("v", 2048, 2560),
             ("gc", 2560, 3584), ("ga", 3584, 4608))


def _params(sem=None, vmem=VMEM_LIMIT):
    return pltpu.CompilerParams(dimension_semantics=sem, vmem_limit_bytes=vmem)


def _dot(a, b):
    return jnp.dot(a, b, preferred_element_type=F32)


def _dot_nt(a, b):
    return lax.dot_general(a, b, NT_DIMS, preferred_element_type=F32)


def _dot_tn(a, b):
    return lax.dot_general(a, b, TN_DIMS, preferred_element_type=F32)


def _sigmoid(x):
    return 1.0 / (1.0 + jnp.exp(-x))


def _rms(x):
    r = lax.rsqrt(jnp.mean(x * x, axis=-1, keepdims=True) + EPS)
    return x * r, r


def _rms_bwd(dy_g, n, r):
    return r * (dy_g - n * jnp.mean(dy_g * n, axis=-1, keepdims=True))


def _row_tile_spec(width, tm=TM):
    return pl.BlockSpec((tm, width), lambda i: (i, 0))


def _full_spec(shape):
    nd = len(shape)
    return pl.BlockSpec(shape, lambda *_: (0,) * nd)


def _weight_spec(shape):
    nd = len(shape)
    return pl.BlockSpec(shape, lambda *_: (0,) * nd, pipeline_mode=pl.Buffered(1))


def _acc_rows(ref, val, first):
    @pl.when(first)
    def _():
        ref[...] = val

    @pl.when(jnp.logical_not(first))
    def _():
        ref[...] += val


TOKEN_SPEC = pl.BlockSpec((8, 128), lambda *_: (0, 0))


def in_proj_fwd(x, g1, w_in_g, after):
    def body(x_ref, g_ref, w_ref, after_ref, h_ref, ci_ref, q_ref, k_ref, v_ref, gc_ref, ga_ref):
        n, _ = _rms(x_ref[...])
        h = (n * g_ref[...]).astype(MM)
        h_ref[...] = h
        outs = dict(ci=ci_ref, q=q_ref, k=k_ref, v=v_ref, gc=gc_ref, ga=ga_ref)
        for j in range(N_CHIPS):
            p = _dot(h, w_ref[j])
            g0 = j * IN_SHARD
            for name, s, e in IN_PIECES:
                lo, hi = max(s, g0), min(e, g0 + IN_SHARD)
                if lo < hi:
                    ref = outs[name]
                    part = p[:, lo - g0:hi - g0]
                    if name == "q":
                        part = part * ATT_SCALE
                    ref[:, lo - s:hi - s] = part.astype(ref.dtype)

    out_shape = [
        jax.ShapeDtypeStruct((SEQ, D_MODEL), MM),
        jax.ShapeDtypeStruct((SEQ, 2 * CONV_DIM), F32),
        jax.ShapeDtypeStruct((SEQ, ATT_DIM), MM),
        jax.ShapeDtypeStruct((SEQ, ATT_DIM), MM),
        jax.ShapeDtypeStruct((SEQ, ATT_DIM), MM),
        jax.ShapeDtypeStruct((SEQ, D_MODEL), F32),
        jax.ShapeDtypeStruct((SEQ, D_MODEL), F32),
    ]
    return pl.pallas_call(
        body, name="in_proj_fwd", grid=(SEQ // TM,), out_shape=out_shape,
        in_specs=[_row_tile_spec(D_MODEL), _full_spec((1, D_MODEL)), _weight_spec(w_in_g.shape), TOKEN_SPEC],
        out_specs=[_row_tile_spec(s.shape[1]) for s in out_shape],
        compiler_params=_params(("arbitrary",)),
    )(x, g1, w_in_g, after)


LANE_GROUPS = [slice(g, g + 128) for g in range(0, CONV_DIM, 128)]
NORM_ROWS = 16


def _shifted_windows(src_ref, t0, cols, offsets):
    win = src_ref[pl.ds(t0, CONV_WIN), cols]
    for rot in range(8):
        ms = [m for m in offsets if m % 8 == rot]
        if ms:
            shifted = win if rot == 0 else pltpu.roll(win, CONV_WIN - rot, 0)
            for m in ms:
                yield m, shifted[m - rot:m - rot + CONV_TILE, :]


def _shifted_sum(src_ref, t0, cols, w_ref, offset_of_tap):
    tap_at = {offset_of_tap(j): j for j in range(CONV_WIDTH)}
    acc = None
    for m, rows in _shifted_windows(src_ref, t0, cols, sorted(tap_at)):
        t = w_ref[tap_at[m]:tap_at[m] + 1, cols] * rows
        acc = t if acc is None else acc + t
    return acc


def _fetch(srcs, dsts, sems):
    copies = [pltpu.make_async_copy(s, d, sems.at[i]) for i, (s, d) in enumerate(zip(srcs, dsts))]
    for cp in copies:
        cp.start()
    return copies


def _row_chunks(src, dst, sems):
    def chunk(i):
        t0 = i * GLU_ROWS
        rows = pl.ds(t0 if isinstance(i, int) else pl.multiple_of(t0, GLU_ROWS), GLU_ROWS)
        return pltpu.make_async_copy(src.at[rows, :], dst.at[rows, :], sems.at[i])

    for i in range(SEQ // GLU_ROWS):
        chunk(i).start()
    return chunk


def _glu_into(ci_chunk, ci_ref, upad_ref):
    upad_ref[0:32, :] = jnp.zeros((32, CONV_DIM), F32)

    def step(i, c):
        ci_chunk(i).wait()
        t0 = pl.multiple_of(i * GLU_ROWS, GLU_ROWS)
        a = ci_ref[pl.ds(t0, GLU_ROWS), 0:CONV_DIM]
        b = ci_ref[pl.ds(t0, GLU_ROWS), CONV_DIM:2 * CONV_DIM]
        upad_ref[pl.ds(t0 + 32, GLU_ROWS), :] = a * _sigmoid(b)
        return c

    lax.fori_loop(0, SEQ // GLU_ROWS, step, 0)


def _layernorm_parts(u1):
    mu = jnp.mean(u1, axis=-1, keepdims=True)
    xc = u1 - mu
    rstd = lax.rsqrt(jnp.mean(xc * xc, axis=-1, keepdims=True) + EPS)
    return xc * rstd, rstd


def conv_fwd(ci, w_dw, b_dw, ln_g, ln_b):
    def body(ci_hbm, w_ref, b_ref, g_ref, bb_ref, u1_ref, u3_ref, upad_ref, ci_ref, sems):
        _glu_into(_row_chunks(ci_hbm, ci_ref, sems), ci_ref, upad_ref)

        def step(i, c):
            t0 = pl.multiple_of(i * CONV_TILE, CONV_TILE)
            for cols in LANE_GROUPS:
                u1_ref[pl.ds(t0, CONV_TILE), cols] = (_shifted_sum(upad_ref, t0, cols, w_ref, lambda j: j + 2)
                                                      + b_ref[:, cols])
            for r in range(0, CONV_TILE, NORM_ROWS):
                rows = pl.ds(t0 + r, NORM_ROWS)
                xh, _ = _layernorm_parts(u1_ref[rows, :])
                u2 = xh * g_ref[...] + bb_ref[...]
                u3_ref[rows, :] = (u2 * _sigmoid(u2)).astype(MM)
            return c

        lax.fori_loop(0, SEQ // CONV_TILE, step, 0)

    return pl.pallas_call(
        body, name="conv_fwd",
        out_shape=[jax.ShapeDtypeStruct((SEQ, CONV_DIM), F32), jax.ShapeDtypeStruct((SEQ, CONV_DIM), MM)],
        in_specs=[ANY] + [VMEM_SPEC] * 4, out_specs=[VMEM_SPEC] * 2,
        scratch_shapes=[pltpu.VMEM((SEQ + 32, CONV_DIM), F32), pltpu.VMEM(ci.shape, ci.dtype),
                        pltpu.SemaphoreType.DMA((SEQ // GLU_ROWS,))],
        compiler_params=_params(),
    )(ci, w_dw, b_dw, ln_g, ln_b)


def _softplus(z):
    return jnp.maximum(z, 0.0) + jnp.log(1.0 + jnp.exp(-jnp.abs(z)))


def _cumsum_weights(suffix, with_total):
    n = 256 if with_total else 128
    r = lax.broadcasted_iota(jnp.int32, (128, n), 0)
    c = lax.broadcasted_iota(jnp.int32, (128, n), 1)
    tri = (r >= c) if suffix else (r <= c)
    return jnp.logical_or(tri, c >= 128).astype(MM)


NO_SCORE = -1e30
N_KB = SEQ // TQ


def _score_bias(lane, row, i, j):
    keep = jnp.logical_and(i >= 0, jnp.logical_or(j < i, lane < row))
    return jnp.where(keep, 0.0, NO_SCORE)


def _block_pipeline(n_stages, descending, step, on_query_block=None):
    n_lag = n_stages - 1
    none = jnp.int32(-1)

    def shift(cur, lag):
        step([cur] + [(lag[2 * s], lag[2 * s + 1]) for s in range(n_lag)])
        return (cur[0], cur[1]) + tuple(lag[:-2])

    def outer(i, lag):
        if on_query_block is not None:
            on_query_block(i)

        def inner(n, lag):
            return shift((i, i - n if descending else n), lag)
        return lax.fori_loop(0, i + 1, inner, lag)

    lag = lax.fori_loop(0, N_KB, outer, (none,) * (2 * n_lag))
    lax.fori_loop(0, n_lag, lambda n, lag: shift((none, none), lag), lag)


def _head_masks():
    lane = lax.broadcasted_iota(jnp.int32, (TQ, 128), 1)
    row = lax.broadcasted_iota(jnp.int32, (TQ, 128), 0)
    return lane, row, lane < 64


def _pick_head(x, head0, h):
    zero = jnp.zeros_like(x)
    return jnp.where(head0, x, zero) if h == 0 else jnp.where(head0, zero, x)


N_PAIRS = ATT_DIM // 128


def _split_heads(src_ref, dst_ref):
    _, _, head0 = _head_masks()

    def block(b, c):
        r0 = pl.multiple_of(b * TQ, TQ)
        d0 = pl.multiple_of(b * 2 * TQ, 2 * TQ)
        for p in range(N_PAIRS):
            x = src_ref[pl.ds(r0, TQ), 128 * p:128 * (p + 1)]
            for h in range(2):
                dst_ref[p, pl.ds(d0 + TQ * h, TQ), :] = _pick_head(x, head0, h)
        return c

    lax.fori_loop(0, N_KB, block, 0)


def attn_fwd(q, k, v):
    def body(q_hbm, k_hbm, v_hbm, o_ref, rc_ref, acc_ref, r_ref, z_ref, spb_ref, ab_ref, qm_ref, vm_ref,
             q_ref, k_ref, v_ref, sems):
        arrive = _fetch((q_hbm, v_hbm, k_hbm), (q_ref, v_ref, k_ref), sems)
        lane, row, _ = _head_masks()
        w = _cumsum_weights(suffix=True, with_total=True)
        acc_ref[...] = jnp.zeros_like(acc_ref)
        r_ref[...] = jnp.zeros_like(r_ref)
        rc_ref[...] = jnp.zeros_like(rc_ref)
        z_ref[...] = jnp.full(z_ref.shape, NO_SCORE, F32)
        spb_ref[...] = jnp.zeros_like(spb_ref)
        ab_ref[...] = jnp.zeros_like(ab_ref)
        arrive[0].wait()
        _split_heads(q_ref, qm_ref)
        arrive[1].wait()
        _split_heads(v_ref, vm_ref)
        arrive[2].wait()

        def step(pairs):
            (i1, j1), (i2, j2), (i3, j3) = pairs
            k1, q2, q3 = (pl.multiple_of(jnp.maximum(b, 0) * TQ, TQ) for b in (j1, i2, i3))
            q1, k3 = (pl.multiple_of(jnp.maximum(b, 0) * 2 * TQ, 2 * TQ) for b in (i1, j3))
            bias1 = _score_bias(lane, row, i1, j1)
            first2 = j2 == i2
            rc_rows = rc_ref[pl.ds(q2, TQ), :]
            for p in range(N_PAIRS):
                cols = slice(128 * p, 128 * (p + 1))
                kb = k_ref[pl.ds(k1, TQ), cols]
                acc_ref[pl.ds(q3, TQ), cols] += _dot(ab_ref[p], vm_ref[p, pl.ds(k3, 2 * TQ), :])
                for h in range(2):
                    hh = 2 * p + h
                    r = _dot(spb_ref[hh], w)
                    r_in = jnp.where(first2, 0.0, r_ref[hh])
                    ab_ref[p, :, 128 * h:128 * (h + 1)] = jnp.exp(z_ref[hh] - (r[:, :128] + r_in)).astype(MM)
                    rc_rows = jnp.where(jnp.logical_and(lane == 16 * hh + j2, i2 >= 0), r_in, rc_rows)
                    r_ref[hh] = r_in + r[:, 128:]
                    z = _dot_nt(qm_ref[p, pl.ds(q1 + TQ * h, TQ), :], kb) + bias1
                    z_ref[hh] = z
                    spb_ref[hh] = _softplus(z).astype(MM)
            rc_ref[pl.ds(q2, TQ), :] = rc_rows

        _block_pipeline(3, True, step)
        o_ref[...] = acc_ref[...].astype(MM)

    return pl.pallas_call(
        body, name="attn_fwd",
        out_shape=[jax.ShapeDtypeStruct((SEQ, ATT_DIM), MM), jax.ShapeDtypeStruct((SEQ, 128), F32)],
        in_specs=[ANY] * 3, out_specs=[VMEM_SPEC] * 2,
        scratch_shapes=[pltpu.VMEM((SEQ, ATT_DIM), F32), pltpu.VMEM((8, TQ, 128), F32),
                        pltpu.VMEM((8, TQ, 128), F32), pltpu.VMEM((8, TQ, 128), MM),
                        pltpu.VMEM((N_PAIRS, TQ, 256), MM), pltpu.VMEM((N_PAIRS, 2 * SEQ, 128), MM),
                        pltpu.VMEM((N_PAIRS, 2 * SEQ, 128), MM)]
                       + [pltpu.VMEM(a.shape, a.dtype) for a in (q, k, v)] + [pltpu.SemaphoreType.DMA((3,))],
        compiler_params=_params(),
    )(q, k, v)


def _branch_outputs(u_ref, a_ref, wcb_ref, bcb_ref, wab_ref):
    u = u_ref[...]
    a = a_ref[...]
    co = jnp.concatenate([_dot(u, wcb_ref[j]) for j in range(N_CHIPS)], axis=1) + bcb_ref[...]
    ao = jnp.concatenate([_dot(a, wab_ref[j]) for j in range(N_CHIPS)], axis=1)
    return co, ao


def mix_fwd(u3, att, gc, ga, x, w_cb_g, b_cb, w_ab_g, w_out_g, g2, g3, after):
    def body(u_ref, a_ref, gc_ref, ga_ref, x_ref, wcb_ref, bcb_ref, wab_ref, wout_ref, g2_ref, g3_ref, after_ref,
             mg_ref, x2_ref, h2_ref):
        co, ao = _branch_outputs(u_ref, a_ref, wcb_ref, bcb_ref, wab_ref)
        merged = (_sigmoid(gc_ref[...]) * co + _sigmoid(ga_ref[...]) * ao).astype(MM)
        mg_ref[...] = merged
        mix = _dot(merged, wout_ref[...])
        n2, _ = _rms(mix)
        x2 = x_ref[...] + n2 * g2_ref[...]
        x2_ref[...] = x2
        n3, _ = _rms(x2)
        h2_ref[...] = (n3 * g3_ref[...]).astype(MM)

    out_shape = [
        jax.ShapeDtypeStruct((SEQ, D_MODEL), MM), jax.ShapeDtypeStruct((SEQ, D_MODEL), F32),
        jax.ShapeDtypeStruct((SEQ, D_MODEL), MM),
    ]
    vec = _full_spec((1, D_MODEL))
    return pl.pallas_call(
        body, name="mix_fwd", grid=(SEQ // TM,), out_shape=out_shape,
        in_specs=[_row_tile_spec(CONV_DIM), _row_tile_spec(ATT_DIM), _row_tile_spec(D_MODEL),
                  _row_tile_spec(D_MODEL), _row_tile_spec(D_MODEL), _weight_spec(w_cb_g.shape), vec,
                  _weight_spec(w_ab_g.shape), _weight_spec(w_out_g.shape), vec, vec, TOKEN_SPEC],
        out_specs=[_row_tile_spec(D_MODEL)] * 3,
        compiler_params=_params(("arbitrary",)),
    )(u3, att, gc, ga, x, w_cb_g, b_cb, w_ab_g, w_out_g, g2, g3, after)


def ffn_loss(h2, w_up_g, w_down_g, x2, target, g4):
    def body(h_ref, wu_ref, wd_ref, x2_ref, t_ref, g_ref, act_ref, dff_ref, dy_ref, loss_ref, dg_ref, dgu_ref):
        h = h_ref[...]
        gate = jnp.concatenate([_dot(h, wu_ref[0]), _dot(h, wu_ref[1])], axis=1)
        up = jnp.concatenate([_dot(h, wu_ref[2]), _dot(h, wu_ref[3])], axis=1)
        sg = _sigmoid(gate)
        act = (gate * sg * up).astype(MM)
        act_ref[...] = act
        ff = _dot(act, wd_ref[...])
        n4, r4 = _rms(ff)
        g4v = g_ref[...]
        err = x2_ref[...] + n4 * g4v - t_ref[...]
        row_loss = jnp.mean(err * err, axis=-1, keepdims=True)
        loss_ref[...] = jnp.zeros((8, 128), F32) + 0.5 * jnp.sum(row_loss, axis=0, keepdims=True)
        dy = err * (1.0 / D_MODEL)
        dy_ref[...] = dy
        dff = _rms_bwd(dy * g4v, n4, r4).astype(MM)
        dff_ref[...] = dff
        _acc_rows(dg_ref, jnp.sum(dy * n4, axis=0, keepdims=True), pl.program_id(0) == 0)
        dact = _dot_nt(dff, wd_ref[...])
        dgu_ref[:, 0:D_FF] = (dact * up * (sg * (1.0 + gate * (1.0 - sg)))).astype(MM)
        dgu_ref[:, D_FF:2 * D_FF] = (dact * (gate * sg)).astype(MM)

    nt = SEQ // TM
    vec = _full_spec((1, D_MODEL))
    return pl.pallas_call(
        body, name="ffn_loss", grid=(nt,),
        out_shape=(jax.ShapeDtypeStruct((SEQ, D_FF), MM),
                   jax.ShapeDtypeStruct((SEQ, D_MODEL), MM), jax.ShapeDtypeStruct((SEQ, D_MODEL), F32),
                   jax.ShapeDtypeStruct((nt * 8, 128), F32), jax.ShapeDtypeStruct((1, D_MODEL), F32),
                   jax.ShapeDtypeStruct((SEQ, 2 * D_FF), MM)),
        in_specs=[_row_tile_spec(D_MODEL), _weight_spec(w_up_g.shape), _weight_spec(w_down_g.shape),
                  _row_tile_spec(D_MODEL), _row_tile_spec(D_MODEL), vec],
        out_specs=[_row_tile_spec(D_FF), _row_tile_spec(D_MODEL), _row_tile_spec(D_MODEL),
                   pl.BlockSpec((8, 128), lambda i: (i, 0)), vec, _row_tile_spec(2 * D_FF)],
        compiler_params=_params(("arbitrary",)),
    )(h2, w_up_g, w_down_g, x2, target, g4)


def ffn_merge_bwd(dgu, w_up_g, x2, merged, dy, g3, g2, w_out_g, gc, ga, u3, att, w_cb_g, b_cb, w_ab_g, after):
    def body(dgu_ref, w_ref, x2_ref, mix_ref, dy_ref, g3_ref, g2_ref,
             wout_ref, gc_ref, ga_ref, u_ref, a_ref, wcb_ref, bcb_ref, wab_ref, after_ref,
             dx2_ref, dmix_ref, dg3_ref, dg2_ref, dco_ref, dao_ref, dg_ref, du3_ref, datt_ref, dbcb_ref):
        dh2 = None
        for j in range(N_CHIPS):
            t = _dot_nt(dgu_ref[:, j * UP_SHARD:(j + 1) * UP_SHARD], w_ref[j])
            dh2 = t if dh2 is None else dh2 + t
        first = pl.program_id(0) == 0
        n3, r3 = _rms(x2_ref[...])
        dx2 = dy_ref[...] + _rms_bwd(dh2 * g3_ref[...], n3, r3)
        dx2_ref[...] = dx2
        _acc_rows(dg3_ref, jnp.sum(dh2 * n3, axis=0, keepdims=True), first)
        n2, r2 = _rms(_dot(mix_ref[...], wout_ref[...]))
        dmix = _rms_bwd(dx2 * g2_ref[...], n2, r2).astype(MM)
        dmix_ref[...] = dmix
        _acc_rows(dg2_ref, jnp.sum(dx2 * n2, axis=0, keepdims=True), first)

        dm = _dot_nt(dmix, wout_ref[...])
        co, ao = _branch_outputs(u_ref, a_ref, wcb_ref, bcb_ref, wab_ref)
        sgc = _sigmoid(gc_ref[...])
        sga = _sigmoid(ga_ref[...])
        dco = dm * sgc
        dao = dm * sga
        dg_ref[:, 0:D_MODEL] = (dm * co * (sgc * (1.0 - sgc))).astype(MM)
        dg_ref[:, D_MODEL:2 * D_MODEL] = (dm * ao * (sga * (1.0 - sga))).astype(MM)
        _acc_rows(dbcb_ref, jnp.sum(dco, axis=0, keepdims=True), pl.program_id(0) == 0)
        dco_ref[...] = dco.astype(MM)
        dao_ref[...] = dao.astype(MM)
        du3 = None
        datt = None
        for j in range(N_CHIPS):
            cols = slice(j * BR_SHARD, (j + 1) * BR_SHARD)
            t = _dot_nt(dco_ref[:, cols], wcb_ref[j])
            s = _dot_nt(dao_ref[:, cols], wab_ref[j])
            du3 = t if du3 is None else du3 + t
            datt = s if datt is None else datt + s
        du3_ref[...] = du3
        datt_ref[...] = datt.astype(MM)

    wide = _row_tile_spec(D_MODEL)
    vec = _full_spec((1, D_MODEL))
    wide_f32, wide_mm = jax.ShapeDtypeStruct((SEQ, D_MODEL), F32), jax.ShapeDtypeStruct((SEQ, D_MODEL), MM)
    vec_f32 = jax.ShapeDtypeStruct((1, D_MODEL), F32)
    return pl.pallas_call(
        body, name="ffn_merge_bwd", grid=(SEQ // TM,),
        out_shape=(wide_f32, wide_mm, vec_f32, vec_f32, wide_mm, wide_mm,
                   jax.ShapeDtypeStruct((SEQ, 2 * D_MODEL), MM),
                   jax.ShapeDtypeStruct((SEQ, CONV_DIM), F32), jax.ShapeDtypeStruct((SEQ, ATT_DIM), MM), vec_f32),
        in_specs=[_row_tile_spec(2 * D_FF), _weight_spec(w_up_g.shape), wide, wide, wide, vec, vec,
                  _weight_spec(w_out_g.shape), wide, wide, _row_tile_spec(CONV_DIM), _row_tile_spec(ATT_DIM),
                  _weight_spec(w_cb_g.shape), vec, _weight_spec(w_ab_g.shape), TOKEN_SPEC],
        out_specs=[wide, wide, vec, vec, wide, wide, _row_tile_spec(2 * D_MODEL), _row_tile_spec(CONV_DIM),
                   _row_tile_spec(ATT_DIM), vec],
        compiler_params=_params(("arbitrary",)),
    )(dgu, w_up_g, x2, merged, dy, g3, g2, w_out_g, gc, ga, u3, att, w_cb_g, b_cb, w_ab_g, after)


def conv_bwd(du3, u1, ci, w_dw, ln_g, ln_b, after):
    def body(du3_hbm, u1_hbm, ci_hbm, w_ref, g_ref, bb_ref, after_ref,
             dci_ref, dw_ref, dbdw_ref, dg_ref, db_ref, upad_ref, dpad_ref, dwacc_ref, vacc_ref,
             du3_ref, u1_ref, ci_ref, ci_sems, u1_sems, du3_sems):
        ci_chunk = _row_chunks(ci_hbm, ci_ref, ci_sems)
        u1_chunk = _row_chunks(u1_hbm, u1_ref, u1_sems)
        du3_chunk = _row_chunks(du3_hbm, du3_ref, du3_sems)
        _glu_into(ci_chunk, ci_ref, upad_ref)
        dpad_ref[SEQ:SEQ + 32, :] = jnp.zeros((32, CONV_DIM), F32)
        dwacc_ref[...] = jnp.zeros_like(dwacc_ref)
        vacc_ref[...] = jnp.zeros_like(vacc_ref)

        def fold8(t):
            s = t[0:8, :]
            for r in range(8, t.shape[0], 8):
                s = s + t[r:r + 8, :]
            return s

        def pass1(i, c):
            t0 = pl.multiple_of(i * CONV_TILE, CONV_TILE)
            gv = g_ref[...]
            for r in range(0, CONV_TILE, NORM_ROWS):
                rows = pl.ds(t0 + r, NORM_ROWS)
                xh, rstd = _layernorm_parts(u1_ref[rows, :])
                u2 = xh * gv + bb_ref[...]
                s2 = _sigmoid(u2)
                du2 = du3_ref[rows, :] * (s2 * (1.0 + u2 * (1.0 - s2)))
                wv = du2 * gv
                du1 = rstd * (wv - jnp.mean(wv, axis=-1, keepdims=True)
                              - xh * jnp.mean(wv * xh, axis=-1, keepdims=True))
                dpad_ref[rows, :] = du1
                vacc_ref[0] += fold8(du2 * xh)
                vacc_ref[1] += fold8(du2)
                vacc_ref[2] += fold8(du1)
            for cols in LANE_GROUPS:
                du1 = dpad_ref[pl.ds(t0, CONV_TILE), cols]
                for m, rows in _shifted_windows(upad_ref, t0, cols, range(2, CONV_WIDTH + 2)):
                    dwacc_ref[m - 2, :, cols] += fold8(du1 * rows)
            return c

        tiles_per_chunk = GLU_ROWS // CONV_TILE

        def pass1_chunk(ch, c):
            u1_chunk(ch).wait()
            du3_chunk(ch).wait()
            return lax.fori_loop(ch * tiles_per_chunk, (ch + 1) * tiles_per_chunk, pass1, c)

        lax.fori_loop(0, SEQ // GLU_ROWS, pass1_chunk, 0)

        def pass2(i, c):
            t0 = pl.multiple_of(i * CONV_TILE, CONV_TILE)
            tile = pl.ds(t0, CONV_TILE)
            for cols in LANE_GROUPS:
                gate_cols = slice(cols.start + CONV_DIM, cols.stop + CONV_DIM)
                du0 = _shifted_sum(dpad_ref, t0, cols, w_ref, lambda j: 30 - j)
                a = ci_ref[tile, cols]
                sb = _sigmoid(ci_ref[tile, gate_cols])
                dci_ref[tile, cols] = (du0 * sb).astype(MM)
                dci_ref[tile, gate_cols] = (du0 * a * (sb * (1.0 - sb))).astype(MM)
            return c

        lax.fori_loop(0, SEQ // CONV_TILE, pass2, 0)

        for j in range(CONV_WIDTH):
            dw_ref[j:j + 1, :] = jnp.sum(dwacc_ref[j], axis=0, keepdims=True)
        dw_ref[CONV_WIDTH:32, :] = jnp.zeros((32 - CONV_WIDTH, CONV_DIM), F32)
        dg_ref[...] = jnp.sum(vacc_ref[0], axis=0, keepdims=True)
        db_ref[...] = jnp.sum(vacc_ref[1], axis=0, keepdims=True)
        dbdw_ref[...] = jnp.sum(vacc_ref[2], axis=0, keepdims=True)

    vec = jax.ShapeDtypeStruct((1, CONV_DIM), F32)
    return pl.pallas_call(
        body, name="conv_bwd",
        out_shape=(jax.ShapeDtypeStruct((SEQ, 2 * CONV_DIM), MM), jax.ShapeDtypeStruct((32, CONV_DIM), F32),
                   vec, vec, vec),
        in_specs=[ANY] * 3 + [VMEM_SPEC] * 4, out_specs=[VMEM_SPEC] * 5,
        scratch_shapes=[pltpu.VMEM((SEQ + 32, CONV_DIM), F32), pltpu.VMEM((SEQ + 32, CONV_DIM), F32),
                        pltpu.VMEM((CONV_WIDTH, 8, CONV_DIM), F32), pltpu.VMEM((3, 8, CONV_DIM), F32)]
                       + [pltpu.VMEM(a.shape, a.dtype) for a in (du3, u1, ci)]
                       + [pltpu.SemaphoreType.DMA((SEQ // GLU_ROWS,))] * 3,
        compiler_params=_params(),
    )(du3, u1, ci, w_dw, ln_g, ln_b, after)


def attn_bwd(q, k, v, datt, rc, after):
    def body(q_hbm, k_hbm, v_hbm, do_hbm, rc_hbm, after_ref, dqkv_ref, dqa_ref, dka_ref, dva_ref, pc_ref, z_ref,
             sig1_ref, sig2_ref, g_ref, spb_ref, gb_ref, ar_ref, dzr_ref, dzc_ref, qm_ref, km_ref, dom_ref,
             q_ref, k_ref, v_ref, do_ref, rc_ref, sems):
        arrive = _fetch((q_hbm, k_hbm, do_hbm, v_hbm, rc_hbm), (q_ref, k_ref, do_ref, v_ref, rc_ref), sems)
        lane, row, _ = _head_masks()
        for ref in (dqa_ref, dka_ref, dva_ref, pc_ref):
            ref[...] = jnp.zeros_like(ref)
        z_ref[...] = jnp.full(z_ref.shape, NO_SCORE, F32)
        for ref in (sig1_ref, sig2_ref, spb_ref, ar_ref, g_ref, gb_ref, dzr_ref, dzc_ref):
            ref[...] = jnp.zeros_like(ref)
        for cp, (src_ref, split_ref) in zip(arrive, ((q_ref, qm_ref), (k_ref, km_ref), (do_ref, dom_ref))):
            cp.wait()
            _split_heads(src_ref, split_ref)
        arrive[3].wait()
        arrive[4].wait()
        w_suffix = _cumsum_weights(suffix=True, with_total=False)
        w_prefix = _cumsum_weights(suffix=False, with_total=True)

        def step(pairs):
            (ia, ja), (ib, jb), (ic, jc), (id_, jd) = pairs
            ka, qb_, kb_, kc, qd, kd = (pl.multiple_of(jnp.maximum(b, 0) * TQ, TQ) for b in (ja, ib, jb, jc, id_, jd))
            qa2, qb2, qc2, qd2, kd2 = (pl.multiple_of(jnp.maximum(b, 0) * 2 * TQ, 2 * TQ)
                                       for b in (ia, ib, ic, id_, jd))
            bias_a = _score_bias(lane, row, ia, ja)
            rc_rows = rc_ref[pl.ds(qb_, TQ), :]
            first_c = jc == 0
            for p in range(N_PAIRS):
                cols = slice(128 * p, 128 * (p + 1))
                k_a = k_ref[pl.ds(ka, TQ), cols]
                v_b = v_ref[pl.ds(kb_, TQ), cols]
                dqa_ref[pl.ds(qd, TQ), cols] += _dot(dzc_ref[p], km_ref[p, pl.ds(kd2, 2 * TQ), :])
                dka_ref[pl.ds(kd, TQ), cols] += _dot_tn(dzr_ref[p], qm_ref[p, pl.ds(qd2, 2 * TQ), :])
                dva_ref[pl.ds(kc, TQ), cols] += _dot_tn(ar_ref[p], dom_ref[p, pl.ds(qc2, 2 * TQ), :])
                for h in range(2):
                    hh = 2 * p + h
                    rows = slice(TQ * h, TQ * (h + 1))
                    r = _dot(gb_ref[hh], w_prefix)
                    p_in = jnp.where(first_c, 0.0, pc_ref[hh])
                    dz = (g_ref[hh] - sig2_ref[hh] * (r[:, :128] + p_in)).astype(MM)
                    dzc_ref[p, :, rows] = dz
                    dzr_ref[p, rows, :] = dz
                    pc_ref[hh] = p_in + r[:, 128:]
                    r_in = jnp.sum(jnp.where(lane == 16 * hh + jb, rc_rows, 0.0), axis=1, keepdims=True)
                    a = jnp.exp(z_ref[hh] - (_dot(spb_ref[hh], w_suffix) + r_in))
                    g = _dot_nt(dom_ref[p, pl.ds(qb2 + TQ * h, TQ), :], v_b) * a
                    ar_ref[p, rows, :] = a.astype(MM)
                    g_ref[hh] = g
                    gb_ref[hh] = g.astype(MM)
                    sig2_ref[hh] = sig1_ref[hh]
                    z = _dot_nt(qm_ref[p, pl.ds(qa2 + TQ * h, TQ), :], k_a) + bias_a
                    sp = _softplus(z)
                    sig1_ref[hh] = jnp.exp(z - sp)
                    z_ref[hh] = z
                    spb_ref[hh] = sp.astype(MM)

        _block_pipeline(4, False, step)
        dqkv_ref[:, 0:ATT_DIM] = (dqa_ref[...] * ATT_SCALE).astype(MM)
        dqkv_ref[:, ATT_DIM:2 * ATT_DIM] = dka_ref[...].astype(MM)
        dqkv_ref[:, 2 * ATT_DIM:3 * ATT_DIM] = dva_ref[...].astype(MM)

    split = pltpu.VMEM((N_PAIRS, 2 * SEQ, 128), MM)
    return pl.pallas_call(
        body, name="attn_bwd", out_shape=jax.ShapeDtypeStruct((SEQ, 3 * ATT_DIM), MM),
        in_specs=[ANY] * 5 + [VMEM_SPEC], out_specs=VMEM_SPEC,
        scratch_shapes=[pltpu.VMEM((SEQ, ATT_DIM), F32)] * 3 + [pltpu.VMEM((8, TQ, 128), F32)] * 5
                       + [pltpu.VMEM((8, TQ, 128), MM)] * 2
                       + [pltpu.VMEM((N_PAIRS, 2 * TQ, 128), MM)] * 2 + [pltpu.VMEM((N_PAIRS, TQ, 256), MM)]
                       + [split] * 3
                       + [pltpu.VMEM(a.shape, a.dtype) for a in (q, k, v, datt, rc)] + [pltpu.SemaphoreType.DMA((5,))],
        compiler_params=_params(),
    )(q, k, v, datt, rc, after)


DPROJ_PIECES = ((0, 1024), (1024, 2560), (2560, 4608))


def _dproj_segments(j):
    g0, g1 = j * IN_SHARD, (j + 1) * IN_SHARD
    segs = []
    for p, (s, e) in enumerate(DPROJ_PIECES):
        lo, hi = max(s, g0), min(e, g1)
        if lo < hi:
            segs.append((p, lo - s, lo - g0, hi - lo))
    return segs


def in_proj_bwd(pieces, w_in_g, x, dx2, g1, after):
    def body(p0_ref, p1_ref, p2_ref, w_ref, x_ref, dx2_ref, g_ref, after_ref, dx_ref, dg_ref):
        p_refs = (p0_ref, p1_ref, p2_ref)
        dh = None
        for j in range(N_CHIPS):
            for p, lo, off, width in _dproj_segments(j):
                t = _dot_nt(p_refs[p][:, lo:lo + width], w_ref[j, :, off:off + width])
                dh = t if dh is None else dh + t
        n1, r1 = _rms(x_ref[...])
        dx_ref[...] = dx2_ref[...] + _rms_bwd(dh * g_ref[...], n1, r1)
        _acc_rows(dg_ref, jnp.sum(dh * n1, axis=0, keepdims=True), pl.program_id(0) == 0)

    vec = _full_spec((1, D_MODEL))
    return pl.pallas_call(
        body, name="in_proj_bwd", grid=(SEQ // TM,),
        out_shape=[jax.ShapeDtypeStruct((SEQ, D_MODEL), F32), jax.ShapeDtypeStruct((1, D_MODEL), F32)],
        in_specs=[_row_tile_spec(p.shape[1]) for p in pieces]
                 + [_weight_spec(w_in_g.shape), _row_tile_spec(D_MODEL), _row_tile_spec(D_MODEL), vec, TOKEN_SPEC],
        out_specs=[_row_tile_spec(D_MODEL), vec],
        compiler_params=_params(("arbitrary",)),
    )(*pieces, w_in_g, x, dx2, g1, after)


def weight_grad_in(h1, pieces):
    kh = D_MODEL // 2
    operands = (h1,) + tuple(pieces)
    out = jax.ShapeDtypeStruct((N_CHIPS, 2, kh, IN_SHARD), MM)

    def body(*refs):
        hbm, o_hbm, bufs, stage_ref, in_sems, out_sems = refs[:4], refs[4], refs[5:9], refs[9], refs[10], refs[11]
        arrive = _fetch(hbm, bufs, in_sems)
        a_ref, p_refs = bufs[0], bufs[1:]
        arrive[0].wait()
        here, leaving = set(), []
        for j in range(N_CHIPS):
            for p in sorted({seg[0] for seg in _dproj_segments(j)} - here):
                arrive[1 + p].wait()
                here.add(p)
            for h in range(2):
                a = a_ref[:, h * kh:(h + 1) * kh]
                for p, lo, off, width in _dproj_segments(j):
                    stage_ref[j, h, :, off:off + width] = _dot_tn(a, p_refs[p][:, lo:lo + width]).astype(MM)
                leaving.append(pltpu.make_async_copy(stage_ref.at[j, h], o_hbm.at[j, h], out_sems.at[2 * j + h]))
                leaving[-1].start()
        for cp in leaving:
            cp.wait()

    return pl.pallas_call(
        body, name="dw_in", out_shape=out, in_specs=[ANY] * 4, out_specs=ANY,
        scratch_shapes=[pltpu.VMEM(a.shape, a.dtype) for a in operands]
                       + [pltpu.VMEM(out.shape, out.dtype), pltpu.SemaphoreType.DMA((4,)),
                          pltpu.SemaphoreType.DMA((2 * N_CHIPS,))],
        compiler_params=_params(),
    )(*operands)


def weight_grad(a, b, name, col_sharded, tk=None):
    kin, n = a.shape[1], b.shape[1]

    def body(a_ref, b_ref, o_ref):
        if col_sharded:
            o_ref[0, 0] = _dot_tn(a_ref[...], b_ref[...]).astype(MM)
        else:
            o_ref[...] = _dot_tn(a_ref[...], b_ref[...]).astype(MM)

    if col_sharded:
        kh, ns = kin // 2, n // N_CHIPS
        out = jax.ShapeDtypeStruct((N_CHIPS, 2, kh, ns), MM)
        grid = (2, N_CHIPS)
        in_specs = [pl.BlockSpec((SEQ, kh), lambda h, j: (0, h)), pl.BlockSpec((SEQ, ns), lambda h, j: (0, j))]
        out_spec = pl.BlockSpec((1, 1, kh, ns), lambda h, j: (j, h, 0, 0))
        sem = ("arbitrary", "arbitrary")
    else:
        out = jax.ShapeDtypeStruct((kin, n), MM)
        grid = (kin // tk,)
        in_specs = [pl.BlockSpec((SEQ, tk), lambda r: (0, r)), pl.BlockSpec((SEQ, n), lambda r: (0, 0))]
        out_spec = pl.BlockSpec((tk, n), lambda r: (r, 0))
        sem = ("arbitrary",)
    res = pl.pallas_call(
        body, name=name, grid=grid, out_shape=out, in_specs=in_specs, out_specs=out_spec,
        compiler_params=_params(sem),
    )(a, b)
    if not col_sharded:
        res = res.reshape(N_CHIPS, 2, kin // (2 * N_CHIPS), n)
    return res


def weight_grad_mix(merged, dmix, u3, dco, att, dao):
    operands = (merged, dmix, u3, dco, att, dao)
    n_out, n_br = D_MODEL // 2, CONV_DIM // 2

    def body(*refs):
        hbm, (o_out, o_cb, o_ab), bufs, sems = refs[:6], refs[6:9], refs[9:15], refs[15]
        copies = [pltpu.make_async_copy(hbm[i], bufs[i], sems.at[i]) for i in range(6)]
        for cp in copies:
            cp.start()
        m_ref, dm_ref, u_ref, dco_ref, a_ref, dao_ref = bufs
        copies[0].wait()
        copies[1].wait()
        for h in range(2):
            o_out[h * n_out:(h + 1) * n_out, :] = _dot_tn(m_ref[:, h * n_out:(h + 1) * n_out], dm_ref[...]).astype(MM)
        for br, (a, d, o) in enumerate(((u_ref, dco_ref, o_cb), (a_ref, dao_ref, o_ab))):
            copies[2 + 2 * br].wait()
            copies[3 + 2 * br].wait()
            for h in range(2):
                g = _dot_tn(a[:, h * n_br:(h + 1) * n_br], d[...])
                for j in range(N_CHIPS):
                    o[j, h] = g[:, j * BR_SHARD:(j + 1) * BR_SHARD].astype(MM)

    branch = jax.ShapeDtypeStruct((N_CHIPS, 2, n_br, BR_SHARD), MM)
    dw_out, dw_cb, dw_ab = pl.pallas_call(
        body, name="dw_mix", out_shape=[jax.ShapeDtypeStruct((D_MODEL, D_MODEL), MM), branch, branch],
        in_specs=[ANY] * 6, out_specs=[VMEM_SPEC] * 3,
        scratch_shapes=[pltpu.VMEM(a.shape, a.dtype) for a in operands] + [pltpu.SemaphoreType.DMA((6,))],
        compiler_params=_params(),
    )(*operands)
    return dw_out.reshape(N_CHIPS, 2, D_MODEL // (2 * N_CHIPS), D_MODEL), dw_cb, dw_ab


def _place():
    x, y, c = lax.axis_index("x"), lax.axis_index("y"), lax.axis_index("c")
    chips = [(1 - x, y), (x, 1 - y), (1 - x, 1 - y)]
    return x, y, c, chips


def _rcopy(src, dst, send_sem, recv_sem, dev):
    return pltpu.make_async_remote_copy(src_ref=src, dst_ref=dst, send_sem=send_sem, recv_sem=recv_sem,
                                        device_id=dev, device_id_type=MESH)


class _Gather:
    N_MOVES = 6

    def __init__(self, shapes, w, o, scratch):
        self.n, self.shapes, self.w, self.o = len(w), shapes, w, o
        self.send, self.recv, self.psend, self.precv, self.loc_in, self.loc_out = scratch[:6]
        self.raw, self.stage = scratch[6:6 + self.n], scratch[6 + self.n:]
        x, y, c, self.chips = _place()
        self.c = c
        self.me, k_x, k_y, k_far = 2 * x + y, 2 * (1 - x) + y, 2 * x + (1 - y), 2 * (1 - x) + (1 - y)
        to_x, to_y = (1 - x, y, c), (x, 1 - y, c)
        self.sib = (x, y, 1 - c)
        self.sent_as = [(self.me, 0, to_x), (self.me, 1, to_y), (self.me, 1, to_x), (self.me, 0, to_y),
                        (k_x, 0, to_y), (k_y, 1, to_x)]
        self.arrives_as = [(k_x, 0, to_x), (k_y, 1, to_y), (k_x, 1, to_x), (k_y, 0, to_y),
                           (k_far, 0, to_y), (k_far, 1, to_x)]
        self.sent_on_after = {0: 4, 1: 5}

    @staticmethod
    def scratch(shards):
        n = len(shards)
        sems = pltpu.SemaphoreType.DMA
        m = _Gather.N_MOVES * n
        return ([sems((m,)), sems((m,)), sems((m,)), sems((m,)), sems((3 * n,)), sems((n,))]
                + [pltpu.VMEM(s.shape, s.dtype) for s in shards] + [pltpu.VMEM(s.shape, MM) for s in shards])

    @staticmethod
    def out_shapes(shards):
        return [jax.ShapeDtypeStruct((N_CHIPS,) + s.shape, MM) for s in shards]

    def _rows(self, t, quarter, cc):
        rq = self.shapes[t][0] // 4
        return pl.ds(pl.multiple_of((2 * cc + quarter) * rq, rq), rq)

    def _own_rows(self, t, piece):
        if piece < 2:
            return self._rows(t, piece, self.c)
        rh = self.shapes[t][0] // 2
        return pl.ds(pl.multiple_of((1 - self.c) * rh, rh), rh)

    def _chip(self, j):
        cx, cy = self.chips[j]
        return 2 * cx + cy, (cx, cy, self.c)

    def local_in(self, t, piece):
        rows = self._own_rows(t, piece)
        return pltpu.make_async_copy(self.w[t].at[rows, :], self.raw[t].at[rows, :], self.loc_in.at[3 * t + piece])

    def local_out(self, t):
        return pltpu.make_async_copy(self.stage[t], self.o[t].at[self.me], self.loc_out.at[t])

    def sent(self, i, t):
        k, quarter, dev = self.sent_as[i]
        rows = self._rows(t, quarter, self.c)
        there = self.o[t].at[k, rows, :]
        return _rcopy(self.stage[t].at[rows, :] if i < 4 else there, there,
                      self.send.at[i * self.n + t], self.recv.at[i * self.n + t], dev)

    def arrived(self, i, t):
        k, quarter, dev = self.arrives_as[i]
        blk = self.o[t].at[k, self._rows(t, quarter, self.c), :]
        return _rcopy(blk, blk, self.send.at[i * self.n + t], self.recv.at[i * self.n + t], dev)

    def passed(self, i, t, cc):
        k, quarter, _ = self.arrives_as[i]
        blk = self.o[t].at[k, self._rows(t, quarter, cc), :]
        return _rcopy(blk, blk, self.psend.at[i * self.n + t], self.precv.at[i * self.n + t], self.sib)

    def start(self):
        for piece in range(3):
            for t in range(self.n):
                self.local_in(t, piece).start()
        for piece, moves in enumerate(((0, 3), (1, 2), ())):
            for t in range(self.n):
                rows = self._own_rows(t, piece)
                self.local_in(t, piece).wait()
                self.stage[t][rows, :] = self.raw[t][rows, :].astype(MM)
                for i in moves:
                    self.sent(i, t).start()
        for t in range(self.n):
            self.local_out(t).start()

    def forward(self):
        for i in range(self.N_MOVES):
            for t in range(self.n):
                self.arrived(i, t).wait_recv()
                if i in self.sent_on_after:
                    self.sent(self.sent_on_after[i], t).start()
                self.passed(i, t, self.c).start()

    def finish(self):
        for i in range(self.N_MOVES):
            for t in range(self.n):
                self.passed(i, t, 1 - self.c).wait_recv()
        for i in range(self.N_MOVES):
            for t in range(self.n):
                self.sent(i, t).wait_send()
                self.passed(i, t, self.c).wait_send()
        for t in range(self.n):
            self.local_out(t).wait()


def all_gather_weights(shards, small, later):
    n, m = len(shards), len(later)
    shapes = [s.shape for s in shards]

    def body(*refs):
        w = refs[:n]
        sm = refs[n]
        lw = refs[n + 1:n + 1 + m]
        o = refs[n + 1 + m:2 * n + 1 + m]
        osm = refs[2 * n + 1 + m]
        lo = refs[2 * n + 2 + m:2 * n + 2 + 2 * m]
        scratch = refs[2 * n + 2 + 2 * m:]
        ssend, srecv, sloc, lsem_in, lsem_out = scratch[:5]
        lraw, lstage = scratch[5:5 + m], scratch[5 + m:5 + 2 * m]
        g = _Gather(shapes, w, o, scratch[5 + 2 * m:])
        own = pltpu.make_async_copy(sm, osm.at[g.me], sloc)
        own.start()
        g.start()
        loads = [pltpu.make_async_copy(lw[t], lraw[t], lsem_in.at[t]) for t in range(m)]
        for cp in loads:
            cp.start()
        small_cps = [_rcopy(sm, osm.at[g.me], ssend.at[j], srecv.at[j], g._chip(j)[1]) for j in range(3)]
        for cp in small_cps:
            cp.start()
        places = []
        for t in range(m):
            loads[t].wait()
            lstage[t][...] = lraw[t][...].astype(MM)
            places.append(pltpu.make_async_copy(lstage[t], lo[t].at[g.me], lsem_out.at[t]))
            places[t].start()
        g.forward()
        g.finish()
        for j in range(3):
            k, dev = g._chip(j)
            _rcopy(sm, osm.at[k], ssend.at[j], srecv.at[j], dev).wait_recv()
            small_cps[j].wait_send()
        own.wait()
        for cp in places:
            cp.wait()

    out_shape = _Gather.out_shapes(shards)
    out_shape.append(jax.ShapeDtypeStruct((N_CHIPS,) + small.shape, small.dtype))
    out_shape += _Gather.out_shapes(later)
    sems = pltpu.SemaphoreType.DMA
    return pl.pallas_call(
        body, name="all_gather_weights", out_shape=out_shape,
        in_specs=[ANY] * (n + 1 + m), out_specs=[ANY] * (n + 1 + m),
        scratch_shapes=[sems((3,)), sems((3,)), sems, sems((m,)), sems((m,))]
                       + [pltpu.VMEM(s.shape, s.dtype) for s in later] + [pltpu.VMEM(s.shape, MM) for s in later]
                       + _Gather.scratch(shards),
        compiler_params=_params(),
    )(*shards, small, *later)


HBM_SPEC = pl.BlockSpec(memory_space=pltpu.HBM)
SEM_SPEC = pl.BlockSpec(memory_space=pltpu.SEMAPHORE)
DATAFLOW = pltpu.SideEffectType.DATAFLOW_SIDE_EFFECTING


def split_start(name, bufs, n_copies, copies):
    nb = len(bufs)

    def body(*refs):
        for cp in copies(refs[:nb], refs[nb], refs[nb + 1]):
            cp.start()
        token = refs[2 * nb + 2]
        token[...] = jnp.zeros_like(token)

    sems = [pltpu.SemaphoreType.DMA((n_copies,))] * 2
    res = pl.pallas_call(
        body, name=name,
        out_shape=sems + [pltpu.HBM(a.shape, a.dtype) for a in bufs] + [jax.ShapeDtypeStruct((8, 128), F32)],
        in_specs=[HBM_SPEC] * nb, out_specs=[SEM_SPEC] * 2 + [HBM_SPEC] * nb + [VMEM_SPEC],
        input_output_aliases={i: 2 + i for i in range(nb)},
        compiler_params=pltpu.CompilerParams(has_side_effects=DATAFLOW),
    )(*[pltpu.with_memory_space_constraint(a, pltpu.HBM) for a in bufs])
    return res[:-1], res[-1]


def split_wait(name, state, after, copies):
    sems, bufs = state[:2], state[2:]
    nb = len(bufs)

    def body(*refs):
        for cp in copies(refs[:nb], refs[nb], refs[nb + 1]):
            cp.wait_send()
            cp.wait_recv()

    return pl.pallas_call(
        body, name=name, out_shape=[pltpu.HBM(a.shape, a.dtype) for a in bufs],
        in_specs=[HBM_SPEC] * nb + [SEM_SPEC] * 2 + [ANY] * len(after), out_specs=[HBM_SPEC] * nb,
        input_output_aliases={i: i for i in range(nb)},
        compiler_params=pltpu.CompilerParams(has_side_effects=DATAFLOW),
    )(*bufs, *sems, *after)


class _Shifted:
    def __init__(self, sems, first):
        self.sems, self.first = sems, first

    @property
    def at(self):
        return self

    def __getitem__(self, i):
        return self.sems.at[self.first + i]


def _scatter_copies(n):
    def copies(refs, send, recv):
        _, _, c, chips = _place()
        return [_rcopy(refs[t].at[2 * cx + cy], refs[n + t].at[j], send.at[3 * t + j], recv.at[3 * t + j], (cx, cy, c))
                for t in range(n) for j, (cx, cy) in enumerate(chips)]
    return copies


def _direct_copies(n):
    def copies(refs, send, recv):
        x, y, c, chips = _place()
        out = []
        for t in range(n):
            src, land = refs[t], refs[n + t]
            for to_core, first in ((c, 0), (1 - c, 3)):
                for j, (cx, cy) in enumerate(chips):
                    i = 7 * t + first + j
                    out.append(_rcopy(src.at[2 * cx + cy, to_core], land.at[first + j], send.at[i], recv.at[i],
                                      (cx, cy, to_core)))
            i = 7 * t + 6
            out.append(_rcopy(src.at[2 * x + y, 1 - c], land.at[6], send.at[i], recv.at[i], (x, y, 1 - c)))
        return out
    return copies


def _scatter_and_direct(n, m):
    def copies(refs, send, recv):
        return (_scatter_copies(n)(refs[:2 * n], send, recv)
                + _direct_copies(m)(refs[2 * n:], _Shifted(send, 3 * n), _Shifted(recv, 3 * n)))
    return copies


def scatter_start(parts, partials):
    n, m = len(parts), len(partials)
    lands = [lax.empty((3,) + p.shape[1:], p.dtype) for p in parts]
    direct_lands = [lax.empty((7,) + p.shape[2:], p.dtype) for p in partials]
    return split_start("scatter_start_rest", list(parts) + lands + list(partials) + direct_lands, 3 * n + 7 * m,
                       _scatter_and_direct(n, m))


def scatter_wait(state, after, n, m):
    res = split_wait("scatter_wait_rest", state, after, _scatter_and_direct(n, m))
    return res[n:2 * n], res[2 * n:2 * n + m], res[2 * n + m:]


def _gather_copies(shapes, level, to_both_cores=()):
    n = len(shapes)

    def copies(refs, send, recv):
        x, y, c, chips = _place()
        out = []
        for t in list(range(n)) + list(to_both_cores):
            rh = shapes[t][0] // 2
            to_core = c if len(out) < 3 * n else 1 - c
            for cx, cy in chips:
                k, dev = (2 * x + y, (cx, cy, to_core)) if level == 1 else (2 * cx + cy, (x, y, 1 - c))
                blk = refs[t].at[k, pl.ds(c * rh, rh), :]
                out.append(_rcopy(blk, blk, send.at[len(out)], recv.at[len(out)], dev))
        return out
    return copies


def _sibling_copies(n, other_half):
    def copies(refs, send, recv):
        x, y, c, _ = _place()
        return [_rcopy(refs[t].at[:, 1 - c] if other_half else refs[t], refs[n + t], send.at[t], recv.at[t],
                       (x, y, 1 - c)) for t in range(n)]
    return copies


def sibling_start(srcs, other_half, tag):
    lands = [lax.empty((a.shape[0],) + a.shape[2:] if other_half else a.shape, a.dtype) for a in srcs]
    return split_start("sibling_start_" + tag, list(srcs) + lands, len(srcs),
                       _sibling_copies(len(srcs), other_half))


def sibling_wait(state, after, other_half, tag):
    n = (len(state) - 2) // 2
    res = split_wait("sibling_wait_" + tag, state, after, _sibling_copies(n, other_half))
    return res[:n], res[n:]


def small_pack(ddw, v512, v1024, loss_parts):
    rows, width = PACK_ROWS, 512
    n512, n1024 = len(VEC512), len(VEC1024)

    def body(*refs):
        ddw_ref = refs[0]
        a_refs = refs[1:1 + n512]
        b_refs = refs[1 + n512:1 + n512 + n1024]
        lp_ref, o_ref, p_ref = refs[1 + n512 + n1024:]
        p_ref[...] = jnp.zeros_like(p_ref)
        p_ref[0:32, :] = ddw_ref[...]
        p_ref[LOSS_ROW:LOSS_ROW + 1, 0:128] = jnp.sum(lp_ref[...], axis=0, keepdims=True) * 0.125
        for i, r in enumerate(a_refs):
            p_ref[32 + i:33 + i, :] = r[...]
        for i, r in enumerate(b_refs):
            base = 32 + n512 + 2 * i
            p_ref[base:base + 1, :] = r[:, 0:512]
            p_ref[base + 1:base + 2, :] = r[:, 512:1024]
        x, y, c, _ = _place()
        o_ref[4 * x + 2 * y + c] = p_ref[...]

    n_in = 2 + n512 + n1024
    return pl.pallas_call(
        body, name="small_pack", out_shape=jax.ShapeDtypeStruct((8, rows, width), F32),
        in_specs=[VMEM_SPEC] * n_in, out_specs=VMEM_SPEC,
        scratch_shapes=[pltpu.VMEM((rows, width), F32)],
    )(ddw, *[v512[n] for n in VEC512], *[v1024[n] for n in VEC1024], loss_parts)


def _small_copies(refs, send, recv):
    x, y, c, _ = _place()
    mine = refs[0].at[4 * x + 2 * y + c]
    peers = [(1 - x if k & 4 else x, 1 - y if k & 2 else y, 1 - c if k & 1 else c) for k in range(1, 8)]
    return [_rcopy(mine, mine, send.at[i], recv.at[i], dev) for i, dev in enumerate(peers)]


def _row_block(r):
    for tr in (512, 352, 256, 128):
        if r % tr == 0:
            return tr
    return r


def add_halves(g, recv, name):
    _, _, r, w = g.shape
    tr = _row_block(r)

    def body(g_ref, r_ref, ob_ref, own_ref):
        k = pl.program_id(1)
        me = 2 * lax.axis_index("x") + lax.axis_index("y")
        t = g_ref[0, 0].astype(F32) + r_ref[0].astype(F32)
        ob_ref[0] = t.astype(MM)
        mine = jnp.where(k == me, t, 0.0)

        @pl.when(k == 0)
        def _():
            own_ref[...] = mine

        @pl.when(k != 0)
        def _():
            own_ref[...] += mine

    return pl.pallas_call(
        body, name=name, grid=(r // tr, N_CHIPS),
        in_specs=[pl.BlockSpec((1, 1, tr, w), lambda i, k: (k, lax.axis_index("c"), i, 0)),
                  pl.BlockSpec((1, tr, w), lambda i, k: (k, i, 0))],
        out_specs=[pl.BlockSpec((1, tr, w), lambda i, k: (k, i, 0)),
                   pl.BlockSpec((tr, w), lambda i, k: (i, 0))],
        out_shape=(jax.ShapeDtypeStruct((N_CHIPS, r, w), MM), jax.ShapeDtypeStruct((r, w), F32)),
        compiler_params=_params(("arbitrary", "arbitrary")),
    )(g, recv)


def sum_parts(own, rin, after, name):
    _, r, w = rin.shape
    tr = _row_block(r)

    def body(o_ref, r_ref, after_ref, out_ref):
        out_ref[...] = ((o_ref[...] + r_ref[0].astype(F32)) + r_ref[1].astype(F32)) + r_ref[2].astype(F32)

    return pl.pallas_call(
        body, name=name, grid=(r // tr,), out_shape=jax.ShapeDtypeStruct((r, w), F32),
        in_specs=[pl.BlockSpec((tr, w), lambda i: (i, 0)), pl.BlockSpec((3, tr, w), lambda i: (0, i, 0)),
                  TOKEN_SPEC],
        out_specs=pl.BlockSpec((tr, w), lambda i: (i, 0)),
        compiler_params=_params(("arbitrary",)),
    )(own, rin, after)


def sum_partials(p, land, after, name):
    _, _, r, w = p.shape
    tr = _row_block(r)

    def body(p_ref, l_ref, after_ref, out_ref):
        total = p_ref[0, 0].astype(F32)
        for slot in (6, 0, 3, 1, 4, 2, 5):
            total = total + l_ref[slot].astype(F32)
        out_ref[...] = total

    def own(i):
        return 2 * lax.axis_index("x") + lax.axis_index("y"), lax.axis_index("c"), i, 0

    return pl.pallas_call(
        body, name=name, grid=(r // tr,), out_shape=jax.ShapeDtypeStruct((r, w), F32),
        in_specs=[pl.BlockSpec((1, 1, tr, w), own), pl.BlockSpec((7, tr, w), lambda i: (0, i, 0)), TOKEN_SPEC],
        out_specs=pl.BlockSpec((tr, w), lambda i: (i, 0)),
        compiler_params=_params(("arbitrary",)),
    )(p, land, after)


def _adamw_math(w, g, m, v):
    mn = ADAM_B1 * m + (1.0 - ADAM_B1) * g
    vn = ADAM_B2 * v + (1.0 - ADAM_B2) * (g * g)
    m_hat = mn / (1.0 - ADAM_B1 ** ADAM_STEP)
    v_hat = vn / (1.0 - ADAM_B2 ** ADAM_STEP)
    return -ADAM_LR * (m_hat / (jnp.sqrt(v_hat) + ADAM_EPS) + ADAM_WD * w), mn, vn


def adamw(w, mine, other, m, v, name):
    r, c = w.shape
    rh = r // 2
    tr = _row_block(rh)
    if c >= 1024 and tr % 512 == 0:
        tr = 256
    nb = rh // tr

    def body(w_ref, a_ref, b_ref, m_ref, v_ref, go_ref, d_ref, mo_ref, vo_ref):
        gv = jnp.where(lax.axis_index("c") == pl.program_id(0), a_ref[...], b_ref[...])
        go_ref[...] = gv
        d_ref[...], mo_ref[...], vo_ref[...] = _adamw_math(w_ref[...], gv, m_ref[...], v_ref[...])

    def half(of_sibling):
        def index(h, i):
            owner = lax.axis_index("c")
            owner = 1 - owner if of_sibling else owner
            return jnp.where(h == owner, i, jnp.where(h < owner, 0, nb - 1)), 0
        return pl.BlockSpec((tr, c), index)

    spec = pl.BlockSpec((tr, c), lambda h, i: (h * nb + i, 0))
    out = jax.ShapeDtypeStruct((r, c), F32)
    return pl.pallas_call(
        body, name=name, grid=(2, nb), out_shape=(out, out, out, out),
        in_specs=[spec, half(False), half(True), spec, spec], out_specs=[spec] * 4,
        compiler_params=_params(("arbitrary", "arbitrary")),
    )(w, mine, other, m, v)


def adamw_small(packs, params, after):
    names = list(params)
    flat = [a for n in names for a in params[n]]

    def body(*refs):
        p_ref = refs[0]
        ins = refs[1:1 + 3 * len(names)]
        loss_ref, g_ref = refs[2 + 3 * len(names):4 + 3 * len(names)]
        outs = refs[4 + 3 * len(names):]
        total = p_ref[0]
        for d in range(1, 8):
            total = total + p_ref[d]
        g_ref[...] = total
        loss_ref[...] = g_ref[LOSS_ROW:LOSS_ROW + 1, 0:1]
        me = 2 * lax.axis_index("x") + lax.axis_index("y")
        for i, n in enumerate(names):
            w_ref, m_ref, v_ref = ins[3 * i:3 * i + 3]
            go_ref, d_ref, mo_ref, vo_ref = outs[4 * i:4 * i + 4]
            if n == "conv_dw_w":
                gv = jnp.zeros((CONV_WIDTH, 128), F32)
                for k in range(N_CHIPS):
                    gv = gv + jnp.where(me == k, g_ref[0:CONV_WIDTH, 128 * k:128 * (k + 1)], 0.0)
            elif n in VEC512:
                r0 = 32 + VEC512.index(n)
                gv = g_ref[r0:r0 + 1, :]
            else:
                r0 = 32 + len(VEC512) + 2 * VEC1024.index(n)
                gv = jnp.concatenate([g_ref[r0:r0 + 1, :], g_ref[r0 + 1:r0 + 2, :]], axis=1)
            go_ref[...] = gv
            d_ref[...], mo_ref[...], vo_ref[...] = _adamw_math(w_ref[...], gv, m_ref[...], v_ref[...])

    out_shape = [jax.ShapeDtypeStruct((1, 1), F32), jax.ShapeDtypeStruct(packs.shape[1:], F32)]
    out_shape += [jax.ShapeDtypeStruct(params[n][0].shape, F32) for n in names for _ in range(4)]
    res = pl.pallas_call(
        body, name="adamw_small", out_shape=out_shape,
        in_specs=[VMEM_SPEC] * (2 + len(flat)), out_specs=[VMEM_SPEC] * len(out_shape),
        compiler_params=_params(),
    )(packs, *flat, after)
    return res[0], res[1], {n: res[2 + 4 * i:6 + 4 * i] for i, n in enumerate(names)}


REST = ("w_ffn_up", "w_ffn_down", "w_out", "w_conv_branch", "w_att_branch")
VEC512 = ("conv_dw_b", "conv_ln_g", "conv_ln_b")
VEC1024 = ("norm_mix_pre", "b_conv_branch", "norm_mix_post", "norm_ffn_pre", "norm_ffn_post")
PACK_ROWS = 48
LOSS_ROW = 47


def kernel(x, norm_mix_pre, w_in, conv_dw_w, conv_dw_b, conv_ln_g, conv_ln_b, w_conv_branch, b_conv_branch, w_att_branch, w_out, norm_mix_post, norm_ffn_pre, w_ffn_up, w_ffn_down, norm_ffn_post, loss_target, m_norm_mix_pre, m_w_in, m_conv_dw_w, m_conv_dw_b, m_conv_ln_g, m_conv_ln_b, m_w_conv_branch, m_b_conv_branch, m_w_att_branch, m_w_out, m_norm_mix_post, m_norm_ffn_pre, m_w_ffn_up, m_w_ffn_down, m_norm_ffn_post, v_norm_mix_pre, v_w_in, v_conv_dw_w, v_conv_dw_b, v_conv_ln_g, v_conv_ln_b, v_w_conv_branch, v_b_conv_branch, v_w_att_branch, v_w_out, v_norm_mix_post, v_norm_ffn_pre, v_w_ffn_up, v_w_ffn_down, v_norm_ffn_post):
    weights = dict(norm_mix_pre=norm_mix_pre, w_in=w_in, conv_dw_w=conv_dw_w, conv_dw_b=conv_dw_b, conv_ln_g=conv_ln_g, conv_ln_b=conv_ln_b, w_conv_branch=w_conv_branch, b_conv_branch=b_conv_branch, w_att_branch=w_att_branch, w_out=w_out, norm_mix_post=norm_mix_post, norm_ffn_pre=norm_ffn_pre, w_ffn_up=w_ffn_up, w_ffn_down=w_ffn_down, norm_ffn_post=norm_ffn_post)
    mom = dict(norm_mix_pre=m_norm_mix_pre, w_in=m_w_in, conv_dw_w=m_conv_dw_w, conv_dw_b=m_conv_dw_b, conv_ln_g=m_conv_ln_g, conv_ln_b=m_conv_ln_b, w_conv_branch=m_w_conv_branch, b_conv_branch=m_b_conv_branch, w_att_branch=m_w_att_branch, w_out=m_w_out, norm_mix_post=m_norm_mix_post, norm_ffn_pre=m_norm_ffn_pre, w_ffn_up=m_w_ffn_up, w_ffn_down=m_w_ffn_down, norm_ffn_post=m_norm_ffn_post)
    var = dict(norm_mix_pre=v_norm_mix_pre, w_in=v_w_in, conv_dw_w=v_conv_dw_w, conv_dw_b=v_conv_dw_b, conv_ln_g=v_conv_ln_g, conv_ln_b=v_conv_ln_b, w_conv_branch=v_w_conv_branch, b_conv_branch=v_b_conv_branch, w_att_branch=v_w_att_branch, w_out=v_w_out, norm_mix_post=v_norm_mix_post, norm_ffn_pre=v_norm_ffn_pre, w_ffn_up=v_w_ffn_up, w_ffn_down=v_w_ffn_down, norm_ffn_post=v_norm_ffn_post)
    order = list(weights)
    grads, deltas, new_m, new_v = {}, {}, {}, {}
    xs = x.reshape(SEQ, D_MODEL)
    tgt = loss_target.reshape(SEQ, D_MODEL)
    row = lambda a: a.reshape(1, -1)
    g1, g2, g3, g4 = (row(weights[n]) for n in ("norm_mix_pre", "norm_mix_post", "norm_ffn_pre", "norm_ffn_post"))
    ln_g, ln_b = row(conv_ln_g), row(conv_ln_b)

    summed, from_chips = {}, {}

    def core_sums(names, state, after, tag):
        own, from_sibling = sibling_wait(state, after, True, tag)
        for n, g, r in zip(names, own, from_sibling):
            summed[n] = add_halves(g, r, "add_" + n)

    def chip_sums(names, after):
        return [sum_parts(summed[n][1], from_chips[n], after, "sum_" + n) for n in names]

    def optimize(names, state, after, tag):
        mine, other = sibling_wait(state, after, False, tag)
        for n, a, b in zip(names, mine, other):
            grads[n], deltas[n], new_m[n], new_v[n] = adamw(weights[n], a, b, mom[n], var[n], "adamw_" + n)

    w_in_g, dw_g, *rest = all_gather_weights([w_in], conv_dw_w, [weights[n] for n in REST])
    w_dw_full = jnp.concatenate([dw_g[k] for k in range(N_CHIPS)], axis=1)
    rest_shapes = [weights[n].shape for n in REST]
    over_ici = _gather_copies(rest_shapes, 1, to_both_cores=(2, 3, 4))
    state, token = split_start("gather_start", rest, 3 * (len(REST) + 3), over_ici)
    h1, ci, q, k, v, gc, ga = in_proj_fwd(xs, g1, w_in_g, token)
    u1, u3 = conv_fwd(ci, w_dw_full, row(conv_dw_b), ln_g, ln_b)
    att, rc = attn_fwd(q, k, v)
    rest = split_wait("gather_wait", state, [att], over_ici)
    w_out_g, w_cb_g, w_ab_g = rest[2:]
    w_out_g = w_out_g.reshape(D_MODEL, D_MODEL)
    to_sibling = _gather_copies(rest_shapes[:2], 2)
    state, token = split_start("pass_start", rest[:2], 3 * 2, to_sibling)
    merged, x2, h2 = mix_fwd(u3, att, gc, ga, xs, w_cb_g, row(b_conv_branch), w_ab_g, w_out_g, g2, g3, token)
    w_up_g, w_down_g = split_wait("pass_wait", state, [h2], to_sibling)
    w_down_g = w_down_g.reshape(D_FF, D_MODEL)
    act, dff, dy, loss_parts, dg4, dgu = ffn_loss(h2, w_up_g, w_down_g, x2, tgt, g4)

    ffn_grads = [weight_grad(h2, dgu, "dw_ffn_up", True), weight_grad(act, dff, "dw_ffn_down", False, tk=UP_SHARD)]
    to_ffn, token = sibling_start(ffn_grads, True, "dw_ffn")
    dx2, dmix, dg3, dg2, dco, dao, dg, du3, datt, dbcb = ffn_merge_bwd(
        dgu, w_up_g, x2, merged, dy, g3, g2, w_out_g, gc, ga, u3, att, w_cb_g, row(b_conv_branch), w_ab_g, token)
    mix_grads = weight_grad_mix(merged, dmix, u3, dco, att, dao)
    core_sums(REST[:2], to_ffn, [mix_grads[0]], "dw_ffn")
    state, token = scatter_start([summed[n][0] for n in REST[:2]], mix_grads)
    dci, ddw, dbdw, dlng, dlnb = conv_bwd(du3, u1, ci, w_dw_full, ln_g, ln_b, token)
    dqkv = attn_bwd(q, k, v, datt, rc, token)
    ffn_from_chips, mix_grads, mix_from_all = scatter_wait(state, [dci, dqkv], 2, len(mix_grads))
    from_chips.update(zip(REST[:2], ffn_from_chips))
    dproj = (dci, dqkv, dg)
    to_in, token = sibling_start([weight_grad_in(h1, dproj)], True, "dw_in")
    grad_x, dg1 = in_proj_bwd(dproj, w_in_g, xs, dx2, g1, token)
    v512 = dict(conv_dw_b=dbdw, conv_ln_g=dlng, conv_ln_b=dlnb)
    v1024 = dict(norm_mix_pre=dg1, b_conv_branch=dbcb, norm_mix_post=dg2, norm_ffn_pre=dg3, norm_ffn_post=dg4)
    packs = small_pack(ddw, v512, v1024, loss_parts)
    core_sums(("w_in",), to_in, [packs], "dw_in")
    to_chips = summed["w_in"][0]
    landing = lax.empty((3,) + to_chips.shape[1:], to_chips.dtype)

    def scatter_and_packs(refs, send, recv):
        return (_scatter_copies(1)(refs[:2], send, recv)
                + _small_copies(refs[2:], _Shifted(send, 3), _Shifted(recv, 3)))

    state, token = split_start("scatter_start_w_in", [to_chips, landing, packs], 3 + 7, scatter_and_packs)
    swap_up, token = sibling_start(chip_sums(REST[:1], token), False, "sum_ffn_up")
    rest_sums = chip_sums(REST[1:2], token) + [sum_partials(p, r, token, "sum_" + n)
                                               for n, p, r in zip(REST[2:], mix_grads, mix_from_all)]
    swap_rest, token = sibling_start(rest_sums, False, "sum_rest")
    optimize(REST[:1], swap_up, [token], "sum_ffn_up")
    optimize(REST[1:], swap_rest, [new_v["w_ffn_up"]], "sum_rest")
    _, from_chips["w_in"], packs = split_wait("scatter_wait_w_in", state, [new_v[n] for n in REST], scatter_and_packs)
    swap_in, token = sibling_start(chip_sums(("w_in",), token), False, "sum_w_in")
    as_rows = lambda n, a: a if n == "conv_dw_w" else a.reshape(1, -1)
    small_names = ("conv_dw_w",) + VEC512 + VEC1024
    loss, gsum, small = adamw_small(
        packs, {n: tuple(as_rows(n, d[n]) for d in (weights, mom, var)) for n in small_names}, token)
    optimize(("w_in",), swap_in, [gsum], "sum_w_in")
    for n in small_names:
        grads[n], deltas[n], new_m[n], new_v[n] = (a.reshape(weights[n].shape) for a in small[n])

    return (loss.reshape(()), grad_x.reshape(1, SEQ, D_MODEL),*[grads[n] for n in order], *[deltas[n] for n in order],
            *[new_m[n] for n in order], *[new_v[n] for n in order])
```
